```python
import math
import jax, jax.numpy as jnp
from jax import lax
import numpy as np

D_MODEL = 1024
BATCH = 16
SEQ = 4096
DEPTH = 1

EXPAND = 2
D_MIX = EXPAND * D_MODEL
D_SSD = D_MIX // 2
D_MLA = D_MIX - D_SSD
SSD_HEAD_DIM = 64
SSD_HEADS = D_SSD // SSD_HEAD_DIM
SSD_GROUPS = 2
SSD_HPG = SSD_HEADS // SSD_GROUPS
SSD_STATE = 128
CONV_WIDTH = 4
CHUNK = 128
MLA_HEADS = 8
QK_NOPE = 64
QK_ROPE = 32
QK_DIM = QK_NOPE + QK_ROPE
V_HEAD = D_MLA // MLA_HEADS
Q_LORA = 384
KV_LORA = 256
ROPE_THETA = 10000.0
Q_BLOCK = 128
D_FF = 2816
N_MOD = 9
EPS = 1e-6
D_CONV = D_SSD + 2 * SSD_GROUPS * SSD_STATE
IN_WIDTHS = (D_SSD, D_SSD, SSD_GROUPS * SSD_STATE, SSD_GROUPS * SSD_STATE,
             SSD_HEADS, Q_LORA, KV_LORA, QK_ROPE)
D_IN_PROJ = sum(IN_WIDTHS)
IN_SPLITS = tuple(int(v) for v in np.cumsum(IN_WIDTHS)[:-1])

kernel_name = "hymba_ssd_mla_macaron_adaln"


def rmsnorm(x, w):
    xf = x.astype(jnp.float32)
    y = xf * lax.rsqrt(jnp.mean(xf * xf, axis=-1, keepdims=True) + EPS)
    return (y * w.astype(jnp.float32)).astype(x.dtype)


def modulate(h, shift, scale):
    return h * (1.0 + scale[:, None, :]) + shift[:, None, :]


def swiglu(h, w_gate, w_up, w_down):
    return (jax.nn.silu(h @ w_gate) * (h @ w_up)) @ w_down


def apply_rope(u, cos, sin):
    u1, u2 = jnp.split(u, 2, axis=-1)
    return jnp.concatenate([u1 * cos - u2 * sin, u2 * cos + u1 * sin], axis=-1)


def causal_depthwise_conv(u, w, b):
    out = lax.conv_general_dilated(
        u, w[:, None, :].astype(u.dtype), window_strides=(1,),
        padding=[(CONV_WIDTH - 1, 0)],
        dimension_numbers=('NWC', 'WIO', 'NWC'),
        feature_group_count=u.shape[-1])
    return out + b.astype(u.dtype)


def ssd_chunked(xh, dt, A, Bm, Cm):
    b, S = xh.shape[0], xh.shape[1]
    nc = S // CHUNK
    dtype = xh.dtype
    xdt = (xh * dt[..., None].astype(dtype)).reshape(b, nc, CHUNK, SSD_GROUPS, SSD_HPG, SSD_HEAD_DIM)
    Bc = Bm.reshape(b, nc, CHUNK, SSD_GROUPS, SSD_STATE)
    Cc = Cm.reshape(b, nc, CHUNK, SSD_GROUPS, SSD_STATE)
    a_cum = jnp.cumsum((dt * A).reshape(b, nc, CHUNK, SSD_GROUPS, SSD_HPG), axis=2)
    seg = a_cum[:, :, :, None] - a_cum[:, :, None, :]
    causal = jnp.tril(jnp.ones((CHUNK, CHUNK), dtype=bool))[None, None, :, :, None, None]
    Lmat = jnp.exp(jnp.where(causal, seg, -jnp.inf)).astype(dtype)
    cb = jnp.einsum('bclgn,bcsgn->bclsg', Cc, Bc)
    y_diag = jnp.einsum('bclsg,bclsgr,bcsgrp->bclgrp', cb, Lmat, xdt)
    decay_states = jnp.exp(a_cum[:, :, -1:] - a_cum).astype(dtype)
    states = jnp.einsum('bclgn,bclgr,bclgrp->bcgrpn', Bc, decay_states, xdt)
    chunk_decay = jnp.exp(a_cum[:, :, -1]).astype(dtype)

    def step(h, inp):
        s_c, d_c = inp
        return d_c[..., None, None] * h + s_c, h

    h0 = jnp.zeros_like(states[:, 0])
    _, prev = lax.scan(step, h0, (jnp.moveaxis(states, 1, 0), jnp.moveaxis(chunk_decay, 1, 0)))
    prev = jnp.moveaxis(prev, 0, 1)
    y_off = jnp.einsum('bclgn,bcgrpn,bclgr->bclgrp', Cc, prev, jnp.exp(a_cum).astype(dtype))
    return (y_diag + y_off).reshape(b, S, SSD_GROUPS, SSD_HPG, SSD_HEAD_DIM)


def mla_causal_attention(q_nope, q_rope, k_nope, k_rope, v):
    b, S = q_nope.shape[0], q_nope.shape[1]
    nblk = S // Q_BLOCK
    scale = 1.0 / math.sqrt(QK_DIM)
    qn_b = q_nope.reshape(b, nblk, Q_BLOCK, MLA_HEADS, QK_NOPE).swapaxes(0, 1)
    qr_b = q_rope.reshape(b, nblk, Q_BLOCK, MLA_HEADS, QK_ROPE).swapaxes(0, 1)
    starts = jnp.arange(nblk, dtype=jnp.int32) * Q_BLOCK
    k_idx = jnp.arange(S, dtype=jnp.int32)

    def one_block(args):
        qn, qr, start = args
        s = (jnp.einsum('bqhd,bkhd->bhqk', qn, k_nope)
             + jnp.einsum('bqhr,bkr->bhqk', qr, k_rope)).astype(jnp.float32) * scale
        q_idx = start + jnp.arange(Q_BLOCK, dtype=jnp.int32)
        mask = k_idx[None, :] <= q_idx[:, None]
        s = jnp.where(mask[None, None], s, -jnp.inf)
        p = jax.nn.softmax(s, axis=-1).astype(v.dtype)
        return jnp.einsum('bhqk,bkhv->bqhv', p, v)

    out = lax.map(one_block, (qn_b, qr_b, starts))
    return out.swapaxes(0, 1).reshape(b, S, MLA_HEADS * V_HEAD)


def hybrid_mixer(h, positions, w_in, conv_w, conv_b, dt_bias, a_log, d_skip, ssd_norm_w,
                 q_norm_w, w_uq, kv_norm_w, w_ukv, mla_norm_w, w_out):
    b, S = h.shape[0], h.shape[1]
    proj = h @ w_in
    z, xs, Bm, Cm, dt_raw, cq, ckv, k_rope_raw = jnp.split(proj, IN_SPLITS, axis=-1)

    xBC = jax.nn.silu(causal_depthwise_conv(jnp.concatenate([xs, Bm, Cm], axis=-1), conv_w, conv_b))
    xs, Bm, Cm = jnp.split(xBC, [D_SSD, D_SSD + SSD_GROUPS * SSD_STATE], axis=-1)
    dt = jax.nn.softplus(dt_raw.astype(jnp.float32) + dt_bias.astype(jnp.float32))
    A = -jnp.exp(a_log.astype(jnp.float32))
    xh = xs.reshape(b, S, SSD_GROUPS, SSD_HPG, SSD_HEAD_DIM)
    y = ssd_chunked(xh, dt.reshape(b, S, SSD_GROUPS, SSD_HPG), A.reshape(SSD_GROUPS, SSD_HPG),
                    Bm.reshape(b, S, SSD_GROUPS, SSD_STATE), Cm.reshape(b, S, SSD_GROUPS, SSD_STATE))
    y = y + d_skip.reshape(SSD_GROUPS, SSD_HPG)[:, :, None].astype(y.dtype) * xh
    yg = (y.reshape(b, S, D_SSD) * jax.nn.silu(z)).reshape(b, S, SSD_GROUPS, D_SSD // SSD_GROUPS)
    y_ssd = rmsnorm(yg, ssd_norm_w.reshape(SSD_GROUPS, D_SSD // SSD_GROUPS)).reshape(b, S, D_SSD)

    q = (rmsnorm(cq, q_norm_w) @ w_uq).reshape(b, S, MLA_HEADS, QK_DIM)
    q_nope, q_rope = jnp.split(q, [QK_NOPE], axis=-1)
    kv = (rmsnorm(ckv, kv_norm_w) @ w_ukv).reshape(b, S, MLA_HEADS, QK_NOPE + V_HEAD)
    k_nope, v = jnp.split(kv, [QK_NOPE], axis=-1)
    inv_freq = ROPE_THETA ** (-jnp.arange(0, QK_ROPE, 2, dtype=jnp.float32) / QK_ROPE)
    ang = positions[..., None].astype(jnp.float32) * inv_freq
    cos, sin = jnp.cos(ang).astype(h.dtype), jnp.sin(ang).astype(h.dtype)
    q_rope = apply_rope(q_rope, cos[:, :, None], sin[:, :, None])
    k_rope = apply_rope(k_rope_raw, cos, sin)
    attn = mla_causal_attention(q_nope, q_rope, k_nope, k_rope, v)
    y_mla = rmsnorm(attn, mla_norm_w)

    return jnp.concatenate([y_ssd, y_mla], axis=-1) @ w_out


def _fwd_setup_inputs(seed: int = 0) -> dict:
    key = jax.random.key(seed)
    ks = iter(jax.random.split(key, 40))

    def dense(shape, fan_in):
        return jax.random.normal(next(ks), shape, jnp.float32) * fan_in ** -0.5

    def gain(shape):
        return 1.0 + 0.05 * jax.random.normal(next(ks), shape, jnp.float32)

    def small(shape, s=0.02):
        return s * jax.random.normal(next(ks), shape, jnp.float32)

    L = DEPTH
    x = jax.random.normal(next(ks), (BATCH, SEQ, D_MODEL), jnp.float32)
    c = jax.random.normal(next(ks), (BATCH, D_MODEL), jnp.float32)
    offsets = jax.random.randint(next(ks), (BATCH, 1), 0, 1024, dtype=jnp.int32)
    positions = offsets + jnp.arange(SEQ, dtype=jnp.int32)[None, :]
    dt0 = jnp.exp(jax.random.uniform(next(ks), (L, SSD_HEADS), jnp.float32,
                                     math.log(1e-3), math.log(1e-1)))
    dt_bias = dt0 + jnp.log(-jnp.expm1(-dt0))
    a_log = jnp.log(jax.random.uniform(next(ks), (L, SSD_HEADS), jnp.float32, 1.0, 16.0))
    return {
        "x": x,
        "c": c,
        "positions": positions,
        "w_ada": dense((L, D_MODEL, N_MOD * D_MODEL), D_MODEL),
        "b_ada": small((L, N_MOD * D_MODEL)),
        "norm_ffn1": gain((L, D_MODEL)),
        "ffn1_w_gate": dense((L, D_MODEL, D_FF), D_MODEL),
        "ffn1_w_up": dense((L, D_MODEL, D_FF), D_MODEL),
        "ffn1_w_down": dense((L, D_FF, D_MODEL), D_FF),
        "norm_mix": gain((L, D_MODEL)),
        "w_in": dense((L, D_MODEL, D_IN_PROJ), D_MODEL),
        "conv_w": dense((L, CONV_WIDTH, D_CONV), CONV_WIDTH),
        "conv_b": small((L, D_CONV)),
        "dt_bias": dt_bias,
        "a_log": a_log,
        "d_skip": gain((L, SSD_HEADS)),
        "ssd_norm_w": gain((L, D_SSD)),
        "q_norm_w": gain((L, Q_LORA)),
        "w_uq": dense((L, Q_LORA, MLA_HEADS * QK_DIM), Q_LORA),
        "kv_norm_w": gain((L, KV_LORA)),
        "w_ukv": dense((L, KV_LORA, MLA_HEADS * (QK_NOPE + V_HEAD)), KV_LORA),
        "mla_norm_w": gain((L, D_MLA)),
        "w_out": dense((L, D_MIX, D_MODEL), D_MIX),
        "norm_ffn2": gain((L, D_MODEL)),
        "ffn2_w_gate": dense((L, D_MODEL, D_FF), D_MODEL),
        "ffn2_w_up": dense((L, D_MODEL, D_FF), D_MODEL),
        "ffn2_w_down": dense((L, D_FF, D_MODEL), D_FF),
        "norm_final": gain((D_MODEL,)),
    }


def _fwd_reference(x, c, positions, w_ada, b_ada, norm_ffn1, ffn1_w_gate, ffn1_w_up, ffn1_w_down,
              norm_mix, w_in, conv_w, conv_b, dt_bias, a_log, d_skip, ssd_norm_w,
              q_norm_w, w_uq, kv_norm_w, w_ukv, mla_norm_w, w_out,
              norm_ffn2, ffn2_w_gate, ffn2_w_up, ffn2_w_down, norm_final):
    c_act = jax.nn.silu(c)
    for l in range(DEPTH):
        mod = c_act @ w_ada[l] + b_ada[l]
        (sh1, sc1, g1, sh2, sc2, g2, sh3, sc3, g3) = jnp.split(mod, N_MOD, axis=-1)
        h = modulate(rmsnorm(x, norm_ffn1[l]), sh1, sc1)
        x = x + 0.5 * g1[:, None, :] * swiglu(h, ffn1_w_gate[l], ffn1_w_up[l], ffn1_w_down[l])
        h = modulate(rmsnorm(x, norm_mix[l]), sh2, sc2)
        x = x + g2[:, None, :] * hybrid_mixer(
            h, positions, w_in[l], conv_w[l], conv_b[l], dt_bias[l], a_log[l], d_skip[l],
            ssd_norm_w[l], q_norm_w[l], w_uq[l], kv_norm_w[l], w_ukv[l], mla_norm_w[l], w_out[l])
        h = modulate(rmsnorm(x, norm_ffn2[l]), sh3, sc3)
        x = x + 0.5 * g3[:, None, :] * swiglu(h, ffn2_w_gate[l], ffn2_w_up[l], ffn2_w_down[l])
    return rmsnorm(x, norm_final)


import jax as _jax
import jax.numpy as _jnp

TWIN_FORMAT = 'train_step'
FWD_PARAMS = ['x', 'c', 'positions', 'w_ada', 'b_ada', 'norm_ffn1', 'ffn1_w_gate', 'ffn1_w_up', 'ffn1_w_down', 'norm_mix', 'w_in', 'conv_w', 'conv_b', 'dt_bias', 'a_log', 'd_skip', 'ssd_norm_w', 'q_norm_w', 'w_uq', 'kv_norm_w', 'w_ukv', 'mla_norm_w', 'w_out', 'norm_ffn2', 'ffn2_w_gate', 'ffn2_w_up', 'ffn2_w_down', 'norm_final']
TWIN_WEIGHTS = ['w_ada', 'b_ada', 'norm_ffn1', 'ffn1_w_gate', 'ffn1_w_up', 'ffn1_w_down', 'norm_mix', 'w_in', 'conv_w', 'conv_b', 'dt_bias', 'a_log', 'd_skip', 'ssd_norm_w', 'q_norm_w', 'w_uq', 'kv_norm_w', 'w_ukv', 'mla_norm_w', 'w_out', 'norm_ffn2', 'ffn2_w_gate', 'ffn2_w_up', 'ffn2_w_down', 'norm_final']
TWIN_DIFF_INPUT = 'x'
TWIN_INPUTS = ['x', 'c', 'positions', 'w_ada', 'b_ada', 'norm_ffn1', 'ffn1_w_gate', 'ffn1_w_up', 'ffn1_w_down', 'norm_mix', 'w_in', 'conv_w', 'conv_b', 'dt_bias', 'a_log', 'd_skip', 'ssd_norm_w', 'q_norm_w', 'w_uq', 'kv_norm_w', 'w_ukv', 'mla_norm_w', 'w_out', 'norm_ffn2', 'ffn2_w_gate', 'ffn2_w_up', 'ffn2_w_down', 'norm_final', 'loss_target', 'm_w_ada', 'm_b_ada', 'm_norm_ffn1', 'm_ffn1_w_gate', 'm_ffn1_w_up', 'm_ffn1_w_down', 'm_norm_mix', 'm_w_in', 'm_conv_w', 'm_conv_b', 'm_dt_bias', 'm_a_log', 'm_d_skip', 'm_ssd_norm_w', 'm_q_norm_w', 'm_w_uq', 'm_kv_norm_w', 'm_w_ukv', 'm_mla_norm_w', 'm_w_out', 'm_norm_ffn2', 'm_ffn2_w_gate', 'm_ffn2_w_up', 'm_ffn2_w_down', 'm_norm_final', 'v_w_ada', 'v_b_ada', 'v_norm_ffn1', 'v_ffn1_w_gate', 'v_ffn1_w_up', 'v_ffn1_w_down', 'v_norm_mix', 'v_w_in', 'v_conv_w', 'v_conv_b', 'v_dt_bias', 'v_a_log', 'v_d_skip', 'v_ssd_norm_w', 'v_q_norm_w', 'v_w_uq', 'v_kv_norm_w', 'v_w_ukv', 'v_mla_norm_w', 'v_w_out', 'v_norm_ffn2', 'v_ffn2_w_gate', 'v_ffn2_w_up', 'v_ffn2_w_down', 'v_norm_final']
TWIN_OUTPUTS = ['loss', 'grad_x', 'grad_w_ada', 'grad_b_ada', 'grad_norm_ffn1', 'grad_ffn1_w_gate', 'grad_ffn1_w_up', 'grad_ffn1_w_down', 'grad_norm_mix', 'grad_w_in', 'grad_conv_w', 'grad_conv_b', 'grad_dt_bias', 'grad_a_log', 'grad_d_skip', 'grad_ssd_norm_w', 'grad_q_norm_w', 'grad_w_uq', 'grad_kv_norm_w', 'grad_w_ukv', 'grad_mla_norm_w', 'grad_w_out', 'grad_norm_ffn2', 'grad_ffn2_w_gate', 'grad_ffn2_w_up', 'grad_ffn2_w_down', 'grad_norm_final', 'delta_w_ada', 'delta_b_ada', 'delta_norm_ffn1', 'delta_ffn1_w_gate', 'delta_ffn1_w_up', 'delta_ffn1_w_down', 'delta_norm_mix', 'delta_w_in', 'delta_conv_w', 'delta_conv_b', 'delta_dt_bias', 'delta_a_log', 'delta_d_skip', 'delta_ssd_norm_w', 'delta_q_norm_w', 'delta_w_uq', 'delta_kv_norm_w', 'delta_w_ukv', 'delta_mla_norm_w', 'delta_w_out', 'delta_norm_ffn2', 'delta_ffn2_w_gate', 'delta_ffn2_w_up', 'delta_ffn2_w_down', 'delta_norm_final', 'new_m_w_ada', 'new_m_b_ada', 'new_m_norm_ffn1', 'new_m_ffn1_w_gate', 'new_m_ffn1_w_up', 'new_m_ffn1_w_down', 'new_m_norm_mix', 'new_m_w_in', 'new_m_conv_w', 'new_m_conv_b', 'new_m_dt_bias', 'new_m_a_log', 'new_m_d_skip', 'new_m_ssd_norm_w', 'new_m_q_norm_w', 'new_m_w_uq', 'new_m_kv_norm_w', 'new_m_w_ukv', 'new_m_mla_norm_w', 'new_m_w_out', 'new_m_norm_ffn2', 'new_m_ffn2_w_gate', 'new_m_ffn2_w_up', 'new_m_ffn2_w_down', 'new_m_norm_final', 'new_v_w_ada', 'new_v_b_ada', 'new_v_norm_ffn1', 'new_v_ffn1_w_gate', 'new_v_ffn1_w_up', 'new_v_ffn1_w_down', 'new_v_norm_mix', 'new_v_w_in', 'new_v_conv_w', 'new_v_conv_b', 'new_v_dt_bias', 'new_v_a_log', 'new_v_d_skip', 'new_v_ssd_norm_w', 'new_v_q_norm_w', 'new_v_w_uq', 'new_v_kv_norm_w', 'new_v_w_ukv', 'new_v_mla_norm_w', 'new_v_w_out', 'new_v_norm_ffn2', 'new_v_ffn2_w_gate', 'new_v_ffn2_w_up', 'new_v_ffn2_w_down', 'new_v_norm_final']
TWIN_LEAF_KINDS = {'loss': 'loss', 'grad_x': 'grad_x', 'grad_w_ada': 'grad_w', 'grad_b_ada': 'grad_w', 'grad_norm_ffn1': 'grad_w', 'grad_ffn1_w_gate': 'grad_w', 'grad_ffn1_w_up': 'grad_w', 'grad_ffn1_w_down': 'grad_w', 'grad_norm_mix': 'grad_w', 'grad_w_in': 'grad_w', 'grad_conv_w': 'grad_w', 'grad_conv_b': 'grad_w', 'grad_dt_bias': 'grad_w', 'grad_a_log': 'grad_w', 'grad_d_skip': 'grad_w', 'grad_ssd_norm_w': 'grad_w', 'grad_q_norm_w': 'grad_w', 'grad_w_uq': 'grad_w', 'grad_kv_norm_w': 'grad_w', 'grad_w_ukv': 'grad_w', 'grad_mla_norm_w': 'grad_w', 'grad_w_out': 'grad_w', 'grad_norm_ffn2': 'grad_w', 'grad_ffn2_w_gate': 'grad_w', 'grad_ffn2_w_up': 'grad_w', 'grad_ffn2_w_down': 'grad_w', 'grad_norm_final': 'grad_w', 'delta_w_ada': 'delta_w', 'delta_b_ada': 'delta_w', 'delta_norm_ffn1': 'delta_w', 'delta_ffn1_w_gate': 'delta_w', 'delta_ffn1_w_up': 'delta_w', 'delta_ffn1_w_down': 'delta_w', 'delta_norm_mix': 'delta_w', 'delta_w_in': 'delta_w', 'delta_conv_w': 'delta_w', 'delta_conv_b': 'delta_w', 'delta_dt_bias': 'delta_w', 'delta_a_log': 'delta_w', 'delta_d_skip': 'delta_w', 'delta_ssd_norm_w': 'delta_w', 'delta_q_norm_w': 'delta_w', 'delta_w_uq': 'delta_w', 'delta_kv_norm_w': 'delta_w', 'delta_w_ukv': 'delta_w', 'delta_mla_norm_w': 'delta_w', 'delta_w_out': 'delta_w', 'delta_norm_ffn2': 'delta_w', 'delta_ffn2_w_gate': 'delta_w', 'delta_ffn2_w_up': 'delta_w', 'delta_ffn2_w_down': 'delta_w', 'delta_norm_final': 'delta_w', 'new_m_w_ada': 'new_m', 'new_m_b_ada': 'new_m', 'new_m_norm_ffn1': 'new_m', 'new_m_ffn1_w_gate': 'new_m', 'new_m_ffn1_w_up': 'new_m', 'new_m_ffn1_w_down': 'new_m', 'new_m_norm_mix': 'new_m', 'new_m_w_in': 'new_m', 'new_m_conv_w': 'new_m', 'new_m_conv_b': 'new_m', 'new_m_dt_bias': 'new_m', 'new_m_a_log': 'new_m', 'new_m_d_skip': 'new_m', 'new_m_ssd_norm_w': 'new_m', 'new_m_q_norm_w': 'new_m', 'new_m_w_uq': 'new_m', 'new_m_kv_norm_w': 'new_m', 'new_m_w_ukv': 'new_m', 'new_m_mla_norm_w': 'new_m', 'new_m_w_out': 'new_m', 'new_m_norm_ffn2': 'new_m', 'new_m_ffn2_w_gate': 'new_m', 'new_m_ffn2_w_up': 'new_m', 'new_m_ffn2_w_down': 'new_m', 'new_m_norm_final': 'new_m', 'new_v_w_ada': 'new_v', 'new_v_b_ada': 'new_v', 'new_v_norm_ffn1': 'new_v', 'new_v_ffn1_w_gate': 'new_v', 'new_v_ffn1_w_up': 'new_v', 'new_v_ffn1_w_down': 'new_v', 'new_v_norm_mix': 'new_v', 'new_v_w_in': 'new_v', 'new_v_conv_w': 'new_v', 'new_v_conv_b': 'new_v', 'new_v_dt_bias': 'new_v', 'new_v_a_log': 'new_v', 'new_v_d_skip': 'new_v', 'new_v_ssd_norm_w': 'new_v', 'new_v_q_norm_w': 'new_v', 'new_v_w_uq': 'new_v', 'new_v_kv_norm_w': 'new_v', 'new_v_w_ukv': 'new_v', 'new_v_mla_norm_w': 'new_v', 'new_v_w_out': 'new_v', 'new_v_norm_ffn2': 'new_v', 'new_v_ffn2_w_gate': 'new_v', 'new_v_ffn2_w_up': 'new_v', 'new_v_ffn2_w_down': 'new_v', 'new_v_norm_final': 'new_v'}


def _forward(args):
    return _fwd_reference(*[args[k] for k in FWD_PARAMS])


def _output_shape():
    out = _jax.eval_shape(lambda: _forward(_fwd_setup_inputs(0)))
    return out.shape, out.dtype

N_MICROBATCH = 1
ADAM_LR = 0.001
ADAM_B1 = 0.9
ADAM_B2 = 0.999
ADAM_EPS = 1e-08
ADAM_WD = 0.01
ADAM_STEP = 10
PER_EXAMPLE_BATCH_AXIS = {'x': 0, 'c': 0, 'positions': 0, 'loss_target': 0}
SHARED_INPUTS = []
_WEIGHT_DTYPES = {'w_ada': _jnp.float32, 'b_ada': _jnp.float32, 'norm_ffn1': _jnp.float32, 'ffn1_w_gate': _jnp.float32, 'ffn1_w_up': _jnp.float32, 'ffn1_w_down': _jnp.float32, 'norm_mix': _jnp.float32, 'w_in': _jnp.float32, 'conv_w': _jnp.float32, 'conv_b': _jnp.float32, 'dt_bias': _jnp.float32, 'a_log': _jnp.float32, 'd_skip': _jnp.float32, 'ssd_norm_w': _jnp.float32, 'q_norm_w': _jnp.float32, 'w_uq': _jnp.float32, 'kv_norm_w': _jnp.float32, 'w_ukv': _jnp.float32, 'mla_norm_w': _jnp.float32, 'w_out': _jnp.float32, 'norm_ffn2': _jnp.float32, 'ffn2_w_gate': _jnp.float32, 'ffn2_w_up': _jnp.float32, 'ffn2_w_down': _jnp.float32, 'norm_final': _jnp.float32}
MOMENT_SCALE = {'w_ada': 2.581025e-01, 'b_ada': 4.458916e-01, 'norm_ffn1': 1.135611e-01, 'ffn1_w_gate': 5.741060e-02, 'ffn1_w_up': 6.292316e-02, 'ffn1_w_down': 1.041439e-01, 'norm_mix': 1.413220e-01, 'w_in': 2.733578e-01, 'conv_w': 1.231981e-01, 'conv_b': 1.797434e-01, 'dt_bias': 8.029165e-01, 'a_log': 8.833745e-01, 'd_skip': 3.417333e-01, 'ssd_norm_w': 1.413848e-01, 'q_norm_w': 5.078902e-02, 'w_uq': 3.554593e-02, 'kv_norm_w': 1.062628e+00, 'w_ukv': 4.069641e-01, 'mla_norm_w': 4.981139e-01, 'w_out': 4.748968e-01, 'norm_ffn2': 1.021277e-01, 'ffn2_w_gate': 4.618573e-02, 'ffn2_w_up': 4.956720e-02, 'ffn2_w_down': 7.998671e-02, 'norm_final': 6.654506e+01}


def _to_microbatches(a, axis):
    t = _jnp.moveaxis(a, axis, 0)
    t = t.reshape((N_MICROBATCH, t.shape[0] // N_MICROBATCH) + t.shape[1:])
    return _jnp.moveaxis(t, 1, axis + 1)


def setup_inputs(seed: int = 0) -> dict:
    inp = _fwd_setup_inputs(seed)
    key = _jax.random.fold_in(_jax.random.key(seed), 7919)
    shape, _ = _output_shape()
    out = dict(inp)
    out["loss_target"] = _jax.random.normal(_jax.random.fold_in(key, 0), shape, _jnp.float32)
    for i, name in enumerate(TWIN_WEIGHTS):
        w = inp[name].astype(_jnp.float32)
        if MOMENT_SCALE is None:
            s = _jnp.sqrt(_jnp.mean(_jnp.square(w)) + 1e-30)
        else:
            s = MOMENT_SCALE[name]
        km, kv = _jax.random.split(_jax.random.fold_in(key, i + 1))
        out[name] = w
        out["m_" + name] = s * _jax.random.normal(km, w.shape, _jnp.float32)
        out["v_" + name] = (s * s) * _jax.random.uniform(kv, w.shape, _jnp.float32, 0.5, 1.5)
    if N_MICROBATCH > 1:
        for name, axis in PER_EXAMPLE_BATCH_AXIS.items():
            out[name] = _to_microbatches(out[name], axis)
    return {'x': out['x'], 'c': out['c'], 'positions': out['positions'], 'w_ada': out['w_ada'], 'b_ada': out['b_ada'], 'norm_ffn1': out['norm_ffn1'], 'ffn1_w_gate': out['ffn1_w_gate'], 'ffn1_w_up': out['ffn1_w_up'], 'ffn1_w_down': out['ffn1_w_down'], 'norm_mix': out['norm_mix'], 'w_in': out['w_in'], 'conv_w': out['conv_w'], 'conv_b': out['conv_b'], 'dt_bias': out['dt_bias'], 'a_log': out['a_log'], 'd_skip': out['d_skip'], 'ssd_norm_w': out['ssd_norm_w'], 'q_norm_w': out['q_norm_w'], 'w_uq': out['w_uq'], 'kv_norm_w': out['kv_norm_w'], 'w_ukv': out['w_ukv'], 'mla_norm_w': out['mla_norm_w'], 'w_out': out['w_out'], 'norm_ffn2': out['norm_ffn2'], 'ffn2_w_gate': out['ffn2_w_gate'], 'ffn2_w_up': out['ffn2_w_up'], 'ffn2_w_down': out['ffn2_w_down'], 'norm_final': out['norm_final'], 'loss_target': out['loss_target'], 'm_w_ada': out['m_w_ada'], 'm_b_ada': out['m_b_ada'], 'm_norm_ffn1': out['m_norm_ffn1'], 'm_ffn1_w_gate': out['m_ffn1_w_gate'], 'm_ffn1_w_up': out['m_ffn1_w_up'], 'm_ffn1_w_down': out['m_ffn1_w_down'], 'm_norm_mix': out['m_norm_mix'], 'm_w_in': out['m_w_in'], 'm_conv_w': out['m_conv_w'], 'm_conv_b': out['m_conv_b'], 'm_dt_bias': out['m_dt_bias'], 'm_a_log': out['m_a_log'], 'm_d_skip': out['m_d_skip'], 'm_ssd_norm_w': out['m_ssd_norm_w'], 'm_q_norm_w': out['m_q_norm_w'], 'm_w_uq': out['m_w_uq'], 'm_kv_norm_w': out['m_kv_norm_w'], 'm_w_ukv': out['m_w_ukv'], 'm_mla_norm_w': out['m_mla_norm_w'], 'm_w_out': out['m_w_out'], 'm_norm_ffn2': out['m_norm_ffn2'], 'm_ffn2_w_gate': out['m_ffn2_w_gate'], 'm_ffn2_w_up': out['m_ffn2_w_up'], 'm_ffn2_w_down': out['m_ffn2_w_down'], 'm_norm_final': out['m_norm_final'], 'v_w_ada': out['v_w_ada'], 'v_b_ada': out['v_b_ada'], 'v_norm_ffn1': out['v_norm_ffn1'], 'v_ffn1_w_gate': out['v_ffn1_w_gate'], 'v_ffn1_w_up': out['v_ffn1_w_up'], 'v_ffn1_w_down': out['v_ffn1_w_down'], 'v_norm_mix': out['v_norm_mix'], 'v_w_in': out['v_w_in'], 'v_conv_w': out['v_conv_w'], 'v_conv_b': out['v_conv_b'], 'v_dt_bias': out['v_dt_bias'], 'v_a_log': out['v_a_log'], 'v_d_skip': out['v_d_skip'], 'v_ssd_norm_w': out['v_ssd_norm_w'], 'v_q_norm_w': out['v_q_norm_w'], 'v_w_uq': out['v_w_uq'], 'v_kv_norm_w': out['v_kv_norm_w'], 'v_w_ukv': out['v_w_ukv'], 'v_mla_norm_w': out['v_mla_norm_w'], 'v_w_out': out['v_w_out'], 'v_norm_ffn2': out['v_norm_ffn2'], 'v_ffn2_w_gate': out['v_ffn2_w_gate'], 'v_ffn2_w_up': out['v_ffn2_w_up'], 'v_ffn2_w_down': out['v_ffn2_w_down'], 'v_norm_final': out['v_norm_final']}


def _loss(weights, diff, rest, loss_target):
    with _jax.named_scope("forward"):
        args = {**rest, TWIN_DIFF_INPUT: diff, **{k: w.astype(_WEIGHT_DTYPES[k]) for k, w in weights.items()}}
        y = _forward(args)
    with _jax.named_scope("loss_head"):
        err = _jnp.square(y.astype(_jnp.float32) - loss_target)
        return 0.5 * _jnp.sum(_jnp.mean(err, axis=-1)) if err.ndim else 0.5 * err


def _adamw(w, g, m, v):
    m = ADAM_B1 * m + (1.0 - ADAM_B1) * g
    v = ADAM_B2 * v + (1.0 - ADAM_B2) * _jnp.square(g)
    m_hat = m / (1.0 - ADAM_B1 ** ADAM_STEP)
    v_hat = v / (1.0 - ADAM_B2 ** ADAM_STEP)
    delta = -ADAM_LR * (m_hat / (_jnp.sqrt(v_hat) + ADAM_EPS) + ADAM_WD * w)
    return delta, m, v


def reference(x, c, positions, w_ada, b_ada, norm_ffn1, ffn1_w_gate, ffn1_w_up, ffn1_w_down, norm_mix, w_in, conv_w, conv_b, dt_bias, a_log, d_skip, ssd_norm_w, q_norm_w, w_uq, kv_norm_w, w_ukv, mla_norm_w, w_out, norm_ffn2, ffn2_w_gate, ffn2_w_up, ffn2_w_down, norm_final, loss_target, m_w_ada, m_b_ada, m_norm_ffn1, m_ffn1_w_gate, m_ffn1_w_up, m_ffn1_w_down, m_norm_mix, m_w_in, m_conv_w, m_conv_b, m_dt_bias, m_a_log, m_d_skip, m_ssd_norm_w, m_q_norm_w, m_w_uq, m_kv_norm_w, m_w_ukv, m_mla_norm_w, m_w_out, m_norm_ffn2, m_ffn2_w_gate, m_ffn2_w_up, m_ffn2_w_down, m_norm_final, v_w_ada, v_b_ada, v_norm_ffn1, v_ffn1_w_gate, v_ffn1_w_up, v_ffn1_w_down, v_norm_mix, v_w_in, v_conv_w, v_conv_b, v_dt_bias, v_a_log, v_d_skip, v_ssd_norm_w, v_q_norm_w, v_w_uq, v_kv_norm_w, v_w_ukv, v_mla_norm_w, v_w_out, v_norm_ffn2, v_ffn2_w_gate, v_ffn2_w_up, v_ffn2_w_down, v_norm_final):
    given = dict(x=x, c=c, positions=positions, w_ada=w_ada, b_ada=b_ada, norm_ffn1=norm_ffn1, ffn1_w_gate=ffn1_w_gate, ffn1_w_up=ffn1_w_up, ffn1_w_down=ffn1_w_down, norm_mix=norm_mix, w_in=w_in, conv_w=conv_w, conv_b=conv_b, dt_bias=dt_bias, a_log=a_log, d_skip=d_skip, ssd_norm_w=ssd_norm_w, q_norm_w=q_norm_w, w_uq=w_uq, kv_norm_w=kv_norm_w, w_ukv=w_ukv, mla_norm_w=mla_norm_w, w_out=w_out, norm_ffn2=norm_ffn2, ffn2_w_gate=ffn2_w_gate, ffn2_w_up=ffn2_w_up, ffn2_w_down=ffn2_w_down, norm_final=norm_final, loss_target=loss_target, m_w_ada=m_w_ada, m_b_ada=m_b_ada, m_norm_ffn1=m_norm_ffn1, m_ffn1_w_gate=m_ffn1_w_gate, m_ffn1_w_up=m_ffn1_w_up, m_ffn1_w_down=m_ffn1_w_down, m_norm_mix=m_norm_mix, m_w_in=m_w_in, m_conv_w=m_conv_w, m_conv_b=m_conv_b, m_dt_bias=m_dt_bias, m_a_log=m_a_log, m_d_skip=m_d_skip, m_ssd_norm_w=m_ssd_norm_w, m_q_norm_w=m_q_norm_w, m_w_uq=m_w_uq, m_kv_norm_w=m_kv_norm_w, m_w_ukv=m_w_ukv, m_mla_norm_w=m_mla_norm_w, m_w_out=m_w_out, m_norm_ffn2=m_norm_ffn2, m_ffn2_w_gate=m_ffn2_w_gate, m_ffn2_w_up=m_ffn2_w_up, m_ffn2_w_down=m_ffn2_w_down, m_norm_final=m_norm_final, v_w_ada=v_w_ada, v_b_ada=v_b_ada, v_norm_ffn1=v_norm_ffn1, v_ffn1_w_gate=v_ffn1_w_gate, v_ffn1_w_up=v_ffn1_w_up, v_ffn1_w_down=v_ffn1_w_down, v_norm_mix=v_norm_mix, v_w_in=v_w_in, v_conv_w=v_conv_w, v_conv_b=v_conv_b, v_dt_bias=v_dt_bias, v_a_log=v_a_log, v_d_skip=v_d_skip, v_ssd_norm_w=v_ssd_norm_w, v_q_norm_w=v_q_norm_w, v_w_uq=v_w_uq, v_kv_norm_w=v_kv_norm_w, v_w_ukv=v_w_ukv, v_mla_norm_w=v_mla_norm_w, v_w_out=v_w_out, v_norm_ffn2=v_norm_ffn2, v_ffn2_w_gate=v_ffn2_w_gate, v_ffn2_w_up=v_ffn2_w_up, v_ffn2_w_down=v_ffn2_w_down, v_norm_final=v_norm_final)
    weights = {n: given[n] for n in TWIN_WEIGHTS}
    shared = {n: given[n] for n in SHARED_INPUTS}
    per_example = {n: given[n] for n in ['x', 'c', 'positions']}
    grad_fn = _jax.value_and_grad(_loss, argnums=(0, 1))

    def one_microbatch(ex, loss_target):
        ex = dict(ex)
        diff = ex.pop(TWIN_DIFF_INPUT)
        return grad_fn(weights, diff, {**shared, **ex}, loss_target)

    if N_MICROBATCH == 1:
        loss, (grad_w, grad_x) = one_microbatch(per_example, given["loss_target"])
    else:
        def body(carry, xs):
            loss_sum, grad_sum = carry
            l_k, (gw_k, gx_k) = one_microbatch(xs[0], xs[1])
            with _jax.named_scope("update"):
                return (loss_sum + l_k, _jax.tree.map(_jnp.add, grad_sum, gw_k)), gx_k

        init = (_jnp.zeros((), _jnp.float32), _jax.tree.map(_jnp.zeros_like, weights))
        (loss, grad_w), grad_x = _jax.lax.scan(body, init, (per_example, given["loss_target"]))
    with _jax.named_scope("update"):
        delta_w, new_m, new_v = {}, {}, {}
        for n in TWIN_WEIGHTS:
            delta_w[n], new_m[n], new_v[n] = _adamw(weights[n], grad_w[n], given["m_" + n], given["v_" + n])
    return (loss, grad_x, *[grad_w[n] for n in TWIN_WEIGHTS], *[delta_w[n] for n in TWIN_WEIGHTS],
            *[new_m[n] for n in TWIN_WEIGHTS], *[new_v[n] for n in TWIN_WEIGHTS])
```

```python
import math

import jax
import jax.numpy as jnp
from jax import lax
from jax.experimental import pallas as pl
from jax.experimental.pallas import tpu as pltpu

F32, BF16, I32 = jnp.float32, jnp.bfloat16, jnp.int32
HI = lax.Precision.HIGHEST
SDS = jax.ShapeDtypeStruct
MESH = pl.DeviceIdType.MESH

D_MODEL = 1024
D_FF = 2816
D_SSD = 1024
SSD_HEADS = 16
SSD_HEAD_DIM = 64
SSD_GROUPS = 2
SSD_STATE = 128
CHUNK = 128
MLA_HEADS = 8
QK_NOPE = 64
QK_ROPE = 32
QK_DIM = 96
V_HEAD = 128
Q_LORA = 384
KV_LORA = 256
ROPE_THETA = 10000.0
N_MOD = 9
EPS = 1e-6
D_CONV = 1536
D_IN = 3248
D_IN_PAD = 3328
HEAD_PAD = 128
N_DEV = 8
ADAM_LR, ADAM_B1, ADAM_B2, ADAM_EPS, ADAM_WD, ADAM_STEP = 0.001, 0.9, 0.999, 1e-08, 0.01, 10

VMEM_LIMIT = 56 * 1024 * 1024
LANES = 128
NT_DIMS = (((1,), (1,)), ((), ()))
TN_DIMS = (((0,), (0,)), ((), ()))


def _cparams(n_axes):
    return pltpu.CompilerParams(dimension_semantics=("arbitrary",) * n_axes, vmem_limit_bytes=VMEM_LIMIT)


def _row(tm, d):
    return pl.BlockSpec((None, tm, d), lambda b, i: (b, i, 0))


def _bvec(d):
    return pl.BlockSpec((None, 1, d), lambda b, i: (b, 0, 0))


def _full(shape):
    n = len(shape)
    return pl.BlockSpec(shape, lambda *_: (0,) * n)


def _sigmoid(x):
    return 1.0 / (1.0 + jnp.exp(-x))


def _softplus(x):
    return jnp.maximum(x, 0.0) + jnp.log(1.0 + jnp.exp(-jnp.abs(x)))


def _rms(x):
    return lax.rsqrt(jnp.mean(x * x, axis=-1, keepdims=True) + EPS)


def _rms_bwd(dn, n, r):
    return r * (dn - n * jnp.mean(dn * n, axis=-1, keepdims=True))


def _first_step():
    return (pl.program_id(0) == 0) & (pl.program_id(1) == 0)


def all_gather8(x, name):
    r, c = x.shape

    def body(x_ref, out_ref, send_sems, recv_sems, local_sem):
        mx, my, mc = lax.axis_index("x"), lax.axis_index("y"), lax.axis_index("c")
        me, sibling = (mx, my, mc), (mx, my, 1 - mc)
        chips = [(1 - mx, my), (mx, 1 - my), (1 - mx, 1 - my)]

        def rows(px, py, pc):
            return out_ref.at[4 * px + 2 * py + pc]

        def copy(k, block, to, src=None):
            return pltpu.make_async_remote_copy(
                src_ref=rows(*block) if src is None else src, dst_ref=rows(*block),
                send_sem=send_sems.at[k], recv_sem=recv_sems.at[k], device_id=to, device_id_type=MESH)

        mine = pltpu.make_async_copy(x_ref, rows(*me), local_sem)
        mine.start()
        first = [copy(0, me, sibling, src=x_ref)]
        first += [copy(1 + j, me, (*chip, mc), src=x_ref) for j, chip in enumerate(chips)]
        for cp in first:
            cp.start()
        passed = [copy(4 + j, (*chip, mc), sibling) for j, chip in enumerate(chips)]
        for j, chip in enumerate(chips):
            copy(1 + j, (*chip, mc), me).wait_recv()
            passed[j].start()
        copy(0, sibling, me).wait_recv()
        for j, chip in enumerate(chips):
            copy(4 + j, (*chip, 1 - mc), me).wait_recv()
        for cp in first + passed:
            cp.wait_send()
        mine.wait()

    return pl.pallas_call(
        body, name=name,
        out_shape=SDS((N_DEV, r, c), x.dtype),
        in_specs=[pl.BlockSpec(memory_space=pl.ANY)],
        out_specs=pl.BlockSpec(memory_space=pl.ANY),
        scratch_shapes=[pltpu.SemaphoreType.DMA((7,)), pltpu.SemaphoreType.DMA((7,)), pltpu.SemaphoreType.DMA],
    )(x)


def all_to_all8(x, name):
    _, r, c = x.shape

    def body(x_ref, out_ref, send_sems, recv_sems, local_sem):
        mx, my, mc = lax.axis_index("x"), lax.axis_index("y"), lax.axis_index("c")
        me = 4 * mx + 2 * my + mc
        mine = pltpu.make_async_copy(x_ref.at[me], out_ref.at[me], local_sem)
        mine.start()
        copies = []
        for rel in range(1, N_DEV):
            px = 1 - mx if rel & 4 else mx
            py = 1 - my if rel & 2 else my
            pc = 1 - mc if rel & 1 else mc
            cp = pltpu.make_async_remote_copy(
                src_ref=x_ref.at[4 * px + 2 * py + pc], dst_ref=out_ref.at[me],
                send_sem=send_sems.at[rel - 1], recv_sem=recv_sems.at[rel - 1],
                device_id=(px, py, pc), device_id_type=MESH)
            cp.start()
            copies.append(cp)
        for cp in copies:
            cp.wait()
        mine.wait()

    return pl.pallas_call(
        body, name=name,
        out_shape=SDS((N_DEV, r, c), x.dtype),
        in_specs=[pl.BlockSpec(memory_space=pl.ANY)],
        out_specs=pl.BlockSpec(memory_space=pl.ANY),
        scratch_shapes=[pltpu.SemaphoreType.DMA((7,)), pltpu.SemaphoreType.DMA((7,)), pltpu.SemaphoreType.DMA],
    )(x)


def norm_mod(x, w, sc, sh, name):
    b, s, d = x.shape
    tm = min(512, s)

    def body(x_ref, w_ref, sc_ref, sh_ref, h_ref):
        xv = x_ref[...]
        n = xv * _rms(xv)
        h_ref[...] = ((n * w_ref[...]) * (1.0 + sc_ref[...]) + sh_ref[...]).astype(BF16)

    return pl.pallas_call(
        body, name=name, grid=(b, s // tm),
        in_specs=[_row(tm, d), _full((1, d)), _bvec(d), _bvec(d)],
        out_specs=_row(tm, d), out_shape=SDS((b, s, d), BF16), compiler_params=_cparams(2))(x, w, sc, sh)


def ffn_up(h, wg_t, wu_t, name):
    b, s, d = h.shape
    f = wg_t.shape[0]
    tm, tn = min(512, s), f // 2

    def body(h_ref, wg_ref, wu_ref, g_ref, u_ref, a_ref):
        hv = h_ref[...]
        g = lax.dot_general(hv, wg_ref[...], NT_DIMS, preferred_element_type=F32)
        u = lax.dot_general(hv, wu_ref[...], NT_DIMS, preferred_element_type=F32)
        g_ref[...] = g
        u_ref[...] = u
        a_ref[...] = (g * _sigmoid(g) * u).astype(BF16)

    hs = pl.BlockSpec((None, tm, d), lambda j, bb, i: (bb, i, 0))
    ws = pl.BlockSpec((tn, d), lambda j, bb, i: (j, 0))
    os_ = pl.BlockSpec((None, tm, tn), lambda j, bb, i: (bb, i, j))
    return pl.pallas_call(
        body, name=name, grid=(f // tn, b, s // tm),
        in_specs=[hs, ws, ws], out_specs=[os_, os_, os_],
        out_shape=[SDS((b, s, f), F32), SDS((b, s, f), F32), SDS((b, s, f), BF16)],
        compiler_params=_cparams(3))(h, wg_t, wu_t)


def ffn_down(a, wd, x, gate, scale, name):
    b, s, f = a.shape
    d = wd.shape[1]
    tm = min(512, s)

    def body(a_ref, wd_ref, x_ref, g_ref, xn_ref, o_ref):
        o = jnp.dot(a_ref[...], wd_ref[...], preferred_element_type=F32)
        xn_ref[...] = x_ref[...] + (scale * g_ref[...]) * o
        o_ref[...] = o.astype(BF16)

    return pl.pallas_call(
        body, name=name, grid=(b, s // tm),
        in_specs=[_row(tm, f), _full((f, d)), _row(tm, d), _bvec(d)],
        out_specs=[_row(tm, d), _row(tm, d)],
        out_shape=[SDS((b, s, d), F32), SDS((b, s, d), BF16)], compiler_params=_cparams(2))(a, wd, x, gate)


def gate_bwd(dxn, o, gate, scale, name):
    b, s, d = dxn.shape
    tm = min(512, s)

    def body(dx_ref, o_ref, g_ref, do_ref, dg_ref):
        @pl.when(pl.program_id(1) == 0)
        def _():
            dg_ref[...] = jnp.zeros_like(dg_ref)
        dx = dx_ref[...]
        do_ref[...] = ((scale * g_ref[...]) * dx).astype(BF16)
        dg_ref[...] += jnp.sum(scale * dx * o_ref[...].astype(F32), axis=0, keepdims=True)

    return pl.pallas_call(
        body, name=name, grid=(b, s // tm),
        in_specs=[_row(tm, d), _row(tm, d), _bvec(d)],
        out_specs=[_row(tm, d), _bvec(d)],
        out_shape=[SDS((b, s, d), BF16), SDS((b, 1, d), F32)], compiler_params=_cparams(2))(dxn, o, gate)


def ffn_dact(do, wd, g, u, name):
    b, s, d = do.shape
    f = wd.shape[0]
    tm, tn = min(512, s), f // 2

    def body(do_ref, wd_ref, g_ref, u_ref, dg_ref, du_ref):
        da = lax.dot_general(do_ref[...], wd_ref[...], NT_DIMS, preferred_element_type=F32)
        gv = g_ref[...]
        sg = _sigmoid(gv)
        dg_ref[...] = (da * u_ref[...] * (sg * (1.0 + gv * (1.0 - sg)))).astype(BF16)
        du_ref[...] = (da * (gv * sg)).astype(BF16)

    dos = pl.BlockSpec((None, tm, d), lambda j, bb, i: (bb, i, 0))
    ws = pl.BlockSpec((tn, d), lambda j, bb, i: (j, 0))
    es = pl.BlockSpec((None, tm, tn), lambda j, bb, i: (bb, i, j))
    return pl.pallas_call(
        body, name=name, grid=(f // tn, b, s // tm),
        in_specs=[dos, ws, es, es], out_specs=[es, es],
        out_shape=[SDS((b, s, f), BF16), SDS((b, s, f), BF16)], compiler_params=_cparams(3))(do, wd, g, u)


def mm_tn(a, bm, tma, tnb, name):
    b, s, ka = a.shape
    nb = bm.shape[2]
    tk = min(512, s)

    def body(a_ref, b_ref, o_ref):
        @pl.when((pl.program_id(2) == 0) & (pl.program_id(3) == 0))
        def _():
            o_ref[...] = jnp.zeros_like(o_ref)
        o_ref[...] += lax.dot_general(a_ref[...], b_ref[...], TN_DIMS, preferred_element_type=F32)

    return pl.pallas_call(
        body, name=name, grid=(ka // tma, nb // tnb, b, s // tk),
        in_specs=[pl.BlockSpec((None, tk, tma), lambda i, j, bb, k: (bb, k, i)),
                  pl.BlockSpec((None, tk, tnb), lambda i, j, bb, k: (bb, k, j))],
        out_specs=pl.BlockSpec((tma, tnb), lambda i, j, bb, k: (i, j)),
        out_shape=SDS((ka, nb), F32), compiler_params=_cparams(4))(a, bm)


def dh_norm_bwd(dys, wts, x, dxn, w, sc, name):
    b, s, d = x.shape
    tm = min(256, s)
    n_in = len(dys)

    def body(*refs):
        dy_refs, w_refs = refs[:n_in], refs[n_in:2 * n_in]
        x_ref, dxn_ref, nw_ref, sc_ref, dx_ref, dsc_ref, dsh_ref, dw_ref = refs[2 * n_in:]

        @pl.when(pl.program_id(1) == 0)
        def _():
            dsc_ref[...] = jnp.zeros_like(dsc_ref)
            dsh_ref[...] = jnp.zeros_like(dsh_ref)

        @pl.when(_first_step())
        def _():
            dw_ref[...] = jnp.zeros_like(dw_ref)

        dh = jnp.dot(dy_refs[0][...], w_refs[0][...], preferred_element_type=F32)
        for k in range(1, n_in):
            dh += jnp.dot(dy_refs[k][...], w_refs[k][...], preferred_element_type=F32)
        xv = x_ref[...]
        r = _rms(xv)
        n = xv * r
        nw = nw_ref[...]
        dsc_ref[...] += jnp.sum(dh * (n * nw), axis=0, keepdims=True)
        dsh_ref[...] += jnp.sum(dh, axis=0, keepdims=True)
        dhn = dh * (1.0 + sc_ref[...])
        dw_ref[...] += jnp.sum(dhn * n, axis=0, keepdims=True)
        dx_ref[...] = dxn_ref[...] + _rms_bwd(dhn * nw, n, r)

    in_specs = [_row(tm, dy.shape[2]) for dy in dys] + [_full(wt.shape) for wt in wts]
    in_specs += [_row(tm, d), _row(tm, d), _full((1, d)), _bvec(d)]
    return pl.pallas_call(
        body, name=name, grid=(b, s // tm), in_specs=in_specs,
        out_specs=[_row(tm, d), _bvec(d), _bvec(d), _full((1, d))],
        out_shape=[SDS((b, s, d), F32), SDS((b, 1, d), F32), SDS((b, 1, d), F32), SDS((1, d), F32)],
        compiler_params=_cparams(2))(*dys, *wts, x, dxn, w, sc)


def final_loss(x, w, tgt, name):
    b, s, d = x.shape
    tm = min(512, s)

    def body(x_ref, w_ref, t_ref, loss_ref, dx_ref, dw_ref):
        @pl.when(_first_step())
        def _():
            loss_ref[...] = jnp.zeros_like(loss_ref)
            dw_ref[...] = jnp.zeros_like(dw_ref)
        xv = x_ref[...]
        r = _rms(xv)
        n = xv * r
        wv = w_ref[...]
        e = n * wv - t_ref[...]
        loss_ref[...] += jnp.sum(e * e) * (0.5 / d)
        dy = e * (1.0 / d)
        dw_ref[...] += jnp.sum(dy * n, axis=0, keepdims=True)
        dx_ref[...] = _rms_bwd(dy * wv, n, r)

    return pl.pallas_call(
        body, name=name, grid=(b, s // tm),
        in_specs=[_row(tm, d), _full((1, d)), _row(tm, d)],
        out_specs=[_full((1, LANES)), _row(tm, d), _full((1, d))],
        out_shape=[SDS((1, LANES), F32), SDS((b, s, d), F32), SDS((1, d), F32)],
        compiler_params=_cparams(2))(x, w, tgt)


def in_proj(h, win_t, name):
    b, s, d = h.shape
    tm = min(256, s)
    widths = (D_SSD, D_SSD + 2 * SSD_GROUPS * SSD_STATE, Q_LORA, KV_LORA, LANES)

    def body(h_ref, w_ref, *outs):
        p = lax.dot_general(h_ref[...], w_ref[...], NT_DIMS, preferred_element_type=F32)
        off = 0
        for o_ref, wd in zip(outs, widths):
            o_ref[...] = p[:, off:off + wd]
            off += wd

    return pl.pallas_call(
        body, name=name, grid=(b, s // tm),
        in_specs=[_row(tm, d), _full(win_t.shape)],
        out_specs=[_row(tm, wd) for wd in widths],
        out_shape=[SDS((b, s, wd), F32) for wd in widths], compiler_params=_cparams(2))(h, win_t)


def _halo_prev(ts, d):
    return pl.BlockSpec((None, 8, d), lambda b, i: (b, jnp.maximum(i * (ts // 8) - 1, 0), 0))


def _conv_taps(ext_ref, w_ref, ts):
    return [ext_ref[5 + k:5 + k + ts, :] for k in range(4)], [w_ref[k:k + 1, :] for k in range(4)]


def conv_fwd(u, cw, cb, name):
    b, s, dc = u.shape
    ts = min(512, s)
    widths = (D_SSD, SSD_GROUPS * SSD_STATE, SSD_GROUPS * SSD_STATE)

    def body(u_ref, up_ref, w_ref, b_ref, xs_ref, bm_ref, cm_ref, ext):
        ext[0:8, :] = jnp.where(pl.program_id(1) > 0, up_ref[...], 0.0)
        ext[8:8 + ts, :] = u_ref[...]
        taps, ws = _conv_taps(ext, w_ref, ts)
        v = b_ref[...] + taps[0] * ws[0] + taps[1] * ws[1] + taps[2] * ws[2] + taps[3] * ws[3]
        y = v * _sigmoid(v)
        xs_ref[...] = y[:, 0:D_SSD]
        bm_ref[...] = y[:, D_SSD:D_SSD + 256]
        cm_ref[...] = y[:, D_SSD + 256:D_SSD + 512]

    return pl.pallas_call(
        body, name=name, grid=(b, s // ts),
        in_specs=[_row(ts, dc), _halo_prev(ts, dc), _full((4, dc)), _full((1, dc))],
        out_specs=[_row(ts, wd) for wd in widths],
        out_shape=[SDS((b, s, wd), F32) for wd in widths],
        scratch_shapes=[pltpu.VMEM((ts + 8, dc), F32)], compiler_params=_cparams(2))(u, u, cw, cb)


def conv_bwd_a(dxs, dbm, dcm, u, cw, cb, name):
    b, s, dc = u.shape
    ts = min(512, s)

    def body(dxs_ref, dbm_ref, dcm_ref, u_ref, up_ref, w_ref, b_ref, dv_ref, dwb_ref, ext):
        @pl.when(_first_step())
        def _():
            dwb_ref[...] = jnp.zeros_like(dwb_ref)
        ext[0:8, :] = jnp.where(pl.program_id(1) > 0, up_ref[...], 0.0)
        ext[8:8 + ts, :] = u_ref[...]
        taps, ws = _conv_taps(ext, w_ref, ts)
        v = b_ref[...] + taps[0] * ws[0] + taps[1] * ws[1] + taps[2] * ws[2] + taps[3] * ws[3]
        sg = _sigmoid(v)
        dy = jnp.concatenate([dxs_ref[...], dbm_ref[...], dcm_ref[...]], axis=1)
        dv = dy * (sg * (1.0 + v * (1.0 - sg)))
        dv_ref[...] = dv
        for k in range(4):
            dwb_ref[k:k + 1, :] += jnp.sum(dv * taps[k], axis=0, keepdims=True)
        dwb_ref[4:5, :] += jnp.sum(dv, axis=0, keepdims=True)

    return pl.pallas_call(
        body, name=name, grid=(b, s // ts),
        in_specs=[_row(ts, D_SSD), _row(ts, 256), _row(ts, 256), _row(ts, dc), _halo_prev(ts, dc),
                  _full((4, dc)), _full((1, dc))],
        out_specs=[_row(ts, dc), _full((8, dc))],
        out_shape=[SDS((b, s, dc), F32), SDS((8, dc), F32)],
        scratch_shapes=[pltpu.VMEM((ts + 8, dc), F32)], compiler_params=_cparams(2))(dxs, dbm, dcm, u, u, cw, cb)


def conv_bwd_b(dv, cw, name):
    b, s, dc = dv.shape
    ts = min(512, s)
    nt = s // ts

    def body(dv_ref, dn_ref, w_ref, du_ref, ext):
        ext[0:ts, :] = dv_ref[...]
        ext[ts:ts + 8, :] = jnp.where(pl.program_id(1) < nt - 1, dn_ref[...], 0.0)
        acc = ext[3:3 + ts, :] * w_ref[0:1, :]
        for k in range(1, 4):
            acc += ext[3 - k:3 - k + ts, :] * w_ref[k:k + 1, :]
        du_ref[...] = acc.astype(BF16)

    nxt = pl.BlockSpec((None, 8, dc), lambda bb, i: (bb, jnp.minimum((i + 1) * (ts // 8), s // 8 - 1), 0))
    return pl.pallas_call(
        body, name=name, grid=(b, nt),
        in_specs=[_row(ts, dc), nxt, _full((4, dc))],
        out_specs=_row(ts, dc), out_shape=SDS((b, s, dc), BF16),
        scratch_shapes=[pltpu.VMEM((ts + 8, dc), F32)], compiler_params=_cparams(2))(dv, dv, cw)


def _ssd_common(misc_ref, dtb_ref, alog_ref, e_ref):
    ln = CHUNK
    lane = lax.broadcasted_iota(I32, (ln, LANES), 1)
    lane1 = lax.broadcasted_iota(I32, (1, LANES), 1)
    pre = misc_ref[...] + dtb_ref[...]
    dt_s = jnp.where(lane < SSD_HEADS, _softplus(pre), 0.0)
    a_neg = jnp.where(lane1 < SSD_HEADS, -jnp.exp(alog_ref[...]), 0.0)
    ri = lax.broadcasted_iota(I32, (ln, ln), 0)
    ci = lax.broadcasted_iota(I32, (ln, ln), 1)
    tril = ci <= ri
    acum = jnp.dot(tril.astype(F32), dt_s * a_neg, preferred_element_type=F32, precision=HI)
    e = e_ref[...]
    expand = lambda t: jnp.dot(t, e, preferred_element_type=F32, precision=HI)
    last = acum[ln - 1:ln, :]
    delta_s = jnp.exp(last - acum)
    return dict(pre=pre, dt_s=dt_s, a_neg=a_neg, tril=tril, ri=ri, ci=ci, acum=acum, acum_t=acum.T,
                dt_e=expand(dt_s), eac_e=expand(jnp.exp(acum)), delta_s=delta_s, del_e=expand(delta_s))


def _decay(cm, h):
    seg = cm["acum"][:, h:h + 1] - cm["acum_t"][h:h + 1, :]
    return jnp.exp(jnp.where(cm["tril"], seg, -jnp.inf))


def ssd_fwd(xs, bm, cm_, misc, z, dtb, alog, dskip_e, norm_w, e_mat, name):
    b, s, _ = xs.shape
    ln, nc = CHUNK, s // CHUNK
    gw = D_SSD // SSD_GROUPS
    hpg = SSD_HEADS // SSD_GROUPS

    def body(xs_ref, b_ref, c_ref, misc_ref, z_ref, dtb_ref, alog_ref, dsk_ref, nw_ref, e_ref,
             ys_ref, y_ref, p_ref, st, yd):
        @pl.when(pl.program_id(1) == 0)
        def _():
            st[...] = jnp.zeros_like(st)
        cm = _ssd_common(misc_ref, dtb_ref, alog_ref, e_ref)
        xsv = xs_ref[...]
        xdt = xsv * cm["dt_e"]
        xdt_b = xdt.astype(BF16)
        xd_b = (xdt * cm["del_e"]).astype(BF16)
        gam_e = cm["eac_e"][ln - 1:ln, :]
        p_ref[...] = st[...]
        yoff = []
        for g in range(SSD_GROUPS):
            gs = slice(gw * g, gw * (g + 1))
            bg = b_ref[:, SSD_STATE * g:SSD_STATE * (g + 1)].astype(BF16)
            cg = c_ref[:, SSD_STATE * g:SSD_STATE * (g + 1)].astype(BF16)
            cb = lax.dot_general(cg, bg, NT_DIMS, preferred_element_type=F32)
            st_g = st[:, gs]
            yoff.append(jnp.dot(cg, st_g.astype(BF16), preferred_element_type=F32) * cm["eac_e"][:, gs])
            for j in range(hpg):
                h = hpg * g + j
                hs = slice(SSD_HEAD_DIM * h, SSD_HEAD_DIM * (h + 1))
                m = (cb * _decay(cm, h)).astype(BF16)
                yd[:, hs] = jnp.dot(m, xdt_b[:, hs], preferred_element_type=F32)
            new = lax.dot_general(bg, xd_b[:, gs], TN_DIMS, preferred_element_type=F32)
            st[:, gs] = st_g * gam_e[:, gs] + new
        y = yd[...] + jnp.concatenate(yoff, axis=1) + dsk_ref[...] * xsv
        y_ref[...] = y
        zz = z_ref[...]
        yg = y * (zz * _sigmoid(zz))
        outs = []
        for g in range(SSD_GROUPS):
            ygg = yg[:, gw * g:gw * (g + 1)]
            outs.append(ygg * _rms(ygg) * nw_ref[:, gw * g:gw * (g + 1)])
        ys_ref[...] = jnp.concatenate(outs, axis=1).astype(BF16)

    row = lambda d: pl.BlockSpec((None, ln, d), lambda bb, c: (bb, c, 0))
    return pl.pallas_call(
        body, name=name, grid=(b, nc),
        in_specs=[row(D_SSD), row(256), row(256), row(LANES), row(D_SSD), _full((1, LANES)), _full((1, LANES)),
                  _full((1, D_SSD)), _full((1, D_SSD)), _full((LANES, D_SSD))],
        out_specs=[row(D_SSD), row(D_SSD), pl.BlockSpec((None, None, SSD_STATE, D_SSD), lambda bb, c: (bb, c, 0, 0))],
        out_shape=[SDS((b, s, D_SSD), BF16), SDS((b, s, D_SSD), F32), SDS((b, nc, SSD_STATE, D_SSD), F32)],
        scratch_shapes=[pltpu.VMEM((SSD_STATE, D_SSD), F32), pltpu.VMEM((ln, D_SSD), F32)],
        compiler_params=_cparams(2))(xs, bm, cm_, misc, z, dtb, alog, dskip_e, norm_w, e_mat)


def ssd_bwd(dys, y, z, xs, bm, cm_, misc, prev, dtb, alog, dskip_e, norm_w, e_mat, et_mat, name):
    b, s, _ = xs.shape
    ln, nc = CHUNK, s // CHUNK
    gw = D_SSD // SSD_GROUPS
    hpg = SSD_HEADS // SSD_GROUPS

    def body(dys_ref, y_ref, z_ref, xs_ref, b_ref, c_ref, misc_ref, p_ref, dtb_ref, alog_ref, dsk_ref, nw_ref,
             e_ref, et_ref, dxs_ref, db_ref, dc_ref, dz_ref, ddt_ref, dnw_ref, ddsk_ref, ddtb_ref, dalog_ref,
             dst, dxd, dac_t):
        @pl.when(_first_step())
        def _():
            for r_ in (dnw_ref, ddsk_ref, ddtb_ref, dalog_ref):
                r_[...] = jnp.zeros_like(r_)

        @pl.when(pl.program_id(1) == 0)
        def _():
            dst[...] = jnp.zeros_like(dst)

        cm = _ssd_common(misc_ref, dtb_ref, alog_ref, e_ref)
        et = et_ref[...]
        squeeze = lambda t: jnp.dot(t, et, preferred_element_type=F32, precision=HI)
        lane = lax.broadcasted_iota(I32, (ln, LANES), 1)
        sub = lax.broadcasted_iota(I32, (LANES, ln), 0)
        xsv = xs_ref[...]
        xdt = xsv * cm["dt_e"]
        xdt_b = xdt.astype(BF16)
        xd_b = (xdt * cm["del_e"]).astype(BF16)
        eac_e = cm["eac_e"]
        gam_e = eac_e[ln - 1:ln, :]

        yv, zz, dyo = y_ref[...], z_ref[...], dys_ref[...]
        sz = _sigmoid(zz)
        silu_z = zz * sz
        yg = yv * silu_z
        dyg, dnw = [], []
        for g in range(SSD_GROUPS):
            gs = slice(gw * g, gw * (g + 1))
            ygg = yg[:, gs]
            r = _rms(ygg)
            n = ygg * r
            dnw.append(jnp.sum(dyo[:, gs] * n, axis=0, keepdims=True))
            dyg.append(_rms_bwd(dyo[:, gs] * nw_ref[:, gs], n, r))
        dyg = jnp.concatenate(dyg, axis=1)
        dnw_ref[...] += jnp.concatenate(dnw, axis=1)
        dz_ref[...] = (dyg * yv * (sz * (1.0 + zz * (1.0 - sz)))).astype(BF16)
        dy = dyg * silu_z
        ddsk_ref[...] += jnp.sum(dy * xsv, axis=0, keepdims=True)
        dy_b = dy.astype(BF16)

        dacum = jnp.zeros((ln, LANES), F32)
        dac_t[...] = jnp.zeros_like(dac_t)
        w1, dgam = [], []
        for g in range(SSD_GROUPS):
            gs = slice(gw * g, gw * (g + 1))
            ss = slice(SSD_STATE * g, SSD_STATE * (g + 1))
            bg = b_ref[:, ss].astype(BF16)
            cg = c_ref[:, ss].astype(BF16)
            cb = lax.dot_general(cg, bg, NT_DIMS, preferred_element_type=F32)
            pt = p_ref[:, gs]
            pt_b = pt.astype(BF16)
            dst_g = dst[:, gs]
            dst_b = dst_g.astype(BF16)
            edy = (dy[:, gs] * eac_e[:, gs]).astype(BF16)
            dcg = lax.dot_general(edy, pt_b, NT_DIMS, preferred_element_type=F32)
            dpt = lax.dot_general(cg, edy, TN_DIMS, preferred_element_type=F32)
            yoff = jnp.dot(cg, pt_b, preferred_element_type=F32) * eac_e[:, gs]
            dxd_g = jnp.dot(bg, dst_b, preferred_element_type=F32)
            dbg = lax.dot_general(xd_b[:, gs], dst_b, NT_DIMS, preferred_element_type=F32)
            ddel = dxd_g * xdt[:, gs] * cm["del_e"][:, gs]
            w1.append(dy[:, gs] * yoff - ddel)
            dgam.append(jnp.sum(ddel, axis=0, keepdims=True) + jnp.sum(dst_g * pt, axis=0, keepdims=True) * gam_e[:, gs])
            dxd[:, gs] = dxd_g * cm["del_e"][:, gs]
            dst[:, gs] = dst_g * gam_e[:, gs] + dpt
            dcb = jnp.zeros((ln, ln), F32)
            for j in range(hpg):
                h = hpg * g + j
                hs = slice(SSD_HEAD_DIM * h, SSD_HEAD_DIM * (h + 1))
                lam = _decay(cm, h)
                m = cb * lam
                dm = lax.dot_general(dy_b[:, hs], xdt_b[:, hs], NT_DIMS, preferred_element_type=F32)
                dxd[:, hs] += lax.dot_general(m.astype(BF16), dy_b[:, hs], TN_DIMS, preferred_element_type=F32)
                dcb += dm * lam
                wl = dm * m
                dacum += jnp.where(lane == h, jnp.sum(wl, axis=1, keepdims=True), 0.0)
                dac_t[...] -= jnp.where(sub == h, jnp.sum(wl, axis=0, keepdims=True), 0.0)
            dcb_b = dcb.astype(BF16)
            dc_ref[:, ss] = dcg + jnp.dot(dcb_b, bg, preferred_element_type=F32)
            db_ref[:, ss] = dbg + lax.dot_general(dcb_b, cg, TN_DIMS, preferred_element_type=F32)

        dxdt = dxd[...]
        dxs_ref[...] = dy * dsk_ref[...] + dxdt * cm["dt_e"]
        dacum += squeeze(jnp.concatenate(w1, axis=1)) + dac_t[...].T
        dlast = squeeze(jnp.broadcast_to(jnp.concatenate(dgam, axis=1), (8, D_SSD)))[0:1, :]
        dacum += jnp.where(lax.broadcasted_iota(I32, (ln, LANES), 0) == ln - 1, dlast, 0.0)
        triu = (cm["ci"] >= cm["ri"]).astype(F32)
        da = jnp.dot(triu, dacum, preferred_element_type=F32, precision=HI)
        ddt = da * cm["a_neg"] + squeeze(dxdt * xsv)
        dalog_ref[...] += jnp.sum(da * cm["dt_s"], axis=0, keepdims=True) * cm["a_neg"]
        ddt_raw = jnp.where(lane < SSD_HEADS, ddt * _sigmoid(cm["pre"]), 0.0)
        ddt_ref[...] = ddt_raw
        ddtb_ref[...] += jnp.sum(ddt_raw, axis=0, keepdims=True)

    row = lambda d: pl.BlockSpec((None, ln, d), lambda bb, c: (bb, nc - 1 - c, 0))
    return pl.pallas_call(
        body, name=name, grid=(b, nc),
        in_specs=[row(D_SSD), row(D_SSD), row(D_SSD), row(D_SSD), row(256), row(256), row(LANES),
                  pl.BlockSpec((None, None, SSD_STATE, D_SSD), lambda bb, c: (bb, nc - 1 - c, 0, 0)),
                  _full((1, LANES)), _full((1, LANES)), _full((1, D_SSD)), _full((1, D_SSD)),
                  _full((LANES, D_SSD)), _full((D_SSD, LANES))],
        out_specs=[row(D_SSD), row(256), row(256), row(D_SSD), row(LANES),
                   _full((1, D_SSD)), _full((1, D_SSD)), _full((1, LANES)), _full((1, LANES))],
        out_shape=[SDS((b, s, D_SSD), F32), SDS((b, s, 256), F32), SDS((b, s, 256), F32), SDS((b, s, D_SSD), BF16),
                   SDS((b, s, LANES), F32), SDS((1, D_SSD), F32), SDS((1, D_SSD), F32), SDS((1, LANES), F32),
                   SDS((1, LANES), F32)],
        scratch_shapes=[pltpu.VMEM((SSD_STATE, D_SSD), F32), pltpu.VMEM((ln, D_SSD), F32), pltpu.VMEM((LANES, ln), F32)],
        compiler_params=_cparams(2))(dys, y, z, xs, bm, cm_, misc, prev, dtb, alog, dskip_e, norm_w, e_mat, et_mat)


def _rope(xv, cc, sp, sm):
    n = xv.shape[1]
    return xv * cc + pltpu.roll(xv, 16, 1) * sp + pltpu.roll(xv, n - 16, 1) * sm


def _rope_bwd(dy, cc, sp, sm):
    n = dy.shape[1]
    return dy * cc + pltpu.roll(dy * sp, n - 16, 1) + pltpu.roll(dy * sm, 16, 1)


def _tile8(t):
    return jnp.concatenate([t] * MLA_HEADS, axis=1)


def qkv_fwd(cq, ckv, misc, cc, sp, sm, qnw, kvnw, wuq_t, wukv_t, place, name):
    b, s, _ = cq.shape
    tm = min(256, s)
    hd = MLA_HEADS * HEAD_PAD

    def body(cq_ref, ckv_ref, misc_ref, cc_ref, sp_ref, sm_ref, qnw_ref, kvnw_ref, wq_ref, wkv_ref, pl_ref,
             q_ref, k_ref, v_ref, qn_ref, kvn_ref):
        cqv, ckvv = cq_ref[...], ckv_ref[...]
        qn = (cqv * _rms(cqv) * qnw_ref[...]).astype(BF16)
        kvn = (ckvv * _rms(ckvv) * kvnw_ref[...]).astype(BF16)
        qn_ref[...] = qn
        kvn_ref[...] = kvn
        cc1, sp1, sm1 = cc_ref[...], sp_ref[...], sm_ref[...]
        q = lax.dot_general(qn, wq_ref[...], NT_DIMS, preferred_element_type=F32)
        q_ref[...] = _rope(q, _tile8(cc1), _tile8(sp1), _tile8(sm1)).astype(BF16)
        kv = lax.dot_general(kvn, wkv_ref[...], NT_DIMS, preferred_element_type=F32)
        kr = jnp.dot(misc_ref[...], pl_ref[...], preferred_element_type=F32, precision=HI)
        kr = _rope(kr, cc1, sp1, sm1)
        k_ref[...] = (kv[:, 0:hd] + _tile8(kr)).astype(BF16)
        v_ref[...] = kv[:, hd:2 * hd].astype(BF16)

    return pl.pallas_call(
        body, name=name, grid=(b, s // tm),
        in_specs=[_row(tm, Q_LORA), _row(tm, KV_LORA), _row(tm, LANES), _row(tm, LANES), _row(tm, LANES), _row(tm, LANES),
                  _full((1, Q_LORA)), _full((1, KV_LORA)), _full(wuq_t.shape), _full(wukv_t.shape), _full((LANES, LANES))],
        out_specs=[_row(tm, hd), _row(tm, hd), _row(tm, hd), _row(tm, Q_LORA), _row(tm, KV_LORA)],
        out_shape=[SDS((b, s, hd), BF16)] * 3 + [SDS((b, s, Q_LORA), BF16), SDS((b, s, KV_LORA), BF16)],
        compiler_params=_cparams(2))(cq, ckv, misc, cc, sp, sm, qnw, kvnw, wuq_t, wukv_t, place)


def qkv_bwd(dq, dk, dv, ddt, cq, ckv, cc, sp, sm, qnw, kvnw, wuq_t, wukv_t, place_t, name):
    b, s, _ = cq.shape
    tm = min(256, s)
    hd = MLA_HEADS * HEAD_PAD

    def body(dq_ref, dk_ref, dv_ref, ddt_ref, cq_ref, ckv_ref, cc_ref, sp_ref, sm_ref, qnw_ref, kvnw_ref,
             wq_ref, wkv_ref, plt_ref, dcq_ref, dckv_ref, dmisc_ref, dqp_ref, dkv_ref, dqnw_ref, dkvnw_ref):
        @pl.when(_first_step())
        def _():
            dqnw_ref[...] = jnp.zeros_like(dqnw_ref)
            dkvnw_ref[...] = jnp.zeros_like(dkvnw_ref)
        cc1, sp1, sm1 = cc_ref[...], sp_ref[...], sm_ref[...]
        dqp = _rope_bwd(dq_ref[...], _tile8(cc1), _tile8(sp1), _tile8(sm1)).astype(BF16)
        dqp_ref[...] = dqp
        dkf = dk_ref[...]
        dkv_b = jnp.concatenate([dkf, dv_ref[...]], axis=1).astype(BF16)
        dkv_ref[...] = dkv_b
        dkr = dkf[:, 0:HEAD_PAD]
        for h in range(1, MLA_HEADS):
            dkr += dkf[:, HEAD_PAD * h:HEAD_PAD * (h + 1)]
        dkr = _rope_bwd(dkr, cc1, sp1, sm1)
        dmisc_ref[...] = (jnp.dot(dkr, plt_ref[...], preferred_element_type=F32, precision=HI) + ddt_ref[...]).astype(BF16)

        def norm_bwd(dn_w, xv, w_ref, dw_ref, dx_ref):
            r = _rms(xv)
            n = xv * r
            dw_ref[...] += jnp.sum(dn_w * n, axis=0, keepdims=True)
            dx_ref[...] = _rms_bwd(dn_w * w_ref[...], n, r).astype(BF16)

        norm_bwd(jnp.dot(dqp, wq_ref[...], preferred_element_type=F32), cq_ref[...], qnw_ref, dqnw_ref, dcq_ref)
        norm_bwd(jnp.dot(dkv_b, wkv_ref[...], preferred_element_type=F32), ckv_ref[...], kvnw_ref, dkvnw_ref, dckv_ref)

    return pl.pallas_call(
        body, name=name, grid=(b, s // tm),
        in_specs=[_row(tm, hd), _row(tm, hd), _row(tm, hd), _row(tm, LANES), _row(tm, Q_LORA), _row(tm, KV_LORA),
                  _row(tm, LANES), _row(tm, LANES), _row(tm, LANES), _full((1, Q_LORA)), _full((1, KV_LORA)),
                  _full(wuq_t.shape), _full(wukv_t.shape), _full((LANES, LANES))],
        out_specs=[_row(tm, Q_LORA), _row(tm, KV_LORA), _row(tm, LANES), _row(tm, hd), _row(tm, 2 * hd),
                   _full((1, Q_LORA)), _full((1, KV_LORA))],
        out_shape=[SDS((b, s, Q_LORA), BF16), SDS((b, s, KV_LORA), BF16), SDS((b, s, LANES), BF16),
                   SDS((b, s, hd), BF16), SDS((b, s, 2 * hd), BF16), SDS((1, Q_LORA), F32), SDS((1, KV_LORA), F32)],
        compiler_params=_cparams(2))(dq, dk, dv, ddt, cq, ckv, cc, sp, sm, qnw, kvnw, wuq_t, wukv_t, place_t)


ATT_SCALE = 1.0 / math.sqrt(QK_DIM)


def _att_tile(s):
    return min(512, s)


def _causal_mask(i, j, t):
    row = lax.broadcasted_iota(I32, (t, t), 0)
    col = lax.broadcasted_iota(I32, (t, t), 1)
    return (j < i) | (col <= row)


def flash_fwd(q, k, v, name):
    b, s, hd = q.shape
    t = _att_tile(s)
    nb = s // t

    def body(q_ref, k_ref, v_ref, o_ref, lse_ref, m_s, l_s, acc):
        i, j = pl.program_id(2), pl.program_id(3)

        @pl.when(j == 0)
        def _():
            m_s[...] = jnp.full_like(m_s, -jnp.inf)
            l_s[...] = jnp.zeros_like(l_s)
            acc[...] = jnp.zeros_like(acc)

        @pl.when(j <= i)
        def _():
            sc = lax.dot_general(q_ref[...], k_ref[...], NT_DIMS, preferred_element_type=F32) * ATT_SCALE
            sc = jnp.where(_causal_mask(i, j, t), sc, -jnp.inf)
            m_prev = m_s[...]
            m_new = jnp.maximum(m_prev, jnp.max(sc, axis=1, keepdims=True))
            alpha = jnp.exp(m_prev - m_new)
            p = jnp.exp(sc - m_new)
            l_s[...] = alpha * l_s[...] + jnp.sum(p, axis=1, keepdims=True)
            acc[...] = alpha * acc[...] + jnp.dot(p.astype(BF16), v_ref[...], preferred_element_type=F32)
            m_s[...] = m_new

        @pl.when(j == nb - 1)
        def _():
            o_ref[...] = acc[...] / l_s[...]
            lse_ref[...] = m_s[...] + jnp.log(l_s[...])

    qs = pl.BlockSpec((None, t, HEAD_PAD), lambda bb, h, i, j: (bb, i, h))
    ks = pl.BlockSpec((None, t, HEAD_PAD), lambda bb, h, i, j: (bb, jnp.minimum(j, i), h))
    ls = pl.BlockSpec((None, None, t, 1), lambda bb, h, i, j: (bb, h, i, 0))
    return pl.pallas_call(
        body, name=name, grid=(b, MLA_HEADS, nb, nb),
        in_specs=[qs, ks, ks], out_specs=[qs, ls],
        out_shape=[SDS((b, s, hd), F32), SDS((b, MLA_HEADS, s, 1), F32)],
        scratch_shapes=[pltpu.VMEM((t, 1), F32), pltpu.VMEM((t, 1), F32), pltpu.VMEM((t, HEAD_PAD), F32)],
        compiler_params=_cparams(4))(q, k, v)


def flash_bwd_dkv(q, k, v, do, lse, dlt, name):
    b, s, hd = q.shape
    t = _att_tile(s)
    nb = s // t

    def body(q_ref, k_ref, v_ref, do_ref, lse_ref, dlt_ref, dk_ref, dv_ref, dk_s, dv_s):
        j, i = pl.program_id(2), pl.program_id(3)

        @pl.when(i == 0)
        def _():
            dk_s[...] = jnp.zeros_like(dk_s)
            dv_s[...] = jnp.zeros_like(dv_s)

        @pl.when(i >= j)
        def _():
            qv, dov = q_ref[...], do_ref[...]
            sc = lax.dot_general(qv, k_ref[...], NT_DIMS, preferred_element_type=F32) * ATT_SCALE
            sc = jnp.where(_causal_mask(i, j, t), sc, -jnp.inf)
            p = jnp.exp(sc - lse_ref[...])
            dv_s[...] += lax.dot_general(p.astype(BF16), dov, TN_DIMS, preferred_element_type=F32)
            dp = lax.dot_general(dov, v_ref[...], NT_DIMS, preferred_element_type=F32)
            ds = (p * (dp - dlt_ref[...]) * ATT_SCALE).astype(BF16)
            dk_s[...] += lax.dot_general(ds, qv, TN_DIMS, preferred_element_type=F32)

        @pl.when(i == nb - 1)
        def _():
            dk_ref[...] = dk_s[...]
            dv_ref[...] = dv_s[...]

    qs = pl.BlockSpec((None, t, HEAD_PAD), lambda bb, h, j, i: (bb, jnp.maximum(i, j), h))
    ks = pl.BlockSpec((None, t, HEAD_PAD), lambda bb, h, j, i: (bb, j, h))
    ls = pl.BlockSpec((None, None, t, 1), lambda bb, h, j, i: (bb, h, jnp.maximum(i, j), 0))
    return pl.pallas_call(
        body, name=name, grid=(b, MLA_HEADS, nb, nb),
        in_specs=[qs, ks, ks, qs, ls, ls], out_specs=[ks, ks],
        out_shape=[SDS((b, s, hd), F32), SDS((b, s, hd), F32)],
        scratch_shapes=[pltpu.VMEM((t, HEAD_PAD), F32), pltpu.VMEM((t, HEAD_PAD), F32)],
        compiler_params=_cparams(4))(q, k, v, do, lse, dlt)


def flash_bwd_dq(q, k, v, do, lse, dlt, name):
    b, s, hd = q.shape
    t = _att_tile(s)
    nb = s // t

    def body(q_ref, k_ref, v_ref, do_ref, lse_ref, dlt_ref, dq_ref, dq_s):
        i, j = pl.program_id(2), pl.program_id(3)

        @pl.when(j == 0)
        def _():
            dq_s[...] = jnp.zeros_like(dq_s)

        @pl.when(j <= i)
        def _():
            kv_ = k_ref[...]
            sc = lax.dot_general(q_ref[...], kv_, NT_DIMS, preferred_element_type=F32) * ATT_SCALE
            sc = jnp.where(_causal_mask(i, j, t), sc, -jnp.inf)
            p = jnp.exp(sc - lse_ref[...])
            dp = lax.dot_general(do_ref[...], v_ref[...], NT_DIMS, preferred_element_type=F32)
            ds = (p * (dp - dlt_ref[...]) * ATT_SCALE).astype(BF16)
            dq_s[...] += jnp.dot(ds, kv_, preferred_element_type=F32)

        @pl.when(j == nb - 1)
        def _():
            dq_ref[...] = dq_s[...]

    qs = pl.BlockSpec((None, t, HEAD_PAD), lambda bb, h, i, j: (bb, i, h))
    ks = pl.BlockSpec((None, t, HEAD_PAD), lambda bb, h, i, j: (bb, jnp.minimum(j, i), h))
    ls = pl.BlockSpec((None, None, t, 1), lambda bb, h, i, j: (bb, h, i, 0))
    return pl.pallas_call(
        body, name=name, grid=(b, MLA_HEADS, nb, nb),
        in_specs=[qs, ks, ks, qs, ls, ls], out_specs=qs,
        out_shape=SDS((b, s, hd), F32),
        scratch_shapes=[pltpu.VMEM((t, HEAD_PAD), F32)], compiler_params=_cparams(4))(q, k, v, do, lse, dlt)


def out_proj(ys, attn, mnw, wo, x, gate, name):
    b, s, d = x.shape
    tm = min(256, s)

    def body(ys_ref, at_ref, mnw_ref, wo_ref, x_ref, g_ref, xn_ref, o_ref, ym_ref):
        av = at_ref[...]
        ym = (av * _rms(av) * mnw_ref[...]).astype(BF16)
        ym_ref[...] = ym
        o = jnp.dot(ys_ref[...], wo_ref[0:D_SSD, :], preferred_element_type=F32)
        o += jnp.dot(ym, wo_ref[D_SSD:2 * D_SSD, :], preferred_element_type=F32)
        xn_ref[...] = x_ref[...] + g_ref[...] * o
        o_ref[...] = o.astype(BF16)

    return pl.pallas_call(
        body, name=name, grid=(b, s // tm),
        in_specs=[_row(tm, D_SSD), _row(tm, D_SSD), _full((1, D_SSD)), _full(wo.shape), _row(tm, d), _bvec(d)],
        out_specs=[_row(tm, d), _row(tm, d), _row(tm, D_SSD)],
        out_shape=[SDS((b, s, d), F32), SDS((b, s, d), BF16), SDS((b, s, D_SSD), BF16)],
        compiler_params=_cparams(2))(ys, attn, mnw, wo, x, gate)


def out_proj_bwd(dout, attn, mnw, wo, name):
    b, s, d = dout.shape
    tm = min(256, s)

    def body(do_ref, at_ref, mnw_ref, wo_ref, dys_ref, dat_ref, dlt_ref, dw_ref):
        @pl.when(_first_step())
        def _():
            dw_ref[...] = jnp.zeros_like(dw_ref)
        dov = do_ref[...]
        dys_ref[...] = lax.dot_general(dov, wo_ref[0:D_SSD, :], NT_DIMS, preferred_element_type=F32)
        dym = lax.dot_general(dov, wo_ref[D_SSD:2 * D_SSD, :], NT_DIMS, preferred_element_type=F32)
        av = at_ref[...]
        r = _rms(av)
        n = av * r
        dw_ref[...] += jnp.sum(dym * n, axis=0, keepdims=True)
        dat = _rms_bwd(dym * mnw_ref[...], n, r)
        dat_ref[...] = dat.astype(BF16)
        prod = dat * av
        for h in range(MLA_HEADS):
            dlt_ref[h] = jnp.sum(prod[:, HEAD_PAD * h:HEAD_PAD * (h + 1)], axis=1, keepdims=True)

    return pl.pallas_call(
        body, name=name, grid=(b, s // tm),
        in_specs=[_row(tm, d), _row(tm, D_SSD), _full((1, D_SSD)), _full(wo.shape)],
        out_specs=[_row(tm, D_SSD), _row(tm, D_SSD),
                   pl.BlockSpec((None, MLA_HEADS, tm, 1), lambda bb, i: (bb, 0, i, 0)), _full((1, D_SSD))],
        out_shape=[SDS((b, s, D_SSD), F32), SDS((b, s, D_SSD), BF16), SDS((b, MLA_HEADS, s, 1), F32),
                   SDS((1, D_SSD), F32)],
        compiler_params=_cparams(2))(dout, attn, mnw, wo)


def adaln_fwd(c_all, w_ada, b_ada, name):
    nb, d = c_all.shape
    n = w_ada.shape[1]

    def body(c_ref, w_ref, b_ref, m_ref, ca_ref):
        cv = c_ref[...]
        ca = (cv * _sigmoid(cv)).astype(BF16)
        ca_ref[...] = ca
        m_ref[...] = jnp.dot(ca, w_ref[...].astype(BF16), preferred_element_type=F32) + b_ref[...]

    return pl.pallas_call(
        body, name=name, out_shape=[SDS((nb, n), F32), SDS((nb, d), BF16)],
        compiler_params=pltpu.CompilerParams(vmem_limit_bytes=VMEM_LIMIT))(c_all, w_ada, b_ada)


def adaln_bwd(c_act, dmod_cols, name):
    d, n = c_act.shape[1], dmod_cols.shape[1]

    def body(c_ref, dm_ref, gw_ref):
        gw_ref[...] = lax.dot_general(c_ref[...], dm_ref[...].astype(BF16), TN_DIMS, preferred_element_type=F32)

    return pl.pallas_call(
        body, name=name, out_shape=SDS((d, n), F32),
        compiler_params=pltpu.CompilerParams(vmem_limit_bytes=VMEM_LIMIT))(c_act, dmod_cols)


def sum_rows(x, name):
    def body(x_ref, o_ref):
        o_ref[...] = jnp.sum(x_ref[...], axis=0, keepdims=True)
    return pl.pallas_call(body, name=name, out_shape=SDS((1, x.shape[1]), F32))(x)


def squeeze_heads(x, et_mat, name):
    def body(x_ref, et_ref, o_ref):
        xv = jnp.broadcast_to(x_ref[...], (8, x.shape[1]))
        o_ref[...] = jnp.dot(xv, et_ref[...], preferred_element_type=F32, precision=HI)[0:1, :]
    return pl.pallas_call(body, name=name, out_shape=SDS((1, LANES), F32))(x, et_mat)


def sum_blocks(x, name):
    n, r, c = x.shape

    def body(x_ref, o_ref):
        acc = x_ref[0].astype(F32)
        for k in range(1, n):
            acc += x_ref[k].astype(F32)
        o_ref[...] = acc

    return pl.pallas_call(body, name=name, out_shape=SDS((r, c), F32),
                          compiler_params=pltpu.CompilerParams(vmem_limit_bytes=VMEM_LIMIT))(x)


def _adam_math(w, g, m, v):
    m = ADAM_B1 * m + (1.0 - ADAM_B1) * g
    v = ADAM_B2 * v + (1.0 - ADAM_B2) * (g * g)
    m_hat = m / (1.0 - ADAM_B1 ** ADAM_STEP)
    v_hat = v / (1.0 - ADAM_B2 ** ADAM_STEP)
    return -ADAM_LR * (m_hat / (jnp.sqrt(v_hat) + ADAM_EPS) + ADAM_WD * w), m, v


def adamw(w, g, m, v, name):
    r, c = w.shape
    tr = r
    for cand in (512, 256, 128, 64, 32, 16, 8):
        if r % cand == 0 and cand * c * 4 <= 2 * 1024 * 1024:
            tr = cand
            break

    def body(w_ref, g_ref, m_ref, v_ref, d_ref, mo_ref, vo_ref):
        d_ref[...], mo_ref[...], vo_ref[...] = _adam_math(w_ref[...], g_ref[...], m_ref[...], v_ref[...])

    spec = pl.BlockSpec((tr, c), lambda i: (i, 0))
    return pl.pallas_call(
        body, name=name, grid=(r // tr,), in_specs=[spec] * 4, out_specs=[spec] * 3,
        out_shape=[SDS((r, c), F32)] * 3, compiler_params=_cparams(1))(w, g, m, v)


def adamw_sum8(w, gparts, m, v, name):
    r, c = w.shape
    tr = 64 if r % 64 == 0 else r

    def body(w_ref, gp_ref, m_ref, v_ref, g_ref, d_ref, mo_ref, vo_ref):
        g = gp_ref[0].astype(F32)
        for k in range(1, N_DEV):
            g += gp_ref[k].astype(F32)
        g_ref[...] = g
        d_ref[...], mo_ref[...], vo_ref[...] = _adam_math(w_ref[...], g, m_ref[...], v_ref[...])

    spec = pl.BlockSpec((tr, c), lambda i: (i, 0))
    gspec = pl.BlockSpec((N_DEV, tr, c), lambda i: (0, i, 0))
    return pl.pallas_call(
        body, name=name, grid=(r // tr,), in_specs=[spec, gspec, spec, spec], out_specs=[spec] * 4,
        out_shape=[SDS((r, c), F32)] * 4, compiler_params=_cparams(1))(w, gparts, m, v)


PACK = (("ffn1_w_gate", 352, 352), ("ffn1_w_up", 352, 352), ("ffn1_w_down", 352, 352),
        ("ffn2_w_gate", 352, 352), ("ffn2_w_up", 352, 352), ("ffn2_w_down", 352, 352),
        ("w_out", 256, 256), ("w_in", 406, 416), ("w_ukv", 48, 48), ("w_uq", 36, 48))
PACK_ROWS = sum(p[2] for p in PACK)
PACK_OFF = {}
_o = 0
for _n, _r, _p in PACK:
    PACK_OFF[_n] = (_o, _r)
    _o += _p
TRANSPOSED = ("ffn1_w_gate", "ffn1_w_up", "ffn2_w_gate", "ffn2_w_up", "w_in", "w_ukv", "w_uq")


def _shard_to_rows(name, w):
    w = w[0]
    if name in TRANSPOSED:
        w = w.T
    return w.reshape(-1, D_MODEL)


def _rows_to_shard(name, rows, like):
    shp = like.shape[1:]
    if name in TRANSPOSED:
        return rows.reshape(shp[1], shp[0]).T[None]
    return rows.reshape(shp)[None]


def _pack_shards(ws, dtype):
    parts = []
    for name, real, padded in PACK:
        rows = _shard_to_rows(name, ws[name]).astype(dtype)
        if padded > real:
            rows = jnp.pad(rows, ((0, padded - real), (0, 0)))
        parts.append(rows)
    return jnp.concatenate(parts, axis=0)


def _seg(g, name):
    o, r = PACK_OFF[name]
    return g[:, o:o + r]


def _pack_rows(arrs):
    parts = []
    for a in arrs:
        flat = a.reshape(-1).astype(F32)
        pad = (-flat.shape[0]) % D_MODEL
        if pad:
            flat = jnp.pad(flat, (0, pad))
        parts.append(flat.reshape(-1, D_MODEL))
    out = jnp.concatenate(parts, axis=0)
    pad = (-out.shape[0]) % 8
    if pad:
        out = jnp.pad(out, ((0, pad), (0, 0)))
    return out


def _unpack_rows(packed, shapes):
    out, row = [], 0
    for shp in shapes:
        n = math.prod(shp)
        nrow = -(-n // D_MODEL)
        out.append(packed[row:row + nrow].reshape(-1)[:n].reshape(shp))
        row += nrow
    return out


def _in_proj_rows(w_t):
    return jnp.concatenate([w_t[0:2560], w_t[2576:2960], w_t[2960:3216], w_t[2560:2576], w_t[3216:3248],
                            jnp.zeros((D_IN_PAD - D_IN, D_MODEL), w_t.dtype)], axis=0)


def _in_proj_rows_inv(d):
    return jnp.concatenate([d[0:2560], d[3200:3216], d[2560:2944], d[2944:3200], d[3216:3248]], axis=0)


def _rope_tables(positions):
    inv_freq = ROPE_THETA ** (-jnp.arange(0, QK_ROPE, 2, dtype=F32) / QK_ROPE)
    ang = positions[..., None].astype(F32) * inv_freq
    cos, sin = jnp.cos(ang), jnp.sin(ang)
    one = jnp.ones(ang.shape[:2] + (QK_NOPE,), F32)
    zero = jnp.zeros_like(one)
    z16, z32, o32 = zero[..., :16], zero[..., :32], one[..., :32]
    cc = jnp.concatenate([one, cos, cos, o32], axis=-1)
    sp = jnp.concatenate([zero, z16, sin, z32], axis=-1)
    sm = jnp.concatenate([zero, -sin, z16, z32], axis=-1)
    return cc, sp, sm


def weight_views(g):
    full = lambda name: _seg(g, name).reshape(-1, D_MODEL)
    ukv = _seg(g, "w_ukv").reshape(MLA_HEADS, QK_NOPE + V_HEAD, KV_LORA)
    wukv_t = jnp.concatenate([jnp.pad(ukv[:, :QK_NOPE], ((0, 0), (0, HEAD_PAD - QK_NOPE), (0, 0))).reshape(-1, KV_LORA),
                              ukv[:, QK_NOPE:].reshape(-1, KV_LORA)], axis=0)
    uq = _seg(g, "w_uq").reshape(MLA_HEADS, QK_DIM, Q_LORA)
    wuq_t = jnp.pad(uq, ((0, 0), (0, HEAD_PAD - QK_DIM), (0, 0))).reshape(-1, Q_LORA)
    return dict(wg1_t=full("ffn1_w_gate"), wu1_t=full("ffn1_w_up"), wd1=full("ffn1_w_down"),
                wg2_t=full("ffn2_w_gate"), wu2_t=full("ffn2_w_up"), wd2=full("ffn2_w_down"),
                wo=full("w_out"), win_t=_in_proj_rows(full("w_in")), wukv_t=wukv_t, wuq_t=wuq_t)


def _ffn_bwd(tag, dxn, x, h, gg, uu, a, o, gate, sc, norm_w, wg_t, wu_t, wd):
    f2 = wd.shape[0] // 2
    do, dgate = gate_bwd(dxn, o, gate, 0.5, tag + "_gate_bwd")
    dgg, duu = ffn_dact(do, wd, gg, uu, tag + "_dact")
    dwd = mm_tn(a, do, f2, D_MODEL, tag + "_dwd")
    dwg_t = mm_tn(dgg, h, f2, D_MODEL, tag + "_dwg")
    dwu_t = mm_tn(duu, h, f2, D_MODEL, tag + "_dwu")
    dx, dsc, dsh, dnw = dh_norm_bwd([dgg, duu], [wg_t, wu_t], x, dxn, norm_w, sc, tag + "_dh")
    return dx, (dsh, dsc, dgate), dnw, (dwg_t, dwu_t, dwd)


def local_step(x, tgt, positions, mod, wv, p):
    nb, s, d = x.shape
    sh1, sc1, g1, sh2, sc2, g2, sh3, sc3, g3 = mod
    cc, sp, sm = _rope_tables(positions)
    lane_head = jnp.arange(D_SSD, dtype=I32)[None, :] // SSD_HEAD_DIM
    e_mat = (lane_head == jnp.arange(LANES, dtype=I32)[:, None]).astype(F32)
    et_mat = e_mat.T
    rr, cl = jnp.arange(LANES, dtype=I32)[:, None], jnp.arange(LANES, dtype=I32)[None, :]
    place = ((cl == rr + (QK_NOPE - SSD_HEADS)) & (rr >= SSD_HEADS) & (rr < SSD_HEADS + QK_ROPE)).astype(F32)
    dtb = jnp.pad(p["dt_bias"], ((0, 0), (0, LANES - SSD_HEADS)))
    alog = jnp.pad(p["a_log"], ((0, 0), (0, LANES - SSD_HEADS)))
    dskip_e = jnp.repeat(p["d_skip"], SSD_HEAD_DIM, axis=1)

    h1 = norm_mod(x, p["norm_ffn1"], sc1, sh1, "ffn1_norm")
    gg1, uu1, a1 = ffn_up(h1, wv["wg1_t"], wv["wu1_t"], "ffn1_up")
    x1, o1 = ffn_down(a1, wv["wd1"], x, g1, 0.5, "ffn1_down")
    h2 = norm_mod(x1, p["norm_mix"], sc2, sh2, "mix_norm")
    z, u, cq, ckv, misc = in_proj(h2, wv["win_t"], "in_proj")
    xs, bm, cm_ = conv_fwd(u, p["conv_w"], p["conv_b"], "conv_fwd")
    ys, y, prev = ssd_fwd(xs, bm, cm_, misc, z, dtb, alog, dskip_e, p["ssd_norm_w"], e_mat, "ssd_fwd")
    q, k, v, qn, kvn = qkv_fwd(cq, ckv, misc, cc, sp, sm, p["q_norm_w"], p["kv_norm_w"], wv["wuq_t"], wv["wukv_t"],
                               place, "qkv_fwd")
    attn, lse = flash_fwd(q, k, v, "flash_fwd")
    x2, o2, ym = out_proj(ys, attn, p["mla_norm_w"], wv["wo"], x1, g2, "out_proj")
    h3 = norm_mod(x2, p["norm_ffn2"], sc3, sh3, "ffn2_norm")
    gg3, uu3, a3 = ffn_up(h3, wv["wg2_t"], wv["wu2_t"], "ffn2_up")
    x3, o3 = ffn_down(a3, wv["wd2"], x2, g3, 0.5, "ffn2_down")
    loss, dx3, dnfin = final_loss(x3, p["norm_final"], tgt, "final_loss")

    dx2, dmod3, dnf2, (dwg2, dwu2, dwd2) = _ffn_bwd("ffn2", dx3, x2, h3, gg3, uu3, a3, o3, g3, sc3, p["norm_ffn2"],
                                                   wv["wg2_t"], wv["wu2_t"], wv["wd2"])
    dout, dg2 = gate_bwd(dx2, o2, g2, 1.0, "mix_gate_bwd")
    dys, dattn, dlt, dmlan = out_proj_bwd(dout, attn, p["mla_norm_w"], wv["wo"], "out_proj_bwd")
    dwo = jnp.concatenate([mm_tn(ys, dout, D_SSD, D_MODEL, "dwo_ssd"), mm_tn(ym, dout, D_SSD, D_MODEL, "dwo_mla")], axis=0)
    dxs, dbm, dcm, dz, ddt, dssdn, ddsk_lane, ddtb, dalog = ssd_bwd(
        dys, y, z, xs, bm, cm_, misc, prev, dtb, alog, dskip_e, p["ssd_norm_w"], e_mat, et_mat, "ssd_bwd")
    dk, dv = flash_bwd_dkv(q, k, v, dattn, lse, dlt, "flash_dkv")
    dq = flash_bwd_dq(q, k, v, dattn, lse, dlt, "flash_dq")
    dcq, dckv, dmisc, dqp, dkvc, dqn, dkvn = qkv_bwd(dq, dk, dv, ddt, cq, ckv, cc, sp, sm, p["q_norm_w"], p["kv_norm_w"],
                                                     wv["wuq_t"], wv["wukv_t"], place.T, "qkv_bwd")
    dwuq = mm_tn(dqp, qn, MLA_HEADS * HEAD_PAD, Q_LORA, "dwuq")
    dwukv = mm_tn(dkvc, kvn, MLA_HEADS * HEAD_PAD, KV_LORA, "dwukv")
    dvv, dconv = conv_bwd_a(dxs, dbm, dcm, u, p["conv_w"], p["conv_b"], "conv_bwd_a")
    du = conv_bwd_b(dvv, p["conv_w"], "conv_bwd_b")
    dproj = jnp.concatenate([dz, du, dcq, dckv, dmisc], axis=-1)
    dwin = mm_tn(dproj, h2, D_IN_PAD // 2, D_MODEL, "dwin")
    dx1, dsc2, dsh2, dnmix = dh_norm_bwd([dproj], [wv["win_t"]], x1, dx2, p["norm_mix"], sc2, "mix_dh")
    dx0, dmod1, dnf1, (dwg1, dwu1, dwd1) = _ffn_bwd("ffn1", dx1, x, h1, gg1, uu1, a1, o1, g1, sc1, p["norm_ffn1"],
                                                   wv["wg1_t"], wv["wu1_t"], wv["wd1"])

    dmod = jnp.concatenate([*dmod1, dsh2, dsc2, dg2, *dmod3], axis=1).reshape(nb, N_MOD * d)
    return dict(
        loss=loss, dx=dx0, dmod=dmod, norm_ffn1=dnf1, norm_mix=dnmix, norm_ffn2=dnf2, norm_final=dnfin,
        ssd_norm_w=dssdn, mla_norm_w=dmlan, q_norm_w=dqn, kv_norm_w=dkvn,
        dt_bias=ddtb[:, :SSD_HEADS], a_log=dalog[:, :SSD_HEADS],
        d_skip=squeeze_heads(ddsk_lane, et_mat, "d_skip_heads")[:, :SSD_HEADS],
        conv_b=dconv[4:5], conv_w=dconv[0:4],
        gw=dict(ffn1_w_gate=dwg1, ffn1_w_up=dwu1, ffn1_w_down=dwd1, ffn2_w_gate=dwg2, ffn2_w_up=dwu2, ffn2_w_down=dwd2,
                w_out=dwo, w_in=dwin, w_ukv=dwukv, w_uq=dwuq))


def kernel(x, c, positions, w_ada, b_ada, norm_ffn1, ffn1_w_gate, ffn1_w_up, ffn1_w_down, norm_mix, w_in, conv_w, conv_b, dt_bias, a_log, d_skip, ssd_norm_w, q_norm_w, w_uq, kv_norm_w, w_ukv, mla_norm_w, w_out, norm_ffn2, ffn2_w_gate, ffn2_w_up, ffn2_w_down, norm_final, loss_target, m_w_ada, m_b_ada, m_norm_ffn1, m_ffn1_w_gate, m_ffn1_w_up, m_ffn1_w_down, m_norm_mix, m_w_in, m_conv_w, m_conv_b, m_dt_bias, m_a_log, m_d_skip, m_ssd_norm_w, m_q_norm_w, m_w_uq, m_kv_norm_w, m_w_ukv, m_mla_norm_w, m_w_out, m_norm_ffn2, m_ffn2_w_gate, m_ffn2_w_up, m_ffn2_w_down, m_norm_final, v_w_ada, v_b_ada, v_norm_ffn1, v_ffn1_w_gate, v_ffn1_w_up, v_ffn1_w_down, v_norm_mix, v_w_in, v_conv_w, v_conv_b, v_dt_bias, v_a_log, v_d_skip, v_ssd_norm_w, v_q_norm_w, v_w_uq, v_kv_norm_w, v_w_ukv, v_mla_norm_w, v_w_out, v_norm_ffn2, v_ffn2_w_gate, v_ffn2_w_up, v_ffn2_w_down, v_norm_final):
    names = ["w_ada", "b_ada", "norm_ffn1", "ffn1_w_gate", "ffn1_w_up", "ffn1_w_down", "norm_mix", "w_in", "conv_w",
             "conv_b", "dt_bias", "a_log", "d_skip", "ssd_norm_w", "q_norm_w", "w_uq", "kv_norm_w", "w_ukv",
             "mla_norm_w", "w_out", "norm_ffn2", "ffn2_w_gate", "ffn2_w_up", "ffn2_w_down", "norm_final"]
    W = dict(zip(names, (w_ada, b_ada, norm_ffn1, ffn1_w_gate, ffn1_w_up, ffn1_w_down, norm_mix, w_in, conv_w, conv_b, dt_bias, a_log, d_skip, ssd_norm_w, q_norm_w, w_uq, kv_norm_w, w_ukv, mla_norm_w, w_out, norm_ffn2, ffn2_w_gate, ffn2_w_up, ffn2_w_down, norm_final)))
    M = dict(zip(names, (m_w_ada, m_b_ada, m_norm_ffn1, m_ffn1_w_gate, m_ffn1_w_up, m_ffn1_w_down, m_norm_mix, m_w_in, m_conv_w, m_conv_b, m_dt_bias, m_a_log, m_d_skip, m_ssd_norm_w, m_q_norm_w, m_w_uq, m_kv_norm_w, m_w_ukv, m_mla_norm_w, m_w_out, m_norm_ffn2, m_ffn2_w_gate, m_ffn2_w_up, m_ffn2_w_down, m_norm_final)))
    V = dict(zip(names, (v_w_ada, v_b_ada, v_norm_ffn1, v_ffn1_w_gate, v_ffn1_w_up, v_ffn1_w_down, v_norm_mix, v_w_in, v_conv_w, v_conv_b, v_dt_bias, v_a_log, v_d_skip, v_ssd_norm_w, v_q_norm_w, v_w_uq, v_kv_norm_w, v_w_ukv, v_mla_norm_w, v_w_out, v_norm_ffn2, v_ffn2_w_gate, v_ffn2_w_up, v_ffn2_w_down, v_norm_final)))

    nb, s, d = x.shape
    me = 4 * lax.axis_index("x") + 2 * lax.axis_index("y") + lax.axis_index("c")
    n_ada = w_ada.shape[2]

    cshape = [(nb, d), conv_w.shape[1:]]
    cg = all_gather8(_pack_rows([c, conv_w[0]]), "gather_c")
    c_all = jnp.stack([_unpack_rows(cg[k], cshape)[0] for k in range(N_DEV)]).reshape(N_DEV * nb, d)
    conv_w_full = jnp.concatenate([_unpack_rows(cg[k], cshape)[1] for k in range(N_DEV)], axis=1)
    wv = weight_views(all_gather8(_pack_shards(W, BF16), "gather_weights"))

    b_ada_cols = lax.dynamic_slice(b_ada, (0, me * n_ada), (1, n_ada))
    mod_cols, c_act = adaln_fwd(c_all, w_ada[0], b_ada_cols, "adaln_fwd")
    mod_g = all_gather8(mod_cols, "gather_mod")
    mod = lax.dynamic_slice(mod_g, (0, me * nb, 0), (N_DEV, nb, n_ada)).transpose(1, 0, 2).reshape(nb, N_MOD, 1, d)
    mod = [mod[:, k] for k in range(N_MOD)]

    P = dict(W)
    P["conv_w"] = conv_w_full
    P["norm_final"] = norm_final.reshape(1, d)
    R = local_step(x, loss_target, positions, mod, wv, P)

    dmod = R["dmod"]
    partial_shapes = [(1,), (1, d), (1, d), (1, d), (1, d), (1, d), (1, d), (1, Q_LORA), (1, KV_LORA),
                      (1, SSD_HEADS), (1, SSD_HEADS), (1, SSD_HEADS), (1, D_CONV), (4, D_CONV), (1, N_MOD * d),
                      (nb, N_MOD * d)]
    partial = _pack_rows([R["loss"][0, :1], R["norm_ffn1"], R["norm_mix"], R["norm_ffn2"], R["norm_final"],
                          R["ssd_norm_w"], R["mla_norm_w"], R["q_norm_w"], R["kv_norm_w"],
                          R["dt_bias"], R["a_log"], R["d_skip"], R["conv_b"], R["conv_w"],
                          sum_rows(dmod, "dmod_rows"), dmod])
    partial_g = all_gather8(partial, "gather_partials")
    (loss, g_nf1, g_nmix, g_nf2, g_nfin, g_ssdn, g_mlan, g_qn, g_kvn, g_dtb, g_alog, g_dskip, g_convb, g_convw,
     g_bada, _) = _unpack_rows(sum_blocks(partial_g, "sum_partials"), partial_shapes)
    dmod_all = jnp.stack([_unpack_rows(partial_g[k], partial_shapes)[-1] for k in range(N_DEV)]).reshape(N_DEV * nb, -1)
    g_wada = adaln_bwd(c_act, lax.dynamic_slice(dmod_all, (0, me * n_ada), (N_DEV * nb, n_ada)), "adaln_bwd")
    n_cw = conv_w.shape[2]
    G = {"w_ada": g_wada[None], "b_ada": g_bada, "norm_ffn1": g_nf1, "norm_mix": g_nmix, "norm_ffn2": g_nf2,
         "norm_final": g_nfin.reshape(d), "ssd_norm_w": g_ssdn, "mla_norm_w": g_mlan, "q_norm_w": g_qn,
         "kv_norm_w": g_kvn, "dt_bias": g_dtb, "a_log": g_alog, "d_skip": g_dskip, "conv_b": g_convb,
         "conv_w": lax.dynamic_slice(g_convw, (0, me * n_cw), (4, n_cw))[None]}

    gw = R["gw"]
    gp = [gw[name].reshape(N_DEV, -1, D_MODEL) for name in ("ffn1_w_gate", "ffn1_w_up", "ffn1_w_down", "ffn2_w_gate",
                                                             "ffn2_w_up", "ffn2_w_down", "w_out")]
    gp.append(jnp.pad(_in_proj_rows_inv(gw["w_in"]).reshape(N_DEV, -1, D_MODEL), ((0, 0), (0, 10), (0, 0))))
    dkv_ = gw["w_ukv"]
    hd = MLA_HEADS * HEAD_PAD
    gp.append(jnp.concatenate([dkv_[:hd].reshape(MLA_HEADS, HEAD_PAD, KV_LORA)[:, :QK_NOPE],
                               dkv_[hd:].reshape(MLA_HEADS, V_HEAD, KV_LORA)], axis=1).reshape(N_DEV, -1, D_MODEL))
    gp.append(jnp.pad(gw["w_uq"].reshape(MLA_HEADS, HEAD_PAD, Q_LORA)[:, :QK_DIM].reshape(N_DEV, -1, D_MODEL),
                      ((0, 0), (0, 12), (0, 0))))
    recv = all_to_all8(jnp.concatenate(gp, axis=1).astype(BF16), "exchange_grads")

    DW, NM, NV = {}, {}, {}
    big = adamw_sum8(_pack_shards(W, F32), recv, _pack_shards(M, F32), _pack_shards(V, F32), "adamw_matrices")
    for name, _, _ in PACK:
        o, r = PACK_OFF[name]
        G[name], DW[name], NM[name], NV[name] = [_rows_to_shard(name, t[o:o + r], W[name]) for t in big]
    dwa, nma, nva = adamw(w_ada[0], g_wada, m_w_ada[0], v_w_ada[0], "adamw_w_ada")
    DW["w_ada"], NM["w_ada"], NV["w_ada"] = dwa[None], nma[None], nva[None]
    small = [n for n in names if n not in DW]
    shapes = [W[n].shape for n in small]
    outs = adamw(_pack_rows([W[n] for n in small]), _pack_rows([G[n] for n in small]),
                 _pack_rows([M[n] for n in small]), _pack_rows([V[n] for n in small]), "adamw_small")
    for res, dst in zip(outs, (DW, NM, NV)):
        for n, t in zip(small, _unpack_rows(res, shapes)):
            dst[n] = t
    return (loss.reshape(()), R["dx"], *[G[n] for n in names], *[DW[n] for n in names], *[NM[n] for n in names],
            *[NV[n] for n in names])
```

```python
import math

import jax
import jax.numpy as jnp
from jax import lax
from jax.experimental import pallas as pl
from jax.experimental.pallas import tpu as pltpu

F32, BF16, I32 = jnp.float32, jnp.bfloat16, jnp.int32
HI = lax.Precision.HIGHEST
SDS = jax.ShapeDtypeStruct
MESH = pl.DeviceIdType.MESH

D_MODEL = 1024
D_FF = 2816
D_SSD = 1024
SSD_HEADS = 16
SSD_HEAD_DIM = 64
SSD_GROUPS = 2
SSD_STATE = 128
CHUNK = 128
MLA_HEADS = 8
QK_NOPE = 64
QK_ROPE = 32
QK_DIM = 96
V_HEAD = 128
Q_LORA = 384
KV_LORA = 256
ROPE_THETA = 10000.0
N_MOD = 9
EPS = 1e-6
D_CONV = 1536
D_IN = 3248
D_IN_PAD = 3328
HEAD_PAD = 128
N_DEV = 8
ADAM_LR, ADAM_B1, ADAM_B2, ADAM_EPS, ADAM_WD, ADAM_STEP = 0.001, 0.9, 0.999, 1e-08, 0.01, 10

VMEM_LIMIT = 56 * 1024 * 1024
LANES = 128
NT_DIMS = (((1,), (1,)), ((), ()))
TN_DIMS = (((0,), (0,)), ((), ()))


def _cparams(n_axes):
    return pltpu.CompilerParams(dimension_semantics=("arbitrary",) * n_axes, vmem_limit_bytes=VMEM_LIMIT)


def _row(tm, d):
    return pl.BlockSpec((None, tm, d), lambda b, i: (b, i, 0))


def _bvec(d):
    return pl.BlockSpec((None, 1, d), lambda b, i: (b, 0, 0))


def _full(shape):
    n = len(shape)
    return pl.BlockSpec(shape, lambda *_: (0,) * n)


def _sigmoid(x):
    return 1.0 / (1.0 + jnp.exp(-x))


def _softplus(x):
    return jnp.maximum(x, 0.0) + jnp.log(1.0 + jnp.exp(-jnp.abs(x)))


def _rms(x):
    return lax.rsqrt(jnp.mean(x * x, axis=-1, keepdims=True) + EPS)


def _rms_bwd(dn, n, r):
    return r * (dn - n * jnp.mean(dn * n, axis=-1, keepdims=True))


def _first_step():
    return (pl.program_id(0) == 0) & (pl.program_id(1) == 0)


def all_gather8(x, name):
    r, c = x.shape

    def body(x_ref, out_ref, send_sems, recv_sems, local_sem):
        mx, my, mc = lax.axis_index("x"), lax.axis_index("y"), lax.axis_index("c")
        me, sibling = (mx, my, mc), (mx, my, 1 - mc)
        chips = [(1 - mx, my), (mx, 1 - my), (1 - mx, 1 - my)]

        def rows(px, py, pc):
            return out_ref.at[4 * px + 2 * py + pc]

        def copy(k, block, to, src=None):
            return pltpu.make_async_remote_copy(
                src_ref=rows(*block) if src is None else src, dst_ref=rows(*block),
                send_sem=send_sems.at[k], recv_sem=recv_sems.at[k], device_id=to, device_id_type=MESH)

        mine = pltpu.make_async_copy(x_ref, rows(*me), local_sem)
        mine.start()
        first = [copy(0, me, sibling, src=x_ref)]
        first += [copy(1 + j, me, (*chip, mc), src=x_ref) for j, chip in enumerate(chips)]
        for cp in first:
            cp.start()
        passed = [copy(4 + j, (*chip, mc), sibling) for j, chip in enumerate(chips)]
        for j, chip in enumerate(chips):
            copy(1 + j, (*chip, mc), me).wait_recv()
            passed[j].start()
        copy(0, sibling, me).wait_recv()
        for j, chip in enumerate(chips):
            copy(4 + j, (*chip, 1 - mc), me).wait_recv()
        for cp in first + passed:
            cp.wait_send()
        mine.wait()

    return pl.pallas_call(
        body, name=name,
        out_shape=SDS((N_DEV, r, c), x.dtype),
        in_specs=[pl.BlockSpec(memory_space=pl.ANY)],
        out_specs=pl.BlockSpec(memory_space=pl.ANY),
        scratch_shapes=[pltpu.SemaphoreType.DMA((7,)), pltpu.SemaphoreType.DMA((7,)), pltpu.SemaphoreType.DMA],
    )(x)


def all_to_all8(x, name):
    _, r, c = x.shape

    def body(x_ref, out_ref, send_sems, recv_sems, local_sem):
        mx, my, mc = lax.axis_index("x"), lax.axis_index("y"), lax.axis_index("c")
        me = 4 * mx + 2 * my + mc
        mine = pltpu.make_async_copy(x_ref.at[me], out_ref.at[me], local_sem)
        mine.start()
        copies = []
        for rel in range(1, N_DEV):
            px = 1 - mx if rel & 4 else mx
            py = 1 - my if rel & 2 else my
            pc = 1 - mc if rel & 1 else mc
            cp = pltpu.make_async_remote_copy(
                src_ref=x_ref.at[4 * px + 2 * py + pc], dst_ref=out_ref.at[me],
                send_sem=send_sems.at[rel - 1], recv_sem=recv_sems.at[rel - 1],
                device_id=(px, py, pc), device_id_type=MESH)
            cp.start()
            copies.append(cp)
        for cp in copies:
            cp.wait()
        mine.wait()

    return pl.pallas_call(
        body, name=name,
        out_shape=SDS((N_DEV, r, c), x.dtype),
        in_specs=[pl.BlockSpec(memory_space=pl.ANY)],
        out_specs=pl.BlockSpec(memory_space=pl.ANY),
        scratch_shapes=[pltpu.SemaphoreType.DMA((7,)), pltpu.SemaphoreType.DMA((7,)), pltpu.SemaphoreType.DMA],
    )(x)


def norm_mod(x, w, sc, sh, name):
    b, s, d = x.shape
    tm = min(512, s)

    def body(x_ref, w_ref, sc_ref, sh_ref, h_ref):
        xv = x_ref[...]
        n = xv * _rms(xv)
        h_ref[...] = ((n * w_ref[...]) * (1.0 + sc_ref[...]) + sh_ref[...]).astype(BF16)

    return pl.pallas_call(
        body, name=name, grid=(b, s // tm),
        in_specs=[_row(tm, d), _full((1, d)), _bvec(d), _bvec(d)],
        out_specs=_row(tm, d), out_shape=SDS((b, s, d), BF16), compiler_params=_cparams(2))(x, w, sc, sh)


def ffn_up(h, wg_t, wu_t, name):
    b, s, d = h.shape
    f = wg_t.shape[0]
    tm, tn = min(512, s), f // 2

    def body(h_ref, wg_ref, wu_ref, g_ref, u_ref, a_ref):
        hv = h_ref[...]
        g = lax.dot_general(hv, wg_ref[...], NT_DIMS, preferred_element_type=F32)
        u = lax.dot_general(hv, wu_ref[...], NT_DIMS, preferred_element_type=F32)
        g_ref[...] = g
        u_ref[...] = u
        a_ref[...] = (g * _sigmoid(g) * u).astype(BF16)

    hs = pl.BlockSpec((None, tm, d), lambda j, bb, i: (bb, i, 0))
    ws = pl.BlockSpec((tn, d), lambda j, bb, i: (j, 0))
    os_ = pl.BlockSpec((None, tm, tn), lambda j, bb, i: (bb, i, j))
    return pl.pallas_call(
        body, name=name, grid=(f // tn, b, s // tm),
        in_specs=[hs, ws, ws], out_specs=[os_, os_, os_],
        out_shape=[SDS((b, s, f), F32), SDS((b, s, f), F32), SDS((b, s, f), BF16)],
        compiler_params=_cparams(3))(h, wg_t, wu_t)


def ffn_down(a, wd, x, gate, scale, name):
    b, s, f = a.shape
    d = wd.shape[1]
    tm = min(512, s)

    def body(a_ref, wd_ref, x_ref, g_ref, xn_ref, o_ref):
        o = jnp.dot(a_ref[...], wd_ref[...], preferred_element_type=F32)
        xn_ref[...] = x_ref[...] + (scale * g_ref[...]) * o
        o_ref[...] = o.astype(BF16)

    return pl.pallas_call(
        body, name=name, grid=(b, s // tm),
        in_specs=[_row(tm, f), _full((f, d)), _row(tm, d), _bvec(d)],
        out_specs=[_row(tm, d), _row(tm, d)],
        out_shape=[SDS((b, s, d), F32), SDS((b, s, d), BF16)], compiler_params=_cparams(2))(a, wd, x, gate)


def gate_bwd(dxn, o, gate, scale, name):
    b, s, d = dxn.shape
    tm = min(512, s)

    def body(dx_ref, o_ref, g_ref, do_ref, dg_ref):
        @pl.when(pl.program_id(1) == 0)
        def _():
            dg_ref[...] = jnp.zeros_like(dg_ref)
        dx = dx_ref[...]
        do_ref[...] = ((scale * g_ref[...]) * dx).astype(BF16)
        dg_ref[...] += jnp.sum(scale * dx * o_ref[...].astype(F32), axis=0, keepdims=True)

    return pl.pallas_call(
        body, name=name, grid=(b, s // tm),
        in_specs=[_row(tm, d), _row(tm, d), _bvec(d)],
        out_specs=[_row(tm, d), _bvec(d)],
        out_shape=[SDS((b, s, d), BF16), SDS((b, 1, d), F32)], compiler_params=_cparams(2))(dxn, o, gate)


def ffn_dact(do, wd, g, u, name):
    b, s, d = do.shape
    f = wd.shape[0]
    tm, tn = min(512, s), f // 2

    def body(do_ref, wd_ref, g_ref, u_ref, dg_ref, du_ref):
        da = lax.dot_general(do_ref[...], wd_ref[...], NT_DIMS, preferred_element_type=F32)
        gv = g_ref[...]
        sg = _sigmoid(gv)
        dg_ref[...] = (da * u_ref[...] * (sg * (1.0 + gv * (1.0 - sg)))).astype(BF16)
        du_ref[...] = (da * (gv * sg)).astype(BF16)

    dos = pl.BlockSpec((None, tm, d), lambda j, bb, i: (bb, i, 0))
    ws = pl.BlockSpec((tn, d), lambda j, bb, i: (j, 0))
    es = pl.BlockSpec((None, tm, tn), lambda j, bb, i: (bb, i, j))
    return pl.pallas_call(
        body, name=name, grid=(f // tn, b, s // tm),
        in_specs=[dos, ws, es, es], out_specs=[es, es],
        out_shape=[SDS((b, s, f), BF16), SDS((b, s, f), BF16)], compiler_params=_cparams(3))(do, wd, g, u)


def mm_tn(a, bm, tma, tnb, name):
    b, s, ka = a.shape
    nb = bm.shape[2]
    tk = min(512, s)

    def body(a_ref, b_ref, o_ref):
        @pl.when((pl.program_id(2) == 0) & (pl.program_id(3) == 0))
        def _():
            o_ref[...] = jnp.zeros_like(o_ref)
        o_ref[...] += lax.dot_general(a_ref[...], b_ref[...], TN_DIMS, preferred_element_type=F32)

    return pl.pallas_call(
        body, name=name, grid=(ka // tma, nb // tnb, b, s // tk),
        in_specs=[pl.BlockSpec((None, tk, tma), lambda i, j, bb, k: (bb, k, i)),
                  pl.BlockSpec((None, tk, tnb), lambda i, j, bb, k: (bb, k, j))],
        out_specs=pl.BlockSpec((tma, tnb), lambda i, j, bb, k: (i, j)),
        out_shape=SDS((ka, nb), F32), compiler_params=_cparams(4))(a, bm)


def dh_norm_bwd(dys, wts, x, dxn, w, sc, name):
    b, s, d = x.shape
    tm = min(256, s)
    n_in = len(dys)

    def body(*refs):
        dy_refs, w_refs = refs[:n_in], refs[n_in:2 * n_in]
        x_ref, dxn_ref, nw_ref, sc_ref, dx_ref, dsc_ref, dsh_ref, dw_ref = refs[2 * n_in:]

        @pl.when(pl.program_id(1) == 0)
        def _():
            dsc_ref[...] = jnp.zeros_like(dsc_ref)
            dsh_ref[...] = jnp.zeros_like(dsh_ref)

        @pl.when(_first_step())
        def _():
            dw_ref[...] = jnp.zeros_like(dw_ref)

        dh = jnp.dot(dy_refs[0][...], w_refs[0][...], preferred_element_type=F32)
        for k in range(1, n_in):
            dh += jnp.dot(dy_refs[k][...], w_refs[k][...], preferred_element_type=F32)
        xv = x_ref[...]
        r = _rms(xv)
        n = xv * r
        nw = nw_ref[...]
        dsc_ref[...] += jnp.sum(dh * (n * nw), axis=0, keepdims=True)
        dsh_ref[...] += jnp.sum(dh, axis=0, keepdims=True)
        dhn = dh * (1.0 + sc_ref[...])
        dw_ref[...] += jnp.sum(dhn * n, axis=0, keepdims=True)
        dx_ref[...] = dxn_ref[...] + _rms_bwd(dhn * nw, n, r)

    in_specs = [_row(tm, dy.shape[2]) for dy in dys] + [_full(wt.shape) for wt in wts]
    in_specs += [_row(tm, d), _row(tm, d), _full((1, d)), _bvec(d)]
    return pl.pallas_call(
        body, name=name, grid=(b, s // tm), in_specs=in_specs,
        out_specs=[_row(tm, d), _bvec(d), _bvec(d), _full((1, d))],
        out_shape=[SDS((b, s, d), F32), SDS((b, 1, d), F32), SDS((b, 1, d), F32), SDS((1, d), F32)],
        compiler_params=_cparams(2))(*dys, *wts, x, dxn, w, sc)


def final_loss(x, w, tgt, name):
    b, s, d = x.shape
    tm = min(512, s)

    def body(x_ref, w_ref, t_ref, loss_ref, dx_ref, dw_ref):
        @pl.when(_first_step())
        def _():
            loss_ref[...] = jnp.zeros_like(loss_ref)
            dw_ref[...] = jnp.zeros_like(dw_ref)
        xv = x_ref[...]
        r = _rms(xv)
        n = xv * r
        wv = w_ref[...]
        e = n * wv - t_ref[...]
        loss_ref[...] += jnp.sum(e * e) * (0.5 / d)
        dy = e * (1.0 / d)
        dw_ref[...] += jnp.sum(dy * n, axis=0, keepdims=True)
        dx_ref[...] = _rms_bwd(dy * wv, n, r)

    return pl.pallas_call(
        body, name=name, grid=(b, s // tm),
        in_specs=[_row(tm, d), _full((1, d)), _row(tm, d)],
        out_specs=[_full((1, LANES)), _row(tm, d), _full((1, d))],
        out_shape=[SDS((1, LANES), F32), SDS((b, s, d), F32), SDS((1, d), F32)],
        compiler_params=_cparams(2))(x, w, tgt)


def in_proj(h, win_t, name):
    b, s, d = h.shape
    tm = min(256, s)
    widths = (D_SSD, D_SSD + 2 * SSD_GROUPS * SSD_STATE, Q_LORA, KV_LORA, LANES)

    def body(h_ref, w_ref, *outs):
        p = lax.dot_general(h_ref[...], w_ref[...], NT_DIMS, preferred_element_type=F32)
        off = 0
        for o_ref, wd in zip(outs, widths):
            o_ref[...] = p[:, off:off + wd]
            off += wd

    return pl.pallas_call(
        body, name=name, grid=(b, s // tm),
        in_specs=[_row(tm, d), _full(win_t.shape)],
        out_specs=[_row(tm, wd) for wd in widths],
        out_shape=[SDS((b, s, wd), F32) for wd in widths], compiler_params=_cparams(2))(h, win_t)


def _halo_prev(ts, d):
    return pl.BlockSpec((None, 8, d), lambda b, i: (b, jnp.maximum(i * (ts // 8) - 1, 0), 0))


def _conv_taps(ext_ref, w_ref, ts):
    return [ext_ref[5 + k:5 + k + ts, :] for k in range(4)], [w_ref[k:k + 1, :] for k in range(4)]


def conv_fwd(u, cw, cb, name):
    b, s, dc = u.shape
    ts = min(512, s)
    widths = (D_SSD, SSD_GROUPS * SSD_STATE, SSD_GROUPS * SSD_STATE)

    def body(u_ref, up_ref, w_ref, b_ref, xs_ref, bm_ref, cm_ref, ext):
        ext[0:8, :] = jnp.where(pl.program_id(1) > 0, up_ref[...], 0.0)
        ext[8:8 + ts, :] = u_ref[...]
        taps, ws = _conv_taps(ext, w_ref, ts)
        v = b_ref[...] + taps[0] * ws[0] + taps[1] * ws[1] + taps[2] * ws[2] + taps[3] * ws[3]
        y = v * _sigmoid(v)
        xs_ref[...] = y[:, 0:D_SSD]
        bm_ref[...] = y[:, D_SSD:D_SSD + 256]
        cm_ref[...] = y[:, D_SSD + 256:D_SSD + 512]

    return pl.pallas_call(
        body, name=name, grid=(b, s // ts),
        in_specs=[_row(ts, dc), _halo_prev(ts, dc), _full((4, dc)), _full((1, dc))],
        out_specs=[_row(ts, wd) for wd in widths],
        out_shape=[SDS((b, s, wd), F32) for wd in widths],
        scratch_shapes=[pltpu.VMEM((ts + 8, dc), F32)], compiler_params=_cparams(2))(u, u, cw, cb)


def conv_bwd_a(dxs, dbm, dcm, u, cw, cb, name):
    b, s, dc = u.shape
    ts = min(512, s)

    def body(dxs_ref, dbm_ref, dcm_ref, u_ref, up_ref, w_ref, b_ref, dv_ref, dwb_ref, ext):
        @pl.when(_first_step())
        def _():
            dwb_ref[...] = jnp.zeros_like(dwb_ref)
        ext[0:8, :] = jnp.where(pl.program_id(1) > 0, up_ref[...], 0.0)
        ext[8:8 + ts, :] = u_ref[...]
        taps, ws = _conv_taps(ext, w_ref, ts)
        v = b_ref[...] + taps[0] * ws[0] + taps[1] * ws[1] + taps[2] * ws[2] + taps[3] * ws[3]
        sg = _sigmoid(v)
        dy = jnp.concatenate([dxs_ref[...], dbm_ref[...], dcm_ref[...]], axis=1)
        dv = dy * (sg * (1.0 + v * (1.0 - sg)))
        dv_ref[...] = dv
        for k in range(4):
            dwb_ref[k:k + 1, :] += jnp.sum(dv * taps[k], axis=0, keepdims=True)
        dwb_ref[4:5, :] += jnp.sum(dv, axis=0, keepdims=True)

    return pl.pallas_call(
        body, name=name, grid=(b, s // ts),
        in_specs=[_row(ts, D_SSD), _row(ts, 256), _row(ts, 256), _row(ts, dc), _halo_prev(ts, dc),
                  _full((4, dc)), _full((1, dc))],
        out_specs=[_row(ts, dc), _full((8, dc))],
        out_shape=[SDS((b, s, dc), F32), SDS((8, dc), F32)],
        scratch_shapes=[pltpu.VMEM((ts + 8, dc), F32)], compiler_params=_cparams(2))(dxs, dbm, dcm, u, u, cw, cb)


def conv_bwd_b(dv, cw, name):
    b, s, dc = dv.shape
    ts = min(512, s)
    nt = s // ts

    def body(dv_ref, dn_ref, w_ref, du_ref, ext):
        ext[0:ts, :] = dv_ref[...]
        ext[ts:ts + 8, :] = jnp.where(pl.program_id(1) < nt - 1, dn_ref[...], 0.0)
        acc = ext[3:3 + ts, :] * w_ref[0:1, :]
        for k in range(1, 4):
            acc += ext[3 - k:3 - k + ts, :] * w_ref[k:k + 1, :]
        du_ref[...] = acc.astype(BF16)

    nxt = pl.BlockSpec((None, 8, dc), lambda bb, i: (bb, jnp.minimum((i + 1) * (ts // 8), s // 8 - 1), 0))
    return pl.pallas_call(
        body, name=name, grid=(b, nt),
        in_specs=[_row(ts, dc), nxt, _full((4, dc))],
        out_specs=_row(ts, dc), out_shape=SDS((b, s, dc), BF16),
        scratch_shapes=[pltpu.VMEM((ts + 8, dc), F32)], compiler_params=_cparams(2))(dv, dv, cw)


def _ssd_common(misc_ref, dtb_ref, alog_ref, e_ref):
    ln = CHUNK
    lane = lax.broadcasted_iota(I32, (ln, LANES), 1)
    lane1 = lax.broadcasted_iota(I32, (1, LANES), 1)
    pre = misc_ref[...] + dtb_ref[...]
    dt_s = jnp.where(lane < SSD_HEADS, _softplus(pre), 0.0)
    a_neg = jnp.where(lane1 < SSD_HEADS, -jnp.exp(alog_ref[...]), 0.0)
    ri = lax.broadcasted_iota(I32, (ln, ln), 0)
    ci = lax.broadcasted_iota(I32, (ln, ln), 1)
    tril = ci <= ri
    acum = jnp.dot(tril.astype(F32), dt_s * a_neg, preferred_element_type=F32, precision=HI)
    e = e_ref[...]
    expand = lambda t: jnp.dot(t, e, preferred_element_type=F32, precision=HI)
    last = acum[ln - 1:ln, :]
    delta_s = jnp.exp(last - acum)
    return dict(pre=pre, dt_s=dt_s, a_neg=a_neg, tril=tril, ri=ri, ci=ci, acum=acum, acum_t=acum.T,
                dt_e=expand(dt_s), eac_e=expand(jnp.exp(acum)), delta_s=delta_s, del_e=expand(delta_s))


def _decay(cm, h):
    seg = cm["acum"][:, h:h + 1] - cm["acum_t"][h:h + 1, :]
    return jnp.exp(jnp.where(cm["tril"], seg, -jnp.inf))


def ssd_fwd(xs, bm, cm_, misc, z, dtb, alog, dskip_e, norm_w, e_mat, name):
    b, s, _ = xs.shape
    ln, nc = CHUNK, s // CHUNK
    gw = D_SSD // SSD_GROUPS
    hpg = SSD_HEADS // SSD_GROUPS

    def body(xs_ref, b_ref, c_ref, misc_ref, z_ref, dtb_ref, alog_ref, dsk_ref, nw_ref, e_ref,
             ys_ref, y_ref, p_ref, st, yd):
        @pl.when(pl.program_id(1) == 0)
        def _():
            st[...] = jnp.zeros_like(st)
        cm = _ssd_common(misc_ref, dtb_ref, alog_ref, e_ref)
        xsv = xs_ref[...]
        xdt = xsv * cm["dt_e"]
        xdt_b = xdt.astype(BF16)
        xd_b = (xdt * cm["del_e"]).astype(BF16)
        gam_e = cm["eac_e"][ln - 1:ln, :]
        p_ref[...] = st[...]
        yoff = []
        for g in range(SSD_GROUPS):
            gs = slice(gw * g, gw * (g + 1))
            bg = b_ref[:, SSD_STATE * g:SSD_STATE * (g + 1)].astype(BF16)
            cg = c_ref[:, SSD_STATE * g:SSD_STATE * (g + 1)].astype(BF16)
            cb = lax.dot_general(cg, bg, NT_DIMS, preferred_element_type=F32)
            st_g = st[:, gs]
            yoff.append(jnp.dot(cg, st_g.astype(BF16), preferred_element_type=F32) * cm["eac_e"][:, gs])
            for j in range(hpg):
                h = hpg * g + j
                hs = slice(SSD_HEAD_DIM * h, SSD_HEAD_DIM * (h + 1))
                m = (cb * _decay(cm, h)).astype(BF16)
                yd[:, hs] = jnp.dot(m, xdt_b[:, hs], preferred_element_type=F32)
            new = lax.dot_general(bg, xd_b[:, gs], TN_DIMS, preferred_element_type=F32)
            st[:, gs] = st_g * gam_e[:, gs] + new
        y = yd[...] + jnp.concatenate(yoff, axis=1) + dsk_ref[...] * xsv
        y_ref[...] = y
        zz = z_ref[...]
        yg = y * (zz * _sigmoid(zz))
        outs = []
        for g in range(SSD_GROUPS):
            ygg = yg[:, gw * g:gw * (g + 1)]
            outs.append(ygg * _rms(ygg) * nw_ref[:, gw * g:gw * (g + 1)])
        ys_ref[...] = jnp.concatenate(outs, axis=1).astype(BF16)

    row = lambda d: pl.BlockSpec((None, ln, d), lambda bb, c: (bb, c, 0))
    return pl.pallas_call(
        body, name=name, grid=(b, nc),
        in_specs=[row(D_SSD), row(256), row(256), row(LANES), row(D_SSD), _full((1, LANES)), _full((1, LANES)),
                  _full((1, D_SSD)), _full((1, D_SSD)), _full((LANES, D_SSD))],
        out_specs=[row(D_SSD), row(D_SSD), pl.BlockSpec((None, None, SSD_STATE, D_SSD), lambda bb, c: (bb, c, 0, 0))],
        out_shape=[SDS((b, s, D_SSD), BF16), SDS((b, s, D_SSD), F32), SDS((b, nc, SSD_STATE, D_SSD), F32)],
        scratch_shapes=[pltpu.VMEM((SSD_STATE, D_SSD), F32), pltpu.VMEM((ln, D_SSD), F32)],
        compiler_params=_cparams(2))(xs, bm, cm_, misc, z, dtb, alog, dskip_e, norm_w, e_mat)


def ssd_bwd(dys, y, z, xs, bm, cm_, misc, prev, dtb, alog, dskip_e, norm_w, e_mat, et_mat, name):
    b, s, _ = xs.shape
    ln, nc = CHUNK, s // CHUNK
    gw = D_SSD // SSD_GROUPS
    hpg = SSD_HEADS // SSD_GROUPS

    def body(dys_ref, y_ref, z_ref, xs_ref, b_ref, c_ref, misc_ref, p_ref, dtb_ref, alog_ref, dsk_ref, nw_ref,
             e_ref, et_ref, dxs_ref, db_ref, dc_ref, dz_ref, ddt_ref, dnw_ref, ddsk_ref, ddtb_ref, dalog_ref,
             dst, dxd, dac_t):
        @pl.when(_first_step())
        def _():
            for r_ in (dnw_ref, ddsk_ref, ddtb_ref, dalog_ref):
                r_[...] = jnp.zeros_like(r_)

        @pl.when(pl.program_id(1) == 0)
        def _():
            dst[...] = jnp.zeros_like(dst)

        cm = _ssd_common(misc_ref, dtb_ref, alog_ref, e_ref)
        et = et_ref[...]
        squeeze = lambda t: jnp.dot(t, et, preferred_element_type=F32, precision=HI)
        lane = lax.broadcasted_iota(I32, (ln, LANES), 1)
        sub = lax.broadcasted_iota(I32, (LANES, ln), 0)
        xsv = xs_ref[...]
        xdt = xsv * cm["dt_e"]
        xdt_b = xdt.astype(BF16)
        xd_b = (xdt * cm["del_e"]).astype(BF16)
        eac_e = cm["eac_e"]
        gam_e = eac_e[ln - 1:ln, :]

        yv, zz, dyo = y_ref[...], z_ref[...], dys_ref[...]
        sz = _sigmoid(zz)
        silu_z = zz * sz
        yg = yv * silu_z
        dyg, dnw = [], []
        for g in range(SSD_GROUPS):
            gs = slice(gw * g, gw * (g + 1))
            ygg = yg[:, gs]
            r = _rms(ygg)
            n = ygg * r
            dnw.append(jnp.sum(dyo[:, gs] * n, axis=0, keepdims=True))
            dyg.append(_rms_bwd(dyo[:, gs] * nw_ref[:, gs], n, r))
        dyg = jnp.concatenate(dyg, axis=1)
        dnw_ref[...] += jnp.concatenate(dnw, axis=1)
        dz_ref[...] = (dyg * yv * (sz * (1.0 + zz * (1.0 - sz)))).astype(BF16)
        dy = dyg * silu_z
        ddsk_ref[...] += jnp.sum(dy * xsv, axis=0, keepdims=True)
        dy_b = dy.astype(BF16)

        dacum = jnp.zeros((ln, LANES), F32)
        dac_t[...] = jnp.zeros_like(dac_t)
        w1, dgam = [], []
        for g in range(SSD_GROUPS):
            gs = slice(gw * g, gw * (g + 1))
            ss = slice(SSD_STATE * g, SSD_STATE * (g + 1))
            bg = b_ref[:, ss].astype(BF16)
            cg = c_ref[:, ss].astype(BF16)
            cb = lax.dot_general(cg, bg, NT_DIMS, preferred_element_type=F32)
            pt = p_ref[:, gs]
            pt_b = pt.astype(BF16)
            dst_g = dst[:, gs]
            dst_b = dst_g.astype(BF16)
            edy = (dy[:, gs] * eac_e[:, gs]).astype(BF16)
            dcg = lax.dot_general(edy, pt_b, NT_DIMS, preferred_element_type=F32)
            dpt = lax.dot_general(cg, edy, TN_DIMS, preferred_element_type=F32)
            yoff = jnp.dot(cg, pt_b, preferred_element_type=F32) * eac_e[:, gs]
            dxd_g = jnp.dot(bg, dst_b, preferred_element_type=F32)
            dbg = lax.dot_general(xd_b[:, gs], dst_b, NT_DIMS, preferred_element_type=F32)
            ddel = dxd_g * xdt[:, gs] * cm["del_e"][:, gs]
            w1.append(dy[:, gs] * yoff - ddel)
            dgam.append(jnp.sum(ddel, axis=0, keepdims=True) + jnp.sum(dst_g * pt, axis=0, keepdims=True) * gam_e[:, gs])
            dxd[:, gs] = dxd_g * cm["del_e"][:, gs]
            dst[:, gs] = dst_g * gam_e[:, gs] + dpt
            dcb = jnp.zeros((ln, ln), F32)
            for j in range(hpg):
                h = hpg * g + j
                hs = slice(SSD_HEAD_DIM * h, SSD_HEAD_DIM * (h + 1))
                lam = _decay(cm, h)
                m = cb * lam
                dm = lax.dot_general(dy_b[:, hs], xdt_b[:, hs], NT_DIMS, preferred_element_type=F32)
                dxd[:, hs] += lax.dot_general(m.astype(BF16), dy_b[:, hs], TN_DIMS, preferred_element_type=F32)
                dcb += dm * lam
                wl = dm * m
                dacum += jnp.where(lane == h, jnp.sum(wl, axis=1, keepdims=True), 0.0)
                dac_t[...] -= jnp.where(sub == h, jnp.sum(wl, axis=0, keepdims=True), 0.0)
            dcb_b = dcb.astype(BF16)
            dc_ref[:, ss] = dcg + jnp.dot(dcb_b, bg, preferred_element_type=F32)
            db_ref[:, ss] = dbg + lax.dot_general(dcb_b, cg, TN_DIMS, preferred_element_type=F32)

        dxdt = dxd[...]
        dxs_ref[...] = dy * dsk_ref[...] + dxdt * cm["dt_e"]
        dacum += squeeze(jnp.concatenate(w1, axis=1)) + dac_t[...].T
        dlast = squeeze(jnp.broadcast_to(jnp.concatenate(dgam, axis=1), (8, D_SSD)))[0:1, :]
        dacum += jnp.where(lax.broadcasted_iota(I32, (ln, LANES), 0) == ln - 1, dlast, 0.0)
        triu = (cm["ci"] >= cm["ri"]).astype(F32)
        da = jnp.dot(triu, dacum, preferred_element_type=F32, precision=HI)
        ddt = da * cm["a_neg"] + squeeze(dxdt * xsv)
        dalog_ref[...] += jnp.sum(da * cm["dt_s"], axis=0, keepdims=True) * cm["a_neg"]
        ddt_raw = jnp.where(lane < SSD_HEADS, ddt * _sigmoid(cm["pre"]), 0.0)
        ddt_ref[...] = ddt_raw
        ddtb_ref[...] += jnp.sum(ddt_raw, axis=0, keepdims=True)

    row = lambda d: pl.BlockSpec((None, ln, d), lambda bb, c: (bb, nc - 1 - c, 0))
    return pl.pallas_call(
        body, name=name, grid=(b, nc),
        in_specs=[row(D_SSD), row(D_SSD), row(D_SSD), row(D_SSD), row(256), row(256), row(LANES),
                  pl.BlockSpec((None, None, SSD_STATE, D_SSD), lambda bb, c: (bb, nc - 1 - c, 0, 0)),
                  _full((1, LANES)), _full((1, LANES)), _full((1, D_SSD)), _full((1, D_SSD)),
                  _full((LANES, D_SSD)), _full((D_SSD, LANES))],
        out_specs=[row(D_SSD), row(256), row(256), row(D_SSD), row(LANES),
                   _full((1, D_SSD)), _full((1, D_SSD)), _full((1, LANES)), _full((1, LANES))],
        out_shape=[SDS((b, s, D_SSD), F32), SDS((b, s, 256), F32), SDS((b, s, 256), F32), SDS((b, s, D_SSD), BF16),
                   SDS((b, s, LANES), F32), SDS((1, D_SSD), F32), SDS((1, D_SSD), F32), SDS((1, LANES), F32),
                   SDS((1, LANES), F32)],
        scratch_shapes=[pltpu.VMEM((SSD_STATE, D_SSD), F32), pltpu.VMEM((ln, D_SSD), F32), pltpu.VMEM((LANES, ln), F32)],
        compiler_params=_cparams(2))(dys, y, z, xs, bm, cm_, misc, prev, dtb, alog, dskip_e, norm_w, e_mat, et_mat)


def _rope(xv, cc, sp, sm):
    n = xv.shape[1]
    return xv * cc + pltpu.roll(xv, 16, 1) * sp + pltpu.roll(xv, n - 16, 1) * sm


def _rope_bwd(dy, cc, sp, sm):
    n = dy.shape[1]
    return dy * cc + pltpu.roll(dy * sp, n - 16, 1) + pltpu.roll(dy * sm, 16, 1)


def _tile8(t):
    return jnp.concatenate([t] * MLA_HEADS, axis=1)


def qkv_fwd(cq, ckv, misc, cc, sp, sm, qnw, kvnw, wuq_t, wukv_t, place, name):
    b, s, _ = cq.shape
    tm = min(256, s)
    hd = MLA_HEADS * HEAD_PAD

    def body(cq_ref, ckv_ref, misc_ref, cc_ref, sp_ref, sm_ref, qnw_ref, kvnw_ref, wq_ref, wkv_ref, pl_ref,
             q_ref, k_ref, v_ref, qn_ref, kvn_ref):
        cqv, ckvv = cq_ref[...], ckv_ref[...]
        qn = (cqv * _rms(cqv) * qnw_ref[...]).astype(BF16)
        kvn = (ckvv * _rms(ckvv) * kvnw_ref[...]).astype(BF16)
        qn_ref[...] = qn
        kvn_ref[...] = kvn
        cc1, sp1, sm1 = cc_ref[...], sp_ref[...], sm_ref[...]
        q = lax.dot_general(qn, wq_ref[...], NT_DIMS, preferred_element_type=F32)
        q_ref[...] = _rope(q, _tile8(cc1), _tile8(sp1), _tile8(sm1)).astype(BF16)
        kv = lax.dot_general(kvn, wkv_ref[...], NT_DIMS, preferred_element_type=F32)
        kr = jnp.dot(misc_ref[...], pl_ref[...], preferred_element_type=F32, precision=HI)
        kr = _rope(kr, cc1, sp1, sm1)
        k_ref[...] = (kv[:, 0:hd] + _tile8(kr)).astype(BF16)
        v_ref[...] = kv[:, hd:2 * hd].astype(BF16)

    return pl.pallas_call(
        body, name=name, grid=(b, s // tm),
        in_specs=[_row(tm, Q_LORA), _row(tm, KV_LORA), _row(tm, LANES), _row(tm, LANES), _row(tm, LANES), _row(tm, LANES),
                  _full((1, Q_LORA)), _full((1, KV_LORA)), _full(wuq_t.shape), _full(wukv_t.shape), _full((LANES, LANES))],
        out_specs=[_row(tm, hd), _row(tm, hd), _row(tm, hd), _row(tm, Q_LORA), _row(tm, KV_LORA)],
        out_shape=[SDS((b, s, hd), BF16)] * 3 + [SDS((b, s, Q_LORA), BF16), SDS((b, s, KV_LORA), BF16)],
        compiler_params=_cparams(2))(cq, ckv, misc, cc, sp, sm, qnw, kvnw, wuq_t, wukv_t, place)


def qkv_bwd(dq, dk, dv, ddt, cq, ckv, cc, sp, sm, qnw, kvnw, wuq_t, wukv_t, place_t, name):
    b, s, _ = cq.shape
    tm = min(256, s)
    hd = MLA_HEADS * HEAD_PAD

    def body(dq_ref, dk_ref, dv_ref, ddt_ref, cq_ref, ckv_ref, cc_ref, sp_ref, sm_ref, qnw_ref, kvnw_ref,
             wq_ref, wkv_ref, plt_ref, dcq_ref, dckv_ref, dmisc_ref, dqp_ref, dkv_ref, dqnw_ref, dkvnw_ref):
        @pl.when(_first_step())
        def _():
            dqnw_ref[...] = jnp.zeros_like(dqnw_ref)
            dkvnw_ref[...] = jnp.zeros_like(dkvnw_ref)
        cc1, sp1, sm1 = cc_ref[...], sp_ref[...], sm_ref[...]
        dqp = _rope_bwd(dq_ref[...], _tile8(cc1), _tile8(sp1), _tile8(sm1)).astype(BF16)
        dqp_ref[...] = dqp
        dkf = dk_ref[...]
        dkv_b = jnp.concatenate([dkf, dv_ref[...]], axis=1).astype(BF16)
        dkv_ref[...] = dkv_b
        dkr = dkf[:, 0:HEAD_PAD]
        for h in range(1, MLA_HEADS):
            dkr += dkf[:, HEAD_PAD * h:HEAD_PAD * (h + 1)]
        dkr = _rope_bwd(dkr, cc1, sp1, sm1)
        dmisc_ref[...] = (jnp.dot(dkr, plt_ref[...], preferred_element_type=F32, precision=HI) + ddt_ref[...]).astype(BF16)

        def norm_bwd(dn_w, xv, w_ref, dw_ref, dx_ref):
            r = _rms(xv)
            n = xv * r
            dw_ref[...] += jnp.sum(dn_w * n, axis=0, keepdims=True)
            dx_ref[...] = _rms_bwd(dn_w * w_ref[...], n, r).astype(BF16)

        norm_bwd(jnp.dot(dqp, wq_ref[...], preferred_element_type=F32), cq_ref[...], qnw_ref, dqnw_ref, dcq_ref)
        norm_bwd(jnp.dot(dkv_b, wkv_ref[...], preferred_element_type=F32), ckv_ref[...], kvnw_ref, dkvnw_ref, dckv_ref)

    return pl.pallas_call(
        body, name=name, grid=(b, s // tm),
        in_specs=[_row(tm, hd), _row(tm, hd), _row(tm, hd), _row(tm, LANES), _row(tm, Q_LORA), _row(tm, KV_LORA),
                  _row(tm, LANES), _row(tm, LANES), _row(tm, LANES), _full((1, Q_LORA)), _full((1, KV_LORA)),
                  _full(wuq_t.shape), _full(wukv_t.shape), _full((LANES, LANES))],
        out_specs=[_row(tm, Q_LORA), _row(tm, KV_LORA), _row(tm, LANES), _row(tm, hd), _row(tm, 2 * hd),
                   _full((1, Q_LORA)), _full((1, KV_LORA))],
        out_shape=[SDS((b, s, Q_LORA), BF16), SDS((b, s, KV_LORA), BF16), SDS((b, s, LANES), BF16),
                   SDS((b, s, hd), BF16), SDS((b, s, 2 * hd), BF16), SDS((1, Q_LORA), F32), SDS((1, KV_LORA), F32)],
        compiler_params=_cparams(2))(dq, dk, dv, ddt, cq, ckv, cc, sp, sm, qnw, kvnw, wuq_t, wukv_t, place_t)


ATT_SCALE = 1.0 / math.sqrt(QK_DIM)


def _att_tile(s):
    return min(512, s)


def flash_fwd(q, k, v, name):
    b, s, hd = q.shape
    t = _att_tile(s)
    nb = s // t
    th = t // 2

    def body(q_ref, k_ref, v_ref, o_ref, lse_ref, m_s, l_s, acc):
        i = pl.program_id(2)
        m_s[...] = jnp.full_like(m_s, -jnp.inf)
        l_s[...] = jnp.zeros_like(l_s)
        acc[...] = jnp.zeros_like(acc)

        def update(j, diagonal):
            ks = pl.ds(pl.multiple_of(j * t, t), t)
            kb, vb = k_ref[ks, :], v_ref[ks, :]
            for half in range(2):
                rs = slice(th * half, th * (half + 1))
                sc = lax.dot_general(q_ref[rs, :], kb, NT_DIMS, preferred_element_type=F32) * ATT_SCALE
                if diagonal:
                    row = lax.broadcasted_iota(I32, (th, t), 0) + th * half
                    col = lax.broadcasted_iota(I32, (th, t), 1)
                    sc = jnp.where(col <= row, sc, -jnp.inf)
                m_prev = m_s[rs, :]
                m_new = jnp.maximum(m_prev, jnp.max(sc, axis=1, keepdims=True))
                alpha = jnp.exp(m_prev - m_new)
                p = jnp.exp(sc - m_new)
                l_s[rs, :] = alpha * l_s[rs, :] + jnp.sum(p, axis=1, keepdims=True)
                acc[rs, :] = alpha * acc[rs, :] + jnp.dot(p.astype(BF16), vb, preferred_element_type=F32)
                m_s[rs, :] = m_new

        def step(j, carry):
            update(j, False)
            return carry

        lax.fori_loop(0, i, step, 0)
        update(i, True)
        o_ref[...] = acc[...] / l_s[...]
        lse_ref[...] = m_s[...] + jnp.log(l_s[...])

    qs = pl.BlockSpec((None, t, HEAD_PAD), lambda bb, h, i: (bb, i, h))
    ks = pl.BlockSpec((None, s, HEAD_PAD), lambda bb, h, i: (bb, 0, h))
    ls = pl.BlockSpec((None, None, t, 1), lambda bb, h, i: (bb, h, i, 0))
    return pl.pallas_call(
        body, name=name, grid=(b, MLA_HEADS, nb),
        in_specs=[qs, ks, ks], out_specs=[qs, ls],
        out_shape=[SDS((b, s, hd), F32), SDS((b, MLA_HEADS, s, 1), F32)],
        scratch_shapes=[pltpu.VMEM((t, 1), F32), pltpu.VMEM((t, 1), F32), pltpu.VMEM((t, HEAD_PAD), F32)],
        compiler_params=_cparams(3))(q, k, v)


def flash_bwd(q, k, v, do, lse, dlt, name):
    b, s, hd = q.shape
    t = _att_tile(s)
    nb = s // t
    th = t // 2
    lse_r = lse.reshape(b, MLA_HEADS, nb, 1, t)
    dlt_r = dlt.reshape(b, MLA_HEADS, nb, 1, t)

    def body(q_ref, k_ref, v_ref, do_ref, lse_ref, dlt_ref, dq_ref, dk_ref, dv_ref):
        dq_ref[...] = jnp.zeros_like(dq_ref)
        dk_ref[...] = jnp.zeros_like(dk_ref)
        dv_ref[...] = jnp.zeros_like(dv_ref)

        def tile(j, i, diagonal):
            qs = pl.ds(pl.multiple_of(i * t, t), t)
            qb, dob = q_ref[qs, :], do_ref[qs, :]
            lse_row, dlt_row = lse_ref[i], dlt_ref[i]
            dq_acc = None
            for half in range(2):
                ks = pl.ds(pl.multiple_of(j * t + th * half, th), th)
                kb, vb = k_ref[ks, :], v_ref[ks, :]
                st = lax.dot_general(kb, qb, NT_DIMS, preferred_element_type=F32) * ATT_SCALE
                if diagonal:
                    row = lax.broadcasted_iota(I32, (th, t), 0) + th * half
                    col = lax.broadcasted_iota(I32, (th, t), 1)
                    st = jnp.where(row <= col, st, -jnp.inf)
                pt = jnp.exp(st - lse_row)
                dv_ref[ks, :] += jnp.dot(pt.astype(BF16), dob, preferred_element_type=F32)
                dpt = lax.dot_general(vb, dob, NT_DIMS, preferred_element_type=F32)
                dst = (pt * (dpt - dlt_row) * ATT_SCALE).astype(BF16)
                dk_ref[ks, :] += jnp.dot(dst, qb, preferred_element_type=F32)
                part = lax.dot_general(dst, kb, TN_DIMS, preferred_element_type=F32)
                dq_acc = part if dq_acc is None else dq_acc + part
            dq_ref[qs, :] += dq_acc

        def key_tile(j, carry):
            tile(j, j, True)

            def query_tile(i, c2):
                tile(j, i, False)
                return c2

            lax.fori_loop(j + 1, nb, query_tile, 0)
            return carry

        lax.fori_loop(0, nb, key_tile, 0)

    hs = pl.BlockSpec((None, s, HEAD_PAD), lambda bb, h: (bb, 0, h))
    ls = pl.BlockSpec((None, None, nb, 1, t), lambda bb, h: (bb, h, 0, 0, 0))
    return pl.pallas_call(
        body, name=name, grid=(b, MLA_HEADS),
        in_specs=[hs, hs, hs, hs, ls, ls], out_specs=[hs, hs, hs],
        out_shape=[SDS((b, s, hd), F32)] * 3, compiler_params=_cparams(2))(q, k, v, do, lse_r, dlt_r)


def out_proj(ys, attn, mnw, wo, x, gate, name):
    b, s, d = x.shape
    tm = min(256, s)

    def body(ys_ref, at_ref, mnw_ref, wo_ref, x_ref, g_ref, xn_ref, o_ref, ym_ref):
        av = at_ref[...]
        ym = (av * _rms(av) * mnw_ref[...]).astype(BF16)
        ym_ref[...] = ym
        o = jnp.dot(ys_ref[...], wo_ref[0:D_SSD, :], preferred_element_type=F32)
        o += jnp.dot(ym, wo_ref[D_SSD:2 * D_SSD, :], preferred_element_type=F32)
        xn_ref[...] = x_ref[...] + g_ref[...] * o
        o_ref[...] = o.astype(BF16)

    return pl.pallas_call(
        body, name=name, grid=(b, s // tm),
        in_specs=[_row(tm, D_SSD), _row(tm, D_SSD), _full((1, D_SSD)), _full(wo.shape), _row(tm, d), _bvec(d)],
        out_specs=[_row(tm, d), _row(tm, d), _row(tm, D_SSD)],
        out_shape=[SDS((b, s, d), F32), SDS((b, s, d), BF16), SDS((b, s, D_SSD), BF16)],
        compiler_params=_cparams(2))(ys, attn, mnw, wo, x, gate)


def out_proj_bwd(dout, attn, mnw, wo, name):
    b, s, d = dout.shape
    tm = min(256, s)

    def body(do_ref, at_ref, mnw_ref, wo_ref, dys_ref, dat_ref, dlt_ref, dw_ref):
        @pl.when(_first_step())
        def _():
            dw_ref[...] = jnp.zeros_like(dw_ref)
        dov = do_ref[...]
        dys_ref[...] = lax.dot_general(dov, wo_ref[0:D_SSD, :], NT_DIMS, preferred_element_type=F32)
        dym = lax.dot_general(dov, wo_ref[D_SSD:2 * D_SSD, :], NT_DIMS, preferred_element_type=F32)
        av = at_ref[...]
        r = _rms(av)
        n = av * r
        dw_ref[...] += jnp.sum(dym * n, axis=0, keepdims=True)
        dat = _rms_bwd(dym * mnw_ref[...], n, r)
        dat_ref[...] = dat.astype(BF16)
        prod = dat * av
        for h in range(MLA_HEADS):
            dlt_ref[h] = jnp.sum(prod[:, HEAD_PAD * h:HEAD_PAD * (h + 1)], axis=1, keepdims=True)

    return pl.pallas_call(
        body, name=name, grid=(b, s // tm),
        in_specs=[_row(tm, d), _row(tm, D_SSD), _full((1, D_SSD)), _full(wo.shape)],
        out_specs=[_row(tm, D_SSD), _row(tm, D_SSD),
                   pl.BlockSpec((None, MLA_HEADS, tm, 1), lambda bb, i: (bb, 0, i, 0)), _full((1, D_SSD))],
        out_shape=[SDS((b, s, D_SSD), F32), SDS((b, s, D_SSD), BF16), SDS((b, MLA_HEADS, s, 1), F32),
                   SDS((1, D_SSD), F32)],
        compiler_params=_cparams(2))(dout, attn, mnw, wo)


def adaln_fwd(c_all, w_ada, b_ada, name):
    nb, d = c_all.shape
    n = w_ada.shape[1]

    def body(c_ref, w_ref, b_ref, m_ref, ca_ref):
        cv = c_ref[...]
        ca = (cv * _sigmoid(cv)).astype(BF16)
        ca_ref[...] = ca
        m_ref[...] = jnp.dot(ca, w_ref[...].astype(BF16), preferred_element_type=F32) + b_ref[...]

    return pl.pallas_call(
        body, name=name, out_shape=[SDS((nb, n), F32), SDS((nb, d), BF16)],
        compiler_params=pltpu.CompilerParams(vmem_limit_bytes=VMEM_LIMIT))(c_all, w_ada, b_ada)


def adaln_bwd(c_act, dmod_cols, name):
    d, n = c_act.shape[1], dmod_cols.shape[1]

    def body(c_ref, dm_ref, gw_ref):
        gw_ref[...] = lax.dot_general(c_ref[...], dm_ref[...].astype(BF16), TN_DIMS, preferred_element_type=F32)

    return pl.pallas_call(
        body, name=name, out_shape=SDS((d, n), F32),
        compiler_params=pltpu.CompilerParams(vmem_limit_bytes=VMEM_LIMIT))(c_act, dmod_cols)


def sum_rows(x, name):
    def body(x_ref, o_ref):
        o_ref[...] = jnp.sum(x_ref[...], axis=0, keepdims=True)
    return pl.pallas_call(body, name=name, out_shape=SDS((1, x.shape[1]), F32))(x)


def squeeze_heads(x, et_mat, name):
    def body(x_ref, et_ref, o_ref):
        xv = jnp.broadcast_to(x_ref[...], (8, x.shape[1]))
        o_ref[...] = jnp.dot(xv, et_ref[...], preferred_element_type=F32, precision=HI)[0:1, :]
    return pl.pallas_call(body, name=name, out_shape=SDS((1, LANES), F32))(x, et_mat)


def sum_blocks(x, name):
    n, r, c = x.shape

    def body(x_ref, o_ref):
        acc = x_ref[0].astype(F32)
        for k in range(1, n):
            acc += x_ref[k].astype(F32)
        o_ref[...] = acc

    return pl.pallas_call(body, name=name, out_shape=SDS((r, c), F32),
                          compiler_params=pltpu.CompilerParams(vmem_limit_bytes=VMEM_LIMIT))(x)


def _adam_math(w, g, m, v):
    m = ADAM_B1 * m + (1.0 - ADAM_B1) * g
    v = ADAM_B2 * v + (1.0 - ADAM_B2) * (g * g)
    m_hat = m / (1.0 - ADAM_B1 ** ADAM_STEP)
    v_hat = v / (1.0 - ADAM_B2 ** ADAM_STEP)
    return -ADAM_LR * (m_hat / (jnp.sqrt(v_hat) + ADAM_EPS) + ADAM_WD * w), m, v


def adamw(w, g, m, v, name):
    r, c = w.shape
    tr = r
    for cand in (512, 256, 128, 64, 32, 16, 8):
        if r % cand == 0 and cand * c * 4 <= 2 * 1024 * 1024:
            tr = cand
            break

    def body(w_ref, g_ref, m_ref, v_ref, d_ref, mo_ref, vo_ref):
        d_ref[...], mo_ref[...], vo_ref[...] = _adam_math(w_ref[...], g_ref[...], m_ref[...], v_ref[...])

    spec = pl.BlockSpec((tr, c), lambda i: (i, 0))
    return pl.pallas_call(
        body, name=name, grid=(r // tr,), in_specs=[spec] * 4, out_specs=[spec] * 3,
        out_shape=[SDS((r, c), F32)] * 3, compiler_params=_cparams(1))(w, g, m, v)


def adamw_sum8(w, gparts, m, v, name):
    r, c = w.shape
    tr = 64 if r % 64 == 0 else r

    def body(w_ref, gp_ref, m_ref, v_ref, g_ref, d_ref, mo_ref, vo_ref):
        g = gp_ref[0].astype(F32)
        for k in range(1, N_DEV):
            g += gp_ref[k].astype(F32)
        g_ref[...] = g
        d_ref[...], mo_ref[...], vo_ref[...] = _adam_math(w_ref[...], g, m_ref[...], v_ref[...])

    spec = pl.BlockSpec((tr, c), lambda i: (i, 0))
    gspec = pl.BlockSpec((N_DEV, tr, c), lambda i: (0, i, 0))
    return pl.pallas_call(
        body, name=name, grid=(r // tr,), in_specs=[spec, gspec, spec, spec], out_specs=[spec] * 4,
        out_shape=[SDS((r, c), F32)] * 4, compiler_params=_cparams(1))(w, gparts, m, v)


PACK = (("ffn1_w_gate", 352, 352), ("ffn1_w_up", 352, 352), ("ffn1_w_down", 352, 352),
        ("ffn2_w_gate", 352, 352), ("ffn2_w_up", 352, 352), ("ffn2_w_down", 352, 352),
        ("w_out", 256, 256), ("w_in", 406, 416), ("w_ukv", 48, 48), ("w_uq", 36, 48))
PACK_ROWS = sum(p[2] for p in PACK)
PACK_OFF = {}
_o = 0
for _n, _r, _p in PACK:
    PACK_OFF[_n] = (_o, _r)
    _o += _p
TRANSPOSED = ("ffn1_w_gate", "ffn1_w_up", "ffn2_w_gate", "ffn2_w_up", "w_in", "w_ukv", "w_uq")


def _shard_to_rows(name, w):
    w = w[0]
    if name in TRANSPOSED:
        w = w.T
    return w.reshape(-1, D_MODEL)


def _rows_to_shard(name, rows, like):
    shp = like.shape[1:]
    if name in TRANSPOSED:
        return rows.reshape(shp[1], shp[0]).T[None]
    return rows.reshape(shp)[None]


def _pack_shards(ws, dtype):
    parts = []
    for name, real, padded in PACK:
        rows = _shard_to_rows(name, ws[name]).astype(dtype)
        if padded > real:
            rows = jnp.pad(rows, ((0, padded - real), (0, 0)))
        parts.append(rows)
    return jnp.concatenate(parts, axis=0)


def _seg(g, name):
    o, r = PACK_OFF[name]
    return g[:, o:o + r]


def _pack_rows(arrs):
    parts = []
    for a in arrs:
        flat = a.reshape(-1).astype(F32)
        pad = (-flat.shape[0]) % D_MODEL
        if pad:
            flat = jnp.pad(flat, (0, pad))
        parts.append(flat.reshape(-1, D_MODEL))
    out = jnp.concatenate(parts, axis=0)
    pad = (-out.shape[0]) % 8
    if pad:
        out = jnp.pad(out, ((0, pad), (0, 0)))
    return out


def _unpack_rows(packed, shapes):
    out, row = [], 0
    for shp in shapes:
        n = math.prod(shp)
        nrow = -(-n // D_MODEL)
        out.append(packed[row:row + nrow].reshape(-1)[:n].reshape(shp))
        row += nrow
    return out


def _in_proj_rows(w_t):
    return jnp.concatenate([w_t[0:2560], w_t[2576:2960], w_t[2960:3216], w_t[2560:2576], w_t[3216:3248],
                            jnp.zeros((D_IN_PAD - D_IN, D_MODEL), w_t.dtype)], axis=0)


def _in_proj_rows_inv(d):
    return jnp.concatenate([d[0:2560], d[3200:3216], d[2560:2944], d[2944:3200], d[3216:3248]], axis=0)


def _rope_tables(positions):
    inv_freq = ROPE_THETA ** (-jnp.arange(0, QK_ROPE, 2, dtype=F32) / QK_ROPE)
    ang = positions[..., None].astype(F32) * inv_freq
    cos, sin = jnp.cos(ang), jnp.sin(ang)
    one = jnp.ones(ang.shape[:2] + (QK_NOPE,), F32)
    zero = jnp.zeros_like(one)
    z16, z32, o32 = zero[..., :16], zero[..., :32], one[..., :32]
    cc = jnp.concatenate([one, cos, cos, o32], axis=-1)
    sp = jnp.concatenate([zero, z16, sin, z32], axis=-1)
    sm = jnp.concatenate([zero, -sin, z16, z32], axis=-1)
    return cc, sp, sm


def weight_views(g):
    full = lambda name: _seg(g, name).reshape(-1, D_MODEL)
    ukv = _seg(g, "w_ukv").reshape(MLA_HEADS, QK_NOPE + V_HEAD, KV_LORA)
    wukv_t = jnp.concatenate([jnp.pad(ukv[:, :QK_NOPE], ((0, 0), (0, HEAD_PAD - QK_NOPE), (0, 0))).reshape(-1, KV_LORA),
                              ukv[:, QK_NOPE:].reshape(-1, KV_LORA)], axis=0)
    uq = _seg(g, "w_uq").reshape(MLA_HEADS, QK_DIM, Q_LORA)
    wuq_t = jnp.pad(uq, ((0, 0), (0, HEAD_PAD - QK_DIM), (0, 0))).reshape(-1, Q_LORA)
    return dict(wg1_t=full("ffn1_w_gate"), wu1_t=full("ffn1_w_up"), wd1=full("ffn1_w_down"),
                wg2_t=full("ffn2_w_gate"), wu2_t=full("ffn2_w_up"), wd2=full("ffn2_w_down"),
                wo=full("w_out"), win_t=_in_proj_rows(full("w_in")), wukv_t=wukv_t, wuq_t=wuq_t)


def _ffn_bwd(tag, dxn, x, h, gg, uu, a, o, gate, sc, norm_w, wg_t, wu_t, wd):
    f2 = wd.shape[0] // 2
    do, dgate = gate_bwd(dxn, o, gate, 0.5, tag + "_gate_bwd")
    dgg, duu = ffn_dact(do, wd, gg, uu, tag + "_dact")
    dwd = mm_tn(a, do, f2, D_MODEL, tag + "_dwd")
    dwg_t = mm_tn(dgg, h, f2, D_MODEL, tag + "_dwg")
    dwu_t = mm_tn(duu, h, f2, D_MODEL, tag + "_dwu")
    dx, dsc, dsh, dnw = dh_norm_bwd([dgg, duu], [wg_t, wu_t], x, dxn, norm_w, sc, tag + "_dh")
    return dx, (dsh, dsc, dgate), dnw, (dwg_t, dwu_t, dwd)


def local_step(x, tgt, positions, mod, wv, p):
    nb, s, d = x.shape
    sh1, sc1, g1, sh2, sc2, g2, sh3, sc3, g3 = mod
    cc, sp, sm = _rope_tables(positions)
    lane_head = jnp.arange(D_SSD, dtype=I32)[None, :] // SSD_HEAD_DIM
    e_mat = (lane_head == jnp.arange(LANES, dtype=I32)[:, None]).astype(F32)
    et_mat = e_mat.T
    rr, cl = jnp.arange(LANES, dtype=I32)[:, None], jnp.arange(LANES, dtype=I32)[None, :]
    place = ((cl == rr + (QK_NOPE - SSD_HEADS)) & (rr >= SSD_HEADS) & (rr < SSD_HEADS + QK_ROPE)).astype(F32)
    dtb = jnp.pad(p["dt_bias"], ((0, 0), (0, LANES - SSD_HEADS)))
    alog = jnp.pad(p["a_log"], ((0, 0), (0, LANES - SSD_HEADS)))
    dskip_e = jnp.repeat(p["d_skip"], SSD_HEAD_DIM, axis=1)

    h1 = norm_mod(x, p["norm_ffn1"], sc1, sh1, "ffn1_norm")
    gg1, uu1, a1 = ffn_up(h1, wv["wg1_t"], wv["wu1_t"], "ffn1_up")
    x1, o1 = ffn_down(a1, wv["wd1"], x, g1, 0.5, "ffn1_down")
    h2 = norm_mod(x1, p["norm_mix"], sc2, sh2, "mix_norm")
    z, u, cq, ckv, misc = in_proj(h2, wv["win_t"], "in_proj")
    xs, bm, cm_ = conv_fwd(u, p["conv_w"], p["conv_b"], "conv_fwd")
    ys, y, prev = ssd_fwd(xs, bm, cm_, misc, z, dtb, alog, dskip_e, p["ssd_norm_w"], e_mat, "ssd_fwd")
    q, k, v, qn, kvn = qkv_fwd(cq, ckv, misc, cc, sp, sm, p["q_norm_w"], p["kv_norm_w"], wv["wuq_t"], wv["wukv_t"],
                               place, "qkv_fwd")
    attn, lse = flash_fwd(q, k, v, "flash_fwd")
    x2, o2, ym = out_proj(ys, attn, p["mla_norm_w"], wv["wo"], x1, g2, "out_proj")
    h3 = norm_mod(x2, p["norm_ffn2"], sc3, sh3, "ffn2_norm")
    gg3, uu3, a3 = ffn_up(h3, wv["wg2_t"], wv["wu2_t"], "ffn2_up")
    x3, o3 = ffn_down(a3, wv["wd2"], x2, g3, 0.5, "ffn2_down")
    loss, dx3, dnfin = final_loss(x3, p["norm_final"], tgt, "final_loss")

    dx2, dmod3, dnf2, (dwg2, dwu2, dwd2) = _ffn_bwd("ffn2", dx3, x2, h3, gg3, uu3, a3, o3, g3, sc3, p["norm_ffn2"],
                                                   wv["wg2_t"], wv["wu2_t"], wv["wd2"])
    dout, dg2 = gate_bwd(dx2, o2, g2, 1.0, "mix_gate_bwd")
    dys, dattn, dlt, dmlan = out_proj_bwd(dout, attn, p["mla_norm_w"], wv["wo"], "out_proj_bwd")
    dwo = jnp.concatenate([mm_tn(ys, dout, D_SSD, D_MODEL, "dwo_ssd"), mm_tn(ym, dout, D_SSD, D_MODEL, "dwo_mla")], axis=0)
    dxs, dbm, dcm, dz, ddt, dssdn, ddsk_lane, ddtb, dalog = ssd_bwd(
        dys, y, z, xs, bm, cm_, misc, prev, dtb, alog, dskip_e, p["ssd_norm_w"], e_mat, et_mat, "ssd_bwd")
    dq, dk, dv = flash_bwd(q, k, v, dattn, lse, dlt, "flash_bwd")
    dcq, dckv, dmisc, dqp, dkvc, dqn, dkvn = qkv_bwd(dq, dk, dv, ddt, cq, ckv, cc, sp, sm, p["q_norm_w"], p["kv_norm_w"],
                                                     wv["wuq_t"], wv["wukv_t"], place.T, "qkv_bwd")
    dwuq = mm_tn(dqp, qn, MLA_HEADS * HEAD_PAD, Q_LORA, "dwuq")
    dwukv = mm_tn(dkvc, kvn, MLA_HEADS * HEAD_PAD, KV_LORA, "dwukv")
    dvv, dconv = conv_bwd_a(dxs, dbm, dcm, u, p["conv_w"], p["conv_b"], "conv_bwd_a")
    du = conv_bwd_b(dvv, p["conv_w"], "conv_bwd_b")
    dproj = jnp.concatenate([dz, du, dcq, dckv, dmisc], axis=-1)
    dwin = mm_tn(dproj, h2, D_IN_PAD // 2, D_MODEL, "dwin")
    dx1, dsc2, dsh2, dnmix = dh_norm_bwd([dproj], [wv["win_t"]], x1, dx2, p["norm_mix"], sc2, "mix_dh")
    dx0, dmod1, dnf1, (dwg1, dwu1, dwd1) = _ffn_bwd("ffn1", dx1, x, h1, gg1, uu1, a1, o1, g1, sc1, p["norm_ffn1"],
                                                   wv["wg1_t"], wv["wu1_t"], wv["wd1"])

    dmod = jnp.concatenate([*dmod1, dsh2, dsc2, dg2, *dmod3], axis=1).reshape(nb, N_MOD * d)
    return dict(
        loss=loss, dx=dx0, dmod=dmod, norm_ffn1=dnf1, norm_mix=dnmix, norm_ffn2=dnf2, norm_final=dnfin,
        ssd_norm_w=dssdn, mla_norm_w=dmlan, q_norm_w=dqn, kv_norm_w=dkvn,
        dt_bias=ddtb[:, :SSD_HEADS], a_log=dalog[:, :SSD_HEADS],
        d_skip=squeeze_heads(ddsk_lane, et_mat, "d_skip_heads")[:, :SSD_HEADS],
        conv_b=dconv[4:5], conv_w=dconv[0:4],
        gw=dict(ffn1_w_gate=dwg1, ffn1_w_up=dwu1, ffn1_w_down=dwd1, ffn2_w_gate=dwg2, ffn2_w_up=dwu2, ffn2_w_down=dwd2,
                w_out=dwo, w_in=dwin, w_ukv=dwukv, w_uq=dwuq))


def kernel(x, c, positions, w_ada, b_ada, norm_ffn1, ffn1_w_gate, ffn1_w_up, ffn1_w_down, norm_mix, w_in, conv_w, conv_b, dt_bias, a_log, d_skip, ssd_norm_w, q_norm_w, w_uq, kv_norm_w, w_ukv, mla_norm_w, w_out, norm_ffn2, ffn2_w_gate, ffn2_w_up, ffn2_w_down, norm_final, loss_target, m_w_ada, m_b_ada, m_norm_ffn1, m_ffn1_w_gate, m_ffn1_w_up, m_ffn1_w_down, m_norm_mix, m_w_in, m_conv_w, m_conv_b, m_dt_bias, m_a_log, m_d_skip, m_ssd_norm_w, m_q_norm_w, m_w_uq, m_kv_norm_w, m_w_ukv, m_mla_norm_w, m_w_out, m_norm_ffn2, m_ffn2_w_gate, m_ffn2_w_up, m_ffn2_w_down, m_norm_final, v_w_ada, v_b_ada, v_norm_ffn1, v_ffn1_w_gate, v_ffn1_w_up, v_ffn1_w_down, v_norm_mix, v_w_in, v_conv_w, v_conv_b, v_dt_bias, v_a_log, v_d_skip, v_ssd_norm_w, v_q_norm_w, v_w_uq, v_kv_norm_w, v_w_ukv, v_mla_norm_w, v_w_out, v_norm_ffn2, v_ffn2_w_gate, v_ffn2_w_up, v_ffn2_w_down, v_norm_final):
    names = ["w_ada", "b_ada", "norm_ffn1", "ffn1_w_gate", "ffn1_w_up", "ffn1_w_down", "norm_mix", "w_in", "conv_w",
             "conv_b", "dt_bias", "a_log", "d_skip", "ssd_norm_w", "q_norm_w", "w_uq", "kv_norm_w", "w_ukv",
             "mla_norm_w", "w_out", "norm_ffn2", "ffn2_w_gate", "ffn2_w_up", "ffn2_w_down", "norm_final"]
    W = dict(zip(names, (w_ada, b_ada, norm_ffn1, ffn1_w_gate, ffn1_w_up, ffn1_w_down, norm_mix, w_in, conv_w, conv_b, dt_bias, a_log, d_skip, ssd_norm_w, q_norm_w, w_uq, kv_norm_w, w_ukv, mla_norm_w, w_out, norm_ffn2, ffn2_w_gate, ffn2_w_up, ffn2_w_down, norm_final)))
    M = dict(zip(names, (m_w_ada, m_b_ada, m_norm_ffn1, m_ffn1_w_gate, m_ffn1_w_up, m_ffn1_w_down, m_norm_mix, m_w_in, m_conv_w, m_conv_b, m_dt_bias, m_a_log, m_d_skip, m_ssd_norm_w, m_q_norm_w, m_w_uq, m_kv_norm_w, m_w_ukv, m_mla_norm_w, m_w_out, m_norm_ffn2, m_ffn2_w_gate, m_ffn2_w_up, m_ffn2_w_down, m_norm_final)))
    V = dict(zip(names, (v_w_ada, v_b_ada, v_norm_ffn1, v_ffn1_w_gate, v_ffn1_w_up, v_ffn1_w_down, v_norm_mix, v_w_in, v_conv_w, v_conv_b, v_dt_bias, v_a_log, v_d_skip, v_ssd_norm_w, v_q_norm_w, v_w_uq, v_kv_norm_w, v_w_ukv, v_mla_norm_w, v_w_out, v_norm_ffn2, v_ffn2_w_gate, v_ffn2_w_up, v_ffn2_w_down, v_norm_final)))

    nb, s, d = x.shape
    me = 4 * lax.axis_index("x") + 2 * lax.axis_index("y") + lax.axis_index("c")
    n_ada = w_ada.shape[2]

    cshape = [(nb, d), conv_w.shape[1:]]
    cg = all_gather8(_pack_rows([c, conv_w[0]]), "gather_c")
    c_all = jnp.stack([_unpack_rows(cg[k], cshape)[0] for k in range(N_DEV)]).reshape(N_DEV * nb, d)
    conv_w_full = jnp.concatenate([_unpack_rows(cg[k], cshape)[1] for k in range(N_DEV)], axis=1)
    wv = weight_views(all_gather8(_pack_shards(W, BF16), "gather_weights"))

    b_ada_cols = lax.dynamic_slice(b_ada, (0, me * n_ada), (1, n_ada))
    mod_cols, c_act = adaln_fwd(c_all, w_ada[0], b_ada_cols, "adaln_fwd")
    mod_g = all_gather8(mod_cols, "gather_mod")
    mod = lax.dynamic_slice(mod_g, (0, me * nb, 0), (N_DEV, nb, n_ada)).transpose(1, 0, 2).reshape(nb, N_MOD, 1, d)
    mod = [mod[:, k] for k in range(N_MOD)]

    P = dict(W)
    P["conv_w"] = conv_w_full
    P["norm_final"] = norm_final.reshape(1, d)
    R = local_step(x, loss_target, positions, mod, wv, P)

    dmod = R["dmod"]
    partial_shapes = [(1,), (1, d), (1, d), (1, d), (1, d), (1, d), (1, d), (1, Q_LORA), (1, KV_LORA),
                      (1, SSD_HEADS), (1, SSD_HEADS), (1, SSD_HEADS), (1, D_CONV), (4, D_CONV), (1, N_MOD * d),
                      (nb, N_MOD * d)]
    partial = _pack_rows([R["loss"][0, :1], R["norm_ffn1"], R["norm_mix"], R["norm_ffn2"], R["norm_final"],
                          R["ssd_norm_w"], R["mla_norm_w"], R["q_norm_w"], R["kv_norm_w"],
                          R["dt_bias"], R["a_log"], R["d_skip"], R["conv_b"], R["conv_w"],
                          sum_rows(dmod, "dmod_rows"), dmod])
    partial_g = all_gather8(partial, "gather_partials")
    (loss, g_nf1, g_nmix, g_nf2, g_nfin, g_ssdn, g_mlan, g_qn, g_kvn, g_dtb, g_alog, g_dskip, g_convb, g_convw,
     g_bada, _) = _unpack_rows(sum_blocks(partial_g, "sum_partials"), partial_shapes)
    dmod_all = jnp.stack([_unpack_rows(partial_g[k], partial_shapes)[-1] for k in range(N_DEV)]).reshape(N_DEV * nb, -1)
    g_wada = adaln_bwd(c_act, lax.dynamic_slice(dmod_all, (0, me * n_ada), (N_DEV * nb, n_ada)), "adaln_bwd")
    n_cw = conv_w.shape[2]
    G = {"w_ada": g_wada[None], "b_ada": g_bada, "norm_ffn1": g_nf1, "norm_mix": g_nmix, "norm_ffn2": g_nf2,
         "norm_final": g_nfin.reshape(d), "ssd_norm_w": g_ssdn, "mla_norm_w": g_mlan, "q_norm_w": g_qn,
         "kv_norm_w": g_kvn, "dt_bias": g_dtb, "a_log": g_alog, "d_skip": g_dskip, "conv_b": g_convb,
         "conv_w": lax.dynamic_slice(g_convw, (0, me * n_cw), (4, n_cw))[None]}

    gw = R["gw"]
    gp = [gw[name].reshape(N_DEV, -1, D_MODEL) for name in ("ffn1_w_gate", "ffn1_w_up", "ffn1_w_down", "ffn2_w_gate",
                                                             "ffn2_w_up", "ffn2_w_down", "w_out")]
    gp.append(jnp.pad(_in_proj_rows_inv(gw["w_in"]).reshape(N_DEV, -1, D_MODEL), ((0, 0), (0, 10), (0, 0))))
    dkv_ = gw["w_ukv"]
    hd = MLA_HEADS * HEAD_PAD
    gp.append(jnp.concatenate([dkv_[:hd].reshape(MLA_HEADS, HEAD_PAD, KV_LORA)[:, :QK_NOPE],
                               dkv_[hd:].reshape(MLA_HEADS, V_HEAD, KV_LORA)], axis=1).reshape(N_DEV, -1, D_MODEL))
    gp.append(jnp.pad(gw["w_uq"].reshape(MLA_HEADS, HEAD_PAD, Q_LORA)[:, :QK_DIM].reshape(N_DEV, -1, D_MODEL),
                      ((0, 0), (0, 12), (0, 0))))
    recv = all_to_all8(jnp.concatenate(gp, axis=1).astype(BF16), "exchange_grads")

    DW, NM, NV = {}, {}, {}
    big = adamw_sum8(_pack_shards(W, F32), recv, _pack_shards(M, F32), _pack_shards(V, F32), "adamw_matrices")
    for name, _, _ in PACK:
        o, r = PACK_OFF[name]
        G[name], DW[name], NM[name], NV[name] = [_rows_to_shard(name, t[o:o + r], W[name]) for t in big]
    dwa, nma, nva = adamw(w_ada[0], g_wada, m_w_ada[0], v_w_ada[0], "adamw_w_ada")
    DW["w_ada"], NM["w_ada"], NV["w_ada"] = dwa[None], nma[None], nva[None]
    small = [n for n in names if n not in DW]
    shapes = [W[n].shape for n in small]
    outs = adamw(_pack_rows([W[n] for n in small]), _pack_rows([G[n] for n in small]),
                 _pack_rows([M[n] for n in small]), _pack_rows([V[n] for n in small]), "adamw_small")
    for res, dst in zip(outs, (DW, NM, NV)):
        for n, t in zip(small, _unpack_rows(res, shapes)):
            dst[n] = t
    return (loss.reshape(()), R["dx"], *[G[n] for n in names], *[DW[n] for n in names], *[NM[n] for n in names],
            *[NV[n] for n in names])
```

```python
import math

import jax
import jax.numpy as jnp
from jax import lax
from jax.experimental import pallas as pl
from jax.experimental.pallas import tpu as pltpu

F32, BF16, I32 = jnp.float32, jnp.bfloat16, jnp.int32
HI = lax.Precision.HIGHEST
SDS = jax.ShapeDtypeStruct
MESH = pl.DeviceIdType.MESH

D_MODEL = 1024
D_FF = 2816
D_SSD = 1024
SSD_HEADS = 16
SSD_HEAD_DIM = 64
SSD_GROUPS = 2
SSD_STATE = 128
CHUNK = 128
MLA_HEADS = 8
QK_NOPE = 64
QK_ROPE = 32
QK_DIM = 96
V_HEAD = 128
Q_LORA = 384
KV_LORA = 256
ROPE_THETA = 10000.0
N_MOD = 9
EPS = 1e-6
D_CONV = 1536
D_IN = 3248
D_IN_PAD = 3328
HEAD_PAD = 128
N_DEV = 8
ADAM_LR, ADAM_B1, ADAM_B2, ADAM_EPS, ADAM_WD, ADAM_STEP = 0.001, 0.9, 0.999, 1e-08, 0.01, 10

VMEM_LIMIT = 56 * 1024 * 1024
LANES = 128
NT_DIMS = (((1,), (1,)), ((), ()))
TN_DIMS = (((0,), (0,)), ((), ()))


def _cparams(n_axes):
    return pltpu.CompilerParams(dimension_semantics=("arbitrary",) * n_axes, vmem_limit_bytes=VMEM_LIMIT)


def _row(tm, d):
    return pl.BlockSpec((None, tm, d), lambda b, i: (b, i, 0))


def _bvec(d):
    return pl.BlockSpec((None, 1, d), lambda b, i: (b, 0, 0))


def _full(shape):
    n = len(shape)
    return pl.BlockSpec(shape, lambda *_: (0,) * n)


def _sigmoid(x):
    return 1.0 / (1.0 + jnp.exp(-x))


def _softplus(x):
    return jnp.maximum(x, 0.0) + jnp.log(1.0 + jnp.exp(-jnp.abs(x)))


def _rms(x):
    return lax.rsqrt(jnp.mean(x * x, axis=-1, keepdims=True) + EPS)


def _rms_bwd(dn, n, r):
    return r * (dn - n * jnp.mean(dn * n, axis=-1, keepdims=True))


def _first_step():
    return (pl.program_id(0) == 0) & (pl.program_id(1) == 0)


def all_gather8(x, name):
    r, c = x.shape

    def body(x_ref, out_ref, send_sems, recv_sems, local_sem):
        mx, my, mc = lax.axis_index("x"), lax.axis_index("y"), lax.axis_index("c")
        me, sibling = (mx, my, mc), (mx, my, 1 - mc)
        chips = [(1 - mx, my), (mx, 1 - my), (1 - mx, 1 - my)]

        def rows(px, py, pc):
            return out_ref.at[4 * px + 2 * py + pc]

        def copy(k, block, to, src=None):
            return pltpu.make_async_remote_copy(
                src_ref=rows(*block) if src is None else src, dst_ref=rows(*block),
                send_sem=send_sems.at[k], recv_sem=recv_sems.at[k], device_id=to, device_id_type=MESH)

        mine = pltpu.make_async_copy(x_ref, rows(*me), local_sem)
        mine.start()
        first = [copy(0, me, sibling, src=x_ref)]
        first += [copy(1 + j, me, (*chip, mc), src=x_ref) for j, chip in enumerate(chips)]
        for cp in first:
            cp.start()
        passed = [copy(4 + j, (*chip, mc), sibling) for j, chip in enumerate(chips)]
        for j, chip in enumerate(chips):
            copy(1 + j, (*chip, mc), me).wait_recv()
            passed[j].start()
        copy(0, sibling, me).wait_recv()
        for j, chip in enumerate(chips):
            copy(4 + j, (*chip, 1 - mc), me).wait_recv()
        for cp in first + passed:
            cp.wait_send()
        mine.wait()

    return pl.pallas_call(
        body, name=name,
        out_shape=SDS((N_DEV, r, c), x.dtype),
        in_specs=[pl.BlockSpec(memory_space=pl.ANY)],
        out_specs=pl.BlockSpec(memory_space=pl.ANY),
        scratch_shapes=[pltpu.SemaphoreType.DMA((7,)), pltpu.SemaphoreType.DMA((7,)), pltpu.SemaphoreType.DMA],
    )(x)


def all_to_all8(x, name):
    _, r, c = x.shape

    def body(x_ref, out_ref, send_sems, recv_sems, local_sem):
        mx, my, mc = lax.axis_index("x"), lax.axis_index("y"), lax.axis_index("c")
        me = 4 * mx + 2 * my + mc
        mine = pltpu.make_async_copy(x_ref.at[me], out_ref.at[me], local_sem)
        mine.start()
        copies = []
        for rel in range(1, N_DEV):
            px = 1 - mx if rel & 4 else mx
            py = 1 - my if rel & 2 else my
            pc = 1 - mc if rel & 1 else mc
            cp = pltpu.make_async_remote_copy(
                src_ref=x_ref.at[4 * px + 2 * py + pc], dst_ref=out_ref.at[me],
                send_sem=send_sems.at[rel - 1], recv_sem=recv_sems.at[rel - 1],
                device_id=(px, py, pc), device_id_type=MESH)
            cp.start()
            copies.append(cp)
        for cp in copies:
            cp.wait()
        mine.wait()

    return pl.pallas_call(
        body, name=name,
        out_shape=SDS((N_DEV, r, c), x.dtype),
        in_specs=[pl.BlockSpec(memory_space=pl.ANY)],
        out_specs=pl.BlockSpec(memory_space=pl.ANY),
        scratch_shapes=[pltpu.SemaphoreType.DMA((7,)), pltpu.SemaphoreType.DMA((7,)), pltpu.SemaphoreType.DMA],
    )(x)


def norm_mod(x, w, sc, sh, name):
    b, s, d = x.shape
    tm = min(512, s)

    def body(x_ref, w_ref, sc_ref, sh_ref, h_ref):
        xv = x_ref[...]
        n = xv * _rms(xv)
        h_ref[...] = ((n * w_ref[...]) * (1.0 + sc_ref[...]) + sh_ref[...]).astype(BF16)

    return pl.pallas_call(
        body, name=name, grid=(b, s // tm),
        in_specs=[_row(tm, d), _full((1, d)), _bvec(d), _bvec(d)],
        out_specs=_row(tm, d), out_shape=SDS((b, s, d), BF16), compiler_params=_cparams(2))(x, w, sc, sh)


def ffn_up(h, wg_t, wu_t, name):
    b, s, d = h.shape
    f = wg_t.shape[0]
    tm, tn = min(512, s), f // 2

    def body(h_ref, wg_ref, wu_ref, g_ref, u_ref, a_ref):
        hv = h_ref[...]
        g = lax.dot_general(hv, wg_ref[...], NT_DIMS, preferred_element_type=F32)
        u = lax.dot_general(hv, wu_ref[...], NT_DIMS, preferred_element_type=F32)
        g_ref[...] = g
        u_ref[...] = u
        a_ref[...] = (g * _sigmoid(g) * u).astype(BF16)

    hs = pl.BlockSpec((None, tm, d), lambda j, bb, i: (bb, i, 0))
    ws = pl.BlockSpec((tn, d), lambda j, bb, i: (j, 0))
    os_ = pl.BlockSpec((None, tm, tn), lambda j, bb, i: (bb, i, j))
    return pl.pallas_call(
        body, name=name, grid=(f // tn, b, s // tm),
        in_specs=[hs, ws, ws], out_specs=[os_, os_, os_],
        out_shape=[SDS((b, s, f), F32), SDS((b, s, f), F32), SDS((b, s, f), BF16)],
        compiler_params=_cparams(3))(h, wg_t, wu_t)


def ffn_down(a, wd, x, gate, scale, name):
    b, s, f = a.shape
    d = wd.shape[1]
    tm = min(512, s)

    def body(a_ref, wd_ref, x_ref, g_ref, xn_ref, o_ref):
        o = jnp.dot(a_ref[...], wd_ref[...], preferred_element_type=F32)
        xn_ref[...] = x_ref[...] + (scale * g_ref[...]) * o
        o_ref[...] = o.astype(BF16)

    return pl.pallas_call(
        body, name=name, grid=(b, s // tm),
        in_specs=[_row(tm, f), _full((f, d)), _row(tm, d), _bvec(d)],
        out_specs=[_row(tm, d), _row(tm, d)],
        out_shape=[SDS((b, s, d), F32), SDS((b, s, d), BF16)], compiler_params=_cparams(2))(a, wd, x, gate)


def gate_bwd(dxn, o, gate, scale, name):
    b, s, d = dxn.shape
    tm = min(512, s)

    def body(dx_ref, o_ref, g_ref, do_ref, dg_ref):
        @pl.when(pl.program_id(1) == 0)
        def _():
            dg_ref[...] = jnp.zeros_like(dg_ref)
        dx = dx_ref[...]
        do_ref[...] = ((scale * g_ref[...]) * dx).astype(BF16)
        dg_ref[...] += jnp.sum(scale * dx * o_ref[...].astype(F32), axis=0, keepdims=True)

    return pl.pallas_call(
        body, name=name, grid=(b, s // tm),
        in_specs=[_row(tm, d), _row(tm, d), _bvec(d)],
        out_specs=[_row(tm, d), _bvec(d)],
        out_shape=[SDS((b, s, d), BF16), SDS((b, 1, d), F32)], compiler_params=_cparams(2))(dxn, o, gate)


def ffn_dact(do, wd, g, u, name):
    b, s, d = do.shape
    f = wd.shape[0]
    tm, tn = min(512, s), f // 2

    def body(do_ref, wd_ref, g_ref, u_ref, dg_ref, du_ref):
        da = lax.dot_general(do_ref[...], wd_ref[...], NT_DIMS, preferred_element_type=F32)
        gv = g_ref[...]
        sg = _sigmoid(gv)
        dg_ref[...] = (da * u_ref[...] * (sg * (1.0 + gv * (1.0 - sg)))).astype(BF16)
        du_ref[...] = (da * (gv * sg)).astype(BF16)

    dos = pl.BlockSpec((None, tm, d), lambda j, bb, i: (bb, i, 0))
    ws = pl.BlockSpec((tn, d), lambda j, bb, i: (j, 0))
    es = pl.BlockSpec((None, tm, tn), lambda j, bb, i: (bb, i, j))
    return pl.pallas_call(
        body, name=name, grid=(f // tn, b, s // tm),
        in_specs=[dos, ws, es, es], out_specs=[es, es],
        out_shape=[SDS((b, s, f), BF16), SDS((b, s, f), BF16)], compiler_params=_cparams(3))(do, wd, g, u)


def mm_tn(a, bm, tma, tnb, name):
    b, s, ka = a.shape
    nb = bm.shape[2]
    tk = min(512, s)

    def body(a_ref, b_ref, o_ref):
        @pl.when((pl.program_id(2) == 0) & (pl.program_id(3) == 0))
        def _():
            o_ref[...] = jnp.zeros_like(o_ref)
        o_ref[...] += lax.dot_general(a_ref[...], b_ref[...], TN_DIMS, preferred_element_type=F32)

    return pl.pallas_call(
        body, name=name, grid=(ka // tma, nb // tnb, b, s // tk),
        in_specs=[pl.BlockSpec((None, tk, tma), lambda i, j, bb, k: (bb, k, i)),
                  pl.BlockSpec((None, tk, tnb), lambda i, j, bb, k: (bb, k, j))],
        out_specs=pl.BlockSpec((tma, tnb), lambda i, j, bb, k: (i, j)),
        out_shape=SDS((ka, nb), F32), compiler_params=_cparams(4))(a, bm)


def dh_norm_bwd(dys, wts, x, dxn, w, sc, name):
    b, s, d = x.shape
    tm = min(256, s)
    n_in = len(dys)

    def body(*refs):
        dy_refs, w_refs = refs[:n_in], refs[n_in:2 * n_in]
        x_ref, dxn_ref, nw_ref, sc_ref, dx_ref, dsc_ref, dsh_ref, dw_ref = refs[2 * n_in:]

        @pl.when(pl.program_id(1) == 0)
        def _():
            dsc_ref[...] = jnp.zeros_like(dsc_ref)
            dsh_ref[...] = jnp.zeros_like(dsh_ref)

        @pl.when(_first_step())
        def _():
            dw_ref[...] = jnp.zeros_like(dw_ref)

        dh = jnp.dot(dy_refs[0][...], w_refs[0][...], preferred_element_type=F32)
        for k in range(1, n_in):
            dh += jnp.dot(dy_refs[k][...], w_refs[k][...], preferred_element_type=F32)
        xv = x_ref[...]
        r = _rms(xv)
        n = xv * r
        nw = nw_ref[...]
        dsc_ref[...] += jnp.sum(dh * (n * nw), axis=0, keepdims=True)
        dsh_ref[...] += jnp.sum(dh, axis=0, keepdims=True)
        dhn = dh * (1.0 + sc_ref[...])
        dw_ref[...] += jnp.sum(dhn * n, axis=0, keepdims=True)
        dx_ref[...] = dxn_ref[...] + _rms_bwd(dhn * nw, n, r)

    in_specs = [_row(tm, dy.shape[2]) for dy in dys] + [_full(wt.shape) for wt in wts]
    in_specs += [_row(tm, d), _row(tm, d), _full((1, d)), _bvec(d)]
    return pl.pallas_call(
        body, name=name, grid=(b, s // tm), in_specs=in_specs,
        out_specs=[_row(tm, d), _bvec(d), _bvec(d), _full((1, d))],
        out_shape=[SDS((b, s, d), F32), SDS((b, 1, d), F32), SDS((b, 1, d), F32), SDS((1, d), F32)],
        compiler_params=_cparams(2))(*dys, *wts, x, dxn, w, sc)


def final_loss(x, w, tgt, name):
    b, s, d = x.shape
    tm = min(512, s)

    def body(x_ref, w_ref, t_ref, loss_ref, dx_ref, dw_ref):
        @pl.when(_first_step())
        def _():
            loss_ref[...] = jnp.zeros_like(loss_ref)
            dw_ref[...] = jnp.zeros_like(dw_ref)
        xv = x_ref[...]
        r = _rms(xv)
        n = xv * r
        wv = w_ref[...]
        e = n * wv - t_ref[...]
        loss_ref[...] += jnp.sum(e * e) * (0.5 / d)
        dy = e * (1.0 / d)
        dw_ref[...] += jnp.sum(dy * n, axis=0, keepdims=True)
        dx_ref[...] = _rms_bwd(dy * wv, n, r)

    return pl.pallas_call(
        body, name=name, grid=(b, s // tm),
        in_specs=[_row(tm, d), _full((1, d)), _row(tm, d)],
        out_specs=[_full((1, LANES)), _row(tm, d), _full((1, d))],
        out_shape=[SDS((1, LANES), F32), SDS((b, s, d), F32), SDS((1, d), F32)],
        compiler_params=_cparams(2))(x, w, tgt)


def in_proj(h, win_t, name):
    b, s, d = h.shape
    tm = min(256, s)
    widths = (D_SSD, D_SSD + 2 * SSD_GROUPS * SSD_STATE, Q_LORA, KV_LORA, LANES)

    def body(h_ref, w_ref, *outs):
        p = lax.dot_general(h_ref[...], w_ref[...], NT_DIMS, preferred_element_type=F32)
        off = 0
        for o_ref, wd in zip(outs, widths):
            o_ref[...] = p[:, off:off + wd]
            off += wd

    return pl.pallas_call(
        body, name=name, grid=(b, s // tm),
        in_specs=[_row(tm, d), _full(win_t.shape)],
        out_specs=[_row(tm, wd) for wd in widths],
        out_shape=[SDS((b, s, wd), F32) for wd in widths], compiler_params=_cparams(2))(h, win_t)


def _halo_prev(ts, d):
    return pl.BlockSpec((None, 8, d), lambda b, i: (b, jnp.maximum(i * (ts // 8) - 1, 0), 0))


def _conv_taps(ext_ref, w_ref, ts):
    return [ext_ref[5 + k:5 + k + ts, :] for k in range(4)], [w_ref[k:k + 1, :] for k in range(4)]


def conv_fwd(u, cw, cb, name):
    b, s, dc = u.shape
    ts = min(512, s)
    widths = (D_SSD, SSD_GROUPS * SSD_STATE, SSD_GROUPS * SSD_STATE)

    def body(u_ref, up_ref, w_ref, b_ref, xs_ref, bm_ref, cm_ref, ext):
        ext[0:8, :] = jnp.where(pl.program_id(1) > 0, up_ref[...], 0.0)
        ext[8:8 + ts, :] = u_ref[...]
        taps, ws = _conv_taps(ext, w_ref, ts)
        v = b_ref[...] + taps[0] * ws[0] + taps[1] * ws[1] + taps[2] * ws[2] + taps[3] * ws[3]
        y = v * _sigmoid(v)
        xs_ref[...] = y[:, 0:D_SSD]
        bm_ref[...] = y[:, D_SSD:D_SSD + 256]
        cm_ref[...] = y[:, D_SSD + 256:D_SSD + 512]

    return pl.pallas_call(
        body, name=name, grid=(b, s // ts),
        in_specs=[_row(ts, dc), _halo_prev(ts, dc), _full((4, dc)), _full((1, dc))],
        out_specs=[_row(ts, wd) for wd in widths],
        out_shape=[SDS((b, s, wd), F32) for wd in widths],
        scratch_shapes=[pltpu.VMEM((ts + 8, dc), F32)], compiler_params=_cparams(2))(u, u, cw, cb)


def conv_bwd_a(dxs, dbm, dcm, u, cw, cb, name):
    b, s, dc = u.shape
    ts = min(512, s)

    def body(dxs_ref, dbm_ref, dcm_ref, u_ref, up_ref, w_ref, b_ref, dv_ref, dwb_ref, ext):
        @pl.when(_first_step())
        def _():
            dwb_ref[...] = jnp.zeros_like(dwb_ref)
        ext[0:8, :] = jnp.where(pl.program_id(1) > 0, up_ref[...], 0.0)
        ext[8:8 + ts, :] = u_ref[...]
        taps, ws = _conv_taps(ext, w_ref, ts)
        v = b_ref[...] + taps[0] * ws[0] + taps[1] * ws[1] + taps[2] * ws[2] + taps[3] * ws[3]
        sg = _sigmoid(v)
        dy = jnp.concatenate([dxs_ref[...], dbm_ref[...], dcm_ref[...]], axis=1)
        dv = dy * (sg * (1.0 + v * (1.0 - sg)))
        dv_ref[...] = dv
        for k in range(4):
            dwb_ref[k:k + 1, :] += jnp.sum(dv * taps[k], axis=0, keepdims=True)
        dwb_ref[4:5, :] += jnp.sum(dv, axis=0, keepdims=True)

    return pl.pallas_call(
        body, name=name, grid=(b, s // ts),
        in_specs=[_row(ts, D_SSD), _row(ts, 256), _row(ts, 256), _row(ts, dc), _halo_prev(ts, dc),
                  _full((4, dc)), _full((1, dc))],
        out_specs=[_row(ts, dc), _full((8, dc))],
        out_shape=[SDS((b, s, dc), F32), SDS((8, dc), F32)],
        scratch_shapes=[pltpu.VMEM((ts + 8, dc), F32)], compiler_params=_cparams(2))(dxs, dbm, dcm, u, u, cw, cb)


def conv_bwd_b(dv, cw, name):
    b, s, dc = dv.shape
    ts = min(512, s)
    nt = s // ts

    def body(dv_ref, dn_ref, w_ref, du_ref, ext):
        ext[0:ts, :] = dv_ref[...]
        ext[ts:ts + 8, :] = jnp.where(pl.program_id(1) < nt - 1, dn_ref[...], 0.0)
        acc = ext[3:3 + ts, :] * w_ref[0:1, :]
        for k in range(1, 4):
            acc += ext[3 - k:3 - k + ts, :] * w_ref[k:k + 1, :]
        du_ref[...] = acc.astype(BF16)

    nxt = pl.BlockSpec((None, 8, dc), lambda bb, i: (bb, jnp.minimum((i + 1) * (ts // 8), s // 8 - 1), 0))
    return pl.pallas_call(
        body, name=name, grid=(b, nt),
        in_specs=[_row(ts, dc), nxt, _full((4, dc))],
        out_specs=_row(ts, dc), out_shape=SDS((b, s, dc), BF16),
        scratch_shapes=[pltpu.VMEM((ts + 8, dc), F32)], compiler_params=_cparams(2))(dv, dv, cw)


def _ssd_common(misc_ref, dtb_ref, alog_ref, e_ref):
    ln = CHUNK
    lane = lax.broadcasted_iota(I32, (ln, LANES), 1)
    lane1 = lax.broadcasted_iota(I32, (1, LANES), 1)
    pre = misc_ref[...] + dtb_ref[...]
    dt_s = jnp.where(lane < SSD_HEADS, _softplus(pre), 0.0)
    a_neg = jnp.where(lane1 < SSD_HEADS, -jnp.exp(alog_ref[...]), 0.0)
    ri = lax.broadcasted_iota(I32, (ln, ln), 0)
    ci = lax.broadcasted_iota(I32, (ln, ln), 1)
    tril = ci <= ri
    acum = jnp.dot(tril.astype(F32), dt_s * a_neg, preferred_element_type=F32, precision=HI)
    e = e_ref[...]
    expand = lambda t: jnp.dot(t, e, preferred_element_type=F32, precision=HI)
    last = acum[ln - 1:ln, :]
    delta_s = jnp.exp(last - acum)
    return dict(pre=pre, dt_s=dt_s, a_neg=a_neg, tril=tril, ri=ri, ci=ci, acum=acum, acum_t=acum.T,
                dt_e=expand(dt_s), eac_e=expand(jnp.exp(acum)), delta_s=delta_s, del_e=expand(delta_s))


def _decay(cm, h):
    seg = cm["acum"][:, h:h + 1] - cm["acum_t"][h:h + 1, :]
    return jnp.exp(jnp.where(cm["tril"], seg, -jnp.inf))


def ssd_fwd(xs, bm, cm_, misc, z, dtb, alog, dskip_e, norm_w, e_mat, name):
    b, s, _ = xs.shape
    ln, nc = CHUNK, s // CHUNK
    gw = D_SSD // SSD_GROUPS
    hpg = SSD_HEADS // SSD_GROUPS

    def body(xs_ref, b_ref, c_ref, misc_ref, z_ref, dtb_ref, alog_ref, dsk_ref, nw_ref, e_ref,
             ys_ref, y_ref, p_ref, st, yd):
        @pl.when(pl.program_id(1) == 0)
        def _():
            st[...] = jnp.zeros_like(st)
        cm = _ssd_common(misc_ref, dtb_ref, alog_ref, e_ref)
        xsv = xs_ref[...]
        xdt = xsv * cm["dt_e"]
        xdt_b = xdt.astype(BF16)
        xd_b = (xdt * cm["del_e"]).astype(BF16)
        gam_e = cm["eac_e"][ln - 1:ln, :]
        p_ref[...] = st[...]
        yoff = []
        for g in range(SSD_GROUPS):
            gs = slice(gw * g, gw * (g + 1))
            bg = b_ref[:, SSD_STATE * g:SSD_STATE * (g + 1)].astype(BF16)
            cg = c_ref[:, SSD_STATE * g:SSD_STATE * (g + 1)].astype(BF16)
            cb = lax.dot_general(cg, bg, NT_DIMS, preferred_element_type=F32)
            st_g = st[:, gs]
            yoff.append(jnp.dot(cg, st_g.astype(BF16), preferred_element_type=F32) * cm["eac_e"][:, gs])
            for j in range(hpg):
                h = hpg * g + j
                hs = slice(SSD_HEAD_DIM * h, SSD_HEAD_DIM * (h + 1))
                m = (cb * _decay(cm, h)).astype(BF16)
                yd[:, hs] = jnp.dot(m, xdt_b[:, hs], preferred_element_type=F32)
            new = lax.dot_general(bg, xd_b[:, gs], TN_DIMS, preferred_element_type=F32)
            st[:, gs] = st_g * gam_e[:, gs] + new
        y = yd[...] + jnp.concatenate(yoff, axis=1) + dsk_ref[...] * xsv
        y_ref[...] = y
        zz = z_ref[...]
        yg = y * (zz * _sigmoid(zz))
        outs = []
        for g in range(SSD_GROUPS):
            ygg = yg[:, gw * g:gw * (g + 1)]
            outs.append(ygg * _rms(ygg) * nw_ref[:, gw * g:gw * (g + 1)])
        ys_ref[...] = jnp.concatenate(outs, axis=1).astype(BF16)

    row = lambda d: pl.BlockSpec((None, ln, d), lambda bb, c: (bb, c, 0))
    return pl.pallas_call(
        body, name=name, grid=(b, nc),
        in_specs=[row(D_SSD), row(256), row(256), row(LANES), row(D_SSD), _full((1, LANES)), _full((1, LANES)),
                  _full((1, D_SSD)), _full((1, D_SSD)), _full((LANES, D_SSD))],
        out_specs=[row(D_SSD), row(D_SSD), pl.BlockSpec((None, None, SSD_STATE, D_SSD), lambda bb, c: (bb, c, 0, 0))],
        out_shape=[SDS((b, s, D_SSD), BF16), SDS((b, s, D_SSD), F32), SDS((b, nc, SSD_STATE, D_SSD), F32)],
        scratch_shapes=[pltpu.VMEM((SSD_STATE, D_SSD), F32), pltpu.VMEM((ln, D_SSD), F32)],
        compiler_params=_cparams(2))(xs, bm, cm_, misc, z, dtb, alog, dskip_e, norm_w, e_mat)


def ssd_bwd(dys, y, z, xs, bm, cm_, misc, prev, dtb, alog, dskip_e, norm_w, e_mat, et_mat, name):
    b, s, _ = xs.shape
    ln, nc = CHUNK, s // CHUNK
    gw = D_SSD // SSD_GROUPS
    hpg = SSD_HEADS // SSD_GROUPS

    def body(dys_ref, y_ref, z_ref, xs_ref, b_ref, c_ref, misc_ref, p_ref, dtb_ref, alog_ref, dsk_ref, nw_ref,
             e_ref, et_ref, dxs_ref, db_ref, dc_ref, dz_ref, ddt_ref, dnw_ref, ddsk_ref, ddtb_ref, dalog_ref,
             dst, dxd, dac_t):
        @pl.when(_first_step())
        def _():
            for r_ in (dnw_ref, ddsk_ref, ddtb_ref, dalog_ref):
                r_[...] = jnp.zeros_like(r_)

        @pl.when(pl.program_id(1) == 0)
        def _():
            dst[...] = jnp.zeros_like(dst)

        cm = _ssd_common(misc_ref, dtb_ref, alog_ref, e_ref)
        et = et_ref[...]
        squeeze = lambda t: jnp.dot(t, et, preferred_element_type=F32, precision=HI)
        lane = lax.broadcasted_iota(I32, (ln, LANES), 1)
        sub = lax.broadcasted_iota(I32, (LANES, ln), 0)
        xsv = xs_ref[...]
        xdt = xsv * cm["dt_e"]
        xdt_b = xdt.astype(BF16)
        xd_b = (xdt * cm["del_e"]).astype(BF16)
        eac_e = cm["eac_e"]
        gam_e = eac_e[ln - 1:ln, :]

        yv, zz, dyo = y_ref[...], z_ref[...], dys_ref[...]
        sz = _sigmoid(zz)
        silu_z = zz * sz
        yg = yv * silu_z
        dyg, dnw = [], []
        for g in range(SSD_GROUPS):
            gs = slice(gw * g, gw * (g + 1))
            ygg = yg[:, gs]
            r = _rms(ygg)
            n = ygg * r
            dnw.append(jnp.sum(dyo[:, gs] * n, axis=0, keepdims=True))
            dyg.append(_rms_bwd(dyo[:, gs] * nw_ref[:, gs], n, r))
        dyg = jnp.concatenate(dyg, axis=1)
        dnw_ref[...] += jnp.concatenate(dnw, axis=1)
        dz_ref[...] = (dyg * yv * (sz * (1.0 + zz * (1.0 - sz)))).astype(BF16)
        dy = dyg * silu_z
        ddsk_ref[...] += jnp.sum(dy * xsv, axis=0, keepdims=True)
        dy_b = dy.astype(BF16)

        dacum = jnp.zeros((ln, LANES), F32)
        dac_t[...] = jnp.zeros_like(dac_t)
        w1, dgam = [], []
        for g in range(SSD_GROUPS):
            gs = slice(gw * g, gw * (g + 1))
            ss = slice(SSD_STATE * g, SSD_STATE * (g + 1))
            bg = b_ref[:, ss].astype(BF16)
            cg = c_ref[:, ss].astype(BF16)
            cb = lax.dot_general(cg, bg, NT_DIMS, preferred_element_type=F32)
            pt = p_ref[:, gs]
            pt_b = pt.astype(BF16)
            dst_g = dst[:, gs]
            dst_b = dst_g.astype(BF16)
            edy = (dy[:, gs] * eac_e[:, gs]).astype(BF16)
            dcg = lax.dot_general(edy, pt_b, NT_DIMS, preferred_element_type=F32)
            dpt = lax.dot_general(cg, edy, TN_DIMS, preferred_element_type=F32)
            yoff = jnp.dot(cg, pt_b, preferred_element_type=F32) * eac_e[:, gs]
            dxd_g = jnp.dot(bg, dst_b, preferred_element_type=F32)
            dbg = lax.dot_general(xd_b[:, gs], dst_b, NT_DIMS, preferred_element_type=F32)
            ddel = dxd_g * xdt[:, gs] * cm["del_e"][:, gs]
            w1.append(dy[:, gs] * yoff - ddel)
            dgam.append(jnp.sum(ddel, axis=0, keepdims=True) + jnp.sum(dst_g * pt, axis=0, keepdims=True) * gam_e[:, gs])
            dxd[:, gs] = dxd_g * cm["del_e"][:, gs]
            dst[:, gs] = dst_g * gam_e[:, gs] + dpt
            dcb = jnp.zeros((ln, ln), F32)
            for j in range(hpg):
                h = hpg * g + j
                hs = slice(SSD_HEAD_DIM * h, SSD_HEAD_DIM * (h + 1))
                lam = _decay(cm, h)
                m = cb * lam
                dm = lax.dot_general(dy_b[:, hs], xdt_b[:, hs], NT_DIMS, preferred_element_type=F32)
                dxd[:, hs] += lax.dot_general(m.astype(BF16), dy_b[:, hs], TN_DIMS, preferred_element_type=F32)
                dcb += dm * lam
                wl = dm * m
                dacum += jnp.where(lane == h, jnp.sum(wl, axis=1, keepdims=True), 0.0)
                dac_t[...] -= jnp.where(sub == h, jnp.sum(wl, axis=0, keepdims=True), 0.0)
            dcb_b = dcb.astype(BF16)
            dc_ref[:, ss] = dcg + jnp.dot(dcb_b, bg, preferred_element_type=F32)
            db_ref[:, ss] = dbg + lax.dot_general(dcb_b, cg, TN_DIMS, preferred_element_type=F32)

        dxdt = dxd[...]
        dxs_ref[...] = dy * dsk_ref[...] + dxdt * cm["dt_e"]
        dacum += squeeze(jnp.concatenate(w1, axis=1)) + dac_t[...].T
        dlast = squeeze(jnp.broadcast_to(jnp.concatenate(dgam, axis=1), (8, D_SSD)))[0:1, :]
        dacum += jnp.where(lax.broadcasted_iota(I32, (ln, LANES), 0) == ln - 1, dlast, 0.0)
        triu = (cm["ci"] >= cm["ri"]).astype(F32)
        da = jnp.dot(triu, dacum, preferred_element_type=F32, precision=HI)
        ddt = da * cm["a_neg"] + squeeze(dxdt * xsv)
        dalog_ref[...] += jnp.sum(da * cm["dt_s"], axis=0, keepdims=True) * cm["a_neg"]
        ddt_raw = jnp.where(lane < SSD_HEADS, ddt * _sigmoid(cm["pre"]), 0.0)
        ddt_ref[...] = ddt_raw
        ddtb_ref[...] += jnp.sum(ddt_raw, axis=0, keepdims=True)

    row = lambda d: pl.BlockSpec((None, ln, d), lambda bb, c: (bb, nc - 1 - c, 0))
    return pl.pallas_call(
        body, name=name, grid=(b, nc),
        in_specs=[row(D_SSD), row(D_SSD), row(D_SSD), row(D_SSD), row(256), row(256), row(LANES),
                  pl.BlockSpec((None, None, SSD_STATE, D_SSD), lambda bb, c: (bb, nc - 1 - c, 0, 0)),
                  _full((1, LANES)), _full((1, LANES)), _full((1, D_SSD)), _full((1, D_SSD)),
                  _full((LANES, D_SSD)), _full((D_SSD, LANES))],
        out_specs=[row(D_SSD), row(256), row(256), row(D_SSD), row(LANES),
                   _full((1, D_SSD)), _full((1, D_SSD)), _full((1, LANES)), _full((1, LANES))],
        out_shape=[SDS((b, s, D_SSD), F32), SDS((b, s, 256), F32), SDS((b, s, 256), F32), SDS((b, s, D_SSD), BF16),
                   SDS((b, s, LANES), F32), SDS((1, D_SSD), F32), SDS((1, D_SSD), F32), SDS((1, LANES), F32),
                   SDS((1, LANES), F32)],
        scratch_shapes=[pltpu.VMEM((SSD_STATE, D_SSD), F32), pltpu.VMEM((ln, D_SSD), F32), pltpu.VMEM((LANES, ln), F32)],
        compiler_params=_cparams(2))(dys, y, z, xs, bm, cm_, misc, prev, dtb, alog, dskip_e, norm_w, e_mat, et_mat)


def _rope(xv, cc, sp, sm):
    n = xv.shape[1]
    return xv * cc + pltpu.roll(xv, 16, 1) * sp + pltpu.roll(xv, n - 16, 1) * sm


def _rope_bwd(dy, cc, sp, sm):
    n = dy.shape[1]
    return dy * cc + pltpu.roll(dy * sp, n - 16, 1) + pltpu.roll(dy * sm, 16, 1)


def _tile8(t):
    return jnp.concatenate([t] * MLA_HEADS, axis=1)


def qkv_fwd(cq, ckv, misc, cc, sp, sm, qnw, kvnw, wuq_t, wukv_t, place, name):
    b, s, _ = cq.shape
    tm = min(256, s)
    hd = MLA_HEADS * HEAD_PAD

    def body(cq_ref, ckv_ref, misc_ref, cc_ref, sp_ref, sm_ref, qnw_ref, kvnw_ref, wq_ref, wkv_ref, pl_ref,
             q_ref, k_ref, v_ref, qn_ref, kvn_ref):
        cqv, ckvv = cq_ref[...], ckv_ref[...]
        qn = (cqv * _rms(cqv) * qnw_ref[...]).astype(BF16)
        kvn = (ckvv * _rms(ckvv) * kvnw_ref[...]).astype(BF16)
        qn_ref[...] = qn
        kvn_ref[...] = kvn
        cc1, sp1, sm1 = cc_ref[...], sp_ref[...], sm_ref[...]
        q = lax.dot_general(qn, wq_ref[...], NT_DIMS, preferred_element_type=F32)
        q_ref[...] = _rope(q, _tile8(cc1), _tile8(sp1), _tile8(sm1)).astype(BF16)
        kv = lax.dot_general(kvn, wkv_ref[...], NT_DIMS, preferred_element_type=F32)
        kr = jnp.dot(misc_ref[...], pl_ref[...], preferred_element_type=F32, precision=HI)
        kr = _rope(kr, cc1, sp1, sm1)
        k_ref[...] = (kv[:, 0:hd] + _tile8(kr)).astype(BF16)
        v_ref[...] = kv[:, hd:2 * hd].astype(BF16)

    return pl.pallas_call(
        body, name=name, grid=(b, s // tm),
        in_specs=[_row(tm, Q_LORA), _row(tm, KV_LORA), _row(tm, LANES), _row(tm, LANES), _row(tm, LANES), _row(tm, LANES),
                  _full((1, Q_LORA)), _full((1, KV_LORA)), _full(wuq_t.shape), _full(wukv_t.shape), _full((LANES, LANES))],
        out_specs=[_row(tm, hd), _row(tm, hd), _row(tm, hd), _row(tm, Q_LORA), _row(tm, KV_LORA)],
        out_shape=[SDS((b, s, hd), BF16)] * 3 + [SDS((b, s, Q_LORA), BF16), SDS((b, s, KV_LORA), BF16)],
        compiler_params=_cparams(2))(cq, ckv, misc, cc, sp, sm, qnw, kvnw, wuq_t, wukv_t, place)


def qkv_bwd(dq, dk, dv, ddt, cq, ckv, cc, sp, sm, qnw, kvnw, wuq_t, wukv_t, place_t, name):
    b, s, _ = cq.shape
    tm = min(256, s)
    hd = MLA_HEADS * HEAD_PAD

    def body(dq_ref, dk_ref, dv_ref, ddt_ref, cq_ref, ckv_ref, cc_ref, sp_ref, sm_ref, qnw_ref, kvnw_ref,
             wq_ref, wkv_ref, plt_ref, dcq_ref, dckv_ref, dmisc_ref, dqp_ref, dkv_ref, dqnw_ref, dkvnw_ref):
        @pl.when(_first_step())
        def _():
            dqnw_ref[...] = jnp.zeros_like(dqnw_ref)
            dkvnw_ref[...] = jnp.zeros_like(dkvnw_ref)
        cc1, sp1, sm1 = cc_ref[...], sp_ref[...], sm_ref[...]
        dqp = _rope_bwd(dq_ref[...], _tile8(cc1), _tile8(sp1), _tile8(sm1)).astype(BF16)
        dqp_ref[...] = dqp
        dkf = dk_ref[...]
        dkv_b = jnp.concatenate([dkf, dv_ref[...]], axis=1).astype(BF16)
        dkv_ref[...] = dkv_b
        dkr = dkf[:, 0:HEAD_PAD]
        for h in range(1, MLA_HEADS):
            dkr += dkf[:, HEAD_PAD * h:HEAD_PAD * (h + 1)]
        dkr = _rope_bwd(dkr, cc1, sp1, sm1)
        dmisc_ref[...] = (jnp.dot(dkr, plt_ref[...], preferred_element_type=F32, precision=HI) + ddt_ref[...]).astype(BF16)

        def norm_bwd(dn_w, xv, w_ref, dw_ref, dx_ref):
            r = _rms(xv)
            n = xv * r
            dw_ref[...] += jnp.sum(dn_w * n, axis=0, keepdims=True)
            dx_ref[...] = _rms_bwd(dn_w * w_ref[...], n, r).astype(BF16)

        norm_bwd(jnp.dot(dqp, wq_ref[...], preferred_element_type=F32), cq_ref[...], qnw_ref, dqnw_ref, dcq_ref)
        norm_bwd(jnp.dot(dkv_b, wkv_ref[...], preferred_element_type=F32), ckv_ref[...], kvnw_ref, dkvnw_ref, dckv_ref)

    return pl.pallas_call(
        body, name=name, grid=(b, s // tm),
        in_specs=[_row(tm, hd), _row(tm, hd), _row(tm, hd), _row(tm, LANES), _row(tm, Q_LORA), _row(tm, KV_LORA),
                  _row(tm, LANES), _row(tm, LANES), _row(tm, LANES), _full((1, Q_LORA)), _full((1, KV_LORA)),
                  _full(wuq_t.shape), _full(wukv_t.shape), _full((LANES, LANES))],
        out_specs=[_row(tm, Q_LORA), _row(tm, KV_LORA), _row(tm, LANES), _row(tm, hd), _row(tm, 2 * hd),
                   _full((1, Q_LORA)), _full((1, KV_LORA))],
        out_shape=[SDS((b, s, Q_LORA), BF16), SDS((b, s, KV_LORA), BF16), SDS((b, s, LANES), BF16),
                   SDS((b, s, hd), BF16), SDS((b, s, 2 * hd), BF16), SDS((1, Q_LORA), F32), SDS((1, KV_LORA), F32)],
        compiler_params=_cparams(2))(dq, dk, dv, ddt, cq, ckv, cc, sp, sm, qnw, kvnw, wuq_t, wukv_t, place_t)


ATT_SCALE = 1.0 / math.sqrt(QK_DIM)


ATT_HEADS_PER_STEP = 2


def _att_tile(s):
    return min(512, s)


def flash_fwd(q, k, v, name):
    b, s, hd = q.shape
    t = _att_tile(s)
    nb = s // t
    th = t // 2
    vt = v.reshape(b, nb, t, MLA_HEADS, HEAD_PAD).transpose(0, 3, 1, 4, 2)

    hps = ATT_HEADS_PER_STEP
    hw = hps * HEAD_PAD

    def body(q_ref, k_ref, vt_ref, o_ref, lse_ref, m_s, l_s, acc):
        i = pl.program_id(2)
        m_s[...] = jnp.full_like(m_s, -jnp.inf)
        l_s[...] = jnp.zeros_like(l_s)
        acc[...] = jnp.zeros_like(acc)

        def update(j, diagonal):
            ks = pl.ds(pl.multiple_of(j * t, t), t)
            chains = [(hh, half) for hh in range(hps) for half in range(2)]
            lanes = lambda hh: slice(HEAD_PAD * hh, HEAD_PAD * (hh + 1))
            cols = lambda half: slice(th * half, th * (half + 1))
            sts = {}
            for hh, half in chains:
                st = lax.dot_general(k_ref[ks, lanes(hh)], q_ref[cols(half), lanes(hh)], NT_DIMS,
                                     preferred_element_type=F32) * ATT_SCALE
                if diagonal:
                    row = lax.broadcasted_iota(I32, (t, th), 0)
                    col = lax.broadcasted_iota(I32, (t, th), 1) + th * half
                    st = jnp.where(row <= col, st, -jnp.inf)
                sts[hh, half] = st
            pts, alphas = {}, {}
            for hh, half in chains:
                st, cs = sts[hh, half], cols(half)
                m_prev = m_s[hh, :, cs]
                m_new = jnp.maximum(m_prev, jnp.max(st, axis=0, keepdims=True))
                alpha = jnp.exp(m_prev - m_new)
                pt = jnp.exp(st - m_new)
                l_s[hh, :, cs] = alpha * l_s[hh, :, cs] + jnp.sum(pt, axis=0, keepdims=True)
                m_s[hh, :, cs] = m_new
                pts[hh, half], alphas[hh, half] = pt.astype(BF16), alpha
            for hh, half in chains:
                cs = cols(half)
                acc[hh, :, cs] = alphas[hh, half] * acc[hh, :, cs] + jnp.dot(vt_ref[hh, j], pts[hh, half],
                                                                             preferred_element_type=F32)

        def step(j, carry):
            update(j, False)
            return carry

        lax.fori_loop(0, i, step, 0)
        update(i, True)
        for hh in range(hps):
            o_ref[:, HEAD_PAD * hh:HEAD_PAD * (hh + 1)] = (acc[hh] / l_s[hh]).T
            lse_ref[hh] = m_s[hh] + jnp.log(l_s[hh])

    qs = pl.BlockSpec((None, t, hw), lambda bb, h, i: (bb, i, h))
    ks = pl.BlockSpec((None, s, hw), lambda bb, h, i: (bb, 0, h))
    vs = pl.BlockSpec((None, hps, nb, HEAD_PAD, t), lambda bb, h, i: (bb, h, 0, 0, 0))
    ls = pl.BlockSpec((None, hps, None, 1, t), lambda bb, h, i: (bb, h, i, 0, 0))
    return pl.pallas_call(
        body, name=name, grid=(b, MLA_HEADS // hps, nb),
        in_specs=[qs, ks, vs], out_specs=[qs, ls],
        out_shape=[SDS((b, s, hd), F32), SDS((b, MLA_HEADS, nb, 1, t), F32)],
        scratch_shapes=[pltpu.VMEM((hps, 1, t), F32), pltpu.VMEM((hps, 1, t), F32), pltpu.VMEM((hps, HEAD_PAD, t), F32)],
        compiler_params=_cparams(3))(q, k, vt)


def flash_bwd(q, k, v, do, lse, dlt, name):
    b, s, hd = q.shape
    t = _att_tile(s)
    nb = s // t
    th = t // 2
    lse_r = lse
    dlt_r = dlt.reshape(b, MLA_HEADS, nb, 1, t)

    hps = ATT_HEADS_PER_STEP
    hw = hps * HEAD_PAD

    def body(q_ref, k_ref, v_ref, do_ref, lse_ref, dlt_ref, dq_ref, dk_ref, dv_ref):
        dq_ref[...] = jnp.zeros_like(dq_ref)
        dk_ref[...] = jnp.zeros_like(dk_ref)
        dv_ref[...] = jnp.zeros_like(dv_ref)

        def tile(j, i, diagonal):
            qs = pl.ds(pl.multiple_of(i * t, t), t)
            chains = [(hh, half) for hh in range(hps) for half in range(2)]
            lanes = lambda hh: slice(HEAD_PAD * hh, HEAD_PAD * (hh + 1))
            keys = lambda half: pl.ds(pl.multiple_of(j * t + th * half, th), th)
            sts, dpts = {}, {}
            for hh, half in chains:
                ls_, ks = lanes(hh), keys(half)
                st = lax.dot_general(k_ref[ks, ls_], q_ref[qs, ls_], NT_DIMS, preferred_element_type=F32) * ATT_SCALE
                if diagonal:
                    row = lax.broadcasted_iota(I32, (th, t), 0) + th * half
                    col = lax.broadcasted_iota(I32, (th, t), 1)
                    st = jnp.where(row <= col, st, -jnp.inf)
                sts[hh, half] = st
                dpts[hh, half] = lax.dot_general(v_ref[ks, ls_], do_ref[qs, ls_], NT_DIMS, preferred_element_type=F32)
            pts, dsts = {}, {}
            for hh, half in chains:
                pt = jnp.exp(sts[hh, half] - lse_ref[hh, i])
                pts[hh, half] = pt.astype(BF16)
                dsts[hh, half] = (pt * (dpts[hh, half] - dlt_ref[hh, i]) * ATT_SCALE).astype(BF16)
            for hh in range(hps):
                ls_ = lanes(hh)
                dq_acc = None
                for half in range(2):
                    ks = keys(half)
                    dv_ref[ks, ls_] += jnp.dot(pts[hh, half], do_ref[qs, ls_], preferred_element_type=F32)
                    dk_ref[ks, ls_] += jnp.dot(dsts[hh, half], q_ref[qs, ls_], preferred_element_type=F32)
                    part = lax.dot_general(dsts[hh, half], k_ref[ks, ls_], TN_DIMS, preferred_element_type=F32)
                    dq_acc = part if dq_acc is None else dq_acc + part
                dq_ref[qs, ls_] += dq_acc

        def key_tile(j, carry):
            tile(j, j, True)

            def query_tile(i, c2):
                tile(j, i, False)
                return c2

            lax.fori_loop(j + 1, nb, query_tile, 0)
            return carry

        lax.fori_loop(0, nb, key_tile, 0)

    hs = pl.BlockSpec((None, s, hw), lambda bb, h: (bb, 0, h))
    ls = pl.BlockSpec((None, hps, nb, 1, t), lambda bb, h: (bb, h, 0, 0, 0))
    return pl.pallas_call(
        body, name=name, grid=(b, MLA_HEADS // hps),
        in_specs=[hs, hs, hs, hs, ls, ls], out_specs=[hs, hs, hs],
        out_shape=[SDS((b, s, hd), F32)] * 3, compiler_params=_cparams(2))(q, k, v, do, lse_r, dlt_r)


def out_proj(ys, attn, mnw, wo, x, gate, name):
    b, s, d = x.shape
    tm = min(256, s)

    def body(ys_ref, at_ref, mnw_ref, wo_ref, x_ref, g_ref, xn_ref, o_ref, ym_ref):
        av = at_ref[...]
        ym = (av * _rms(av) * mnw_ref[...]).astype(BF16)
        ym_ref[...] = ym
        o = jnp.dot(ys_ref[...], wo_ref[0:D_SSD, :], preferred_element_type=F32)
        o += jnp.dot(ym, wo_ref[D_SSD:2 * D_SSD, :], preferred_element_type=F32)
        xn_ref[...] = x_ref[...] + g_ref[...] * o
        o_ref[...] = o.astype(BF16)

    return pl.pallas_call(
        body, name=name, grid=(b, s // tm),
        in_specs=[_row(tm, D_SSD), _row(tm, D_SSD), _full((1, D_SSD)), _full(wo.shape), _row(tm, d), _bvec(d)],
        out_specs=[_row(tm, d), _row(tm, d), _row(tm, D_SSD)],
        out_shape=[SDS((b, s, d), F32), SDS((b, s, d), BF16), SDS((b, s, D_SSD), BF16)],
        compiler_params=_cparams(2))(ys, attn, mnw, wo, x, gate)


def out_proj_bwd(dout, attn, mnw, wo, name):
    b, s, d = dout.shape
    tm = min(256, s)

    def body(do_ref, at_ref, mnw_ref, wo_ref, dys_ref, dat_ref, dlt_ref, dw_ref):
        @pl.when(_first_step())
        def _():
            dw_ref[...] = jnp.zeros_like(dw_ref)
        dov = do_ref[...]
        dys_ref[...] = lax.dot_general(dov, wo_ref[0:D_SSD, :], NT_DIMS, preferred_element_type=F32)
        dym = lax.dot_general(dov, wo_ref[D_SSD:2 * D_SSD, :], NT_DIMS, preferred_element_type=F32)
        av = at_ref[...]
        r = _rms(av)
        n = av * r
        dw_ref[...] += jnp.sum(dym * n, axis=0, keepdims=True)
        dat = _rms_bwd(dym * mnw_ref[...], n, r)
        dat_ref[...] = dat.astype(BF16)
        prod = dat * av
        for h in range(MLA_HEADS):
            dlt_ref[h] = jnp.sum(prod[:, HEAD_PAD * h:HEAD_PAD * (h + 1)], axis=1, keepdims=True)

    return pl.pallas_call(
        body, name=name, grid=(b, s // tm),
        in_specs=[_row(tm, d), _row(tm, D_SSD), _full((1, D_SSD)), _full(wo.shape)],
        out_specs=[_row(tm, D_SSD), _row(tm, D_SSD),
                   pl.BlockSpec((None, MLA_HEADS, tm, 1), lambda bb, i: (bb, 0, i, 0)), _full((1, D_SSD))],
        out_shape=[SDS((b, s, D_SSD), F32), SDS((b, s, D_SSD), BF16), SDS((b, MLA_HEADS, s, 1), F32),
                   SDS((1, D_SSD), F32)],
        compiler_params=_cparams(2))(dout, attn, mnw, wo)


def adaln_fwd(c_all, w_ada, b_ada, name):
    nb, d = c_all.shape
    n = w_ada.shape[1]

    def body(c_ref, w_ref, b_ref, m_ref, ca_ref):
        cv = c_ref[...]
        ca = (cv * _sigmoid(cv)).astype(BF16)
        ca_ref[...] = ca
        m_ref[...] = jnp.dot(ca, w_ref[...].astype(BF16), preferred_element_type=F32) + b_ref[...]

    return pl.pallas_call(
        body, name=name, out_shape=[SDS((nb, n), F32), SDS((nb, d), BF16)],
        compiler_params=pltpu.CompilerParams(vmem_limit_bytes=VMEM_LIMIT))(c_all, w_ada, b_ada)


def adaln_bwd(c_act, dmod_cols, name):
    d, n = c_act.shape[1], dmod_cols.shape[1]

    def body(c_ref, dm_ref, gw_ref):
        gw_ref[...] = lax.dot_general(c_ref[...], dm_ref[...].astype(BF16), TN_DIMS, preferred_element_type=F32)

    return pl.pallas_call(
        body, name=name, out_shape=SDS((d, n), F32),
        compiler_params=pltpu.CompilerParams(vmem_limit_bytes=VMEM_LIMIT))(c_act, dmod_cols)


def sum_rows(x, name):
    def body(x_ref, o_ref):
        o_ref[...] = jnp.sum(x_ref[...], axis=0, keepdims=True)
    return pl.pallas_call(body, name=name, out_shape=SDS((1, x.shape[1]), F32))(x)


def squeeze_heads(x, et_mat, name):
    def body(x_ref, et_ref, o_ref):
        xv = jnp.broadcast_to(x_ref[...], (8, x.shape[1]))
        o_ref[...] = jnp.dot(xv, et_ref[...], preferred_element_type=F32, precision=HI)[0:1, :]
    return pl.pallas_call(body, name=name, out_shape=SDS((1, LANES), F32))(x, et_mat)


def sum_blocks(x, name):
    n, r, c = x.shape

    def body(x_ref, o_ref):
        acc = x_ref[0].astype(F32)
        for k in range(1, n):
            acc += x_ref[k].astype(F32)
        o_ref[...] = acc

    return pl.pallas_call(body, name=name, out_shape=SDS((r, c), F32),
                          compiler_params=pltpu.CompilerParams(vmem_limit_bytes=VMEM_LIMIT))(x)


def _adam_math(w, g, m, v):
    m = ADAM_B1 * m + (1.0 - ADAM_B1) * g
    v = ADAM_B2 * v + (1.0 - ADAM_B2) * (g * g)
    m_hat = m / (1.0 - ADAM_B1 ** ADAM_STEP)
    v_hat = v / (1.0 - ADAM_B2 ** ADAM_STEP)
    return -ADAM_LR * (m_hat / (jnp.sqrt(v_hat) + ADAM_EPS) + ADAM_WD * w), m, v


def adamw(w, g, m, v, name):
    r, c = w.shape
    tr = r
    for cand in (512, 256, 128, 64, 32, 16, 8):
        if r % cand == 0 and cand * c * 4 <= 2 * 1024 * 1024:
            tr = cand
            break

    def body(w_ref, g_ref, m_ref, v_ref, d_ref, mo_ref, vo_ref):
        d_ref[...], mo_ref[...], vo_ref[...] = _adam_math(w_ref[...], g_ref[...], m_ref[...], v_ref[...])

    spec = pl.BlockSpec((tr, c), lambda i: (i, 0))
    return pl.pallas_call(
        body, name=name, grid=(r // tr,), in_specs=[spec] * 4, out_specs=[spec] * 3,
        out_shape=[SDS((r, c), F32)] * 3, compiler_params=_cparams(1))(w, g, m, v)


def adamw_sum8(w, gparts, m, v, name):
    r, c = w.shape
    tr = 64 if r % 64 == 0 else r

    def body(w_ref, gp_ref, m_ref, v_ref, g_ref, d_ref, mo_ref, vo_ref):
        g = gp_ref[0].astype(F32)
        for k in range(1, N_DEV):
            g += gp_ref[k].astype(F32)
        g_ref[...] = g
        d_ref[...], mo_ref[...], vo_ref[...] = _adam_math(w_ref[...], g, m_ref[...], v_ref[...])

    spec = pl.BlockSpec((tr, c), lambda i: (i, 0))
    gspec = pl.BlockSpec((N_DEV, tr, c), lambda i: (0, i, 0))
    return pl.pallas_call(
        body, name=name, grid=(r // tr,), in_specs=[spec, gspec, spec, spec], out_specs=[spec] * 4,
        out_shape=[SDS((r, c), F32)] * 4, compiler_params=_cparams(1))(w, gparts, m, v)


PACK = (("ffn1_w_gate", 352, 352), ("ffn1_w_up", 352, 352), ("ffn1_w_down", 352, 352),
        ("ffn2_w_gate", 352, 352), ("ffn2_w_up", 352, 352), ("ffn2_w_down", 352, 352),
        ("w_out", 256, 256), ("w_in", 406, 416), ("w_ukv", 48, 48), ("w_uq", 36, 48))
PACK_ROWS = sum(p[2] for p in PACK)
PACK_OFF = {}
_o = 0
for _n, _r, _p in PACK:
    PACK_OFF[_n] = (_o, _r)
    _o += _p
TRANSPOSED = ("ffn1_w_gate", "ffn1_w_up", "ffn2_w_gate", "ffn2_w_up", "w_in", "w_ukv", "w_uq")


def _shard_to_rows(name, w):
    w = w[0]
    if name in TRANSPOSED:
        w = w.T
    return w.reshape(-1, D_MODEL)


def _rows_to_shard(name, rows, like):
    shp = like.shape[1:]
    if name in TRANSPOSED:
        return rows.reshape(shp[1], shp[0]).T[None]
    return rows.reshape(shp)[None]


def _pack_shards(ws, dtype):
    parts = []
    for name, real, padded in PACK:
        rows = _shard_to_rows(name, ws[name]).astype(dtype)
        if padded > real:
            rows = jnp.pad(rows, ((0, padded - real), (0, 0)))
        parts.append(rows)
    return jnp.concatenate(parts, axis=0)


def _seg(g, name):
    o, r = PACK_OFF[name]
    return g[:, o:o + r]


def _pack_rows(arrs):
    parts = []
    for a in arrs:
        flat = a.reshape(-1).astype(F32)
        pad = (-flat.shape[0]) % D_MODEL
        if pad:
            flat = jnp.pad(flat, (0, pad))
        parts.append(flat.reshape(-1, D_MODEL))
    out = jnp.concatenate(parts, axis=0)
    pad = (-out.shape[0]) % 8
    if pad:
        out = jnp.pad(out, ((0, pad), (0, 0)))
    return out


def _unpack_rows(packed, shapes):
    out, row = [], 0
    for shp in shapes:
        n = math.prod(shp)
        nrow = -(-n // D_MODEL)
        out.append(packed[row:row + nrow].reshape(-1)[:n].reshape(shp))
        row += nrow
    return out


def _in_proj_rows(w_t):
    return jnp.concatenate([w_t[0:2560], w_t[2576:2960], w_t[2960:3216], w_t[2560:2576], w_t[3216:3248],
                            jnp.zeros((D_IN_PAD - D_IN, D_MODEL), w_t.dtype)], axis=0)


def _in_proj_rows_inv(d):
    return jnp.concatenate([d[0:2560], d[3200:3216], d[2560:2944], d[2944:3200], d[3216:3248]], axis=0)


def _rope_tables(positions):
    inv_freq = ROPE_THETA ** (-jnp.arange(0, QK_ROPE, 2, dtype=F32) / QK_ROPE)
    ang = positions[..., None].astype(F32) * inv_freq
    cos, sin = jnp.cos(ang), jnp.sin(ang)
    one = jnp.ones(ang.shape[:2] + (QK_NOPE,), F32)
    zero = jnp.zeros_like(one)
    z16, z32, o32 = zero[..., :16], zero[..., :32], one[..., :32]
    cc = jnp.concatenate([one, cos, cos, o32], axis=-1)
    sp = jnp.concatenate([zero, z16, sin, z32], axis=-1)
    sm = jnp.concatenate([zero, -sin, z16, z32], axis=-1)
    return cc, sp, sm


def weight_views(g):
    full = lambda name: _seg(g, name).reshape(-1, D_MODEL)
    ukv = _seg(g, "w_ukv").reshape(MLA_HEADS, QK_NOPE + V_HEAD, KV_LORA)
    wukv_t = jnp.concatenate([jnp.pad(ukv[:, :QK_NOPE], ((0, 0), (0, HEAD_PAD - QK_NOPE), (0, 0))).reshape(-1, KV_LORA),
                              ukv[:, QK_NOPE:].reshape(-1, KV_LORA)], axis=0)
    uq = _seg(g, "w_uq").reshape(MLA_HEADS, QK_DIM, Q_LORA)
    wuq_t = jnp.pad(uq, ((0, 0), (0, HEAD_PAD - QK_DIM), (0, 0))).reshape(-1, Q_LORA)
    return dict(wg1_t=full("ffn1_w_gate"), wu1_t=full("ffn1_w_up"), wd1=full("ffn1_w_down"),
                wg2_t=full("ffn2_w_gate"), wu2_t=full("ffn2_w_up"), wd2=full("ffn2_w_down"),
                wo=full("w_out"), win_t=_in_proj_rows(full("w_in")), wukv_t=wukv_t, wuq_t=wuq_t)


def _ffn_bwd(tag, dxn, x, h, gg, uu, a, o, gate, sc, norm_w, wg_t, wu_t, wd):
    f2 = wd.shape[0] // 2
    do, dgate = gate_bwd(dxn, o, gate, 0.5, tag + "_gate_bwd")
    dgg, duu = ffn_dact(do, wd, gg, uu, tag + "_dact")
    dwd = mm_tn(a, do, f2, D_MODEL, tag + "_dwd")
    dwg_t = mm_tn(dgg, h, f2, D_MODEL, tag + "_dwg")
    dwu_t = mm_tn(duu, h, f2, D_MODEL, tag + "_dwu")
    dx, dsc, dsh, dnw = dh_norm_bwd([dgg, duu], [wg_t, wu_t], x, dxn, norm_w, sc, tag + "_dh")
    return dx, (dsh, dsc, dgate), dnw, (dwg_t, dwu_t, dwd)


def local_step(x, tgt, positions, mod, wv, p):
    nb, s, d = x.shape
    sh1, sc1, g1, sh2, sc2, g2, sh3, sc3, g3 = mod
    cc, sp, sm = _rope_tables(positions)
    lane_head = jnp.arange(D_SSD, dtype=I32)[None, :] // SSD_HEAD_DIM
    e_mat = (lane_head == jnp.arange(LANES, dtype=I32)[:, None]).astype(F32)
    et_mat = e_mat.T
    rr, cl = jnp.arange(LANES, dtype=I32)[:, None], jnp.arange(LANES, dtype=I32)[None, :]
    place = ((cl == rr + (QK_NOPE - SSD_HEADS)) & (rr >= SSD_HEADS) & (rr < SSD_HEADS + QK_ROPE)).astype(F32)
    dtb = jnp.pad(p["dt_bias"], ((0, 0), (0, LANES - SSD_HEADS)))
    alog = jnp.pad(p["a_log"], ((0, 0), (0, LANES - SSD_HEADS)))
    dskip_e = jnp.repeat(p["d_skip"], SSD_HEAD_DIM, axis=1)

    h1 = norm_mod(x, p["norm_ffn1"], sc1, sh1, "ffn1_norm")
    gg1, uu1, a1 = ffn_up(h1, wv["wg1_t"], wv["wu1_t"], "ffn1_up")
    x1, o1 = ffn_down(a1, wv["wd1"], x, g1, 0.5, "ffn1_down")
    h2 = norm_mod(x1, p["norm_mix"], sc2, sh2, "mix_norm")
    z, u, cq, ckv, misc = in_proj(h2, wv["win_t"], "in_proj")
    xs, bm, cm_ = conv_fwd(u, p["conv_w"], p["conv_b"], "conv_fwd")
    ys, y, prev = ssd_fwd(xs, bm, cm_, misc, z, dtb, alog, dskip_e, p["ssd_norm_w"], e_mat, "ssd_fwd")
    q, k, v, qn, kvn = qkv_fwd(cq, ckv, misc, cc, sp, sm, p["q_norm_w"], p["kv_norm_w"], wv["wuq_t"], wv["wukv_t"],
                               place, "qkv_fwd")
    attn, lse = flash_fwd(q, k, v, "flash_fwd")
    x2, o2, ym = out_proj(ys, attn, p["mla_norm_w"], wv["wo"], x1, g2, "out_proj")
    h3 = norm_mod(x2, p["norm_ffn2"], sc3, sh3, "ffn2_norm")
    gg3, uu3, a3 = ffn_up(h3, wv["wg2_t"], wv["wu2_t"], "ffn2_up")
    x3, o3 = ffn_down(a3, wv["wd2"], x2, g3, 0.5, "ffn2_down")
    loss, dx3, dnfin = final_loss(x3, p["norm_final"], tgt, "final_loss")

    dx2, dmod3, dnf2, (dwg2, dwu2, dwd2) = _ffn_bwd("ffn2", dx3, x2, h3, gg3, uu3, a3, o3, g3, sc3, p["norm_ffn2"],
                                                   wv["wg2_t"], wv["wu2_t"], wv["wd2"])
    dout, dg2 = gate_bwd(dx2, o2, g2, 1.0, "mix_gate_bwd")
    dys, dattn, dlt, dmlan = out_proj_bwd(dout, attn, p["mla_norm_w"], wv["wo"], "out_proj_bwd")
    dwo = jnp.concatenate([mm_tn(ys, dout, D_SSD, D_MODEL, "dwo_ssd"), mm_tn(ym, dout, D_SSD, D_MODEL, "dwo_mla")], axis=0)
    dxs, dbm, dcm, dz, ddt, dssdn, ddsk_lane, ddtb, dalog = ssd_bwd(
        dys, y, z, xs, bm, cm_, misc, prev, dtb, alog, dskip_e, p["ssd_norm_w"], e_mat, et_mat, "ssd_bwd")
    dq, dk, dv = flash_bwd(q, k, v, dattn, lse, dlt, "flash_bwd")
    dcq, dckv, dmisc, dqp, dkvc, dqn, dkvn = qkv_bwd(dq, dk, dv, ddt, cq, ckv, cc, sp, sm, p["q_norm_w"], p["kv_norm_w"],
                                                     wv["wuq_t"], wv["wukv_t"], place.T, "qkv_bwd")
    dwuq = mm_tn(dqp, qn, MLA_HEADS * HEAD_PAD, Q_LORA, "dwuq")
    dwukv = mm_tn(dkvc, kvn, MLA_HEADS * HEAD_PAD, KV_LORA, "dwukv")
    dvv, dconv = conv_bwd_a(dxs, dbm, dcm, u, p["conv_w"], p["conv_b"], "conv_bwd_a")
    du = conv_bwd_b(dvv, p["conv_w"], "conv_bwd_b")
    dproj = jnp.concatenate([dz, du, dcq, dckv, dmisc], axis=-1)
    dwin = mm_tn(dproj, h2, D_IN_PAD // 2, D_MODEL, "dwin")
    dx1, dsc2, dsh2, dnmix = dh_norm_bwd([dproj], [wv["win_t"]], x1, dx2, p["norm_mix"], sc2, "mix_dh")
    dx0, dmod1, dnf1, (dwg1, dwu1, dwd1) = _ffn_bwd("ffn1", dx1, x, h1, gg1, uu1, a1, o1, g1, sc1, p["norm_ffn1"],
                                                   wv["wg1_t"], wv["wu1_t"], wv["wd1"])

    dmod = jnp.concatenate([*dmod1, dsh2, dsc2, dg2, *dmod3], axis=1).reshape(nb, N_MOD * d)
    return dict(
        loss=loss, dx=dx0, dmod=dmod, norm_ffn1=dnf1, norm_mix=dnmix, norm_ffn2=dnf2, norm_final=dnfin,
        ssd_norm_w=dssdn, mla_norm_w=dmlan, q_norm_w=dqn, kv_norm_w=dkvn,
        dt_bias=ddtb[:, :SSD_HEADS], a_log=dalog[:, :SSD_HEADS],
        d_skip=squeeze_heads(ddsk_lane, et_mat, "d_skip_heads")[:, :SSD_HEADS],
        conv_b=dconv[4:5], conv_w=dconv[0:4],
        gw=dict(ffn1_w_gate=dwg1, ffn1_w_up=dwu1, ffn1_w_down=dwd1, ffn2_w_gate=dwg2, ffn2_w_up=dwu2, ffn2_w_down=dwd2,
                w_out=dwo, w_in=dwin, w_ukv=dwukv, w_uq=dwuq))


def kernel(x, c, positions, w_ada, b_ada, norm_ffn1, ffn1_w_gate, ffn1_w_up, ffn1_w_down, norm_mix, w_in, conv_w, conv_b, dt_bias, a_log, d_skip, ssd_norm_w, q_norm_w, w_uq, kv_norm_w, w_ukv, mla_norm_w, w_out, norm_ffn2, ffn2_w_gate, ffn2_w_up, ffn2_w_down, norm_final, loss_target, m_w_ada, m_b_ada, m_norm_ffn1, m_ffn1_w_gate, m_ffn1_w_up, m_ffn1_w_down, m_norm_mix, m_w_in, m_conv_w, m_conv_b, m_dt_bias, m_a_log, m_d_skip, m_ssd_norm_w, m_q_norm_w, m_w_uq, m_kv_norm_w, m_w_ukv, m_mla_norm_w, m_w_out, m_norm_ffn2, m_ffn2_w_gate, m_ffn2_w_up, m_ffn2_w_down, m_norm_final, v_w_ada, v_b_ada, v_norm_ffn1, v_ffn1_w_gate, v_ffn1_w_up, v_ffn1_w_down, v_norm_mix, v_w_in, v_conv_w, v_conv_b, v_dt_bias, v_a_log, v_d_skip, v_ssd_norm_w, v_q_norm_w, v_w_uq, v_kv_norm_w, v_w_ukv, v_mla_norm_w, v_w_out, v_norm_ffn2, v_ffn2_w_gate, v_ffn2_w_up, v_ffn2_w_down, v_norm_final):
    names = ["w_ada", "b_ada", "norm_ffn1", "ffn1_w_gate", "ffn1_w_up", "ffn1_w_down", "norm_mix", "w_in", "conv_w",
             "conv_b", "dt_bias", "a_log", "d_skip", "ssd_norm_w", "q_norm_w", "w_uq", "kv_norm_w", "w_ukv",
             "mla_norm_w", "w_out", "norm_ffn2", "ffn2_w_gate", "ffn2_w_up", "ffn2_w_down", "norm_final"]
    W = dict(zip(names, (w_ada, b_ada, norm_ffn1, ffn1_w_gate, ffn1_w_up, ffn1_w_down, norm_mix, w_in, conv_w, conv_b, dt_bias, a_log, d_skip, ssd_norm_w, q_norm_w, w_uq, kv_norm_w, w_ukv, mla_norm_w, w_out, norm_ffn2, ffn2_w_gate, ffn2_w_up, ffn2_w_down, norm_final)))
    M = dict(zip(names, (m_w_ada, m_b_ada, m_norm_ffn1, m_ffn1_w_gate, m_ffn1_w_up, m_ffn1_w_down, m_norm_mix, m_w_in, m_conv_w, m_conv_b, m_dt_bias, m_a_log, m_d_skip, m_ssd_norm_w, m_q_norm_w, m_w_uq, m_kv_norm_w, m_w_ukv, m_mla_norm_w, m_w_out, m_norm_ffn2, m_ffn2_w_gate, m_ffn2_w_up, m_ffn2_w_down, m_norm_final)))
    V = dict(zip(names, (v_w_ada, v_b_ada, v_norm_ffn1, v_ffn1_w_gate, v_ffn1_w_up, v_ffn1_w_down, v_norm_mix, v_w_in, v_conv_w, v_conv_b, v_dt_bias, v_a_log, v_d_skip, v_ssd_norm_w, v_q_norm_w, v_w_uq, v_kv_norm_w, v_w_ukv, v_mla_norm_w, v_w_out, v_norm_ffn2, v_ffn2_w_gate, v_ffn2_w_up, v_ffn2_w_down, v_norm_final)))

    nb, s, d = x.shape
    me = 4 * lax.axis_index("x") + 2 * lax.axis_index("y") + lax.axis_index("c")
    n_ada = w_ada.shape[2]

    cshape = [(nb, d), conv_w.shape[1:]]
    cg = all_gather8(_pack_rows([c, conv_w[0]]), "gather_c")
    c_all = jnp.stack([_unpack_rows(cg[k], cshape)[0] for k in range(N_DEV)]).reshape(N_DEV * nb, d)
    conv_w_full = jnp.concatenate([_unpack_rows(cg[k], cshape)[1] for k in range(N_DEV)], axis=1)
    wv = weight_views(all_gather8(_pack_shards(W, BF16), "gather_weights"))

    b_ada_cols = lax.dynamic_slice(b_ada, (0, me * n_ada), (1, n_ada))
    mod_cols, c_act = adaln_fwd(c_all, w_ada[0], b_ada_cols, "adaln_fwd")
    mod_g = all_gather8(mod_cols, "gather_mod")
    mod = lax.dynamic_slice(mod_g, (0, me * nb, 0), (N_DEV, nb, n_ada)).transpose(1, 0, 2).reshape(nb, N_MOD, 1, d)
    mod = [mod[:, k] for k in range(N_MOD)]

    P = dict(W)
    P["conv_w"] = conv_w_full
    P["norm_final"] = norm_final.reshape(1, d)
    R = local_step(x, loss_target, positions, mod, wv, P)

    dmod = R["dmod"]
    partial_shapes = [(1,), (1, d), (1, d), (1, d), (1, d), (1, d), (1, d), (1, Q_LORA), (1, KV_LORA),
                      (1, SSD_HEADS), (1, SSD_HEADS), (1, SSD_HEADS), (1, D_CONV), (4, D_CONV), (1, N_MOD * d),
                      (nb, N_MOD * d)]
    partial = _pack_rows([R["loss"][0, :1], R["norm_ffn1"], R["norm_mix"], R["norm_ffn2"], R["norm_final"],
                          R["ssd_norm_w"], R["mla_norm_w"], R["q_norm_w"], R["kv_norm_w"],
                          R["dt_bias"], R["a_log"], R["d_skip"], R["conv_b"], R["conv_w"],
                          sum_rows(dmod, "dmod_rows"), dmod])
    partial_g = all_gather8(partial, "gather_partials")
    (loss, g_nf1, g_nmix, g_nf2, g_nfin, g_ssdn, g_mlan, g_qn, g_kvn, g_dtb, g_alog, g_dskip, g_convb, g_convw,
     g_bada, _) = _unpack_rows(sum_blocks(partial_g, "sum_partials"), partial_shapes)
    dmod_all = jnp.stack([_unpack_rows(partial_g[k], partial_shapes)[-1] for k in range(N_DEV)]).reshape(N_DEV * nb, -1)
    g_wada = adaln_bwd(c_act, lax.dynamic_slice(dmod_all, (0, me * n_ada), (N_DEV * nb, n_ada)), "adaln_bwd")
    n_cw = conv_w.shape[2]
    G = {"w_ada": g_wada[None], "b_ada": g_bada, "norm_ffn1": g_nf1, "norm_mix": g_nmix, "norm_ffn2": g_nf2,
         "norm_final": g_nfin.reshape(d), "ssd_norm_w": g_ssdn, "mla_norm_w": g_mlan, "q_norm_w": g_qn,
         "kv_norm_w": g_kvn, "dt_bias": g_dtb, "a_log": g_alog, "d_skip": g_dskip, "conv_b": g_convb,
         "conv_w": lax.dynamic_slice(g_convw, (0, me * n_cw), (4, n_cw))[None]}

    gw = R["gw"]
    gp = [gw[name].reshape(N_DEV, -1, D_MODEL) for name in ("ffn1_w_gate", "ffn1_w_up", "ffn1_w_down", "ffn2_w_gate",
                                                             "ffn2_w_up", "ffn2_w_down", "w_out")]
    gp.append(jnp.pad(_in_proj_rows_inv(gw["w_in"]).reshape(N_DEV, -1, D_MODEL), ((0, 0), (0, 10), (0, 0))))
    dkv_ = gw["w_ukv"]
    hd = MLA_HEADS * HEAD_PAD
    gp.append(jnp.concatenate([dkv_[:hd].reshape(MLA_HEADS, HEAD_PAD, KV_LORA)[:, :QK_NOPE],
                               dkv_[hd:].reshape(MLA_HEADS, V_HEAD, KV_LORA)], axis=1).reshape(N_DEV, -1, D_MODEL))
    gp.append(jnp.pad(gw["w_uq"].reshape(MLA_HEADS, HEAD_PAD, Q_LORA)[:, :QK_DIM].reshape(N_DEV, -1, D_MODEL),
                      ((0, 0), (0, 12), (0, 0))))
    recv = all_to_all8(jnp.concatenate(gp, axis=1).astype(BF16), "exchange_grads")

    DW, NM, NV = {}, {}, {}
    big = adamw_sum8(_pack_shards(W, F32), recv, _pack_shards(M, F32), _pack_shards(V, F32), "adamw_matrices")
    for name, _, _ in PACK:
        o, r = PACK_OFF[name]
        G[name], DW[name], NM[name], NV[name] = [_rows_to_shard(name, t[o:o + r], W[name]) for t in big]
    dwa, nma, nva = adamw(w_ada[0], g_wada, m_w_ada[0], v_w_ada[0], "adamw_w_ada")
    DW["w_ada"], NM["w_ada"], NV["w_ada"] = dwa[None], nma[None], nva[None]
    small = [n for n in names if n not in DW]
    shapes = [W[n].shape for n in small]
    outs = adamw(_pack_rows([W[n] for n in small]), _pack_rows([G[n] for n in small]),
                 _pack_rows([M[n] for n in small]), _pack_rows([V[n] for n in small]), "adamw_small")
    for res, dst in zip(outs, (DW, NM, NV)):
        for n, t in zip(small, _unpack_rows(res, shapes)):
            dst[n] = t
    return (loss.reshape(()), R["dx"], *[G[n] for n in names], *[DW[n] for n in names], *[NM[n] for n in names],
            *[NV[n] for n in names])
```

```python
import math

import jax
import jax.numpy as jnp
from jax import lax
from jax.experimental import pallas as pl
from jax.experimental.pallas import tpu as pltpu
from jax.experimental.pallas import tpu_sc as plsc

F32, BF16, I32 = jnp.float32, jnp.bfloat16, jnp.int32
HI = lax.Precision.HIGHEST
SDS = jax.ShapeDtypeStruct
MESH = pl.DeviceIdType.MESH

D_MODEL = 1024
D_FF = 2816
D_SSD = 1024
SSD_HEADS = 16
SSD_HEAD_DIM = 64
SSD_GROUPS = 2
SSD_STATE = 128
CHUNK = 128
MLA_HEADS = 8
QK_NOPE = 64
QK_ROPE = 32
QK_DIM = 96
V_HEAD = 128
Q_LORA = 384
KV_LORA = 256
ROPE_THETA = 10000.0
N_MOD = 9
EPS = 1e-6
D_CONV = 1536
D_IN = 3248
D_IN_PAD = 3328
HEAD_PAD = 128
N_DEV = 8
ADAM_LR, ADAM_B1, ADAM_B2, ADAM_EPS, ADAM_WD, ADAM_STEP = 0.001, 0.9, 0.999, 1e-08, 0.01, 10

VMEM_LIMIT = 56 * 1024 * 1024
LANES = 128
NT_DIMS = (((1,), (1,)), ((), ()))
TN_DIMS = (((0,), (0,)), ((), ()))


def _cparams(n_axes):
    return pltpu.CompilerParams(dimension_semantics=("arbitrary",) * n_axes, vmem_limit_bytes=VMEM_LIMIT)


def _row(tm, d):
    return pl.BlockSpec((None, tm, d), lambda b, i: (b, i, 0))


def _bvec(d):
    return pl.BlockSpec((None, 1, d), lambda b, i: (b, 0, 0))


def _full(shape):
    n = len(shape)
    return pl.BlockSpec(shape, lambda *_: (0,) * n)


def _sigmoid(x):
    return 1.0 / (1.0 + jnp.exp(-x))


def _softplus(x):
    return jnp.maximum(x, 0.0) + jnp.log(1.0 + jnp.exp(-jnp.abs(x)))


def _rms(x):
    return lax.rsqrt(jnp.mean(x * x, axis=-1, keepdims=True) + EPS)


def _rms_bwd(dn, n, r):
    return r * (dn - n * jnp.mean(dn * n, axis=-1, keepdims=True))


def _first_step():
    return (pl.program_id(0) == 0) & (pl.program_id(1) == 0)


def all_gather8(x, name):
    r, c = x.shape

    def body(x_ref, out_ref, send_sems, recv_sems, local_sem):
        mx, my, mc = lax.axis_index("x"), lax.axis_index("y"), lax.axis_index("c")
        me, sibling = (mx, my, mc), (mx, my, 1 - mc)
        chips = [(1 - mx, my), (mx, 1 - my), (1 - mx, 1 - my)]

        def rows(px, py, pc):
            return out_ref.at[4 * px + 2 * py + pc]

        def copy(k, block, to, src=None):
            return pltpu.make_async_remote_copy(
                src_ref=rows(*block) if src is None else src, dst_ref=rows(*block),
                send_sem=send_sems.at[k], recv_sem=recv_sems.at[k], device_id=to, device_id_type=MESH)

        mine = pltpu.make_async_copy(x_ref, rows(*me), local_sem)
        mine.start()
        first = [copy(0, me, sibling, src=x_ref)]
        first += [copy(1 + j, me, (*chip, mc), src=x_ref) for j, chip in enumerate(chips)]
        for cp in first:
            cp.start()
        passed = [copy(4 + j, (*chip, mc), sibling) for j, chip in enumerate(chips)]
        for j, chip in enumerate(chips):
            copy(1 + j, (*chip, mc), me).wait_recv()
            passed[j].start()
        copy(0, sibling, me).wait_recv()
        for j, chip in enumerate(chips):
            copy(4 + j, (*chip, 1 - mc), me).wait_recv()
        for cp in first + passed:
            cp.wait_send()
        mine.wait()

    return pl.pallas_call(
        body, name=name,
        out_shape=SDS((N_DEV, r, c), x.dtype),
        in_specs=[pl.BlockSpec(memory_space=pl.ANY)],
        out_specs=pl.BlockSpec(memory_space=pl.ANY),
        scratch_shapes=[pltpu.SemaphoreType.DMA((7,)), pltpu.SemaphoreType.DMA((7,)), pltpu.SemaphoreType.DMA],
    )(x)


def all_to_all8(x, name):
    _, r, c = x.shape

    def body(x_ref, out_ref, send_sems, recv_sems, local_sem):
        mx, my, mc = lax.axis_index("x"), lax.axis_index("y"), lax.axis_index("c")
        me = 4 * mx + 2 * my + mc
        mine = pltpu.make_async_copy(x_ref.at[me], out_ref.at[me], local_sem)
        mine.start()
        copies = []
        for rel in range(1, N_DEV):
            px = 1 - mx if rel & 4 else mx
            py = 1 - my if rel & 2 else my
            pc = 1 - mc if rel & 1 else mc
            cp = pltpu.make_async_remote_copy(
                src_ref=x_ref.at[4 * px + 2 * py + pc], dst_ref=out_ref.at[me],
                send_sem=send_sems.at[rel - 1], recv_sem=recv_sems.at[rel - 1],
                device_id=(px, py, pc), device_id_type=MESH)
            cp.start()
            copies.append(cp)
        for cp in copies:
            cp.wait()
        mine.wait()

    return pl.pallas_call(
        body, name=name,
        out_shape=SDS((N_DEV, r, c), x.dtype),
        in_specs=[pl.BlockSpec(memory_space=pl.ANY)],
        out_specs=pl.BlockSpec(memory_space=pl.ANY),
        scratch_shapes=[pltpu.SemaphoreType.DMA((7,)), pltpu.SemaphoreType.DMA((7,)), pltpu.SemaphoreType.DMA],
    )(x)


def _sequencer_kernel(name, collective_id):
    return pl.kernel(
        mesh=plsc.ScalarSubcoreMesh(axis_name="seq", num_cores=1), name=name,
        scratch_types=(pltpu.SemaphoreType.DMA((7,)), pltpu.SemaphoreType.DMA((7,)), pltpu.SemaphoreType.DMA),
        compiler_params=pltpu.CompilerParams(collective_id=collective_id))


def _handshake(peers):
    barrier = pltpu.get_barrier_semaphore()
    for peer in peers:
        pl.semaphore_signal(barrier, inc=1, device_id=peer, device_id_type=MESH)
    pl.semaphore_wait(barrier, len(peers))


def sc_all_gather8(x, name, collective_id):
    r, c = x.shape
    x_ref = jax.new_ref(x, memory_space=pltpu.MemorySpace.HBM)
    out_ref = jax.empty_ref(SDS((N_DEV, r, c), x.dtype), memory_space=pltpu.MemorySpace.HBM)

    @_sequencer_kernel(name, collective_id)
    def launch(send_sems, recv_sems, local_sem):
        mx, my, mc = lax.axis_index("x"), lax.axis_index("y"), lax.axis_index("c")
        me, sibling = (mx, my, mc), (mx, my, 1 - mc)
        chips = [(1 - mx, my), (mx, 1 - my), (1 - mx, 1 - my)]
        _handshake([sibling] + [(*chip, mc) for chip in chips])

        def rows(px, py, pc):
            return out_ref.at[4 * px + 2 * py + pc]

        def copy(k, block, to, src=None):
            return pltpu.make_async_remote_copy(
                src_ref=rows(*block) if src is None else src, dst_ref=rows(*block),
                send_sem=send_sems.at[k], recv_sem=recv_sems.at[k], device_id=to, device_id_type=MESH)

        mine = pltpu.make_async_copy(x_ref, rows(*me), local_sem)
        mine.start()
        first = [copy(0, me, sibling, src=x_ref)]
        first += [copy(1 + j, me, (*chip, mc), src=x_ref) for j, chip in enumerate(chips)]
        for cp in first:
            cp.start()
        passed = [copy(4 + j, (*chip, mc), sibling) for j, chip in enumerate(chips)]
        for j, chip in enumerate(chips):
            copy(1 + j, (*chip, mc), me).wait_recv()
            passed[j].start()
        copy(0, sibling, me).wait_recv()
        for j, chip in enumerate(chips):
            copy(4 + j, (*chip, 1 - mc), me).wait_recv()
        for cp in first + passed:
            cp.wait_send()
        mine.wait()

    launch()
    return out_ref[...]


def sc_all_to_all8(x, name, collective_id):
    x_ref = jax.new_ref(x, memory_space=pltpu.MemorySpace.HBM)
    out_ref = jax.empty_ref(SDS(x.shape, x.dtype), memory_space=pltpu.MemorySpace.HBM)

    @_sequencer_kernel(name, collective_id)
    def launch(send_sems, recv_sems, local_sem):
        mx, my, mc = lax.axis_index("x"), lax.axis_index("y"), lax.axis_index("c")
        me = 4 * mx + 2 * my + mc
        peers = [(1 - mx if rel & 4 else mx, 1 - my if rel & 2 else my, 1 - mc if rel & 1 else mc)
                 for rel in range(1, N_DEV)]
        _handshake(peers)
        mine = pltpu.make_async_copy(x_ref.at[me], out_ref.at[me], local_sem)
        mine.start()
        copies = []
        for k, (px, py, pc) in enumerate(peers):
            cp = pltpu.make_async_remote_copy(
                src_ref=x_ref.at[4 * px + 2 * py + pc], dst_ref=out_ref.at[me],
                send_sem=send_sems.at[k], recv_sem=recv_sems.at[k], device_id=(px, py, pc), device_id_type=MESH)
            cp.start()
            copies.append(cp)
        for cp in copies:
            cp.wait()
        mine.wait()

    launch()
    return out_ref[...]


def norm_mod(x, w, sc, sh, name):
    b, s, d = x.shape
    tm = min(512, s)

    def body(x_ref, w_ref, sc_ref, sh_ref, h_ref):
        xv = x_ref[...]
        n = xv * _rms(xv)
        h_ref[...] = ((n * w_ref[...]) * (1.0 + sc_ref[...]) + sh_ref[...]).astype(BF16)

    return pl.pallas_call(
        body, name=name, grid=(b, s // tm),
        in_specs=[_row(tm, d), _full((1, d)), _bvec(d), _bvec(d)],
        out_specs=_row(tm, d), out_shape=SDS((b, s, d), BF16), compiler_params=_cparams(2))(x, w, sc, sh)


def ffn_up(h, wg_t, wu_t, name):
    b, s, d = h.shape
    f = wg_t.shape[0]
    tm, tn = min(512, s), f // 2

    def body(h_ref, wg_ref, wu_ref, g_ref, u_ref, a_ref):
        hv = h_ref[...]
        g = lax.dot_general(hv, wg_ref[...], NT_DIMS, preferred_element_type=F32)
        u = lax.dot_general(hv, wu_ref[...], NT_DIMS, preferred_element_type=F32)
        g_ref[...] = g
        u_ref[...] = u
        a_ref[...] = (g * _sigmoid(g) * u).astype(BF16)

    hs = pl.BlockSpec((None, tm, d), lambda j, bb, i: (bb, i, 0))
    ws = pl.BlockSpec((tn, d), lambda j, bb, i: (j, 0))
    os_ = pl.BlockSpec((None, tm, tn), lambda j, bb, i: (bb, i, j))
    return pl.pallas_call(
        body, name=name, grid=(f // tn, b, s // tm),
        in_specs=[hs, ws, ws], out_specs=[os_, os_, os_],
        out_shape=[SDS((b, s, f), F32), SDS((b, s, f), F32), SDS((b, s, f), BF16)],
        compiler_params=_cparams(3))(h, wg_t, wu_t)


def ffn_down(a, wd, x, gate, scale, name):
    b, s, f = a.shape
    d = wd.shape[1]
    tm = min(512, s)

    def body(a_ref, wd_ref, x_ref, g_ref, xn_ref, o_ref):
        o = jnp.dot(a_ref[...], wd_ref[...], preferred_element_type=F32)
        xn_ref[...] = x_ref[...] + (scale * g_ref[...]) * o
        o_ref[...] = o.astype(BF16)

    return pl.pallas_call(
        body, name=name, grid=(b, s // tm),
        in_specs=[_row(tm, f), _full((f, d)), _row(tm, d), _bvec(d)],
        out_specs=[_row(tm, d), _row(tm, d)],
        out_shape=[SDS((b, s, d), F32), SDS((b, s, d), BF16)], compiler_params=_cparams(2))(a, wd, x, gate)


def gate_bwd(dxn, o, gate, scale, name):
    b, s, d = dxn.shape
    tm = min(512, s)

    def body(dx_ref, o_ref, g_ref, do_ref, dg_ref):
        @pl.when(pl.program_id(1) == 0)
        def _():
            dg_ref[...] = jnp.zeros_like(dg_ref)
        dx = dx_ref[...]
        do_ref[...] = ((scale * g_ref[...]) * dx).astype(BF16)
        dg_ref[...] += jnp.sum(scale * dx * o_ref[...].astype(F32), axis=0, keepdims=True)

    return pl.pallas_call(
        body, name=name, grid=(b, s // tm),
        in_specs=[_row(tm, d), _row(tm, d), _bvec(d)],
        out_specs=[_row(tm, d), _bvec(d)],
        out_shape=[SDS((b, s, d), BF16), SDS((b, 1, d), F32)], compiler_params=_cparams(2))(dxn, o, gate)


def ffn_dact(do, wd, g, u, name):
    b, s, d = do.shape
    f = wd.shape[0]
    tm, tn = min(512, s), f // 2

    def body(do_ref, wd_ref, g_ref, u_ref, dg_ref, du_ref):
        da = lax.dot_general(do_ref[...], wd_ref[...], NT_DIMS, preferred_element_type=F32)
        gv = g_ref[...]
        sg = _sigmoid(gv)
        dg_ref[...] = (da * u_ref[...] * (sg * (1.0 + gv * (1.0 - sg)))).astype(BF16)
        du_ref[...] = (da * (gv * sg)).astype(BF16)

    dos = pl.BlockSpec((None, tm, d), lambda j, bb, i: (bb, i, 0))
    ws = pl.BlockSpec((tn, d), lambda j, bb, i: (j, 0))
    es = pl.BlockSpec((None, tm, tn), lambda j, bb, i: (bb, i, j))
    return pl.pallas_call(
        body, name=name, grid=(f // tn, b, s // tm),
        in_specs=[dos, ws, es, es], out_specs=[es, es],
        out_shape=[SDS((b, s, f), BF16), SDS((b, s, f), BF16)], compiler_params=_cparams(3))(do, wd, g, u)


def mm_tn(a, bm, tma, tnb, name):
    b, s, ka = a.shape
    nb = bm.shape[2]
    tk = min(512, s)

    def body(a_ref, b_ref, o_ref):
        @pl.when((pl.program_id(2) == 0) & (pl.program_id(3) == 0))
        def _():
            o_ref[...] = jnp.zeros_like(o_ref)
        o_ref[...] += lax.dot_general(a_ref[...], b_ref[...], TN_DIMS, preferred_element_type=F32)

    return pl.pallas_call(
        body, name=name, grid=(ka // tma, nb // tnb, b, s // tk),
        in_specs=[pl.BlockSpec((None, tk, tma), lambda i, j, bb, k: (bb, k, i)),
                  pl.BlockSpec((None, tk, tnb), lambda i, j, bb, k: (bb, k, j))],
        out_specs=pl.BlockSpec((tma, tnb), lambda i, j, bb, k: (i, j)),
        out_shape=SDS((ka, nb), F32), compiler_params=_cparams(4))(a, bm)


def dh_norm_bwd(dys, wts, x, dxn, w, sc, name):
    b, s, d = x.shape
    tm = min(256, s)
    n_in = len(dys)

    def body(*refs):
        dy_refs, w_refs = refs[:n_in], refs[n_in:2 * n_in]
        x_ref, dxn_ref, nw_ref, sc_ref, dx_ref, dsc_ref, dsh_ref, dw_ref = refs[2 * n_in:]

        @pl.when(pl.program_id(1) == 0)
        def _():
            dsc_ref[...] = jnp.zeros_like(dsc_ref)
            dsh_ref[...] = jnp.zeros_like(dsh_ref)

        @pl.when(_first_step())
        def _():
            dw_ref[...] = jnp.zeros_like(dw_ref)

        dh = jnp.dot(dy_refs[0][...], w_refs[0][...], preferred_element_type=F32)
        for k in range(1, n_in):
            dh += jnp.dot(dy_refs[k][...], w_refs[k][...], preferred_element_type=F32)
        xv = x_ref[...]
        r = _rms(xv)
        n = xv * r
        nw = nw_ref[...]
        dsc_ref[...] += jnp.sum(dh * (n * nw), axis=0, keepdims=True)
        dsh_ref[...] += jnp.sum(dh, axis=0, keepdims=True)
        dhn = dh * (1.0 + sc_ref[...])
        dw_ref[...] += jnp.sum(dhn * n, axis=0, keepdims=True)
        dx_ref[...] = dxn_ref[...] + _rms_bwd(dhn * nw, n, r)

    in_specs = [_row(tm, dy.shape[2]) for dy in dys] + [_full(wt.shape) for wt in wts]
    in_specs += [_row(tm, d), _row(tm, d), _full((1, d)), _bvec(d)]
    return pl.pallas_call(
        body, name=name, grid=(b, s // tm), in_specs=in_specs,
        out_specs=[_row(tm, d), _bvec(d), _bvec(d), _full((1, d))],
        out_shape=[SDS((b, s, d), F32), SDS((b, 1, d), F32), SDS((b, 1, d), F32), SDS((1, d), F32)],
        compiler_params=_cparams(2))(*dys, *wts, x, dxn, w, sc)


def final_loss(x, w, tgt, name):
    b, s, d = x.shape
    tm = min(512, s)

    def body(x_ref, w_ref, t_ref, loss_ref, dx_ref, dw_ref):
        @pl.when(_first_step())
        def _():
            loss_ref[...] = jnp.zeros_like(loss_ref)
            dw_ref[...] = jnp.zeros_like(dw_ref)
        xv = x_ref[...]
        r = _rms(xv)
        n = xv * r
        wv = w_ref[...]
        e = n * wv - t_ref[...]
        loss_ref[...] += jnp.sum(e * e) * (0.5 / d)
        dy = e * (1.0 / d)
        dw_ref[...] += jnp.sum(dy * n, axis=0, keepdims=True)
        dx_ref[...] = _rms_bwd(dy * wv, n, r)

    return pl.pallas_call(
        body, name=name, grid=(b, s // tm),
        in_specs=[_row(tm, d), _full((1, d)), _row(tm, d)],
        out_specs=[_full((1, LANES)), _row(tm, d), _full((1, d))],
        out_shape=[SDS((1, LANES), F32), SDS((b, s, d), F32), SDS((1, d), F32)],
        compiler_params=_cparams(2))(x, w, tgt)


def in_proj(h, win_t, name):
    b, s, d = h.shape
    tm = min(256, s)
    widths = (D_SSD, D_SSD + 2 * SSD_GROUPS * SSD_STATE, Q_LORA, KV_LORA, LANES)

    def body(h_ref, w_ref, *outs):
        p = lax.dot_general(h_ref[...], w_ref[...], NT_DIMS, preferred_element_type=F32)
        off = 0
        for o_ref, wd in zip(outs, widths):
            o_ref[...] = p[:, off:off + wd]
            off += wd

    return pl.pallas_call(
        body, name=name, grid=(b, s // tm),
        in_specs=[_row(tm, d), _full(win_t.shape)],
        out_specs=[_row(tm, wd) for wd in widths],
        out_shape=[SDS((b, s, wd), F32) for wd in widths], compiler_params=_cparams(2))(h, win_t)


def _halo_prev(ts, d):
    return pl.BlockSpec((None, 8, d), lambda b, i: (b, jnp.maximum(i * (ts // 8) - 1, 0), 0))


def _conv_taps(ext_ref, w_ref, ts):
    return [ext_ref[5 + k:5 + k + ts, :] for k in range(4)], [w_ref[k:k + 1, :] for k in range(4)]


def conv_fwd(u, cw, cb, name):
    b, s, dc = u.shape
    ts = min(512, s)
    widths = (D_SSD, SSD_GROUPS * SSD_STATE, SSD_GROUPS * SSD_STATE)

    def body(u_ref, up_ref, w_ref, b_ref, xs_ref, bm_ref, cm_ref, ext):
        ext[0:8, :] = jnp.where(pl.program_id(1) > 0, up_ref[...], 0.0)
        ext[8:8 + ts, :] = u_ref[...]
        taps, ws = _conv_taps(ext, w_ref, ts)
        v = b_ref[...] + taps[0] * ws[0] + taps[1] * ws[1] + taps[2] * ws[2] + taps[3] * ws[3]
        y = v * _sigmoid(v)
        xs_ref[...] = y[:, 0:D_SSD]
        bm_ref[...] = y[:, D_SSD:D_SSD + 256]
        cm_ref[...] = y[:, D_SSD + 256:D_SSD + 512]

    return pl.pallas_call(
        body, name=name, grid=(b, s // ts),
        in_specs=[_row(ts, dc), _halo_prev(ts, dc), _full((4, dc)), _full((1, dc))],
        out_specs=[_row(ts, wd) for wd in widths],
        out_shape=[SDS((b, s, wd), F32) for wd in widths],
        scratch_shapes=[pltpu.VMEM((ts + 8, dc), F32)], compiler_params=_cparams(2))(u, u, cw, cb)


def conv_bwd_a(dxs, dbm, dcm, u, cw, cb, name):
    b, s, dc = u.shape
    ts = min(512, s)

    def body(dxs_ref, dbm_ref, dcm_ref, u_ref, up_ref, w_ref, b_ref, dv_ref, dwb_ref, ext):
        @pl.when(_first_step())
        def _():
            dwb_ref[...] = jnp.zeros_like(dwb_ref)
        ext[0:8, :] = jnp.where(pl.program_id(1) > 0, up_ref[...], 0.0)
        ext[8:8 + ts, :] = u_ref[...]
        taps, ws = _conv_taps(ext, w_ref, ts)
        v = b_ref[...] + taps[0] * ws[0] + taps[1] * ws[1] + taps[2] * ws[2] + taps[3] * ws[3]
        sg = _sigmoid(v)
        dy = jnp.concatenate([dxs_ref[...], dbm_ref[...], dcm_ref[...]], axis=1)
        dv = dy * (sg * (1.0 + v * (1.0 - sg)))
        dv_ref[...] = dv
        for k in range(4):
            dwb_ref[k:k + 1, :] += jnp.sum(dv * taps[k], axis=0, keepdims=True)
        dwb_ref[4:5, :] += jnp.sum(dv, axis=0, keepdims=True)

    return pl.pallas_call(
        body, name=name, grid=(b, s // ts),
        in_specs=[_row(ts, D_SSD), _row(ts, 256), _row(ts, 256), _row(ts, dc), _halo_prev(ts, dc),
                  _full((4, dc)), _full((1, dc))],
        out_specs=[_row(ts, dc), _full((8, dc))],
        out_shape=[SDS((b, s, dc), F32), SDS((8, dc), F32)],
        scratch_shapes=[pltpu.VMEM((ts + 8, dc), F32)], compiler_params=_cparams(2))(dxs, dbm, dcm, u, u, cw, cb)


def conv_bwd_b(dv, cw, name):
    b, s, dc = dv.shape
    ts = min(512, s)
    nt = s // ts

    def body(dv_ref, dn_ref, w_ref, du_ref, ext):
        ext[0:ts, :] = dv_ref[...]
        ext[ts:ts + 8, :] = jnp.where(pl.program_id(1) < nt - 1, dn_ref[...], 0.0)
        acc = ext[3:3 + ts, :] * w_ref[0:1, :]
        for k in range(1, 4):
            acc += ext[3 - k:3 - k + ts, :] * w_ref[k:k + 1, :]
        du_ref[...] = acc.astype(BF16)

    nxt = pl.BlockSpec((None, 8, dc), lambda bb, i: (bb, jnp.minimum((i + 1) * (ts // 8), s // 8 - 1), 0))
    return pl.pallas_call(
        body, name=name, grid=(b, nt),
        in_specs=[_row(ts, dc), nxt, _full((4, dc))],
        out_specs=_row(ts, dc), out_shape=SDS((b, s, dc), BF16),
        scratch_shapes=[pltpu.VMEM((ts + 8, dc), F32)], compiler_params=_cparams(2))(dv, dv, cw)


def _ssd_common(misc_ref, dtb_ref, alog_ref, e_ref):
    ln = CHUNK
    lane = lax.broadcasted_iota(I32, (ln, LANES), 1)
    lane1 = lax.broadcasted_iota(I32, (1, LANES), 1)
    pre = misc_ref[...] + dtb_ref[...]
    dt_s = jnp.where(lane < SSD_HEADS, _softplus(pre), 0.0)
    a_neg = jnp.where(lane1 < SSD_HEADS, -jnp.exp(alog_ref[...]), 0.0)
    ri = lax.broadcasted_iota(I32, (ln, ln), 0)
    ci = lax.broadcasted_iota(I32, (ln, ln), 1)
    tril = ci <= ri
    acum = jnp.dot(tril.astype(F32), dt_s * a_neg, preferred_element_type=F32, precision=HI)
    e = e_ref[...]
    expand = lambda t: jnp.dot(t, e, preferred_element_type=F32, precision=HI)
    last = acum[ln - 1:ln, :]
    delta_s = jnp.exp(last - acum)
    return dict(pre=pre, dt_s=dt_s, a_neg=a_neg, tril=tril, ri=ri, ci=ci, acum=acum, acum_t=acum.T,
                dt_e=expand(dt_s), eac_e=expand(jnp.exp(acum)), delta_s=delta_s, del_e=expand(delta_s))


def _decay(cm, h):
    seg = cm["acum"][:, h:h + 1] - cm["acum_t"][h:h + 1, :]
    return jnp.exp(jnp.where(cm["tril"], seg, -jnp.inf))


def ssd_fwd(xs, bm, cm_, misc, z, dtb, alog, dskip_e, norm_w, e_mat, name):
    b, s, _ = xs.shape
    ln, nc = CHUNK, s // CHUNK
    gw = D_SSD // SSD_GROUPS
    hpg = SSD_HEADS // SSD_GROUPS

    def body(xs_ref, b_ref, c_ref, misc_ref, z_ref, dtb_ref, alog_ref, dsk_ref, nw_ref, e_ref,
             ys_ref, y_ref, p_ref, st, yd):
        @pl.when(pl.program_id(1) == 0)
        def _():
            st[...] = jnp.zeros_like(st)
        cm = _ssd_common(misc_ref, dtb_ref, alog_ref, e_ref)
        xsv = xs_ref[...]
        xdt = xsv * cm["dt_e"]
        xdt_b = xdt.astype(BF16)
        xd_b = (xdt * cm["del_e"]).astype(BF16)
        gam_e = cm["eac_e"][ln - 1:ln, :]
        p_ref[...] = st[...]
        yoff = []
        for g in range(SSD_GROUPS):
            gs = slice(gw * g, gw * (g + 1))
            bg = b_ref[:, SSD_STATE * g:SSD_STATE * (g + 1)].astype(BF16)
            cg = c_ref[:, SSD_STATE * g:SSD_STATE * (g + 1)].astype(BF16)
            cb = lax.dot_general(cg, bg, NT_DIMS, preferred_element_type=F32)
            st_g = st[:, gs]
            yoff.append(jnp.dot(cg, st_g.astype(BF16), preferred_element_type=F32) * cm["eac_e"][:, gs])
            for j in range(hpg):
                h = hpg * g + j
                hs = slice(SSD_HEAD_DIM * h, SSD_HEAD_DIM * (h + 1))
                m = (cb * _decay(cm, h)).astype(BF16)
                yd[:, hs] = jnp.dot(m, xdt_b[:, hs], preferred_element_type=F32)
            new = lax.dot_general(bg, xd_b[:, gs], TN_DIMS, preferred_element_type=F32)
            st[:, gs] = st_g * gam_e[:, gs] + new
        y = yd[...] + jnp.concatenate(yoff, axis=1) + dsk_ref[...] * xsv
        y_ref[...] = y
        zz = z_ref[...]
        yg = y * (zz * _sigmoid(zz))
        outs = []
        for g in range(SSD_GROUPS):
            ygg = yg[:, gw * g:gw * (g + 1)]
            outs.append(ygg * _rms(ygg) * nw_ref[:, gw * g:gw * (g + 1)])
        ys_ref[...] = jnp.concatenate(outs, axis=1).astype(BF16)

    row = lambda d: pl.BlockSpec((None, ln, d), lambda bb, c: (bb, c, 0))
    return pl.pallas_call(
        body, name=name, grid=(b, nc),
        in_specs=[row(D_SSD), row(256), row(256), row(LANES), row(D_SSD), _full((1, LANES)), _full((1, LANES)),
                  _full((1, D_SSD)), _full((1, D_SSD)), _full((LANES, D_SSD))],
        out_specs=[row(D_SSD), row(D_SSD), pl.BlockSpec((None, None, SSD_STATE, D_SSD), lambda bb, c: (bb, c, 0, 0))],
        out_shape=[SDS((b, s, D_SSD), BF16), SDS((b, s, D_SSD), F32), SDS((b, nc, SSD_STATE, D_SSD), F32)],
        scratch_shapes=[pltpu.VMEM((SSD_STATE, D_SSD), F32), pltpu.VMEM((ln, D_SSD), F32)],
        compiler_params=_cparams(2))(xs, bm, cm_, misc, z, dtb, alog, dskip_e, norm_w, e_mat)


def ssd_bwd(dys, y, z, xs, bm, cm_, misc, prev, dtb, alog, dskip_e, norm_w, e_mat, et_mat, name):
    b, s, _ = xs.shape
    ln, nc = CHUNK, s // CHUNK
    gw = D_SSD // SSD_GROUPS
    hpg = SSD_HEADS // SSD_GROUPS

    def body(dys_ref, y_ref, z_ref, xs_ref, b_ref, c_ref, misc_ref, p_ref, dtb_ref, alog_ref, dsk_ref, nw_ref,
             e_ref, et_ref, dxs_ref, db_ref, dc_ref, dz_ref, ddt_ref, dnw_ref, ddsk_ref, ddtb_ref, dalog_ref,
             dst, dxd, dac_t):
        @pl.when(_first_step())
        def _():
            for r_ in (dnw_ref, ddsk_ref, ddtb_ref, dalog_ref):
                r_[...] = jnp.zeros_like(r_)

        @pl.when(pl.program_id(1) == 0)
        def _():
            dst[...] = jnp.zeros_like(dst)

        cm = _ssd_common(misc_ref, dtb_ref, alog_ref, e_ref)
        et = et_ref[...]
        squeeze = lambda t: jnp.dot(t, et, preferred_element_type=F32, precision=HI)
        lane = lax.broadcasted_iota(I32, (ln, LANES), 1)
        sub = lax.broadcasted_iota(I32, (LANES, ln), 0)
        xsv = xs_ref[...]
        xdt = xsv * cm["dt_e"]
        xdt_b = xdt.astype(BF16)
        xd_b = (xdt * cm["del_e"]).astype(BF16)
        eac_e = cm["eac_e"]
        gam_e = eac_e[ln - 1:ln, :]

        yv, zz, dyo = y_ref[...], z_ref[...], dys_ref[...]
        sz = _sigmoid(zz)
        silu_z = zz * sz
        yg = yv * silu_z
        dyg, dnw = [], []
        for g in range(SSD_GROUPS):
            gs = slice(gw * g, gw * (g + 1))
            ygg = yg[:, gs]
            r = _rms(ygg)
            n = ygg * r
            dnw.append(jnp.sum(dyo[:, gs] * n, axis=0, keepdims=True))
            dyg.append(_rms_bwd(dyo[:, gs] * nw_ref[:, gs], n, r))
        dyg = jnp.concatenate(dyg, axis=1)
        dnw_ref[...] += jnp.concatenate(dnw, axis=1)
        dz_ref[...] = (dyg * yv * (sz * (1.0 + zz * (1.0 - sz)))).astype(BF16)
        dy = dyg * silu_z
        ddsk_ref[...] += jnp.sum(dy * xsv, axis=0, keepdims=True)
        dy_b = dy.astype(BF16)

        dacum = jnp.zeros((ln, LANES), F32)
        dac_t[...] = jnp.zeros_like(dac_t)
        w1, dgam = [], []
        for g in range(SSD_GROUPS):
            gs = slice(gw * g, gw * (g + 1))
            ss = slice(SSD_STATE * g, SSD_STATE * (g + 1))
            bg = b_ref[:, ss].astype(BF16)
            cg = c_ref[:, ss].astype(BF16)
            cb = lax.dot_general(cg, bg, NT_DIMS, preferred_element_type=F32)
            pt = p_ref[:, gs]
            pt_b = pt.astype(BF16)
            dst_g = dst[:, gs]
            dst_b = dst_g.astype(BF16)
            edy = (dy[:, gs] * eac_e[:, gs]).astype(BF16)
            dcg = lax.dot_general(edy, pt_b, NT_DIMS, preferred_element_type=F32)
            dpt = lax.dot_general(cg, edy, TN_DIMS, preferred_element_type=F32)
            yoff = jnp.dot(cg, pt_b, preferred_element_type=F32) * eac_e[:, gs]
            dxd_g = jnp.dot(bg, dst_b, preferred_element_type=F32)
            dbg = lax.dot_general(xd_b[:, gs], dst_b, NT_DIMS, preferred_element_type=F32)
            ddel = dxd_g * xdt[:, gs] * cm["del_e"][:, gs]
            w1.append(dy[:, gs] * yoff - ddel)
            dgam.append(jnp.sum(ddel, axis=0, keepdims=True) + jnp.sum(dst_g * pt, axis=0, keepdims=True) * gam_e[:, gs])
            dxd[:, gs] = dxd_g * cm["del_e"][:, gs]
            dst[:, gs] = dst_g * gam_e[:, gs] + dpt
            dcb = jnp.zeros((ln, ln), F32)
            for j in range(hpg):
                h = hpg * g + j
                hs = slice(SSD_HEAD_DIM * h, SSD_HEAD_DIM * (h + 1))
                lam = _decay(cm, h)
                m = cb * lam
                dm = lax.dot_general(dy_b[:, hs], xdt_b[:, hs], NT_DIMS, preferred_element_type=F32)
                dxd[:, hs] += lax.dot_general(m.astype(BF16), dy_b[:, hs], TN_DIMS, preferred_element_type=F32)
                dcb += dm * lam
                wl = dm * m
                dacum += jnp.where(lane == h, jnp.sum(wl, axis=1, keepdims=True), 0.0)
                dac_t[...] -= jnp.where(sub == h, jnp.sum(wl, axis=0, keepdims=True), 0.0)
            dcb_b = dcb.astype(BF16)
            dc_ref[:, ss] = dcg + jnp.dot(dcb_b, bg, preferred_element_type=F32)
            db_ref[:, ss] = dbg + lax.dot_general(dcb_b, cg, TN_DIMS, preferred_element_type=F32)

        dxdt = dxd[...]
        dxs_ref[...] = dy * dsk_ref[...] + dxdt * cm["dt_e"]
        dacum += squeeze(jnp.concatenate(w1, axis=1)) + dac_t[...].T
        dlast = squeeze(jnp.broadcast_to(jnp.concatenate(dgam, axis=1), (8, D_SSD)))[0:1, :]
        dacum += jnp.where(lax.broadcasted_iota(I32, (ln, LANES), 0) == ln - 1, dlast, 0.0)
        triu = (cm["ci"] >= cm["ri"]).astype(F32)
        da = jnp.dot(triu, dacum, preferred_element_type=F32, precision=HI)
        ddt = da * cm["a_neg"] + squeeze(dxdt * xsv)
        dalog_ref[...] += jnp.sum(da * cm["dt_s"], axis=0, keepdims=True) * cm["a_neg"]
        ddt_raw = jnp.where(lane < SSD_HEADS, ddt * _sigmoid(cm["pre"]), 0.0)
        ddt_ref[...] = ddt_raw
        ddtb_ref[...] += jnp.sum(ddt_raw, axis=0, keepdims=True)

    row = lambda d: pl.BlockSpec((None, ln, d), lambda bb, c: (bb, nc - 1 - c, 0))
    return pl.pallas_call(
        body, name=name, grid=(b, nc),
        in_specs=[row(D_SSD), row(D_SSD), row(D_SSD), row(D_SSD), row(256), row(256), row(LANES),
                  pl.BlockSpec((None, None, SSD_STATE, D_SSD), lambda bb, c: (bb, nc - 1 - c, 0, 0)),
                  _full((1, LANES)), _full((1, LANES)), _full((1, D_SSD)), _full((1, D_SSD)),
                  _full((LANES, D_SSD)), _full((D_SSD, LANES))],
        out_specs=[row(D_SSD), row(256), row(256), row(D_SSD), row(LANES),
                   _full((1, D_SSD)), _full((1, D_SSD)), _full((1, LANES)), _full((1, LANES))],
        out_shape=[SDS((b, s, D_SSD), F32), SDS((b, s, 256), F32), SDS((b, s, 256), F32), SDS((b, s, D_SSD), BF16),
                   SDS((b, s, LANES), F32), SDS((1, D_SSD), F32), SDS((1, D_SSD), F32), SDS((1, LANES), F32),
                   SDS((1, LANES), F32)],
        scratch_shapes=[pltpu.VMEM((SSD_STATE, D_SSD), F32), pltpu.VMEM((ln, D_SSD), F32), pltpu.VMEM((LANES, ln), F32)],
        compiler_params=_cparams(2))(dys, y, z, xs, bm, cm_, misc, prev, dtb, alog, dskip_e, norm_w, e_mat, et_mat)


def _rope(xv, cc, sp, sm):
    n = xv.shape[1]
    return xv * cc + pltpu.roll(xv, 16, 1) * sp + pltpu.roll(xv, n - 16, 1) * sm


def _rope_bwd(dy, cc, sp, sm):
    n = dy.shape[1]
    return dy * cc + pltpu.roll(dy * sp, n - 16, 1) + pltpu.roll(dy * sm, 16, 1)


def _tile8(t):
    return jnp.concatenate([t] * MLA_HEADS, axis=1)


def qkv_fwd(cq, ckv, misc, cc, sp, sm, qnw, kvnw, wuq_t, wukv_t, place, name):
    b, s, _ = cq.shape
    tm = min(256, s)
    hd = MLA_HEADS * HEAD_PAD

    def body(cq_ref, ckv_ref, misc_ref, cc_ref, sp_ref, sm_ref, qnw_ref, kvnw_ref, wq_ref, wkv_ref, pl_ref,
             q_ref, k_ref, v_ref, qn_ref, kvn_ref):
        cqv, ckvv = cq_ref[...], ckv_ref[...]
        qn = (cqv * _rms(cqv) * qnw_ref[...]).astype(BF16)
        kvn = (ckvv * _rms(ckvv) * kvnw_ref[...]).astype(BF16)
        qn_ref[...] = qn
        kvn_ref[...] = kvn
        cc1, sp1, sm1 = cc_ref[...], sp_ref[...], sm_ref[...]
        q = lax.dot_general(qn, wq_ref[...], NT_DIMS, preferred_element_type=F32)
        q_ref[...] = _rope(q, _tile8(cc1), _tile8(sp1), _tile8(sm1)).astype(BF16)
        kv = lax.dot_general(kvn, wkv_ref[...], NT_DIMS, preferred_element_type=F32)
        kr = jnp.dot(misc_ref[...], pl_ref[...], preferred_element_type=F32, precision=HI)
        kr = _rope(kr, cc1, sp1, sm1)
        k_ref[...] = (kv[:, 0:hd] + _tile8(kr)).astype(BF16)
        v_ref[...] = kv[:, hd:2 * hd].astype(BF16)

    return pl.pallas_call(
        body, name=name, grid=(b, s // tm),
        in_specs=[_row(tm, Q_LORA), _row(tm, KV_LORA), _row(tm, LANES), _row(tm, LANES), _row(tm, LANES), _row(tm, LANES),
                  _full((1, Q_LORA)), _full((1, KV_LORA)), _full(wuq_t.shape), _full(wukv_t.shape), _full((LANES, LANES))],
        out_specs=[_row(tm, hd), _row(tm, hd), _row(tm, hd), _row(tm, Q_LORA), _row(tm, KV_LORA)],
        out_shape=[SDS((b, s, hd), BF16)] * 3 + [SDS((b, s, Q_LORA), BF16), SDS((b, s, KV_LORA), BF16)],
        compiler_params=_cparams(2))(cq, ckv, misc, cc, sp, sm, qnw, kvnw, wuq_t, wukv_t, place)


def qkv_bwd(dq, dk, dv, ddt, cq, ckv, cc, sp, sm, qnw, kvnw, wuq_t, wukv_t, place_t, name):
    b, s, _ = cq.shape
    tm = min(256, s)
    hd = MLA_HEADS * HEAD_PAD

    def body(dq_ref, dk_ref, dv_ref, ddt_ref, cq_ref, ckv_ref, cc_ref, sp_ref, sm_ref, qnw_ref, kvnw_ref,
             wq_ref, wkv_ref, plt_ref, dcq_ref, dckv_ref, dmisc_ref, dqp_ref, dkv_ref, dqnw_ref, dkvnw_ref):
        @pl.when(_first_step())
        def _():
            dqnw_ref[...] = jnp.zeros_like(dqnw_ref)
            dkvnw_ref[...] = jnp.zeros_like(dkvnw_ref)
        cc1, sp1, sm1 = cc_ref[...], sp_ref[...], sm_ref[...]
        dqp = _rope_bwd(dq_ref[...], _tile8(cc1), _tile8(sp1), _tile8(sm1)).astype(BF16)
        dqp_ref[...] = dqp
        dkf = dk_ref[...]
        dkv_b = jnp.concatenate([dkf, dv_ref[...]], axis=1).astype(BF16)
        dkv_ref[...] = dkv_b
        dkr = dkf[:, 0:HEAD_PAD]
        for h in range(1, MLA_HEADS):
            dkr += dkf[:, HEAD_PAD * h:HEAD_PAD * (h + 1)]
        dkr = _rope_bwd(dkr, cc1, sp1, sm1)
        dmisc_ref[...] = (jnp.dot(dkr, plt_ref[...], preferred_element_type=F32, precision=HI) + ddt_ref[...]).astype(BF16)

        def norm_bwd(dn_w, xv, w_ref, dw_ref, dx_ref):
            r = _rms(xv)
            n = xv * r
            dw_ref[...] += jnp.sum(dn_w * n, axis=0, keepdims=True)
            dx_ref[...] = _rms_bwd(dn_w * w_ref[...], n, r).astype(BF16)

        norm_bwd(jnp.dot(dqp, wq_ref[...], preferred_element_type=F32), cq_ref[...], qnw_ref, dqnw_ref, dcq_ref)
        norm_bwd(jnp.dot(dkv_b, wkv_ref[...], preferred_element_type=F32), ckv_ref[...], kvnw_ref, dkvnw_ref, dckv_ref)

    return pl.pallas_call(
        body, name=name, grid=(b, s // tm),
        in_specs=[_row(tm, hd), _row(tm, hd), _row(tm, hd), _row(tm, LANES), _row(tm, Q_LORA), _row(tm, KV_LORA),
                  _row(tm, LANES), _row(tm, LANES), _row(tm, LANES), _full((1, Q_LORA)), _full((1, KV_LORA)),
                  _full(wuq_t.shape), _full(wukv_t.shape), _full((LANES, LANES))],
        out_specs=[_row(tm, Q_LORA), _row(tm, KV_LORA), _row(tm, LANES), _row(tm, hd), _row(tm, 2 * hd),
                   _full((1, Q_LORA)), _full((1, KV_LORA))],
        out_shape=[SDS((b, s, Q_LORA), BF16), SDS((b, s, KV_LORA), BF16), SDS((b, s, LANES), BF16),
                   SDS((b, s, hd), BF16), SDS((b, s, 2 * hd), BF16), SDS((1, Q_LORA), F32), SDS((1, KV_LORA), F32)],
        compiler_params=_cparams(2))(dq, dk, dv, ddt, cq, ckv, cc, sp, sm, qnw, kvnw, wuq_t, wukv_t, place_t)


ATT_SCALE = 1.0 / math.sqrt(QK_DIM)


ATT_HEADS_PER_STEP = 2


def _att_tile(s):
    return min(512, s)


def flash_fwd(q, k, v, name):
    b, s, hd = q.shape
    t = _att_tile(s)
    nb = s // t
    th = t // 2
    vt = v.reshape(b, nb, t, MLA_HEADS, HEAD_PAD).transpose(0, 3, 1, 4, 2)

    hps = ATT_HEADS_PER_STEP
    hw = hps * HEAD_PAD

    def body(q_ref, k_ref, vt_ref, o_ref, lse_ref, m_s, l_s, acc):
        i = pl.program_id(2)
        m_s[...] = jnp.full_like(m_s, -jnp.inf)
        l_s[...] = jnp.zeros_like(l_s)
        acc[...] = jnp.zeros_like(acc)

        def update(j, diagonal):
            ks = pl.ds(pl.multiple_of(j * t, t), t)
            chains = [(hh, half) for hh in range(hps) for half in range(2)]
            lanes = lambda hh: slice(HEAD_PAD * hh, HEAD_PAD * (hh + 1))
            cols = lambda half: slice(th * half, th * (half + 1))
            sts = {}
            for hh, half in chains:
                st = lax.dot_general(k_ref[ks, lanes(hh)], q_ref[cols(half), lanes(hh)], NT_DIMS,
                                     preferred_element_type=F32) * ATT_SCALE
                if diagonal:
                    row = lax.broadcasted_iota(I32, (t, th), 0)
                    col = lax.broadcasted_iota(I32, (t, th), 1) + th * half
                    st = jnp.where(row <= col, st, -jnp.inf)
                sts[hh, half] = st
            pts, alphas = {}, {}
            for hh, half in chains:
                st, cs = sts[hh, half], cols(half)
                m_prev = m_s[hh, :, cs]
                m_new = jnp.maximum(m_prev, jnp.max(st, axis=0, keepdims=True))
                alpha = jnp.exp(m_prev - m_new)
                pt = jnp.exp(st - m_new)
                l_s[hh, :, cs] = alpha * l_s[hh, :, cs] + jnp.sum(pt, axis=0, keepdims=True)
                m_s[hh, :, cs] = m_new
                pts[hh, half], alphas[hh, half] = pt.astype(BF16), alpha
            for hh, half in chains:
                cs = cols(half)
                acc[hh, :, cs] = alphas[hh, half] * acc[hh, :, cs] + jnp.dot(vt_ref[hh, j], pts[hh, half],
                                                                             preferred_element_type=F32)

        def step(j, carry):
            update(j, False)
            return carry

        lax.fori_loop(0, i, step, 0)
        update(i, True)
        for hh in range(hps):
            o_ref[:, HEAD_PAD * hh:HEAD_PAD * (hh + 1)] = (acc[hh] / l_s[hh]).T
            lse_ref[hh] = m_s[hh] + jnp.log(l_s[hh])

    qs = pl.BlockSpec((None, t, hw), lambda bb, h, i: (bb, i, h))
    ks = pl.BlockSpec((None, s, hw), lambda bb, h, i: (bb, 0, h))
    vs = pl.BlockSpec((None, hps, nb, HEAD_PAD, t), lambda bb, h, i: (bb, h, 0, 0, 0))
    ls = pl.BlockSpec((None, hps, None, 1, t), lambda bb, h, i: (bb, h, i, 0, 0))
    return pl.pallas_call(
        body, name=name, grid=(b, MLA_HEADS // hps, nb),
        in_specs=[qs, ks, vs], out_specs=[qs, ls],
        out_shape=[SDS((b, s, hd), F32), SDS((b, MLA_HEADS, nb, 1, t), F32)],
        scratch_shapes=[pltpu.VMEM((hps, 1, t), F32), pltpu.VMEM((hps, 1, t), F32), pltpu.VMEM((hps, HEAD_PAD, t), F32)],
        compiler_params=_cparams(3))(q, k, vt)


def flash_bwd(q, k, v, do, lse, dlt, name):
    b, s, hd = q.shape
    t = _att_tile(s)
    nb = s // t
    th = t // 2
    lse_r = lse
    dlt_r = dlt.reshape(b, MLA_HEADS, nb, 1, t)

    hps = ATT_HEADS_PER_STEP
    hw = hps * HEAD_PAD

    def body(q_ref, k_ref, v_ref, do_ref, lse_ref, dlt_ref, dq_ref, dk_ref, dv_ref):
        dq_ref[...] = jnp.zeros_like(dq_ref)
        dk_ref[...] = jnp.zeros_like(dk_ref)
        dv_ref[...] = jnp.zeros_like(dv_ref)

        def tile(j, i, diagonal):
            qs = pl.ds(pl.multiple_of(i * t, t), t)
            chains = [(hh, half) for hh in range(hps) for half in range(2)]
            lanes = lambda hh: slice(HEAD_PAD * hh, HEAD_PAD * (hh + 1))
            keys = lambda half: pl.ds(pl.multiple_of(j * t + th * half, th), th)
            sts, dpts = {}, {}
            for hh, half in chains:
                ls_, ks = lanes(hh), keys(half)
                st = lax.dot_general(k_ref[ks, ls_], q_ref[qs, ls_], NT_DIMS, preferred_element_type=F32) * ATT_SCALE
                if diagonal:
                    row = lax.broadcasted_iota(I32, (th, t), 0) + th * half
                    col = lax.broadcasted_iota(I32, (th, t), 1)
                    st = jnp.where(row <= col, st, -jnp.inf)
                sts[hh, half] = st
                dpts[hh, half] = lax.dot_general(v_ref[ks, ls_], do_ref[qs, ls_], NT_DIMS, preferred_element_type=F32)
            pts, dsts = {}, {}
            for hh, half in chains:
                pt = jnp.exp(sts[hh, half] - lse_ref[hh, i])
                pts[hh, half] = pt.astype(BF16)
                dsts[hh, half] = (pt * (dpts[hh, half] - dlt_ref[hh, i]) * ATT_SCALE).astype(BF16)
            for hh in range(hps):
                ls_ = lanes(hh)
                dq_acc = None
                for half in range(2):
                    ks = keys(half)
                    dv_ref[ks, ls_] += jnp.dot(pts[hh, half], do_ref[qs, ls_], preferred_element_type=F32)
                    dk_ref[ks, ls_] += jnp.dot(dsts[hh, half], q_ref[qs, ls_], preferred_element_type=F32)
                    part = lax.dot_general(dsts[hh, half], k_ref[ks, ls_], TN_DIMS, preferred_element_type=F32)
                    dq_acc = part if dq_acc is None else dq_acc + part
                dq_ref[qs, ls_] += dq_acc

        def key_tile(j, carry):
            tile(j, j, True)

            def query_tile(i, c2):
                tile(j, i, False)
                return c2

            lax.fori_loop(j + 1, nb, query_tile, 0)
            return carry

        lax.fori_loop(0, nb, key_tile, 0)

    hs = pl.BlockSpec((None, s, hw), lambda bb, h: (bb, 0, h))
    ls = pl.BlockSpec((None, hps, nb, 1, t), lambda bb, h: (bb, h, 0, 0, 0))
    return pl.pallas_call(
        body, name=name, grid=(b, MLA_HEADS // hps),
        in_specs=[hs, hs, hs, hs, ls, ls], out_specs=[hs, hs, hs],
        out_shape=[SDS((b, s, hd), F32)] * 3, compiler_params=_cparams(2))(q, k, v, do, lse_r, dlt_r)


def out_proj(ys, attn, mnw, wo, x, gate, name):
    b, s, d = x.shape
    tm = min(256, s)

    def body(ys_ref, at_ref, mnw_ref, wo_ref, x_ref, g_ref, xn_ref, o_ref, ym_ref):
        av = at_ref[...]
        ym = (av * _rms(av) * mnw_ref[...]).astype(BF16)
        ym_ref[...] = ym
        o = jnp.dot(ys_ref[...], wo_ref[0:D_SSD, :], preferred_element_type=F32)
        o += jnp.dot(ym, wo_ref[D_SSD:2 * D_SSD, :], preferred_element_type=F32)
        xn_ref[...] = x_ref[...] + g_ref[...] * o
        o_ref[...] = o.astype(BF16)

    return pl.pallas_call(
        body, name=name, grid=(b, s // tm),
        in_specs=[_row(tm, D_SSD), _row(tm, D_SSD), _full((1, D_SSD)), _full(wo.shape), _row(tm, d), _bvec(d)],
        out_specs=[_row(tm, d), _row(tm, d), _row(tm, D_SSD)],
        out_shape=[SDS((b, s, d), F32), SDS((b, s, d), BF16), SDS((b, s, D_SSD), BF16)],
        compiler_params=_cparams(2))(ys, attn, mnw, wo, x, gate)


def out_proj_bwd(dout, attn, mnw, wo, name):
    b, s, d = dout.shape
    tm = min(256, s)

    def body(do_ref, at_ref, mnw_ref, wo_ref, dys_ref, dat_ref, dlt_ref, dw_ref):
        @pl.when(_first_step())
        def _():
            dw_ref[...] = jnp.zeros_like(dw_ref)
        dov = do_ref[...]
        dys_ref[...] = lax.dot_general(dov, wo_ref[0:D_SSD, :], NT_DIMS, preferred_element_type=F32)
        dym = lax.dot_general(dov, wo_ref[D_SSD:2 * D_SSD, :], NT_DIMS, preferred_element_type=F32)
        av = at_ref[...]
        r = _rms(av)
        n = av * r
        dw_ref[...] += jnp.sum(dym * n, axis=0, keepdims=True)
        dat = _rms_bwd(dym * mnw_ref[...], n, r)
        dat_ref[...] = dat.astype(BF16)
        prod = dat * av
        for h in range(MLA_HEADS):
            dlt_ref[h] = jnp.sum(prod[:, HEAD_PAD * h:HEAD_PAD * (h + 1)], axis=1, keepdims=True)

    return pl.pallas_call(
        body, name=name, grid=(b, s // tm),
        in_specs=[_row(tm, d), _row(tm, D_SSD), _full((1, D_SSD)), _full(wo.shape)],
        out_specs=[_row(tm, D_SSD), _row(tm, D_SSD),
                   pl.BlockSpec((None, MLA_HEADS, tm, 1), lambda bb, i: (bb, 0, i, 0)), _full((1, D_SSD))],
        out_shape=[SDS((b, s, D_SSD), F32), SDS((b, s, D_SSD), BF16), SDS((b, MLA_HEADS, s, 1), F32),
                   SDS((1, D_SSD), F32)],
        compiler_params=_cparams(2))(dout, attn, mnw, wo)


def adaln_fwd(c_all, w_ada, b_ada, name):
    nb, d = c_all.shape
    n = w_ada.shape[1]

    def body(c_ref, w_ref, b_ref, m_ref, ca_ref):
        cv = c_ref[...]
        ca = (cv * _sigmoid(cv)).astype(BF16)
        ca_ref[...] = ca
        m_ref[...] = jnp.dot(ca, w_ref[...].astype(BF16), preferred_element_type=F32) + b_ref[...]

    return pl.pallas_call(
        body, name=name, out_shape=[SDS((nb, n), F32), SDS((nb, d), BF16)],
        compiler_params=pltpu.CompilerParams(vmem_limit_bytes=VMEM_LIMIT))(c_all, w_ada, b_ada)


def adaln_bwd(c_act, dmod_cols, name):
    d, n = c_act.shape[1], dmod_cols.shape[1]

    def body(c_ref, dm_ref, gw_ref):
        gw_ref[...] = lax.dot_general(c_ref[...], dm_ref[...].astype(BF16), TN_DIMS, preferred_element_type=F32)

    return pl.pallas_call(
        body, name=name, out_shape=SDS((d, n), F32),
        compiler_params=pltpu.CompilerParams(vmem_limit_bytes=VMEM_LIMIT))(c_act, dmod_cols)


def sum_rows(x, name):
    def body(x_ref, o_ref):
        o_ref[...] = jnp.sum(x_ref[...], axis=0, keepdims=True)
    return pl.pallas_call(body, name=name, out_shape=SDS((1, x.shape[1]), F32))(x)


def squeeze_heads(x, et_mat, name):
    def body(x_ref, et_ref, o_ref):
        xv = jnp.broadcast_to(x_ref[...], (8, x.shape[1]))
        o_ref[...] = jnp.dot(xv, et_ref[...], preferred_element_type=F32, precision=HI)[0:1, :]
    return pl.pallas_call(body, name=name, out_shape=SDS((1, LANES), F32))(x, et_mat)


def sum_blocks(x, name):
    n, r, c = x.shape

    def body(x_ref, o_ref):
        acc = x_ref[0].astype(F32)
        for k in range(1, n):
            acc += x_ref[k].astype(F32)
        o_ref[...] = acc

    return pl.pallas_call(body, name=name, out_shape=SDS((r, c), F32),
                          compiler_params=pltpu.CompilerParams(vmem_limit_bytes=VMEM_LIMIT))(x)


def _adam_math(w, g, m, v):
    m = ADAM_B1 * m + (1.0 - ADAM_B1) * g
    v = ADAM_B2 * v + (1.0 - ADAM_B2) * (g * g)
    m_hat = m / (1.0 - ADAM_B1 ** ADAM_STEP)
    v_hat = v / (1.0 - ADAM_B2 ** ADAM_STEP)
    return -ADAM_LR * (m_hat / (jnp.sqrt(v_hat) + ADAM_EPS) + ADAM_WD * w), m, v


def adamw(w, g, m, v, name):
    r, c = w.shape
    tr = r
    for cand in (512, 256, 128, 64, 32, 16, 8):
        if r % cand == 0 and cand * c * 4 <= 2 * 1024 * 1024:
            tr = cand
            break

    def body(w_ref, g_ref, m_ref, v_ref, d_ref, mo_ref, vo_ref):
        d_ref[...], mo_ref[...], vo_ref[...] = _adam_math(w_ref[...], g_ref[...], m_ref[...], v_ref[...])

    spec = pl.BlockSpec((tr, c), lambda i: (i, 0))
    return pl.pallas_call(
        body, name=name, grid=(r // tr,), in_specs=[spec] * 4, out_specs=[spec] * 3,
        out_shape=[SDS((r, c), F32)] * 3, compiler_params=_cparams(1))(w, g, m, v)


def adamw_sum8(w, gparts, m, v, name):
    r, c = w.shape
    tr = next(cand for cand in (64, 32, 16, 8) if r % cand == 0)

    def body(w_ref, gp_ref, m_ref, v_ref, g_ref, d_ref, mo_ref, vo_ref):
        g = gp_ref[0].astype(F32)
        for k in range(1, N_DEV):
            g += gp_ref[k].astype(F32)
        g_ref[...] = g
        d_ref[...], mo_ref[...], vo_ref[...] = _adam_math(w_ref[...], g, m_ref[...], v_ref[...])

    spec = pl.BlockSpec((tr, c), lambda i: (i, 0))
    gspec = pl.BlockSpec((N_DEV, tr, c), lambda i: (0, i, 0))
    return pl.pallas_call(
        body, name=name, grid=(r // tr,), in_specs=[spec, gspec, spec, spec], out_specs=[spec] * 4,
        out_shape=[SDS((r, c), F32)] * 4, compiler_params=_cparams(1))(w, gparts, m, v)


PACK = {"ffn1_w_gate": (352, 352), "ffn1_w_up": (352, 352), "ffn1_w_down": (352, 352),
        "ffn2_w_gate": (352, 352), "ffn2_w_up": (352, 352), "ffn2_w_down": (352, 352),
        "w_out": (256, 256), "w_in": (406, 416), "w_ukv": (48, 48), "w_uq": (36, 48)}
TRANSPOSED = ("ffn1_w_gate", "ffn1_w_up", "ffn2_w_gate", "ffn2_w_up", "w_in", "w_ukv", "w_uq")
GROUP_FFN1 = ("ffn1_w_gate", "ffn1_w_up", "ffn1_w_down")
GROUP_REST = ("ffn2_w_gate", "ffn2_w_up", "ffn2_w_down", "w_out", "w_in", "w_ukv", "w_uq")
GRAD_GROUPS = (("ffn2", ("ffn2_w_gate", "ffn2_w_up", "ffn2_w_down")), ("mixer", ("w_out", "w_in", "w_ukv", "w_uq")),
               ("ffn1_down", ("ffn1_w_down",)), ("ffn1_up", ("ffn1_w_gate", "ffn1_w_up")))


def _pack_offsets(names):
    off, o = {}, 0
    for n in names:
        off[n] = (o, PACK[n][0])
        o += PACK[n][1]
    return off, o


def _shard_to_rows(name, w):
    w = w[0]
    if name in TRANSPOSED:
        w = w.T
    return w.reshape(-1, D_MODEL)


def _rows_to_shard(name, rows, like):
    shp = like.shape[1:]
    if name in TRANSPOSED:
        return rows.reshape(shp[1], shp[0]).T[None]
    return rows.reshape(shp)[None]


def _pack_shards(ws, names, dtype):
    parts = []
    for name in names:
        real, padded = PACK[name]
        rows = _shard_to_rows(name, ws[name]).astype(dtype)
        if padded > real:
            rows = jnp.pad(rows, ((0, padded - real), (0, 0)))
        parts.append(rows)
    return jnp.concatenate(parts, axis=0)


def _grad_rows(name, gw):
    real, padded = PACK[name]
    if name == "w_in":
        rows = _in_proj_rows_inv(gw).reshape(N_DEV, -1, D_MODEL)
    elif name == "w_ukv":
        hd = MLA_HEADS * HEAD_PAD
        rows = jnp.concatenate([gw[:hd].reshape(MLA_HEADS, HEAD_PAD, KV_LORA)[:, :QK_NOPE],
                                gw[hd:].reshape(MLA_HEADS, V_HEAD, KV_LORA)], axis=1).reshape(N_DEV, -1, D_MODEL)
    elif name == "w_uq":
        rows = gw.reshape(MLA_HEADS, HEAD_PAD, Q_LORA)[:, :QK_DIM].reshape(N_DEV, -1, D_MODEL)
    else:
        rows = gw.reshape(N_DEV, -1, D_MODEL)
    if padded > real:
        rows = jnp.pad(rows, ((0, 0), (0, padded - real), (0, 0)))
    return rows


def _pack_rows(arrs):
    parts = []
    for a in arrs:
        flat = a.reshape(-1).astype(F32)
        pad = (-flat.shape[0]) % D_MODEL
        if pad:
            flat = jnp.pad(flat, (0, pad))
        parts.append(flat.reshape(-1, D_MODEL))
    out = jnp.concatenate(parts, axis=0)
    pad = (-out.shape[0]) % 8
    if pad:
        out = jnp.pad(out, ((0, pad), (0, 0)))
    return out


def _unpack_rows(packed, shapes):
    out, row = [], 0
    for shp in shapes:
        n = math.prod(shp)
        nrow = -(-n // D_MODEL)
        out.append(packed[row:row + nrow].reshape(-1)[:n].reshape(shp))
        row += nrow
    return out


def _in_proj_rows(w_t):
    return jnp.concatenate([w_t[0:2560], w_t[2576:2960], w_t[2960:3216], w_t[2560:2576], w_t[3216:3248],
                            jnp.zeros((D_IN_PAD - D_IN, D_MODEL), w_t.dtype)], axis=0)


def _in_proj_rows_inv(d):
    return jnp.concatenate([d[0:2560], d[3200:3216], d[2560:2944], d[2944:3200], d[3216:3248]], axis=0)


def _rope_tables(positions):
    inv_freq = ROPE_THETA ** (-jnp.arange(0, QK_ROPE, 2, dtype=F32) / QK_ROPE)
    ang = positions[..., None].astype(F32) * inv_freq
    cos, sin = jnp.cos(ang), jnp.sin(ang)
    one = jnp.ones(ang.shape[:2] + (QK_NOPE,), F32)
    zero = jnp.zeros_like(one)
    z16, z32, o32 = zero[..., :16], zero[..., :32], one[..., :32]
    cc = jnp.concatenate([one, cos, cos, o32], axis=-1)
    sp = jnp.concatenate([zero, z16, sin, z32], axis=-1)
    sm = jnp.concatenate([zero, -sin, z16, z32], axis=-1)
    return cc, sp, sm


def weight_views(g_ffn1, g_rest):
    def _seg(name):
        names, g = (GROUP_FFN1, g_ffn1) if name in GROUP_FFN1 else (GROUP_REST, g_rest)
        o, r = _pack_offsets(names)[0][name]
        return g[:, o:o + r]

    full = lambda name: _seg(name).reshape(-1, D_MODEL)
    ukv = _seg("w_ukv").reshape(MLA_HEADS, QK_NOPE + V_HEAD, KV_LORA)
    wukv_t = jnp.concatenate([jnp.pad(ukv[:, :QK_NOPE], ((0, 0), (0, HEAD_PAD - QK_NOPE), (0, 0))).reshape(-1, KV_LORA),
                              ukv[:, QK_NOPE:].reshape(-1, KV_LORA)], axis=0)
    uq = _seg("w_uq").reshape(MLA_HEADS, QK_DIM, Q_LORA)
    wuq_t = jnp.pad(uq, ((0, 0), (0, HEAD_PAD - QK_DIM), (0, 0))).reshape(-1, Q_LORA)
    return dict(wg1_t=full("ffn1_w_gate"), wu1_t=full("ffn1_w_up"), wd1=full("ffn1_w_down"),
                wg2_t=full("ffn2_w_gate"), wu2_t=full("ffn2_w_up"), wd2=full("ffn2_w_down"),
                wo=full("w_out"), win_t=_in_proj_rows(full("w_in")), wukv_t=wukv_t, wuq_t=wuq_t)


def _ffn_bwd(tag, dxn, x, h, gg, uu, a, o, gate, sc, norm_w, wg_t, wu_t, wd):
    f2 = wd.shape[0] // 2
    do, dgate = gate_bwd(dxn, o, gate, 0.5, tag + "_gate_bwd")
    dgg, duu = ffn_dact(do, wd, gg, uu, tag + "_dact")
    dwd = mm_tn(a, do, f2, D_MODEL, tag + "_dwd")
    dwg_t = mm_tn(dgg, h, f2, D_MODEL, tag + "_dwg")
    dwu_t = mm_tn(duu, h, f2, D_MODEL, tag + "_dwu")
    dx, dsc, dsh, dnw = dh_norm_bwd([dgg, duu], [wg_t, wu_t], x, dxn, norm_w, sc, tag + "_dh")
    return dx, (dsh, dsc, dgate), dnw, (dwg_t, dwu_t, dwd)


def local_step(x, tgt, positions, mod, wv, p):
    nb, s, d = x.shape
    sh1, sc1, g1, sh2, sc2, g2, sh3, sc3, g3 = mod
    cc, sp, sm = _rope_tables(positions)
    lane_head = jnp.arange(D_SSD, dtype=I32)[None, :] // SSD_HEAD_DIM
    e_mat = (lane_head == jnp.arange(LANES, dtype=I32)[:, None]).astype(F32)
    et_mat = e_mat.T
    rr, cl = jnp.arange(LANES, dtype=I32)[:, None], jnp.arange(LANES, dtype=I32)[None, :]
    place = ((cl == rr + (QK_NOPE - SSD_HEADS)) & (rr >= SSD_HEADS) & (rr < SSD_HEADS + QK_ROPE)).astype(F32)
    dtb = jnp.pad(p["dt_bias"], ((0, 0), (0, LANES - SSD_HEADS)))
    alog = jnp.pad(p["a_log"], ((0, 0), (0, LANES - SSD_HEADS)))
    dskip_e = jnp.repeat(p["d_skip"], SSD_HEAD_DIM, axis=1)

    h1 = norm_mod(x, p["norm_ffn1"], sc1, sh1, "ffn1_norm")
    gg1, uu1, a1 = ffn_up(h1, wv["wg1_t"], wv["wu1_t"], "ffn1_up")
    x1, o1 = ffn_down(a1, wv["wd1"], x, g1, 0.5, "ffn1_down")
    h2 = norm_mod(x1, p["norm_mix"], sc2, sh2, "mix_norm")
    z, u, cq, ckv, misc = in_proj(h2, wv["win_t"], "in_proj")
    xs, bm, cm_ = conv_fwd(u, p["conv_w"], p["conv_b"], "conv_fwd")
    ys, y, prev = ssd_fwd(xs, bm, cm_, misc, z, dtb, alog, dskip_e, p["ssd_norm_w"], e_mat, "ssd_fwd")
    q, k, v, qn, kvn = qkv_fwd(cq, ckv, misc, cc, sp, sm, p["q_norm_w"], p["kv_norm_w"], wv["wuq_t"], wv["wukv_t"],
                               place, "qkv_fwd")
    attn, lse = flash_fwd(q, k, v, "flash_fwd")
    x2, o2, ym = out_proj(ys, attn, p["mla_norm_w"], wv["wo"], x1, g2, "out_proj")
    h3 = norm_mod(x2, p["norm_ffn2"], sc3, sh3, "ffn2_norm")
    gg3, uu3, a3 = ffn_up(h3, wv["wg2_t"], wv["wu2_t"], "ffn2_up")
    x3, o3 = ffn_down(a3, wv["wd2"], x2, g3, 0.5, "ffn2_down")
    loss, dx3, dnfin = final_loss(x3, p["norm_final"], tgt, "final_loss")

    dx2, dmod3, dnf2, (dwg2, dwu2, dwd2) = _ffn_bwd("ffn2", dx3, x2, h3, gg3, uu3, a3, o3, g3, sc3, p["norm_ffn2"],
                                                   wv["wg2_t"], wv["wu2_t"], wv["wd2"])
    dout, dg2 = gate_bwd(dx2, o2, g2, 1.0, "mix_gate_bwd")
    dys, dattn, dlt, dmlan = out_proj_bwd(dout, attn, p["mla_norm_w"], wv["wo"], "out_proj_bwd")
    dwo = jnp.concatenate([mm_tn(ys, dout, D_SSD, D_MODEL, "dwo_ssd"), mm_tn(ym, dout, D_SSD, D_MODEL, "dwo_mla")], axis=0)
    dxs, dbm, dcm, dz, ddt, dssdn, ddsk_lane, ddtb, dalog = ssd_bwd(
        dys, y, z, xs, bm, cm_, misc, prev, dtb, alog, dskip_e, p["ssd_norm_w"], e_mat, et_mat, "ssd_bwd")
    dq, dk, dv = flash_bwd(q, k, v, dattn, lse, dlt, "flash_bwd")
    dcq, dckv, dmisc, dqp, dkvc, dqn, dkvn = qkv_bwd(dq, dk, dv, ddt, cq, ckv, cc, sp, sm, p["q_norm_w"], p["kv_norm_w"],
                                                     wv["wuq_t"], wv["wukv_t"], place.T, "qkv_bwd")
    dwuq = mm_tn(dqp, qn, MLA_HEADS * HEAD_PAD, Q_LORA, "dwuq")
    dwukv = mm_tn(dkvc, kvn, MLA_HEADS * HEAD_PAD, KV_LORA, "dwukv")
    dvv, dconv = conv_bwd_a(dxs, dbm, dcm, u, p["conv_w"], p["conv_b"], "conv_bwd_a")
    du = conv_bwd_b(dvv, p["conv_w"], "conv_bwd_b")
    dproj = jnp.concatenate([dz, du, dcq, dckv, dmisc], axis=-1)
    dwin = mm_tn(dproj, h2, D_IN_PAD // 2, D_MODEL, "dwin")
    dx1, dsc2, dsh2, dnmix = dh_norm_bwd([dproj], [wv["win_t"]], x1, dx2, p["norm_mix"], sc2, "mix_dh")
    dx0, dmod1, dnf1, (dwg1, dwu1, dwd1) = _ffn_bwd("ffn1", dx1, x, h1, gg1, uu1, a1, o1, g1, sc1, p["norm_ffn1"],
                                                   wv["wg1_t"], wv["wu1_t"], wv["wd1"])

    dmod = jnp.concatenate([*dmod1, dsh2, dsc2, dg2, *dmod3], axis=1).reshape(nb, N_MOD * d)
    return dict(
        loss=loss, dx=dx0, dmod=dmod, norm_ffn1=dnf1, norm_mix=dnmix, norm_ffn2=dnf2, norm_final=dnfin,
        ssd_norm_w=dssdn, mla_norm_w=dmlan, q_norm_w=dqn, kv_norm_w=dkvn,
        dt_bias=ddtb[:, :SSD_HEADS], a_log=dalog[:, :SSD_HEADS],
        d_skip=squeeze_heads(ddsk_lane, et_mat, "d_skip_heads")[:, :SSD_HEADS],
        conv_b=dconv[4:5], conv_w=dconv[0:4],
        gw=dict(ffn1_w_gate=dwg1, ffn1_w_up=dwu1, ffn1_w_down=dwd1, ffn2_w_gate=dwg2, ffn2_w_up=dwu2, ffn2_w_down=dwd2,
                w_out=dwo, w_in=dwin, w_ukv=dwukv, w_uq=dwuq))


def kernel(x, c, positions, w_ada, b_ada, norm_ffn1, ffn1_w_gate, ffn1_w_up, ffn1_w_down, norm_mix, w_in, conv_w, conv_b, dt_bias, a_log, d_skip, ssd_norm_w, q_norm_w, w_uq, kv_norm_w, w_ukv, mla_norm_w, w_out, norm_ffn2, ffn2_w_gate, ffn2_w_up, ffn2_w_down, norm_final, loss_target, m_w_ada, m_b_ada, m_norm_ffn1, m_ffn1_w_gate, m_ffn1_w_up, m_ffn1_w_down, m_norm_mix, m_w_in, m_conv_w, m_conv_b, m_dt_bias, m_a_log, m_d_skip, m_ssd_norm_w, m_q_norm_w, m_w_uq, m_kv_norm_w, m_w_ukv, m_mla_norm_w, m_w_out, m_norm_ffn2, m_ffn2_w_gate, m_ffn2_w_up, m_ffn2_w_down, m_norm_final, v_w_ada, v_b_ada, v_norm_ffn1, v_ffn1_w_gate, v_ffn1_w_up, v_ffn1_w_down, v_norm_mix, v_w_in, v_conv_w, v_conv_b, v_dt_bias, v_a_log, v_d_skip, v_ssd_norm_w, v_q_norm_w, v_w_uq, v_kv_norm_w, v_w_ukv, v_mla_norm_w, v_w_out, v_norm_ffn2, v_ffn2_w_gate, v_ffn2_w_up, v_ffn2_w_down, v_norm_final):
    names = ["w_ada", "b_ada", "norm_ffn1", "ffn1_w_gate", "ffn1_w_up", "ffn1_w_down", "norm_mix", "w_in", "conv_w",
             "conv_b", "dt_bias", "a_log", "d_skip", "ssd_norm_w", "q_norm_w", "w_uq", "kv_norm_w", "w_ukv",
             "mla_norm_w", "w_out", "norm_ffn2", "ffn2_w_gate", "ffn2_w_up", "ffn2_w_down", "norm_final"]
    W = dict(zip(names, (w_ada, b_ada, norm_ffn1, ffn1_w_gate, ffn1_w_up, ffn1_w_down, norm_mix, w_in, conv_w, conv_b, dt_bias, a_log, d_skip, ssd_norm_w, q_norm_w, w_uq, kv_norm_w, w_ukv, mla_norm_w, w_out, norm_ffn2, ffn2_w_gate, ffn2_w_up, ffn2_w_down, norm_final)))
    M = dict(zip(names, (m_w_ada, m_b_ada, m_norm_ffn1, m_ffn1_w_gate, m_ffn1_w_up, m_ffn1_w_down, m_norm_mix, m_w_in, m_conv_w, m_conv_b, m_dt_bias, m_a_log, m_d_skip, m_ssd_norm_w, m_q_norm_w, m_w_uq, m_kv_norm_w, m_w_ukv, m_mla_norm_w, m_w_out, m_norm_ffn2, m_ffn2_w_gate, m_ffn2_w_up, m_ffn2_w_down, m_norm_final)))
    V = dict(zip(names, (v_w_ada, v_b_ada, v_norm_ffn1, v_ffn1_w_gate, v_ffn1_w_up, v_ffn1_w_down, v_norm_mix, v_w_in, v_conv_w, v_conv_b, v_dt_bias, v_a_log, v_d_skip, v_ssd_norm_w, v_q_norm_w, v_w_uq, v_kv_norm_w, v_w_ukv, v_mla_norm_w, v_w_out, v_norm_ffn2, v_ffn2_w_gate, v_ffn2_w_up, v_ffn2_w_down, v_norm_final)))

    nb, s, d = x.shape
    me = 4 * lax.axis_index("x") + 2 * lax.axis_index("y") + lax.axis_index("c")
    n_ada = w_ada.shape[2]

    cshape = [(nb, d), conv_w.shape[1:]]
    cg = all_gather8(_pack_rows([c, conv_w[0]]), "gather_c")
    c_all = jnp.stack([_unpack_rows(cg[k], cshape)[0] for k in range(N_DEV)]).reshape(N_DEV * nb, d)
    conv_w_full = jnp.concatenate([_unpack_rows(cg[k], cshape)[1] for k in range(N_DEV)], axis=1)
    g_ffn1 = all_gather8(_pack_shards(W, GROUP_FFN1, BF16), "gather_w_ffn1")
    g_ffn1, rest = lax.optimization_barrier((g_ffn1, _pack_shards(W, GROUP_REST, BF16)))
    wv = weight_views(g_ffn1, sc_all_gather8(rest, "gather_w_rest", 1))

    b_ada_cols = lax.dynamic_slice(b_ada, (0, me * n_ada), (1, n_ada))
    mod_cols, c_act = adaln_fwd(c_all, w_ada[0], b_ada_cols, "adaln_fwd")
    mod_g = all_gather8(mod_cols, "gather_mod")
    mod = lax.dynamic_slice(mod_g, (0, me * nb, 0), (N_DEV, nb, n_ada)).transpose(1, 0, 2).reshape(nb, N_MOD, 1, d)
    mod = [mod[:, k] for k in range(N_MOD)]

    P = dict(W)
    P["conv_w"] = conv_w_full
    P["norm_final"] = norm_final.reshape(1, d)
    R = local_step(x, loss_target, positions, mod, wv, P)

    dmod = R["dmod"]
    partial_shapes = [(1,), (1, d), (1, d), (1, d), (1, d), (1, d), (1, d), (1, Q_LORA), (1, KV_LORA),
                      (1, SSD_HEADS), (1, SSD_HEADS), (1, SSD_HEADS), (1, D_CONV), (4, D_CONV), (1, N_MOD * d),
                      (nb, N_MOD * d)]
    partial = _pack_rows([R["loss"][0, :1], R["norm_ffn1"], R["norm_mix"], R["norm_ffn2"], R["norm_final"],
                          R["ssd_norm_w"], R["mla_norm_w"], R["q_norm_w"], R["kv_norm_w"],
                          R["dt_bias"], R["a_log"], R["d_skip"], R["conv_b"], R["conv_w"],
                          sum_rows(dmod, "dmod_rows"), dmod])
    partial_g = all_gather8(partial, "gather_partials")
    (loss, g_nf1, g_nmix, g_nf2, g_nfin, g_ssdn, g_mlan, g_qn, g_kvn, g_dtb, g_alog, g_dskip, g_convb, g_convw,
     g_bada, _) = _unpack_rows(sum_blocks(partial_g, "sum_partials"), partial_shapes)
    dmod_all = jnp.stack([_unpack_rows(partial_g[k], partial_shapes)[-1] for k in range(N_DEV)]).reshape(N_DEV * nb, -1)
    g_wada = adaln_bwd(c_act, lax.dynamic_slice(dmod_all, (0, me * n_ada), (N_DEV * nb, n_ada)), "adaln_bwd")
    n_cw = conv_w.shape[2]
    G = {"w_ada": g_wada[None], "b_ada": g_bada, "norm_ffn1": g_nf1, "norm_mix": g_nmix, "norm_ffn2": g_nf2,
         "norm_final": g_nfin.reshape(d), "ssd_norm_w": g_ssdn, "mla_norm_w": g_mlan, "q_norm_w": g_qn,
         "kv_norm_w": g_kvn, "dt_bias": g_dtb, "a_log": g_alog, "d_skip": g_dskip, "conv_b": g_convb,
         "conv_w": lax.dynamic_slice(g_convw, (0, me * n_cw), (4, n_cw))[None]}

    DW, NM, NV = {}, {}, {}
    gw = R["gw"]
    for k, (tag, group) in enumerate(GRAD_GROUPS):
        send = jnp.concatenate([_grad_rows(name, gw[name]) for name in group], axis=1).astype(BF16)
        recv = sc_all_to_all8(send, "exchange_" + tag, 2 + k)
        big = adamw_sum8(_pack_shards(W, group, F32), recv, _pack_shards(M, group, F32), _pack_shards(V, group, F32),
                         "adamw_" + tag)
        for name, (o, r) in _pack_offsets(group)[0].items():
            G[name], DW[name], NM[name], NV[name] = [_rows_to_shard(name, t[o:o + r], W[name]) for t in big]
    dwa, nma, nva = adamw(w_ada[0], g_wada, m_w_ada[0], v_w_ada[0], "adamw_w_ada")
    DW["w_ada"], NM["w_ada"], NV["w_ada"] = dwa[None], nma[None], nva[None]
    small = [n for n in names if n not in DW]
    shapes = [W[n].shape for n in small]
    outs = adamw(_pack_rows([W[n] for n in small]), _pack_rows([G[n] for n in small]),
                 _pack_rows([M[n] for n in small]), _pack_rows([V[n] for n in small]), "adamw_small")
    for res, dst in zip(outs, (DW, NM, NV)):
        for n, t in zip(small, _unpack_rows(res, shapes)):
            dst[n] = t
    return (loss.reshape(()), R["dx"], *[G[n] for n in names], *[DW[n] for n in names], *[NM[n] for n in names],
            *[NV[n] for n in names])
```

```python
import math

import jax
import jax.numpy as jnp
from jax import lax
from jax.experimental import pallas as pl
from jax.experimental.pallas import tpu as pltpu
from jax.experimental.pallas import tpu_sc as plsc

F32, BF16, I32 = jnp.float32, jnp.bfloat16, jnp.int32
HI = lax.Precision.HIGHEST
SDS = jax.ShapeDtypeStruct
MESH = pl.DeviceIdType.MESH

D_MODEL = 1024
D_FF = 2816
D_SSD = 1024
SSD_HEADS = 16
SSD_HEAD_DIM = 64
SSD_GROUPS = 2
SSD_STATE = 128
CHUNK = 128
MLA_HEADS = 8
QK_NOPE = 64
QK_ROPE = 32
QK_DIM = 96
V_HEAD = 128
Q_LORA = 384
KV_LORA = 256
ROPE_THETA = 10000.0
N_MOD = 9
EPS = 1e-6
D_CONV = 1536
D_IN = 3248
D_IN_PAD = 3328
HEAD_PAD = 128
N_DEV = 8
ADAM_LR, ADAM_B1, ADAM_B2, ADAM_EPS, ADAM_WD, ADAM_STEP = 0.001, 0.9, 0.999, 1e-08, 0.01, 10

VMEM_LIMIT = 56 * 1024 * 1024
LANES = 128
NT_DIMS = (((1,), (1,)), ((), ()))
TN_DIMS = (((0,), (0,)), ((), ()))


def _cparams(n_axes):
    return pltpu.CompilerParams(dimension_semantics=("arbitrary",) * n_axes, vmem_limit_bytes=VMEM_LIMIT)


def _row(tm, d):
    return pl.BlockSpec((None, tm, d), lambda b, i: (b, i, 0))


def _bvec(d):
    return pl.BlockSpec((None, 1, d), lambda b, i: (b, 0, 0))


def _full(shape):
    n = len(shape)
    return pl.BlockSpec(shape, lambda *_: (0,) * n)


def _sigmoid(x):
    return 1.0 / (1.0 + jnp.exp(-x))


def _softplus(x):
    return jnp.maximum(x, 0.0) + jnp.log(1.0 + jnp.exp(-jnp.abs(x)))


def _rms(x):
    return lax.rsqrt(jnp.mean(x * x, axis=-1, keepdims=True) + EPS)


def _rms_bwd(dn, n, r):
    return r * (dn - n * jnp.mean(dn * n, axis=-1, keepdims=True))


def _first_step():
    return (pl.program_id(0) == 0) & (pl.program_id(1) == 0)


def all_gather8(x, name):
    r, c = x.shape

    def body(x_ref, out_ref, send_sems, recv_sems, local_sem):
        mx, my, mc = lax.axis_index("x"), lax.axis_index("y"), lax.axis_index("c")
        me, sibling = (mx, my, mc), (mx, my, 1 - mc)
        chips = [(1 - mx, my), (mx, 1 - my), (1 - mx, 1 - my)]

        def rows(px, py, pc):
            return out_ref.at[4 * px + 2 * py + pc]

        def copy(k, block, to, src=None):
            return pltpu.make_async_remote_copy(
                src_ref=rows(*block) if src is None else src, dst_ref=rows(*block),
                send_sem=send_sems.at[k], recv_sem=recv_sems.at[k], device_id=to, device_id_type=MESH)

        mine = pltpu.make_async_copy(x_ref, rows(*me), local_sem)
        mine.start()
        first = [copy(0, me, sibling, src=x_ref)]
        first += [copy(1 + j, me, (*chip, mc), src=x_ref) for j, chip in enumerate(chips)]
        for cp in first:
            cp.start()
        passed = [copy(4 + j, (*chip, mc), sibling) for j, chip in enumerate(chips)]
        for j, chip in enumerate(chips):
            copy(1 + j, (*chip, mc), me).wait_recv()
            passed[j].start()
        copy(0, sibling, me).wait_recv()
        for j, chip in enumerate(chips):
            copy(4 + j, (*chip, 1 - mc), me).wait_recv()
        for cp in first + passed:
            cp.wait_send()
        mine.wait()

    return pl.pallas_call(
        body, name=name,
        out_shape=SDS((N_DEV, r, c), x.dtype),
        in_specs=[pl.BlockSpec(memory_space=pl.ANY)],
        out_specs=pl.BlockSpec(memory_space=pl.ANY),
        scratch_shapes=[pltpu.SemaphoreType.DMA((7,)), pltpu.SemaphoreType.DMA((7,)), pltpu.SemaphoreType.DMA],
    )(x)


def all_to_all8(x, name):
    _, r, c = x.shape

    def body(x_ref, out_ref, send_sems, recv_sems, local_sem):
        mx, my, mc = lax.axis_index("x"), lax.axis_index("y"), lax.axis_index("c")
        me = 4 * mx + 2 * my + mc
        mine = pltpu.make_async_copy(x_ref.at[me], out_ref.at[me], local_sem)
        mine.start()
        copies = []
        for rel in range(1, N_DEV):
            px = 1 - mx if rel & 4 else mx
            py = 1 - my if rel & 2 else my
            pc = 1 - mc if rel & 1 else mc
            cp = pltpu.make_async_remote_copy(
                src_ref=x_ref.at[4 * px + 2 * py + pc], dst_ref=out_ref.at[me],
                send_sem=send_sems.at[rel - 1], recv_sem=recv_sems.at[rel - 1],
                device_id=(px, py, pc), device_id_type=MESH)
            cp.start()
            copies.append(cp)
        for cp in copies:
            cp.wait()
        mine.wait()

    return pl.pallas_call(
        body, name=name,
        out_shape=SDS((N_DEV, r, c), x.dtype),
        in_specs=[pl.BlockSpec(memory_space=pl.ANY)],
        out_specs=pl.BlockSpec(memory_space=pl.ANY),
        scratch_shapes=[pltpu.SemaphoreType.DMA((7,)), pltpu.SemaphoreType.DMA((7,)), pltpu.SemaphoreType.DMA],
    )(x)


def _sequencer_kernel(name, collective_id):
    return pl.kernel(
        mesh=plsc.ScalarSubcoreMesh(axis_name="seq", num_cores=1), name=name,
        scratch_types=(pltpu.SemaphoreType.DMA((7,)), pltpu.SemaphoreType.DMA((7,)), pltpu.SemaphoreType.DMA),
        compiler_params=pltpu.CompilerParams(collective_id=collective_id))


def _handshake(peers):
    barrier = pltpu.get_barrier_semaphore()
    for peer in peers:
        pl.semaphore_signal(barrier, inc=1, device_id=peer, device_id_type=MESH)
    pl.semaphore_wait(barrier, len(peers))


def sc_all_gather8(x, name, collective_id):
    r, c = x.shape
    x_ref = jax.new_ref(x, memory_space=pltpu.MemorySpace.HBM)
    out_ref = jax.empty_ref(SDS((N_DEV, r, c), x.dtype), memory_space=pltpu.MemorySpace.HBM)

    @_sequencer_kernel(name, collective_id)
    def launch(send_sems, recv_sems, local_sem):
        mx, my, mc = lax.axis_index("x"), lax.axis_index("y"), lax.axis_index("c")
        me, sibling = (mx, my, mc), (mx, my, 1 - mc)
        chips = [(1 - mx, my), (mx, 1 - my), (1 - mx, 1 - my)]
        _handshake([sibling] + [(*chip, mc) for chip in chips])

        def rows(px, py, pc):
            return out_ref.at[4 * px + 2 * py + pc]

        def copy(k, block, to, src=None):
            return pltpu.make_async_remote_copy(
                src_ref=rows(*block) if src is None else src, dst_ref=rows(*block),
                send_sem=send_sems.at[k], recv_sem=recv_sems.at[k], device_id=to, device_id_type=MESH)

        mine = pltpu.make_async_copy(x_ref, rows(*me), local_sem)
        mine.start()
        first = [copy(0, me, sibling, src=x_ref)]
        first += [copy(1 + j, me, (*chip, mc), src=x_ref) for j, chip in enumerate(chips)]
        for cp in first:
            cp.start()
        passed = [copy(4 + j, (*chip, mc), sibling) for j, chip in enumerate(chips)]
        for j, chip in enumerate(chips):
            copy(1 + j, (*chip, mc), me).wait_recv()
            passed[j].start()
        copy(0, sibling, me).wait_recv()
        for j, chip in enumerate(chips):
            copy(4 + j, (*chip, 1 - mc), me).wait_recv()
        for cp in first + passed:
            cp.wait_send()
        mine.wait()

    launch()
    return out_ref[...]


def sc_all_to_all8(x, name, collective_id):
    x_ref = jax.new_ref(x, memory_space=pltpu.MemorySpace.HBM)
    out_ref = jax.empty_ref(SDS(x.shape, x.dtype), memory_space=pltpu.MemorySpace.HBM)

    @_sequencer_kernel(name, collective_id)
    def launch(send_sems, recv_sems, local_sem):
        mx, my, mc = lax.axis_index("x"), lax.axis_index("y"), lax.axis_index("c")
        me = 4 * mx + 2 * my + mc
        peers = [(1 - mx if rel & 4 else mx, 1 - my if rel & 2 else my, 1 - mc if rel & 1 else mc)
                 for rel in range(1, N_DEV)]
        _handshake(peers)
        mine = pltpu.make_async_copy(x_ref.at[me], out_ref.at[me], local_sem)
        mine.start()
        copies = []
        for k, (px, py, pc) in enumerate(peers):
            cp = pltpu.make_async_remote_copy(
                src_ref=x_ref.at[4 * px + 2 * py + pc], dst_ref=out_ref.at[me],
                send_sem=send_sems.at[k], recv_sem=recv_sems.at[k], device_id=(px, py, pc), device_id_type=MESH)
            cp.start()
            copies.append(cp)
        for cp in copies:
            cp.wait()
        mine.wait()

    launch()
    return out_ref[...]


def norm_mod(x, w, sc, sh, name):
    b, s, d = x.shape
    tm = min(512, s)

    def body(x_ref, w_ref, sc_ref, sh_ref, h_ref):
        xv = x_ref[...]
        n = xv * _rms(xv)
        h_ref[...] = ((n * w_ref[...]) * (1.0 + sc_ref[...]) + sh_ref[...]).astype(BF16)

    return pl.pallas_call(
        body, name=name, grid=(b, s // tm),
        in_specs=[_row(tm, d), _full((1, d)), _bvec(d), _bvec(d)],
        out_specs=_row(tm, d), out_shape=SDS((b, s, d), BF16), compiler_params=_cparams(2))(x, w, sc, sh)


def ffn_up(h, wg_t, wu_t, name):
    b, s, d = h.shape
    f = wg_t.shape[0]
    tm, tn = min(512, s), f // 2

    def body(h_ref, wg_ref, wu_ref, g_ref, u_ref, a_ref):
        hv = h_ref[...]
        g = lax.dot_general(hv, wg_ref[...], NT_DIMS, preferred_element_type=F32)
        u = lax.dot_general(hv, wu_ref[...], NT_DIMS, preferred_element_type=F32)
        g_ref[...] = g
        u_ref[...] = u
        a_ref[...] = (g * _sigmoid(g) * u).astype(BF16)

    hs = pl.BlockSpec((None, tm, d), lambda j, bb, i: (bb, i, 0))
    ws = pl.BlockSpec((tn, d), lambda j, bb, i: (j, 0))
    os_ = pl.BlockSpec((None, tm, tn), lambda j, bb, i: (bb, i, j))
    return pl.pallas_call(
        body, name=name, grid=(f // tn, b, s // tm),
        in_specs=[hs, ws, ws], out_specs=[os_, os_, os_],
        out_shape=[SDS((b, s, f), F32), SDS((b, s, f), F32), SDS((b, s, f), BF16)],
        compiler_params=_cparams(3))(h, wg_t, wu_t)


def ffn_down(a, wd, x, gate, scale, name):
    b, s, f = a.shape
    d = wd.shape[1]
    tm = min(512, s)

    def body(a_ref, wd_ref, x_ref, g_ref, xn_ref, o_ref):
        o = jnp.dot(a_ref[...], wd_ref[...], preferred_element_type=F32)
        xn_ref[...] = x_ref[...] + (scale * g_ref[...]) * o
        o_ref[...] = o.astype(BF16)

    return pl.pallas_call(
        body, name=name, grid=(b, s // tm),
        in_specs=[_row(tm, f), _full((f, d)), _row(tm, d), _bvec(d)],
        out_specs=[_row(tm, d), _row(tm, d)],
        out_shape=[SDS((b, s, d), F32), SDS((b, s, d), BF16)], compiler_params=_cparams(2))(a, wd, x, gate)


def gate_bwd(dxn, o, gate, scale, name):
    b, s, d = dxn.shape
    tm = min(512, s)

    def body(dx_ref, o_ref, g_ref, do_ref, dg_ref):
        @pl.when(pl.program_id(1) == 0)
        def _():
            dg_ref[...] = jnp.zeros_like(dg_ref)
        dx = dx_ref[...]
        do_ref[...] = ((scale * g_ref[...]) * dx).astype(BF16)
        dg_ref[...] += jnp.sum(scale * dx * o_ref[...].astype(F32), axis=0, keepdims=True)

    return pl.pallas_call(
        body, name=name, grid=(b, s // tm),
        in_specs=[_row(tm, d), _row(tm, d), _bvec(d)],
        out_specs=[_row(tm, d), _bvec(d)],
        out_shape=[SDS((b, s, d), BF16), SDS((b, 1, d), F32)], compiler_params=_cparams(2))(dxn, o, gate)


def ffn_dact(do, wd, g, u, name):
    b, s, d = do.shape
    f = wd.shape[0]
    tm, tn = min(512, s), f // 2

    def body(do_ref, wd_ref, g_ref, u_ref, dg_ref, du_ref):
        da = lax.dot_general(do_ref[...], wd_ref[...], NT_DIMS, preferred_element_type=F32)
        gv = g_ref[...]
        sg = _sigmoid(gv)
        dg_ref[...] = (da * u_ref[...] * (sg * (1.0 + gv * (1.0 - sg)))).astype(BF16)
        du_ref[...] = (da * (gv * sg)).astype(BF16)

    dos = pl.BlockSpec((None, tm, d), lambda j, bb, i: (bb, i, 0))
    ws = pl.BlockSpec((tn, d), lambda j, bb, i: (j, 0))
    es = pl.BlockSpec((None, tm, tn), lambda j, bb, i: (bb, i, j))
    return pl.pallas_call(
        body, name=name, grid=(f // tn, b, s // tm),
        in_specs=[dos, ws, es, es], out_specs=[es, es],
        out_shape=[SDS((b, s, f), BF16), SDS((b, s, f), BF16)], compiler_params=_cparams(3))(do, wd, g, u)


def mm_tn(a, bm, tma, tnb, name):
    b, s, ka = a.shape
    nb = bm.shape[2]
    tk = min(512, s)

    def body(a_ref, b_ref, o_ref):
        @pl.when((pl.program_id(2) == 0) & (pl.program_id(3) == 0))
        def _():
            o_ref[...] = jnp.zeros_like(o_ref)
        o_ref[...] += lax.dot_general(a_ref[...], b_ref[...], TN_DIMS, preferred_element_type=F32)

    return pl.pallas_call(
        body, name=name, grid=(ka // tma, nb // tnb, b, s // tk),
        in_specs=[pl.BlockSpec((None, tk, tma), lambda i, j, bb, k: (bb, k, i)),
                  pl.BlockSpec((None, tk, tnb), lambda i, j, bb, k: (bb, k, j))],
        out_specs=pl.BlockSpec((tma, tnb), lambda i, j, bb, k: (i, j)),
        out_shape=SDS((ka, nb), F32), compiler_params=_cparams(4))(a, bm)


def _gate_bwd_specs(tm, d, b, s):
    return ([_row(tm, d), _bvec(d)], [_row(tm, d), _bvec(d)], [SDS((b, s, d), BF16), SDS((b, 1, d), F32)])


def _gate_bwd_tile(dx, scale, o_ref, g_ref, do_ref, dg_ref):
    do_ref[...] = ((scale * g_ref[...]) * dx).astype(BF16)
    dg_ref[...] += jnp.sum(scale * dx * o_ref[...].astype(F32), axis=0, keepdims=True)


def dh_norm_bwd(dys, wts, x, dxn, w, sc, name, below=None):
    b, s, d = x.shape
    tm = min(256, s)
    n_in = len(dys)
    extra_in, extra_out, extra_shape = _gate_bwd_specs(tm, d, b, s) if below else ([], [], [])

    def body(*refs):
        dy_refs, w_refs = refs[:n_in], refs[n_in:2 * n_in]
        x_ref, dxn_ref, nw_ref, sc_ref = refs[2 * n_in:2 * n_in + 4]
        rest = refs[2 * n_in + 4:]
        if below:
            o_ref, g_ref, dx_ref, dsc_ref, dsh_ref, dw_ref, do_ref, dg_ref = rest
        else:
            dx_ref, dsc_ref, dsh_ref, dw_ref = rest

        @pl.when(pl.program_id(1) == 0)
        def _():
            dsc_ref[...] = jnp.zeros_like(dsc_ref)
            dsh_ref[...] = jnp.zeros_like(dsh_ref)
            if below:
                dg_ref[...] = jnp.zeros_like(dg_ref)

        @pl.when(_first_step())
        def _():
            dw_ref[...] = jnp.zeros_like(dw_ref)

        dh = jnp.dot(dy_refs[0][...], w_refs[0][...], preferred_element_type=F32)
        for k in range(1, n_in):
            dh += jnp.dot(dy_refs[k][...], w_refs[k][...], preferred_element_type=F32)
        xv = x_ref[...]
        r = _rms(xv)
        n = xv * r
        nw = nw_ref[...]
        dsc_ref[...] += jnp.sum(dh * (n * nw), axis=0, keepdims=True)
        dsh_ref[...] += jnp.sum(dh, axis=0, keepdims=True)
        dhn = dh * (1.0 + sc_ref[...])
        dw_ref[...] += jnp.sum(dhn * n, axis=0, keepdims=True)
        dx = dxn_ref[...] + _rms_bwd(dhn * nw, n, r)
        dx_ref[...] = dx
        if below:
            _gate_bwd_tile(dx, below[2], o_ref, g_ref, do_ref, dg_ref)

    in_specs = [_row(tm, dy.shape[2]) for dy in dys] + [_full(wt.shape) for wt in wts]
    in_specs += [_row(tm, d), _row(tm, d), _full((1, d)), _bvec(d)] + extra_in
    return pl.pallas_call(
        body, name=name, grid=(b, s // tm), in_specs=in_specs,
        out_specs=[_row(tm, d), _bvec(d), _bvec(d), _full((1, d))] + extra_out,
        out_shape=[SDS((b, s, d), F32), SDS((b, 1, d), F32), SDS((b, 1, d), F32), SDS((1, d), F32)] + extra_shape,
        compiler_params=_cparams(2))(*dys, *wts, x, dxn, w, sc, *(below[:2] if below else ()))


def final_loss(x, w, tgt, below, name):
    b, s, d = x.shape
    tm = min(512, s)
    extra_in, extra_out, extra_shape = _gate_bwd_specs(tm, d, b, s)

    def body(x_ref, w_ref, t_ref, o_ref, g_ref, loss_ref, dx_ref, dw_ref, do_ref, dg_ref):
        @pl.when(_first_step())
        def _():
            loss_ref[...] = jnp.zeros_like(loss_ref)
            dw_ref[...] = jnp.zeros_like(dw_ref)

        @pl.when(pl.program_id(1) == 0)
        def _():
            dg_ref[...] = jnp.zeros_like(dg_ref)
        xv = x_ref[...]
        r = _rms(xv)
        n = xv * r
        wv = w_ref[...]
        e = n * wv - t_ref[...]
        loss_ref[...] += jnp.sum(e * e) * (0.5 / d)
        dy = e * (1.0 / d)
        dw_ref[...] += jnp.sum(dy * n, axis=0, keepdims=True)
        dx = _rms_bwd(dy * wv, n, r)
        dx_ref[...] = dx
        _gate_bwd_tile(dx, below[2], o_ref, g_ref, do_ref, dg_ref)

    return pl.pallas_call(
        body, name=name, grid=(b, s // tm),
        in_specs=[_row(tm, d), _full((1, d)), _row(tm, d)] + extra_in,
        out_specs=[_full((1, LANES)), _row(tm, d), _full((1, d))] + extra_out,
        out_shape=[SDS((1, LANES), F32), SDS((b, s, d), F32), SDS((1, d), F32)] + extra_shape,
        compiler_params=_cparams(2))(x, w, tgt, *below[:2])


def in_proj(h, win_t, name):
    b, s, d = h.shape
    tm = min(256, s)
    widths = (D_SSD, D_SSD + 2 * SSD_GROUPS * SSD_STATE, Q_LORA, KV_LORA, LANES)

    def body(h_ref, w_ref, *outs):
        p = lax.dot_general(h_ref[...], w_ref[...], NT_DIMS, preferred_element_type=F32)
        off = 0
        for o_ref, wd in zip(outs, widths):
            o_ref[...] = p[:, off:off + wd]
            off += wd

    return pl.pallas_call(
        body, name=name, grid=(b, s // tm),
        in_specs=[_row(tm, d), _full(win_t.shape)],
        out_specs=[_row(tm, wd) for wd in widths],
        out_shape=[SDS((b, s, wd), F32) for wd in widths], compiler_params=_cparams(2))(h, win_t)


def _halo_prev(ts, d):
    return pl.BlockSpec((None, 8, d), lambda b, i: (b, jnp.maximum(i * (ts // 8) - 1, 0), 0))


def _conv_taps(ext_ref, w_ref, ts):
    return [ext_ref[5 + k:5 + k + ts, :] for k in range(4)], [w_ref[k:k + 1, :] for k in range(4)]


def conv_fwd(u, cw, cb, name):
    b, s, dc = u.shape
    ts = min(512, s)
    widths = (D_SSD, SSD_GROUPS * SSD_STATE, SSD_GROUPS * SSD_STATE)

    def body(u_ref, up_ref, w_ref, b_ref, xs_ref, bm_ref, cm_ref, ext):
        ext[0:8, :] = jnp.where(pl.program_id(1) > 0, up_ref[...], 0.0)
        ext[8:8 + ts, :] = u_ref[...]
        taps, ws = _conv_taps(ext, w_ref, ts)
        v = b_ref[...] + taps[0] * ws[0] + taps[1] * ws[1] + taps[2] * ws[2] + taps[3] * ws[3]
        y = v * _sigmoid(v)
        xs_ref[...] = y[:, 0:D_SSD]
        bm_ref[...] = y[:, D_SSD:D_SSD + 256]
        cm_ref[...] = y[:, D_SSD + 256:D_SSD + 512]

    return pl.pallas_call(
        body, name=name, grid=(b, s // ts),
        in_specs=[_row(ts, dc), _halo_prev(ts, dc), _full((4, dc)), _full((1, dc))],
        out_specs=[_row(ts, wd) for wd in widths],
        out_shape=[SDS((b, s, wd), F32) for wd in widths],
        scratch_shapes=[pltpu.VMEM((ts + 8, dc), F32)], compiler_params=_cparams(2))(u, u, cw, cb)


def conv_bwd_a(dxs, dbm, dcm, u, cw, cb, name):
    b, s, dc = u.shape
    ts = min(512, s)

    def body(dxs_ref, dbm_ref, dcm_ref, u_ref, up_ref, w_ref, b_ref, dv_ref, dwb_ref, ext):
        @pl.when(_first_step())
        def _():
            dwb_ref[...] = jnp.zeros_like(dwb_ref)
        ext[0:8, :] = jnp.where(pl.program_id(1) > 0, up_ref[...], 0.0)
        ext[8:8 + ts, :] = u_ref[...]
        taps, ws = _conv_taps(ext, w_ref, ts)
        v = b_ref[...] + taps[0] * ws[0] + taps[1] * ws[1] + taps[2] * ws[2] + taps[3] * ws[3]
        sg = _sigmoid(v)
        dy = jnp.concatenate([dxs_ref[...], dbm_ref[...], dcm_ref[...]], axis=1)
        dv = dy * (sg * (1.0 + v * (1.0 - sg)))
        dv_ref[...] = dv
        for k in range(4):
            dwb_ref[k:k + 1, :] += jnp.sum(dv * taps[k], axis=0, keepdims=True)
        dwb_ref[4:5, :] += jnp.sum(dv, axis=0, keepdims=True)

    return pl.pallas_call(
        body, name=name, grid=(b, s // ts),
        in_specs=[_row(ts, D_SSD), _row(ts, 256), _row(ts, 256), _row(ts, dc), _halo_prev(ts, dc),
                  _full((4, dc)), _full((1, dc))],
        out_specs=[_row(ts, dc), _full((8, dc))],
        out_shape=[SDS((b, s, dc), F32), SDS((8, dc), F32)],
        scratch_shapes=[pltpu.VMEM((ts + 8, dc), F32)], compiler_params=_cparams(2))(dxs, dbm, dcm, u, u, cw, cb)


def conv_bwd_b(dv, cw, name):
    b, s, dc = dv.shape
    ts = min(512, s)
    nt = s // ts

    def body(dv_ref, dn_ref, w_ref, du_ref, ext):
        ext[0:ts, :] = dv_ref[...]
        ext[ts:ts + 8, :] = jnp.where(pl.program_id(1) < nt - 1, dn_ref[...], 0.0)
        acc = ext[3:3 + ts, :] * w_ref[0:1, :]
        for k in range(1, 4):
            acc += ext[3 - k:3 - k + ts, :] * w_ref[k:k + 1, :]
        du_ref[...] = acc.astype(BF16)

    nxt = pl.BlockSpec((None, 8, dc), lambda bb, i: (bb, jnp.minimum((i + 1) * (ts // 8), s // 8 - 1), 0))
    return pl.pallas_call(
        body, name=name, grid=(b, nt),
        in_specs=[_row(ts, dc), nxt, _full((4, dc))],
        out_specs=_row(ts, dc), out_shape=SDS((b, s, dc), BF16),
        scratch_shapes=[pltpu.VMEM((ts + 8, dc), F32)], compiler_params=_cparams(2))(dv, dv, cw)


def _ssd_common(misc_ref, dtb_ref, alog_ref, e_ref):
    ln = CHUNK
    lane = lax.broadcasted_iota(I32, (ln, LANES), 1)
    lane1 = lax.broadcasted_iota(I32, (1, LANES), 1)
    pre = misc_ref[...] + dtb_ref[...]
    dt_s = jnp.where(lane < SSD_HEADS, _softplus(pre), 0.0)
    a_neg = jnp.where(lane1 < SSD_HEADS, -jnp.exp(alog_ref[...]), 0.0)
    ri = lax.broadcasted_iota(I32, (ln, ln), 0)
    ci = lax.broadcasted_iota(I32, (ln, ln), 1)
    tril = ci <= ri
    acum = jnp.dot(tril.astype(F32), dt_s * a_neg, preferred_element_type=F32, precision=HI)
    both_e = _dot_01(jnp.concatenate([dt_s, acum], axis=0), e_ref[...], 3)
    dt_e, acum_e = both_e[0:ln], both_e[ln:2 * ln]
    return dict(pre=pre, dt_s=dt_s, a_neg=a_neg, tril=tril, ri=ri, ci=ci, acum=acum, acum_t=acum.T,
                dt_e=dt_e, eac_e=jnp.exp(acum_e), del_e=jnp.exp(acum_e[ln - 1:ln, :] - acum_e))


def _dot_01(x, m01, terms):
    acc, rest = None, x
    for k in range(terms):
        part = rest.astype(BF16)
        if k + 1 < terms:
            rest = rest - part.astype(F32)
        d = jnp.dot(part, m01, preferred_element_type=F32)
        acc = d if acc is None else acc + d
    return acc


def _decay(cm, h):
    seg = cm["acum"][:, h:h + 1] - cm["acum_t"][h:h + 1, :]
    return jnp.exp(jnp.where(cm["tril"], seg, -jnp.inf))


def ssd_fwd(xs, bm, cm_, misc, z, dtb, alog, dskip_e, norm_w, e_mat, name):
    b, s, _ = xs.shape
    ln, nc = CHUNK, s // CHUNK
    gw = D_SSD // SSD_GROUPS
    hpg = SSD_HEADS // SSD_GROUPS

    def body(xs_ref, b_ref, c_ref, misc_ref, z_ref, dtb_ref, alog_ref, dsk_ref, nw_ref, e_ref,
             ys_ref, y_ref, p_ref, st, yd):
        @pl.when(pl.program_id(1) == 0)
        def _():
            st[...] = jnp.zeros_like(st)
        cm = _ssd_common(misc_ref, dtb_ref, alog_ref, e_ref)
        xsv = xs_ref[...]
        xdt = xsv * cm["dt_e"]
        xdt_b = xdt.astype(BF16)
        xd_b = (xdt * cm["del_e"]).astype(BF16)
        gam_e = cm["eac_e"][ln - 1:ln, :]
        p_ref[...] = st[...]
        yoff = []
        for g in range(SSD_GROUPS):
            gs = slice(gw * g, gw * (g + 1))
            bg = b_ref[:, SSD_STATE * g:SSD_STATE * (g + 1)].astype(BF16)
            cg = c_ref[:, SSD_STATE * g:SSD_STATE * (g + 1)].astype(BF16)
            cb = lax.dot_general(cg, bg, NT_DIMS, preferred_element_type=F32)
            st_g = st[:, gs]
            yoff.append(jnp.dot(cg, st_g.astype(BF16), preferred_element_type=F32) * cm["eac_e"][:, gs])
            for j in range(hpg):
                h = hpg * g + j
                hs = slice(SSD_HEAD_DIM * h, SSD_HEAD_DIM * (h + 1))
                m = (cb * _decay(cm, h)).astype(BF16)
                yd[:, hs] = jnp.dot(m, xdt_b[:, hs], preferred_element_type=F32)
            new = lax.dot_general(bg, xd_b[:, gs], TN_DIMS, preferred_element_type=F32)
            st[:, gs] = st_g * gam_e[:, gs] + new
        y = yd[...] + jnp.concatenate(yoff, axis=1) + dsk_ref[...] * xsv
        y_ref[...] = y
        zz = z_ref[...]
        yg = y * (zz * _sigmoid(zz))
        outs = []
        for g in range(SSD_GROUPS):
            ygg = yg[:, gw * g:gw * (g + 1)]
            outs.append(ygg * _rms(ygg) * nw_ref[:, gw * g:gw * (g + 1)])
        ys_ref[...] = jnp.concatenate(outs, axis=1).astype(BF16)

    row = lambda d: pl.BlockSpec((None, ln, d), lambda bb, c: (bb, c, 0))
    return pl.pallas_call(
        body, name=name, grid=(b, nc),
        in_specs=[row(D_SSD), row(256), row(256), row(LANES), row(D_SSD), _full((1, LANES)), _full((1, LANES)),
                  _full((1, D_SSD)), _full((1, D_SSD)), _full((LANES, D_SSD))],
        out_specs=[row(D_SSD), row(D_SSD), pl.BlockSpec((None, None, SSD_STATE, D_SSD), lambda bb, c: (bb, c, 0, 0))],
        out_shape=[SDS((b, s, D_SSD), BF16), SDS((b, s, D_SSD), F32), SDS((b, nc, SSD_STATE, D_SSD), F32)],
        scratch_shapes=[pltpu.VMEM((SSD_STATE, D_SSD), F32), pltpu.VMEM((ln, D_SSD), F32)],
        compiler_params=_cparams(2))(xs, bm, cm_, misc, z, dtb, alog, dskip_e, norm_w, e_mat)


def ssd_bwd(dys, y, z, xs, bm, cm_, misc, prev, dtb, alog, dskip_e, norm_w, e_mat, et_mat, name):
    b, s, _ = xs.shape
    ln, nc = CHUNK, s // CHUNK
    gw = D_SSD // SSD_GROUPS
    hpg = SSD_HEADS // SSD_GROUPS

    def body(dys_ref, y_ref, z_ref, xs_ref, b_ref, c_ref, misc_ref, p_ref, dtb_ref, alog_ref, dsk_ref, nw_ref,
             e_ref, et_ref, dxs_ref, db_ref, dc_ref, dz_ref, ddt_ref, dnw_ref, ddsk_ref, ddtb_ref, dalog_ref,
             dst, dxd, dac_t):
        @pl.when(_first_step())
        def _():
            for r_ in (dnw_ref, ddsk_ref, ddtb_ref, dalog_ref):
                r_[...] = jnp.zeros_like(r_)

        @pl.when(pl.program_id(1) == 0)
        def _():
            dst[...] = jnp.zeros_like(dst)

        cm = _ssd_common(misc_ref, dtb_ref, alog_ref, e_ref)
        et = et_ref[...]
        squeeze = lambda t: _dot_01(t, et, 2)
        lane = lax.broadcasted_iota(I32, (ln, LANES), 1)
        sub = lax.broadcasted_iota(I32, (LANES, ln), 0)
        xsv = xs_ref[...]
        xdt = xsv * cm["dt_e"]
        xdt_b = xdt.astype(BF16)
        xd_b = (xdt * cm["del_e"]).astype(BF16)
        eac_e = cm["eac_e"]
        gam_e = eac_e[ln - 1:ln, :]

        yv, zz, dyo = y_ref[...], z_ref[...], dys_ref[...]
        sz = _sigmoid(zz)
        silu_z = zz * sz
        yg = yv * silu_z
        dyg, dnw = [], []
        for g in range(SSD_GROUPS):
            gs = slice(gw * g, gw * (g + 1))
            ygg = yg[:, gs]
            r = _rms(ygg)
            n = ygg * r
            dnw.append(jnp.sum(dyo[:, gs] * n, axis=0, keepdims=True))
            dyg.append(_rms_bwd(dyo[:, gs] * nw_ref[:, gs], n, r))
        dyg = jnp.concatenate(dyg, axis=1)
        dnw_ref[...] += jnp.concatenate(dnw, axis=1)
        dz_ref[...] = (dyg * yv * (sz * (1.0 + zz * (1.0 - sz)))).astype(BF16)
        dy = dyg * silu_z
        ddsk_ref[...] += jnp.sum(dy * xsv, axis=0, keepdims=True)
        dy_b = dy.astype(BF16)

        dacum = jnp.zeros((ln, LANES), F32)
        dac_t[...] = jnp.zeros_like(dac_t)
        w1, dgam = [], []
        for g in range(SSD_GROUPS):
            gs = slice(gw * g, gw * (g + 1))
            ss = slice(SSD_STATE * g, SSD_STATE * (g + 1))
            bg = b_ref[:, ss].astype(BF16)
            cg = c_ref[:, ss].astype(BF16)
            cb = lax.dot_general(cg, bg, NT_DIMS, preferred_element_type=F32)
            pt = p_ref[:, gs]
            pt_b = pt.astype(BF16)
            dst_g = dst[:, gs]
            dst_b = dst_g.astype(BF16)
            edy = (dy[:, gs] * eac_e[:, gs]).astype(BF16)
            dcg = lax.dot_general(edy, pt_b, NT_DIMS, preferred_element_type=F32)
            dpt = lax.dot_general(cg, edy, TN_DIMS, preferred_element_type=F32)
            yoff = jnp.dot(cg, pt_b, preferred_element_type=F32) * eac_e[:, gs]
            dxd_g = jnp.dot(bg, dst_b, preferred_element_type=F32)
            dbg = lax.dot_general(xd_b[:, gs], dst_b, NT_DIMS, preferred_element_type=F32)
            ddel = dxd_g * xdt[:, gs] * cm["del_e"][:, gs]
            w1.append(dy[:, gs] * yoff - ddel)
            dgam.append(jnp.sum(ddel, axis=0, keepdims=True) + jnp.sum(dst_g * pt, axis=0, keepdims=True) * gam_e[:, gs])
            dxd[:, gs] = dxd_g * cm["del_e"][:, gs]
            dst[:, gs] = dst_g * gam_e[:, gs] + dpt
            dcb = jnp.zeros((ln, ln), F32)
            for j in range(hpg):
                h = hpg * g + j
                hs = slice(SSD_HEAD_DIM * h, SSD_HEAD_DIM * (h + 1))
                lam = _decay(cm, h)
                m = cb * lam
                dm = lax.dot_general(dy_b[:, hs], xdt_b[:, hs], NT_DIMS, preferred_element_type=F32)
                dxd[:, hs] += lax.dot_general(m.astype(BF16), dy_b[:, hs], TN_DIMS, preferred_element_type=F32)
                dcb += dm * lam
                wl = dm * m
                dacum += jnp.where(lane == h, jnp.sum(wl, axis=1, keepdims=True), 0.0)
                dac_t[...] -= jnp.where(sub == h, jnp.sum(wl, axis=0, keepdims=True), 0.0)
            dcb_b = dcb.astype(BF16)
            dc_ref[:, ss] = dcg + jnp.dot(dcb_b, bg, preferred_element_type=F32)
            db_ref[:, ss] = dbg + lax.dot_general(dcb_b, cg, TN_DIMS, preferred_element_type=F32)

        dxdt = dxd[...]
        dxs_ref[...] = dy * dsk_ref[...] + dxdt * cm["dt_e"]
        dacum += squeeze(jnp.concatenate(w1, axis=1)) + dac_t[...].T
        dlast = squeeze(jnp.broadcast_to(jnp.concatenate(dgam, axis=1), (8, D_SSD)))[0:1, :]
        dacum += jnp.where(lax.broadcasted_iota(I32, (ln, LANES), 0) == ln - 1, dlast, 0.0)
        triu = (cm["ci"] >= cm["ri"]).astype(F32)
        da = jnp.dot(triu, dacum, preferred_element_type=F32, precision=HI)
        ddt = da * cm["a_neg"] + squeeze(dxdt * xsv)
        dalog_ref[...] += jnp.sum(da * cm["dt_s"], axis=0, keepdims=True) * cm["a_neg"]
        ddt_raw = jnp.where(lane < SSD_HEADS, ddt * _sigmoid(cm["pre"]), 0.0)
        ddt_ref[...] = ddt_raw
        ddtb_ref[...] += jnp.sum(ddt_raw, axis=0, keepdims=True)

    row = lambda d: pl.BlockSpec((None, ln, d), lambda bb, c: (bb, nc - 1 - c, 0))
    return pl.pallas_call(
        body, name=name, grid=(b, nc),
        in_specs=[row(D_SSD), row(D_SSD), row(D_SSD), row(D_SSD), row(256), row(256), row(LANES),
                  pl.BlockSpec((None, None, SSD_STATE, D_SSD), lambda bb, c: (bb, nc - 1 - c, 0, 0)),
                  _full((1, LANES)), _full((1, LANES)), _full((1, D_SSD)), _full((1, D_SSD)),
                  _full((LANES, D_SSD)), _full((D_SSD, LANES))],
        out_specs=[row(D_SSD), row(256), row(256), row(D_SSD), row(LANES),
                   _full((1, D_SSD)), _full((1, D_SSD)), _full((1, LANES)), _full((1, LANES))],
        out_shape=[SDS((b, s, D_SSD), F32), SDS((b, s, 256), F32), SDS((b, s, 256), F32), SDS((b, s, D_SSD), BF16),
                   SDS((b, s, LANES), F32), SDS((1, D_SSD), F32), SDS((1, D_SSD), F32), SDS((1, LANES), F32),
                   SDS((1, LANES), F32)],
        scratch_shapes=[pltpu.VMEM((SSD_STATE, D_SSD), F32), pltpu.VMEM((ln, D_SSD), F32), pltpu.VMEM((LANES, ln), F32)],
        compiler_params=_cparams(2))(dys, y, z, xs, bm, cm_, misc, prev, dtb, alog, dskip_e, norm_w, e_mat, et_mat)


def _rope(xv, cc, sp, sm):
    n = xv.shape[1]
    return xv * cc + pltpu.roll(xv, 16, 1) * sp + pltpu.roll(xv, n - 16, 1) * sm


def _rope_bwd(dy, cc, sp, sm):
    n = dy.shape[1]
    return dy * cc + pltpu.roll(dy * sp, n - 16, 1) + pltpu.roll(dy * sm, 16, 1)


def _tile8(t):
    return jnp.concatenate([t] * MLA_HEADS, axis=1)


def qkv_fwd(cq, ckv, misc, cc, sp, sm, qnw, kvnw, wuq_t, wukv_t, place, name):
    b, s, _ = cq.shape
    tm = min(256, s)
    hd = MLA_HEADS * HEAD_PAD

    def body(cq_ref, ckv_ref, misc_ref, cc_ref, sp_ref, sm_ref, qnw_ref, kvnw_ref, wq_ref, wkv_ref, pl_ref,
             q_ref, k_ref, v_ref, qn_ref, kvn_ref):
        cqv, ckvv = cq_ref[...], ckv_ref[...]
        qn = (cqv * _rms(cqv) * qnw_ref[...]).astype(BF16)
        kvn = (ckvv * _rms(ckvv) * kvnw_ref[...]).astype(BF16)
        qn_ref[...] = qn
        kvn_ref[...] = kvn
        cc1, sp1, sm1 = cc_ref[...], sp_ref[...], sm_ref[...]
        q = lax.dot_general(qn, wq_ref[...], NT_DIMS, preferred_element_type=F32)
        q_ref[...] = _rope(q, _tile8(cc1), _tile8(sp1), _tile8(sm1)).astype(BF16)
        kv = lax.dot_general(kvn, wkv_ref[...], NT_DIMS, preferred_element_type=F32)
        kr = jnp.dot(misc_ref[...], pl_ref[...], preferred_element_type=F32, precision=HI)
        kr = _rope(kr, cc1, sp1, sm1)
        k_ref[...] = (kv[:, 0:hd] + _tile8(kr)).astype(BF16)
        v_ref[...] = kv[:, hd:2 * hd].astype(BF16)

    return pl.pallas_call(
        body, name=name, grid=(b, s // tm),
        in_specs=[_row(tm, Q_LORA), _row(tm, KV_LORA), _row(tm, LANES), _row(tm, LANES), _row(tm, LANES), _row(tm, LANES),
                  _full((1, Q_LORA)), _full((1, KV_LORA)), _full(wuq_t.shape), _full(wukv_t.shape), _full((LANES, LANES))],
        out_specs=[_row(tm, hd), _row(tm, hd), _row(tm, hd), _row(tm, Q_LORA), _row(tm, KV_LORA)],
        out_shape=[SDS((b, s, hd), BF16)] * 3 + [SDS((b, s, Q_LORA), BF16), SDS((b, s, KV_LORA), BF16)],
        compiler_params=_cparams(2))(cq, ckv, misc, cc, sp, sm, qnw, kvnw, wuq_t, wukv_t, place)


def qkv_bwd(dq, dk, dv, ddt, cq, ckv, cc, sp, sm, qnw, kvnw, wuq_t, wukv_t, place_t, name):
    b, s, _ = cq.shape
    tm = min(256, s)
    hd = MLA_HEADS * HEAD_PAD

    def body(dq_ref, dk_ref, dv_ref, ddt_ref, cq_ref, ckv_ref, cc_ref, sp_ref, sm_ref, qnw_ref, kvnw_ref,
             wq_ref, wkv_ref, plt_ref, dcq_ref, dckv_ref, dmisc_ref, dqp_ref, dkv_ref, dqnw_ref, dkvnw_ref):
        @pl.when(_first_step())
        def _():
            dqnw_ref[...] = jnp.zeros_like(dqnw_ref)
            dkvnw_ref[...] = jnp.zeros_like(dkvnw_ref)
        cc1, sp1, sm1 = cc_ref[...], sp_ref[...], sm_ref[...]
        dqp = _rope_bwd(dq_ref[...], _tile8(cc1), _tile8(sp1), _tile8(sm1)).astype(BF16)
        dqp_ref[...] = dqp
        dkf = dk_ref[...]
        dkv_b = jnp.concatenate([dkf, dv_ref[...]], axis=1).astype(BF16)
        dkv_ref[...] = dkv_b
        dkr = dkf[:, 0:HEAD_PAD]
        for h in range(1, MLA_HEADS):
            dkr += dkf[:, HEAD_PAD * h:HEAD_PAD * (h + 1)]
        dkr = _rope_bwd(dkr, cc1, sp1, sm1)
        dmisc_ref[...] = (jnp.dot(dkr, plt_ref[...], preferred_element_type=F32, precision=HI) + ddt_ref[...]).astype(BF16)

        def norm_bwd(dn_w, xv, w_ref, dw_ref, dx_ref):
            r = _rms(xv)
            n = xv * r
            dw_ref[...] += jnp.sum(dn_w * n, axis=0, keepdims=True)
            dx_ref[...] = _rms_bwd(dn_w * w_ref[...], n, r).astype(BF16)

        norm_bwd(jnp.dot(dqp, wq_ref[...], preferred_element_type=F32), cq_ref[...], qnw_ref, dqnw_ref, dcq_ref)
        norm_bwd(jnp.dot(dkv_b, wkv_ref[...], preferred_element_type=F32), ckv_ref[...], kvnw_ref, dkvnw_ref, dckv_ref)

    return pl.pallas_call(
        body, name=name, grid=(b, s // tm),
        in_specs=[_row(tm, hd), _row(tm, hd), _row(tm, hd), _row(tm, LANES), _row(tm, Q_LORA), _row(tm, KV_LORA),
                  _row(tm, LANES), _row(tm, LANES), _row(tm, LANES), _full((1, Q_LORA)), _full((1, KV_LORA)),
                  _full(wuq_t.shape), _full(wukv_t.shape), _full((LANES, LANES))],
        out_specs=[_row(tm, Q_LORA), _row(tm, KV_LORA), _row(tm, LANES), _row(tm, hd), _row(tm, 2 * hd),
                   _full((1, Q_LORA)), _full((1, KV_LORA))],
        out_shape=[SDS((b, s, Q_LORA), BF16), SDS((b, s, KV_LORA), BF16), SDS((b, s, LANES), BF16),
                   SDS((b, s, hd), BF16), SDS((b, s, 2 * hd), BF16), SDS((1, Q_LORA), F32), SDS((1, KV_LORA), F32)],
        compiler_params=_cparams(2))(dq, dk, dv, ddt, cq, ckv, cc, sp, sm, qnw, kvnw, wuq_t, wukv_t, place_t)


ATT_SCALE = 1.0 / math.sqrt(QK_DIM)


ATT_HEADS_PER_STEP = 2


def _att_tile(s):
    return min(512, s)


def flash_fwd(q, k, v, name):
    b, s, hd = q.shape
    t = _att_tile(s)
    nb = s // t
    th = t // 2
    vt = v.reshape(b, nb, t, MLA_HEADS, HEAD_PAD).transpose(0, 3, 1, 4, 2)

    hps = ATT_HEADS_PER_STEP
    hw = hps * HEAD_PAD

    def body(q_ref, k_ref, vt_ref, o_ref, lse_ref, m_s, l_s, acc):
        i = pl.program_id(2)
        m_s[...] = jnp.full_like(m_s, -jnp.inf)
        l_s[...] = jnp.zeros_like(l_s)
        acc[...] = jnp.zeros_like(acc)

        def update(j, diagonal):
            ks = pl.ds(pl.multiple_of(j * t, t), t)
            chains = [(hh, half) for hh in range(hps) for half in range(2)]
            lanes = lambda hh: slice(HEAD_PAD * hh, HEAD_PAD * (hh + 1))
            cols = lambda half: slice(th * half, th * (half + 1))
            sts = {}
            for hh, half in chains:
                st = lax.dot_general(k_ref[ks, lanes(hh)], q_ref[cols(half), lanes(hh)], NT_DIMS,
                                     preferred_element_type=F32) * ATT_SCALE
                if diagonal:
                    row = lax.broadcasted_iota(I32, (t, th), 0)
                    col = lax.broadcasted_iota(I32, (t, th), 1) + th * half
                    st = jnp.where(row <= col, st, -jnp.inf)
                sts[hh, half] = st
            pts, alphas = {}, {}
            for hh, half in chains:
                st, cs = sts[hh, half], cols(half)
                m_prev = m_s[hh, :, cs]
                m_new = jnp.maximum(m_prev, jnp.max(st, axis=0, keepdims=True))
                alpha = jnp.exp(m_prev - m_new)
                pt = jnp.exp(st - m_new)
                l_s[hh, :, cs] = alpha * l_s[hh, :, cs] + jnp.sum(pt, axis=0, keepdims=True)
                m_s[hh, :, cs] = m_new
                pts[hh, half], alphas[hh, half] = pt.astype(BF16), alpha
            for hh, half in chains:
                cs = cols(half)
                acc[hh, :, cs] = alphas[hh, half] * acc[hh, :, cs] + jnp.dot(vt_ref[hh, j], pts[hh, half],
                                                                             preferred_element_type=F32)

        def step(j, carry):
            update(j, False)
            return carry

        lax.fori_loop(0, i, step, 0)
        update(i, True)
        for hh in range(hps):
            o_ref[:, HEAD_PAD * hh:HEAD_PAD * (hh + 1)] = (acc[hh] / l_s[hh]).T
            lse_ref[hh] = m_s[hh] + jnp.log(l_s[hh])

    qs = pl.BlockSpec((None, t, hw), lambda bb, h, i: (bb, i, h))
    ks = pl.BlockSpec((None, s, hw), lambda bb, h, i: (bb, 0, h))
    vs = pl.BlockSpec((None, hps, nb, HEAD_PAD, t), lambda bb, h, i: (bb, h, 0, 0, 0))
    ls = pl.BlockSpec((None, hps, None, 1, t), lambda bb, h, i: (bb, h, i, 0, 0))
    return pl.pallas_call(
        body, name=name, grid=(b, MLA_HEADS // hps, nb),
        in_specs=[qs, ks, vs], out_specs=[qs, ls],
        out_shape=[SDS((b, s, hd), F32), SDS((b, MLA_HEADS, nb, 1, t), F32)],
        scratch_shapes=[pltpu.VMEM((hps, 1, t), F32), pltpu.VMEM((hps, 1, t), F32), pltpu.VMEM((hps, HEAD_PAD, t), F32)],
        compiler_params=_cparams(3))(q, k, vt)


def flash_bwd(q, k, v, do, lse, dlt, name):
    b, s, hd = q.shape
    t = _att_tile(s)
    nb = s // t
    th = t // 2
    lse_r = lse
    dlt_r = dlt.reshape(b, MLA_HEADS, nb, 1, t)

    hps = ATT_HEADS_PER_STEP
    hw = hps * HEAD_PAD

    def body(q_ref, k_ref, v_ref, do_ref, lse_ref, dlt_ref, dq_ref, dk_ref, dv_ref):
        dq_ref[...] = jnp.zeros_like(dq_ref)
        dk_ref[...] = jnp.zeros_like(dk_ref)
        dv_ref[...] = jnp.zeros_like(dv_ref)

        def tile(j, i, diagonal):
            qs = pl.ds(pl.multiple_of(i * t, t), t)
            chains = [(hh, half) for hh in range(hps) for half in range(2)]
            lanes = lambda hh: slice(HEAD_PAD * hh, HEAD_PAD * (hh + 1))
            keys = lambda half: pl.ds(pl.multiple_of(j * t + th * half, th), th)
            sts, dpts = {}, {}
            for hh, half in chains:
                ls_, ks = lanes(hh), keys(half)
                st = lax.dot_general(k_ref[ks, ls_], q_ref[qs, ls_], NT_DIMS, preferred_element_type=F32) * ATT_SCALE
                if diagonal:
                    row = lax.broadcasted_iota(I32, (th, t), 0) + th * half
                    col = lax.broadcasted_iota(I32, (th, t), 1)
                    st = jnp.where(row <= col, st, -jnp.inf)
                sts[hh, half] = st
                dpts[hh, half] = lax.dot_general(v_ref[ks, ls_], do_ref[qs, ls_], NT_DIMS, preferred_element_type=F32)
            pts, dsts = {}, {}
            for hh, half in chains:
                pt = jnp.exp(sts[hh, half] - lse_ref[hh, i])
                pts[hh, half] = pt.astype(BF16)
                dsts[hh, half] = (pt * (dpts[hh, half] - dlt_ref[hh, i]) * ATT_SCALE).astype(BF16)
            for hh in range(hps):
                ls_ = lanes(hh)
                dq_acc = None
                for half in range(2):
                    ks = keys(half)
                    dv_ref[ks, ls_] += jnp.dot(pts[hh, half], do_ref[qs, ls_], preferred_element_type=F32)
                    dk_ref[ks, ls_] += jnp.dot(dsts[hh, half], q_ref[qs, ls_], preferred_element_type=F32)
                    part = lax.dot_general(dsts[hh, half], k_ref[ks, ls_], TN_DIMS, preferred_element_type=F32)
                    dq_acc = part if dq_acc is None else dq_acc + part
                dq_ref[qs, ls_] += dq_acc

        def key_tile(j, carry):
            tile(j, j, True)

            def query_tile(i, c2):
                tile(j, i, False)
                return c2

            lax.fori_loop(j + 1, nb, query_tile, 0)
            return carry

        lax.fori_loop(0, nb, key_tile, 0)

    hs = pl.BlockSpec((None, s, hw), lambda bb, h: (bb, 0, h))
    ls = pl.BlockSpec((None, hps, nb, 1, t), lambda bb, h: (bb, h, 0, 0, 0))
    return pl.pallas_call(
        body, name=name, grid=(b, MLA_HEADS // hps),
        in_specs=[hs, hs, hs, hs, ls, ls], out_specs=[hs, hs, hs],
        out_shape=[SDS((b, s, hd), F32)] * 3, compiler_params=_cparams(2))(q, k, v, do, lse_r, dlt_r)


def out_proj(ys, attn, mnw, wo, x, gate, name):
    b, s, d = x.shape
    tm = min(256, s)

    def body(ys_ref, at_ref, mnw_ref, wo_ref, x_ref, g_ref, xn_ref, o_ref, ym_ref):
        av = at_ref[...]
        ym = (av * _rms(av) * mnw_ref[...]).astype(BF16)
        ym_ref[...] = ym
        o = jnp.dot(ys_ref[...], wo_ref[0:D_SSD, :], preferred_element_type=F32)
        o += jnp.dot(ym, wo_ref[D_SSD:2 * D_SSD, :], preferred_element_type=F32)
        xn_ref[...] = x_ref[...] + g_ref[...] * o
        o_ref[...] = o.astype(BF16)

    return pl.pallas_call(
        body, name=name, grid=(b, s // tm),
        in_specs=[_row(tm, D_SSD), _row(tm, D_SSD), _full((1, D_SSD)), _full(wo.shape), _row(tm, d), _bvec(d)],
        out_specs=[_row(tm, d), _row(tm, d), _row(tm, D_SSD)],
        out_shape=[SDS((b, s, d), F32), SDS((b, s, d), BF16), SDS((b, s, D_SSD), BF16)],
        compiler_params=_cparams(2))(ys, attn, mnw, wo, x, gate)


def out_proj_bwd(dout, attn, mnw, wo, name):
    b, s, d = dout.shape
    tm = min(256, s)

    def body(do_ref, at_ref, mnw_ref, wo_ref, dys_ref, dat_ref, dlt_ref, dw_ref):
        @pl.when(_first_step())
        def _():
            dw_ref[...] = jnp.zeros_like(dw_ref)
        dov = do_ref[...]
        dys_ref[...] = lax.dot_general(dov, wo_ref[0:D_SSD, :], NT_DIMS, preferred_element_type=F32)
        dym = lax.dot_general(dov, wo_ref[D_SSD:2 * D_SSD, :], NT_DIMS, preferred_element_type=F32)
        av = at_ref[...]
        r = _rms(av)
        n = av * r
        dw_ref[...] += jnp.sum(dym * n, axis=0, keepdims=True)
        dat = _rms_bwd(dym * mnw_ref[...], n, r)
        dat_ref[...] = dat.astype(BF16)
        prod = dat * av
        for h in range(MLA_HEADS):
            dlt_ref[h] = jnp.sum(prod[:, HEAD_PAD * h:HEAD_PAD * (h + 1)], axis=1, keepdims=True)

    return pl.pallas_call(
        body, name=name, grid=(b, s // tm),
        in_specs=[_row(tm, d), _row(tm, D_SSD), _full((1, D_SSD)), _full(wo.shape)],
        out_specs=[_row(tm, D_SSD), _row(tm, D_SSD),
                   pl.BlockSpec((None, MLA_HEADS, tm, 1), lambda bb, i: (bb, 0, i, 0)), _full((1, D_SSD))],
        out_shape=[SDS((b, s, D_SSD), F32), SDS((b, s, D_SSD), BF16), SDS((b, MLA_HEADS, s, 1), F32),
                   SDS((1, D_SSD), F32)],
        compiler_params=_cparams(2))(dout, attn, mnw, wo)


def adaln_fwd(c_all, w_ada, b_ada, name):
    nb, d = c_all.shape
    n = w_ada.shape[1]

    def body(c_ref, w_ref, b_ref, m_ref, ca_ref):
        cv = c_ref[...]
        ca = (cv * _sigmoid(cv)).astype(BF16)
        ca_ref[...] = ca
        m_ref[...] = jnp.dot(ca, w_ref[...].astype(BF16), preferred_element_type=F32) + b_ref[...]

    return pl.pallas_call(
        body, name=name, out_shape=[SDS((nb, n), F32), SDS((nb, d), BF16)],
        compiler_params=pltpu.CompilerParams(vmem_limit_bytes=VMEM_LIMIT))(c_all, w_ada, b_ada)


def adaln_bwd(c_act, dmod_cols, name):
    d, n = c_act.shape[1], dmod_cols.shape[1]

    def body(c_ref, dm_ref, gw_ref):
        gw_ref[...] = lax.dot_general(c_ref[...], dm_ref[...].astype(BF16), TN_DIMS, preferred_element_type=F32)

    return pl.pallas_call(
        body, name=name, out_shape=SDS((d, n), F32),
        compiler_params=pltpu.CompilerParams(vmem_limit_bytes=VMEM_LIMIT))(c_act, dmod_cols)


def sum_rows(x, name):
    def body(x_ref, o_ref):
        o_ref[...] = jnp.sum(x_ref[...], axis=0, keepdims=True)
    return pl.pallas_call(body, name=name, out_shape=SDS((1, x.shape[1]), F32))(x)


def squeeze_heads(x, et_mat, name):
    def body(x_ref, et_ref, o_ref):
        xv = jnp.broadcast_to(x_ref[...], (8, x.shape[1]))
        o_ref[...] = _dot_01(xv, et_ref[...], 3)[0:1, :]
    return pl.pallas_call(body, name=name, out_shape=SDS((1, LANES), F32))(x, et_mat)


def sum_blocks(x, name):
    n, r, c = x.shape

    def body(x_ref, o_ref):
        acc = x_ref[0].astype(F32)
        for k in range(1, n):
            acc += x_ref[k].astype(F32)
        o_ref[...] = acc

    return pl.pallas_call(body, name=name, out_shape=SDS((r, c), F32),
                          compiler_params=pltpu.CompilerParams(vmem_limit_bytes=VMEM_LIMIT))(x)


def _adam_math(w, g, m, v):
    m = ADAM_B1 * m + (1.0 - ADAM_B1) * g
    v = ADAM_B2 * v + (1.0 - ADAM_B2) * (g * g)
    m_hat = m / (1.0 - ADAM_B1 ** ADAM_STEP)
    v_hat = v / (1.0 - ADAM_B2 ** ADAM_STEP)
    return -ADAM_LR * (m_hat / (jnp.sqrt(v_hat) + ADAM_EPS) + ADAM_WD * w), m, v


def adamw(w, g, m, v, name):
    r, c = w.shape
    tr = r
    for cand in (512, 256, 128, 64, 32, 16, 8):
        if r % cand == 0 and cand * c * 4 <= 2 * 1024 * 1024:
            tr = cand
            break

    def body(w_ref, g_ref, m_ref, v_ref, d_ref, mo_ref, vo_ref):
        d_ref[...], mo_ref[...], vo_ref[...] = _adam_math(w_ref[...], g_ref[...], m_ref[...], v_ref[...])

    spec = pl.BlockSpec((tr, c), lambda i: (i, 0))
    return pl.pallas_call(
        body, name=name, grid=(r // tr,), in_specs=[spec] * 4, out_specs=[spec] * 3,
        out_shape=[SDS((r, c), F32)] * 3, compiler_params=_cparams(1))(w, g, m, v)


def adamw_sum8(w, gparts, m, v, name):
    r, c = w.shape
    tr = next(cand for cand in (64, 32, 16, 8) if r % cand == 0)

    def body(w_ref, gp_ref, m_ref, v_ref, g_ref, d_ref, mo_ref, vo_ref):
        g = gp_ref[0].astype(F32)
        for k in range(1, N_DEV):
            g += gp_ref[k].astype(F32)
        g_ref[...] = g
        d_ref[...], mo_ref[...], vo_ref[...] = _adam_math(w_ref[...], g, m_ref[...], v_ref[...])

    spec = pl.BlockSpec((tr, c), lambda i: (i, 0))
    gspec = pl.BlockSpec((N_DEV, tr, c), lambda i: (0, i, 0))
    return pl.pallas_call(
        body, name=name, grid=(r // tr,), in_specs=[spec, gspec, spec, spec], out_specs=[spec] * 4,
        out_shape=[SDS((r, c), F32)] * 4, compiler_params=_cparams(1))(w, gparts, m, v)


PACK = {"ffn1_w_gate": (352, 352), "ffn1_w_up": (352, 352), "ffn1_w_down": (352, 352),
        "ffn2_w_gate": (352, 352), "ffn2_w_up": (352, 352), "ffn2_w_down": (352, 352),
        "w_out": (256, 256), "w_in": (406, 416), "w_ukv": (48, 48), "w_uq": (36, 48)}
TRANSPOSED = ("ffn1_w_gate", "ffn1_w_up", "ffn2_w_gate", "ffn2_w_up", "w_in", "w_ukv", "w_uq")
GROUP_FFN1 = ("ffn1_w_gate", "ffn1_w_up", "ffn1_w_down")
GROUP_REST = ("ffn2_w_gate", "ffn2_w_up", "ffn2_w_down", "w_out", "w_in", "w_ukv", "w_uq")
GRAD_GROUPS = (("ffn2", ("ffn2_w_gate", "ffn2_w_up", "ffn2_w_down")), ("mixer", ("w_out", "w_in", "w_ukv", "w_uq")),
               ("ffn1_down", ("ffn1_w_down",)), ("ffn1_up", ("ffn1_w_gate", "ffn1_w_up")))


def _pack_offsets(names):
    off, o = {}, 0
    for n in names:
        off[n] = (o, PACK[n][0])
        o += PACK[n][1]
    return off, o


def _shard_to_rows(name, w):
    w = w[0]
    if name in TRANSPOSED:
        w = w.T
    return w.reshape(-1, D_MODEL)


def _rows_to_shard(name, rows, like):
    shp = like.shape[1:]
    if name in TRANSPOSED:
        return rows.reshape(shp[1], shp[0]).T[None]
    return rows.reshape(shp)[None]


def _pack_shards(ws, names, dtype):
    parts = []
    for name in names:
        real, padded = PACK[name]
        rows = _shard_to_rows(name, ws[name]).astype(dtype)
        if padded > real:
            rows = jnp.pad(rows, ((0, padded - real), (0, 0)))
        parts.append(rows)
    return jnp.concatenate(parts, axis=0)


def _grad_rows(name, gw):
    real, padded = PACK[name]
    if name == "w_in":
        rows = _in_proj_rows_inv(gw).reshape(N_DEV, -1, D_MODEL)
    elif name == "w_ukv":
        hd = MLA_HEADS * HEAD_PAD
        rows = jnp.concatenate([gw[:hd].reshape(MLA_HEADS, HEAD_PAD, KV_LORA)[:, :QK_NOPE],
                                gw[hd:].reshape(MLA_HEADS, V_HEAD, KV_LORA)], axis=1).reshape(N_DEV, -1, D_MODEL)
    elif name == "w_uq":
        rows = gw.reshape(MLA_HEADS, HEAD_PAD, Q_LORA)[:, :QK_DIM].reshape(N_DEV, -1, D_MODEL)
    else:
        rows = gw.reshape(N_DEV, -1, D_MODEL)
    if padded > real:
        rows = jnp.pad(rows, ((0, 0), (0, padded - real), (0, 0)))
    return rows


def _pack_rows(arrs):
    parts = []
    for a in arrs:
        flat = a.reshape(-1).astype(F32)
        pad = (-flat.shape[0]) % D_MODEL
        if pad:
            flat = jnp.pad(flat, (0, pad))
        parts.append(flat.reshape(-1, D_MODEL))
    out = jnp.concatenate(parts, axis=0)
    pad = (-out.shape[0]) % 8
    if pad:
        out = jnp.pad(out, ((0, pad), (0, 0)))
    return out


def _unpack_rows(packed, shapes):
    out, row = [], 0
    for shp in shapes:
        n = math.prod(shp)
        nrow = -(-n // D_MODEL)
        out.append(packed[row:row + nrow].reshape(-1)[:n].reshape(shp))
        row += nrow
    return out


def _in_proj_rows(w_t):
    return jnp.concatenate([w_t[0:2560], w_t[2576:2960], w_t[2960:3216], w_t[2560:2576], w_t[3216:3248],
                            jnp.zeros((D_IN_PAD - D_IN, D_MODEL), w_t.dtype)], axis=0)


def _in_proj_rows_inv(d):
    return jnp.concatenate([d[0:2560], d[3200:3216], d[2560:2944], d[2944:3200], d[3216:3248]], axis=0)


def _rope_tables(positions):
    inv_freq = ROPE_THETA ** (-jnp.arange(0, QK_ROPE, 2, dtype=F32) / QK_ROPE)
    ang = positions[..., None].astype(F32) * inv_freq
    cos, sin = jnp.cos(ang), jnp.sin(ang)
    one = jnp.ones(ang.shape[:2] + (QK_NOPE,), F32)
    zero = jnp.zeros_like(one)
    z16, z32, o32 = zero[..., :16], zero[..., :32], one[..., :32]
    cc = jnp.concatenate([one, cos, cos, o32], axis=-1)
    sp = jnp.concatenate([zero, z16, sin, z32], axis=-1)
    sm = jnp.concatenate([zero, -sin, z16, z32], axis=-1)
    return cc, sp, sm


def weight_views(g_ffn1, g_rest):
    def _seg(name):
        names, g = (GROUP_FFN1, g_ffn1) if name in GROUP_FFN1 else (GROUP_REST, g_rest)
        o, r = _pack_offsets(names)[0][name]
        return g[:, o:o + r]

    full = lambda name: _seg(name).reshape(-1, D_MODEL)
    ukv = _seg("w_ukv").reshape(MLA_HEADS, QK_NOPE + V_HEAD, KV_LORA)
    wukv_t = jnp.concatenate([jnp.pad(ukv[:, :QK_NOPE], ((0, 0), (0, HEAD_PAD - QK_NOPE), (0, 0))).reshape(-1, KV_LORA),
                              ukv[:, QK_NOPE:].reshape(-1, KV_LORA)], axis=0)
    uq = _seg("w_uq").reshape(MLA_HEADS, QK_DIM, Q_LORA)
    wuq_t = jnp.pad(uq, ((0, 0), (0, HEAD_PAD - QK_DIM), (0, 0))).reshape(-1, Q_LORA)
    return dict(wg1_t=full("ffn1_w_gate"), wu1_t=full("ffn1_w_up"), wd1=full("ffn1_w_down"),
                wg2_t=full("ffn2_w_gate"), wu2_t=full("ffn2_w_up"), wd2=full("ffn2_w_down"),
                wo=full("w_out"), win_t=_in_proj_rows(full("w_in")), wukv_t=wukv_t, wuq_t=wuq_t)


def _ffn_bwd(tag, dxn, do, dgate, x, h, gg, uu, a, sc, norm_w, wg_t, wu_t, wd, below):
    f2 = wd.shape[0] // 2
    dgg, duu = ffn_dact(do, wd, gg, uu, tag + "_dact")
    dwd = mm_tn(a, do, f2, D_MODEL, tag + "_dwd")
    dwg_t = mm_tn(dgg, h, f2, D_MODEL, tag + "_dwg")
    dwu_t = mm_tn(duu, h, f2, D_MODEL, tag + "_dwu")
    dx, dsc, dsh, dnw, *nxt = dh_norm_bwd([dgg, duu], [wg_t, wu_t], x, dxn, norm_w, sc, tag + "_dh", below)
    return dx, (dsh, dsc, dgate), dnw, (dwg_t, dwu_t, dwd), nxt


def local_step(x, tgt, positions, mod, wv, p):
    nb, s, d = x.shape
    sh1, sc1, g1, sh2, sc2, g2, sh3, sc3, g3 = mod
    cc, sp, sm = _rope_tables(positions)
    lane_head = jnp.arange(D_SSD, dtype=I32)[None, :] // SSD_HEAD_DIM
    e_mat = (lane_head == jnp.arange(LANES, dtype=I32)[:, None]).astype(BF16)
    et_mat = e_mat.T
    rr, cl = jnp.arange(LANES, dtype=I32)[:, None], jnp.arange(LANES, dtype=I32)[None, :]
    place = ((cl == rr + (QK_NOPE - SSD_HEADS)) & (rr >= SSD_HEADS) & (rr < SSD_HEADS + QK_ROPE)).astype(F32)
    dtb = jnp.pad(p["dt_bias"], ((0, 0), (0, LANES - SSD_HEADS)))
    alog = jnp.pad(p["a_log"], ((0, 0), (0, LANES - SSD_HEADS)))
    dskip_e = jnp.repeat(p["d_skip"], SSD_HEAD_DIM, axis=1)

    h1 = norm_mod(x, p["norm_ffn1"], sc1, sh1, "ffn1_norm")
    gg1, uu1, a1 = ffn_up(h1, wv["wg1_t"], wv["wu1_t"], "ffn1_up")
    x1, o1 = ffn_down(a1, wv["wd1"], x, g1, 0.5, "ffn1_down")
    h2 = norm_mod(x1, p["norm_mix"], sc2, sh2, "mix_norm")
    z, u, cq, ckv, misc = in_proj(h2, wv["win_t"], "in_proj")
    xs, bm, cm_ = conv_fwd(u, p["conv_w"], p["conv_b"], "conv_fwd")
    ys, y, prev = ssd_fwd(xs, bm, cm_, misc, z, dtb, alog, dskip_e, p["ssd_norm_w"], e_mat, "ssd_fwd")
    q, k, v, qn, kvn = qkv_fwd(cq, ckv, misc, cc, sp, sm, p["q_norm_w"], p["kv_norm_w"], wv["wuq_t"], wv["wukv_t"],
                               place, "qkv_fwd")
    attn, lse = flash_fwd(q, k, v, "flash_fwd")
    x2, o2, ym = out_proj(ys, attn, p["mla_norm_w"], wv["wo"], x1, g2, "out_proj")
    h3 = norm_mod(x2, p["norm_ffn2"], sc3, sh3, "ffn2_norm")
    gg3, uu3, a3 = ffn_up(h3, wv["wg2_t"], wv["wu2_t"], "ffn2_up")
    x3, o3 = ffn_down(a3, wv["wd2"], x2, g3, 0.5, "ffn2_down")
    loss, dx3, dnfin, do3, dg3 = final_loss(x3, p["norm_final"], tgt, (o3, g3, 0.5), "final_loss")

    dx2, dmod3, dnf2, (dwg2, dwu2, dwd2), (dout, dg2) = _ffn_bwd(
        "ffn2", dx3, do3, dg3, x2, h3, gg3, uu3, a3, sc3, p["norm_ffn2"], wv["wg2_t"], wv["wu2_t"], wv["wd2"],
        (o2, g2, 1.0))
    dys, dattn, dlt, dmlan = out_proj_bwd(dout, attn, p["mla_norm_w"], wv["wo"], "out_proj_bwd")
    dwo = jnp.concatenate([mm_tn(ys, dout, D_SSD, D_MODEL, "dwo_ssd"), mm_tn(ym, dout, D_SSD, D_MODEL, "dwo_mla")], axis=0)
    dxs, dbm, dcm, dz, ddt, dssdn, ddsk_lane, ddtb, dalog = ssd_bwd(
        dys, y, z, xs, bm, cm_, misc, prev, dtb, alog, dskip_e, p["ssd_norm_w"], e_mat, et_mat, "ssd_bwd")
    dq, dk, dv = flash_bwd(q, k, v, dattn, lse, dlt, "flash_bwd")
    dcq, dckv, dmisc, dqp, dkvc, dqn, dkvn = qkv_bwd(dq, dk, dv, ddt, cq, ckv, cc, sp, sm, p["q_norm_w"], p["kv_norm_w"],
                                                     wv["wuq_t"], wv["wukv_t"], place.T, "qkv_bwd")
    dwuq = mm_tn(dqp, qn, MLA_HEADS * HEAD_PAD, Q_LORA, "dwuq")
    dwukv = mm_tn(dkvc, kvn, MLA_HEADS * HEAD_PAD, KV_LORA, "dwukv")
    dvv, dconv = conv_bwd_a(dxs, dbm, dcm, u, p["conv_w"], p["conv_b"], "conv_bwd_a")
    du = conv_bwd_b(dvv, p["conv_w"], "conv_bwd_b")
    dproj = jnp.concatenate([dz, du, dcq, dckv, dmisc], axis=-1)
    dwin = mm_tn(dproj, h2, D_IN_PAD // 2, D_MODEL, "dwin")
    dx1, dsc2, dsh2, dnmix, do1, dg1 = dh_norm_bwd([dproj], [wv["win_t"]], x1, dx2, p["norm_mix"], sc2, "mix_dh",
                                                   (o1, g1, 0.5))
    dx0, dmod1, dnf1, (dwg1, dwu1, dwd1), _ = _ffn_bwd(
        "ffn1", dx1, do1, dg1, x, h1, gg1, uu1, a1, sc1, p["norm_ffn1"], wv["wg1_t"], wv["wu1_t"], wv["wd1"], None)

    dmod = jnp.concatenate([*dmod1, dsh2, dsc2, dg2, *dmod3], axis=1).reshape(nb, N_MOD * d)
    return dict(
        loss=loss, dx=dx0, dmod=dmod, norm_ffn1=dnf1, norm_mix=dnmix, norm_ffn2=dnf2, norm_final=dnfin,
        ssd_norm_w=dssdn, mla_norm_w=dmlan, q_norm_w=dqn, kv_norm_w=dkvn,
        dt_bias=ddtb[:, :SSD_HEADS], a_log=dalog[:, :SSD_HEADS],
        d_skip=squeeze_heads(ddsk_lane, et_mat, "d_skip_heads")[:, :SSD_HEADS],
        conv_b=dconv[4:5], conv_w=dconv[0:4],
        gw=dict(ffn1_w_gate=dwg1, ffn1_w_up=dwu1, ffn1_w_down=dwd1, ffn2_w_gate=dwg2, ffn2_w_up=dwu2, ffn2_w_down=dwd2,
                w_out=dwo, w_in=dwin, w_ukv=dwukv, w_uq=dwuq))


def kernel(x, c, positions, w_ada, b_ada, norm_ffn1, ffn1_w_gate, ffn1_w_up, ffn1_w_down, norm_mix, w_in, conv_w, conv_b, dt_bias, a_log, d_skip, ssd_norm_w, q_norm_w, w_uq, kv_norm_w, w_ukv, mla_norm_w, w_out, norm_ffn2, ffn2_w_gate, ffn2_w_up, ffn2_w_down, norm_final, loss_target, m_w_ada, m_b_ada, m_norm_ffn1, m_ffn1_w_gate, m_ffn1_w_up, m_ffn1_w_down, m_norm_mix, m_w_in, m_conv_w, m_conv_b, m_dt_bias, m_a_log, m_d_skip, m_ssd_norm_w, m_q_norm_w, m_w_uq, m_kv_norm_w, m_w_ukv, m_mla_norm_w, m_w_out, m_norm_ffn2, m_ffn2_w_gate, m_ffn2_w_up, m_ffn2_w_down, m_norm_final, v_w_ada, v_b_ada, v_norm_ffn1, v_ffn1_w_gate, v_ffn1_w_up, v_ffn1_w_down, v_norm_mix, v_w_in, v_conv_w, v_conv_b, v_dt_bias, v_a_log, v_d_skip, v_ssd_norm_w, v_q_norm_w, v_w_uq, v_kv_norm_w, v_w_ukv, v_mla_norm_w, v_w_out, v_norm_ffn2, v_ffn2_w_gate, v_ffn2_w_up, v_ffn2_w_down, v_norm_final):
    names = ["w_ada", "b_ada", "norm_ffn1", "ffn1_w_gate", "ffn1_w_up", "ffn1_w_down", "norm_mix", "w_in", "conv_w",
             "conv_b", "dt_bias", "a_log", "d_skip", "ssd_norm_w", "q_norm_w", "w_uq", "kv_norm_w", "w_ukv",
             "mla_norm_w", "w_out", "norm_ffn2", "ffn2_w_gate", "ffn2_w_up", "ffn2_w_down", "norm_final"]
    W = dict(zip(names, (w_ada, b_ada, norm_ffn1, ffn1_w_gate, ffn1_w_up, ffn1_w_down, norm_mix, w_in, conv_w, conv_b, dt_bias, a_log, d_skip, ssd_norm_w, q_norm_w, w_uq, kv_norm_w, w_ukv, mla_norm_w, w_out, norm_ffn2, ffn2_w_gate, ffn2_w_up, ffn2_w_down, norm_final)))
    M = dict(zip(names, (m_w_ada, m_b_ada, m_norm_ffn1, m_ffn1_w_gate, m_ffn1_w_up, m_ffn1_w_down, m_norm_mix, m_w_in, m_conv_w, m_conv_b, m_dt_bias, m_a_log, m_d_skip, m_ssd_norm_w, m_q_norm_w, m_w_uq, m_kv_norm_w, m_w_ukv, m_mla_norm_w, m_w_out, m_norm_ffn2, m_ffn2_w_gate, m_ffn2_w_up, m_ffn2_w_down, m_norm_final)))
    V = dict(zip(names, (v_w_ada, v_b_ada, v_norm_ffn1, v_ffn1_w_gate, v_ffn1_w_up, v_ffn1_w_down, v_norm_mix, v_w_in, v_conv_w, v_conv_b, v_dt_bias, v_a_log, v_d_skip, v_ssd_norm_w, v_q_norm_w, v_w_uq, v_kv_norm_w, v_w_ukv, v_mla_norm_w, v_w_out, v_norm_ffn2, v_ffn2_w_gate, v_ffn2_w_up, v_ffn2_w_down, v_norm_final)))

    nb, s, d = x.shape
    me = 4 * lax.axis_index("x") + 2 * lax.axis_index("y") + lax.axis_index("c")
    n_ada = w_ada.shape[2]

    cshape = [(nb, d), conv_w.shape[1:]]
    cg = all_gather8(_pack_rows([c, conv_w[0]]), "gather_c")
    c_all = jnp.stack([_unpack_rows(cg[k], cshape)[0] for k in range(N_DEV)]).reshape(N_DEV * nb, d)
    conv_w_full = jnp.concatenate([_unpack_rows(cg[k], cshape)[1] for k in range(N_DEV)], axis=1)
    g_ffn1 = all_gather8(_pack_shards(W, GROUP_FFN1, BF16), "gather_w_ffn1")
    g_ffn1, rest = lax.optimization_barrier((g_ffn1, _pack_shards(W, GROUP_REST, BF16)))
    wv = weight_views(g_ffn1, sc_all_gather8(rest, "gather_w_rest", 1))

    b_ada_cols = lax.dynamic_slice(b_ada, (0, me * n_ada), (1, n_ada))
    mod_cols, c_act = adaln_fwd(c_all, w_ada[0], b_ada_cols, "adaln_fwd")
    mod_g = all_gather8(mod_cols, "gather_mod")
    mod = lax.dynamic_slice(mod_g, (0, me * nb, 0), (N_DEV, nb, n_ada)).transpose(1, 0, 2).reshape(nb, N_MOD, 1, d)
    mod = [mod[:, k] for k in range(N_MOD)]

    P = dict(W)
    P["conv_w"] = conv_w_full
    P["norm_final"] = norm_final.reshape(1, d)
    R = local_step(x, loss_target, positions, mod, wv, P)

    dmod = R["dmod"]
    partial_shapes = [(1,), (1, d), (1, d), (1, d), (1, d), (1, d), (1, d), (1, Q_LORA), (1, KV_LORA),
                      (1, SSD_HEADS), (1, SSD_HEADS), (1, SSD_HEADS), (1, D_CONV), (4, D_CONV), (1, N_MOD * d),
                      (nb, N_MOD * d)]
    partial = _pack_rows([R["loss"][0, :1], R["norm_ffn1"], R["norm_mix"], R["norm_ffn2"], R["norm_final"],
                          R["ssd_norm_w"], R["mla_norm_w"], R["q_norm_w"], R["kv_norm_w"],
                          R["dt_bias"], R["a_log"], R["d_skip"], R["conv_b"], R["conv_w"],
                          sum_rows(dmod, "dmod_rows"), dmod])
    partial_g = all_gather8(partial, "gather_partials")
    (loss, g_nf1, g_nmix, g_nf2, g_nfin, g_ssdn, g_mlan, g_qn, g_kvn, g_dtb, g_alog, g_dskip, g_convb, g_convw,
     g_bada, _) = _unpack_rows(sum_blocks(partial_g, "sum_partials"), partial_shapes)
    dmod_all = jnp.stack([_unpack_rows(partial_g[k], partial_shapes)[-1] for k in range(N_DEV)]).reshape(N_DEV * nb, -1)
    g_wada = adaln_bwd(c_act, lax.dynamic_slice(dmod_all, (0, me * n_ada), (N_DEV * nb, n_ada)), "adaln_bwd")
    n_cw = conv_w.shape[2]
    G = {"w_ada": g_wada[None], "b_ada": g_bada, "norm_ffn1": g_nf1, "norm_mix": g_nmix, "norm_ffn2": g_nf2,
         "norm_final": g_nfin.reshape(d), "ssd_norm_w": g_ssdn, "mla_norm_w": g_mlan, "q_norm_w": g_qn,
         "kv_norm_w": g_kvn, "dt_bias": g_dtb, "a_log": g_alog, "d_skip": g_dskip, "conv_b": g_convb,
         "conv_w": lax.dynamic_slice(g_convw, (0, me * n_cw), (4, n_cw))[None]}

    DW, NM, NV = {}, {}, {}
    gw = R["gw"]
    for k, (tag, group) in enumerate(GRAD_GROUPS):
        send = jnp.concatenate([_grad_rows(name, gw[name]) for name in group], axis=1).astype(BF16)
        recv = sc_all_to_all8(send, "exchange_" + tag, 2 + k)
        big = adamw_sum8(_pack_shards(W, group, F32), recv, _pack_shards(M, group, F32), _pack_shards(V, group, F32),
                         "adamw_" + tag)
        for name, (o, r) in _pack_offsets(group)[0].items():
            G[name], DW[name], NM[name], NV[name] = [_rows_to_shard(name, t[o:o + r], W[name]) for t in big]
    dwa, nma, nva = adamw(w_ada[0], g_wada, m_w_ada[0], v_w_ada[0], "adamw_w_ada")
    DW["w_ada"], NM["w_ada"], NV["w_ada"] = dwa[None], nma[None], nva[None]
    small = [n for n in names if n not in DW]
    shapes = [W[n].shape for n in small]
    outs = adamw(_pack_rows([W[n] for n in small]), _pack_rows([G[n] for n in small]),
                 _pack_rows([M[n] for n in small]), _pack_rows([V[n] for n in small]), "adamw_small")
    for res, dst in zip(outs, (DW, NM, NV)):
        for n, t in zip(small, _unpack_rows(res, shapes)):
            dst[n] = t
    return (loss.reshape(()), R["dx"], *[G[n] for n in names], *[DW[n] for n in names], *[NM[n] for n in names],
            *[NV[n] for n in names])
```

```python
import math

import jax
import jax.numpy as jnp
from jax import lax
from jax.experimental import pallas as pl
from jax.experimental.pallas import tpu as pltpu
from jax.experimental.pallas import tpu_sc as plsc

F32, BF16, I32 = jnp.float32, jnp.bfloat16, jnp.int32
HI = lax.Precision.HIGHEST
SDS = jax.ShapeDtypeStruct
MESH = pl.DeviceIdType.MESH

D_MODEL = 1024
D_FF = 2816
D_SSD = 1024
SSD_HEADS = 16
SSD_HEAD_DIM = 64
SSD_GROUPS = 2
SSD_STATE = 128
CHUNK = 128
MLA_HEADS = 8
QK_NOPE = 64
QK_ROPE = 32
QK_DIM = 96
V_HEAD = 128
Q_LORA = 384
KV_LORA = 256
ROPE_THETA = 10000.0
N_MOD = 9
EPS = 1e-6
D_CONV = 1536
D_IN = 3248
D_IN_PAD = 3328
HEAD_PAD = 128
N_DEV = 8
ADAM_LR, ADAM_B1, ADAM_B2, ADAM_EPS, ADAM_WD, ADAM_STEP = 0.001, 0.9, 0.999, 1e-08, 0.01, 10

SAVED_ACT = BF16
VMEM_LIMIT = 56 * 1024 * 1024
LANES = 128
NT_DIMS = (((1,), (1,)), ((), ()))
TN_DIMS = (((0,), (0,)), ((), ()))


def _cparams(n_axes):
    return pltpu.CompilerParams(dimension_semantics=("arbitrary",) * n_axes, vmem_limit_bytes=VMEM_LIMIT)


def _row(tm, d):
    return pl.BlockSpec((None, tm, d), lambda b, i: (b, i, 0))


def _bvec(d):
    return pl.BlockSpec((None, 1, d), lambda b, i: (b, 0, 0))


def _full(shape):
    n = len(shape)
    return pl.BlockSpec(shape, lambda *_: (0,) * n)


def _sigmoid(x):
    return 1.0 / (1.0 + jnp.exp(-x))


def _softplus(x):
    return jnp.maximum(x, 0.0) + jnp.log(1.0 + jnp.exp(-jnp.abs(x)))


def _rms(x):
    return lax.rsqrt(jnp.mean(x * x, axis=-1, keepdims=True) + EPS)


def _rms_bwd(dn, n, r):
    return r * (dn - n * jnp.mean(dn * n, axis=-1, keepdims=True))


def _first_step():
    return (pl.program_id(0) == 0) & (pl.program_id(1) == 0)


def all_gather8(x, name):
    r, c = x.shape

    def body(x_ref, out_ref, send_sems, recv_sems, local_sem):
        mx, my, mc = lax.axis_index("x"), lax.axis_index("y"), lax.axis_index("c")
        me, sibling = (mx, my, mc), (mx, my, 1 - mc)
        chips = [(1 - mx, my), (mx, 1 - my), (1 - mx, 1 - my)]

        def rows(px, py, pc):
            return out_ref.at[4 * px + 2 * py + pc]

        def copy(k, block, to, src=None):
            return pltpu.make_async_remote_copy(
                src_ref=rows(*block) if src is None else src, dst_ref=rows(*block),
                send_sem=send_sems.at[k], recv_sem=recv_sems.at[k], device_id=to, device_id_type=MESH)

        mine = pltpu.make_async_copy(x_ref, rows(*me), local_sem)
        mine.start()
        first = [copy(0, me, sibling, src=x_ref)]
        first += [copy(1 + j, me, (*chip, mc), src=x_ref) for j, chip in enumerate(chips)]
        for cp in first:
            cp.start()
        passed = [copy(4 + j, (*chip, mc), sibling) for j, chip in enumerate(chips)]
        for j, chip in enumerate(chips):
            copy(1 + j, (*chip, mc), me).wait_recv()
            passed[j].start()
        copy(0, sibling, me).wait_recv()
        for j, chip in enumerate(chips):
            copy(4 + j, (*chip, 1 - mc), me).wait_recv()
        for cp in first + passed:
            cp.wait_send()
        mine.wait()

    return pl.pallas_call(
        body, name=name,
        out_shape=SDS((N_DEV, r, c), x.dtype),
        in_specs=[pl.BlockSpec(memory_space=pl.ANY)],
        out_specs=pl.BlockSpec(memory_space=pl.ANY),
        scratch_shapes=[pltpu.SemaphoreType.DMA((7,)), pltpu.SemaphoreType.DMA((7,)), pltpu.SemaphoreType.DMA],
    )(x)


def all_to_all8(x, name):
    _, r, c = x.shape

    def body(x_ref, out_ref, send_sems, recv_sems, local_sem):
        mx, my, mc = lax.axis_index("x"), lax.axis_index("y"), lax.axis_index("c")
        me = 4 * mx + 2 * my + mc
        mine = pltpu.make_async_copy(x_ref.at[me], out_ref.at[me], local_sem)
        mine.start()
        copies = []
        for rel in range(1, N_DEV):
            px = 1 - mx if rel & 4 else mx
            py = 1 - my if rel & 2 else my
            pc = 1 - mc if rel & 1 else mc
            cp = pltpu.make_async_remote_copy(
                src_ref=x_ref.at[4 * px + 2 * py + pc], dst_ref=out_ref.at[me],
                send_sem=send_sems.at[rel - 1], recv_sem=recv_sems.at[rel - 1],
                device_id=(px, py, pc), device_id_type=MESH)
            cp.start()
            copies.append(cp)
        for cp in copies:
            cp.wait()
        mine.wait()

    return pl.pallas_call(
        body, name=name,
        out_shape=SDS((N_DEV, r, c), x.dtype),
        in_specs=[pl.BlockSpec(memory_space=pl.ANY)],
        out_specs=pl.BlockSpec(memory_space=pl.ANY),
        scratch_shapes=[pltpu.SemaphoreType.DMA((7,)), pltpu.SemaphoreType.DMA((7,)), pltpu.SemaphoreType.DMA],
    )(x)


def _sequencer_kernel(name, collective_id):
    return pl.kernel(
        mesh=plsc.ScalarSubcoreMesh(axis_name="seq", num_cores=1), name=name,
        scratch_types=(pltpu.SemaphoreType.DMA((7,)), pltpu.SemaphoreType.DMA((7,)), pltpu.SemaphoreType.DMA),
        compiler_params=pltpu.CompilerParams(collective_id=collective_id))


def _handshake(peers):
    barrier = pltpu.get_barrier_semaphore()
    for peer in peers:
        pl.semaphore_signal(barrier, inc=1, device_id=peer, device_id_type=MESH)
    pl.semaphore_wait(barrier, len(peers))


def sc_all_gather8(x, name, collective_id):
    r, c = x.shape
    x_ref = jax.new_ref(x, memory_space=pltpu.MemorySpace.HBM)
    out_ref = jax.empty_ref(SDS((N_DEV, r, c), x.dtype), memory_space=pltpu.MemorySpace.HBM)

    @_sequencer_kernel(name, collective_id)
    def launch(send_sems, recv_sems, local_sem):
        mx, my, mc = lax.axis_index("x"), lax.axis_index("y"), lax.axis_index("c")
        me, sibling = (mx, my, mc), (mx, my, 1 - mc)
        chips = [(1 - mx, my), (mx, 1 - my), (1 - mx, 1 - my)]
        _handshake([sibling] + [(*chip, mc) for chip in chips])

        def rows(px, py, pc):
            return out_ref.at[4 * px + 2 * py + pc]

        def copy(k, block, to, src=None):
            return pltpu.make_async_remote_copy(
                src_ref=rows(*block) if src is None else src, dst_ref=rows(*block),
                send_sem=send_sems.at[k], recv_sem=recv_sems.at[k], device_id=to, device_id_type=MESH)

        mine = pltpu.make_async_copy(x_ref, rows(*me), local_sem)
        mine.start()
        first = [copy(0, me, sibling, src=x_ref)]
        first += [copy(1 + j, me, (*chip, mc), src=x_ref) for j, chip in enumerate(chips)]
        for cp in first:
            cp.start()
        passed = [copy(4 + j, (*chip, mc), sibling) for j, chip in enumerate(chips)]
        for j, chip in enumerate(chips):
            copy(1 + j, (*chip, mc), me).wait_recv()
            passed[j].start()
        copy(0, sibling, me).wait_recv()
        for j, chip in enumerate(chips):
            copy(4 + j, (*chip, 1 - mc), me).wait_recv()
        for cp in first + passed:
            cp.wait_send()
        mine.wait()

    launch()
    return out_ref[...]


def sc_all_to_all8(x, name, collective_id):
    x_ref = jax.new_ref(x, memory_space=pltpu.MemorySpace.HBM)
    out_ref = jax.empty_ref(SDS(x.shape, x.dtype), memory_space=pltpu.MemorySpace.HBM)

    @_sequencer_kernel(name, collective_id)
    def launch(send_sems, recv_sems, local_sem):
        mx, my, mc = lax.axis_index("x"), lax.axis_index("y"), lax.axis_index("c")
        me = 4 * mx + 2 * my + mc
        peers = [(1 - mx if rel & 4 else mx, 1 - my if rel & 2 else my, 1 - mc if rel & 1 else mc)
                 for rel in range(1, N_DEV)]
        _handshake(peers)
        mine = pltpu.make_async_copy(x_ref.at[me], out_ref.at[me], local_sem)
        mine.start()
        copies = []
        for k, (px, py, pc) in enumerate(peers):
            cp = pltpu.make_async_remote_copy(
                src_ref=x_ref.at[4 * px + 2 * py + pc], dst_ref=out_ref.at[me],
                send_sem=send_sems.at[k], recv_sem=recv_sems.at[k], device_id=(px, py, pc), device_id_type=MESH)
            cp.start()
            copies.append(cp)
        for cp in copies:
            cp.wait()
        mine.wait()

    launch()
    return out_ref[...]


def norm_mod(x, w, sc, sh, name):
    b, s, d = x.shape
    tm = min(512, s)

    def body(x_ref, w_ref, sc_ref, sh_ref, h_ref):
        xv = x_ref[...]
        n = xv * _rms(xv)
        h_ref[...] = ((n * w_ref[...]) * (1.0 + sc_ref[...]) + sh_ref[...]).astype(BF16)

    return pl.pallas_call(
        body, name=name, grid=(b, s // tm),
        in_specs=[_row(tm, d), _full((1, d)), _bvec(d), _bvec(d)],
        out_specs=_row(tm, d), out_shape=SDS((b, s, d), BF16), compiler_params=_cparams(2))(x, w, sc, sh)


def ffn_up(h, wg_t, wu_t, name):
    b, s, d = h.shape
    f = wg_t.shape[0]
    tm, tn = min(512, s), f // 2

    def body(h_ref, wg_ref, wu_ref, g_ref, u_ref, a_ref):
        hv = h_ref[...]
        g = lax.dot_general(hv, wg_ref[...], NT_DIMS, preferred_element_type=F32)
        u = lax.dot_general(hv, wu_ref[...], NT_DIMS, preferred_element_type=F32)
        g_ref[...] = g.astype(g_ref.dtype)
        u_ref[...] = u.astype(u_ref.dtype)
        a_ref[...] = (g * _sigmoid(g) * u).astype(BF16)

    hs = pl.BlockSpec((None, tm, d), lambda j, bb, i: (bb, i, 0))
    ws = pl.BlockSpec((tn, d), lambda j, bb, i: (j, 0))
    os_ = pl.BlockSpec((None, tm, tn), lambda j, bb, i: (bb, i, j))
    return pl.pallas_call(
        body, name=name, grid=(f // tn, b, s // tm),
        in_specs=[hs, ws, ws], out_specs=[os_, os_, os_],
        out_shape=[SDS((b, s, f), SAVED_ACT), SDS((b, s, f), SAVED_ACT), SDS((b, s, f), BF16)],
        compiler_params=_cparams(3))(h, wg_t, wu_t)


def ffn_down(a, wd, x, gate, scale, name):
    b, s, f = a.shape
    d = wd.shape[1]
    tm = min(512, s)

    def body(a_ref, wd_ref, x_ref, g_ref, xn_ref, o_ref):
        o = jnp.dot(a_ref[...], wd_ref[...], preferred_element_type=F32)
        xn_ref[...] = x_ref[...] + (scale * g_ref[...]) * o
        o_ref[...] = o.astype(BF16)

    return pl.pallas_call(
        body, name=name, grid=(b, s // tm),
        in_specs=[_row(tm, f), _full((f, d)), _row(tm, d), _bvec(d)],
        out_specs=[_row(tm, d), _row(tm, d)],
        out_shape=[SDS((b, s, d), F32), SDS((b, s, d), BF16)], compiler_params=_cparams(2))(a, wd, x, gate)


def ffn_dact(do, wd, g, u, name):
    b, s, d = do.shape
    f = wd.shape[0]
    tm, tn = min(512, s), f // 2

    def body(do_ref, wd_ref, g_ref, u_ref, dg_ref, du_ref):
        da = lax.dot_general(do_ref[...], wd_ref[...], NT_DIMS, preferred_element_type=F32)
        gv = g_ref[...].astype(F32)
        sg = _sigmoid(gv)
        dg_ref[...] = (da * u_ref[...].astype(F32) * (sg * (1.0 + gv * (1.0 - sg)))).astype(BF16)
        du_ref[...] = (da * (gv * sg)).astype(BF16)

    dos = pl.BlockSpec((None, tm, d), lambda j, bb, i: (bb, i, 0))
    ws = pl.BlockSpec((tn, d), lambda j, bb, i: (j, 0))
    es = pl.BlockSpec((None, tm, tn), lambda j, bb, i: (bb, i, j))
    return pl.pallas_call(
        body, name=name, grid=(f // tn, b, s // tm),
        in_specs=[dos, ws, es, es], out_specs=[es, es],
        out_shape=[SDS((b, s, f), BF16), SDS((b, s, f), BF16)], compiler_params=_cparams(3))(do, wd, g, u)


def mm_tn(a, bm, tma, tnb, name):
    b, s, ka = a.shape
    nb = bm.shape[2]
    tk = min(2048, s)

    def body(a_ref, b_ref, o_ref):
        @pl.when((pl.program_id(2) == 0) & (pl.program_id(3) == 0))
        def _():
            o_ref[...] = jnp.zeros_like(o_ref)
        o_ref[...] += lax.dot_general(a_ref[...], b_ref[...], TN_DIMS, preferred_element_type=F32)

    return pl.pallas_call(
        body, name=name, grid=(ka // tma, nb // tnb, b, s // tk),
        in_specs=[pl.BlockSpec((None, tk, tma), lambda i, j, bb, k: (bb, k, i)),
                  pl.BlockSpec((None, tk, tnb), lambda i, j, bb, k: (bb, k, j))],
        out_specs=pl.BlockSpec((tma, tnb), lambda i, j, bb, k: (i, j)),
        out_shape=SDS((ka, nb), F32), compiler_params=_cparams(4))(a, bm)


def _gate_bwd_specs(tm, d, b, s):
    return ([_row(tm, d), _bvec(d)], [_row(tm, d), _bvec(d)], [SDS((b, s, d), BF16), SDS((b, 1, d), F32)])


def _gate_bwd_tile(dx, scale, o_ref, g_ref, do_ref, dg_ref):
    do_ref[...] = ((scale * g_ref[...]) * dx).astype(BF16)
    dg_ref[...] += jnp.sum(scale * dx * o_ref[...].astype(F32), axis=0, keepdims=True)


def dh_norm_bwd(dys, wts, x, dxn, w, sc, name, below=None):
    b, s, d = x.shape
    tm = min(256, s)
    n_in = len(dys)
    extra_in, extra_out, extra_shape = _gate_bwd_specs(tm, d, b, s) if below else ([], [], [])

    def body(*refs):
        dy_refs, w_refs = refs[:n_in], refs[n_in:2 * n_in]
        x_ref, dxn_ref, nw_ref, sc_ref = refs[2 * n_in:2 * n_in + 4]
        rest = refs[2 * n_in + 4:]
        if below:
            o_ref, g_ref, dx_ref, dsc_ref, dsh_ref, dw_ref, do_ref, dg_ref = rest
        else:
            dx_ref, dsc_ref, dsh_ref, dw_ref = rest

        @pl.when(pl.program_id(1) == 0)
        def _():
            dsc_ref[...] = jnp.zeros_like(dsc_ref)
            dsh_ref[...] = jnp.zeros_like(dsh_ref)
            if below:
                dg_ref[...] = jnp.zeros_like(dg_ref)

        @pl.when(_first_step())
        def _():
            dw_ref[...] = jnp.zeros_like(dw_ref)

        dh = jnp.dot(dy_refs[0][...], w_refs[0][...], preferred_element_type=F32)
        for k in range(1, n_in):
            dh += jnp.dot(dy_refs[k][...], w_refs[k][...], preferred_element_type=F32)
        xv = x_ref[...]
        r = _rms(xv)
        n = xv * r
        nw = nw_ref[...]
        dsc_ref[...] += jnp.sum(dh * (n * nw), axis=0, keepdims=True)
        dsh_ref[...] += jnp.sum(dh, axis=0, keepdims=True)
        dhn = dh * (1.0 + sc_ref[...])
        dw_ref[...] += jnp.sum(dhn * n, axis=0, keepdims=True)
        dx = dxn_ref[...] + _rms_bwd(dhn * nw, n, r)
        dx_ref[...] = dx
        if below:
            _gate_bwd_tile(dx, below[2], o_ref, g_ref, do_ref, dg_ref)

    in_specs = [_row(tm, dy.shape[2]) for dy in dys] + [_full(wt.shape) for wt in wts]
    in_specs += [_row(tm, d), _row(tm, d), _full((1, d)), _bvec(d)] + extra_in
    return pl.pallas_call(
        body, name=name, grid=(b, s // tm), in_specs=in_specs,
        out_specs=[_row(tm, d), _bvec(d), _bvec(d), _full((1, d))] + extra_out,
        out_shape=[SDS((b, s, d), F32), SDS((b, 1, d), F32), SDS((b, 1, d), F32), SDS((1, d), F32)] + extra_shape,
        compiler_params=_cparams(2))(*dys, *wts, x, dxn, w, sc, *(below[:2] if below else ()))


def final_loss(x, w, tgt, below, name):
    b, s, d = x.shape
    tm = min(512, s)
    extra_in, extra_out, extra_shape = _gate_bwd_specs(tm, d, b, s)

    def body(x_ref, w_ref, t_ref, o_ref, g_ref, loss_ref, dx_ref, dw_ref, do_ref, dg_ref):
        @pl.when(_first_step())
        def _():
            loss_ref[...] = jnp.zeros_like(loss_ref)
            dw_ref[...] = jnp.zeros_like(dw_ref)

        @pl.when(pl.program_id(1) == 0)
        def _():
            dg_ref[...] = jnp.zeros_like(dg_ref)
        xv = x_ref[...]
        r = _rms(xv)
        n = xv * r
        wv = w_ref[...]
        e = n * wv - t_ref[...]
        loss_ref[...] += jnp.sum(e * e) * (0.5 / d)
        dy = e * (1.0 / d)
        dw_ref[...] += jnp.sum(dy * n, axis=0, keepdims=True)
        dx = _rms_bwd(dy * wv, n, r)
        dx_ref[...] = dx
        _gate_bwd_tile(dx, below[2], o_ref, g_ref, do_ref, dg_ref)

    return pl.pallas_call(
        body, name=name, grid=(b, s // tm),
        in_specs=[_row(tm, d), _full((1, d)), _row(tm, d)] + extra_in,
        out_specs=[_full((1, LANES)), _row(tm, d), _full((1, d))] + extra_out,
        out_shape=[SDS((1, LANES), F32), SDS((b, s, d), F32), SDS((1, d), F32)] + extra_shape,
        compiler_params=_cparams(2))(x, w, tgt, *below[:2])


def in_proj(h, win_t, name):
    b, s, d = h.shape
    tm = min(256, s)
    widths = (D_SSD, D_SSD + 2 * SSD_GROUPS * SSD_STATE, Q_LORA, KV_LORA, LANES)

    def body(h_ref, w_ref, *outs):
        p = lax.dot_general(h_ref[...], w_ref[...], NT_DIMS, preferred_element_type=F32)
        off = 0
        for o_ref, wd in zip(outs, widths):
            o_ref[...] = p[:, off:off + wd]
            off += wd

    return pl.pallas_call(
        body, name=name, grid=(b, s // tm),
        in_specs=[_row(tm, d), _full(win_t.shape)],
        out_specs=[_row(tm, wd) for wd in widths],
        out_shape=[SDS((b, s, wd), F32) for wd in widths], compiler_params=_cparams(2))(h, win_t)


def _halo_prev(ts, d):
    return pl.BlockSpec((None, 8, d), lambda b, i: (b, jnp.maximum(i * (ts // 8) - 1, 0), 0))


def _conv_taps(ext_ref, w_ref, ts):
    return [ext_ref[5 + k:5 + k + ts, :] for k in range(4)], [w_ref[k:k + 1, :] for k in range(4)]


def conv_fwd(u, cw, cb, name):
    b, s, dc = u.shape
    ts = min(512, s)
    widths = (D_SSD, SSD_GROUPS * SSD_STATE, SSD_GROUPS * SSD_STATE)

    def body(u_ref, up_ref, w_ref, b_ref, xs_ref, bm_ref, cm_ref, ext):
        ext[0:8, :] = jnp.where(pl.program_id(1) > 0, up_ref[...], 0.0)
        ext[8:8 + ts, :] = u_ref[...]
        taps, ws = _conv_taps(ext, w_ref, ts)
        v = b_ref[...] + taps[0] * ws[0] + taps[1] * ws[1] + taps[2] * ws[2] + taps[3] * ws[3]
        y = v * _sigmoid(v)
        xs_ref[...] = y[:, 0:D_SSD]
        bm_ref[...] = y[:, D_SSD:D_SSD + 256]
        cm_ref[...] = y[:, D_SSD + 256:D_SSD + 512]

    return pl.pallas_call(
        body, name=name, grid=(b, s // ts),
        in_specs=[_row(ts, dc), _halo_prev(ts, dc), _full((4, dc)), _full((1, dc))],
        out_specs=[_row(ts, wd) for wd in widths],
        out_shape=[SDS((b, s, wd), F32) for wd in widths],
        scratch_shapes=[pltpu.VMEM((ts + 8, dc), F32)], compiler_params=_cparams(2))(u, u, cw, cb)


def conv_bwd_a(dxs, dbm, dcm, u, cw, cb, name):
    b, s, dc = u.shape
    ts = min(512, s)

    def body(dxs_ref, dbm_ref, dcm_ref, u_ref, up_ref, w_ref, b_ref, dv_ref, dwb_ref, ext):
        @pl.when(_first_step())
        def _():
            dwb_ref[...] = jnp.zeros_like(dwb_ref)
        ext[0:8, :] = jnp.where(pl.program_id(1) > 0, up_ref[...], 0.0)
        ext[8:8 + ts, :] = u_ref[...]
        taps, ws = _conv_taps(ext, w_ref, ts)
        v = b_ref[...] + taps[0] * ws[0] + taps[1] * ws[1] + taps[2] * ws[2] + taps[3] * ws[3]
        sg = _sigmoid(v)
        dy = jnp.concatenate([dxs_ref[...], dbm_ref[...], dcm_ref[...]], axis=1)
        dv = dy * (sg * (1.0 + v * (1.0 - sg)))
        dv_ref[...] = dv
        for k in range(4):
            dwb_ref[k:k + 1, :] += jnp.sum(dv * taps[k], axis=0, keepdims=True)
        dwb_ref[4:5, :] += jnp.sum(dv, axis=0, keepdims=True)

    return pl.pallas_call(
        body, name=name, grid=(b, s // ts),
        in_specs=[_row(ts, D_SSD), _row(ts, 256), _row(ts, 256), _row(ts, dc), _halo_prev(ts, dc),
                  _full((4, dc)), _full((1, dc))],
        out_specs=[_row(ts, dc), _full((8, dc))],
        out_shape=[SDS((b, s, dc), F32), SDS((8, dc), F32)],
        scratch_shapes=[pltpu.VMEM((ts + 8, dc), F32)], compiler_params=_cparams(2))(dxs, dbm, dcm, u, u, cw, cb)


def conv_bwd_b(dv, cw, name):
    b, s, dc = dv.shape
    ts = min(512, s)
    nt = s // ts

    def body(dv_ref, dn_ref, w_ref, du_ref, ext):
        ext[0:ts, :] = dv_ref[...]
        ext[ts:ts + 8, :] = jnp.where(pl.program_id(1) < nt - 1, dn_ref[...], 0.0)
        acc = ext[3:3 + ts, :] * w_ref[0:1, :]
        for k in range(1, 4):
            acc += ext[3 - k:3 - k + ts, :] * w_ref[k:k + 1, :]
        du_ref[...] = acc.astype(BF16)

    nxt = pl.BlockSpec((None, 8, dc), lambda bb, i: (bb, jnp.minimum((i + 1) * (ts // 8), s // 8 - 1), 0))
    return pl.pallas_call(
        body, name=name, grid=(b, nt),
        in_specs=[_row(ts, dc), nxt, _full((4, dc))],
        out_specs=_row(ts, dc), out_shape=SDS((b, s, dc), BF16),
        scratch_shapes=[pltpu.VMEM((ts + 8, dc), F32)], compiler_params=_cparams(2))(dv, dv, cw)


def _ssd_common(misc_ref, dtb_ref, alog_ref, e_ref):
    ln = CHUNK
    lane = lax.broadcasted_iota(I32, (ln, LANES), 1)
    lane1 = lax.broadcasted_iota(I32, (1, LANES), 1)
    pre = misc_ref[...] + dtb_ref[...]
    dt_s = jnp.where(lane < SSD_HEADS, _softplus(pre), 0.0)
    a_neg = jnp.where(lane1 < SSD_HEADS, -jnp.exp(alog_ref[...]), 0.0)
    ri = lax.broadcasted_iota(I32, (ln, ln), 0)
    ci = lax.broadcasted_iota(I32, (ln, ln), 1)
    tril = ci <= ri
    acum = jnp.dot(tril.astype(F32), dt_s * a_neg, preferred_element_type=F32, precision=HI)
    both_e = _dot_01(jnp.concatenate([dt_s, acum], axis=0), e_ref[...], 3)
    dt_e, acum_e = both_e[0:ln], both_e[ln:2 * ln]
    return dict(pre=pre, dt_s=dt_s, a_neg=a_neg, tril=tril, ri=ri, ci=ci, acum=acum, acum_t=acum.T,
                dt_e=dt_e, eac_e=jnp.exp(acum_e), del_e=jnp.exp(acum_e[ln - 1:ln, :] - acum_e))


def _dot_01(x, m01, terms):
    acc, rest = None, x
    for k in range(terms):
        part = rest.astype(BF16)
        if k + 1 < terms:
            rest = rest - part.astype(F32)
        d = jnp.dot(part, m01, preferred_element_type=F32)
        acc = d if acc is None else acc + d
    return acc


def _decay(cm, h):
    seg = cm["acum"][:, h:h + 1] - cm["acum_t"][h:h + 1, :]
    return jnp.exp(jnp.where(cm["tril"], seg, -jnp.inf))


def ssd_fwd(xs, bm, cm_, misc, z, dtb, alog, dskip_e, norm_w, e_mat, name):
    b, s, _ = xs.shape
    ln, nc = CHUNK, s // CHUNK
    gw = D_SSD // SSD_GROUPS
    hpg = SSD_HEADS // SSD_GROUPS

    def body(xs_ref, b_ref, c_ref, misc_ref, z_ref, dtb_ref, alog_ref, dsk_ref, nw_ref, e_ref,
             ys_ref, y_ref, p_ref, st, yd):
        @pl.when(pl.program_id(1) == 0)
        def _():
            st[...] = jnp.zeros_like(st)
        cm = _ssd_common(misc_ref, dtb_ref, alog_ref, e_ref)
        xsv = xs_ref[...]
        xdt = xsv * cm["dt_e"]
        xdt_b = xdt.astype(BF16)
        xd_b = (xdt * cm["del_e"]).astype(BF16)
        gam_e = cm["eac_e"][ln - 1:ln, :]
        p_ref[...] = st[...]
        yoff = []
        for g in range(SSD_GROUPS):
            gs = slice(gw * g, gw * (g + 1))
            bg = b_ref[:, SSD_STATE * g:SSD_STATE * (g + 1)].astype(BF16)
            cg = c_ref[:, SSD_STATE * g:SSD_STATE * (g + 1)].astype(BF16)
            cb = lax.dot_general(cg, bg, NT_DIMS, preferred_element_type=F32)
            st_g = st[:, gs]
            yoff.append(jnp.dot(cg, st_g.astype(BF16), preferred_element_type=F32) * cm["eac_e"][:, gs])
            for j in range(hpg):
                h = hpg * g + j
                hs = slice(SSD_HEAD_DIM * h, SSD_HEAD_DIM * (h + 1))
                m = (cb * _decay(cm, h)).astype(BF16)
                yd[:, hs] = jnp.dot(m, xdt_b[:, hs], preferred_element_type=F32)
            new = lax.dot_general(bg, xd_b[:, gs], TN_DIMS, preferred_element_type=F32)
            st[:, gs] = st_g * gam_e[:, gs] + new
        y = yd[...] + jnp.concatenate(yoff, axis=1) + dsk_ref[...] * xsv
        y_ref[...] = y
        zz = z_ref[...]
        yg = y * (zz * _sigmoid(zz))
        outs = []
        for g in range(SSD_GROUPS):
            ygg = yg[:, gw * g:gw * (g + 1)]
            outs.append(ygg * _rms(ygg) * nw_ref[:, gw * g:gw * (g + 1)])
        ys_ref[...] = jnp.concatenate(outs, axis=1).astype(BF16)

    row = lambda d: pl.BlockSpec((None, ln, d), lambda bb, c: (bb, c, 0))
    return pl.pallas_call(
        body, name=name, grid=(b, nc),
        in_specs=[row(D_SSD), row(256), row(256), row(LANES), row(D_SSD), _full((1, LANES)), _full((1, LANES)),
                  _full((1, D_SSD)), _full((1, D_SSD)), _full((LANES, D_SSD))],
        out_specs=[row(D_SSD), row(D_SSD), pl.BlockSpec((None, None, SSD_STATE, D_SSD), lambda bb, c: (bb, c, 0, 0))],
        out_shape=[SDS((b, s, D_SSD), BF16), SDS((b, s, D_SSD), F32), SDS((b, nc, SSD_STATE, D_SSD), F32)],
        scratch_shapes=[pltpu.VMEM((SSD_STATE, D_SSD), F32), pltpu.VMEM((ln, D_SSD), F32)],
        compiler_params=_cparams(2))(xs, bm, cm_, misc, z, dtb, alog, dskip_e, norm_w, e_mat)


def ssd_bwd(dys, y, z, xs, bm, cm_, misc, prev, dtb, alog, dskip_e, norm_w, e_mat, et_mat, name):
    b, s, _ = xs.shape
    ln, nc = CHUNK, s // CHUNK
    gw = D_SSD // SSD_GROUPS
    hpg = SSD_HEADS // SSD_GROUPS

    def body(dys_ref, y_ref, z_ref, xs_ref, b_ref, c_ref, misc_ref, p_ref, dtb_ref, alog_ref, dsk_ref, nw_ref,
             e_ref, et_ref, dxs_ref, db_ref, dc_ref, dz_ref, ddt_ref, dnw_ref, ddsk_ref, ddtb_ref, dalog_ref,
             dst, dxd, dac_t):
        @pl.when(_first_step())
        def _():
            for r_ in (dnw_ref, ddsk_ref, ddtb_ref, dalog_ref):
                r_[...] = jnp.zeros_like(r_)

        @pl.when(pl.program_id(1) == 0)
        def _():
            dst[...] = jnp.zeros_like(dst)

        cm = _ssd_common(misc_ref, dtb_ref, alog_ref, e_ref)
        et = et_ref[...]
        squeeze = lambda t: _dot_01(t, et, 2)
        lane = lax.broadcasted_iota(I32, (ln, LANES), 1)
        sub = lax.broadcasted_iota(I32, (LANES, ln), 0)
        xsv = xs_ref[...]
        xdt = xsv * cm["dt_e"]
        xdt_b = xdt.astype(BF16)
        xd_b = (xdt * cm["del_e"]).astype(BF16)
        eac_e = cm["eac_e"]
        gam_e = eac_e[ln - 1:ln, :]

        yv, zz, dyo = y_ref[...], z_ref[...], dys_ref[...]
        sz = _sigmoid(zz)
        silu_z = zz * sz
        yg = yv * silu_z
        dyg, dnw = [], []
        for g in range(SSD_GROUPS):
            gs = slice(gw * g, gw * (g + 1))
            ygg = yg[:, gs]
            r = _rms(ygg)
            n = ygg * r
            dnw.append(jnp.sum(dyo[:, gs] * n, axis=0, keepdims=True))
            dyg.append(_rms_bwd(dyo[:, gs] * nw_ref[:, gs], n, r))
        dyg = jnp.concatenate(dyg, axis=1)
        dnw_ref[...] += jnp.concatenate(dnw, axis=1)
        dz_ref[...] = (dyg * yv * (sz * (1.0 + zz * (1.0 - sz)))).astype(BF16)
        dy = dyg * silu_z
        ddsk_ref[...] += jnp.sum(dy * xsv, axis=0, keepdims=True)
        dy_b = dy.astype(BF16)

        dacum = jnp.zeros((ln, LANES), F32)
        dac_t[...] = jnp.zeros_like(dac_t)
        w1, dgam = [], []
        for g in range(SSD_GROUPS):
            gs = slice(gw * g, gw * (g + 1))
            ss = slice(SSD_STATE * g, SSD_STATE * (g + 1))
            bg = b_ref[:, ss].astype(BF16)
            cg = c_ref[:, ss].astype(BF16)
            cb = lax.dot_general(cg, bg, NT_DIMS, preferred_element_type=F32)
            pt = p_ref[:, gs]
            pt_b = pt.astype(BF16)
            dst_g = dst[:, gs]
            dst_b = dst_g.astype(BF16)
            edy = (dy[:, gs] * eac_e[:, gs]).astype(BF16)
            dcg = lax.dot_general(edy, pt_b, NT_DIMS, preferred_element_type=F32)
            dpt = lax.dot_general(cg, edy, TN_DIMS, preferred_element_type=F32)
            yoff = jnp.dot(cg, pt_b, preferred_element_type=F32) * eac_e[:, gs]
            dxd_g = jnp.dot(bg, dst_b, preferred_element_type=F32)
            dbg = lax.dot_general(xd_b[:, gs], dst_b, NT_DIMS, preferred_element_type=F32)
            ddel = dxd_g * xdt[:, gs] * cm["del_e"][:, gs]
            w1.append(dy[:, gs] * yoff - ddel)
            dgam.append(jnp.sum(ddel, axis=0, keepdims=True) + jnp.sum(dst_g * pt, axis=0, keepdims=True) * gam_e[:, gs])
            dxd[:, gs] = dxd_g * cm["del_e"][:, gs]
            dst[:, gs] = dst_g * gam_e[:, gs] + dpt
            dcb = jnp.zeros((ln, ln), F32)
            for j in range(hpg):
                h = hpg * g + j
                hs = slice(SSD_HEAD_DIM * h, SSD_HEAD_DIM * (h + 1))
                lam = _decay(cm, h)
                m = cb * lam
                dm = lax.dot_general(dy_b[:, hs], xdt_b[:, hs], NT_DIMS, preferred_element_type=F32)
                dxd[:, hs] += lax.dot_general(m.astype(BF16), dy_b[:, hs], TN_DIMS, preferred_element_type=F32)
                dcb += dm * lam
                wl = dm * m
                dacum += jnp.where(lane == h, jnp.sum(wl, axis=1, keepdims=True), 0.0)
                dac_t[...] -= jnp.where(sub == h, jnp.sum(wl, axis=0, keepdims=True), 0.0)
            dcb_b = dcb.astype(BF16)
            dc_ref[:, ss] = dcg + jnp.dot(dcb_b, bg, preferred_element_type=F32)
            db_ref[:, ss] = dbg + lax.dot_general(dcb_b, cg, TN_DIMS, preferred_element_type=F32)

        dxdt = dxd[...]
        dxs_ref[...] = dy * dsk_ref[...] + dxdt * cm["dt_e"]
        dacum += squeeze(jnp.concatenate(w1, axis=1)) + dac_t[...].T
        dlast = squeeze(jnp.broadcast_to(jnp.concatenate(dgam, axis=1), (8, D_SSD)))[0:1, :]
        dacum += jnp.where(lax.broadcasted_iota(I32, (ln, LANES), 0) == ln - 1, dlast, 0.0)
        triu = (cm["ci"] >= cm["ri"]).astype(F32)
        da = jnp.dot(triu, dacum, preferred_element_type=F32, precision=HI)
        ddt = da * cm["a_neg"] + squeeze(dxdt * xsv)
        dalog_ref[...] += jnp.sum(da * cm["dt_s"], axis=0, keepdims=True) * cm["a_neg"]
        ddt_raw = jnp.where(lane < SSD_HEADS, ddt * _sigmoid(cm["pre"]), 0.0)
        ddt_ref[...] = ddt_raw
        ddtb_ref[...] += jnp.sum(ddt_raw, axis=0, keepdims=True)

    row = lambda d: pl.BlockSpec((None, ln, d), lambda bb, c: (bb, nc - 1 - c, 0))
    return pl.pallas_call(
        body, name=name, grid=(b, nc),
        in_specs=[row(D_SSD), row(D_SSD), row(D_SSD), row(D_SSD), row(256), row(256), row(LANES),
                  pl.BlockSpec((None, None, SSD_STATE, D_SSD), lambda bb, c: (bb, nc - 1 - c, 0, 0)),
                  _full((1, LANES)), _full((1, LANES)), _full((1, D_SSD)), _full((1, D_SSD)),
                  _full((LANES, D_SSD)), _full((D_SSD, LANES))],
        out_specs=[row(D_SSD), row(256), row(256), row(D_SSD), row(LANES),
                   _full((1, D_SSD)), _full((1, D_SSD)), _full((1, LANES)), _full((1, LANES))],
        out_shape=[SDS((b, s, D_SSD), F32), SDS((b, s, 256), F32), SDS((b, s, 256), F32), SDS((b, s, D_SSD), BF16),
                   SDS((b, s, LANES), F32), SDS((1, D_SSD), F32), SDS((1, D_SSD), F32), SDS((1, LANES), F32),
                   SDS((1, LANES), F32)],
        scratch_shapes=[pltpu.VMEM((SSD_STATE, D_SSD), F32), pltpu.VMEM((ln, D_SSD), F32), pltpu.VMEM((LANES, ln), F32)],
        compiler_params=_cparams(2))(dys, y, z, xs, bm, cm_, misc, prev, dtb, alog, dskip_e, norm_w, e_mat, et_mat)


def _rope(xv, cc, sp, sm):
    n = xv.shape[1]
    return xv * cc + pltpu.roll(xv, 16, 1) * sp + pltpu.roll(xv, n - 16, 1) * sm


def _rope_bwd(dy, cc, sp, sm):
    n = dy.shape[1]
    return dy * cc + pltpu.roll(dy * sp, n - 16, 1) + pltpu.roll(dy * sm, 16, 1)


def _tile8(t):
    return jnp.concatenate([t] * MLA_HEADS, axis=1)


def qkv_fwd(cq, ckv, misc, cc, sp, sm, qnw, kvnw, wuq_t, wukv_t, place, name):
    b, s, _ = cq.shape
    tm = min(256, s)
    hd = MLA_HEADS * HEAD_PAD

    def body(cq_ref, ckv_ref, misc_ref, cc_ref, sp_ref, sm_ref, qnw_ref, kvnw_ref, wq_ref, wkv_ref, pl_ref,
             q_ref, k_ref, v_ref, qn_ref, kvn_ref):
        cqv, ckvv = cq_ref[...], ckv_ref[...]
        qn = (cqv * _rms(cqv) * qnw_ref[...]).astype(BF16)
        kvn = (ckvv * _rms(ckvv) * kvnw_ref[...]).astype(BF16)
        qn_ref[...] = qn
        kvn_ref[...] = kvn
        cc1, sp1, sm1 = cc_ref[...], sp_ref[...], sm_ref[...]
        q = lax.dot_general(qn, wq_ref[...], NT_DIMS, preferred_element_type=F32)
        q_ref[...] = _rope(q, _tile8(cc1), _tile8(sp1), _tile8(sm1)).astype(BF16)
        kv = lax.dot_general(kvn, wkv_ref[...], NT_DIMS, preferred_element_type=F32)
        kr = jnp.dot(misc_ref[...], pl_ref[...], preferred_element_type=F32, precision=HI)
        kr = _rope(kr, cc1, sp1, sm1)
        k_ref[...] = (kv[:, 0:hd] + _tile8(kr)).astype(BF16)
        v_ref[...] = kv[:, hd:2 * hd].astype(BF16)

    return pl.pallas_call(
        body, name=name, grid=(b, s // tm),
        in_specs=[_row(tm, Q_LORA), _row(tm, KV_LORA), _row(tm, LANES), _row(tm, LANES), _row(tm, LANES), _row(tm, LANES),
                  _full((1, Q_LORA)), _full((1, KV_LORA)), _full(wuq_t.shape), _full(wukv_t.shape), _full((LANES, LANES))],
        out_specs=[_row(tm, hd), _row(tm, hd), _row(tm, hd), _row(tm, Q_LORA), _row(tm, KV_LORA)],
        out_shape=[SDS((b, s, hd), BF16)] * 3 + [SDS((b, s, Q_LORA), BF16), SDS((b, s, KV_LORA), BF16)],
        compiler_params=_cparams(2))(cq, ckv, misc, cc, sp, sm, qnw, kvnw, wuq_t, wukv_t, place)


def qkv_bwd(dq, dk, dv, ddt, cq, ckv, cc, sp, sm, qnw, kvnw, wuq_t, wukv_t, place_t, name):
    b, s, _ = cq.shape
    tm = min(256, s)
    hd = MLA_HEADS * HEAD_PAD

    def body(dq_ref, dk_ref, dv_ref, ddt_ref, cq_ref, ckv_ref, cc_ref, sp_ref, sm_ref, qnw_ref, kvnw_ref,
             wq_ref, wkv_ref, plt_ref, dcq_ref, dckv_ref, dmisc_ref, dqp_ref, dkv_ref, dqnw_ref, dkvnw_ref):
        @pl.when(_first_step())
        def _():
            dqnw_ref[...] = jnp.zeros_like(dqnw_ref)
            dkvnw_ref[...] = jnp.zeros_like(dkvnw_ref)
        cc1, sp1, sm1 = cc_ref[...], sp_ref[...], sm_ref[...]
        dqp = _rope_bwd(dq_ref[...], _tile8(cc1), _tile8(sp1), _tile8(sm1)).astype(BF16)
        dqp_ref[...] = dqp
        dkf = dk_ref[...]
        dkv_b = jnp.concatenate([dkf, dv_ref[...]], axis=1).astype(BF16)
        dkv_ref[...] = dkv_b
        dkr = dkf[:, 0:HEAD_PAD]
        for h in range(1, MLA_HEADS):
            dkr += dkf[:, HEAD_PAD * h:HEAD_PAD * (h + 1)]
        dkr = _rope_bwd(dkr, cc1, sp1, sm1)
        dmisc_ref[...] = (jnp.dot(dkr, plt_ref[...], preferred_element_type=F32, precision=HI) + ddt_ref[...]).astype(BF16)

        def norm_bwd(dn_w, xv, w_ref, dw_ref, dx_ref):
            r = _rms(xv)
            n = xv * r
            dw_ref[...] += jnp.sum(dn_w * n, axis=0, keepdims=True)
            dx_ref[...] = _rms_bwd(dn_w * w_ref[...], n, r).astype(BF16)

        norm_bwd(jnp.dot(dqp, wq_ref[...], preferred_element_type=F32), cq_ref[...], qnw_ref, dqnw_ref, dcq_ref)
        norm_bwd(jnp.dot(dkv_b, wkv_ref[...], preferred_element_type=F32), ckv_ref[...], kvnw_ref, dkvnw_ref, dckv_ref)

    return pl.pallas_call(
        body, name=name, grid=(b, s // tm),
        in_specs=[_row(tm, hd), _row(tm, hd), _row(tm, hd), _row(tm, LANES), _row(tm, Q_LORA), _row(tm, KV_LORA),
                  _row(tm, LANES), _row(tm, LANES), _row(tm, LANES), _full((1, Q_LORA)), _full((1, KV_LORA)),
                  _full(wuq_t.shape), _full(wukv_t.shape), _full((LANES, LANES))],
        out_specs=[_row(tm, Q_LORA), _row(tm, KV_LORA), _row(tm, LANES), _row(tm, hd), _row(tm, 2 * hd),
                   _full((1, Q_LORA)), _full((1, KV_LORA))],
        out_shape=[SDS((b, s, Q_LORA), BF16), SDS((b, s, KV_LORA), BF16), SDS((b, s, LANES), BF16),
                   SDS((b, s, hd), BF16), SDS((b, s, 2 * hd), BF16), SDS((1, Q_LORA), F32), SDS((1, KV_LORA), F32)],
        compiler_params=_cparams(2))(dq, dk, dv, ddt, cq, ckv, cc, sp, sm, qnw, kvnw, wuq_t, wukv_t, place_t)


ATT_SCALE = 1.0 / math.sqrt(QK_DIM)
LOG2E = math.log2(math.e)
ATT_SCALE_LOG2E = ATT_SCALE * LOG2E


ATT_HEADS_PER_STEP = 2


def _att_tile(s):
    return min(512, s)


def flash_fwd(q, k, v, name):
    b, s, hd = q.shape
    t = _att_tile(s)
    nb = s // t
    th = t // 2
    vt = v.reshape(b, nb, t, MLA_HEADS, HEAD_PAD).transpose(0, 3, 1, 4, 2)

    hps = ATT_HEADS_PER_STEP
    hw = hps * HEAD_PAD

    def body(q_ref, k_ref, vt_ref, o_ref, lse_ref, m_s, l_s, acc):
        i = pl.program_id(2)
        m_s[...] = jnp.full_like(m_s, -jnp.inf)
        l_s[...] = jnp.zeros_like(l_s)
        acc[...] = jnp.zeros_like(acc)

        def update(j, diagonal):
            ks = pl.ds(pl.multiple_of(j * t, t), t)
            chains = [(hh, half) for hh in range(hps) for half in range(2)]
            lanes = lambda hh: slice(HEAD_PAD * hh, HEAD_PAD * (hh + 1))
            cols = lambda half: slice(th * half, th * (half + 1))
            sts = {}
            for hh, half in chains:
                st = lax.dot_general(k_ref[ks, lanes(hh)], q_ref[cols(half), lanes(hh)], NT_DIMS,
                                     preferred_element_type=F32)
                if diagonal:
                    row = lax.broadcasted_iota(I32, (t, th), 0)
                    col = lax.broadcasted_iota(I32, (t, th), 1) + th * half
                    st = jnp.where(row <= col, st, -jnp.inf)
                sts[hh, half] = st
            pts, alphas = {}, {}
            for hh, half in chains:
                st, cs = sts[hh, half], cols(half)
                m_prev = m_s[hh, :, cs]
                m_new = jnp.maximum(m_prev, jnp.max(st, axis=0, keepdims=True))
                alpha = jnp.exp2((m_prev - m_new) * ATT_SCALE_LOG2E)
                pt = jnp.exp2((st - m_new) * ATT_SCALE_LOG2E)
                l_s[hh, :, cs] = alpha * l_s[hh, :, cs] + jnp.sum(pt, axis=0, keepdims=True)
                m_s[hh, :, cs] = m_new
                pts[hh, half], alphas[hh, half] = pt.astype(BF16), alpha
            for hh, half in chains:
                cs = cols(half)
                acc[hh, :, cs] = alphas[hh, half] * acc[hh, :, cs] + jnp.dot(vt_ref[hh, j], pts[hh, half],
                                                                             preferred_element_type=F32)

        def step(j, carry):
            update(j, False)
            return carry

        lax.fori_loop(0, i, step, 0)
        update(i, True)
        for hh in range(hps):
            o_ref[:, HEAD_PAD * hh:HEAD_PAD * (hh + 1)] = (acc[hh] / l_s[hh]).T
            lse_ref[hh] = m_s[hh] * ATT_SCALE + jnp.log(l_s[hh])

    qs = pl.BlockSpec((None, t, hw), lambda bb, h, i: (bb, i, h))
    ks = pl.BlockSpec((None, s, hw), lambda bb, h, i: (bb, 0, h))
    vs = pl.BlockSpec((None, hps, nb, HEAD_PAD, t), lambda bb, h, i: (bb, h, 0, 0, 0))
    ls = pl.BlockSpec((None, hps, None, 1, t), lambda bb, h, i: (bb, h, i, 0, 0))
    return pl.pallas_call(
        body, name=name, grid=(b, MLA_HEADS // hps, nb),
        in_specs=[qs, ks, vs], out_specs=[qs, ls],
        out_shape=[SDS((b, s, hd), F32), SDS((b, MLA_HEADS, nb, 1, t), F32)],
        scratch_shapes=[pltpu.VMEM((hps, 1, t), F32), pltpu.VMEM((hps, 1, t), F32), pltpu.VMEM((hps, HEAD_PAD, t), F32)],
        compiler_params=_cparams(3))(q, k, vt)


def flash_bwd(q, k, v, do, lse, dlt, name):
    b, s, hd = q.shape
    t = _att_tile(s)
    nb = s // t
    th = t // 2
    lse_r = lse
    dlt_r = dlt.reshape(b, MLA_HEADS, nb, 1, t)

    hps = ATT_HEADS_PER_STEP
    hw = hps * HEAD_PAD

    def body(q_ref, k_ref, v_ref, do_ref, lse_ref, dlt_ref, dq_ref, dk_ref, dv_ref):
        dq_ref[...] = jnp.zeros_like(dq_ref)
        dk_ref[...] = jnp.zeros_like(dk_ref)
        dv_ref[...] = jnp.zeros_like(dv_ref)

        def tile(j, i, diagonal):
            qs = pl.ds(pl.multiple_of(i * t, t), t)
            chains = [(hh, half) for hh in range(hps) for half in range(2)]
            lanes = lambda hh: slice(HEAD_PAD * hh, HEAD_PAD * (hh + 1))
            keys = lambda half: pl.ds(pl.multiple_of(j * t + th * half, th), th)
            sts, dpts = {}, {}
            for hh, half in chains:
                ls_, ks = lanes(hh), keys(half)
                st = lax.dot_general(k_ref[ks, ls_], q_ref[qs, ls_], NT_DIMS, preferred_element_type=F32)
                if diagonal:
                    row = lax.broadcasted_iota(I32, (th, t), 0) + th * half
                    col = lax.broadcasted_iota(I32, (th, t), 1)
                    st = jnp.where(row <= col, st, -jnp.inf)
                sts[hh, half] = st
                dpts[hh, half] = lax.dot_general(v_ref[ks, ls_], do_ref[qs, ls_], NT_DIMS, preferred_element_type=F32)
            pts, dsts = {}, {}
            for hh, half in chains:
                pt = jnp.exp2(sts[hh, half] * ATT_SCALE_LOG2E - lse_ref[hh, i] * LOG2E)
                pts[hh, half] = pt.astype(BF16)
                dsts[hh, half] = (pt * (dpts[hh, half] - dlt_ref[hh, i])).astype(BF16)
            for hh in range(hps):
                ls_ = lanes(hh)
                dq_acc = None
                for half in range(2):
                    ks = keys(half)
                    dv_ref[ks, ls_] += jnp.dot(pts[hh, half], do_ref[qs, ls_], preferred_element_type=F32)
                    dk_ref[ks, ls_] += jnp.dot(dsts[hh, half], q_ref[qs, ls_], preferred_element_type=F32)
                    part = lax.dot_general(dsts[hh, half], k_ref[ks, ls_], TN_DIMS, preferred_element_type=F32)
                    dq_acc = part if dq_acc is None else dq_acc + part
                dq_ref[qs, ls_] += dq_acc

        def key_tile(j, carry):
            tile(j, j, True)

            def query_tile(i, c2):
                tile(j, i, False)
                return c2

            lax.fori_loop(j + 1, nb, query_tile, 0)
            return carry

        lax.fori_loop(0, nb, key_tile, 0)
        dq_ref[...] *= ATT_SCALE
        dk_ref[...] *= ATT_SCALE

    hs = pl.BlockSpec((None, s, hw), lambda bb, h: (bb, 0, h))
    ls = pl.BlockSpec((None, hps, nb, 1, t), lambda bb, h: (bb, h, 0, 0, 0))
    return pl.pallas_call(
        body, name=name, grid=(b, MLA_HEADS // hps),
        in_specs=[hs, hs, hs, hs, ls, ls], out_specs=[hs, hs, hs],
        out_shape=[SDS((b, s, hd), F32)] * 3, compiler_params=_cparams(2))(q, k, v, do, lse_r, dlt_r)


def out_proj(ys, attn, mnw, wo, x, gate, name):
    b, s, d = x.shape
    tm = min(256, s)

    def body(ys_ref, at_ref, mnw_ref, wo_ref, x_ref, g_ref, xn_ref, o_ref, ym_ref):
        av = at_ref[...]
        ym = (av * _rms(av) * mnw_ref[...]).astype(BF16)
        ym_ref[...] = ym
        o = jnp.dot(ys_ref[...], wo_ref[0:D_SSD, :], preferred_element_type=F32)
        o += jnp.dot(ym, wo_ref[D_SSD:2 * D_SSD, :], preferred_element_type=F32)
        xn_ref[...] = x_ref[...] + g_ref[...] * o
        o_ref[...] = o.astype(BF16)

    return pl.pallas_call(
        body, name=name, grid=(b, s // tm),
        in_specs=[_row(tm, D_SSD), _row(tm, D_SSD), _full((1, D_SSD)), _full(wo.shape), _row(tm, d), _bvec(d)],
        out_specs=[_row(tm, d), _row(tm, d), _row(tm, D_SSD)],
        out_shape=[SDS((b, s, d), F32), SDS((b, s, d), BF16), SDS((b, s, D_SSD), BF16)],
        compiler_params=_cparams(2))(ys, attn, mnw, wo, x, gate)


def out_proj_bwd(dout, attn, mnw, wo, name):
    b, s, d = dout.shape
    tm = min(256, s)

    def body(do_ref, at_ref, mnw_ref, wo_ref, dys_ref, dat_ref, dlt_ref, dw_ref):
        @pl.when(_first_step())
        def _():
            dw_ref[...] = jnp.zeros_like(dw_ref)
        dov = do_ref[...]
        dys_ref[...] = lax.dot_general(dov, wo_ref[0:D_SSD, :], NT_DIMS, preferred_element_type=F32)
        dym = lax.dot_general(dov, wo_ref[D_SSD:2 * D_SSD, :], NT_DIMS, preferred_element_type=F32)
        av = at_ref[...]
        r = _rms(av)
        n = av * r
        dw_ref[...] += jnp.sum(dym * n, axis=0, keepdims=True)
        dat = _rms_bwd(dym * mnw_ref[...], n, r)
        dat_ref[...] = dat.astype(BF16)
        prod = dat * av
        for h in range(MLA_HEADS):
            dlt_ref[h] = jnp.sum(prod[:, HEAD_PAD * h:HEAD_PAD * (h + 1)], axis=1, keepdims=True)

    return pl.pallas_call(
        body, name=name, grid=(b, s // tm),
        in_specs=[_row(tm, d), _row(tm, D_SSD), _full((1, D_SSD)), _full(wo.shape)],
        out_specs=[_row(tm, D_SSD), _row(tm, D_SSD),
                   pl.BlockSpec((None, MLA_HEADS, tm, 1), lambda bb, i: (bb, 0, i, 0)), _full((1, D_SSD))],
        out_shape=[SDS((b, s, D_SSD), F32), SDS((b, s, D_SSD), BF16), SDS((b, MLA_HEADS, s, 1), F32),
                   SDS((1, D_SSD), F32)],
        compiler_params=_cparams(2))(dout, attn, mnw, wo)


def adaln_fwd(c_all, w_ada, b_ada, name):
    nb, d = c_all.shape
    n = w_ada.shape[1]

    def body(c_ref, w_ref, b_ref, m_ref, ca_ref):
        cv = c_ref[...]
        ca = (cv * _sigmoid(cv)).astype(BF16)
        ca_ref[...] = ca
        m_ref[...] = jnp.dot(ca, w_ref[...].astype(BF16), preferred_element_type=F32) + b_ref[...]

    return pl.pallas_call(
        body, name=name, out_shape=[SDS((nb, n), F32), SDS((nb, d), BF16)],
        compiler_params=pltpu.CompilerParams(vmem_limit_bytes=VMEM_LIMIT))(c_all, w_ada, b_ada)


def adaln_bwd(c_act, dmod_cols, name):
    d, n = c_act.shape[1], dmod_cols.shape[1]

    def body(c_ref, dm_ref, gw_ref):
        gw_ref[...] = lax.dot_general(c_ref[...], dm_ref[...].astype(BF16), TN_DIMS, preferred_element_type=F32)

    return pl.pallas_call(
        body, name=name, out_shape=SDS((d, n), F32),
        compiler_params=pltpu.CompilerParams(vmem_limit_bytes=VMEM_LIMIT))(c_act, dmod_cols)


def sum_rows(x, name):
    def body(x_ref, o_ref):
        o_ref[...] = jnp.sum(x_ref[...], axis=0, keepdims=True)
    return pl.pallas_call(body, name=name, out_shape=SDS((1, x.shape[1]), F32))(x)


def squeeze_heads(x, et_mat, name):
    def body(x_ref, et_ref, o_ref):
        xv = jnp.broadcast_to(x_ref[...], (8, x.shape[1]))
        o_ref[...] = _dot_01(xv, et_ref[...], 3)[0:1, :]
    return pl.pallas_call(body, name=name, out_shape=SDS((1, LANES), F32))(x, et_mat)


def sum_blocks(x, name):
    n, r, c = x.shape

    def body(x_ref, o_ref):
        acc = x_ref[0].astype(F32)
        for k in range(1, n):
            acc += x_ref[k].astype(F32)
        o_ref[...] = acc

    return pl.pallas_call(body, name=name, out_shape=SDS((r, c), F32),
                          compiler_params=pltpu.CompilerParams(vmem_limit_bytes=VMEM_LIMIT))(x)


def _adam_math(w, g, m, v):
    m = ADAM_B1 * m + (1.0 - ADAM_B1) * g
    v = ADAM_B2 * v + (1.0 - ADAM_B2) * (g * g)
    m_hat = m / (1.0 - ADAM_B1 ** ADAM_STEP)
    v_hat = v / (1.0 - ADAM_B2 ** ADAM_STEP)
    return -ADAM_LR * (m_hat / (jnp.sqrt(v_hat) + ADAM_EPS) + ADAM_WD * w), m, v


def adamw(w, g, m, v, name):
    r, c = w.shape
    tr = r
    for cand in (512, 256, 128, 64, 32, 16, 8):
        if r % cand == 0 and cand * c * 4 <= 2 * 1024 * 1024:
            tr = cand
            break

    def body(w_ref, g_ref, m_ref, v_ref, d_ref, mo_ref, vo_ref):
        d_ref[...], mo_ref[...], vo_ref[...] = _adam_math(w_ref[...], g_ref[...], m_ref[...], v_ref[...])

    spec = pl.BlockSpec((tr, c), lambda i: (i, 0))
    return pl.pallas_call(
        body, name=name, grid=(r // tr,), in_specs=[spec] * 4, out_specs=[spec] * 3,
        out_shape=[SDS((r, c), F32)] * 3, compiler_params=_cparams(1))(w, g, m, v)


def adamw_sum8(w, gparts, m, v, name):
    r, c = w.shape
    tr = next(cand for cand in (64, 32, 16, 8) if r % cand == 0)

    def body(w_ref, gp_ref, m_ref, v_ref, g_ref, d_ref, mo_ref, vo_ref):
        g = gp_ref[0].astype(F32)
        for k in range(1, N_DEV):
            g += gp_ref[k].astype(F32)
        g_ref[...] = g
        d_ref[...], mo_ref[...], vo_ref[...] = _adam_math(w_ref[...], g, m_ref[...], v_ref[...])

    spec = pl.BlockSpec((tr, c), lambda i: (i, 0))
    gspec = pl.BlockSpec((N_DEV, tr, c), lambda i: (0, i, 0))
    return pl.pallas_call(
        body, name=name, grid=(r // tr,), in_specs=[spec, gspec, spec, spec], out_specs=[spec] * 4,
        out_shape=[SDS((r, c), F32)] * 4, compiler_params=_cparams(1))(w, gparts, m, v)


PACK = {"ffn1_w_gate": (352, 352), "ffn1_w_up": (352, 352), "ffn1_w_down": (352, 352),
        "ffn2_w_gate": (352, 352), "ffn2_w_up": (352, 352), "ffn2_w_down": (352, 352),
        "w_out": (256, 256), "w_in": (406, 416), "w_ukv": (48, 48), "w_uq": (36, 48)}
TRANSPOSED = ("ffn1_w_gate", "ffn1_w_up", "ffn2_w_gate", "ffn2_w_up", "w_in", "w_ukv", "w_uq")
GROUP_FFN1 = ("ffn1_w_gate", "ffn1_w_up", "ffn1_w_down")
GROUP_REST = ("ffn2_w_gate", "ffn2_w_up", "ffn2_w_down", "w_out", "w_in", "w_ukv", "w_uq")
GRAD_GROUPS = (("ffn2", ("ffn2_w_gate", "ffn2_w_up", "ffn2_w_down")), ("mixer", ("w_out", "w_in", "w_ukv", "w_uq")),
               ("ffn1_up", ("ffn1_w_gate", "ffn1_w_up")), ("ffn1_down", ("ffn1_w_down",)))


def _pack_offsets(names):
    off, o = {}, 0
    for n in names:
        off[n] = (o, PACK[n][0])
        o += PACK[n][1]
    return off, o


def _shard_to_rows(name, w):
    w = w[0]
    if name in TRANSPOSED:
        w = w.T
    return w.reshape(-1, D_MODEL)


def _rows_to_shard(name, rows, like):
    shp = like.shape[1:]
    if name in TRANSPOSED:
        return rows.reshape(shp[1], shp[0]).T[None]
    return rows.reshape(shp)[None]


def _pack_shards(ws, names, dtype):
    parts = []
    for name in names:
        real, padded = PACK[name]
        rows = _shard_to_rows(name, ws[name]).astype(dtype)
        if padded > real:
            rows = jnp.pad(rows, ((0, padded - real), (0, 0)))
        parts.append(rows)
    return jnp.concatenate(parts, axis=0)


def _grad_rows(name, gw):
    real, padded = PACK[name]
    if name == "w_in":
        rows = _in_proj_rows_inv(gw).reshape(N_DEV, -1, D_MODEL)
    elif name == "w_ukv":
        hd = MLA_HEADS * HEAD_PAD
        rows = jnp.concatenate([gw[:hd].reshape(MLA_HEADS, HEAD_PAD, KV_LORA)[:, :QK_NOPE],
                                gw[hd:].reshape(MLA_HEADS, V_HEAD, KV_LORA)], axis=1).reshape(N_DEV, -1, D_MODEL)
    elif name == "w_uq":
        rows = gw.reshape(MLA_HEADS, HEAD_PAD, Q_LORA)[:, :QK_DIM].reshape(N_DEV, -1, D_MODEL)
    else:
        rows = gw.reshape(N_DEV, -1, D_MODEL)
    if padded > real:
        rows = jnp.pad(rows, ((0, 0), (0, padded - real), (0, 0)))
    return rows


def _pack_rows(arrs):
    parts = []
    for a in arrs:
        flat = a.reshape(-1).astype(F32)
        pad = (-flat.shape[0]) % D_MODEL
        if pad:
            flat = jnp.pad(flat, (0, pad))
        parts.append(flat.reshape(-1, D_MODEL))
    out = jnp.concatenate(parts, axis=0)
    pad = (-out.shape[0]) % 8
    if pad:
        out = jnp.pad(out, ((0, pad), (0, 0)))
    return out


def _unpack_rows(packed, shapes):
    out, row = [], 0
    for shp in shapes:
        n = math.prod(shp)
        nrow = -(-n // D_MODEL)
        out.append(packed[row:row + nrow].reshape(-1)[:n].reshape(shp))
        row += nrow
    return out


def _in_proj_rows(w_t):
    return jnp.concatenate([w_t[0:2560], w_t[2576:2960], w_t[2960:3216], w_t[2560:2576], w_t[3216:3248],
                            jnp.zeros((D_IN_PAD - D_IN, D_MODEL), w_t.dtype)], axis=0)


def _in_proj_rows_inv(d):
    return jnp.concatenate([d[0:2560], d[3200:3216], d[2560:2944], d[2944:3200], d[3216:3248]], axis=0)


def _rope_tables(positions):
    inv_freq = ROPE_THETA ** (-jnp.arange(0, QK_ROPE, 2, dtype=F32) / QK_ROPE)
    ang = positions[..., None].astype(F32) * inv_freq
    cos, sin = jnp.cos(ang), jnp.sin(ang)
    one = jnp.ones(ang.shape[:2] + (QK_NOPE,), F32)
    zero = jnp.zeros_like(one)
    z16, z32, o32 = zero[..., :16], zero[..., :32], one[..., :32]
    cc = jnp.concatenate([one, cos, cos, o32], axis=-1)
    sp = jnp.concatenate([zero, z16, sin, z32], axis=-1)
    sm = jnp.concatenate([zero, -sin, z16, z32], axis=-1)
    return cc, sp, sm


def weight_views(g_ffn1, g_rest):
    def _seg(name):
        names, g = (GROUP_FFN1, g_ffn1) if name in GROUP_FFN1 else (GROUP_REST, g_rest)
        o, r = _pack_offsets(names)[0][name]
        return g[:, o:o + r]

    full = lambda name: _seg(name).reshape(-1, D_MODEL)
    ukv = _seg("w_ukv").reshape(MLA_HEADS, QK_NOPE + V_HEAD, KV_LORA)
    wukv_t = jnp.concatenate([jnp.pad(ukv[:, :QK_NOPE], ((0, 0), (0, HEAD_PAD - QK_NOPE), (0, 0))).reshape(-1, KV_LORA),
                              ukv[:, QK_NOPE:].reshape(-1, KV_LORA)], axis=0)
    uq = _seg("w_uq").reshape(MLA_HEADS, QK_DIM, Q_LORA)
    wuq_t = jnp.pad(uq, ((0, 0), (0, HEAD_PAD - QK_DIM), (0, 0))).reshape(-1, Q_LORA)
    return dict(wg1_t=full("ffn1_w_gate"), wu1_t=full("ffn1_w_up"), wd1=full("ffn1_w_down"),
                wg2_t=full("ffn2_w_gate"), wu2_t=full("ffn2_w_up"), wd2=full("ffn2_w_down"),
                wo=full("w_out"), win_t=_in_proj_rows(full("w_in")), wukv_t=wukv_t, wuq_t=wuq_t)


def _ffn_bwd(tag, dxn, do, dgate, x, h, gg, uu, a, sc, norm_w, wg_t, wu_t, wd, below):
    f2 = wd.shape[0] // 2
    dgg, duu = ffn_dact(do, wd, gg, uu, tag + "_dact")
    dwg_t = mm_tn(dgg, h, f2, D_MODEL, tag + "_dwg")
    dwu_t = mm_tn(duu, h, f2, D_MODEL, tag + "_dwu")
    dwd = mm_tn(a, do, f2, D_MODEL, tag + "_dwd")
    dx, dsc, dsh, dnw, *nxt = dh_norm_bwd([dgg, duu], [wg_t, wu_t], x, dxn, norm_w, sc, tag + "_dh", below)
    return dx, (dsh, dsc, dgate), dnw, (dwg_t, dwu_t, dwd), nxt


def local_step(x, tgt, positions, mod, wv, p):
    nb, s, d = x.shape
    sh1, sc1, g1, sh2, sc2, g2, sh3, sc3, g3 = mod
    cc, sp, sm = _rope_tables(positions)
    lane_head = jnp.arange(D_SSD, dtype=I32)[None, :] // SSD_HEAD_DIM
    e_mat = (lane_head == jnp.arange(LANES, dtype=I32)[:, None]).astype(BF16)
    et_mat = e_mat.T
    rr, cl = jnp.arange(LANES, dtype=I32)[:, None], jnp.arange(LANES, dtype=I32)[None, :]
    place = ((cl == rr + (QK_NOPE - SSD_HEADS)) & (rr >= SSD_HEADS) & (rr < SSD_HEADS + QK_ROPE)).astype(F32)
    dtb = jnp.pad(p["dt_bias"], ((0, 0), (0, LANES - SSD_HEADS)))
    alog = jnp.pad(p["a_log"], ((0, 0), (0, LANES - SSD_HEADS)))
    dskip_e = jnp.repeat(p["d_skip"], SSD_HEAD_DIM, axis=1)

    h1 = norm_mod(x, p["norm_ffn1"], sc1, sh1, "ffn1_norm")
    gg1, uu1, a1 = ffn_up(h1, wv["wg1_t"], wv["wu1_t"], "ffn1_up")
    x1, o1 = ffn_down(a1, wv["wd1"], x, g1, 0.5, "ffn1_down")
    h2 = norm_mod(x1, p["norm_mix"], sc2, sh2, "mix_norm")
    z, u, cq, ckv, misc = in_proj(h2, wv["win_t"], "in_proj")
    xs, bm, cm_ = conv_fwd(u, p["conv_w"], p["conv_b"], "conv_fwd")
    ys, y, prev = ssd_fwd(xs, bm, cm_, misc, z, dtb, alog, dskip_e, p["ssd_norm_w"], e_mat, "ssd_fwd")
    q, k, v, qn, kvn = qkv_fwd(cq, ckv, misc, cc, sp, sm, p["q_norm_w"], p["kv_norm_w"], wv["wuq_t"], wv["wukv_t"],
                               place, "qkv_fwd")
    attn, lse = flash_fwd(q, k, v, "flash_fwd")
    x2, o2, ym = out_proj(ys, attn, p["mla_norm_w"], wv["wo"], x1, g2, "out_proj")
    h3 = norm_mod(x2, p["norm_ffn2"], sc3, sh3, "ffn2_norm")
    gg3, uu3, a3 = ffn_up(h3, wv["wg2_t"], wv["wu2_t"], "ffn2_up")
    x3, o3 = ffn_down(a3, wv["wd2"], x2, g3, 0.5, "ffn2_down")
    loss, dx3, dnfin, do3, dg3 = final_loss(x3, p["norm_final"], tgt, (o3, g3, 0.5), "final_loss")

    dx2, dmod3, dnf2, (dwg2, dwu2, dwd2), (dout, dg2) = _ffn_bwd(
        "ffn2", dx3, do3, dg3, x2, h3, gg3, uu3, a3, sc3, p["norm_ffn2"], wv["wg2_t"], wv["wu2_t"], wv["wd2"],
        (o2, g2, 1.0))
    dys, dattn, dlt, dmlan = out_proj_bwd(dout, attn, p["mla_norm_w"], wv["wo"], "out_proj_bwd")
    dwo = jnp.concatenate([mm_tn(ys, dout, D_SSD, D_MODEL, "dwo_ssd"), mm_tn(ym, dout, D_SSD, D_MODEL, "dwo_mla")], axis=0)
    dxs, dbm, dcm, dz, ddt, dssdn, ddsk_lane, ddtb, dalog = ssd_bwd(
        dys, y, z, xs, bm, cm_, misc, prev, dtb, alog, dskip_e, p["ssd_norm_w"], e_mat, et_mat, "ssd_bwd")
    dq, dk, dv = flash_bwd(q, k, v, dattn, lse, dlt, "flash_bwd")
    dcq, dckv, dmisc, dqp, dkvc, dqn, dkvn = qkv_bwd(dq, dk, dv, ddt, cq, ckv, cc, sp, sm, p["q_norm_w"], p["kv_norm_w"],
                                                     wv["wuq_t"], wv["wukv_t"], place.T, "qkv_bwd")
    dwuq = mm_tn(dqp, qn, MLA_HEADS * HEAD_PAD, Q_LORA, "dwuq")
    dwukv = mm_tn(dkvc, kvn, MLA_HEADS * HEAD_PAD, KV_LORA, "dwukv")
    dvv, dconv = conv_bwd_a(dxs, dbm, dcm, u, p["conv_w"], p["conv_b"], "conv_bwd_a")
    du = conv_bwd_b(dvv, p["conv_w"], "conv_bwd_b")
    dproj = jnp.concatenate([dz, du, dcq, dckv, dmisc], axis=-1)
    dwin = mm_tn(dproj, h2, D_IN_PAD // 2, D_MODEL, "dwin")
    dx1, dsc2, dsh2, dnmix, do1, dg1 = dh_norm_bwd([dproj], [wv["win_t"]], x1, dx2, p["norm_mix"], sc2, "mix_dh",
                                                   (o1, g1, 0.5))
    dx0, dmod1, dnf1, (dwg1, dwu1, dwd1), _ = _ffn_bwd(
        "ffn1", dx1, do1, dg1, x, h1, gg1, uu1, a1, sc1, p["norm_ffn1"], wv["wg1_t"], wv["wu1_t"], wv["wd1"], None)

    dmod = jnp.concatenate([*dmod1, dsh2, dsc2, dg2, *dmod3], axis=1).reshape(nb, N_MOD * d)
    return dict(
        loss=loss, dx=dx0, dmod=dmod, norm_ffn1=dnf1, norm_mix=dnmix, norm_ffn2=dnf2, norm_final=dnfin,
        ssd_norm_w=dssdn, mla_norm_w=dmlan, q_norm_w=dqn, kv_norm_w=dkvn,
        dt_bias=ddtb[:, :SSD_HEADS], a_log=dalog[:, :SSD_HEADS],
        d_skip=squeeze_heads(ddsk_lane, et_mat, "d_skip_heads")[:, :SSD_HEADS],
        conv_b=dconv[4:5], conv_w=dconv[0:4],
        gw=dict(ffn1_w_gate=dwg1, ffn1_w_up=dwu1, ffn1_w_down=dwd1, ffn2_w_gate=dwg2, ffn2_w_up=dwu2, ffn2_w_down=dwd2,
                w_out=dwo, w_in=dwin, w_ukv=dwukv, w_uq=dwuq))


def kernel(x, c, positions, w_ada, b_ada, norm_ffn1, ffn1_w_gate, ffn1_w_up, ffn1_w_down, norm_mix, w_in, conv_w, conv_b, dt_bias, a_log, d_skip, ssd_norm_w, q_norm_w, w_uq, kv_norm_w, w_ukv, mla_norm_w, w_out, norm_ffn2, ffn2_w_gate, ffn2_w_up, ffn2_w_down, norm_final, loss_target, m_w_ada, m_b_ada, m_norm_ffn1, m_ffn1_w_gate, m_ffn1_w_up, m_ffn1_w_down, m_norm_mix, m_w_in, m_conv_w, m_conv_b, m_dt_bias, m_a_log, m_d_skip, m_ssd_norm_w, m_q_norm_w, m_w_uq, m_kv_norm_w, m_w_ukv, m_mla_norm_w, m_w_out, m_norm_ffn2, m_ffn2_w_gate, m_ffn2_w_up, m_ffn2_w_down, m_norm_final, v_w_ada, v_b_ada, v_norm_ffn1, v_ffn1_w_gate, v_ffn1_w_up, v_ffn1_w_down, v_norm_mix, v_w_in, v_conv_w, v_conv_b, v_dt_bias, v_a_log, v_d_skip, v_ssd_norm_w, v_q_norm_w, v_w_uq, v_kv_norm_w, v_w_ukv, v_mla_norm_w, v_w_out, v_norm_ffn2, v_ffn2_w_gate, v_ffn2_w_up, v_ffn2_w_down, v_norm_final):
    names = ["w_ada", "b_ada", "norm_ffn1", "ffn1_w_gate", "ffn1_w_up", "ffn1_w_down", "norm_mix", "w_in", "conv_w",
             "conv_b", "dt_bias", "a_log", "d_skip", "ssd_norm_w", "q_norm_w", "w_uq", "kv_norm_w", "w_ukv",
             "mla_norm_w", "w_out", "norm_ffn2", "ffn2_w_gate", "ffn2_w_up", "ffn2_w_down", "norm_final"]
    W = dict(zip(names, (w_ada, b_ada, norm_ffn1, ffn1_w_gate, ffn1_w_up, ffn1_w_down, norm_mix, w_in, conv_w, conv_b, dt_bias, a_log, d_skip, ssd_norm_w, q_norm_w, w_uq, kv_norm_w, w_ukv, mla_norm_w, w_out, norm_ffn2, ffn2_w_gate, ffn2_w_up, ffn2_w_down, norm_final)))
    M = dict(zip(names, (m_w_ada, m_b_ada, m_norm_ffn1, m_ffn1_w_gate, m_ffn1_w_up, m_ffn1_w_down, m_norm_mix, m_w_in, m_conv_w, m_conv_b, m_dt_bias, m_a_log, m_d_skip, m_ssd_norm_w, m_q_norm_w, m_w_uq, m_kv_norm_w, m_w_ukv, m_mla_norm_w, m_w_out, m_norm_ffn2, m_ffn2_w_gate, m_ffn2_w_up, m_ffn2_w_down, m_norm_final)))
    V = dict(zip(names, (v_w_ada, v_b_ada, v_norm_ffn1, v_ffn1_w_gate, v_ffn1_w_up, v_ffn1_w_down, v_norm_mix, v_w_in, v_conv_w, v_conv_b, v_dt_bias, v_a_log, v_d_skip, v_ssd_norm_w, v_q_norm_w, v_w_uq, v_kv_norm_w, v_w_ukv, v_mla_norm_w, v_w_out, v_norm_ffn2, v_ffn2_w_gate, v_ffn2_w_up, v_ffn2_w_down, v_norm_final)))

    nb, s, d = x.shape
    me = 4 * lax.axis_index("x") + 2 * lax.axis_index("y") + lax.axis_index("c")
    n_ada = w_ada.shape[2]

    cshape = [(nb, d), conv_w.shape[1:]]
    cg = all_gather8(_pack_rows([c, conv_w[0]]), "gather_c")
    c_all = jnp.stack([_unpack_rows(cg[k], cshape)[0] for k in range(N_DEV)]).reshape(N_DEV * nb, d)
    conv_w_full = jnp.concatenate([_unpack_rows(cg[k], cshape)[1] for k in range(N_DEV)], axis=1)
    g_ffn1 = all_gather8(_pack_shards(W, GROUP_FFN1, BF16), "gather_w_ffn1")

    b_ada_cols = lax.dynamic_slice(b_ada, (0, me * n_ada), (1, n_ada))
    mod_cols, c_act = adaln_fwd(c_all, w_ada[0], b_ada_cols, "adaln_fwd")
    mod_g = all_gather8(mod_cols, "gather_mod")
    g_ffn1, mod_g, rest = lax.optimization_barrier((g_ffn1, mod_g, _pack_shards(W, GROUP_REST, BF16)))
    wv = weight_views(g_ffn1, sc_all_gather8(rest, "gather_w_rest", 1))
    mod = lax.dynamic_slice(mod_g, (0, me * nb, 0), (N_DEV, nb, n_ada)).transpose(1, 0, 2).reshape(nb, N_MOD, 1, d)
    mod = [mod[:, k] for k in range(N_MOD)]

    P = dict(W)
    P["conv_w"] = conv_w_full
    P["norm_final"] = norm_final.reshape(1, d)
    R = local_step(x, loss_target, positions, mod, wv, P)

    dmod = R["dmod"]
    partial_shapes = [(1,), (1, d), (1, d), (1, d), (1, d), (1, d), (1, d), (1, Q_LORA), (1, KV_LORA),
                      (1, SSD_HEADS), (1, SSD_HEADS), (1, SSD_HEADS), (1, D_CONV), (4, D_CONV), (1, N_MOD * d),
                      (nb, N_MOD * d)]
    partial = _pack_rows([R["loss"][0, :1], R["norm_ffn1"], R["norm_mix"], R["norm_ffn2"], R["norm_final"],
                          R["ssd_norm_w"], R["mla_norm_w"], R["q_norm_w"], R["kv_norm_w"],
                          R["dt_bias"], R["a_log"], R["d_skip"], R["conv_b"], R["conv_w"],
                          sum_rows(dmod, "dmod_rows"), dmod])
    partial_g = all_gather8(partial, "gather_partials")
    (loss, g_nf1, g_nmix, g_nf2, g_nfin, g_ssdn, g_mlan, g_qn, g_kvn, g_dtb, g_alog, g_dskip, g_convb, g_convw,
     g_bada, _) = _unpack_rows(sum_blocks(partial_g, "sum_partials"), partial_shapes)
    dmod_all = jnp.stack([_unpack_rows(partial_g[k], partial_shapes)[-1] for k in range(N_DEV)]).reshape(N_DEV * nb, -1)
    g_wada = adaln_bwd(c_act, lax.dynamic_slice(dmod_all, (0, me * n_ada), (N_DEV * nb, n_ada)), "adaln_bwd")
    n_cw = conv_w.shape[2]
    G = {"w_ada": g_wada[None], "b_ada": g_bada, "norm_ffn1": g_nf1, "norm_mix": g_nmix, "norm_ffn2": g_nf2,
         "norm_final": g_nfin.reshape(d), "ssd_norm_w": g_ssdn, "mla_norm_w": g_mlan, "q_norm_w": g_qn,
         "kv_norm_w": g_kvn, "dt_bias": g_dtb, "a_log": g_alog, "d_skip": g_dskip, "conv_b": g_convb,
         "conv_w": lax.dynamic_slice(g_convw, (0, me * n_cw), (4, n_cw))[None]}

    DW, NM, NV = {}, {}, {}
    gw = R["gw"]
    for k, (tag, group) in enumerate(GRAD_GROUPS):
        send = jnp.concatenate([_grad_rows(name, gw[name]) for name in group], axis=1).astype(BF16)
        recv = sc_all_to_all8(send, "exchange_" + tag, 2 + k)
        big = adamw_sum8(_pack_shards(W, group, F32), recv, _pack_shards(M, group, F32), _pack_shards(V, group, F32),
                         "adamw_" + tag)
        for name, (o, r) in _pack_offsets(group)[0].items():
            G[name], DW[name], NM[name], NV[name] = [_rows_to_shard(name, t[o:o + r], W[name]) for t in big]
    dwa, nma, nva = adamw(w_ada[0], g_wada, m_w_ada[0], v_w_ada[0], "adamw_w_ada")
    DW["w_ada"], NM["w_ada"], NV["w_ada"] = dwa[None], nma[None], nva[None]
    small = [n for n in names if n not in DW]
    shapes = [W[n].shape for n in small]
    outs = adamw(_pack_rows([W[n] for n in small]), _pack_rows([G[n] for n in small]),
                 _pack_rows([M[n] for n in small]), _pack_rows([V[n] for n in small]), "adamw_small")
    for res, dst in zip(outs, (DW, NM, NV)):
        for n, t in zip(small, _unpack_rows(res, shapes)):
            dst[n] = t
    return (loss.reshape(()), R["dx"], *[G[n] for n in names], *[DW[n] for n in names], *[NM[n] for n in names],
            *[NV[n] for n in names])
```

```python
import math

import jax
import jax.numpy as jnp
from jax import lax
from jax.experimental import pallas as pl
from jax.experimental.pallas import tpu as pltpu
from jax.experimental.pallas import tpu_sc as plsc

F32, BF16, I32 = jnp.float32, jnp.bfloat16, jnp.int32
HI = lax.Precision.HIGHEST
SDS = jax.ShapeDtypeStruct
MESH = pl.DeviceIdType.MESH

D_MODEL = 1024
D_FF = 2816
D_SSD = 1024
SSD_HEADS = 16
SSD_HEAD_DIM = 64
SSD_GROUPS = 2
SSD_STATE = 128
CHUNK = 128
MLA_HEADS = 8
QK_NOPE = 64
QK_ROPE = 32
QK_DIM = 96
V_HEAD = 128
Q_LORA = 384
KV_LORA = 256
ROPE_THETA = 10000.0
N_MOD = 9
EPS = 1e-6
D_CONV = 1536
D_IN = 3248
D_IN_PAD = 3328
HEAD_PAD = 128
N_DEV = 8
ADAM_LR, ADAM_B1, ADAM_B2, ADAM_EPS, ADAM_WD, ADAM_STEP = 0.001, 0.9, 0.999, 1e-08, 0.01, 10

SAVED_ACT = BF16
VMEM_LIMIT = 56 * 1024 * 1024
LANES = 128
NT_DIMS = (((1,), (1,)), ((), ()))
TN_DIMS = (((0,), (0,)), ((), ()))


def _cparams(n_axes):
    return pltpu.CompilerParams(dimension_semantics=("arbitrary",) * n_axes, vmem_limit_bytes=VMEM_LIMIT)


def _row(tm, d):
    return pl.BlockSpec((None, tm, d), lambda b, i: (b, i, 0))


def _bvec(d):
    return pl.BlockSpec((None, 1, d), lambda b, i: (b, 0, 0))


def _full(shape):
    n = len(shape)
    return pl.BlockSpec(shape, lambda *_: (0,) * n)


def _sigmoid(x):
    return 1.0 / (1.0 + jnp.exp(-x))


def _softplus(x):
    return jnp.maximum(x, 0.0) + jnp.log(1.0 + jnp.exp(-jnp.abs(x)))


def _rms(x):
    return lax.rsqrt(jnp.mean(x * x, axis=-1, keepdims=True) + EPS)


def _rms_bwd(dn, n, r):
    return r * (dn - n * jnp.mean(dn * n, axis=-1, keepdims=True))


def _first_step():
    return (pl.program_id(0) == 0) & (pl.program_id(1) == 0)


def all_gather8(x, name):
    r, c = x.shape

    def body(x_ref, out_ref, send_sems, recv_sems, local_sem):
        mx, my, mc = lax.axis_index("x"), lax.axis_index("y"), lax.axis_index("c")
        me, sibling = (mx, my, mc), (mx, my, 1 - mc)
        chips = [(1 - mx, my), (mx, 1 - my), (1 - mx, 1 - my)]

        def rows(px, py, pc):
            return out_ref.at[4 * px + 2 * py + pc]

        def copy(k, block, to, src=None):
            return pltpu.make_async_remote_copy(
                src_ref=rows(*block) if src is None else src, dst_ref=rows(*block),
                send_sem=send_sems.at[k], recv_sem=recv_sems.at[k], device_id=to, device_id_type=MESH)

        mine = pltpu.make_async_copy(x_ref, rows(*me), local_sem)
        mine.start()
        first = [copy(0, me, sibling, src=x_ref)]
        first += [copy(1 + j, me, (*chip, mc), src=x_ref) for j, chip in enumerate(chips)]
        for cp in first:
            cp.start()
        passed = [copy(4 + j, (*chip, mc), sibling) for j, chip in enumerate(chips)]
        for j, chip in enumerate(chips):
            copy(1 + j, (*chip, mc), me).wait_recv()
            passed[j].start()
        copy(0, sibling, me).wait_recv()
        for j, chip in enumerate(chips):
            copy(4 + j, (*chip, 1 - mc), me).wait_recv()
        for cp in first + passed:
            cp.wait_send()
        mine.wait()

    return pl.pallas_call(
        body, name=name,
        out_shape=SDS((N_DEV, r, c), x.dtype),
        in_specs=[pl.BlockSpec(memory_space=pl.ANY)],
        out_specs=pl.BlockSpec(memory_space=pl.ANY),
        scratch_shapes=[pltpu.SemaphoreType.DMA((7,)), pltpu.SemaphoreType.DMA((7,)), pltpu.SemaphoreType.DMA],
    )(x)


def all_to_all8(x, name):
    _, r, c = x.shape

    def body(x_ref, out_ref, send_sems, recv_sems, local_sem):
        mx, my, mc = lax.axis_index("x"), lax.axis_index("y"), lax.axis_index("c")
        me = 4 * mx + 2 * my + mc
        mine = pltpu.make_async_copy(x_ref.at[me], out_ref.at[me], local_sem)
        mine.start()
        copies = []
        for rel in range(1, N_DEV):
            px = 1 - mx if rel & 4 else mx
            py = 1 - my if rel & 2 else my
            pc = 1 - mc if rel & 1 else mc
            cp = pltpu.make_async_remote_copy(
                src_ref=x_ref.at[4 * px + 2 * py + pc], dst_ref=out_ref.at[me],
                send_sem=send_sems.at[rel - 1], recv_sem=recv_sems.at[rel - 1],
                device_id=(px, py, pc), device_id_type=MESH)
            cp.start()
            copies.append(cp)
        for cp in copies:
            cp.wait()
        mine.wait()

    return pl.pallas_call(
        body, name=name,
        out_shape=SDS((N_DEV, r, c), x.dtype),
        in_specs=[pl.BlockSpec(memory_space=pl.ANY)],
        out_specs=pl.BlockSpec(memory_space=pl.ANY),
        scratch_shapes=[pltpu.SemaphoreType.DMA((7,)), pltpu.SemaphoreType.DMA((7,)), pltpu.SemaphoreType.DMA],
    )(x)


def _sequencer_kernel(name, collective_id):
    return pl.kernel(
        mesh=plsc.ScalarSubcoreMesh(axis_name="seq", num_cores=1), name=name,
        scratch_types=(pltpu.SemaphoreType.DMA((7,)), pltpu.SemaphoreType.DMA((7,)), pltpu.SemaphoreType.DMA),
        compiler_params=pltpu.CompilerParams(collective_id=collective_id))


def _handshake(peers):
    barrier = pltpu.get_barrier_semaphore()
    for peer in peers:
        pl.semaphore_signal(barrier, inc=1, device_id=peer, device_id_type=MESH)
    pl.semaphore_wait(barrier, len(peers))


def sc_all_gather8(x, name, collective_id):
    r, c = x.shape
    x_ref = jax.new_ref(x, memory_space=pltpu.MemorySpace.HBM)
    out_ref = jax.empty_ref(SDS((N_DEV, r, c), x.dtype), memory_space=pltpu.MemorySpace.HBM)

    @_sequencer_kernel(name, collective_id)
    def launch(send_sems, recv_sems, local_sem):
        mx, my, mc = lax.axis_index("x"), lax.axis_index("y"), lax.axis_index("c")
        me, sibling = (mx, my, mc), (mx, my, 1 - mc)
        chips = [(1 - mx, my), (mx, 1 - my), (1 - mx, 1 - my)]
        _handshake([sibling] + [(*chip, mc) for chip in chips])

        def rows(px, py, pc):
            return out_ref.at[4 * px + 2 * py + pc]

        def copy(k, block, to, src=None):
            return pltpu.make_async_remote_copy(
                src_ref=rows(*block) if src is None else src, dst_ref=rows(*block),
                send_sem=send_sems.at[k], recv_sem=recv_sems.at[k], device_id=to, device_id_type=MESH)

        mine = pltpu.make_async_copy(x_ref, rows(*me), local_sem)
        mine.start()
        first = [copy(0, me, sibling, src=x_ref)]
        first += [copy(1 + j, me, (*chip, mc), src=x_ref) for j, chip in enumerate(chips)]
        for cp in first:
            cp.start()
        passed = [copy(4 + j, (*chip, mc), sibling) for j, chip in enumerate(chips)]
        for j, chip in enumerate(chips):
            copy(1 + j, (*chip, mc), me).wait_recv()
            passed[j].start()
        copy(0, sibling, me).wait_recv()
        for j, chip in enumerate(chips):
            copy(4 + j, (*chip, 1 - mc), me).wait_recv()
        for cp in first + passed:
            cp.wait_send()
        mine.wait()

    launch()
    return out_ref[...]


def sc_all_to_all8(x, name, collective_id):
    x_ref = jax.new_ref(x, memory_space=pltpu.MemorySpace.HBM)
    out_ref = jax.empty_ref(SDS(x.shape, x.dtype), memory_space=pltpu.MemorySpace.HBM)

    @_sequencer_kernel(name, collective_id)
    def launch(send_sems, recv_sems, local_sem):
        mx, my, mc = lax.axis_index("x"), lax.axis_index("y"), lax.axis_index("c")
        me = 4 * mx + 2 * my + mc
        peers = [(1 - mx if rel & 4 else mx, 1 - my if rel & 2 else my, 1 - mc if rel & 1 else mc)
                 for rel in range(1, N_DEV)]
        _handshake(peers)
        mine = pltpu.make_async_copy(x_ref.at[me], out_ref.at[me], local_sem)
        mine.start()
        copies = []
        for k, (px, py, pc) in enumerate(peers):
            cp = pltpu.make_async_remote_copy(
                src_ref=x_ref.at[4 * px + 2 * py + pc], dst_ref=out_ref.at[me],
                send_sem=send_sems.at[k], recv_sem=recv_sems.at[k], device_id=(px, py, pc), device_id_type=MESH)
            cp.start()
            copies.append(cp)
        for cp in copies:
            cp.wait()
        mine.wait()

    launch()
    return out_ref[...]


def norm_mod(x, w, sc, sh, name):
    b, s, d = x.shape
    tm = min(512, s)

    def body(x_ref, w_ref, sc_ref, sh_ref, h_ref):
        xv = x_ref[...]
        n = xv * _rms(xv)
        h_ref[...] = ((n * w_ref[...]) * (1.0 + sc_ref[...]) + sh_ref[...]).astype(BF16)

    return pl.pallas_call(
        body, name=name, grid=(b, s // tm),
        in_specs=[_row(tm, d), _full((1, d)), _bvec(d), _bvec(d)],
        out_specs=_row(tm, d), out_shape=SDS((b, s, d), BF16), compiler_params=_cparams(2))(x, w, sc, sh)


def ffn_up(h, wg_t, wu_t, name):
    b, s, d = h.shape
    f = wg_t.shape[0]
    tm, tn = min(512, s), f // 2

    def body(h_ref, wg_ref, wu_ref, g_ref, u_ref, a_ref):
        hv = h_ref[...]
        g = lax.dot_general(hv, wg_ref[...], NT_DIMS, preferred_element_type=F32)
        u = lax.dot_general(hv, wu_ref[...], NT_DIMS, preferred_element_type=F32)
        g_ref[...] = g.astype(g_ref.dtype)
        u_ref[...] = u.astype(u_ref.dtype)
        a_ref[...] = (g * _sigmoid(g) * u).astype(BF16)

    hs = pl.BlockSpec((None, tm, d), lambda j, bb, i: (bb, i, 0))
    ws = pl.BlockSpec((tn, d), lambda j, bb, i: (j, 0))
    os_ = pl.BlockSpec((None, tm, tn), lambda j, bb, i: (bb, i, j))
    return pl.pallas_call(
        body, name=name, grid=(f // tn, b, s // tm),
        in_specs=[hs, ws, ws], out_specs=[os_, os_, os_],
        out_shape=[SDS((b, s, f), SAVED_ACT), SDS((b, s, f), SAVED_ACT), SDS((b, s, f), BF16)],
        compiler_params=_cparams(3))(h, wg_t, wu_t)


def ffn_down(a, wd, x, gate, scale, name):
    b, s, f = a.shape
    d = wd.shape[1]
    tm = min(512, s)

    def body(a_ref, wd_ref, x_ref, g_ref, xn_ref, o_ref):
        o = jnp.dot(a_ref[...], wd_ref[...], preferred_element_type=F32)
        xn_ref[...] = x_ref[...] + (scale * g_ref[...]) * o
        o_ref[...] = o.astype(BF16)

    return pl.pallas_call(
        body, name=name, grid=(b, s // tm),
        in_specs=[_row(tm, f), _full((f, d)), _row(tm, d), _bvec(d)],
        out_specs=[_row(tm, d), _row(tm, d)],
        out_shape=[SDS((b, s, d), F32), SDS((b, s, d), BF16)], compiler_params=_cparams(2))(a, wd, x, gate)


def ffn_dact(do, wd, g, u, name):
    b, s, d = do.shape
    f = wd.shape[0]
    tm, tn = min(512, s), f // 2

    def body(do_ref, wd_ref, g_ref, u_ref, dg_ref, du_ref):
        da = lax.dot_general(do_ref[...], wd_ref[...], NT_DIMS, preferred_element_type=F32)
        gv = g_ref[...].astype(F32)
        sg = _sigmoid(gv)
        dg_ref[...] = (da * u_ref[...].astype(F32) * (sg * (1.0 + gv * (1.0 - sg)))).astype(BF16)
        du_ref[...] = (da * (gv * sg)).astype(BF16)

    dos = pl.BlockSpec((None, tm, d), lambda j, bb, i: (bb, i, 0))
    ws = pl.BlockSpec((tn, d), lambda j, bb, i: (j, 0))
    es = pl.BlockSpec((None, tm, tn), lambda j, bb, i: (bb, i, j))
    return pl.pallas_call(
        body, name=name, grid=(f // tn, b, s // tm),
        in_specs=[dos, ws, es, es], out_specs=[es, es],
        out_shape=[SDS((b, s, f), BF16), SDS((b, s, f), BF16)], compiler_params=_cparams(3))(do, wd, g, u)


def mm_tn(a, bm, tma, tnb, name):
    b, s, ka = a.shape
    nb = bm.shape[2]
    tk = min(2048, s)
    nk = s // tk

    def body(a_ref, b_ref, o_ref, acc):
        first = (pl.program_id(2) == 0) & (pl.program_id(3) == 0)
        last = (pl.program_id(2) == b - 1) & (pl.program_id(3) == nk - 1)
        part = lax.dot_general(a_ref[...], b_ref[...], TN_DIMS, preferred_element_type=F32)

        @pl.when(first)
        def _():
            acc[...] = part

        @pl.when(jnp.logical_not(first))
        def _():
            acc[...] += part

        @pl.when(last)
        def _():
            o_ref[...] = acc[...].astype(BF16)

    return pl.pallas_call(
        body, name=name, grid=(ka // tma, nb // tnb, b, nk),
        in_specs=[pl.BlockSpec((None, tk, tma), lambda i, j, bb, k: (bb, k, i)),
                  pl.BlockSpec((None, tk, tnb), lambda i, j, bb, k: (bb, k, j))],
        out_specs=pl.BlockSpec((tma, tnb), lambda i, j, bb, k: (i, j)),
        out_shape=SDS((ka, nb), BF16), scratch_shapes=[pltpu.VMEM((tma, tnb), F32)],
        compiler_params=_cparams(4))(a, bm)


def _gate_bwd_specs(tm, d, b, s):
    return ([_row(tm, d), _bvec(d)], [_row(tm, d), _bvec(d)], [SDS((b, s, d), BF16), SDS((b, 1, d), F32)])


def _gate_bwd_tile(dx, scale, o_ref, g_ref, do_ref, dg_ref):
    do_ref[...] = ((scale * g_ref[...]) * dx).astype(BF16)
    dg_ref[...] += jnp.sum(scale * dx * o_ref[...].astype(F32), axis=0, keepdims=True)


def dh_norm_bwd(dys, wts, x, dxn, w, sc, name, below=None):
    b, s, d = x.shape
    tm = min(256, s)
    n_in = len(dys)
    extra_in, extra_out, extra_shape = _gate_bwd_specs(tm, d, b, s) if below else ([], [], [])

    def body(*refs):
        dy_refs, w_refs = refs[:n_in], refs[n_in:2 * n_in]
        x_ref, dxn_ref, nw_ref, sc_ref = refs[2 * n_in:2 * n_in + 4]
        rest = refs[2 * n_in + 4:]
        if below:
            o_ref, g_ref, dx_ref, dsc_ref, dsh_ref, dw_ref, do_ref, dg_ref = rest
        else:
            dx_ref, dsc_ref, dsh_ref, dw_ref = rest

        @pl.when(pl.program_id(1) == 0)
        def _():
            dsc_ref[...] = jnp.zeros_like(dsc_ref)
            dsh_ref[...] = jnp.zeros_like(dsh_ref)
            if below:
                dg_ref[...] = jnp.zeros_like(dg_ref)

        @pl.when(_first_step())
        def _():
            dw_ref[...] = jnp.zeros_like(dw_ref)

        dh = jnp.dot(dy_refs[0][...], w_refs[0][...], preferred_element_type=F32)
        for k in range(1, n_in):
            dh += jnp.dot(dy_refs[k][...], w_refs[k][...], preferred_element_type=F32)
        xv = x_ref[...]
        r = _rms(xv)
        n = xv * r
        nw = nw_ref[...]
        dsc_ref[...] += jnp.sum(dh * (n * nw), axis=0, keepdims=True)
        dsh_ref[...] += jnp.sum(dh, axis=0, keepdims=True)
        dhn = dh * (1.0 + sc_ref[...])
        dw_ref[...] += jnp.sum(dhn * n, axis=0, keepdims=True)
        dx = dxn_ref[...] + _rms_bwd(dhn * nw, n, r)
        dx_ref[...] = dx
        if below:
            _gate_bwd_tile(dx, below[2], o_ref, g_ref, do_ref, dg_ref)

    in_specs = [_row(tm, dy.shape[2]) for dy in dys] + [_full(wt.shape) for wt in wts]
    in_specs += [_row(tm, d), _row(tm, d), _full((1, d)), _bvec(d)] + extra_in
    return pl.pallas_call(
        body, name=name, grid=(b, s // tm), in_specs=in_specs,
        out_specs=[_row(tm, d), _bvec(d), _bvec(d), _full((1, d))] + extra_out,
        out_shape=[SDS((b, s, d), F32), SDS((b, 1, d), F32), SDS((b, 1, d), F32), SDS((1, d), F32)] + extra_shape,
        compiler_params=_cparams(2))(*dys, *wts, x, dxn, w, sc, *(below[:2] if below else ()))


def final_loss(x, w, tgt, below, name):
    b, s, d = x.shape
    tm = min(512, s)
    extra_in, extra_out, extra_shape = _gate_bwd_specs(tm, d, b, s)

    def body(x_ref, w_ref, t_ref, o_ref, g_ref, loss_ref, dx_ref, dw_ref, do_ref, dg_ref):
        @pl.when(_first_step())
        def _():
            loss_ref[...] = jnp.zeros_like(loss_ref)
            dw_ref[...] = jnp.zeros_like(dw_ref)

        @pl.when(pl.program_id(1) == 0)
        def _():
            dg_ref[...] = jnp.zeros_like(dg_ref)
        xv = x_ref[...]
        r = _rms(xv)
        n = xv * r
        wv = w_ref[...]
        e = n * wv - t_ref[...]
        loss_ref[...] += jnp.sum(e * e) * (0.5 / d)
        dy = e * (1.0 / d)
        dw_ref[...] += jnp.sum(dy * n, axis=0, keepdims=True)
        dx = _rms_bwd(dy * wv, n, r)
        dx_ref[...] = dx
        _gate_bwd_tile(dx, below[2], o_ref, g_ref, do_ref, dg_ref)

    return pl.pallas_call(
        body, name=name, grid=(b, s // tm),
        in_specs=[_row(tm, d), _full((1, d)), _row(tm, d)] + extra_in,
        out_specs=[_full((1, LANES)), _row(tm, d), _full((1, d))] + extra_out,
        out_shape=[SDS((1, LANES), F32), SDS((b, s, d), F32), SDS((1, d), F32)] + extra_shape,
        compiler_params=_cparams(2))(x, w, tgt, *below[:2])


def in_proj(h, win_t, name):
    b, s, d = h.shape
    tm = min(256, s)
    widths = (D_SSD, D_SSD + 2 * SSD_GROUPS * SSD_STATE, Q_LORA, KV_LORA, LANES)

    def body(h_ref, w_ref, *outs):
        p = lax.dot_general(h_ref[...], w_ref[...], NT_DIMS, preferred_element_type=F32)
        off = 0
        for o_ref, wd in zip(outs, widths):
            o_ref[...] = p[:, off:off + wd]
            off += wd

    return pl.pallas_call(
        body, name=name, grid=(b, s // tm),
        in_specs=[_row(tm, d), _full(win_t.shape)],
        out_specs=[_row(tm, wd) for wd in widths],
        out_shape=[SDS((b, s, wd), F32) for wd in widths], compiler_params=_cparams(2))(h, win_t)


def _halo_prev(ts, d):
    return pl.BlockSpec((None, 8, d), lambda b, i: (b, jnp.maximum(i * (ts // 8) - 1, 0), 0))


CONV_ROWS = 32


def _conv_head(head, u_ref, up_ref):
    head[0:8, :] = jnp.where(pl.program_id(1) > 0, up_ref[...], 0.0)
    head[8:8 + CONV_ROWS, :] = u_ref[0:CONV_ROWS, :]


def _conv_windows(u_ref, head, r0):
    if r0 == 0:
        return [head[5 + k:5 + k + CONV_ROWS, :] for k in range(4)]
    return [u_ref[r0 - 3 + k:r0 - 3 + k + CONV_ROWS, :] for k in range(4)]


def _fold8(t):
    acc = t[0:8, :]
    for r in range(8, CONV_ROWS, 8):
        acc += t[r:r + 8, :]
    return acc


def conv_fwd(u, cw, cb, name):
    b, s, dc = u.shape
    ts = min(512, s)
    widths = (D_SSD, SSD_GROUPS * SSD_STATE, SSD_GROUPS * SSD_STATE)

    def body(u_ref, up_ref, w_ref, b_ref, xs_ref, bm_ref, cm_ref, head):
        _conv_head(head, u_ref, up_ref)
        ws = [w_ref[k:k + 1, :] for k in range(4)]
        bias = b_ref[...]
        for r0 in range(0, ts, CONV_ROWS):
            taps = _conv_windows(u_ref, head, r0)
            v = bias + taps[0] * ws[0] + taps[1] * ws[1] + taps[2] * ws[2] + taps[3] * ws[3]
            y = v * _sigmoid(v)
            rs = slice(r0, r0 + CONV_ROWS)
            xs_ref[rs, :] = y[:, 0:D_SSD]
            bm_ref[rs, :] = y[:, D_SSD:D_SSD + 256]
            cm_ref[rs, :] = y[:, D_SSD + 256:D_SSD + 512]

    return pl.pallas_call(
        body, name=name, grid=(b, s // ts),
        in_specs=[_row(ts, dc), _halo_prev(ts, dc), _full((4, dc)), _full((1, dc))],
        out_specs=[_row(ts, wd) for wd in widths],
        out_shape=[SDS((b, s, wd), F32) for wd in widths],
        scratch_shapes=[pltpu.VMEM((8 + CONV_ROWS, dc), F32)], compiler_params=_cparams(2))(u, u, cw, cb)


def conv_bwd_a(dxs, dbm, dcm, u, cw, cb, name):
    b, s, dc = u.shape
    ts = min(512, s)

    def body(dxs_ref, dbm_ref, dcm_ref, u_ref, up_ref, w_ref, b_ref, dv_ref, dwb_ref, head):
        @pl.when(_first_step())
        def _():
            dwb_ref[...] = jnp.zeros_like(dwb_ref)
        _conv_head(head, u_ref, up_ref)
        ws = [w_ref[k:k + 1, :] for k in range(4)]
        bias = b_ref[...]
        for r0 in range(0, ts, CONV_ROWS):
            taps = _conv_windows(u_ref, head, r0)
            v = bias + taps[0] * ws[0] + taps[1] * ws[1] + taps[2] * ws[2] + taps[3] * ws[3]
            sg = _sigmoid(v)
            rs = slice(r0, r0 + CONV_ROWS)
            dy = jnp.concatenate([dxs_ref[rs, :], dbm_ref[rs, :], dcm_ref[rs, :]], axis=1)
            dv = dy * (sg * (1.0 + v * (1.0 - sg)))
            dv_ref[rs, :] = dv
            for k in range(4):
                dwb_ref[8 * k:8 * k + 8, :] += _fold8(dv * taps[k])
            dwb_ref[32:40, :] += _fold8(dv)

    return pl.pallas_call(
        body, name=name, grid=(b, s // ts),
        in_specs=[_row(ts, D_SSD), _row(ts, 256), _row(ts, 256), _row(ts, dc), _halo_prev(ts, dc),
                  _full((4, dc)), _full((1, dc))],
        out_specs=[_row(ts, dc), _full((40, dc))],
        out_shape=[SDS((b, s, dc), F32), SDS((40, dc), F32)],
        scratch_shapes=[pltpu.VMEM((8 + CONV_ROWS, dc), F32)], compiler_params=_cparams(2))(dxs, dbm, dcm, u, u, cw, cb)


def conv_grads_fold(x, name):
    c = x.shape[1]

    def body(x_ref, o_ref):
        o_ref[...] = jnp.zeros_like(o_ref)
        for k in range(5):
            o_ref[k:k + 1, :] = jnp.sum(x_ref[8 * k:8 * k + 8, :], axis=0, keepdims=True)

    return pl.pallas_call(body, name=name, out_shape=SDS((8, c), F32))(x)


def conv_bwd_b(dv, cw, name):
    b, s, dc = dv.shape
    ts = min(512, s)
    nt = s // ts

    def body(dv_ref, dn_ref, w_ref, du_ref, tail):
        tail[0:CONV_ROWS, :] = dv_ref[ts - CONV_ROWS:ts, :]
        tail[CONV_ROWS:CONV_ROWS + 8, :] = jnp.where(pl.program_id(1) < nt - 1, dn_ref[...], 0.0)
        ws = [w_ref[k:k + 1, :] for k in range(4)]
        for r0 in range(0, ts, CONV_ROWS):
            if r0 == ts - CONV_ROWS:
                win = [tail[3 - k:3 - k + CONV_ROWS, :] for k in range(4)]
            else:
                win = [dv_ref[r0 + 3 - k:r0 + 3 - k + CONV_ROWS, :] for k in range(4)]
            acc = win[0] * ws[0] + win[1] * ws[1] + win[2] * ws[2] + win[3] * ws[3]
            du_ref[r0:r0 + CONV_ROWS, :] = acc.astype(BF16)

    nxt = pl.BlockSpec((None, 8, dc), lambda bb, i: (bb, jnp.minimum((i + 1) * (ts // 8), s // 8 - 1), 0))
    return pl.pallas_call(
        body, name=name, grid=(b, nt),
        in_specs=[_row(ts, dc), nxt, _full((4, dc))],
        out_specs=_row(ts, dc), out_shape=SDS((b, s, dc), BF16),
        scratch_shapes=[pltpu.VMEM((CONV_ROWS + 8, dc), F32)], compiler_params=_cparams(2))(dv, dv, cw)


def _ssd_common(misc_ref, dtb_ref, alog_ref, e_ref):
    ln = CHUNK
    lane = lax.broadcasted_iota(I32, (ln, LANES), 1)
    lane1 = lax.broadcasted_iota(I32, (1, LANES), 1)
    pre = misc_ref[...] + dtb_ref[...]
    dt_s = jnp.where(lane < SSD_HEADS, _softplus(pre), 0.0)
    a_neg = jnp.where(lane1 < SSD_HEADS, -jnp.exp(alog_ref[...]), 0.0)
    ri = lax.broadcasted_iota(I32, (ln, ln), 0)
    ci = lax.broadcasted_iota(I32, (ln, ln), 1)
    tril = ci <= ri
    acum = jnp.dot(tril.astype(F32), dt_s * a_neg, preferred_element_type=F32, precision=HI)
    both_e = _dot_01(jnp.concatenate([dt_s, acum], axis=0), e_ref[...], 3)
    dt_e, acum_e = both_e[0:ln], both_e[ln:2 * ln]
    return dict(pre=pre, dt_s=dt_s, a_neg=a_neg, tril=tril, ri=ri, ci=ci, acum=acum, acum_t=acum.T,
                dt_e=dt_e, eac_e=jnp.exp(acum_e), del_e=jnp.exp(acum_e[ln - 1:ln, :] - acum_e))


def _dot_01(x, m01, terms):
    acc, rest = None, x
    for k in range(terms):
        part = rest.astype(BF16)
        if k + 1 < terms:
            rest = rest - part.astype(F32)
        d = jnp.dot(part, m01, preferred_element_type=F32)
        acc = d if acc is None else acc + d
    return acc


def _decay(cm, h):
    seg = cm["acum"][:, h:h + 1] - cm["acum_t"][h:h + 1, :]
    return jnp.exp(jnp.where(cm["tril"], seg, -jnp.inf))


def ssd_fwd(xs, bm, cm_, misc, z, dtb, alog, dskip_e, norm_w, e_mat, name):
    b, s, _ = xs.shape
    ln, nc = CHUNK, s // CHUNK
    gw = D_SSD // SSD_GROUPS
    hpg = SSD_HEADS // SSD_GROUPS

    def body(xs_ref, b_ref, c_ref, misc_ref, z_ref, dtb_ref, alog_ref, dsk_ref, nw_ref, e_ref,
             ys_ref, y_ref, p_ref, st, yd):
        @pl.when(pl.program_id(1) == 0)
        def _():
            st[...] = jnp.zeros_like(st)
        cm = _ssd_common(misc_ref, dtb_ref, alog_ref, e_ref)
        xsv = xs_ref[...]
        xdt = xsv * cm["dt_e"]
        xdt_b = xdt.astype(BF16)
        xd_b = (xdt * cm["del_e"]).astype(BF16)
        gam_e = cm["eac_e"][ln - 1:ln, :]
        p_ref[...] = st[...]
        yoff = []
        for g in range(SSD_GROUPS):
            gs = slice(gw * g, gw * (g + 1))
            bg = b_ref[:, SSD_STATE * g:SSD_STATE * (g + 1)].astype(BF16)
            cg = c_ref[:, SSD_STATE * g:SSD_STATE * (g + 1)].astype(BF16)
            cb = lax.dot_general(cg, bg, NT_DIMS, preferred_element_type=F32)
            st_g = st[:, gs]
            yoff.append(jnp.dot(cg, st_g.astype(BF16), preferred_element_type=F32) * cm["eac_e"][:, gs])
            for j in range(hpg):
                h = hpg * g + j
                hs = slice(SSD_HEAD_DIM * h, SSD_HEAD_DIM * (h + 1))
                m = (cb * _decay(cm, h)).astype(BF16)
                yd[:, hs] = jnp.dot(m, xdt_b[:, hs], preferred_element_type=F32)
            new = lax.dot_general(bg, xd_b[:, gs], TN_DIMS, preferred_element_type=F32)
            st[:, gs] = st_g * gam_e[:, gs] + new
        y = yd[...] + jnp.concatenate(yoff, axis=1) + dsk_ref[...] * xsv
        y_ref[...] = y
        zz = z_ref[...]
        yg = y * (zz * _sigmoid(zz))
        outs = []
        for g in range(SSD_GROUPS):
            ygg = yg[:, gw * g:gw * (g + 1)]
            outs.append(ygg * _rms(ygg) * nw_ref[:, gw * g:gw * (g + 1)])
        ys_ref[...] = jnp.concatenate(outs, axis=1).astype(BF16)

    row = lambda d: pl.BlockSpec((None, ln, d), lambda bb, c: (bb, c, 0))
    return pl.pallas_call(
        body, name=name, grid=(b, nc),
        in_specs=[row(D_SSD), row(256), row(256), row(LANES), row(D_SSD), _full((1, LANES)), _full((1, LANES)),
                  _full((1, D_SSD)), _full((1, D_SSD)), _full((LANES, D_SSD))],
        out_specs=[row(D_SSD), row(D_SSD), pl.BlockSpec((None, None, SSD_STATE, D_SSD), lambda bb, c: (bb, c, 0, 0))],
        out_shape=[SDS((b, s, D_SSD), BF16), SDS((b, s, D_SSD), F32), SDS((b, nc, SSD_STATE, D_SSD), F32)],
        scratch_shapes=[pltpu.VMEM((SSD_STATE, D_SSD), F32), pltpu.VMEM((ln, D_SSD), F32)],
        compiler_params=_cparams(2))(xs, bm, cm_, misc, z, dtb, alog, dskip_e, norm_w, e_mat)


def ssd_bwd(dys, y, z, xs, bm, cm_, misc, prev, dtb, alog, dskip_e, norm_w, e_mat, et_mat, name):
    b, s, _ = xs.shape
    ln, nc = CHUNK, s // CHUNK
    gw = D_SSD // SSD_GROUPS
    hpg = SSD_HEADS // SSD_GROUPS

    def body(dys_ref, y_ref, z_ref, xs_ref, b_ref, c_ref, misc_ref, p_ref, dtb_ref, alog_ref, dsk_ref, nw_ref,
             e_ref, et_ref, dxs_ref, db_ref, dc_ref, dz_ref, ddt_ref, dnw_ref, ddsk_ref, ddtb_ref, dalog_ref,
             dst, dxd, dac_t):
        @pl.when(_first_step())
        def _():
            for r_ in (dnw_ref, ddsk_ref, ddtb_ref, dalog_ref):
                r_[...] = jnp.zeros_like(r_)

        @pl.when(pl.program_id(1) == 0)
        def _():
            dst[...] = jnp.zeros_like(dst)

        cm = _ssd_common(misc_ref, dtb_ref, alog_ref, e_ref)
        et = et_ref[...]
        squeeze = lambda t: _dot_01(t, et, 2)
        lane = lax.broadcasted_iota(I32, (ln, LANES), 1)
        sub = lax.broadcasted_iota(I32, (LANES, ln), 0)
        xsv = xs_ref[...]
        xdt = xsv * cm["dt_e"]
        xdt_b = xdt.astype(BF16)
        xd_b = (xdt * cm["del_e"]).astype(BF16)
        eac_e = cm["eac_e"]
        gam_e = eac_e[ln - 1:ln, :]

        yv, zz, dyo = y_ref[...], z_ref[...], dys_ref[...]
        sz = _sigmoid(zz)
        silu_z = zz * sz
        yg = yv * silu_z
        dyg, dnw = [], []
        for g in range(SSD_GROUPS):
            gs = slice(gw * g, gw * (g + 1))
            ygg = yg[:, gs]
            r = _rms(ygg)
            n = ygg * r
            dnw.append(jnp.sum(dyo[:, gs] * n, axis=0, keepdims=True))
            dyg.append(_rms_bwd(dyo[:, gs] * nw_ref[:, gs], n, r))
        dyg = jnp.concatenate(dyg, axis=1)
        dnw_ref[...] += jnp.concatenate(dnw, axis=1)
        dz_ref[...] = (dyg * yv * (sz * (1.0 + zz * (1.0 - sz)))).astype(BF16)
        dy = dyg * silu_z
        ddsk_ref[...] += jnp.sum(dy * xsv, axis=0, keepdims=True)
        dy_b = dy.astype(BF16)

        dacum = jnp.zeros((ln, LANES), F32)
        dac_t[...] = jnp.zeros_like(dac_t)
        w1, dgam = [], []
        for g in range(SSD_GROUPS):
            gs = slice(gw * g, gw * (g + 1))
            ss = slice(SSD_STATE * g, SSD_STATE * (g + 1))
            bg = b_ref[:, ss].astype(BF16)
            cg = c_ref[:, ss].astype(BF16)
            cb = lax.dot_general(cg, bg, NT_DIMS, preferred_element_type=F32)
            pt = p_ref[:, gs]
            pt_b = pt.astype(BF16)
            dst_g = dst[:, gs]
            dst_b = dst_g.astype(BF16)
            edy = (dy[:, gs] * eac_e[:, gs]).astype(BF16)
            dcg = lax.dot_general(edy, pt_b, NT_DIMS, preferred_element_type=F32)
            dpt = lax.dot_general(cg, edy, TN_DIMS, preferred_element_type=F32)
            yoff = jnp.dot(cg, pt_b, preferred_element_type=F32) * eac_e[:, gs]
            dxd_g = jnp.dot(bg, dst_b, preferred_element_type=F32)
            dbg = lax.dot_general(xd_b[:, gs], dst_b, NT_DIMS, preferred_element_type=F32)
            ddel = dxd_g * xdt[:, gs] * cm["del_e"][:, gs]
            w1.append(dy[:, gs] * yoff - ddel)
            dgam.append(jnp.sum(ddel, axis=0, keepdims=True) + jnp.sum(dst_g * pt, axis=0, keepdims=True) * gam_e[:, gs])
            dxd[:, gs] = dxd_g * cm["del_e"][:, gs]
            dst[:, gs] = dst_g * gam_e[:, gs] + dpt
            dcb = jnp.zeros((ln, ln), F32)
            for j in range(hpg):
                h = hpg * g + j
                hs = slice(SSD_HEAD_DIM * h, SSD_HEAD_DIM * (h + 1))
                lam = _decay(cm, h)
                m = cb * lam
                dm = lax.dot_general(dy_b[:, hs], xdt_b[:, hs], NT_DIMS, preferred_element_type=F32)
                dxd[:, hs] += lax.dot_general(m.astype(BF16), dy_b[:, hs], TN_DIMS, preferred_element_type=F32)
                dcb += dm * lam
                wl = dm * m
                dacum += jnp.where(lane == h, jnp.sum(wl, axis=1, keepdims=True), 0.0)
                dac_t[...] -= jnp.where(sub == h, jnp.sum(wl, axis=0, keepdims=True), 0.0)
            dcb_b = dcb.astype(BF16)
            dc_ref[:, ss] = dcg + jnp.dot(dcb_b, bg, preferred_element_type=F32)
            db_ref[:, ss] = dbg + lax.dot_general(dcb_b, cg, TN_DIMS, preferred_element_type=F32)

        dxdt = dxd[...]
        dxs_ref[...] = dy * dsk_ref[...] + dxdt * cm["dt_e"]
        dacum += squeeze(jnp.concatenate(w1, axis=1)) + dac_t[...].T
        dlast = squeeze(jnp.broadcast_to(jnp.concatenate(dgam, axis=1), (8, D_SSD)))[0:1, :]
        dacum += jnp.where(lax.broadcasted_iota(I32, (ln, LANES), 0) == ln - 1, dlast, 0.0)
        triu = (cm["ci"] >= cm["ri"]).astype(F32)
        da = jnp.dot(triu, dacum, preferred_element_type=F32, precision=HI)
        ddt = da * cm["a_neg"] + squeeze(dxdt * xsv)
        dalog_ref[...] += jnp.sum(da * cm["dt_s"], axis=0, keepdims=True) * cm["a_neg"]
        ddt_raw = jnp.where(lane < SSD_HEADS, ddt * _sigmoid(cm["pre"]), 0.0)
        ddt_ref[...] = ddt_raw
        ddtb_ref[...] += jnp.sum(ddt_raw, axis=0, keepdims=True)

    row = lambda d: pl.BlockSpec((None, ln, d), lambda bb, c: (bb, nc - 1 - c, 0))
    return pl.pallas_call(
        body, name=name, grid=(b, nc),
        in_specs=[row(D_SSD), row(D_SSD), row(D_SSD), row(D_SSD), row(256), row(256), row(LANES),
                  pl.BlockSpec((None, None, SSD_STATE, D_SSD), lambda bb, c: (bb, nc - 1 - c, 0, 0)),
                  _full((1, LANES)), _full((1, LANES)), _full((1, D_SSD)), _full((1, D_SSD)),
                  _full((LANES, D_SSD)), _full((D_SSD, LANES))],
        out_specs=[row(D_SSD), row(256), row(256), row(D_SSD), row(LANES),
                   _full((1, D_SSD)), _full((1, D_SSD)), _full((1, LANES)), _full((1, LANES))],
        out_shape=[SDS((b, s, D_SSD), F32), SDS((b, s, 256), F32), SDS((b, s, 256), F32), SDS((b, s, D_SSD), BF16),
                   SDS((b, s, LANES), F32), SDS((1, D_SSD), F32), SDS((1, D_SSD), F32), SDS((1, LANES), F32),
                   SDS((1, LANES), F32)],
        scratch_shapes=[pltpu.VMEM((SSD_STATE, D_SSD), F32), pltpu.VMEM((ln, D_SSD), F32), pltpu.VMEM((LANES, ln), F32)],
        compiler_params=_cparams(2))(dys, y, z, xs, bm, cm_, misc, prev, dtb, alog, dskip_e, norm_w, e_mat, et_mat)


def _rope(xv, cc, sp, sm):
    n = xv.shape[1]
    return xv * cc + pltpu.roll(xv, 16, 1) * sp + pltpu.roll(xv, n - 16, 1) * sm


def _rope_bwd(dy, cc, sp, sm):
    n = dy.shape[1]
    return dy * cc + pltpu.roll(dy * sp, n - 16, 1) + pltpu.roll(dy * sm, 16, 1)


def _tile8(t):
    return jnp.concatenate([t] * MLA_HEADS, axis=1)


def qkv_fwd(cq, ckv, misc, cc, sp, sm, qnw, kvnw, wuq_t, wukv_t, place, name):
    b, s, _ = cq.shape
    tm = min(256, s)
    hd = MLA_HEADS * HEAD_PAD

    def body(cq_ref, ckv_ref, misc_ref, cc_ref, sp_ref, sm_ref, qnw_ref, kvnw_ref, wq_ref, wkv_ref, pl_ref,
             q_ref, k_ref, v_ref, qn_ref, kvn_ref):
        cqv, ckvv = cq_ref[...], ckv_ref[...]
        qn = (cqv * _rms(cqv) * qnw_ref[...]).astype(BF16)
        kvn = (ckvv * _rms(ckvv) * kvnw_ref[...]).astype(BF16)
        qn_ref[...] = qn
        kvn_ref[...] = kvn
        cc1, sp1, sm1 = cc_ref[...], sp_ref[...], sm_ref[...]
        q = lax.dot_general(qn, wq_ref[...], NT_DIMS, preferred_element_type=F32)
        q_ref[...] = _rope(q, _tile8(cc1), _tile8(sp1), _tile8(sm1)).astype(BF16)
        kv = lax.dot_general(kvn, wkv_ref[...], NT_DIMS, preferred_element_type=F32)
        kr = jnp.dot(misc_ref[...], pl_ref[...], preferred_element_type=F32, precision=HI)
        kr = _rope(kr, cc1, sp1, sm1)
        k_ref[...] = (kv[:, 0:hd] + _tile8(kr)).astype(BF16)
        v_ref[...] = kv[:, hd:2 * hd].astype(BF16)

    return pl.pallas_call(
        body, name=name, grid=(b, s // tm),
        in_specs=[_row(tm, Q_LORA), _row(tm, KV_LORA), _row(tm, LANES), _row(tm, LANES), _row(tm, LANES), _row(tm, LANES),
                  _full((1, Q_LORA)), _full((1, KV_LORA)), _full(wuq_t.shape), _full(wukv_t.shape), _full((LANES, LANES))],
        out_specs=[_row(tm, hd), _row(tm, hd), _row(tm, hd), _row(tm, Q_LORA), _row(tm, KV_LORA)],
        out_shape=[SDS((b, s, hd), BF16)] * 3 + [SDS((b, s, Q_LORA), BF16), SDS((b, s, KV_LORA), BF16)],
        compiler_params=_cparams(2))(cq, ckv, misc, cc, sp, sm, qnw, kvnw, wuq_t, wukv_t, place)


def qkv_bwd(dq, dk, dv, ddt, cq, ckv, cc, sp, sm, qnw, kvnw, wuq_t, wukv_t, place_t, name):
    b, s, _ = cq.shape
    tm = min(256, s)
    hd = MLA_HEADS * HEAD_PAD

    def body(dq_ref, dk_ref, dv_ref, ddt_ref, cq_ref, ckv_ref, cc_ref, sp_ref, sm_ref, qnw_ref, kvnw_ref,
             wq_ref, wkv_ref, plt_ref, dcq_ref, dckv_ref, dmisc_ref, dqp_ref, dkv_ref, dqnw_ref, dkvnw_ref):
        @pl.when(_first_step())
        def _():
            dqnw_ref[...] = jnp.zeros_like(dqnw_ref)
            dkvnw_ref[...] = jnp.zeros_like(dkvnw_ref)
        cc1, sp1, sm1 = cc_ref[...], sp_ref[...], sm_ref[...]
        dqp = _rope_bwd(dq_ref[...], _tile8(cc1), _tile8(sp1), _tile8(sm1)).astype(BF16)
        dqp_ref[...] = dqp
        dkf = dk_ref[...]
        dkv_b = jnp.concatenate([dkf, dv_ref[...]], axis=1).astype(BF16)
        dkv_ref[...] = dkv_b
        dkr = dkf[:, 0:HEAD_PAD]
        for h in range(1, MLA_HEADS):
            dkr += dkf[:, HEAD_PAD * h:HEAD_PAD * (h + 1)]
        dkr = _rope_bwd(dkr, cc1, sp1, sm1)
        dmisc_ref[...] = (jnp.dot(dkr, plt_ref[...], preferred_element_type=F32, precision=HI) + ddt_ref[...]).astype(BF16)

        def norm_bwd(dn_w, xv, w_ref, dw_ref, dx_ref):
            r = _rms(xv)
            n = xv * r
            dw_ref[...] += jnp.sum(dn_w * n, axis=0, keepdims=True)
            dx_ref[...] = _rms_bwd(dn_w * w_ref[...], n, r).astype(BF16)

        norm_bwd(jnp.dot(dqp, wq_ref[...], preferred_element_type=F32), cq_ref[...], qnw_ref, dqnw_ref, dcq_ref)
        norm_bwd(jnp.dot(dkv_b, wkv_ref[...], preferred_element_type=F32), ckv_ref[...], kvnw_ref, dkvnw_ref, dckv_ref)

    return pl.pallas_call(
        body, name=name, grid=(b, s // tm),
        in_specs=[_row(tm, hd), _row(tm, hd), _row(tm, hd), _row(tm, LANES), _row(tm, Q_LORA), _row(tm, KV_LORA),
                  _row(tm, LANES), _row(tm, LANES), _row(tm, LANES), _full((1, Q_LORA)), _full((1, KV_LORA)),
                  _full(wuq_t.shape), _full(wukv_t.shape), _full((LANES, LANES))],
        out_specs=[_row(tm, Q_LORA), _row(tm, KV_LORA), _row(tm, LANES), _row(tm, hd), _row(tm, 2 * hd),
                   _full((1, Q_LORA)), _full((1, KV_LORA))],
        out_shape=[SDS((b, s, Q_LORA), BF16), SDS((b, s, KV_LORA), BF16), SDS((b, s, LANES), BF16),
                   SDS((b, s, hd), BF16), SDS((b, s, 2 * hd), BF16), SDS((1, Q_LORA), F32), SDS((1, KV_LORA), F32)],
        compiler_params=_cparams(2))(dq, dk, dv, ddt, cq, ckv, cc, sp, sm, qnw, kvnw, wuq_t, wukv_t, place_t)


ATT_SCALE = 1.0 / math.sqrt(QK_DIM)
LOG2E = math.log2(math.e)
ATT_SCALE_LOG2E = ATT_SCALE * LOG2E


ATT_HEADS_PER_STEP = 2


def _att_tile(s):
    return min(512, s)


def flash_fwd(q, k, v, name):
    b, s, hd = q.shape
    t = _att_tile(s)
    nb = s // t
    th = t // 2
    vt = v.reshape(b, nb, t, MLA_HEADS, HEAD_PAD).transpose(0, 3, 1, 4, 2)

    hps = ATT_HEADS_PER_STEP
    hw = hps * HEAD_PAD

    def body(q_ref, k_ref, vt_ref, o_ref, lse_ref, m_s, l_s, acc):
        i = pl.program_id(2)
        m_s[...] = jnp.full_like(m_s, -jnp.inf)
        l_s[...] = jnp.zeros_like(l_s)
        acc[...] = jnp.zeros_like(acc)

        def update(j, diagonal):
            ks = pl.ds(pl.multiple_of(j * t, t), t)
            chains = [(hh, half) for hh in range(hps) for half in range(2)]
            lanes = lambda hh: slice(HEAD_PAD * hh, HEAD_PAD * (hh + 1))
            cols = lambda half: slice(th * half, th * (half + 1))
            sts = {}
            for hh, half in chains:
                st = lax.dot_general(k_ref[ks, lanes(hh)], q_ref[cols(half), lanes(hh)], NT_DIMS,
                                     preferred_element_type=F32)
                if diagonal:
                    row = lax.broadcasted_iota(I32, (t, th), 0)
                    col = lax.broadcasted_iota(I32, (t, th), 1) + th * half
                    st = jnp.where(row <= col, st, -jnp.inf)
                sts[hh, half] = st
            pts, alphas = {}, {}
            for hh, half in chains:
                st, cs = sts[hh, half], cols(half)
                m_prev = m_s[hh, :, cs]
                m_new = jnp.maximum(m_prev, jnp.max(st, axis=0, keepdims=True))
                alpha = jnp.exp2((m_prev - m_new) * ATT_SCALE_LOG2E)
                pt = jnp.exp2((st - m_new) * ATT_SCALE_LOG2E)
                l_s[hh, :, cs] = alpha * l_s[hh, :, cs] + jnp.sum(pt, axis=0, keepdims=True)
                m_s[hh, :, cs] = m_new
                pts[hh, half], alphas[hh, half] = pt.astype(BF16), alpha
            for hh, half in chains:
                cs = cols(half)
                acc[hh, :, cs] = alphas[hh, half] * acc[hh, :, cs] + jnp.dot(vt_ref[hh, j], pts[hh, half],
                                                                             preferred_element_type=F32)

        def step(j, carry):
            update(j, False)
            return carry

        lax.fori_loop(0, i, step, 0)
        update(i, True)
        for hh in range(hps):
            o_ref[:, HEAD_PAD * hh:HEAD_PAD * (hh + 1)] = (acc[hh] / l_s[hh]).T
            lse_ref[hh] = m_s[hh] * ATT_SCALE + jnp.log(l_s[hh])

    qs = pl.BlockSpec((None, t, hw), lambda bb, h, i: (bb, i, h))
    ks = pl.BlockSpec((None, s, hw), lambda bb, h, i: (bb, 0, h))
    vs = pl.BlockSpec((None, hps, nb, HEAD_PAD, t), lambda bb, h, i: (bb, h, 0, 0, 0))
    ls = pl.BlockSpec((None, hps, None, 1, t), lambda bb, h, i: (bb, h, i, 0, 0))
    return pl.pallas_call(
        body, name=name, grid=(b, MLA_HEADS // hps, nb),
        in_specs=[qs, ks, vs], out_specs=[qs, ls],
        out_shape=[SDS((b, s, hd), F32), SDS((b, MLA_HEADS, nb, 1, t), F32)],
        scratch_shapes=[pltpu.VMEM((hps, 1, t), F32), pltpu.VMEM((hps, 1, t), F32), pltpu.VMEM((hps, HEAD_PAD, t), F32)],
        compiler_params=_cparams(3))(q, k, vt)


def flash_bwd(q, k, v, do, lse, dlt, name):
    b, s, hd = q.shape
    t = _att_tile(s)
    nb = s // t
    th = t // 2
    lse_r = lse
    dlt_r = dlt.reshape(b, MLA_HEADS, nb, 1, t)

    hps = ATT_HEADS_PER_STEP
    hw = hps * HEAD_PAD

    def body(q_ref, k_ref, v_ref, do_ref, lse_ref, dlt_ref, dq_ref, dk_ref, dv_ref):
        dq_ref[...] = jnp.zeros_like(dq_ref)
        dk_ref[...] = jnp.zeros_like(dk_ref)
        dv_ref[...] = jnp.zeros_like(dv_ref)

        def tile(j, i, diagonal):
            qs = pl.ds(pl.multiple_of(i * t, t), t)
            chains = [(hh, half) for hh in range(hps) for half in range(2)]
            lanes = lambda hh: slice(HEAD_PAD * hh, HEAD_PAD * (hh + 1))
            keys = lambda half: pl.ds(pl.multiple_of(j * t + th * half, th), th)
            sts, dpts = {}, {}
            for hh, half in chains:
                ls_, ks = lanes(hh), keys(half)
                st = lax.dot_general(k_ref[ks, ls_], q_ref[qs, ls_], NT_DIMS, preferred_element_type=F32)
                if diagonal:
                    row = lax.broadcasted_iota(I32, (th, t), 0) + th * half
                    col = lax.broadcasted_iota(I32, (th, t), 1)
                    st = jnp.where(row <= col, st, -jnp.inf)
                sts[hh, half] = st
                dpts[hh, half] = lax.dot_general(v_ref[ks, ls_], do_ref[qs, ls_], NT_DIMS, preferred_element_type=F32)
            pts, dsts = {}, {}
            for hh, half in chains:
                pt = jnp.exp2(sts[hh, half] * ATT_SCALE_LOG2E - lse_ref[hh, i] * LOG2E)
                pts[hh, half] = pt.astype(BF16)
                dsts[hh, half] = (pt * (dpts[hh, half] - dlt_ref[hh, i])).astype(BF16)
            for hh in range(hps):
                ls_ = lanes(hh)
                dq_acc = None
                for half in range(2):
                    ks = keys(half)
                    dv_ref[ks, ls_] += jnp.dot(pts[hh, half], do_ref[qs, ls_], preferred_element_type=F32)
                    dk_ref[ks, ls_] += jnp.dot(dsts[hh, half], q_ref[qs, ls_], preferred_element_type=F32)
                    part = lax.dot_general(dsts[hh, half], k_ref[ks, ls_], TN_DIMS, preferred_element_type=F32)
                    dq_acc = part if dq_acc is None else dq_acc + part
                dq_ref[qs, ls_] += dq_acc

        def key_tile(j, carry):
            tile(j, j, True)

            def query_tile(i, c2):
                tile(j, i, False)
                return c2

            lax.fori_loop(j + 1, nb, query_tile, 0)
            return carry

        lax.fori_loop(0, nb, key_tile, 0)
        dq_ref[...] *= ATT_SCALE
        dk_ref[...] *= ATT_SCALE

    hs = pl.BlockSpec((None, s, hw), lambda bb, h: (bb, 0, h))
    ls = pl.BlockSpec((None, hps, nb, 1, t), lambda bb, h: (bb, h, 0, 0, 0))
    return pl.pallas_call(
        body, name=name, grid=(b, MLA_HEADS // hps),
        in_specs=[hs, hs, hs, hs, ls, ls], out_specs=[hs, hs, hs],
        out_shape=[SDS((b, s, hd), F32)] * 3, compiler_params=_cparams(2))(q, k, v, do, lse_r, dlt_r)


def out_proj(ys, attn, mnw, wo, x, gate, name):
    b, s, d = x.shape
    tm = min(256, s)

    def body(ys_ref, at_ref, mnw_ref, wo_ref, x_ref, g_ref, xn_ref, o_ref, ym_ref):
        av = at_ref[...]
        ym = (av * _rms(av) * mnw_ref[...]).astype(BF16)
        ym_ref[...] = ym
        o = jnp.dot(ys_ref[...], wo_ref[0:D_SSD, :], preferred_element_type=F32)
        o += jnp.dot(ym, wo_ref[D_SSD:2 * D_SSD, :], preferred_element_type=F32)
        xn_ref[...] = x_ref[...] + g_ref[...] * o
        o_ref[...] = o.astype(BF16)

    return pl.pallas_call(
        body, name=name, grid=(b, s // tm),
        in_specs=[_row(tm, D_SSD), _row(tm, D_SSD), _full((1, D_SSD)), _full(wo.shape), _row(tm, d), _bvec(d)],
        out_specs=[_row(tm, d), _row(tm, d), _row(tm, D_SSD)],
        out_shape=[SDS((b, s, d), F32), SDS((b, s, d), BF16), SDS((b, s, D_SSD), BF16)],
        compiler_params=_cparams(2))(ys, attn, mnw, wo, x, gate)


def out_proj_bwd(dout, attn, mnw, wo, name):
    b, s, d = dout.shape
    tm = min(256, s)

    def body(do_ref, at_ref, mnw_ref, wo_ref, dys_ref, dat_ref, dlt_ref, dw_ref):
        @pl.when(_first_step())
        def _():
            dw_ref[...] = jnp.zeros_like(dw_ref)
        dov = do_ref[...]
        dys_ref[...] = lax.dot_general(dov, wo_ref[0:D_SSD, :], NT_DIMS, preferred_element_type=F32)
        dym = lax.dot_general(dov, wo_ref[D_SSD:2 * D_SSD, :], NT_DIMS, preferred_element_type=F32)
        av = at_ref[...]
        r = _rms(av)
        n = av * r
        dw_ref[...] += jnp.sum(dym * n, axis=0, keepdims=True)
        dat = _rms_bwd(dym * mnw_ref[...], n, r)
        dat_ref[...] = dat.astype(BF16)
        prod = dat * av
        for h in range(MLA_HEADS):
            dlt_ref[h] = jnp.sum(prod[:, HEAD_PAD * h:HEAD_PAD * (h + 1)], axis=1, keepdims=True)

    return pl.pallas_call(
        body, name=name, grid=(b, s // tm),
        in_specs=[_row(tm, d), _row(tm, D_SSD), _full((1, D_SSD)), _full(wo.shape)],
        out_specs=[_row(tm, D_SSD), _row(tm, D_SSD),
                   pl.BlockSpec((None, MLA_HEADS, tm, 1), lambda bb, i: (bb, 0, i, 0)), _full((1, D_SSD))],
        out_shape=[SDS((b, s, D_SSD), F32), SDS((b, s, D_SSD), BF16), SDS((b, MLA_HEADS, s, 1), F32),
                   SDS((1, D_SSD), F32)],
        compiler_params=_cparams(2))(dout, attn, mnw, wo)


def adaln_fwd(c_all, w_ada, b_ada, name):
    nb, d = c_all.shape
    n = w_ada.shape[1]

    def body(c_ref, w_ref, b_ref, m_ref, ca_ref):
        cv = c_ref[...]
        ca = (cv * _sigmoid(cv)).astype(BF16)
        ca_ref[...] = ca
        m_ref[...] = jnp.dot(ca, w_ref[...].astype(BF16), preferred_element_type=F32) + b_ref[...]

    return pl.pallas_call(
        body, name=name, out_shape=[SDS((nb, n), F32), SDS((nb, d), BF16)],
        compiler_params=pltpu.CompilerParams(vmem_limit_bytes=VMEM_LIMIT))(c_all, w_ada, b_ada)


def adaln_bwd(c_act, dmod_cols, name):
    d, n = c_act.shape[1], dmod_cols.shape[1]

    def body(c_ref, dm_ref, gw_ref):
        gw_ref[...] = lax.dot_general(c_ref[...], dm_ref[...].astype(BF16), TN_DIMS, preferred_element_type=F32)

    return pl.pallas_call(
        body, name=name, out_shape=SDS((d, n), F32),
        compiler_params=pltpu.CompilerParams(vmem_limit_bytes=VMEM_LIMIT))(c_act, dmod_cols)


def sum_rows(x, name):
    def body(x_ref, o_ref):
        o_ref[...] = jnp.sum(x_ref[...], axis=0, keepdims=True)
    return pl.pallas_call(body, name=name, out_shape=SDS((1, x.shape[1]), F32))(x)


def squeeze_heads(x, et_mat, name):
    def body(x_ref, et_ref, o_ref):
        xv = jnp.broadcast_to(x_ref[...], (8, x.shape[1]))
        o_ref[...] = _dot_01(xv, et_ref[...], 3)[0:1, :]
    return pl.pallas_call(body, name=name, out_shape=SDS((1, LANES), F32))(x, et_mat)


def sum_blocks(x, name):
    n, r, c = x.shape

    def body(x_ref, o_ref):
        acc = x_ref[0].astype(F32)
        for k in range(1, n):
            acc += x_ref[k].astype(F32)
        o_ref[...] = acc

    return pl.pallas_call(body, name=name, out_shape=SDS((r, c), F32),
                          compiler_params=pltpu.CompilerParams(vmem_limit_bytes=VMEM_LIMIT))(x)


def _adam_math(w, g, m, v):
    m = ADAM_B1 * m + (1.0 - ADAM_B1) * g
    v = ADAM_B2 * v + (1.0 - ADAM_B2) * (g * g)
    m_hat = m / (1.0 - ADAM_B1 ** ADAM_STEP)
    v_hat = v / (1.0 - ADAM_B2 ** ADAM_STEP)
    return -ADAM_LR * (m_hat / (jnp.sqrt(v_hat) + ADAM_EPS) + ADAM_WD * w), m, v


def adamw(w, g, m, v, name):
    r, c = w.shape
    tr = r
    for cand in (512, 256, 128, 64, 32, 16, 8):
        if r % cand == 0 and cand * c * 4 <= 2 * 1024 * 1024:
            tr = cand
            break

    def body(w_ref, g_ref, m_ref, v_ref, d_ref, mo_ref, vo_ref):
        d_ref[...], mo_ref[...], vo_ref[...] = _adam_math(w_ref[...], g_ref[...], m_ref[...], v_ref[...])

    spec = pl.BlockSpec((tr, c), lambda i: (i, 0))
    return pl.pallas_call(
        body, name=name, grid=(r // tr,), in_specs=[spec] * 4, out_specs=[spec] * 3,
        out_shape=[SDS((r, c), F32)] * 3, compiler_params=_cparams(1))(w, g, m, v)


def adamw_sum8(w, gparts, m, v, name):
    r, c = w.shape
    tr = next(cand for cand in (64, 32, 16, 8) if r % cand == 0)

    def body(w_ref, gp_ref, m_ref, v_ref, g_ref, d_ref, mo_ref, vo_ref):
        g = gp_ref[0].astype(F32)
        for k in range(1, N_DEV):
            g += gp_ref[k].astype(F32)
        g_ref[...] = g
        d_ref[...], mo_ref[...], vo_ref[...] = _adam_math(w_ref[...], g, m_ref[...], v_ref[...])

    spec = pl.BlockSpec((tr, c), lambda i: (i, 0))
    gspec = pl.BlockSpec((N_DEV, tr, c), lambda i: (0, i, 0))
    return pl.pallas_call(
        body, name=name, grid=(r // tr,), in_specs=[spec, gspec, spec, spec], out_specs=[spec] * 4,
        out_shape=[SDS((r, c), F32)] * 4, compiler_params=_cparams(1))(w, gparts, m, v)


PACK = {"ffn1_w_gate": (352, 352), "ffn1_w_up": (352, 352), "ffn1_w_down": (352, 352),
        "ffn2_w_gate": (352, 352), "ffn2_w_up": (352, 352), "ffn2_w_down": (352, 352),
        "w_out": (256, 256), "w_in": (406, 416), "w_ukv": (48, 48), "w_uq": (36, 48)}
TRANSPOSED = ("ffn1_w_gate", "ffn1_w_up", "ffn2_w_gate", "ffn2_w_up", "w_in", "w_ukv", "w_uq")
GATHER_GROUPS = (("ffn1_w_gate", "ffn1_w_up"), ("ffn1_w_down",),
                 ("w_in", "w_ukv", "w_uq", "w_out", "ffn2_w_gate", "ffn2_w_up", "ffn2_w_down"))
GRAD_GROUPS = (("ffn2", ("ffn2_w_gate", "ffn2_w_up", "ffn2_w_down")), ("mixer", ("w_out", "w_in", "w_ukv", "w_uq")),
               ("ffn1_up", ("ffn1_w_gate", "ffn1_w_up")), ("ffn1_down", ("ffn1_w_down",)))


def _pack_offsets(names):
    off, o = {}, 0
    for n in names:
        off[n] = (o, PACK[n][0])
        o += PACK[n][1]
    return off, o


def _shard_to_rows(name, w):
    w = w[0]
    if name in TRANSPOSED:
        w = w.T
    return w.reshape(-1, D_MODEL)


def _rows_to_shard(name, rows, like):
    shp = like.shape[1:]
    if name in TRANSPOSED:
        return rows.reshape(shp[1], shp[0]).T[None]
    return rows.reshape(shp)[None]


def _pack_shards(ws, names, dtype):
    parts = []
    for name in names:
        real, padded = PACK[name]
        rows = _shard_to_rows(name, ws[name]).astype(dtype)
        if padded > real:
            rows = jnp.pad(rows, ((0, padded - real), (0, 0)))
        parts.append(rows)
    return jnp.concatenate(parts, axis=0)


def _grad_rows(name, gw):
    real, padded = PACK[name]
    if name == "w_in":
        rows = _in_proj_rows_inv(gw).reshape(N_DEV, -1, D_MODEL)
    elif name == "w_ukv":
        hd = MLA_HEADS * HEAD_PAD
        rows = jnp.concatenate([gw[:hd].reshape(MLA_HEADS, HEAD_PAD, KV_LORA)[:, :QK_NOPE],
                                gw[hd:].reshape(MLA_HEADS, V_HEAD, KV_LORA)], axis=1).reshape(N_DEV, -1, D_MODEL)
    elif name == "w_uq":
        rows = gw.reshape(MLA_HEADS, HEAD_PAD, Q_LORA)[:, :QK_DIM].reshape(N_DEV, -1, D_MODEL)
    else:
        rows = gw.reshape(N_DEV, -1, D_MODEL)
    if padded > real:
        rows = jnp.pad(rows, ((0, 0), (0, padded - real), (0, 0)))
    return rows


def _pack_rows(arrs):
    parts = []
    for a in arrs:
        flat = a.reshape(-1).astype(F32)
        pad = (-flat.shape[0]) % D_MODEL
        if pad:
            flat = jnp.pad(flat, (0, pad))
        parts.append(flat.reshape(-1, D_MODEL))
    out = jnp.concatenate(parts, axis=0)
    pad = (-out.shape[0]) % 8
    if pad:
        out = jnp.pad(out, ((0, pad), (0, 0)))
    return out


def _unpack_rows(packed, shapes):
    out, row = [], 0
    for shp in shapes:
        n = math.prod(shp)
        nrow = -(-n // D_MODEL)
        out.append(packed[row:row + nrow].reshape(-1)[:n].reshape(shp))
        row += nrow
    return out


def _in_proj_rows(w_t):
    return jnp.concatenate([w_t[0:2560], w_t[2576:2960], w_t[2960:3216], w_t[2560:2576], w_t[3216:3248],
                            jnp.zeros((D_IN_PAD - D_IN, D_MODEL), w_t.dtype)], axis=0)


def _in_proj_rows_inv(d):
    return jnp.concatenate([d[0:2560], d[3200:3216], d[2560:2944], d[2944:3200], d[3216:3248]], axis=0)


def _rope_tables(positions):
    inv_freq = ROPE_THETA ** (-jnp.arange(0, QK_ROPE, 2, dtype=F32) / QK_ROPE)
    ang = positions[..., None].astype(F32) * inv_freq
    cos, sin = jnp.cos(ang), jnp.sin(ang)
    one = jnp.ones(ang.shape[:2] + (QK_NOPE,), F32)
    zero = jnp.zeros_like(one)
    z16, z32, o32 = zero[..., :16], zero[..., :32], one[..., :32]
    cc = jnp.concatenate([one, cos, cos, o32], axis=-1)
    sp = jnp.concatenate([zero, z16, sin, z32], axis=-1)
    sm = jnp.concatenate([zero, -sin, z16, z32], axis=-1)
    return cc, sp, sm


def weight_views(gathered):
    def _seg(name):
        names, g = next((names, g) for names, g in zip(GATHER_GROUPS, gathered) if name in names)
        o, r = _pack_offsets(names)[0][name]
        return g[:, o:o + r]

    full = lambda name: _seg(name).reshape(-1, D_MODEL)
    ukv = _seg("w_ukv").reshape(MLA_HEADS, QK_NOPE + V_HEAD, KV_LORA)
    wukv_t = jnp.concatenate([jnp.pad(ukv[:, :QK_NOPE], ((0, 0), (0, HEAD_PAD - QK_NOPE), (0, 0))).reshape(-1, KV_LORA),
                              ukv[:, QK_NOPE:].reshape(-1, KV_LORA)], axis=0)
    uq = _seg("w_uq").reshape(MLA_HEADS, QK_DIM, Q_LORA)
    wuq_t = jnp.pad(uq, ((0, 0), (0, HEAD_PAD - QK_DIM), (0, 0))).reshape(-1, Q_LORA)
    return dict(wg1_t=full("ffn1_w_gate"), wu1_t=full("ffn1_w_up"), wd1=full("ffn1_w_down"),
                wg2_t=full("ffn2_w_gate"), wu2_t=full("ffn2_w_up"), wd2=full("ffn2_w_down"),
                wo=full("w_out"), win_t=_in_proj_rows(full("w_in")), wukv_t=wukv_t, wuq_t=wuq_t)


def _ffn_bwd(tag, dxn, do, dgate, x, h, gg, uu, a, sc, norm_w, wg_t, wu_t, wd, below):
    f2 = wd.shape[0] // 2
    dgg, duu = ffn_dact(do, wd, gg, uu, tag + "_dact")
    dwg_t = mm_tn(dgg, h, f2, D_MODEL, tag + "_dwg")
    dwu_t = mm_tn(duu, h, f2, D_MODEL, tag + "_dwu")
    dwd = mm_tn(a, do, f2, D_MODEL, tag + "_dwd")
    dx, dsc, dsh, dnw, *nxt = dh_norm_bwd([dgg, duu], [wg_t, wu_t], x, dxn, norm_w, sc, tag + "_dh", below)
    return dx, (dsh, dsc, dgate), dnw, (dwg_t, dwu_t, dwd), nxt


def local_step(x, tgt, positions, mod, wv, p):
    nb, s, d = x.shape
    sh1, sc1, g1, sh2, sc2, g2, sh3, sc3, g3 = mod
    cc, sp, sm = _rope_tables(positions)
    lane_head = jnp.arange(D_SSD, dtype=I32)[None, :] // SSD_HEAD_DIM
    e_mat = (lane_head == jnp.arange(LANES, dtype=I32)[:, None]).astype(BF16)
    et_mat = e_mat.T
    rr, cl = jnp.arange(LANES, dtype=I32)[:, None], jnp.arange(LANES, dtype=I32)[None, :]
    place = ((cl == rr + (QK_NOPE - SSD_HEADS)) & (rr >= SSD_HEADS) & (rr < SSD_HEADS + QK_ROPE)).astype(F32)
    dtb = jnp.pad(p["dt_bias"], ((0, 0), (0, LANES - SSD_HEADS)))
    alog = jnp.pad(p["a_log"], ((0, 0), (0, LANES - SSD_HEADS)))
    dskip_e = jnp.repeat(p["d_skip"], SSD_HEAD_DIM, axis=1)

    h1 = norm_mod(x, p["norm_ffn1"], sc1, sh1, "ffn1_norm")
    gg1, uu1, a1 = ffn_up(h1, wv["wg1_t"], wv["wu1_t"], "ffn1_up")
    x1, o1 = ffn_down(a1, wv["wd1"], x, g1, 0.5, "ffn1_down")
    h2 = norm_mod(x1, p["norm_mix"], sc2, sh2, "mix_norm")
    z, u, cq, ckv, misc = in_proj(h2, wv["win_t"], "in_proj")
    xs, bm, cm_ = conv_fwd(u, p["conv_w"], p["conv_b"], "conv_fwd")
    ys, y, prev = ssd_fwd(xs, bm, cm_, misc, z, dtb, alog, dskip_e, p["ssd_norm_w"], e_mat, "ssd_fwd")
    q, k, v, qn, kvn = qkv_fwd(cq, ckv, misc, cc, sp, sm, p["q_norm_w"], p["kv_norm_w"], wv["wuq_t"], wv["wukv_t"],
                               place, "qkv_fwd")
    attn, lse = flash_fwd(q, k, v, "flash_fwd")
    x2, o2, ym = out_proj(ys, attn, p["mla_norm_w"], wv["wo"], x1, g2, "out_proj")
    h3 = norm_mod(x2, p["norm_ffn2"], sc3, sh3, "ffn2_norm")
    gg3, uu3, a3 = ffn_up(h3, wv["wg2_t"], wv["wu2_t"], "ffn2_up")
    x3, o3 = ffn_down(a3, wv["wd2"], x2, g3, 0.5, "ffn2_down")
    loss, dx3, dnfin, do3, dg3 = final_loss(x3, p["norm_final"], tgt, (o3, g3, 0.5), "final_loss")

    dx2, dmod3, dnf2, (dwg2, dwu2, dwd2), (dout, dg2) = _ffn_bwd(
        "ffn2", dx3, do3, dg3, x2, h3, gg3, uu3, a3, sc3, p["norm_ffn2"], wv["wg2_t"], wv["wu2_t"], wv["wd2"],
        (o2, g2, 1.0))
    dys, dattn, dlt, dmlan = out_proj_bwd(dout, attn, p["mla_norm_w"], wv["wo"], "out_proj_bwd")
    dwo = jnp.concatenate([mm_tn(ys, dout, D_SSD, D_MODEL, "dwo_ssd"), mm_tn(ym, dout, D_SSD, D_MODEL, "dwo_mla")], axis=0)
    dxs, dbm, dcm, dz, ddt, dssdn, ddsk_lane, ddtb, dalog = ssd_bwd(
        dys, y, z, xs, bm, cm_, misc, prev, dtb, alog, dskip_e, p["ssd_norm_w"], e_mat, et_mat, "ssd_bwd")
    dq, dk, dv = flash_bwd(q, k, v, dattn, lse, dlt, "flash_bwd")
    dcq, dckv, dmisc, dqp, dkvc, dqn, dkvn = qkv_bwd(dq, dk, dv, ddt, cq, ckv, cc, sp, sm, p["q_norm_w"], p["kv_norm_w"],
                                                     wv["wuq_t"], wv["wukv_t"], place.T, "qkv_bwd")
    dwuq = mm_tn(dqp, qn, MLA_HEADS * HEAD_PAD, Q_LORA, "dwuq")
    dwukv = mm_tn(dkvc, kvn, MLA_HEADS * HEAD_PAD, KV_LORA, "dwukv")
    dvv, dconv = conv_bwd_a(dxs, dbm, dcm, u, p["conv_w"], p["conv_b"], "conv_bwd_a")
    dconv = conv_grads_fold(dconv, "conv_grads_fold")
    du = conv_bwd_b(dvv, p["conv_w"], "conv_bwd_b")
    dproj = jnp.concatenate([dz, du, dcq, dckv, dmisc], axis=-1)
    dwin = mm_tn(dproj, h2, D_IN_PAD // 2, D_MODEL, "dwin")
    dx1, dsc2, dsh2, dnmix, do1, dg1 = dh_norm_bwd([dproj], [wv["win_t"]], x1, dx2, p["norm_mix"], sc2, "mix_dh",
                                                   (o1, g1, 0.5))
    dx0, dmod1, dnf1, (dwg1, dwu1, dwd1), _ = _ffn_bwd(
        "ffn1", dx1, do1, dg1, x, h1, gg1, uu1, a1, sc1, p["norm_ffn1"], wv["wg1_t"], wv["wu1_t"], wv["wd1"], None)

    dmod = jnp.concatenate([*dmod1, dsh2, dsc2, dg2, *dmod3], axis=1).reshape(nb, N_MOD * d)
    return dict(
        loss=loss, dx=dx0, dmod=dmod, norm_ffn1=dnf1, norm_mix=dnmix, norm_ffn2=dnf2, norm_final=dnfin,
        ssd_norm_w=dssdn, mla_norm_w=dmlan, q_norm_w=dqn, kv_norm_w=dkvn,
        dt_bias=ddtb[:, :SSD_HEADS], a_log=dalog[:, :SSD_HEADS],
        d_skip=squeeze_heads(ddsk_lane, et_mat, "d_skip_heads")[:, :SSD_HEADS],
        conv_b=dconv[4:5], conv_w=dconv[0:4],
        gw=dict(ffn1_w_gate=dwg1, ffn1_w_up=dwu1, ffn1_w_down=dwd1, ffn2_w_gate=dwg2, ffn2_w_up=dwu2, ffn2_w_down=dwd2,
                w_out=dwo, w_in=dwin, w_ukv=dwukv, w_uq=dwuq))


def kernel(x, c, positions, w_ada, b_ada, norm_ffn1, ffn1_w_gate, ffn1_w_up, ffn1_w_down, norm_mix, w_in, conv_w, conv_b, dt_bias, a_log, d_skip, ssd_norm_w, q_norm_w, w_uq, kv_norm_w, w_ukv, mla_norm_w, w_out, norm_ffn2, ffn2_w_gate, ffn2_w_up, ffn2_w_down, norm_final, loss_target, m_w_ada, m_b_ada, m_norm_ffn1, m_ffn1_w_gate, m_ffn1_w_up, m_ffn1_w_down, m_norm_mix, m_w_in, m_conv_w, m_conv_b, m_dt_bias, m_a_log, m_d_skip, m_ssd_norm_w, m_q_norm_w, m_w_uq, m_kv_norm_w, m_w_ukv, m_mla_norm_w, m_w_out, m_norm_ffn2, m_ffn2_w_gate, m_ffn2_w_up, m_ffn2_w_down, m_norm_final, v_w_ada, v_b_ada, v_norm_ffn1, v_ffn1_w_gate, v_ffn1_w_up, v_ffn1_w_down, v_norm_mix, v_w_in, v_conv_w, v_conv_b, v_dt_bias, v_a_log, v_d_skip, v_ssd_norm_w, v_q_norm_w, v_w_uq, v_kv_norm_w, v_w_ukv, v_mla_norm_w, v_w_out, v_norm_ffn2, v_ffn2_w_gate, v_ffn2_w_up, v_ffn2_w_down, v_norm_final):
    names = ["w_ada", "b_ada", "norm_ffn1", "ffn1_w_gate", "ffn1_w_up", "ffn1_w_down", "norm_mix", "w_in", "conv_w",
             "conv_b", "dt_bias", "a_log", "d_skip", "ssd_norm_w", "q_norm_w", "w_uq", "kv_norm_w", "w_ukv",
             "mla_norm_w", "w_out", "norm_ffn2", "ffn2_w_gate", "ffn2_w_up", "ffn2_w_down", "norm_final"]
    W = dict(zip(names, (w_ada, b_ada, norm_ffn1, ffn1_w_gate, ffn1_w_up, ffn1_w_down, norm_mix, w_in, conv_w, conv_b, dt_bias, a_log, d_skip, ssd_norm_w, q_norm_w, w_uq, kv_norm_w, w_ukv, mla_norm_w, w_out, norm_ffn2, ffn2_w_gate, ffn2_w_up, ffn2_w_down, norm_final)))
    M = dict(zip(names, (m_w_ada, m_b_ada, m_norm_ffn1, m_ffn1_w_gate, m_ffn1_w_up, m_ffn1_w_down, m_norm_mix, m_w_in, m_conv_w, m_conv_b, m_dt_bias, m_a_log, m_d_skip, m_ssd_norm_w, m_q_norm_w, m_w_uq, m_kv_norm_w, m_w_ukv, m_mla_norm_w, m_w_out, m_norm_ffn2, m_ffn2_w_gate, m_ffn2_w_up, m_ffn2_w_down, m_norm_final)))
    V = dict(zip(names, (v_w_ada, v_b_ada, v_norm_ffn1, v_ffn1_w_gate, v_ffn1_w_up, v_ffn1_w_down, v_norm_mix, v_w_in, v_conv_w, v_conv_b, v_dt_bias, v_a_log, v_d_skip, v_ssd_norm_w, v_q_norm_w, v_w_uq, v_kv_norm_w, v_w_ukv, v_mla_norm_w, v_w_out, v_norm_ffn2, v_ffn2_w_gate, v_ffn2_w_up, v_ffn2_w_down, v_norm_final)))

    nb, s, d = x.shape
    me = 4 * lax.axis_index("x") + 2 * lax.axis_index("y") + lax.axis_index("c")
    n_ada = w_ada.shape[2]

    cshape = [(nb, d), conv_w.shape[1:]]
    cg = all_gather8(_pack_rows([c, conv_w[0]]), "gather_c")
    c_all = jnp.stack([_unpack_rows(cg[k], cshape)[0] for k in range(N_DEV)]).reshape(N_DEV * nb, d)
    conv_w_full = jnp.concatenate([_unpack_rows(cg[k], cshape)[1] for k in range(N_DEV)], axis=1)
    g_ffn1 = all_gather8(_pack_shards(W, GATHER_GROUPS[0], BF16), "gather_w_ffn1")

    b_ada_cols = lax.dynamic_slice(b_ada, (0, me * n_ada), (1, n_ada))
    mod_cols, c_act = adaln_fwd(c_all, w_ada[0], b_ada_cols, "adaln_fwd")
    mod_g = all_gather8(mod_cols, "gather_mod")
    g_ffn1, mod_g, down1, rest = lax.optimization_barrier(
        (g_ffn1, mod_g, _pack_shards(W, GATHER_GROUPS[1], BF16), _pack_shards(W, GATHER_GROUPS[2], BF16)))
    g_down1 = sc_all_gather8(down1, "gather_w_ffn1_down", 1)
    wv = weight_views((g_ffn1, g_down1, sc_all_gather8(rest, "gather_w_rest", 6)))
    mod = lax.dynamic_slice(mod_g, (0, me * nb, 0), (N_DEV, nb, n_ada)).transpose(1, 0, 2).reshape(nb, N_MOD, 1, d)
    mod = [mod[:, k] for k in range(N_MOD)]

    P = dict(W)
    P["conv_w"] = conv_w_full
    P["norm_final"] = norm_final.reshape(1, d)
    R = local_step(x, loss_target, positions, mod, wv, P)

    dmod = R["dmod"]
    partial_shapes = [(1,), (1, d), (1, d), (1, d), (1, d), (1, d), (1, d), (1, Q_LORA), (1, KV_LORA),
                      (1, SSD_HEADS), (1, SSD_HEADS), (1, SSD_HEADS), (1, D_CONV), (4, D_CONV), (1, N_MOD * d),
                      (nb, N_MOD * d)]
    partial = _pack_rows([R["loss"][0, :1], R["norm_ffn1"], R["norm_mix"], R["norm_ffn2"], R["norm_final"],
                          R["ssd_norm_w"], R["mla_norm_w"], R["q_norm_w"], R["kv_norm_w"],
                          R["dt_bias"], R["a_log"], R["d_skip"], R["conv_b"], R["conv_w"],
                          sum_rows(dmod, "dmod_rows"), dmod])
    partial_g = all_gather8(partial, "gather_partials")
    (loss, g_nf1, g_nmix, g_nf2, g_nfin, g_ssdn, g_mlan, g_qn, g_kvn, g_dtb, g_alog, g_dskip, g_convb, g_convw,
     g_bada, _) = _unpack_rows(sum_blocks(partial_g, "sum_partials"), partial_shapes)
    dmod_all = jnp.stack([_unpack_rows(partial_g[k], partial_shapes)[-1] for k in range(N_DEV)]).reshape(N_DEV * nb, -1)
    g_wada = adaln_bwd(c_act, lax.dynamic_slice(dmod_all, (0, me * n_ada), (N_DEV * nb, n_ada)), "adaln_bwd")
    n_cw = conv_w.shape[2]
    G = {"w_ada": g_wada[None], "b_ada": g_bada, "norm_ffn1": g_nf1, "norm_mix": g_nmix, "norm_ffn2": g_nf2,
         "norm_final": g_nfin.reshape(d), "ssd_norm_w": g_ssdn, "mla_norm_w": g_mlan, "q_norm_w": g_qn,
         "kv_norm_w": g_kvn, "dt_bias": g_dtb, "a_log": g_alog, "d_skip": g_dskip, "conv_b": g_convb,
         "conv_w": lax.dynamic_slice(g_convw, (0, me * n_cw), (4, n_cw))[None]}

    DW, NM, NV = {}, {}, {}
    gw = R["gw"]
    for k, (tag, group) in enumerate(GRAD_GROUPS):
        send = jnp.concatenate([_grad_rows(name, gw[name]) for name in group], axis=1).astype(BF16)
        recv = sc_all_to_all8(send, "exchange_" + tag, 2 + k)
        big = adamw_sum8(_pack_shards(W, group, F32), recv, _pack_shards(M, group, F32), _pack_shards(V, group, F32),
                         "adamw_" + tag)
        for name, (o, r) in _pack_offsets(group)[0].items():
            G[name], DW[name], NM[name], NV[name] = [_rows_to_shard(name, t[o:o + r], W[name]) for t in big]
    dwa, nma, nva = adamw(w_ada[0], g_wada, m_w_ada[0], v_w_ada[0], "adamw_w_ada")
    DW["w_ada"], NM["w_ada"], NV["w_ada"] = dwa[None], nma[None], nva[None]
    small = [n for n in names if n not in DW]
    shapes = [W[n].shape for n in small]
    outs = adamw(_pack_rows([W[n] for n in small]), _pack_rows([G[n] for n in small]),
                 _pack_rows([M[n] for n in small]), _pack_rows([V[n] for n in small]), "adamw_small")
    for res, dst in zip(outs, (DW, NM, NV)):
        for n, t in zip(small, _unpack_rows(res, shapes)):
            dst[n] = t
    return (loss.reshape(()), R["dx"], *[G[n] for n in names], *[DW[n] for n in names], *[NM[n] for n in names],
            *[NV[n] for n in names])
```

```python
import math

import jax
import jax.numpy as jnp
from jax import lax
from jax.experimental import pallas as pl
from jax.experimental.pallas import tpu as pltpu
from jax.experimental.pallas import tpu_sc as plsc

F32, BF16, I32 = jnp.float32, jnp.bfloat16, jnp.int32
HI = lax.Precision.HIGHEST
SDS = jax.ShapeDtypeStruct
MESH = pl.DeviceIdType.MESH

D_MODEL = 1024
D_FF = 2816
D_SSD = 1024
SSD_HEADS = 16
SSD_HEAD_DIM = 64
SSD_GROUPS = 2
SSD_STATE = 128
CHUNK = 128
MLA_HEADS = 8
QK_NOPE = 64
QK_ROPE = 32
QK_DIM = 96
V_HEAD = 128
Q_LORA = 384
KV_LORA = 256
ROPE_THETA = 10000.0
N_MOD = 9
EPS = 1e-6
D_CONV = 1536
D_IN = 3248
D_IN_PAD = 3328
HEAD_PAD = 128
N_DEV = 8
ADAM_LR, ADAM_B1, ADAM_B2, ADAM_EPS, ADAM_WD, ADAM_STEP = 0.001, 0.9, 0.999, 1e-08, 0.01, 10

SAVED_ACT = BF16
VMEM_LIMIT = 56 * 1024 * 1024
LANES = 128
NT_DIMS = (((1,), (1,)), ((), ()))
TN_DIMS = (((0,), (0,)), ((), ()))


def _cparams(n_axes):
    return pltpu.CompilerParams(dimension_semantics=("arbitrary",) * n_axes, vmem_limit_bytes=VMEM_LIMIT)


def _row(tm, d):
    return pl.BlockSpec((None, tm, d), lambda b, i: (b, i, 0))


def _bvec(d):
    return pl.BlockSpec((None, 1, d), lambda b, i: (b, 0, 0))


def _full(shape):
    n = len(shape)
    return pl.BlockSpec(shape, lambda *_: (0,) * n)


def _sigmoid(x):
    return 1.0 / (1.0 + jnp.exp(-x))


def _softplus(x):
    return jnp.maximum(x, 0.0) + jnp.log(1.0 + jnp.exp(-jnp.abs(x)))


def _rms(x):
    return lax.rsqrt(jnp.mean(x * x, axis=-1, keepdims=True) + EPS)


def _rms_bwd(dn, n, r):
    return r * (dn - n * jnp.mean(dn * n, axis=-1, keepdims=True))


def _first_step():
    return (pl.program_id(0) == 0) & (pl.program_id(1) == 0)


def all_gather8(x, name):
    r, c = x.shape

    def body(x_ref, out_ref, send_sems, recv_sems, local_sem):
        mx, my, mc = lax.axis_index("x"), lax.axis_index("y"), lax.axis_index("c")
        me, sibling = (mx, my, mc), (mx, my, 1 - mc)
        chips = [(1 - mx, my), (mx, 1 - my), (1 - mx, 1 - my)]

        def rows(px, py, pc):
            return out_ref.at[4 * px + 2 * py + pc]

        def copy(k, block, to, src=None):
            return pltpu.make_async_remote_copy(
                src_ref=rows(*block) if src is None else src, dst_ref=rows(*block),
                send_sem=send_sems.at[k], recv_sem=recv_sems.at[k], device_id=to, device_id_type=MESH)

        mine = pltpu.make_async_copy(x_ref, rows(*me), local_sem)
        mine.start()
        first = [copy(0, me, sibling, src=x_ref)]
        first += [copy(1 + j, me, (*chip, mc), src=x_ref) for j, chip in enumerate(chips)]
        for cp in first:
            cp.start()
        passed = [copy(4 + j, (*chip, mc), sibling) for j, chip in enumerate(chips)]
        for j, chip in enumerate(chips):
            copy(1 + j, (*chip, mc), me).wait_recv()
            passed[j].start()
        copy(0, sibling, me).wait_recv()
        for j, chip in enumerate(chips):
            copy(4 + j, (*chip, 1 - mc), me).wait_recv()
        for cp in first + passed:
            cp.wait_send()
        mine.wait()

    return pl.pallas_call(
        body, name=name,
        out_shape=SDS((N_DEV, r, c), x.dtype),
        in_specs=[pl.BlockSpec(memory_space=pl.ANY)],
        out_specs=pl.BlockSpec(memory_space=pl.ANY),
        scratch_shapes=[pltpu.SemaphoreType.DMA((7,)), pltpu.SemaphoreType.DMA((7,)), pltpu.SemaphoreType.DMA],
    )(x)


def all_to_all8(x, name):
    _, r, c = x.shape

    def body(x_ref, out_ref, send_sems, recv_sems, local_sem):
        mx, my, mc = lax.axis_index("x"), lax.axis_index("y"), lax.axis_index("c")
        me = 4 * mx + 2 * my + mc
        mine = pltpu.make_async_copy(x_ref.at[me], out_ref.at[me], local_sem)
        mine.start()
        copies = []
        for rel in range(1, N_DEV):
            px = 1 - mx if rel & 4 else mx
            py = 1 - my if rel & 2 else my
            pc = 1 - mc if rel & 1 else mc
            cp = pltpu.make_async_remote_copy(
                src_ref=x_ref.at[4 * px + 2 * py + pc], dst_ref=out_ref.at[me],
                send_sem=send_sems.at[rel - 1], recv_sem=recv_sems.at[rel - 1],
                device_id=(px, py, pc), device_id_type=MESH)
            cp.start()
            copies.append(cp)
        for cp in copies:
            cp.wait()
        mine.wait()

    return pl.pallas_call(
        body, name=name,
        out_shape=SDS((N_DEV, r, c), x.dtype),
        in_specs=[pl.BlockSpec(memory_space=pl.ANY)],
        out_specs=pl.BlockSpec(memory_space=pl.ANY),
        scratch_shapes=[pltpu.SemaphoreType.DMA((7,)), pltpu.SemaphoreType.DMA((7,)), pltpu.SemaphoreType.DMA],
    )(x)


def _sequencer_kernel(name, collective_id):
    return pl.kernel(
        mesh=plsc.ScalarSubcoreMesh(axis_name="seq", num_cores=1), name=name,
        scratch_types=(pltpu.SemaphoreType.DMA((7,)), pltpu.SemaphoreType.DMA((7,)), pltpu.SemaphoreType.DMA),
        compiler_params=pltpu.CompilerParams(collective_id=collective_id))


def _handshake(peers):
    barrier = pltpu.get_barrier_semaphore()
    for peer in peers:
        pl.semaphore_signal(barrier, inc=1, device_id=peer, device_id_type=MESH)
    pl.semaphore_wait(barrier, len(peers))


def sc_all_gather8(x, name, collective_id):
    r, c = x.shape
    x_ref = jax.new_ref(x, memory_space=pltpu.MemorySpace.HBM)
    out_ref = jax.empty_ref(SDS((N_DEV, r, c), x.dtype), memory_space=pltpu.MemorySpace.HBM)

    @_sequencer_kernel(name, collective_id)
    def launch(send_sems, recv_sems, local_sem):
        mx, my, mc = lax.axis_index("x"), lax.axis_index("y"), lax.axis_index("c")
        me, sibling = (mx, my, mc), (mx, my, 1 - mc)
        chips = [(1 - mx, my), (mx, 1 - my), (1 - mx, 1 - my)]
        _handshake([sibling] + [(*chip, mc) for chip in chips])

        def rows(px, py, pc):
            return out_ref.at[4 * px + 2 * py + pc]

        def copy(k, block, to, src=None):
            return pltpu.make_async_remote_copy(
                src_ref=rows(*block) if src is None else src, dst_ref=rows(*block),
                send_sem=send_sems.at[k], recv_sem=recv_sems.at[k], device_id=to, device_id_type=MESH)

        mine = pltpu.make_async_copy(x_ref, rows(*me), local_sem)
        mine.start()
        first = [copy(0, me, sibling, src=x_ref)]
        first += [copy(1 + j, me, (*chip, mc), src=x_ref) for j, chip in enumerate(chips)]
        for cp in first:
            cp.start()
        passed = [copy(4 + j, (*chip, mc), sibling) for j, chip in enumerate(chips)]
        for j, chip in enumerate(chips):
            copy(1 + j, (*chip, mc), me).wait_recv()
            passed[j].start()
        copy(0, sibling, me).wait_recv()
        for j, chip in enumerate(chips):
            copy(4 + j, (*chip, 1 - mc), me).wait_recv()
        for cp in first + passed:
            cp.wait_send()
        mine.wait()

    launch()
    return out_ref[...]


def sc_all_to_all8(x, name, collective_id):
    x_ref = jax.new_ref(x, memory_space=pltpu.MemorySpace.HBM)
    out_ref = jax.empty_ref(SDS(x.shape, x.dtype), memory_space=pltpu.MemorySpace.HBM)

    @_sequencer_kernel(name, collective_id)
    def launch(send_sems, recv_sems, local_sem):
        mx, my, mc = lax.axis_index("x"), lax.axis_index("y"), lax.axis_index("c")
        me = 4 * mx + 2 * my + mc
        peers = [(1 - mx if rel & 4 else mx, 1 - my if rel & 2 else my, 1 - mc if rel & 1 else mc)
                 for rel in range(1, N_DEV)]
        _handshake(peers)
        mine = pltpu.make_async_copy(x_ref.at[me], out_ref.at[me], local_sem)
        mine.start()
        copies = []
        for k, (px, py, pc) in enumerate(peers):
            cp = pltpu.make_async_remote_copy(
                src_ref=x_ref.at[4 * px + 2 * py + pc], dst_ref=out_ref.at[me],
                send_sem=send_sems.at[k], recv_sem=recv_sems.at[k], device_id=(px, py, pc), device_id_type=MESH)
            cp.start()
            copies.append(cp)
        for cp in copies:
            cp.wait()
        mine.wait()

    launch()
    return out_ref[...]


def norm_mod(x, w, sc, sh, name):
    b, s, d = x.shape
    tm = min(512, s)

    def body(x_ref, w_ref, sc_ref, sh_ref, h_ref):
        xv = x_ref[...]
        n = xv * _rms(xv)
        h_ref[...] = ((n * w_ref[...]) * (1.0 + sc_ref[...]) + sh_ref[...]).astype(BF16)

    return pl.pallas_call(
        body, name=name, grid=(b, s // tm),
        in_specs=[_row(tm, d), _full((1, d)), _bvec(d), _bvec(d)],
        out_specs=_row(tm, d), out_shape=SDS((b, s, d), BF16), compiler_params=_cparams(2))(x, w, sc, sh)


def ffn_up(h, wg_t, wu_t, name):
    b, s, d = h.shape
    f = wg_t.shape[0]
    tm, tn = min(512, s), f // 2

    def body(h_ref, wg_ref, wu_ref, g_ref, u_ref, a_ref):
        hv = h_ref[...]
        g = lax.dot_general(hv, wg_ref[...], NT_DIMS, preferred_element_type=F32)
        u = lax.dot_general(hv, wu_ref[...], NT_DIMS, preferred_element_type=F32)
        g_ref[...] = g.astype(g_ref.dtype)
        u_ref[...] = u.astype(u_ref.dtype)
        a_ref[...] = (g * _sigmoid(g) * u).astype(BF16)

    hs = pl.BlockSpec((None, tm, d), lambda j, bb, i: (bb, i, 0))
    ws = pl.BlockSpec((tn, d), lambda j, bb, i: (j, 0))
    os_ = pl.BlockSpec((None, tm, tn), lambda j, bb, i: (bb, i, j))
    return pl.pallas_call(
        body, name=name, grid=(f // tn, b, s // tm),
        in_specs=[hs, ws, ws], out_specs=[os_, os_, os_],
        out_shape=[SDS((b, s, f), SAVED_ACT), SDS((b, s, f), SAVED_ACT), SDS((b, s, f), BF16)],
        compiler_params=_cparams(3))(h, wg_t, wu_t)


def _norm_mod_tile(xv, w_ref, sc_ref, sh_ref):
    return ((xv * _rms(xv) * w_ref[...]) * (1.0 + sc_ref[...]) + sh_ref[...]).astype(BF16)


def ffn_down(a, wd, x, gate, scale, name, above=None):
    b, s, f = a.shape
    d = wd.shape[1]
    tm = min(512, s)

    def body(a_ref, wd_ref, x_ref, g_ref, *rest):
        xn_ref, o_ref = rest[-3:-1] if above else rest
        o = jnp.dot(a_ref[...], wd_ref[...], preferred_element_type=F32)
        xn = x_ref[...] + (scale * g_ref[...]) * o
        xn_ref[...] = xn
        o_ref[...] = o.astype(BF16)
        if above:
            rest[-1][...] = _norm_mod_tile(xn, *rest[0:3])

    extra = above is not None
    return pl.pallas_call(
        body, name=name, grid=(b, s // tm),
        in_specs=[_row(tm, f), _full((f, d)), _row(tm, d), _bvec(d)] + ([_full((1, d)), _bvec(d), _bvec(d)] if extra else []),
        out_specs=[_row(tm, d), _row(tm, d)] + ([_row(tm, d)] if extra else []),
        out_shape=[SDS((b, s, d), F32), SDS((b, s, d), BF16)] + ([SDS((b, s, d), BF16)] if extra else []),
        compiler_params=_cparams(2))(a, wd, x, gate, *(above or ()))


def ffn_dact(do, wd, g, u, name):
    b, s, d = do.shape
    f = wd.shape[0]
    tm, tn = min(512, s), f // 2

    def body(do_ref, wd_ref, g_ref, u_ref, dg_ref, du_ref):
        da = lax.dot_general(do_ref[...], wd_ref[...], NT_DIMS, preferred_element_type=F32)
        gv = g_ref[...].astype(F32)
        sg = _sigmoid(gv)
        dg_ref[...] = (da * u_ref[...].astype(F32) * (sg * (1.0 + gv * (1.0 - sg)))).astype(BF16)
        du_ref[...] = (da * (gv * sg)).astype(BF16)

    dos = pl.BlockSpec((None, tm, d), lambda j, bb, i: (bb, i, 0))
    ws = pl.BlockSpec((tn, d), lambda j, bb, i: (j, 0))
    es = pl.BlockSpec((None, tm, tn), lambda j, bb, i: (bb, i, j))
    return pl.pallas_call(
        body, name=name, grid=(f // tn, b, s // tm),
        in_specs=[dos, ws, es, es], out_specs=[es, es],
        out_shape=[SDS((b, s, f), BF16), SDS((b, s, f), BF16)], compiler_params=_cparams(3))(do, wd, g, u)


def mm_tn(a, bm, tma, tnb, name):
    b, s, ka = a.shape
    nb = bm.shape[2]
    tk = min(2048, s)
    nk = s // tk

    def body(a_ref, b_ref, o_ref, acc):
        first = (pl.program_id(2) == 0) & (pl.program_id(3) == 0)
        last = (pl.program_id(2) == b - 1) & (pl.program_id(3) == nk - 1)
        part = lax.dot_general(a_ref[...], b_ref[...], TN_DIMS, preferred_element_type=F32)

        @pl.when(first)
        def _():
            acc[...] = part

        @pl.when(jnp.logical_not(first))
        def _():
            acc[...] += part

        @pl.when(last)
        def _():
            o_ref[...] = acc[...].astype(BF16)

    return pl.pallas_call(
        body, name=name, grid=(ka // tma, nb // tnb, b, nk),
        in_specs=[pl.BlockSpec((None, tk, tma), lambda i, j, bb, k: (bb, k, i)),
                  pl.BlockSpec((None, tk, tnb), lambda i, j, bb, k: (bb, k, j))],
        out_specs=pl.BlockSpec((tma, tnb), lambda i, j, bb, k: (i, j)),
        out_shape=SDS((ka, nb), BF16), scratch_shapes=[pltpu.VMEM((tma, tnb), F32)],
        compiler_params=_cparams(4))(a, bm)


def _gate_bwd_specs(tm, d, b, s):
    return ([_row(tm, d), _bvec(d)], [_row(tm, d), _bvec(d)], [SDS((b, s, d), BF16), SDS((b, 1, d), F32)])


def _gate_bwd_tile(dx, scale, o_ref, g_ref, do_ref, dg_ref):
    do_ref[...] = ((scale * g_ref[...]) * dx).astype(BF16)
    dg_ref[...] += jnp.sum(scale * dx * o_ref[...].astype(F32), axis=0, keepdims=True)


def dh_norm_bwd(dys, wts, x, dxn, w, sc, name, below=None):
    b, s, d = x.shape
    tm = min(256, s)
    n_in = len(dys)
    extra_in, extra_out, extra_shape = _gate_bwd_specs(tm, d, b, s) if below else ([], [], [])

    def body(*refs):
        dy_refs, w_refs = refs[:n_in], refs[n_in:2 * n_in]
        x_ref, dxn_ref, nw_ref, sc_ref = refs[2 * n_in:2 * n_in + 4]
        rest = refs[2 * n_in + 4:]
        if below:
            o_ref, g_ref, dx_ref, dsc_ref, dsh_ref, dw_ref, do_ref, dg_ref = rest
        else:
            dx_ref, dsc_ref, dsh_ref, dw_ref = rest

        @pl.when(pl.program_id(1) == 0)
        def _():
            dsc_ref[...] = jnp.zeros_like(dsc_ref)
            dsh_ref[...] = jnp.zeros_like(dsh_ref)
            if below:
                dg_ref[...] = jnp.zeros_like(dg_ref)

        @pl.when(_first_step())
        def _():
            dw_ref[...] = jnp.zeros_like(dw_ref)

        dh = jnp.dot(dy_refs[0][...], w_refs[0][...], preferred_element_type=F32)
        for k in range(1, n_in):
            dh += jnp.dot(dy_refs[k][...], w_refs[k][...], preferred_element_type=F32)
        xv = x_ref[...]
        r = _rms(xv)
        n = xv * r
        nw = nw_ref[...]
        dsc_ref[...] += jnp.sum(dh * (n * nw), axis=0, keepdims=True)
        dsh_ref[...] += jnp.sum(dh, axis=0, keepdims=True)
        dhn = dh * (1.0 + sc_ref[...])
        dw_ref[...] += jnp.sum(dhn * n, axis=0, keepdims=True)
        dx = dxn_ref[...] + _rms_bwd(dhn * nw, n, r)
        dx_ref[...] = dx
        if below:
            _gate_bwd_tile(dx, below[2], o_ref, g_ref, do_ref, dg_ref)

    in_specs = [_row(tm, dy.shape[2]) for dy in dys] + [_full(wt.shape) for wt in wts]
    in_specs += [_row(tm, d), _row(tm, d), _full((1, d)), _bvec(d)] + extra_in
    return pl.pallas_call(
        body, name=name, grid=(b, s // tm), in_specs=in_specs,
        out_specs=[_row(tm, d), _bvec(d), _bvec(d), _full((1, d))] + extra_out,
        out_shape=[SDS((b, s, d), F32), SDS((b, 1, d), F32), SDS((b, 1, d), F32), SDS((1, d), F32)] + extra_shape,
        compiler_params=_cparams(2))(*dys, *wts, x, dxn, w, sc, *(below[:2] if below else ()))


def final_loss(x, w, tgt, below, name):
    b, s, d = x.shape
    tm = min(512, s)
    extra_in, extra_out, extra_shape = _gate_bwd_specs(tm, d, b, s)

    def body(x_ref, w_ref, t_ref, o_ref, g_ref, loss_ref, dx_ref, dw_ref, do_ref, dg_ref):
        @pl.when(_first_step())
        def _():
            loss_ref[...] = jnp.zeros_like(loss_ref)
            dw_ref[...] = jnp.zeros_like(dw_ref)

        @pl.when(pl.program_id(1) == 0)
        def _():
            dg_ref[...] = jnp.zeros_like(dg_ref)
        xv = x_ref[...]
        r = _rms(xv)
        n = xv * r
        wv = w_ref[...]
        e = n * wv - t_ref[...]
        loss_ref[...] += jnp.sum(e * e) * (0.5 / d)
        dy = e * (1.0 / d)
        dw_ref[...] += jnp.sum(dy * n, axis=0, keepdims=True)
        dx = _rms_bwd(dy * wv, n, r)
        dx_ref[...] = dx
        _gate_bwd_tile(dx, below[2], o_ref, g_ref, do_ref, dg_ref)

    return pl.pallas_call(
        body, name=name, grid=(b, s // tm),
        in_specs=[_row(tm, d), _full((1, d)), _row(tm, d)] + extra_in,
        out_specs=[_full((1, LANES)), _row(tm, d), _full((1, d))] + extra_out,
        out_shape=[SDS((1, LANES), F32), SDS((b, s, d), F32), SDS((1, d), F32)] + extra_shape,
        compiler_params=_cparams(2))(x, w, tgt, *below[:2])


def in_proj(h, win_t, name):
    b, s, d = h.shape
    tm = min(512, s)
    widths = (D_SSD, D_SSD + 2 * SSD_GROUPS * SSD_STATE, Q_LORA, KV_LORA, LANES)

    def body(h_ref, w_ref, *outs):
        p = lax.dot_general(h_ref[...], w_ref[...], NT_DIMS, preferred_element_type=F32)
        off = 0
        for o_ref, wd in zip(outs, widths):
            o_ref[...] = p[:, off:off + wd]
            off += wd

    return pl.pallas_call(
        body, name=name, grid=(b, s // tm),
        in_specs=[_row(tm, d), _full(win_t.shape)],
        out_specs=[_row(tm, wd) for wd in widths],
        out_shape=[SDS((b, s, wd), F32) for wd in widths], compiler_params=_cparams(2))(h, win_t)


def _halo_prev(ts, d):
    return pl.BlockSpec((None, 8, d), lambda b, i: (b, jnp.maximum(i * (ts // 8) - 1, 0), 0))


CONV_ROWS = 32


def _conv_head(head, u_ref, up_ref):
    head[0:8, :] = jnp.where(pl.program_id(1) > 0, up_ref[...], 0.0)
    head[8:8 + CONV_ROWS, :] = u_ref[0:CONV_ROWS, :]


def _conv_windows(u_ref, head, r0):
    if r0 == 0:
        return [head[5 + k:5 + k + CONV_ROWS, :] for k in range(4)]
    return [u_ref[r0 - 3 + k:r0 - 3 + k + CONV_ROWS, :] for k in range(4)]


def _fold8(t):
    acc = t[0:8, :]
    for r in range(8, CONV_ROWS, 8):
        acc += t[r:r + 8, :]
    return acc


def conv_fwd(u, cw, cb, name):
    b, s, dc = u.shape
    ts = min(512, s)
    widths = (D_SSD, SSD_GROUPS * SSD_STATE, SSD_GROUPS * SSD_STATE)

    def body(u_ref, up_ref, w_ref, b_ref, xs_ref, bm_ref, cm_ref, head):
        _conv_head(head, u_ref, up_ref)
        ws = [w_ref[k:k + 1, :] for k in range(4)]
        bias = b_ref[...]
        for r0 in range(0, ts, CONV_ROWS):
            taps = _conv_windows(u_ref, head, r0)
            v = bias + taps[0] * ws[0] + taps[1] * ws[1] + taps[2] * ws[2] + taps[3] * ws[3]
            y = v * _sigmoid(v)
            rs = slice(r0, r0 + CONV_ROWS)
            xs_ref[rs, :] = y[:, 0:D_SSD]
            bm_ref[rs, :] = y[:, D_SSD:D_SSD + 256]
            cm_ref[rs, :] = y[:, D_SSD + 256:D_SSD + 512]

    return pl.pallas_call(
        body, name=name, grid=(b, s // ts),
        in_specs=[_row(ts, dc), _halo_prev(ts, dc), _full((4, dc)), _full((1, dc))],
        out_specs=[_row(ts, wd) for wd in widths],
        out_shape=[SDS((b, s, wd), F32) for wd in widths],
        scratch_shapes=[pltpu.VMEM((8 + CONV_ROWS, dc), F32)], compiler_params=_cparams(2))(u, u, cw, cb)


def conv_bwd_a(dxs, dbm, dcm, u, cw, cb, name):
    b, s, dc = u.shape
    ts = min(512, s)

    def body(dxs_ref, dbm_ref, dcm_ref, u_ref, up_ref, w_ref, b_ref, dv_ref, dwb_ref, head):
        @pl.when(_first_step())
        def _():
            dwb_ref[...] = jnp.zeros_like(dwb_ref)
        _conv_head(head, u_ref, up_ref)
        ws = [w_ref[k:k + 1, :] for k in range(4)]
        bias = b_ref[...]
        for r0 in range(0, ts, CONV_ROWS):
            taps = _conv_windows(u_ref, head, r0)
            v = bias + taps[0] * ws[0] + taps[1] * ws[1] + taps[2] * ws[2] + taps[3] * ws[3]
            sg = _sigmoid(v)
            rs = slice(r0, r0 + CONV_ROWS)
            dy = jnp.concatenate([dxs_ref[rs, :], dbm_ref[rs, :], dcm_ref[rs, :]], axis=1)
            dv = dy * (sg * (1.0 + v * (1.0 - sg)))
            dv_ref[rs, :] = dv
            for k in range(4):
                dwb_ref[8 * k:8 * k + 8, :] += _fold8(dv * taps[k])
            dwb_ref[32:40, :] += _fold8(dv)

    return pl.pallas_call(
        body, name=name, grid=(b, s // ts),
        in_specs=[_row(ts, D_SSD), _row(ts, 256), _row(ts, 256), _row(ts, dc), _halo_prev(ts, dc),
                  _full((4, dc)), _full((1, dc))],
        out_specs=[_row(ts, dc), _full((40, dc))],
        out_shape=[SDS((b, s, dc), F32), SDS((40, dc), F32)],
        scratch_shapes=[pltpu.VMEM((8 + CONV_ROWS, dc), F32)], compiler_params=_cparams(2))(dxs, dbm, dcm, u, u, cw, cb)


def conv_grads_fold(x, name):
    c = x.shape[1]

    def body(x_ref, o_ref):
        o_ref[...] = jnp.zeros_like(o_ref)
        for k in range(5):
            o_ref[k:k + 1, :] = jnp.sum(x_ref[8 * k:8 * k + 8, :], axis=0, keepdims=True)

    return pl.pallas_call(body, name=name, out_shape=SDS((8, c), F32))(x)


def conv_bwd_b(dv, cw, name):
    b, s, dc = dv.shape
    ts = min(512, s)
    nt = s // ts

    def body(dv_ref, dn_ref, w_ref, du_ref, tail):
        tail[0:CONV_ROWS, :] = dv_ref[ts - CONV_ROWS:ts, :]
        tail[CONV_ROWS:CONV_ROWS + 8, :] = jnp.where(pl.program_id(1) < nt - 1, dn_ref[...], 0.0)
        ws = [w_ref[k:k + 1, :] for k in range(4)]
        for r0 in range(0, ts, CONV_ROWS):
            if r0 == ts - CONV_ROWS:
                win = [tail[3 - k:3 - k + CONV_ROWS, :] for k in range(4)]
            else:
                win = [dv_ref[r0 + 3 - k:r0 + 3 - k + CONV_ROWS, :] for k in range(4)]
            acc = win[0] * ws[0] + win[1] * ws[1] + win[2] * ws[2] + win[3] * ws[3]
            du_ref[r0:r0 + CONV_ROWS, :] = acc.astype(BF16)

    nxt = pl.BlockSpec((None, 8, dc), lambda bb, i: (bb, jnp.minimum((i + 1) * (ts // 8), s // 8 - 1), 0))
    return pl.pallas_call(
        body, name=name, grid=(b, nt),
        in_specs=[_row(ts, dc), nxt, _full((4, dc))],
        out_specs=_row(ts, dc), out_shape=SDS((b, s, dc), BF16),
        scratch_shapes=[pltpu.VMEM((CONV_ROWS + 8, dc), F32)], compiler_params=_cparams(2))(dv, dv, cw)


def _ssd_common(misc_ref, dtb_ref, alog_ref, e_ref):
    ln = CHUNK
    lane = lax.broadcasted_iota(I32, (ln, LANES), 1)
    lane1 = lax.broadcasted_iota(I32, (1, LANES), 1)
    pre = misc_ref[...] + dtb_ref[...]
    dt_s = jnp.where(lane < SSD_HEADS, _softplus(pre), 0.0)
    a_neg = jnp.where(lane1 < SSD_HEADS, -jnp.exp(alog_ref[...]), 0.0)
    ri = lax.broadcasted_iota(I32, (ln, ln), 0)
    ci = lax.broadcasted_iota(I32, (ln, ln), 1)
    tril = ci <= ri
    acum = jnp.dot(tril.astype(F32), dt_s * a_neg, preferred_element_type=F32, precision=HI)
    both_e = _dot_01(jnp.concatenate([dt_s, acum], axis=0), e_ref[...], 3)
    dt_e, acum_e = both_e[0:ln], both_e[ln:2 * ln]
    return dict(pre=pre, dt_s=dt_s, a_neg=a_neg, tril=tril, ri=ri, ci=ci, acum=acum, acum_t=acum.T,
                dt_e=dt_e, eac_e=jnp.exp(acum_e), del_e=jnp.exp(acum_e[ln - 1:ln, :] - acum_e))


def _dot_01(x, m01, terms):
    acc, rest = None, x
    for k in range(terms):
        part = rest.astype(BF16)
        if k + 1 < terms:
            rest = rest - part.astype(F32)
        d = jnp.dot(part, m01, preferred_element_type=F32)
        acc = d if acc is None else acc + d
    return acc


def _decay(cm, h):
    seg = cm["acum"][:, h:h + 1] - cm["acum_t"][h:h + 1, :]
    return jnp.exp(jnp.where(cm["tril"], seg, -jnp.inf))


def ssd_fwd(xs, bm, cm_, misc, z, dtb, alog, dskip_e, norm_w, e_mat, name):
    b, s, _ = xs.shape
    ln, nc = CHUNK, s // CHUNK
    gw = D_SSD // SSD_GROUPS
    hpg = SSD_HEADS // SSD_GROUPS

    def body(xs_ref, b_ref, c_ref, misc_ref, z_ref, dtb_ref, alog_ref, dsk_ref, nw_ref, e_ref,
             ys_ref, y_ref, p_ref, st, yd):
        @pl.when(pl.program_id(1) == 0)
        def _():
            st[...] = jnp.zeros_like(st)
        cm = _ssd_common(misc_ref, dtb_ref, alog_ref, e_ref)
        xsv = xs_ref[...]
        xdt = xsv * cm["dt_e"]
        xdt_b = xdt.astype(BF16)
        xd_b = (xdt * cm["del_e"]).astype(BF16)
        gam_e = cm["eac_e"][ln - 1:ln, :]
        p_ref[...] = st[...]
        yoff = []
        for g in range(SSD_GROUPS):
            gs = slice(gw * g, gw * (g + 1))
            bg = b_ref[:, SSD_STATE * g:SSD_STATE * (g + 1)].astype(BF16)
            cg = c_ref[:, SSD_STATE * g:SSD_STATE * (g + 1)].astype(BF16)
            cb = lax.dot_general(cg, bg, NT_DIMS, preferred_element_type=F32)
            st_g = st[:, gs]
            yoff.append(jnp.dot(cg, st_g.astype(BF16), preferred_element_type=F32) * cm["eac_e"][:, gs])
            for j in range(hpg):
                h = hpg * g + j
                hs = slice(SSD_HEAD_DIM * h, SSD_HEAD_DIM * (h + 1))
                m = (cb * _decay(cm, h)).astype(BF16)
                yd[:, hs] = jnp.dot(m, xdt_b[:, hs], preferred_element_type=F32)
            new = lax.dot_general(bg, xd_b[:, gs], TN_DIMS, preferred_element_type=F32)
            st[:, gs] = st_g * gam_e[:, gs] + new
        y = yd[...] + jnp.concatenate(yoff, axis=1) + dsk_ref[...] * xsv
        y_ref[...] = y
        zz = z_ref[...]
        yg = y * (zz * _sigmoid(zz))
        outs = []
        for g in range(SSD_GROUPS):
            ygg = yg[:, gw * g:gw * (g + 1)]
            outs.append(ygg * _rms(ygg) * nw_ref[:, gw * g:gw * (g + 1)])
        ys_ref[...] = jnp.concatenate(outs, axis=1).astype(BF16)

    row = lambda d: pl.BlockSpec((None, ln, d), lambda bb, c: (bb, c, 0))
    return pl.pallas_call(
        body, name=name, grid=(b, nc),
        in_specs=[row(D_SSD), row(256), row(256), row(LANES), row(D_SSD), _full((1, LANES)), _full((1, LANES)),
                  _full((1, D_SSD)), _full((1, D_SSD)), _full((LANES, D_SSD))],
        out_specs=[row(D_SSD), row(D_SSD), pl.BlockSpec((None, None, SSD_STATE, D_SSD), lambda bb, c: (bb, c, 0, 0))],
        out_shape=[SDS((b, s, D_SSD), BF16), SDS((b, s, D_SSD), F32), SDS((b, nc, SSD_STATE, D_SSD), F32)],
        scratch_shapes=[pltpu.VMEM((SSD_STATE, D_SSD), F32), pltpu.VMEM((ln, D_SSD), F32)],
        compiler_params=_cparams(2))(xs, bm, cm_, misc, z, dtb, alog, dskip_e, norm_w, e_mat)


def ssd_bwd(dys, y, z, xs, bm, cm_, misc, prev, dtb, alog, dskip_e, norm_w, e_mat, et_mat, name):
    b, s, _ = xs.shape
    ln, nc = CHUNK, s // CHUNK
    gw = D_SSD // SSD_GROUPS
    hpg = SSD_HEADS // SSD_GROUPS

    def body(dys_ref, y_ref, z_ref, xs_ref, b_ref, c_ref, misc_ref, p_ref, dtb_ref, alog_ref, dsk_ref, nw_ref,
             e_ref, et_ref, dxs_ref, db_ref, dc_ref, dz_ref, ddt_ref, dnw_ref, ddsk_ref, ddtb_ref, dalog_ref,
             dst, dxd, dac_t):
        @pl.when(_first_step())
        def _():
            for r_ in (dnw_ref, ddsk_ref, ddtb_ref, dalog_ref):
                r_[...] = jnp.zeros_like(r_)

        @pl.when(pl.program_id(1) == 0)
        def _():
            dst[...] = jnp.zeros_like(dst)

        cm = _ssd_common(misc_ref, dtb_ref, alog_ref, e_ref)
        et = et_ref[...]
        squeeze = lambda t: _dot_01(t, et, 2)
        lane = lax.broadcasted_iota(I32, (ln, LANES), 1)
        sub = lax.broadcasted_iota(I32, (LANES, ln), 0)
        xsv = xs_ref[...]
        xdt = xsv * cm["dt_e"]
        xdt_b = xdt.astype(BF16)
        xd_b = (xdt * cm["del_e"]).astype(BF16)
        eac_e = cm["eac_e"]
        gam_e = eac_e[ln - 1:ln, :]

        yv, zz, dyo = y_ref[...], z_ref[...], dys_ref[...]
        sz = _sigmoid(zz)
        silu_z = zz * sz
        yg = yv * silu_z
        dyg, dnw = [], []
        for g in range(SSD_GROUPS):
            gs = slice(gw * g, gw * (g + 1))
            ygg = yg[:, gs]
            r = _rms(ygg)
            n = ygg * r
            dnw.append(jnp.sum(dyo[:, gs] * n, axis=0, keepdims=True))
            dyg.append(_rms_bwd(dyo[:, gs] * nw_ref[:, gs], n, r))
        dyg = jnp.concatenate(dyg, axis=1)
        dnw_ref[...] += jnp.concatenate(dnw, axis=1)
        dz_ref[...] = (dyg * yv * (sz * (1.0 + zz * (1.0 - sz)))).astype(BF16)
        dy = dyg * silu_z
        ddsk_ref[...] += jnp.sum(dy * xsv, axis=0, keepdims=True)
        dy_b = dy.astype(BF16)

        dacum = jnp.zeros((ln, LANES), F32)
        dac_t[...] = jnp.zeros_like(dac_t)
        w1, dgam = [], []
        for g in range(SSD_GROUPS):
            gs = slice(gw * g, gw * (g + 1))
            ss = slice(SSD_STATE * g, SSD_STATE * (g + 1))
            bg = b_ref[:, ss].astype(BF16)
            cg = c_ref[:, ss].astype(BF16)
            cb = lax.dot_general(cg, bg, NT_DIMS, preferred_element_type=F32)
            pt = p_ref[:, gs]
            pt_b = pt.astype(BF16)
            dst_g = dst[:, gs]
            dst_b = dst_g.astype(BF16)
            edy = (dy[:, gs] * eac_e[:, gs]).astype(BF16)
            dcg = lax.dot_general(edy, pt_b, NT_DIMS, preferred_element_type=F32)
            dpt = lax.dot_general(cg, edy, TN_DIMS, preferred_element_type=F32)
            yoff = jnp.dot(cg, pt_b, preferred_element_type=F32) * eac_e[:, gs]
            dxd_g = jnp.dot(bg, dst_b, preferred_element_type=F32)
            dbg = lax.dot_general(xd_b[:, gs], dst_b, NT_DIMS, preferred_element_type=F32)
            ddel = dxd_g * xdt[:, gs] * cm["del_e"][:, gs]
            w1.append(dy[:, gs] * yoff - ddel)
            dgam.append(jnp.sum(ddel, axis=0, keepdims=True) + jnp.sum(dst_g * pt, axis=0, keepdims=True) * gam_e[:, gs])
            dxd[:, gs] = dxd_g * cm["del_e"][:, gs]
            dst[:, gs] = dst_g * gam_e[:, gs] + dpt
            dcb = jnp.zeros((ln, ln), F32)
            for j in range(hpg):
                h = hpg * g + j
                hs = slice(SSD_HEAD_DIM * h, SSD_HEAD_DIM * (h + 1))
                lam = _decay(cm, h)
                m = cb * lam
                dm = lax.dot_general(dy_b[:, hs], xdt_b[:, hs], NT_DIMS, preferred_element_type=F32)
                dxd[:, hs] += lax.dot_general(m.astype(BF16), dy_b[:, hs], TN_DIMS, preferred_element_type=F32)
                dcb += dm * lam
                wl = dm * m
                dacum += jnp.where(lane == h, jnp.sum(wl, axis=1, keepdims=True), 0.0)
                dac_t[...] -= jnp.where(sub == h, jnp.sum(wl, axis=0, keepdims=True), 0.0)
            dcb_b = dcb.astype(BF16)
            dc_ref[:, ss] = dcg + jnp.dot(dcb_b, bg, preferred_element_type=F32)
            db_ref[:, ss] = dbg + lax.dot_general(dcb_b, cg, TN_DIMS, preferred_element_type=F32)

        dxdt = dxd[...]
        dxs_ref[...] = dy * dsk_ref[...] + dxdt * cm["dt_e"]
        dacum += squeeze(jnp.concatenate(w1, axis=1)) + dac_t[...].T
        dlast = squeeze(jnp.broadcast_to(jnp.concatenate(dgam, axis=1), (8, D_SSD)))[0:1, :]
        dacum += jnp.where(lax.broadcasted_iota(I32, (ln, LANES), 0) == ln - 1, dlast, 0.0)
        triu = (cm["ci"] >= cm["ri"]).astype(F32)
        da = jnp.dot(triu, dacum, preferred_element_type=F32, precision=HI)
        ddt = da * cm["a_neg"] + squeeze(dxdt * xsv)
        dalog_ref[...] += jnp.sum(da * cm["dt_s"], axis=0, keepdims=True) * cm["a_neg"]
        ddt_raw = jnp.where(lane < SSD_HEADS, ddt * _sigmoid(cm["pre"]), 0.0)
        ddt_ref[...] = ddt_raw
        ddtb_ref[...] += jnp.sum(ddt_raw, axis=0, keepdims=True)

    row = lambda d: pl.BlockSpec((None, ln, d), lambda bb, c: (bb, nc - 1 - c, 0))
    return pl.pallas_call(
        body, name=name, grid=(b, nc),
        in_specs=[row(D_SSD), row(D_SSD), row(D_SSD), row(D_SSD), row(256), row(256), row(LANES),
                  pl.BlockSpec((None, None, SSD_STATE, D_SSD), lambda bb, c: (bb, nc - 1 - c, 0, 0)),
                  _full((1, LANES)), _full((1, LANES)), _full((1, D_SSD)), _full((1, D_SSD)),
                  _full((LANES, D_SSD)), _full((D_SSD, LANES))],
        out_specs=[row(D_SSD), row(256), row(256), row(D_SSD), row(LANES),
                   _full((1, D_SSD)), _full((1, D_SSD)), _full((1, LANES)), _full((1, LANES))],
        out_shape=[SDS((b, s, D_SSD), F32), SDS((b, s, 256), F32), SDS((b, s, 256), F32), SDS((b, s, D_SSD), BF16),
                   SDS((b, s, LANES), F32), SDS((1, D_SSD), F32), SDS((1, D_SSD), F32), SDS((1, LANES), F32),
                   SDS((1, LANES), F32)],
        scratch_shapes=[pltpu.VMEM((SSD_STATE, D_SSD), F32), pltpu.VMEM((ln, D_SSD), F32), pltpu.VMEM((LANES, ln), F32)],
        compiler_params=_cparams(2))(dys, y, z, xs, bm, cm_, misc, prev, dtb, alog, dskip_e, norm_w, e_mat, et_mat)


def _rope(xv, cc, sp, sm):
    n = xv.shape[1]
    return xv * cc + pltpu.roll(xv, 16, 1) * sp + pltpu.roll(xv, n - 16, 1) * sm


def _rope_bwd(dy, cc, sp, sm):
    n = dy.shape[1]
    return dy * cc + pltpu.roll(dy * sp, n - 16, 1) + pltpu.roll(dy * sm, 16, 1)


def _tile8(t):
    return jnp.concatenate([t] * MLA_HEADS, axis=1)


def qkv_fwd(cq, ckv, misc, cc, sp, sm, qnw, kvnw, wuq_t, wukv_t, place, name):
    b, s, _ = cq.shape
    tm = min(512, s)
    hd = MLA_HEADS * HEAD_PAD

    def body(cq_ref, ckv_ref, misc_ref, cc_ref, sp_ref, sm_ref, qnw_ref, kvnw_ref, wq_ref, wkv_ref, pl_ref,
             q_ref, k_ref, v_ref, qn_ref, kvn_ref):
        cqv, ckvv = cq_ref[...], ckv_ref[...]
        qn = (cqv * _rms(cqv) * qnw_ref[...]).astype(BF16)
        kvn = (ckvv * _rms(ckvv) * kvnw_ref[...]).astype(BF16)
        qn_ref[...] = qn
        kvn_ref[...] = kvn
        cc1, sp1, sm1 = cc_ref[...], sp_ref[...], sm_ref[...]
        q = lax.dot_general(qn, wq_ref[...], NT_DIMS, preferred_element_type=F32)
        q_ref[...] = _rope(q, _tile8(cc1), _tile8(sp1), _tile8(sm1)).astype(BF16)
        kv = lax.dot_general(kvn, wkv_ref[...], NT_DIMS, preferred_element_type=F32)
        kr = jnp.dot(misc_ref[...], pl_ref[...], preferred_element_type=F32, precision=HI)
        kr = _rope(kr, cc1, sp1, sm1)
        k_ref[...] = (kv[:, 0:hd] + _tile8(kr)).astype(BF16)
        v_ref[...] = kv[:, hd:2 * hd].astype(BF16)

    return pl.pallas_call(
        body, name=name, grid=(b, s // tm),
        in_specs=[_row(tm, Q_LORA), _row(tm, KV_LORA), _row(tm, LANES), _row(tm, LANES), _row(tm, LANES), _row(tm, LANES),
                  _full((1, Q_LORA)), _full((1, KV_LORA)), _full(wuq_t.shape), _full(wukv_t.shape), _full((LANES, LANES))],
        out_specs=[_row(tm, hd), _row(tm, hd), _row(tm, hd), _row(tm, Q_LORA), _row(tm, KV_LORA)],
        out_shape=[SDS((b, s, hd), BF16)] * 3 + [SDS((b, s, Q_LORA), BF16), SDS((b, s, KV_LORA), BF16)],
        compiler_params=_cparams(2))(cq, ckv, misc, cc, sp, sm, qnw, kvnw, wuq_t, wukv_t, place)


def qkv_bwd(dq, dk, dv, ddt, cq, ckv, cc, sp, sm, qnw, kvnw, wuq_t, wukv_t, place_t, name):
    b, s, _ = cq.shape
    tm = min(512, s)
    hd = MLA_HEADS * HEAD_PAD

    def body(dq_ref, dk_ref, dv_ref, ddt_ref, cq_ref, ckv_ref, cc_ref, sp_ref, sm_ref, qnw_ref, kvnw_ref,
             wq_ref, wkv_ref, plt_ref, dcq_ref, dckv_ref, dmisc_ref, dqp_ref, dkv_ref, dqnw_ref, dkvnw_ref):
        @pl.when(_first_step())
        def _():
            dqnw_ref[...] = jnp.zeros_like(dqnw_ref)
            dkvnw_ref[...] = jnp.zeros_like(dkvnw_ref)
        cc1, sp1, sm1 = cc_ref[...], sp_ref[...], sm_ref[...]
        dqp = _rope_bwd(dq_ref[...], _tile8(cc1), _tile8(sp1), _tile8(sm1)).astype(BF16)
        dqp_ref[...] = dqp
        dkf = dk_ref[...]
        dkv_b = jnp.concatenate([dkf, dv_ref[...]], axis=1).astype(BF16)
        dkv_ref[...] = dkv_b
        dkr = dkf[:, 0:HEAD_PAD]
        for h in range(1, MLA_HEADS):
            dkr += dkf[:, HEAD_PAD * h:HEAD_PAD * (h + 1)]
        dkr = _rope_bwd(dkr, cc1, sp1, sm1)
        dmisc_ref[...] = (jnp.dot(dkr, plt_ref[...], preferred_element_type=F32, precision=HI) + ddt_ref[...]).astype(BF16)

        def norm_bwd(dn_w, xv, w_ref, dw_ref, dx_ref):
            r = _rms(xv)
            n = xv * r
            dw_ref[...] += jnp.sum(dn_w * n, axis=0, keepdims=True)
            dx_ref[...] = _rms_bwd(dn_w * w_ref[...], n, r).astype(BF16)

        norm_bwd(jnp.dot(dqp, wq_ref[...], preferred_element_type=F32), cq_ref[...], qnw_ref, dqnw_ref, dcq_ref)
        norm_bwd(jnp.dot(dkv_b, wkv_ref[...], preferred_element_type=F32), ckv_ref[...], kvnw_ref, dkvnw_ref, dckv_ref)

    return pl.pallas_call(
        body, name=name, grid=(b, s // tm),
        in_specs=[_row(tm, hd), _row(tm, hd), _row(tm, hd), _row(tm, LANES), _row(tm, Q_LORA), _row(tm, KV_LORA),
                  _row(tm, LANES), _row(tm, LANES), _row(tm, LANES), _full((1, Q_LORA)), _full((1, KV_LORA)),
                  _full(wuq_t.shape), _full(wukv_t.shape), _full((LANES, LANES))],
        out_specs=[_row(tm, Q_LORA), _row(tm, KV_LORA), _row(tm, LANES), _row(tm, hd), _row(tm, 2 * hd),
                   _full((1, Q_LORA)), _full((1, KV_LORA))],
        out_shape=[SDS((b, s, Q_LORA), BF16), SDS((b, s, KV_LORA), BF16), SDS((b, s, LANES), BF16),
                   SDS((b, s, hd), BF16), SDS((b, s, 2 * hd), BF16), SDS((1, Q_LORA), F32), SDS((1, KV_LORA), F32)],
        compiler_params=_cparams(2))(dq, dk, dv, ddt, cq, ckv, cc, sp, sm, qnw, kvnw, wuq_t, wukv_t, place_t)


ATT_SCALE = 1.0 / math.sqrt(QK_DIM)
LOG2E = math.log2(math.e)
ATT_SCALE_LOG2E = ATT_SCALE * LOG2E


ATT_HEADS_PER_STEP = 4
ATT_HEADS_PER_STEP_BWD = 2


def _att_tile(s):
    return min(512, s)


def flash_fwd(q, k, v, name):
    b, s, hd = q.shape
    t = _att_tile(s)
    nb = s // t
    th = t // 2
    vt = v.reshape(b, nb, t, MLA_HEADS, HEAD_PAD).transpose(0, 3, 1, 4, 2)

    hps = ATT_HEADS_PER_STEP
    hw = hps * HEAD_PAD

    def body(q_ref, k_ref, vt_ref, o_ref, lse_ref, m_s, l_s, acc):
        i = pl.program_id(2)
        m_s[...] = jnp.full_like(m_s, -jnp.inf)
        l_s[...] = jnp.zeros_like(l_s)
        acc[...] = jnp.zeros_like(acc)

        def update(j, diagonal):
            ks = pl.ds(pl.multiple_of(j * t, t), t)
            chains = [(hh, half) for hh in range(hps) for half in range(2)]
            lanes = lambda hh: slice(HEAD_PAD * hh, HEAD_PAD * (hh + 1))
            cols = lambda half: slice(th * half, th * (half + 1))
            sts = {}
            for hh, half in chains:
                st = lax.dot_general(k_ref[ks, lanes(hh)], q_ref[cols(half), lanes(hh)], NT_DIMS,
                                     preferred_element_type=F32)
                if diagonal:
                    row = lax.broadcasted_iota(I32, (t, th), 0)
                    col = lax.broadcasted_iota(I32, (t, th), 1) + th * half
                    st = jnp.where(row <= col, st, -jnp.inf)
                sts[hh, half] = st
            pts, alphas = {}, {}
            for hh, half in chains:
                st, cs = sts[hh, half], cols(half)
                m_prev = m_s[hh, :, cs]
                m_new = jnp.maximum(m_prev, jnp.max(st, axis=0, keepdims=True))
                alpha = jnp.exp2((m_prev - m_new) * ATT_SCALE_LOG2E)
                pt = jnp.exp2((st - m_new) * ATT_SCALE_LOG2E)
                l_s[hh, :, cs] = alpha * l_s[hh, :, cs] + jnp.sum(pt, axis=0, keepdims=True)
                m_s[hh, :, cs] = m_new
                pts[hh, half], alphas[hh, half] = pt.astype(BF16), alpha
            for hh, half in chains:
                cs = cols(half)
                acc[hh, :, cs] = alphas[hh, half] * acc[hh, :, cs] + jnp.dot(vt_ref[hh, j], pts[hh, half],
                                                                             preferred_element_type=F32)

        def step(j, carry):
            update(j, False)
            return carry

        lax.fori_loop(0, i, step, 0)
        update(i, True)
        for hh in range(hps):
            o_ref[:, HEAD_PAD * hh:HEAD_PAD * (hh + 1)] = (acc[hh] / l_s[hh]).T
            lse_ref[hh] = m_s[hh] * ATT_SCALE + jnp.log(l_s[hh])

    qs = pl.BlockSpec((None, t, hw), lambda bb, h, i: (bb, i, h))
    ks = pl.BlockSpec((None, s, hw), lambda bb, h, i: (bb, 0, h))
    vs = pl.BlockSpec((None, hps, nb, HEAD_PAD, t), lambda bb, h, i: (bb, h, 0, 0, 0))
    ls = pl.BlockSpec((None, hps, None, 1, t), lambda bb, h, i: (bb, h, i, 0, 0))
    return pl.pallas_call(
        body, name=name, grid=(b, MLA_HEADS // hps, nb),
        in_specs=[qs, ks, vs], out_specs=[qs, ls],
        out_shape=[SDS((b, s, hd), F32), SDS((b, MLA_HEADS, nb, 1, t), F32)],
        scratch_shapes=[pltpu.VMEM((hps, 1, t), F32), pltpu.VMEM((hps, 1, t), F32), pltpu.VMEM((hps, HEAD_PAD, t), F32)],
        compiler_params=_cparams(3))(q, k, vt)


def flash_bwd(q, k, v, do, lse, dlt, name):
    b, s, hd = q.shape
    t = _att_tile(s)
    nb = s // t
    th = t // 2
    lse_r = lse
    dlt_r = dlt.reshape(b, MLA_HEADS, nb, 1, t)

    hps = ATT_HEADS_PER_STEP_BWD
    hw = hps * HEAD_PAD

    def body(q_ref, k_ref, v_ref, do_ref, lse_ref, dlt_ref, dq_ref, dk_ref, dv_ref):
        dq_ref[...] = jnp.zeros_like(dq_ref)
        dk_ref[...] = jnp.zeros_like(dk_ref)
        dv_ref[...] = jnp.zeros_like(dv_ref)

        def tile(j, i, diagonal):
            qs = pl.ds(pl.multiple_of(i * t, t), t)
            chains = [(hh, half) for hh in range(hps) for half in range(2)]
            lanes = lambda hh: slice(HEAD_PAD * hh, HEAD_PAD * (hh + 1))
            keys = lambda half: pl.ds(pl.multiple_of(j * t + th * half, th), th)
            sts, dpts = {}, {}
            for hh, half in chains:
                ls_, ks = lanes(hh), keys(half)
                st = lax.dot_general(k_ref[ks, ls_], q_ref[qs, ls_], NT_DIMS, preferred_element_type=F32)
                if diagonal:
                    row = lax.broadcasted_iota(I32, (th, t), 0) + th * half
                    col = lax.broadcasted_iota(I32, (th, t), 1)
                    st = jnp.where(row <= col, st, -jnp.inf)
                sts[hh, half] = st
                dpts[hh, half] = lax.dot_general(v_ref[ks, ls_], do_ref[qs, ls_], NT_DIMS, preferred_element_type=F32)
            pts, dsts = {}, {}
            for hh, half in chains:
                pt = jnp.exp2(sts[hh, half] * ATT_SCALE_LOG2E - lse_ref[hh, i] * LOG2E)
                pts[hh, half] = pt.astype(BF16)
                dsts[hh, half] = (pt * (dpts[hh, half] - dlt_ref[hh, i])).astype(BF16)
            for hh in range(hps):
                ls_ = lanes(hh)
                dq_acc = None
                for half in range(2):
                    ks = keys(half)
                    dv_ref[ks, ls_] += jnp.dot(pts[hh, half], do_ref[qs, ls_], preferred_element_type=F32)
                    dk_ref[ks, ls_] += jnp.dot(dsts[hh, half], q_ref[qs, ls_], preferred_element_type=F32)
                    part = lax.dot_general(dsts[hh, half], k_ref[ks, ls_], TN_DIMS, preferred_element_type=F32)
                    dq_acc = part if dq_acc is None else dq_acc + part
                dq_ref[qs, ls_] += dq_acc

        def key_tile(j, carry):
            tile(j, j, True)

            def query_tile(i, c2):
                tile(j, i, False)
                return c2

            lax.fori_loop(j + 1, nb, query_tile, 0)
            return carry

        lax.fori_loop(0, nb, key_tile, 0)
        dq_ref[...] *= ATT_SCALE
        dk_ref[...] *= ATT_SCALE

    hs = pl.BlockSpec((None, s, hw), lambda bb, h: (bb, 0, h))
    ls = pl.BlockSpec((None, hps, nb, 1, t), lambda bb, h: (bb, h, 0, 0, 0))
    return pl.pallas_call(
        body, name=name, grid=(b, MLA_HEADS // hps),
        in_specs=[hs, hs, hs, hs, ls, ls], out_specs=[hs, hs, hs],
        out_shape=[SDS((b, s, hd), F32)] * 3, compiler_params=_cparams(2))(q, k, v, do, lse_r, dlt_r)


def out_proj(ys, attn, mnw, wo, x, gate, above, name):
    b, s, d = x.shape
    tm = min(512, s)

    def body(ys_ref, at_ref, mnw_ref, wo_ref, x_ref, g_ref, nw_ref, sc_ref, sh_ref, xn_ref, o_ref, ym_ref, h_ref):
        av = at_ref[...]
        ym = (av * _rms(av) * mnw_ref[...]).astype(BF16)
        ym_ref[...] = ym
        o = jnp.dot(ys_ref[...], wo_ref[0:D_SSD, :], preferred_element_type=F32)
        o += jnp.dot(ym, wo_ref[D_SSD:2 * D_SSD, :], preferred_element_type=F32)
        xn = x_ref[...] + g_ref[...] * o
        xn_ref[...] = xn
        o_ref[...] = o.astype(BF16)
        h_ref[...] = _norm_mod_tile(xn, nw_ref, sc_ref, sh_ref)

    return pl.pallas_call(
        body, name=name, grid=(b, s // tm),
        in_specs=[_row(tm, D_SSD), _row(tm, D_SSD), _full((1, D_SSD)), _full(wo.shape), _row(tm, d), _bvec(d),
                  _full((1, d)), _bvec(d), _bvec(d)],
        out_specs=[_row(tm, d), _row(tm, d), _row(tm, D_SSD), _row(tm, d)],
        out_shape=[SDS((b, s, d), F32), SDS((b, s, d), BF16), SDS((b, s, D_SSD), BF16), SDS((b, s, d), BF16)],
        compiler_params=_cparams(2))(ys, attn, mnw, wo, x, gate, *above)


def out_proj_bwd(dout, attn, mnw, wo, name):
    b, s, d = dout.shape
    tm = min(512, s)

    def body(do_ref, at_ref, mnw_ref, wo_ref, dys_ref, dat_ref, dlt_ref, dw_ref):
        @pl.when(_first_step())
        def _():
            dw_ref[...] = jnp.zeros_like(dw_ref)
        dov = do_ref[...]
        dys_ref[...] = lax.dot_general(dov, wo_ref[0:D_SSD, :], NT_DIMS, preferred_element_type=F32)
        dym = lax.dot_general(dov, wo_ref[D_SSD:2 * D_SSD, :], NT_DIMS, preferred_element_type=F32)
        av = at_ref[...]
        r = _rms(av)
        n = av * r
        dw_ref[...] += jnp.sum(dym * n, axis=0, keepdims=True)
        dat = _rms_bwd(dym * mnw_ref[...], n, r)
        dat_ref[...] = dat.astype(BF16)
        prod = dat * av
        for h in range(MLA_HEADS):
            dlt_ref[h] = jnp.sum(prod[:, HEAD_PAD * h:HEAD_PAD * (h + 1)], axis=1, keepdims=True)

    return pl.pallas_call(
        body, name=name, grid=(b, s // tm),
        in_specs=[_row(tm, d), _row(tm, D_SSD), _full((1, D_SSD)), _full(wo.shape)],
        out_specs=[_row(tm, D_SSD), _row(tm, D_SSD),
                   pl.BlockSpec((None, MLA_HEADS, tm, 1), lambda bb, i: (bb, 0, i, 0)), _full((1, D_SSD))],
        out_shape=[SDS((b, s, D_SSD), F32), SDS((b, s, D_SSD), BF16), SDS((b, MLA_HEADS, s, 1), F32),
                   SDS((1, D_SSD), F32)],
        compiler_params=_cparams(2))(dout, attn, mnw, wo)


def adaln_fwd(c_all, w_ada, b_ada, name):
    nb, d = c_all.shape
    n = w_ada.shape[1]

    def body(c_ref, w_ref, b_ref, m_ref, ca_ref):
        cv = c_ref[...]
        ca = (cv * _sigmoid(cv)).astype(BF16)
        ca_ref[...] = ca
        m_ref[...] = jnp.dot(ca, w_ref[...].astype(BF16), preferred_element_type=F32) + b_ref[...]

    return pl.pallas_call(
        body, name=name, out_shape=[SDS((nb, n), F32), SDS((nb, d), BF16)],
        compiler_params=pltpu.CompilerParams(vmem_limit_bytes=VMEM_LIMIT))(c_all, w_ada, b_ada)


def adaln_bwd(c_act, dmod_cols, name):
    d, n = c_act.shape[1], dmod_cols.shape[1]

    def body(c_ref, dm_ref, gw_ref):
        gw_ref[...] = lax.dot_general(c_ref[...], dm_ref[...].astype(BF16), TN_DIMS, preferred_element_type=F32)

    return pl.pallas_call(
        body, name=name, out_shape=SDS((d, n), F32),
        compiler_params=pltpu.CompilerParams(vmem_limit_bytes=VMEM_LIMIT))(c_act, dmod_cols)


def sum_rows(x, name):
    def body(x_ref, o_ref):
        o_ref[...] = jnp.sum(x_ref[...], axis=0, keepdims=True)
    return pl.pallas_call(body, name=name, out_shape=SDS((1, x.shape[1]), F32))(x)


def squeeze_heads(x, et_mat, name):
    def body(x_ref, et_ref, o_ref):
        xv = jnp.broadcast_to(x_ref[...], (8, x.shape[1]))
        o_ref[...] = _dot_01(xv, et_ref[...], 3)[0:1, :]
    return pl.pallas_call(body, name=name, out_shape=SDS((1, LANES), F32))(x, et_mat)


def sum_blocks(x, name):
    n, r, c = x.shape
    tr = next(cand for cand in (256, 128, 64, 32, 16, 8) if r % cand == 0)

    def body(x_ref, o_ref):
        acc = x_ref[0].astype(F32)
        for k in range(1, n):
            acc += x_ref[k].astype(F32)
        o_ref[...] = acc

    return pl.pallas_call(
        body, name=name, grid=(r // tr,), in_specs=[pl.BlockSpec((n, tr, c), lambda i: (0, i, 0))],
        out_specs=pl.BlockSpec((tr, c), lambda i: (i, 0)), out_shape=SDS((r, c), F32),
        compiler_params=_cparams(1))(x)


def _adam_math(w, g, m, v):
    m = ADAM_B1 * m + (1.0 - ADAM_B1) * g
    v = ADAM_B2 * v + (1.0 - ADAM_B2) * (g * g)
    m_hat = m / (1.0 - ADAM_B1 ** ADAM_STEP)
    v_hat = v / (1.0 - ADAM_B2 ** ADAM_STEP)
    return -ADAM_LR * (m_hat / (jnp.sqrt(v_hat) + ADAM_EPS) + ADAM_WD * w), m, v


def adamw(w, g, m, v, name):
    r, c = w.shape
    tr = r
    for cand in (512, 256, 128, 64, 32, 16, 8):
        if r % cand == 0 and cand * c * 4 <= 2 * 1024 * 1024:
            tr = cand
            break

    def body(w_ref, g_ref, m_ref, v_ref, d_ref, mo_ref, vo_ref):
        d_ref[...], mo_ref[...], vo_ref[...] = _adam_math(w_ref[...], g_ref[...], m_ref[...], v_ref[...])

    spec = pl.BlockSpec((tr, c), lambda i: (i, 0))
    return pl.pallas_call(
        body, name=name, grid=(r // tr,), in_specs=[spec] * 4, out_specs=[spec] * 3,
        out_shape=[SDS((r, c), F32)] * 3, compiler_params=_cparams(1))(w, g, m, v)


PACK = {"ffn1_w_gate": (352, 352), "ffn1_w_up": (352, 352), "ffn1_w_down": (352, 352),
        "ffn2_w_gate": (352, 352), "ffn2_w_up": (352, 352), "ffn2_w_down": (352, 352),
        "w_out": (256, 256), "w_in": (406, 416), "w_ukv": (48, 48), "w_uq": (36, 48)}
TRANSPOSED = ("ffn1_w_gate", "ffn1_w_up", "ffn2_w_gate", "ffn2_w_up", "w_in", "w_ukv", "w_uq")
GATHER_GROUPS = (("ffn1_w_gate", "ffn1_w_up"), ("ffn1_w_down",),
                 ("w_in", "w_ukv", "w_uq", "w_out", "ffn2_w_gate", "ffn2_w_up", "ffn2_w_down"))
GRAD_GROUPS = (("ffn2", ("ffn2_w_gate", "ffn2_w_up", "ffn2_w_down")), ("mixer", ("w_out", "w_in", "w_ukv", "w_uq")),
               ("ffn1_up", ("ffn1_w_gate", "ffn1_w_up")), ("ffn1_down", ("ffn1_w_down",)))


def _pack_offsets(names):
    off, o = {}, 0
    for n in names:
        off[n] = (o, PACK[n][0])
        o += PACK[n][1]
    return off, o


def _shard_to_rows(name, w):
    w = w[0]
    if name in TRANSPOSED:
        w = w.T
    return w.reshape(-1, D_MODEL)


def _rows_to_shard(name, rows, like):
    shp = like.shape[1:]
    if name in TRANSPOSED:
        return rows.reshape(shp[1], shp[0]).T[None]
    return rows.reshape(shp)[None]


def _pack_shards(ws, names, dtype):
    parts = []
    for name in names:
        real, padded = PACK[name]
        rows = _shard_to_rows(name, ws[name]).astype(dtype)
        if padded > real:
            rows = jnp.pad(rows, ((0, padded - real), (0, 0)))
        parts.append(rows)
    return jnp.concatenate(parts, axis=0)


def _grad_rows(name, gw):
    real, padded = PACK[name]
    if name == "w_in":
        rows = _in_proj_rows_inv(gw).reshape(N_DEV, -1, D_MODEL)
    elif name == "w_ukv":
        hd = MLA_HEADS * HEAD_PAD
        rows = jnp.concatenate([gw[:hd].reshape(MLA_HEADS, HEAD_PAD, KV_LORA)[:, :QK_NOPE],
                                gw[hd:].reshape(MLA_HEADS, V_HEAD, KV_LORA)], axis=1).reshape(N_DEV, -1, D_MODEL)
    elif name == "w_uq":
        rows = gw.reshape(MLA_HEADS, HEAD_PAD, Q_LORA)[:, :QK_DIM].reshape(N_DEV, -1, D_MODEL)
    else:
        rows = gw.reshape(N_DEV, -1, D_MODEL)
    if padded > real:
        rows = jnp.pad(rows, ((0, 0), (0, padded - real), (0, 0)))
    return rows


def _pack_rows(arrs):
    parts = []
    for a in arrs:
        flat = a.reshape(-1).astype(F32)
        pad = (-flat.shape[0]) % D_MODEL
        if pad:
            flat = jnp.pad(flat, (0, pad))
        parts.append(flat.reshape(-1, D_MODEL))
    out = jnp.concatenate(parts, axis=0)
    pad = (-out.shape[0]) % 8
    if pad:
        out = jnp.pad(out, ((0, pad), (0, 0)))
    return out


def _unpack_rows(packed, shapes):
    out, row = [], 0
    for shp in shapes:
        n = math.prod(shp)
        nrow = -(-n // D_MODEL)
        out.append(packed[row:row + nrow].reshape(-1)[:n].reshape(shp))
        row += nrow
    return out


def _in_proj_rows(w_t):
    return jnp.concatenate([w_t[0:2560], w_t[2576:2960], w_t[2960:3216], w_t[2560:2576], w_t[3216:3248],
                            jnp.zeros((D_IN_PAD - D_IN, D_MODEL), w_t.dtype)], axis=0)


def _in_proj_rows_inv(d):
    return jnp.concatenate([d[0:2560], d[3200:3216], d[2560:2944], d[2944:3200], d[3216:3248]], axis=0)


def _rope_tables(positions):
    inv_freq = ROPE_THETA ** (-jnp.arange(0, QK_ROPE, 2, dtype=F32) / QK_ROPE)
    ang = positions[..., None].astype(F32) * inv_freq
    cos, sin = jnp.cos(ang), jnp.sin(ang)
    one = jnp.ones(ang.shape[:2] + (QK_NOPE,), F32)
    zero = jnp.zeros_like(one)
    z16, z32, o32 = zero[..., :16], zero[..., :32], one[..., :32]
    cc = jnp.concatenate([one, cos, cos, o32], axis=-1)
    sp = jnp.concatenate([zero, z16, sin, z32], axis=-1)
    sm = jnp.concatenate([zero, -sin, z16, z32], axis=-1)
    return cc, sp, sm


def weight_views(gathered):
    def _seg(name):
        names, g = next((names, g) for names, g in zip(GATHER_GROUPS, gathered) if name in names)
        o, r = _pack_offsets(names)[0][name]
        return g[:, o:o + r]

    full = lambda name: _seg(name).reshape(-1, D_MODEL)
    ukv = _seg("w_ukv").reshape(MLA_HEADS, QK_NOPE + V_HEAD, KV_LORA)
    wukv_t = jnp.concatenate([jnp.pad(ukv[:, :QK_NOPE], ((0, 0), (0, HEAD_PAD - QK_NOPE), (0, 0))).reshape(-1, KV_LORA),
                              ukv[:, QK_NOPE:].reshape(-1, KV_LORA)], axis=0)
    uq = _seg("w_uq").reshape(MLA_HEADS, QK_DIM, Q_LORA)
    wuq_t = jnp.pad(uq, ((0, 0), (0, HEAD_PAD - QK_DIM), (0, 0))).reshape(-1, Q_LORA)
    return dict(wg1_t=full("ffn1_w_gate"), wu1_t=full("ffn1_w_up"), wd1=full("ffn1_w_down"),
                wg2_t=full("ffn2_w_gate"), wu2_t=full("ffn2_w_up"), wd2=full("ffn2_w_down"),
                wo=full("w_out"), win_t=_in_proj_rows(full("w_in")), wukv_t=wukv_t, wuq_t=wuq_t)


def _ffn_bwd(tag, dxn, do, dgate, x, h, gg, uu, a, sc, norm_w, wg_t, wu_t, wd, below):
    f2 = wd.shape[0] // 2
    dgg, duu = ffn_dact(do, wd, gg, uu, tag + "_dact")
    dwg_t = mm_tn(dgg, h, f2, D_MODEL, tag + "_dwg")
    dwu_t = mm_tn(duu, h, f2, D_MODEL, tag + "_dwu")
    dwd = mm_tn(a, do, f2, D_MODEL, tag + "_dwd")
    dx, dsc, dsh, dnw, *nxt = dh_norm_bwd([dgg, duu], [wg_t, wu_t], x, dxn, norm_w, sc, tag + "_dh", below)
    return dx, (dsh, dsc, dgate), dnw, (dwg_t, dwu_t, dwd), nxt


def local_step(x, tgt, positions, mod, wv, p):
    nb, s, d = x.shape
    sh1, sc1, g1, sh2, sc2, g2, sh3, sc3, g3 = mod
    cc, sp, sm = _rope_tables(positions)
    lane_head = jnp.arange(D_SSD, dtype=I32)[None, :] // SSD_HEAD_DIM
    e_mat = (lane_head == jnp.arange(LANES, dtype=I32)[:, None]).astype(BF16)
    et_mat = e_mat.T
    rr, cl = jnp.arange(LANES, dtype=I32)[:, None], jnp.arange(LANES, dtype=I32)[None, :]
    place = ((cl == rr + (QK_NOPE - SSD_HEADS)) & (rr >= SSD_HEADS) & (rr < SSD_HEADS + QK_ROPE)).astype(F32)
    dtb = jnp.pad(p["dt_bias"], ((0, 0), (0, LANES - SSD_HEADS)))
    alog = jnp.pad(p["a_log"], ((0, 0), (0, LANES - SSD_HEADS)))
    dskip_e = jnp.repeat(p["d_skip"], SSD_HEAD_DIM, axis=1)

    h1 = norm_mod(x, p["norm_ffn1"], sc1, sh1, "ffn1_norm")
    gg1, uu1, a1 = ffn_up(h1, wv["wg1_t"], wv["wu1_t"], "ffn1_up")
    x1, o1, h2 = ffn_down(a1, wv["wd1"], x, g1, 0.5, "ffn1_down", (p["norm_mix"], sc2, sh2))
    z, u, cq, ckv, misc = in_proj(h2, wv["win_t"], "in_proj")
    xs, bm, cm_ = conv_fwd(u, p["conv_w"], p["conv_b"], "conv_fwd")
    ys, y, prev = ssd_fwd(xs, bm, cm_, misc, z, dtb, alog, dskip_e, p["ssd_norm_w"], e_mat, "ssd_fwd")
    q, k, v, qn, kvn = qkv_fwd(cq, ckv, misc, cc, sp, sm, p["q_norm_w"], p["kv_norm_w"], wv["wuq_t"], wv["wukv_t"],
                               place, "qkv_fwd")
    attn, lse = flash_fwd(q, k, v, "flash_fwd")
    x2, o2, ym, h3 = out_proj(ys, attn, p["mla_norm_w"], wv["wo"], x1, g2, (p["norm_ffn2"], sc3, sh3), "out_proj")
    gg3, uu3, a3 = ffn_up(h3, wv["wg2_t"], wv["wu2_t"], "ffn2_up")
    x3, o3 = ffn_down(a3, wv["wd2"], x2, g3, 0.5, "ffn2_down")
    loss, dx3, dnfin, do3, dg3 = final_loss(x3, p["norm_final"], tgt, (o3, g3, 0.5), "final_loss")

    dx2, dmod3, dnf2, (dwg2, dwu2, dwd2), (dout, dg2) = _ffn_bwd(
        "ffn2", dx3, do3, dg3, x2, h3, gg3, uu3, a3, sc3, p["norm_ffn2"], wv["wg2_t"], wv["wu2_t"], wv["wd2"],
        (o2, g2, 1.0))
    dys, dattn, dlt, dmlan = out_proj_bwd(dout, attn, p["mla_norm_w"], wv["wo"], "out_proj_bwd")
    dwo = jnp.concatenate([mm_tn(ys, dout, D_SSD, D_MODEL, "dwo_ssd"), mm_tn(ym, dout, D_SSD, D_MODEL, "dwo_mla")], axis=0)
    dxs, dbm, dcm, dz, ddt, dssdn, ddsk_lane, ddtb, dalog = ssd_bwd(
        dys, y, z, xs, bm, cm_, misc, prev, dtb, alog, dskip_e, p["ssd_norm_w"], e_mat, et_mat, "ssd_bwd")
    dq, dk, dv = flash_bwd(q, k, v, dattn, lse, dlt, "flash_bwd")
    dcq, dckv, dmisc, dqp, dkvc, dqn, dkvn = qkv_bwd(dq, dk, dv, ddt, cq, ckv, cc, sp, sm, p["q_norm_w"], p["kv_norm_w"],
                                                     wv["wuq_t"], wv["wukv_t"], place.T, "qkv_bwd")
    dwuq = mm_tn(dqp, qn, MLA_HEADS * HEAD_PAD, Q_LORA, "dwuq")
    dwukv = mm_tn(dkvc, kvn, MLA_HEADS * HEAD_PAD, KV_LORA, "dwukv")
    dvv, dconv = conv_bwd_a(dxs, dbm, dcm, u, p["conv_w"], p["conv_b"], "conv_bwd_a")
    dconv = conv_grads_fold(dconv, "conv_grads_fold")
    du = conv_bwd_b(dvv, p["conv_w"], "conv_bwd_b")
    dproj = jnp.concatenate([dz, du, dcq, dckv, dmisc], axis=-1)
    dwin = mm_tn(dproj, h2, D_IN_PAD // 2, D_MODEL, "dwin")
    dx1, dsc2, dsh2, dnmix, do1, dg1 = dh_norm_bwd([dproj], [wv["win_t"]], x1, dx2, p["norm_mix"], sc2, "mix_dh",
                                                   (o1, g1, 0.5))
    dx0, dmod1, dnf1, (dwg1, dwu1, dwd1), _ = _ffn_bwd(
        "ffn1", dx1, do1, dg1, x, h1, gg1, uu1, a1, sc1, p["norm_ffn1"], wv["wg1_t"], wv["wu1_t"], wv["wd1"], None)

    dmod = jnp.concatenate([*dmod1, dsh2, dsc2, dg2, *dmod3], axis=1).reshape(nb, N_MOD * d)
    return dict(
        loss=loss, dx=dx0, dmod=dmod, norm_ffn1=dnf1, norm_mix=dnmix, norm_ffn2=dnf2, norm_final=dnfin,
        ssd_norm_w=dssdn, mla_norm_w=dmlan, q_norm_w=dqn, kv_norm_w=dkvn,
        dt_bias=ddtb[:, :SSD_HEADS], a_log=dalog[:, :SSD_HEADS],
        d_skip=squeeze_heads(ddsk_lane, et_mat, "d_skip_heads")[:, :SSD_HEADS],
        conv_b=dconv[4:5], conv_w=dconv[0:4],
        gw=dict(ffn1_w_gate=dwg1, ffn1_w_up=dwu1, ffn1_w_down=dwd1, ffn2_w_gate=dwg2, ffn2_w_up=dwu2, ffn2_w_down=dwd2,
                w_out=dwo, w_in=dwin, w_ukv=dwukv, w_uq=dwuq))


def kernel(x, c, positions, w_ada, b_ada, norm_ffn1, ffn1_w_gate, ffn1_w_up, ffn1_w_down, norm_mix, w_in, conv_w, conv_b, dt_bias, a_log, d_skip, ssd_norm_w, q_norm_w, w_uq, kv_norm_w, w_ukv, mla_norm_w, w_out, norm_ffn2, ffn2_w_gate, ffn2_w_up, ffn2_w_down, norm_final, loss_target, m_w_ada, m_b_ada, m_norm_ffn1, m_ffn1_w_gate, m_ffn1_w_up, m_ffn1_w_down, m_norm_mix, m_w_in, m_conv_w, m_conv_b, m_dt_bias, m_a_log, m_d_skip, m_ssd_norm_w, m_q_norm_w, m_w_uq, m_kv_norm_w, m_w_ukv, m_mla_norm_w, m_w_out, m_norm_ffn2, m_ffn2_w_gate, m_ffn2_w_up, m_ffn2_w_down, m_norm_final, v_w_ada, v_b_ada, v_norm_ffn1, v_ffn1_w_gate, v_ffn1_w_up, v_ffn1_w_down, v_norm_mix, v_w_in, v_conv_w, v_conv_b, v_dt_bias, v_a_log, v_d_skip, v_ssd_norm_w, v_q_norm_w, v_w_uq, v_kv_norm_w, v_w_ukv, v_mla_norm_w, v_w_out, v_norm_ffn2, v_ffn2_w_gate, v_ffn2_w_up, v_ffn2_w_down, v_norm_final):
    names = ["w_ada", "b_ada", "norm_ffn1", "ffn1_w_gate", "ffn1_w_up", "ffn1_w_down", "norm_mix", "w_in", "conv_w",
             "conv_b", "dt_bias", "a_log", "d_skip", "ssd_norm_w", "q_norm_w", "w_uq", "kv_norm_w", "w_ukv",
             "mla_norm_w", "w_out", "norm_ffn2", "ffn2_w_gate", "ffn2_w_up", "ffn2_w_down", "norm_final"]
    W = dict(zip(names, (w_ada, b_ada, norm_ffn1, ffn1_w_gate, ffn1_w_up, ffn1_w_down, norm_mix, w_in, conv_w, conv_b, dt_bias, a_log, d_skip, ssd_norm_w, q_norm_w, w_uq, kv_norm_w, w_ukv, mla_norm_w, w_out, norm_ffn2, ffn2_w_gate, ffn2_w_up, ffn2_w_down, norm_final)))
    M = dict(zip(names, (m_w_ada, m_b_ada, m_norm_ffn1, m_ffn1_w_gate, m_ffn1_w_up, m_ffn1_w_down, m_norm_mix, m_w_in, m_conv_w, m_conv_b, m_dt_bias, m_a_log, m_d_skip, m_ssd_norm_w, m_q_norm_w, m_w_uq, m_kv_norm_w, m_w_ukv, m_mla_norm_w, m_w_out, m_norm_ffn2, m_ffn2_w_gate, m_ffn2_w_up, m_ffn2_w_down, m_norm_final)))
    V = dict(zip(names, (v_w_ada, v_b_ada, v_norm_ffn1, v_ffn1_w_gate, v_ffn1_w_up, v_ffn1_w_down, v_norm_mix, v_w_in, v_conv_w, v_conv_b, v_dt_bias, v_a_log, v_d_skip, v_ssd_norm_w, v_q_norm_w, v_w_uq, v_kv_norm_w, v_w_ukv, v_mla_norm_w, v_w_out, v_norm_ffn2, v_ffn2_w_gate, v_ffn2_w_up, v_ffn2_w_down, v_norm_final)))

    nb, s, d = x.shape
    me = 4 * lax.axis_index("x") + 2 * lax.axis_index("y") + lax.axis_index("c")
    n_ada = w_ada.shape[2]

    cshape = [(nb, d), conv_w.shape[1:]]
    cg = all_gather8(_pack_rows([c, conv_w[0]]), "gather_c")
    c_all = jnp.stack([_unpack_rows(cg[k], cshape)[0] for k in range(N_DEV)]).reshape(N_DEV * nb, d)
    conv_w_full = jnp.concatenate([_unpack_rows(cg[k], cshape)[1] for k in range(N_DEV)], axis=1)
    g_ffn1 = all_gather8(_pack_shards(W, GATHER_GROUPS[0], BF16), "gather_w_ffn1")

    b_ada_cols = lax.dynamic_slice(b_ada, (0, me * n_ada), (1, n_ada))
    mod_cols, c_act = adaln_fwd(c_all, w_ada[0], b_ada_cols, "adaln_fwd")
    mod_g = all_gather8(mod_cols, "gather_mod")
    g_ffn1, mod_g, down1, rest = lax.optimization_barrier(
        (g_ffn1, mod_g, _pack_shards(W, GATHER_GROUPS[1], BF16), _pack_shards(W, GATHER_GROUPS[2], BF16)))
    g_down1 = sc_all_gather8(down1, "gather_w_ffn1_down", 1)
    wv = weight_views((g_ffn1, g_down1, sc_all_gather8(rest, "gather_w_rest", 6)))
    mod = lax.dynamic_slice(mod_g, (0, me * nb, 0), (N_DEV, nb, n_ada)).transpose(1, 0, 2).reshape(nb, N_MOD, 1, d)
    mod = [mod[:, k] for k in range(N_MOD)]

    P = dict(W)
    P["conv_w"] = conv_w_full
    P["norm_final"] = norm_final.reshape(1, d)
    R = local_step(x, loss_target, positions, mod, wv, P)

    dmod = R["dmod"]
    partial_shapes = [(1,), (1, d), (1, d), (1, d), (1, d), (1, d), (1, d), (1, Q_LORA), (1, KV_LORA),
                      (1, SSD_HEADS), (1, SSD_HEADS), (1, SSD_HEADS), (1, D_CONV), (4, D_CONV), (1, N_MOD * d),
                      (nb, N_MOD * d)]
    partial = _pack_rows([R["loss"][0, :1], R["norm_ffn1"], R["norm_mix"], R["norm_ffn2"], R["norm_final"],
                          R["ssd_norm_w"], R["mla_norm_w"], R["q_norm_w"], R["kv_norm_w"],
                          R["dt_bias"], R["a_log"], R["d_skip"], R["conv_b"], R["conv_w"],
                          sum_rows(dmod, "dmod_rows"), dmod])
    partial_g = all_gather8(partial, "gather_partials")
    (loss, g_nf1, g_nmix, g_nf2, g_nfin, g_ssdn, g_mlan, g_qn, g_kvn, g_dtb, g_alog, g_dskip, g_convb, g_convw,
     g_bada, _) = _unpack_rows(sum_blocks(partial_g, "sum_partials"), partial_shapes)
    dmod_all = jnp.stack([_unpack_rows(partial_g[k], partial_shapes)[-1] for k in range(N_DEV)]).reshape(N_DEV * nb, -1)
    g_wada = adaln_bwd(c_act, lax.dynamic_slice(dmod_all, (0, me * n_ada), (N_DEV * nb, n_ada)), "adaln_bwd")
    n_cw = conv_w.shape[2]
    G = {"w_ada": g_wada[None], "b_ada": g_bada, "norm_ffn1": g_nf1, "norm_mix": g_nmix, "norm_ffn2": g_nf2,
         "norm_final": g_nfin.reshape(d), "ssd_norm_w": g_ssdn, "mla_norm_w": g_mlan, "q_norm_w": g_qn,
         "kv_norm_w": g_kvn, "dt_bias": g_dtb, "a_log": g_alog, "d_skip": g_dskip, "conv_b": g_convb,
         "conv_w": lax.dynamic_slice(g_convw, (0, me * n_cw), (4, n_cw))[None]}

    DW, NM, NV = {}, {}, {}
    gw = R["gw"]
    for k, (tag, group) in enumerate(GRAD_GROUPS):
        send = jnp.concatenate([_grad_rows(name, gw[name]) for name in group], axis=1).astype(BF16)
        recv = sc_all_to_all8(send, "exchange_" + tag, 2 + k)
        gsum = sum_blocks(recv, "sum_" + tag)
        for name, (o, r) in _pack_offsets(group)[0].items():
            G[name] = _rows_to_shard(name, gsum[o:o + r], W[name])
            DW[name], NM[name], NV[name] = [t[None] for t in adamw(W[name][0], G[name][0], M[name][0], V[name][0],
                                                                  "adamw_" + name)]
    dwa, nma, nva = adamw(w_ada[0], g_wada, m_w_ada[0], v_w_ada[0], "adamw_w_ada")
    DW["w_ada"], NM["w_ada"], NV["w_ada"] = dwa[None], nma[None], nva[None]
    small = [n for n in names if n not in DW]
    shapes = [W[n].shape for n in small]
    outs = adamw(_pack_rows([W[n] for n in small]), _pack_rows([G[n] for n in small]),
                 _pack_rows([M[n] for n in small]), _pack_rows([V[n] for n in small]), "adamw_small")
    for res, dst in zip(outs, (DW, NM, NV)):
        for n, t in zip(small, _unpack_rows(res, shapes)):
            dst[n] = t
    return (loss.reshape(()), R["dx"], *[G[n] for n in names], *[DW[n] for n in names], *[NM[n] for n in names],
            *[NV[n] for n in names])
```

```python
import math

import jax
import jax.numpy as jnp
from jax import lax
from jax.experimental import pallas as pl
from jax.experimental.pallas import tpu as pltpu
from jax.experimental.pallas import tpu_sc as plsc

F32, BF16, I32 = jnp.float32, jnp.bfloat16, jnp.int32
HI = lax.Precision.HIGHEST
SDS = jax.ShapeDtypeStruct
MESH = pl.DeviceIdType.MESH

D_MODEL = 1024
D_FF = 2816
D_SSD = 1024
SSD_HEADS = 16
SSD_HEAD_DIM = 64
SSD_GROUPS = 2
SSD_STATE = 128
CHUNK = 128
MLA_HEADS = 8
QK_NOPE = 64
QK_ROPE = 32
QK_DIM = 96
V_HEAD = 128
Q_LORA = 384
KV_LORA = 256
ROPE_THETA = 10000.0
N_MOD = 9
EPS = 1e-6
D_CONV = 1536
D_IN = 3248
D_IN_PAD = 3328
HEAD_PAD = 128
N_DEV = 8
ADAM_LR, ADAM_B1, ADAM_B2, ADAM_EPS, ADAM_WD, ADAM_STEP = 0.001, 0.9, 0.999, 1e-08, 0.01, 10

SAVED_ACT = BF16
VMEM_LIMIT = 56 * 1024 * 1024
LANES = 128
NT_DIMS = (((1,), (1,)), ((), ()))
TN_DIMS = (((0,), (0,)), ((), ()))


def _cparams(n_axes):
    return pltpu.CompilerParams(dimension_semantics=("arbitrary",) * n_axes, vmem_limit_bytes=VMEM_LIMIT)


def _row(tm, d):
    return pl.BlockSpec((None, tm, d), lambda b, i: (b, i, 0))


def _bvec(d):
    return pl.BlockSpec((None, 1, d), lambda b, i: (b, 0, 0))


def _full(shape):
    n = len(shape)
    return pl.BlockSpec(shape, lambda *_: (0,) * n)


def _sigmoid(x):
    return 1.0 / (1.0 + jnp.exp(-x))


def _softplus(x):
    return jnp.maximum(x, 0.0) + jnp.log(1.0 + jnp.exp(-jnp.abs(x)))


def _rms(x):
    return lax.rsqrt(jnp.mean(x * x, axis=-1, keepdims=True) + EPS)


def _rms_bwd(dn, n, r):
    return r * (dn - n * jnp.mean(dn * n, axis=-1, keepdims=True))


def _first_step():
    return (pl.program_id(0) == 0) & (pl.program_id(1) == 0)


def all_gather8(x, name):
    r, c = x.shape

    def body(x_ref, out_ref, send_sems, recv_sems, local_sem):
        mx, my, mc = lax.axis_index("x"), lax.axis_index("y"), lax.axis_index("c")
        me, sibling = (mx, my, mc), (mx, my, 1 - mc)
        chips = [(1 - mx, my), (mx, 1 - my), (1 - mx, 1 - my)]

        def rows(px, py, pc):
            return out_ref.at[4 * px + 2 * py + pc]

        def copy(k, block, to, src=None):
            return pltpu.make_async_remote_copy(
                src_ref=rows(*block) if src is None else src, dst_ref=rows(*block),
                send_sem=send_sems.at[k], recv_sem=recv_sems.at[k], device_id=to, device_id_type=MESH)

        mine = pltpu.make_async_copy(x_ref, rows(*me), local_sem)
        mine.start()
        first = [copy(0, me, sibling, src=x_ref)]
        first += [copy(1 + j, me, (*chip, mc), src=x_ref) for j, chip in enumerate(chips)]
        for cp in first:
            cp.start()
        passed = [copy(4 + j, (*chip, mc), sibling) for j, chip in enumerate(chips)]
        for j, chip in enumerate(chips):
            copy(1 + j, (*chip, mc), me).wait_recv()
            passed[j].start()
        copy(0, sibling, me).wait_recv()
        for j, chip in enumerate(chips):
            copy(4 + j, (*chip, 1 - mc), me).wait_recv()
        for cp in first + passed:
            cp.wait_send()
        mine.wait()

    return pl.pallas_call(
        body, name=name,
        out_shape=SDS((N_DEV, r, c), x.dtype),
        in_specs=[pl.BlockSpec(memory_space=pl.ANY)],
        out_specs=pl.BlockSpec(memory_space=pl.ANY),
        scratch_shapes=[pltpu.SemaphoreType.DMA((7,)), pltpu.SemaphoreType.DMA((7,)), pltpu.SemaphoreType.DMA],
    )(x)


def all_to_all8(x, name):
    _, r, c = x.shape

    def body(x_ref, out_ref, send_sems, recv_sems, local_sem):
        mx, my, mc = lax.axis_index("x"), lax.axis_index("y"), lax.axis_index("c")
        me = 4 * mx + 2 * my + mc
        mine = pltpu.make_async_copy(x_ref.at[me], out_ref.at[me], local_sem)
        mine.start()
        copies = []
        for rel in range(1, N_DEV):
            px = 1 - mx if rel & 4 else mx
            py = 1 - my if rel & 2 else my
            pc = 1 - mc if rel & 1 else mc
            cp = pltpu.make_async_remote_copy(
                src_ref=x_ref.at[4 * px + 2 * py + pc], dst_ref=out_ref.at[me],
                send_sem=send_sems.at[rel - 1], recv_sem=recv_sems.at[rel - 1],
                device_id=(px, py, pc), device_id_type=MESH)
            cp.start()
            copies.append(cp)
        for cp in copies:
            cp.wait()
        mine.wait()

    return pl.pallas_call(
        body, name=name,
        out_shape=SDS((N_DEV, r, c), x.dtype),
        in_specs=[pl.BlockSpec(memory_space=pl.ANY)],
        out_specs=pl.BlockSpec(memory_space=pl.ANY),
        scratch_shapes=[pltpu.SemaphoreType.DMA((7,)), pltpu.SemaphoreType.DMA((7,)), pltpu.SemaphoreType.DMA],
    )(x)


def _sequencer_kernel(name, collective_id):
    return pl.kernel(
        mesh=plsc.ScalarSubcoreMesh(axis_name="seq", num_cores=1), name=name,
        scratch_types=(pltpu.SemaphoreType.DMA((7,)), pltpu.SemaphoreType.DMA((7,)), pltpu.SemaphoreType.DMA),
        compiler_params=pltpu.CompilerParams(collective_id=collective_id))


def _handshake(peers):
    barrier = pltpu.get_barrier_semaphore()
    for peer in peers:
        pl.semaphore_signal(barrier, inc=1, device_id=peer, device_id_type=MESH)
    pl.semaphore_wait(barrier, len(peers))


def sc_all_gather8(x, name, collective_id):
    r, c = x.shape
    x_ref = jax.new_ref(x, memory_space=pltpu.MemorySpace.HBM)
    out_ref = jax.empty_ref(SDS((N_DEV, r, c), x.dtype), memory_space=pltpu.MemorySpace.HBM)

    @_sequencer_kernel(name, collective_id)
    def launch(send_sems, recv_sems, local_sem):
        mx, my, mc = lax.axis_index("x"), lax.axis_index("y"), lax.axis_index("c")
        me, sibling = (mx, my, mc), (mx, my, 1 - mc)
        chips = [(1 - mx, my), (mx, 1 - my), (1 - mx, 1 - my)]
        _handshake([sibling] + [(*chip, mc) for chip in chips])

        def rows(px, py, pc):
            return out_ref.at[4 * px + 2 * py + pc]

        def copy(k, block, to, src=None):
            return pltpu.make_async_remote_copy(
                src_ref=rows(*block) if src is None else src, dst_ref=rows(*block),
                send_sem=send_sems.at[k], recv_sem=recv_sems.at[k], device_id=to, device_id_type=MESH)

        mine = pltpu.make_async_copy(x_ref, rows(*me), local_sem)
        mine.start()
        first = [copy(0, me, sibling, src=x_ref)]
        first += [copy(1 + j, me, (*chip, mc), src=x_ref) for j, chip in enumerate(chips)]
        for cp in first:
            cp.start()
        passed = [copy(4 + j, (*chip, mc), sibling) for j, chip in enumerate(chips)]
        for j, chip in enumerate(chips):
            copy(1 + j, (*chip, mc), me).wait_recv()
            passed[j].start()
        copy(0, sibling, me).wait_recv()
        for j, chip in enumerate(chips):
            copy(4 + j, (*chip, 1 - mc), me).wait_recv()
        for cp in first + passed:
            cp.wait_send()
        mine.wait()

    launch()
    return out_ref[...]


def sc_all_to_all8(x, name, collective_id):
    x_ref = jax.new_ref(x, memory_space=pltpu.MemorySpace.HBM)
    out_ref = jax.empty_ref(SDS(x.shape, x.dtype), memory_space=pltpu.MemorySpace.HBM)

    @_sequencer_kernel(name, collective_id)
    def launch(send_sems, recv_sems, local_sem):
        mx, my, mc = lax.axis_index("x"), lax.axis_index("y"), lax.axis_index("c")
        me = 4 * mx + 2 * my + mc
        peers = [(1 - mx if rel & 4 else mx, 1 - my if rel & 2 else my, 1 - mc if rel & 1 else mc)
                 for rel in range(1, N_DEV)]
        _handshake(peers)
        mine = pltpu.make_async_copy(x_ref.at[me], out_ref.at[me], local_sem)
        mine.start()
        copies = []
        for k, (px, py, pc) in enumerate(peers):
            cp = pltpu.make_async_remote_copy(
                src_ref=x_ref.at[4 * px + 2 * py + pc], dst_ref=out_ref.at[me],
                send_sem=send_sems.at[k], recv_sem=recv_sems.at[k], device_id=(px, py, pc), device_id_type=MESH)
            cp.start()
            copies.append(cp)
        for cp in copies:
            cp.wait()
        mine.wait()

    launch()
    return out_ref[...]


def norm_mod(x, w, sc, sh, name):
    b, s, d = x.shape
    tm = min(512, s)

    def body(x_ref, w_ref, sc_ref, sh_ref, h_ref):
        xv = x_ref[...]
        n = xv * _rms(xv)
        h_ref[...] = ((n * w_ref[...]) * (1.0 + sc_ref[...]) + sh_ref[...]).astype(BF16)

    return pl.pallas_call(
        body, name=name, grid=(b, s // tm),
        in_specs=[_row(tm, d), _full((1, d)), _bvec(d), _bvec(d)],
        out_specs=_row(tm, d), out_shape=SDS((b, s, d), BF16), compiler_params=_cparams(2))(x, w, sc, sh)


def ffn_up(h, wg_t, wu_t, name):
    b, s, d = h.shape
    f = wg_t.shape[0]
    tm, tn = min(512, s), f // 2

    def body(h_ref, wg_ref, wu_ref, g_ref, u_ref, a_ref):
        hv = h_ref[...]
        g = lax.dot_general(hv, wg_ref[...], NT_DIMS, preferred_element_type=F32)
        u = lax.dot_general(hv, wu_ref[...], NT_DIMS, preferred_element_type=F32)
        g_ref[...] = g.astype(g_ref.dtype)
        u_ref[...] = u.astype(u_ref.dtype)
        a_ref[...] = (g * _sigmoid(g) * u).astype(BF16)

    hs = pl.BlockSpec((None, tm, d), lambda j, bb, i: (bb, i, 0))
    ws = pl.BlockSpec((tn, d), lambda j, bb, i: (j, 0))
    os_ = pl.BlockSpec((None, tm, tn), lambda j, bb, i: (bb, i, j))
    return pl.pallas_call(
        body, name=name, grid=(f // tn, b, s // tm),
        in_specs=[hs, ws, ws], out_specs=[os_, os_, os_],
        out_shape=[SDS((b, s, f), SAVED_ACT), SDS((b, s, f), SAVED_ACT), SDS((b, s, f), BF16)],
        compiler_params=_cparams(3))(h, wg_t, wu_t)


def _norm_mod_tile(xv, w_ref, sc_ref, sh_ref):
    return ((xv * _rms(xv) * w_ref[...]) * (1.0 + sc_ref[...]) + sh_ref[...]).astype(BF16)


def ffn_down(a, wd, x, gate, scale, name, above=None):
    b, s, f = a.shape
    d = wd.shape[1]
    tm = min(512, s)

    def body(a_ref, wd_ref, x_ref, g_ref, *rest):
        xn_ref, o_ref = rest[-3:-1] if above else rest
        o = jnp.dot(a_ref[...], wd_ref[...], preferred_element_type=F32)
        xn = x_ref[...] + (scale * g_ref[...]) * o
        xn_ref[...] = xn
        o_ref[...] = o.astype(BF16)
        if above:
            rest[-1][...] = _norm_mod_tile(xn, *rest[0:3])

    extra = above is not None
    return pl.pallas_call(
        body, name=name, grid=(b, s // tm),
        in_specs=[_row(tm, f), _full((f, d)), _row(tm, d), _bvec(d)] + ([_full((1, d)), _bvec(d), _bvec(d)] if extra else []),
        out_specs=[_row(tm, d), _row(tm, d)] + ([_row(tm, d)] if extra else []),
        out_shape=[SDS((b, s, d), F32), SDS((b, s, d), BF16)] + ([SDS((b, s, d), BF16)] if extra else []),
        compiler_params=_cparams(2))(a, wd, x, gate, *(above or ()))


def ffn_dact(do, wd, g, u, name):
    b, s, d = do.shape
    f = wd.shape[0]
    tm, tn = min(512, s), f // 2

    def body(do_ref, wd_ref, g_ref, u_ref, dg_ref, du_ref):
        da = lax.dot_general(do_ref[...], wd_ref[...], NT_DIMS, preferred_element_type=F32)
        gv = g_ref[...].astype(F32)
        sg = _sigmoid(gv)
        dg_ref[...] = (da * u_ref[...].astype(F32) * (sg * (1.0 + gv * (1.0 - sg)))).astype(BF16)
        du_ref[...] = (da * (gv * sg)).astype(BF16)

    dos = pl.BlockSpec((None, tm, d), lambda j, bb, i: (bb, i, 0))
    ws = pl.BlockSpec((tn, d), lambda j, bb, i: (j, 0))
    es = pl.BlockSpec((None, tm, tn), lambda j, bb, i: (bb, i, j))
    return pl.pallas_call(
        body, name=name, grid=(f // tn, b, s // tm),
        in_specs=[dos, ws, es, es], out_specs=[es, es],
        out_shape=[SDS((b, s, f), BF16), SDS((b, s, f), BF16)], compiler_params=_cparams(3))(do, wd, g, u)


def mm_tn(a, bm, tma, tnb, name):
    b, s, ka = a.shape
    nb = bm.shape[2]
    tk = min(2048, s)
    nk = s // tk

    def body(a_ref, b_ref, o_ref, acc):
        first = (pl.program_id(2) == 0) & (pl.program_id(3) == 0)
        last = (pl.program_id(2) == b - 1) & (pl.program_id(3) == nk - 1)
        part = lax.dot_general(a_ref[...], b_ref[...], TN_DIMS, preferred_element_type=F32)

        @pl.when(first)
        def _():
            acc[...] = part

        @pl.when(jnp.logical_not(first))
        def _():
            acc[...] += part

        @pl.when(last)
        def _():
            o_ref[...] = acc[...].astype(BF16)

    return pl.pallas_call(
        body, name=name, grid=(ka // tma, nb // tnb, b, nk),
        in_specs=[pl.BlockSpec((None, tk, tma), lambda i, j, bb, k: (bb, k, i)),
                  pl.BlockSpec((None, tk, tnb), lambda i, j, bb, k: (bb, k, j))],
        out_specs=pl.BlockSpec((tma, tnb), lambda i, j, bb, k: (i, j)),
        out_shape=SDS((ka, nb), BF16), scratch_shapes=[pltpu.VMEM((tma, tnb), F32)],
        compiler_params=_cparams(4))(a, bm)


def _gate_bwd_specs(tm, d, b, s):
    return ([_row(tm, d), _bvec(d)], [_row(tm, d), _bvec(d)], [SDS((b, s, d), BF16), SDS((b, 1, d), F32)])


def _gate_bwd_tile(dx, scale, o_ref, g_ref, do_ref, dg_ref):
    do_ref[...] = ((scale * g_ref[...]) * dx).astype(BF16)
    dg_ref[...] += jnp.sum(scale * dx * o_ref[...].astype(F32), axis=0, keepdims=True)


def dh_norm_bwd(dys, wts, x, dxn, w, sc, name, below=None):
    b, s, d = x.shape
    tm = min(256, s)
    n_in = len(dys)
    extra_in, extra_out, extra_shape = _gate_bwd_specs(tm, d, b, s) if below else ([], [], [])

    def body(*refs):
        dy_refs, w_refs = refs[:n_in], refs[n_in:2 * n_in]
        x_ref, dxn_ref, nw_ref, sc_ref = refs[2 * n_in:2 * n_in + 4]
        rest = refs[2 * n_in + 4:]
        if below:
            o_ref, g_ref, dx_ref, dsc_ref, dsh_ref, dw_ref, do_ref, dg_ref = rest
        else:
            dx_ref, dsc_ref, dsh_ref, dw_ref = rest

        @pl.when(pl.program_id(1) == 0)
        def _():
            dsc_ref[...] = jnp.zeros_like(dsc_ref)
            dsh_ref[...] = jnp.zeros_like(dsh_ref)
            if below:
                dg_ref[...] = jnp.zeros_like(dg_ref)

        @pl.when(_first_step())
        def _():
            dw_ref[...] = jnp.zeros_like(dw_ref)

        dh = jnp.dot(dy_refs[0][...], w_refs[0][...], preferred_element_type=F32)
        for k in range(1, n_in):
            dh += jnp.dot(dy_refs[k][...], w_refs[k][...], preferred_element_type=F32)
        xv = x_ref[...]
        r = _rms(xv)
        n = xv * r
        nw = nw_ref[...]
        dsc_ref[...] += jnp.sum(dh * (n * nw), axis=0, keepdims=True)
        dsh_ref[...] += jnp.sum(dh, axis=0, keepdims=True)
        dhn = dh * (1.0 + sc_ref[...])
        dw_ref[...] += jnp.sum(dhn * n, axis=0, keepdims=True)
        dx = dxn_ref[...] + _rms_bwd(dhn * nw, n, r)
        dx_ref[...] = dx
        if below:
            _gate_bwd_tile(dx, below[2], o_ref, g_ref, do_ref, dg_ref)

    in_specs = [_row(tm, dy.shape[2]) for dy in dys] + [_full(wt.shape) for wt in wts]
    in_specs += [_row(tm, d), _row(tm, d), _full((1, d)), _bvec(d)] + extra_in
    return pl.pallas_call(
        body, name=name, grid=(b, s // tm), in_specs=in_specs,
        out_specs=[_row(tm, d), _bvec(d), _bvec(d), _full((1, d))] + extra_out,
        out_shape=[SDS((b, s, d), F32), SDS((b, 1, d), F32), SDS((b, 1, d), F32), SDS((1, d), F32)] + extra_shape,
        compiler_params=_cparams(2))(*dys, *wts, x, dxn, w, sc, *(below[:2] if below else ()))


def ffn_bwd_main(do, wd, g, u, wg_t, wu_t, x, dxn, w, sc, name, below=None):
    b, s, d = x.shape
    f = wd.shape[0]
    tm = min(256, s)
    th = tm // 2
    extra_in, extra_out, extra_shape = _gate_bwd_specs(tm, d, b, s) if below else ([], [], [])

    def body(do_ref, wd_ref, g_ref, u_ref, wg_ref, wu_ref, x_ref, dxn_ref, nw_ref, sc_ref, *rest):
        if below:
            o_ref, gt_ref, dg_ref, du_ref, dx_ref, dsc_ref, dsh_ref, dw_ref, dob_ref, dgt_ref = rest
        else:
            dg_ref, du_ref, dx_ref, dsc_ref, dsh_ref, dw_ref = rest

        @pl.when(pl.program_id(1) == 0)
        def _():
            dsc_ref[...] = jnp.zeros_like(dsc_ref)
            dsh_ref[...] = jnp.zeros_like(dsh_ref)
            if below:
                dgt_ref[...] = jnp.zeros_like(dgt_ref)

        @pl.when(_first_step())
        def _():
            dw_ref[...] = jnp.zeros_like(dw_ref)

        halves = [slice(0, th), slice(th, tm)]
        das = [lax.dot_general(do_ref[rs, :], wd_ref[...], NT_DIMS, preferred_element_type=F32) for rs in halves]
        dgs, dus = [], []
        for rs, da in zip(halves, das):
            gv = g_ref[rs, :].astype(F32)
            sg = _sigmoid(gv)
            dg = (da * u_ref[rs, :].astype(F32) * (sg * (1.0 + gv * (1.0 - sg)))).astype(BF16)
            du = (da * (gv * sg)).astype(BF16)
            dg_ref[rs, :] = dg
            du_ref[rs, :] = du
            dgs.append(dg)
            dus.append(du)
        dhs = [jnp.dot(dg, wg_ref[...], preferred_element_type=F32) + jnp.dot(du, wu_ref[...], preferred_element_type=F32)
               for dg, du in zip(dgs, dus)]
        nw = nw_ref[...]
        for rs, dh in zip(halves, dhs):
            xv = x_ref[rs, :]
            r = _rms(xv)
            n = xv * r
            dsc_ref[...] += jnp.sum(dh * (n * nw), axis=0, keepdims=True)
            dsh_ref[...] += jnp.sum(dh, axis=0, keepdims=True)
            dhn = dh * (1.0 + sc_ref[...])
            dw_ref[...] += jnp.sum(dhn * n, axis=0, keepdims=True)
            dx = dxn_ref[rs, :] + _rms_bwd(dhn * nw, n, r)
            dx_ref[rs, :] = dx
            if below:
                dob_ref[rs, :] = ((below[2] * gt_ref[...]) * dx).astype(BF16)
                dgt_ref[...] += jnp.sum(below[2] * dx * o_ref[rs, :].astype(F32), axis=0, keepdims=True)

    resident = lambda shape: pl.BlockSpec(shape, lambda *_: (0,) * len(shape), pipeline_mode=pl.Buffered(1))
    in_specs = [_row(tm, d), resident((f, d)), _row(tm, f), _row(tm, f), resident((f, d)), resident((f, d)),
                _row(tm, d), _row(tm, d), _full((1, d)), _bvec(d)] + extra_in
    return pl.pallas_call(
        body, name=name, grid=(b, s // tm), in_specs=in_specs,
        out_specs=[_row(tm, f), _row(tm, f), _row(tm, d), _bvec(d), _bvec(d), _full((1, d))] + extra_out,
        out_shape=[SDS((b, s, f), BF16), SDS((b, s, f), BF16), SDS((b, s, d), F32), SDS((b, 1, d), F32),
                   SDS((b, 1, d), F32), SDS((1, d), F32)] + extra_shape,
        compiler_params=_cparams(2))(do, wd, g, u, wg_t, wu_t, x, dxn, w, sc, *(below[:2] if below else ()))


def final_loss(x, w, tgt, below, name):
    b, s, d = x.shape
    tm = min(512, s)
    extra_in, extra_out, extra_shape = _gate_bwd_specs(tm, d, b, s)

    def body(x_ref, w_ref, t_ref, o_ref, g_ref, loss_ref, dx_ref, dw_ref, do_ref, dg_ref):
        @pl.when(_first_step())
        def _():
            loss_ref[...] = jnp.zeros_like(loss_ref)
            dw_ref[...] = jnp.zeros_like(dw_ref)

        @pl.when(pl.program_id(1) == 0)
        def _():
            dg_ref[...] = jnp.zeros_like(dg_ref)
        xv = x_ref[...]
        r = _rms(xv)
        n = xv * r
        wv = w_ref[...]
        e = n * wv - t_ref[...]
        loss_ref[...] += jnp.sum(e * e) * (0.5 / d)
        dy = e * (1.0 / d)
        dw_ref[...] += jnp.sum(dy * n, axis=0, keepdims=True)
        dx = _rms_bwd(dy * wv, n, r)
        dx_ref[...] = dx
        _gate_bwd_tile(dx, below[2], o_ref, g_ref, do_ref, dg_ref)

    return pl.pallas_call(
        body, name=name, grid=(b, s // tm),
        in_specs=[_row(tm, d), _full((1, d)), _row(tm, d)] + extra_in,
        out_specs=[_full((1, LANES)), _row(tm, d), _full((1, d))] + extra_out,
        out_shape=[SDS((1, LANES), F32), SDS((b, s, d), F32), SDS((1, d), F32)] + extra_shape,
        compiler_params=_cparams(2))(x, w, tgt, *below[:2])


def in_proj(h, win_t, name):
    b, s, d = h.shape
    tm = min(512, s)
    widths = (D_SSD, D_SSD + 2 * SSD_GROUPS * SSD_STATE, Q_LORA, KV_LORA, LANES)

    def body(h_ref, w_ref, *outs):
        p = lax.dot_general(h_ref[...], w_ref[...], NT_DIMS, preferred_element_type=F32)
        off = 0
        for o_ref, wd in zip(outs, widths):
            o_ref[...] = p[:, off:off + wd]
            off += wd

    return pl.pallas_call(
        body, name=name, grid=(b, s // tm),
        in_specs=[_row(tm, d), _full(win_t.shape)],
        out_specs=[_row(tm, wd) for wd in widths],
        out_shape=[SDS((b, s, wd), F32) for wd in widths], compiler_params=_cparams(2))(h, win_t)


def _halo_prev(ts, d):
    return pl.BlockSpec((None, 8, d), lambda b, i: (b, jnp.maximum(i * (ts // 8) - 1, 0), 0))


CONV_ROWS = 32


def _conv_head(head, u_ref, up_ref):
    head[0:8, :] = jnp.where(pl.program_id(1) > 0, up_ref[...], 0.0)
    head[8:8 + CONV_ROWS, :] = u_ref[0:CONV_ROWS, :]


def _conv_windows(u_ref, head, r0):
    if r0 == 0:
        return [head[5 + k:5 + k + CONV_ROWS, :] for k in range(4)]
    return [u_ref[r0 - 3 + k:r0 - 3 + k + CONV_ROWS, :] for k in range(4)]


def _fold8(t):
    acc = t[0:8, :]
    for r in range(8, CONV_ROWS, 8):
        acc += t[r:r + 8, :]
    return acc


def conv_fwd(u, cw, cb, name):
    b, s, dc = u.shape
    ts = min(512, s)
    widths = (D_SSD, SSD_GROUPS * SSD_STATE, SSD_GROUPS * SSD_STATE)

    def body(u_ref, up_ref, w_ref, b_ref, xs_ref, bm_ref, cm_ref, head):
        _conv_head(head, u_ref, up_ref)
        ws = [w_ref[k:k + 1, :] for k in range(4)]
        bias = b_ref[...]
        for r0 in range(0, ts, CONV_ROWS):
            taps = _conv_windows(u_ref, head, r0)
            v = bias + taps[0] * ws[0] + taps[1] * ws[1] + taps[2] * ws[2] + taps[3] * ws[3]
            y = v * _sigmoid(v)
            rs = slice(r0, r0 + CONV_ROWS)
            xs_ref[rs, :] = y[:, 0:D_SSD]
            bm_ref[rs, :] = y[:, D_SSD:D_SSD + 256]
            cm_ref[rs, :] = y[:, D_SSD + 256:D_SSD + 512]

    return pl.pallas_call(
        body, name=name, grid=(b, s // ts),
        in_specs=[_row(ts, dc), _halo_prev(ts, dc), _full((4, dc)), _full((1, dc))],
        out_specs=[_row(ts, wd) for wd in widths],
        out_shape=[SDS((b, s, wd), F32) for wd in widths],
        scratch_shapes=[pltpu.VMEM((8 + CONV_ROWS, dc), F32)], compiler_params=_cparams(2))(u, u, cw, cb)


def conv_bwd_a(dxs, dbm, dcm, u, cw, cb, name):
    b, s, dc = u.shape
    ts = min(512, s)

    def body(dxs_ref, dbm_ref, dcm_ref, u_ref, up_ref, w_ref, b_ref, dv_ref, dwb_ref, head):
        @pl.when(_first_step())
        def _():
            dwb_ref[...] = jnp.zeros_like(dwb_ref)
        _conv_head(head, u_ref, up_ref)
        ws = [w_ref[k:k + 1, :] for k in range(4)]
        bias = b_ref[...]
        for r0 in range(0, ts, CONV_ROWS):
            taps = _conv_windows(u_ref, head, r0)
            v = bias + taps[0] * ws[0] + taps[1] * ws[1] + taps[2] * ws[2] + taps[3] * ws[3]
            sg = _sigmoid(v)
            rs = slice(r0, r0 + CONV_ROWS)
            dy = jnp.concatenate([dxs_ref[rs, :], dbm_ref[rs, :], dcm_ref[rs, :]], axis=1)
            dv = dy * (sg * (1.0 + v * (1.0 - sg)))
            dv_ref[rs, :] = dv
            for k in range(4):
                dwb_ref[8 * k:8 * k + 8, :] += _fold8(dv * taps[k])
            dwb_ref[32:40, :] += _fold8(dv)

    return pl.pallas_call(
        body, name=name, grid=(b, s // ts),
        in_specs=[_row(ts, D_SSD), _row(ts, 256), _row(ts, 256), _row(ts, dc), _halo_prev(ts, dc),
                  _full((4, dc)), _full((1, dc))],
        out_specs=[_row(ts, dc), _full((40, dc))],
        out_shape=[SDS((b, s, dc), F32), SDS((40, dc), F32)],
        scratch_shapes=[pltpu.VMEM((8 + CONV_ROWS, dc), F32)], compiler_params=_cparams(2))(dxs, dbm, dcm, u, u, cw, cb)


def conv_grads_fold(x, name):
    c = x.shape[1]

    def body(x_ref, o_ref):
        o_ref[...] = jnp.zeros_like(o_ref)
        for k in range(5):
            o_ref[k:k + 1, :] = jnp.sum(x_ref[8 * k:8 * k + 8, :], axis=0, keepdims=True)

    return pl.pallas_call(body, name=name, out_shape=SDS((8, c), F32))(x)


def conv_bwd_b(dv, cw, name):
    b, s, dc = dv.shape
    ts = min(512, s)
    nt = s // ts

    def body(dv_ref, dn_ref, w_ref, du_ref, tail):
        tail[0:CONV_ROWS, :] = dv_ref[ts - CONV_ROWS:ts, :]
        tail[CONV_ROWS:CONV_ROWS + 8, :] = jnp.where(pl.program_id(1) < nt - 1, dn_ref[...], 0.0)
        ws = [w_ref[k:k + 1, :] for k in range(4)]
        for r0 in range(0, ts, CONV_ROWS):
            if r0 == ts - CONV_ROWS:
                win = [tail[3 - k:3 - k + CONV_ROWS, :] for k in range(4)]
            else:
                win = [dv_ref[r0 + 3 - k:r0 + 3 - k + CONV_ROWS, :] for k in range(4)]
            acc = win[0] * ws[0] + win[1] * ws[1] + win[2] * ws[2] + win[3] * ws[3]
            du_ref[r0:r0 + CONV_ROWS, :] = acc.astype(BF16)

    nxt = pl.BlockSpec((None, 8, dc), lambda bb, i: (bb, jnp.minimum((i + 1) * (ts // 8), s // 8 - 1), 0))
    return pl.pallas_call(
        body, name=name, grid=(b, nt),
        in_specs=[_row(ts, dc), nxt, _full((4, dc))],
        out_specs=_row(ts, dc), out_shape=SDS((b, s, dc), BF16),
        scratch_shapes=[pltpu.VMEM((CONV_ROWS + 8, dc), F32)], compiler_params=_cparams(2))(dv, dv, cw)


def _ssd_common(misc_ref, dtb_ref, alog_ref, e_ref):
    ln = CHUNK
    lane = lax.broadcasted_iota(I32, (ln, LANES), 1)
    lane1 = lax.broadcasted_iota(I32, (1, LANES), 1)
    pre = misc_ref[...] + dtb_ref[...]
    dt_s = jnp.where(lane < SSD_HEADS, _softplus(pre), 0.0)
    a_neg = jnp.where(lane1 < SSD_HEADS, -jnp.exp(alog_ref[...]), 0.0)
    ri = lax.broadcasted_iota(I32, (ln, ln), 0)
    ci = lax.broadcasted_iota(I32, (ln, ln), 1)
    tril = ci <= ri
    acum = jnp.dot(tril.astype(F32), dt_s * a_neg, preferred_element_type=F32, precision=HI)
    both_e = _dot_01(jnp.concatenate([dt_s, acum], axis=0), e_ref[...], 3)
    dt_e, acum_e = both_e[0:ln], both_e[ln:2 * ln]
    return dict(pre=pre, dt_s=dt_s, a_neg=a_neg, tril=tril, ri=ri, ci=ci, acum=acum, acum_t=acum.T,
                dt_e=dt_e, eac_e=jnp.exp(acum_e), del_e=jnp.exp(acum_e[ln - 1:ln, :] - acum_e))


def _dot_01(x, m01, terms):
    acc, rest = None, x
    for k in range(terms):
        part = rest.astype(BF16)
        if k + 1 < terms:
            rest = rest - part.astype(F32)
        d = jnp.dot(part, m01, preferred_element_type=F32)
        acc = d if acc is None else acc + d
    return acc


def _decay(cm, h):
    seg = cm["acum"][:, h:h + 1] - cm["acum_t"][h:h + 1, :]
    return jnp.exp(jnp.where(cm["tril"], seg, -jnp.inf))


def ssd_fwd(xs, bm, cm_, misc, z, dtb, alog, dskip_e, norm_w, e_mat, name):
    b, s, _ = xs.shape
    ln, nc = CHUNK, s // CHUNK
    gw = D_SSD // SSD_GROUPS
    hpg = SSD_HEADS // SSD_GROUPS

    def body(xs_ref, b_ref, c_ref, misc_ref, z_ref, dtb_ref, alog_ref, dsk_ref, nw_ref, e_ref,
             ys_ref, y_ref, p_ref, st, yd):
        @pl.when(pl.program_id(1) == 0)
        def _():
            st[...] = jnp.zeros_like(st)
        cm = _ssd_common(misc_ref, dtb_ref, alog_ref, e_ref)
        xsv = xs_ref[...]
        xdt = xsv * cm["dt_e"]
        xdt_b = xdt.astype(BF16)
        xd_b = (xdt * cm["del_e"]).astype(BF16)
        gam_e = cm["eac_e"][ln - 1:ln, :]
        p_ref[...] = st[...]
        yoff = []
        for g in range(SSD_GROUPS):
            gs = slice(gw * g, gw * (g + 1))
            bg = b_ref[:, SSD_STATE * g:SSD_STATE * (g + 1)].astype(BF16)
            cg = c_ref[:, SSD_STATE * g:SSD_STATE * (g + 1)].astype(BF16)
            cb = lax.dot_general(cg, bg, NT_DIMS, preferred_element_type=F32)
            st_g = st[:, gs]
            yoff.append(jnp.dot(cg, st_g.astype(BF16), preferred_element_type=F32) * cm["eac_e"][:, gs])
            for j in range(hpg):
                h = hpg * g + j
                hs = slice(SSD_HEAD_DIM * h, SSD_HEAD_DIM * (h + 1))
                m = (cb * _decay(cm, h)).astype(BF16)
                yd[:, hs] = jnp.dot(m, xdt_b[:, hs], preferred_element_type=F32)
            new = lax.dot_general(bg, xd_b[:, gs], TN_DIMS, preferred_element_type=F32)
            st[:, gs] = st_g * gam_e[:, gs] + new
        y = yd[...] + jnp.concatenate(yoff, axis=1) + dsk_ref[...] * xsv
        y_ref[...] = y
        zz = z_ref[...]
        yg = y * (zz * _sigmoid(zz))
        outs = []
        for g in range(SSD_GROUPS):
            ygg = yg[:, gw * g:gw * (g + 1)]
            outs.append(ygg * _rms(ygg) * nw_ref[:, gw * g:gw * (g + 1)])
        ys_ref[...] = jnp.concatenate(outs, axis=1).astype(BF16)

    row = lambda d: pl.BlockSpec((None, ln, d), lambda bb, c: (bb, c, 0))
    return pl.pallas_call(
        body, name=name, grid=(b, nc),
        in_specs=[row(D_SSD), row(256), row(256), row(LANES), row(D_SSD), _full((1, LANES)), _full((1, LANES)),
                  _full((1, D_SSD)), _full((1, D_SSD)), _full((LANES, D_SSD))],
        out_specs=[row(D_SSD), row(D_SSD), pl.BlockSpec((None, None, SSD_STATE, D_SSD), lambda bb, c: (bb, c, 0, 0))],
        out_shape=[SDS((b, s, D_SSD), BF16), SDS((b, s, D_SSD), F32), SDS((b, nc, SSD_STATE, D_SSD), F32)],
        scratch_shapes=[pltpu.VMEM((SSD_STATE, D_SSD), F32), pltpu.VMEM((ln, D_SSD), F32)],
        compiler_params=_cparams(2))(xs, bm, cm_, misc, z, dtb, alog, dskip_e, norm_w, e_mat)


def ssd_bwd(dys, y, z, xs, bm, cm_, misc, prev, dtb, alog, dskip_e, norm_w, e_mat, et_mat, name):
    b, s, _ = xs.shape
    ln, nc = CHUNK, s // CHUNK
    gw = D_SSD // SSD_GROUPS
    hpg = SSD_HEADS // SSD_GROUPS

    def body(dys_ref, y_ref, z_ref, xs_ref, b_ref, c_ref, misc_ref, p_ref, dtb_ref, alog_ref, dsk_ref, nw_ref,
             e_ref, et_ref, dxs_ref, db_ref, dc_ref, dz_ref, ddt_ref, dnw_ref, ddsk_ref, ddtb_ref, dalog_ref,
             dst, dxd, dac_t):
        @pl.when(_first_step())
        def _():
            for r_ in (dnw_ref, ddsk_ref, ddtb_ref, dalog_ref):
                r_[...] = jnp.zeros_like(r_)

        @pl.when(pl.program_id(1) == 0)
        def _():
            dst[...] = jnp.zeros_like(dst)

        cm = _ssd_common(misc_ref, dtb_ref, alog_ref, e_ref)
        et = et_ref[...]
        squeeze = lambda t: _dot_01(t, et, 2)
        lane = lax.broadcasted_iota(I32, (ln, LANES), 1)
        sub = lax.broadcasted_iota(I32, (LANES, ln), 0)
        xsv = xs_ref[...]
        xdt = xsv * cm["dt_e"]
        xdt_b = xdt.astype(BF16)
        xd_b = (xdt * cm["del_e"]).astype(BF16)
        eac_e = cm["eac_e"]
        gam_e = eac_e[ln - 1:ln, :]

        yv, zz, dyo = y_ref[...], z_ref[...], dys_ref[...]
        sz = _sigmoid(zz)
        silu_z = zz * sz
        yg = yv * silu_z
        dyg, dnw = [], []
        for g in range(SSD_GROUPS):
            gs = slice(gw * g, gw * (g + 1))
            ygg = yg[:, gs]
            r = _rms(ygg)
            n = ygg * r
            dnw.append(jnp.sum(dyo[:, gs] * n, axis=0, keepdims=True))
            dyg.append(_rms_bwd(dyo[:, gs] * nw_ref[:, gs], n, r))
        dyg = jnp.concatenate(dyg, axis=1)
        dnw_ref[...] += jnp.concatenate(dnw, axis=1)
        dz_ref[...] = (dyg * yv * (sz * (1.0 + zz * (1.0 - sz)))).astype(BF16)
        dy = dyg * silu_z
        ddsk_ref[...] += jnp.sum(dy * xsv, axis=0, keepdims=True)
        dy_b = dy.astype(BF16)

        dacum = jnp.zeros((ln, LANES), F32)
        dac_t[...] = jnp.zeros_like(dac_t)
        w1, dgam = [], []
        for g in range(SSD_GROUPS):
            gs = slice(gw * g, gw * (g + 1))
            ss = slice(SSD_STATE * g, SSD_STATE * (g + 1))
            bg = b_ref[:, ss].astype(BF16)
            cg = c_ref[:, ss].astype(BF16)
            cb = lax.dot_general(cg, bg, NT_DIMS, preferred_element_type=F32)
            pt = p_ref[:, gs]
            pt_b = pt.astype(BF16)
            dst_g = dst[:, gs]
            dst_b = dst_g.astype(BF16)
            edy = (dy[:, gs] * eac_e[:, gs]).astype(BF16)
            dcg = lax.dot_general(edy, pt_b, NT_DIMS, preferred_element_type=F32)
            dpt = lax.dot_general(cg, edy, TN_DIMS, preferred_element_type=F32)
            yoff = jnp.dot(cg, pt_b, preferred_element_type=F32) * eac_e[:, gs]
            dxd_g = jnp.dot(bg, dst_b, preferred_element_type=F32)
            dbg = lax.dot_general(xd_b[:, gs], dst_b, NT_DIMS, preferred_element_type=F32)
            ddel = dxd_g * xdt[:, gs] * cm["del_e"][:, gs]
            w1.append(dy[:, gs] * yoff - ddel)
            dgam.append(jnp.sum(ddel, axis=0, keepdims=True) + jnp.sum(dst_g * pt, axis=0, keepdims=True) * gam_e[:, gs])
            dxd[:, gs] = dxd_g * cm["del_e"][:, gs]
            dst[:, gs] = dst_g * gam_e[:, gs] + dpt
            dcb = jnp.zeros((ln, ln), F32)
            for j in range(hpg):
                h = hpg * g + j
                hs = slice(SSD_HEAD_DIM * h, SSD_HEAD_DIM * (h + 1))
                lam = _decay(cm, h)
                m = cb * lam
                dm = lax.dot_general(dy_b[:, hs], xdt_b[:, hs], NT_DIMS, preferred_element_type=F32)
                dxd[:, hs] += lax.dot_general(m.astype(BF16), dy_b[:, hs], TN_DIMS, preferred_element_type=F32)
                dcb += dm * lam
                wl = dm * m
                dacum += jnp.where(lane == h, jnp.sum(wl, axis=1, keepdims=True), 0.0)
                dac_t[...] -= jnp.where(sub == h, jnp.sum(wl, axis=0, keepdims=True), 0.0)
            dcb_b = dcb.astype(BF16)
            dc_ref[:, ss] = dcg + jnp.dot(dcb_b, bg, preferred_element_type=F32)
            db_ref[:, ss] = dbg + lax.dot_general(dcb_b, cg, TN_DIMS, preferred_element_type=F32)

        dxdt = dxd[...]
        dxs_ref[...] = dy * dsk_ref[...] + dxdt * cm["dt_e"]
        dacum += squeeze(jnp.concatenate(w1, axis=1)) + dac_t[...].T
        dlast = squeeze(jnp.broadcast_to(jnp.concatenate(dgam, axis=1), (8, D_SSD)))[0:1, :]
        dacum += jnp.where(lax.broadcasted_iota(I32, (ln, LANES), 0) == ln - 1, dlast, 0.0)
        triu = (cm["ci"] >= cm["ri"]).astype(F32)
        da = jnp.dot(triu, dacum, preferred_element_type=F32, precision=HI)
        ddt = da * cm["a_neg"] + squeeze(dxdt * xsv)
        dalog_ref[...] += jnp.sum(da * cm["dt_s"], axis=0, keepdims=True) * cm["a_neg"]
        ddt_raw = jnp.where(lane < SSD_HEADS, ddt * _sigmoid(cm["pre"]), 0.0)
        ddt_ref[...] = ddt_raw
        ddtb_ref[...] += jnp.sum(ddt_raw, axis=0, keepdims=True)

    row = lambda d: pl.BlockSpec((None, ln, d), lambda bb, c: (bb, nc - 1 - c, 0))
    return pl.pallas_call(
        body, name=name, grid=(b, nc),
        in_specs=[row(D_SSD), row(D_SSD), row(D_SSD), row(D_SSD), row(256), row(256), row(LANES),
                  pl.BlockSpec((None, None, SSD_STATE, D_SSD), lambda bb, c: (bb, nc - 1 - c, 0, 0)),
                  _full((1, LANES)), _full((1, LANES)), _full((1, D_SSD)), _full((1, D_SSD)),
                  _full((LANES, D_SSD)), _full((D_SSD, LANES))],
        out_specs=[row(D_SSD), row(256), row(256), row(D_SSD), row(LANES),
                   _full((1, D_SSD)), _full((1, D_SSD)), _full((1, LANES)), _full((1, LANES))],
        out_shape=[SDS((b, s, D_SSD), F32), SDS((b, s, 256), F32), SDS((b, s, 256), F32), SDS((b, s, D_SSD), BF16),
                   SDS((b, s, LANES), F32), SDS((1, D_SSD), F32), SDS((1, D_SSD), F32), SDS((1, LANES), F32),
                   SDS((1, LANES), F32)],
        scratch_shapes=[pltpu.VMEM((SSD_STATE, D_SSD), F32), pltpu.VMEM((ln, D_SSD), F32), pltpu.VMEM((LANES, ln), F32)],
        compiler_params=_cparams(2))(dys, y, z, xs, bm, cm_, misc, prev, dtb, alog, dskip_e, norm_w, e_mat, et_mat)


def _rope(xv, cc, sp, sm):
    n = xv.shape[1]
    return xv * cc + pltpu.roll(xv, 16, 1) * sp + pltpu.roll(xv, n - 16, 1) * sm


def _rope_bwd(dy, cc, sp, sm):
    n = dy.shape[1]
    return dy * cc + pltpu.roll(dy * sp, n - 16, 1) + pltpu.roll(dy * sm, 16, 1)


def _tile8(t):
    return jnp.concatenate([t] * MLA_HEADS, axis=1)


def qkv_fwd(cq, ckv, misc, cc, sp, sm, qnw, kvnw, wuq_t, wukv_t, place, name):
    b, s, _ = cq.shape
    tm = min(512, s)
    hd = MLA_HEADS * HEAD_PAD

    def body(cq_ref, ckv_ref, misc_ref, cc_ref, sp_ref, sm_ref, qnw_ref, kvnw_ref, wq_ref, wkv_ref, pl_ref,
             q_ref, k_ref, v_ref, qn_ref, kvn_ref):
        cqv, ckvv = cq_ref[...], ckv_ref[...]
        qn = (cqv * _rms(cqv) * qnw_ref[...]).astype(BF16)
        kvn = (ckvv * _rms(ckvv) * kvnw_ref[...]).astype(BF16)
        qn_ref[...] = qn
        kvn_ref[...] = kvn
        cc1, sp1, sm1 = cc_ref[...], sp_ref[...], sm_ref[...]
        q = lax.dot_general(qn, wq_ref[...], NT_DIMS, preferred_element_type=F32)
        q_ref[...] = _rope(q, _tile8(cc1), _tile8(sp1), _tile8(sm1)).astype(BF16)
        kv = lax.dot_general(kvn, wkv_ref[...], NT_DIMS, preferred_element_type=F32)
        kr = jnp.dot(misc_ref[...], pl_ref[...], preferred_element_type=F32, precision=HI)
        kr = _rope(kr, cc1, sp1, sm1)
        k_ref[...] = (kv[:, 0:hd] + _tile8(kr)).astype(BF16)
        v_ref[...] = kv[:, hd:2 * hd].astype(BF16)

    return pl.pallas_call(
        body, name=name, grid=(b, s // tm),
        in_specs=[_row(tm, Q_LORA), _row(tm, KV_LORA), _row(tm, LANES), _row(tm, LANES), _row(tm, LANES), _row(tm, LANES),
                  _full((1, Q_LORA)), _full((1, KV_LORA)), _full(wuq_t.shape), _full(wukv_t.shape), _full((LANES, LANES))],
        out_specs=[_row(tm, hd), _row(tm, hd), _row(tm, hd), _row(tm, Q_LORA), _row(tm, KV_LORA)],
        out_shape=[SDS((b, s, hd), BF16)] * 3 + [SDS((b, s, Q_LORA), BF16), SDS((b, s, KV_LORA), BF16)],
        compiler_params=_cparams(2))(cq, ckv, misc, cc, sp, sm, qnw, kvnw, wuq_t, wukv_t, place)


def qkv_bwd(dq, dk, dv, ddt, cq, ckv, cc, sp, sm, qnw, kvnw, wuq_t, wukv_t, place_t, name):
    b, s, _ = cq.shape
    tm = min(512, s)
    hd = MLA_HEADS * HEAD_PAD

    def body(dq_ref, dk_ref, dv_ref, ddt_ref, cq_ref, ckv_ref, cc_ref, sp_ref, sm_ref, qnw_ref, kvnw_ref,
             wq_ref, wkv_ref, plt_ref, dcq_ref, dckv_ref, dmisc_ref, dqp_ref, dkv_ref, dqnw_ref, dkvnw_ref):
        @pl.when(_first_step())
        def _():
            dqnw_ref[...] = jnp.zeros_like(dqnw_ref)
            dkvnw_ref[...] = jnp.zeros_like(dkvnw_ref)
        cc1, sp1, sm1 = cc_ref[...], sp_ref[...], sm_ref[...]
        dqp = _rope_bwd(dq_ref[...], _tile8(cc1), _tile8(sp1), _tile8(sm1)).astype(BF16)
        dqp_ref[...] = dqp
        dkf = dk_ref[...]
        dkv_b = jnp.concatenate([dkf, dv_ref[...]], axis=1).astype(BF16)
        dkv_ref[...] = dkv_b
        dkr = dkf[:, 0:HEAD_PAD]
        for h in range(1, MLA_HEADS):
            dkr += dkf[:, HEAD_PAD * h:HEAD_PAD * (h + 1)]
        dkr = _rope_bwd(dkr, cc1, sp1, sm1)
        dmisc_ref[...] = (jnp.dot(dkr, plt_ref[...], preferred_element_type=F32, precision=HI) + ddt_ref[...]).astype(BF16)

        def norm_bwd(dn_w, xv, w_ref, dw_ref, dx_ref):
            r = _rms(xv)
            n = xv * r
            dw_ref[...] += jnp.sum(dn_w * n, axis=0, keepdims=True)
            dx_ref[...] = _rms_bwd(dn_w * w_ref[...], n, r).astype(BF16)

        norm_bwd(jnp.dot(dqp, wq_ref[...], preferred_element_type=F32), cq_ref[...], qnw_ref, dqnw_ref, dcq_ref)
        norm_bwd(jnp.dot(dkv_b, wkv_ref[...], preferred_element_type=F32), ckv_ref[...], kvnw_ref, dkvnw_ref, dckv_ref)

    return pl.pallas_call(
        body, name=name, grid=(b, s // tm),
        in_specs=[_row(tm, hd), _row(tm, hd), _row(tm, hd), _row(tm, LANES), _row(tm, Q_LORA), _row(tm, KV_LORA),
                  _row(tm, LANES), _row(tm, LANES), _row(tm, LANES), _full((1, Q_LORA)), _full((1, KV_LORA)),
                  _full(wuq_t.shape), _full(wukv_t.shape), _full((LANES, LANES))],
        out_specs=[_row(tm, Q_LORA), _row(tm, KV_LORA), _row(tm, LANES), _row(tm, hd), _row(tm, 2 * hd),
                   _full((1, Q_LORA)), _full((1, KV_LORA))],
        out_shape=[SDS((b, s, Q_LORA), BF16), SDS((b, s, KV_LORA), BF16), SDS((b, s, LANES), BF16),
                   SDS((b, s, hd), BF16), SDS((b, s, 2 * hd), BF16), SDS((1, Q_LORA), F32), SDS((1, KV_LORA), F32)],
        compiler_params=_cparams(2))(dq, dk, dv, ddt, cq, ckv, cc, sp, sm, qnw, kvnw, wuq_t, wukv_t, place_t)


ATT_SCALE = 1.0 / math.sqrt(QK_DIM)
LOG2E = math.log2(math.e)
ATT_SCALE_LOG2E = ATT_SCALE * LOG2E


ATT_HEADS_PER_STEP = 4
ATT_HEADS_PER_STEP_BWD = 2


def _att_tile(s):
    return min(512, s)


def flash_fwd(q, k, v, name):
    b, s, hd = q.shape
    t = _att_tile(s)
    nb = s // t
    th = t // 2
    vt = v.reshape(b, nb, t, MLA_HEADS, HEAD_PAD).transpose(0, 3, 1, 4, 2)

    hps = ATT_HEADS_PER_STEP
    hw = hps * HEAD_PAD

    def body(q_ref, k_ref, vt_ref, o_ref, lse_ref, m_s, l_s, acc):
        i = pl.program_id(2)
        m_s[...] = jnp.full_like(m_s, -jnp.inf)
        l_s[...] = jnp.zeros_like(l_s)
        acc[...] = jnp.zeros_like(acc)

        def update(j, diagonal):
            ks = pl.ds(pl.multiple_of(j * t, t), t)
            chains = [(hh, half) for hh in range(hps) for half in range(2)]
            lanes = lambda hh: slice(HEAD_PAD * hh, HEAD_PAD * (hh + 1))
            cols = lambda half: slice(th * half, th * (half + 1))
            sts = {}
            for hh, half in chains:
                st = lax.dot_general(k_ref[ks, lanes(hh)], q_ref[cols(half), lanes(hh)], NT_DIMS,
                                     preferred_element_type=F32)
                if diagonal:
                    row = lax.broadcasted_iota(I32, (t, th), 0)
                    col = lax.broadcasted_iota(I32, (t, th), 1) + th * half
                    st = jnp.where(row <= col, st, -jnp.inf)
                sts[hh, half] = st
            pts, alphas = {}, {}
            for hh, half in chains:
                st, cs = sts[hh, half], cols(half)
                m_prev = m_s[hh, :, cs]
                m_new = jnp.maximum(m_prev, jnp.max(st, axis=0, keepdims=True))
                alpha = jnp.exp2((m_prev - m_new) * ATT_SCALE_LOG2E)
                pt = jnp.exp2((st - m_new) * ATT_SCALE_LOG2E)
                l_s[hh, :, cs] = alpha * l_s[hh, :, cs] + jnp.sum(pt, axis=0, keepdims=True)
                m_s[hh, :, cs] = m_new
                pts[hh, half], alphas[hh, half] = pt.astype(BF16), alpha
            for hh, half in chains:
                cs = cols(half)
                acc[hh, :, cs] = alphas[hh, half] * acc[hh, :, cs] + jnp.dot(vt_ref[hh, j], pts[hh, half],
                                                                             preferred_element_type=F32)

        def step(j, carry):
            update(j, False)
            return carry

        lax.fori_loop(0, i, step, 0)
        update(i, True)
        for hh in range(hps):
            o_ref[:, HEAD_PAD * hh:HEAD_PAD * (hh + 1)] = (acc[hh] / l_s[hh]).T
            lse_ref[hh] = m_s[hh] * ATT_SCALE + jnp.log(l_s[hh])

    qs = pl.BlockSpec((None, t, hw), lambda bb, h, i: (bb, i, h))
    ks = pl.BlockSpec((None, s, hw), lambda bb, h, i: (bb, 0, h))
    vs = pl.BlockSpec((None, hps, nb, HEAD_PAD, t), lambda bb, h, i: (bb, h, 0, 0, 0))
    ls = pl.BlockSpec((None, hps, None, 1, t), lambda bb, h, i: (bb, h, i, 0, 0))
    return pl.pallas_call(
        body, name=name, grid=(b, MLA_HEADS // hps, nb),
        in_specs=[qs, ks, vs], out_specs=[qs, ls],
        out_shape=[SDS((b, s, hd), F32), SDS((b, MLA_HEADS, nb, 1, t), F32)],
        scratch_shapes=[pltpu.VMEM((hps, 1, t), F32), pltpu.VMEM((hps, 1, t), F32), pltpu.VMEM((hps, HEAD_PAD, t), F32)],
        compiler_params=_cparams(3))(q, k, vt)


def flash_bwd(q, k, v, do, lse, dlt, name):
    b, s, hd = q.shape
    t = _att_tile(s)
    nb = s // t
    th = t // 2
    lse_r = lse
    dlt_r = dlt.reshape(b, MLA_HEADS, nb, 1, t)

    hps = ATT_HEADS_PER_STEP_BWD
    hw = hps * HEAD_PAD

    def body(q_ref, k_ref, v_ref, do_ref, lse_ref, dlt_ref, dq_ref, dk_ref, dv_ref):
        dq_ref[...] = jnp.zeros_like(dq_ref)
        dk_ref[...] = jnp.zeros_like(dk_ref)
        dv_ref[...] = jnp.zeros_like(dv_ref)

        def tile(j, i, diagonal):
            qs = pl.ds(pl.multiple_of(i * t, t), t)
            chains = [(hh, half) for hh in range(hps) for half in range(2)]
            lanes = lambda hh: slice(HEAD_PAD * hh, HEAD_PAD * (hh + 1))
            keys = lambda half: pl.ds(pl.multiple_of(j * t + th * half, th), th)
            sts, dpts = {}, {}
            for hh, half in chains:
                ls_, ks = lanes(hh), keys(half)
                st = lax.dot_general(k_ref[ks, ls_], q_ref[qs, ls_], NT_DIMS, preferred_element_type=F32)
                if diagonal:
                    row = lax.broadcasted_iota(I32, (th, t), 0) + th * half
                    col = lax.broadcasted_iota(I32, (th, t), 1)
                    st = jnp.where(row <= col, st, -jnp.inf)
                sts[hh, half] = st
                dpts[hh, half] = lax.dot_general(v_ref[ks, ls_], do_ref[qs, ls_], NT_DIMS, preferred_element_type=F32)
            pts, dsts = {}, {}
            for hh, half in chains:
                pt = jnp.exp2(sts[hh, half] * ATT_SCALE_LOG2E - lse_ref[hh, i] * LOG2E)
                pts[hh, half] = pt.astype(BF16)
                dsts[hh, half] = (pt * (dpts[hh, half] - dlt_ref[hh, i])).astype(BF16)
            for hh in range(hps):
                ls_ = lanes(hh)
                dq_acc = None
                for half in range(2):
                    ks = keys(half)
                    dv_ref[ks, ls_] += jnp.dot(pts[hh, half], do_ref[qs, ls_], preferred_element_type=F32)
                    dk_ref[ks, ls_] += jnp.dot(dsts[hh, half], q_ref[qs, ls_], preferred_element_type=F32)
                    part = lax.dot_general(dsts[hh, half], k_ref[ks, ls_], TN_DIMS, preferred_element_type=F32)
                    dq_acc = part if dq_acc is None else dq_acc + part
                dq_ref[qs, ls_] += dq_acc

        def key_tile(j, carry):
            tile(j, j, True)

            def query_tile(i, c2):
                tile(j, i, False)
                return c2

            lax.fori_loop(j + 1, nb, query_tile, 0)
            return carry

        lax.fori_loop(0, nb, key_tile, 0)
        dq_ref[...] *= ATT_SCALE
        dk_ref[...] *= ATT_SCALE

    hs = pl.BlockSpec((None, s, hw), lambda bb, h: (bb, 0, h))
    ls = pl.BlockSpec((None, hps, nb, 1, t), lambda bb, h: (bb, h, 0, 0, 0))
    return pl.pallas_call(
        body, name=name, grid=(b, MLA_HEADS // hps),
        in_specs=[hs, hs, hs, hs, ls, ls], out_specs=[hs, hs, hs],
        out_shape=[SDS((b, s, hd), F32)] * 3, compiler_params=_cparams(2))(q, k, v, do, lse_r, dlt_r)


def out_proj(ys, attn, mnw, wo, x, gate, above, name):
    b, s, d = x.shape
    tm = min(512, s)

    def body(ys_ref, at_ref, mnw_ref, wo_ref, x_ref, g_ref, nw_ref, sc_ref, sh_ref, xn_ref, o_ref, ym_ref, h_ref):
        av = at_ref[...]
        ym = (av * _rms(av) * mnw_ref[...]).astype(BF16)
        ym_ref[...] = ym
        o = jnp.dot(ys_ref[...], wo_ref[0:D_SSD, :], preferred_element_type=F32)
        o += jnp.dot(ym, wo_ref[D_SSD:2 * D_SSD, :], preferred_element_type=F32)
        xn = x_ref[...] + g_ref[...] * o
        xn_ref[...] = xn
        o_ref[...] = o.astype(BF16)
        h_ref[...] = _norm_mod_tile(xn, nw_ref, sc_ref, sh_ref)

    return pl.pallas_call(
        body, name=name, grid=(b, s // tm),
        in_specs=[_row(tm, D_SSD), _row(tm, D_SSD), _full((1, D_SSD)), _full(wo.shape), _row(tm, d), _bvec(d),
                  _full((1, d)), _bvec(d), _bvec(d)],
        out_specs=[_row(tm, d), _row(tm, d), _row(tm, D_SSD), _row(tm, d)],
        out_shape=[SDS((b, s, d), F32), SDS((b, s, d), BF16), SDS((b, s, D_SSD), BF16), SDS((b, s, d), BF16)],
        compiler_params=_cparams(2))(ys, attn, mnw, wo, x, gate, *above)


def out_proj_bwd(dout, attn, mnw, wo, name):
    b, s, d = dout.shape
    tm = min(512, s)

    def body(do_ref, at_ref, mnw_ref, wo_ref, dys_ref, dat_ref, dlt_ref, dw_ref):
        @pl.when(_first_step())
        def _():
            dw_ref[...] = jnp.zeros_like(dw_ref)
        dov = do_ref[...]
        dys_ref[...] = lax.dot_general(dov, wo_ref[0:D_SSD, :], NT_DIMS, preferred_element_type=F32)
        dym = lax.dot_general(dov, wo_ref[D_SSD:2 * D_SSD, :], NT_DIMS, preferred_element_type=F32)
        av = at_ref[...]
        r = _rms(av)
        n = av * r
        dw_ref[...] += jnp.sum(dym * n, axis=0, keepdims=True)
        dat = _rms_bwd(dym * mnw_ref[...], n, r)
        dat_ref[...] = dat.astype(BF16)
        prod = dat * av
        for h in range(MLA_HEADS):
            dlt_ref[h] = jnp.sum(prod[:, HEAD_PAD * h:HEAD_PAD * (h + 1)], axis=1, keepdims=True)

    return pl.pallas_call(
        body, name=name, grid=(b, s // tm),
        in_specs=[_row(tm, d), _row(tm, D_SSD), _full((1, D_SSD)), _full(wo.shape)],
        out_specs=[_row(tm, D_SSD), _row(tm, D_SSD),
                   pl.BlockSpec((None, MLA_HEADS, tm, 1), lambda bb, i: (bb, 0, i, 0)), _full((1, D_SSD))],
        out_shape=[SDS((b, s, D_SSD), F32), SDS((b, s, D_SSD), BF16), SDS((b, MLA_HEADS, s, 1), F32),
                   SDS((1, D_SSD), F32)],
        compiler_params=_cparams(2))(dout, attn, mnw, wo)


def adaln_fwd(c_all, w_ada, b_ada, name):
    nb, d = c_all.shape
    n = w_ada.shape[1]

    def body(c_ref, w_ref, b_ref, m_ref, ca_ref):
        cv = c_ref[...]
        ca = (cv * _sigmoid(cv)).astype(BF16)
        ca_ref[...] = ca
        m_ref[...] = jnp.dot(ca, w_ref[...].astype(BF16), preferred_element_type=F32) + b_ref[...]

    return pl.pallas_call(
        body, name=name, out_shape=[SDS((nb, n), F32), SDS((nb, d), BF16)],
        compiler_params=pltpu.CompilerParams(vmem_limit_bytes=VMEM_LIMIT))(c_all, w_ada, b_ada)


def adaln_bwd(c_act, dmod_cols, name):
    d, n = c_act.shape[1], dmod_cols.shape[1]

    def body(c_ref, dm_ref, gw_ref):
        gw_ref[...] = lax.dot_general(c_ref[...], dm_ref[...].astype(BF16), TN_DIMS, preferred_element_type=F32)

    return pl.pallas_call(
        body, name=name, out_shape=SDS((d, n), F32),
        compiler_params=pltpu.CompilerParams(vmem_limit_bytes=VMEM_LIMIT))(c_act, dmod_cols)


def sum_rows(x, name):
    def body(x_ref, o_ref):
        o_ref[...] = jnp.sum(x_ref[...], axis=0, keepdims=True)
    return pl.pallas_call(body, name=name, out_shape=SDS((1, x.shape[1]), F32))(x)


def squeeze_heads(x, et_mat, name):
    def body(x_ref, et_ref, o_ref):
        xv = jnp.broadcast_to(x_ref[...], (8, x.shape[1]))
        o_ref[...] = _dot_01(xv, et_ref[...], 3)[0:1, :]
    return pl.pallas_call(body, name=name, out_shape=SDS((1, LANES), F32))(x, et_mat)


def sum_blocks(x, name):
    n, r, c = x.shape
    tr = next(cand for cand in (256, 128, 64, 32, 16, 8) if r % cand == 0)

    def body(x_ref, o_ref):
        acc = x_ref[0].astype(F32)
        for k in range(1, n):
            acc += x_ref[k].astype(F32)
        o_ref[...] = acc

    return pl.pallas_call(
        body, name=name, grid=(r // tr,), in_specs=[pl.BlockSpec((n, tr, c), lambda i: (0, i, 0))],
        out_specs=pl.BlockSpec((tr, c), lambda i: (i, 0)), out_shape=SDS((r, c), F32),
        compiler_params=_cparams(1))(x)


def _adam_math(w, g, m, v):
    m = ADAM_B1 * m + (1.0 - ADAM_B1) * g
    v = ADAM_B2 * v + (1.0 - ADAM_B2) * (g * g)
    m_hat = m / (1.0 - ADAM_B1 ** ADAM_STEP)
    v_hat = v / (1.0 - ADAM_B2 ** ADAM_STEP)
    return -ADAM_LR * (m_hat / (jnp.sqrt(v_hat) + ADAM_EPS) + ADAM_WD * w), m, v


def adamw(w, g, m, v, name):
    r, c = w.shape
    tr = r
    for cand in (512, 256, 128, 64, 32, 16, 8):
        if r % cand == 0 and cand * c * 4 <= 2 * 1024 * 1024:
            tr = cand
            break

    def body(w_ref, g_ref, m_ref, v_ref, d_ref, mo_ref, vo_ref):
        d_ref[...], mo_ref[...], vo_ref[...] = _adam_math(w_ref[...], g_ref[...], m_ref[...], v_ref[...])

    spec = pl.BlockSpec((tr, c), lambda i: (i, 0))
    return pl.pallas_call(
        body, name=name, grid=(r // tr,), in_specs=[spec] * 4, out_specs=[spec] * 3,
        out_shape=[SDS((r, c), F32)] * 3, compiler_params=_cparams(1))(w, g, m, v)


PACK = {"ffn1_w_gate": (352, 352), "ffn1_w_up": (352, 352), "ffn1_w_down": (352, 352),
        "ffn2_w_gate": (352, 352), "ffn2_w_up": (352, 352), "ffn2_w_down": (352, 352),
        "w_out": (256, 256), "w_in": (406, 416), "w_ukv": (48, 48), "w_uq": (36, 48)}
TRANSPOSED = ("ffn1_w_gate", "ffn1_w_up", "ffn2_w_gate", "ffn2_w_up", "w_in", "w_ukv", "w_uq")
GATHER_GROUPS = (("ffn1_w_gate", "ffn1_w_up"), ("ffn1_w_down",),
                 ("w_in", "w_ukv", "w_uq", "w_out", "ffn2_w_gate", "ffn2_w_up", "ffn2_w_down"))
GRAD_GROUPS = (("ffn2", ("ffn2_w_gate", "ffn2_w_up", "ffn2_w_down")), ("mixer", ("w_out", "w_in", "w_ukv", "w_uq")),
               ("ffn1_down", ("ffn1_w_down",)), ("ffn1_gate", ("ffn1_w_gate",)), ("ffn1_up", ("ffn1_w_up",)))


def _pack_offsets(names):
    off, o = {}, 0
    for n in names:
        off[n] = (o, PACK[n][0])
        o += PACK[n][1]
    return off, o


def _shard_to_rows(name, w):
    w = w[0]
    if name in TRANSPOSED:
        w = w.T
    return w.reshape(-1, D_MODEL)


def _rows_to_shard(name, rows, like):
    shp = like.shape[1:]
    if name in TRANSPOSED:
        return rows.reshape(shp[1], shp[0]).T[None]
    return rows.reshape(shp)[None]


def _pack_shards(ws, names, dtype):
    parts = []
    for name in names:
        real, padded = PACK[name]
        rows = _shard_to_rows(name, ws[name]).astype(dtype)
        if padded > real:
            rows = jnp.pad(rows, ((0, padded - real), (0, 0)))
        parts.append(rows)
    return jnp.concatenate(parts, axis=0)


def _grad_rows(name, gw):
    real, padded = PACK[name]
    if name == "w_in":
        rows = _in_proj_rows_inv(gw).reshape(N_DEV, -1, D_MODEL)
    elif name == "w_ukv":
        hd = MLA_HEADS * HEAD_PAD
        rows = jnp.concatenate([gw[:hd].reshape(MLA_HEADS, HEAD_PAD, KV_LORA)[:, :QK_NOPE],
                                gw[hd:].reshape(MLA_HEADS, V_HEAD, KV_LORA)], axis=1).reshape(N_DEV, -1, D_MODEL)
    elif name == "w_uq":
        rows = gw.reshape(MLA_HEADS, HEAD_PAD, Q_LORA)[:, :QK_DIM].reshape(N_DEV, -1, D_MODEL)
    else:
        rows = gw.reshape(N_DEV, -1, D_MODEL)
    if padded > real:
        rows = jnp.pad(rows, ((0, 0), (0, padded - real), (0, 0)))
    return rows


def _pack_rows(arrs):
    parts = []
    for a in arrs:
        flat = a.reshape(-1).astype(F32)
        pad = (-flat.shape[0]) % D_MODEL
        if pad:
            flat = jnp.pad(flat, (0, pad))
        parts.append(flat.reshape(-1, D_MODEL))
    out = jnp.concatenate(parts, axis=0)
    pad = (-out.shape[0]) % 8
    if pad:
        out = jnp.pad(out, ((0, pad), (0, 0)))
    return out


def _unpack_rows(packed, shapes):
    out, row = [], 0
    for shp in shapes:
        n = math.prod(shp)
        nrow = -(-n // D_MODEL)
        out.append(packed[row:row + nrow].reshape(-1)[:n].reshape(shp))
        row += nrow
    return out


def _in_proj_rows(w_t):
    return jnp.concatenate([w_t[0:2560], w_t[2576:2960], w_t[2960:3216], w_t[2560:2576], w_t[3216:3248],
                            jnp.zeros((D_IN_PAD - D_IN, D_MODEL), w_t.dtype)], axis=0)


def _in_proj_rows_inv(d):
    return jnp.concatenate([d[0:2560], d[3200:3216], d[2560:2944], d[2944:3200], d[3216:3248]], axis=0)


def _rope_tables(positions):
    inv_freq = ROPE_THETA ** (-jnp.arange(0, QK_ROPE, 2, dtype=F32) / QK_ROPE)
    ang = positions[..., None].astype(F32) * inv_freq
    cos, sin = jnp.cos(ang), jnp.sin(ang)
    one = jnp.ones(ang.shape[:2] + (QK_NOPE,), F32)
    zero = jnp.zeros_like(one)
    z16, z32, o32 = zero[..., :16], zero[..., :32], one[..., :32]
    cc = jnp.concatenate([one, cos, cos, o32], axis=-1)
    sp = jnp.concatenate([zero, z16, sin, z32], axis=-1)
    sm = jnp.concatenate([zero, -sin, z16, z32], axis=-1)
    return cc, sp, sm


def weight_views(gathered):
    def _seg(name):
        names, g = next((names, g) for names, g in zip(GATHER_GROUPS, gathered) if name in names)
        o, r = _pack_offsets(names)[0][name]
        return g[:, o:o + r]

    full = lambda name: _seg(name).reshape(-1, D_MODEL)
    ukv = _seg("w_ukv").reshape(MLA_HEADS, QK_NOPE + V_HEAD, KV_LORA)
    wukv_t = jnp.concatenate([jnp.pad(ukv[:, :QK_NOPE], ((0, 0), (0, HEAD_PAD - QK_NOPE), (0, 0))).reshape(-1, KV_LORA),
                              ukv[:, QK_NOPE:].reshape(-1, KV_LORA)], axis=0)
    uq = _seg("w_uq").reshape(MLA_HEADS, QK_DIM, Q_LORA)
    wuq_t = jnp.pad(uq, ((0, 0), (0, HEAD_PAD - QK_DIM), (0, 0))).reshape(-1, Q_LORA)
    return dict(wg1_t=full("ffn1_w_gate"), wu1_t=full("ffn1_w_up"), wd1=full("ffn1_w_down"),
                wg2_t=full("ffn2_w_gate"), wu2_t=full("ffn2_w_up"), wd2=full("ffn2_w_down"),
                wo=full("w_out"), win_t=_in_proj_rows(full("w_in")), wukv_t=wukv_t, wuq_t=wuq_t)


def _ffn_bwd(tag, dxn, do, dgate, x, h, gg, uu, a, sc, norm_w, wg_t, wu_t, wd, below):
    f2 = wd.shape[0] // 2
    dwd = mm_tn(a, do, f2, D_MODEL, tag + "_dwd")
    dgg, duu, dx, dsc, dsh, dnw, *nxt = ffn_bwd_main(do, wd, gg, uu, wg_t, wu_t, x, dxn, norm_w, sc, tag + "_bwd", below)
    dwg_t = mm_tn(dgg, h, f2, D_MODEL, tag + "_dwg")
    dwu_t = mm_tn(duu, h, f2, D_MODEL, tag + "_dwu")
    return dx, (dsh, dsc, dgate), dnw, (dwg_t, dwu_t, dwd), nxt


def local_step(x, tgt, positions, mod, wv, p):
    nb, s, d = x.shape
    sh1, sc1, g1, sh2, sc2, g2, sh3, sc3, g3 = mod
    cc, sp, sm = _rope_tables(positions)
    lane_head = jnp.arange(D_SSD, dtype=I32)[None, :] // SSD_HEAD_DIM
    e_mat = (lane_head == jnp.arange(LANES, dtype=I32)[:, None]).astype(BF16)
    et_mat = e_mat.T
    rr, cl = jnp.arange(LANES, dtype=I32)[:, None], jnp.arange(LANES, dtype=I32)[None, :]
    place = ((cl == rr + (QK_NOPE - SSD_HEADS)) & (rr >= SSD_HEADS) & (rr < SSD_HEADS + QK_ROPE)).astype(F32)
    dtb = jnp.pad(p["dt_bias"], ((0, 0), (0, LANES - SSD_HEADS)))
    alog = jnp.pad(p["a_log"], ((0, 0), (0, LANES - SSD_HEADS)))
    dskip_e = jnp.repeat(p["d_skip"], SSD_HEAD_DIM, axis=1)

    h1 = norm_mod(x, p["norm_ffn1"], sc1, sh1, "ffn1_norm")
    gg1, uu1, a1 = ffn_up(h1, wv["wg1_t"], wv["wu1_t"], "ffn1_up")
    x1, o1, h2 = ffn_down(a1, wv["wd1"], x, g1, 0.5, "ffn1_down", (p["norm_mix"], sc2, sh2))
    z, u, cq, ckv, misc = in_proj(h2, wv["win_t"], "in_proj")
    xs, bm, cm_ = conv_fwd(u, p["conv_w"], p["conv_b"], "conv_fwd")
    ys, y, prev = ssd_fwd(xs, bm, cm_, misc, z, dtb, alog, dskip_e, p["ssd_norm_w"], e_mat, "ssd_fwd")
    q, k, v, qn, kvn = qkv_fwd(cq, ckv, misc, cc, sp, sm, p["q_norm_w"], p["kv_norm_w"], wv["wuq_t"], wv["wukv_t"],
                               place, "qkv_fwd")
    attn, lse = flash_fwd(q, k, v, "flash_fwd")
    x2, o2, ym, h3 = out_proj(ys, attn, p["mla_norm_w"], wv["wo"], x1, g2, (p["norm_ffn2"], sc3, sh3), "out_proj")
    gg3, uu3, a3 = ffn_up(h3, wv["wg2_t"], wv["wu2_t"], "ffn2_up")
    x3, o3 = ffn_down(a3, wv["wd2"], x2, g3, 0.5, "ffn2_down")
    loss, dx3, dnfin, do3, dg3 = final_loss(x3, p["norm_final"], tgt, (o3, g3, 0.5), "final_loss")

    dx2, dmod3, dnf2, (dwg2, dwu2, dwd2), (dout, dg2) = _ffn_bwd(
        "ffn2", dx3, do3, dg3, x2, h3, gg3, uu3, a3, sc3, p["norm_ffn2"], wv["wg2_t"], wv["wu2_t"], wv["wd2"],
        (o2, g2, 1.0))
    dys, dattn, dlt, dmlan = out_proj_bwd(dout, attn, p["mla_norm_w"], wv["wo"], "out_proj_bwd")
    dwo = jnp.concatenate([mm_tn(ys, dout, D_SSD, D_MODEL, "dwo_ssd"), mm_tn(ym, dout, D_SSD, D_MODEL, "dwo_mla")], axis=0)
    dxs, dbm, dcm, dz, ddt, dssdn, ddsk_lane, ddtb, dalog = ssd_bwd(
        dys, y, z, xs, bm, cm_, misc, prev, dtb, alog, dskip_e, p["ssd_norm_w"], e_mat, et_mat, "ssd_bwd")
    dq, dk, dv = flash_bwd(q, k, v, dattn, lse, dlt, "flash_bwd")
    dcq, dckv, dmisc, dqp, dkvc, dqn, dkvn = qkv_bwd(dq, dk, dv, ddt, cq, ckv, cc, sp, sm, p["q_norm_w"], p["kv_norm_w"],
                                                     wv["wuq_t"], wv["wukv_t"], place.T, "qkv_bwd")
    dwuq = mm_tn(dqp, qn, MLA_HEADS * HEAD_PAD, Q_LORA, "dwuq")
    dwukv = mm_tn(dkvc, kvn, MLA_HEADS * HEAD_PAD, KV_LORA, "dwukv")
    dvv, dconv = conv_bwd_a(dxs, dbm, dcm, u, p["conv_w"], p["conv_b"], "conv_bwd_a")
    dconv = conv_grads_fold(dconv, "conv_grads_fold")
    du = conv_bwd_b(dvv, p["conv_w"], "conv_bwd_b")
    dproj = jnp.concatenate([dz, du, dcq, dckv, dmisc], axis=-1)
    dwin = mm_tn(dproj, h2, D_IN_PAD // 2, D_MODEL, "dwin")
    dx1, dsc2, dsh2, dnmix, do1, dg1 = dh_norm_bwd([dproj], [wv["win_t"]], x1, dx2, p["norm_mix"], sc2, "mix_dh",
                                                   (o1, g1, 0.5))
    dx0, dmod1, dnf1, (dwg1, dwu1, dwd1), _ = _ffn_bwd(
        "ffn1", dx1, do1, dg1, x, h1, gg1, uu1, a1, sc1, p["norm_ffn1"], wv["wg1_t"], wv["wu1_t"], wv["wd1"], None)

    dmod = jnp.concatenate([*dmod1, dsh2, dsc2, dg2, *dmod3], axis=1).reshape(nb, N_MOD * d)
    return dict(
        loss=loss, dx=dx0, dmod=dmod, norm_ffn1=dnf1, norm_mix=dnmix, norm_ffn2=dnf2, norm_final=dnfin,
        ssd_norm_w=dssdn, mla_norm_w=dmlan, q_norm_w=dqn, kv_norm_w=dkvn,
        dt_bias=ddtb[:, :SSD_HEADS], a_log=dalog[:, :SSD_HEADS],
        d_skip=squeeze_heads(ddsk_lane, et_mat, "d_skip_heads")[:, :SSD_HEADS],
        conv_b=dconv[4:5], conv_w=dconv[0:4],
        gw=dict(ffn1_w_gate=dwg1, ffn1_w_up=dwu1, ffn1_w_down=dwd1, ffn2_w_gate=dwg2, ffn2_w_up=dwu2, ffn2_w_down=dwd2,
                w_out=dwo, w_in=dwin, w_ukv=dwukv, w_uq=dwuq))


def kernel(x, c, positions, w_ada, b_ada, norm_ffn1, ffn1_w_gate, ffn1_w_up, ffn1_w_down, norm_mix, w_in, conv_w, conv_b, dt_bias, a_log, d_skip, ssd_norm_w, q_norm_w, w_uq, kv_norm_w, w_ukv, mla_norm_w, w_out, norm_ffn2, ffn2_w_gate, ffn2_w_up, ffn2_w_down, norm_final, loss_target, m_w_ada, m_b_ada, m_norm_ffn1, m_ffn1_w_gate, m_ffn1_w_up, m_ffn1_w_down, m_norm_mix, m_w_in, m_conv_w, m_conv_b, m_dt_bias, m_a_log, m_d_skip, m_ssd_norm_w, m_q_norm_w, m_w_uq, m_kv_norm_w, m_w_ukv, m_mla_norm_w, m_w_out, m_norm_ffn2, m_ffn2_w_gate, m_ffn2_w_up, m_ffn2_w_down, m_norm_final, v_w_ada, v_b_ada, v_norm_ffn1, v_ffn1_w_gate, v_ffn1_w_up, v_ffn1_w_down, v_norm_mix, v_w_in, v_conv_w, v_conv_b, v_dt_bias, v_a_log, v_d_skip, v_ssd_norm_w, v_q_norm_w, v_w_uq, v_kv_norm_w, v_w_ukv, v_mla_norm_w, v_w_out, v_norm_ffn2, v_ffn2_w_gate, v_ffn2_w_up, v_ffn2_w_down, v_norm_final):
    names = ["w_ada", "b_ada", "norm_ffn1", "ffn1_w_gate", "ffn1_w_up", "ffn1_w_down", "norm_mix", "w_in", "conv_w",
             "conv_b", "dt_bias", "a_log", "d_skip", "ssd_norm_w", "q_norm_w", "w_uq", "kv_norm_w", "w_ukv",
             "mla_norm_w", "w_out", "norm_ffn2", "ffn2_w_gate", "ffn2_w_up", "ffn2_w_down", "norm_final"]
    W = dict(zip(names, (w_ada, b_ada, norm_ffn1, ffn1_w_gate, ffn1_w_up, ffn1_w_down, norm_mix, w_in, conv_w, conv_b, dt_bias, a_log, d_skip, ssd_norm_w, q_norm_w, w_uq, kv_norm_w, w_ukv, mla_norm_w, w_out, norm_ffn2, ffn2_w_gate, ffn2_w_up, ffn2_w_down, norm_final)))
    M = dict(zip(names, (m_w_ada, m_b_ada, m_norm_ffn1, m_ffn1_w_gate, m_ffn1_w_up, m_ffn1_w_down, m_norm_mix, m_w_in, m_conv_w, m_conv_b, m_dt_bias, m_a_log, m_d_skip, m_ssd_norm_w, m_q_norm_w, m_w_uq, m_kv_norm_w, m_w_ukv, m_mla_norm_w, m_w_out, m_norm_ffn2, m_ffn2_w_gate, m_ffn2_w_up, m_ffn2_w_down, m_norm_final)))
    V = dict(zip(names, (v_w_ada, v_b_ada, v_norm_ffn1, v_ffn1_w_gate, v_ffn1_w_up, v_ffn1_w_down, v_norm_mix, v_w_in, v_conv_w, v_conv_b, v_dt_bias, v_a_log, v_d_skip, v_ssd_norm_w, v_q_norm_w, v_w_uq, v_kv_norm_w, v_w_ukv, v_mla_norm_w, v_w_out, v_norm_ffn2, v_ffn2_w_gate, v_ffn2_w_up, v_ffn2_w_down, v_norm_final)))

    nb, s, d = x.shape
    me = 4 * lax.axis_index("x") + 2 * lax.axis_index("y") + lax.axis_index("c")
    n_ada = w_ada.shape[2]

    cshape = [(nb, d), conv_w.shape[1:]]
    cg = all_gather8(_pack_rows([c, conv_w[0]]), "gather_c")
    c_all = jnp.stack([_unpack_rows(cg[k], cshape)[0] for k in range(N_DEV)]).reshape(N_DEV * nb, d)
    conv_w_full = jnp.concatenate([_unpack_rows(cg[k], cshape)[1] for k in range(N_DEV)], axis=1)
    g_ffn1 = all_gather8(_pack_shards(W, GATHER_GROUPS[0], BF16), "gather_w_ffn1")

    b_ada_cols = lax.dynamic_slice(b_ada, (0, me * n_ada), (1, n_ada))
    mod_cols, c_act = adaln_fwd(c_all, w_ada[0], b_ada_cols, "adaln_fwd")
    mod_g = all_gather8(mod_cols, "gather_mod")
    g_ffn1, mod_g, down1, rest = lax.optimization_barrier(
        (g_ffn1, mod_g, _pack_shards(W, GATHER_GROUPS[1], BF16), _pack_shards(W, GATHER_GROUPS[2], BF16)))
    g_down1 = sc_all_gather8(down1, "gather_w_ffn1_down", 1)
    wv = weight_views((g_ffn1, g_down1, sc_all_gather8(rest, "gather_w_rest", 7)))
    mod = lax.dynamic_slice(mod_g, (0, me * nb, 0), (N_DEV, nb, n_ada)).transpose(1, 0, 2).reshape(nb, N_MOD, 1, d)
    mod = [mod[:, k] for k in range(N_MOD)]

    P = dict(W)
    P["conv_w"] = conv_w_full
    P["norm_final"] = norm_final.reshape(1, d)
    R = local_step(x, loss_target, positions, mod, wv, P)

    dmod = R["dmod"]
    partial_shapes = [(1,), (1, d), (1, d), (1, d), (1, d), (1, d), (1, d), (1, Q_LORA), (1, KV_LORA),
                      (1, SSD_HEADS), (1, SSD_HEADS), (1, SSD_HEADS), (1, D_CONV), (4, D_CONV), (1, N_MOD * d),
                      (nb, N_MOD * d)]
    partial = _pack_rows([R["loss"][0, :1], R["norm_ffn1"], R["norm_mix"], R["norm_ffn2"], R["norm_final"],
                          R["ssd_norm_w"], R["mla_norm_w"], R["q_norm_w"], R["kv_norm_w"],
                          R["dt_bias"], R["a_log"], R["d_skip"], R["conv_b"], R["conv_w"],
                          sum_rows(dmod, "dmod_rows"), dmod])
    partial_g = all_gather8(partial, "gather_partials")
    (loss, g_nf1, g_nmix, g_nf2, g_nfin, g_ssdn, g_mlan, g_qn, g_kvn, g_dtb, g_alog, g_dskip, g_convb, g_convw,
     g_bada, _) = _unpack_rows(sum_blocks(partial_g, "sum_partials"), partial_shapes)
    dmod_all = jnp.stack([_unpack_rows(partial_g[k], partial_shapes)[-1] for k in range(N_DEV)]).reshape(N_DEV * nb, -1)
    g_wada = adaln_bwd(c_act, lax.dynamic_slice(dmod_all, (0, me * n_ada), (N_DEV * nb, n_ada)), "adaln_bwd")
    n_cw = conv_w.shape[2]
    G = {"w_ada": g_wada[None], "b_ada": g_bada, "norm_ffn1": g_nf1, "norm_mix": g_nmix, "norm_ffn2": g_nf2,
         "norm_final": g_nfin.reshape(d), "ssd_norm_w": g_ssdn, "mla_norm_w": g_mlan, "q_norm_w": g_qn,
         "kv_norm_w": g_kvn, "dt_bias": g_dtb, "a_log": g_alog, "d_skip": g_dskip, "conv_b": g_convb,
         "conv_w": lax.dynamic_slice(g_convw, (0, me * n_cw), (4, n_cw))[None]}

    DW, NM, NV = {}, {}, {}
    gw = R["gw"]
    for k, (tag, group) in enumerate(GRAD_GROUPS):
        send = jnp.concatenate([_grad_rows(name, gw[name]) for name in group], axis=1).astype(BF16)
        recv = sc_all_to_all8(send, "exchange_" + tag, 2 + k)
        gsum = sum_blocks(recv, "sum_" + tag)
        for name, (o, r) in _pack_offsets(group)[0].items():
            G[name] = _rows_to_shard(name, gsum[o:o + r], W[name])
            DW[name], NM[name], NV[name] = [t[None] for t in adamw(W[name][0], G[name][0], M[name][0], V[name][0],
                                                                  "adamw_" + name)]
    dwa, nma, nva = adamw(w_ada[0], g_wada, m_w_ada[0], v_w_ada[0], "adamw_w_ada")
    DW["w_ada"], NM["w_ada"], NV["w_ada"] = dwa[None], nma[None], nva[None]
    small = [n for n in names if n not in DW]
    shapes = [W[n].shape for n in small]
    outs = adamw(_pack_rows([W[n] for n in small]), _pack_rows([G[n] for n in small]),
                 _pack_rows([M[n] for n in small]), _pack_rows([V[n] for n in small]), "adamw_small")
    for res, dst in zip(outs, (DW, NM, NV)):
        for n, t in zip(small, _unpack_rows(res, shapes)):
            dst[n] = t
    return (loss.reshape(()), R["dx"], *[G[n] for n in names], *[DW[n] for n in names], *[NM[n] for n in names],
            *[NV[n] for n in names])
```

```python
import math

import jax
import jax.numpy as jnp
from jax import lax
from jax.experimental import pallas as pl
from jax.experimental.pallas import tpu as pltpu
from jax.experimental.pallas import tpu_sc as plsc

F32, BF16, I32 = jnp.float32, jnp.bfloat16, jnp.int32
HI = lax.Precision.HIGHEST
SDS = jax.ShapeDtypeStruct
MESH = pl.DeviceIdType.MESH

D_MODEL = 1024
D_FF = 2816
D_SSD = 1024
SSD_HEADS = 16
SSD_HEAD_DIM = 64
SSD_GROUPS = 2
SSD_STATE = 128
CHUNK = 128
MLA_HEADS = 8
QK_NOPE = 64
QK_ROPE = 32
QK_DIM = 96
V_HEAD = 128
Q_LORA = 384
KV_LORA = 256
ROPE_THETA = 10000.0
N_MOD = 9
EPS = 1e-6
D_CONV = 1536
D_IN = 3248
D_IN_PAD = 3328
HEAD_PAD = 128
N_DEV = 8
ADAM_LR, ADAM_B1, ADAM_B2, ADAM_EPS, ADAM_WD, ADAM_STEP = 0.001, 0.9, 0.999, 1e-08, 0.01, 10

SAVED_ACT = BF16
VMEM_LIMIT = 56 * 1024 * 1024
LANES = 128
NT_DIMS = (((1,), (1,)), ((), ()))
TN_DIMS = (((0,), (0,)), ((), ()))


def _cparams(n_axes):
    return pltpu.CompilerParams(dimension_semantics=("arbitrary",) * n_axes, vmem_limit_bytes=VMEM_LIMIT)


def _row(tm, d):
    return pl.BlockSpec((None, tm, d), lambda b, i: (b, i, 0))


def _bvec(d):
    return pl.BlockSpec((None, 1, d), lambda b, i: (b, 0, 0))


def _full(shape):
    n = len(shape)
    return pl.BlockSpec(shape, lambda *_: (0,) * n)


def _sigmoid(x):
    return 1.0 / (1.0 + jnp.exp(-x))


def _softplus(x):
    return jnp.maximum(x, 0.0) + jnp.log(1.0 + jnp.exp(-jnp.abs(x)))


def _rms(x):
    return lax.rsqrt(jnp.mean(x * x, axis=-1, keepdims=True) + EPS)


def _rms_bwd(dn, n, r):
    return r * (dn - n * jnp.mean(dn * n, axis=-1, keepdims=True))


def _first_step():
    return (pl.program_id(0) == 0) & (pl.program_id(1) == 0)


def all_gather8(x, name):
    r, c = x.shape

    def body(x_ref, out_ref, send_sems, recv_sems, local_sem):
        mx, my, mc = lax.axis_index("x"), lax.axis_index("y"), lax.axis_index("c")
        me, sibling = (mx, my, mc), (mx, my, 1 - mc)
        chips = [(1 - mx, my), (mx, 1 - my), (1 - mx, 1 - my)]

        def rows(px, py, pc):
            return out_ref.at[4 * px + 2 * py + pc]

        def copy(k, block, to, src=None):
            return pltpu.make_async_remote_copy(
                src_ref=rows(*block) if src is None else src, dst_ref=rows(*block),
                send_sem=send_sems.at[k], recv_sem=recv_sems.at[k], device_id=to, device_id_type=MESH)

        mine = pltpu.make_async_copy(x_ref, rows(*me), local_sem)
        mine.start()
        first = [copy(0, me, sibling, src=x_ref)]
        first += [copy(1 + j, me, (*chip, mc), src=x_ref) for j, chip in enumerate(chips)]
        for cp in first:
            cp.start()
        passed = [copy(4 + j, (*chip, mc), sibling) for j, chip in enumerate(chips)]
        for j, chip in enumerate(chips):
            copy(1 + j, (*chip, mc), me).wait_recv()
            passed[j].start()
        copy(0, sibling, me).wait_recv()
        for j, chip in enumerate(chips):
            copy(4 + j, (*chip, 1 - mc), me).wait_recv()
        for cp in first + passed:
            cp.wait_send()
        mine.wait()

    return pl.pallas_call(
        body, name=name,
        out_shape=SDS((N_DEV, r, c), x.dtype),
        in_specs=[pl.BlockSpec(memory_space=pl.ANY)],
        out_specs=pl.BlockSpec(memory_space=pl.ANY),
        scratch_shapes=[pltpu.SemaphoreType.DMA((7,)), pltpu.SemaphoreType.DMA((7,)), pltpu.SemaphoreType.DMA],
    )(x)


def all_to_all8(x, name):
    _, r, c = x.shape

    def body(x_ref, out_ref, send_sems, recv_sems, local_sem):
        mx, my, mc = lax.axis_index("x"), lax.axis_index("y"), lax.axis_index("c")
        me = 4 * mx + 2 * my + mc
        mine = pltpu.make_async_copy(x_ref.at[me], out_ref.at[me], local_sem)
        mine.start()
        copies = []
        for rel in range(1, N_DEV):
            px = 1 - mx if rel & 4 else mx
            py = 1 - my if rel & 2 else my
            pc = 1 - mc if rel & 1 else mc
            cp = pltpu.make_async_remote_copy(
                src_ref=x_ref.at[4 * px + 2 * py + pc], dst_ref=out_ref.at[me],
                send_sem=send_sems.at[rel - 1], recv_sem=recv_sems.at[rel - 1],
                device_id=(px, py, pc), device_id_type=MESH)
            cp.start()
            copies.append(cp)
        for cp in copies:
            cp.wait()
        mine.wait()

    return pl.pallas_call(
        body, name=name,
        out_shape=SDS((N_DEV, r, c), x.dtype),
        in_specs=[pl.BlockSpec(memory_space=pl.ANY)],
        out_specs=pl.BlockSpec(memory_space=pl.ANY),
        scratch_shapes=[pltpu.SemaphoreType.DMA((7,)), pltpu.SemaphoreType.DMA((7,)), pltpu.SemaphoreType.DMA],
    )(x)


def _sequencer_kernel(name, collective_id):
    return pl.kernel(
        mesh=plsc.ScalarSubcoreMesh(axis_name="seq", num_cores=1), name=name,
        scratch_types=(pltpu.SemaphoreType.DMA((7,)), pltpu.SemaphoreType.DMA((7,)), pltpu.SemaphoreType.DMA),
        compiler_params=pltpu.CompilerParams(collective_id=collective_id))


def _handshake(peers):
    barrier = pltpu.get_barrier_semaphore()
    for peer in peers:
        pl.semaphore_signal(barrier, inc=1, device_id=peer, device_id_type=MESH)
    pl.semaphore_wait(barrier, len(peers))


def sc_all_gather8(x, name, collective_id):
    r, c = x.shape
    x_ref = jax.new_ref(x, memory_space=pltpu.MemorySpace.HBM)
    out_ref = jax.empty_ref(SDS((N_DEV, r, c), x.dtype), memory_space=pltpu.MemorySpace.HBM)

    @_sequencer_kernel(name, collective_id)
    def launch(send_sems, recv_sems, local_sem):
        mx, my, mc = lax.axis_index("x"), lax.axis_index("y"), lax.axis_index("c")
        me, sibling = (mx, my, mc), (mx, my, 1 - mc)
        chips = [(1 - mx, my), (mx, 1 - my), (1 - mx, 1 - my)]
        _handshake([sibling] + [(*chip, mc) for chip in chips])

        def rows(px, py, pc):
            return out_ref.at[4 * px + 2 * py + pc]

        def copy(k, block, to, src=None):
            return pltpu.make_async_remote_copy(
                src_ref=rows(*block) if src is None else src, dst_ref=rows(*block),
                send_sem=send_sems.at[k], recv_sem=recv_sems.at[k], device_id=to, device_id_type=MESH)

        mine = pltpu.make_async_copy(x_ref, rows(*me), local_sem)
        mine.start()
        first = [copy(0, me, sibling, src=x_ref)]
        first += [copy(1 + j, me, (*chip, mc), src=x_ref) for j, chip in enumerate(chips)]
        for cp in first:
            cp.start()
        passed = [copy(4 + j, (*chip, mc), sibling) for j, chip in enumerate(chips)]
        for j, chip in enumerate(chips):
            copy(1 + j, (*chip, mc), me).wait_recv()
            passed[j].start()
        copy(0, sibling, me).wait_recv()
        for j, chip in enumerate(chips):
            copy(4 + j, (*chip, 1 - mc), me).wait_recv()
        for cp in first + passed:
            cp.wait_send()
        mine.wait()

    launch()
    return out_ref[...]


def sc_all_to_all8(x, name, collective_id):
    x_ref = jax.new_ref(x, memory_space=pltpu.MemorySpace.HBM)
    out_ref = jax.empty_ref(SDS(x.shape, x.dtype), memory_space=pltpu.MemorySpace.HBM)

    @_sequencer_kernel(name, collective_id)
    def launch(send_sems, recv_sems, local_sem):
        mx, my, mc = lax.axis_index("x"), lax.axis_index("y"), lax.axis_index("c")
        me = 4 * mx + 2 * my + mc
        peers = [(1 - mx if rel & 4 else mx, 1 - my if rel & 2 else my, 1 - mc if rel & 1 else mc)
                 for rel in range(1, N_DEV)]
        _handshake(peers)
        mine = pltpu.make_async_copy(x_ref.at[me], out_ref.at[me], local_sem)
        mine.start()
        copies = []
        for k, (px, py, pc) in enumerate(peers):
            cp = pltpu.make_async_remote_copy(
                src_ref=x_ref.at[4 * px + 2 * py + pc], dst_ref=out_ref.at[me],
                send_sem=send_sems.at[k], recv_sem=recv_sems.at[k], device_id=(px, py, pc), device_id_type=MESH)
            cp.start()
            copies.append(cp)
        for cp in copies:
            cp.wait()
        mine.wait()

    launch()
    return out_ref[...]


def norm_mod(x, w, sc, sh, name):
    b, s, d = x.shape
    tm = min(512, s)

    def body(x_ref, w_ref, sc_ref, sh_ref, h_ref):
        xv = x_ref[...]
        n = xv * _rms(xv)
        h_ref[...] = ((n * w_ref[...]) * (1.0 + sc_ref[...]) + sh_ref[...]).astype(BF16)

    return pl.pallas_call(
        body, name=name, grid=(b, s // tm),
        in_specs=[_row(tm, d), _full((1, d)), _bvec(d), _bvec(d)],
        out_specs=_row(tm, d), out_shape=SDS((b, s, d), BF16), compiler_params=_cparams(2))(x, w, sc, sh)


def ffn_up(h, wg_t, wu_t, name):
    b, s, d = h.shape
    f = wg_t.shape[0]
    tm, tn = min(512, s), f // 2

    def body(h_ref, wg_ref, wu_ref, g_ref, u_ref, a_ref):
        hv = h_ref[...]
        g = lax.dot_general(hv, wg_ref[...], NT_DIMS, preferred_element_type=F32)
        u = lax.dot_general(hv, wu_ref[...], NT_DIMS, preferred_element_type=F32)
        g_ref[...] = g.astype(g_ref.dtype)
        u_ref[...] = u.astype(u_ref.dtype)
        a_ref[...] = (g * _sigmoid(g) * u).astype(BF16)

    hs = pl.BlockSpec((None, tm, d), lambda j, bb, i: (bb, i, 0))
    ws = pl.BlockSpec((tn, d), lambda j, bb, i: (j, 0))
    os_ = pl.BlockSpec((None, tm, tn), lambda j, bb, i: (bb, i, j))
    return pl.pallas_call(
        body, name=name, grid=(f // tn, b, s // tm),
        in_specs=[hs, ws, ws], out_specs=[os_, os_, os_],
        out_shape=[SDS((b, s, f), SAVED_ACT), SDS((b, s, f), SAVED_ACT), SDS((b, s, f), BF16)],
        compiler_params=_cparams(3))(h, wg_t, wu_t)


def _norm_mod_tile(xv, w_ref, sc_ref, sh_ref):
    return ((xv * _rms(xv) * w_ref[...]) * (1.0 + sc_ref[...]) + sh_ref[...]).astype(BF16)


def ffn_down(a, wd, x, gate, scale, name, above=None):
    b, s, f = a.shape
    d = wd.shape[1]
    tm = min(512, s)

    def body(a_ref, wd_ref, x_ref, g_ref, *rest):
        xn_ref, o_ref = rest[-3:-1] if above else rest
        o = jnp.dot(a_ref[...], wd_ref[...], preferred_element_type=F32)
        xn = x_ref[...] + (scale * g_ref[...]) * o
        xn_ref[...] = xn
        o_ref[...] = o.astype(BF16)
        if above:
            rest[-1][...] = _norm_mod_tile(xn, *rest[0:3])

    extra = above is not None
    return pl.pallas_call(
        body, name=name, grid=(b, s // tm),
        in_specs=[_row(tm, f), _full((f, d)), _row(tm, d), _bvec(d)] + ([_full((1, d)), _bvec(d), _bvec(d)] if extra else []),
        out_specs=[_row(tm, d), _row(tm, d)] + ([_row(tm, d)] if extra else []),
        out_shape=[SDS((b, s, d), F32), SDS((b, s, d), BF16)] + ([SDS((b, s, d), BF16)] if extra else []),
        compiler_params=_cparams(2))(a, wd, x, gate, *(above or ()))


def ffn_down_final(a, wd, x, gate, scale, w_final, tgt, name):
    b, s, f = a.shape
    d = wd.shape[1]
    tm = min(512, s)

    def body(a_ref, wd_ref, x_ref, g_ref, w_ref, t_ref, loss_ref, dx_ref, dw_ref, do_ref, dg_ref):
        @pl.when(_first_step())
        def _():
            loss_ref[...] = jnp.zeros_like(loss_ref)
            dw_ref[...] = jnp.zeros_like(dw_ref)

        @pl.when(pl.program_id(1) == 0)
        def _():
            dg_ref[...] = jnp.zeros_like(dg_ref)
        o = jnp.dot(a_ref[...], wd_ref[...], preferred_element_type=F32)
        sg = scale * g_ref[...]
        xv = x_ref[...] + sg * o
        r = _rms(xv)
        n = xv * r
        wv = w_ref[...]
        e = n * wv - t_ref[...]
        loss_ref[...] += jnp.sum(e * e) * (0.5 / d)
        dy = e * (1.0 / d)
        dw_ref[...] += jnp.sum(dy * n, axis=0, keepdims=True)
        dx = _rms_bwd(dy * wv, n, r)
        dx_ref[...] = dx
        do_ref[...] = (sg * dx).astype(BF16)
        dg_ref[...] += jnp.sum(scale * dx * o, axis=0, keepdims=True)

    return pl.pallas_call(
        body, name=name, grid=(b, s // tm),
        in_specs=[_row(tm, f), _full((f, d)), _row(tm, d), _bvec(d), _full((1, d)), _row(tm, d)],
        out_specs=[_full((1, LANES)), _row(tm, d), _full((1, d)), _row(tm, d), _bvec(d)],
        out_shape=[SDS((1, LANES), F32), SDS((b, s, d), F32), SDS((1, d), F32), SDS((b, s, d), BF16), SDS((b, 1, d), F32)],
        compiler_params=_cparams(2))(a, wd, x, gate, w_final, tgt)


def ffn_dact(do, wd, g, u, name):
    b, s, d = do.shape
    f = wd.shape[0]
    tm, tn = min(512, s), f // 2

    def body(do_ref, wd_ref, g_ref, u_ref, dg_ref, du_ref):
        da = lax.dot_general(do_ref[...], wd_ref[...], NT_DIMS, preferred_element_type=F32)
        gv = g_ref[...].astype(F32)
        sg = _sigmoid(gv)
        dg_ref[...] = (da * u_ref[...].astype(F32) * (sg * (1.0 + gv * (1.0 - sg)))).astype(BF16)
        du_ref[...] = (da * (gv * sg)).astype(BF16)

    dos = pl.BlockSpec((None, tm, d), lambda j, bb, i: (bb, i, 0))
    ws = pl.BlockSpec((tn, d), lambda j, bb, i: (j, 0))
    es = pl.BlockSpec((None, tm, tn), lambda j, bb, i: (bb, i, j))
    return pl.pallas_call(
        body, name=name, grid=(f // tn, b, s // tm),
        in_specs=[dos, ws, es, es], out_specs=[es, es],
        out_shape=[SDS((b, s, f), BF16), SDS((b, s, f), BF16)], compiler_params=_cparams(3))(do, wd, g, u)


def mm_tn(a, bm, tma, tnb, name):
    b, s, ka = a.shape
    nb = bm.shape[2]
    tk = min(2048, s)
    nk = s // tk

    def body(a_ref, b_ref, o_ref, acc):
        first = (pl.program_id(2) == 0) & (pl.program_id(3) == 0)
        last = (pl.program_id(2) == b - 1) & (pl.program_id(3) == nk - 1)
        part = lax.dot_general(a_ref[...], b_ref[...], TN_DIMS, preferred_element_type=F32)

        @pl.when(first)
        def _():
            acc[...] = part

        @pl.when(jnp.logical_not(first))
        def _():
            acc[...] += part

        @pl.when(last)
        def _():
            o_ref[...] = acc[...].astype(BF16)

    return pl.pallas_call(
        body, name=name, grid=(ka // tma, nb // tnb, b, nk),
        in_specs=[pl.BlockSpec((None, tk, tma), lambda i, j, bb, k: (bb, k, i)),
                  pl.BlockSpec((None, tk, tnb), lambda i, j, bb, k: (bb, k, j))],
        out_specs=pl.BlockSpec((tma, tnb), lambda i, j, bb, k: (i, j)),
        out_shape=SDS((ka, nb), BF16), scratch_shapes=[pltpu.VMEM((tma, tnb), F32)],
        compiler_params=_cparams(4))(a, bm)


def _gate_bwd_specs(tm, d, b, s):
    return ([_row(tm, d), _bvec(d)], [_row(tm, d), _bvec(d)], [SDS((b, s, d), BF16), SDS((b, 1, d), F32)])


def _gate_bwd_tile(dx, scale, o_ref, g_ref, do_ref, dg_ref):
    do_ref[...] = ((scale * g_ref[...]) * dx).astype(BF16)
    dg_ref[...] += jnp.sum(scale * dx * o_ref[...].astype(F32), axis=0, keepdims=True)


def n_in_bytes(arrs):
    return sum(a.size * a.dtype.itemsize for a in arrs)


def dh_norm_bwd(dys, wts, x, dxn, w, sc, name, below=None):
    b, s, d = x.shape
    tm = min(512 if n_in_bytes(wts) <= 8 * 1024 * 1024 else 256, s)
    n_in = len(dys)
    extra_in, extra_out, extra_shape = _gate_bwd_specs(tm, d, b, s) if below else ([], [], [])

    def body(*refs):
        dy_refs, w_refs = refs[:n_in], refs[n_in:2 * n_in]
        x_ref, dxn_ref, nw_ref, sc_ref = refs[2 * n_in:2 * n_in + 4]
        rest = refs[2 * n_in + 4:]
        if below:
            o_ref, g_ref, dx_ref, dsc_ref, dsh_ref, dw_ref, do_ref, dg_ref = rest
        else:
            dx_ref, dsc_ref, dsh_ref, dw_ref = rest

        @pl.when(pl.program_id(1) == 0)
        def _():
            dsc_ref[...] = jnp.zeros_like(dsc_ref)
            dsh_ref[...] = jnp.zeros_like(dsh_ref)
            if below:
                dg_ref[...] = jnp.zeros_like(dg_ref)

        @pl.when(_first_step())
        def _():
            dw_ref[...] = jnp.zeros_like(dw_ref)

        dh = jnp.dot(dy_refs[0][...], w_refs[0][...], preferred_element_type=F32)
        for k in range(1, n_in):
            dh += jnp.dot(dy_refs[k][...], w_refs[k][...], preferred_element_type=F32)
        xv = x_ref[...]
        r = _rms(xv)
        n = xv * r
        nw = nw_ref[...]
        dsc_ref[...] += jnp.sum(dh * (n * nw), axis=0, keepdims=True)
        dsh_ref[...] += jnp.sum(dh, axis=0, keepdims=True)
        dhn = dh * (1.0 + sc_ref[...])
        dw_ref[...] += jnp.sum(dhn * n, axis=0, keepdims=True)
        dx = dxn_ref[...] + _rms_bwd(dhn * nw, n, r)
        dx_ref[...] = dx
        if below:
            _gate_bwd_tile(dx, below[2], o_ref, g_ref, do_ref, dg_ref)

    in_specs = [_row(tm, dy.shape[2]) for dy in dys] + [_full(wt.shape) for wt in wts]
    in_specs += [_row(tm, d), _row(tm, d), _full((1, d)), _bvec(d)] + extra_in
    return pl.pallas_call(
        body, name=name, grid=(b, s // tm), in_specs=in_specs,
        out_specs=[_row(tm, d), _bvec(d), _bvec(d), _full((1, d))] + extra_out,
        out_shape=[SDS((b, s, d), F32), SDS((b, 1, d), F32), SDS((b, 1, d), F32), SDS((1, d), F32)] + extra_shape,
        compiler_params=_cparams(2))(*dys, *wts, x, dxn, w, sc, *(below[:2] if below else ()))


def in_proj(h, win_t, name):
    b, s, d = h.shape
    tm = min(512, s)
    widths = (D_SSD, D_SSD + 2 * SSD_GROUPS * SSD_STATE, Q_LORA, KV_LORA, LANES)

    def body(h_ref, w_ref, *outs):
        p = lax.dot_general(h_ref[...], w_ref[...], NT_DIMS, preferred_element_type=F32)
        off = 0
        for o_ref, wd in zip(outs, widths):
            o_ref[...] = p[:, off:off + wd]
            off += wd

    return pl.pallas_call(
        body, name=name, grid=(b, s // tm),
        in_specs=[_row(tm, d), _full(win_t.shape)],
        out_specs=[_row(tm, wd) for wd in widths],
        out_shape=[SDS((b, s, wd), F32) for wd in widths], compiler_params=_cparams(2))(h, win_t)


def _halo_prev(ts, d):
    return pl.BlockSpec((None, 8, d), lambda b, i: (b, jnp.maximum(i * (ts // 8) - 1, 0), 0))


CONV_ROWS = 32


def _conv_head(head, u_ref, up_ref):
    head[0:8, :] = jnp.where(pl.program_id(1) > 0, up_ref[...], 0.0)
    head[8:8 + CONV_ROWS, :] = u_ref[0:CONV_ROWS, :]


def _conv_windows(u_ref, head, r0):
    if r0 == 0:
        return [head[5 + k:5 + k + CONV_ROWS, :] for k in range(4)]
    return [u_ref[r0 - 3 + k:r0 - 3 + k + CONV_ROWS, :] for k in range(4)]


def _fold8(t):
    acc = t[0:8, :]
    for r in range(8, CONV_ROWS, 8):
        acc += t[r:r + 8, :]
    return acc


def conv_fwd(u, cw, cb, name):
    b, s, dc = u.shape
    ts = min(512, s)
    widths = (D_SSD, SSD_GROUPS * SSD_STATE, SSD_GROUPS * SSD_STATE)

    def body(u_ref, up_ref, w_ref, b_ref, xs_ref, bm_ref, cm_ref, head):
        _conv_head(head, u_ref, up_ref)
        ws = [w_ref[k:k + 1, :] for k in range(4)]
        bias = b_ref[...]
        for r0 in range(0, ts, CONV_ROWS):
            taps = _conv_windows(u_ref, head, r0)
            v = bias + taps[0] * ws[0] + taps[1] * ws[1] + taps[2] * ws[2] + taps[3] * ws[3]
            y = v * _sigmoid(v)
            rs = slice(r0, r0 + CONV_ROWS)
            xs_ref[rs, :] = y[:, 0:D_SSD]
            bm_ref[rs, :] = y[:, D_SSD:D_SSD + 256]
            cm_ref[rs, :] = y[:, D_SSD + 256:D_SSD + 512]

    return pl.pallas_call(
        body, name=name, grid=(b, s // ts),
        in_specs=[_row(ts, dc), _halo_prev(ts, dc), _full((4, dc)), _full((1, dc))],
        out_specs=[_row(ts, wd) for wd in widths],
        out_shape=[SDS((b, s, wd), F32) for wd in widths],
        scratch_shapes=[pltpu.VMEM((8 + CONV_ROWS, dc), F32)], compiler_params=_cparams(2))(u, u, cw, cb)


def conv_bwd_a(dxs, dbm, dcm, u, cw, cb, name):
    b, s, dc = u.shape
    ts = min(512, s)

    def body(dxs_ref, dbm_ref, dcm_ref, u_ref, up_ref, w_ref, b_ref, dv_ref, dwb_ref, head):
        @pl.when(_first_step())
        def _():
            dwb_ref[...] = jnp.zeros_like(dwb_ref)
        _conv_head(head, u_ref, up_ref)
        ws = [w_ref[k:k + 1, :] for k in range(4)]
        bias = b_ref[...]
        for r0 in range(0, ts, CONV_ROWS):
            taps = _conv_windows(u_ref, head, r0)
            v = bias + taps[0] * ws[0] + taps[1] * ws[1] + taps[2] * ws[2] + taps[3] * ws[3]
            sg = _sigmoid(v)
            rs = slice(r0, r0 + CONV_ROWS)
            dy = jnp.concatenate([dxs_ref[rs, :], dbm_ref[rs, :], dcm_ref[rs, :]], axis=1)
            dv = dy * (sg * (1.0 + v * (1.0 - sg)))
            dv_ref[rs, :] = dv
            for k in range(4):
                dwb_ref[8 * k:8 * k + 8, :] += _fold8(dv * taps[k])
            dwb_ref[32:40, :] += _fold8(dv)

    return pl.pallas_call(
        body, name=name, grid=(b, s // ts),
        in_specs=[_row(ts, D_SSD), _row(ts, 256), _row(ts, 256), _row(ts, dc), _halo_prev(ts, dc),
                  _full((4, dc)), _full((1, dc))],
        out_specs=[_row(ts, dc), _full((40, dc))],
        out_shape=[SDS((b, s, dc), F32), SDS((40, dc), F32)],
        scratch_shapes=[pltpu.VMEM((8 + CONV_ROWS, dc), F32)], compiler_params=_cparams(2))(dxs, dbm, dcm, u, u, cw, cb)


def conv_grads_fold(x, name):
    c = x.shape[1]

    def body(x_ref, o_ref):
        o_ref[...] = jnp.zeros_like(o_ref)
        for k in range(5):
            o_ref[k:k + 1, :] = jnp.sum(x_ref[8 * k:8 * k + 8, :], axis=0, keepdims=True)

    return pl.pallas_call(body, name=name, out_shape=SDS((8, c), F32))(x)


def conv_bwd_b(dv, cw, name):
    b, s, dc = dv.shape
    ts = min(512, s)
    nt = s // ts

    def body(dv_ref, dn_ref, w_ref, du_ref, tail):
        tail[0:CONV_ROWS, :] = dv_ref[ts - CONV_ROWS:ts, :]
        tail[CONV_ROWS:CONV_ROWS + 8, :] = jnp.where(pl.program_id(1) < nt - 1, dn_ref[...], 0.0)
        ws = [w_ref[k:k + 1, :] for k in range(4)]
        for r0 in range(0, ts, CONV_ROWS):
            if r0 == ts - CONV_ROWS:
                win = [tail[3 - k:3 - k + CONV_ROWS, :] for k in range(4)]
            else:
                win = [dv_ref[r0 + 3 - k:r0 + 3 - k + CONV_ROWS, :] for k in range(4)]
            acc = win[0] * ws[0] + win[1] * ws[1] + win[2] * ws[2] + win[3] * ws[3]
            du_ref[r0:r0 + CONV_ROWS, :] = acc.astype(BF16)

    nxt = pl.BlockSpec((None, 8, dc), lambda bb, i: (bb, jnp.minimum((i + 1) * (ts // 8), s // 8 - 1), 0))
    return pl.pallas_call(
        body, name=name, grid=(b, nt),
        in_specs=[_row(ts, dc), nxt, _full((4, dc))],
        out_specs=_row(ts, dc), out_shape=SDS((b, s, dc), BF16),
        scratch_shapes=[pltpu.VMEM((CONV_ROWS + 8, dc), F32)], compiler_params=_cparams(2))(dv, dv, cw)


def _ssd_common(misc_ref, dtb_ref, alog_ref, e_ref):
    ln = CHUNK
    lane = lax.broadcasted_iota(I32, (ln, LANES), 1)
    lane1 = lax.broadcasted_iota(I32, (1, LANES), 1)
    pre = misc_ref[...] + dtb_ref[...]
    dt_s = jnp.where(lane < SSD_HEADS, _softplus(pre), 0.0)
    a_neg = jnp.where(lane1 < SSD_HEADS, -jnp.exp(alog_ref[...]), 0.0)
    ri = lax.broadcasted_iota(I32, (ln, ln), 0)
    ci = lax.broadcasted_iota(I32, (ln, ln), 1)
    tril = ci <= ri
    acum = jnp.dot(tril.astype(F32), dt_s * a_neg, preferred_element_type=F32, precision=HI)
    both_e = _dot_01(jnp.concatenate([dt_s, acum], axis=0), e_ref[...], 3)
    dt_e, acum_e = both_e[0:ln], both_e[ln:2 * ln]
    return dict(pre=pre, dt_s=dt_s, a_neg=a_neg, tril=tril, ri=ri, ci=ci, acum=acum, acum_t=acum.T,
                dt_e=dt_e, eac_e=jnp.exp(acum_e), del_e=jnp.exp(acum_e[ln - 1:ln, :] - acum_e))


def _dot_01(x, m01, terms):
    acc, rest = None, x
    for k in range(terms):
        part = rest.astype(BF16)
        if k + 1 < terms:
            rest = rest - part.astype(F32)
        d = jnp.dot(part, m01, preferred_element_type=F32)
        acc = d if acc is None else acc + d
    return acc


def _decay(cm, h):
    seg = cm["acum"][:, h:h + 1] - cm["acum_t"][h:h + 1, :]
    return jnp.exp(jnp.where(cm["tril"], seg, -jnp.inf))


def ssd_fwd(xs, bm, cm_, misc, z, dtb, alog, dskip_e, norm_w, e_mat, name):
    b, s, _ = xs.shape
    ln, nc = CHUNK, s // CHUNK
    gw = D_SSD // SSD_GROUPS
    hpg = SSD_HEADS // SSD_GROUPS

    def body(xs_ref, b_ref, c_ref, misc_ref, z_ref, dtb_ref, alog_ref, dsk_ref, nw_ref, e_ref,
             ys_ref, y_ref, p_ref, st, yd):
        @pl.when(pl.program_id(1) == 0)
        def _():
            st[...] = jnp.zeros_like(st)
        cm = _ssd_common(misc_ref, dtb_ref, alog_ref, e_ref)
        xsv = xs_ref[...]
        xdt = xsv * cm["dt_e"]
        xdt_b = xdt.astype(BF16)
        xd_b = (xdt * cm["del_e"]).astype(BF16)
        gam_e = cm["eac_e"][ln - 1:ln, :]
        p_ref[...] = st[...]
        yoff = []
        for g in range(SSD_GROUPS):
            gs = slice(gw * g, gw * (g + 1))
            bg = b_ref[:, SSD_STATE * g:SSD_STATE * (g + 1)].astype(BF16)
            cg = c_ref[:, SSD_STATE * g:SSD_STATE * (g + 1)].astype(BF16)
            cb = lax.dot_general(cg, bg, NT_DIMS, preferred_element_type=F32)
            st_g = st[:, gs]
            yoff.append(jnp.dot(cg, st_g.astype(BF16), preferred_element_type=F32) * cm["eac_e"][:, gs])
            for j in range(hpg):
                h = hpg * g + j
                hs = slice(SSD_HEAD_DIM * h, SSD_HEAD_DIM * (h + 1))
                m = (cb * _decay(cm, h)).astype(BF16)
                yd[:, hs] = jnp.dot(m, xdt_b[:, hs], preferred_element_type=F32)
            new = lax.dot_general(bg, xd_b[:, gs], TN_DIMS, preferred_element_type=F32)
            st[:, gs] = st_g * gam_e[:, gs] + new
        y = yd[...] + jnp.concatenate(yoff, axis=1) + dsk_ref[...] * xsv
        y_ref[...] = y
        zz = z_ref[...]
        yg = y * (zz * _sigmoid(zz))
        outs = []
        for g in range(SSD_GROUPS):
            ygg = yg[:, gw * g:gw * (g + 1)]
            outs.append(ygg * _rms(ygg) * nw_ref[:, gw * g:gw * (g + 1)])
        ys_ref[...] = jnp.concatenate(outs, axis=1).astype(BF16)

    row = lambda d: pl.BlockSpec((None, ln, d), lambda bb, c: (bb, c, 0))
    return pl.pallas_call(
        body, name=name, grid=(b, nc),
        in_specs=[row(D_SSD), row(256), row(256), row(LANES), row(D_SSD), _full((1, LANES)), _full((1, LANES)),
                  _full((1, D_SSD)), _full((1, D_SSD)), _full((LANES, D_SSD))],
        out_specs=[row(D_SSD), row(D_SSD), pl.BlockSpec((None, None, SSD_STATE, D_SSD), lambda bb, c: (bb, c, 0, 0))],
        out_shape=[SDS((b, s, D_SSD), BF16), SDS((b, s, D_SSD), F32), SDS((b, nc, SSD_STATE, D_SSD), F32)],
        scratch_shapes=[pltpu.VMEM((SSD_STATE, D_SSD), F32), pltpu.VMEM((ln, D_SSD), F32)],
        compiler_params=_cparams(2))(xs, bm, cm_, misc, z, dtb, alog, dskip_e, norm_w, e_mat)


def ssd_bwd(dys, y, z, xs, bm, cm_, misc, prev, dtb, alog, dskip_e, norm_w, e_mat, et_mat, name):
    b, s, _ = xs.shape
    ln, nc = CHUNK, s // CHUNK
    gw = D_SSD // SSD_GROUPS
    hpg = SSD_HEADS // SSD_GROUPS

    def body(dys_ref, y_ref, z_ref, xs_ref, b_ref, c_ref, misc_ref, p_ref, dtb_ref, alog_ref, dsk_ref, nw_ref,
             e_ref, et_ref, dxs_ref, db_ref, dc_ref, dz_ref, ddt_ref, dnw_ref, ddsk_ref, ddtb_ref, dalog_ref,
             dst, dxd, dac_t):
        @pl.when(_first_step())
        def _():
            for r_ in (dnw_ref, ddsk_ref, ddtb_ref, dalog_ref):
                r_[...] = jnp.zeros_like(r_)

        @pl.when(pl.program_id(1) == 0)
        def _():
            dst[...] = jnp.zeros_like(dst)

        cm = _ssd_common(misc_ref, dtb_ref, alog_ref, e_ref)
        et = et_ref[...]
        squeeze = lambda t: _dot_01(t, et, 2)
        lane = lax.broadcasted_iota(I32, (ln, LANES), 1)
        sub = lax.broadcasted_iota(I32, (LANES, ln), 0)
        xsv = xs_ref[...]
        xdt = xsv * cm["dt_e"]
        xdt_b = xdt.astype(BF16)
        xd_b = (xdt * cm["del_e"]).astype(BF16)
        eac_e = cm["eac_e"]
        gam_e = eac_e[ln - 1:ln, :]

        yv, zz, dyo = y_ref[...], z_ref[...], dys_ref[...]
        sz = _sigmoid(zz)
        silu_z = zz * sz
        yg = yv * silu_z
        dyg, dnw = [], []
        for g in range(SSD_GROUPS):
            gs = slice(gw * g, gw * (g + 1))
            ygg = yg[:, gs]
            r = _rms(ygg)
            n = ygg * r
            dnw.append(jnp.sum(dyo[:, gs] * n, axis=0, keepdims=True))
            dyg.append(_rms_bwd(dyo[:, gs] * nw_ref[:, gs], n, r))
        dyg = jnp.concatenate(dyg, axis=1)
        dnw_ref[...] += jnp.concatenate(dnw, axis=1)
        dz_ref[...] = (dyg * yv * (sz * (1.0 + zz * (1.0 - sz)))).astype(BF16)
        dy = dyg * silu_z
        ddsk_ref[...] += jnp.sum(dy * xsv, axis=0, keepdims=True)
        dy_b = dy.astype(BF16)

        dacum = jnp.zeros((ln, LANES), F32)
        dac_t[...] = jnp.zeros_like(dac_t)
        w1, dgam = [], []
        for g in range(SSD_GROUPS):
            gs = slice(gw * g, gw * (g + 1))
            ss = slice(SSD_STATE * g, SSD_STATE * (g + 1))
            bg = b_ref[:, ss].astype(BF16)
            cg = c_ref[:, ss].astype(BF16)
            cb = lax.dot_general(cg, bg, NT_DIMS, preferred_element_type=F32)
            pt = p_ref[:, gs]
            pt_b = pt.astype(BF16)
            dst_g = dst[:, gs]
            dst_b = dst_g.astype(BF16)
            edy = (dy[:, gs] * eac_e[:, gs]).astype(BF16)
            dcg = lax.dot_general(edy, pt_b, NT_DIMS, preferred_element_type=F32)
            dpt = lax.dot_general(cg, edy, TN_DIMS, preferred_element_type=F32)
            yoff = jnp.dot(cg, pt_b, preferred_element_type=F32) * eac_e[:, gs]
            dxd_g = jnp.dot(bg, dst_b, preferred_element_type=F32)
            dbg = lax.dot_general(xd_b[:, gs], dst_b, NT_DIMS, preferred_element_type=F32)
            ddel = dxd_g * xdt[:, gs] * cm["del_e"][:, gs]
            w1.append(dy[:, gs] * yoff - ddel)
            dgam.append(jnp.sum(ddel, axis=0, keepdims=True) + jnp.sum(dst_g * pt, axis=0, keepdims=True) * gam_e[:, gs])
            dxd[:, gs] = dxd_g * cm["del_e"][:, gs]
            dst[:, gs] = dst_g * gam_e[:, gs] + dpt
            dcb = jnp.zeros((ln, ln), F32)
            for j in range(hpg):
                h = hpg * g + j
                hs = slice(SSD_HEAD_DIM * h, SSD_HEAD_DIM * (h + 1))
                lam = _decay(cm, h)
                m = cb * lam
                dm = lax.dot_general(dy_b[:, hs], xdt_b[:, hs], NT_DIMS, preferred_element_type=F32)
                dxd[:, hs] += lax.dot_general(m.astype(BF16), dy_b[:, hs], TN_DIMS, preferred_element_type=F32)
                dcb += dm * lam
                wl = dm * m
                dacum += jnp.where(lane == h, jnp.sum(wl, axis=1, keepdims=True), 0.0)
                dac_t[...] -= jnp.where(sub == h, jnp.sum(wl, axis=0, keepdims=True), 0.0)
            dcb_b = dcb.astype(BF16)
            dc_ref[:, ss] = dcg + jnp.dot(dcb_b, bg, preferred_element_type=F32)
            db_ref[:, ss] = dbg + lax.dot_general(dcb_b, cg, TN_DIMS, preferred_element_type=F32)

        dxdt = dxd[...]
        dxs_ref[...] = dy * dsk_ref[...] + dxdt * cm["dt_e"]
        dacum += squeeze(jnp.concatenate(w1, axis=1)) + dac_t[...].T
        dlast = squeeze(jnp.broadcast_to(jnp.concatenate(dgam, axis=1), (8, D_SSD)))[0:1, :]
        dacum += jnp.where(lax.broadcasted_iota(I32, (ln, LANES), 0) == ln - 1, dlast, 0.0)
        triu = (cm["ci"] >= cm["ri"]).astype(F32)
        da = jnp.dot(triu, dacum, preferred_element_type=F32, precision=HI)
        ddt = da * cm["a_neg"] + squeeze(dxdt * xsv)
        dalog_ref[...] += jnp.sum(da * cm["dt_s"], axis=0, keepdims=True) * cm["a_neg"]
        ddt_raw = jnp.where(lane < SSD_HEADS, ddt * _sigmoid(cm["pre"]), 0.0)
        ddt_ref[...] = ddt_raw
        ddtb_ref[...] += jnp.sum(ddt_raw, axis=0, keepdims=True)

    row = lambda d: pl.BlockSpec((None, ln, d), lambda bb, c: (bb, nc - 1 - c, 0))
    return pl.pallas_call(
        body, name=name, grid=(b, nc),
        in_specs=[row(D_SSD), row(D_SSD), row(D_SSD), row(D_SSD), row(256), row(256), row(LANES),
                  pl.BlockSpec((None, None, SSD_STATE, D_SSD), lambda bb, c: (bb, nc - 1 - c, 0, 0)),
                  _full((1, LANES)), _full((1, LANES)), _full((1, D_SSD)), _full((1, D_SSD)),
                  _full((LANES, D_SSD)), _full((D_SSD, LANES))],
        out_specs=[row(D_SSD), row(256), row(256), row(D_SSD), row(LANES),
                   _full((1, D_SSD)), _full((1, D_SSD)), _full((1, LANES)), _full((1, LANES))],
        out_shape=[SDS((b, s, D_SSD), F32), SDS((b, s, 256), F32), SDS((b, s, 256), F32), SDS((b, s, D_SSD), BF16),
                   SDS((b, s, LANES), F32), SDS((1, D_SSD), F32), SDS((1, D_SSD), F32), SDS((1, LANES), F32),
                   SDS((1, LANES), F32)],
        scratch_shapes=[pltpu.VMEM((SSD_STATE, D_SSD), F32), pltpu.VMEM((ln, D_SSD), F32), pltpu.VMEM((LANES, ln), F32)],
        compiler_params=_cparams(2))(dys, y, z, xs, bm, cm_, misc, prev, dtb, alog, dskip_e, norm_w, e_mat, et_mat)


def _rope(xv, cc, sp, sm):
    n = xv.shape[1]
    return xv * cc + pltpu.roll(xv, 16, 1) * sp + pltpu.roll(xv, n - 16, 1) * sm


def _rope_bwd(dy, cc, sp, sm):
    n = dy.shape[1]
    return dy * cc + pltpu.roll(dy * sp, n - 16, 1) + pltpu.roll(dy * sm, 16, 1)


def _tile8(t):
    return jnp.concatenate([t] * MLA_HEADS, axis=1)


def qkv_fwd(cq, ckv, misc, cc, sp, sm, qnw, kvnw, wuq_t, wukv_t, place, name):
    b, s, _ = cq.shape
    tm = min(512, s)
    hd = MLA_HEADS * HEAD_PAD

    def body(cq_ref, ckv_ref, misc_ref, cc_ref, sp_ref, sm_ref, qnw_ref, kvnw_ref, wq_ref, wkv_ref, pl_ref,
             q_ref, k_ref, v_ref, qn_ref, kvn_ref):
        cqv, ckvv = cq_ref[...], ckv_ref[...]
        qn = (cqv * _rms(cqv) * qnw_ref[...]).astype(BF16)
        kvn = (ckvv * _rms(ckvv) * kvnw_ref[...]).astype(BF16)
        qn_ref[...] = qn
        kvn_ref[...] = kvn
        cc1, sp1, sm1 = cc_ref[...], sp_ref[...], sm_ref[...]
        q = lax.dot_general(qn, wq_ref[...], NT_DIMS, preferred_element_type=F32)
        q_ref[...] = _rope(q, _tile8(cc1), _tile8(sp1), _tile8(sm1)).astype(BF16)
        kv = lax.dot_general(kvn, wkv_ref[...], NT_DIMS, preferred_element_type=F32)
        kr = jnp.dot(misc_ref[...], pl_ref[...], preferred_element_type=F32, precision=HI)
        kr = _rope(kr, cc1, sp1, sm1)
        k_ref[...] = (kv[:, 0:hd] + _tile8(kr)).astype(BF16)
        v_ref[...] = kv[:, hd:2 * hd].astype(BF16)

    return pl.pallas_call(
        body, name=name, grid=(b, s // tm),
        in_specs=[_row(tm, Q_LORA), _row(tm, KV_LORA), _row(tm, LANES), _row(tm, LANES), _row(tm, LANES), _row(tm, LANES),
                  _full((1, Q_LORA)), _full((1, KV_LORA)), _full(wuq_t.shape), _full(wukv_t.shape), _full((LANES, LANES))],
        out_specs=[_row(tm, hd), _row(tm, hd), _row(tm, hd), _row(tm, Q_LORA), _row(tm, KV_LORA)],
        out_shape=[SDS((b, s, hd), BF16)] * 3 + [SDS((b, s, Q_LORA), BF16), SDS((b, s, KV_LORA), BF16)],
        compiler_params=_cparams(2))(cq, ckv, misc, cc, sp, sm, qnw, kvnw, wuq_t, wukv_t, place)


def qkv_bwd(dq, dk, dv, ddt, cq, ckv, cc, sp, sm, qnw, kvnw, wuq_t, wukv_t, place_t, name):
    b, s, _ = cq.shape
    tm = min(512, s)
    hd = MLA_HEADS * HEAD_PAD

    def body(dq_ref, dk_ref, dv_ref, ddt_ref, cq_ref, ckv_ref, cc_ref, sp_ref, sm_ref, qnw_ref, kvnw_ref,
             wq_ref, wkv_ref, plt_ref, dcq_ref, dckv_ref, dmisc_ref, dqp_ref, dkv_ref, dqnw_ref, dkvnw_ref):
        @pl.when(_first_step())
        def _():
            dqnw_ref[...] = jnp.zeros_like(dqnw_ref)
            dkvnw_ref[...] = jnp.zeros_like(dkvnw_ref)
        cc1, sp1, sm1 = cc_ref[...], sp_ref[...], sm_ref[...]
        dqp = _rope_bwd(dq_ref[...], _tile8(cc1), _tile8(sp1), _tile8(sm1)).astype(BF16)
        dqp_ref[...] = dqp
        dkf = dk_ref[...]
        dkv_b = jnp.concatenate([dkf, dv_ref[...]], axis=1).astype(BF16)
        dkv_ref[...] = dkv_b
        dkr = dkf[:, 0:HEAD_PAD]
        for h in range(1, MLA_HEADS):
            dkr += dkf[:, HEAD_PAD * h:HEAD_PAD * (h + 1)]
        dkr = _rope_bwd(dkr, cc1, sp1, sm1)
        dmisc_ref[...] = (jnp.dot(dkr, plt_ref[...], preferred_element_type=F32, precision=HI) + ddt_ref[...]).astype(BF16)

        def norm_bwd(dn_w, xv, w_ref, dw_ref, dx_ref):
            r = _rms(xv)
            n = xv * r
            dw_ref[...] += jnp.sum(dn_w * n, axis=0, keepdims=True)
            dx_ref[...] = _rms_bwd(dn_w * w_ref[...], n, r).astype(BF16)

        norm_bwd(jnp.dot(dqp, wq_ref[...], preferred_element_type=F32), cq_ref[...], qnw_ref, dqnw_ref, dcq_ref)
        norm_bwd(jnp.dot(dkv_b, wkv_ref[...], preferred_element_type=F32), ckv_ref[...], kvnw_ref, dkvnw_ref, dckv_ref)

    return pl.pallas_call(
        body, name=name, grid=(b, s // tm),
        in_specs=[_row(tm, hd), _row(tm, hd), _row(tm, hd), _row(tm, LANES), _row(tm, Q_LORA), _row(tm, KV_LORA),
                  _row(tm, LANES), _row(tm, LANES), _row(tm, LANES), _full((1, Q_LORA)), _full((1, KV_LORA)),
                  _full(wuq_t.shape), _full(wukv_t.shape), _full((LANES, LANES))],
        out_specs=[_row(tm, Q_LORA), _row(tm, KV_LORA), _row(tm, LANES), _row(tm, hd), _row(tm, 2 * hd),
                   _full((1, Q_LORA)), _full((1, KV_LORA))],
        out_shape=[SDS((b, s, Q_LORA), BF16), SDS((b, s, KV_LORA), BF16), SDS((b, s, LANES), BF16),
                   SDS((b, s, hd), BF16), SDS((b, s, 2 * hd), BF16), SDS((1, Q_LORA), F32), SDS((1, KV_LORA), F32)],
        compiler_params=_cparams(2))(dq, dk, dv, ddt, cq, ckv, cc, sp, sm, qnw, kvnw, wuq_t, wukv_t, place_t)


ATT_SCALE = 1.0 / math.sqrt(QK_DIM)
LOG2E = math.log2(math.e)
ATT_SCALE_LOG2E = ATT_SCALE * LOG2E


ATT_HEADS_PER_STEP = 4
ATT_HEADS_PER_STEP_BWD = 2


def _att_tile(s):
    return min(512, s)


def flash_fwd(q, k, v, name):
    b, s, hd = q.shape
    t = _att_tile(s)
    nb = s // t
    th = t // 2
    vt = v.reshape(b, nb, t, MLA_HEADS, HEAD_PAD).transpose(0, 3, 1, 4, 2)

    hps = ATT_HEADS_PER_STEP
    hw = hps * HEAD_PAD

    def body(q_ref, k_ref, vt_ref, o_ref, lse_ref, m_s, l_s, acc):
        i = pl.program_id(2)
        m_s[...] = jnp.full_like(m_s, -jnp.inf)
        l_s[...] = jnp.zeros_like(l_s)
        acc[...] = jnp.zeros_like(acc)

        def update(j, diagonal):
            ks = pl.ds(pl.multiple_of(j * t, t), t)
            chains = [(hh, half) for hh in range(hps) for half in range(2)]
            lanes = lambda hh: slice(HEAD_PAD * hh, HEAD_PAD * (hh + 1))
            cols = lambda half: slice(th * half, th * (half + 1))
            sts = {}
            for hh, half in chains:
                st = lax.dot_general(k_ref[ks, lanes(hh)], q_ref[cols(half), lanes(hh)], NT_DIMS,
                                     preferred_element_type=F32)
                if diagonal:
                    row = lax.broadcasted_iota(I32, (t, th), 0)
                    col = lax.broadcasted_iota(I32, (t, th), 1) + th * half
                    st = jnp.where(row <= col, st, -jnp.inf)
                sts[hh, half] = st
            pts, alphas = {}, {}
            for hh, half in chains:
                st, cs = sts[hh, half], cols(half)
                m_prev = m_s[hh, :, cs]
                m_new = jnp.maximum(m_prev, jnp.max(st, axis=0, keepdims=True))
                alpha = jnp.exp2((m_prev - m_new) * ATT_SCALE_LOG2E)
                pt = jnp.exp2((st - m_new) * ATT_SCALE_LOG2E)
                l_s[hh, :, cs] = alpha * l_s[hh, :, cs] + jnp.sum(pt, axis=0, keepdims=True)
                m_s[hh, :, cs] = m_new
                pts[hh, half], alphas[hh, half] = pt.astype(BF16), alpha
            for hh, half in chains:
                cs = cols(half)
                acc[hh, :, cs] = alphas[hh, half] * acc[hh, :, cs] + jnp.dot(vt_ref[hh, j], pts[hh, half],
                                                                             preferred_element_type=F32)

        def step(j, carry):
            update(j, False)
            return carry

        lax.fori_loop(0, i, step, 0)
        update(i, True)
        for hh in range(hps):
            o_ref[:, HEAD_PAD * hh:HEAD_PAD * (hh + 1)] = (acc[hh] / l_s[hh]).T
            lse_ref[hh] = m_s[hh] * ATT_SCALE + jnp.log(l_s[hh])

    qs = pl.BlockSpec((None, t, hw), lambda bb, h, i: (bb, i, h))
    ks = pl.BlockSpec((None, s, hw), lambda bb, h, i: (bb, 0, h))
    vs = pl.BlockSpec((None, hps, nb, HEAD_PAD, t), lambda bb, h, i: (bb, h, 0, 0, 0))
    ls = pl.BlockSpec((None, hps, None, 1, t), lambda bb, h, i: (bb, h, i, 0, 0))
    return pl.pallas_call(
        body, name=name, grid=(b, MLA_HEADS // hps, nb),
        in_specs=[qs, ks, vs], out_specs=[qs, ls],
        out_shape=[SDS((b, s, hd), F32), SDS((b, MLA_HEADS, nb, 1, t), F32)],
        scratch_shapes=[pltpu.VMEM((hps, 1, t), F32), pltpu.VMEM((hps, 1, t), F32), pltpu.VMEM((hps, HEAD_PAD, t), F32)],
        compiler_params=_cparams(3))(q, k, vt)


def flash_bwd(q, k, v, do, lse, dlt, name):
    b, s, hd = q.shape
    t = _att_tile(s)
    nb = s // t
    th = t // 2
    lse_r = lse
    dlt_r = dlt.reshape(b, MLA_HEADS, nb, 1, t)

    hps = ATT_HEADS_PER_STEP_BWD
    hw = hps * HEAD_PAD

    def body(q_ref, k_ref, v_ref, do_ref, lse_ref, dlt_ref, dq_ref, dk_ref, dv_ref):
        dq_ref[...] = jnp.zeros_like(dq_ref)
        dk_ref[...] = jnp.zeros_like(dk_ref)
        dv_ref[...] = jnp.zeros_like(dv_ref)

        def tile(j, i, diagonal):
            qs = pl.ds(pl.multiple_of(i * t, t), t)
            chains = [(hh, half) for hh in range(hps) for half in range(2)]
            lanes = lambda hh: slice(HEAD_PAD * hh, HEAD_PAD * (hh + 1))
            keys = lambda half: pl.ds(pl.multiple_of(j * t + th * half, th), th)
            sts, dpts = {}, {}
            for hh, half in chains:
                ls_, ks = lanes(hh), keys(half)
                st = lax.dot_general(k_ref[ks, ls_], q_ref[qs, ls_], NT_DIMS, preferred_element_type=F32)
                if diagonal:
                    row = lax.broadcasted_iota(I32, (th, t), 0) + th * half
                    col = lax.broadcasted_iota(I32, (th, t), 1)
                    st = jnp.where(row <= col, st, -jnp.inf)
                sts[hh, half] = st
                dpts[hh, half] = lax.dot_general(v_ref[ks, ls_], do_ref[qs, ls_], NT_DIMS, preferred_element_type=F32)
            pts, dsts = {}, {}
            for hh, half in chains:
                pt = jnp.exp2(sts[hh, half] * ATT_SCALE_LOG2E - lse_ref[hh, i] * LOG2E)
                pts[hh, half] = pt.astype(BF16)
                dsts[hh, half] = (pt * (dpts[hh, half] - dlt_ref[hh, i])).astype(BF16)
            for hh in range(hps):
                ls_ = lanes(hh)
                dq_acc = None
                for half in range(2):
                    ks = keys(half)
                    dv_ref[ks, ls_] += jnp.dot(pts[hh, half], do_ref[qs, ls_], preferred_element_type=F32)
                    dk_ref[ks, ls_] += jnp.dot(dsts[hh, half], q_ref[qs, ls_], preferred_element_type=F32)
                    part = lax.dot_general(dsts[hh, half], k_ref[ks, ls_], TN_DIMS, preferred_element_type=F32)
                    dq_acc = part if dq_acc is None else dq_acc + part
                dq_ref[qs, ls_] += dq_acc

        def key_tile(j, carry):
            tile(j, j, True)

            def query_tile(i, c2):
                tile(j, i, False)
                return c2

            lax.fori_loop(j + 1, nb, query_tile, 0)
            return carry

        lax.fori_loop(0, nb, key_tile, 0)
        dq_ref[...] *= ATT_SCALE
        dk_ref[...] *= ATT_SCALE

    hs = pl.BlockSpec((None, s, hw), lambda bb, h: (bb, 0, h))
    ls = pl.BlockSpec((None, hps, nb, 1, t), lambda bb, h: (bb, h, 0, 0, 0))
    return pl.pallas_call(
        body, name=name, grid=(b, MLA_HEADS // hps),
        in_specs=[hs, hs, hs, hs, ls, ls], out_specs=[hs, hs, hs],
        out_shape=[SDS((b, s, hd), F32)] * 3, compiler_params=_cparams(2))(q, k, v, do, lse_r, dlt_r)


def out_proj(ys, attn, mnw, wo, x, gate, above, name):
    b, s, d = x.shape
    tm = min(512, s)

    def body(ys_ref, at_ref, mnw_ref, wo_ref, x_ref, g_ref, nw_ref, sc_ref, sh_ref, xn_ref, o_ref, ym_ref, h_ref):
        av = at_ref[...]
        ym = (av * _rms(av) * mnw_ref[...]).astype(BF16)
        ym_ref[...] = ym
        o = jnp.dot(ys_ref[...], wo_ref[0:D_SSD, :], preferred_element_type=F32)
        o += jnp.dot(ym, wo_ref[D_SSD:2 * D_SSD, :], preferred_element_type=F32)
        xn = x_ref[...] + g_ref[...] * o
        xn_ref[...] = xn
        o_ref[...] = o.astype(BF16)
        h_ref[...] = _norm_mod_tile(xn, nw_ref, sc_ref, sh_ref)

    return pl.pallas_call(
        body, name=name, grid=(b, s // tm),
        in_specs=[_row(tm, D_SSD), _row(tm, D_SSD), _full((1, D_SSD)), _full(wo.shape), _row(tm, d), _bvec(d),
                  _full((1, d)), _bvec(d), _bvec(d)],
        out_specs=[_row(tm, d), _row(tm, d), _row(tm, D_SSD), _row(tm, d)],
        out_shape=[SDS((b, s, d), F32), SDS((b, s, d), BF16), SDS((b, s, D_SSD), BF16), SDS((b, s, d), BF16)],
        compiler_params=_cparams(2))(ys, attn, mnw, wo, x, gate, *above)


def out_proj_bwd(dout, attn, mnw, wo, name):
    b, s, d = dout.shape
    tm = min(512, s)

    def body(do_ref, at_ref, mnw_ref, wo_ref, dys_ref, dat_ref, dlt_ref, dw_ref):
        @pl.when(_first_step())
        def _():
            dw_ref[...] = jnp.zeros_like(dw_ref)
        dov = do_ref[...]
        dys_ref[...] = lax.dot_general(dov, wo_ref[0:D_SSD, :], NT_DIMS, preferred_element_type=F32)
        dym = lax.dot_general(dov, wo_ref[D_SSD:2 * D_SSD, :], NT_DIMS, preferred_element_type=F32)
        av = at_ref[...]
        r = _rms(av)
        n = av * r
        dw_ref[...] += jnp.sum(dym * n, axis=0, keepdims=True)
        dat = _rms_bwd(dym * mnw_ref[...], n, r)
        dat_ref[...] = dat.astype(BF16)
        prod = dat * av
        for h in range(MLA_HEADS):
            dlt_ref[h] = jnp.sum(prod[:, HEAD_PAD * h:HEAD_PAD * (h + 1)], axis=1, keepdims=True)

    return pl.pallas_call(
        body, name=name, grid=(b, s // tm),
        in_specs=[_row(tm, d), _row(tm, D_SSD), _full((1, D_SSD)), _full(wo.shape)],
        out_specs=[_row(tm, D_SSD), _row(tm, D_SSD),
                   pl.BlockSpec((None, MLA_HEADS, tm, 1), lambda bb, i: (bb, 0, i, 0)), _full((1, D_SSD))],
        out_shape=[SDS((b, s, D_SSD), F32), SDS((b, s, D_SSD), BF16), SDS((b, MLA_HEADS, s, 1), F32),
                   SDS((1, D_SSD), F32)],
        compiler_params=_cparams(2))(dout, attn, mnw, wo)


def adaln_fwd(c_all, w_ada, b_ada, name):
    nb, d = c_all.shape
    n = w_ada.shape[1]

    def body(c_ref, w_ref, b_ref, m_ref, ca_ref):
        cv = c_ref[...]
        ca = (cv * _sigmoid(cv)).astype(BF16)
        ca_ref[...] = ca
        m_ref[...] = jnp.dot(ca, w_ref[...].astype(BF16), preferred_element_type=F32) + b_ref[...]

    return pl.pallas_call(
        body, name=name, out_shape=[SDS((nb, n), F32), SDS((nb, d), BF16)],
        compiler_params=pltpu.CompilerParams(vmem_limit_bytes=VMEM_LIMIT))(c_all, w_ada, b_ada)


def adaln_bwd(c_act, dmod_cols, name):
    d, n = c_act.shape[1], dmod_cols.shape[1]

    def body(c_ref, dm_ref, gw_ref):
        gw_ref[...] = lax.dot_general(c_ref[...], dm_ref[...].astype(BF16), TN_DIMS, preferred_element_type=F32)

    return pl.pallas_call(
        body, name=name, out_shape=SDS((d, n), F32),
        compiler_params=pltpu.CompilerParams(vmem_limit_bytes=VMEM_LIMIT))(c_act, dmod_cols)


def sum_rows(x, name):
    def body(x_ref, o_ref):
        o_ref[...] = jnp.sum(x_ref[...], axis=0, keepdims=True)
    return pl.pallas_call(body, name=name, out_shape=SDS((1, x.shape[1]), F32))(x)


def squeeze_heads(x, et_mat, name):
    def body(x_ref, et_ref, o_ref):
        xv = jnp.broadcast_to(x_ref[...], (8, x.shape[1]))
        o_ref[...] = _dot_01(xv, et_ref[...], 3)[0:1, :]
    return pl.pallas_call(body, name=name, out_shape=SDS((1, LANES), F32))(x, et_mat)


def sum_blocks(x, name):
    n, r, c = x.shape
    tr = next(cand for cand in (256, 128, 64, 32, 16, 8) if r % cand == 0)

    def body(x_ref, o_ref):
        acc = x_ref[0].astype(F32)
        for k in range(1, n):
            acc += x_ref[k].astype(F32)
        o_ref[...] = acc

    return pl.pallas_call(
        body, name=name, grid=(r // tr,), in_specs=[pl.BlockSpec((n, tr, c), lambda i: (0, i, 0))],
        out_specs=pl.BlockSpec((tr, c), lambda i: (i, 0)), out_shape=SDS((r, c), F32),
        compiler_params=_cparams(1))(x)


def _adam_math(w, g, m, v):
    m = ADAM_B1 * m + (1.0 - ADAM_B1) * g
    v = ADAM_B2 * v + (1.0 - ADAM_B2) * (g * g)
    m_hat = m / (1.0 - ADAM_B1 ** ADAM_STEP)
    v_hat = v / (1.0 - ADAM_B2 ** ADAM_STEP)
    return -ADAM_LR * (m_hat / (jnp.sqrt(v_hat) + ADAM_EPS) + ADAM_WD * w), m, v


def adamw(w, g, m, v, name):
    r, c = w.shape[-2:]
    tr = r
    for cand in (512, 256, 128, 64, 32, 16, 8):
        if r % cand == 0 and cand * c * 4 <= 2 * 1024 * 1024:
            tr = cand
            break

    def body(w_ref, g_ref, m_ref, v_ref, d_ref, mo_ref, vo_ref):
        d_ref[...], mo_ref[...], vo_ref[...] = _adam_math(w_ref[...], g_ref[...], m_ref[...], v_ref[...])

    def spec(a):
        return pl.BlockSpec((tr, c), lambda i: (i, 0)) if a.ndim == 2 else pl.BlockSpec((None, tr, c), lambda i: (0, i, 0))

    return pl.pallas_call(
        body, name=name, grid=(r // tr,), in_specs=[spec(w), spec(g), spec(m), spec(v)], out_specs=[spec(w)] * 3,
        out_shape=[SDS(w.shape, F32)] * 3, compiler_params=_cparams(1))(w, g, m, v)


def adamw_many(ws, gs, ms, vs, name):
    n = len(ws)

    def body(*refs):
        w_r, g_r, m_r, v_r = (refs[k * n:(k + 1) * n] for k in range(4))
        d_r, mo_r, vo_r = (refs[(4 + k) * n:(5 + k) * n] for k in range(3))
        for k in range(n):
            d_r[k][...], mo_r[k][...], vo_r[k][...] = _adam_math(w_r[k][...], g_r[k][...], m_r[k][...], v_r[k][...])

    shapes = [SDS(w.shape, F32) for w in ws]
    outs = pl.pallas_call(body, name=name, out_shape=shapes * 3)(*ws, *gs, *ms, *vs)
    return outs[:n], outs[n:2 * n], outs[2 * n:]


PACK = {"ffn1_w_gate": (352, 352), "ffn1_w_up": (352, 352), "ffn1_w_down": (352, 352),
        "ffn2_w_gate": (352, 352), "ffn2_w_up": (352, 352), "ffn2_w_down": (352, 352),
        "w_out": (256, 256), "w_in": (406, 416), "w_ukv": (48, 48), "w_uq": (36, 48)}
TRANSPOSED = ("ffn1_w_gate", "ffn1_w_up", "ffn2_w_gate", "ffn2_w_up", "w_in", "w_ukv", "w_uq")
GATHER_GROUPS = (("ffn1_w_gate", "ffn1_w_up"), ("ffn1_w_down",),
                 ("w_in", "w_ukv", "w_uq", "w_out", "ffn2_w_gate", "ffn2_w_up", "ffn2_w_down"))
GRAD_GROUPS = (("ffn2", ("ffn2_w_gate", "ffn2_w_up", "ffn2_w_down")), ("mixer", ("w_out", "w_in", "w_ukv", "w_uq")),
               ("ffn1_down", ("ffn1_w_down",)), ("ffn1_gate", ("ffn1_w_gate",)), ("ffn1_up", ("ffn1_w_up",)))


def _pack_offsets(names):
    off, o = {}, 0
    for n in names:
        off[n] = (o, PACK[n][0])
        o += PACK[n][1]
    return off, o


def _shard_to_rows(name, w):
    w = w[0]
    if name in TRANSPOSED:
        w = w.T
    return w.reshape(-1, D_MODEL)


def _rows_to_shard(name, rows, like):
    shp = like.shape[1:]
    if name in TRANSPOSED:
        return rows.reshape(shp[1], shp[0]).T[None]
    return rows.reshape(shp)[None]


def _pack_shards(ws, names, dtype):
    parts = []
    for name in names:
        real, padded = PACK[name]
        rows = _shard_to_rows(name, ws[name]).astype(dtype)
        if padded > real:
            rows = jnp.pad(rows, ((0, padded - real), (0, 0)))
        parts.append(rows)
    return jnp.concatenate(parts, axis=0)


def _grad_rows(name, gw):
    real, padded = PACK[name]
    if name == "w_in":
        rows = _in_proj_rows_inv(gw).reshape(N_DEV, -1, D_MODEL)
    elif name == "w_ukv":
        hd = MLA_HEADS * HEAD_PAD
        rows = jnp.concatenate([gw[:hd].reshape(MLA_HEADS, HEAD_PAD, KV_LORA)[:, :QK_NOPE],
                                gw[hd:].reshape(MLA_HEADS, V_HEAD, KV_LORA)], axis=1).reshape(N_DEV, -1, D_MODEL)
    elif name == "w_uq":
        rows = gw.reshape(MLA_HEADS, HEAD_PAD, Q_LORA)[:, :QK_DIM].reshape(N_DEV, -1, D_MODEL)
    else:
        rows = gw.reshape(N_DEV, -1, D_MODEL)
    if padded > real:
        rows = jnp.pad(rows, ((0, 0), (0, padded - real), (0, 0)))
    return rows


def _pack_rows(arrs):
    parts = []
    for a in arrs:
        flat = a.reshape(-1).astype(F32)
        pad = (-flat.shape[0]) % D_MODEL
        if pad:
            flat = jnp.pad(flat, (0, pad))
        parts.append(flat.reshape(-1, D_MODEL))
    out = jnp.concatenate(parts, axis=0)
    pad = (-out.shape[0]) % 8
    if pad:
        out = jnp.pad(out, ((0, pad), (0, 0)))
    return out


def _unpack_rows(packed, shapes):
    out, row = [], 0
    for shp in shapes:
        n = math.prod(shp)
        nrow = -(-n // D_MODEL)
        out.append(packed[row:row + nrow].reshape(-1)[:n].reshape(shp))
        row += nrow
    return out


def _in_proj_rows(w_t):
    return jnp.concatenate([w_t[0:2560], w_t[2576:2960], w_t[2960:3216], w_t[2560:2576], w_t[3216:3248],
                            jnp.zeros((D_IN_PAD - D_IN, D_MODEL), w_t.dtype)], axis=0)


def _in_proj_rows_inv(d):
    return jnp.concatenate([d[0:2560], d[3200:3216], d[2560:2944], d[2944:3200], d[3216:3248]], axis=0)


def _rope_tables(positions):
    inv_freq = ROPE_THETA ** (-jnp.arange(0, QK_ROPE, 2, dtype=F32) / QK_ROPE)
    ang = positions[..., None].astype(F32) * inv_freq
    cos, sin = jnp.cos(ang), jnp.sin(ang)
    one = jnp.ones(ang.shape[:2] + (QK_NOPE,), F32)
    zero = jnp.zeros_like(one)
    z16, z32, o32 = zero[..., :16], zero[..., :32], one[..., :32]
    cc = jnp.concatenate([one, cos, cos, o32], axis=-1)
    sp = jnp.concatenate([zero, z16, sin, z32], axis=-1)
    sm = jnp.concatenate([zero, -sin, z16, z32], axis=-1)
    return cc, sp, sm


def weight_views(gathered):
    def _seg(name):
        names, g = next((names, g) for names, g in zip(GATHER_GROUPS, gathered) if name in names)
        o, r = _pack_offsets(names)[0][name]
        return g[:, o:o + r]

    full = lambda name: _seg(name).reshape(-1, D_MODEL)
    ukv = _seg("w_ukv").reshape(MLA_HEADS, QK_NOPE + V_HEAD, KV_LORA)
    wukv_t = jnp.concatenate([jnp.pad(ukv[:, :QK_NOPE], ((0, 0), (0, HEAD_PAD - QK_NOPE), (0, 0))).reshape(-1, KV_LORA),
                              ukv[:, QK_NOPE:].reshape(-1, KV_LORA)], axis=0)
    uq = _seg("w_uq").reshape(MLA_HEADS, QK_DIM, Q_LORA)
    wuq_t = jnp.pad(uq, ((0, 0), (0, HEAD_PAD - QK_DIM), (0, 0))).reshape(-1, Q_LORA)
    return dict(wg1_t=full("ffn1_w_gate"), wu1_t=full("ffn1_w_up"), wd1=full("ffn1_w_down"),
                wg2_t=full("ffn2_w_gate"), wu2_t=full("ffn2_w_up"), wd2=full("ffn2_w_down"),
                wo=full("w_out"), win_t=_in_proj_rows(full("w_in")), wukv_t=wukv_t, wuq_t=wuq_t)


def _ffn_bwd(tag, dxn, do, dgate, x, h, gg, uu, a, sc, norm_w, wg_t, wu_t, wd, below):
    f2 = wd.shape[0] // 2
    dwd = mm_tn(a, do, f2, D_MODEL, tag + "_dwd")
    dgg, duu = ffn_dact(do, wd, gg, uu, tag + "_dact")
    dwg_t = mm_tn(dgg, h, f2, D_MODEL, tag + "_dwg")
    dwu_t = mm_tn(duu, h, f2, D_MODEL, tag + "_dwu")
    dx, dsc, dsh, dnw, *nxt = dh_norm_bwd([dgg, duu], [wg_t, wu_t], x, dxn, norm_w, sc, tag + "_dh", below)
    return dx, (dsh, dsc, dgate), dnw, (dwg_t, dwu_t, dwd), nxt


def local_step(x, tgt, positions, mod, wv, p):
    nb, s, d = x.shape
    sh1, sc1, g1, sh2, sc2, g2, sh3, sc3, g3 = mod
    cc, sp, sm = _rope_tables(positions)
    lane_head = jnp.arange(D_SSD, dtype=I32)[None, :] // SSD_HEAD_DIM
    e_mat = (lane_head == jnp.arange(LANES, dtype=I32)[:, None]).astype(BF16)
    et_mat = e_mat.T
    rr, cl = jnp.arange(LANES, dtype=I32)[:, None], jnp.arange(LANES, dtype=I32)[None, :]
    place = ((cl == rr + (QK_NOPE - SSD_HEADS)) & (rr >= SSD_HEADS) & (rr < SSD_HEADS + QK_ROPE)).astype(F32)
    dtb = jnp.pad(p["dt_bias"], ((0, 0), (0, LANES - SSD_HEADS)))
    alog = jnp.pad(p["a_log"], ((0, 0), (0, LANES - SSD_HEADS)))
    dskip_e = jnp.repeat(p["d_skip"], SSD_HEAD_DIM, axis=1)

    h1 = norm_mod(x, p["norm_ffn1"], sc1, sh1, "ffn1_norm")
    gg1, uu1, a1 = ffn_up(h1, wv["wg1_t"], wv["wu1_t"], "ffn1_up")
    x1, o1, h2 = ffn_down(a1, wv["wd1"], x, g1, 0.5, "ffn1_down", (p["norm_mix"], sc2, sh2))
    z, u, cq, ckv, misc = in_proj(h2, wv["win_t"], "in_proj")
    xs, bm, cm_ = conv_fwd(u, p["conv_w"], p["conv_b"], "conv_fwd")
    ys, y, prev = ssd_fwd(xs, bm, cm_, misc, z, dtb, alog, dskip_e, p["ssd_norm_w"], e_mat, "ssd_fwd")
    q, k, v, qn, kvn = qkv_fwd(cq, ckv, misc, cc, sp, sm, p["q_norm_w"], p["kv_norm_w"], wv["wuq_t"], wv["wukv_t"],
                               place, "qkv_fwd")
    attn, lse = flash_fwd(q, k, v, "flash_fwd")
    x2, o2, ym, h3 = out_proj(ys, attn, p["mla_norm_w"], wv["wo"], x1, g2, (p["norm_ffn2"], sc3, sh3), "out_proj")
    gg3, uu3, a3 = ffn_up(h3, wv["wg2_t"], wv["wu2_t"], "ffn2_up")
    loss, dx3, dnfin, do3, dg3 = ffn_down_final(a3, wv["wd2"], x2, g3, 0.5, p["norm_final"], tgt, "ffn2_down_loss")

    dx2, dmod3, dnf2, (dwg2, dwu2, dwd2), (dout, dg2) = _ffn_bwd(
        "ffn2", dx3, do3, dg3, x2, h3, gg3, uu3, a3, sc3, p["norm_ffn2"], wv["wg2_t"], wv["wu2_t"], wv["wd2"],
        (o2, g2, 1.0))
    dys, dattn, dlt, dmlan = out_proj_bwd(dout, attn, p["mla_norm_w"], wv["wo"], "out_proj_bwd")
    dwo = jnp.concatenate([mm_tn(ys, dout, D_SSD, D_MODEL, "dwo_ssd"), mm_tn(ym, dout, D_SSD, D_MODEL, "dwo_mla")], axis=0)
    dxs, dbm, dcm, dz, ddt, dssdn, ddsk_lane, ddtb, dalog = ssd_bwd(
        dys, y, z, xs, bm, cm_, misc, prev, dtb, alog, dskip_e, p["ssd_norm_w"], e_mat, et_mat, "ssd_bwd")
    dq, dk, dv = flash_bwd(q, k, v, dattn, lse, dlt, "flash_bwd")
    dcq, dckv, dmisc, dqp, dkvc, dqn, dkvn = qkv_bwd(dq, dk, dv, ddt, cq, ckv, cc, sp, sm, p["q_norm_w"], p["kv_norm_w"],
                                                     wv["wuq_t"], wv["wukv_t"], place.T, "qkv_bwd")
    dwuq = mm_tn(dqp, qn, MLA_HEADS * HEAD_PAD, Q_LORA, "dwuq")
    dwukv = mm_tn(dkvc, kvn, MLA_HEADS * HEAD_PAD, KV_LORA, "dwukv")
    dvv, dconv = conv_bwd_a(dxs, dbm, dcm, u, p["conv_w"], p["conv_b"], "conv_bwd_a")
    dconv = conv_grads_fold(dconv, "conv_grads_fold")
    du = conv_bwd_b(dvv, p["conv_w"], "conv_bwd_b")
    dproj = jnp.concatenate([dz, du, dcq, dckv, dmisc], axis=-1)
    dwin = mm_tn(dproj, h2, D_IN_PAD // 2, D_MODEL, "dwin")
    dx1, dsc2, dsh2, dnmix, do1, dg1 = dh_norm_bwd([dproj], [wv["win_t"]], x1, dx2, p["norm_mix"], sc2, "mix_dh",
                                                   (o1, g1, 0.5))
    dx0, dmod1, dnf1, (dwg1, dwu1, dwd1), _ = _ffn_bwd(
        "ffn1", dx1, do1, dg1, x, h1, gg1, uu1, a1, sc1, p["norm_ffn1"], wv["wg1_t"], wv["wu1_t"], wv["wd1"], None)

    dmod = jnp.concatenate([*dmod1, dsh2, dsc2, dg2, *dmod3], axis=1).reshape(nb, N_MOD * d)
    return dict(
        loss=loss, dx=dx0, dmod=dmod, norm_ffn1=dnf1, norm_mix=dnmix, norm_ffn2=dnf2, norm_final=dnfin,
        ssd_norm_w=dssdn, mla_norm_w=dmlan, q_norm_w=dqn, kv_norm_w=dkvn,
        dt_bias=ddtb[:, :SSD_HEADS], a_log=dalog[:, :SSD_HEADS],
        d_skip=squeeze_heads(ddsk_lane, et_mat, "d_skip_heads")[:, :SSD_HEADS],
        conv_b=dconv[4:5], conv_w=dconv[0:4],
        gw=dict(ffn1_w_gate=dwg1, ffn1_w_up=dwu1, ffn1_w_down=dwd1, ffn2_w_gate=dwg2, ffn2_w_up=dwu2, ffn2_w_down=dwd2,
                w_out=dwo, w_in=dwin, w_ukv=dwukv, w_uq=dwuq))


def kernel(x, c, positions, w_ada, b_ada, norm_ffn1, ffn1_w_gate, ffn1_w_up, ffn1_w_down, norm_mix, w_in, conv_w, conv_b, dt_bias, a_log, d_skip, ssd_norm_w, q_norm_w, w_uq, kv_norm_w, w_ukv, mla_norm_w, w_out, norm_ffn2, ffn2_w_gate, ffn2_w_up, ffn2_w_down, norm_final, loss_target, m_w_ada, m_b_ada, m_norm_ffn1, m_ffn1_w_gate, m_ffn1_w_up, m_ffn1_w_down, m_norm_mix, m_w_in, m_conv_w, m_conv_b, m_dt_bias, m_a_log, m_d_skip, m_ssd_norm_w, m_q_norm_w, m_w_uq, m_kv_norm_w, m_w_ukv, m_mla_norm_w, m_w_out, m_norm_ffn2, m_ffn2_w_gate, m_ffn2_w_up, m_ffn2_w_down, m_norm_final, v_w_ada, v_b_ada, v_norm_ffn1, v_ffn1_w_gate, v_ffn1_w_up, v_ffn1_w_down, v_norm_mix, v_w_in, v_conv_w, v_conv_b, v_dt_bias, v_a_log, v_d_skip, v_ssd_norm_w, v_q_norm_w, v_w_uq, v_kv_norm_w, v_w_ukv, v_mla_norm_w, v_w_out, v_norm_ffn2, v_ffn2_w_gate, v_ffn2_w_up, v_ffn2_w_down, v_norm_final):
    names = ["w_ada", "b_ada", "norm_ffn1", "ffn1_w_gate", "ffn1_w_up", "ffn1_w_down", "norm_mix", "w_in", "conv_w",
             "conv_b", "dt_bias", "a_log", "d_skip", "ssd_norm_w", "q_norm_w", "w_uq", "kv_norm_w", "w_ukv",
             "mla_norm_w", "w_out", "norm_ffn2", "ffn2_w_gate", "ffn2_w_up", "ffn2_w_down", "norm_final"]
    W = dict(zip(names, (w_ada, b_ada, norm_ffn1, ffn1_w_gate, ffn1_w_up, ffn1_w_down, norm_mix, w_in, conv_w, conv_b, dt_bias, a_log, d_skip, ssd_norm_w, q_norm_w, w_uq, kv_norm_w, w_ukv, mla_norm_w, w_out, norm_ffn2, ffn2_w_gate, ffn2_w_up, ffn2_w_down, norm_final)))
    M = dict(zip(names, (m_w_ada, m_b_ada, m_norm_ffn1, m_ffn1_w_gate, m_ffn1_w_up, m_ffn1_w_down, m_norm_mix, m_w_in, m_conv_w, m_conv_b, m_dt_bias, m_a_log, m_d_skip, m_ssd_norm_w, m_q_norm_w, m_w_uq, m_kv_norm_w, m_w_ukv, m_mla_norm_w, m_w_out, m_norm_ffn2, m_ffn2_w_gate, m_ffn2_w_up, m_ffn2_w_down, m_norm_final)))
    V = dict(zip(names, (v_w_ada, v_b_ada, v_norm_ffn1, v_ffn1_w_gate, v_ffn1_w_up, v_ffn1_w_down, v_norm_mix, v_w_in, v_conv_w, v_conv_b, v_dt_bias, v_a_log, v_d_skip, v_ssd_norm_w, v_q_norm_w, v_w_uq, v_kv_norm_w, v_w_ukv, v_mla_norm_w, v_w_out, v_norm_ffn2, v_ffn2_w_gate, v_ffn2_w_up, v_ffn2_w_down, v_norm_final)))

    nb, s, d = x.shape
    me = 4 * lax.axis_index("x") + 2 * lax.axis_index("y") + lax.axis_index("c")
    n_ada = w_ada.shape[2]

    cshape = [(nb, d), conv_w.shape[1:]]
    cg = all_gather8(_pack_rows([c, conv_w[0]]), "gather_c")
    c_all = jnp.stack([_unpack_rows(cg[k], cshape)[0] for k in range(N_DEV)]).reshape(N_DEV * nb, d)
    conv_w_full = jnp.concatenate([_unpack_rows(cg[k], cshape)[1] for k in range(N_DEV)], axis=1)
    g_ffn1 = all_gather8(_pack_shards(W, GATHER_GROUPS[0], BF16), "gather_w_ffn1")

    b_ada_cols = lax.dynamic_slice(b_ada, (0, me * n_ada), (1, n_ada))
    mod_cols, c_act = adaln_fwd(c_all, w_ada[0], b_ada_cols, "adaln_fwd")
    mod_g = all_gather8(mod_cols, "gather_mod")
    g_ffn1, mod_g, down1, rest = lax.optimization_barrier(
        (g_ffn1, mod_g, _pack_shards(W, GATHER_GROUPS[1], BF16), _pack_shards(W, GATHER_GROUPS[2], BF16)))
    g_down1 = sc_all_gather8(down1, "gather_w_ffn1_down", 1)
    wv = weight_views((g_ffn1, g_down1, sc_all_gather8(rest, "gather_w_rest", 7)))
    mod = lax.dynamic_slice(mod_g, (0, me * nb, 0), (N_DEV, nb, n_ada)).transpose(1, 0, 2).reshape(nb, N_MOD, 1, d)
    mod = [mod[:, k] for k in range(N_MOD)]

    P = dict(W)
    P["conv_w"] = conv_w_full
    P["norm_final"] = norm_final.reshape(1, d)
    R = local_step(x, loss_target, positions, mod, wv, P)

    dmod = R["dmod"]
    partial_shapes = [(1,), (1, d), (1, d), (1, d), (1, d), (1, d), (1, d), (1, Q_LORA), (1, KV_LORA),
                      (1, SSD_HEADS), (1, SSD_HEADS), (1, SSD_HEADS), (1, D_CONV), (4, D_CONV), (1, N_MOD * d),
                      (nb, N_MOD * d)]
    partial = _pack_rows([R["loss"][0, :1], R["norm_ffn1"], R["norm_mix"], R["norm_ffn2"], R["norm_final"],
                          R["ssd_norm_w"], R["mla_norm_w"], R["q_norm_w"], R["kv_norm_w"],
                          R["dt_bias"], R["a_log"], R["d_skip"], R["conv_b"], R["conv_w"],
                          sum_rows(dmod, "dmod_rows"), dmod])
    partial_g = all_gather8(partial, "gather_partials")
    (loss, g_nf1, g_nmix, g_nf2, g_nfin, g_ssdn, g_mlan, g_qn, g_kvn, g_dtb, g_alog, g_dskip, g_convb, g_convw,
     g_bada, _) = _unpack_rows(sum_blocks(partial_g, "sum_partials"), partial_shapes)
    dmod_all = jnp.stack([_unpack_rows(partial_g[k], partial_shapes)[-1] for k in range(N_DEV)]).reshape(N_DEV * nb, -1)
    g_wada = adaln_bwd(c_act, lax.dynamic_slice(dmod_all, (0, me * n_ada), (N_DEV * nb, n_ada)), "adaln_bwd")
    n_cw = conv_w.shape[2]
    G = {"w_ada": g_wada[None], "b_ada": g_bada, "norm_ffn1": g_nf1, "norm_mix": g_nmix, "norm_ffn2": g_nf2,
         "norm_final": g_nfin.reshape(d), "ssd_norm_w": g_ssdn, "mla_norm_w": g_mlan, "q_norm_w": g_qn,
         "kv_norm_w": g_kvn, "dt_bias": g_dtb, "a_log": g_alog, "d_skip": g_dskip, "conv_b": g_convb,
         "conv_w": lax.dynamic_slice(g_convw, (0, me * n_cw), (4, n_cw))[None]}

    DW, NM, NV = {}, {}, {}
    gw = R["gw"]
    for k, (tag, group) in enumerate(GRAD_GROUPS):
        send = jnp.concatenate([_grad_rows(name, gw[name]) for name in group], axis=1).astype(BF16)
        recv = sc_all_to_all8(send, "exchange_" + tag, 2 + k)
        gsum = sum_blocks(recv, "sum_" + tag)
        for name, (o, r) in _pack_offsets(group)[0].items():
            G[name] = _rows_to_shard(name, gsum[o:o + r], W[name])
            DW[name], NM[name], NV[name] = adamw(W[name], G[name], M[name], V[name], "adamw_" + name)
    DW["w_ada"], NM["w_ada"], NV["w_ada"] = adamw(w_ada, g_wada, m_w_ada, v_w_ada, "adamw_w_ada")
    small = [n for n in names if n not in DW]
    as2d = lambda a: a.reshape(-1, a.shape[-1])
    outs = adamw_many([as2d(W[n]) for n in small], [as2d(G[n]) for n in small], [as2d(M[n]) for n in small],
                      [as2d(V[n]) for n in small], "adamw_small")
    for res, dst in zip(outs, (DW, NM, NV)):
        for n, t in zip(small, res):
            dst[n] = t.reshape(W[n].shape)
    return (loss.reshape(()), R["dx"], *[G[n] for n in names], *[DW[n] for n in names], *[NM[n] for n in names],
            *[NV[n] for n in names])
```

```python
import math

import jax
import jax.numpy as jnp
from jax import lax
from jax.experimental import pallas as pl
from jax.experimental.pallas import tpu as pltpu
from jax.experimental.pallas import tpu_sc as plsc

F32, BF16, I32 = jnp.float32, jnp.bfloat16, jnp.int32
HI = lax.Precision.HIGHEST
SDS = jax.ShapeDtypeStruct
MESH = pl.DeviceIdType.MESH

D_MODEL = 1024
D_FF = 2816
D_SSD = 1024
SSD_HEADS = 16
SSD_HEAD_DIM = 64
SSD_GROUPS = 2
SSD_STATE = 128
CHUNK = 128
MLA_HEADS = 8
QK_NOPE = 64
QK_ROPE = 32
QK_DIM = 96
V_HEAD = 128
Q_LORA = 384
KV_LORA = 256
ROPE_THETA = 10000.0
N_MOD = 9
EPS = 1e-6
D_CONV = 1536
D_IN = 3248
D_IN_PAD = 3328
HEAD_PAD = 128
N_DEV = 8
ADAM_LR, ADAM_B1, ADAM_B2, ADAM_EPS, ADAM_WD, ADAM_STEP = 0.001, 0.9, 0.999, 1e-08, 0.01, 10

SAVED_ACT = BF16
VMEM_LIMIT = 56 * 1024 * 1024
LANES = 128
NT_DIMS = (((1,), (1,)), ((), ()))
TN_DIMS = (((0,), (0,)), ((), ()))


def _cparams(n_axes):
    return pltpu.CompilerParams(dimension_semantics=("arbitrary",) * n_axes, vmem_limit_bytes=VMEM_LIMIT)


def _row(tm, d):
    return pl.BlockSpec((None, tm, d), lambda b, i: (b, i, 0))


def _bvec(d):
    return pl.BlockSpec((None, 1, d), lambda b, i: (b, 0, 0))


def _full(shape):
    n = len(shape)
    return pl.BlockSpec(shape, lambda *_: (0,) * n)


def _sigmoid(x):
    return 1.0 / (1.0 + jnp.exp(-x))


def _softplus(x):
    return jnp.maximum(x, 0.0) + jnp.log(1.0 + jnp.exp(-jnp.abs(x)))


def _rms(x):
    return lax.rsqrt(jnp.mean(x * x, axis=-1, keepdims=True) + EPS)


def _rms_bwd(dn, n, r):
    return r * (dn - n * jnp.mean(dn * n, axis=-1, keepdims=True))


def _first_step():
    return (pl.program_id(0) == 0) & (pl.program_id(1) == 0)


def all_gather8(x, name):
    r, c = x.shape

    def body(x_ref, out_ref, send_sems, recv_sems, local_sem):
        mx, my, mc = lax.axis_index("x"), lax.axis_index("y"), lax.axis_index("c")
        me, sibling = (mx, my, mc), (mx, my, 1 - mc)
        chips = [(1 - mx, my), (mx, 1 - my), (1 - mx, 1 - my)]

        def rows(px, py, pc):
            return out_ref.at[4 * px + 2 * py + pc]

        def copy(k, block, to, src=None):
            return pltpu.make_async_remote_copy(
                src_ref=rows(*block) if src is None else src, dst_ref=rows(*block),
                send_sem=send_sems.at[k], recv_sem=recv_sems.at[k], device_id=to, device_id_type=MESH)

        mine = pltpu.make_async_copy(x_ref, rows(*me), local_sem)
        mine.start()
        first = [copy(0, me, sibling, src=x_ref)]
        first += [copy(1 + j, me, (*chip, mc), src=x_ref) for j, chip in enumerate(chips)]
        for cp in first:
            cp.start()
        passed = [copy(4 + j, (*chip, mc), sibling) for j, chip in enumerate(chips)]
        for j, chip in enumerate(chips):
            copy(1 + j, (*chip, mc), me).wait_recv()
            passed[j].start()
        copy(0, sibling, me).wait_recv()
        for j, chip in enumerate(chips):
            copy(4 + j, (*chip, 1 - mc), me).wait_recv()
        for cp in first + passed:
            cp.wait_send()
        mine.wait()

    return pl.pallas_call(
        body, name=name,
        out_shape=SDS((N_DEV, r, c), x.dtype),
        in_specs=[pl.BlockSpec(memory_space=pl.ANY)],
        out_specs=pl.BlockSpec(memory_space=pl.ANY),
        scratch_shapes=[pltpu.SemaphoreType.DMA((7,)), pltpu.SemaphoreType.DMA((7,)), pltpu.SemaphoreType.DMA],
    )(x)


def all_to_all8(x, name):
    _, r, c = x.shape

    def body(x_ref, out_ref, send_sems, recv_sems, local_sem):
        mx, my, mc = lax.axis_index("x"), lax.axis_index("y"), lax.axis_index("c")
        me = 4 * mx + 2 * my + mc
        mine = pltpu.make_async_copy(x_ref.at[me], out_ref.at[me], local_sem)
        mine.start()
        copies = []
        for rel in range(1, N_DEV):
            px = 1 - mx if rel & 4 else mx
            py = 1 - my if rel & 2 else my
            pc = 1 - mc if rel & 1 else mc
            cp = pltpu.make_async_remote_copy(
                src_ref=x_ref.at[4 * px + 2 * py + pc], dst_ref=out_ref.at[me],
                send_sem=send_sems.at[rel - 1], recv_sem=recv_sems.at[rel - 1],
                device_id=(px, py, pc), device_id_type=MESH)
            cp.start()
            copies.append(cp)
        for cp in copies:
            cp.wait()
        mine.wait()

    return pl.pallas_call(
        body, name=name,
        out_shape=SDS((N_DEV, r, c), x.dtype),
        in_specs=[pl.BlockSpec(memory_space=pl.ANY)],
        out_specs=pl.BlockSpec(memory_space=pl.ANY),
        scratch_shapes=[pltpu.SemaphoreType.DMA((7,)), pltpu.SemaphoreType.DMA((7,)), pltpu.SemaphoreType.DMA],
    )(x)


def _sequencer_kernel(name, collective_id):
    return pl.kernel(
        mesh=plsc.ScalarSubcoreMesh(axis_name="seq", num_cores=1), name=name,
        scratch_types=(pltpu.SemaphoreType.DMA((7,)), pltpu.SemaphoreType.DMA((7,)), pltpu.SemaphoreType.DMA),
        compiler_params=pltpu.CompilerParams(collective_id=collective_id))


def _handshake(peers):
    barrier = pltpu.get_barrier_semaphore()
    for peer in peers:
        pl.semaphore_signal(barrier, inc=1, device_id=peer, device_id_type=MESH)
    pl.semaphore_wait(barrier, len(peers))


def sc_all_gather8(x, name, collective_id):
    r, c = x.shape
    x_ref = jax.new_ref(x, memory_space=pltpu.MemorySpace.HBM)
    out_ref = jax.empty_ref(SDS((N_DEV, r, c), x.dtype), memory_space=pltpu.MemorySpace.HBM)

    @_sequencer_kernel(name, collective_id)
    def launch(send_sems, recv_sems, local_sem):
        mx, my, mc = lax.axis_index("x"), lax.axis_index("y"), lax.axis_index("c")
        me, sibling = (mx, my, mc), (mx, my, 1 - mc)
        chips = [(1 - mx, my), (mx, 1 - my), (1 - mx, 1 - my)]
        _handshake([sibling] + [(*chip, mc) for chip in chips])

        def rows(px, py, pc):
            return out_ref.at[4 * px + 2 * py + pc]

        def copy(k, block, to, src=None):
            return pltpu.make_async_remote_copy(
                src_ref=rows(*block) if src is None else src, dst_ref=rows(*block),
                send_sem=send_sems.at[k], recv_sem=recv_sems.at[k], device_id=to, device_id_type=MESH)

        mine = pltpu.make_async_copy(x_ref, rows(*me), local_sem)
        mine.start()
        first = [copy(0, me, sibling, src=x_ref)]
        first += [copy(1 + j, me, (*chip, mc), src=x_ref) for j, chip in enumerate(chips)]
        for cp in first:
            cp.start()
        passed = [copy(4 + j, (*chip, mc), sibling) for j, chip in enumerate(chips)]
        for j, chip in enumerate(chips):
            copy(1 + j, (*chip, mc), me).wait_recv()
            passed[j].start()
        copy(0, sibling, me).wait_recv()
        for j, chip in enumerate(chips):
            copy(4 + j, (*chip, 1 - mc), me).wait_recv()
        for cp in first + passed:
            cp.wait_send()
        mine.wait()

    launch()
    return out_ref[...]


def sc_all_to_all8(x, name, collective_id):
    x_ref = jax.new_ref(x, memory_space=pltpu.MemorySpace.HBM)
    out_ref = jax.empty_ref(SDS(x.shape, x.dtype), memory_space=pltpu.MemorySpace.HBM)

    @_sequencer_kernel(name, collective_id)
    def launch(send_sems, recv_sems, local_sem):
        mx, my, mc = lax.axis_index("x"), lax.axis_index("y"), lax.axis_index("c")
        me = 4 * mx + 2 * my + mc
        peers = [(1 - mx if rel & 4 else mx, 1 - my if rel & 2 else my, 1 - mc if rel & 1 else mc)
                 for rel in range(1, N_DEV)]
        _handshake(peers)
        mine = pltpu.make_async_copy(x_ref.at[me], out_ref.at[me], local_sem)
        mine.start()
        copies = []
        for k, (px, py, pc) in enumerate(peers):
            cp = pltpu.make_async_remote_copy(
                src_ref=x_ref.at[4 * px + 2 * py + pc], dst_ref=out_ref.at[me],
                send_sem=send_sems.at[k], recv_sem=recv_sems.at[k], device_id=(px, py, pc), device_id_type=MESH)
            cp.start()
            copies.append(cp)
        for cp in copies:
            cp.wait()
        mine.wait()

    launch()
    return out_ref[...]


def norm_mod(x, w, sc, sh, name):
    b, s, d = x.shape
    tm = min(512, s)

    def body(x_ref, w_ref, sc_ref, sh_ref, h_ref):
        xv = x_ref[...]
        n = xv * _rms(xv)
        h_ref[...] = ((n * w_ref[...]) * (1.0 + sc_ref[...]) + sh_ref[...]).astype(BF16)

    return pl.pallas_call(
        body, name=name, grid=(b, s // tm),
        in_specs=[_row(tm, d), _full((1, d)), _bvec(d), _bvec(d)],
        out_specs=_row(tm, d), out_shape=SDS((b, s, d), BF16), compiler_params=_cparams(2))(x, w, sc, sh)


def ffn_up(h, wg_t, wu_t, name):
    b, s, d = h.shape
    f = wg_t.shape[0]
    tm, tn = min(512, s), f // 2

    def body(h_ref, wg_ref, wu_ref, g_ref, u_ref, a_ref):
        hv = h_ref[...]
        g = lax.dot_general(hv, wg_ref[...], NT_DIMS, preferred_element_type=F32)
        u = lax.dot_general(hv, wu_ref[...], NT_DIMS, preferred_element_type=F32)
        g_ref[...] = g.astype(g_ref.dtype)
        u_ref[...] = u.astype(u_ref.dtype)
        a_ref[...] = (g * _sigmoid(g) * u).astype(BF16)

    hs = pl.BlockSpec((None, tm, d), lambda j, bb, i: (bb, i, 0))
    ws = pl.BlockSpec((tn, d), lambda j, bb, i: (j, 0))
    os_ = pl.BlockSpec((None, tm, tn), lambda j, bb, i: (bb, i, j))
    return pl.pallas_call(
        body, name=name, grid=(f // tn, b, s // tm),
        in_specs=[hs, ws, ws], out_specs=[os_, os_, os_],
        out_shape=[SDS((b, s, f), SAVED_ACT), SDS((b, s, f), SAVED_ACT), SDS((b, s, f), BF16)],
        compiler_params=_cparams(3))(h, wg_t, wu_t)


def _norm_mod_tile(xv, w_ref, sc_ref, sh_ref):
    return ((xv * _rms(xv) * w_ref[...]) * (1.0 + sc_ref[...]) + sh_ref[...]).astype(BF16)


def ffn_down(a, wd, x, gate, scale, name, above=None):
    b, s, f = a.shape
    d = wd.shape[1]
    tm = min(512, s)

    def body(a_ref, wd_ref, x_ref, g_ref, *rest):
        xn_ref, o_ref = rest[-3:-1] if above else rest
        o = jnp.dot(a_ref[...], wd_ref[...], preferred_element_type=F32)
        xn = x_ref[...] + (scale * g_ref[...]) * o
        xn_ref[...] = xn
        o_ref[...] = o.astype(BF16)
        if above:
            rest[-1][...] = _norm_mod_tile(xn, *rest[0:3])

    extra = above is not None
    return pl.pallas_call(
        body, name=name, grid=(b, s // tm),
        in_specs=[_row(tm, f), _full((f, d)), _row(tm, d), _bvec(d)] + ([_full((1, d)), _bvec(d), _bvec(d)] if extra else []),
        out_specs=[_row(tm, d), _row(tm, d)] + ([_row(tm, d)] if extra else []),
        out_shape=[SDS((b, s, d), F32), SDS((b, s, d), BF16)] + ([SDS((b, s, d), BF16)] if extra else []),
        compiler_params=_cparams(2))(a, wd, x, gate, *(above or ()))


def ffn_down_final(a, wd, x, gate, scale, w_final, tgt, name):
    b, s, f = a.shape
    d = wd.shape[1]
    tm = min(512, s)

    def body(a_ref, wd_ref, x_ref, g_ref, w_ref, t_ref, loss_ref, dx_ref, dw_ref, do_ref, dg_ref):
        @pl.when(_first_step())
        def _():
            loss_ref[...] = jnp.zeros_like(loss_ref)
            dw_ref[...] = jnp.zeros_like(dw_ref)

        @pl.when(pl.program_id(1) == 0)
        def _():
            dg_ref[...] = jnp.zeros_like(dg_ref)
        o = jnp.dot(a_ref[...], wd_ref[...], preferred_element_type=F32)
        sg = scale * g_ref[...]
        xv = x_ref[...] + sg * o
        r = _rms(xv)
        n = xv * r
        wv = w_ref[...]
        e = n * wv - t_ref[...]
        loss_ref[...] += jnp.sum(e * e) * (0.5 / d)
        dy = e * (1.0 / d)
        dw_ref[...] += jnp.sum(dy * n, axis=0, keepdims=True)
        dx = _rms_bwd(dy * wv, n, r)
        dx_ref[...] = dx
        do_ref[...] = (sg * dx).astype(BF16)
        dg_ref[...] += jnp.sum(scale * dx * o, axis=0, keepdims=True)

    return pl.pallas_call(
        body, name=name, grid=(b, s // tm),
        in_specs=[_row(tm, f), _full((f, d)), _row(tm, d), _bvec(d), _full((1, d)), _row(tm, d)],
        out_specs=[_full((1, LANES)), _row(tm, d), _full((1, d)), _row(tm, d), _bvec(d)],
        out_shape=[SDS((1, LANES), F32), SDS((b, s, d), F32), SDS((1, d), F32), SDS((b, s, d), BF16), SDS((b, 1, d), F32)],
        compiler_params=_cparams(2))(a, wd, x, gate, w_final, tgt)


def ffn_dact(do, wd, g, u, name):
    b, s, d = do.shape
    f = wd.shape[0]
    tm, tn = min(512, s), f // 2

    def body(do_ref, wd_ref, g_ref, u_ref, dg_ref, du_ref):
        da = lax.dot_general(do_ref[...], wd_ref[...], NT_DIMS, preferred_element_type=F32)
        gv = g_ref[...].astype(F32)
        sg = _sigmoid(gv)
        dg_ref[...] = (da * u_ref[...].astype(F32) * (sg * (1.0 + gv * (1.0 - sg)))).astype(BF16)
        du_ref[...] = (da * (gv * sg)).astype(BF16)

    dos = pl.BlockSpec((None, tm, d), lambda j, bb, i: (bb, i, 0))
    ws = pl.BlockSpec((tn, d), lambda j, bb, i: (j, 0))
    es = pl.BlockSpec((None, tm, tn), lambda j, bb, i: (bb, i, j))
    return pl.pallas_call(
        body, name=name, grid=(f // tn, b, s // tm),
        in_specs=[dos, ws, es, es], out_specs=[es, es],
        out_shape=[SDS((b, s, f), BF16), SDS((b, s, f), BF16)], compiler_params=_cparams(3))(do, wd, g, u)


def mm_tn(a, bm, tma, tnb, name):
    b, s, ka = a.shape
    nb = bm.shape[2]
    tk = min(2048, s)
    nk = s // tk

    def body(a_ref, b_ref, o_ref, acc):
        first = (pl.program_id(2) == 0) & (pl.program_id(3) == 0)
        last = (pl.program_id(2) == b - 1) & (pl.program_id(3) == nk - 1)
        part = lax.dot_general(a_ref[...], b_ref[...], TN_DIMS, preferred_element_type=F32)

        @pl.when(first)
        def _():
            acc[...] = part

        @pl.when(jnp.logical_not(first))
        def _():
            acc[...] += part

        @pl.when(last)
        def _():
            o_ref[...] = acc[...].astype(BF16)

    return pl.pallas_call(
        body, name=name, grid=(ka // tma, nb // tnb, b, nk),
        in_specs=[pl.BlockSpec((None, tk, tma), lambda i, j, bb, k: (bb, k, i)),
                  pl.BlockSpec((None, tk, tnb), lambda i, j, bb, k: (bb, k, j))],
        out_specs=pl.BlockSpec((tma, tnb), lambda i, j, bb, k: (i, j)),
        out_shape=SDS((ka, nb), BF16), scratch_shapes=[pltpu.VMEM((tma, tnb), F32)],
        compiler_params=_cparams(4))(a, bm)


def _gate_bwd_specs(tm, d, b, s):
    return ([_row(tm, d), _bvec(d)], [_row(tm, d), _bvec(d)], [SDS((b, s, d), BF16), SDS((b, 1, d), F32)])


def _gate_bwd_tile(dx, scale, o_ref, g_ref, do_ref, dg_ref):
    do_ref[...] = ((scale * g_ref[...]) * dx).astype(BF16)
    dg_ref[...] += jnp.sum(scale * dx * o_ref[...].astype(F32), axis=0, keepdims=True)


def n_in_bytes(arrs):
    return sum(a.size * a.dtype.itemsize for a in arrs)


def dh_norm_bwd(dys, wts, x, dxn, w, sc, name, below=None):
    b, s, d = x.shape
    tm = min(512 if n_in_bytes(wts) <= 8 * 1024 * 1024 else 256, s)
    n_in = len(dys)
    extra_in, extra_out, extra_shape = _gate_bwd_specs(tm, d, b, s) if below else ([], [], [])

    def body(*refs):
        dy_refs, w_refs = refs[:n_in], refs[n_in:2 * n_in]
        x_ref, dxn_ref, nw_ref, sc_ref = refs[2 * n_in:2 * n_in + 4]
        rest = refs[2 * n_in + 4:]
        if below:
            o_ref, g_ref, dx_ref, dsc_ref, dsh_ref, dw_ref, do_ref, dg_ref = rest
        else:
            dx_ref, dsc_ref, dsh_ref, dw_ref = rest

        @pl.when(pl.program_id(1) == 0)
        def _():
            dsc_ref[...] = jnp.zeros_like(dsc_ref)
            dsh_ref[...] = jnp.zeros_like(dsh_ref)
            if below:
                dg_ref[...] = jnp.zeros_like(dg_ref)

        @pl.when(_first_step())
        def _():
            dw_ref[...] = jnp.zeros_like(dw_ref)

        dh = jnp.dot(dy_refs[0][...], w_refs[0][...], preferred_element_type=F32)
        for k in range(1, n_in):
            dh += jnp.dot(dy_refs[k][...], w_refs[k][...], preferred_element_type=F32)
        xv = x_ref[...]
        r = _rms(xv)
        n = xv * r
        nw = nw_ref[...]
        dsc_ref[...] += jnp.sum(dh * (n * nw), axis=0, keepdims=True)
        dsh_ref[...] += jnp.sum(dh, axis=0, keepdims=True)
        dhn = dh * (1.0 + sc_ref[...])
        dw_ref[...] += jnp.sum(dhn * n, axis=0, keepdims=True)
        dx = dxn_ref[...] + _rms_bwd(dhn * nw, n, r)
        dx_ref[...] = dx
        if below:
            _gate_bwd_tile(dx, below[2], o_ref, g_ref, do_ref, dg_ref)

    in_specs = [_row(tm, dy.shape[2]) for dy in dys] + [_full(wt.shape) for wt in wts]
    in_specs += [_row(tm, d), _row(tm, d), _full((1, d)), _bvec(d)] + extra_in
    return pl.pallas_call(
        body, name=name, grid=(b, s // tm), in_specs=in_specs,
        out_specs=[_row(tm, d), _bvec(d), _bvec(d), _full((1, d))] + extra_out,
        out_shape=[SDS((b, s, d), F32), SDS((b, 1, d), F32), SDS((b, 1, d), F32), SDS((1, d), F32)] + extra_shape,
        compiler_params=_cparams(2))(*dys, *wts, x, dxn, w, sc, *(below[:2] if below else ()))


def in_proj(h, win_t, name):
    b, s, d = h.shape
    tm = min(512, s)
    widths = (D_SSD, D_SSD + 2 * SSD_GROUPS * SSD_STATE, Q_LORA, KV_LORA, LANES)

    def body(h_ref, w_ref, *outs):
        p = lax.dot_general(h_ref[...], w_ref[...], NT_DIMS, preferred_element_type=F32)
        off = 0
        for o_ref, wd in zip(outs, widths):
            o_ref[...] = p[:, off:off + wd]
            off += wd

    return pl.pallas_call(
        body, name=name, grid=(b, s // tm),
        in_specs=[_row(tm, d), _full(win_t.shape)],
        out_specs=[_row(tm, wd) for wd in widths],
        out_shape=[SDS((b, s, wd), F32) for wd in widths], compiler_params=_cparams(2))(h, win_t)


def _halo_prev(ts, d):
    return pl.BlockSpec((None, 8, d), lambda b, i: (b, jnp.maximum(i * (ts // 8) - 1, 0), 0))


CONV_ROWS = 32


def _conv_head(head, u_ref, up_ref):
    head[0:8, :] = jnp.where(pl.program_id(1) > 0, up_ref[...], 0.0)
    head[8:8 + CONV_ROWS, :] = u_ref[0:CONV_ROWS, :]


def _conv_windows(u_ref, head, r0):
    if r0 == 0:
        return [head[5 + k:5 + k + CONV_ROWS, :] for k in range(4)]
    return [u_ref[r0 - 3 + k:r0 - 3 + k + CONV_ROWS, :] for k in range(4)]


def _fold8(t):
    acc = t[0:8, :]
    for r in range(8, CONV_ROWS, 8):
        acc += t[r:r + 8, :]
    return acc


def conv_fwd(u, cw, cb, name):
    b, s, dc = u.shape
    ts = min(512, s)
    widths = (D_SSD, SSD_GROUPS * SSD_STATE, SSD_GROUPS * SSD_STATE)

    def body(u_ref, up_ref, w_ref, b_ref, xs_ref, bm_ref, cm_ref, head):
        _conv_head(head, u_ref, up_ref)
        ws = [w_ref[k:k + 1, :] for k in range(4)]
        bias = b_ref[...]
        for r0 in range(0, ts, CONV_ROWS):
            taps = _conv_windows(u_ref, head, r0)
            v = bias + taps[0] * ws[0] + taps[1] * ws[1] + taps[2] * ws[2] + taps[3] * ws[3]
            y = v * _sigmoid(v)
            rs = slice(r0, r0 + CONV_ROWS)
            xs_ref[rs, :] = y[:, 0:D_SSD]
            bm_ref[rs, :] = y[:, D_SSD:D_SSD + 256]
            cm_ref[rs, :] = y[:, D_SSD + 256:D_SSD + 512]

    return pl.pallas_call(
        body, name=name, grid=(b, s // ts),
        in_specs=[_row(ts, dc), _halo_prev(ts, dc), _full((4, dc)), _full((1, dc))],
        out_specs=[_row(ts, wd) for wd in widths],
        out_shape=[SDS((b, s, wd), F32) for wd in widths],
        scratch_shapes=[pltpu.VMEM((8 + CONV_ROWS, dc), F32)], compiler_params=_cparams(2))(u, u, cw, cb)


def conv_bwd_a(dxs, dbm, dcm, u, cw, cb, name):
    b, s, dc = u.shape
    ts = min(512, s)

    def body(dxs_ref, dbm_ref, dcm_ref, u_ref, up_ref, w_ref, b_ref, dv_ref, dwb_ref, head):
        @pl.when(_first_step())
        def _():
            dwb_ref[...] = jnp.zeros_like(dwb_ref)
        _conv_head(head, u_ref, up_ref)
        ws = [w_ref[k:k + 1, :] for k in range(4)]
        bias = b_ref[...]
        for r0 in range(0, ts, CONV_ROWS):
            taps = _conv_windows(u_ref, head, r0)
            v = bias + taps[0] * ws[0] + taps[1] * ws[1] + taps[2] * ws[2] + taps[3] * ws[3]
            sg = _sigmoid(v)
            rs = slice(r0, r0 + CONV_ROWS)
            dy = jnp.concatenate([dxs_ref[rs, :], dbm_ref[rs, :], dcm_ref[rs, :]], axis=1)
            dv = dy * (sg * (1.0 + v * (1.0 - sg)))
            dv_ref[rs, :] = dv
            for k in range(4):
                dwb_ref[8 * k:8 * k + 8, :] += _fold8(dv * taps[k])
            dwb_ref[32:40, :] += _fold8(dv)

    return pl.pallas_call(
        body, name=name, grid=(b, s // ts),
        in_specs=[_row(ts, D_SSD), _row(ts, 256), _row(ts, 256), _row(ts, dc), _halo_prev(ts, dc),
                  _full((4, dc)), _full((1, dc))],
        out_specs=[_row(ts, dc), _full((40, dc))],
        out_shape=[SDS((b, s, dc), F32), SDS((40, dc), F32)],
        scratch_shapes=[pltpu.VMEM((8 + CONV_ROWS, dc), F32)], compiler_params=_cparams(2))(dxs, dbm, dcm, u, u, cw, cb)


def conv_grads_fold(x, name):
    c = x.shape[1]

    def body(x_ref, o_ref):
        o_ref[...] = jnp.zeros_like(o_ref)
        for k in range(5):
            o_ref[k:k + 1, :] = jnp.sum(x_ref[8 * k:8 * k + 8, :], axis=0, keepdims=True)

    return pl.pallas_call(body, name=name, out_shape=SDS((8, c), F32))(x)


def conv_bwd_b(dv, cw, name):
    b, s, dc = dv.shape
    ts = min(512, s)
    nt = s // ts

    def body(dv_ref, dn_ref, w_ref, du_ref, tail):
        tail[0:CONV_ROWS, :] = dv_ref[ts - CONV_ROWS:ts, :]
        tail[CONV_ROWS:CONV_ROWS + 8, :] = jnp.where(pl.program_id(1) < nt - 1, dn_ref[...], 0.0)
        ws = [w_ref[k:k + 1, :] for k in range(4)]
        for r0 in range(0, ts, CONV_ROWS):
            if r0 == ts - CONV_ROWS:
                win = [tail[3 - k:3 - k + CONV_ROWS, :] for k in range(4)]
            else:
                win = [dv_ref[r0 + 3 - k:r0 + 3 - k + CONV_ROWS, :] for k in range(4)]
            acc = win[0] * ws[0] + win[1] * ws[1] + win[2] * ws[2] + win[3] * ws[3]
            du_ref[r0:r0 + CONV_ROWS, :] = acc.astype(BF16)

    nxt = pl.BlockSpec((None, 8, dc), lambda bb, i: (bb, jnp.minimum((i + 1) * (ts // 8), s // 8 - 1), 0))
    return pl.pallas_call(
        body, name=name, grid=(b, nt),
        in_specs=[_row(ts, dc), nxt, _full((4, dc))],
        out_specs=_row(ts, dc), out_shape=SDS((b, s, dc), BF16),
        scratch_shapes=[pltpu.VMEM((CONV_ROWS + 8, dc), F32)], compiler_params=_cparams(2))(dv, dv, cw)


def _ssd_common(misc_ref, dtb_ref, alog_ref, e_ref):
    ln = CHUNK
    lane = lax.broadcasted_iota(I32, (ln, LANES), 1)
    lane1 = lax.broadcasted_iota(I32, (1, LANES), 1)
    pre = misc_ref[...] + dtb_ref[...]
    dt_s = jnp.where(lane < SSD_HEADS, _softplus(pre), 0.0)
    a_neg = jnp.where(lane1 < SSD_HEADS, -jnp.exp(alog_ref[...]), 0.0)
    ri = lax.broadcasted_iota(I32, (ln, ln), 0)
    ci = lax.broadcasted_iota(I32, (ln, ln), 1)
    tril = ci <= ri
    acum = jnp.dot(tril.astype(F32), dt_s * a_neg, preferred_element_type=F32, precision=HI)
    both_e = _dot_01(jnp.concatenate([dt_s, acum], axis=0), e_ref[...], 3)
    dt_e, acum_e = both_e[0:ln], both_e[ln:2 * ln]
    return dict(pre=pre, dt_s=dt_s, a_neg=a_neg, tril=tril, ri=ri, ci=ci, acum=acum, acum_t=acum.T,
                dt_e=dt_e, eac_e=jnp.exp(acum_e), del_e=jnp.exp(acum_e[ln - 1:ln, :] - acum_e))


def _dot_01(x, m01, terms):
    acc, rest = None, x
    for k in range(terms):
        part = rest.astype(BF16)
        if k + 1 < terms:
            rest = rest - part.astype(F32)
        d = jnp.dot(part, m01, preferred_element_type=F32)
        acc = d if acc is None else acc + d
    return acc


def _decay(cm, h):
    seg = cm["acum"][:, h:h + 1] - cm["acum_t"][h:h + 1, :]
    return jnp.exp(jnp.where(cm["tril"], seg, -jnp.inf))


def ssd_fwd(xs, bm, cm_, misc, z, dtb, alog, dskip_e, norm_w, e_mat, name):
    b, s, _ = xs.shape
    ln, nc = CHUNK, s // CHUNK
    gw = D_SSD // SSD_GROUPS
    hpg = SSD_HEADS // SSD_GROUPS

    def body(xs_ref, b_ref, c_ref, misc_ref, z_ref, dtb_ref, alog_ref, dsk_ref, nw_ref, e_ref,
             ys_ref, y_ref, p_ref, st, yd):
        @pl.when(pl.program_id(1) == 0)
        def _():
            st[...] = jnp.zeros_like(st)
        cm = _ssd_common(misc_ref, dtb_ref, alog_ref, e_ref)
        xsv = xs_ref[...]
        xdt = xsv * cm["dt_e"]
        xdt_b = xdt.astype(BF16)
        xd_b = (xdt * cm["del_e"]).astype(BF16)
        gam_e = cm["eac_e"][ln - 1:ln, :]
        p_ref[...] = st[...]
        groups = [slice(gw * g, gw * (g + 1)) for g in range(SSD_GROUPS)]
        heads = [slice(SSD_HEAD_DIM * h, SSD_HEAD_DIM * (h + 1)) for h in range(SSD_HEADS)]
        bgs = [b_ref[:, SSD_STATE * g:SSD_STATE * (g + 1)].astype(BF16) for g in range(SSD_GROUPS)]
        cgs = [c_ref[:, SSD_STATE * g:SSD_STATE * (g + 1)].astype(BF16) for g in range(SSD_GROUPS)]
        cbs = [lax.dot_general(cg, bg, NT_DIMS, preferred_element_type=F32) for cg, bg in zip(cgs, bgs)]
        sts = [st[:, gs] for gs in groups]
        yoff = [jnp.dot(cg, st_g.astype(BF16), preferred_element_type=F32) * cm["eac_e"][:, gs]
                for cg, st_g, gs in zip(cgs, sts, groups)]
        news = [lax.dot_general(bg, xd_b[:, gs], TN_DIMS, preferred_element_type=F32) for bg, gs in zip(bgs, groups)]
        for gs, st_g, new in zip(groups, sts, news):
            st[:, gs] = st_g * gam_e[:, gs] + new
        ms = [(cbs[h // hpg] * _decay(cm, h)).astype(BF16) for h in range(SSD_HEADS)]
        for h, hs in enumerate(heads):
            yd[:, hs] = jnp.dot(ms[h], xdt_b[:, hs], preferred_element_type=F32)
        y = yd[...] + jnp.concatenate(yoff, axis=1) + dsk_ref[...] * xsv
        y_ref[...] = y
        zz = z_ref[...]
        yg = y * (zz * _sigmoid(zz))
        outs = []
        for g in range(SSD_GROUPS):
            ygg = yg[:, gw * g:gw * (g + 1)]
            outs.append(ygg * _rms(ygg) * nw_ref[:, gw * g:gw * (g + 1)])
        ys_ref[...] = jnp.concatenate(outs, axis=1).astype(BF16)

    row = lambda d: pl.BlockSpec((None, ln, d), lambda bb, c: (bb, c, 0))
    return pl.pallas_call(
        body, name=name, grid=(b, nc),
        in_specs=[row(D_SSD), row(256), row(256), row(LANES), row(D_SSD), _full((1, LANES)), _full((1, LANES)),
                  _full((1, D_SSD)), _full((1, D_SSD)), _full((LANES, D_SSD))],
        out_specs=[row(D_SSD), row(D_SSD), pl.BlockSpec((None, None, SSD_STATE, D_SSD), lambda bb, c: (bb, c, 0, 0))],
        out_shape=[SDS((b, s, D_SSD), BF16), SDS((b, s, D_SSD), F32), SDS((b, nc, SSD_STATE, D_SSD), F32)],
        scratch_shapes=[pltpu.VMEM((SSD_STATE, D_SSD), F32), pltpu.VMEM((ln, D_SSD), F32)],
        compiler_params=_cparams(2))(xs, bm, cm_, misc, z, dtb, alog, dskip_e, norm_w, e_mat)


def ssd_bwd(dys, y, z, xs, bm, cm_, misc, prev, dtb, alog, dskip_e, norm_w, e_mat, et_mat, name):
    b, s, _ = xs.shape
    ln, nc = CHUNK, s // CHUNK
    gw = D_SSD // SSD_GROUPS
    hpg = SSD_HEADS // SSD_GROUPS

    def body(dys_ref, y_ref, z_ref, xs_ref, b_ref, c_ref, misc_ref, p_ref, dtb_ref, alog_ref, dsk_ref, nw_ref,
             e_ref, et_ref, dxs_ref, db_ref, dc_ref, dz_ref, ddt_ref, dnw_ref, ddsk_ref, ddtb_ref, dalog_ref,
             dst, dxd, dac_t):
        @pl.when(_first_step())
        def _():
            for r_ in (dnw_ref, ddsk_ref, ddtb_ref, dalog_ref):
                r_[...] = jnp.zeros_like(r_)

        @pl.when(pl.program_id(1) == 0)
        def _():
            dst[...] = jnp.zeros_like(dst)

        cm = _ssd_common(misc_ref, dtb_ref, alog_ref, e_ref)
        et = et_ref[...]
        squeeze = lambda t: _dot_01(t, et, 2)
        lane = lax.broadcasted_iota(I32, (ln, LANES), 1)
        sub = lax.broadcasted_iota(I32, (LANES, ln), 0)
        xsv = xs_ref[...]
        xdt = xsv * cm["dt_e"]
        xdt_b = xdt.astype(BF16)
        xd_b = (xdt * cm["del_e"]).astype(BF16)
        eac_e = cm["eac_e"]
        gam_e = eac_e[ln - 1:ln, :]

        yv, zz, dyo = y_ref[...], z_ref[...], dys_ref[...]
        sz = _sigmoid(zz)
        silu_z = zz * sz
        yg = yv * silu_z
        dyg, dnw = [], []
        for g in range(SSD_GROUPS):
            gs = slice(gw * g, gw * (g + 1))
            ygg = yg[:, gs]
            r = _rms(ygg)
            n = ygg * r
            dnw.append(jnp.sum(dyo[:, gs] * n, axis=0, keepdims=True))
            dyg.append(_rms_bwd(dyo[:, gs] * nw_ref[:, gs], n, r))
        dyg = jnp.concatenate(dyg, axis=1)
        dnw_ref[...] += jnp.concatenate(dnw, axis=1)
        dz_ref[...] = (dyg * yv * (sz * (1.0 + zz * (1.0 - sz)))).astype(BF16)
        dy = dyg * silu_z
        ddsk_ref[...] += jnp.sum(dy * xsv, axis=0, keepdims=True)
        dy_b = dy.astype(BF16)

        dacum = jnp.zeros((ln, LANES), F32)
        dac_t[...] = jnp.zeros_like(dac_t)
        w1, dgam = [], []
        for g in range(SSD_GROUPS):
            gs = slice(gw * g, gw * (g + 1))
            ss = slice(SSD_STATE * g, SSD_STATE * (g + 1))
            bg = b_ref[:, ss].astype(BF16)
            cg = c_ref[:, ss].astype(BF16)
            cb = lax.dot_general(cg, bg, NT_DIMS, preferred_element_type=F32)
            pt = p_ref[:, gs]
            pt_b = pt.astype(BF16)
            dst_g = dst[:, gs]
            dst_b = dst_g.astype(BF16)
            edy = (dy[:, gs] * eac_e[:, gs]).astype(BF16)
            dcg = lax.dot_general(edy, pt_b, NT_DIMS, preferred_element_type=F32)
            dpt = lax.dot_general(cg, edy, TN_DIMS, preferred_element_type=F32)
            yoff = jnp.dot(cg, pt_b, preferred_element_type=F32) * eac_e[:, gs]
            dxd_g = jnp.dot(bg, dst_b, preferred_element_type=F32)
            dbg = lax.dot_general(xd_b[:, gs], dst_b, NT_DIMS, preferred_element_type=F32)
            ddel = dxd_g * xdt[:, gs] * cm["del_e"][:, gs]
            w1.append(dy[:, gs] * yoff - ddel)
            dgam.append(jnp.sum(ddel, axis=0, keepdims=True) + jnp.sum(dst_g * pt, axis=0, keepdims=True) * gam_e[:, gs])
            dxd[:, gs] = dxd_g * cm["del_e"][:, gs]
            dst[:, gs] = dst_g * gam_e[:, gs] + dpt
            dcb = jnp.zeros((ln, ln), F32)
            for j in range(hpg):
                h = hpg * g + j
                hs = slice(SSD_HEAD_DIM * h, SSD_HEAD_DIM * (h + 1))
                lam = _decay(cm, h)
                m = cb * lam
                dm = lax.dot_general(dy_b[:, hs], xdt_b[:, hs], NT_DIMS, preferred_element_type=F32)
                dxd[:, hs] += lax.dot_general(m.astype(BF16), dy_b[:, hs], TN_DIMS, preferred_element_type=F32)
                dcb += dm * lam
                wl = dm * m
                dacum += jnp.where(lane == h, jnp.sum(wl, axis=1, keepdims=True), 0.0)
                dac_t[...] -= jnp.where(sub == h, jnp.sum(wl, axis=0, keepdims=True), 0.0)
            dcb_b = dcb.astype(BF16)
            dc_ref[:, ss] = dcg + jnp.dot(dcb_b, bg, preferred_element_type=F32)
            db_ref[:, ss] = dbg + lax.dot_general(dcb_b, cg, TN_DIMS, preferred_element_type=F32)

        dxdt = dxd[...]
        dxs_ref[...] = dy * dsk_ref[...] + dxdt * cm["dt_e"]
        dacum += squeeze(jnp.concatenate(w1, axis=1)) + dac_t[...].T
        dlast = squeeze(jnp.broadcast_to(jnp.concatenate(dgam, axis=1), (8, D_SSD)))[0:1, :]
        dacum += jnp.where(lax.broadcasted_iota(I32, (ln, LANES), 0) == ln - 1, dlast, 0.0)
        triu = (cm["ci"] >= cm["ri"]).astype(F32)
        da = jnp.dot(triu, dacum, preferred_element_type=F32, precision=HI)
        ddt = da * cm["a_neg"] + squeeze(dxdt * xsv)
        dalog_ref[...] += jnp.sum(da * cm["dt_s"], axis=0, keepdims=True) * cm["a_neg"]
        ddt_raw = jnp.where(lane < SSD_HEADS, ddt * _sigmoid(cm["pre"]), 0.0)
        ddt_ref[...] = ddt_raw
        ddtb_ref[...] += jnp.sum(ddt_raw, axis=0, keepdims=True)

    row = lambda d: pl.BlockSpec((None, ln, d), lambda bb, c: (bb, nc - 1 - c, 0))
    return pl.pallas_call(
        body, name=name, grid=(b, nc),
        in_specs=[row(D_SSD), row(D_SSD), row(D_SSD), row(D_SSD), row(256), row(256), row(LANES),
                  pl.BlockSpec((None, None, SSD_STATE, D_SSD), lambda bb, c: (bb, nc - 1 - c, 0, 0)),
                  _full((1, LANES)), _full((1, LANES)), _full((1, D_SSD)), _full((1, D_SSD)),
                  _full((LANES, D_SSD)), _full((D_SSD, LANES))],
        out_specs=[row(D_SSD), row(256), row(256), row(D_SSD), row(LANES),
                   _full((1, D_SSD)), _full((1, D_SSD)), _full((1, LANES)), _full((1, LANES))],
        out_shape=[SDS((b, s, D_SSD), F32), SDS((b, s, 256), F32), SDS((b, s, 256), F32), SDS((b, s, D_SSD), BF16),
                   SDS((b, s, LANES), F32), SDS((1, D_SSD), F32), SDS((1, D_SSD), F32), SDS((1, LANES), F32),
                   SDS((1, LANES), F32)],
        scratch_shapes=[pltpu.VMEM((SSD_STATE, D_SSD), F32), pltpu.VMEM((ln, D_SSD), F32), pltpu.VMEM((LANES, ln), F32)],
        compiler_params=_cparams(2))(dys, y, z, xs, bm, cm_, misc, prev, dtb, alog, dskip_e, norm_w, e_mat, et_mat)


def _rope(xv, cc, sp, sm):
    n = xv.shape[1]
    return xv * cc + pltpu.roll(xv, 16, 1) * sp + pltpu.roll(xv, n - 16, 1) * sm


def _rope_bwd(dy, cc, sp, sm):
    n = dy.shape[1]
    return dy * cc + pltpu.roll(dy * sp, n - 16, 1) + pltpu.roll(dy * sm, 16, 1)


def _tile8(t):
    return jnp.concatenate([t] * MLA_HEADS, axis=1)


def qkv_fwd(cq, ckv, misc, cc, sp, sm, qnw, kvnw, wuq_t, wukv_t, place, name):
    b, s, _ = cq.shape
    tm = min(512, s)
    hd = MLA_HEADS * HEAD_PAD

    def body(cq_ref, ckv_ref, misc_ref, cc_ref, sp_ref, sm_ref, qnw_ref, kvnw_ref, wq_ref, wkv_ref, pl_ref,
             q_ref, k_ref, v_ref, qn_ref, kvn_ref):
        cqv, ckvv = cq_ref[...], ckv_ref[...]
        qn = (cqv * _rms(cqv) * qnw_ref[...]).astype(BF16)
        kvn = (ckvv * _rms(ckvv) * kvnw_ref[...]).astype(BF16)
        qn_ref[...] = qn
        kvn_ref[...] = kvn
        cc1, sp1, sm1 = cc_ref[...], sp_ref[...], sm_ref[...]
        q = lax.dot_general(qn, wq_ref[...], NT_DIMS, preferred_element_type=F32)
        q_ref[...] = _rope(q, _tile8(cc1), _tile8(sp1), _tile8(sm1)).astype(BF16)
        kv = lax.dot_general(kvn, wkv_ref[...], NT_DIMS, preferred_element_type=F32)
        kr = jnp.dot(misc_ref[...], pl_ref[...], preferred_element_type=F32, precision=HI)
        kr = _rope(kr, cc1, sp1, sm1)
        k_ref[...] = (kv[:, 0:hd] + _tile8(kr)).astype(BF16)
        v_ref[...] = kv[:, hd:2 * hd].astype(BF16)

    return pl.pallas_call(
        body, name=name, grid=(b, s // tm),
        in_specs=[_row(tm, Q_LORA), _row(tm, KV_LORA), _row(tm, LANES), _row(tm, LANES), _row(tm, LANES), _row(tm, LANES),
                  _full((1, Q_LORA)), _full((1, KV_LORA)), _full(wuq_t.shape), _full(wukv_t.shape), _full((LANES, LANES))],
        out_specs=[_row(tm, hd), _row(tm, hd), _row(tm, hd), _row(tm, Q_LORA), _row(tm, KV_LORA)],
        out_shape=[SDS((b, s, hd), BF16)] * 3 + [SDS((b, s, Q_LORA), BF16), SDS((b, s, KV_LORA), BF16)],
        compiler_params=_cparams(2))(cq, ckv, misc, cc, sp, sm, qnw, kvnw, wuq_t, wukv_t, place)


def qkv_bwd(dq, dk, dv, ddt, cq, ckv, cc, sp, sm, qnw, kvnw, wuq_t, wukv_t, place_t, name):
    b, s, _ = cq.shape
    tm = min(512, s)
    hd = MLA_HEADS * HEAD_PAD

    def body(dq_ref, dk_ref, dv_ref, ddt_ref, cq_ref, ckv_ref, cc_ref, sp_ref, sm_ref, qnw_ref, kvnw_ref,
             wq_ref, wkv_ref, plt_ref, dcq_ref, dckv_ref, dmisc_ref, dqp_ref, dkv_ref, dqnw_ref, dkvnw_ref):
        @pl.when(_first_step())
        def _():
            dqnw_ref[...] = jnp.zeros_like(dqnw_ref)
            dkvnw_ref[...] = jnp.zeros_like(dkvnw_ref)
        cc1, sp1, sm1 = cc_ref[...], sp_ref[...], sm_ref[...]
        dqp = _rope_bwd(dq_ref[...].astype(F32), _tile8(cc1), _tile8(sp1), _tile8(sm1)).astype(BF16)
        dqp_ref[...] = dqp
        dkv_b = jnp.concatenate([dk_ref[...], dv_ref[...]], axis=1)
        dkf = dk_ref[...].astype(F32)
        dkv_ref[...] = dkv_b
        dkr = dkf[:, 0:HEAD_PAD]
        for h in range(1, MLA_HEADS):
            dkr += dkf[:, HEAD_PAD * h:HEAD_PAD * (h + 1)]
        dkr = _rope_bwd(dkr, cc1, sp1, sm1)
        dmisc_ref[...] = (jnp.dot(dkr, plt_ref[...], preferred_element_type=F32, precision=HI) + ddt_ref[...]).astype(BF16)

        def norm_bwd(dn_w, xv, w_ref, dw_ref, dx_ref):
            r = _rms(xv)
            n = xv * r
            dw_ref[...] += jnp.sum(dn_w * n, axis=0, keepdims=True)
            dx_ref[...] = _rms_bwd(dn_w * w_ref[...], n, r).astype(BF16)

        norm_bwd(jnp.dot(dqp, wq_ref[...], preferred_element_type=F32), cq_ref[...], qnw_ref, dqnw_ref, dcq_ref)
        norm_bwd(jnp.dot(dkv_b, wkv_ref[...], preferred_element_type=F32), ckv_ref[...], kvnw_ref, dkvnw_ref, dckv_ref)

    return pl.pallas_call(
        body, name=name, grid=(b, s // tm),
        in_specs=[_row(tm, hd), _row(tm, hd), _row(tm, hd), _row(tm, LANES), _row(tm, Q_LORA), _row(tm, KV_LORA),
                  _row(tm, LANES), _row(tm, LANES), _row(tm, LANES), _full((1, Q_LORA)), _full((1, KV_LORA)),
                  _full(wuq_t.shape), _full(wukv_t.shape), _full((LANES, LANES))],
        out_specs=[_row(tm, Q_LORA), _row(tm, KV_LORA), _row(tm, LANES), _row(tm, hd), _row(tm, 2 * hd),
                   _full((1, Q_LORA)), _full((1, KV_LORA))],
        out_shape=[SDS((b, s, Q_LORA), BF16), SDS((b, s, KV_LORA), BF16), SDS((b, s, LANES), BF16),
                   SDS((b, s, hd), BF16), SDS((b, s, 2 * hd), BF16), SDS((1, Q_LORA), F32), SDS((1, KV_LORA), F32)],
        compiler_params=_cparams(2))(dq, dk, dv, ddt, cq, ckv, cc, sp, sm, qnw, kvnw, wuq_t, wukv_t, place_t)


ATT_SCALE = 1.0 / math.sqrt(QK_DIM)
LOG2E = math.log2(math.e)
ATT_SCALE_LOG2E = ATT_SCALE * LOG2E


ATT_HEADS_PER_STEP = 4
ATT_HEADS_PER_STEP_BWD = 2


def _att_tile(s):
    return min(512, s)


def flash_fwd(q, k, v, name):
    b, s, hd = q.shape
    t = _att_tile(s)
    nb = s // t
    th = t // 2
    vt = v.reshape(b, nb, t, MLA_HEADS, HEAD_PAD).transpose(0, 3, 1, 4, 2)

    hps = ATT_HEADS_PER_STEP
    hw = hps * HEAD_PAD

    def body(q_ref, k_ref, vt_ref, o_ref, lse_ref, m_s, l_s, acc):
        i = pl.program_id(2)
        m_s[...] = jnp.full_like(m_s, -jnp.inf)
        l_s[...] = jnp.zeros_like(l_s)
        acc[...] = jnp.zeros_like(acc)

        def update(j, diagonal):
            ks = pl.ds(pl.multiple_of(j * t, t), t)
            chains = [(hh, half) for hh in range(hps) for half in range(2)]
            lanes = lambda hh: slice(HEAD_PAD * hh, HEAD_PAD * (hh + 1))
            cols = lambda half: slice(th * half, th * (half + 1))
            sts = {}
            for hh, half in chains:
                st = lax.dot_general(k_ref[ks, lanes(hh)], q_ref[cols(half), lanes(hh)], NT_DIMS,
                                     preferred_element_type=F32)
                if diagonal:
                    row = lax.broadcasted_iota(I32, (t, th), 0)
                    col = lax.broadcasted_iota(I32, (t, th), 1) + th * half
                    st = jnp.where(row <= col, st, -jnp.inf)
                sts[hh, half] = st
            pts, alphas = {}, {}
            for hh, half in chains:
                st, cs = sts[hh, half], cols(half)
                m_prev = m_s[hh, :, cs]
                m_new = jnp.maximum(m_prev, jnp.max(st, axis=0, keepdims=True))
                alpha = jnp.exp2((m_prev - m_new) * ATT_SCALE_LOG2E)
                pt = jnp.exp2((st - m_new) * ATT_SCALE_LOG2E)
                l_s[hh, :, cs] = alpha * l_s[hh, :, cs] + jnp.sum(pt, axis=0, keepdims=True)
                m_s[hh, :, cs] = m_new
                pts[hh, half], alphas[hh, half] = pt.astype(BF16), alpha
            for hh, half in chains:
                cs = cols(half)
                acc[hh, :, cs] = alphas[hh, half] * acc[hh, :, cs] + jnp.dot(vt_ref[hh, j], pts[hh, half],
                                                                             preferred_element_type=F32)

        def step(j, carry):
            update(j, False)
            return carry

        lax.fori_loop(0, i, step, 0)
        update(i, True)
        for hh in range(hps):
            o_ref[:, HEAD_PAD * hh:HEAD_PAD * (hh + 1)] = (acc[hh] / l_s[hh]).T
            lse_ref[hh] = m_s[hh] * ATT_SCALE + jnp.log(l_s[hh])

    qs = pl.BlockSpec((None, t, hw), lambda bb, h, i: (bb, i, h))
    ks = pl.BlockSpec((None, s, hw), lambda bb, h, i: (bb, 0, h))
    vs = pl.BlockSpec((None, hps, nb, HEAD_PAD, t), lambda bb, h, i: (bb, h, 0, 0, 0))
    ls = pl.BlockSpec((None, hps, None, 1, t), lambda bb, h, i: (bb, h, i, 0, 0))
    return pl.pallas_call(
        body, name=name, grid=(b, MLA_HEADS // hps, nb),
        in_specs=[qs, ks, vs], out_specs=[qs, ls],
        out_shape=[SDS((b, s, hd), F32), SDS((b, MLA_HEADS, nb, 1, t), F32)],
        scratch_shapes=[pltpu.VMEM((hps, 1, t), F32), pltpu.VMEM((hps, 1, t), F32), pltpu.VMEM((hps, HEAD_PAD, t), F32)],
        compiler_params=_cparams(3))(q, k, vt)


def flash_bwd(q, k, v, do, lse, dlt, name):
    b, s, hd = q.shape
    t = _att_tile(s)
    nb = s // t
    th = t // 2
    lse_r = lse
    dlt_r = dlt.reshape(b, MLA_HEADS, nb, 1, t)

    hps = ATT_HEADS_PER_STEP_BWD
    hw = hps * HEAD_PAD

    def body(q_ref, k_ref, v_ref, do_ref, lse_ref, dlt_ref, dq_ref, dk_ref, dv_ref, dq_s, dk_s, dv_s):
        dq_s[...] = jnp.zeros_like(dq_s)
        dk_s[...] = jnp.zeros_like(dk_s)
        dv_s[...] = jnp.zeros_like(dv_s)

        def tile(j, i, diagonal):
            qs = pl.ds(pl.multiple_of(i * t, t), t)
            chains = [(hh, half) for hh in range(hps) for half in range(2)]
            lanes = lambda hh: slice(HEAD_PAD * hh, HEAD_PAD * (hh + 1))
            keys = lambda half: pl.ds(pl.multiple_of(j * t + th * half, th), th)
            sts, dpts = {}, {}
            for hh, half in chains:
                ls_, ks = lanes(hh), keys(half)
                st = lax.dot_general(k_ref[ks, ls_], q_ref[qs, ls_], NT_DIMS, preferred_element_type=F32)
                if diagonal:
                    row = lax.broadcasted_iota(I32, (th, t), 0) + th * half
                    col = lax.broadcasted_iota(I32, (th, t), 1)
                    st = jnp.where(row <= col, st, -jnp.inf)
                sts[hh, half] = st
                dpts[hh, half] = lax.dot_general(v_ref[ks, ls_], do_ref[qs, ls_], NT_DIMS, preferred_element_type=F32)
            pts, dsts = {}, {}
            for hh, half in chains:
                pt = jnp.exp2(sts[hh, half] * ATT_SCALE_LOG2E - lse_ref[hh, i] * LOG2E)
                pts[hh, half] = pt.astype(BF16)
                dsts[hh, half] = (pt * (dpts[hh, half] - dlt_ref[hh, i])).astype(BF16)
            for hh in range(hps):
                ls_ = lanes(hh)
                dq_acc = None
                for half in range(2):
                    ks = keys(half)
                    dv_s[ks, ls_] += jnp.dot(pts[hh, half], do_ref[qs, ls_], preferred_element_type=F32)
                    dk_s[ks, ls_] += jnp.dot(dsts[hh, half], q_ref[qs, ls_], preferred_element_type=F32)
                    part = lax.dot_general(dsts[hh, half], k_ref[ks, ls_], TN_DIMS, preferred_element_type=F32)
                    dq_acc = part if dq_acc is None else dq_acc + part
                dq_s[qs, ls_] += dq_acc

        def key_tile(j, carry):
            tile(j, j, True)

            def query_tile(i, c2):
                tile(j, i, False)
                return c2

            lax.fori_loop(j + 1, nb, query_tile, 0)
            return carry

        lax.fori_loop(0, nb, key_tile, 0)
        dq_ref[...] = (dq_s[...] * ATT_SCALE).astype(BF16)
        dk_ref[...] = (dk_s[...] * ATT_SCALE).astype(BF16)
        dv_ref[...] = dv_s[...].astype(BF16)

    hs = pl.BlockSpec((None, s, hw), lambda bb, h: (bb, 0, h))
    ls = pl.BlockSpec((None, hps, nb, 1, t), lambda bb, h: (bb, h, 0, 0, 0))
    return pl.pallas_call(
        body, name=name, grid=(b, MLA_HEADS // hps),
        in_specs=[hs, hs, hs, hs, ls, ls], out_specs=[hs, hs, hs],
        out_shape=[SDS((b, s, hd), BF16)] * 3, scratch_shapes=[pltpu.VMEM((s, hw), F32)] * 3,
        compiler_params=_cparams(2))(q, k, v, do, lse_r, dlt_r)


def out_proj(ys, attn, mnw, wo, x, gate, above, name):
    b, s, d = x.shape
    tm = min(512, s)

    def body(ys_ref, at_ref, mnw_ref, wo_ref, x_ref, g_ref, nw_ref, sc_ref, sh_ref, xn_ref, o_ref, ym_ref, h_ref):
        av = at_ref[...]
        ym = (av * _rms(av) * mnw_ref[...]).astype(BF16)
        ym_ref[...] = ym
        o = jnp.dot(ys_ref[...], wo_ref[0:D_SSD, :], preferred_element_type=F32)
        o += jnp.dot(ym, wo_ref[D_SSD:2 * D_SSD, :], preferred_element_type=F32)
        xn = x_ref[...] + g_ref[...] * o
        xn_ref[...] = xn
        o_ref[...] = o.astype(BF16)
        h_ref[...] = _norm_mod_tile(xn, nw_ref, sc_ref, sh_ref)

    return pl.pallas_call(
        body, name=name, grid=(b, s // tm),
        in_specs=[_row(tm, D_SSD), _row(tm, D_SSD), _full((1, D_SSD)), _full(wo.shape), _row(tm, d), _bvec(d),
                  _full((1, d)), _bvec(d), _bvec(d)],
        out_specs=[_row(tm, d), _row(tm, d), _row(tm, D_SSD), _row(tm, d)],
        out_shape=[SDS((b, s, d), F32), SDS((b, s, d), BF16), SDS((b, s, D_SSD), BF16), SDS((b, s, d), BF16)],
        compiler_params=_cparams(2))(ys, attn, mnw, wo, x, gate, *above)


def out_proj_bwd(dout, attn, mnw, wo, name):
    b, s, d = dout.shape
    tm = min(512, s)

    def body(do_ref, at_ref, mnw_ref, wo_ref, dys_ref, dat_ref, dlt_ref, dw_ref):
        lane = lax.broadcasted_iota(I32, (tm, LANES), 1)
        @pl.when(_first_step())
        def _():
            dw_ref[...] = jnp.zeros_like(dw_ref)
        dov = do_ref[...]
        dys_ref[...] = lax.dot_general(dov, wo_ref[0:D_SSD, :], NT_DIMS, preferred_element_type=F32)
        dym = lax.dot_general(dov, wo_ref[D_SSD:2 * D_SSD, :], NT_DIMS, preferred_element_type=F32)
        av = at_ref[...]
        r = _rms(av)
        n = av * r
        dw_ref[...] += jnp.sum(dym * n, axis=0, keepdims=True)
        dat = _rms_bwd(dym * mnw_ref[...], n, r)
        dat_ref[...] = dat.astype(BF16)
        prod = dat * av
        cols = jnp.zeros((tm, LANES), F32)
        for h in range(MLA_HEADS):
            cols += jnp.where(lane == h, jnp.sum(prod[:, HEAD_PAD * h:HEAD_PAD * (h + 1)], axis=1, keepdims=True), 0.0)
        dlt_ref[...] = cols.T[0:MLA_HEADS, :]

    return pl.pallas_call(
        body, name=name, grid=(b, s // tm),
        in_specs=[_row(tm, d), _row(tm, D_SSD), _full((1, D_SSD)), _full(wo.shape)],
        out_specs=[_row(tm, D_SSD), _row(tm, D_SSD),
                   pl.BlockSpec((None, MLA_HEADS, tm), lambda bb, i: (bb, 0, i)), _full((1, D_SSD))],
        out_shape=[SDS((b, s, D_SSD), F32), SDS((b, s, D_SSD), BF16), SDS((b, MLA_HEADS, s), F32),
                   SDS((1, D_SSD), F32)],
        compiler_params=_cparams(2))(dout, attn, mnw, wo)


def adaln_fwd(c_all, w_ada, b_ada, name):
    nb, d = c_all.shape
    n = w_ada.shape[1]

    def body(c_ref, w_ref, b_ref, m_ref, ca_ref):
        cv = c_ref[...]
        ca = (cv * _sigmoid(cv)).astype(BF16)
        ca_ref[...] = ca
        m_ref[...] = jnp.dot(ca, w_ref[...].astype(BF16), preferred_element_type=F32) + b_ref[...]

    return pl.pallas_call(
        body, name=name, out_shape=[SDS((nb, n), F32), SDS((nb, d), BF16)],
        compiler_params=pltpu.CompilerParams(vmem_limit_bytes=VMEM_LIMIT))(c_all, w_ada, b_ada)


def adaln_bwd(c_act, dmod_cols, name):
    d, n = c_act.shape[1], dmod_cols.shape[1]

    def body(c_ref, dm_ref, gw_ref):
        gw_ref[...] = lax.dot_general(c_ref[...], dm_ref[...].astype(BF16), TN_DIMS, preferred_element_type=F32)

    return pl.pallas_call(
        body, name=name, out_shape=SDS((d, n), F32),
        compiler_params=pltpu.CompilerParams(vmem_limit_bytes=VMEM_LIMIT))(c_act, dmod_cols)


def sum_rows(x, name):
    def body(x_ref, o_ref):
        o_ref[...] = jnp.sum(x_ref[...], axis=0, keepdims=True)
    return pl.pallas_call(body, name=name, out_shape=SDS((1, x.shape[1]), F32))(x)


def squeeze_heads(x, et_mat, name):
    def body(x_ref, et_ref, o_ref):
        xv = jnp.broadcast_to(x_ref[...], (8, x.shape[1]))
        o_ref[...] = _dot_01(xv, et_ref[...], 3)[0:1, :]
    return pl.pallas_call(body, name=name, out_shape=SDS((1, LANES), F32))(x, et_mat)


def sum_blocks(x, name):
    n, r, c = x.shape
    tr = next(cand for cand in (256, 128, 64, 32, 16, 8) if r % cand == 0)

    def body(x_ref, o_ref):
        acc = x_ref[0].astype(F32)
        for k in range(1, n):
            acc += x_ref[k].astype(F32)
        o_ref[...] = acc

    return pl.pallas_call(
        body, name=name, grid=(r // tr,), in_specs=[pl.BlockSpec((n, tr, c), lambda i: (0, i, 0))],
        out_specs=pl.BlockSpec((tr, c), lambda i: (i, 0)), out_shape=SDS((r, c), F32),
        compiler_params=_cparams(1))(x)


def _adam_math(w, g, m, v):
    m = ADAM_B1 * m + (1.0 - ADAM_B1) * g
    v = ADAM_B2 * v + (1.0 - ADAM_B2) * (g * g)
    m_hat = m / (1.0 - ADAM_B1 ** ADAM_STEP)
    v_hat = v / (1.0 - ADAM_B2 ** ADAM_STEP)
    return -ADAM_LR * (m_hat / (jnp.sqrt(v_hat) + ADAM_EPS) + ADAM_WD * w), m, v


def adamw(w, g, m, v, name):
    r, c = w.shape[-2:]
    tr = r
    for cand in (512, 256, 128, 64, 32, 16, 8):
        if r % cand == 0 and cand * c * 4 <= 2 * 1024 * 1024:
            tr = cand
            break

    def body(w_ref, g_ref, m_ref, v_ref, d_ref, mo_ref, vo_ref):
        d_ref[...], mo_ref[...], vo_ref[...] = _adam_math(w_ref[...], g_ref[...], m_ref[...], v_ref[...])

    def spec(a):
        return pl.BlockSpec((tr, c), lambda i: (i, 0)) if a.ndim == 2 else pl.BlockSpec((None, tr, c), lambda i: (0, i, 0))

    return pl.pallas_call(
        body, name=name, grid=(r // tr,), in_specs=[spec(w), spec(g), spec(m), spec(v)], out_specs=[spec(w)] * 3,
        out_shape=[SDS(w.shape, F32)] * 3, compiler_params=_cparams(1))(w, g, m, v)


def adamw_many(ws, gs, ms, vs, name):
    n = len(ws)

    def body(*refs):
        w_r, g_r, m_r, v_r = (refs[k * n:(k + 1) * n] for k in range(4))
        d_r, mo_r, vo_r = (refs[(4 + k) * n:(5 + k) * n] for k in range(3))
        for k in range(n):
            d_r[k][...], mo_r[k][...], vo_r[k][...] = _adam_math(w_r[k][...], g_r[k][...], m_r[k][...], v_r[k][...])

    shapes = [SDS(w.shape, F32) for w in ws]
    outs = pl.pallas_call(body, name=name, out_shape=shapes * 3)(*ws, *gs, *ms, *vs)
    return outs[:n], outs[n:2 * n], outs[2 * n:]


PACK = {"ffn1_w_gate": (352, 352), "ffn1_w_up": (352, 352), "ffn1_w_down": (352, 352),
        "ffn2_w_gate": (352, 352), "ffn2_w_up": (352, 352), "ffn2_w_down": (352, 352),
        "w_out": (256, 256), "w_in": (406, 416), "w_ukv": (48, 48), "w_uq": (36, 48)}
TRANSPOSED = ("ffn1_w_gate", "ffn1_w_up", "ffn2_w_gate", "ffn2_w_up", "w_in", "w_ukv", "w_uq")
GATHER_GROUPS = (("ffn1_w_gate", "ffn1_w_up"), ("ffn1_w_down",),
                 ("w_in", "w_ukv", "w_uq", "w_out", "ffn2_w_gate", "ffn2_w_up", "ffn2_w_down"))
GRAD_GROUPS = (("ffn2", ("ffn2_w_gate", "ffn2_w_up", "ffn2_w_down")), ("mixer", ("w_out", "w_in", "w_ukv", "w_uq")),
               ("ffn1_down", ("ffn1_w_down",)), ("ffn1_gate", ("ffn1_w_gate",)), ("ffn1_up", ("ffn1_w_up",)))


def _pack_offsets(names):
    off, o = {}, 0
    for n in names:
        off[n] = (o, PACK[n][0])
        o += PACK[n][1]
    return off, o


def _shard_to_rows(name, w):
    w = w[0]
    if name in TRANSPOSED:
        w = w.T
    return w.reshape(-1, D_MODEL)


def _rows_to_shard(name, rows, like):
    shp = like.shape[1:]
    if name in TRANSPOSED:
        return rows.reshape(shp[1], shp[0]).T[None]
    return rows.reshape(shp)[None]


def _pack_shards(ws, names, dtype):
    parts = []
    for name in names:
        real, padded = PACK[name]
        rows = _shard_to_rows(name, ws[name]).astype(dtype)
        if padded > real:
            rows = jnp.pad(rows, ((0, padded - real), (0, 0)))
        parts.append(rows)
    return jnp.concatenate(parts, axis=0)


def _grad_rows(name, gw):
    real, padded = PACK[name]
    if name == "w_in":
        rows = _in_proj_rows_inv(gw).reshape(N_DEV, -1, D_MODEL)
    elif name == "w_ukv":
        hd = MLA_HEADS * HEAD_PAD
        rows = jnp.concatenate([gw[:hd].reshape(MLA_HEADS, HEAD_PAD, KV_LORA)[:, :QK_NOPE],
                                gw[hd:].reshape(MLA_HEADS, V_HEAD, KV_LORA)], axis=1).reshape(N_DEV, -1, D_MODEL)
    elif name == "w_uq":
        rows = gw.reshape(MLA_HEADS, HEAD_PAD, Q_LORA)[:, :QK_DIM].reshape(N_DEV, -1, D_MODEL)
    else:
        rows = gw.reshape(N_DEV, -1, D_MODEL)
    if padded > real:
        rows = jnp.pad(rows, ((0, 0), (0, padded - real), (0, 0)))
    return rows


def _pack_rows(arrs):
    parts = []
    for a in arrs:
        flat = a.reshape(-1).astype(F32)
        pad = (-flat.shape[0]) % D_MODEL
        if pad:
            flat = jnp.pad(flat, (0, pad))
        parts.append(flat.reshape(-1, D_MODEL))
    out = jnp.concatenate(parts, axis=0)
    pad = (-out.shape[0]) % 8
    if pad:
        out = jnp.pad(out, ((0, pad), (0, 0)))
    return out


def _unpack_rows(packed, shapes):
    out, row = [], 0
    for shp in shapes:
        n = math.prod(shp)
        nrow = -(-n // D_MODEL)
        out.append(packed[row:row + nrow].reshape(-1)[:n].reshape(shp))
        row += nrow
    return out


def _in_proj_rows(w_t):
    return jnp.concatenate([w_t[0:2560], w_t[2576:2960], w_t[2960:3216], w_t[2560:2576], w_t[3216:3248],
                            jnp.zeros((D_IN_PAD - D_IN, D_MODEL), w_t.dtype)], axis=0)


def _in_proj_rows_inv(d):
    return jnp.concatenate([d[0:2560], d[3200:3216], d[2560:2944], d[2944:3200], d[3216:3248]], axis=0)


def _rope_tables(positions):
    inv_freq = ROPE_THETA ** (-jnp.arange(0, QK_ROPE, 2, dtype=F32) / QK_ROPE)
    ang = positions[..., None].astype(F32) * inv_freq
    cos, sin = jnp.cos(ang), jnp.sin(ang)
    one = jnp.ones(ang.shape[:2] + (QK_NOPE,), F32)
    zero = jnp.zeros_like(one)
    z16, z32, o32 = zero[..., :16], zero[..., :32], one[..., :32]
    cc = jnp.concatenate([one, cos, cos, o32], axis=-1)
    sp = jnp.concatenate([zero, z16, sin, z32], axis=-1)
    sm = jnp.concatenate([zero, -sin, z16, z32], axis=-1)
    return cc, sp, sm


def weight_views(gathered):
    def _seg(name):
        names, g = next((names, g) for names, g in zip(GATHER_GROUPS, gathered) if name in names)
        o, r = _pack_offsets(names)[0][name]
        return g[:, o:o + r]

    full = lambda name: _seg(name).reshape(-1, D_MODEL)
    ukv = _seg("w_ukv").reshape(MLA_HEADS, QK_NOPE + V_HEAD, KV_LORA)
    wukv_t = jnp.concatenate([jnp.pad(ukv[:, :QK_NOPE], ((0, 0), (0, HEAD_PAD - QK_NOPE), (0, 0))).reshape(-1, KV_LORA),
                              ukv[:, QK_NOPE:].reshape(-1, KV_LORA)], axis=0)
    uq = _seg("w_uq").reshape(MLA_HEADS, QK_DIM, Q_LORA)
    wuq_t = jnp.pad(uq, ((0, 0), (0, HEAD_PAD - QK_DIM), (0, 0))).reshape(-1, Q_LORA)
    return dict(wg1_t=full("ffn1_w_gate"), wu1_t=full("ffn1_w_up"), wd1=full("ffn1_w_down"),
                wg2_t=full("ffn2_w_gate"), wu2_t=full("ffn2_w_up"), wd2=full("ffn2_w_down"),
                wo=full("w_out"), win_t=_in_proj_rows(full("w_in")), wukv_t=wukv_t, wuq_t=wuq_t)


def _ffn_bwd(tag, dxn, do, dgate, x, h, gg, uu, a, sc, norm_w, wg_t, wu_t, wd, below):
    f2 = wd.shape[0] // 2
    dwd = mm_tn(a, do, f2, D_MODEL, tag + "_dwd")
    dgg, duu = ffn_dact(do, wd, gg, uu, tag + "_dact")
    dwg_t = mm_tn(dgg, h, f2, D_MODEL, tag + "_dwg")
    dwu_t = mm_tn(duu, h, f2, D_MODEL, tag + "_dwu")
    dx, dsc, dsh, dnw, *nxt = dh_norm_bwd([dgg, duu], [wg_t, wu_t], x, dxn, norm_w, sc, tag + "_dh", below)
    return dx, (dsh, dsc, dgate), dnw, (dwg_t, dwu_t, dwd), nxt


def local_step(x, tgt, positions, mod, wv, p):
    nb, s, d = x.shape
    sh1, sc1, g1, sh2, sc2, g2, sh3, sc3, g3 = mod
    cc, sp, sm = _rope_tables(positions)
    lane_head = jnp.arange(D_SSD, dtype=I32)[None, :] // SSD_HEAD_DIM
    e_mat = (lane_head == jnp.arange(LANES, dtype=I32)[:, None]).astype(BF16)
    et_mat = e_mat.T
    rr, cl = jnp.arange(LANES, dtype=I32)[:, None], jnp.arange(LANES, dtype=I32)[None, :]
    place = ((cl == rr + (QK_NOPE - SSD_HEADS)) & (rr >= SSD_HEADS) & (rr < SSD_HEADS + QK_ROPE)).astype(F32)
    dtb = jnp.pad(p["dt_bias"], ((0, 0), (0, LANES - SSD_HEADS)))
    alog = jnp.pad(p["a_log"], ((0, 0), (0, LANES - SSD_HEADS)))
    dskip_e = jnp.repeat(p["d_skip"], SSD_HEAD_DIM, axis=1)

    h1 = norm_mod(x, p["norm_ffn1"], sc1, sh1, "ffn1_norm")
    gg1, uu1, a1 = ffn_up(h1, wv["wg1_t"], wv["wu1_t"], "ffn1_up")
    x1, o1, h2 = ffn_down(a1, wv["wd1"], x, g1, 0.5, "ffn1_down", (p["norm_mix"], sc2, sh2))
    z, u, cq, ckv, misc = in_proj(h2, wv["win_t"], "in_proj")
    xs, bm, cm_ = conv_fwd(u, p["conv_w"], p["conv_b"], "conv_fwd")
    ys, y, prev = ssd_fwd(xs, bm, cm_, misc, z, dtb, alog, dskip_e, p["ssd_norm_w"], e_mat, "ssd_fwd")
    q, k, v, qn, kvn = qkv_fwd(cq, ckv, misc, cc, sp, sm, p["q_norm_w"], p["kv_norm_w"], wv["wuq_t"], wv["wukv_t"],
                               place, "qkv_fwd")
    attn, lse = flash_fwd(q, k, v, "flash_fwd")
    x2, o2, ym, h3 = out_proj(ys, attn, p["mla_norm_w"], wv["wo"], x1, g2, (p["norm_ffn2"], sc3, sh3), "out_proj")
    gg3, uu3, a3 = ffn_up(h3, wv["wg2_t"], wv["wu2_t"], "ffn2_up")
    loss, dx3, dnfin, do3, dg3 = ffn_down_final(a3, wv["wd2"], x2, g3, 0.5, p["norm_final"], tgt, "ffn2_down_loss")

    dx2, dmod3, dnf2, (dwg2, dwu2, dwd2), (dout, dg2) = _ffn_bwd(
        "ffn2", dx3, do3, dg3, x2, h3, gg3, uu3, a3, sc3, p["norm_ffn2"], wv["wg2_t"], wv["wu2_t"], wv["wd2"],
        (o2, g2, 1.0))
    dys, dattn, dlt, dmlan = out_proj_bwd(dout, attn, p["mla_norm_w"], wv["wo"], "out_proj_bwd")
    dwo = jnp.concatenate([mm_tn(ys, dout, D_SSD, D_MODEL, "dwo_ssd"), mm_tn(ym, dout, D_SSD, D_MODEL, "dwo_mla")], axis=0)
    dxs, dbm, dcm, dz, ddt, dssdn, ddsk_lane, ddtb, dalog = ssd_bwd(
        dys, y, z, xs, bm, cm_, misc, prev, dtb, alog, dskip_e, p["ssd_norm_w"], e_mat, et_mat, "ssd_bwd")
    dq, dk, dv = flash_bwd(q, k, v, dattn, lse, dlt, "flash_bwd")
    dcq, dckv, dmisc, dqp, dkvc, dqn, dkvn = qkv_bwd(dq, dk, dv, ddt, cq, ckv, cc, sp, sm, p["q_norm_w"], p["kv_norm_w"],
                                                     wv["wuq_t"], wv["wukv_t"], place.T, "qkv_bwd")
    dwuq = mm_tn(dqp, qn, MLA_HEADS * HEAD_PAD, Q_LORA, "dwuq")
    dwukv = mm_tn(dkvc, kvn, MLA_HEADS * HEAD_PAD, KV_LORA, "dwukv")
    dvv, dconv = conv_bwd_a(dxs, dbm, dcm, u, p["conv_w"], p["conv_b"], "conv_bwd_a")
    dconv = conv_grads_fold(dconv, "conv_grads_fold")
    du = conv_bwd_b(dvv, p["conv_w"], "conv_bwd_b")
    dproj = jnp.concatenate([dz, du, dcq, dckv, dmisc], axis=-1)
    dwin = mm_tn(dproj, h2, D_IN_PAD // 2, D_MODEL, "dwin")
    dx1, dsc2, dsh2, dnmix, do1, dg1 = dh_norm_bwd([dproj], [wv["win_t"]], x1, dx2, p["norm_mix"], sc2, "mix_dh",
                                                   (o1, g1, 0.5))
    dx0, dmod1, dnf1, (dwg1, dwu1, dwd1), _ = _ffn_bwd(
        "ffn1", dx1, do1, dg1, x, h1, gg1, uu1, a1, sc1, p["norm_ffn1"], wv["wg1_t"], wv["wu1_t"], wv["wd1"], None)

    dmod = jnp.concatenate([*dmod1, dsh2, dsc2, dg2, *dmod3], axis=1).reshape(nb, N_MOD * d)
    return dict(
        loss=loss, dx=dx0, dmod=dmod, norm_ffn1=dnf1, norm_mix=dnmix, norm_ffn2=dnf2, norm_final=dnfin,
        ssd_norm_w=dssdn, mla_norm_w=dmlan, q_norm_w=dqn, kv_norm_w=dkvn,
        dt_bias=ddtb[:, :SSD_HEADS], a_log=dalog[:, :SSD_HEADS],
        d_skip=squeeze_heads(ddsk_lane, et_mat, "d_skip_heads")[:, :SSD_HEADS],
        conv_b=dconv[4:5], conv_w=dconv[0:4],
        gw=dict(ffn1_w_gate=dwg1, ffn1_w_up=dwu1, ffn1_w_down=dwd1, ffn2_w_gate=dwg2, ffn2_w_up=dwu2, ffn2_w_down=dwd2,
                w_out=dwo, w_in=dwin, w_ukv=dwukv, w_uq=dwuq))


def kernel(x, c, positions, w_ada, b_ada, norm_ffn1, ffn1_w_gate, ffn1_w_up, ffn1_w_down, norm_mix, w_in, conv_w, conv_b, dt_bias, a_log, d_skip, ssd_norm_w, q_norm_w, w_uq, kv_norm_w, w_ukv, mla_norm_w, w_out, norm_ffn2, ffn2_w_gate, ffn2_w_up, ffn2_w_down, norm_final, loss_target, m_w_ada, m_b_ada, m_norm_ffn1, m_ffn1_w_gate, m_ffn1_w_up, m_ffn1_w_down, m_norm_mix, m_w_in, m_conv_w, m_conv_b, m_dt_bias, m_a_log, m_d_skip, m_ssd_norm_w, m_q_norm_w, m_w_uq, m_kv_norm_w, m_w_ukv, m_mla_norm_w, m_w_out, m_norm_ffn2, m_ffn2_w_gate, m_ffn2_w_up, m_ffn2_w_down, m_norm_final, v_w_ada, v_b_ada, v_norm_ffn1, v_ffn1_w_gate, v_ffn1_w_up, v_ffn1_w_down, v_norm_mix, v_w_in, v_conv_w, v_conv_b, v_dt_bias, v_a_log, v_d_skip, v_ssd_norm_w, v_q_norm_w, v_w_uq, v_kv_norm_w, v_w_ukv, v_mla_norm_w, v_w_out, v_norm_ffn2, v_ffn2_w_gate, v_ffn2_w_up, v_ffn2_w_down, v_norm_final):
    names = ["w_ada", "b_ada", "norm_ffn1", "ffn1_w_gate", "ffn1_w_up", "ffn1_w_down", "norm_mix", "w_in", "conv_w",
             "conv_b", "dt_bias", "a_log", "d_skip", "ssd_norm_w", "q_norm_w", "w_uq", "kv_norm_w", "w_ukv",
             "mla_norm_w", "w_out", "norm_ffn2", "ffn2_w_gate", "ffn2_w_up", "ffn2_w_down", "norm_final"]
    W = dict(zip(names, (w_ada, b_ada, norm_ffn1, ffn1_w_gate, ffn1_w_up, ffn1_w_down, norm_mix, w_in, conv_w, conv_b, dt_bias, a_log, d_skip, ssd_norm_w, q_norm_w, w_uq, kv_norm_w, w_ukv, mla_norm_w, w_out, norm_ffn2, ffn2_w_gate, ffn2_w_up, ffn2_w_down, norm_final)))
    M = dict(zip(names, (m_w_ada, m_b_ada, m_norm_ffn1, m_ffn1_w_gate, m_ffn1_w_up, m_ffn1_w_down, m_norm_mix, m_w_in, m_conv_w, m_conv_b, m_dt_bias, m_a_log, m_d_skip, m_ssd_norm_w, m_q_norm_w, m_w_uq, m_kv_norm_w, m_w_ukv, m_mla_norm_w, m_w_out, m_norm_ffn2, m_ffn2_w_gate, m_ffn2_w_up, m_ffn2_w_down, m_norm_final)))
    V = dict(zip(names, (v_w_ada, v_b_ada, v_norm_ffn1, v_ffn1_w_gate, v_ffn1_w_up, v_ffn1_w_down, v_norm_mix, v_w_in, v_conv_w, v_conv_b, v_dt_bias, v_a_log, v_d_skip, v_ssd_norm_w, v_q_norm_w, v_w_uq, v_kv_norm_w, v_w_ukv, v_mla_norm_w, v_w_out, v_norm_ffn2, v_ffn2_w_gate, v_ffn2_w_up, v_ffn2_w_down, v_norm_final)))

    nb, s, d = x.shape
    me = 4 * lax.axis_index("x") + 2 * lax.axis_index("y") + lax.axis_index("c")
    n_ada = w_ada.shape[2]

    taps, n_cw = conv_w.shape[1:]
    cg = all_gather8(_pack_rows([c, conv_w[0]]), "gather_c")
    c_all = cg[:, 0:nb].reshape(N_DEV * nb, d)
    conv_w_full = cg[:, nb, 0:taps * n_cw].reshape(N_DEV, taps, n_cw).transpose(1, 0, 2).reshape(taps, N_DEV * n_cw)
    g_ffn1 = all_gather8(_pack_shards(W, GATHER_GROUPS[0], BF16), "gather_w_ffn1")

    b_ada_cols = lax.dynamic_slice(b_ada, (0, me * n_ada), (1, n_ada))
    mod_cols, c_act = adaln_fwd(c_all, w_ada[0], b_ada_cols, "adaln_fwd")
    mod_g = all_gather8(mod_cols, "gather_mod")
    g_ffn1, mod_g, down1, rest = lax.optimization_barrier(
        (g_ffn1, mod_g, _pack_shards(W, GATHER_GROUPS[1], BF16), _pack_shards(W, GATHER_GROUPS[2], BF16)))
    g_down1 = sc_all_gather8(down1, "gather_w_ffn1_down", 1)
    wv = weight_views((g_ffn1, g_down1, sc_all_gather8(rest, "gather_w_rest", 7)))
    mod = lax.dynamic_slice(mod_g, (0, me * nb, 0), (N_DEV, nb, n_ada)).transpose(1, 0, 2).reshape(nb, N_MOD, 1, d)
    mod = [mod[:, k] for k in range(N_MOD)]

    P = dict(W)
    P["conv_w"] = conv_w_full
    P["norm_final"] = norm_final.reshape(1, d)
    R = local_step(x, loss_target, positions, mod, wv, P)

    dmod = R["dmod"]
    partial_shapes = [(1,), (1, d), (1, d), (1, d), (1, d), (1, d), (1, d), (1, Q_LORA), (1, KV_LORA),
                      (1, SSD_HEADS), (1, SSD_HEADS), (1, SSD_HEADS), (1, D_CONV), (4, D_CONV), (1, N_MOD * d),
                      (nb, N_MOD * d)]
    partial = _pack_rows([R["loss"][0, :1], R["norm_ffn1"], R["norm_mix"], R["norm_ffn2"], R["norm_final"],
                          R["ssd_norm_w"], R["mla_norm_w"], R["q_norm_w"], R["kv_norm_w"],
                          R["dt_bias"], R["a_log"], R["d_skip"], R["conv_b"], R["conv_w"],
                          sum_rows(dmod, "dmod_rows"), dmod])
    partial_g = all_gather8(partial, "gather_partials")
    (loss, g_nf1, g_nmix, g_nf2, g_nfin, g_ssdn, g_mlan, g_qn, g_kvn, g_dtb, g_alog, g_dskip, g_convb, g_convw,
     g_bada, _) = _unpack_rows(sum_blocks(partial_g, "sum_partials"), partial_shapes)
    dmod_row = sum(-(-math.prod(shp) // D_MODEL) for shp in partial_shapes[:-1])
    dmod_all = partial_g[:, dmod_row:dmod_row + nb * N_MOD].reshape(N_DEV * nb, N_MOD * d)
    g_wada = adaln_bwd(c_act, lax.dynamic_slice(dmod_all, (0, me * n_ada), (N_DEV * nb, n_ada)), "adaln_bwd")
    n_cw = conv_w.shape[2]
    G = {"w_ada": g_wada[None], "b_ada": g_bada, "norm_ffn1": g_nf1, "norm_mix": g_nmix, "norm_ffn2": g_nf2,
         "norm_final": g_nfin.reshape(d), "ssd_norm_w": g_ssdn, "mla_norm_w": g_mlan, "q_norm_w": g_qn,
         "kv_norm_w": g_kvn, "dt_bias": g_dtb, "a_log": g_alog, "d_skip": g_dskip, "conv_b": g_convb,
         "conv_w": lax.dynamic_slice(g_convw, (0, me * n_cw), (4, n_cw))[None]}

    DW, NM, NV = {}, {}, {}
    gw = R["gw"]
    for k, (tag, group) in enumerate(GRAD_GROUPS):
        send = jnp.concatenate([_grad_rows(name, gw[name]) for name in group], axis=1).astype(BF16)
        recv = sc_all_to_all8(send, "exchange_" + tag, 2 + k)
        gsum = sum_blocks(recv, "sum_" + tag)
        for name, (o, r) in _pack_offsets(group)[0].items():
            G[name] = _rows_to_shard(name, gsum[o:o + r], W[name])
            DW[name], NM[name], NV[name] = adamw(W[name], G[name], M[name], V[name], "adamw_" + name)
    DW["w_ada"], NM["w_ada"], NV["w_ada"] = adamw(w_ada, g_wada, m_w_ada, v_w_ada, "adamw_w_ada")
    small = [n for n in names if n not in DW]
    as2d = lambda a: a.reshape(-1, a.shape[-1])
    outs = adamw_many([as2d(W[n]) for n in small], [as2d(G[n]) for n in small], [as2d(M[n]) for n in small],
                      [as2d(V[n]) for n in small], "adamw_small")
    for res, dst in zip(outs, (DW, NM, NV)):
        for n, t in zip(small, res):
            dst[n] = t.reshape(W[n].shape)
    return (loss.reshape(()), R["dx"], *[G[n] for n in names], *[DW[n] for n in names], *[NM[n] for n in names],
            *[NV[n] for n in names])
```

```python
import math

import jax
import jax.numpy as jnp
from jax import lax
from jax.experimental import pallas as pl
from jax.experimental.pallas import tpu as pltpu
from jax.experimental.pallas import tpu_sc as plsc

F32, BF16, I32 = jnp.float32, jnp.bfloat16, jnp.int32
HI = lax.Precision.HIGHEST
SDS = jax.ShapeDtypeStruct
MESH = pl.DeviceIdType.MESH

D_MODEL = 1024
D_FF = 2816
D_SSD = 1024
SSD_HEADS = 16
SSD_HEAD_DIM = 64
SSD_GROUPS = 2
SSD_STATE = 128
CHUNK = 128
MLA_HEADS = 8
QK_NOPE = 64
QK_ROPE = 32
QK_DIM = 96
V_HEAD = 128
Q_LORA = 384
KV_LORA = 256
ROPE_THETA = 10000.0
N_MOD = 9
EPS = 1e-6
D_CONV = 1536
D_IN = 3248
D_IN_PAD = 3328
HEAD_PAD = 128
N_DEV = 8
ADAM_LR, ADAM_B1, ADAM_B2, ADAM_EPS, ADAM_WD, ADAM_STEP = 0.001, 0.9, 0.999, 1e-08, 0.01, 10

SAVED_ACT = BF16
VMEM_LIMIT = 56 * 1024 * 1024
LANES = 128
NT_DIMS = (((1,), (1,)), ((), ()))
TN_DIMS = (((0,), (0,)), ((), ()))


def _cparams(n_axes):
    return pltpu.CompilerParams(dimension_semantics=("arbitrary",) * n_axes, vmem_limit_bytes=VMEM_LIMIT)


def _row(tm, d):
    return pl.BlockSpec((None, tm, d), lambda b, i: (b, i, 0))


def _bvec(d):
    return pl.BlockSpec((None, 1, d), lambda b, i: (b, 0, 0))


def _full(shape):
    n = len(shape)
    return pl.BlockSpec(shape, lambda *_: (0,) * n)


def _sigmoid(x):
    return 1.0 / (1.0 + jnp.exp(-x))


def _softplus(x):
    return jnp.maximum(x, 0.0) + jnp.log(1.0 + jnp.exp(-jnp.abs(x)))


def _rms(x):
    return lax.rsqrt(jnp.mean(x * x, axis=-1, keepdims=True) + EPS)


def _rms_bwd(dn, n, r):
    return r * (dn - n * jnp.mean(dn * n, axis=-1, keepdims=True))


def _first_step():
    return (pl.program_id(0) == 0) & (pl.program_id(1) == 0)


def all_gather8(x, name):
    r, c = x.shape

    def body(x_ref, out_ref, send_sems, recv_sems, local_sem):
        mx, my, mc = lax.axis_index("x"), lax.axis_index("y"), lax.axis_index("c")
        me, sibling = (mx, my, mc), (mx, my, 1 - mc)
        chips = [(1 - mx, my), (mx, 1 - my), (1 - mx, 1 - my)]

        def rows(px, py, pc):
            return out_ref.at[4 * px + 2 * py + pc]

        def copy(k, block, to, src=None):
            return pltpu.make_async_remote_copy(
                src_ref=rows(*block) if src is None else src, dst_ref=rows(*block),
                send_sem=send_sems.at[k], recv_sem=recv_sems.at[k], device_id=to, device_id_type=MESH)

        mine = pltpu.make_async_copy(x_ref, rows(*me), local_sem)
        mine.start()
        first = [copy(0, me, sibling, src=x_ref)]
        first += [copy(1 + j, me, (*chip, mc), src=x_ref) for j, chip in enumerate(chips)]
        for cp in first:
            cp.start()
        passed = [copy(4 + j, (*chip, mc), sibling) for j, chip in enumerate(chips)]
        for j, chip in enumerate(chips):
            copy(1 + j, (*chip, mc), me).wait_recv()
            passed[j].start()
        copy(0, sibling, me).wait_recv()
        for j, chip in enumerate(chips):
            copy(4 + j, (*chip, 1 - mc), me).wait_recv()
        for cp in first + passed:
            cp.wait_send()
        mine.wait()

    return pl.pallas_call(
        body, name=name,
        out_shape=SDS((N_DEV, r, c), x.dtype),
        in_specs=[pl.BlockSpec(memory_space=pl.ANY)],
        out_specs=pl.BlockSpec(memory_space=pl.ANY),
        scratch_shapes=[pltpu.SemaphoreType.DMA((7,)), pltpu.SemaphoreType.DMA((7,)), pltpu.SemaphoreType.DMA],
    )(x)


def all_to_all8(x, name):
    _, r, c = x.shape

    def body(x_ref, out_ref, send_sems, recv_sems, local_sem):
        mx, my, mc = lax.axis_index("x"), lax.axis_index("y"), lax.axis_index("c")
        me = 4 * mx + 2 * my + mc
        mine = pltpu.make_async_copy(x_ref.at[me], out_ref.at[me], local_sem)
        mine.start()
        copies = []
        for rel in range(1, N_DEV):
            px = 1 - mx if rel & 4 else mx
            py = 1 - my if rel & 2 else my
            pc = 1 - mc if rel & 1 else mc
            cp = pltpu.make_async_remote_copy(
                src_ref=x_ref.at[4 * px + 2 * py + pc], dst_ref=out_ref.at[me],
                send_sem=send_sems.at[rel - 1], recv_sem=recv_sems.at[rel - 1],
                device_id=(px, py, pc), device_id_type=MESH)
            cp.start()
            copies.append(cp)
        for cp in copies:
            cp.wait()
        mine.wait()

    return pl.pallas_call(
        body, name=name,
        out_shape=SDS((N_DEV, r, c), x.dtype),
        in_specs=[pl.BlockSpec(memory_space=pl.ANY)],
        out_specs=pl.BlockSpec(memory_space=pl.ANY),
        scratch_shapes=[pltpu.SemaphoreType.DMA((7,)), pltpu.SemaphoreType.DMA((7,)), pltpu.SemaphoreType.DMA],
    )(x)


def _sequencer_kernel(name, collective_id):
    return pl.kernel(
        mesh=plsc.ScalarSubcoreMesh(axis_name="seq", num_cores=1), name=name,
        scratch_types=(pltpu.SemaphoreType.DMA((7,)), pltpu.SemaphoreType.DMA((7,)), pltpu.SemaphoreType.DMA),
        compiler_params=pltpu.CompilerParams(collective_id=collective_id))


def _handshake(peers):
    barrier = pltpu.get_barrier_semaphore()
    for peer in peers:
        pl.semaphore_signal(barrier, inc=1, device_id=peer, device_id_type=MESH)
    pl.semaphore_wait(barrier, len(peers))


def sc_all_gather8(x, name, collective_id):
    r, c = x.shape
    x_ref = jax.new_ref(x, memory_space=pltpu.MemorySpace.HBM)
    out_ref = jax.empty_ref(SDS((N_DEV, r, c), x.dtype), memory_space=pltpu.MemorySpace.HBM)

    @_sequencer_kernel(name, collective_id)
    def launch(send_sems, recv_sems, local_sem):
        mx, my, mc = lax.axis_index("x"), lax.axis_index("y"), lax.axis_index("c")
        me, sibling = (mx, my, mc), (mx, my, 1 - mc)
        chips = [(1 - mx, my), (mx, 1 - my), (1 - mx, 1 - my)]
        _handshake([sibling] + [(*chip, mc) for chip in chips])

        def rows(px, py, pc):
            return out_ref.at[4 * px + 2 * py + pc]

        def copy(k, block, to, src=None):
            return pltpu.make_async_remote_copy(
                src_ref=rows(*block) if src is None else src, dst_ref=rows(*block),
                send_sem=send_sems.at[k], recv_sem=recv_sems.at[k], device_id=to, device_id_type=MESH)

        mine = pltpu.make_async_copy(x_ref, rows(*me), local_sem)
        mine.start()
        first = [copy(0, me, sibling, src=x_ref)]
        first += [copy(1 + j, me, (*chip, mc), src=x_ref) for j, chip in enumerate(chips)]
        for cp in first:
            cp.start()
        passed = [copy(4 + j, (*chip, mc), sibling) for j, chip in enumerate(chips)]
        for j, chip in enumerate(chips):
            copy(1 + j, (*chip, mc), me).wait_recv()
            passed[j].start()
        copy(0, sibling, me).wait_recv()
        for j, chip in enumerate(chips):
            copy(4 + j, (*chip, 1 - mc), me).wait_recv()
        for cp in first + passed:
            cp.wait_send()
        mine.wait()

    launch()
    return out_ref[...]


def sc_all_to_all8(x, name, collective_id):
    x_ref = jax.new_ref(x, memory_space=pltpu.MemorySpace.HBM)
    out_ref = jax.empty_ref(SDS(x.shape, x.dtype), memory_space=pltpu.MemorySpace.HBM)

    @_sequencer_kernel(name, collective_id)
    def launch(send_sems, recv_sems, local_sem):
        mx, my, mc = lax.axis_index("x"), lax.axis_index("y"), lax.axis_index("c")
        me = 4 * mx + 2 * my + mc
        peers = [(1 - mx if rel & 4 else mx, 1 - my if rel & 2 else my, 1 - mc if rel & 1 else mc)
                 for rel in range(1, N_DEV)]
        _handshake(peers)
        mine = pltpu.make_async_copy(x_ref.at[me], out_ref.at[me], local_sem)
        mine.start()
        copies = []
        for k, (px, py, pc) in enumerate(peers):
            cp = pltpu.make_async_remote_copy(
                src_ref=x_ref.at[4 * px + 2 * py + pc], dst_ref=out_ref.at[me],
                send_sem=send_sems.at[k], recv_sem=recv_sems.at[k], device_id=(px, py, pc), device_id_type=MESH)
            cp.start()
            copies.append(cp)
        for cp in copies:
            cp.wait()
        mine.wait()

    launch()
    return out_ref[...]


def norm_mod(x, w, sc, sh, name):
    b, s, d = x.shape
    tm = min(512, s)

    def body(x_ref, w_ref, sc_ref, sh_ref, h_ref):
        xv = x_ref[...]
        n = xv * _rms(xv)
        h_ref[...] = ((n * w_ref[...]) * (1.0 + sc_ref[...]) + sh_ref[...]).astype(BF16)

    return pl.pallas_call(
        body, name=name, grid=(b, s // tm),
        in_specs=[_row(tm, d), _full((1, d)), _bvec(d), _bvec(d)],
        out_specs=_row(tm, d), out_shape=SDS((b, s, d), BF16), compiler_params=_cparams(2))(x, w, sc, sh)


def ffn_up(h, wg_t, wu_t, name):
    b, s, d = h.shape
    f = wg_t.shape[0]
    tm, tn = min(512, s), f // 2

    def body(h_ref, wg_ref, wu_ref, s_ref, t_ref, a_ref):
        hv = h_ref[...]
        g = lax.dot_general(hv, wg_ref[...], NT_DIMS, preferred_element_type=F32)
        u = lax.dot_general(hv, wu_ref[...], NT_DIMS, preferred_element_type=F32)
        sg = _sigmoid(g)
        silu = g * sg
        s_ref[...] = silu.astype(s_ref.dtype)
        t_ref[...] = (u * (sg + silu * (1.0 - sg))).astype(t_ref.dtype)
        a_ref[...] = (silu * u).astype(BF16)

    hs = pl.BlockSpec((None, tm, d), lambda j, bb, i: (bb, i, 0))
    ws = pl.BlockSpec((tn, d), lambda j, bb, i: (j, 0))
    os_ = pl.BlockSpec((None, tm, tn), lambda j, bb, i: (bb, i, j))
    return pl.pallas_call(
        body, name=name, grid=(f // tn, b, s // tm),
        in_specs=[hs, ws, ws], out_specs=[os_, os_, os_],
        out_shape=[SDS((b, s, f), SAVED_ACT), SDS((b, s, f), SAVED_ACT), SDS((b, s, f), BF16)],
        compiler_params=_cparams(3))(h, wg_t, wu_t)


def _norm_mod_tile(xv, w_ref, sc_ref, sh_ref):
    return ((xv * _rms(xv) * w_ref[...]) * (1.0 + sc_ref[...]) + sh_ref[...]).astype(BF16)


def ffn_down(a, wd, x, gate, scale, name, above=None):
    b, s, f = a.shape
    d = wd.shape[1]
    tm = min(512, s)

    def body(a_ref, wd_ref, x_ref, g_ref, *rest):
        xn_ref, o_ref = rest[-3:-1] if above else rest
        o = jnp.dot(a_ref[...], wd_ref[...], preferred_element_type=F32)
        xn = x_ref[...] + (scale * g_ref[...]) * o
        xn_ref[...] = xn
        o_ref[...] = o.astype(BF16)
        if above:
            rest[-1][...] = _norm_mod_tile(xn, *rest[0:3])

    extra = above is not None
    return pl.pallas_call(
        body, name=name, grid=(b, s // tm),
        in_specs=[_row(tm, f), _full((f, d)), _row(tm, d), _bvec(d)] + ([_full((1, d)), _bvec(d), _bvec(d)] if extra else []),
        out_specs=[_row(tm, d), _row(tm, d)] + ([_row(tm, d)] if extra else []),
        out_shape=[SDS((b, s, d), F32), SDS((b, s, d), BF16)] + ([SDS((b, s, d), BF16)] if extra else []),
        compiler_params=_cparams(2))(a, wd, x, gate, *(above or ()))


def ffn_down_final(a, wd, x, gate, scale, w_final, tgt, name):
    b, s, f = a.shape
    d = wd.shape[1]
    tm = min(512, s)

    def body(a_ref, wd_ref, x_ref, g_ref, w_ref, t_ref, loss_ref, dx_ref, dw_ref, do_ref, dg_ref):
        @pl.when(_first_step())
        def _():
            loss_ref[...] = jnp.zeros_like(loss_ref)
            dw_ref[...] = jnp.zeros_like(dw_ref)

        @pl.when(pl.program_id(1) == 0)
        def _():
            dg_ref[...] = jnp.zeros_like(dg_ref)
        o = jnp.dot(a_ref[...], wd_ref[...], preferred_element_type=F32)
        sg = scale * g_ref[...]
        xv = x_ref[...] + sg * o
        r = _rms(xv)
        n = xv * r
        wv = w_ref[...]
        e = n * wv - t_ref[...]
        loss_ref[...] += jnp.sum(e * e) * (0.5 / d)
        dy = e * (1.0 / d)
        dw_ref[...] += jnp.sum(dy * n, axis=0, keepdims=True)
        dx = _rms_bwd(dy * wv, n, r)
        dx_ref[...] = dx
        do_ref[...] = (sg * dx).astype(BF16)
        dg_ref[...] += jnp.sum(scale * dx * o, axis=0, keepdims=True)

    return pl.pallas_call(
        body, name=name, grid=(b, s // tm),
        in_specs=[_row(tm, f), _full((f, d)), _row(tm, d), _bvec(d), _full((1, d)), _row(tm, d)],
        out_specs=[_full((1, LANES)), _row(tm, d), _full((1, d)), _row(tm, d), _bvec(d)],
        out_shape=[SDS((1, LANES), F32), SDS((b, s, d), F32), SDS((1, d), F32), SDS((b, s, d), BF16), SDS((b, 1, d), F32)],
        compiler_params=_cparams(2))(a, wd, x, gate, w_final, tgt)


def ffn_dact(do, wd, silu_g, u_dsilu, name):
    b, s, d = do.shape
    f = wd.shape[0]
    tm, tn = min(512, s), f // 2

    def body(do_ref, wd_ref, s_ref, t_ref, dg_ref, du_ref):
        da = lax.dot_general(do_ref[...], wd_ref[...], NT_DIMS, preferred_element_type=F32)
        dg_ref[...] = (da * t_ref[...].astype(F32)).astype(BF16)
        du_ref[...] = (da * s_ref[...].astype(F32)).astype(BF16)

    dos = pl.BlockSpec((None, tm, d), lambda j, bb, i: (bb, i, 0))
    ws = pl.BlockSpec((tn, d), lambda j, bb, i: (j, 0))
    es = pl.BlockSpec((None, tm, tn), lambda j, bb, i: (bb, i, j))
    return pl.pallas_call(
        body, name=name, grid=(f // tn, b, s // tm),
        in_specs=[dos, ws, es, es], out_specs=[es, es],
        out_shape=[SDS((b, s, f), BF16), SDS((b, s, f), BF16)], compiler_params=_cparams(3))(do, wd, silu_g, u_dsilu)


def mm_tn(a, bm, tma, tnb, name):
    b, s, ka = a.shape
    nb = bm.shape[2]
    tk = min(2048, s)
    nk = s // tk

    def body(a_ref, b_ref, o_ref, acc):
        first = (pl.program_id(2) == 0) & (pl.program_id(3) == 0)
        last = (pl.program_id(2) == b - 1) & (pl.program_id(3) == nk - 1)
        part = lax.dot_general(a_ref[...], b_ref[...], TN_DIMS, preferred_element_type=F32)

        @pl.when(first)
        def _():
            acc[...] = part

        @pl.when(jnp.logical_not(first))
        def _():
            acc[...] += part

        @pl.when(last)
        def _():
            o_ref[...] = acc[...].astype(BF16)

    return pl.pallas_call(
        body, name=name, grid=(ka // tma, nb // tnb, b, nk),
        in_specs=[pl.BlockSpec((None, tk, tma), lambda i, j, bb, k: (bb, k, i)),
                  pl.BlockSpec((None, tk, tnb), lambda i, j, bb, k: (bb, k, j))],
        out_specs=pl.BlockSpec((tma, tnb), lambda i, j, bb, k: (i, j)),
        out_shape=SDS((ka, nb), BF16), scratch_shapes=[pltpu.VMEM((tma, tnb), F32)],
        compiler_params=_cparams(4))(a, bm)


def _gate_bwd_specs(tm, d, b, s):
    return ([_row(tm, d), _bvec(d)], [_row(tm, d), _bvec(d)], [SDS((b, s, d), BF16), SDS((b, 1, d), F32)])


def _gate_bwd_tile(dx, scale, o_ref, g_ref, do_ref, dg_ref):
    do_ref[...] = ((scale * g_ref[...]) * dx).astype(BF16)
    dg_ref[...] += jnp.sum(scale * dx * o_ref[...].astype(F32), axis=0, keepdims=True)


def n_in_bytes(arrs):
    return sum(a.size * a.dtype.itemsize for a in arrs)


def dh_norm_bwd(dys, wts, x, dxn, w, sc, name, below=None):
    b, s, d = x.shape
    tm = min(512 if n_in_bytes(wts) <= 8 * 1024 * 1024 else 256, s)
    n_in = len(dys)
    extra_in, extra_out, extra_shape = _gate_bwd_specs(tm, d, b, s) if below else ([], [], [])

    def body(*refs):
        dy_refs, w_refs = refs[:n_in], refs[n_in:2 * n_in]
        x_ref, dxn_ref, nw_ref, sc_ref = refs[2 * n_in:2 * n_in + 4]
        rest = refs[2 * n_in + 4:]
        if below:
            o_ref, g_ref, dx_ref, dsc_ref, dsh_ref, dw_ref, do_ref, dg_ref = rest
        else:
            dx_ref, dsc_ref, dsh_ref, dw_ref = rest

        @pl.when(pl.program_id(1) == 0)
        def _():
            dsc_ref[...] = jnp.zeros_like(dsc_ref)
            dsh_ref[...] = jnp.zeros_like(dsh_ref)
            if below:
                dg_ref[...] = jnp.zeros_like(dg_ref)

        @pl.when(_first_step())
        def _():
            dw_ref[...] = jnp.zeros_like(dw_ref)

        dh = jnp.dot(dy_refs[0][...], w_refs[0][...], preferred_element_type=F32)
        for k in range(1, n_in):
            dh += jnp.dot(dy_refs[k][...], w_refs[k][...], preferred_element_type=F32)
        xv = x_ref[...]
        r = _rms(xv)
        n = xv * r
        nw = nw_ref[...]
        dsc_ref[...] += jnp.sum(dh * (n * nw), axis=0, keepdims=True)
        dsh_ref[...] += jnp.sum(dh, axis=0, keepdims=True)
        dhn = dh * (1.0 + sc_ref[...])
        dw_ref[...] += jnp.sum(dhn * n, axis=0, keepdims=True)
        dx = dxn_ref[...] + _rms_bwd(dhn * nw, n, r)
        dx_ref[...] = dx
        if below:
            _gate_bwd_tile(dx, below[2], o_ref, g_ref, do_ref, dg_ref)

    in_specs = [_row(tm, dy.shape[2]) for dy in dys] + [_full(wt.shape) for wt in wts]
    in_specs += [_row(tm, d), _row(tm, d), _full((1, d)), _bvec(d)] + extra_in
    return pl.pallas_call(
        body, name=name, grid=(b, s // tm), in_specs=in_specs,
        out_specs=[_row(tm, d), _bvec(d), _bvec(d), _full((1, d))] + extra_out,
        out_shape=[SDS((b, s, d), F32), SDS((b, 1, d), F32), SDS((b, 1, d), F32), SDS((1, d), F32)] + extra_shape,
        compiler_params=_cparams(2))(*dys, *wts, x, dxn, w, sc, *(below[:2] if below else ()))


def in_proj(h, win_t, name):
    b, s, d = h.shape
    tm = min(512, s)
    widths = (D_SSD, D_SSD + 2 * SSD_GROUPS * SSD_STATE, Q_LORA, KV_LORA, LANES)

    def body(h_ref, w_ref, *outs):
        p = lax.dot_general(h_ref[...], w_ref[...], NT_DIMS, preferred_element_type=F32)
        off = 0
        for o_ref, wd in zip(outs, widths):
            o_ref[...] = p[:, off:off + wd]
            off += wd

    return pl.pallas_call(
        body, name=name, grid=(b, s // tm),
        in_specs=[_row(tm, d), _full(win_t.shape)],
        out_specs=[_row(tm, wd) for wd in widths],
        out_shape=[SDS((b, s, wd), F32) for wd in widths], compiler_params=_cparams(2))(h, win_t)


def _halo_prev(ts, d):
    return pl.BlockSpec((None, 8, d), lambda b, i: (b, jnp.maximum(i * (ts // 8) - 1, 0), 0))


CONV_ROWS = 32


def _conv_head(head, u_ref, up_ref):
    head[0:8, :] = jnp.where(pl.program_id(1) > 0, up_ref[...], 0.0)
    head[8:8 + CONV_ROWS, :] = u_ref[0:CONV_ROWS, :]


def _conv_windows(u_ref, head, r0):
    if r0 == 0:
        return [head[5 + k:5 + k + CONV_ROWS, :] for k in range(4)]
    return [u_ref[r0 - 3 + k:r0 - 3 + k + CONV_ROWS, :] for k in range(4)]


def _fold8(t):
    acc = t[0:8, :]
    for r in range(8, CONV_ROWS, 8):
        acc += t[r:r + 8, :]
    return acc


def conv_fwd(u, cw, cb, name):
    b, s, dc = u.shape
    ts = min(512, s)
    widths = (D_SSD, SSD_GROUPS * SSD_STATE, SSD_GROUPS * SSD_STATE)

    def body(u_ref, up_ref, w_ref, b_ref, xs_ref, bm_ref, cm_ref, head):
        _conv_head(head, u_ref, up_ref)
        ws = [w_ref[k:k + 1, :] for k in range(4)]
        bias = b_ref[...]
        for r0 in range(0, ts, CONV_ROWS):
            taps = _conv_windows(u_ref, head, r0)
            v = bias + taps[0] * ws[0] + taps[1] * ws[1] + taps[2] * ws[2] + taps[3] * ws[3]
            y = v * _sigmoid(v)
            rs = slice(r0, r0 + CONV_ROWS)
            xs_ref[rs, :] = y[:, 0:D_SSD]
            bm_ref[rs, :] = y[:, D_SSD:D_SSD + 256]
            cm_ref[rs, :] = y[:, D_SSD + 256:D_SSD + 512]

    return pl.pallas_call(
        body, name=name, grid=(b, s // ts),
        in_specs=[_row(ts, dc), _halo_prev(ts, dc), _full((4, dc)), _full((1, dc))],
        out_specs=[_row(ts, wd) for wd in widths],
        out_shape=[SDS((b, s, wd), F32) for wd in widths],
        scratch_shapes=[pltpu.VMEM((8 + CONV_ROWS, dc), F32)], compiler_params=_cparams(2))(u, u, cw, cb)


def conv_bwd_a(dxs, dbm, dcm, u, cw, cb, name):
    b, s, dc = u.shape
    ts = min(512, s)

    def body(dxs_ref, dbm_ref, dcm_ref, u_ref, up_ref, w_ref, b_ref, dv_ref, dwb_ref, head):
        @pl.when(_first_step())
        def _():
            dwb_ref[...] = jnp.zeros_like(dwb_ref)
        _conv_head(head, u_ref, up_ref)
        ws = [w_ref[k:k + 1, :] for k in range(4)]
        bias = b_ref[...]
        for r0 in range(0, ts, CONV_ROWS):
            taps = _conv_windows(u_ref, head, r0)
            v = bias + taps[0] * ws[0] + taps[1] * ws[1] + taps[2] * ws[2] + taps[3] * ws[3]
            sg = _sigmoid(v)
            rs = slice(r0, r0 + CONV_ROWS)
            dy = jnp.concatenate([dxs_ref[rs, :], dbm_ref[rs, :], dcm_ref[rs, :]], axis=1)
            dv = dy * (sg * (1.0 + v * (1.0 - sg)))
            dv_ref[rs, :] = dv
            for k in range(4):
                dwb_ref[8 * k:8 * k + 8, :] += _fold8(dv * taps[k])
            dwb_ref[32:40, :] += _fold8(dv)

    return pl.pallas_call(
        body, name=name, grid=(b, s // ts),
        in_specs=[_row(ts, D_SSD), _row(ts, 256), _row(ts, 256), _row(ts, dc), _halo_prev(ts, dc),
                  _full((4, dc)), _full((1, dc))],
        out_specs=[_row(ts, dc), _full((40, dc))],
        out_shape=[SDS((b, s, dc), F32), SDS((40, dc), F32)],
        scratch_shapes=[pltpu.VMEM((8 + CONV_ROWS, dc), F32)], compiler_params=_cparams(2))(dxs, dbm, dcm, u, u, cw, cb)


def conv_grads_fold(x, name):
    c = x.shape[1]

    def body(x_ref, o_ref):
        o_ref[...] = jnp.zeros_like(o_ref)
        for k in range(5):
            o_ref[k:k + 1, :] = jnp.sum(x_ref[8 * k:8 * k + 8, :], axis=0, keepdims=True)

    return pl.pallas_call(body, name=name, out_shape=SDS((8, c), F32))(x)


def conv_bwd_b(dv, cw, name):
    b, s, dc = dv.shape
    ts = min(512, s)
    nt = s // ts

    def body(dv_ref, dn_ref, w_ref, du_ref, tail):
        tail[0:CONV_ROWS, :] = dv_ref[ts - CONV_ROWS:ts, :]
        tail[CONV_ROWS:CONV_ROWS + 8, :] = jnp.where(pl.program_id(1) < nt - 1, dn_ref[...], 0.0)
        ws = [w_ref[k:k + 1, :] for k in range(4)]
        for r0 in range(0, ts, CONV_ROWS):
            if r0 == ts - CONV_ROWS:
                win = [tail[3 - k:3 - k + CONV_ROWS, :] for k in range(4)]
            else:
                win = [dv_ref[r0 + 3 - k:r0 + 3 - k + CONV_ROWS, :] for k in range(4)]
            acc = win[0] * ws[0] + win[1] * ws[1] + win[2] * ws[2] + win[3] * ws[3]
            du_ref[r0:r0 + CONV_ROWS, :] = acc.astype(BF16)

    nxt = pl.BlockSpec((None, 8, dc), lambda bb, i: (bb, jnp.minimum((i + 1) * (ts // 8), s // 8 - 1), 0))
    return pl.pallas_call(
        body, name=name, grid=(b, nt),
        in_specs=[_row(ts, dc), nxt, _full((4, dc))],
        out_specs=_row(ts, dc), out_shape=SDS((b, s, dc), BF16),
        scratch_shapes=[pltpu.VMEM((CONV_ROWS + 8, dc), F32)], compiler_params=_cparams(2))(dv, dv, cw)


def _ssd_common(misc_ref, dtb_ref, alog_ref, e_ref):
    ln = CHUNK
    lane = lax.broadcasted_iota(I32, (ln, LANES), 1)
    lane1 = lax.broadcasted_iota(I32, (1, LANES), 1)
    pre = misc_ref[...] + dtb_ref[...]
    dt_s = jnp.where(lane < SSD_HEADS, _softplus(pre), 0.0)
    a_neg = jnp.where(lane1 < SSD_HEADS, -jnp.exp(alog_ref[...]), 0.0)
    ri = lax.broadcasted_iota(I32, (ln, ln), 0)
    ci = lax.broadcasted_iota(I32, (ln, ln), 1)
    tril = ci <= ri
    acum = jnp.dot(tril.astype(F32), dt_s * a_neg, preferred_element_type=F32, precision=HI)
    both_e = _dot_01(jnp.concatenate([dt_s, acum], axis=0), e_ref[...], 3)
    dt_e, acum_e = both_e[0:ln], both_e[ln:2 * ln]
    return dict(pre=pre, dt_s=dt_s, a_neg=a_neg, tril=tril, ri=ri, ci=ci, acum=acum, acum_t=acum.T,
                dt_e=dt_e, eac_e=jnp.exp(acum_e), del_e=jnp.exp(acum_e[ln - 1:ln, :] - acum_e))


def _dot_01(x, m01, terms):
    acc, rest = None, x
    for k in range(terms):
        part = rest.astype(BF16)
        if k + 1 < terms:
            rest = rest - part.astype(F32)
        d = jnp.dot(part, m01, preferred_element_type=F32)
        acc = d if acc is None else acc + d
    return acc


def _decay(cm, h):
    seg = cm["acum"][:, h:h + 1] - cm["acum_t"][h:h + 1, :]
    return jnp.exp(jnp.where(cm["tril"], seg, -jnp.inf))


def ssd_fwd(xs, bm, cm_, misc, z, dtb, alog, dskip_e, norm_w, e_mat, name):
    b, s, _ = xs.shape
    ln, nc = CHUNK, s // CHUNK
    gw = D_SSD // SSD_GROUPS
    hpg = SSD_HEADS // SSD_GROUPS

    def body(xs_ref, b_ref, c_ref, misc_ref, z_ref, dtb_ref, alog_ref, dsk_ref, nw_ref, e_ref,
             ys_ref, y_ref, p_ref, st, yd):
        @pl.when(pl.program_id(1) == 0)
        def _():
            st[...] = jnp.zeros_like(st)
        cm = _ssd_common(misc_ref, dtb_ref, alog_ref, e_ref)
        xsv = xs_ref[...]
        xdt = xsv * cm["dt_e"]
        xdt_b = xdt.astype(BF16)
        xd_b = (xdt * cm["del_e"]).astype(BF16)
        gam_e = cm["eac_e"][ln - 1:ln, :]
        p_ref[...] = st[...]
        groups = [slice(gw * g, gw * (g + 1)) for g in range(SSD_GROUPS)]
        heads = [slice(SSD_HEAD_DIM * h, SSD_HEAD_DIM * (h + 1)) for h in range(SSD_HEADS)]
        bgs = [b_ref[:, SSD_STATE * g:SSD_STATE * (g + 1)].astype(BF16) for g in range(SSD_GROUPS)]
        cgs = [c_ref[:, SSD_STATE * g:SSD_STATE * (g + 1)].astype(BF16) for g in range(SSD_GROUPS)]
        cbs = [lax.dot_general(cg, bg, NT_DIMS, preferred_element_type=F32) for cg, bg in zip(cgs, bgs)]
        sts = [st[:, gs] for gs in groups]
        yoff = [jnp.dot(cg, st_g.astype(BF16), preferred_element_type=F32) * cm["eac_e"][:, gs]
                for cg, st_g, gs in zip(cgs, sts, groups)]
        news = [lax.dot_general(bg, xd_b[:, gs], TN_DIMS, preferred_element_type=F32) for bg, gs in zip(bgs, groups)]
        for gs, st_g, new in zip(groups, sts, news):
            st[:, gs] = st_g * gam_e[:, gs] + new
        ms = [(cbs[h // hpg] * _decay(cm, h)).astype(BF16) for h in range(SSD_HEADS)]
        for h, hs in enumerate(heads):
            yd[:, hs] = jnp.dot(ms[h], xdt_b[:, hs], preferred_element_type=F32)
        y = yd[...] + jnp.concatenate(yoff, axis=1) + dsk_ref[...] * xsv
        y_ref[...] = y
        zz = z_ref[...]
        yg = y * (zz * _sigmoid(zz))
        outs = []
        for g in range(SSD_GROUPS):
            ygg = yg[:, gw * g:gw * (g + 1)]
            outs.append(ygg * _rms(ygg) * nw_ref[:, gw * g:gw * (g + 1)])
        ys_ref[...] = jnp.concatenate(outs, axis=1).astype(BF16)

    row = lambda d: pl.BlockSpec((None, ln, d), lambda bb, c: (bb, c, 0))
    return pl.pallas_call(
        body, name=name, grid=(b, nc),
        in_specs=[row(D_SSD), row(256), row(256), row(LANES), row(D_SSD), _full((1, LANES)), _full((1, LANES)),
                  _full((1, D_SSD)), _full((1, D_SSD)), _full((LANES, D_SSD))],
        out_specs=[row(D_SSD), row(D_SSD), pl.BlockSpec((None, None, SSD_STATE, D_SSD), lambda bb, c: (bb, c, 0, 0))],
        out_shape=[SDS((b, s, D_SSD), BF16), SDS((b, s, D_SSD), F32), SDS((b, nc, SSD_STATE, D_SSD), F32)],
        scratch_shapes=[pltpu.VMEM((SSD_STATE, D_SSD), F32), pltpu.VMEM((ln, D_SSD), F32)],
        compiler_params=_cparams(2))(xs, bm, cm_, misc, z, dtb, alog, dskip_e, norm_w, e_mat)


def ssd_bwd(dys, y, z, xs, bm, cm_, misc, prev, dtb, alog, dskip_e, norm_w, e_mat, et_mat, name):
    b, s, _ = xs.shape
    ln, nc = CHUNK, s // CHUNK
    gw = D_SSD // SSD_GROUPS
    hpg = SSD_HEADS // SSD_GROUPS

    def body(dys_ref, y_ref, z_ref, xs_ref, b_ref, c_ref, misc_ref, p_ref, dtb_ref, alog_ref, dsk_ref, nw_ref,
             e_ref, et_ref, dxs_ref, db_ref, dc_ref, dz_ref, ddt_ref, dnw_ref, ddsk_ref, ddtb_ref, dalog_ref,
             dst, dxd, dac_t):
        @pl.when(_first_step())
        def _():
            for r_ in (dnw_ref, ddsk_ref, ddtb_ref, dalog_ref):
                r_[...] = jnp.zeros_like(r_)

        @pl.when(pl.program_id(1) == 0)
        def _():
            dst[...] = jnp.zeros_like(dst)

        cm = _ssd_common(misc_ref, dtb_ref, alog_ref, e_ref)
        et = et_ref[...]
        squeeze = lambda t: _dot_01(t, et, 2)
        lane = lax.broadcasted_iota(I32, (ln, LANES), 1)
        sub = lax.broadcasted_iota(I32, (LANES, ln), 0)
        xsv = xs_ref[...]
        xdt = xsv * cm["dt_e"]
        xdt_b = xdt.astype(BF16)
        xd_b = (xdt * cm["del_e"]).astype(BF16)
        eac_e = cm["eac_e"]
        gam_e = eac_e[ln - 1:ln, :]

        yv, zz, dyo = y_ref[...], z_ref[...], dys_ref[...]
        sz = _sigmoid(zz)
        silu_z = zz * sz
        yg = yv * silu_z
        dyg, dnw = [], []
        for g in range(SSD_GROUPS):
            gs = slice(gw * g, gw * (g + 1))
            ygg = yg[:, gs]
            r = _rms(ygg)
            n = ygg * r
            dnw.append(jnp.sum(dyo[:, gs] * n, axis=0, keepdims=True))
            dyg.append(_rms_bwd(dyo[:, gs] * nw_ref[:, gs], n, r))
        dyg = jnp.concatenate(dyg, axis=1)
        dnw_ref[...] += jnp.concatenate(dnw, axis=1)
        dz_ref[...] = (dyg * yv * (sz * (1.0 + zz * (1.0 - sz)))).astype(BF16)
        dy = dyg * silu_z
        ddsk_ref[...] += jnp.sum(dy * xsv, axis=0, keepdims=True)
        dy_b = dy.astype(BF16)

        dacum = jnp.zeros((ln, LANES), F32)
        dac_t[...] = jnp.zeros_like(dac_t)
        w1, dgam = [], []
        for g in range(SSD_GROUPS):
            gs = slice(gw * g, gw * (g + 1))
            ss = slice(SSD_STATE * g, SSD_STATE * (g + 1))
            bg = b_ref[:, ss].astype(BF16)
            cg = c_ref[:, ss].astype(BF16)
            cb = lax.dot_general(cg, bg, NT_DIMS, preferred_element_type=F32)
            pt = p_ref[:, gs]
            pt_b = pt.astype(BF16)
            dst_g = dst[:, gs]
            dst_b = dst_g.astype(BF16)
            edy = (dy[:, gs] * eac_e[:, gs]).astype(BF16)
            dcg = lax.dot_general(edy, pt_b, NT_DIMS, preferred_element_type=F32)
            dpt = lax.dot_general(cg, edy, TN_DIMS, preferred_element_type=F32)
            yoff = jnp.dot(cg, pt_b, preferred_element_type=F32) * eac_e[:, gs]
            dxd_g = jnp.dot(bg, dst_b, preferred_element_type=F32)
            dbg = lax.dot_general(xd_b[:, gs], dst_b, NT_DIMS, preferred_element_type=F32)
            ddel = dxd_g * xdt[:, gs] * cm["del_e"][:, gs]
            w1.append(dy[:, gs] * yoff - ddel)
            dgam.append(jnp.sum(ddel, axis=0, keepdims=True) + jnp.sum(dst_g * pt, axis=0, keepdims=True) * gam_e[:, gs])
            dxd[:, gs] = dxd_g * cm["del_e"][:, gs]
            dst[:, gs] = dst_g * gam_e[:, gs] + dpt
            dcb = jnp.zeros((ln, ln), F32)
            for j in range(hpg):
                h = hpg * g + j
                hs = slice(SSD_HEAD_DIM * h, SSD_HEAD_DIM * (h + 1))
                lam = _decay(cm, h)
                m = cb * lam
                dm = lax.dot_general(dy_b[:, hs], xdt_b[:, hs], NT_DIMS, preferred_element_type=F32)
                dxd[:, hs] += lax.dot_general(m.astype(BF16), dy_b[:, hs], TN_DIMS, preferred_element_type=F32)
                dcb += dm * lam
                wl = dm * m
                dacum += jnp.where(lane == h, jnp.sum(wl, axis=1, keepdims=True), 0.0)
                dac_t[...] -= jnp.where(sub == h, jnp.sum(wl, axis=0, keepdims=True), 0.0)
            dcb_b = dcb.astype(BF16)
            dc_ref[:, ss] = dcg + jnp.dot(dcb_b, bg, preferred_element_type=F32)
            db_ref[:, ss] = dbg + lax.dot_general(dcb_b, cg, TN_DIMS, preferred_element_type=F32)

        dxdt = dxd[...]
        dxs_ref[...] = dy * dsk_ref[...] + dxdt * cm["dt_e"]
        dacum += squeeze(jnp.concatenate(w1, axis=1)) + dac_t[...].T
        dlast = squeeze(jnp.broadcast_to(jnp.concatenate(dgam, axis=1), (8, D_SSD)))[0:1, :]
        dacum += jnp.where(lax.broadcasted_iota(I32, (ln, LANES), 0) == ln - 1, dlast, 0.0)
        triu = (cm["ci"] >= cm["ri"]).astype(F32)
        da = jnp.dot(triu, dacum, preferred_element_type=F32, precision=HI)
        ddt = da * cm["a_neg"] + squeeze(dxdt * xsv)
        dalog_ref[...] += jnp.sum(da * cm["dt_s"], axis=0, keepdims=True) * cm["a_neg"]
        ddt_raw = jnp.where(lane < SSD_HEADS, ddt * _sigmoid(cm["pre"]), 0.0)
        ddt_ref[...] = ddt_raw
        ddtb_ref[...] += jnp.sum(ddt_raw, axis=0, keepdims=True)

    row = lambda d: pl.BlockSpec((None, ln, d), lambda bb, c: (bb, nc - 1 - c, 0))
    return pl.pallas_call(
        body, name=name, grid=(b, nc),
        in_specs=[row(D_SSD), row(D_SSD), row(D_SSD), row(D_SSD), row(256), row(256), row(LANES),
                  pl.BlockSpec((None, None, SSD_STATE, D_SSD), lambda bb, c: (bb, nc - 1 - c, 0, 0)),
                  _full((1, LANES)), _full((1, LANES)), _full((1, D_SSD)), _full((1, D_SSD)),
                  _full((LANES, D_SSD)), _full((D_SSD, LANES))],
        out_specs=[row(D_SSD), row(256), row(256), row(D_SSD), row(LANES),
                   _full((1, D_SSD)), _full((1, D_SSD)), _full((1, LANES)), _full((1, LANES))],
        out_shape=[SDS((b, s, D_SSD), F32), SDS((b, s, 256), F32), SDS((b, s, 256), F32), SDS((b, s, D_SSD), BF16),
                   SDS((b, s, LANES), F32), SDS((1, D_SSD), F32), SDS((1, D_SSD), F32), SDS((1, LANES), F32),
                   SDS((1, LANES), F32)],
        scratch_shapes=[pltpu.VMEM((SSD_STATE, D_SSD), F32), pltpu.VMEM((ln, D_SSD), F32), pltpu.VMEM((LANES, ln), F32)],
        compiler_params=_cparams(2))(dys, y, z, xs, bm, cm_, misc, prev, dtb, alog, dskip_e, norm_w, e_mat, et_mat)


def _rope(xv, cc, sp, sm):
    n = xv.shape[1]
    return xv * cc + pltpu.roll(xv, 16, 1) * sp + pltpu.roll(xv, n - 16, 1) * sm


def _rope_bwd(dy, cc, sp, sm):
    n = dy.shape[1]
    return dy * cc + pltpu.roll(dy * sp, n - 16, 1) + pltpu.roll(dy * sm, 16, 1)


def _tile8(t):
    return jnp.concatenate([t] * MLA_HEADS, axis=1)


def qkv_fwd(cq, ckv, misc, cc, sp, sm, qnw, kvnw, wuq_t, wukv_t, place, name):
    b, s, _ = cq.shape
    tm = min(512, s)
    hd = MLA_HEADS * HEAD_PAD

    def body(cq_ref, ckv_ref, misc_ref, cc_ref, sp_ref, sm_ref, qnw_ref, kvnw_ref, wq_ref, wkv_ref, pl_ref,
             q_ref, k_ref, v_ref, qn_ref, kvn_ref):
        cqv, ckvv = cq_ref[...], ckv_ref[...]
        qn = (cqv * _rms(cqv) * qnw_ref[...]).astype(BF16)
        kvn = (ckvv * _rms(ckvv) * kvnw_ref[...]).astype(BF16)
        qn_ref[...] = qn
        kvn_ref[...] = kvn
        cc1, sp1, sm1 = cc_ref[...], sp_ref[...], sm_ref[...]
        q = lax.dot_general(qn, wq_ref[...], NT_DIMS, preferred_element_type=F32)
        q_ref[...] = _rope(q, _tile8(cc1), _tile8(sp1), _tile8(sm1)).astype(BF16)
        kv = lax.dot_general(kvn, wkv_ref[...], NT_DIMS, preferred_element_type=F32)
        kr = jnp.dot(misc_ref[...], pl_ref[...], preferred_element_type=F32, precision=HI)
        kr = _rope(kr, cc1, sp1, sm1)
        k_ref[...] = (kv[:, 0:hd] + _tile8(kr)).astype(BF16)
        v_ref[...] = kv[:, hd:2 * hd].astype(BF16)

    return pl.pallas_call(
        body, name=name, grid=(b, s // tm),
        in_specs=[_row(tm, Q_LORA), _row(tm, KV_LORA), _row(tm, LANES), _row(tm, LANES), _row(tm, LANES), _row(tm, LANES),
                  _full((1, Q_LORA)), _full((1, KV_LORA)), _full(wuq_t.shape), _full(wukv_t.shape), _full((LANES, LANES))],
        out_specs=[_row(tm, hd), _row(tm, hd), _row(tm, hd), _row(tm, Q_LORA), _row(tm, KV_LORA)],
        out_shape=[SDS((b, s, hd), BF16)] * 3 + [SDS((b, s, Q_LORA), BF16), SDS((b, s, KV_LORA), BF16)],
        compiler_params=_cparams(2))(cq, ckv, misc, cc, sp, sm, qnw, kvnw, wuq_t, wukv_t, place)


def qkv_bwd(dq, dk, dv, ddt, cq, ckv, cc, sp, sm, qnw, kvnw, wuq_t, wukv_t, place_t, name):
    b, s, _ = cq.shape
    tm = min(512, s)
    hd = MLA_HEADS * HEAD_PAD

    def body(dq_ref, dk_ref, dv_ref, ddt_ref, cq_ref, ckv_ref, cc_ref, sp_ref, sm_ref, qnw_ref, kvnw_ref,
             wq_ref, wkv_ref, plt_ref, dcq_ref, dckv_ref, dmisc_ref, dqp_ref, dkv_ref, dqnw_ref, dkvnw_ref):
        @pl.when(_first_step())
        def _():
            dqnw_ref[...] = jnp.zeros_like(dqnw_ref)
            dkvnw_ref[...] = jnp.zeros_like(dkvnw_ref)
        cc1, sp1, sm1 = cc_ref[...], sp_ref[...], sm_ref[...]
        dqp = _rope_bwd(dq_ref[...].astype(F32), _tile8(cc1), _tile8(sp1), _tile8(sm1)).astype(BF16)
        dqp_ref[...] = dqp
        dkv_b = jnp.concatenate([dk_ref[...], dv_ref[...]], axis=1)
        dkf = dk_ref[...].astype(F32)
        dkv_ref[...] = dkv_b
        dkr = dkf[:, 0:HEAD_PAD]
        for h in range(1, MLA_HEADS):
            dkr += dkf[:, HEAD_PAD * h:HEAD_PAD * (h + 1)]
        dkr = _rope_bwd(dkr, cc1, sp1, sm1)
        dmisc_ref[...] = (jnp.dot(dkr, plt_ref[...], preferred_element_type=F32, precision=HI) + ddt_ref[...]).astype(BF16)

        def norm_bwd(dn_w, xv, w_ref, dw_ref, dx_ref):
            r = _rms(xv)
            n = xv * r
            dw_ref[...] += jnp.sum(dn_w * n, axis=0, keepdims=True)
            dx_ref[...] = _rms_bwd(dn_w * w_ref[...], n, r).astype(BF16)

        norm_bwd(jnp.dot(dqp, wq_ref[...], preferred_element_type=F32), cq_ref[...], qnw_ref, dqnw_ref, dcq_ref)
        norm_bwd(jnp.dot(dkv_b, wkv_ref[...], preferred_element_type=F32), ckv_ref[...], kvnw_ref, dkvnw_ref, dckv_ref)

    return pl.pallas_call(
        body, name=name, grid=(b, s // tm),
        in_specs=[_row(tm, hd), _row(tm, hd), _row(tm, hd), _row(tm, LANES), _row(tm, Q_LORA), _row(tm, KV_LORA),
                  _row(tm, LANES), _row(tm, LANES), _row(tm, LANES), _full((1, Q_LORA)), _full((1, KV_LORA)),
                  _full(wuq_t.shape), _full(wukv_t.shape), _full((LANES, LANES))],
        out_specs=[_row(tm, Q_LORA), _row(tm, KV_LORA), _row(tm, LANES), _row(tm, hd), _row(tm, 2 * hd),
                   _full((1, Q_LORA)), _full((1, KV_LORA))],
        out_shape=[SDS((b, s, Q_LORA), BF16), SDS((b, s, KV_LORA), BF16), SDS((b, s, LANES), BF16),
                   SDS((b, s, hd), BF16), SDS((b, s, 2 * hd), BF16), SDS((1, Q_LORA), F32), SDS((1, KV_LORA), F32)],
        compiler_params=_cparams(2))(dq, dk, dv, ddt, cq, ckv, cc, sp, sm, qnw, kvnw, wuq_t, wukv_t, place_t)


ATT_SCALE = 1.0 / math.sqrt(QK_DIM)
LOG2E = math.log2(math.e)
ATT_SCALE_LOG2E = ATT_SCALE * LOG2E


ATT_HEADS_PER_STEP = 4
ATT_HEADS_PER_STEP_BWD = 2


def _att_tile(s):
    return min(512, s)


def flash_fwd(q, k, v, name):
    b, s, hd = q.shape
    t = _att_tile(s)
    nb = s // t
    th = t // 2
    vt = v.reshape(b, nb, t, MLA_HEADS, HEAD_PAD).transpose(0, 3, 1, 4, 2)

    hps = ATT_HEADS_PER_STEP
    hw = hps * HEAD_PAD

    def body(q_ref, k_ref, vt_ref, o_ref, lse_ref, m_s, l_s, acc):
        i = pl.program_id(2)
        m_s[...] = jnp.full_like(m_s, -jnp.inf)
        l_s[...] = jnp.zeros_like(l_s)
        acc[...] = jnp.zeros_like(acc)

        def update(j, diagonal):
            ks = pl.ds(pl.multiple_of(j * t, t), t)
            chains = [(hh, half) for hh in range(hps) for half in range(2)]
            lanes = lambda hh: slice(HEAD_PAD * hh, HEAD_PAD * (hh + 1))
            cols = lambda half: slice(th * half, th * (half + 1))
            sts = {}
            for hh, half in chains:
                st = lax.dot_general(k_ref[ks, lanes(hh)], q_ref[cols(half), lanes(hh)], NT_DIMS,
                                     preferred_element_type=F32)
                if diagonal:
                    row = lax.broadcasted_iota(I32, (t, th), 0)
                    col = lax.broadcasted_iota(I32, (t, th), 1) + th * half
                    st = jnp.where(row <= col, st, -jnp.inf)
                sts[hh, half] = st
            pts, alphas = {}, {}
            for hh, half in chains:
                st, cs = sts[hh, half], cols(half)
                m_prev = m_s[hh, :, cs]
                m_new = jnp.maximum(m_prev, jnp.max(st, axis=0, keepdims=True))
                alpha = jnp.exp2((m_prev - m_new) * ATT_SCALE_LOG2E)
                pt = jnp.exp2((st - m_new) * ATT_SCALE_LOG2E)
                l_s[hh, :, cs] = alpha * l_s[hh, :, cs] + jnp.sum(pt, axis=0, keepdims=True)
                m_s[hh, :, cs] = m_new
                pts[hh, half], alphas[hh, half] = pt.astype(BF16), alpha
            for hh, half in chains:
                cs = cols(half)
                acc[hh, :, cs] = alphas[hh, half] * acc[hh, :, cs] + jnp.dot(vt_ref[hh, j], pts[hh, half],
                                                                             preferred_element_type=F32)

        def step(j, carry):
            update(j, False)
            return carry

        lax.fori_loop(0, i, step, 0)
        update(i, True)
        for hh in range(hps):
            o_ref[:, HEAD_PAD * hh:HEAD_PAD * (hh + 1)] = (acc[hh] / l_s[hh]).T
            lse_ref[hh] = m_s[hh] * ATT_SCALE + jnp.log(l_s[hh])

    qs = pl.BlockSpec((None, t, hw), lambda bb, h, i: (bb, i, h))
    ks = pl.BlockSpec((None, s, hw), lambda bb, h, i: (bb, 0, h))
    vs = pl.BlockSpec((None, hps, nb, HEAD_PAD, t), lambda bb, h, i: (bb, h, 0, 0, 0))
    ls = pl.BlockSpec((None, hps, None, 1, t), lambda bb, h, i: (bb, h, i, 0, 0))
    return pl.pallas_call(
        body, name=name, grid=(b, MLA_HEADS // hps, nb),
        in_specs=[qs, ks, vs], out_specs=[qs, ls],
        out_shape=[SDS((b, s, hd), F32), SDS((b, MLA_HEADS, nb, 1, t), F32)],
        scratch_shapes=[pltpu.VMEM((hps, 1, t), F32), pltpu.VMEM((hps, 1, t), F32), pltpu.VMEM((hps, HEAD_PAD, t), F32)],
        compiler_params=_cparams(3))(q, k, vt)


def flash_bwd(q, k, v, do, lse, dlt, name):
    b, s, hd = q.shape
    t = _att_tile(s)
    nb = s // t
    th = t // 2
    lse_r = lse
    dlt_r = dlt.reshape(b, MLA_HEADS, nb, 1, t)

    hps = ATT_HEADS_PER_STEP_BWD
    hw = hps * HEAD_PAD

    def body(q_ref, k_ref, v_ref, do_ref, lse_ref, dlt_ref, dq_ref, dk_ref, dv_ref, dq_s, dk_s, dv_s):
        dq_s[...] = jnp.zeros_like(dq_s)
        dk_s[...] = jnp.zeros_like(dk_s)
        dv_s[...] = jnp.zeros_like(dv_s)

        def tile(j, i, diagonal):
            qs = pl.ds(pl.multiple_of(i * t, t), t)
            chains = [(hh, half) for hh in range(hps) for half in range(2)]
            lanes = lambda hh: slice(HEAD_PAD * hh, HEAD_PAD * (hh + 1))
            keys = lambda half: pl.ds(pl.multiple_of(j * t + th * half, th), th)
            sts, dpts = {}, {}
            for hh, half in chains:
                ls_, ks = lanes(hh), keys(half)
                st = lax.dot_general(k_ref[ks, ls_], q_ref[qs, ls_], NT_DIMS, preferred_element_type=F32)
                if diagonal:
                    row = lax.broadcasted_iota(I32, (th, t), 0) + th * half
                    col = lax.broadcasted_iota(I32, (th, t), 1)
                    st = jnp.where(row <= col, st, -jnp.inf)
                sts[hh, half] = st
                dpts[hh, half] = lax.dot_general(v_ref[ks, ls_], do_ref[qs, ls_], NT_DIMS, preferred_element_type=F32)
            pts, dsts = {}, {}
            for hh, half in chains:
                pt = jnp.exp2(sts[hh, half] * ATT_SCALE_LOG2E - lse_ref[hh, i] * LOG2E)
                pts[hh, half] = pt.astype(BF16)
                dsts[hh, half] = (pt * (dpts[hh, half] - dlt_ref[hh, i])).astype(BF16)
            for hh in range(hps):
                ls_ = lanes(hh)
                dq_acc = None
                for half in range(2):
                    ks = keys(half)
                    dv_s[ks, ls_] += jnp.dot(pts[hh, half], do_ref[qs, ls_], preferred_element_type=F32)
                    dk_s[ks, ls_] += jnp.dot(dsts[hh, half], q_ref[qs, ls_], preferred_element_type=F32)
                    part = lax.dot_general(dsts[hh, half], k_ref[ks, ls_], TN_DIMS, preferred_element_type=F32)
                    dq_acc = part if dq_acc is None else dq_acc + part
                dq_s[qs, ls_] += dq_acc

        def key_tile(j, carry):
            tile(j, j, True)

            def query_tile(i, c2):
                tile(j, i, False)
                return c2

            lax.fori_loop(j + 1, nb, query_tile, 0)
            return carry

        lax.fori_loop(0, nb, key_tile, 0)
        dq_ref[...] = (dq_s[...] * ATT_SCALE).astype(BF16)
        dk_ref[...] = (dk_s[...] * ATT_SCALE).astype(BF16)
        dv_ref[...] = dv_s[...].astype(BF16)

    hs = pl.BlockSpec((None, s, hw), lambda bb, h: (bb, 0, h))
    ls = pl.BlockSpec((None, hps, nb, 1, t), lambda bb, h: (bb, h, 0, 0, 0))
    return pl.pallas_call(
        body, name=name, grid=(b, MLA_HEADS // hps),
        in_specs=[hs, hs, hs, hs, ls, ls], out_specs=[hs, hs, hs],
        out_shape=[SDS((b, s, hd), BF16)] * 3, scratch_shapes=[pltpu.VMEM((s, hw), F32)] * 3,
        compiler_params=_cparams(2))(q, k, v, do, lse_r, dlt_r)


def out_proj(ys, attn, mnw, wo, x, gate, above, name):
    b, s, d = x.shape
    tm = min(512, s)

    def body(ys_ref, at_ref, mnw_ref, wo_ref, x_ref, g_ref, nw_ref, sc_ref, sh_ref, xn_ref, o_ref, ym_ref, h_ref):
        av = at_ref[...]
        ym = (av * _rms(av) * mnw_ref[...]).astype(BF16)
        ym_ref[...] = ym
        o = jnp.dot(ys_ref[...], wo_ref[0:D_SSD, :], preferred_element_type=F32)
        o += jnp.dot(ym, wo_ref[D_SSD:2 * D_SSD, :], preferred_element_type=F32)
        xn = x_ref[...] + g_ref[...] * o
        xn_ref[...] = xn
        o_ref[...] = o.astype(BF16)
        h_ref[...] = _norm_mod_tile(xn, nw_ref, sc_ref, sh_ref)

    return pl.pallas_call(
        body, name=name, grid=(b, s // tm),
        in_specs=[_row(tm, D_SSD), _row(tm, D_SSD), _full((1, D_SSD)), _full(wo.shape), _row(tm, d), _bvec(d),
                  _full((1, d)), _bvec(d), _bvec(d)],
        out_specs=[_row(tm, d), _row(tm, d), _row(tm, D_SSD), _row(tm, d)],
        out_shape=[SDS((b, s, d), F32), SDS((b, s, d), BF16), SDS((b, s, D_SSD), BF16), SDS((b, s, d), BF16)],
        compiler_params=_cparams(2))(ys, attn, mnw, wo, x, gate, *above)


def out_proj_bwd(dout, attn, mnw, wo, name):
    b, s, d = dout.shape
    tm = min(512, s)

    def body(do_ref, at_ref, mnw_ref, wo_ref, dys_ref, dat_ref, dlt_ref, dw_ref):
        lane = lax.broadcasted_iota(I32, (tm, LANES), 1)
        @pl.when(_first_step())
        def _():
            dw_ref[...] = jnp.zeros_like(dw_ref)
        dov = do_ref[...]
        dys_ref[...] = lax.dot_general(dov, wo_ref[0:D_SSD, :], NT_DIMS, preferred_element_type=F32)
        dym = lax.dot_general(dov, wo_ref[D_SSD:2 * D_SSD, :], NT_DIMS, preferred_element_type=F32)
        av = at_ref[...]
        r = _rms(av)
        n = av * r
        dw_ref[...] += jnp.sum(dym * n, axis=0, keepdims=True)
        dat = _rms_bwd(dym * mnw_ref[...], n, r)
        dat_ref[...] = dat.astype(BF16)
        prod = dat * av
        cols = jnp.zeros((tm, LANES), F32)
        for h in range(MLA_HEADS):
            cols += jnp.where(lane == h, jnp.sum(prod[:, HEAD_PAD * h:HEAD_PAD * (h + 1)], axis=1, keepdims=True), 0.0)
        dlt_ref[...] = cols.T[0:MLA_HEADS, :]

    return pl.pallas_call(
        body, name=name, grid=(b, s // tm),
        in_specs=[_row(tm, d), _row(tm, D_SSD), _full((1, D_SSD)), _full(wo.shape)],
        out_specs=[_row(tm, D_SSD), _row(tm, D_SSD),
                   pl.BlockSpec((None, MLA_HEADS, tm), lambda bb, i: (bb, 0, i)), _full((1, D_SSD))],
        out_shape=[SDS((b, s, D_SSD), F32), SDS((b, s, D_SSD), BF16), SDS((b, MLA_HEADS, s), F32),
                   SDS((1, D_SSD), F32)],
        compiler_params=_cparams(2))(dout, attn, mnw, wo)


def adaln_fwd(c_all, w_ada, b_ada, name):
    nb, d = c_all.shape
    n = w_ada.shape[1]

    def body(c_ref, w_ref, b_ref, m_ref, ca_ref):
        cv = c_ref[...]
        ca = (cv * _sigmoid(cv)).astype(BF16)
        ca_ref[...] = ca
        m_ref[...] = jnp.dot(ca, w_ref[...].astype(BF16), preferred_element_type=F32) + b_ref[...]

    return pl.pallas_call(
        body, name=name, out_shape=[SDS((nb, n), F32), SDS((nb, d), BF16)],
        compiler_params=pltpu.CompilerParams(vmem_limit_bytes=VMEM_LIMIT))(c_all, w_ada, b_ada)


def adaln_bwd(c_act, dmod_cols, name):
    d, n = c_act.shape[1], dmod_cols.shape[1]

    def body(c_ref, dm_ref, gw_ref):
        gw_ref[...] = lax.dot_general(c_ref[...], dm_ref[...].astype(BF16), TN_DIMS, preferred_element_type=F32)

    return pl.pallas_call(
        body, name=name, out_shape=SDS((d, n), F32),
        compiler_params=pltpu.CompilerParams(vmem_limit_bytes=VMEM_LIMIT))(c_act, dmod_cols)


def sum_rows(x, name):
    def body(x_ref, o_ref):
        o_ref[...] = jnp.sum(x_ref[...], axis=0, keepdims=True)
    return pl.pallas_call(body, name=name, out_shape=SDS((1, x.shape[1]), F32))(x)


def squeeze_heads(x, et_mat, name):
    def body(x_ref, et_ref, o_ref):
        xv = jnp.broadcast_to(x_ref[...], (8, x.shape[1]))
        o_ref[...] = _dot_01(xv, et_ref[...], 3)[0:1, :]
    return pl.pallas_call(body, name=name, out_shape=SDS((1, LANES), F32))(x, et_mat)


def sum_blocks(x, name):
    n, r, c = x.shape
    tr = next(cand for cand in (256, 128, 64, 32, 16, 8) if r % cand == 0)

    def body(x_ref, o_ref):
        acc = x_ref[0].astype(F32)
        for k in range(1, n):
            acc += x_ref[k].astype(F32)
        o_ref[...] = acc

    return pl.pallas_call(
        body, name=name, grid=(r // tr,), in_specs=[pl.BlockSpec((n, tr, c), lambda i: (0, i, 0))],
        out_specs=pl.BlockSpec((tr, c), lambda i: (i, 0)), out_shape=SDS((r, c), F32),
        compiler_params=_cparams(1))(x)


def _adam_math(w, g, m, v):
    m = ADAM_B1 * m + (1.0 - ADAM_B1) * g
    v = ADAM_B2 * v + (1.0 - ADAM_B2) * (g * g)
    m_hat = m / (1.0 - ADAM_B1 ** ADAM_STEP)
    v_hat = v / (1.0 - ADAM_B2 ** ADAM_STEP)
    return -ADAM_LR * (m_hat / (jnp.sqrt(v_hat) + ADAM_EPS) + ADAM_WD * w), m, v


def adamw(w, g, m, v, name):
    r, c = w.shape[-2:]
    tr = r
    for cand in (512, 256, 128, 64, 32, 16, 8):
        if r % cand == 0 and cand * c * 4 <= 2 * 1024 * 1024:
            tr = cand
            break

    def body(w_ref, g_ref, m_ref, v_ref, d_ref, mo_ref, vo_ref):
        d_ref[...], mo_ref[...], vo_ref[...] = _adam_math(w_ref[...], g_ref[...], m_ref[...], v_ref[...])

    def spec(a):
        return pl.BlockSpec((tr, c), lambda i: (i, 0)) if a.ndim == 2 else pl.BlockSpec((None, tr, c), lambda i: (0, i, 0))

    return pl.pallas_call(
        body, name=name, grid=(r // tr,), in_specs=[spec(w), spec(g), spec(m), spec(v)], out_specs=[spec(w)] * 3,
        out_shape=[SDS(w.shape, F32)] * 3, compiler_params=_cparams(1))(w, g, m, v)


def adamw_many(ws, gs, ms, vs, name):
    n = len(ws)

    def body(*refs):
        w_r, g_r, m_r, v_r = (refs[k * n:(k + 1) * n] for k in range(4))
        d_r, mo_r, vo_r = (refs[(4 + k) * n:(5 + k) * n] for k in range(3))
        for k in range(n):
            d_r[k][...], mo_r[k][...], vo_r[k][...] = _adam_math(w_r[k][...], g_r[k][...], m_r[k][...], v_r[k][...])

    shapes = [SDS(w.shape, F32) for w in ws]
    outs = pl.pallas_call(body, name=name, out_shape=shapes * 3)(*ws, *gs, *ms, *vs)
    return outs[:n], outs[n:2 * n], outs[2 * n:]


PACK = {"ffn1_w_gate": (352, 352), "ffn1_w_up": (352, 352), "ffn1_w_down": (352, 352),
        "ffn2_w_gate": (352, 352), "ffn2_w_up": (352, 352), "ffn2_w_down": (352, 352),
        "w_out": (256, 256), "w_in": (406, 416), "w_ukv": (48, 48), "w_uq": (36, 48)}
TRANSPOSED = ("ffn1_w_gate", "ffn1_w_up", "ffn2_w_gate", "ffn2_w_up", "w_in", "w_ukv", "w_uq")
GATHER_GROUPS = (("ffn1_w_gate", "ffn1_w_up"), ("ffn1_w_down",),
                 ("w_in", "w_ukv", "w_uq", "w_out", "ffn2_w_gate", "ffn2_w_up", "ffn2_w_down"))
GRAD_GROUPS = (("ffn2", ("ffn2_w_gate", "ffn2_w_up", "ffn2_w_down")), ("mixer", ("w_out", "w_in", "w_ukv", "w_uq")),
               ("ffn1_down", ("ffn1_w_down",)), ("ffn1_gate", ("ffn1_w_gate",)), ("ffn1_up", ("ffn1_w_up",)))


def _pack_offsets(names):
    off, o = {}, 0
    for n in names:
        off[n] = (o, PACK[n][0])
        o += PACK[n][1]
    return off, o


def _shard_to_rows(name, w):
    w = w[0]
    if name in TRANSPOSED:
        w = w.T
    return w.reshape(-1, D_MODEL)


def _rows_to_shard(name, rows, like):
    shp = like.shape[1:]
    if name in TRANSPOSED:
        return rows.reshape(shp[1], shp[0]).T[None]
    return rows.reshape(shp)[None]


def _pack_shards(ws, names, dtype):
    parts = []
    for name in names:
        real, padded = PACK[name]
        rows = _shard_to_rows(name, ws[name]).astype(dtype)
        if padded > real:
            rows = jnp.pad(rows, ((0, padded - real), (0, 0)))
        parts.append(rows)
    return jnp.concatenate(parts, axis=0)


def _grad_rows(name, gw):
    real, padded = PACK[name]
    if name == "w_in":
        rows = _in_proj_rows_inv(gw).reshape(N_DEV, -1, D_MODEL)
    elif name == "w_ukv":
        hd = MLA_HEADS * HEAD_PAD
        rows = jnp.concatenate([gw[:hd].reshape(MLA_HEADS, HEAD_PAD, KV_LORA)[:, :QK_NOPE],
                                gw[hd:].reshape(MLA_HEADS, V_HEAD, KV_LORA)], axis=1).reshape(N_DEV, -1, D_MODEL)
    elif name == "w_uq":
        rows = gw.reshape(MLA_HEADS, HEAD_PAD, Q_LORA)[:, :QK_DIM].reshape(N_DEV, -1, D_MODEL)
    else:
        rows = gw.reshape(N_DEV, -1, D_MODEL)
    if padded > real:
        rows = jnp.pad(rows, ((0, 0), (0, padded - real), (0, 0)))
    return rows


def _pack_rows(arrs):
    parts = []
    for a in arrs:
        flat = a.reshape(-1).astype(F32)
        pad = (-flat.shape[0]) % D_MODEL
        if pad:
            flat = jnp.pad(flat, (0, pad))
        parts.append(flat.reshape(-1, D_MODEL))
    out = jnp.concatenate(parts, axis=0)
    pad = (-out.shape[0]) % 8
    if pad:
        out = jnp.pad(out, ((0, pad), (0, 0)))
    return out


def _unpack_rows(packed, shapes):
    out, row = [], 0
    for shp in shapes:
        n = math.prod(shp)
        nrow = -(-n // D_MODEL)
        out.append(packed[row:row + nrow].reshape(-1)[:n].reshape(shp))
        row += nrow
    return out


def _in_proj_rows(w_t):
    return jnp.concatenate([w_t[0:2560], w_t[2576:2960], w_t[2960:3216], w_t[2560:2576], w_t[3216:3248],
                            jnp.zeros((D_IN_PAD - D_IN, D_MODEL), w_t.dtype)], axis=0)


def _in_proj_rows_inv(d):
    return jnp.concatenate([d[0:2560], d[3200:3216], d[2560:2944], d[2944:3200], d[3216:3248]], axis=0)


def _rope_tables(positions):
    inv_freq = ROPE_THETA ** (-jnp.arange(0, QK_ROPE, 2, dtype=F32) / QK_ROPE)
    ang = positions[..., None].astype(F32) * inv_freq
    cos, sin = jnp.cos(ang), jnp.sin(ang)
    one = jnp.ones(ang.shape[:2] + (QK_NOPE,), F32)
    zero = jnp.zeros_like(one)
    z16, z32, o32 = zero[..., :16], zero[..., :32], one[..., :32]
    cc = jnp.concatenate([one, cos, cos, o32], axis=-1)
    sp = jnp.concatenate([zero, z16, sin, z32], axis=-1)
    sm = jnp.concatenate([zero, -sin, z16, z32], axis=-1)
    return cc, sp, sm


def weight_views(gathered):
    def _seg(name):
        names, g = next((names, g) for names, g in zip(GATHER_GROUPS, gathered) if name in names)
        o, r = _pack_offsets(names)[0][name]
        return g[:, o:o + r]

    full = lambda name: _seg(name).reshape(-1, D_MODEL)
    ukv = _seg("w_ukv").reshape(MLA_HEADS, QK_NOPE + V_HEAD, KV_LORA)
    wukv_t = jnp.concatenate([jnp.pad(ukv[:, :QK_NOPE], ((0, 0), (0, HEAD_PAD - QK_NOPE), (0, 0))).reshape(-1, KV_LORA),
                              ukv[:, QK_NOPE:].reshape(-1, KV_LORA)], axis=0)
    uq = _seg("w_uq").reshape(MLA_HEADS, QK_DIM, Q_LORA)
    wuq_t = jnp.pad(uq, ((0, 0), (0, HEAD_PAD - QK_DIM), (0, 0))).reshape(-1, Q_LORA)
    return dict(wg1_t=full("ffn1_w_gate"), wu1_t=full("ffn1_w_up"), wd1=full("ffn1_w_down"),
                wg2_t=full("ffn2_w_gate"), wu2_t=full("ffn2_w_up"), wd2=full("ffn2_w_down"),
                wo=full("w_out"), win_t=_in_proj_rows(full("w_in")), wukv_t=wukv_t, wuq_t=wuq_t)


def _ffn_bwd(tag, dxn, do, dgate, x, h, gg, uu, a, sc, norm_w, wg_t, wu_t, wd, below):
    f2 = wd.shape[0] // 2
    dwd = mm_tn(a, do, f2, D_MODEL, tag + "_dwd")
    dgg, duu = ffn_dact(do, wd, gg, uu, tag + "_dact")
    dwg_t = mm_tn(dgg, h, f2, D_MODEL, tag + "_dwg")
    dwu_t = mm_tn(duu, h, f2, D_MODEL, tag + "_dwu")
    dx, dsc, dsh, dnw, *nxt = dh_norm_bwd([dgg, duu], [wg_t, wu_t], x, dxn, norm_w, sc, tag + "_dh", below)
    return dx, (dsh, dsc, dgate), dnw, (dwg_t, dwu_t, dwd), nxt


def local_step(x, tgt, positions, mod, wv, p):
    nb, s, d = x.shape
    sh1, sc1, g1, sh2, sc2, g2, sh3, sc3, g3 = mod
    cc, sp, sm = _rope_tables(positions)
    lane_head = jnp.arange(D_SSD, dtype=I32)[None, :] // SSD_HEAD_DIM
    e_mat = (lane_head == jnp.arange(LANES, dtype=I32)[:, None]).astype(BF16)
    et_mat = e_mat.T
    rr, cl = jnp.arange(LANES, dtype=I32)[:, None], jnp.arange(LANES, dtype=I32)[None, :]
    place = ((cl == rr + (QK_NOPE - SSD_HEADS)) & (rr >= SSD_HEADS) & (rr < SSD_HEADS + QK_ROPE)).astype(F32)
    dtb = jnp.pad(p["dt_bias"], ((0, 0), (0, LANES - SSD_HEADS)))
    alog = jnp.pad(p["a_log"], ((0, 0), (0, LANES - SSD_HEADS)))
    dskip_e = jnp.repeat(p["d_skip"], SSD_HEAD_DIM, axis=1)

    h1 = norm_mod(x, p["norm_ffn1"], sc1, sh1, "ffn1_norm")
    gg1, uu1, a1 = ffn_up(h1, wv["wg1_t"], wv["wu1_t"], "ffn1_up")
    x1, o1, h2 = ffn_down(a1, wv["wd1"], x, g1, 0.5, "ffn1_down", (p["norm_mix"], sc2, sh2))
    z, u, cq, ckv, misc = in_proj(h2, wv["win_t"], "in_proj")
    xs, bm, cm_ = conv_fwd(u, p["conv_w"], p["conv_b"], "conv_fwd")
    ys, y, prev = ssd_fwd(xs, bm, cm_, misc, z, dtb, alog, dskip_e, p["ssd_norm_w"], e_mat, "ssd_fwd")
    q, k, v, qn, kvn = qkv_fwd(cq, ckv, misc, cc, sp, sm, p["q_norm_w"], p["kv_norm_w"], wv["wuq_t"], wv["wukv_t"],
                               place, "qkv_fwd")
    attn, lse = flash_fwd(q, k, v, "flash_fwd")
    x2, o2, ym, h3 = out_proj(ys, attn, p["mla_norm_w"], wv["wo"], x1, g2, (p["norm_ffn2"], sc3, sh3), "out_proj")
    gg3, uu3, a3 = ffn_up(h3, wv["wg2_t"], wv["wu2_t"], "ffn2_up")
    loss, dx3, dnfin, do3, dg3 = ffn_down_final(a3, wv["wd2"], x2, g3, 0.5, p["norm_final"], tgt, "ffn2_down_loss")

    dx2, dmod3, dnf2, (dwg2, dwu2, dwd2), (dout, dg2) = _ffn_bwd(
        "ffn2", dx3, do3, dg3, x2, h3, gg3, uu3, a3, sc3, p["norm_ffn2"], wv["wg2_t"], wv["wu2_t"], wv["wd2"],
        (o2, g2, 1.0))
    dys, dattn, dlt, dmlan = out_proj_bwd(dout, attn, p["mla_norm_w"], wv["wo"], "out_proj_bwd")
    dwo = jnp.concatenate([mm_tn(ys, dout, D_SSD, D_MODEL, "dwo_ssd"), mm_tn(ym, dout, D_SSD, D_MODEL, "dwo_mla")], axis=0)
    dxs, dbm, dcm, dz, ddt, dssdn, ddsk_lane, ddtb, dalog = ssd_bwd(
        dys, y, z, xs, bm, cm_, misc, prev, dtb, alog, dskip_e, p["ssd_norm_w"], e_mat, et_mat, "ssd_bwd")
    dq, dk, dv = flash_bwd(q, k, v, dattn, lse, dlt, "flash_bwd")
    dcq, dckv, dmisc, dqp, dkvc, dqn, dkvn = qkv_bwd(dq, dk, dv, ddt, cq, ckv, cc, sp, sm, p["q_norm_w"], p["kv_norm_w"],
                                                     wv["wuq_t"], wv["wukv_t"], place.T, "qkv_bwd")
    dwuq = mm_tn(dqp, qn, MLA_HEADS * HEAD_PAD, Q_LORA, "dwuq")
    dwukv = mm_tn(dkvc, kvn, MLA_HEADS * HEAD_PAD, KV_LORA, "dwukv")
    dvv, dconv = conv_bwd_a(dxs, dbm, dcm, u, p["conv_w"], p["conv_b"], "conv_bwd_a")
    dconv = conv_grads_fold(dconv, "conv_grads_fold")
    du = conv_bwd_b(dvv, p["conv_w"], "conv_bwd_b")
    dproj = jnp.concatenate([dz, du, dcq, dckv, dmisc], axis=-1)
    dwin = mm_tn(dproj, h2, D_IN_PAD // 2, D_MODEL, "dwin")
    dx1, dsc2, dsh2, dnmix, do1, dg1 = dh_norm_bwd([dproj], [wv["win_t"]], x1, dx2, p["norm_mix"], sc2, "mix_dh",
                                                   (o1, g1, 0.5))
    dx0, dmod1, dnf1, (dwg1, dwu1, dwd1), _ = _ffn_bwd(
        "ffn1", dx1, do1, dg1, x, h1, gg1, uu1, a1, sc1, p["norm_ffn1"], wv["wg1_t"], wv["wu1_t"], wv["wd1"], None)

    dmod = jnp.concatenate([*dmod1, dsh2, dsc2, dg2, *dmod3], axis=1).reshape(nb, N_MOD * d)
    return dict(
        loss=loss, dx=dx0, dmod=dmod, norm_ffn1=dnf1, norm_mix=dnmix, norm_ffn2=dnf2, norm_final=dnfin,
        ssd_norm_w=dssdn, mla_norm_w=dmlan, q_norm_w=dqn, kv_norm_w=dkvn,
        dt_bias=ddtb[:, :SSD_HEADS], a_log=dalog[:, :SSD_HEADS],
        d_skip=squeeze_heads(ddsk_lane, et_mat, "d_skip_heads")[:, :SSD_HEADS],
        conv_b=dconv[4:5], conv_w=dconv[0:4],
        gw=dict(ffn1_w_gate=dwg1, ffn1_w_up=dwu1, ffn1_w_down=dwd1, ffn2_w_gate=dwg2, ffn2_w_up=dwu2, ffn2_w_down=dwd2,
                w_out=dwo, w_in=dwin, w_ukv=dwukv, w_uq=dwuq))


def kernel(x, c, positions, w_ada, b_ada, norm_ffn1, ffn1_w_gate, ffn1_w_up, ffn1_w_down, norm_mix, w_in, conv_w, conv_b, dt_bias, a_log, d_skip, ssd_norm_w, q_norm_w, w_uq, kv_norm_w, w_ukv, mla_norm_w, w_out, norm_ffn2, ffn2_w_gate, ffn2_w_up, ffn2_w_down, norm_final, loss_target, m_w_ada, m_b_ada, m_norm_ffn1, m_ffn1_w_gate, m_ffn1_w_up, m_ffn1_w_down, m_norm_mix, m_w_in, m_conv_w, m_conv_b, m_dt_bias, m_a_log, m_d_skip, m_ssd_norm_w, m_q_norm_w, m_w_uq, m_kv_norm_w, m_w_ukv, m_mla_norm_w, m_w_out, m_norm_ffn2, m_ffn2_w_gate, m_ffn2_w_up, m_ffn2_w_down, m_norm_final, v_w_ada, v_b_ada, v_norm_ffn1, v_ffn1_w_gate, v_ffn1_w_up, v_ffn1_w_down, v_norm_mix, v_w_in, v_conv_w, v_conv_b, v_dt_bias, v_a_log, v_d_skip, v_ssd_norm_w, v_q_norm_w, v_w_uq, v_kv_norm_w, v_w_ukv, v_mla_norm_w, v_w_out, v_norm_ffn2, v_ffn2_w_gate, v_ffn2_w_up, v_ffn2_w_down, v_norm_final):
    names = ["w_ada", "b_ada", "norm_ffn1", "ffn1_w_gate", "ffn1_w_up", "ffn1_w_down", "norm_mix", "w_in", "conv_w",
             "conv_b", "dt_bias", "a_log", "d_skip", "ssd_norm_w", "q_norm_w", "w_uq", "kv_norm_w", "w_ukv",
             "mla_norm_w", "w_out", "norm_ffn2", "ffn2_w_gate", "ffn2_w_up", "ffn2_w_down", "norm_final"]
    W = dict(zip(names, (w_ada, b_ada, norm_ffn1, ffn1_w_gate, ffn1_w_up, ffn1_w_down, norm_mix, w_in, conv_w, conv_b, dt_bias, a_log, d_skip, ssd_norm_w, q_norm_w, w_uq, kv_norm_w, w_ukv, mla_norm_w, w_out, norm_ffn2, ffn2_w_gate, ffn2_w_up, ffn2_w_down, norm_final)))
    M = dict(zip(names, (m_w_ada, m_b_ada, m_norm_ffn1, m_ffn1_w_gate, m_ffn1_w_up, m_ffn1_w_down, m_norm_mix, m_w_in, m_conv_w, m_conv_b, m_dt_bias, m_a_log, m_d_skip, m_ssd_norm_w, m_q_norm_w, m_w_uq, m_kv_norm_w, m_w_ukv, m_mla_norm_w, m_w_out, m_norm_ffn2, m_ffn2_w_gate, m_ffn2_w_up, m_ffn2_w_down, m_norm_final)))
    V = dict(zip(names, (v_w_ada, v_b_ada, v_norm_ffn1, v_ffn1_w_gate, v_ffn1_w_up, v_ffn1_w_down, v_norm_mix, v_w_in, v_conv_w, v_conv_b, v_dt_bias, v_a_log, v_d_skip, v_ssd_norm_w, v_q_norm_w, v_w_uq, v_kv_norm_w, v_w_ukv, v_mla_norm_w, v_w_out, v_norm_ffn2, v_ffn2_w_gate, v_ffn2_w_up, v_ffn2_w_down, v_norm_final)))

    nb, s, d = x.shape
    me = 4 * lax.axis_index("x") + 2 * lax.axis_index("y") + lax.axis_index("c")
    n_ada = w_ada.shape[2]

    taps, n_cw = conv_w.shape[1:]
    cg = all_gather8(_pack_rows([c, conv_w[0]]), "gather_c")
    c_all = cg[:, 0:nb].reshape(N_DEV * nb, d)
    conv_w_full = cg[:, nb, 0:taps * n_cw].reshape(N_DEV, taps, n_cw).transpose(1, 0, 2).reshape(taps, N_DEV * n_cw)
    g_ffn1 = all_gather8(_pack_shards(W, GATHER_GROUPS[0], BF16), "gather_w_ffn1")

    b_ada_cols = lax.dynamic_slice(b_ada, (0, me * n_ada), (1, n_ada))
    mod_cols, c_act = adaln_fwd(c_all, w_ada[0], b_ada_cols, "adaln_fwd")
    mod_g = all_gather8(mod_cols, "gather_mod")
    g_ffn1, mod_g, down1, rest = lax.optimization_barrier(
        (g_ffn1, mod_g, _pack_shards(W, GATHER_GROUPS[1], BF16), _pack_shards(W, GATHER_GROUPS[2], BF16)))
    g_down1 = sc_all_gather8(down1, "gather_w_ffn1_down", 1)
    wv = weight_views((g_ffn1, g_down1, sc_all_gather8(rest, "gather_w_rest", 7)))
    mod = lax.dynamic_slice(mod_g, (0, me * nb, 0), (N_DEV, nb, n_ada)).transpose(1, 0, 2).reshape(nb, N_MOD, 1, d)
    mod = [mod[:, k] for k in range(N_MOD)]

    P = dict(W)
    P["conv_w"] = conv_w_full
    P["norm_final"] = norm_final.reshape(1, d)
    R = local_step(x, loss_target, positions, mod, wv, P)

    dmod = R["dmod"]
    partial_shapes = [(1,), (1, d), (1, d), (1, d), (1, d), (1, d), (1, d), (1, Q_LORA), (1, KV_LORA),
                      (1, SSD_HEADS), (1, SSD_HEADS), (1, SSD_HEADS), (1, D_CONV), (4, D_CONV), (1, N_MOD * d),
                      (nb, N_MOD * d)]
    partial = _pack_rows([R["loss"][0, :1], R["norm_ffn1"], R["norm_mix"], R["norm_ffn2"], R["norm_final"],
                          R["ssd_norm_w"], R["mla_norm_w"], R["q_norm_w"], R["kv_norm_w"],
                          R["dt_bias"], R["a_log"], R["d_skip"], R["conv_b"], R["conv_w"],
                          sum_rows(dmod, "dmod_rows"), dmod])
    partial_g = all_gather8(partial, "gather_partials")
    (loss, g_nf1, g_nmix, g_nf2, g_nfin, g_ssdn, g_mlan, g_qn, g_kvn, g_dtb, g_alog, g_dskip, g_convb, g_convw,
     g_bada, _) = _unpack_rows(sum_blocks(partial_g, "sum_partials"), partial_shapes)
    dmod_row = sum(-(-math.prod(shp) // D_MODEL) for shp in partial_shapes[:-1])
    dmod_all = partial_g[:, dmod_row:dmod_row + nb * N_MOD].reshape(N_DEV * nb, N_MOD * d)
    g_wada = adaln_bwd(c_act, lax.dynamic_slice(dmod_all, (0, me * n_ada), (N_DEV * nb, n_ada)), "adaln_bwd")
    n_cw = conv_w.shape[2]
    G = {"w_ada": g_wada[None], "b_ada": g_bada, "norm_ffn1": g_nf1, "norm_mix": g_nmix, "norm_ffn2": g_nf2,
         "norm_final": g_nfin.reshape(d), "ssd_norm_w": g_ssdn, "mla_norm_w": g_mlan, "q_norm_w": g_qn,
         "kv_norm_w": g_kvn, "dt_bias": g_dtb, "a_log": g_alog, "d_skip": g_dskip, "conv_b": g_convb,
         "conv_w": lax.dynamic_slice(g_convw, (0, me * n_cw), (4, n_cw))[None]}

    DW, NM, NV = {}, {}, {}
    gw = R["gw"]
    for k, (tag, group) in enumerate(GRAD_GROUPS):
        send = jnp.concatenate([_grad_rows(name, gw[name]) for name in group], axis=1).astype(BF16)
        recv = sc_all_to_all8(send, "exchange_" + tag, 2 + k)
        gsum = sum_blocks(recv, "sum_" + tag)
        for name, (o, r) in _pack_offsets(group)[0].items():
            G[name] = _rows_to_shard(name, gsum[o:o + r], W[name])
            DW[name], NM[name], NV[name] = adamw(W[name], G[name], M[name], V[name], "adamw_" + name)
    DW["w_ada"], NM["w_ada"], NV["w_ada"] = adamw(w_ada, g_wada, m_w_ada, v_w_ada, "adamw_w_ada")
    small = [n for n in names if n not in DW]
    as2d = lambda a: a.reshape(-1, a.shape[-1])
    outs = adamw_many([as2d(W[n]) for n in small], [as2d(G[n]) for n in small], [as2d(M[n]) for n in small],
                      [as2d(V[n]) for n in small], "adamw_small")
    for res, dst in zip(outs, (DW, NM, NV)):
        for n, t in zip(small, res):
            dst[n] = t.reshape(W[n].shape)
    return (loss.reshape(()), R["dx"], *[G[n] for n in names], *[DW[n] for n in names], *[NM[n] for n in names],
            *[NV[n] for n in names])
```

```python
import math

import jax
import jax.numpy as jnp
from jax import lax
from jax.experimental import pallas as pl
from jax.experimental.pallas import tpu as pltpu
from jax.experimental.pallas import tpu_sc as plsc

F32, BF16, I32 = jnp.float32, jnp.bfloat16, jnp.int32
HI = lax.Precision.HIGHEST
SDS = jax.ShapeDtypeStruct
MESH = pl.DeviceIdType.MESH

D_MODEL = 1024
D_FF = 2816
D_SSD = 1024
SSD_HEADS = 16
SSD_HEAD_DIM = 64
SSD_GROUPS = 2
SSD_STATE = 128
CHUNK = 128
MLA_HEADS = 8
QK_NOPE = 64
QK_ROPE = 32
QK_DIM = 96
V_HEAD = 128
Q_LORA = 384
KV_LORA = 256
ROPE_THETA = 10000.0
N_MOD = 9
EPS = 1e-6
D_CONV = 1536
D_IN = 3248
D_IN_PAD = 3328
HEAD_PAD = 128
N_DEV = 8
ADAM_LR, ADAM_B1, ADAM_B2, ADAM_EPS, ADAM_WD, ADAM_STEP = 0.001, 0.9, 0.999, 1e-08, 0.01, 10

SAVED_ACT = BF16
VMEM_LIMIT = 56 * 1024 * 1024
LANES = 128
NT_DIMS = (((1,), (1,)), ((), ()))
TN_DIMS = (((0,), (0,)), ((), ()))


def _cparams(n_axes):
    return pltpu.CompilerParams(dimension_semantics=("arbitrary",) * n_axes, vmem_limit_bytes=VMEM_LIMIT)


def _row(tm, d):
    return pl.BlockSpec((None, tm, d), lambda b, i: (b, i, 0))


def _bvec(d):
    return pl.BlockSpec((None, 1, d), lambda b, i: (b, 0, 0))


def _full(shape):
    n = len(shape)
    return pl.BlockSpec(shape, lambda *_: (0,) * n)


def _sigmoid(x):
    return 1.0 / (1.0 + jnp.exp(-x))


def _softplus(x):
    return jnp.maximum(x, 0.0) + jnp.log(1.0 + jnp.exp(-jnp.abs(x)))


def _rms(x):
    return lax.rsqrt(jnp.mean(x * x, axis=-1, keepdims=True) + EPS)


def _rms_bwd(dn, n, r):
    return r * (dn - n * jnp.mean(dn * n, axis=-1, keepdims=True))


def _first_step():
    return (pl.program_id(0) == 0) & (pl.program_id(1) == 0)


def all_gather8(x, name):
    r, c = x.shape

    def body(x_ref, out_ref, send_sems, recv_sems, local_sem):
        mx, my, mc = lax.axis_index("x"), lax.axis_index("y"), lax.axis_index("c")
        me, sibling = (mx, my, mc), (mx, my, 1 - mc)
        chips = [(1 - mx, my), (mx, 1 - my), (1 - mx, 1 - my)]

        def rows(px, py, pc):
            return out_ref.at[4 * px + 2 * py + pc]

        def copy(k, block, to, src=None):
            return pltpu.make_async_remote_copy(
                src_ref=rows(*block) if src is None else src, dst_ref=rows(*block),
                send_sem=send_sems.at[k], recv_sem=recv_sems.at[k], device_id=to, device_id_type=MESH)

        mine = pltpu.make_async_copy(x_ref, rows(*me), local_sem)
        mine.start()
        first = [copy(0, me, sibling, src=x_ref)]
        first += [copy(1 + j, me, (*chip, mc), src=x_ref) for j, chip in enumerate(chips)]
        for cp in first:
            cp.start()
        passed = [copy(4 + j, (*chip, mc), sibling) for j, chip in enumerate(chips)]
        for j, chip in enumerate(chips):
            copy(1 + j, (*chip, mc), me).wait_recv()
            passed[j].start()
        copy(0, sibling, me).wait_recv()
        for j, chip in enumerate(chips):
            copy(4 + j, (*chip, 1 - mc), me).wait_recv()
        for cp in first + passed:
            cp.wait_send()
        mine.wait()

    return pl.pallas_call(
        body, name=name,
        out_shape=SDS((N_DEV, r, c), x.dtype),
        in_specs=[pl.BlockSpec(memory_space=pl.ANY)],
        out_specs=pl.BlockSpec(memory_space=pl.ANY),
        scratch_shapes=[pltpu.SemaphoreType.DMA((7,)), pltpu.SemaphoreType.DMA((7,)), pltpu.SemaphoreType.DMA],
    )(x)


def all_to_all8(x, name):
    _, r, c = x.shape

    def body(x_ref, out_ref, send_sems, recv_sems, local_sem):
        mx, my, mc = lax.axis_index("x"), lax.axis_index("y"), lax.axis_index("c")
        me = 4 * mx + 2 * my + mc
        mine = pltpu.make_async_copy(x_ref.at[me], out_ref.at[me], local_sem)
        mine.start()
        copies = []
        for rel in range(1, N_DEV):
            px = 1 - mx if rel & 4 else mx
            py = 1 - my if rel & 2 else my
            pc = 1 - mc if rel & 1 else mc
            cp = pltpu.make_async_remote_copy(
                src_ref=x_ref.at[4 * px + 2 * py + pc], dst_ref=out_ref.at[me],
                send_sem=send_sems.at[rel - 1], recv_sem=recv_sems.at[rel - 1],
                device_id=(px, py, pc), device_id_type=MESH)
            cp.start()
            copies.append(cp)
        for cp in copies:
            cp.wait()
        mine.wait()

    return pl.pallas_call(
        body, name=name,
        out_shape=SDS((N_DEV, r, c), x.dtype),
        in_specs=[pl.BlockSpec(memory_space=pl.ANY)],
        out_specs=pl.BlockSpec(memory_space=pl.ANY),
        scratch_shapes=[pltpu.SemaphoreType.DMA((7,)), pltpu.SemaphoreType.DMA((7,)), pltpu.SemaphoreType.DMA],
    )(x)


def _sequencer_kernel(name, collective_id):
    return pl.kernel(
        mesh=plsc.ScalarSubcoreMesh(axis_name="seq", num_cores=1), name=name,
        scratch_types=(pltpu.SemaphoreType.DMA((7,)), pltpu.SemaphoreType.DMA((7,)), pltpu.SemaphoreType.DMA),
        compiler_params=pltpu.CompilerParams(collective_id=collective_id))


def _handshake(peers):
    barrier = pltpu.get_barrier_semaphore()
    for peer in peers:
        pl.semaphore_signal(barrier, inc=1, device_id=peer, device_id_type=MESH)
    pl.semaphore_wait(barrier, len(peers))


def sc_all_gather8(x, name, collective_id):
    r, c = x.shape
    x_ref = jax.new_ref(x, memory_space=pltpu.MemorySpace.HBM)
    out_ref = jax.empty_ref(SDS((N_DEV, r, c), x.dtype), memory_space=pltpu.MemorySpace.HBM)

    @_sequencer_kernel(name, collective_id)
    def launch(send_sems, recv_sems, local_sem):
        mx, my, mc = lax.axis_index("x"), lax.axis_index("y"), lax.axis_index("c")
        me, sibling = (mx, my, mc), (mx, my, 1 - mc)
        chips = [(1 - mx, my), (mx, 1 - my), (1 - mx, 1 - my)]
        _handshake([sibling] + [(*chip, mc) for chip in chips])

        def rows(px, py, pc):
            return out_ref.at[4 * px + 2 * py + pc]

        def copy(k, block, to, src=None):
            return pltpu.make_async_remote_copy(
                src_ref=rows(*block) if src is None else src, dst_ref=rows(*block),
                send_sem=send_sems.at[k], recv_sem=recv_sems.at[k], device_id=to, device_id_type=MESH)

        mine = pltpu.make_async_copy(x_ref, rows(*me), local_sem)
        mine.start()
        first = [copy(0, me, sibling, src=x_ref)]
        first += [copy(1 + j, me, (*chip, mc), src=x_ref) for j, chip in enumerate(chips)]
        for cp in first:
            cp.start()
        passed = [copy(4 + j, (*chip, mc), sibling) for j, chip in enumerate(chips)]
        for j, chip in enumerate(chips):
            copy(1 + j, (*chip, mc), me).wait_recv()
            passed[j].start()
        copy(0, sibling, me).wait_recv()
        for j, chip in enumerate(chips):
            copy(4 + j, (*chip, 1 - mc), me).wait_recv()
        for cp in first + passed:
            cp.wait_send()
        mine.wait()

    launch()
    return out_ref[...]


def sc_all_to_all8(x, name, collective_id):
    x_ref = jax.new_ref(x, memory_space=pltpu.MemorySpace.HBM)
    out_ref = jax.empty_ref(SDS(x.shape, x.dtype), memory_space=pltpu.MemorySpace.HBM)

    @_sequencer_kernel(name, collective_id)
    def launch(send_sems, recv_sems, local_sem):
        mx, my, mc = lax.axis_index("x"), lax.axis_index("y"), lax.axis_index("c")
        me = 4 * mx + 2 * my + mc
        peers = [(1 - mx if rel & 4 else mx, 1 - my if rel & 2 else my, 1 - mc if rel & 1 else mc)
                 for rel in range(1, N_DEV)]
        _handshake(peers)
        mine = pltpu.make_async_copy(x_ref.at[me], out_ref.at[me], local_sem)
        mine.start()
        copies = []
        for k, (px, py, pc) in enumerate(peers):
            cp = pltpu.make_async_remote_copy(
                src_ref=x_ref.at[4 * px + 2 * py + pc], dst_ref=out_ref.at[me],
                send_sem=send_sems.at[k], recv_sem=recv_sems.at[k], device_id=(px, py, pc), device_id_type=MESH)
            cp.start()
            copies.append(cp)
        for cp in copies:
            cp.wait()
        mine.wait()

    launch()
    return out_ref[...]


def norm_mod(x, w, sc, sh, name):
    b, s, d = x.shape
    tm = min(512, s)

    def body(x_ref, w_ref, sc_ref, sh_ref, h_ref):
        xv = x_ref[...]
        n = xv * _rms(xv)
        h_ref[...] = ((n * w_ref[...]) * (1.0 + sc_ref[...]) + sh_ref[...]).astype(BF16)

    return pl.pallas_call(
        body, name=name, grid=(b, s // tm),
        in_specs=[_row(tm, d), _full((1, d)), _bvec(d), _bvec(d)],
        out_specs=_row(tm, d), out_shape=SDS((b, s, d), BF16), compiler_params=_cparams(2))(x, w, sc, sh)


def ffn_up(h, wg_t, wu_t, name):
    b, s, d = h.shape
    f = wg_t.shape[0]
    tm, tn = min(512, s), f // 2

    def body(h_ref, wg_ref, wu_ref, s_ref, t_ref, a_ref):
        hv = h_ref[...]
        g = lax.dot_general(hv, wg_ref[...], NT_DIMS, preferred_element_type=F32)
        u = lax.dot_general(hv, wu_ref[...], NT_DIMS, preferred_element_type=F32)
        sg = _sigmoid(g)
        silu = g * sg
        s_ref[...] = silu.astype(s_ref.dtype)
        t_ref[...] = (u * (sg + silu * (1.0 - sg))).astype(t_ref.dtype)
        a_ref[...] = (silu * u).astype(BF16)

    hs = pl.BlockSpec((None, tm, d), lambda j, bb, i: (bb, i, 0))
    ws = pl.BlockSpec((tn, d), lambda j, bb, i: (j, 0))
    os_ = pl.BlockSpec((None, tm, tn), lambda j, bb, i: (bb, i, j))
    return pl.pallas_call(
        body, name=name, grid=(f // tn, b, s // tm),
        in_specs=[hs, ws, ws], out_specs=[os_, os_, os_],
        out_shape=[SDS((b, s, f), SAVED_ACT), SDS((b, s, f), SAVED_ACT), SDS((b, s, f), BF16)],
        compiler_params=_cparams(3))(h, wg_t, wu_t)


def _norm_mod_tile(xv, w_ref, sc_ref, sh_ref):
    return ((xv * _rms(xv) * w_ref[...]) * (1.0 + sc_ref[...]) + sh_ref[...]).astype(BF16)


def ffn_down(a, wd, x, gate, scale, name, above=None):
    b, s, f = a.shape
    d = wd.shape[1]
    tm = min(512, s)

    def body(a_ref, wd_ref, x_ref, g_ref, *rest):
        xn_ref, o_ref = rest[-3:-1] if above else rest
        o = jnp.dot(a_ref[...], wd_ref[...], preferred_element_type=F32)
        xn = x_ref[...] + (scale * g_ref[...]) * o
        xn_ref[...] = xn
        o_ref[...] = o.astype(BF16)
        if above:
            rest[-1][...] = _norm_mod_tile(xn, *rest[0:3])

    extra = above is not None
    return pl.pallas_call(
        body, name=name, grid=(b, s // tm),
        in_specs=[_row(tm, f), _full((f, d)), _row(tm, d), _bvec(d)] + ([_full((1, d)), _bvec(d), _bvec(d)] if extra else []),
        out_specs=[_row(tm, d), _row(tm, d)] + ([_row(tm, d)] if extra else []),
        out_shape=[SDS((b, s, d), F32), SDS((b, s, d), BF16)] + ([SDS((b, s, d), BF16)] if extra else []),
        compiler_params=_cparams(2))(a, wd, x, gate, *(above or ()))


def ffn_down_final(a, wd, x, gate, scale, w_final, tgt, name):
    b, s, f = a.shape
    d = wd.shape[1]
    tm = min(512, s)

    def body(a_ref, wd_ref, x_ref, g_ref, w_ref, t_ref, loss_ref, dx_ref, dw_ref, do_ref, dg_ref):
        @pl.when(_first_step())
        def _():
            loss_ref[...] = jnp.zeros_like(loss_ref)
            dw_ref[...] = jnp.zeros_like(dw_ref)

        @pl.when(pl.program_id(1) == 0)
        def _():
            dg_ref[...] = jnp.zeros_like(dg_ref)
        o = jnp.dot(a_ref[...], wd_ref[...], preferred_element_type=F32)
        sg = scale * g_ref[...]
        xv = x_ref[...] + sg * o
        r = _rms(xv)
        n = xv * r
        wv = w_ref[...]
        e = n * wv - t_ref[...]
        loss_ref[...] += jnp.sum(e * e) * (0.5 / d)
        dy = e * (1.0 / d)
        dw_ref[...] += jnp.sum(dy * n, axis=0, keepdims=True)
        dx = _rms_bwd(dy * wv, n, r)
        dx_ref[...] = dx
        do_ref[...] = (sg * dx).astype(BF16)
        dg_ref[...] += jnp.sum(scale * dx * o, axis=0, keepdims=True)

    return pl.pallas_call(
        body, name=name, grid=(b, s // tm),
        in_specs=[_row(tm, f), _full((f, d)), _row(tm, d), _bvec(d), _full((1, d)), _row(tm, d)],
        out_specs=[_full((1, LANES)), _row(tm, d), _full((1, d)), _row(tm, d), _bvec(d)],
        out_shape=[SDS((1, LANES), F32), SDS((b, s, d), F32), SDS((1, d), F32), SDS((b, s, d), BF16), SDS((b, 1, d), F32)],
        compiler_params=_cparams(2))(a, wd, x, gate, w_final, tgt)


def ffn_dact(do, wd, silu_g, u_dsilu, name):
    b, s, d = do.shape
    f = wd.shape[0]
    tm, tn = min(512, s), f // 2

    def body(do_ref, wd_ref, s_ref, t_ref, dg_ref, du_ref):
        da = lax.dot_general(do_ref[...], wd_ref[...], NT_DIMS, preferred_element_type=F32)
        dg_ref[...] = (da * t_ref[...].astype(F32)).astype(BF16)
        du_ref[...] = (da * s_ref[...].astype(F32)).astype(BF16)

    dos = pl.BlockSpec((None, tm, d), lambda j, bb, i: (bb, i, 0))
    ws = pl.BlockSpec((tn, d), lambda j, bb, i: (j, 0))
    es = pl.BlockSpec((None, tm, tn), lambda j, bb, i: (bb, i, j))
    return pl.pallas_call(
        body, name=name, grid=(f // tn, b, s // tm),
        in_specs=[dos, ws, es, es], out_specs=[es, es],
        out_shape=[SDS((b, s, f), BF16), SDS((b, s, f), BF16)], compiler_params=_cparams(3))(do, wd, silu_g, u_dsilu)


def mm_tn(a, bm, tma, tnb, name):
    b, s, ka = a.shape
    nb = bm.shape[2]
    tk = min(2048, s)
    nk = s // tk

    def body(a_ref, b_ref, o_ref, acc):
        first = (pl.program_id(2) == 0) & (pl.program_id(3) == 0)
        last = (pl.program_id(2) == b - 1) & (pl.program_id(3) == nk - 1)
        part = lax.dot_general(a_ref[...], b_ref[...], TN_DIMS, preferred_element_type=F32)

        @pl.when(first)
        def _():
            acc[...] = part

        @pl.when(jnp.logical_not(first))
        def _():
            acc[...] += part

        @pl.when(last)
        def _():
            o_ref[...] = acc[...].astype(BF16)

    return pl.pallas_call(
        body, name=name, grid=(ka // tma, nb // tnb, b, nk),
        in_specs=[pl.BlockSpec((None, tk, tma), lambda i, j, bb, k: (bb, k, i)),
                  pl.BlockSpec((None, tk, tnb), lambda i, j, bb, k: (bb, k, j))],
        out_specs=pl.BlockSpec((tma, tnb), lambda i, j, bb, k: (i, j)),
        out_shape=SDS((ka, nb), BF16), scratch_shapes=[pltpu.VMEM((tma, tnb), F32)],
        compiler_params=_cparams(4))(a, bm)


def _gate_bwd_specs(tm, d, b, s):
    return ([_row(tm, d), _bvec(d)], [_row(tm, d), _bvec(d)], [SDS((b, s, d), BF16), SDS((b, 1, d), F32)])


def _gate_bwd_tile(dx, scale, o_ref, g_ref, do_ref, dg_ref):
    do_ref[...] = ((scale * g_ref[...]) * dx).astype(BF16)
    dg_ref[...] += jnp.sum(scale * dx * o_ref[...].astype(F32), axis=0, keepdims=True)


def n_in_bytes(arrs):
    return sum(a.size * a.dtype.itemsize for a in arrs)


def dh_norm_bwd(dys, wts, x, dxn, w, sc, name, below=None):
    b, s, d = x.shape
    tm = min(512 if n_in_bytes(wts) <= 8 * 1024 * 1024 else 256, s)
    n_in = len(dys)
    extra_in, extra_out, extra_shape = _gate_bwd_specs(tm, d, b, s) if below else ([], [], [])

    def body(*refs):
        dy_refs, w_refs = refs[:n_in], refs[n_in:2 * n_in]
        x_ref, dxn_ref, nw_ref, sc_ref = refs[2 * n_in:2 * n_in + 4]
        rest = refs[2 * n_in + 4:]
        if below:
            o_ref, g_ref, dx_ref, dsc_ref, dsh_ref, dw_ref, do_ref, dg_ref = rest
        else:
            dx_ref, dsc_ref, dsh_ref, dw_ref = rest

        @pl.when(pl.program_id(1) == 0)
        def _():
            dsc_ref[...] = jnp.zeros_like(dsc_ref)
            dsh_ref[...] = jnp.zeros_like(dsh_ref)
            if below:
                dg_ref[...] = jnp.zeros_like(dg_ref)

        @pl.when(_first_step())
        def _():
            dw_ref[...] = jnp.zeros_like(dw_ref)

        dh = jnp.dot(dy_refs[0][...], w_refs[0][...], preferred_element_type=F32)
        for k in range(1, n_in):
            dh += jnp.dot(dy_refs[k][...], w_refs[k][...], preferred_element_type=F32)
        xv = x_ref[...]
        r = _rms(xv)
        n = xv * r
        nw = nw_ref[...]
        dsc_ref[...] += jnp.sum(dh * (n * nw), axis=0, keepdims=True)
        dsh_ref[...] += jnp.sum(dh, axis=0, keepdims=True)
        dhn = dh * (1.0 + sc_ref[...])
        dw_ref[...] += jnp.sum(dhn * n, axis=0, keepdims=True)
        dx = dxn_ref[...] + _rms_bwd(dhn * nw, n, r)
        dx_ref[...] = dx
        if below:
            _gate_bwd_tile(dx, below[2], o_ref, g_ref, do_ref, dg_ref)

    in_specs = [_row(tm, dy.shape[2]) for dy in dys] + [_full(wt.shape) for wt in wts]
    in_specs += [_row(tm, d), _row(tm, d), _full((1, d)), _bvec(d)] + extra_in
    return pl.pallas_call(
        body, name=name, grid=(b, s // tm), in_specs=in_specs,
        out_specs=[_row(tm, d), _bvec(d), _bvec(d), _full((1, d))] + extra_out,
        out_shape=[SDS((b, s, d), F32), SDS((b, 1, d), F32), SDS((b, 1, d), F32), SDS((1, d), F32)] + extra_shape,
        compiler_params=_cparams(2))(*dys, *wts, x, dxn, w, sc, *(below[:2] if below else ()))


def in_proj(h, win_t, name):
    b, s, d = h.shape
    tm = min(512, s)
    widths = (D_SSD, D_SSD + 2 * SSD_GROUPS * SSD_STATE, Q_LORA, KV_LORA, LANES)

    def body(h_ref, w_ref, *outs):
        p = lax.dot_general(h_ref[...], w_ref[...], NT_DIMS, preferred_element_type=F32)
        off = 0
        for o_ref, wd in zip(outs, widths):
            o_ref[...] = p[:, off:off + wd]
            off += wd

    return pl.pallas_call(
        body, name=name, grid=(b, s // tm),
        in_specs=[_row(tm, d), _full(win_t.shape)],
        out_specs=[_row(tm, wd) for wd in widths],
        out_shape=[SDS((b, s, wd), F32) for wd in widths], compiler_params=_cparams(2))(h, win_t)


def _halo_prev(ts, d):
    return pl.BlockSpec((None, 8, d), lambda b, i: (b, jnp.maximum(i * (ts // 8) - 1, 0), 0))


CONV_ROWS = 32


def _conv_head(head, u_ref, up_ref):
    head[0:8, :] = jnp.where(pl.program_id(1) > 0, up_ref[...], 0.0)
    head[8:8 + CONV_ROWS, :] = u_ref[0:CONV_ROWS, :]


def _conv_windows(u_ref, head, r0):
    if r0 == 0:
        return [head[5 + k:5 + k + CONV_ROWS, :] for k in range(4)]
    return [u_ref[r0 - 3 + k:r0 - 3 + k + CONV_ROWS, :] for k in range(4)]


def _fold8(t):
    acc = t[0:8, :]
    for r in range(8, CONV_ROWS, 8):
        acc += t[r:r + 8, :]
    return acc


def conv_fwd(u, cw, cb, name):
    b, s, dc = u.shape
    ts = min(512, s)
    widths = (D_SSD, SSD_GROUPS * SSD_STATE, SSD_GROUPS * SSD_STATE)

    def body(u_ref, up_ref, w_ref, b_ref, xs_ref, bm_ref, cm_ref, head):
        _conv_head(head, u_ref, up_ref)
        ws = [w_ref[k:k + 1, :] for k in range(4)]
        bias = b_ref[...]
        for r0 in range(0, ts, CONV_ROWS):
            taps = _conv_windows(u_ref, head, r0)
            v = bias + taps[0] * ws[0] + taps[1] * ws[1] + taps[2] * ws[2] + taps[3] * ws[3]
            y = v * _sigmoid(v)
            rs = slice(r0, r0 + CONV_ROWS)
            xs_ref[rs, :] = y[:, 0:D_SSD]
            bm_ref[rs, :] = y[:, D_SSD:D_SSD + 256]
            cm_ref[rs, :] = y[:, D_SSD + 256:D_SSD + 512]

    return pl.pallas_call(
        body, name=name, grid=(b, s // ts),
        in_specs=[_row(ts, dc), _halo_prev(ts, dc), _full((4, dc)), _full((1, dc))],
        out_specs=[_row(ts, wd) for wd in widths],
        out_shape=[SDS((b, s, wd), F32) for wd in widths],
        scratch_shapes=[pltpu.VMEM((8 + CONV_ROWS, dc), F32)], compiler_params=_cparams(2))(u, u, cw, cb)


def conv_bwd_a(dxs, dbm, dcm, u, cw, cb, name):
    b, s, dc = u.shape
    ts = min(512, s)

    def body(dxs_ref, dbm_ref, dcm_ref, u_ref, up_ref, w_ref, b_ref, dv_ref, dwb_ref, head):
        @pl.when(_first_step())
        def _():
            dwb_ref[...] = jnp.zeros_like(dwb_ref)
        _conv_head(head, u_ref, up_ref)
        ws = [w_ref[k:k + 1, :] for k in range(4)]
        bias = b_ref[...]
        for r0 in range(0, ts, CONV_ROWS):
            taps = _conv_windows(u_ref, head, r0)
            v = bias + taps[0] * ws[0] + taps[1] * ws[1] + taps[2] * ws[2] + taps[3] * ws[3]
            sg = _sigmoid(v)
            rs = slice(r0, r0 + CONV_ROWS)
            dy = jnp.concatenate([dxs_ref[rs, :], dbm_ref[rs, :], dcm_ref[rs, :]], axis=1)
            dv = dy * (sg * (1.0 + v * (1.0 - sg)))
            dv_ref[rs, :] = dv
            for k in range(4):
                dwb_ref[8 * k:8 * k + 8, :] += _fold8(dv * taps[k])
            dwb_ref[32:40, :] += _fold8(dv)

    return pl.pallas_call(
        body, name=name, grid=(b, s // ts),
        in_specs=[_row(ts, D_SSD), _row(ts, 256), _row(ts, 256), _row(ts, dc), _halo_prev(ts, dc),
                  _full((4, dc)), _full((1, dc))],
        out_specs=[_row(ts, dc), _full((40, dc))],
        out_shape=[SDS((b, s, dc), F32), SDS((40, dc), F32)],
        scratch_shapes=[pltpu.VMEM((8 + CONV_ROWS, dc), F32)], compiler_params=_cparams(2))(dxs, dbm, dcm, u, u, cw, cb)


def conv_grads_fold(x, name):
    c = x.shape[1]

    def body(x_ref, o_ref):
        o_ref[...] = jnp.zeros_like(o_ref)
        for k in range(5):
            o_ref[k:k + 1, :] = jnp.sum(x_ref[8 * k:8 * k + 8, :], axis=0, keepdims=True)

    return pl.pallas_call(body, name=name, out_shape=SDS((8, c), F32))(x)


def conv_bwd_b(dv, cw, name):
    b, s, dc = dv.shape
    ts = min(512, s)
    nt = s // ts

    def body(dv_ref, dn_ref, w_ref, du_ref, tail):
        tail[0:CONV_ROWS, :] = dv_ref[ts - CONV_ROWS:ts, :]
        tail[CONV_ROWS:CONV_ROWS + 8, :] = jnp.where(pl.program_id(1) < nt - 1, dn_ref[...], 0.0)
        ws = [w_ref[k:k + 1, :] for k in range(4)]
        for r0 in range(0, ts, CONV_ROWS):
            if r0 == ts - CONV_ROWS:
                win = [tail[3 - k:3 - k + CONV_ROWS, :] for k in range(4)]
            else:
                win = [dv_ref[r0 + 3 - k:r0 + 3 - k + CONV_ROWS, :] for k in range(4)]
            acc = win[0] * ws[0] + win[1] * ws[1] + win[2] * ws[2] + win[3] * ws[3]
            du_ref[r0:r0 + CONV_ROWS, :] = acc.astype(BF16)

    nxt = pl.BlockSpec((None, 8, dc), lambda bb, i: (bb, jnp.minimum((i + 1) * (ts // 8), s // 8 - 1), 0))
    return pl.pallas_call(
        body, name=name, grid=(b, nt),
        in_specs=[_row(ts, dc), nxt, _full((4, dc))],
        out_specs=_row(ts, dc), out_shape=SDS((b, s, dc), BF16),
        scratch_shapes=[pltpu.VMEM((CONV_ROWS + 8, dc), F32)], compiler_params=_cparams(2))(dv, dv, cw)


def _ssd_common(misc_ref, dtb_ref, alog_ref, e_ref):
    ln = CHUNK
    lane = lax.broadcasted_iota(I32, (ln, LANES), 1)
    lane1 = lax.broadcasted_iota(I32, (1, LANES), 1)
    pre = misc_ref[...] + dtb_ref[...]
    dt_s = jnp.where(lane < SSD_HEADS, _softplus(pre), 0.0)
    a_neg = jnp.where(lane1 < SSD_HEADS, -jnp.exp(alog_ref[...]), 0.0)
    ri = lax.broadcasted_iota(I32, (ln, ln), 0)
    ci = lax.broadcasted_iota(I32, (ln, ln), 1)
    tril = ci <= ri
    acum = jnp.dot(tril.astype(F32), dt_s * a_neg, preferred_element_type=F32, precision=HI)
    both_e = _dot_01(jnp.concatenate([dt_s, acum], axis=0), e_ref[...], 3)
    dt_e, acum_e = both_e[0:ln], both_e[ln:2 * ln]
    return dict(pre=pre, dt_s=dt_s, a_neg=a_neg, tril=tril, ri=ri, ci=ci, acum=acum, acum_t=acum.T,
                dt_e=dt_e, eac_e=jnp.exp(acum_e), del_e=jnp.exp(acum_e[ln - 1:ln, :] - acum_e))


def _dot_01(x, m01, terms):
    acc, rest = None, x
    for k in range(terms):
        part = rest.astype(BF16)
        if k + 1 < terms:
            rest = rest - part.astype(F32)
        d = jnp.dot(part, m01, preferred_element_type=F32)
        acc = d if acc is None else acc + d
    return acc


def _decay(cm, h):
    seg = cm["acum"][:, h:h + 1] - cm["acum_t"][h:h + 1, :]
    return jnp.exp(jnp.where(cm["tril"], seg, -jnp.inf))


def ssd_fwd(xs, bm, cm_, misc, z, dtb, alog, dskip_e, norm_w, e_mat, name):
    b, s, _ = xs.shape
    ln, nc = CHUNK, s // CHUNK
    gw = D_SSD // SSD_GROUPS
    hpg = SSD_HEADS // SSD_GROUPS

    def body(xs_ref, b_ref, c_ref, misc_ref, z_ref, dtb_ref, alog_ref, dsk_ref, nw_ref, e_ref,
             ys_ref, y_ref, p_ref, st, yd):
        @pl.when(pl.program_id(1) == 0)
        def _():
            st[...] = jnp.zeros_like(st)
        cm = _ssd_common(misc_ref, dtb_ref, alog_ref, e_ref)
        xsv = xs_ref[...]
        xdt = xsv * cm["dt_e"]
        xdt_b = xdt.astype(BF16)
        xd_b = (xdt * cm["del_e"]).astype(BF16)
        gam_e = cm["eac_e"][ln - 1:ln, :]
        p_ref[...] = st[...]
        groups = [slice(gw * g, gw * (g + 1)) for g in range(SSD_GROUPS)]
        heads = [slice(SSD_HEAD_DIM * h, SSD_HEAD_DIM * (h + 1)) for h in range(SSD_HEADS)]
        bgs = [b_ref[:, SSD_STATE * g:SSD_STATE * (g + 1)].astype(BF16) for g in range(SSD_GROUPS)]
        cgs = [c_ref[:, SSD_STATE * g:SSD_STATE * (g + 1)].astype(BF16) for g in range(SSD_GROUPS)]
        cbs = [lax.dot_general(cg, bg, NT_DIMS, preferred_element_type=F32) for cg, bg in zip(cgs, bgs)]
        sts = [st[:, gs] for gs in groups]
        yoff = [jnp.dot(cg, st_g.astype(BF16), preferred_element_type=F32) * cm["eac_e"][:, gs]
                for cg, st_g, gs in zip(cgs, sts, groups)]
        news = [lax.dot_general(bg, xd_b[:, gs], TN_DIMS, preferred_element_type=F32) for bg, gs in zip(bgs, groups)]
        for gs, st_g, new in zip(groups, sts, news):
            st[:, gs] = st_g * gam_e[:, gs] + new
        ms = [(cbs[h // hpg] * _decay(cm, h)).astype(BF16) for h in range(SSD_HEADS)]
        for h, hs in enumerate(heads):
            yd[:, hs] = jnp.dot(ms[h], xdt_b[:, hs], preferred_element_type=F32)
        y = yd[...] + jnp.concatenate(yoff, axis=1) + dsk_ref[...] * xsv
        y_ref[...] = y
        zz = z_ref[...]
        yg = y * (zz * _sigmoid(zz))
        outs = []
        for g in range(SSD_GROUPS):
            ygg = yg[:, gw * g:gw * (g + 1)]
            outs.append(ygg * _rms(ygg) * nw_ref[:, gw * g:gw * (g + 1)])
        ys_ref[...] = jnp.concatenate(outs, axis=1).astype(BF16)

    row = lambda d: pl.BlockSpec((None, ln, d), lambda bb, c: (bb, c, 0))
    return pl.pallas_call(
        body, name=name, grid=(b, nc),
        in_specs=[row(D_SSD), row(256), row(256), row(LANES), row(D_SSD), _full((1, LANES)), _full((1, LANES)),
                  _full((1, D_SSD)), _full((1, D_SSD)), _full((LANES, D_SSD))],
        out_specs=[row(D_SSD), row(D_SSD), pl.BlockSpec((None, None, SSD_STATE, D_SSD), lambda bb, c: (bb, c, 0, 0))],
        out_shape=[SDS((b, s, D_SSD), BF16), SDS((b, s, D_SSD), F32), SDS((b, nc, SSD_STATE, D_SSD), F32)],
        scratch_shapes=[pltpu.VMEM((SSD_STATE, D_SSD), F32), pltpu.VMEM((ln, D_SSD), F32)],
        compiler_params=_cparams(2))(xs, bm, cm_, misc, z, dtb, alog, dskip_e, norm_w, e_mat)


def ssd_bwd(dys, y, z, xs, bm, cm_, misc, prev, dtb, alog, dskip_e, norm_w, e_mat, et_mat, name):
    b, s, _ = xs.shape
    ln, nc = CHUNK, s // CHUNK
    gw = D_SSD // SSD_GROUPS
    hpg = SSD_HEADS // SSD_GROUPS

    def body(dys_ref, y_ref, z_ref, xs_ref, b_ref, c_ref, misc_ref, p_ref, dtb_ref, alog_ref, dsk_ref, nw_ref,
             e_ref, et_ref, dxs_ref, db_ref, dc_ref, dz_ref, ddt_ref, dnw_ref, ddsk_ref, ddtb_ref, dalog_ref,
             dst, dxd, dac_t):
        @pl.when(_first_step())
        def _():
            for r_ in (dnw_ref, ddsk_ref, ddtb_ref, dalog_ref):
                r_[...] = jnp.zeros_like(r_)

        @pl.when(pl.program_id(1) == 0)
        def _():
            dst[...] = jnp.zeros_like(dst)

        cm = _ssd_common(misc_ref, dtb_ref, alog_ref, e_ref)
        et = et_ref[...]
        squeeze = lambda t: _dot_01(t, et, 2)
        lane = lax.broadcasted_iota(I32, (ln, LANES), 1)
        sub = lax.broadcasted_iota(I32, (LANES, ln), 0)
        xsv = xs_ref[...]
        xdt = xsv * cm["dt_e"]
        xdt_b = xdt.astype(BF16)
        xd_b = (xdt * cm["del_e"]).astype(BF16)
        eac_e = cm["eac_e"]
        gam_e = eac_e[ln - 1:ln, :]

        yv, zz, dyo = y_ref[...], z_ref[...], dys_ref[...]
        sz = _sigmoid(zz)
        silu_z = zz * sz
        yg = yv * silu_z
        dyg, dnw = [], []
        for g in range(SSD_GROUPS):
            gs = slice(gw * g, gw * (g + 1))
            ygg = yg[:, gs]
            r = _rms(ygg)
            n = ygg * r
            dnw.append(jnp.sum(dyo[:, gs] * n, axis=0, keepdims=True))
            dyg.append(_rms_bwd(dyo[:, gs] * nw_ref[:, gs], n, r))
        dyg = jnp.concatenate(dyg, axis=1)
        dnw_ref[...] += jnp.concatenate(dnw, axis=1)
        dz_ref[...] = (dyg * yv * (sz * (1.0 + zz * (1.0 - sz)))).astype(BF16)
        dy = dyg * silu_z
        ddsk_ref[...] += jnp.sum(dy * xsv, axis=0, keepdims=True)
        dy_b = dy.astype(BF16)

        dacum = jnp.zeros((ln, LANES), F32)
        dac_t[...] = jnp.zeros_like(dac_t)
        w1, dgam = [], []
        for g in range(SSD_GROUPS):
            gs = slice(gw * g, gw * (g + 1))
            ss = slice(SSD_STATE * g, SSD_STATE * (g + 1))
            bg = b_ref[:, ss].astype(BF16)
            cg = c_ref[:, ss].astype(BF16)
            cb = lax.dot_general(cg, bg, NT_DIMS, preferred_element_type=F32)
            pt = p_ref[:, gs]
            pt_b = pt.astype(BF16)
            dst_g = dst[:, gs]
            dst_b = dst_g.astype(BF16)
            edy = (dy[:, gs] * eac_e[:, gs]).astype(BF16)
            dcg = lax.dot_general(edy, pt_b, NT_DIMS, preferred_element_type=F32)
            dpt = lax.dot_general(cg, edy, TN_DIMS, preferred_element_type=F32)
            yoff = jnp.dot(cg, pt_b, preferred_element_type=F32) * eac_e[:, gs]
            dxd_g = jnp.dot(bg, dst_b, preferred_element_type=F32)
            dbg = lax.dot_general(xd_b[:, gs], dst_b, NT_DIMS, preferred_element_type=F32)
            ddel = dxd_g * xdt[:, gs] * cm["del_e"][:, gs]
            w1.append(dy[:, gs] * yoff - ddel)
            dgam.append(jnp.sum(ddel, axis=0, keepdims=True) + jnp.sum(dst_g * pt, axis=0, keepdims=True) * gam_e[:, gs])
            dxd[:, gs] = dxd_g * cm["del_e"][:, gs]
            dst[:, gs] = dst_g * gam_e[:, gs] + dpt
            dcb = jnp.zeros((ln, ln), F32)
            for j in range(hpg):
                h = hpg * g + j
                hs = slice(SSD_HEAD_DIM * h, SSD_HEAD_DIM * (h + 1))
                lam = _decay(cm, h)
                m = cb * lam
                dm = lax.dot_general(dy_b[:, hs], xdt_b[:, hs], NT_DIMS, preferred_element_type=F32)
                dxd[:, hs] += lax.dot_general(m.astype(BF16), dy_b[:, hs], TN_DIMS, preferred_element_type=F32)
                dcb += dm * lam
                wl = dm * m
                dacum += jnp.where(lane == h, jnp.sum(wl, axis=1, keepdims=True), 0.0)
                dac_t[...] -= jnp.where(sub == h, jnp.sum(wl, axis=0, keepdims=True), 0.0)
            dcb_b = dcb.astype(BF16)
            dc_ref[:, ss] = dcg + jnp.dot(dcb_b, bg, preferred_element_type=F32)
            db_ref[:, ss] = dbg + lax.dot_general(dcb_b, cg, TN_DIMS, preferred_element_type=F32)

        dxdt = dxd[...]
        dxs_ref[...] = dy * dsk_ref[...] + dxdt * cm["dt_e"]
        dacum += squeeze(jnp.concatenate(w1, axis=1)) + dac_t[...].T
        dlast = squeeze(jnp.broadcast_to(jnp.concatenate(dgam, axis=1), (8, D_SSD)))[0:1, :]
        dacum += jnp.where(lax.broadcasted_iota(I32, (ln, LANES), 0) == ln - 1, dlast, 0.0)
        triu = (cm["ci"] >= cm["ri"]).astype(F32)
        da = jnp.dot(triu, dacum, preferred_element_type=F32, precision=HI)
        ddt = da * cm["a_neg"] + squeeze(dxdt * xsv)
        dalog_ref[...] += jnp.sum(da * cm["dt_s"], axis=0, keepdims=True) * cm["a_neg"]
        ddt_raw = jnp.where(lane < SSD_HEADS, ddt * _sigmoid(cm["pre"]), 0.0)
        ddt_ref[...] = ddt_raw
        ddtb_ref[...] += jnp.sum(ddt_raw, axis=0, keepdims=True)

    row = lambda d: pl.BlockSpec((None, ln, d), lambda bb, c: (bb, nc - 1 - c, 0))
    return pl.pallas_call(
        body, name=name, grid=(b, nc),
        in_specs=[row(D_SSD), row(D_SSD), row(D_SSD), row(D_SSD), row(256), row(256), row(LANES),
                  pl.BlockSpec((None, None, SSD_STATE, D_SSD), lambda bb, c: (bb, nc - 1 - c, 0, 0)),
                  _full((1, LANES)), _full((1, LANES)), _full((1, D_SSD)), _full((1, D_SSD)),
                  _full((LANES, D_SSD)), _full((D_SSD, LANES))],
        out_specs=[row(D_SSD), row(256), row(256), row(D_SSD), row(LANES),
                   _full((1, D_SSD)), _full((1, D_SSD)), _full((1, LANES)), _full((1, LANES))],
        out_shape=[SDS((b, s, D_SSD), F32), SDS((b, s, 256), F32), SDS((b, s, 256), F32), SDS((b, s, D_SSD), BF16),
                   SDS((b, s, LANES), F32), SDS((1, D_SSD), F32), SDS((1, D_SSD), F32), SDS((1, LANES), F32),
                   SDS((1, LANES), F32)],
        scratch_shapes=[pltpu.VMEM((SSD_STATE, D_SSD), F32), pltpu.VMEM((ln, D_SSD), F32), pltpu.VMEM((LANES, ln), F32)],
        compiler_params=_cparams(2))(dys, y, z, xs, bm, cm_, misc, prev, dtb, alog, dskip_e, norm_w, e_mat, et_mat)


def _rope(xv, cc, sp, sm):
    n = xv.shape[1]
    return xv * cc + pltpu.roll(xv, 16, 1) * sp + pltpu.roll(xv, n - 16, 1) * sm


def _rope_bwd(dy, cc, sp, sm):
    n = dy.shape[1]
    return dy * cc + pltpu.roll(dy * sp, n - 16, 1) + pltpu.roll(dy * sm, 16, 1)


def _tile8(t):
    return jnp.concatenate([t] * MLA_HEADS, axis=1)


def qkv_fwd(cq, ckv, misc, cc, sp, sm, qnw, kvnw, wuq_t, wukv_t, place, name):
    b, s, _ = cq.shape
    tm = min(512, s)
    hd = MLA_HEADS * HEAD_PAD

    def body(cq_ref, ckv_ref, misc_ref, cc_ref, sp_ref, sm_ref, qnw_ref, kvnw_ref, wq_ref, wkv_ref, pl_ref,
             q_ref, k_ref, v_ref, qn_ref, kvn_ref):
        cqv, ckvv = cq_ref[...], ckv_ref[...]
        qn = (cqv * _rms(cqv) * qnw_ref[...]).astype(BF16)
        kvn = (ckvv * _rms(ckvv) * kvnw_ref[...]).astype(BF16)
        qn_ref[...] = qn
        kvn_ref[...] = kvn
        cc1, sp1, sm1 = cc_ref[...], sp_ref[...], sm_ref[...]
        q = lax.dot_general(qn, wq_ref[...], NT_DIMS, preferred_element_type=F32)
        q_ref[...] = _rope(q, _tile8(cc1), _tile8(sp1), _tile8(sm1)).astype(BF16)
        kv = lax.dot_general(kvn, wkv_ref[...], NT_DIMS, preferred_element_type=F32)
        kr = jnp.dot(misc_ref[...], pl_ref[...], preferred_element_type=F32, precision=HI)
        kr = _rope(kr, cc1, sp1, sm1)
        k_ref[...] = (kv[:, 0:hd] + _tile8(kr)).astype(BF16)
        v_ref[...] = kv[:, hd:2 * hd].astype(BF16)

    return pl.pallas_call(
        body, name=name, grid=(b, s // tm),
        in_specs=[_row(tm, Q_LORA), _row(tm, KV_LORA), _row(tm, LANES), _row(tm, LANES), _row(tm, LANES), _row(tm, LANES),
                  _full((1, Q_LORA)), _full((1, KV_LORA)), _full(wuq_t.shape), _full(wukv_t.shape), _full((LANES, LANES))],
        out_specs=[_row(tm, hd), _row(tm, hd), _row(tm, hd), _row(tm, Q_LORA), _row(tm, KV_LORA)],
        out_shape=[SDS((b, s, hd), BF16)] * 3 + [SDS((b, s, Q_LORA), BF16), SDS((b, s, KV_LORA), BF16)],
        compiler_params=_cparams(2))(cq, ckv, misc, cc, sp, sm, qnw, kvnw, wuq_t, wukv_t, place)


def qkv_bwd(dq, dk, dv, ddt, cq, ckv, cc, sp, sm, qnw, kvnw, wuq_t, wukv_t, place_t, name):
    b, s, _ = cq.shape
    tm = min(512, s)
    hd = MLA_HEADS * HEAD_PAD

    def body(dq_ref, dk_ref, dv_ref, ddt_ref, cq_ref, ckv_ref, cc_ref, sp_ref, sm_ref, qnw_ref, kvnw_ref,
             wq_ref, wkv_ref, plt_ref, dcq_ref, dckv_ref, dmisc_ref, dqp_ref, dkv_ref, dqnw_ref, dkvnw_ref):
        @pl.when(_first_step())
        def _():
            dqnw_ref[...] = jnp.zeros_like(dqnw_ref)
            dkvnw_ref[...] = jnp.zeros_like(dkvnw_ref)
        cc1, sp1, sm1 = cc_ref[...], sp_ref[...], sm_ref[...]
        dqp = _rope_bwd(dq_ref[...].astype(F32), _tile8(cc1), _tile8(sp1), _tile8(sm1)).astype(BF16)
        dqp_ref[...] = dqp
        dkv_b = jnp.concatenate([dk_ref[...], dv_ref[...]], axis=1)
        dkf = dk_ref[...].astype(F32)
        dkv_ref[...] = dkv_b
        dkr = dkf[:, 0:HEAD_PAD]
        for h in range(1, MLA_HEADS):
            dkr += dkf[:, HEAD_PAD * h:HEAD_PAD * (h + 1)]
        dkr = _rope_bwd(dkr, cc1, sp1, sm1)
        dmisc_ref[...] = (jnp.dot(dkr, plt_ref[...], preferred_element_type=F32, precision=HI) + ddt_ref[...]).astype(BF16)

        def norm_bwd(dn_w, xv, w_ref, dw_ref, dx_ref):
            r = _rms(xv)
            n = xv * r
            dw_ref[...] += jnp.sum(dn_w * n, axis=0, keepdims=True)
            dx_ref[...] = _rms_bwd(dn_w * w_ref[...], n, r).astype(BF16)

        norm_bwd(jnp.dot(dqp, wq_ref[...], preferred_element_type=F32), cq_ref[...], qnw_ref, dqnw_ref, dcq_ref)
        norm_bwd(jnp.dot(dkv_b, wkv_ref[...], preferred_element_type=F32), ckv_ref[...], kvnw_ref, dkvnw_ref, dckv_ref)

    return pl.pallas_call(
        body, name=name, grid=(b, s // tm),
        in_specs=[_row(tm, hd), _row(tm, hd), _row(tm, hd), _row(tm, LANES), _row(tm, Q_LORA), _row(tm, KV_LORA),
                  _row(tm, LANES), _row(tm, LANES), _row(tm, LANES), _full((1, Q_LORA)), _full((1, KV_LORA)),
                  _full(wuq_t.shape), _full(wukv_t.shape), _full((LANES, LANES))],
        out_specs=[_row(tm, Q_LORA), _row(tm, KV_LORA), _row(tm, LANES), _row(tm, hd), _row(tm, 2 * hd),
                   _full((1, Q_LORA)), _full((1, KV_LORA))],
        out_shape=[SDS((b, s, Q_LORA), BF16), SDS((b, s, KV_LORA), BF16), SDS((b, s, LANES), BF16),
                   SDS((b, s, hd), BF16), SDS((b, s, 2 * hd), BF16), SDS((1, Q_LORA), F32), SDS((1, KV_LORA), F32)],
        compiler_params=_cparams(2))(dq, dk, dv, ddt, cq, ckv, cc, sp, sm, qnw, kvnw, wuq_t, wukv_t, place_t)


ATT_SCALE = 1.0 / math.sqrt(QK_DIM)
LOG2E = math.log2(math.e)
ATT_SCALE_LOG2E = ATT_SCALE * LOG2E


ATT_HEADS_PER_STEP = 4
ATT_HEADS_PER_STEP_BWD = 2


def _att_tile(s):
    return min(512, s)


def flash_fwd(q, k, v, name):
    b, s, hd = q.shape
    t = _att_tile(s)
    nb = s // t
    th = t // 2
    vt = v.reshape(b, nb, t, MLA_HEADS, HEAD_PAD).transpose(0, 3, 1, 4, 2)

    hps = ATT_HEADS_PER_STEP
    hw = hps * HEAD_PAD

    def body(q_ref, k_ref, vt_ref, o_ref, lse_ref, m_s, l_s, acc):
        i = pl.program_id(2)
        m_s[...] = jnp.full_like(m_s, -jnp.inf)
        l_s[...] = jnp.zeros_like(l_s)
        acc[...] = jnp.zeros_like(acc)

        def update(j, diagonal):
            ks = pl.ds(pl.multiple_of(j * t, t), t)
            chains = [(hh, half) for hh in range(hps) for half in range(2)]
            lanes = lambda hh: slice(HEAD_PAD * hh, HEAD_PAD * (hh + 1))
            cols = lambda half: slice(th * half, th * (half + 1))
            sts = {}
            for hh, half in chains:
                st = lax.dot_general(k_ref[ks, lanes(hh)], q_ref[cols(half), lanes(hh)], NT_DIMS,
                                     preferred_element_type=F32)
                if diagonal:
                    row = lax.broadcasted_iota(I32, (t, th), 0)
                    col = lax.broadcasted_iota(I32, (t, th), 1) + th * half
                    st = jnp.where(row <= col, st, -jnp.inf)
                sts[hh, half] = st
            pts, alphas = {}, {}
            for hh, half in chains:
                st, cs = sts[hh, half], cols(half)
                m_prev = m_s[hh, :, cs]
                m_new = jnp.maximum(m_prev, jnp.max(st, axis=0, keepdims=True))
                alpha = jnp.exp2((m_prev - m_new) * ATT_SCALE_LOG2E)
                pt = jnp.exp2((st - m_new) * ATT_SCALE_LOG2E)
                l_s[hh, :, cs] = alpha * l_s[hh, :, cs] + jnp.sum(pt, axis=0, keepdims=True)
                m_s[hh, :, cs] = m_new
                pts[hh, half], alphas[hh, half] = pt.astype(BF16), alpha
            for hh, half in chains:
                cs = cols(half)
                acc[hh, :, cs] = alphas[hh, half] * acc[hh, :, cs] + jnp.dot(vt_ref[hh, j], pts[hh, half],
                                                                             preferred_element_type=F32)

        def step(j, carry):
            update(j, False)
            return carry

        lax.fori_loop(0, i, step, 0)
        update(i, True)
        for hh in range(hps):
            o_ref[:, HEAD_PAD * hh:HEAD_PAD * (hh + 1)] = (acc[hh] / l_s[hh]).T
            lse_ref[hh] = m_s[hh] * ATT_SCALE + jnp.log(l_s[hh])

    qs = pl.BlockSpec((None, t, hw), lambda bb, h, i: (bb, i, h))
    ks = pl.BlockSpec((None, s, hw), lambda bb, h, i: (bb, 0, h))
    vs = pl.BlockSpec((None, hps, nb, HEAD_PAD, t), lambda bb, h, i: (bb, h, 0, 0, 0))
    ls = pl.BlockSpec((None, hps, None, 1, t), lambda bb, h, i: (bb, h, i, 0, 0))
    return pl.pallas_call(
        body, name=name, grid=(b, MLA_HEADS // hps, nb),
        in_specs=[qs, ks, vs], out_specs=[qs, ls],
        out_shape=[SDS((b, s, hd), F32), SDS((b, MLA_HEADS, nb, 1, t), F32)],
        scratch_shapes=[pltpu.VMEM((hps, 1, t), F32), pltpu.VMEM((hps, 1, t), F32), pltpu.VMEM((hps, HEAD_PAD, t), F32)],
        compiler_params=_cparams(3))(q, k, vt)


def flash_bwd(q, k, v, do, lse, dlt, name):
    b, s, hd = q.shape
    t = _att_tile(s)
    nb = s // t
    th = t // 2
    lse_r = lse
    dlt_r = dlt.reshape(b, MLA_HEADS, nb, 1, t)

    hps = ATT_HEADS_PER_STEP_BWD
    hw = hps * HEAD_PAD

    def body(q_ref, k_ref, v_ref, do_ref, lse_ref, dlt_ref, dq_ref, dk_ref, dv_ref, dq_s, dk_s, dv_s):
        dq_s[...] = jnp.zeros_like(dq_s)
        dk_s[...] = jnp.zeros_like(dk_s)
        dv_s[...] = jnp.zeros_like(dv_s)

        def tile(j, i, diagonal):
            qs = pl.ds(pl.multiple_of(i * t, t), t)
            chains = [(hh, half) for hh in range(hps) for half in range(2)]
            lanes = lambda hh: slice(HEAD_PAD * hh, HEAD_PAD * (hh + 1))
            keys = lambda half: pl.ds(pl.multiple_of(j * t + th * half, th), th)
            sts, dpts = {}, {}
            for hh, half in chains:
                ls_, ks = lanes(hh), keys(half)
                st = lax.dot_general(k_ref[ks, ls_], q_ref[qs, ls_], NT_DIMS, preferred_element_type=F32)
                if diagonal:
                    row = lax.broadcasted_iota(I32, (th, t), 0) + th * half
                    col = lax.broadcasted_iota(I32, (th, t), 1)
                    st = jnp.where(row <= col, st, -jnp.inf)
                sts[hh, half] = st
                dpts[hh, half] = lax.dot_general(v_ref[ks, ls_], do_ref[qs, ls_], NT_DIMS, preferred_element_type=F32)
            pts, dsts = {}, {}
            for hh, half in chains:
                pt = jnp.exp2(sts[hh, half] * ATT_SCALE_LOG2E - lse_ref[hh, i] * LOG2E)
                pts[hh, half] = pt.astype(BF16)
                dsts[hh, half] = (pt * (dpts[hh, half] - dlt_ref[hh, i])).astype(BF16)
            for hh in range(hps):
                ls_ = lanes(hh)
                dq_acc = None
                for half in range(2):
                    ks = keys(half)
                    dv_s[ks, ls_] += jnp.dot(pts[hh, half], do_ref[qs, ls_], preferred_element_type=F32)
                    dk_s[ks, ls_] += jnp.dot(dsts[hh, half], q_ref[qs, ls_], preferred_element_type=F32)
                    part = lax.dot_general(dsts[hh, half], k_ref[ks, ls_], TN_DIMS, preferred_element_type=F32)
                    dq_acc = part if dq_acc is None else dq_acc + part
                dq_s[qs, ls_] += dq_acc

        def key_tile(j, carry):
            tile(j, j, True)

            def query_tile(i, c2):
                tile(j, i, False)
                return c2

            lax.fori_loop(j + 1, nb, query_tile, 0)
            return carry

        lax.fori_loop(0, nb, key_tile, 0)
        dq_ref[...] = (dq_s[...] * ATT_SCALE).astype(BF16)
        dk_ref[...] = (dk_s[...] * ATT_SCALE).astype(BF16)
        dv_ref[...] = dv_s[...].astype(BF16)

    hs = pl.BlockSpec((None, s, hw), lambda bb, h: (bb, 0, h))
    ls = pl.BlockSpec((None, hps, nb, 1, t), lambda bb, h: (bb, h, 0, 0, 0))
    return pl.pallas_call(
        body, name=name, grid=(b, MLA_HEADS // hps),
        in_specs=[hs, hs, hs, hs, ls, ls], out_specs=[hs, hs, hs],
        out_shape=[SDS((b, s, hd), BF16)] * 3, scratch_shapes=[pltpu.VMEM((s, hw), F32)] * 3,
        compiler_params=_cparams(2))(q, k, v, do, lse_r, dlt_r)


def out_proj(ys, attn, mnw, wo, x, gate, above, name):
    b, s, d = x.shape
    tm = min(512, s)

    def body(ys_ref, at_ref, mnw_ref, wo_ref, x_ref, g_ref, nw_ref, sc_ref, sh_ref, xn_ref, o_ref, ym_ref, h_ref):
        av = at_ref[...]
        ym = (av * _rms(av) * mnw_ref[...]).astype(BF16)
        ym_ref[...] = ym
        o = jnp.dot(ys_ref[...], wo_ref[0:D_SSD, :], preferred_element_type=F32)
        o += jnp.dot(ym, wo_ref[D_SSD:2 * D_SSD, :], preferred_element_type=F32)
        xn = x_ref[...] + g_ref[...] * o
        xn_ref[...] = xn
        o_ref[...] = o.astype(BF16)
        h_ref[...] = _norm_mod_tile(xn, nw_ref, sc_ref, sh_ref)

    return pl.pallas_call(
        body, name=name, grid=(b, s // tm),
        in_specs=[_row(tm, D_SSD), _row(tm, D_SSD), _full((1, D_SSD)), _full(wo.shape), _row(tm, d), _bvec(d),
                  _full((1, d)), _bvec(d), _bvec(d)],
        out_specs=[_row(tm, d), _row(tm, d), _row(tm, D_SSD), _row(tm, d)],
        out_shape=[SDS((b, s, d), F32), SDS((b, s, d), BF16), SDS((b, s, D_SSD), BF16), SDS((b, s, d), BF16)],
        compiler_params=_cparams(2))(ys, attn, mnw, wo, x, gate, *above)


def out_proj_bwd(dout, attn, mnw, wo, name):
    b, s, d = dout.shape
    tm = min(512, s)

    def body(do_ref, at_ref, mnw_ref, wo_ref, dys_ref, dat_ref, dlt_ref, dw_ref):
        lane = lax.broadcasted_iota(I32, (tm, LANES), 1)
        @pl.when(_first_step())
        def _():
            dw_ref[...] = jnp.zeros_like(dw_ref)
        dov = do_ref[...]
        dys_ref[...] = lax.dot_general(dov, wo_ref[0:D_SSD, :], NT_DIMS, preferred_element_type=F32)
        dym = lax.dot_general(dov, wo_ref[D_SSD:2 * D_SSD, :], NT_DIMS, preferred_element_type=F32)
        av = at_ref[...]
        r = _rms(av)
        n = av * r
        dw_ref[...] += jnp.sum(dym * n, axis=0, keepdims=True)
        dat = _rms_bwd(dym * mnw_ref[...], n, r)
        dat_ref[...] = dat.astype(BF16)
        prod = dat * av
        cols = jnp.zeros((tm, LANES), F32)
        for h in range(MLA_HEADS):
            cols += jnp.where(lane == h, jnp.sum(prod[:, HEAD_PAD * h:HEAD_PAD * (h + 1)], axis=1, keepdims=True), 0.0)
        dlt_ref[...] = cols.T[0:MLA_HEADS, :]

    return pl.pallas_call(
        body, name=name, grid=(b, s // tm),
        in_specs=[_row(tm, d), _row(tm, D_SSD), _full((1, D_SSD)), _full(wo.shape)],
        out_specs=[_row(tm, D_SSD), _row(tm, D_SSD),
                   pl.BlockSpec((None, MLA_HEADS, tm), lambda bb, i: (bb, 0, i)), _full((1, D_SSD))],
        out_shape=[SDS((b, s, D_SSD), F32), SDS((b, s, D_SSD), BF16), SDS((b, MLA_HEADS, s), F32),
                   SDS((1, D_SSD), F32)],
        compiler_params=_cparams(2))(dout, attn, mnw, wo)


def adaln_fwd(c_all, w_ada, b_ada, name):
    nb, d = c_all.shape
    n = w_ada.shape[1]

    def body(c_ref, w_ref, b_ref, m_ref, ca_ref):
        cv = c_ref[...]
        ca = (cv * _sigmoid(cv)).astype(BF16)
        ca_ref[...] = ca
        m_ref[...] = jnp.dot(ca, w_ref[...].astype(BF16), preferred_element_type=F32) + b_ref[...]

    return pl.pallas_call(
        body, name=name, out_shape=[SDS((nb, n), F32), SDS((nb, d), BF16)],
        compiler_params=pltpu.CompilerParams(vmem_limit_bytes=VMEM_LIMIT))(c_all, w_ada, b_ada)


def adaln_bwd(c_act, dmod_cols, name):
    d, n = c_act.shape[1], dmod_cols.shape[1]

    def body(c_ref, dm_ref, gw_ref):
        gw_ref[...] = lax.dot_general(c_ref[...], dm_ref[...].astype(BF16), TN_DIMS, preferred_element_type=F32)

    return pl.pallas_call(
        body, name=name, out_shape=SDS((d, n), F32),
        compiler_params=pltpu.CompilerParams(vmem_limit_bytes=VMEM_LIMIT))(c_act, dmod_cols)


def sum_rows(x, name):
    def body(x_ref, o_ref):
        o_ref[...] = jnp.sum(x_ref[...], axis=0, keepdims=True)
    return pl.pallas_call(body, name=name, out_shape=SDS((1, x.shape[1]), F32))(x)


def squeeze_heads(x, et_mat, name):
    def body(x_ref, et_ref, o_ref):
        xv = jnp.broadcast_to(x_ref[...], (8, x.shape[1]))
        o_ref[...] = _dot_01(xv, et_ref[...], 3)[0:1, :]
    return pl.pallas_call(body, name=name, out_shape=SDS((1, LANES), F32))(x, et_mat)


def sum_blocks(x, name):
    n, r, c = x.shape
    tr = next(cand for cand in (256, 128, 64, 32, 16, 8) if r % cand == 0)

    def body(x_ref, o_ref):
        acc = x_ref[0].astype(F32)
        for k in range(1, n):
            acc += x_ref[k].astype(F32)
        o_ref[...] = acc

    return pl.pallas_call(
        body, name=name, grid=(r // tr,), in_specs=[pl.BlockSpec((n, tr, c), lambda i: (0, i, 0))],
        out_specs=pl.BlockSpec((tr, c), lambda i: (i, 0)), out_shape=SDS((r, c), F32),
        compiler_params=_cparams(1))(x)


def _adam_math(w, g, m, v):
    m = ADAM_B1 * m + (1.0 - ADAM_B1) * g
    v = ADAM_B2 * v + (1.0 - ADAM_B2) * (g * g)
    m_hat = m / (1.0 - ADAM_B1 ** ADAM_STEP)
    v_hat = v / (1.0 - ADAM_B2 ** ADAM_STEP)
    return -ADAM_LR * (m_hat / (jnp.sqrt(v_hat) + ADAM_EPS) + ADAM_WD * w), m, v


def adamw(w, g, m, v, name):
    r, c = w.shape[-2:]
    tr = r
    for cand in (512, 256, 128, 64, 32, 16, 8):
        if r % cand == 0 and cand * c * 4 <= 2 * 1024 * 1024:
            tr = cand
            break

    def body(w_ref, g_ref, m_ref, v_ref, d_ref, mo_ref, vo_ref):
        d_ref[...], mo_ref[...], vo_ref[...] = _adam_math(w_ref[...], g_ref[...], m_ref[...], v_ref[...])

    def spec(a):
        return pl.BlockSpec((tr, c), lambda i: (i, 0)) if a.ndim == 2 else pl.BlockSpec((None, tr, c), lambda i: (0, i, 0))

    return pl.pallas_call(
        body, name=name, grid=(r // tr,), in_specs=[spec(w), spec(g), spec(m), spec(v)], out_specs=[spec(w)] * 3,
        out_shape=[SDS(w.shape, F32)] * 3, compiler_params=_cparams(1))(w, g, m, v)


def adamw_many(ws, gs, ms, vs, name):
    n = len(ws)

    def body(*refs):
        w_r, g_r, m_r, v_r = (refs[k * n:(k + 1) * n] for k in range(4))
        d_r, mo_r, vo_r = (refs[(4 + k) * n:(5 + k) * n] for k in range(3))
        for k in range(n):
            d_r[k][...], mo_r[k][...], vo_r[k][...] = _adam_math(w_r[k][...], g_r[k][...], m_r[k][...], v_r[k][...])

    shapes = [SDS(w.shape, F32) for w in ws]
    outs = pl.pallas_call(body, name=name, out_shape=shapes * 3)(*ws, *gs, *ms, *vs)
    return outs[:n], outs[n:2 * n], outs[2 * n:]


PACK = {"ffn1_w_gate": (352, 352), "ffn1_w_up": (352, 352), "ffn1_w_down": (352, 352),
        "ffn2_w_gate": (352, 352), "ffn2_w_up": (352, 352), "ffn2_w_down": (352, 352),
        "w_out": (256, 256), "w_in": (406, 416), "w_ukv": (48, 48), "w_uq": (36, 48)}
TRANSPOSED = ("ffn1_w_gate", "ffn1_w_up", "ffn2_w_gate", "ffn2_w_up", "w_in", "w_ukv", "w_uq")
GATHER_GROUPS = (("ffn1_w_gate", "ffn1_w_up"), ("ffn1_w_down",),
                 ("w_in", "w_ukv", "w_uq", "w_out", "ffn2_w_gate", "ffn2_w_up", "ffn2_w_down"))
GRAD_GROUPS = (("ffn2", ("ffn2_w_gate", "ffn2_w_up", "ffn2_w_down")), ("mixer", ("w_out", "w_in", "w_ukv", "w_uq")),
               ("ffn1_down", ("ffn1_w_down",)), ("ffn1_gate", ("ffn1_w_gate",)), ("ffn1_up", ("ffn1_w_up",)))


def _pack_offsets(names):
    off, o = {}, 0
    for n in names:
        off[n] = (o, PACK[n][0])
        o += PACK[n][1]
    return off, o


def _shard_to_rows(name, w):
    w = w[0]
    if name in TRANSPOSED:
        w = w.T
    return w.reshape(-1, D_MODEL)


def _rows_to_shard(name, rows, like):
    shp = like.shape[1:]
    if name in TRANSPOSED:
        return rows.reshape(shp[1], shp[0]).T[None]
    return rows.reshape(shp)[None]


def _pack_shards(ws, names, dtype):
    parts = []
    for name in names:
        real, padded = PACK[name]
        rows = _shard_to_rows(name, ws[name]).astype(dtype)
        if padded > real:
            rows = jnp.pad(rows, ((0, padded - real), (0, 0)))
        parts.append(rows)
    return jnp.concatenate(parts, axis=0)


def _grad_rows(name, gw):
    real, padded = PACK[name]
    if name == "w_in":
        rows = _in_proj_rows_inv(gw).reshape(N_DEV, -1, D_MODEL)
    elif name == "w_ukv":
        hd = MLA_HEADS * HEAD_PAD
        rows = jnp.concatenate([gw[:hd].reshape(MLA_HEADS, HEAD_PAD, KV_LORA)[:, :QK_NOPE],
                                gw[hd:].reshape(MLA_HEADS, V_HEAD, KV_LORA)], axis=1).reshape(N_DEV, -1, D_MODEL)
    elif name == "w_uq":
        rows = gw.reshape(MLA_HEADS, HEAD_PAD, Q_LORA)[:, :QK_DIM].reshape(N_DEV, -1, D_MODEL)
    else:
        rows = gw.reshape(N_DEV, -1, D_MODEL)
    if padded > real:
        rows = jnp.pad(rows, ((0, 0), (0, padded - real), (0, 0)))
    return rows


def _pack_rows(arrs):
    parts = []
    for a in arrs:
        flat = a.reshape(-1).astype(F32)
        pad = (-flat.shape[0]) % D_MODEL
        if pad:
            flat = jnp.pad(flat, (0, pad))
        parts.append(flat.reshape(-1, D_MODEL))
    out = jnp.concatenate(parts, axis=0)
    pad = (-out.shape[0]) % 8
    if pad:
        out = jnp.pad(out, ((0, pad), (0, 0)))
    return out


def _unpack_rows(packed, shapes):
    out, row = [], 0
    for shp in shapes:
        n = math.prod(shp)
        nrow = -(-n // D_MODEL)
        out.append(packed[row:row + nrow].reshape(-1)[:n].reshape(shp))
        row += nrow
    return out


def _in_proj_rows(w_t):
    return jnp.concatenate([w_t[0:2560], w_t[2576:2960], w_t[2960:3216], w_t[2560:2576], w_t[3216:3248],
                            jnp.zeros((D_IN_PAD - D_IN, D_MODEL), w_t.dtype)], axis=0)


def _in_proj_rows_inv(d):
    return jnp.concatenate([d[0:2560], d[3200:3216], d[2560:2944], d[2944:3200], d[3216:3248]], axis=0)


def _rope_tables(positions):
    inv_freq = ROPE_THETA ** (-jnp.arange(0, QK_ROPE, 2, dtype=F32) / QK_ROPE)
    ang = positions[..., None].astype(F32) * inv_freq
    cos, sin = jnp.cos(ang), jnp.sin(ang)
    one = jnp.ones(ang.shape[:2] + (QK_NOPE,), F32)
    zero = jnp.zeros_like(one)
    z16, z32, o32 = zero[..., :16], zero[..., :32], one[..., :32]
    cc = jnp.concatenate([one, cos, cos, o32], axis=-1)
    sp = jnp.concatenate([zero, z16, sin, z32], axis=-1)
    sm = jnp.concatenate([zero, -sin, z16, z32], axis=-1)
    return cc, sp, sm


def weight_views(gathered):
    def _seg(name):
        names, g = next((names, g) for names, g in zip(GATHER_GROUPS, gathered) if name in names)
        o, r = _pack_offsets(names)[0][name]
        return g[:, o:o + r]

    full = lambda name: _seg(name).reshape(-1, D_MODEL)
    ukv = _seg("w_ukv").reshape(MLA_HEADS, QK_NOPE + V_HEAD, KV_LORA)
    wukv_t = jnp.concatenate([jnp.pad(ukv[:, :QK_NOPE], ((0, 0), (0, HEAD_PAD - QK_NOPE), (0, 0))).reshape(-1, KV_LORA),
                              ukv[:, QK_NOPE:].reshape(-1, KV_LORA)], axis=0)
    uq = _seg("w_uq").reshape(MLA_HEADS, QK_DIM, Q_LORA)
    wuq_t = jnp.pad(uq, ((0, 0), (0, HEAD_PAD - QK_DIM), (0, 0))).reshape(-1, Q_LORA)
    return dict(wg1_t=full("ffn1_w_gate"), wu1_t=full("ffn1_w_up"), wd1=full("ffn1_w_down"),
                wg2_t=full("ffn2_w_gate"), wu2_t=full("ffn2_w_up"), wd2=full("ffn2_w_down"),
                wo=full("w_out"), win_t=_in_proj_rows(full("w_in")), wukv_t=wukv_t, wuq_t=wuq_t)


def _ffn_bwd(tag, dxn, do, dgate, x, h, gg, uu, a, sc, norm_w, wg_t, wu_t, wd, below):
    f2 = wd.shape[0] // 2
    dwd = mm_tn(a, do, f2, D_MODEL, tag + "_dwd")
    dgg, duu = ffn_dact(do, wd, gg, uu, tag + "_dact")
    dwg_t = mm_tn(dgg, h, f2, D_MODEL, tag + "_dwg")
    dwu_t = mm_tn(duu, h, f2, D_MODEL, tag + "_dwu")
    dx, dsc, dsh, dnw, *nxt = dh_norm_bwd([dgg, duu], [wg_t, wu_t], x, dxn, norm_w, sc, tag + "_dh", below)
    return dx, (dsh, dsc, dgate), dnw, (dwg_t, dwu_t, dwd), nxt


def local_step(x, tgt, positions, mod, wv, p):
    nb, s, d = x.shape
    sh1, sc1, g1, sh2, sc2, g2, sh3, sc3, g3 = mod
    cc, sp, sm = _rope_tables(positions)
    lane_head = jnp.arange(D_SSD, dtype=I32)[None, :] // SSD_HEAD_DIM
    e_mat = (lane_head == jnp.arange(LANES, dtype=I32)[:, None]).astype(BF16)
    et_mat = e_mat.T
    rr, cl = jnp.arange(LANES, dtype=I32)[:, None], jnp.arange(LANES, dtype=I32)[None, :]
    place = ((cl == rr + (QK_NOPE - SSD_HEADS)) & (rr >= SSD_HEADS) & (rr < SSD_HEADS + QK_ROPE)).astype(F32)
    dtb = jnp.pad(p["dt_bias"], ((0, 0), (0, LANES - SSD_HEADS)))
    alog = jnp.pad(p["a_log"], ((0, 0), (0, LANES - SSD_HEADS)))
    dskip_e = jnp.repeat(p["d_skip"], SSD_HEAD_DIM, axis=1)

    h1 = norm_mod(x, p["norm_ffn1"], sc1, sh1, "ffn1_norm")
    gg1, uu1, a1 = ffn_up(h1, wv["wg1_t"], wv["wu1_t"], "ffn1_up")
    x1, o1, h2 = ffn_down(a1, wv["wd1"], x, g1, 0.5, "ffn1_down", (p["norm_mix"], sc2, sh2))
    z, u, cq, ckv, misc = in_proj(h2, wv["win_t"], "in_proj")
    xs, bm, cm_ = conv_fwd(u, p["conv_w"], p["conv_b"], "conv_fwd")
    ys, y, prev = ssd_fwd(xs, bm, cm_, misc, z, dtb, alog, dskip_e, p["ssd_norm_w"], e_mat, "ssd_fwd")
    q, k, v, qn, kvn = qkv_fwd(cq, ckv, misc, cc, sp, sm, p["q_norm_w"], p["kv_norm_w"], wv["wuq_t"], wv["wukv_t"],
                               place, "qkv_fwd")
    attn, lse = flash_fwd(q, k, v, "flash_fwd")
    x2, o2, ym, h3 = out_proj(ys, attn, p["mla_norm_w"], wv["wo"], x1, g2, (p["norm_ffn2"], sc3, sh3), "out_proj")
    gg3, uu3, a3 = ffn_up(h3, wv["wg2_t"], wv["wu2_t"], "ffn2_up")
    loss, dx3, dnfin, do3, dg3 = ffn_down_final(a3, wv["wd2"], x2, g3, 0.5, p["norm_final"], tgt, "ffn2_down_loss")

    dx2, dmod3, dnf2, (dwg2, dwu2, dwd2), (dout, dg2) = _ffn_bwd(
        "ffn2", dx3, do3, dg3, x2, h3, gg3, uu3, a3, sc3, p["norm_ffn2"], wv["wg2_t"], wv["wu2_t"], wv["wd2"],
        (o2, g2, 1.0))
    dys, dattn, dlt, dmlan = out_proj_bwd(dout, attn, p["mla_norm_w"], wv["wo"], "out_proj_bwd")
    dwo = jnp.concatenate([mm_tn(ys, dout, D_SSD, D_MODEL, "dwo_ssd"), mm_tn(ym, dout, D_SSD, D_MODEL, "dwo_mla")], axis=0)
    dxs, dbm, dcm, dz, ddt, dssdn, ddsk_lane, ddtb, dalog = ssd_bwd(
        dys, y, z, xs, bm, cm_, misc, prev, dtb, alog, dskip_e, p["ssd_norm_w"], e_mat, et_mat, "ssd_bwd")
    dq, dk, dv = flash_bwd(q, k, v, dattn, lse, dlt, "flash_bwd")
    dcq, dckv, dmisc, dqp, dkvc, dqn, dkvn = qkv_bwd(dq, dk, dv, ddt, cq, ckv, cc, sp, sm, p["q_norm_w"], p["kv_norm_w"],
                                                     wv["wuq_t"], wv["wukv_t"], place.T, "qkv_bwd")
    dwuq = mm_tn(dqp, qn, MLA_HEADS * HEAD_PAD, Q_LORA, "dwuq")
    dwukv = mm_tn(dkvc, kvn, MLA_HEADS * HEAD_PAD, KV_LORA, "dwukv")
    dvv, dconv = conv_bwd_a(dxs, dbm, dcm, u, p["conv_w"], p["conv_b"], "conv_bwd_a")
    dconv = conv_grads_fold(dconv, "conv_grads_fold")
    du = conv_bwd_b(dvv, p["conv_w"], "conv_bwd_b")
    dproj = jnp.concatenate([dz, du, dcq, dckv, dmisc], axis=-1)
    dwin = mm_tn(dproj, h2, D_IN_PAD // 2, D_MODEL, "dwin")
    dx1, dsc2, dsh2, dnmix, do1, dg1 = dh_norm_bwd([dproj], [wv["win_t"]], x1, dx2, p["norm_mix"], sc2, "mix_dh",
                                                   (o1, g1, 0.5))
    dx0, dmod1, dnf1, (dwg1, dwu1, dwd1), _ = _ffn_bwd(
        "ffn1", dx1, do1, dg1, x, h1, gg1, uu1, a1, sc1, p["norm_ffn1"], wv["wg1_t"], wv["wu1_t"], wv["wd1"], None)

    dmod = jnp.concatenate([*dmod1, dsh2, dsc2, dg2, *dmod3], axis=1).reshape(nb, N_MOD * d)
    return dict(
        loss=loss, dx=dx0, dmod=dmod, norm_ffn1=dnf1, norm_mix=dnmix, norm_ffn2=dnf2, norm_final=dnfin,
        ssd_norm_w=dssdn, mla_norm_w=dmlan, q_norm_w=dqn, kv_norm_w=dkvn,
        dt_bias=ddtb[:, :SSD_HEADS], a_log=dalog[:, :SSD_HEADS],
        d_skip=squeeze_heads(ddsk_lane, et_mat, "d_skip_heads")[:, :SSD_HEADS],
        conv_b=dconv[4:5], conv_w=dconv[0:4],
        gw=dict(ffn1_w_gate=dwg1, ffn1_w_up=dwu1, ffn1_w_down=dwd1, ffn2_w_gate=dwg2, ffn2_w_up=dwu2, ffn2_w_down=dwd2,
                w_out=dwo, w_in=dwin, w_ukv=dwukv, w_uq=dwuq))


def kernel(x, c, positions, w_ada, b_ada, norm_ffn1, ffn1_w_gate, ffn1_w_up, ffn1_w_down, norm_mix, w_in, conv_w, conv_b, dt_bias, a_log, d_skip, ssd_norm_w, q_norm_w, w_uq, kv_norm_w, w_ukv, mla_norm_w, w_out, norm_ffn2, ffn2_w_gate, ffn2_w_up, ffn2_w_down, norm_final, loss_target, m_w_ada, m_b_ada, m_norm_ffn1, m_ffn1_w_gate, m_ffn1_w_up, m_ffn1_w_down, m_norm_mix, m_w_in, m_conv_w, m_conv_b, m_dt_bias, m_a_log, m_d_skip, m_ssd_norm_w, m_q_norm_w, m_w_uq, m_kv_norm_w, m_w_ukv, m_mla_norm_w, m_w_out, m_norm_ffn2, m_ffn2_w_gate, m_ffn2_w_up, m_ffn2_w_down, m_norm_final, v_w_ada, v_b_ada, v_norm_ffn1, v_ffn1_w_gate, v_ffn1_w_up, v_ffn1_w_down, v_norm_mix, v_w_in, v_conv_w, v_conv_b, v_dt_bias, v_a_log, v_d_skip, v_ssd_norm_w, v_q_norm_w, v_w_uq, v_kv_norm_w, v_w_ukv, v_mla_norm_w, v_w_out, v_norm_ffn2, v_ffn2_w_gate, v_ffn2_w_up, v_ffn2_w_down, v_norm_final):
    names = ["w_ada", "b_ada", "norm_ffn1", "ffn1_w_gate", "ffn1_w_up", "ffn1_w_down", "norm_mix", "w_in", "conv_w",
             "conv_b", "dt_bias", "a_log", "d_skip", "ssd_norm_w", "q_norm_w", "w_uq", "kv_norm_w", "w_ukv",
             "mla_norm_w", "w_out", "norm_ffn2", "ffn2_w_gate", "ffn2_w_up", "ffn2_w_down", "norm_final"]
    W = dict(zip(names, (w_ada, b_ada, norm_ffn1, ffn1_w_gate, ffn1_w_up, ffn1_w_down, norm_mix, w_in, conv_w, conv_b, dt_bias, a_log, d_skip, ssd_norm_w, q_norm_w, w_uq, kv_norm_w, w_ukv, mla_norm_w, w_out, norm_ffn2, ffn2_w_gate, ffn2_w_up, ffn2_w_down, norm_final)))
    M = dict(zip(names, (m_w_ada, m_b_ada, m_norm_ffn1, m_ffn1_w_gate, m_ffn1_w_up, m_ffn1_w_down, m_norm_mix, m_w_in, m_conv_w, m_conv_b, m_dt_bias, m_a_log, m_d_skip, m_ssd_norm_w, m_q_norm_w, m_w_uq, m_kv_norm_w, m_w_ukv, m_mla_norm_w, m_w_out, m_norm_ffn2, m_ffn2_w_gate, m_ffn2_w_up, m_ffn2_w_down, m_norm_final)))
    V = dict(zip(names, (v_w_ada, v_b_ada, v_norm_ffn1, v_ffn1_w_gate, v_ffn1_w_up, v_ffn1_w_down, v_norm_mix, v_w_in, v_conv_w, v_conv_b, v_dt_bias, v_a_log, v_d_skip, v_ssd_norm_w, v_q_norm_w, v_w_uq, v_kv_norm_w, v_w_ukv, v_mla_norm_w, v_w_out, v_norm_ffn2, v_ffn2_w_gate, v_ffn2_w_up, v_ffn2_w_down, v_norm_final)))

    nb, s, d = x.shape
    me = 4 * lax.axis_index("x") + 2 * lax.axis_index("y") + lax.axis_index("c")
    n_ada = w_ada.shape[2]

    taps, n_cw = conv_w.shape[1:]
    cg = all_gather8(_pack_rows([c, conv_w[0]]), "gather_c")
    c_all = cg[:, 0:nb].reshape(N_DEV * nb, d)
    conv_w_full = cg[:, nb, 0:taps * n_cw].reshape(N_DEV, taps, n_cw).transpose(1, 0, 2).reshape(taps, N_DEV * n_cw)
    g_ffn1 = all_gather8(_pack_shards(W, GATHER_GROUPS[0], BF16), "gather_w_ffn1")

    b_ada_cols = lax.dynamic_slice(b_ada, (0, me * n_ada), (1, n_ada))
    mod_cols, c_act = adaln_fwd(c_all, w_ada[0], b_ada_cols, "adaln_fwd")
    mod_g = all_gather8(mod_cols, "gather_mod")
    g_ffn1, mod_g, down1, rest = lax.optimization_barrier(
        (g_ffn1, mod_g, _pack_shards(W, GATHER_GROUPS[1], BF16), _pack_shards(W, GATHER_GROUPS[2], BF16)))
    g_down1 = sc_all_gather8(down1, "gather_w_ffn1_down", 1)
    wv = weight_views((g_ffn1, g_down1, sc_all_gather8(rest, "gather_w_rest", 7)))
    mod = lax.dynamic_slice(mod_g, (0, me * nb, 0), (N_DEV, nb, n_ada)).transpose(1, 0, 2).reshape(nb, N_MOD, 1, d)
    mod = [mod[:, k] for k in range(N_MOD)]

    P = dict(W)
    P["conv_w"] = conv_w_full
    P["norm_final"] = norm_final.reshape(1, d)
    R = local_step(x, loss_target, positions, mod, wv, P)

    dmod = R["dmod"]
    partial_shapes = [(1,), (1, d), (1, d), (1, d), (1, d), (1, d), (1, d), (1, Q_LORA), (1, KV_LORA),
                      (1, SSD_HEADS), (1, SSD_HEADS), (1, SSD_HEADS), (1, D_CONV), (4, D_CONV), (1, N_MOD * d),
                      (nb, N_MOD * d)]
    partial = _pack_rows([R["loss"][0, :1], R["norm_ffn1"], R["norm_mix"], R["norm_ffn2"], R["norm_final"],
                          R["ssd_norm_w"], R["mla_norm_w"], R["q_norm_w"], R["kv_norm_w"],
                          R["dt_bias"], R["a_log"], R["d_skip"], R["conv_b"], R["conv_w"],
                          sum_rows(dmod, "dmod_rows"), dmod])
    partial_g = all_gather8(partial, "gather_partials")
    (loss, g_nf1, g_nmix, g_nf2, g_nfin, g_ssdn, g_mlan, g_qn, g_kvn, g_dtb, g_alog, g_dskip, g_convb, g_convw,
     g_bada, _) = _unpack_rows(sum_blocks(partial_g, "sum_partials"), partial_shapes)
    dmod_row = sum(-(-math.prod(shp) // D_MODEL) for shp in partial_shapes[:-1])
    dmod_all = partial_g[:, dmod_row:dmod_row + nb * N_MOD].reshape(N_DEV * nb, N_MOD * d)
    g_wada = adaln_bwd(c_act, lax.dynamic_slice(dmod_all, (0, me * n_ada), (N_DEV * nb, n_ada)), "adaln_bwd")
    n_cw = conv_w.shape[2]
    G = {"w_ada": g_wada[None], "b_ada": g_bada, "norm_ffn1": g_nf1, "norm_mix": g_nmix, "norm_ffn2": g_nf2,
         "norm_final": g_nfin.reshape(d), "ssd_norm_w": g_ssdn, "mla_norm_w": g_mlan, "q_norm_w": g_qn,
         "kv_norm_w": g_kvn, "dt_bias": g_dtb, "a_log": g_alog, "d_skip": g_dskip, "conv_b": g_convb,
         "conv_w": lax.dynamic_slice(g_convw, (0, me * n_cw), (4, n_cw))[None]}

    DW, NM, NV = {}, {}, {}
    gw = R["gw"]
    for k, (tag, group) in enumerate(GRAD_GROUPS):
        send = jnp.concatenate([_grad_rows(name, gw[name]) for name in group], axis=1).astype(BF16)
        recv = sc_all_to_all8(send, "exchange_" + tag, 2 + k)
        gsum = sum_blocks(recv, "sum_" + tag)
        for name, (o, r) in _pack_offsets(group)[0].items():
            if name in TRANSPOSED:
                _, rr, cc = W[name].shape
                g_t = gsum[o:o + r].reshape(cc, rr)
                res = adamw(W[name][0].T, g_t, M[name][0].T, V[name][0].T, "adamw_" + name)
                G[name], DW[name], NM[name], NV[name] = [t.T[None] for t in (g_t, *res)]
            else:
                G[name] = _rows_to_shard(name, gsum[o:o + r], W[name])
                DW[name], NM[name], NV[name] = adamw(W[name], G[name], M[name], V[name], "adamw_" + name)
    DW["w_ada"], NM["w_ada"], NV["w_ada"] = adamw(w_ada, g_wada, m_w_ada, v_w_ada, "adamw_w_ada")
    small = [n for n in names if n not in DW]
    as2d = lambda a: a.reshape(-1, a.shape[-1])
    outs = adamw_many([as2d(W[n]) for n in small], [as2d(G[n]) for n in small], [as2d(M[n]) for n in small],
                      [as2d(V[n]) for n in small], "adamw_small")
    for res, dst in zip(outs, (DW, NM, NV)):
        for n, t in zip(small, res):
            dst[n] = t.reshape(W[n].shape)
    return (loss.reshape(()), R["dx"], *[G[n] for n in names], *[DW[n] for n in names], *[NM[n] for n in names],
            *[NV[n] for n in names])
```

```python
import math

import jax
import jax.numpy as jnp
from jax import lax
from jax.experimental import pallas as pl
from jax.experimental.pallas import tpu as pltpu
from jax.experimental.pallas import tpu_sc as plsc

F32, BF16, I32 = jnp.float32, jnp.bfloat16, jnp.int32
HI = lax.Precision.HIGHEST
SDS = jax.ShapeDtypeStruct
MESH = pl.DeviceIdType.MESH

D_MODEL = 1024
D_FF = 2816
D_SSD = 1024
SSD_HEADS = 16
SSD_HEAD_DIM = 64
SSD_GROUPS = 2
SSD_STATE = 128
CHUNK = 128
MLA_HEADS = 8
QK_NOPE = 64
QK_ROPE = 32
QK_DIM = 96
V_HEAD = 128
Q_LORA = 384
KV_LORA = 256
ROPE_THETA = 10000.0
N_MOD = 9
EPS = 1e-6
D_CONV = 1536
D_IN = 3248
D_IN_PAD = 3328
HEAD_PAD = 128
N_DEV = 8
ADAM_LR, ADAM_B1, ADAM_B2, ADAM_EPS, ADAM_WD, ADAM_STEP = 0.001, 0.9, 0.999, 1e-08, 0.01, 10

SAVED_ACT = BF16
VMEM_LIMIT = 56 * 1024 * 1024
LANES = 128
NT_DIMS = (((1,), (1,)), ((), ()))
TN_DIMS = (((0,), (0,)), ((), ()))


def _cparams(n_axes):
    return pltpu.CompilerParams(dimension_semantics=("arbitrary",) * n_axes, vmem_limit_bytes=VMEM_LIMIT)


def _row(tm, d):
    return pl.BlockSpec((None, tm, d), lambda b, i: (b, i, 0))


def _bvec(d):
    return pl.BlockSpec((None, 1, d), lambda b, i: (b, 0, 0))


def _full(shape):
    n = len(shape)
    return pl.BlockSpec(shape, lambda *_: (0,) * n)


def _sigmoid(x):
    return 1.0 / (1.0 + jnp.exp(-x))


def _softplus(x):
    return jnp.maximum(x, 0.0) + jnp.log(1.0 + jnp.exp(-jnp.abs(x)))


def _rms(x):
    return lax.rsqrt(jnp.mean(x * x, axis=-1, keepdims=True) + EPS)


def _rms_bwd(dn, n, r):
    return r * (dn - n * jnp.mean(dn * n, axis=-1, keepdims=True))


def _first_step():
    return (pl.program_id(0) == 0) & (pl.program_id(1) == 0)


def _gather_copies(x_refs, out_refs, send_sems, recv_sems, local_sems):
    mx, my, mc = lax.axis_index("x"), lax.axis_index("y"), lax.axis_index("c")
    me, sibling = (mx, my, mc), (mx, my, 1 - mc)
    chips = [(1 - mx, my), (mx, 1 - my), (1 - mx, 1 - my)]

    def copy(a, k, block, to, src=None):
        rows = out_refs[a].at[4 * block[0] + 2 * block[1] + block[2]]
        return pltpu.make_async_remote_copy(
            src_ref=rows if src is None else src, dst_ref=rows,
            send_sem=send_sems.at[7 * a + k], recv_sem=recv_sems.at[7 * a + k], device_id=to, device_id_type=MESH)

    arrays = range(len(x_refs))
    mine = [pltpu.make_async_copy(x_refs[a], out_refs[a].at[4 * mx + 2 * my + mc], local_sems.at[a]) for a in arrays]
    first = [[copy(a, 0, me, sibling, src=x_refs[a])] + [copy(a, 1 + j, me, (*chip, mc), src=x_refs[a])
                                                          for j, chip in enumerate(chips)] for a in arrays]
    passed = [[copy(a, 4 + j, (*chip, mc), sibling) for j, chip in enumerate(chips)] for a in arrays]
    for a in arrays:
        mine[a].start()
        for cp in first[a]:
            cp.start()
    for a in arrays:
        for j, chip in enumerate(chips):
            copy(a, 1 + j, (*chip, mc), me).wait_recv()
            passed[a][j].start()
    for a in arrays:
        copy(a, 0, sibling, me).wait_recv()
        for j, chip in enumerate(chips):
            copy(a, 4 + j, (*chip, 1 - mc), me).wait_recv()
    for a in arrays:
        for cp in first[a] + passed[a]:
            cp.wait_send()
        mine[a].wait()


def _gather_peers():
    mx, my, mc = lax.axis_index("x"), lax.axis_index("y"), lax.axis_index("c")
    return [(mx, my, 1 - mc), (1 - mx, my, mc), (mx, 1 - my, mc), (1 - mx, 1 - my, mc)]


def _comm_scratch(n):
    return [pltpu.SemaphoreType.DMA((7 * n,)), pltpu.SemaphoreType.DMA((7 * n,)), pltpu.SemaphoreType.DMA((n,))]


def all_gather8(xs, name):
    n = len(xs)

    def body(*refs):
        _gather_copies(refs[:n], refs[n:2 * n], *refs[2 * n:])

    return pl.pallas_call(
        body, name=name,
        out_shape=[SDS((N_DEV, *x.shape), x.dtype) for x in xs],
        in_specs=[pl.BlockSpec(memory_space=pl.ANY)] * n,
        out_specs=[pl.BlockSpec(memory_space=pl.ANY)] * n,
        scratch_shapes=_comm_scratch(n),
    )(*xs)


def _exchange_peers():
    mx, my, mc = lax.axis_index("x"), lax.axis_index("y"), lax.axis_index("c")
    return [(1 - mx if rel & 4 else mx, 1 - my if rel & 2 else my, 1 - mc if rel & 1 else mc) for rel in range(1, N_DEV)]


def _exchange_copies(x_refs, out_refs, send_sems, recv_sems, local_sems):
    mx, my, mc = lax.axis_index("x"), lax.axis_index("y"), lax.axis_index("c")
    me = 4 * mx + 2 * my + mc
    copies = []
    for a, (x_ref, out_ref) in enumerate(zip(x_refs, out_refs)):
        mine = pltpu.make_async_copy(x_ref.at[me], out_ref.at[me], local_sems.at[a])
        mine.start()
        copies.append(mine)
        for k, (px, py, pc) in enumerate(_exchange_peers()):
            cp = pltpu.make_async_remote_copy(
                src_ref=x_ref.at[4 * px + 2 * py + pc], dst_ref=out_ref.at[me],
                send_sem=send_sems.at[7 * a + k], recv_sem=recv_sems.at[7 * a + k],
                device_id=(px, py, pc), device_id_type=MESH)
            cp.start()
            copies.append(cp)
    for cp in copies:
        cp.wait()


def all_to_all8(xs, name):
    n = len(xs)

    def body(*refs):
        _exchange_copies(refs[:n], refs[n:2 * n], *refs[2 * n:])

    return pl.pallas_call(
        body, name=name,
        out_shape=[SDS(x.shape, x.dtype) for x in xs],
        in_specs=[pl.BlockSpec(memory_space=pl.ANY)] * n,
        out_specs=[pl.BlockSpec(memory_space=pl.ANY)] * n,
        scratch_shapes=_comm_scratch(n),
    )(*xs)


def _sequencer_kernel(name, collective_id, n_arrays):
    return pl.kernel(
        mesh=plsc.ScalarSubcoreMesh(axis_name="seq", num_cores=1), name=name,
        scratch_types=tuple(_comm_scratch(n_arrays)),
        compiler_params=pltpu.CompilerParams(collective_id=collective_id))


def _handshake(peers):
    barrier = pltpu.get_barrier_semaphore()
    for peer in peers:
        pl.semaphore_signal(barrier, inc=1, device_id=peer, device_id_type=MESH)
    pl.semaphore_wait(barrier, len(peers))


def _hbm_refs(xs, out_shapes):
    x_refs = [jax.new_ref(x, memory_space=pltpu.MemorySpace.HBM) for x in xs]
    out_refs = [jax.empty_ref(SDS(shp, x.dtype), memory_space=pltpu.MemorySpace.HBM) for x, shp in zip(xs, out_shapes)]
    return x_refs, out_refs


def sc_all_gather8(xs, name, collective_id):
    x_refs, out_refs = _hbm_refs(xs, [(N_DEV, *x.shape) for x in xs])

    @_sequencer_kernel(name, collective_id, len(xs))
    def launch(send_sems, recv_sems, local_sems):
        _handshake(_gather_peers())
        _gather_copies(x_refs, out_refs, send_sems, recv_sems, local_sems)

    launch()
    return [ref[...] for ref in out_refs]


def sc_all_to_all8(xs, name, collective_id):
    x_refs, out_refs = _hbm_refs(xs, [x.shape for x in xs])

    @_sequencer_kernel(name, collective_id, len(xs))
    def launch(send_sems, recv_sems, local_sems):
        _handshake(_exchange_peers())
        _exchange_copies(x_refs, out_refs, send_sems, recv_sems, local_sems)

    launch()
    return [ref[...] for ref in out_refs]


def norm_mod(x, w, sc, sh, name):
    b, s, d = x.shape
    tm = min(512, s)

    def body(x_ref, w_ref, sc_ref, sh_ref, h_ref):
        xv = x_ref[...]
        n = xv * _rms(xv)
        h_ref[...] = ((n * w_ref[...]) * (1.0 + sc_ref[...]) + sh_ref[...]).astype(BF16)

    return pl.pallas_call(
        body, name=name, grid=(b, s // tm),
        in_specs=[_row(tm, d), _full((1, d)), _bvec(d), _bvec(d)],
        out_specs=_row(tm, d), out_shape=SDS((b, s, d), BF16), compiler_params=_cparams(2))(x, w, sc, sh)


def ffn_up(h, wg_t, wu_t, name):
    b, s, d = h.shape
    f = wg_t.shape[0]
    tm, tn = min(512, s), f // 2

    def body(h_ref, wg_ref, wu_ref, s_ref, t_ref, a_ref):
        hv = h_ref[...]
        g = lax.dot_general(hv, wg_ref[...], NT_DIMS, preferred_element_type=F32)
        u = lax.dot_general(hv, wu_ref[...], NT_DIMS, preferred_element_type=F32)
        sg = _sigmoid(g)
        silu = g * sg
        s_ref[...] = silu.astype(s_ref.dtype)
        t_ref[...] = (u * (sg + silu * (1.0 - sg))).astype(t_ref.dtype)
        a_ref[...] = (silu * u).astype(BF16)

    hs = pl.BlockSpec((None, tm, d), lambda j, bb, i: (bb, i, 0))
    ws = pl.BlockSpec((tn, d), lambda j, bb, i: (j, 0))
    os_ = pl.BlockSpec((None, tm, tn), lambda j, bb, i: (bb, i, j))
    return pl.pallas_call(
        body, name=name, grid=(f // tn, b, s // tm),
        in_specs=[hs, ws, ws], out_specs=[os_, os_, os_],
        out_shape=[SDS((b, s, f), SAVED_ACT), SDS((b, s, f), SAVED_ACT), SDS((b, s, f), BF16)],
        compiler_params=_cparams(3))(h, wg_t, wu_t)


def _norm_mod_tile(xv, w_ref, sc_ref, sh_ref):
    return ((xv * _rms(xv) * w_ref[...]) * (1.0 + sc_ref[...]) + sh_ref[...]).astype(BF16)


def ffn_down(a, wd, x, gate, scale, name, above=None):
    b, s, f = a.shape
    d = wd.shape[1]
    tm = min(512, s)

    def body(a_ref, wd_ref, x_ref, g_ref, *rest):
        xn_ref, o_ref = rest[-3:-1] if above else rest
        o = jnp.dot(a_ref[...], wd_ref[...], preferred_element_type=F32)
        xn = x_ref[...] + (scale * g_ref[...]) * o
        xn_ref[...] = xn
        o_ref[...] = o.astype(BF16)
        if above:
            rest[-1][...] = _norm_mod_tile(xn, *rest[0:3])

    extra = above is not None
    return pl.pallas_call(
        body, name=name, grid=(b, s // tm),
        in_specs=[_row(tm, f), _full((f, d)), _row(tm, d), _bvec(d)] + ([_full((1, d)), _bvec(d), _bvec(d)] if extra else []),
        out_specs=[_row(tm, d), _row(tm, d)] + ([_row(tm, d)] if extra else []),
        out_shape=[SDS((b, s, d), F32), SDS((b, s, d), BF16)] + ([SDS((b, s, d), BF16)] if extra else []),
        compiler_params=_cparams(2))(a, wd, x, gate, *(above or ()))


def ffn_down_final(a, wd, x, gate, scale, w_final, tgt, name):
    b, s, f = a.shape
    d = wd.shape[1]
    tm = min(512, s)

    def body(a_ref, wd_ref, x_ref, g_ref, w_ref, t_ref, loss_ref, dx_ref, dw_ref, do_ref, dg_ref):
        @pl.when(_first_step())
        def _():
            loss_ref[...] = jnp.zeros_like(loss_ref)
            dw_ref[...] = jnp.zeros_like(dw_ref)

        @pl.when(pl.program_id(1) == 0)
        def _():
            dg_ref[...] = jnp.zeros_like(dg_ref)
        o = jnp.dot(a_ref[...], wd_ref[...], preferred_element_type=F32)
        sg = scale * g_ref[...]
        xv = x_ref[...] + sg * o
        r = _rms(xv)
        n = xv * r
        wv = w_ref[...]
        e = n * wv - t_ref[...]
        loss_ref[...] += jnp.sum(e * e) * (0.5 / d)
        dy = e * (1.0 / d)
        dw_ref[...] += jnp.sum(dy * n, axis=0, keepdims=True)
        dx = _rms_bwd(dy * wv, n, r)
        dx_ref[...] = dx
        do_ref[...] = (sg * dx).astype(BF16)
        dg_ref[...] += jnp.sum(scale * dx * o, axis=0, keepdims=True)

    return pl.pallas_call(
        body, name=name, grid=(b, s // tm),
        in_specs=[_row(tm, f), _full((f, d)), _row(tm, d), _bvec(d), _full((1, d)), _row(tm, d)],
        out_specs=[_full((1, LANES)), _row(tm, d), _full((1, d)), _row(tm, d), _bvec(d)],
        out_shape=[SDS((1, LANES), F32), SDS((b, s, d), F32), SDS((1, d), F32), SDS((b, s, d), BF16), SDS((b, 1, d), F32)],
        compiler_params=_cparams(2))(a, wd, x, gate, w_final, tgt)


def ffn_dact(do, wd, silu_g, u_dsilu, name):
    b, s, d = do.shape
    f = wd.shape[0]
    tm, tn = min(512, s), f // 2

    def body(do_ref, wd_ref, s_ref, t_ref, dg_ref, du_ref):
        da = lax.dot_general(do_ref[...], wd_ref[...], NT_DIMS, preferred_element_type=F32)
        dg_ref[...] = (da * t_ref[...].astype(F32)).astype(BF16)
        du_ref[...] = (da * s_ref[...].astype(F32)).astype(BF16)

    dos = pl.BlockSpec((None, tm, d), lambda j, bb, i: (bb, i, 0))
    ws = pl.BlockSpec((tn, d), lambda j, bb, i: (j, 0))
    es = pl.BlockSpec((None, tm, tn), lambda j, bb, i: (bb, i, j))
    return pl.pallas_call(
        body, name=name, grid=(f // tn, b, s // tm),
        in_specs=[dos, ws, es, es], out_specs=[es, es],
        out_shape=[SDS((b, s, f), BF16), SDS((b, s, f), BF16)], compiler_params=_cparams(3))(do, wd, silu_g, u_dsilu)


def mm_tn(a, bm, tma, tnb, name):
    b, s, ka = a.shape
    nb = bm.shape[2]
    tk = min(2048, s)
    nk = s // tk

    def body(a_ref, b_ref, o_ref, acc):
        first = (pl.program_id(2) == 0) & (pl.program_id(3) == 0)
        last = (pl.program_id(2) == b - 1) & (pl.program_id(3) == nk - 1)
        part = lax.dot_general(a_ref[...], b_ref[...], TN_DIMS, preferred_element_type=F32)

        @pl.when(first)
        def _():
            acc[...] = part

        @pl.when(jnp.logical_not(first))
        def _():
            acc[...] += part

        @pl.when(last)
        def _():
            o_ref[...] = acc[...].astype(BF16)

    return pl.pallas_call(
        body, name=name, grid=(ka // tma, nb // tnb, b, nk),
        in_specs=[pl.BlockSpec((None, tk, tma), lambda i, j, bb, k: (bb, k, i)),
                  pl.BlockSpec((None, tk, tnb), lambda i, j, bb, k: (bb, k, j))],
        out_specs=pl.BlockSpec((tma, tnb), lambda i, j, bb, k: (i, j)),
        out_shape=SDS((ka, nb), BF16), scratch_shapes=[pltpu.VMEM((tma, tnb), F32)],
        compiler_params=_cparams(4))(a, bm)


def _gate_bwd_specs(tm, d, b, s):
    return ([_row(tm, d), _bvec(d)], [_row(tm, d), _bvec(d)], [SDS((b, s, d), BF16), SDS((b, 1, d), F32)])


def _gate_bwd_tile(dx, scale, o_ref, g_ref, do_ref, dg_ref):
    do_ref[...] = ((scale * g_ref[...]) * dx).astype(BF16)
    dg_ref[...] += jnp.sum(scale * dx * o_ref[...].astype(F32), axis=0, keepdims=True)


def n_in_bytes(arrs):
    return sum(a.size * a.dtype.itemsize for a in arrs)


def dh_norm_bwd(dys, wts, x, dxn, w, sc, name, below=None):
    b, s, d = x.shape
    tm = min(512 if n_in_bytes(wts) <= 8 * 1024 * 1024 else 256, s)
    n_in = len(dys)
    extra_in, extra_out, extra_shape = _gate_bwd_specs(tm, d, b, s) if below else ([], [], [])

    def body(*refs):
        dy_refs, w_refs = refs[:n_in], refs[n_in:2 * n_in]
        x_ref, dxn_ref, nw_ref, sc_ref = refs[2 * n_in:2 * n_in + 4]
        rest = refs[2 * n_in + 4:]
        if below:
            o_ref, g_ref, dx_ref, dsc_ref, dsh_ref, dw_ref, do_ref, dg_ref = rest
        else:
            dx_ref, dsc_ref, dsh_ref, dw_ref = rest

        @pl.when(pl.program_id(1) == 0)
        def _():
            dsc_ref[...] = jnp.zeros_like(dsc_ref)
            dsh_ref[...] = jnp.zeros_like(dsh_ref)
            if below:
                dg_ref[...] = jnp.zeros_like(dg_ref)

        @pl.when(_first_step())
        def _():
            dw_ref[...] = jnp.zeros_like(dw_ref)

        dh = jnp.dot(dy_refs[0][...], w_refs[0][...], preferred_element_type=F32)
        for k in range(1, n_in):
            dh += jnp.dot(dy_refs[k][...], w_refs[k][...], preferred_element_type=F32)
        xv = x_ref[...]
        r = _rms(xv)
        n = xv * r
        nw = nw_ref[...]
        dsc_ref[...] += jnp.sum(dh * (n * nw), axis=0, keepdims=True)
        dsh_ref[...] += jnp.sum(dh, axis=0, keepdims=True)
        dhn = dh * (1.0 + sc_ref[...])
        dw_ref[...] += jnp.sum(dhn * n, axis=0, keepdims=True)
        dx = dxn_ref[...] + _rms_bwd(dhn * nw, n, r)
        dx_ref[...] = dx
        if below:
            _gate_bwd_tile(dx, below[2], o_ref, g_ref, do_ref, dg_ref)

    in_specs = [_row(tm, dy.shape[2]) for dy in dys] + [_full(wt.shape) for wt in wts]
    in_specs += [_row(tm, d), _row(tm, d), _full((1, d)), _bvec(d)] + extra_in
    return pl.pallas_call(
        body, name=name, grid=(b, s // tm), in_specs=in_specs,
        out_specs=[_row(tm, d), _bvec(d), _bvec(d), _full((1, d))] + extra_out,
        out_shape=[SDS((b, s, d), F32), SDS((b, 1, d), F32), SDS((b, 1, d), F32), SDS((1, d), F32)] + extra_shape,
        compiler_params=_cparams(2))(*dys, *wts, x, dxn, w, sc, *(below[:2] if below else ()))


def in_proj(h, win_t, name):
    b, s, d = h.shape
    tm = min(512, s)
    widths = (D_SSD, D_SSD + 2 * SSD_GROUPS * SSD_STATE, Q_LORA, KV_LORA, LANES)

    def body(h_ref, w_ref, *outs):
        p = lax.dot_general(h_ref[...], w_ref[...], NT_DIMS, preferred_element_type=F32)
        off = 0
        for o_ref, wd in zip(outs, widths):
            o_ref[...] = p[:, off:off + wd]
            off += wd

    return pl.pallas_call(
        body, name=name, grid=(b, s // tm),
        in_specs=[_row(tm, d), _full(win_t.shape)],
        out_specs=[_row(tm, wd) for wd in widths],
        out_shape=[SDS((b, s, wd), F32) for wd in widths], compiler_params=_cparams(2))(h, win_t)


def _halo_prev(ts, d):
    return pl.BlockSpec((None, 8, d), lambda b, i: (b, jnp.maximum(i * (ts // 8) - 1, 0), 0))


CONV_ROWS = 32


def _conv_head(head, u_ref, up_ref):
    head[0:8, :] = jnp.where(pl.program_id(1) > 0, up_ref[...], 0.0)
    head[8:8 + CONV_ROWS, :] = u_ref[0:CONV_ROWS, :]


def _conv_windows(u_ref, head, r0):
    if r0 == 0:
        return [head[5 + k:5 + k + CONV_ROWS, :] for k in range(4)]
    return [u_ref[r0 - 3 + k:r0 - 3 + k + CONV_ROWS, :] for k in range(4)]


def _fold8(t):
    acc = t[0:8, :]
    for r in range(8, CONV_ROWS, 8):
        acc += t[r:r + 8, :]
    return acc


def conv_fwd(u, cw, cb, name):
    b, s, dc = u.shape
    ts = min(512, s)
    widths = (D_SSD, SSD_GROUPS * SSD_STATE, SSD_GROUPS * SSD_STATE)

    def body(u_ref, up_ref, w_ref, b_ref, xs_ref, bm_ref, cm_ref, head):
        _conv_head(head, u_ref, up_ref)
        ws = [w_ref[k:k + 1, :] for k in range(4)]
        bias = b_ref[...]
        for r0 in range(0, ts, CONV_ROWS):
            taps = _conv_windows(u_ref, head, r0)
            v = bias + taps[0] * ws[0] + taps[1] * ws[1] + taps[2] * ws[2] + taps[3] * ws[3]
            y = v * _sigmoid(v)
            rs = slice(r0, r0 + CONV_ROWS)
            xs_ref[rs, :] = y[:, 0:D_SSD]
            bm_ref[rs, :] = y[:, D_SSD:D_SSD + 256]
            cm_ref[rs, :] = y[:, D_SSD + 256:D_SSD + 512]

    return pl.pallas_call(
        body, name=name, grid=(b, s // ts),
        in_specs=[_row(ts, dc), _halo_prev(ts, dc), _full((4, dc)), _full((1, dc))],
        out_specs=[_row(ts, wd) for wd in widths],
        out_shape=[SDS((b, s, wd), F32) for wd in widths],
        scratch_shapes=[pltpu.VMEM((8 + CONV_ROWS, dc), F32)], compiler_params=_cparams(2))(u, u, cw, cb)


def conv_bwd_a(dxs, dbm, dcm, u, cw, cb, name):
    b, s, dc = u.shape
    ts = min(512, s)

    def body(dxs_ref, dbm_ref, dcm_ref, u_ref, up_ref, w_ref, b_ref, dv_ref, dwb_ref, head):
        @pl.when(_first_step())
        def _():
            dwb_ref[...] = jnp.zeros_like(dwb_ref)
        _conv_head(head, u_ref, up_ref)
        ws = [w_ref[k:k + 1, :] for k in range(4)]
        bias = b_ref[...]
        for r0 in range(0, ts, CONV_ROWS):
            taps = _conv_windows(u_ref, head, r0)
            v = bias + taps[0] * ws[0] + taps[1] * ws[1] + taps[2] * ws[2] + taps[3] * ws[3]
            sg = _sigmoid(v)
            rs = slice(r0, r0 + CONV_ROWS)
            dy = jnp.concatenate([dxs_ref[rs, :], dbm_ref[rs, :], dcm_ref[rs, :]], axis=1)
            dv = dy * (sg * (1.0 + v * (1.0 - sg)))
            dv_ref[rs, :] = dv
            for k in range(4):
                dwb_ref[8 * k:8 * k + 8, :] += _fold8(dv * taps[k])
            dwb_ref[32:40, :] += _fold8(dv)

    return pl.pallas_call(
        body, name=name, grid=(b, s // ts),
        in_specs=[_row(ts, D_SSD), _row(ts, 256), _row(ts, 256), _row(ts, dc), _halo_prev(ts, dc),
                  _full((4, dc)), _full((1, dc))],
        out_specs=[_row(ts, dc), _full((40, dc))],
        out_shape=[SDS((b, s, dc), F32), SDS((40, dc), F32)],
        scratch_shapes=[pltpu.VMEM((8 + CONV_ROWS, dc), F32)], compiler_params=_cparams(2))(dxs, dbm, dcm, u, u, cw, cb)


def conv_grads_fold(x, name):
    c = x.shape[1]

    def body(x_ref, o_ref):
        o_ref[...] = jnp.zeros_like(o_ref)
        for k in range(5):
            o_ref[k:k + 1, :] = jnp.sum(x_ref[8 * k:8 * k + 8, :], axis=0, keepdims=True)

    return pl.pallas_call(body, name=name, out_shape=SDS((8, c), F32))(x)


def conv_bwd_b(dv, cw, name):
    b, s, dc = dv.shape
    ts = min(512, s)
    nt = s // ts

    def body(dv_ref, dn_ref, w_ref, du_ref, tail):
        tail[0:CONV_ROWS, :] = dv_ref[ts - CONV_ROWS:ts, :]
        tail[CONV_ROWS:CONV_ROWS + 8, :] = jnp.where(pl.program_id(1) < nt - 1, dn_ref[...], 0.0)
        ws = [w_ref[k:k + 1, :] for k in range(4)]
        for r0 in range(0, ts, CONV_ROWS):
            if r0 == ts - CONV_ROWS:
                win = [tail[3 - k:3 - k + CONV_ROWS, :] for k in range(4)]
            else:
                win = [dv_ref[r0 + 3 - k:r0 + 3 - k + CONV_ROWS, :] for k in range(4)]
            acc = win[0] * ws[0] + win[1] * ws[1] + win[2] * ws[2] + win[3] * ws[3]
            du_ref[r0:r0 + CONV_ROWS, :] = acc.astype(BF16)

    nxt = pl.BlockSpec((None, 8, dc), lambda bb, i: (bb, jnp.minimum((i + 1) * (ts // 8), s // 8 - 1), 0))
    return pl.pallas_call(
        body, name=name, grid=(b, nt),
        in_specs=[_row(ts, dc), nxt, _full((4, dc))],
        out_specs=_row(ts, dc), out_shape=SDS((b, s, dc), BF16),
        scratch_shapes=[pltpu.VMEM((CONV_ROWS + 8, dc), F32)], compiler_params=_cparams(2))(dv, dv, cw)


def _ssd_common(misc_ref, dtb_ref, alog_ref, e_ref):
    ln = CHUNK
    lane = lax.broadcasted_iota(I32, (ln, LANES), 1)
    lane1 = lax.broadcasted_iota(I32, (1, LANES), 1)
    pre = misc_ref[...] + dtb_ref[...]
    dt_s = jnp.where(lane < SSD_HEADS, _softplus(pre), 0.0)
    a_neg = jnp.where(lane1 < SSD_HEADS, -jnp.exp(alog_ref[...]), 0.0)
    ri = lax.broadcasted_iota(I32, (ln, ln), 0)
    ci = lax.broadcasted_iota(I32, (ln, ln), 1)
    tril = ci <= ri
    acum = jnp.dot(tril.astype(F32), dt_s * a_neg, preferred_element_type=F32, precision=HI)
    both_e = _dot_01(jnp.concatenate([dt_s, acum], axis=0), e_ref[...], 3)
    dt_e, acum_e = both_e[0:ln], both_e[ln:2 * ln]
    return dict(pre=pre, dt_s=dt_s, a_neg=a_neg, tril=tril, ri=ri, ci=ci, acum=acum, acum_t=acum.T,
                dt_e=dt_e, eac_e=jnp.exp(acum_e), del_e=jnp.exp(acum_e[ln - 1:ln, :] - acum_e))


def _dot_01(x, m01, terms):
    acc, rest = None, x
    for k in range(terms):
        part = rest.astype(BF16)
        if k + 1 < terms:
            rest = rest - part.astype(F32)
        d = jnp.dot(part, m01, preferred_element_type=F32)
        acc = d if acc is None else acc + d
    return acc


def _decay(cm, h):
    seg = cm["acum"][:, h:h + 1] - cm["acum_t"][h:h + 1, :]
    return jnp.exp(jnp.where(cm["tril"], seg, -jnp.inf))


def ssd_fwd(xs, bm, cm_, misc, z, dtb, alog, dskip_e, norm_w, e_mat, name):
    b, s, _ = xs.shape
    ln, nc = CHUNK, s // CHUNK
    gw = D_SSD // SSD_GROUPS
    hpg = SSD_HEADS // SSD_GROUPS

    def body(xs_ref, b_ref, c_ref, misc_ref, z_ref, dtb_ref, alog_ref, dsk_ref, nw_ref, e_ref,
             ys_ref, y_ref, p_ref, st, yd):
        @pl.when(pl.program_id(1) == 0)
        def _():
            st[...] = jnp.zeros_like(st)
        cm = _ssd_common(misc_ref, dtb_ref, alog_ref, e_ref)
        xsv = xs_ref[...]
        xdt = xsv * cm["dt_e"]
        xdt_b = xdt.astype(BF16)
        xd_b = (xdt * cm["del_e"]).astype(BF16)
        gam_e = cm["eac_e"][ln - 1:ln, :]
        p_ref[...] = st[...]
        groups = [slice(gw * g, gw * (g + 1)) for g in range(SSD_GROUPS)]
        heads = [slice(SSD_HEAD_DIM * h, SSD_HEAD_DIM * (h + 1)) for h in range(SSD_HEADS)]
        bgs = [b_ref[:, SSD_STATE * g:SSD_STATE * (g + 1)].astype(BF16) for g in range(SSD_GROUPS)]
        cgs = [c_ref[:, SSD_STATE * g:SSD_STATE * (g + 1)].astype(BF16) for g in range(SSD_GROUPS)]
        cbs = [lax.dot_general(cg, bg, NT_DIMS, preferred_element_type=F32) for cg, bg in zip(cgs, bgs)]
        sts = [st[:, gs] for gs in groups]
        yoff = [jnp.dot(cg, st_g.astype(BF16), preferred_element_type=F32) * cm["eac_e"][:, gs]
                for cg, st_g, gs in zip(cgs, sts, groups)]
        news = [lax.dot_general(bg, xd_b[:, gs], TN_DIMS, preferred_element_type=F32) for bg, gs in zip(bgs, groups)]
        for gs, st_g, new in zip(groups, sts, news):
            st[:, gs] = st_g * gam_e[:, gs] + new
        ms = [(cbs[h // hpg] * _decay(cm, h)).astype(BF16) for h in range(SSD_HEADS)]
        for h, hs in enumerate(heads):
            yd[:, hs] = jnp.dot(ms[h], xdt_b[:, hs], preferred_element_type=F32)
        y = yd[...] + jnp.concatenate(yoff, axis=1) + dsk_ref[...] * xsv
        y_ref[...] = y
        zz = z_ref[...]
        yg = y * (zz * _sigmoid(zz))
        outs = []
        for g in range(SSD_GROUPS):
            ygg = yg[:, gw * g:gw * (g + 1)]
            outs.append(ygg * _rms(ygg) * nw_ref[:, gw * g:gw * (g + 1)])
        ys_ref[...] = jnp.concatenate(outs, axis=1).astype(BF16)

    row = lambda d: pl.BlockSpec((None, ln, d), lambda bb, c: (bb, c, 0))
    return pl.pallas_call(
        body, name=name, grid=(b, nc),
        in_specs=[row(D_SSD), row(256), row(256), row(LANES), row(D_SSD), _full((1, LANES)), _full((1, LANES)),
                  _full((1, D_SSD)), _full((1, D_SSD)), _full((LANES, D_SSD))],
        out_specs=[row(D_SSD), row(D_SSD), pl.BlockSpec((None, None, SSD_STATE, D_SSD), lambda bb, c: (bb, c, 0, 0))],
        out_shape=[SDS((b, s, D_SSD), BF16), SDS((b, s, D_SSD), F32), SDS((b, nc, SSD_STATE, D_SSD), F32)],
        scratch_shapes=[pltpu.VMEM((SSD_STATE, D_SSD), F32), pltpu.VMEM((ln, D_SSD), F32)],
        compiler_params=_cparams(2))(xs, bm, cm_, misc, z, dtb, alog, dskip_e, norm_w, e_mat)


def ssd_bwd(dys, y, z, xs, bm, cm_, misc, prev, dtb, alog, dskip_e, norm_w, e_mat, et_mat, name):
    b, s, _ = xs.shape
    ln, nc = CHUNK, s // CHUNK
    gw = D_SSD // SSD_GROUPS
    hpg = SSD_HEADS // SSD_GROUPS

    def body(dys_ref, y_ref, z_ref, xs_ref, b_ref, c_ref, misc_ref, p_ref, dtb_ref, alog_ref, dsk_ref, nw_ref,
             e_ref, et_ref, dxs_ref, db_ref, dc_ref, dz_ref, ddt_ref, dnw_ref, ddsk_ref, ddtb_ref, dalog_ref,
             dst, dxd, dac_t):
        @pl.when(_first_step())
        def _():
            for r_ in (dnw_ref, ddsk_ref, ddtb_ref, dalog_ref):
                r_[...] = jnp.zeros_like(r_)

        @pl.when(pl.program_id(1) == 0)
        def _():
            dst[...] = jnp.zeros_like(dst)

        cm = _ssd_common(misc_ref, dtb_ref, alog_ref, e_ref)
        et = et_ref[...]
        squeeze = lambda t: _dot_01(t, et, 2)
        lane = lax.broadcasted_iota(I32, (ln, LANES), 1)
        sub = lax.broadcasted_iota(I32, (LANES, ln), 0)
        xsv = xs_ref[...]
        xdt = xsv * cm["dt_e"]
        xdt_b = xdt.astype(BF16)
        xd_b = (xdt * cm["del_e"]).astype(BF16)
        eac_e = cm["eac_e"]
        gam_e = eac_e[ln - 1:ln, :]

        yv, zz, dyo = y_ref[...], z_ref[...], dys_ref[...]
        sz = _sigmoid(zz)
        silu_z = zz * sz
        yg = yv * silu_z
        dyg, dnw = [], []
        for g in range(SSD_GROUPS):
            gs = slice(gw * g, gw * (g + 1))
            ygg = yg[:, gs]
            r = _rms(ygg)
            n = ygg * r
            dnw.append(jnp.sum(dyo[:, gs] * n, axis=0, keepdims=True))
            dyg.append(_rms_bwd(dyo[:, gs] * nw_ref[:, gs], n, r))
        dyg = jnp.concatenate(dyg, axis=1)
        dnw_ref[...] += jnp.concatenate(dnw, axis=1)
        dz_ref[...] = (dyg * yv * (sz * (1.0 + zz * (1.0 - sz)))).astype(BF16)
        dy = dyg * silu_z
        ddsk_ref[...] += jnp.sum(dy * xsv, axis=0, keepdims=True)
        dy_b = dy.astype(BF16)

        dacum = jnp.zeros((ln, LANES), F32)
        dac_t[...] = jnp.zeros_like(dac_t)
        w1, dgam = [], []
        for g in range(SSD_GROUPS):
            gs = slice(gw * g, gw * (g + 1))
            ss = slice(SSD_STATE * g, SSD_STATE * (g + 1))
            bg = b_ref[:, ss].astype(BF16)
            cg = c_ref[:, ss].astype(BF16)
            cb = lax.dot_general(cg, bg, NT_DIMS, preferred_element_type=F32)
            pt = p_ref[:, gs]
            pt_b = pt.astype(BF16)
            dst_g = dst[:, gs]
            dst_b = dst_g.astype(BF16)
            edy = (dy[:, gs] * eac_e[:, gs]).astype(BF16)
            dcg = lax.dot_general(edy, pt_b, NT_DIMS, preferred_element_type=F32)
            dpt = lax.dot_general(cg, edy, TN_DIMS, preferred_element_type=F32)
            yoff = jnp.dot(cg, pt_b, preferred_element_type=F32) * eac_e[:, gs]
            dxd_g = jnp.dot(bg, dst_b, preferred_element_type=F32)
            dbg = lax.dot_general(xd_b[:, gs], dst_b, NT_DIMS, preferred_element_type=F32)
            ddel = dxd_g * xdt[:, gs] * cm["del_e"][:, gs]
            w1.append(dy[:, gs] * yoff - ddel)
            dgam.append(jnp.sum(ddel, axis=0, keepdims=True) + jnp.sum(dst_g * pt, axis=0, keepdims=True) * gam_e[:, gs])
            dxd[:, gs] = dxd_g * cm["del_e"][:, gs]
            dst[:, gs] = dst_g * gam_e[:, gs] + dpt
            dcb = jnp.zeros((ln, ln), F32)
            for j in range(hpg):
                h = hpg * g + j
                hs = slice(SSD_HEAD_DIM * h, SSD_HEAD_DIM * (h + 1))
                lam = _decay(cm, h)
                m = cb * lam
                dm = lax.dot_general(dy_b[:, hs], xdt_b[:, hs], NT_DIMS, preferred_element_type=F32)
                dxd[:, hs] += lax.dot_general(m.astype(BF16), dy_b[:, hs], TN_DIMS, preferred_element_type=F32)
                dcb += dm * lam
                wl = dm * m
                dacum += jnp.where(lane == h, jnp.sum(wl, axis=1, keepdims=True), 0.0)
                dac_t[...] -= jnp.where(sub == h, jnp.sum(wl, axis=0, keepdims=True), 0.0)
            dcb_b = dcb.astype(BF16)
            dc_ref[:, ss] = dcg + jnp.dot(dcb_b, bg, preferred_element_type=F32)
            db_ref[:, ss] = dbg + lax.dot_general(dcb_b, cg, TN_DIMS, preferred_element_type=F32)

        dxdt = dxd[...]
        dxs_ref[...] = dy * dsk_ref[...] + dxdt * cm["dt_e"]
        dacum += squeeze(jnp.concatenate(w1, axis=1)) + dac_t[...].T
        dlast = squeeze(jnp.broadcast_to(jnp.concatenate(dgam, axis=1), (8, D_SSD)))[0:1, :]
        dacum += jnp.where(lax.broadcasted_iota(I32, (ln, LANES), 0) == ln - 1, dlast, 0.0)
        triu = (cm["ci"] >= cm["ri"]).astype(F32)
        da = jnp.dot(triu, dacum, preferred_element_type=F32, precision=HI)
        ddt = da * cm["a_neg"] + squeeze(dxdt * xsv)
        dalog_ref[...] += jnp.sum(da * cm["dt_s"], axis=0, keepdims=True) * cm["a_neg"]
        ddt_raw = jnp.where(lane < SSD_HEADS, ddt * _sigmoid(cm["pre"]), 0.0)
        ddt_ref[...] = ddt_raw
        ddtb_ref[...] += jnp.sum(ddt_raw, axis=0, keepdims=True)

    row = lambda d: pl.BlockSpec((None, ln, d), lambda bb, c: (bb, nc - 1 - c, 0))
    return pl.pallas_call(
        body, name=name, grid=(b, nc),
        in_specs=[row(D_SSD), row(D_SSD), row(D_SSD), row(D_SSD), row(256), row(256), row(LANES),
                  pl.BlockSpec((None, None, SSD_STATE, D_SSD), lambda bb, c: (bb, nc - 1 - c, 0, 0)),
                  _full((1, LANES)), _full((1, LANES)), _full((1, D_SSD)), _full((1, D_SSD)),
                  _full((LANES, D_SSD)), _full((D_SSD, LANES))],
        out_specs=[row(D_SSD), row(256), row(256), row(D_SSD), row(LANES),
                   _full((1, D_SSD)), _full((1, D_SSD)), _full((1, LANES)), _full((1, LANES))],
        out_shape=[SDS((b, s, D_SSD), F32), SDS((b, s, 256), F32), SDS((b, s, 256), F32), SDS((b, s, D_SSD), BF16),
                   SDS((b, s, LANES), F32), SDS((1, D_SSD), F32), SDS((1, D_SSD), F32), SDS((1, LANES), F32),
                   SDS((1, LANES), F32)],
        scratch_shapes=[pltpu.VMEM((SSD_STATE, D_SSD), F32), pltpu.VMEM((ln, D_SSD), F32), pltpu.VMEM((LANES, ln), F32)],
        compiler_params=_cparams(2))(dys, y, z, xs, bm, cm_, misc, prev, dtb, alog, dskip_e, norm_w, e_mat, et_mat)


def _rope(xv, cc, sp, sm):
    n = xv.shape[1]
    return xv * cc + pltpu.roll(xv, 16, 1) * sp + pltpu.roll(xv, n - 16, 1) * sm


def _rope_bwd(dy, cc, sp, sm):
    n = dy.shape[1]
    return dy * cc + pltpu.roll(dy * sp, n - 16, 1) + pltpu.roll(dy * sm, 16, 1)


def _tile8(t):
    return jnp.concatenate([t] * MLA_HEADS, axis=1)


def qkv_fwd(cq, ckv, misc, cc, sp, sm, qnw, kvnw, wuq_t, wukv_t, place, name):
    b, s, _ = cq.shape
    tm = min(512, s)
    hd = MLA_HEADS * HEAD_PAD

    def body(cq_ref, ckv_ref, misc_ref, cc_ref, sp_ref, sm_ref, qnw_ref, kvnw_ref, wq_ref, wkv_ref, pl_ref,
             q_ref, k_ref, v_ref, qn_ref, kvn_ref):
        cqv, ckvv = cq_ref[...], ckv_ref[...]
        qn = (cqv * _rms(cqv) * qnw_ref[...]).astype(BF16)
        kvn = (ckvv * _rms(ckvv) * kvnw_ref[...]).astype(BF16)
        qn_ref[...] = qn
        kvn_ref[...] = kvn
        cc1, sp1, sm1 = cc_ref[...], sp_ref[...], sm_ref[...]
        q = lax.dot_general(qn, wq_ref[...], NT_DIMS, preferred_element_type=F32)
        q_ref[...] = _rope(q, _tile8(cc1), _tile8(sp1), _tile8(sm1)).astype(BF16)
        kv = lax.dot_general(kvn, wkv_ref[...], NT_DIMS, preferred_element_type=F32)
        kr = jnp.dot(misc_ref[...], pl_ref[...], preferred_element_type=F32, precision=HI)
        kr = _rope(kr, cc1, sp1, sm1)
        k_ref[...] = (kv[:, 0:hd] + _tile8(kr)).astype(BF16)
        v_ref[...] = kv[:, hd:2 * hd].astype(BF16)

    return pl.pallas_call(
        body, name=name, grid=(b, s // tm),
        in_specs=[_row(tm, Q_LORA), _row(tm, KV_LORA), _row(tm, LANES), _row(tm, LANES), _row(tm, LANES), _row(tm, LANES),
                  _full((1, Q_LORA)), _full((1, KV_LORA)), _full(wuq_t.shape), _full(wukv_t.shape), _full((LANES, LANES))],
        out_specs=[_row(tm, hd), _row(tm, hd), _row(tm, hd), _row(tm, Q_LORA), _row(tm, KV_LORA)],
        out_shape=[SDS((b, s, hd), BF16)] * 3 + [SDS((b, s, Q_LORA), BF16), SDS((b, s, KV_LORA), BF16)],
        compiler_params=_cparams(2))(cq, ckv, misc, cc, sp, sm, qnw, kvnw, wuq_t, wukv_t, place)


def qkv_bwd(dq, dk, dv, ddt, cq, ckv, cc, sp, sm, qnw, kvnw, wuq_t, wukv_t, place_t, name):
    b, s, _ = cq.shape
    tm = min(512, s)
    hd = MLA_HEADS * HEAD_PAD

    def body(dq_ref, dk_ref, dv_ref, ddt_ref, cq_ref, ckv_ref, cc_ref, sp_ref, sm_ref, qnw_ref, kvnw_ref,
             wq_ref, wkv_ref, plt_ref, dcq_ref, dckv_ref, dmisc_ref, dqp_ref, dkv_ref, dqnw_ref, dkvnw_ref):
        @pl.when(_first_step())
        def _():
            dqnw_ref[...] = jnp.zeros_like(dqnw_ref)
            dkvnw_ref[...] = jnp.zeros_like(dkvnw_ref)
        cc1, sp1, sm1 = cc_ref[...], sp_ref[...], sm_ref[...]
        dqp = _rope_bwd(dq_ref[...].astype(F32), _tile8(cc1), _tile8(sp1), _tile8(sm1)).astype(BF16)
        dqp_ref[...] = dqp
        dkv_b = jnp.concatenate([dk_ref[...], dv_ref[...]], axis=1)
        dkf = dk_ref[...].astype(F32)
        dkv_ref[...] = dkv_b
        dkr = dkf[:, 0:HEAD_PAD]
        for h in range(1, MLA_HEADS):
            dkr += dkf[:, HEAD_PAD * h:HEAD_PAD * (h + 1)]
        dkr = _rope_bwd(dkr, cc1, sp1, sm1)
        dmisc_ref[...] = (jnp.dot(dkr, plt_ref[...], preferred_element_type=F32, precision=HI) + ddt_ref[...]).astype(BF16)

        def norm_bwd(dn_w, xv, w_ref, dw_ref, dx_ref):
            r = _rms(xv)
            n = xv * r
            dw_ref[...] += jnp.sum(dn_w * n, axis=0, keepdims=True)
            dx_ref[...] = _rms_bwd(dn_w * w_ref[...], n, r).astype(BF16)

        norm_bwd(jnp.dot(dqp, wq_ref[...], preferred_element_type=F32), cq_ref[...], qnw_ref, dqnw_ref, dcq_ref)
        norm_bwd(jnp.dot(dkv_b, wkv_ref[...], preferred_element_type=F32), ckv_ref[...], kvnw_ref, dkvnw_ref, dckv_ref)

    return pl.pallas_call(
        body, name=name, grid=(b, s // tm),
        in_specs=[_row(tm, hd), _row(tm, hd), _row(tm, hd), _row(tm, LANES), _row(tm, Q_LORA), _row(tm, KV_LORA),
                  _row(tm, LANES), _row(tm, LANES), _row(tm, LANES), _full((1, Q_LORA)), _full((1, KV_LORA)),
                  _full(wuq_t.shape), _full(wukv_t.shape), _full((LANES, LANES))],
        out_specs=[_row(tm, Q_LORA), _row(tm, KV_LORA), _row(tm, LANES), _row(tm, hd), _row(tm, 2 * hd),
                   _full((1, Q_LORA)), _full((1, KV_LORA))],
        out_shape=[SDS((b, s, Q_LORA), BF16), SDS((b, s, KV_LORA), BF16), SDS((b, s, LANES), BF16),
                   SDS((b, s, hd), BF16), SDS((b, s, 2 * hd), BF16), SDS((1, Q_LORA), F32), SDS((1, KV_LORA), F32)],
        compiler_params=_cparams(2))(dq, dk, dv, ddt, cq, ckv, cc, sp, sm, qnw, kvnw, wuq_t, wukv_t, place_t)


ATT_SCALE = 1.0 / math.sqrt(QK_DIM)
LOG2E = math.log2(math.e)
ATT_SCALE_LOG2E = ATT_SCALE * LOG2E


ATT_HEADS_PER_STEP = 4
ATT_HEADS_PER_STEP_BWD = 2


def _att_tile(s):
    return min(512, s)


def flash_fwd(q, k, v, name):
    b, s, hd = q.shape
    t = _att_tile(s)
    nb = s // t
    th = t // 2
    vt = v.reshape(b, nb, t, MLA_HEADS, HEAD_PAD).transpose(0, 3, 1, 4, 2)

    hps = ATT_HEADS_PER_STEP
    hw = hps * HEAD_PAD

    def body(q_ref, k_ref, vt_ref, o_ref, lse_ref, m_s, l_s, acc):
        i = pl.program_id(2)
        m_s[...] = jnp.full_like(m_s, -jnp.inf)
        l_s[...] = jnp.zeros_like(l_s)
        acc[...] = jnp.zeros_like(acc)

        def update(j, diagonal):
            ks = pl.ds(pl.multiple_of(j * t, t), t)
            chains = [(hh, half) for hh in range(hps) for half in range(2)]
            lanes = lambda hh: slice(HEAD_PAD * hh, HEAD_PAD * (hh + 1))
            cols = lambda half: slice(th * half, th * (half + 1))
            sts = {}
            for hh, half in chains:
                st = lax.dot_general(k_ref[ks, lanes(hh)], q_ref[cols(half), lanes(hh)], NT_DIMS,
                                     preferred_element_type=F32)
                if diagonal:
                    row = lax.broadcasted_iota(I32, (t, th), 0)
                    col = lax.broadcasted_iota(I32, (t, th), 1) + th * half
                    st = jnp.where(row <= col, st, -jnp.inf)
                sts[hh, half] = st
            pts, alphas = {}, {}
            for hh, half in chains:
                st, cs = sts[hh, half], cols(half)
                m_prev = m_s[hh, :, cs]
                m_new = jnp.maximum(m_prev, jnp.max(st, axis=0, keepdims=True))
                alpha = jnp.exp2((m_prev - m_new) * ATT_SCALE_LOG2E)
                pt = jnp.exp2((st - m_new) * ATT_SCALE_LOG2E)
                l_s[hh, :, cs] = alpha * l_s[hh, :, cs] + jnp.sum(pt, axis=0, keepdims=True)
                m_s[hh, :, cs] = m_new
                pts[hh, half], alphas[hh, half] = pt.astype(BF16), alpha
            for hh, half in chains:
                cs = cols(half)
                acc[hh, :, cs] = alphas[hh, half] * acc[hh, :, cs] + jnp.dot(vt_ref[hh, j], pts[hh, half],
                                                                             preferred_element_type=F32)

        def step(j, carry):
            update(j, False)
            return carry

        lax.fori_loop(0, i, step, 0)
        update(i, True)
        for hh in range(hps):
            o_ref[:, HEAD_PAD * hh:HEAD_PAD * (hh + 1)] = (acc[hh] / l_s[hh]).T
            lse_ref[hh] = m_s[hh] * ATT_SCALE + jnp.log(l_s[hh])

    qs = pl.BlockSpec((None, t, hw), lambda bb, h, i: (bb, i, h))
    ks = pl.BlockSpec((None, s, hw), lambda bb, h, i: (bb, 0, h))
    vs = pl.BlockSpec((None, hps, nb, HEAD_PAD, t), lambda bb, h, i: (bb, h, 0, 0, 0))
    ls = pl.BlockSpec((None, hps, None, 1, t), lambda bb, h, i: (bb, h, i, 0, 0))
    return pl.pallas_call(
        body, name=name, grid=(b, MLA_HEADS // hps, nb),
        in_specs=[qs, ks, vs], out_specs=[qs, ls],
        out_shape=[SDS((b, s, hd), F32), SDS((b, MLA_HEADS, nb, 1, t), F32)],
        scratch_shapes=[pltpu.VMEM((hps, 1, t), F32), pltpu.VMEM((hps, 1, t), F32), pltpu.VMEM((hps, HEAD_PAD, t), F32)],
        compiler_params=_cparams(3))(q, k, vt)


def flash_bwd(q, k, v, do, lse, dlt, name):
    b, s, hd = q.shape
    t = _att_tile(s)
    nb = s // t
    th = t // 2
    lse_r = lse
    dlt_r = dlt.reshape(b, MLA_HEADS, nb, 1, t)

    hps = ATT_HEADS_PER_STEP_BWD
    hw = hps * HEAD_PAD

    def body(q_ref, k_ref, v_ref, do_ref, lse_ref, dlt_ref, dq_ref, dk_ref, dv_ref, dq_s, dk_s, dv_s):
        dq_s[...] = jnp.zeros_like(dq_s)
        dk_s[...] = jnp.zeros_like(dk_s)
        dv_s[...] = jnp.zeros_like(dv_s)

        def tile(j, i, diagonal):
            qs = pl.ds(pl.multiple_of(i * t, t), t)
            chains = [(hh, half) for hh in range(hps) for half in range(2)]
            lanes = lambda hh: slice(HEAD_PAD * hh, HEAD_PAD * (hh + 1))
            keys = lambda half: pl.ds(pl.multiple_of(j * t + th * half, th), th)
            sts, dpts = {}, {}
            for hh, half in chains:
                ls_, ks = lanes(hh), keys(half)
                st = lax.dot_general(k_ref[ks, ls_], q_ref[qs, ls_], NT_DIMS, preferred_element_type=F32)
                if diagonal:
                    row = lax.broadcasted_iota(I32, (th, t), 0) + th * half
                    col = lax.broadcasted_iota(I32, (th, t), 1)
                    st = jnp.where(row <= col, st, -jnp.inf)
                sts[hh, half] = st
                dpts[hh, half] = lax.dot_general(v_ref[ks, ls_], do_ref[qs, ls_], NT_DIMS, preferred_element_type=F32)
            pts, dsts = {}, {}
            for hh, half in chains:
                pt = jnp.exp2(sts[hh, half] * ATT_SCALE_LOG2E - lse_ref[hh, i] * LOG2E)
                pts[hh, half] = pt.astype(BF16)
                dsts[hh, half] = (pt * (dpts[hh, half] - dlt_ref[hh, i])).astype(BF16)
            for hh in range(hps):
                ls_ = lanes(hh)
                dq_acc = None
                for half in range(2):
                    ks = keys(half)
                    dv_s[ks, ls_] += jnp.dot(pts[hh, half], do_ref[qs, ls_], preferred_element_type=F32)
                    dk_s[ks, ls_] += jnp.dot(dsts[hh, half], q_ref[qs, ls_], preferred_element_type=F32)
                    part = lax.dot_general(dsts[hh, half], k_ref[ks, ls_], TN_DIMS, preferred_element_type=F32)
                    dq_acc = part if dq_acc is None else dq_acc + part
                dq_s[qs, ls_] += dq_acc

        def key_tile(j, carry):
            tile(j, j, True)

            def query_tile(i, c2):
                tile(j, i, False)
                return c2

            lax.fori_loop(j + 1, nb, query_tile, 0)
            return carry

        lax.fori_loop(0, nb, key_tile, 0)
        dq_ref[...] = (dq_s[...] * ATT_SCALE).astype(BF16)
        dk_ref[...] = (dk_s[...] * ATT_SCALE).astype(BF16)
        dv_ref[...] = dv_s[...].astype(BF16)

    hs = pl.BlockSpec((None, s, hw), lambda bb, h: (bb, 0, h))
    ls = pl.BlockSpec((None, hps, nb, 1, t), lambda bb, h: (bb, h, 0, 0, 0))
    return pl.pallas_call(
        body, name=name, grid=(b, MLA_HEADS // hps),
        in_specs=[hs, hs, hs, hs, ls, ls], out_specs=[hs, hs, hs],
        out_shape=[SDS((b, s, hd), BF16)] * 3, scratch_shapes=[pltpu.VMEM((s, hw), F32)] * 3,
        compiler_params=_cparams(2))(q, k, v, do, lse_r, dlt_r)


def out_proj(ys, attn, mnw, wo, x, gate, above, name):
    b, s, d = x.shape
    tm = min(512, s)

    def body(ys_ref, at_ref, mnw_ref, wo_ref, x_ref, g_ref, nw_ref, sc_ref, sh_ref, xn_ref, o_ref, ym_ref, h_ref):
        av = at_ref[...]
        ym = (av * _rms(av) * mnw_ref[...]).astype(BF16)
        ym_ref[...] = ym
        o = jnp.dot(ys_ref[...], wo_ref[0:D_SSD, :], preferred_element_type=F32)
        o += jnp.dot(ym, wo_ref[D_SSD:2 * D_SSD, :], preferred_element_type=F32)
        xn = x_ref[...] + g_ref[...] * o
        xn_ref[...] = xn
        o_ref[...] = o.astype(BF16)
        h_ref[...] = _norm_mod_tile(xn, nw_ref, sc_ref, sh_ref)

    return pl.pallas_call(
        body, name=name, grid=(b, s // tm),
        in_specs=[_row(tm, D_SSD), _row(tm, D_SSD), _full((1, D_SSD)), _full(wo.shape), _row(tm, d), _bvec(d),
                  _full((1, d)), _bvec(d), _bvec(d)],
        out_specs=[_row(tm, d), _row(tm, d), _row(tm, D_SSD), _row(tm, d)],
        out_shape=[SDS((b, s, d), F32), SDS((b, s, d), BF16), SDS((b, s, D_SSD), BF16), SDS((b, s, d), BF16)],
        compiler_params=_cparams(2))(ys, attn, mnw, wo, x, gate, *above)


def out_proj_bwd(dout, attn, mnw, wo, name):
    b, s, d = dout.shape
    tm = min(512, s)

    def body(do_ref, at_ref, mnw_ref, wo_ref, dys_ref, dat_ref, dlt_ref, dw_ref):
        lane = lax.broadcasted_iota(I32, (tm, LANES), 1)
        @pl.when(_first_step())
        def _():
            dw_ref[...] = jnp.zeros_like(dw_ref)
        dov = do_ref[...]
        dys_ref[...] = lax.dot_general(dov, wo_ref[0:D_SSD, :], NT_DIMS, preferred_element_type=F32)
        dym = lax.dot_general(dov, wo_ref[D_SSD:2 * D_SSD, :], NT_DIMS, preferred_element_type=F32)
        av = at_ref[...]
        r = _rms(av)
        n = av * r
        dw_ref[...] += jnp.sum(dym * n, axis=0, keepdims=True)
        dat = _rms_bwd(dym * mnw_ref[...], n, r)
        dat_ref[...] = dat.astype(BF16)
        prod = dat * av
        cols = jnp.zeros((tm, LANES), F32)
        for h in range(MLA_HEADS):
            cols += jnp.where(lane == h, jnp.sum(prod[:, HEAD_PAD * h:HEAD_PAD * (h + 1)], axis=1, keepdims=True), 0.0)
        dlt_ref[...] = cols.T[0:MLA_HEADS, :]

    return pl.pallas_call(
        body, name=name, grid=(b, s // tm),
        in_specs=[_row(tm, d), _row(tm, D_SSD), _full((1, D_SSD)), _full(wo.shape)],
        out_specs=[_row(tm, D_SSD), _row(tm, D_SSD),
                   pl.BlockSpec((None, MLA_HEADS, tm), lambda bb, i: (bb, 0, i)), _full((1, D_SSD))],
        out_shape=[SDS((b, s, D_SSD), F32), SDS((b, s, D_SSD), BF16), SDS((b, MLA_HEADS, s), F32),
                   SDS((1, D_SSD), F32)],
        compiler_params=_cparams(2))(dout, attn, mnw, wo)


def adaln_fwd(c_all, w_ada, b_ada, name):
    nb, d = c_all.shape
    n = w_ada.shape[1]

    def body(c_ref, w_ref, b_ref, m_ref, ca_ref):
        cv = c_ref[...]
        ca = (cv * _sigmoid(cv)).astype(BF16)
        ca_ref[...] = ca
        m_ref[...] = jnp.dot(ca, w_ref[...].astype(BF16), preferred_element_type=F32) + b_ref[...]

    return pl.pallas_call(
        body, name=name, out_shape=[SDS((nb, n), F32), SDS((nb, d), BF16)],
        compiler_params=pltpu.CompilerParams(vmem_limit_bytes=VMEM_LIMIT))(c_all, w_ada, b_ada)


def adaln_bwd(c_act, dmod_cols, name):
    d, n = c_act.shape[1], dmod_cols.shape[1]

    def body(c_ref, dm_ref, gw_ref):
        gw_ref[...] = lax.dot_general(c_ref[...], dm_ref[...].astype(BF16), TN_DIMS, preferred_element_type=F32)

    return pl.pallas_call(
        body, name=name, out_shape=SDS((d, n), F32),
        compiler_params=pltpu.CompilerParams(vmem_limit_bytes=VMEM_LIMIT))(c_act, dmod_cols)


def sum_rows(x, name):
    def body(x_ref, o_ref):
        o_ref[...] = jnp.sum(x_ref[...], axis=0, keepdims=True)
    return pl.pallas_call(body, name=name, out_shape=SDS((1, x.shape[1]), F32))(x)


def squeeze_heads(x, et_mat, name):
    def body(x_ref, et_ref, o_ref):
        xv = jnp.broadcast_to(x_ref[...], (8, x.shape[1]))
        o_ref[...] = _dot_01(xv, et_ref[...], 3)[0:1, :]
    return pl.pallas_call(body, name=name, out_shape=SDS((1, LANES), F32))(x, et_mat)


def sum_blocks(x, name):
    n, r, c = x.shape
    tr = next(cand for cand in (256, 128, 64, 32, 16, 8) if r % cand == 0)

    def body(x_ref, o_ref):
        acc = x_ref[0].astype(F32)
        for k in range(1, n):
            acc += x_ref[k].astype(F32)
        o_ref[...] = acc

    return pl.pallas_call(
        body, name=name, grid=(r // tr,), in_specs=[pl.BlockSpec((n, tr, c), lambda i: (0, i, 0))],
        out_specs=pl.BlockSpec((tr, c), lambda i: (i, 0)), out_shape=SDS((r, c), F32),
        compiler_params=_cparams(1))(x)


def _adam_math(w, g, m, v):
    m = ADAM_B1 * m + (1.0 - ADAM_B1) * g
    v = ADAM_B2 * v + (1.0 - ADAM_B2) * (g * g)
    m_hat = m / (1.0 - ADAM_B1 ** ADAM_STEP)
    v_hat = v / (1.0 - ADAM_B2 ** ADAM_STEP)
    return -ADAM_LR * (m_hat / (jnp.sqrt(v_hat) + ADAM_EPS) + ADAM_WD * w), m, v


def adamw(w, g, m, v, name):
    r, c = w.shape[-2:]
    tr = r
    for cand in (512, 256, 128, 64, 32, 16, 8):
        if r % cand == 0 and cand * c * 4 <= 2 * 1024 * 1024:
            tr = cand
            break

    def body(w_ref, g_ref, m_ref, v_ref, d_ref, mo_ref, vo_ref):
        d_ref[...], mo_ref[...], vo_ref[...] = _adam_math(w_ref[...], g_ref[...], m_ref[...], v_ref[...])

    def spec(a):
        return pl.BlockSpec((tr, c), lambda i: (i, 0)) if a.ndim == 2 else pl.BlockSpec((None, tr, c), lambda i: (0, i, 0))

    return pl.pallas_call(
        body, name=name, grid=(r // tr,), in_specs=[spec(w), spec(g), spec(m), spec(v)], out_specs=[spec(w)] * 3,
        out_shape=[SDS(w.shape, F32)] * 3, compiler_params=_cparams(1))(w, g, m, v)


def adamw_blocks(w, blocks, m, v, name):
    r, c = w.shape
    tr = next((cand for cand in (128, 64, 32, 16, 8) if r % cand == 0), r)

    def body(w_ref, b_ref, m_ref, v_ref, g_ref, d_ref, mo_ref, vo_ref):
        g = b_ref[0].astype(F32)
        for k in range(1, N_DEV):
            g += b_ref[k].astype(F32)
        g_ref[...] = g
        d_ref[...], mo_ref[...], vo_ref[...] = _adam_math(w_ref[...], g, m_ref[...], v_ref[...])

    spec = pl.BlockSpec((tr, c), lambda i: (i, 0))
    return pl.pallas_call(
        body, name=name, grid=(r // tr,),
        in_specs=[spec, pl.BlockSpec((N_DEV, tr, c), lambda i: (0, i, 0)), spec, spec], out_specs=[spec] * 4,
        out_shape=[SDS((r, c), F32)] * 4, compiler_params=_cparams(1))(w, blocks, m, v)


def adamw_many(ws, gs, ms, vs, name):
    n = len(ws)

    def body(*refs):
        w_r, g_r, m_r, v_r = (refs[k * n:(k + 1) * n] for k in range(4))
        d_r, mo_r, vo_r = (refs[(4 + k) * n:(5 + k) * n] for k in range(3))
        for k in range(n):
            d_r[k][...], mo_r[k][...], vo_r[k][...] = _adam_math(w_r[k][...], g_r[k][...], m_r[k][...], v_r[k][...])

    shapes = [SDS(w.shape, F32) for w in ws]
    outs = pl.pallas_call(body, name=name, out_shape=shapes * 3)(*ws, *gs, *ms, *vs)
    return outs[:n], outs[n:2 * n], outs[2 * n:]


TRANSPOSED = ("ffn1_w_gate", "ffn1_w_up", "ffn2_w_gate", "ffn2_w_up", "w_in", "w_ukv", "w_uq")
GATHER_GROUPS = (("ffn1_w_gate", "ffn1_w_up"), ("ffn1_w_down",),
                 ("w_in", "w_ukv", "w_uq", "w_out", "ffn2_w_gate", "ffn2_w_up", "ffn2_w_down"))
GRAD_GROUPS = (("ffn2", ("ffn2_w_gate", "ffn2_w_up", "ffn2_w_down")), ("mixer", ("w_out", "w_in", "w_ukv", "w_uq")),
               ("ffn1_down", ("ffn1_w_down",)), ("ffn1_gate", ("ffn1_w_gate",)), ("ffn1_up", ("ffn1_w_up",)))


def _shard_view(name, w):
    return w[0].T if name in TRANSPOSED else w[0]


def _shard_unview(name, t):
    return t.T[None] if name in TRANSPOSED else t[None]


def _grad_blocks(name, gw):
    if name == "w_in":
        return _in_proj_rows_inv(gw).reshape(N_DEV, -1, D_MODEL)
    if name == "w_ukv":
        hd = MLA_HEADS * HEAD_PAD
        return jnp.concatenate([gw[:hd].reshape(MLA_HEADS, HEAD_PAD, KV_LORA)[:, :QK_NOPE],
                                gw[hd:].reshape(MLA_HEADS, V_HEAD, KV_LORA)], axis=1)
    if name == "w_uq":
        return gw.reshape(MLA_HEADS, HEAD_PAD, Q_LORA)[:, :QK_DIM]
    return gw.reshape(N_DEV, -1, D_MODEL)


def _pack_rows(arrs):
    parts = []
    for a in arrs:
        flat = a.reshape(-1).astype(F32)
        pad = (-flat.shape[0]) % D_MODEL
        if pad:
            flat = jnp.pad(flat, (0, pad))
        parts.append(flat.reshape(-1, D_MODEL))
    out = jnp.concatenate(parts, axis=0)
    pad = (-out.shape[0]) % 8
    if pad:
        out = jnp.pad(out, ((0, pad), (0, 0)))
    return out


def _unpack_rows(packed, shapes):
    out, row = [], 0
    for shp in shapes:
        n = math.prod(shp)
        nrow = -(-n // D_MODEL)
        out.append(packed[row:row + nrow].reshape(-1)[:n].reshape(shp))
        row += nrow
    return out


def _in_proj_rows(w_t):
    return jnp.concatenate([w_t[0:2560], w_t[2576:2960], w_t[2960:3216], w_t[2560:2576], w_t[3216:3248],
                            jnp.zeros((D_IN_PAD - D_IN, D_MODEL), w_t.dtype)], axis=0)


def _in_proj_rows_inv(d):
    return jnp.concatenate([d[0:2560], d[3200:3216], d[2560:2944], d[2944:3200], d[3216:3248]], axis=0)


def _rope_tables(positions):
    inv_freq = ROPE_THETA ** (-jnp.arange(0, QK_ROPE, 2, dtype=F32) / QK_ROPE)
    ang = positions[..., None].astype(F32) * inv_freq
    cos, sin = jnp.cos(ang), jnp.sin(ang)
    one = jnp.ones(ang.shape[:2] + (QK_NOPE,), F32)
    zero = jnp.zeros_like(one)
    z16, z32, o32 = zero[..., :16], zero[..., :32], one[..., :32]
    cc = jnp.concatenate([one, cos, cos, o32], axis=-1)
    sp = jnp.concatenate([zero, z16, sin, z32], axis=-1)
    sm = jnp.concatenate([zero, -sin, z16, z32], axis=-1)
    return cc, sp, sm


def weight_views(gathered):
    full = lambda name: gathered[name].reshape(-1, gathered[name].shape[2])
    ukv = full("w_ukv").reshape(MLA_HEADS, QK_NOPE + V_HEAD, KV_LORA)
    wukv_t = jnp.concatenate([jnp.pad(ukv[:, :QK_NOPE], ((0, 0), (0, HEAD_PAD - QK_NOPE), (0, 0))).reshape(-1, KV_LORA),
                              ukv[:, QK_NOPE:].reshape(-1, KV_LORA)], axis=0)
    uq = full("w_uq").reshape(MLA_HEADS, QK_DIM, Q_LORA)
    wuq_t = jnp.pad(uq, ((0, 0), (0, HEAD_PAD - QK_DIM), (0, 0))).reshape(-1, Q_LORA)
    return dict(wg1_t=full("ffn1_w_gate"), wu1_t=full("ffn1_w_up"), wd1=full("ffn1_w_down"),
                wg2_t=full("ffn2_w_gate"), wu2_t=full("ffn2_w_up"), wd2=full("ffn2_w_down"),
                wo=full("w_out"), win_t=_in_proj_rows(full("w_in")), wukv_t=wukv_t, wuq_t=wuq_t)


def _ffn_bwd(tag, dxn, do, dgate, x, h, gg, uu, a, sc, norm_w, wg_t, wu_t, wd, below):
    f2 = wd.shape[0] // 2
    dwd = mm_tn(a, do, f2, D_MODEL, tag + "_dwd")
    dgg, duu = ffn_dact(do, wd, gg, uu, tag + "_dact")
    dwg_t = mm_tn(dgg, h, f2, D_MODEL, tag + "_dwg")
    dwu_t = mm_tn(duu, h, f2, D_MODEL, tag + "_dwu")
    dx, dsc, dsh, dnw, *nxt = dh_norm_bwd([dgg, duu], [wg_t, wu_t], x, dxn, norm_w, sc, tag + "_dh", below)
    return dx, (dsh, dsc, dgate), dnw, (dwg_t, dwu_t, dwd), nxt


def local_step(x, tgt, positions, mod, wv, p):
    nb, s, d = x.shape
    sh1, sc1, g1, sh2, sc2, g2, sh3, sc3, g3 = mod
    cc, sp, sm = _rope_tables(positions)
    lane_head = jnp.arange(D_SSD, dtype=I32)[None, :] // SSD_HEAD_DIM
    e_mat = (lane_head == jnp.arange(LANES, dtype=I32)[:, None]).astype(BF16)
    et_mat = e_mat.T
    rr, cl = jnp.arange(LANES, dtype=I32)[:, None], jnp.arange(LANES, dtype=I32)[None, :]
    place = ((cl == rr + (QK_NOPE - SSD_HEADS)) & (rr >= SSD_HEADS) & (rr < SSD_HEADS + QK_ROPE)).astype(F32)
    dtb = jnp.pad(p["dt_bias"], ((0, 0), (0, LANES - SSD_HEADS)))
    alog = jnp.pad(p["a_log"], ((0, 0), (0, LANES - SSD_HEADS)))
    dskip_e = jnp.repeat(p["d_skip"], SSD_HEAD_DIM, axis=1)

    h1 = norm_mod(x, p["norm_ffn1"], sc1, sh1, "ffn1_norm")
    gg1, uu1, a1 = ffn_up(h1, wv["wg1_t"], wv["wu1_t"], "ffn1_up")
    x1, o1, h2 = ffn_down(a1, wv["wd1"], x, g1, 0.5, "ffn1_down", (p["norm_mix"], sc2, sh2))
    z, u, cq, ckv, misc = in_proj(h2, wv["win_t"], "in_proj")
    xs, bm, cm_ = conv_fwd(u, p["conv_w"], p["conv_b"], "conv_fwd")
    ys, y, prev = ssd_fwd(xs, bm, cm_, misc, z, dtb, alog, dskip_e, p["ssd_norm_w"], e_mat, "ssd_fwd")
    q, k, v, qn, kvn = qkv_fwd(cq, ckv, misc, cc, sp, sm, p["q_norm_w"], p["kv_norm_w"], wv["wuq_t"], wv["wukv_t"],
                               place, "qkv_fwd")
    attn, lse = flash_fwd(q, k, v, "flash_fwd")
    x2, o2, ym, h3 = out_proj(ys, attn, p["mla_norm_w"], wv["wo"], x1, g2, (p["norm_ffn2"], sc3, sh3), "out_proj")
    gg3, uu3, a3 = ffn_up(h3, wv["wg2_t"], wv["wu2_t"], "ffn2_up")
    loss, dx3, dnfin, do3, dg3 = ffn_down_final(a3, wv["wd2"], x2, g3, 0.5, p["norm_final"], tgt, "ffn2_down_loss")

    dx2, dmod3, dnf2, (dwg2, dwu2, dwd2), (dout, dg2) = _ffn_bwd(
        "ffn2", dx3, do3, dg3, x2, h3, gg3, uu3, a3, sc3, p["norm_ffn2"], wv["wg2_t"], wv["wu2_t"], wv["wd2"],
        (o2, g2, 1.0))
    dys, dattn, dlt, dmlan = out_proj_bwd(dout, attn, p["mla_norm_w"], wv["wo"], "out_proj_bwd")
    dwo = jnp.concatenate([mm_tn(ys, dout, D_SSD, D_MODEL, "dwo_ssd"), mm_tn(ym, dout, D_SSD, D_MODEL, "dwo_mla")], axis=0)
    dxs, dbm, dcm, dz, ddt, dssdn, ddsk_lane, ddtb, dalog = ssd_bwd(
        dys, y, z, xs, bm, cm_, misc, prev, dtb, alog, dskip_e, p["ssd_norm_w"], e_mat, et_mat, "ssd_bwd")
    dq, dk, dv = flash_bwd(q, k, v, dattn, lse, dlt, "flash_bwd")
    dcq, dckv, dmisc, dqp, dkvc, dqn, dkvn = qkv_bwd(dq, dk, dv, ddt, cq, ckv, cc, sp, sm, p["q_norm_w"], p["kv_norm_w"],
                                                     wv["wuq_t"], wv["wukv_t"], place.T, "qkv_bwd")
    dwuq = mm_tn(dqp, qn, MLA_HEADS * HEAD_PAD, Q_LORA, "dwuq")
    dwukv = mm_tn(dkvc, kvn, MLA_HEADS * HEAD_PAD, KV_LORA, "dwukv")
    dvv, dconv = conv_bwd_a(dxs, dbm, dcm, u, p["conv_w"], p["conv_b"], "conv_bwd_a")
    dconv = conv_grads_fold(dconv, "conv_grads_fold")
    du = conv_bwd_b(dvv, p["conv_w"], "conv_bwd_b")
    dproj = jnp.concatenate([dz, du, dcq, dckv, dmisc], axis=-1)
    dwin = mm_tn(dproj, h2, D_IN_PAD // 2, D_MODEL, "dwin")
    dx1, dsc2, dsh2, dnmix, do1, dg1 = dh_norm_bwd([dproj], [wv["win_t"]], x1, dx2, p["norm_mix"], sc2, "mix_dh",
                                                   (o1, g1, 0.5))
    dx0, dmod1, dnf1, (dwg1, dwu1, dwd1), _ = _ffn_bwd(
        "ffn1", dx1, do1, dg1, x, h1, gg1, uu1, a1, sc1, p["norm_ffn1"], wv["wg1_t"], wv["wu1_t"], wv["wd1"], None)

    dmod = jnp.concatenate([*dmod1, dsh2, dsc2, dg2, *dmod3], axis=1).reshape(nb, N_MOD * d)
    return dict(
        loss=loss, dx=dx0, dmod=dmod, norm_ffn1=dnf1, norm_mix=dnmix, norm_ffn2=dnf2, norm_final=dnfin,
        ssd_norm_w=dssdn, mla_norm_w=dmlan, q_norm_w=dqn, kv_norm_w=dkvn,
        dt_bias=ddtb[:, :SSD_HEADS], a_log=dalog[:, :SSD_HEADS],
        d_skip=squeeze_heads(ddsk_lane, et_mat, "d_skip_heads")[:, :SSD_HEADS],
        conv_b=dconv[4:5], conv_w=dconv[0:4],
        gw=dict(ffn1_w_gate=dwg1, ffn1_w_up=dwu1, ffn1_w_down=dwd1, ffn2_w_gate=dwg2, ffn2_w_up=dwu2, ffn2_w_down=dwd2,
                w_out=dwo, w_in=dwin, w_ukv=dwukv, w_uq=dwuq))


def kernel(x, c, positions, w_ada, b_ada, norm_ffn1, ffn1_w_gate, ffn1_w_up, ffn1_w_down, norm_mix, w_in, conv_w, conv_b, dt_bias, a_log, d_skip, ssd_norm_w, q_norm_w, w_uq, kv_norm_w, w_ukv, mla_norm_w, w_out, norm_ffn2, ffn2_w_gate, ffn2_w_up, ffn2_w_down, norm_final, loss_target, m_w_ada, m_b_ada, m_norm_ffn1, m_ffn1_w_gate, m_ffn1_w_up, m_ffn1_w_down, m_norm_mix, m_w_in, m_conv_w, m_conv_b, m_dt_bias, m_a_log, m_d_skip, m_ssd_norm_w, m_q_norm_w, m_w_uq, m_kv_norm_w, m_w_ukv, m_mla_norm_w, m_w_out, m_norm_ffn2, m_ffn2_w_gate, m_ffn2_w_up, m_ffn2_w_down, m_norm_final, v_w_ada, v_b_ada, v_norm_ffn1, v_ffn1_w_gate, v_ffn1_w_up, v_ffn1_w_down, v_norm_mix, v_w_in, v_conv_w, v_conv_b, v_dt_bias, v_a_log, v_d_skip, v_ssd_norm_w, v_q_norm_w, v_w_uq, v_kv_norm_w, v_w_ukv, v_mla_norm_w, v_w_out, v_norm_ffn2, v_ffn2_w_gate, v_ffn2_w_up, v_ffn2_w_down, v_norm_final):
    names = ["w_ada", "b_ada", "norm_ffn1", "ffn1_w_gate", "ffn1_w_up", "ffn1_w_down", "norm_mix", "w_in", "conv_w",
             "conv_b", "dt_bias", "a_log", "d_skip", "ssd_norm_w", "q_norm_w", "w_uq", "kv_norm_w", "w_ukv",
             "mla_norm_w", "w_out", "norm_ffn2", "ffn2_w_gate", "ffn2_w_up", "ffn2_w_down", "norm_final"]
    W = dict(zip(names, (w_ada, b_ada, norm_ffn1, ffn1_w_gate, ffn1_w_up, ffn1_w_down, norm_mix, w_in, conv_w, conv_b, dt_bias, a_log, d_skip, ssd_norm_w, q_norm_w, w_uq, kv_norm_w, w_ukv, mla_norm_w, w_out, norm_ffn2, ffn2_w_gate, ffn2_w_up, ffn2_w_down, norm_final)))
    M = dict(zip(names, (m_w_ada, m_b_ada, m_norm_ffn1, m_ffn1_w_gate, m_ffn1_w_up, m_ffn1_w_down, m_norm_mix, m_w_in, m_conv_w, m_conv_b, m_dt_bias, m_a_log, m_d_skip, m_ssd_norm_w, m_q_norm_w, m_w_uq, m_kv_norm_w, m_w_ukv, m_mla_norm_w, m_w_out, m_norm_ffn2, m_ffn2_w_gate, m_ffn2_w_up, m_ffn2_w_down, m_norm_final)))
    V = dict(zip(names, (v_w_ada, v_b_ada, v_norm_ffn1, v_ffn1_w_gate, v_ffn1_w_up, v_ffn1_w_down, v_norm_mix, v_w_in, v_conv_w, v_conv_b, v_dt_bias, v_a_log, v_d_skip, v_ssd_norm_w, v_q_norm_w, v_w_uq, v_kv_norm_w, v_w_ukv, v_mla_norm_w, v_w_out, v_norm_ffn2, v_ffn2_w_gate, v_ffn2_w_up, v_ffn2_w_down, v_norm_final)))

    nb, s, d = x.shape
    me = 4 * lax.axis_index("x") + 2 * lax.axis_index("y") + lax.axis_index("c")
    n_ada = w_ada.shape[2]

    taps, n_cw = conv_w.shape[1:]
    (cg,) = all_gather8([_pack_rows([c, conv_w[0]])], "gather_c")
    c_all = cg[:, 0:nb].reshape(N_DEV * nb, d)
    conv_w_full = cg[:, nb, 0:taps * n_cw].reshape(N_DEV, taps, n_cw).transpose(1, 0, 2).reshape(taps, N_DEV * n_cw)
    shards = [[_shard_view(name, W[name]).astype(BF16) for name in group] for group in GATHER_GROUPS]
    gathered = dict(zip(GATHER_GROUPS[0], all_gather8(shards[0], "gather_w_ffn1")))

    b_ada_cols = lax.dynamic_slice(b_ada, (0, me * n_ada), (1, n_ada))
    mod_cols, c_act = adaln_fwd(c_all, w_ada[0], b_ada_cols, "adaln_fwd")
    (mod_g,) = all_gather8([mod_cols], "gather_mod")
    gathered, mod_g, shards = lax.optimization_barrier((gathered, mod_g, shards))
    gathered.update(zip(GATHER_GROUPS[1], sc_all_gather8(shards[1], "gather_w_ffn1_down", 1)))
    gathered.update(zip(GATHER_GROUPS[2], sc_all_gather8(shards[2], "gather_w_rest", 7)))
    wv = weight_views(gathered)
    mod = lax.dynamic_slice(mod_g, (0, me * nb, 0), (N_DEV, nb, n_ada)).transpose(1, 0, 2).reshape(nb, N_MOD, 1, d)
    mod = [mod[:, k] for k in range(N_MOD)]

    P = dict(W)
    P["conv_w"] = conv_w_full
    P["norm_final"] = norm_final.reshape(1, d)
    R = local_step(x, loss_target, positions, mod, wv, P)

    dmod = R["dmod"]
    partial_shapes = [(1,), (1, d), (1, d), (1, d), (1, d), (1, d), (1, d), (1, Q_LORA), (1, KV_LORA),
                      (1, SSD_HEADS), (1, SSD_HEADS), (1, SSD_HEADS), (1, D_CONV), (4, D_CONV), (1, N_MOD * d),
                      (nb, N_MOD * d)]
    partial = _pack_rows([R["loss"][0, :1], R["norm_ffn1"], R["norm_mix"], R["norm_ffn2"], R["norm_final"],
                          R["ssd_norm_w"], R["mla_norm_w"], R["q_norm_w"], R["kv_norm_w"],
                          R["dt_bias"], R["a_log"], R["d_skip"], R["conv_b"], R["conv_w"],
                          sum_rows(dmod, "dmod_rows"), dmod])
    (partial_g,) = all_gather8([partial], "gather_partials")
    (loss, g_nf1, g_nmix, g_nf2, g_nfin, g_ssdn, g_mlan, g_qn, g_kvn, g_dtb, g_alog, g_dskip, g_convb, g_convw,
     g_bada, _) = _unpack_rows(sum_blocks(partial_g, "sum_partials"), partial_shapes)
    dmod_row = sum(-(-math.prod(shp) // D_MODEL) for shp in partial_shapes[:-1])
    dmod_all = partial_g[:, dmod_row:dmod_row + nb * N_MOD].reshape(N_DEV * nb, N_MOD * d)
    g_wada = adaln_bwd(c_act, lax.dynamic_slice(dmod_all, (0, me * n_ada), (N_DEV * nb, n_ada)), "adaln_bwd")
    n_cw = conv_w.shape[2]
    G = {"w_ada": g_wada[None], "b_ada": g_bada, "norm_ffn1": g_nf1, "norm_mix": g_nmix, "norm_ffn2": g_nf2,
         "norm_final": g_nfin.reshape(d), "ssd_norm_w": g_ssdn, "mla_norm_w": g_mlan, "q_norm_w": g_qn,
         "kv_norm_w": g_kvn, "dt_bias": g_dtb, "a_log": g_alog, "d_skip": g_dskip, "conv_b": g_convb,
         "conv_w": lax.dynamic_slice(g_convw, (0, me * n_cw), (4, n_cw))[None]}

    DW, NM, NV = {}, {}, {}
    gw = R["gw"]
    for k, (tag, group) in enumerate(GRAD_GROUPS):
        send = [_grad_blocks(name, gw[name]).reshape(N_DEV, *_shard_view(name, W[name]).shape) for name in group]
        recv = sc_all_to_all8(send, "exchange_" + tag, 2 + k)
        for name, blocks in zip(group, recv):
            res = adamw_blocks(_shard_view(name, W[name]), blocks, _shard_view(name, M[name]), _shard_view(name, V[name]),
                               "adamw_" + name)
            G[name], DW[name], NM[name], NV[name] = [_shard_unview(name, t) for t in res]
    DW["w_ada"], NM["w_ada"], NV["w_ada"] = adamw(w_ada, g_wada, m_w_ada, v_w_ada, "adamw_w_ada")
    small = [n for n in names if n not in DW]
    as2d = lambda a: a.reshape(-1, a.shape[-1])
    outs = adamw_many([as2d(W[n]) for n in small], [as2d(G[n]) for n in small], [as2d(M[n]) for n in small],
                      [as2d(V[n]) for n in small], "adamw_small")
    for res, dst in zip(outs, (DW, NM, NV)):
        for n, t in zip(small, res):
            dst[n] = t.reshape(W[n].shape)
    return (loss.reshape(()), R["dx"], *[G[n] for n in names], *[DW[n] for n in names], *[NM[n] for n in names],
            *[NV[n] for n in names])
```

```python
import math

import jax
import jax.numpy as jnp
from jax import lax
from jax.experimental import pallas as pl
from jax.experimental.pallas import tpu as pltpu
from jax.experimental.pallas import tpu_sc as plsc

F32, BF16, I32 = jnp.float32, jnp.bfloat16, jnp.int32
HI = lax.Precision.HIGHEST
SDS = jax.ShapeDtypeStruct
MESH = pl.DeviceIdType.MESH

D_MODEL = 1024
D_FF = 2816
D_SSD = 1024
SSD_HEADS = 16
SSD_HEAD_DIM = 64
SSD_GROUPS = 2
SSD_STATE = 128
CHUNK = 128
MLA_HEADS = 8
QK_NOPE = 64
QK_ROPE = 32
QK_DIM = 96
V_HEAD = 128
Q_LORA = 384
KV_LORA = 256
ROPE_THETA = 10000.0
N_MOD = 9
EPS = 1e-6
D_CONV = 1536
D_IN = 3248
D_IN_PAD = 3328
HEAD_PAD = 128
N_DEV = 8
ADAM_LR, ADAM_B1, ADAM_B2, ADAM_EPS, ADAM_WD, ADAM_STEP = 0.001, 0.9, 0.999, 1e-08, 0.01, 10

SAVED_ACT = BF16
VMEM_LIMIT = 56 * 1024 * 1024
LANES = 128
NT_DIMS = (((1,), (1,)), ((), ()))
TN_DIMS = (((0,), (0,)), ((), ()))


def _cparams(n_axes):
    return pltpu.CompilerParams(dimension_semantics=("arbitrary",) * n_axes, vmem_limit_bytes=VMEM_LIMIT)


def _row(tm, d):
    return pl.BlockSpec((None, tm, d), lambda b, i: (b, i, 0))


def _bvec(d):
    return pl.BlockSpec((None, 1, d), lambda b, i: (b, 0, 0))


def _full(shape):
    n = len(shape)
    return pl.BlockSpec(shape, lambda *_: (0,) * n)


def _sigmoid(x):
    return 1.0 / (1.0 + jnp.exp(-x))


def _softplus(x):
    return jnp.maximum(x, 0.0) + jnp.log(1.0 + jnp.exp(-jnp.abs(x)))


def _rms(x):
    return lax.rsqrt(jnp.mean(x * x, axis=-1, keepdims=True) + EPS)


def _rms_bwd(dn, n, r):
    return r * (dn - n * jnp.mean(dn * n, axis=-1, keepdims=True))


def _first_step():
    return (pl.program_id(0) == 0) & (pl.program_id(1) == 0)


def _gather_copies(x_refs, out_refs, send_sems, recv_sems, local_sems):
    mx, my, mc = lax.axis_index("x"), lax.axis_index("y"), lax.axis_index("c")
    me, sibling = (mx, my, mc), (mx, my, 1 - mc)
    chips = [(1 - mx, my), (mx, 1 - my), (1 - mx, 1 - my)]

    def copy(a, k, block, to, src=None):
        rows = out_refs[a].at[4 * block[0] + 2 * block[1] + block[2]]
        return pltpu.make_async_remote_copy(
            src_ref=rows if src is None else src, dst_ref=rows,
            send_sem=send_sems.at[7 * a + k], recv_sem=recv_sems.at[7 * a + k], device_id=to, device_id_type=MESH)

    arrays = range(len(x_refs))
    mine = [pltpu.make_async_copy(x_refs[a], out_refs[a].at[4 * mx + 2 * my + mc], local_sems.at[a]) for a in arrays]
    first = [[copy(a, 0, me, sibling, src=x_refs[a])] + [copy(a, 1 + j, me, (*chip, mc), src=x_refs[a])
                                                          for j, chip in enumerate(chips)] for a in arrays]
    passed = [[copy(a, 4 + j, (*chip, mc), sibling) for j, chip in enumerate(chips)] for a in arrays]
    for a in arrays:
        mine[a].start()
        for cp in first[a]:
            cp.start()
    for a in arrays:
        for j, chip in enumerate(chips):
            copy(a, 1 + j, (*chip, mc), me).wait_recv()
            passed[a][j].start()
    for a in arrays:
        copy(a, 0, sibling, me).wait_recv()
        for j, chip in enumerate(chips):
            copy(a, 4 + j, (*chip, 1 - mc), me).wait_recv()
    for a in arrays:
        for cp in first[a] + passed[a]:
            cp.wait_send()
        mine[a].wait()


def _gather_peers():
    mx, my, mc = lax.axis_index("x"), lax.axis_index("y"), lax.axis_index("c")
    return [(mx, my, 1 - mc), (1 - mx, my, mc), (mx, 1 - my, mc), (1 - mx, 1 - my, mc)]


def _comm_scratch(n):
    return [pltpu.SemaphoreType.DMA((7 * n,)), pltpu.SemaphoreType.DMA((7 * n,)), pltpu.SemaphoreType.DMA((n,))]


def all_gather8(xs, name):
    n = len(xs)

    def body(*refs):
        _gather_copies(refs[:n], refs[n:2 * n], *refs[2 * n:])

    return pl.pallas_call(
        body, name=name,
        out_shape=[SDS((N_DEV, *x.shape), x.dtype) for x in xs],
        in_specs=[pl.BlockSpec(memory_space=pl.ANY)] * n,
        out_specs=[pl.BlockSpec(memory_space=pl.ANY)] * n,
        scratch_shapes=_comm_scratch(n),
    )(*xs)


def _exchange_peers():
    mx, my, mc = lax.axis_index("x"), lax.axis_index("y"), lax.axis_index("c")
    return [(1 - mx if rel & 4 else mx, 1 - my if rel & 2 else my, 1 - mc if rel & 1 else mc) for rel in range(1, N_DEV)]


def _exchange_copies(x_refs, out_refs, send_sems, recv_sems, local_sems):
    mx, my, mc = lax.axis_index("x"), lax.axis_index("y"), lax.axis_index("c")
    me = 4 * mx + 2 * my + mc
    copies = []
    for a, (x_ref, out_ref) in enumerate(zip(x_refs, out_refs)):
        mine = pltpu.make_async_copy(x_ref.at[me], out_ref.at[me], local_sems.at[a])
        mine.start()
        copies.append(mine)
        for k, (px, py, pc) in enumerate(_exchange_peers()):
            cp = pltpu.make_async_remote_copy(
                src_ref=x_ref.at[4 * px + 2 * py + pc], dst_ref=out_ref.at[me],
                send_sem=send_sems.at[7 * a + k], recv_sem=recv_sems.at[7 * a + k],
                device_id=(px, py, pc), device_id_type=MESH)
            cp.start()
            copies.append(cp)
    for cp in copies:
        cp.wait()


def all_to_all8(xs, name):
    n = len(xs)

    def body(*refs):
        _exchange_copies(refs[:n], refs[n:2 * n], *refs[2 * n:])

    return pl.pallas_call(
        body, name=name,
        out_shape=[SDS(x.shape, x.dtype) for x in xs],
        in_specs=[pl.BlockSpec(memory_space=pl.ANY)] * n,
        out_specs=[pl.BlockSpec(memory_space=pl.ANY)] * n,
        scratch_shapes=_comm_scratch(n),
    )(*xs)


def _sequencer_kernel(name, collective_id, n_arrays):
    return pl.kernel(
        mesh=plsc.ScalarSubcoreMesh(axis_name="seq", num_cores=1), name=name,
        scratch_types=tuple(_comm_scratch(n_arrays)),
        compiler_params=pltpu.CompilerParams(collective_id=collective_id))


def _handshake(peers):
    barrier = pltpu.get_barrier_semaphore()
    for peer in peers:
        pl.semaphore_signal(barrier, inc=1, device_id=peer, device_id_type=MESH)
    pl.semaphore_wait(barrier, len(peers))


def _hbm_refs(xs, out_shapes):
    x_refs = [jax.new_ref(x, memory_space=pltpu.MemorySpace.HBM) for x in xs]
    out_refs = [jax.empty_ref(SDS(shp, x.dtype), memory_space=pltpu.MemorySpace.HBM) for x, shp in zip(xs, out_shapes)]
    return x_refs, out_refs


def sc_all_gather8(xs, name, collective_id):
    x_refs, out_refs = _hbm_refs(xs, [(N_DEV, *x.shape) for x in xs])

    @_sequencer_kernel(name, collective_id, len(xs))
    def launch(send_sems, recv_sems, local_sems):
        _handshake(_gather_peers())
        _gather_copies(x_refs, out_refs, send_sems, recv_sems, local_sems)

    launch()
    return [ref[...] for ref in out_refs]


def sc_all_to_all8(xs, name, collective_id):
    x_refs, out_refs = _hbm_refs(xs, [x.shape for x in xs])

    @_sequencer_kernel(name, collective_id, len(xs))
    def launch(send_sems, recv_sems, local_sems):
        _handshake(_exchange_peers())
        _exchange_copies(x_refs, out_refs, send_sems, recv_sems, local_sems)

    launch()
    return [ref[...] for ref in out_refs]


def norm_mod(x, w, sc, sh, name):
    b, s, d = x.shape
    tm = min(512, s)

    def body(x_ref, w_ref, sc_ref, sh_ref, h_ref):
        xv = x_ref[...]
        n = xv * _rms(xv)
        h_ref[...] = ((n * w_ref[...]) * (1.0 + sc_ref[...]) + sh_ref[...]).astype(BF16)

    return pl.pallas_call(
        body, name=name, grid=(b, s // tm),
        in_specs=[_row(tm, d), _full((1, d)), _bvec(d), _bvec(d)],
        out_specs=_row(tm, d), out_shape=SDS((b, s, d), BF16), compiler_params=_cparams(2))(x, w, sc, sh)


def ffn_up(h, wg_t, wu_t, name):
    b, s, d = h.shape
    f = wg_t.shape[0]
    tm, tn = min(512, s), f // 2

    def body(h_ref, wg_ref, wu_ref, s_ref, t_ref, a_ref):
        hv = h_ref[...]
        g = lax.dot_general(hv, wg_ref[...], NT_DIMS, preferred_element_type=F32)
        u = lax.dot_general(hv, wu_ref[...], NT_DIMS, preferred_element_type=F32)
        sg = _sigmoid(g)
        silu = g * sg
        s_ref[...] = silu.astype(s_ref.dtype)
        t_ref[...] = (u * (sg + silu * (1.0 - sg))).astype(t_ref.dtype)
        a_ref[...] = (silu * u).astype(BF16)

    hs = pl.BlockSpec((None, tm, d), lambda j, bb, i: (bb, i, 0))
    ws = pl.BlockSpec((tn, d), lambda j, bb, i: (j, 0))
    os_ = pl.BlockSpec((None, tm, tn), lambda j, bb, i: (bb, i, j))
    return pl.pallas_call(
        body, name=name, grid=(f // tn, b, s // tm),
        in_specs=[hs, ws, ws], out_specs=[os_, os_, os_],
        out_shape=[SDS((b, s, f), SAVED_ACT), SDS((b, s, f), SAVED_ACT), SDS((b, s, f), BF16)],
        compiler_params=_cparams(3))(h, wg_t, wu_t)


def _norm_mod_tile(xv, w_ref, sc_ref, sh_ref):
    return ((xv * _rms(xv) * w_ref[...]) * (1.0 + sc_ref[...]) + sh_ref[...]).astype(BF16)


def ffn_down(a, wd, x, gate, scale, name, above=None):
    b, s, f = a.shape
    d = wd.shape[1]
    tm = min(512, s)

    def body(a_ref, wd_ref, x_ref, g_ref, *rest):
        xn_ref, o_ref = rest[-3:-1] if above else rest
        o = jnp.dot(a_ref[...], wd_ref[...], preferred_element_type=F32)
        xn = x_ref[...] + (scale * g_ref[...]) * o
        xn_ref[...] = xn
        o_ref[...] = o.astype(BF16)
        if above:
            rest[-1][...] = _norm_mod_tile(xn, *rest[0:3])

    extra = above is not None
    return pl.pallas_call(
        body, name=name, grid=(b, s // tm),
        in_specs=[_row(tm, f), _full((f, d)), _row(tm, d), _bvec(d)] + ([_full((1, d)), _bvec(d), _bvec(d)] if extra else []),
        out_specs=[_row(tm, d), _row(tm, d)] + ([_row(tm, d)] if extra else []),
        out_shape=[SDS((b, s, d), F32), SDS((b, s, d), BF16)] + ([SDS((b, s, d), BF16)] if extra else []),
        compiler_params=_cparams(2))(a, wd, x, gate, *(above or ()))


def ffn_down_final(a, wd, x, gate, scale, w_final, tgt, name):
    b, s, f = a.shape
    d = wd.shape[1]
    tm = min(512, s)

    def body(a_ref, wd_ref, x_ref, g_ref, w_ref, t_ref, loss_ref, dx_ref, dw_ref, do_ref, dg_ref):
        @pl.when(_first_step())
        def _():
            loss_ref[...] = jnp.zeros_like(loss_ref)
            dw_ref[...] = jnp.zeros_like(dw_ref)

        @pl.when(pl.program_id(1) == 0)
        def _():
            dg_ref[...] = jnp.zeros_like(dg_ref)
        o = jnp.dot(a_ref[...], wd_ref[...], preferred_element_type=F32)
        sg = scale * g_ref[...]
        xv = x_ref[...] + sg * o
        r = _rms(xv)
        n = xv * r
        wv = w_ref[...]
        e = n * wv - t_ref[...]
        loss_ref[...] += jnp.sum(e * e) * (0.5 / d)
        dy = e * (1.0 / d)
        dw_ref[...] += jnp.sum(dy * n, axis=0, keepdims=True)
        dx = _rms_bwd(dy * wv, n, r)
        dx_ref[...] = dx
        do_ref[...] = (sg * dx).astype(BF16)
        dg_ref[...] += jnp.sum(scale * dx * o, axis=0, keepdims=True)

    return pl.pallas_call(
        body, name=name, grid=(b, s // tm),
        in_specs=[_row(tm, f), _full((f, d)), _row(tm, d), _bvec(d), _full((1, d)), _row(tm, d)],
        out_specs=[_full((1, LANES)), _row(tm, d), _full((1, d)), _row(tm, d), _bvec(d)],
        out_shape=[SDS((1, LANES), F32), SDS((b, s, d), F32), SDS((1, d), F32), SDS((b, s, d), BF16), SDS((b, 1, d), F32)],
        compiler_params=_cparams(2))(a, wd, x, gate, w_final, tgt)


def ffn_dact(do, wd, silu_g, u_dsilu, name):
    b, s, d = do.shape
    f = wd.shape[0]
    tm, tn = min(512, s), f // 2

    def body(do_ref, wd_ref, s_ref, t_ref, dg_ref, du_ref):
        da = lax.dot_general(do_ref[...], wd_ref[...], NT_DIMS, preferred_element_type=F32)
        dg_ref[...] = (da * t_ref[...].astype(F32)).astype(BF16)
        du_ref[...] = (da * s_ref[...].astype(F32)).astype(BF16)

    dos = pl.BlockSpec((None, tm, d), lambda j, bb, i: (bb, i, 0))
    ws = pl.BlockSpec((tn, d), lambda j, bb, i: (j, 0))
    es = pl.BlockSpec((None, tm, tn), lambda j, bb, i: (bb, i, j))
    return pl.pallas_call(
        body, name=name, grid=(f // tn, b, s // tm),
        in_specs=[dos, ws, es, es], out_specs=[es, es],
        out_shape=[SDS((b, s, f), BF16), SDS((b, s, f), BF16)], compiler_params=_cparams(3))(do, wd, silu_g, u_dsilu)


def mm_tn(a, bm, tma, tnb, name):
    b, s, ka = a.shape
    nb = bm.shape[2]
    tk = min(2048, s)
    nk = s // tk

    def body(a_ref, b_ref, o_ref, acc):
        first = (pl.program_id(2) == 0) & (pl.program_id(3) == 0)
        last = (pl.program_id(2) == b - 1) & (pl.program_id(3) == nk - 1)
        part = lax.dot_general(a_ref[...], b_ref[...], TN_DIMS, preferred_element_type=F32)

        @pl.when(first)
        def _():
            acc[...] = part

        @pl.when(jnp.logical_not(first))
        def _():
            acc[...] += part

        @pl.when(last)
        def _():
            o_ref[...] = acc[...].astype(BF16)

    return pl.pallas_call(
        body, name=name, grid=(ka // tma, nb // tnb, b, nk),
        in_specs=[pl.BlockSpec((None, tk, tma), lambda i, j, bb, k: (bb, k, i)),
                  pl.BlockSpec((None, tk, tnb), lambda i, j, bb, k: (bb, k, j))],
        out_specs=pl.BlockSpec((tma, tnb), lambda i, j, bb, k: (i, j)),
        out_shape=SDS((ka, nb), BF16), scratch_shapes=[pltpu.VMEM((tma, tnb), F32)],
        compiler_params=_cparams(4))(a, bm)


def mm_tn_blocks(a_blocks, bm, name):
    b, s, nb = bm.shape
    widths = [a.shape[2] for a in a_blocks]
    starts = [sum(widths[:k]) for k in range(len(widths))]
    tk = min(1024, s)
    nk = s // tk
    n = len(a_blocks)

    def body(*refs):
        a_refs, b_ref, o_ref, acc = refs[:n], refs[n], refs[n + 1], refs[n + 2]
        first = (pl.program_id(0) == 0) & (pl.program_id(1) == 0)
        last = (pl.program_id(0) == b - 1) & (pl.program_id(1) == nk - 1)

        @pl.when(first)
        def _():
            acc[...] = jnp.zeros_like(acc)
        bv = b_ref[...]
        for a_ref, st, wd in zip(a_refs, starts, widths):
            acc[st:st + wd, :] += lax.dot_general(a_ref[...], bv, TN_DIMS, preferred_element_type=F32)

        @pl.when(last)
        def _():
            o_ref[...] = acc[...].astype(BF16)

    return pl.pallas_call(
        body, name=name, grid=(b, nk),
        in_specs=[_row(tk, wd) for wd in widths] + [_row(tk, nb)],
        out_specs=_full((sum(widths), nb)), out_shape=SDS((sum(widths), nb), BF16),
        scratch_shapes=[pltpu.VMEM((sum(widths), nb), F32)], compiler_params=_cparams(2))(*a_blocks, bm)


def _gate_bwd_specs(tm, d, b, s):
    return ([_row(tm, d), _bvec(d)], [_row(tm, d), _bvec(d)], [SDS((b, s, d), BF16), SDS((b, 1, d), F32)])


def _gate_bwd_tile(dx, scale, o_ref, g_ref, do_ref, dg_ref):
    do_ref[...] = ((scale * g_ref[...]) * dx).astype(BF16)
    dg_ref[...] += jnp.sum(scale * dx * o_ref[...].astype(F32), axis=0, keepdims=True)


def n_in_bytes(arrs):
    return sum(a.size * a.dtype.itemsize for a in arrs)


def dh_norm_bwd(dys, wts, x, dxn, w, sc, name, below=None):
    b, s, d = x.shape
    tm = min(512 if n_in_bytes(wts) <= 8 * 1024 * 1024 else 256, s)
    n_in, n_w = len(dys), len(wts)
    extra_in, extra_out, extra_shape = _gate_bwd_specs(tm, d, b, s) if below else ([], [], [])
    starts = [sum(dy.shape[2] for dy in dys[:k]) for k in range(n_in)]

    def body(*refs):
        dy_refs, w_refs = refs[:n_in], refs[n_in:n_in + n_w]
        x_ref, dxn_ref, nw_ref, sc_ref = refs[n_in + n_w:n_in + n_w + 4]
        rest = refs[n_in + n_w + 4:]
        if below:
            o_ref, g_ref, dx_ref, dsc_ref, dsh_ref, dw_ref, do_ref, dg_ref = rest
        else:
            dx_ref, dsc_ref, dsh_ref, dw_ref = rest

        @pl.when(pl.program_id(1) == 0)
        def _():
            dsc_ref[...] = jnp.zeros_like(dsc_ref)
            dsh_ref[...] = jnp.zeros_like(dsh_ref)
            if below:
                dg_ref[...] = jnp.zeros_like(dg_ref)

        @pl.when(_first_step())
        def _():
            dw_ref[...] = jnp.zeros_like(dw_ref)

        def weight(k):
            return w_refs[k][...] if n_w == n_in else w_refs[0][starts[k]:starts[k] + dys[k].shape[2], :]

        dh = jnp.dot(dy_refs[0][...], weight(0), preferred_element_type=F32)
        for k in range(1, n_in):
            dh += jnp.dot(dy_refs[k][...], weight(k), preferred_element_type=F32)
        xv = x_ref[...]
        r = _rms(xv)
        n = xv * r
        nw = nw_ref[...]
        dsc_ref[...] += jnp.sum(dh * (n * nw), axis=0, keepdims=True)
        dsh_ref[...] += jnp.sum(dh, axis=0, keepdims=True)
        dhn = dh * (1.0 + sc_ref[...])
        dw_ref[...] += jnp.sum(dhn * n, axis=0, keepdims=True)
        dx = dxn_ref[...] + _rms_bwd(dhn * nw, n, r)
        dx_ref[...] = dx
        if below:
            _gate_bwd_tile(dx, below[2], o_ref, g_ref, do_ref, dg_ref)

    in_specs = [_row(tm, dy.shape[2]) for dy in dys] + [_full(wt.shape) for wt in wts]
    in_specs += [_row(tm, d), _row(tm, d), _full((1, d)), _bvec(d)] + extra_in
    return pl.pallas_call(
        body, name=name, grid=(b, s // tm), in_specs=in_specs,
        out_specs=[_row(tm, d), _bvec(d), _bvec(d), _full((1, d))] + extra_out,
        out_shape=[SDS((b, s, d), F32), SDS((b, 1, d), F32), SDS((b, 1, d), F32), SDS((1, d), F32)] + extra_shape,
        compiler_params=_cparams(2))(*dys, *wts, x, dxn, w, sc, *(below[:2] if below else ()))


def in_proj(h, win_t, name):
    b, s, d = h.shape
    tm = min(512, s)
    widths = (D_SSD, D_SSD + 2 * SSD_GROUPS * SSD_STATE, Q_LORA, KV_LORA, LANES)

    def body(h_ref, w_ref, *outs):
        p = lax.dot_general(h_ref[...], w_ref[...], NT_DIMS, preferred_element_type=F32)
        off = 0
        for o_ref, wd in zip(outs, widths):
            o_ref[...] = p[:, off:off + wd]
            off += wd

    return pl.pallas_call(
        body, name=name, grid=(b, s // tm),
        in_specs=[_row(tm, d), _full(win_t.shape)],
        out_specs=[_row(tm, wd) for wd in widths],
        out_shape=[SDS((b, s, wd), F32) for wd in widths], compiler_params=_cparams(2))(h, win_t)


def _halo_prev(ts, d):
    return pl.BlockSpec((None, 8, d), lambda b, i: (b, jnp.maximum(i * (ts // 8) - 1, 0), 0))


CONV_ROWS = 32


def _conv_head(head, u_ref, up_ref):
    head[0:8, :] = jnp.where(pl.program_id(1) > 0, up_ref[...], 0.0)
    head[8:8 + CONV_ROWS, :] = u_ref[0:CONV_ROWS, :]


def _conv_windows(u_ref, head, r0):
    if r0 == 0:
        return [head[5 + k:5 + k + CONV_ROWS, :] for k in range(4)]
    return [u_ref[r0 - 3 + k:r0 - 3 + k + CONV_ROWS, :] for k in range(4)]


def _fold8(t):
    acc = t[0:8, :]
    for r in range(8, CONV_ROWS, 8):
        acc += t[r:r + 8, :]
    return acc


def conv_fwd(u, cw, cb, name):
    b, s, dc = u.shape
    ts = min(512, s)
    widths = (D_SSD, SSD_GROUPS * SSD_STATE, SSD_GROUPS * SSD_STATE)

    def body(u_ref, up_ref, w_ref, b_ref, xs_ref, bm_ref, cm_ref, head):
        _conv_head(head, u_ref, up_ref)
        ws = [w_ref[k:k + 1, :] for k in range(4)]
        bias = b_ref[...]
        for r0 in range(0, ts, CONV_ROWS):
            taps = _conv_windows(u_ref, head, r0)
            v = bias + taps[0] * ws[0] + taps[1] * ws[1] + taps[2] * ws[2] + taps[3] * ws[3]
            y = v * _sigmoid(v)
            rs = slice(r0, r0 + CONV_ROWS)
            xs_ref[rs, :] = y[:, 0:D_SSD]
            bm_ref[rs, :] = y[:, D_SSD:D_SSD + 256]
            cm_ref[rs, :] = y[:, D_SSD + 256:D_SSD + 512]

    return pl.pallas_call(
        body, name=name, grid=(b, s // ts),
        in_specs=[_row(ts, dc), _halo_prev(ts, dc), _full((4, dc)), _full((1, dc))],
        out_specs=[_row(ts, wd) for wd in widths],
        out_shape=[SDS((b, s, wd), F32) for wd in widths],
        scratch_shapes=[pltpu.VMEM((8 + CONV_ROWS, dc), F32)], compiler_params=_cparams(2))(u, u, cw, cb)


def conv_bwd_a(dxs, dbm, dcm, u, cw, cb, name):
    b, s, dc = u.shape
    ts = min(512, s)

    def body(dxs_ref, dbm_ref, dcm_ref, u_ref, up_ref, w_ref, b_ref, dv_ref, dwb_ref, head):
        @pl.when(_first_step())
        def _():
            dwb_ref[...] = jnp.zeros_like(dwb_ref)
        _conv_head(head, u_ref, up_ref)
        ws = [w_ref[k:k + 1, :] for k in range(4)]
        bias = b_ref[...]
        for r0 in range(0, ts, CONV_ROWS):
            taps = _conv_windows(u_ref, head, r0)
            v = bias + taps[0] * ws[0] + taps[1] * ws[1] + taps[2] * ws[2] + taps[3] * ws[3]
            sg = _sigmoid(v)
            rs = slice(r0, r0 + CONV_ROWS)
            dy = jnp.concatenate([dxs_ref[rs, :], dbm_ref[rs, :], dcm_ref[rs, :]], axis=1)
            dv = dy * (sg * (1.0 + v * (1.0 - sg)))
            dv_ref[rs, :] = dv
            for k in range(4):
                dwb_ref[8 * k:8 * k + 8, :] += _fold8(dv * taps[k])
            dwb_ref[32:40, :] += _fold8(dv)

    return pl.pallas_call(
        body, name=name, grid=(b, s // ts),
        in_specs=[_row(ts, D_SSD), _row(ts, 256), _row(ts, 256), _row(ts, dc), _halo_prev(ts, dc),
                  _full((4, dc)), _full((1, dc))],
        out_specs=[_row(ts, dc), _full((40, dc))],
        out_shape=[SDS((b, s, dc), F32), SDS((40, dc), F32)],
        scratch_shapes=[pltpu.VMEM((8 + CONV_ROWS, dc), F32)], compiler_params=_cparams(2))(dxs, dbm, dcm, u, u, cw, cb)


def conv_grads_fold(x, name):
    c = x.shape[1]

    def body(x_ref, o_ref):
        o_ref[...] = jnp.zeros_like(o_ref)
        for k in range(5):
            o_ref[k:k + 1, :] = jnp.sum(x_ref[8 * k:8 * k + 8, :], axis=0, keepdims=True)

    return pl.pallas_call(body, name=name, out_shape=SDS((8, c), F32))(x)


def conv_bwd_b(dv, cw, name):
    b, s, dc = dv.shape
    ts = min(512, s)
    nt = s // ts

    def body(dv_ref, dn_ref, w_ref, du_ref, tail):
        tail[0:CONV_ROWS, :] = dv_ref[ts - CONV_ROWS:ts, :]
        tail[CONV_ROWS:CONV_ROWS + 8, :] = jnp.where(pl.program_id(1) < nt - 1, dn_ref[...], 0.0)
        ws = [w_ref[k:k + 1, :] for k in range(4)]
        for r0 in range(0, ts, CONV_ROWS):
            if r0 == ts - CONV_ROWS:
                win = [tail[3 - k:3 - k + CONV_ROWS, :] for k in range(4)]
            else:
                win = [dv_ref[r0 + 3 - k:r0 + 3 - k + CONV_ROWS, :] for k in range(4)]
            acc = win[0] * ws[0] + win[1] * ws[1] + win[2] * ws[2] + win[3] * ws[3]
            du_ref[r0:r0 + CONV_ROWS, :] = acc.astype(BF16)

    nxt = pl.BlockSpec((None, 8, dc), lambda bb, i: (bb, jnp.minimum((i + 1) * (ts // 8), s // 8 - 1), 0))
    return pl.pallas_call(
        body, name=name, grid=(b, nt),
        in_specs=[_row(ts, dc), nxt, _full((4, dc))],
        out_specs=_row(ts, dc), out_shape=SDS((b, s, dc), BF16),
        scratch_shapes=[pltpu.VMEM((CONV_ROWS + 8, dc), F32)], compiler_params=_cparams(2))(dv, dv, cw)


def _ssd_common(misc_ref, dtb_ref, alog_ref, e_ref):
    ln = CHUNK
    lane = lax.broadcasted_iota(I32, (ln, LANES), 1)
    lane1 = lax.broadcasted_iota(I32, (1, LANES), 1)
    pre = misc_ref[...] + dtb_ref[...]
    dt_s = jnp.where(lane < SSD_HEADS, _softplus(pre), 0.0)
    a_neg = jnp.where(lane1 < SSD_HEADS, -jnp.exp(alog_ref[...]), 0.0)
    ri = lax.broadcasted_iota(I32, (ln, ln), 0)
    ci = lax.broadcasted_iota(I32, (ln, ln), 1)
    tril = ci <= ri
    acum = jnp.dot(tril.astype(F32), dt_s * a_neg, preferred_element_type=F32, precision=HI)
    both_e = _dot_01(jnp.concatenate([dt_s, acum], axis=0), e_ref[...], 3)
    dt_e, acum_e = both_e[0:ln], both_e[ln:2 * ln]
    return dict(pre=pre, dt_s=dt_s, a_neg=a_neg, tril=tril, ri=ri, ci=ci, acum=acum, acum_t=acum.T,
                dt_e=dt_e, eac_e=jnp.exp(acum_e), del_e=jnp.exp(acum_e[ln - 1:ln, :] - acum_e))


def _dot_01(x, m01, terms):
    acc, rest = None, x
    for k in range(terms):
        part = rest.astype(BF16)
        if k + 1 < terms:
            rest = rest - part.astype(F32)
        d = jnp.dot(part, m01, preferred_element_type=F32)
        acc = d if acc is None else acc + d
    return acc


def _decay(cm, h):
    seg = cm["acum"][:, h:h + 1] - cm["acum_t"][h:h + 1, :]
    return jnp.exp(jnp.where(cm["tril"], seg, -jnp.inf))


def ssd_fwd(xs, bm, cm_, misc, z, dtb, alog, dskip_e, norm_w, e_mat, name):
    b, s, _ = xs.shape
    ln, nc = CHUNK, s // CHUNK
    gw = D_SSD // SSD_GROUPS
    hpg = SSD_HEADS // SSD_GROUPS

    def body(xs_ref, b_ref, c_ref, misc_ref, z_ref, dtb_ref, alog_ref, dsk_ref, nw_ref, e_ref,
             ys_ref, y_ref, p_ref, st, yd):
        @pl.when(pl.program_id(1) == 0)
        def _():
            st[...] = jnp.zeros_like(st)
        cm = _ssd_common(misc_ref, dtb_ref, alog_ref, e_ref)
        xsv = xs_ref[...]
        xdt = xsv * cm["dt_e"]
        xdt_b = xdt.astype(BF16)
        xd_b = (xdt * cm["del_e"]).astype(BF16)
        gam_e = cm["eac_e"][ln - 1:ln, :]
        p_ref[...] = st[...]
        groups = [slice(gw * g, gw * (g + 1)) for g in range(SSD_GROUPS)]
        heads = [slice(SSD_HEAD_DIM * h, SSD_HEAD_DIM * (h + 1)) for h in range(SSD_HEADS)]
        bgs = [b_ref[:, SSD_STATE * g:SSD_STATE * (g + 1)].astype(BF16) for g in range(SSD_GROUPS)]
        cgs = [c_ref[:, SSD_STATE * g:SSD_STATE * (g + 1)].astype(BF16) for g in range(SSD_GROUPS)]
        cbs = [lax.dot_general(cg, bg, NT_DIMS, preferred_element_type=F32) for cg, bg in zip(cgs, bgs)]
        sts = [st[:, gs] for gs in groups]
        yoff = [jnp.dot(cg, st_g.astype(BF16), preferred_element_type=F32) * cm["eac_e"][:, gs]
                for cg, st_g, gs in zip(cgs, sts, groups)]
        news = [lax.dot_general(bg, xd_b[:, gs], TN_DIMS, preferred_element_type=F32) for bg, gs in zip(bgs, groups)]
        for gs, st_g, new in zip(groups, sts, news):
            st[:, gs] = st_g * gam_e[:, gs] + new
        ms = [(cbs[h // hpg] * _decay(cm, h)).astype(BF16) for h in range(SSD_HEADS)]
        for h, hs in enumerate(heads):
            yd[:, hs] = jnp.dot(ms[h], xdt_b[:, hs], preferred_element_type=F32)
        y = yd[...] + jnp.concatenate(yoff, axis=1) + dsk_ref[...] * xsv
        y_ref[...] = y
        zz = z_ref[...]
        yg = y * (zz * _sigmoid(zz))
        outs = []
        for g in range(SSD_GROUPS):
            ygg = yg[:, gw * g:gw * (g + 1)]
            outs.append(ygg * _rms(ygg) * nw_ref[:, gw * g:gw * (g + 1)])
        ys_ref[...] = jnp.concatenate(outs, axis=1).astype(BF16)

    row = lambda d: pl.BlockSpec((None, ln, d), lambda bb, c: (bb, c, 0))
    return pl.pallas_call(
        body, name=name, grid=(b, nc),
        in_specs=[row(D_SSD), row(256), row(256), row(LANES), row(D_SSD), _full((1, LANES)), _full((1, LANES)),
                  _full((1, D_SSD)), _full((1, D_SSD)), _full((LANES, D_SSD))],
        out_specs=[row(D_SSD), row(D_SSD), pl.BlockSpec((None, None, SSD_STATE, D_SSD), lambda bb, c: (bb, c, 0, 0))],
        out_shape=[SDS((b, s, D_SSD), BF16), SDS((b, s, D_SSD), F32), SDS((b, nc, SSD_STATE, D_SSD), F32)],
        scratch_shapes=[pltpu.VMEM((SSD_STATE, D_SSD), F32), pltpu.VMEM((ln, D_SSD), F32)],
        compiler_params=_cparams(2))(xs, bm, cm_, misc, z, dtb, alog, dskip_e, norm_w, e_mat)


def ssd_bwd(dys, y, z, xs, bm, cm_, misc, prev, dtb, alog, dskip_e, norm_w, e_mat, et_mat, name):
    b, s, _ = xs.shape
    ln, nc = CHUNK, s // CHUNK
    gw = D_SSD // SSD_GROUPS
    hpg = SSD_HEADS // SSD_GROUPS

    def body(dys_ref, y_ref, z_ref, xs_ref, b_ref, c_ref, misc_ref, p_ref, dtb_ref, alog_ref, dsk_ref, nw_ref,
             e_ref, et_ref, dxs_ref, db_ref, dc_ref, dz_ref, ddt_ref, dnw_ref, ddsk_ref, ddtb_ref, dalog_ref,
             dst, dxd, dac_t):
        @pl.when(_first_step())
        def _():
            for r_ in (dnw_ref, ddsk_ref, ddtb_ref, dalog_ref):
                r_[...] = jnp.zeros_like(r_)

        @pl.when(pl.program_id(1) == 0)
        def _():
            dst[...] = jnp.zeros_like(dst)

        cm = _ssd_common(misc_ref, dtb_ref, alog_ref, e_ref)
        et = et_ref[...]
        squeeze = lambda t: _dot_01(t, et, 2)
        lane = lax.broadcasted_iota(I32, (ln, LANES), 1)
        sub = lax.broadcasted_iota(I32, (LANES, ln), 0)
        xsv = xs_ref[...]
        xdt = xsv * cm["dt_e"]
        xdt_b = xdt.astype(BF16)
        xd_b = (xdt * cm["del_e"]).astype(BF16)
        eac_e = cm["eac_e"]
        gam_e = eac_e[ln - 1:ln, :]

        yv, zz, dyo = y_ref[...], z_ref[...], dys_ref[...]
        sz = _sigmoid(zz)
        silu_z = zz * sz
        yg = yv * silu_z
        dyg, dnw = [], []
        for g in range(SSD_GROUPS):
            gs = slice(gw * g, gw * (g + 1))
            ygg = yg[:, gs]
            r = _rms(ygg)
            n = ygg * r
            dnw.append(jnp.sum(dyo[:, gs] * n, axis=0, keepdims=True))
            dyg.append(_rms_bwd(dyo[:, gs] * nw_ref[:, gs], n, r))
        dyg = jnp.concatenate(dyg, axis=1)
        dnw_ref[...] += jnp.concatenate(dnw, axis=1)
        dz_ref[...] = (dyg * yv * (sz * (1.0 + zz * (1.0 - sz)))).astype(BF16)
        dy = dyg * silu_z
        ddsk_ref[...] += jnp.sum(dy * xsv, axis=0, keepdims=True)
        dy_b = dy.astype(BF16)

        dacum = jnp.zeros((ln, LANES), F32)
        dac_t[...] = jnp.zeros_like(dac_t)
        w1, dgam = [], []
        for g in range(SSD_GROUPS):
            gs = slice(gw * g, gw * (g + 1))
            ss = slice(SSD_STATE * g, SSD_STATE * (g + 1))
            bg = b_ref[:, ss].astype(BF16)
            cg = c_ref[:, ss].astype(BF16)
            cb = lax.dot_general(cg, bg, NT_DIMS, preferred_element_type=F32)
            pt = p_ref[:, gs]
            pt_b = pt.astype(BF16)
            dst_g = dst[:, gs]
            dst_b = dst_g.astype(BF16)
            edy = (dy[:, gs] * eac_e[:, gs]).astype(BF16)
            dcg = lax.dot_general(edy, pt_b, NT_DIMS, preferred_element_type=F32)
            dpt = lax.dot_general(cg, edy, TN_DIMS, preferred_element_type=F32)
            yoff = jnp.dot(cg, pt_b, preferred_element_type=F32) * eac_e[:, gs]
            dxd_g = jnp.dot(bg, dst_b, preferred_element_type=F32)
            dbg = lax.dot_general(xd_b[:, gs], dst_b, NT_DIMS, preferred_element_type=F32)
            ddel = dxd_g * xdt[:, gs] * cm["del_e"][:, gs]
            w1.append(dy[:, gs] * yoff - ddel)
            dgam.append(jnp.sum(ddel, axis=0, keepdims=True) + jnp.sum(dst_g * pt, axis=0, keepdims=True) * gam_e[:, gs])
            dxd[:, gs] = dxd_g * cm["del_e"][:, gs]
            dst[:, gs] = dst_g * gam_e[:, gs] + dpt
            dcb = jnp.zeros((ln, ln), F32)
            for j in range(hpg):
                h = hpg * g + j
                hs = slice(SSD_HEAD_DIM * h, SSD_HEAD_DIM * (h + 1))
                lam = _decay(cm, h)
                m = cb * lam
                dm = lax.dot_general(dy_b[:, hs], xdt_b[:, hs], NT_DIMS, preferred_element_type=F32)
                dxd[:, hs] += lax.dot_general(m.astype(BF16), dy_b[:, hs], TN_DIMS, preferred_element_type=F32)
                dcb += dm * lam
                wl = dm * m
                dacum += jnp.where(lane == h, jnp.sum(wl, axis=1, keepdims=True), 0.0)
                dac_t[...] -= jnp.where(sub == h, jnp.sum(wl, axis=0, keepdims=True), 0.0)
            dcb_b = dcb.astype(BF16)
            dc_ref[:, ss] = dcg + jnp.dot(dcb_b, bg, preferred_element_type=F32)
            db_ref[:, ss] = dbg + lax.dot_general(dcb_b, cg, TN_DIMS, preferred_element_type=F32)

        dxdt = dxd[...]
        dxs_ref[...] = dy * dsk_ref[...] + dxdt * cm["dt_e"]
        dacum += squeeze(jnp.concatenate(w1, axis=1)) + dac_t[...].T
        dlast = squeeze(jnp.broadcast_to(jnp.concatenate(dgam, axis=1), (8, D_SSD)))[0:1, :]
        dacum += jnp.where(lax.broadcasted_iota(I32, (ln, LANES), 0) == ln - 1, dlast, 0.0)
        triu = (cm["ci"] >= cm["ri"]).astype(F32)
        da = jnp.dot(triu, dacum, preferred_element_type=F32, precision=HI)
        ddt = da * cm["a_neg"] + squeeze(dxdt * xsv)
        dalog_ref[...] += jnp.sum(da * cm["dt_s"], axis=0, keepdims=True) * cm["a_neg"]
        ddt_raw = jnp.where(lane < SSD_HEADS, ddt * _sigmoid(cm["pre"]), 0.0)
        ddt_ref[...] = ddt_raw
        ddtb_ref[...] += jnp.sum(ddt_raw, axis=0, keepdims=True)

    row = lambda d: pl.BlockSpec((None, ln, d), lambda bb, c: (bb, nc - 1 - c, 0))
    return pl.pallas_call(
        body, name=name, grid=(b, nc),
        in_specs=[row(D_SSD), row(D_SSD), row(D_SSD), row(D_SSD), row(256), row(256), row(LANES),
                  pl.BlockSpec((None, None, SSD_STATE, D_SSD), lambda bb, c: (bb, nc - 1 - c, 0, 0)),
                  _full((1, LANES)), _full((1, LANES)), _full((1, D_SSD)), _full((1, D_SSD)),
                  _full((LANES, D_SSD)), _full((D_SSD, LANES))],
        out_specs=[row(D_SSD), row(256), row(256), row(D_SSD), row(LANES),
                   _full((1, D_SSD)), _full((1, D_SSD)), _full((1, LANES)), _full((1, LANES))],
        out_shape=[SDS((b, s, D_SSD), F32), SDS((b, s, 256), F32), SDS((b, s, 256), F32), SDS((b, s, D_SSD), BF16),
                   SDS((b, s, LANES), F32), SDS((1, D_SSD), F32), SDS((1, D_SSD), F32), SDS((1, LANES), F32),
                   SDS((1, LANES), F32)],
        scratch_shapes=[pltpu.VMEM((SSD_STATE, D_SSD), F32), pltpu.VMEM((ln, D_SSD), F32), pltpu.VMEM((LANES, ln), F32)],
        compiler_params=_cparams(2))(dys, y, z, xs, bm, cm_, misc, prev, dtb, alog, dskip_e, norm_w, e_mat, et_mat)


def _rope(xv, cc, sp, sm):
    n = xv.shape[1]
    return xv * cc + pltpu.roll(xv, 16, 1) * sp + pltpu.roll(xv, n - 16, 1) * sm


def _rope_bwd(dy, cc, sp, sm):
    n = dy.shape[1]
    return dy * cc + pltpu.roll(dy * sp, n - 16, 1) + pltpu.roll(dy * sm, 16, 1)


def _tile8(t):
    return jnp.concatenate([t] * MLA_HEADS, axis=1)


def qkv_fwd(cq, ckv, misc, cc, sp, sm, qnw, kvnw, wuq_t, wukv_t, place, name):
    b, s, _ = cq.shape
    tm = _att_tile(s)
    hd = MLA_HEADS * HEAD_PAD

    def body(cq_ref, ckv_ref, misc_ref, cc_ref, sp_ref, sm_ref, qnw_ref, kvnw_ref, wq_ref, wkv_ref, pl_ref,
             q_ref, k_ref, v_ref, vt_ref, qn_ref, kvn_ref):
        cqv, ckvv = cq_ref[...], ckv_ref[...]
        qn = (cqv * _rms(cqv) * qnw_ref[...]).astype(BF16)
        kvn = (ckvv * _rms(ckvv) * kvnw_ref[...]).astype(BF16)
        qn_ref[...] = qn
        kvn_ref[...] = kvn
        cc1, sp1, sm1 = cc_ref[...], sp_ref[...], sm_ref[...]
        q = lax.dot_general(qn, wq_ref[...], NT_DIMS, preferred_element_type=F32)
        q_ref[...] = _rope(q, _tile8(cc1), _tile8(sp1), _tile8(sm1)).astype(BF16)
        kv = lax.dot_general(kvn, wkv_ref[...], NT_DIMS, preferred_element_type=F32)
        kr = jnp.dot(misc_ref[...], pl_ref[...], preferred_element_type=F32, precision=HI)
        kr = _rope(kr, cc1, sp1, sm1)
        k_ref[...] = (kv[:, 0:hd] + _tile8(kr)).astype(BF16)
        v_ref[...] = kv[:, hd:2 * hd].astype(BF16)
        for h in range(MLA_HEADS):
            vt_ref[h] = kv[:, hd + HEAD_PAD * h:hd + HEAD_PAD * (h + 1)].T.astype(BF16)

    return pl.pallas_call(
        body, name=name, grid=(b, s // tm),
        in_specs=[_row(tm, Q_LORA), _row(tm, KV_LORA), _row(tm, LANES), _row(tm, LANES), _row(tm, LANES), _row(tm, LANES),
                  _full((1, Q_LORA)), _full((1, KV_LORA)), _full(wuq_t.shape), _full(wukv_t.shape), _full((LANES, LANES))],
        out_specs=[_row(tm, hd), _row(tm, hd), _row(tm, hd),
                   pl.BlockSpec((None, MLA_HEADS, None, HEAD_PAD, tm), lambda bb, i: (bb, 0, i, 0, 0)),
                   _row(tm, Q_LORA), _row(tm, KV_LORA)],
        out_shape=[SDS((b, s, hd), BF16)] * 3 + [SDS((b, MLA_HEADS, s // tm, HEAD_PAD, tm), BF16),
                                                 SDS((b, s, Q_LORA), BF16), SDS((b, s, KV_LORA), BF16)],
        compiler_params=_cparams(2))(cq, ckv, misc, cc, sp, sm, qnw, kvnw, wuq_t, wukv_t, place)


def qkv_bwd(dq, dk, dv, ddt, cq, ckv, cc, sp, sm, qnw, kvnw, wuq_t, wukv_t, place_t, name):
    b, s, _ = cq.shape
    tm = min(512, s)
    hd = MLA_HEADS * HEAD_PAD

    def body(dq_ref, dk_ref, dv_ref, ddt_ref, cq_ref, ckv_ref, cc_ref, sp_ref, sm_ref, qnw_ref, kvnw_ref,
             wq_ref, wkv_ref, plt_ref, dcq_ref, dckv_ref, dmisc_ref, dqp_ref, dkv_ref, dqnw_ref, dkvnw_ref):
        @pl.when(_first_step())
        def _():
            dqnw_ref[...] = jnp.zeros_like(dqnw_ref)
            dkvnw_ref[...] = jnp.zeros_like(dkvnw_ref)
        cc1, sp1, sm1 = cc_ref[...], sp_ref[...], sm_ref[...]
        dqp = _rope_bwd(dq_ref[...].astype(F32), _tile8(cc1), _tile8(sp1), _tile8(sm1)).astype(BF16)
        dqp_ref[...] = dqp
        dkv_b = jnp.concatenate([dk_ref[...], dv_ref[...]], axis=1)
        dkf = dk_ref[...].astype(F32)
        dkv_ref[...] = dkv_b
        dkr = dkf[:, 0:HEAD_PAD]
        for h in range(1, MLA_HEADS):
            dkr += dkf[:, HEAD_PAD * h:HEAD_PAD * (h + 1)]
        dkr = _rope_bwd(dkr, cc1, sp1, sm1)
        dmisc_ref[...] = (jnp.dot(dkr, plt_ref[...], preferred_element_type=F32, precision=HI) + ddt_ref[...]).astype(BF16)

        def norm_bwd(dn_w, xv, w_ref, dw_ref, dx_ref):
            r = _rms(xv)
            n = xv * r
            dw_ref[...] += jnp.sum(dn_w * n, axis=0, keepdims=True)
            dx_ref[...] = _rms_bwd(dn_w * w_ref[...], n, r).astype(BF16)

        norm_bwd(jnp.dot(dqp, wq_ref[...], preferred_element_type=F32), cq_ref[...], qnw_ref, dqnw_ref, dcq_ref)
        norm_bwd(jnp.dot(dkv_b, wkv_ref[...], preferred_element_type=F32), ckv_ref[...], kvnw_ref, dkvnw_ref, dckv_ref)

    return pl.pallas_call(
        body, name=name, grid=(b, s // tm),
        in_specs=[_row(tm, hd), _row(tm, hd), _row(tm, hd), _row(tm, LANES), _row(tm, Q_LORA), _row(tm, KV_LORA),
                  _row(tm, LANES), _row(tm, LANES), _row(tm, LANES), _full((1, Q_LORA)), _full((1, KV_LORA)),
                  _full(wuq_t.shape), _full(wukv_t.shape), _full((LANES, LANES))],
        out_specs=[_row(tm, Q_LORA), _row(tm, KV_LORA), _row(tm, LANES), _row(tm, hd), _row(tm, 2 * hd),
                   _full((1, Q_LORA)), _full((1, KV_LORA))],
        out_shape=[SDS((b, s, Q_LORA), BF16), SDS((b, s, KV_LORA), BF16), SDS((b, s, LANES), BF16),
                   SDS((b, s, hd), BF16), SDS((b, s, 2 * hd), BF16), SDS((1, Q_LORA), F32), SDS((1, KV_LORA), F32)],
        compiler_params=_cparams(2))(dq, dk, dv, ddt, cq, ckv, cc, sp, sm, qnw, kvnw, wuq_t, wukv_t, place_t)


ATT_SCALE = 1.0 / math.sqrt(QK_DIM)
LOG2E = math.log2(math.e)
ATT_SCALE_LOG2E = ATT_SCALE * LOG2E


ATT_HEADS_PER_STEP = 4
ATT_HEADS_PER_STEP_BWD = 2


def _att_tile(s):
    return min(512, s)


def flash_fwd(q, k, vt, name):
    b, s, hd = q.shape
    t = _att_tile(s)
    nb = s // t
    th = t // 2

    hps = ATT_HEADS_PER_STEP
    hw = hps * HEAD_PAD

    def body(q_ref, k_ref, vt_ref, o_ref, lse_ref, m_s, l_s, acc):
        i = pl.program_id(2)
        m_s[...] = jnp.full_like(m_s, -jnp.inf)
        l_s[...] = jnp.zeros_like(l_s)
        acc[...] = jnp.zeros_like(acc)

        def update(j, diagonal):
            ks = pl.ds(pl.multiple_of(j * t, t), t)
            chains = [(hh, half) for hh in range(hps) for half in range(2)]
            lanes = lambda hh: slice(HEAD_PAD * hh, HEAD_PAD * (hh + 1))
            cols = lambda half: slice(th * half, th * (half + 1))
            sts = {}
            for hh, half in chains:
                st = lax.dot_general(k_ref[ks, lanes(hh)], q_ref[cols(half), lanes(hh)], NT_DIMS,
                                     preferred_element_type=F32)
                if diagonal:
                    row = lax.broadcasted_iota(I32, (t, th), 0)
                    col = lax.broadcasted_iota(I32, (t, th), 1) + th * half
                    st = jnp.where(row <= col, st, -jnp.inf)
                sts[hh, half] = st
            pts, alphas = {}, {}
            for hh, half in chains:
                st, cs = sts[hh, half], cols(half)
                m_prev = m_s[hh, :, cs]
                m_new = jnp.maximum(m_prev, jnp.max(st, axis=0, keepdims=True))
                alpha = jnp.exp2((m_prev - m_new) * ATT_SCALE_LOG2E)
                pt = jnp.exp2((st - m_new) * ATT_SCALE_LOG2E)
                l_s[hh, :, cs] = alpha * l_s[hh, :, cs] + jnp.sum(pt, axis=0, keepdims=True)
                m_s[hh, :, cs] = m_new
                pts[hh, half], alphas[hh, half] = pt.astype(BF16), alpha
            for hh, half in chains:
                cs = cols(half)
                acc[hh, :, cs] = alphas[hh, half] * acc[hh, :, cs] + jnp.dot(vt_ref[hh, j], pts[hh, half],
                                                                             preferred_element_type=F32)

        def step(j, carry):
            update(j, False)
            return carry

        lax.fori_loop(0, i, step, 0)
        update(i, True)
        for hh in range(hps):
            o_ref[:, HEAD_PAD * hh:HEAD_PAD * (hh + 1)] = (acc[hh] / l_s[hh]).T
            lse_ref[hh] = m_s[hh] * ATT_SCALE + jnp.log(l_s[hh])

    qs = pl.BlockSpec((None, t, hw), lambda bb, h, i: (bb, i, h))
    ks = pl.BlockSpec((None, s, hw), lambda bb, h, i: (bb, 0, h))
    vs = pl.BlockSpec((None, hps, nb, HEAD_PAD, t), lambda bb, h, i: (bb, h, 0, 0, 0))
    ls = pl.BlockSpec((None, hps, None, 1, t), lambda bb, h, i: (bb, h, i, 0, 0))
    return pl.pallas_call(
        body, name=name, grid=(b, MLA_HEADS // hps, nb),
        in_specs=[qs, ks, vs], out_specs=[qs, ls],
        out_shape=[SDS((b, s, hd), F32), SDS((b, MLA_HEADS, nb, 1, t), F32)],
        scratch_shapes=[pltpu.VMEM((hps, 1, t), F32), pltpu.VMEM((hps, 1, t), F32), pltpu.VMEM((hps, HEAD_PAD, t), F32)],
        compiler_params=_cparams(3))(q, k, vt)


def flash_bwd(q, k, v, do, lse, dlt, name):
    b, s, hd = q.shape
    t = _att_tile(s)
    nb = s // t
    th = t // 2
    lse_r = lse
    dlt_r = dlt.reshape(b, MLA_HEADS, nb, 1, t)

    hps = ATT_HEADS_PER_STEP_BWD
    hw = hps * HEAD_PAD

    def body(q_ref, k_ref, v_ref, do_ref, lse_ref, dlt_ref, dq_ref, dk_ref, dv_ref, dq_s, dk_s, dv_s):
        dq_s[...] = jnp.zeros_like(dq_s)
        dk_s[...] = jnp.zeros_like(dk_s)
        dv_s[...] = jnp.zeros_like(dv_s)

        def tile(j, i, diagonal):
            qs = pl.ds(pl.multiple_of(i * t, t), t)
            chains = [(hh, half) for hh in range(hps) for half in range(2)]
            lanes = lambda hh: slice(HEAD_PAD * hh, HEAD_PAD * (hh + 1))
            keys = lambda half: pl.ds(pl.multiple_of(j * t + th * half, th), th)
            sts, dpts = {}, {}
            for hh, half in chains:
                ls_, ks = lanes(hh), keys(half)
                st = lax.dot_general(k_ref[ks, ls_], q_ref[qs, ls_], NT_DIMS, preferred_element_type=F32)
                if diagonal:
                    row = lax.broadcasted_iota(I32, (th, t), 0) + th * half
                    col = lax.broadcasted_iota(I32, (th, t), 1)
                    st = jnp.where(row <= col, st, -jnp.inf)
                sts[hh, half] = st
                dpts[hh, half] = lax.dot_general(v_ref[ks, ls_], do_ref[qs, ls_], NT_DIMS, preferred_element_type=F32)
            pts, dsts = {}, {}
            for hh, half in chains:
                pt = jnp.exp2(sts[hh, half] * ATT_SCALE_LOG2E - lse_ref[hh, i] * LOG2E)
                pts[hh, half] = pt.astype(BF16)
                dsts[hh, half] = (pt * (dpts[hh, half] - dlt_ref[hh, i])).astype(BF16)
            for hh in range(hps):
                ls_ = lanes(hh)
                dq_acc = None
                for half in range(2):
                    ks = keys(half)
                    dv_s[ks, ls_] += jnp.dot(pts[hh, half], do_ref[qs, ls_], preferred_element_type=F32)
                    dk_s[ks, ls_] += jnp.dot(dsts[hh, half], q_ref[qs, ls_], preferred_element_type=F32)
                    part = lax.dot_general(dsts[hh, half], k_ref[ks, ls_], TN_DIMS, preferred_element_type=F32)
                    dq_acc = part if dq_acc is None else dq_acc + part
                dq_s[qs, ls_] += dq_acc

        def key_tile(j, carry):
            tile(j, j, True)

            def query_tile(i, c2):
                tile(j, i, False)
                return c2

            lax.fori_loop(j + 1, nb, query_tile, 0)
            return carry

        lax.fori_loop(0, nb, key_tile, 0)
        dq_ref[...] = (dq_s[...] * ATT_SCALE).astype(BF16)
        dk_ref[...] = (dk_s[...] * ATT_SCALE).astype(BF16)
        dv_ref[...] = dv_s[...].astype(BF16)

    hs = pl.BlockSpec((None, s, hw), lambda bb, h: (bb, 0, h))
    ls = pl.BlockSpec((None, hps, nb, 1, t), lambda bb, h: (bb, h, 0, 0, 0))
    return pl.pallas_call(
        body, name=name, grid=(b, MLA_HEADS // hps),
        in_specs=[hs, hs, hs, hs, ls, ls], out_specs=[hs, hs, hs],
        out_shape=[SDS((b, s, hd), BF16)] * 3, scratch_shapes=[pltpu.VMEM((s, hw), F32)] * 3,
        compiler_params=_cparams(2))(q, k, v, do, lse_r, dlt_r)


def out_proj(ys, attn, mnw, wo, x, gate, above, name):
    b, s, d = x.shape
    tm = min(512, s)

    def body(ys_ref, at_ref, mnw_ref, wo_ref, x_ref, g_ref, nw_ref, sc_ref, sh_ref, xn_ref, o_ref, ym_ref, h_ref):
        av = at_ref[...]
        ym = (av * _rms(av) * mnw_ref[...]).astype(BF16)
        ym_ref[...] = ym
        o = jnp.dot(ys_ref[...], wo_ref[0:D_SSD, :], preferred_element_type=F32)
        o += jnp.dot(ym, wo_ref[D_SSD:2 * D_SSD, :], preferred_element_type=F32)
        xn = x_ref[...] + g_ref[...] * o
        xn_ref[...] = xn
        o_ref[...] = o.astype(BF16)
        h_ref[...] = _norm_mod_tile(xn, nw_ref, sc_ref, sh_ref)

    return pl.pallas_call(
        body, name=name, grid=(b, s // tm),
        in_specs=[_row(tm, D_SSD), _row(tm, D_SSD), _full((1, D_SSD)), _full(wo.shape), _row(tm, d), _bvec(d),
                  _full((1, d)), _bvec(d), _bvec(d)],
        out_specs=[_row(tm, d), _row(tm, d), _row(tm, D_SSD), _row(tm, d)],
        out_shape=[SDS((b, s, d), F32), SDS((b, s, d), BF16), SDS((b, s, D_SSD), BF16), SDS((b, s, d), BF16)],
        compiler_params=_cparams(2))(ys, attn, mnw, wo, x, gate, *above)


def out_proj_bwd(dout, attn, mnw, wo, name):
    b, s, d = dout.shape
    tm = min(512, s)

    def body(do_ref, at_ref, mnw_ref, wo_ref, dys_ref, dat_ref, dlt_ref, dw_ref):
        lane = lax.broadcasted_iota(I32, (tm, LANES), 1)
        @pl.when(_first_step())
        def _():
            dw_ref[...] = jnp.zeros_like(dw_ref)
        dov = do_ref[...]
        dys_ref[...] = lax.dot_general(dov, wo_ref[0:D_SSD, :], NT_DIMS, preferred_element_type=F32)
        dym = lax.dot_general(dov, wo_ref[D_SSD:2 * D_SSD, :], NT_DIMS, preferred_element_type=F32)
        av = at_ref[...]
        r = _rms(av)
        n = av * r
        dw_ref[...] += jnp.sum(dym * n, axis=0, keepdims=True)
        dat = _rms_bwd(dym * mnw_ref[...], n, r)
        dat_ref[...] = dat.astype(BF16)
        prod = dat * av
        cols = jnp.zeros((tm, LANES), F32)
        for h in range(MLA_HEADS):
            cols += jnp.where(lane == h, jnp.sum(prod[:, HEAD_PAD * h:HEAD_PAD * (h + 1)], axis=1, keepdims=True), 0.0)
        dlt_ref[...] = cols.T[0:MLA_HEADS, :]

    return pl.pallas_call(
        body, name=name, grid=(b, s // tm),
        in_specs=[_row(tm, d), _row(tm, D_SSD), _full((1, D_SSD)), _full(wo.shape)],
        out_specs=[_row(tm, D_SSD), _row(tm, D_SSD),
                   pl.BlockSpec((None, MLA_HEADS, tm), lambda bb, i: (bb, 0, i)), _full((1, D_SSD))],
        out_shape=[SDS((b, s, D_SSD), F32), SDS((b, s, D_SSD), BF16), SDS((b, MLA_HEADS, s), F32),
                   SDS((1, D_SSD), F32)],
        compiler_params=_cparams(2))(dout, attn, mnw, wo)


def adaln_fwd(c_all, w_ada, b_ada, name):
    nb, d = c_all.shape
    n = w_ada.shape[1]

    def body(c_ref, w_ref, b_ref, m_ref, ca_ref):
        cv = c_ref[...]
        ca = (cv * _sigmoid(cv)).astype(BF16)
        ca_ref[...] = ca
        m_ref[...] = jnp.dot(ca, w_ref[...].astype(BF16), preferred_element_type=F32) + b_ref[...]

    return pl.pallas_call(
        body, name=name, out_shape=[SDS((nb, n), F32), SDS((nb, d), BF16)],
        compiler_params=pltpu.CompilerParams(vmem_limit_bytes=VMEM_LIMIT))(c_all, w_ada, b_ada)


def adaln_bwd(c_act, dmod_cols, name):
    d, n = c_act.shape[1], dmod_cols.shape[1]

    def body(c_ref, dm_ref, gw_ref):
        gw_ref[...] = lax.dot_general(c_ref[...], dm_ref[...].astype(BF16), TN_DIMS, preferred_element_type=F32)

    return pl.pallas_call(
        body, name=name, out_shape=SDS((d, n), F32),
        compiler_params=pltpu.CompilerParams(vmem_limit_bytes=VMEM_LIMIT))(c_act, dmod_cols)


def sum_rows(x, name):
    def body(x_ref, o_ref):
        o_ref[...] = jnp.sum(x_ref[...], axis=0, keepdims=True)
    return pl.pallas_call(body, name=name, out_shape=SDS((1, x.shape[1]), F32))(x)


def squeeze_heads(x, et_mat, name):
    def body(x_ref, et_ref, o_ref):
        xv = jnp.broadcast_to(x_ref[...], (8, x.shape[1]))
        o_ref[...] = _dot_01(xv, et_ref[...], 3)[0:1, :]
    return pl.pallas_call(body, name=name, out_shape=SDS((1, LANES), F32))(x, et_mat)


def sum_blocks(x, name):
    n, r, c = x.shape
    tr = next(cand for cand in (256, 128, 64, 32, 16, 8) if r % cand == 0)

    def body(x_ref, o_ref):
        acc = x_ref[0].astype(F32)
        for k in range(1, n):
            acc += x_ref[k].astype(F32)
        o_ref[...] = acc

    return pl.pallas_call(
        body, name=name, grid=(r // tr,), in_specs=[pl.BlockSpec((n, tr, c), lambda i: (0, i, 0))],
        out_specs=pl.BlockSpec((tr, c), lambda i: (i, 0)), out_shape=SDS((r, c), F32),
        compiler_params=_cparams(1))(x)


def _adam_math(w, g, m, v):
    m = ADAM_B1 * m + (1.0 - ADAM_B1) * g
    v = ADAM_B2 * v + (1.0 - ADAM_B2) * (g * g)
    m_hat = m / (1.0 - ADAM_B1 ** ADAM_STEP)
    v_hat = v / (1.0 - ADAM_B2 ** ADAM_STEP)
    return -ADAM_LR * (m_hat / (jnp.sqrt(v_hat) + ADAM_EPS) + ADAM_WD * w), m, v


def adamw(w, g, m, v, name):
    r, c = w.shape[-2:]
    tr = r
    for cand in (512, 256, 128, 64, 32, 16, 8):
        if r % cand == 0 and cand * c * 4 <= 2 * 1024 * 1024:
            tr = cand
            break

    def body(w_ref, g_ref, m_ref, v_ref, d_ref, mo_ref, vo_ref):
        d_ref[...], mo_ref[...], vo_ref[...] = _adam_math(w_ref[...], g_ref[...], m_ref[...], v_ref[...])

    def spec(a):
        return pl.BlockSpec((tr, c), lambda i: (i, 0)) if a.ndim == 2 else pl.BlockSpec((None, tr, c), lambda i: (0, i, 0))

    return pl.pallas_call(
        body, name=name, grid=(r // tr,), in_specs=[spec(w), spec(g), spec(m), spec(v)], out_specs=[spec(w)] * 3,
        out_shape=[SDS(w.shape, F32)] * 3, compiler_params=_cparams(1))(w, g, m, v)


def adamw_blocks(w, blocks, m, v, name):
    r, c = w.shape
    tr = next((cand for cand in (128, 64, 32, 16, 8) if r % cand == 0), r)

    def body(w_ref, b_ref, m_ref, v_ref, g_ref, d_ref, mo_ref, vo_ref):
        g = b_ref[0].astype(F32)
        for k in range(1, N_DEV):
            g += b_ref[k].astype(F32)
        g_ref[...] = g
        d_ref[...], mo_ref[...], vo_ref[...] = _adam_math(w_ref[...], g, m_ref[...], v_ref[...])

    spec = pl.BlockSpec((tr, c), lambda i: (i, 0))
    return pl.pallas_call(
        body, name=name, grid=(r // tr,),
        in_specs=[spec, pl.BlockSpec((N_DEV, tr, c), lambda i: (0, i, 0)), spec, spec], out_specs=[spec] * 4,
        out_shape=[SDS((r, c), F32)] * 4, compiler_params=_cparams(1))(w, blocks, m, v)


def adamw_many(ws, gs, ms, vs, name):
    n = len(ws)

    def body(*refs):
        w_r, g_r, m_r, v_r = (refs[k * n:(k + 1) * n] for k in range(4))
        d_r, mo_r, vo_r = (refs[(4 + k) * n:(5 + k) * n] for k in range(3))
        for k in range(n):
            d_r[k][...], mo_r[k][...], vo_r[k][...] = _adam_math(w_r[k][...], g_r[k][...], m_r[k][...], v_r[k][...])

    shapes = [SDS(w.shape, F32) for w in ws]
    outs = pl.pallas_call(body, name=name, out_shape=shapes * 3)(*ws, *gs, *ms, *vs)
    return outs[:n], outs[n:2 * n], outs[2 * n:]


TRANSPOSED = ("ffn1_w_gate", "ffn1_w_up", "ffn2_w_gate", "ffn2_w_up", "w_in", "w_ukv", "w_uq")
GATHER_GROUPS = (("ffn1_w_gate", "ffn1_w_up"), ("ffn1_w_down",),
                 ("w_in", "w_ukv", "w_uq", "w_out", "ffn2_w_gate", "ffn2_w_up", "ffn2_w_down"))
GRAD_GROUPS = (("ffn2", ("ffn2_w_gate", "ffn2_w_up", "ffn2_w_down")), ("mixer", ("w_out", "w_in", "w_ukv", "w_uq")),
               ("ffn1_down", ("ffn1_w_down",)), ("ffn1_gate", ("ffn1_w_gate",)), ("ffn1_up", ("ffn1_w_up",)))


def _shard_view(name, w):
    return w[0].T if name in TRANSPOSED else w[0]


def _shard_unview(name, t):
    return t.T[None] if name in TRANSPOSED else t[None]


def _grad_blocks(name, gw):
    if name == "w_in":
        return _in_proj_rows_inv(gw).reshape(N_DEV, -1, D_MODEL)
    if name == "w_ukv":
        hd = MLA_HEADS * HEAD_PAD
        return jnp.concatenate([gw[:hd].reshape(MLA_HEADS, HEAD_PAD, KV_LORA)[:, :QK_NOPE],
                                gw[hd:].reshape(MLA_HEADS, V_HEAD, KV_LORA)], axis=1)
    if name == "w_uq":
        return gw.reshape(MLA_HEADS, HEAD_PAD, Q_LORA)[:, :QK_DIM]
    return gw.reshape(N_DEV, -1, D_MODEL)


def _pack_rows(arrs):
    parts = []
    for a in arrs:
        flat = a.reshape(-1).astype(F32)
        pad = (-flat.shape[0]) % D_MODEL
        if pad:
            flat = jnp.pad(flat, (0, pad))
        parts.append(flat.reshape(-1, D_MODEL))
    out = jnp.concatenate(parts, axis=0)
    pad = (-out.shape[0]) % 8
    if pad:
        out = jnp.pad(out, ((0, pad), (0, 0)))
    return out


def _unpack_rows(packed, shapes):
    out, row = [], 0
    for shp in shapes:
        n = math.prod(shp)
        nrow = -(-n // D_MODEL)
        out.append(packed[row:row + nrow].reshape(-1)[:n].reshape(shp))
        row += nrow
    return out


def _in_proj_rows(w_t):
    return jnp.concatenate([w_t[0:2560], w_t[2576:2960], w_t[2960:3216], w_t[2560:2576], w_t[3216:3248],
                            jnp.zeros((D_IN_PAD - D_IN, D_MODEL), w_t.dtype)], axis=0)


def _in_proj_rows_inv(d):
    return jnp.concatenate([d[0:2560], d[3200:3216], d[2560:2944], d[2944:3200], d[3216:3248]], axis=0)


def _rope_tables(positions):
    inv_freq = ROPE_THETA ** (-jnp.arange(0, QK_ROPE, 2, dtype=F32) / QK_ROPE)
    ang = positions[..., None].astype(F32) * inv_freq
    cos, sin = jnp.cos(ang), jnp.sin(ang)
    one = jnp.ones(ang.shape[:2] + (QK_NOPE,), F32)
    zero = jnp.zeros_like(one)
    z16, z32, o32 = zero[..., :16], zero[..., :32], one[..., :32]
    cc = jnp.concatenate([one, cos, cos, o32], axis=-1)
    sp = jnp.concatenate([zero, z16, sin, z32], axis=-1)
    sm = jnp.concatenate([zero, -sin, z16, z32], axis=-1)
    return cc, sp, sm


def weight_views(gathered):
    full = lambda name: gathered[name].reshape(-1, gathered[name].shape[2])
    ukv = full("w_ukv").reshape(MLA_HEADS, QK_NOPE + V_HEAD, KV_LORA)
    wukv_t = jnp.concatenate([jnp.pad(ukv[:, :QK_NOPE], ((0, 0), (0, HEAD_PAD - QK_NOPE), (0, 0))).reshape(-1, KV_LORA),
                              ukv[:, QK_NOPE:].reshape(-1, KV_LORA)], axis=0)
    uq = full("w_uq").reshape(MLA_HEADS, QK_DIM, Q_LORA)
    wuq_t = jnp.pad(uq, ((0, 0), (0, HEAD_PAD - QK_DIM), (0, 0))).reshape(-1, Q_LORA)
    return dict(wg1_t=full("ffn1_w_gate"), wu1_t=full("ffn1_w_up"), wd1=full("ffn1_w_down"),
                wg2_t=full("ffn2_w_gate"), wu2_t=full("ffn2_w_up"), wd2=full("ffn2_w_down"),
                wo=full("w_out"), win_t=_in_proj_rows(full("w_in")), wukv_t=wukv_t, wuq_t=wuq_t)


def _ffn_bwd(tag, dxn, do, dgate, x, h, gg, uu, a, sc, norm_w, wg_t, wu_t, wd, below):
    f2 = wd.shape[0] // 2
    dwd = mm_tn(a, do, f2, D_MODEL, tag + "_dwd")
    dgg, duu = ffn_dact(do, wd, gg, uu, tag + "_dact")
    dwg_t = mm_tn(dgg, h, f2, D_MODEL, tag + "_dwg")
    dwu_t = mm_tn(duu, h, f2, D_MODEL, tag + "_dwu")
    dx, dsc, dsh, dnw, *nxt = dh_norm_bwd([dgg, duu], [wg_t, wu_t], x, dxn, norm_w, sc, tag + "_dh", below)
    return dx, (dsh, dsc, dgate), dnw, (dwg_t, dwu_t, dwd), nxt


def local_step(x, tgt, positions, mod, wv, p):
    nb, s, d = x.shape
    sh1, sc1, g1, sh2, sc2, g2, sh3, sc3, g3 = mod
    cc, sp, sm = _rope_tables(positions)
    lane_head = jnp.arange(D_SSD, dtype=I32)[None, :] // SSD_HEAD_DIM
    e_mat = (lane_head == jnp.arange(LANES, dtype=I32)[:, None]).astype(BF16)
    et_mat = e_mat.T
    rr, cl = jnp.arange(LANES, dtype=I32)[:, None], jnp.arange(LANES, dtype=I32)[None, :]
    place = ((cl == rr + (QK_NOPE - SSD_HEADS)) & (rr >= SSD_HEADS) & (rr < SSD_HEADS + QK_ROPE)).astype(F32)
    dtb = jnp.pad(p["dt_bias"], ((0, 0), (0, LANES - SSD_HEADS)))
    alog = jnp.pad(p["a_log"], ((0, 0), (0, LANES - SSD_HEADS)))
    dskip_e = jnp.repeat(p["d_skip"], SSD_HEAD_DIM, axis=1)

    h1 = norm_mod(x, p["norm_ffn1"], sc1, sh1, "ffn1_norm")
    gg1, uu1, a1 = ffn_up(h1, wv["wg1_t"], wv["wu1_t"], "ffn1_up")
    x1, o1, h2 = ffn_down(a1, wv["wd1"], x, g1, 0.5, "ffn1_down", (p["norm_mix"], sc2, sh2))
    z, u, cq, ckv, misc = in_proj(h2, wv["win_t"], "in_proj")
    xs, bm, cm_ = conv_fwd(u, p["conv_w"], p["conv_b"], "conv_fwd")
    ys, y, prev = ssd_fwd(xs, bm, cm_, misc, z, dtb, alog, dskip_e, p["ssd_norm_w"], e_mat, "ssd_fwd")
    q, k, v, vt, qn, kvn = qkv_fwd(cq, ckv, misc, cc, sp, sm, p["q_norm_w"], p["kv_norm_w"], wv["wuq_t"], wv["wukv_t"],
                               place, "qkv_fwd")
    attn, lse = flash_fwd(q, k, vt, "flash_fwd")
    x2, o2, ym, h3 = out_proj(ys, attn, p["mla_norm_w"], wv["wo"], x1, g2, (p["norm_ffn2"], sc3, sh3), "out_proj")
    gg3, uu3, a3 = ffn_up(h3, wv["wg2_t"], wv["wu2_t"], "ffn2_up")
    loss, dx3, dnfin, do3, dg3 = ffn_down_final(a3, wv["wd2"], x2, g3, 0.5, p["norm_final"], tgt, "ffn2_down_loss")

    dx2, dmod3, dnf2, (dwg2, dwu2, dwd2), (dout, dg2) = _ffn_bwd(
        "ffn2", dx3, do3, dg3, x2, h3, gg3, uu3, a3, sc3, p["norm_ffn2"], wv["wg2_t"], wv["wu2_t"], wv["wd2"],
        (o2, g2, 1.0))
    dys, dattn, dlt, dmlan = out_proj_bwd(dout, attn, p["mla_norm_w"], wv["wo"], "out_proj_bwd")
    dwo = jnp.concatenate([mm_tn(ys, dout, D_SSD, D_MODEL, "dwo_ssd"), mm_tn(ym, dout, D_SSD, D_MODEL, "dwo_mla")], axis=0)
    dxs, dbm, dcm, dz, ddt, dssdn, ddsk_lane, ddtb, dalog = ssd_bwd(
        dys, y, z, xs, bm, cm_, misc, prev, dtb, alog, dskip_e, p["ssd_norm_w"], e_mat, et_mat, "ssd_bwd")
    dq, dk, dv = flash_bwd(q, k, v, dattn, lse, dlt, "flash_bwd")
    dcq, dckv, dmisc, dqp, dkvc, dqn, dkvn = qkv_bwd(dq, dk, dv, ddt, cq, ckv, cc, sp, sm, p["q_norm_w"], p["kv_norm_w"],
                                                     wv["wuq_t"], wv["wukv_t"], place.T, "qkv_bwd")
    dwuq = mm_tn(dqp, qn, MLA_HEADS * HEAD_PAD, Q_LORA, "dwuq")
    dwukv = mm_tn(dkvc, kvn, MLA_HEADS * HEAD_PAD, KV_LORA, "dwukv")
    dvv, dconv = conv_bwd_a(dxs, dbm, dcm, u, p["conv_w"], p["conv_b"], "conv_bwd_a")
    dconv = conv_grads_fold(dconv, "conv_grads_fold")
    du = conv_bwd_b(dvv, p["conv_w"], "conv_bwd_b")
    dproj = [dz, du, dcq, dckv, dmisc]
    dwin = mm_tn_blocks(dproj, h2, "dwin")
    dx1, dsc2, dsh2, dnmix, do1, dg1 = dh_norm_bwd(dproj, [wv["win_t"]], x1, dx2, p["norm_mix"], sc2, "mix_dh",
                                                   (o1, g1, 0.5))
    dx0, dmod1, dnf1, (dwg1, dwu1, dwd1), _ = _ffn_bwd(
        "ffn1", dx1, do1, dg1, x, h1, gg1, uu1, a1, sc1, p["norm_ffn1"], wv["wg1_t"], wv["wu1_t"], wv["wd1"], None)

    dmod = jnp.concatenate([*dmod1, dsh2, dsc2, dg2, *dmod3], axis=1).reshape(nb, N_MOD * d)
    return dict(
        loss=loss, dx=dx0, dmod=dmod, norm_ffn1=dnf1, norm_mix=dnmix, norm_ffn2=dnf2, norm_final=dnfin,
        ssd_norm_w=dssdn, mla_norm_w=dmlan, q_norm_w=dqn, kv_norm_w=dkvn,
        dt_bias=ddtb[:, :SSD_HEADS], a_log=dalog[:, :SSD_HEADS],
        d_skip=squeeze_heads(ddsk_lane, et_mat, "d_skip_heads")[:, :SSD_HEADS],
        conv_b=dconv[4:5], conv_w=dconv[0:4],
        gw=dict(ffn1_w_gate=dwg1, ffn1_w_up=dwu1, ffn1_w_down=dwd1, ffn2_w_gate=dwg2, ffn2_w_up=dwu2, ffn2_w_down=dwd2,
                w_out=dwo, w_in=dwin, w_ukv=dwukv, w_uq=dwuq))


def kernel(x, c, positions, w_ada, b_ada, norm_ffn1, ffn1_w_gate, ffn1_w_up, ffn1_w_down, norm_mix, w_in, conv_w, conv_b, dt_bias, a_log, d_skip, ssd_norm_w, q_norm_w, w_uq, kv_norm_w, w_ukv, mla_norm_w, w_out, norm_ffn2, ffn2_w_gate, ffn2_w_up, ffn2_w_down, norm_final, loss_target, m_w_ada, m_b_ada, m_norm_ffn1, m_ffn1_w_gate, m_ffn1_w_up, m_ffn1_w_down, m_norm_mix, m_w_in, m_conv_w, m_conv_b, m_dt_bias, m_a_log, m_d_skip, m_ssd_norm_w, m_q_norm_w, m_w_uq, m_kv_norm_w, m_w_ukv, m_mla_norm_w, m_w_out, m_norm_ffn2, m_ffn2_w_gate, m_ffn2_w_up, m_ffn2_w_down, m_norm_final, v_w_ada, v_b_ada, v_norm_ffn1, v_ffn1_w_gate, v_ffn1_w_up, v_ffn1_w_down, v_norm_mix, v_w_in, v_conv_w, v_conv_b, v_dt_bias, v_a_log, v_d_skip, v_ssd_norm_w, v_q_norm_w, v_w_uq, v_kv_norm_w, v_w_ukv, v_mla_norm_w, v_w_out, v_norm_ffn2, v_ffn2_w_gate, v_ffn2_w_up, v_ffn2_w_down, v_norm_final):
    names = ["w_ada", "b_ada", "norm_ffn1", "ffn1_w_gate", "ffn1_w_up", "ffn1_w_down", "norm_mix", "w_in", "conv_w",
             "conv_b", "dt_bias", "a_log", "d_skip", "ssd_norm_w", "q_norm_w", "w_uq", "kv_norm_w", "w_ukv",
             "mla_norm_w", "w_out", "norm_ffn2", "ffn2_w_gate", "ffn2_w_up", "ffn2_w_down", "norm_final"]
    W = dict(zip(names, (w_ada, b_ada, norm_ffn1, ffn1_w_gate, ffn1_w_up, ffn1_w_down, norm_mix, w_in, conv_w, conv_b, dt_bias, a_log, d_skip, ssd_norm_w, q_norm_w, w_uq, kv_norm_w, w_ukv, mla_norm_w, w_out, norm_ffn2, ffn2_w_gate, ffn2_w_up, ffn2_w_down, norm_final)))
    M = dict(zip(names, (m_w_ada, m_b_ada, m_norm_ffn1, m_ffn1_w_gate, m_ffn1_w_up, m_ffn1_w_down, m_norm_mix, m_w_in, m_conv_w, m_conv_b, m_dt_bias, m_a_log, m_d_skip, m_ssd_norm_w, m_q_norm_w, m_w_uq, m_kv_norm_w, m_w_ukv, m_mla_norm_w, m_w_out, m_norm_ffn2, m_ffn2_w_gate, m_ffn2_w_up, m_ffn2_w_down, m_norm_final)))
    V = dict(zip(names, (v_w_ada, v_b_ada, v_norm_ffn1, v_ffn1_w_gate, v_ffn1_w_up, v_ffn1_w_down, v_norm_mix, v_w_in, v_conv_w, v_conv_b, v_dt_bias, v_a_log, v_d_skip, v_ssd_norm_w, v_q_norm_w, v_w_uq, v_kv_norm_w, v_w_ukv, v_mla_norm_w, v_w_out, v_norm_ffn2, v_ffn2_w_gate, v_ffn2_w_up, v_ffn2_w_down, v_norm_final)))

    nb, s, d = x.shape
    me = 4 * lax.axis_index("x") + 2 * lax.axis_index("y") + lax.axis_index("c")
    n_ada = w_ada.shape[2]

    taps, n_cw = conv_w.shape[1:]
    (cg,) = all_gather8([_pack_rows([c, conv_w[0]])], "gather_c")
    c_all = cg[:, 0:nb].reshape(N_DEV * nb, d)
    conv_w_full = cg[:, nb, 0:taps * n_cw].reshape(N_DEV, taps, n_cw).transpose(1, 0, 2).reshape(taps, N_DEV * n_cw)
    shards = [[_shard_view(name, W[name]).astype(BF16) for name in group] for group in GATHER_GROUPS]
    gathered = dict(zip(GATHER_GROUPS[0], all_gather8(shards[0], "gather_w_ffn1")))

    b_ada_cols = lax.dynamic_slice(b_ada, (0, me * n_ada), (1, n_ada))
    mod_cols, c_act = adaln_fwd(c_all, w_ada[0], b_ada_cols, "adaln_fwd")
    (mod_g,) = all_gather8([mod_cols], "gather_mod")
    gathered, mod_g, shards = lax.optimization_barrier((gathered, mod_g, shards))
    gathered.update(zip(GATHER_GROUPS[1], sc_all_gather8(shards[1], "gather_w_ffn1_down", 1)))
    gathered.update(zip(GATHER_GROUPS[2], sc_all_gather8(shards[2], "gather_w_rest", 7)))
    wv = weight_views(gathered)
    mod = lax.dynamic_slice(mod_g, (0, me * nb, 0), (N_DEV, nb, n_ada)).transpose(1, 0, 2).reshape(nb, N_MOD, 1, d)
    mod = [mod[:, k] for k in range(N_MOD)]

    P = dict(W)
    P["conv_w"] = conv_w_full
    P["norm_final"] = norm_final.reshape(1, d)
    R = local_step(x, loss_target, positions, mod, wv, P)

    dmod = R["dmod"]
    partial_shapes = [(1,), (1, d), (1, d), (1, d), (1, d), (1, d), (1, d), (1, Q_LORA), (1, KV_LORA),
                      (1, SSD_HEADS), (1, SSD_HEADS), (1, SSD_HEADS), (1, D_CONV), (4, D_CONV), (1, N_MOD * d),
                      (nb, N_MOD * d)]
    partial = _pack_rows([R["loss"][0, :1], R["norm_ffn1"], R["norm_mix"], R["norm_ffn2"], R["norm_final"],
                          R["ssd_norm_w"], R["mla_norm_w"], R["q_norm_w"], R["kv_norm_w"],
                          R["dt_bias"], R["a_log"], R["d_skip"], R["conv_b"], R["conv_w"],
                          sum_rows(dmod, "dmod_rows"), dmod])
    (partial_g,) = all_gather8([partial], "gather_partials")
    (loss, g_nf1, g_nmix, g_nf2, g_nfin, g_ssdn, g_mlan, g_qn, g_kvn, g_dtb, g_alog, g_dskip, g_convb, g_convw,
     g_bada, _) = _unpack_rows(sum_blocks(partial_g, "sum_partials"), partial_shapes)
    dmod_row = sum(-(-math.prod(shp) // D_MODEL) for shp in partial_shapes[:-1])
    dmod_all = partial_g[:, dmod_row:dmod_row + nb * N_MOD].reshape(N_DEV * nb, N_MOD * d)
    g_wada = adaln_bwd(c_act, lax.dynamic_slice(dmod_all, (0, me * n_ada), (N_DEV * nb, n_ada)), "adaln_bwd")
    n_cw = conv_w.shape[2]
    G = {"w_ada": g_wada[None], "b_ada": g_bada, "norm_ffn1": g_nf1, "norm_mix": g_nmix, "norm_ffn2": g_nf2,
         "norm_final": g_nfin.reshape(d), "ssd_norm_w": g_ssdn, "mla_norm_w": g_mlan, "q_norm_w": g_qn,
         "kv_norm_w": g_kvn, "dt_bias": g_dtb, "a_log": g_alog, "d_skip": g_dskip, "conv_b": g_convb,
         "conv_w": lax.dynamic_slice(g_convw, (0, me * n_cw), (4, n_cw))[None]}

    DW, NM, NV = {}, {}, {}
    gw = R["gw"]
    for k, (tag, group) in enumerate(GRAD_GROUPS):
        send = [_grad_blocks(name, gw[name]).reshape(N_DEV, *_shard_view(name, W[name]).shape) for name in group]
        recv = sc_all_to_all8(send, "exchange_" + tag, 2 + k)
        for name, blocks in zip(group, recv):
            res = adamw_blocks(_shard_view(name, W[name]), blocks, _shard_view(name, M[name]), _shard_view(name, V[name]),
                               "adamw_" + name)
            G[name], DW[name], NM[name], NV[name] = [_shard_unview(name, t) for t in res]
    DW["w_ada"], NM["w_ada"], NV["w_ada"] = adamw(w_ada, g_wada, m_w_ada, v_w_ada, "adamw_w_ada")
    small = [n for n in names if n not in DW]
    as2d = lambda a: a.reshape(-1, a.shape[-1])
    outs = adamw_many([as2d(W[n]) for n in small], [as2d(G[n]) for n in small], [as2d(M[n]) for n in small],
                      [as2d(V[n]) for n in small], "adamw_small")
    for res, dst in zip(outs, (DW, NM, NV)):
        for n, t in zip(small, res):
            dst[n] = t.reshape(W[n].shape)
    return (loss.reshape(()), R["dx"], *[G[n] for n in names], *[DW[n] for n in names], *[NM[n] for n in names],
            *[NV[n] for n in names])
```

```python
import math

import jax
import jax.numpy as jnp
from jax import lax
from jax.experimental import pallas as pl
from jax.experimental.pallas import tpu as pltpu
from jax.experimental.pallas import tpu_sc as plsc

F32, BF16, I32 = jnp.float32, jnp.bfloat16, jnp.int32
HI = lax.Precision.HIGHEST
SDS = jax.ShapeDtypeStruct
MESH = pl.DeviceIdType.MESH

D_MODEL = 1024
D_FF = 2816
D_SSD = 1024
SSD_HEADS = 16
SSD_HEAD_DIM = 64
SSD_GROUPS = 2
SSD_STATE = 128
CHUNK = 128
MLA_HEADS = 8
QK_NOPE = 64
QK_ROPE = 32
QK_DIM = 96
V_HEAD = 128
Q_LORA = 384
KV_LORA = 256
ROPE_THETA = 10000.0
N_MOD = 9
EPS = 1e-6
D_CONV = 1536
D_IN = 3248
D_IN_PAD = 3328
HEAD_PAD = 128
N_DEV = 8
ADAM_LR, ADAM_B1, ADAM_B2, ADAM_EPS, ADAM_WD, ADAM_STEP = 0.001, 0.9, 0.999, 1e-08, 0.01, 10

SAVED_ACT = BF16
VMEM_LIMIT = 56 * 1024 * 1024
LANES = 128
NT_DIMS = (((1,), (1,)), ((), ()))
TN_DIMS = (((0,), (0,)), ((), ()))


def _cparams(n_axes):
    return pltpu.CompilerParams(dimension_semantics=("arbitrary",) * n_axes, vmem_limit_bytes=VMEM_LIMIT)


def _row(tm, d):
    return pl.BlockSpec((None, tm, d), lambda b, i: (b, i, 0))


def _bvec(d):
    return pl.BlockSpec((None, 1, d), lambda b, i: (b, 0, 0))


def _full(shape):
    n = len(shape)
    return pl.BlockSpec(shape, lambda *_: (0,) * n)


def _sigmoid(x):
    return 1.0 / (1.0 + jnp.exp(-x))


def _softplus(x):
    return jnp.maximum(x, 0.0) + jnp.log(1.0 + jnp.exp(-jnp.abs(x)))


def _rms(x):
    return lax.rsqrt(jnp.mean(x * x, axis=-1, keepdims=True) + EPS)


def _rms_bwd(dn, n, r):
    return r * (dn - n * jnp.mean(dn * n, axis=-1, keepdims=True))


def _first_step():
    return (pl.program_id(0) == 0) & (pl.program_id(1) == 0)


def _gather_copies(x_refs, out_refs, send_sems, recv_sems, local_sems):
    mx, my, mc = lax.axis_index("x"), lax.axis_index("y"), lax.axis_index("c")
    me, sibling = (mx, my, mc), (mx, my, 1 - mc)
    chips = [(1 - mx, my), (mx, 1 - my), (1 - mx, 1 - my)]

    def copy(a, k, block, to, src=None):
        rows = out_refs[a].at[4 * block[0] + 2 * block[1] + block[2]]
        return pltpu.make_async_remote_copy(
            src_ref=rows if src is None else src, dst_ref=rows,
            send_sem=send_sems.at[7 * a + k], recv_sem=recv_sems.at[7 * a + k], device_id=to, device_id_type=MESH)

    arrays = range(len(x_refs))
    mine = [pltpu.make_async_copy(x_refs[a], out_refs[a].at[4 * mx + 2 * my + mc], local_sems.at[a]) for a in arrays]
    first = [[copy(a, 0, me, sibling, src=x_refs[a])] + [copy(a, 1 + j, me, (*chip, mc), src=x_refs[a])
                                                          for j, chip in enumerate(chips)] for a in arrays]
    passed = [[copy(a, 4 + j, (*chip, mc), sibling) for j, chip in enumerate(chips)] for a in arrays]
    for a in arrays:
        mine[a].start()
        for cp in first[a]:
            cp.start()
    for a in arrays:
        for j, chip in enumerate(chips):
            copy(a, 1 + j, (*chip, mc), me).wait_recv()
            passed[a][j].start()
    for a in arrays:
        copy(a, 0, sibling, me).wait_recv()
        for j, chip in enumerate(chips):
            copy(a, 4 + j, (*chip, 1 - mc), me).wait_recv()
    for a in arrays:
        for cp in first[a] + passed[a]:
            cp.wait_send()
        mine[a].wait()


def _gather_peers():
    mx, my, mc = lax.axis_index("x"), lax.axis_index("y"), lax.axis_index("c")
    return [(mx, my, 1 - mc), (1 - mx, my, mc), (mx, 1 - my, mc), (1 - mx, 1 - my, mc)]


def _comm_scratch(n):
    return [pltpu.SemaphoreType.DMA((7 * n,)), pltpu.SemaphoreType.DMA((7 * n,)), pltpu.SemaphoreType.DMA((n,))]


def all_gather8(xs, name):
    n = len(xs)

    def body(*refs):
        _gather_copies(refs[:n], refs[n:2 * n], *refs[2 * n:])

    return pl.pallas_call(
        body, name=name,
        out_shape=[SDS((N_DEV, *x.shape), x.dtype) for x in xs],
        in_specs=[pl.BlockSpec(memory_space=pl.ANY)] * n,
        out_specs=[pl.BlockSpec(memory_space=pl.ANY)] * n,
        scratch_shapes=_comm_scratch(n),
    )(*xs)


def _exchange_peers():
    mx, my, mc = lax.axis_index("x"), lax.axis_index("y"), lax.axis_index("c")
    return [(1 - mx if rel & 4 else mx, 1 - my if rel & 2 else my, 1 - mc if rel & 1 else mc) for rel in range(1, N_DEV)]


def _exchange_copies(x_refs, out_refs, send_sems, recv_sems, local_sems):
    mx, my, mc = lax.axis_index("x"), lax.axis_index("y"), lax.axis_index("c")
    me = 4 * mx + 2 * my + mc
    copies = []
    for a, (x_ref, out_ref) in enumerate(zip(x_refs, out_refs)):
        mine = pltpu.make_async_copy(x_ref.at[me], out_ref.at[me], local_sems.at[a])
        mine.start()
        copies.append(mine)
        for k, (px, py, pc) in enumerate(_exchange_peers()):
            cp = pltpu.make_async_remote_copy(
                src_ref=x_ref.at[4 * px + 2 * py + pc], dst_ref=out_ref.at[me],
                send_sem=send_sems.at[7 * a + k], recv_sem=recv_sems.at[7 * a + k],
                device_id=(px, py, pc), device_id_type=MESH)
            cp.start()
            copies.append(cp)
    for cp in copies:
        cp.wait()


def all_to_all8(xs, name):
    n = len(xs)

    def body(*refs):
        _exchange_copies(refs[:n], refs[n:2 * n], *refs[2 * n:])

    return pl.pallas_call(
        body, name=name,
        out_shape=[SDS(x.shape, x.dtype) for x in xs],
        in_specs=[pl.BlockSpec(memory_space=pl.ANY)] * n,
        out_specs=[pl.BlockSpec(memory_space=pl.ANY)] * n,
        scratch_shapes=_comm_scratch(n),
    )(*xs)


def _sequencer_kernel(name, collective_id, n_arrays):
    return pl.kernel(
        mesh=plsc.ScalarSubcoreMesh(axis_name="seq", num_cores=1), name=name,
        scratch_types=tuple(_comm_scratch(n_arrays)),
        compiler_params=pltpu.CompilerParams(collective_id=collective_id))


def _handshake(peers):
    barrier = pltpu.get_barrier_semaphore()
    for peer in peers:
        pl.semaphore_signal(barrier, inc=1, device_id=peer, device_id_type=MESH)
    pl.semaphore_wait(barrier, len(peers))


def _hbm_refs(xs, out_shapes):
    x_refs = [jax.new_ref(x, memory_space=pltpu.MemorySpace.HBM) for x in xs]
    out_refs = [jax.empty_ref(SDS(shp, x.dtype), memory_space=pltpu.MemorySpace.HBM) for x, shp in zip(xs, out_shapes)]
    return x_refs, out_refs


def sc_all_gather8(xs, name, collective_id):
    x_refs, out_refs = _hbm_refs(xs, [(N_DEV, *x.shape) for x in xs])

    @_sequencer_kernel(name, collective_id, len(xs))
    def launch(send_sems, recv_sems, local_sems):
        _handshake(_gather_peers())
        _gather_copies(x_refs, out_refs, send_sems, recv_sems, local_sems)

    launch()
    return [ref[...] for ref in out_refs]


def sc_all_to_all8(xs, name, collective_id):
    x_refs, out_refs = _hbm_refs(xs, [x.shape for x in xs])

    @_sequencer_kernel(name, collective_id, len(xs))
    def launch(send_sems, recv_sems, local_sems):
        _handshake(_exchange_peers())
        _exchange_copies(x_refs, out_refs, send_sems, recv_sems, local_sems)

    launch()
    return [ref[...] for ref in out_refs]


def norm_mod(x, w, sc, sh, name):
    b, s, d = x.shape
    tm = min(512, s)

    def body(x_ref, w_ref, sc_ref, sh_ref, h_ref):
        xv = x_ref[...]
        n = xv * _rms(xv)
        h_ref[...] = ((n * w_ref[...]) * (1.0 + sc_ref[...]) + sh_ref[...]).astype(BF16)

    return pl.pallas_call(
        body, name=name, grid=(b, s // tm),
        in_specs=[_row(tm, d), _full((1, d)), _bvec(d), _bvec(d)],
        out_specs=_row(tm, d), out_shape=SDS((b, s, d), BF16), compiler_params=_cparams(2))(x, w, sc, sh)


def ffn_up(h, wg_t, wu_t, name):
    b, s, d = h.shape
    f = wg_t.shape[0]
    tm, tn = min(512, s), f // 2

    def body(h_ref, wg_ref, wu_ref, s_ref, t_ref, a_ref):
        hv = h_ref[...]
        g = lax.dot_general(hv, wg_ref[...], NT_DIMS, preferred_element_type=F32)
        u = lax.dot_general(hv, wu_ref[...], NT_DIMS, preferred_element_type=F32)
        sg = _sigmoid(g)
        silu = g * sg
        s_ref[...] = silu.astype(s_ref.dtype)
        t_ref[...] = (u * (sg + silu * (1.0 - sg))).astype(t_ref.dtype)
        a_ref[...] = (silu * u).astype(BF16)

    hs = pl.BlockSpec((None, tm, d), lambda j, bb, i: (bb, i, 0))
    ws = pl.BlockSpec((tn, d), lambda j, bb, i: (j, 0))
    os_ = pl.BlockSpec((None, tm, tn), lambda j, bb, i: (bb, i, j))
    return pl.pallas_call(
        body, name=name, grid=(f // tn, b, s // tm),
        in_specs=[hs, ws, ws], out_specs=[os_, os_, os_],
        out_shape=[SDS((b, s, f), SAVED_ACT), SDS((b, s, f), SAVED_ACT), SDS((b, s, f), BF16)],
        compiler_params=_cparams(3))(h, wg_t, wu_t)


def _norm_mod_tile(xv, w_ref, sc_ref, sh_ref):
    return ((xv * _rms(xv) * w_ref[...]) * (1.0 + sc_ref[...]) + sh_ref[...]).astype(BF16)


def ffn_down(a, wd, x, gate, scale, name, above=None):
    b, s, f = a.shape
    d = wd.shape[1]
    tm = min(512, s)

    def body(a_ref, wd_ref, x_ref, g_ref, *rest):
        xn_ref, o_ref = rest[-3:-1] if above else rest
        o = jnp.dot(a_ref[...], wd_ref[...], preferred_element_type=F32)
        xn = x_ref[...] + (scale * g_ref[...]) * o
        xn_ref[...] = xn
        o_ref[...] = o.astype(BF16)
        if above:
            rest[-1][...] = _norm_mod_tile(xn, *rest[0:3])

    extra = above is not None
    return pl.pallas_call(
        body, name=name, grid=(b, s // tm),
        in_specs=[_row(tm, f), _full((f, d)), _row(tm, d), _bvec(d)] + ([_full((1, d)), _bvec(d), _bvec(d)] if extra else []),
        out_specs=[_row(tm, d), _row(tm, d)] + ([_row(tm, d)] if extra else []),
        out_shape=[SDS((b, s, d), F32), SDS((b, s, d), BF16)] + ([SDS((b, s, d), BF16)] if extra else []),
        compiler_params=_cparams(2))(a, wd, x, gate, *(above or ()))


def ffn_down_final(a, wd, x, gate, scale, w_final, tgt, name):
    b, s, f = a.shape
    d = wd.shape[1]
    tm = min(512, s)

    def body(a_ref, wd_ref, x_ref, g_ref, w_ref, t_ref, loss_ref, dx_ref, dw_ref, do_ref, dg_ref):
        @pl.when(_first_step())
        def _():
            loss_ref[...] = jnp.zeros_like(loss_ref)
            dw_ref[...] = jnp.zeros_like(dw_ref)

        @pl.when(pl.program_id(1) == 0)
        def _():
            dg_ref[...] = jnp.zeros_like(dg_ref)
        o = jnp.dot(a_ref[...], wd_ref[...], preferred_element_type=F32)
        sg = scale * g_ref[...]
        xv = x_ref[...] + sg * o
        r = _rms(xv)
        n = xv * r
        wv = w_ref[...]
        e = n * wv - t_ref[...]
        loss_ref[...] += jnp.sum(e * e) * (0.5 / d)
        dy = e * (1.0 / d)
        dw_ref[...] += jnp.sum(dy * n, axis=0, keepdims=True)
        dx = _rms_bwd(dy * wv, n, r)
        dx_ref[...] = dx
        do_ref[...] = (sg * dx).astype(BF16)
        dg_ref[...] += jnp.sum(scale * dx * o, axis=0, keepdims=True)

    return pl.pallas_call(
        body, name=name, grid=(b, s // tm),
        in_specs=[_row(tm, f), _full((f, d)), _row(tm, d), _bvec(d), _full((1, d)), _row(tm, d)],
        out_specs=[_full((1, LANES)), _row(tm, d), _full((1, d)), _row(tm, d), _bvec(d)],
        out_shape=[SDS((1, LANES), F32), SDS((b, s, d), F32), SDS((1, d), F32), SDS((b, s, d), BF16), SDS((b, 1, d), F32)],
        compiler_params=_cparams(2))(a, wd, x, gate, w_final, tgt)


def ffn_dact(do, wd, silu_g, u_dsilu, name):
    b, s, d = do.shape
    f = wd.shape[0]
    tm, tn = min(512, s), f // 2

    def body(do_ref, wd_ref, s_ref, t_ref, dg_ref, du_ref):
        da = lax.dot_general(do_ref[...], wd_ref[...], NT_DIMS, preferred_element_type=F32)
        dg_ref[...] = (da * t_ref[...].astype(F32)).astype(BF16)
        du_ref[...] = (da * s_ref[...].astype(F32)).astype(BF16)

    dos = pl.BlockSpec((None, tm, d), lambda j, bb, i: (bb, i, 0))
    ws = pl.BlockSpec((tn, d), lambda j, bb, i: (j, 0))
    es = pl.BlockSpec((None, tm, tn), lambda j, bb, i: (bb, i, j))
    return pl.pallas_call(
        body, name=name, grid=(f // tn, b, s // tm),
        in_specs=[dos, ws, es, es], out_specs=[es, es],
        out_shape=[SDS((b, s, f), BF16), SDS((b, s, f), BF16)], compiler_params=_cparams(3))(do, wd, silu_g, u_dsilu)


def mm_tn(a, bm, tma, tnb, name):
    b, s, ka = a.shape
    nb = bm.shape[2]
    tk = min(2048, s)
    nk = s // tk

    def body(a_ref, b_ref, o_ref, acc):
        first = (pl.program_id(2) == 0) & (pl.program_id(3) == 0)
        last = (pl.program_id(2) == b - 1) & (pl.program_id(3) == nk - 1)
        part = lax.dot_general(a_ref[...], b_ref[...], TN_DIMS, preferred_element_type=F32)

        @pl.when(first)
        def _():
            acc[...] = part

        @pl.when(jnp.logical_not(first))
        def _():
            acc[...] += part

        @pl.when(last)
        def _():
            o_ref[...] = acc[...].astype(BF16)

    return pl.pallas_call(
        body, name=name, grid=(ka // tma, nb // tnb, b, nk),
        in_specs=[pl.BlockSpec((None, tk, tma), lambda i, j, bb, k: (bb, k, i)),
                  pl.BlockSpec((None, tk, tnb), lambda i, j, bb, k: (bb, k, j))],
        out_specs=pl.BlockSpec((tma, tnb), lambda i, j, bb, k: (i, j)),
        out_shape=SDS((ka, nb), BF16), scratch_shapes=[pltpu.VMEM((tma, tnb), F32)],
        compiler_params=_cparams(4))(a, bm)


def mm_tn_blocks(a_blocks, bm, name):
    b, s, nb = bm.shape
    widths = [a.shape[2] for a in a_blocks]
    starts = [sum(widths[:k]) for k in range(len(widths))]
    tk = min(1024, s)
    nk = s // tk
    n = len(a_blocks)

    def body(*refs):
        a_refs, b_ref, o_ref, acc = refs[:n], refs[n], refs[n + 1], refs[n + 2]
        first = (pl.program_id(0) == 0) & (pl.program_id(1) == 0)
        last = (pl.program_id(0) == b - 1) & (pl.program_id(1) == nk - 1)

        @pl.when(first)
        def _():
            acc[...] = jnp.zeros_like(acc)
        bv = b_ref[...]
        for a_ref, st, wd in zip(a_refs, starts, widths):
            acc[st:st + wd, :] += lax.dot_general(a_ref[...], bv, TN_DIMS, preferred_element_type=F32)

        @pl.when(last)
        def _():
            o_ref[...] = acc[...].astype(BF16)

    return pl.pallas_call(
        body, name=name, grid=(b, nk),
        in_specs=[_row(tk, wd) for wd in widths] + [_row(tk, nb)],
        out_specs=_full((sum(widths), nb)), out_shape=SDS((sum(widths), nb), BF16),
        scratch_shapes=[pltpu.VMEM((sum(widths), nb), F32)], compiler_params=_cparams(2))(*a_blocks, bm)


def _gate_bwd_specs(tm, d, b, s):
    return ([_row(tm, d), _bvec(d)], [_row(tm, d), _bvec(d)], [SDS((b, s, d), BF16), SDS((b, 1, d), F32)])


def _gate_bwd_tile(dx, scale, o_ref, g_ref, do_ref, dg_ref):
    do_ref[...] = ((scale * g_ref[...]) * dx).astype(BF16)
    dg_ref[...] += jnp.sum(scale * dx * o_ref[...].astype(F32), axis=0, keepdims=True)


def n_in_bytes(arrs):
    return sum(a.size * a.dtype.itemsize for a in arrs)


def dh_norm_bwd(dys, wts, x, dxn, w, sc, name, below=None):
    b, s, d = x.shape
    tm = min(512 if n_in_bytes(wts) <= 8 * 1024 * 1024 else 256, s)
    n_in, n_w = len(dys), len(wts)
    extra_in, extra_out, extra_shape = _gate_bwd_specs(tm, d, b, s) if below else ([], [], [])
    starts = [sum(dy.shape[2] for dy in dys[:k]) for k in range(n_in)]

    def body(*refs):
        dy_refs, w_refs = refs[:n_in], refs[n_in:n_in + n_w]
        x_ref, dxn_ref, nw_ref, sc_ref = refs[n_in + n_w:n_in + n_w + 4]
        rest = refs[n_in + n_w + 4:]
        if below:
            o_ref, g_ref, dx_ref, dsc_ref, dsh_ref, dw_ref, do_ref, dg_ref = rest
        else:
            dx_ref, dsc_ref, dsh_ref, dw_ref = rest

        @pl.when(pl.program_id(1) == 0)
        def _():
            dsc_ref[...] = jnp.zeros_like(dsc_ref)
            dsh_ref[...] = jnp.zeros_like(dsh_ref)
            if below:
                dg_ref[...] = jnp.zeros_like(dg_ref)

        @pl.when(_first_step())
        def _():
            dw_ref[...] = jnp.zeros_like(dw_ref)

        def weight(k):
            return w_refs[k][...] if n_w == n_in else w_refs[0][starts[k]:starts[k] + dys[k].shape[2], :]

        dh = jnp.dot(dy_refs[0][...], weight(0), preferred_element_type=F32)
        for k in range(1, n_in):
            dh += jnp.dot(dy_refs[k][...], weight(k), preferred_element_type=F32)
        xv = x_ref[...]
        r = _rms(xv)
        n = xv * r
        nw = nw_ref[...]
        dsc_ref[...] += jnp.sum(dh * (n * nw), axis=0, keepdims=True)
        dsh_ref[...] += jnp.sum(dh, axis=0, keepdims=True)
        dhn = dh * (1.0 + sc_ref[...])
        dw_ref[...] += jnp.sum(dhn * n, axis=0, keepdims=True)
        dx = dxn_ref[...] + _rms_bwd(dhn * nw, n, r)
        dx_ref[...] = dx
        if below:
            _gate_bwd_tile(dx, below[2], o_ref, g_ref, do_ref, dg_ref)

    in_specs = [_row(tm, dy.shape[2]) for dy in dys] + [_full(wt.shape) for wt in wts]
    in_specs += [_row(tm, d), _row(tm, d), _full((1, d)), _bvec(d)] + extra_in
    return pl.pallas_call(
        body, name=name, grid=(b, s // tm), in_specs=in_specs,
        out_specs=[_row(tm, d), _bvec(d), _bvec(d), _full((1, d))] + extra_out,
        out_shape=[SDS((b, s, d), F32), SDS((b, 1, d), F32), SDS((b, 1, d), F32), SDS((1, d), F32)] + extra_shape,
        compiler_params=_cparams(2))(*dys, *wts, x, dxn, w, sc, *(below[:2] if below else ()))


def in_proj(h, win_t, name):
    b, s, d = h.shape
    tm = min(512, s)
    widths = (D_SSD, D_SSD + 2 * SSD_GROUPS * SSD_STATE, Q_LORA, KV_LORA, LANES)

    def body(h_ref, w_ref, *outs):
        p = lax.dot_general(h_ref[...], w_ref[...], NT_DIMS, preferred_element_type=F32)
        off = 0
        for o_ref, wd in zip(outs, widths):
            o_ref[...] = p[:, off:off + wd]
            off += wd

    return pl.pallas_call(
        body, name=name, grid=(b, s // tm),
        in_specs=[_row(tm, d), _full(win_t.shape)],
        out_specs=[_row(tm, wd) for wd in widths],
        out_shape=[SDS((b, s, wd), F32) for wd in widths], compiler_params=_cparams(2))(h, win_t)


def _halo_prev(ts, d):
    return pl.BlockSpec((None, 8, d), lambda b, i: (b, jnp.maximum(i * (ts // 8) - 1, 0), 0))


CONV_ROWS = 32


def _conv_head(head, u_ref, up_ref):
    head[0:8, :] = jnp.where(pl.program_id(1) > 0, up_ref[...], 0.0)
    head[8:8 + CONV_ROWS, :] = u_ref[0:CONV_ROWS, :]


def _conv_windows(u_ref, head, r0):
    if r0 == 0:
        return [head[5 + k:5 + k + CONV_ROWS, :] for k in range(4)]
    return [u_ref[r0 - 3 + k:r0 - 3 + k + CONV_ROWS, :] for k in range(4)]


def _fold8(t):
    acc = t[0:8, :]
    for r in range(8, CONV_ROWS, 8):
        acc += t[r:r + 8, :]
    return acc


def conv_fwd(u, cw, cb, name):
    b, s, dc = u.shape
    ts = min(512, s)
    widths = (D_SSD, SSD_GROUPS * SSD_STATE, SSD_GROUPS * SSD_STATE)

    def body(u_ref, up_ref, w_ref, b_ref, xs_ref, bm_ref, cm_ref, head):
        _conv_head(head, u_ref, up_ref)
        ws = [w_ref[k:k + 1, :] for k in range(4)]
        bias = b_ref[...]
        for r0 in range(0, ts, CONV_ROWS):
            taps = _conv_windows(u_ref, head, r0)
            v = bias + taps[0] * ws[0] + taps[1] * ws[1] + taps[2] * ws[2] + taps[3] * ws[3]
            y = v * _sigmoid(v)
            rs = slice(r0, r0 + CONV_ROWS)
            xs_ref[rs, :] = y[:, 0:D_SSD]
            bm_ref[rs, :] = y[:, D_SSD:D_SSD + 256]
            cm_ref[rs, :] = y[:, D_SSD + 256:D_SSD + 512]

    return pl.pallas_call(
        body, name=name, grid=(b, s // ts),
        in_specs=[_row(ts, dc), _halo_prev(ts, dc), _full((4, dc)), _full((1, dc))],
        out_specs=[_row(ts, wd) for wd in widths],
        out_shape=[SDS((b, s, wd), F32) for wd in widths],
        scratch_shapes=[pltpu.VMEM((8 + CONV_ROWS, dc), F32)], compiler_params=_cparams(2))(u, u, cw, cb)


def conv_bwd_a(dxs, dbm, dcm, u, cw, cb, name):
    b, s, dc = u.shape
    ts = min(512, s)

    def body(dxs_ref, dbm_ref, dcm_ref, u_ref, up_ref, w_ref, b_ref, dv_ref, dwb_ref, head):
        @pl.when(_first_step())
        def _():
            dwb_ref[...] = jnp.zeros_like(dwb_ref)
        _conv_head(head, u_ref, up_ref)
        ws = [w_ref[k:k + 1, :] for k in range(4)]
        bias = b_ref[...]
        for r0 in range(0, ts, CONV_ROWS):
            taps = _conv_windows(u_ref, head, r0)
            v = bias + taps[0] * ws[0] + taps[1] * ws[1] + taps[2] * ws[2] + taps[3] * ws[3]
            sg = _sigmoid(v)
            rs = slice(r0, r0 + CONV_ROWS)
            dy = jnp.concatenate([dxs_ref[rs, :], dbm_ref[rs, :], dcm_ref[rs, :]], axis=1)
            dv = dy * (sg * (1.0 + v * (1.0 - sg)))
            dv_ref[rs, :] = dv
            for k in range(4):
                dwb_ref[8 * k:8 * k + 8, :] += _fold8(dv * taps[k])
            dwb_ref[32:40, :] += _fold8(dv)

    return pl.pallas_call(
        body, name=name, grid=(b, s // ts),
        in_specs=[_row(ts, D_SSD), _row(ts, 256), _row(ts, 256), _row(ts, dc), _halo_prev(ts, dc),
                  _full((4, dc)), _full((1, dc))],
        out_specs=[_row(ts, dc), _full((40, dc))],
        out_shape=[SDS((b, s, dc), F32), SDS((40, dc), F32)],
        scratch_shapes=[pltpu.VMEM((8 + CONV_ROWS, dc), F32)], compiler_params=_cparams(2))(dxs, dbm, dcm, u, u, cw, cb)


def conv_grads_fold(x, name):
    c = x.shape[1]

    def body(x_ref, o_ref):
        o_ref[...] = jnp.zeros_like(o_ref)
        for k in range(5):
            o_ref[k:k + 1, :] = jnp.sum(x_ref[8 * k:8 * k + 8, :], axis=0, keepdims=True)

    return pl.pallas_call(body, name=name, out_shape=SDS((8, c), F32))(x)


def conv_bwd_b(dv, cw, name):
    b, s, dc = dv.shape
    ts = min(512, s)
    nt = s // ts

    def body(dv_ref, dn_ref, w_ref, du_ref, tail):
        tail[0:CONV_ROWS, :] = dv_ref[ts - CONV_ROWS:ts, :]
        tail[CONV_ROWS:CONV_ROWS + 8, :] = jnp.where(pl.program_id(1) < nt - 1, dn_ref[...], 0.0)
        ws = [w_ref[k:k + 1, :] for k in range(4)]
        for r0 in range(0, ts, CONV_ROWS):
            if r0 == ts - CONV_ROWS:
                win = [tail[3 - k:3 - k + CONV_ROWS, :] for k in range(4)]
            else:
                win = [dv_ref[r0 + 3 - k:r0 + 3 - k + CONV_ROWS, :] for k in range(4)]
            acc = win[0] * ws[0] + win[1] * ws[1] + win[2] * ws[2] + win[3] * ws[3]
            du_ref[r0:r0 + CONV_ROWS, :] = acc.astype(BF16)

    nxt = pl.BlockSpec((None, 8, dc), lambda bb, i: (bb, jnp.minimum((i + 1) * (ts // 8), s // 8 - 1), 0))
    return pl.pallas_call(
        body, name=name, grid=(b, nt),
        in_specs=[_row(ts, dc), nxt, _full((4, dc))],
        out_specs=_row(ts, dc), out_shape=SDS((b, s, dc), BF16),
        scratch_shapes=[pltpu.VMEM((CONV_ROWS + 8, dc), F32)], compiler_params=_cparams(2))(dv, dv, cw)


def _ssd_common(misc_ref, dtb_ref, alog_ref, e_ref):
    ln = CHUNK
    lane = lax.broadcasted_iota(I32, (ln, LANES), 1)
    lane1 = lax.broadcasted_iota(I32, (1, LANES), 1)
    pre = misc_ref[...] + dtb_ref[...]
    dt_s = jnp.where(lane < SSD_HEADS, _softplus(pre), 0.0)
    a_neg = jnp.where(lane1 < SSD_HEADS, -jnp.exp(alog_ref[...]), 0.0)
    ri = lax.broadcasted_iota(I32, (ln, ln), 0)
    ci = lax.broadcasted_iota(I32, (ln, ln), 1)
    tril = ci <= ri
    acum = jnp.dot(tril.astype(F32), dt_s * a_neg, preferred_element_type=F32, precision=HI)
    both_e = _dot_01(jnp.concatenate([dt_s, acum], axis=0), e_ref[...], 3)
    dt_e, acum_e = both_e[0:ln], both_e[ln:2 * ln]
    return dict(pre=pre, dt_s=dt_s, a_neg=a_neg, tril=tril, ri=ri, ci=ci, acum=acum, acum_t=acum.T,
                dt_e=dt_e, eac_e=jnp.exp(acum_e), del_e=jnp.exp(acum_e[ln - 1:ln, :] - acum_e))


def _dot_01(x, m01, terms):
    acc, rest = None, x
    for k in range(terms):
        part = rest.astype(BF16)
        if k + 1 < terms:
            rest = rest - part.astype(F32)
        d = jnp.dot(part, m01, preferred_element_type=F32)
        acc = d if acc is None else acc + d
    return acc


def _decay(cm, h):
    seg = cm["acum"][:, h:h + 1] - cm["acum_t"][h:h + 1, :]
    return jnp.exp(jnp.where(cm["tril"], seg, -jnp.inf))


def ssd_fwd(xs, bm, cm_, misc, z, dtb, alog, dskip_e, norm_w, e_mat, name):
    b, s, _ = xs.shape
    ln, nc = CHUNK, s // CHUNK
    gw = D_SSD // SSD_GROUPS
    hpg = SSD_HEADS // SSD_GROUPS

    def body(xs_ref, b_ref, c_ref, misc_ref, z_ref, dtb_ref, alog_ref, dsk_ref, nw_ref, e_ref,
             ys_ref, y_ref, p_ref, st, yd):
        @pl.when(pl.program_id(1) == 0)
        def _():
            st[...] = jnp.zeros_like(st)
        cm = _ssd_common(misc_ref, dtb_ref, alog_ref, e_ref)
        xsv = xs_ref[...]
        xdt = xsv * cm["dt_e"]
        xdt_b = xdt.astype(BF16)
        xd_b = (xdt * cm["del_e"]).astype(BF16)
        gam_e = cm["eac_e"][ln - 1:ln, :]
        p_ref[...] = st[...]
        groups = [slice(gw * g, gw * (g + 1)) for g in range(SSD_GROUPS)]
        heads = [slice(SSD_HEAD_DIM * h, SSD_HEAD_DIM * (h + 1)) for h in range(SSD_HEADS)]
        bgs = [b_ref[:, SSD_STATE * g:SSD_STATE * (g + 1)].astype(BF16) for g in range(SSD_GROUPS)]
        cgs = [c_ref[:, SSD_STATE * g:SSD_STATE * (g + 1)].astype(BF16) for g in range(SSD_GROUPS)]
        cbs = [lax.dot_general(cg, bg, NT_DIMS, preferred_element_type=F32) for cg, bg in zip(cgs, bgs)]
        sts = [st[:, gs] for gs in groups]
        yoff = [jnp.dot(cg, st_g.astype(BF16), preferred_element_type=F32) * cm["eac_e"][:, gs]
                for cg, st_g, gs in zip(cgs, sts, groups)]
        news = [lax.dot_general(bg, xd_b[:, gs], TN_DIMS, preferred_element_type=F32) for bg, gs in zip(bgs, groups)]
        for gs, st_g, new in zip(groups, sts, news):
            st[:, gs] = st_g * gam_e[:, gs] + new
        ms = [(cbs[h // hpg] * _decay(cm, h)).astype(BF16) for h in range(SSD_HEADS)]
        for h, hs in enumerate(heads):
            yd[:, hs] = jnp.dot(ms[h], xdt_b[:, hs], preferred_element_type=F32)
        y = yd[...] + jnp.concatenate(yoff, axis=1) + dsk_ref[...] * xsv
        y_ref[...] = y
        zz = z_ref[...]
        yg = y * (zz * _sigmoid(zz))
        outs = []
        for g in range(SSD_GROUPS):
            ygg = yg[:, gw * g:gw * (g + 1)]
            outs.append(ygg * _rms(ygg) * nw_ref[:, gw * g:gw * (g + 1)])
        ys_ref[...] = jnp.concatenate(outs, axis=1).astype(BF16)

    row = lambda d: pl.BlockSpec((None, ln, d), lambda bb, c: (bb, c, 0))
    return pl.pallas_call(
        body, name=name, grid=(b, nc),
        in_specs=[row(D_SSD), row(256), row(256), row(LANES), row(D_SSD), _full((1, LANES)), _full((1, LANES)),
                  _full((1, D_SSD)), _full((1, D_SSD)), _full((LANES, D_SSD))],
        out_specs=[row(D_SSD), row(D_SSD), pl.BlockSpec((None, None, SSD_STATE, D_SSD), lambda bb, c: (bb, c, 0, 0))],
        out_shape=[SDS((b, s, D_SSD), BF16), SDS((b, s, D_SSD), F32), SDS((b, nc, SSD_STATE, D_SSD), F32)],
        scratch_shapes=[pltpu.VMEM((SSD_STATE, D_SSD), F32), pltpu.VMEM((ln, D_SSD), F32)],
        compiler_params=_cparams(2))(xs, bm, cm_, misc, z, dtb, alog, dskip_e, norm_w, e_mat)


def ssd_bwd(dys, y, z, xs, bm, cm_, misc, prev, dtb, alog, dskip_e, norm_w, e_mat, et_mat, name):
    b, s, _ = xs.shape
    ln, nc = CHUNK, s // CHUNK
    gw = D_SSD // SSD_GROUPS
    hpg = SSD_HEADS // SSD_GROUPS

    def body(dys_ref, y_ref, z_ref, xs_ref, b_ref, c_ref, misc_ref, p_ref, dtb_ref, alog_ref, dsk_ref, nw_ref,
             e_ref, et_ref, dxs_ref, db_ref, dc_ref, dz_ref, ddt_ref, dnw_ref, ddsk_ref, ddtb_ref, dalog_ref,
             dst, dxd, dac_t):
        @pl.when(_first_step())
        def _():
            for r_ in (dnw_ref, ddsk_ref, ddtb_ref, dalog_ref):
                r_[...] = jnp.zeros_like(r_)

        @pl.when(pl.program_id(1) == 0)
        def _():
            dst[...] = jnp.zeros_like(dst)

        cm = _ssd_common(misc_ref, dtb_ref, alog_ref, e_ref)
        et = et_ref[...]
        squeeze = lambda t: _dot_01(t, et, 2)
        lane = lax.broadcasted_iota(I32, (ln, LANES), 1)
        sub = lax.broadcasted_iota(I32, (LANES, ln), 0)
        xsv = xs_ref[...]
        xdt = xsv * cm["dt_e"]
        xdt_b = xdt.astype(BF16)
        xd_b = (xdt * cm["del_e"]).astype(BF16)
        eac_e = cm["eac_e"]
        gam_e = eac_e[ln - 1:ln, :]

        yv, zz, dyo = y_ref[...], z_ref[...], dys_ref[...]
        sz = _sigmoid(zz)
        silu_z = zz * sz
        yg = yv * silu_z
        dyg, dnw = [], []
        for g in range(SSD_GROUPS):
            gs = slice(gw * g, gw * (g + 1))
            ygg = yg[:, gs]
            r = _rms(ygg)
            n = ygg * r
            dnw.append(jnp.sum(dyo[:, gs] * n, axis=0, keepdims=True))
            dyg.append(_rms_bwd(dyo[:, gs] * nw_ref[:, gs], n, r))
        dyg = jnp.concatenate(dyg, axis=1)
        dnw_ref[...] += jnp.concatenate(dnw, axis=1)
        dz_ref[...] = (dyg * yv * (sz * (1.0 + zz * (1.0 - sz)))).astype(BF16)
        dy = dyg * silu_z
        ddsk_ref[...] += jnp.sum(dy * xsv, axis=0, keepdims=True)
        dy_b = dy.astype(BF16)

        dacum = jnp.zeros((ln, LANES), F32)
        dac_t[...] = jnp.zeros_like(dac_t)
        w1, dgam = [], []
        for g in range(SSD_GROUPS):
            gs = slice(gw * g, gw * (g + 1))
            ss = slice(SSD_STATE * g, SSD_STATE * (g + 1))
            bg = b_ref[:, ss].astype(BF16)
            cg = c_ref[:, ss].astype(BF16)
            cb = lax.dot_general(cg, bg, NT_DIMS, preferred_element_type=F32)
            pt = p_ref[:, gs]
            pt_b = pt.astype(BF16)
            dst_g = dst[:, gs]
            dst_b = dst_g.astype(BF16)
            edy = (dy[:, gs] * eac_e[:, gs]).astype(BF16)
            dcg = lax.dot_general(edy, pt_b, NT_DIMS, preferred_element_type=F32)
            dpt = lax.dot_general(cg, edy, TN_DIMS, preferred_element_type=F32)
            yoff = jnp.dot(cg, pt_b, preferred_element_type=F32) * eac_e[:, gs]
            dxd_g = jnp.dot(bg, dst_b, preferred_element_type=F32)
            dbg = lax.dot_general(xd_b[:, gs], dst_b, NT_DIMS, preferred_element_type=F32)
            ddel = dxd_g * xdt[:, gs] * cm["del_e"][:, gs]
            w1.append(dy[:, gs] * yoff - ddel)
            dgam.append(jnp.sum(ddel, axis=0, keepdims=True) + jnp.sum(dst_g * pt, axis=0, keepdims=True) * gam_e[:, gs])
            dxd[:, gs] = dxd_g * cm["del_e"][:, gs]
            dst[:, gs] = dst_g * gam_e[:, gs] + dpt
            dcb = jnp.zeros((ln, ln), F32)
            for j in range(hpg):
                h = hpg * g + j
                hs = slice(SSD_HEAD_DIM * h, SSD_HEAD_DIM * (h + 1))
                lam = _decay(cm, h)
                m = cb * lam
                dm = lax.dot_general(dy_b[:, hs], xdt_b[:, hs], NT_DIMS, preferred_element_type=F32)
                dxd[:, hs] += lax.dot_general(m.astype(BF16), dy_b[:, hs], TN_DIMS, preferred_element_type=F32)
                dcb += dm * lam
                wl = dm * m
                dacum += jnp.where(lane == h, jnp.sum(wl, axis=1, keepdims=True), 0.0)
                dac_t[...] -= jnp.where(sub == h, jnp.sum(wl, axis=0, keepdims=True), 0.0)
            dcb_b = dcb.astype(BF16)
            dc_ref[:, ss] = dcg + jnp.dot(dcb_b, bg, preferred_element_type=F32)
            db_ref[:, ss] = dbg + lax.dot_general(dcb_b, cg, TN_DIMS, preferred_element_type=F32)

        dxdt = dxd[...]
        dxs_ref[...] = dy * dsk_ref[...] + dxdt * cm["dt_e"]
        dacum += squeeze(jnp.concatenate(w1, axis=1)) + dac_t[...].T
        dlast = squeeze(jnp.broadcast_to(jnp.concatenate(dgam, axis=1), (8, D_SSD)))[0:1, :]
        dacum += jnp.where(lax.broadcasted_iota(I32, (ln, LANES), 0) == ln - 1, dlast, 0.0)
        triu = (cm["ci"] >= cm["ri"]).astype(F32)
        da = jnp.dot(triu, dacum, preferred_element_type=F32, precision=HI)
        ddt = da * cm["a_neg"] + squeeze(dxdt * xsv)
        dalog_ref[...] += jnp.sum(da * cm["dt_s"], axis=0, keepdims=True) * cm["a_neg"]
        ddt_raw = jnp.where(lane < SSD_HEADS, ddt * _sigmoid(cm["pre"]), 0.0)
        ddt_ref[...] = ddt_raw
        ddtb_ref[...] += jnp.sum(ddt_raw, axis=0, keepdims=True)

    row = lambda d: pl.BlockSpec((None, ln, d), lambda bb, c: (bb, nc - 1 - c, 0))
    return pl.pallas_call(
        body, name=name, grid=(b, nc),
        in_specs=[row(D_SSD), row(D_SSD), row(D_SSD), row(D_SSD), row(256), row(256), row(LANES),
                  pl.BlockSpec((None, None, SSD_STATE, D_SSD), lambda bb, c: (bb, nc - 1 - c, 0, 0)),
                  _full((1, LANES)), _full((1, LANES)), _full((1, D_SSD)), _full((1, D_SSD)),
                  _full((LANES, D_SSD)), _full((D_SSD, LANES))],
        out_specs=[row(D_SSD), row(256), row(256), row(D_SSD), row(LANES),
                   _full((1, D_SSD)), _full((1, D_SSD)), _full((1, LANES)), _full((1, LANES))],
        out_shape=[SDS((b, s, D_SSD), F32), SDS((b, s, 256), F32), SDS((b, s, 256), F32), SDS((b, s, D_SSD), BF16),
                   SDS((b, s, LANES), F32), SDS((1, D_SSD), F32), SDS((1, D_SSD), F32), SDS((1, LANES), F32),
                   SDS((1, LANES), F32)],
        scratch_shapes=[pltpu.VMEM((SSD_STATE, D_SSD), F32), pltpu.VMEM((ln, D_SSD), F32), pltpu.VMEM((LANES, ln), F32)],
        compiler_params=_cparams(2))(dys, y, z, xs, bm, cm_, misc, prev, dtb, alog, dskip_e, norm_w, e_mat, et_mat)


def _rope(xv, cc, sp, sm):
    n = xv.shape[1]
    return xv * cc + pltpu.roll(xv, 16, 1) * sp + pltpu.roll(xv, n - 16, 1) * sm


def _rope_bwd(dy, cc, sp, sm):
    n = dy.shape[1]
    return dy * cc + pltpu.roll(dy * sp, n - 16, 1) + pltpu.roll(dy * sm, 16, 1)


def _tile8(t):
    return jnp.concatenate([t] * MLA_HEADS, axis=1)


def qkv_fwd(cq, ckv, misc, cc, sp, sm, qnw, kvnw, wuq_t, wukv_t, place, name):
    b, s, _ = cq.shape
    tm = _att_tile(s)
    hd = MLA_HEADS * HEAD_PAD

    def body(cq_ref, ckv_ref, misc_ref, cc_ref, sp_ref, sm_ref, qnw_ref, kvnw_ref, wq_ref, wkv_ref, pl_ref,
             q_ref, k_ref, v_ref, vt_ref, qn_ref, kvn_ref):
        cqv, ckvv = cq_ref[...], ckv_ref[...]
        qn = (cqv * _rms(cqv) * qnw_ref[...]).astype(BF16)
        kvn = (ckvv * _rms(ckvv) * kvnw_ref[...]).astype(BF16)
        qn_ref[...] = qn
        kvn_ref[...] = kvn
        cc1, sp1, sm1 = cc_ref[...], sp_ref[...], sm_ref[...]
        q = lax.dot_general(qn, wq_ref[...], NT_DIMS, preferred_element_type=F32)
        q_ref[...] = _rope(q, _tile8(cc1), _tile8(sp1), _tile8(sm1)).astype(BF16)
        kv = lax.dot_general(kvn, wkv_ref[...], NT_DIMS, preferred_element_type=F32)
        kr = jnp.dot(misc_ref[...], pl_ref[...], preferred_element_type=F32, precision=HI)
        kr = _rope(kr, cc1, sp1, sm1)
        k_ref[...] = (kv[:, 0:hd] + _tile8(kr)).astype(BF16)
        v_ref[...] = kv[:, hd:2 * hd].astype(BF16)
        for h in range(MLA_HEADS):
            vt_ref[h] = kv[:, hd + HEAD_PAD * h:hd + HEAD_PAD * (h + 1)].T.astype(BF16)

    return pl.pallas_call(
        body, name=name, grid=(b, s // tm),
        in_specs=[_row(tm, Q_LORA), _row(tm, KV_LORA), _row(tm, LANES), _row(tm, LANES), _row(tm, LANES), _row(tm, LANES),
                  _full((1, Q_LORA)), _full((1, KV_LORA)), _full(wuq_t.shape), _full(wukv_t.shape), _full((LANES, LANES))],
        out_specs=[_row(tm, hd), _row(tm, hd), _row(tm, hd),
                   pl.BlockSpec((None, MLA_HEADS, None, HEAD_PAD, tm), lambda bb, i: (bb, 0, i, 0, 0)),
                   _row(tm, Q_LORA), _row(tm, KV_LORA)],
        out_shape=[SDS((b, s, hd), BF16)] * 3 + [SDS((b, MLA_HEADS, s // tm, HEAD_PAD, tm), BF16),
                                                 SDS((b, s, Q_LORA), BF16), SDS((b, s, KV_LORA), BF16)],
        compiler_params=_cparams(2))(cq, ckv, misc, cc, sp, sm, qnw, kvnw, wuq_t, wukv_t, place)


def qkv_bwd(dq, dk, dv, ddt, cq, ckv, cc, sp, sm, qnw, kvnw, wuq_t, wukv_t, place_t, name):
    b, s, _ = cq.shape
    tm = min(512, s)
    hd = MLA_HEADS * HEAD_PAD

    def body(dq_ref, dk_ref, dv_ref, ddt_ref, cq_ref, ckv_ref, cc_ref, sp_ref, sm_ref, qnw_ref, kvnw_ref,
             wq_ref, wkv_ref, plt_ref, dcq_ref, dckv_ref, dmisc_ref, dqp_ref, dkv_ref, dqnw_ref, dkvnw_ref):
        @pl.when(_first_step())
        def _():
            dqnw_ref[...] = jnp.zeros_like(dqnw_ref)
            dkvnw_ref[...] = jnp.zeros_like(dkvnw_ref)
        cc1, sp1, sm1 = cc_ref[...], sp_ref[...], sm_ref[...]
        dqp = _rope_bwd(dq_ref[...].astype(F32), _tile8(cc1), _tile8(sp1), _tile8(sm1)).astype(BF16)
        dqp_ref[...] = dqp
        dkv_b = jnp.concatenate([dk_ref[...], dv_ref[...]], axis=1)
        dkf = dk_ref[...].astype(F32)
        dkv_ref[...] = dkv_b
        dkr = dkf[:, 0:HEAD_PAD]
        for h in range(1, MLA_HEADS):
            dkr += dkf[:, HEAD_PAD * h:HEAD_PAD * (h + 1)]
        dkr = _rope_bwd(dkr, cc1, sp1, sm1)
        dmisc_ref[...] = (jnp.dot(dkr, plt_ref[...], preferred_element_type=F32, precision=HI) + ddt_ref[...]).astype(BF16)

        def norm_bwd(dn_w, xv, w_ref, dw_ref, dx_ref):
            r = _rms(xv)
            n = xv * r
            dw_ref[...] += jnp.sum(dn_w * n, axis=0, keepdims=True)
            dx_ref[...] = _rms_bwd(dn_w * w_ref[...], n, r).astype(BF16)

        norm_bwd(jnp.dot(dqp, wq_ref[...], preferred_element_type=F32), cq_ref[...], qnw_ref, dqnw_ref, dcq_ref)
        norm_bwd(jnp.dot(dkv_b, wkv_ref[...], preferred_element_type=F32), ckv_ref[...], kvnw_ref, dkvnw_ref, dckv_ref)

    return pl.pallas_call(
        body, name=name, grid=(b, s // tm),
        in_specs=[_row(tm, hd), _row(tm, hd), _row(tm, hd), _row(tm, LANES), _row(tm, Q_LORA), _row(tm, KV_LORA),
                  _row(tm, LANES), _row(tm, LANES), _row(tm, LANES), _full((1, Q_LORA)), _full((1, KV_LORA)),
                  _full(wuq_t.shape), _full(wukv_t.shape), _full((LANES, LANES))],
        out_specs=[_row(tm, Q_LORA), _row(tm, KV_LORA), _row(tm, LANES), _row(tm, hd), _row(tm, 2 * hd),
                   _full((1, Q_LORA)), _full((1, KV_LORA))],
        out_shape=[SDS((b, s, Q_LORA), BF16), SDS((b, s, KV_LORA), BF16), SDS((b, s, LANES), BF16),
                   SDS((b, s, hd), BF16), SDS((b, s, 2 * hd), BF16), SDS((1, Q_LORA), F32), SDS((1, KV_LORA), F32)],
        compiler_params=_cparams(2))(dq, dk, dv, ddt, cq, ckv, cc, sp, sm, qnw, kvnw, wuq_t, wukv_t, place_t)


ATT_SCALE = 1.0 / math.sqrt(QK_DIM)
LOG2E = math.log2(math.e)
ATT_SCALE_LOG2E = ATT_SCALE * LOG2E


ATT_HEADS_PER_STEP = 4
ATT_HEADS_PER_STEP_BWD = 2


def _att_tile(s):
    return min(512, s)


def flash_fwd(q, k, vt, name):
    b, s, hd = q.shape
    t = _att_tile(s)
    nb = s // t
    th = t // 2

    hps = ATT_HEADS_PER_STEP
    hw = hps * HEAD_PAD

    def body(q_ref, k_ref, vt_ref, o_ref, lse_ref, m_s, l_s, acc):
        i = pl.program_id(2)
        m_s[...] = jnp.full_like(m_s, -jnp.inf)
        l_s[...] = jnp.zeros_like(l_s)
        acc[...] = jnp.zeros_like(acc)

        def update(j, diagonal):
            ks = pl.ds(pl.multiple_of(j * t, t), t)
            chains = [(hh, half) for hh in range(hps) for half in range(2)]
            lanes = lambda hh: slice(HEAD_PAD * hh, HEAD_PAD * (hh + 1))
            cols = lambda half: slice(th * half, th * (half + 1))
            sts = {}
            for hh, half in chains:
                st = lax.dot_general(k_ref[ks, lanes(hh)], q_ref[cols(half), lanes(hh)], NT_DIMS,
                                     preferred_element_type=F32)
                if diagonal:
                    row = lax.broadcasted_iota(I32, (t, th), 0)
                    col = lax.broadcasted_iota(I32, (t, th), 1) + th * half
                    st = jnp.where(row <= col, st, -jnp.inf)
                sts[hh, half] = st
            pts, alphas = {}, {}
            for hh, half in chains:
                st, cs = sts[hh, half], cols(half)
                m_prev = m_s[hh, :, cs]
                m_new = jnp.maximum(m_prev, jnp.max(st, axis=0, keepdims=True))
                alpha = jnp.exp2((m_prev - m_new) * ATT_SCALE_LOG2E)
                pt = jnp.exp2((st - m_new) * ATT_SCALE_LOG2E)
                l_s[hh, :, cs] = alpha * l_s[hh, :, cs] + jnp.sum(pt, axis=0, keepdims=True)
                m_s[hh, :, cs] = m_new
                pts[hh, half], alphas[hh, half] = pt.astype(BF16), alpha
            for hh, half in chains:
                cs = cols(half)
                acc[hh, :, cs] = alphas[hh, half] * acc[hh, :, cs] + jnp.dot(vt_ref[hh, j], pts[hh, half],
                                                                             preferred_element_type=F32)

        def step(j, carry):
            update(j, False)
            return carry

        lax.fori_loop(0, i, step, 0)
        update(i, True)
        for hh in range(hps):
            o_ref[:, HEAD_PAD * hh:HEAD_PAD * (hh + 1)] = (acc[hh] / l_s[hh]).T
            lse_ref[hh] = m_s[hh] * ATT_SCALE + jnp.log(l_s[hh])

    qs = pl.BlockSpec((None, t, hw), lambda bb, h, i: (bb, i, h))
    ks = pl.BlockSpec((None, s, hw), lambda bb, h, i: (bb, 0, h))
    vs = pl.BlockSpec((None, hps, nb, HEAD_PAD, t), lambda bb, h, i: (bb, h, 0, 0, 0))
    ls = pl.BlockSpec((None, hps, None, 1, t), lambda bb, h, i: (bb, h, i, 0, 0))
    return pl.pallas_call(
        body, name=name, grid=(b, MLA_HEADS // hps, nb),
        in_specs=[qs, ks, vs], out_specs=[qs, ls],
        out_shape=[SDS((b, s, hd), F32), SDS((b, MLA_HEADS, nb, 1, t), F32)],
        scratch_shapes=[pltpu.VMEM((hps, 1, t), F32), pltpu.VMEM((hps, 1, t), F32), pltpu.VMEM((hps, HEAD_PAD, t), F32)],
        compiler_params=_cparams(3))(q, k, vt)


def flash_bwd(q, k, v, do, lse, dlt, name):
    b, s, hd = q.shape
    t = _att_tile(s)
    nb = s // t
    th = t // 2
    lse_r = lse
    dlt_r = dlt.reshape(b, MLA_HEADS, nb, 1, t)

    hps = ATT_HEADS_PER_STEP_BWD
    hw = hps * HEAD_PAD

    def body(q_ref, k_ref, v_ref, do_ref, lse_ref, dlt_ref, dq_ref, dk_ref, dv_ref, dq_s, dk_s, dv_s):
        dq_s[...] = jnp.zeros_like(dq_s)
        dk_s[...] = jnp.zeros_like(dk_s)
        dv_s[...] = jnp.zeros_like(dv_s)

        def tile(j, i, diagonal):
            qs = pl.ds(pl.multiple_of(i * t, t), t)
            chains = [(hh, half) for hh in range(hps) for half in range(2)]
            lanes = lambda hh: slice(HEAD_PAD * hh, HEAD_PAD * (hh + 1))
            keys = lambda half: pl.ds(pl.multiple_of(j * t + th * half, th), th)
            sts, dpts = {}, {}
            for hh, half in chains:
                ls_, ks = lanes(hh), keys(half)
                st = lax.dot_general(k_ref[ks, ls_], q_ref[qs, ls_], NT_DIMS, preferred_element_type=F32)
                if diagonal:
                    row = lax.broadcasted_iota(I32, (th, t), 0) + th * half
                    col = lax.broadcasted_iota(I32, (th, t), 1)
                    st = jnp.where(row <= col, st, -jnp.inf)
                sts[hh, half] = st
                dpts[hh, half] = lax.dot_general(v_ref[ks, ls_], do_ref[qs, ls_], NT_DIMS, preferred_element_type=F32)
            pts, dsts = {}, {}
            for hh, half in chains:
                pt = jnp.exp2(sts[hh, half] * ATT_SCALE_LOG2E - lse_ref[hh, i] * LOG2E)
                pts[hh, half] = pt.astype(BF16)
                dsts[hh, half] = (pt * (dpts[hh, half] - dlt_ref[hh, i])).astype(BF16)
            for hh in range(hps):
                ls_ = lanes(hh)
                dq_acc = None
                for half in range(2):
                    ks = keys(half)
                    dv_s[ks, ls_] += jnp.dot(pts[hh, half], do_ref[qs, ls_], preferred_element_type=F32)
                    dk_s[ks, ls_] += jnp.dot(dsts[hh, half], q_ref[qs, ls_], preferred_element_type=F32)
                    part = lax.dot_general(dsts[hh, half], k_ref[ks, ls_], TN_DIMS, preferred_element_type=F32)
                    dq_acc = part if dq_acc is None else dq_acc + part
                dq_s[qs, ls_] += dq_acc

        def key_tile(j, carry):
            tile(j, j, True)

            def query_tile(i, c2):
                tile(j, i, False)
                return c2

            lax.fori_loop(j + 1, nb, query_tile, 0)
            return carry

        lax.fori_loop(0, nb, key_tile, 0)
        dq_ref[...] = (dq_s[...] * ATT_SCALE).astype(BF16)
        dk_ref[...] = (dk_s[...] * ATT_SCALE).astype(BF16)
        dv_ref[...] = dv_s[...].astype(BF16)

    hs = pl.BlockSpec((None, s, hw), lambda bb, h: (bb, 0, h))
    ls = pl.BlockSpec((None, hps, nb, 1, t), lambda bb, h: (bb, h, 0, 0, 0))
    return pl.pallas_call(
        body, name=name, grid=(b, MLA_HEADS // hps),
        in_specs=[hs, hs, hs, hs, ls, ls], out_specs=[hs, hs, hs],
        out_shape=[SDS((b, s, hd), BF16)] * 3, scratch_shapes=[pltpu.VMEM((s, hw), F32)] * 3,
        compiler_params=_cparams(2))(q, k, v, do, lse_r, dlt_r)


def out_proj(ys, attn, mnw, wo, x, gate, above, name):
    b, s, d = x.shape
    tm = min(512, s)

    def body(ys_ref, at_ref, mnw_ref, wo_ref, x_ref, g_ref, nw_ref, sc_ref, sh_ref, xn_ref, o_ref, ym_ref, h_ref):
        av = at_ref[...]
        ym = (av * _rms(av) * mnw_ref[...]).astype(BF16)
        ym_ref[...] = ym
        o = jnp.dot(ys_ref[...], wo_ref[0:D_SSD, :], preferred_element_type=F32)
        o += jnp.dot(ym, wo_ref[D_SSD:2 * D_SSD, :], preferred_element_type=F32)
        xn = x_ref[...] + g_ref[...] * o
        xn_ref[...] = xn
        o_ref[...] = o.astype(BF16)
        h_ref[...] = _norm_mod_tile(xn, nw_ref, sc_ref, sh_ref)

    return pl.pallas_call(
        body, name=name, grid=(b, s // tm),
        in_specs=[_row(tm, D_SSD), _row(tm, D_SSD), _full((1, D_SSD)), _full(wo.shape), _row(tm, d), _bvec(d),
                  _full((1, d)), _bvec(d), _bvec(d)],
        out_specs=[_row(tm, d), _row(tm, d), _row(tm, D_SSD), _row(tm, d)],
        out_shape=[SDS((b, s, d), F32), SDS((b, s, d), BF16), SDS((b, s, D_SSD), BF16), SDS((b, s, d), BF16)],
        compiler_params=_cparams(2))(ys, attn, mnw, wo, x, gate, *above)


def out_proj_bwd(dout, attn, mnw, wo, name):
    b, s, d = dout.shape
    tm = min(512, s)

    def body(do_ref, at_ref, mnw_ref, wo_ref, dys_ref, dat_ref, dlt_ref, dw_ref):
        lane = lax.broadcasted_iota(I32, (tm, LANES), 1)
        @pl.when(_first_step())
        def _():
            dw_ref[...] = jnp.zeros_like(dw_ref)
        dov = do_ref[...]
        dys_ref[...] = lax.dot_general(dov, wo_ref[0:D_SSD, :], NT_DIMS, preferred_element_type=F32)
        dym = lax.dot_general(dov, wo_ref[D_SSD:2 * D_SSD, :], NT_DIMS, preferred_element_type=F32)
        av = at_ref[...]
        r = _rms(av)
        n = av * r
        dw_ref[...] += jnp.sum(dym * n, axis=0, keepdims=True)
        dat = _rms_bwd(dym * mnw_ref[...], n, r)
        dat_ref[...] = dat.astype(BF16)
        prod = dat * av
        cols = jnp.zeros((tm, LANES), F32)
        for h in range(MLA_HEADS):
            cols += jnp.where(lane == h, jnp.sum(prod[:, HEAD_PAD * h:HEAD_PAD * (h + 1)], axis=1, keepdims=True), 0.0)
        dlt_ref[...] = cols.T[0:MLA_HEADS, :]

    return pl.pallas_call(
        body, name=name, grid=(b, s // tm),
        in_specs=[_row(tm, d), _row(tm, D_SSD), _full((1, D_SSD)), _full(wo.shape)],
        out_specs=[_row(tm, D_SSD), _row(tm, D_SSD),
                   pl.BlockSpec((None, MLA_HEADS, tm), lambda bb, i: (bb, 0, i)), _full((1, D_SSD))],
        out_shape=[SDS((b, s, D_SSD), F32), SDS((b, s, D_SSD), BF16), SDS((b, MLA_HEADS, s), F32),
                   SDS((1, D_SSD), F32)],
        compiler_params=_cparams(2))(dout, attn, mnw, wo)


def adaln_fwd(c_all, w_ada, b_ada, name):
    nb, d = c_all.shape
    n = w_ada.shape[1]

    def body(c_ref, w_ref, b_ref, m_ref, ca_ref):
        cv = c_ref[...]
        ca = (cv * _sigmoid(cv)).astype(BF16)
        ca_ref[...] = ca
        m_ref[...] = jnp.dot(ca, w_ref[...].astype(BF16), preferred_element_type=F32) + b_ref[...]

    return pl.pallas_call(
        body, name=name, out_shape=[SDS((nb, n), F32), SDS((nb, d), BF16)],
        compiler_params=pltpu.CompilerParams(vmem_limit_bytes=VMEM_LIMIT))(c_all, w_ada, b_ada)


def adaln_bwd(c_act, dmod_cols, name):
    d, n = c_act.shape[1], dmod_cols.shape[1]

    def body(c_ref, dm_ref, gw_ref):
        gw_ref[...] = lax.dot_general(c_ref[...], dm_ref[...].astype(BF16), TN_DIMS, preferred_element_type=F32)

    return pl.pallas_call(
        body, name=name, out_shape=SDS((d, n), F32),
        compiler_params=pltpu.CompilerParams(vmem_limit_bytes=VMEM_LIMIT))(c_act, dmod_cols)


def sum_rows(x, name):
    def body(x_ref, o_ref):
        o_ref[...] = jnp.sum(x_ref[...], axis=0, keepdims=True)
    return pl.pallas_call(body, name=name, out_shape=SDS((1, x.shape[1]), F32))(x)


def squeeze_heads(x, et_mat, name):
    def body(x_ref, et_ref, o_ref):
        xv = jnp.broadcast_to(x_ref[...], (8, x.shape[1]))
        o_ref[...] = _dot_01(xv, et_ref[...], 3)[0:1, :]
    return pl.pallas_call(body, name=name, out_shape=SDS((1, LANES), F32))(x, et_mat)


def sum_blocks(x, name):
    n, r, c = x.shape
    tr = next(cand for cand in (256, 128, 64, 32, 16, 8) if r % cand == 0)

    def body(x_ref, o_ref):
        acc = x_ref[0].astype(F32)
        for k in range(1, n):
            acc += x_ref[k].astype(F32)
        o_ref[...] = acc

    return pl.pallas_call(
        body, name=name, grid=(r // tr,), in_specs=[pl.BlockSpec((n, tr, c), lambda i: (0, i, 0))],
        out_specs=pl.BlockSpec((tr, c), lambda i: (i, 0)), out_shape=SDS((r, c), F32),
        compiler_params=_cparams(1))(x)


def _adam_math(w, g, m, v):
    m = ADAM_B1 * m + (1.0 - ADAM_B1) * g
    v = ADAM_B2 * v + (1.0 - ADAM_B2) * (g * g)
    m_hat = m / (1.0 - ADAM_B1 ** ADAM_STEP)
    v_hat = v / (1.0 - ADAM_B2 ** ADAM_STEP)
    return -ADAM_LR * (m_hat / (jnp.sqrt(v_hat) + ADAM_EPS) + ADAM_WD * w), m, v


def adamw(w, g, m, v, name):
    r, c = w.shape[-2:]
    tr = r
    for cand in (512, 256, 128, 64, 32, 16, 8):
        if r % cand == 0 and cand * c * 4 <= 2 * 1024 * 1024:
            tr = cand
            break

    def body(w_ref, g_ref, m_ref, v_ref, d_ref, mo_ref, vo_ref):
        d_ref[...], mo_ref[...], vo_ref[...] = _adam_math(w_ref[...], g_ref[...], m_ref[...], v_ref[...])

    def spec(a):
        return pl.BlockSpec((tr, c), lambda i: (i, 0)) if a.ndim == 2 else pl.BlockSpec((None, tr, c), lambda i: (0, i, 0))

    return pl.pallas_call(
        body, name=name, grid=(r // tr,), in_specs=[spec(w), spec(g), spec(m), spec(v)], out_specs=[spec(w)] * 3,
        out_shape=[SDS(w.shape, F32)] * 3, compiler_params=_cparams(1))(w, g, m, v)


def adamw_blocks(w, blocks, m, v, name):
    r, c = w.shape
    tr = next((cand for cand in (128, 64, 32, 16, 8) if r % cand == 0), r)

    def body(w_ref, b_ref, m_ref, v_ref, g_ref, d_ref, mo_ref, vo_ref):
        g = b_ref[0].astype(F32)
        for k in range(1, N_DEV):
            g += b_ref[k].astype(F32)
        g_ref[...] = g
        d_ref[...], mo_ref[...], vo_ref[...] = _adam_math(w_ref[...], g, m_ref[...], v_ref[...])

    spec = pl.BlockSpec((tr, c), lambda i: (i, 0))
    return pl.pallas_call(
        body, name=name, grid=(r // tr,),
        in_specs=[spec, pl.BlockSpec((N_DEV, tr, c), lambda i: (0, i, 0)), spec, spec], out_specs=[spec] * 4,
        out_shape=[SDS((r, c), F32)] * 4, compiler_params=_cparams(1))(w, blocks, m, v)


def adamw_many(ws, gs, ms, vs, name):
    n = len(ws)

    def body(*refs):
        w_r, g_r, m_r, v_r = (refs[k * n:(k + 1) * n] for k in range(4))
        d_r, mo_r, vo_r = (refs[(4 + k) * n:(5 + k) * n] for k in range(3))
        for k in range(n):
            d_r[k][...], mo_r[k][...], vo_r[k][...] = _adam_math(w_r[k][...], g_r[k][...], m_r[k][...], v_r[k][...])

    shapes = [SDS(w.shape, F32) for w in ws]
    outs = pl.pallas_call(body, name=name, out_shape=shapes * 3)(*ws, *gs, *ms, *vs)
    return outs[:n], outs[n:2 * n], outs[2 * n:]


TRANSPOSED = ("ffn1_w_gate", "ffn1_w_up", "ffn2_w_gate", "ffn2_w_up", "w_in", "w_ukv", "w_uq")
GATHER_GROUPS = (("ffn1_w_gate", "ffn1_w_up"), ("ffn1_w_down",),
                 ("w_in", "w_ukv", "w_uq", "w_out", "ffn2_w_gate", "ffn2_w_up", "ffn2_w_down"))
GRAD_GROUPS = (("ffn2", ("ffn2_w_gate", "ffn2_w_up", "ffn2_w_down")), ("mixer", ("w_out", "w_in", "w_ukv", "w_uq")),
               ("ffn1_down", ("ffn1_w_down",)), ("ffn1_gate", ("ffn1_w_gate",)), ("ffn1_up", ("ffn1_w_up",)))


def _shard_view(name, w):
    return w[0].T if name in TRANSPOSED else w[0]


def _shard_unview(name, t):
    return t.T[None] if name in TRANSPOSED else t[None]


def _grad_blocks(name, gw):
    if name == "w_in":
        return _in_proj_rows_inv(gw).reshape(N_DEV, -1, D_MODEL)
    if name == "w_ukv":
        hd = MLA_HEADS * HEAD_PAD
        return jnp.concatenate([gw[:hd].reshape(MLA_HEADS, HEAD_PAD, KV_LORA)[:, :QK_NOPE],
                                gw[hd:].reshape(MLA_HEADS, V_HEAD, KV_LORA)], axis=1)
    if name == "w_uq":
        return gw.reshape(MLA_HEADS, HEAD_PAD, Q_LORA)[:, :QK_DIM]
    return gw.reshape(N_DEV, -1, D_MODEL)


def _pack_rows(arrs):
    parts = []
    for a in arrs:
        flat = a.reshape(-1).astype(F32)
        pad = (-flat.shape[0]) % D_MODEL
        if pad:
            flat = jnp.pad(flat, (0, pad))
        parts.append(flat.reshape(-1, D_MODEL))
    out = jnp.concatenate(parts, axis=0)
    pad = (-out.shape[0]) % 8
    if pad:
        out = jnp.pad(out, ((0, pad), (0, 0)))
    return out


def _unpack_rows(packed, shapes):
    out, row = [], 0
    for shp in shapes:
        n = math.prod(shp)
        nrow = -(-n // D_MODEL)
        out.append(packed[row:row + nrow].reshape(-1)[:n].reshape(shp))
        row += nrow
    return out


def _in_proj_rows(w_t):
    return jnp.concatenate([w_t[0:2560], w_t[2576:2960], w_t[2960:3216], w_t[2560:2576], w_t[3216:3248],
                            jnp.zeros((D_IN_PAD - D_IN, D_MODEL), w_t.dtype)], axis=0)


def _in_proj_rows_inv(d):
    return jnp.concatenate([d[0:2560], d[3200:3216], d[2560:2944], d[2944:3200], d[3216:3248]], axis=0)


def _rope_tables(positions):
    inv_freq = ROPE_THETA ** (-jnp.arange(0, QK_ROPE, 2, dtype=F32) / QK_ROPE)
    ang = positions[..., None].astype(F32) * inv_freq
    cos, sin = jnp.cos(ang), jnp.sin(ang)
    one = jnp.ones(ang.shape[:2] + (QK_NOPE,), F32)
    zero = jnp.zeros_like(one)
    z16, z32, o32 = zero[..., :16], zero[..., :32], one[..., :32]
    cc = jnp.concatenate([one, cos, cos, o32], axis=-1)
    sp = jnp.concatenate([zero, z16, sin, z32], axis=-1)
    sm = jnp.concatenate([zero, -sin, z16, z32], axis=-1)
    return cc, sp, sm


def weight_views(gathered):
    full = lambda name: gathered[name].reshape(-1, gathered[name].shape[2])
    ukv = full("w_ukv").reshape(MLA_HEADS, QK_NOPE + V_HEAD, KV_LORA)
    wukv_t = jnp.concatenate([jnp.pad(ukv[:, :QK_NOPE], ((0, 0), (0, HEAD_PAD - QK_NOPE), (0, 0))).reshape(-1, KV_LORA),
                              ukv[:, QK_NOPE:].reshape(-1, KV_LORA)], axis=0)
    uq = full("w_uq").reshape(MLA_HEADS, QK_DIM, Q_LORA)
    wuq_t = jnp.pad(uq, ((0, 0), (0, HEAD_PAD - QK_DIM), (0, 0))).reshape(-1, Q_LORA)
    return dict(wg1_t=full("ffn1_w_gate"), wu1_t=full("ffn1_w_up"), wd1=full("ffn1_w_down"),
                wg2_t=full("ffn2_w_gate"), wu2_t=full("ffn2_w_up"), wd2=full("ffn2_w_down"),
                wo=full("w_out"), win_t=_in_proj_rows(full("w_in")), wukv_t=wukv_t, wuq_t=wuq_t)


def _ffn_bwd(tag, dxn, do, dgate, x, h, gg, uu, a, sc, norm_w, wg_t, wu_t, wd, below):
    f2 = wd.shape[0] // 2
    dwd = mm_tn(a, do, f2, D_MODEL, tag + "_dwd")
    dgg, duu = ffn_dact(do, wd, gg, uu, tag + "_dact")
    dwg_t = mm_tn(dgg, h, f2, D_MODEL, tag + "_dwg")
    dwu_t = mm_tn(duu, h, f2, D_MODEL, tag + "_dwu")
    dx, dsc, dsh, dnw, *nxt = dh_norm_bwd([dgg, duu], [wg_t, wu_t], x, dxn, norm_w, sc, tag + "_dh", below)
    return dx, (dsh, dsc, dgate), dnw, (dwg_t, dwu_t, dwd), nxt


def local_step(x, tgt, positions, mod, wv, p):
    nb, s, d = x.shape
    sh1, sc1, g1, sh2, sc2, g2, sh3, sc3, g3 = mod
    cc, sp, sm = _rope_tables(positions)
    lane_head = jnp.arange(D_SSD, dtype=I32)[None, :] // SSD_HEAD_DIM
    e_mat = (lane_head == jnp.arange(LANES, dtype=I32)[:, None]).astype(BF16)
    et_mat = e_mat.T
    rr, cl = jnp.arange(LANES, dtype=I32)[:, None], jnp.arange(LANES, dtype=I32)[None, :]
    place = ((cl == rr + (QK_NOPE - SSD_HEADS)) & (rr >= SSD_HEADS) & (rr < SSD_HEADS + QK_ROPE)).astype(F32)
    dtb = jnp.pad(p["dt_bias"], ((0, 0), (0, LANES - SSD_HEADS)))
    alog = jnp.pad(p["a_log"], ((0, 0), (0, LANES - SSD_HEADS)))
    dskip_e = jnp.repeat(p["d_skip"], SSD_HEAD_DIM, axis=1)

    h1 = norm_mod(x, p["norm_ffn1"], sc1, sh1, "ffn1_norm")
    gg1, uu1, a1 = ffn_up(h1, wv["wg1_t"], wv["wu1_t"], "ffn1_up")
    x1, o1, h2 = ffn_down(a1, wv["wd1"], x, g1, 0.5, "ffn1_down", (p["norm_mix"], sc2, sh2))
    z, u, cq, ckv, misc = in_proj(h2, wv["win_t"], "in_proj")
    xs, bm, cm_ = conv_fwd(u, p["conv_w"], p["conv_b"], "conv_fwd")
    ys, y, prev = ssd_fwd(xs, bm, cm_, misc, z, dtb, alog, dskip_e, p["ssd_norm_w"], e_mat, "ssd_fwd")
    q, k, v, vt, qn, kvn = qkv_fwd(cq, ckv, misc, cc, sp, sm, p["q_norm_w"], p["kv_norm_w"], wv["wuq_t"], wv["wukv_t"],
                               place, "qkv_fwd")
    attn, lse = flash_fwd(q, k, vt, "flash_fwd")
    x2, o2, ym, h3 = out_proj(ys, attn, p["mla_norm_w"], wv["wo"], x1, g2, (p["norm_ffn2"], sc3, sh3), "out_proj")
    gg3, uu3, a3 = ffn_up(h3, wv["wg2_t"], wv["wu2_t"], "ffn2_up")
    loss, dx3, dnfin, do3, dg3 = ffn_down_final(a3, wv["wd2"], x2, g3, 0.5, p["norm_final"], tgt, "ffn2_down_loss")

    dx2, dmod3, dnf2, (dwg2, dwu2, dwd2), (dout, dg2) = _ffn_bwd(
        "ffn2", dx3, do3, dg3, x2, h3, gg3, uu3, a3, sc3, p["norm_ffn2"], wv["wg2_t"], wv["wu2_t"], wv["wd2"],
        (o2, g2, 1.0))
    dys, dattn, dlt, dmlan = out_proj_bwd(dout, attn, p["mla_norm_w"], wv["wo"], "out_proj_bwd")
    dwo = jnp.concatenate([mm_tn(ys, dout, D_SSD, D_MODEL, "dwo_ssd"), mm_tn(ym, dout, D_SSD, D_MODEL, "dwo_mla")], axis=0)
    dxs, dbm, dcm, dz, ddt, dssdn, ddsk_lane, ddtb, dalog = ssd_bwd(
        dys, y, z, xs, bm, cm_, misc, prev, dtb, alog, dskip_e, p["ssd_norm_w"], e_mat, et_mat, "ssd_bwd")
    dq, dk, dv = flash_bwd(q, k, v, dattn, lse, dlt, "flash_bwd")
    dcq, dckv, dmisc, dqp, dkvc, dqn, dkvn = qkv_bwd(dq, dk, dv, ddt, cq, ckv, cc, sp, sm, p["q_norm_w"], p["kv_norm_w"],
                                                     wv["wuq_t"], wv["wukv_t"], place.T, "qkv_bwd")
    dwuq = mm_tn(dqp, qn, MLA_HEADS * HEAD_PAD, Q_LORA, "dwuq")
    dwukv = mm_tn(dkvc, kvn, MLA_HEADS * HEAD_PAD, KV_LORA, "dwukv")
    dvv, dconv = conv_bwd_a(dxs, dbm, dcm, u, p["conv_w"], p["conv_b"], "conv_bwd_a")
    dconv = conv_grads_fold(dconv, "conv_grads_fold")
    du = conv_bwd_b(dvv, p["conv_w"], "conv_bwd_b")
    dproj = [dz, du, dcq, dckv, dmisc]
    dwin = mm_tn_blocks(dproj, h2, "dwin")
    dx1, dsc2, dsh2, dnmix, do1, dg1 = dh_norm_bwd(dproj, [wv["win_t"]], x1, dx2, p["norm_mix"], sc2, "mix_dh",
                                                   (o1, g1, 0.5))
    dx0, dmod1, dnf1, (dwg1, dwu1, dwd1), _ = _ffn_bwd(
        "ffn1", dx1, do1, dg1, x, h1, gg1, uu1, a1, sc1, p["norm_ffn1"], wv["wg1_t"], wv["wu1_t"], wv["wd1"], None)

    dmod = jnp.concatenate([*dmod1, dsh2, dsc2, dg2, *dmod3], axis=1).reshape(nb, N_MOD * d)
    return dict(
        loss=loss, dx=dx0, dmod=dmod, norm_ffn1=dnf1, norm_mix=dnmix, norm_ffn2=dnf2, norm_final=dnfin,
        ssd_norm_w=dssdn, mla_norm_w=dmlan, q_norm_w=dqn, kv_norm_w=dkvn,
        dt_bias=ddtb[:, :SSD_HEADS], a_log=dalog[:, :SSD_HEADS],
        d_skip=squeeze_heads(ddsk_lane, et_mat, "d_skip_heads")[:, :SSD_HEADS],
        conv_b=dconv[4:5], conv_w=dconv[0:4],
        gw=dict(ffn1_w_gate=dwg1, ffn1_w_up=dwu1, ffn1_w_down=dwd1, ffn2_w_gate=dwg2, ffn2_w_up=dwu2, ffn2_w_down=dwd2,
                w_out=dwo, w_in=dwin, w_ukv=dwukv, w_uq=dwuq))


def kernel(x, c, positions, w_ada, b_ada, norm_ffn1, ffn1_w_gate, ffn1_w_up, ffn1_w_down, norm_mix, w_in, conv_w, conv_b, dt_bias, a_log, d_skip, ssd_norm_w, q_norm_w, w_uq, kv_norm_w, w_ukv, mla_norm_w, w_out, norm_ffn2, ffn2_w_gate, ffn2_w_up, ffn2_w_down, norm_final, loss_target, m_w_ada, m_b_ada, m_norm_ffn1, m_ffn1_w_gate, m_ffn1_w_up, m_ffn1_w_down, m_norm_mix, m_w_in, m_conv_w, m_conv_b, m_dt_bias, m_a_log, m_d_skip, m_ssd_norm_w, m_q_norm_w, m_w_uq, m_kv_norm_w, m_w_ukv, m_mla_norm_w, m_w_out, m_norm_ffn2, m_ffn2_w_gate, m_ffn2_w_up, m_ffn2_w_down, m_norm_final, v_w_ada, v_b_ada, v_norm_ffn1, v_ffn1_w_gate, v_ffn1_w_up, v_ffn1_w_down, v_norm_mix, v_w_in, v_conv_w, v_conv_b, v_dt_bias, v_a_log, v_d_skip, v_ssd_norm_w, v_q_norm_w, v_w_uq, v_kv_norm_w, v_w_ukv, v_mla_norm_w, v_w_out, v_norm_ffn2, v_ffn2_w_gate, v_ffn2_w_up, v_ffn2_w_down, v_norm_final):
    names = ["w_ada", "b_ada", "norm_ffn1", "ffn1_w_gate", "ffn1_w_up", "ffn1_w_down", "norm_mix", "w_in", "conv_w",
             "conv_b", "dt_bias", "a_log", "d_skip", "ssd_norm_w", "q_norm_w", "w_uq", "kv_norm_w", "w_ukv",
             "mla_norm_w", "w_out", "norm_ffn2", "ffn2_w_gate", "ffn2_w_up", "ffn2_w_down", "norm_final"]
    W = dict(zip(names, (w_ada, b_ada, norm_ffn1, ffn1_w_gate, ffn1_w_up, ffn1_w_down, norm_mix, w_in, conv_w, conv_b, dt_bias, a_log, d_skip, ssd_norm_w, q_norm_w, w_uq, kv_norm_w, w_ukv, mla_norm_w, w_out, norm_ffn2, ffn2_w_gate, ffn2_w_up, ffn2_w_down, norm_final)))
    M = dict(zip(names, (m_w_ada, m_b_ada, m_norm_ffn1, m_ffn1_w_gate, m_ffn1_w_up, m_ffn1_w_down, m_norm_mix, m_w_in, m_conv_w, m_conv_b, m_dt_bias, m_a_log, m_d_skip, m_ssd_norm_w, m_q_norm_w, m_w_uq, m_kv_norm_w, m_w_ukv, m_mla_norm_w, m_w_out, m_norm_ffn2, m_ffn2_w_gate, m_ffn2_w_up, m_ffn2_w_down, m_norm_final)))
    V = dict(zip(names, (v_w_ada, v_b_ada, v_norm_ffn1, v_ffn1_w_gate, v_ffn1_w_up, v_ffn1_w_down, v_norm_mix, v_w_in, v_conv_w, v_conv_b, v_dt_bias, v_a_log, v_d_skip, v_ssd_norm_w, v_q_norm_w, v_w_uq, v_kv_norm_w, v_w_ukv, v_mla_norm_w, v_w_out, v_norm_ffn2, v_ffn2_w_gate, v_ffn2_w_up, v_ffn2_w_down, v_norm_final)))

    nb, s, d = x.shape
    me = 4 * lax.axis_index("x") + 2 * lax.axis_index("y") + lax.axis_index("c")
    n_ada = w_ada.shape[2]

    taps, n_cw = conv_w.shape[1:]
    (cg,) = all_gather8([_pack_rows([c, conv_w[0]])], "gather_c")
    c_all = cg[:, 0:nb].reshape(N_DEV * nb, d)
    conv_w_full = cg[:, nb, 0:taps * n_cw].reshape(N_DEV, taps, n_cw).transpose(1, 0, 2).reshape(taps, N_DEV * n_cw)
    shards = [[_shard_view(name, W[name]).astype(BF16) for name in group] for group in GATHER_GROUPS]
    cg, shards = lax.optimization_barrier((cg, shards))
    gathered = dict(zip(GATHER_GROUPS[0], sc_all_gather8(shards[0], "gather_w_ffn1", 8)))

    b_ada_cols = lax.dynamic_slice(b_ada, (0, me * n_ada), (1, n_ada))
    mod_cols, c_act = adaln_fwd(c_all, w_ada[0], b_ada_cols, "adaln_fwd")
    (mod_g,) = all_gather8([mod_cols], "gather_mod")
    mod_g, shards = lax.optimization_barrier((mod_g, shards))
    gathered.update(zip(GATHER_GROUPS[1], sc_all_gather8(shards[1], "gather_w_ffn1_down", 1)))
    gathered.update(zip(GATHER_GROUPS[2], sc_all_gather8(shards[2], "gather_w_rest", 7)))
    wv = weight_views(gathered)
    mod = lax.dynamic_slice(mod_g, (0, me * nb, 0), (N_DEV, nb, n_ada)).transpose(1, 0, 2).reshape(nb, N_MOD, 1, d)
    mod = [mod[:, k] for k in range(N_MOD)]

    P = dict(W)
    P["conv_w"] = conv_w_full
    P["norm_final"] = norm_final.reshape(1, d)
    R = local_step(x, loss_target, positions, mod, wv, P)

    dmod = R["dmod"]
    partial_shapes = [(1,), (1, d), (1, d), (1, d), (1, d), (1, d), (1, d), (1, Q_LORA), (1, KV_LORA),
                      (1, SSD_HEADS), (1, SSD_HEADS), (1, SSD_HEADS), (1, D_CONV), (4, D_CONV), (1, N_MOD * d),
                      (nb, N_MOD * d)]
    partial = _pack_rows([R["loss"][0, :1], R["norm_ffn1"], R["norm_mix"], R["norm_ffn2"], R["norm_final"],
                          R["ssd_norm_w"], R["mla_norm_w"], R["q_norm_w"], R["kv_norm_w"],
                          R["dt_bias"], R["a_log"], R["d_skip"], R["conv_b"], R["conv_w"],
                          sum_rows(dmod, "dmod_rows"), dmod])
    (partial_g,) = all_gather8([partial], "gather_partials")
    (loss, g_nf1, g_nmix, g_nf2, g_nfin, g_ssdn, g_mlan, g_qn, g_kvn, g_dtb, g_alog, g_dskip, g_convb, g_convw,
     g_bada, _) = _unpack_rows(sum_blocks(partial_g, "sum_partials"), partial_shapes)
    dmod_row = sum(-(-math.prod(shp) // D_MODEL) for shp in partial_shapes[:-1])
    dmod_all = partial_g[:, dmod_row:dmod_row + nb * N_MOD].reshape(N_DEV * nb, N_MOD * d)
    g_wada = adaln_bwd(c_act, lax.dynamic_slice(dmod_all, (0, me * n_ada), (N_DEV * nb, n_ada)), "adaln_bwd")
    n_cw = conv_w.shape[2]
    G = {"w_ada": g_wada[None], "b_ada": g_bada, "norm_ffn1": g_nf1, "norm_mix": g_nmix, "norm_ffn2": g_nf2,
         "norm_final": g_nfin.reshape(d), "ssd_norm_w": g_ssdn, "mla_norm_w": g_mlan, "q_norm_w": g_qn,
         "kv_norm_w": g_kvn, "dt_bias": g_dtb, "a_log": g_alog, "d_skip": g_dskip, "conv_b": g_convb,
         "conv_w": lax.dynamic_slice(g_convw, (0, me * n_cw), (4, n_cw))[None]}

    DW, NM, NV = {}, {}, {}
    gw = R["gw"]
    for k, (tag, group) in enumerate(GRAD_GROUPS):
        send = [_grad_blocks(name, gw[name]).reshape(N_DEV, *_shard_view(name, W[name]).shape) for name in group]
        recv = sc_all_to_all8(send, "exchange_" + tag, 2 + k)
        for name, blocks in zip(group, recv):
            res = adamw_blocks(_shard_view(name, W[name]), blocks, _shard_view(name, M[name]), _shard_view(name, V[name]),
                               "adamw_" + name)
            G[name], DW[name], NM[name], NV[name] = [_shard_unview(name, t) for t in res]
    DW["w_ada"], NM["w_ada"], NV["w_ada"] = adamw(w_ada, g_wada, m_w_ada, v_w_ada, "adamw_w_ada")
    small = [n for n in names if n not in DW]
    as2d = lambda a: a.reshape(-1, a.shape[-1])
    outs = adamw_many([as2d(W[n]) for n in small], [as2d(G[n]) for n in small], [as2d(M[n]) for n in small],
                      [as2d(V[n]) for n in small], "adamw_small")
    for res, dst in zip(outs, (DW, NM, NV)):
        for n, t in zip(small, res):
            dst[n] = t.reshape(W[n].shape)
    return (loss.reshape(()), R["dx"], *[G[n] for n in names], *[DW[n] for n in names], *[NM[n] for n in names],
            *[NV[n] for n in names])
```

```python
import math

import jax
import jax.numpy as jnp
from jax import lax
from jax.experimental import pallas as pl
from jax.experimental.pallas import tpu as pltpu
from jax.experimental.pallas import tpu_sc as plsc

F32, BF16, I32 = jnp.float32, jnp.bfloat16, jnp.int32
HI = lax.Precision.HIGHEST
SDS = jax.ShapeDtypeStruct
MESH = pl.DeviceIdType.MESH

D_MODEL = 1024
D_FF = 2816
D_SSD = 1024
SSD_HEADS = 16
SSD_HEAD_DIM = 64
SSD_GROUPS = 2
SSD_STATE = 128
CHUNK = 128
MLA_HEADS = 8
QK_NOPE = 64
QK_ROPE = 32
QK_DIM = 96
V_HEAD = 128
Q_LORA = 384
KV_LORA = 256
ROPE_THETA = 10000.0
N_MOD = 9
EPS = 1e-6
D_CONV = 1536
D_IN = 3248
D_IN_PAD = 3328
HEAD_PAD = 128
N_DEV = 8
ADAM_LR, ADAM_B1, ADAM_B2, ADAM_EPS, ADAM_WD, ADAM_STEP = 0.001, 0.9, 0.999, 1e-08, 0.01, 10

SAVED_ACT = BF16
VMEM_LIMIT = 56 * 1024 * 1024
LANES = 128
NT_DIMS = (((1,), (1,)), ((), ()))
TN_DIMS = (((0,), (0,)), ((), ()))


def _cparams(n_axes):
    return pltpu.CompilerParams(dimension_semantics=("arbitrary",) * n_axes, vmem_limit_bytes=VMEM_LIMIT)


def _row(tm, d):
    return pl.BlockSpec((None, tm, d), lambda b, i: (b, i, 0))


def _bvec(d):
    return pl.BlockSpec((None, 1, d), lambda b, i: (b, 0, 0))


def _full(shape):
    n = len(shape)
    return pl.BlockSpec(shape, lambda *_: (0,) * n)


def _sigmoid(x):
    return 1.0 / (1.0 + jnp.exp(-x))


def _softplus(x):
    return jnp.maximum(x, 0.0) + jnp.log(1.0 + jnp.exp(-jnp.abs(x)))


def _rms(x):
    return lax.rsqrt(jnp.mean(x * x, axis=-1, keepdims=True) + EPS)


def _rms_bwd(dn, n, r):
    return r * (dn - n * jnp.mean(dn * n, axis=-1, keepdims=True))


def _first_step():
    return (pl.program_id(0) == 0) & (pl.program_id(1) == 0)


def _gather_copies(x_refs, out_refs, send_sems, recv_sems, local_sems):
    mx, my, mc = lax.axis_index("x"), lax.axis_index("y"), lax.axis_index("c")
    me, sibling = (mx, my, mc), (mx, my, 1 - mc)
    chips = [(1 - mx, my), (mx, 1 - my), (1 - mx, 1 - my)]

    def copy(a, k, block, to, src=None):
        rows = out_refs[a].at[4 * block[0] + 2 * block[1] + block[2]]
        return pltpu.make_async_remote_copy(
            src_ref=rows if src is None else src, dst_ref=rows,
            send_sem=send_sems.at[7 * a + k], recv_sem=recv_sems.at[7 * a + k], device_id=to, device_id_type=MESH)

    arrays = range(len(x_refs))
    mine = [pltpu.make_async_copy(x_refs[a], out_refs[a].at[4 * mx + 2 * my + mc], local_sems.at[a]) for a in arrays]
    first = [[copy(a, 0, me, sibling, src=x_refs[a])] + [copy(a, 1 + j, me, (*chip, mc), src=x_refs[a])
                                                          for j, chip in enumerate(chips)] for a in arrays]
    passed = [[copy(a, 4 + j, (*chip, mc), sibling) for j, chip in enumerate(chips)] for a in arrays]
    for a in arrays:
        mine[a].start()
        for cp in first[a]:
            cp.start()
    for a in arrays:
        for j, chip in enumerate(chips):
            copy(a, 1 + j, (*chip, mc), me).wait_recv()
            passed[a][j].start()
    for a in arrays:
        copy(a, 0, sibling, me).wait_recv()
        for j, chip in enumerate(chips):
            copy(a, 4 + j, (*chip, 1 - mc), me).wait_recv()
    for a in arrays:
        for cp in first[a] + passed[a]:
            cp.wait_send()
        mine[a].wait()


def _gather_peers():
    mx, my, mc = lax.axis_index("x"), lax.axis_index("y"), lax.axis_index("c")
    return [(mx, my, 1 - mc), (1 - mx, my, mc), (mx, 1 - my, mc), (1 - mx, 1 - my, mc)]


def _comm_scratch(n):
    return [pltpu.SemaphoreType.DMA((7 * n,)), pltpu.SemaphoreType.DMA((7 * n,)), pltpu.SemaphoreType.DMA((n,))]


def all_gather8(xs, name):
    n = len(xs)

    def body(*refs):
        _gather_copies(refs[:n], refs[n:2 * n], *refs[2 * n:])

    return pl.pallas_call(
        body, name=name,
        out_shape=[SDS((N_DEV, *x.shape), x.dtype) for x in xs],
        in_specs=[pl.BlockSpec(memory_space=pl.ANY)] * n,
        out_specs=[pl.BlockSpec(memory_space=pl.ANY)] * n,
        scratch_shapes=_comm_scratch(n),
    )(*xs)


def _exchange_peers():
    mx, my, mc = lax.axis_index("x"), lax.axis_index("y"), lax.axis_index("c")
    return [(1 - mx if rel & 4 else mx, 1 - my if rel & 2 else my, 1 - mc if rel & 1 else mc) for rel in range(1, N_DEV)]


def _exchange_copies(x_refs, out_refs, send_sems, recv_sems, local_sems):
    mx, my, mc = lax.axis_index("x"), lax.axis_index("y"), lax.axis_index("c")
    me = 4 * mx + 2 * my + mc
    copies = []
    for a, (x_ref, out_ref) in enumerate(zip(x_refs, out_refs)):
        mine = pltpu.make_async_copy(x_ref.at[me], out_ref.at[me], local_sems.at[a])
        mine.start()
        copies.append(mine)
        for k, (px, py, pc) in enumerate(_exchange_peers()):
            cp = pltpu.make_async_remote_copy(
                src_ref=x_ref.at[4 * px + 2 * py + pc], dst_ref=out_ref.at[me],
                send_sem=send_sems.at[7 * a + k], recv_sem=recv_sems.at[7 * a + k],
                device_id=(px, py, pc), device_id_type=MESH)
            cp.start()
            copies.append(cp)
    for cp in copies:
        cp.wait()


def all_to_all8(xs, name):
    n = len(xs)

    def body(*refs):
        _exchange_copies(refs[:n], refs[n:2 * n], *refs[2 * n:])

    return pl.pallas_call(
        body, name=name,
        out_shape=[SDS(x.shape, x.dtype) for x in xs],
        in_specs=[pl.BlockSpec(memory_space=pl.ANY)] * n,
        out_specs=[pl.BlockSpec(memory_space=pl.ANY)] * n,
        scratch_shapes=_comm_scratch(n),
    )(*xs)


def _sequencer_kernel(name, collective_id, n_arrays):
    return pl.kernel(
        mesh=plsc.ScalarSubcoreMesh(axis_name="seq", num_cores=1), name=name,
        scratch_types=tuple(_comm_scratch(n_arrays)),
        compiler_params=pltpu.CompilerParams(collective_id=collective_id))


def _handshake(peers):
    barrier = pltpu.get_barrier_semaphore()
    for peer in peers:
        pl.semaphore_signal(barrier, inc=1, device_id=peer, device_id_type=MESH)
    pl.semaphore_wait(barrier, len(peers))


def _hbm_refs(xs, out_shapes):
    x_refs = [jax.new_ref(x, memory_space=pltpu.MemorySpace.HBM) for x in xs]
    out_refs = [jax.empty_ref(SDS(shp, x.dtype), memory_space=pltpu.MemorySpace.HBM) for x, shp in zip(xs, out_shapes)]
    return x_refs, out_refs


def sc_all_gather8(xs, name, collective_id):
    x_refs, out_refs = _hbm_refs(xs, [(N_DEV, *x.shape) for x in xs])

    @_sequencer_kernel(name, collective_id, len(xs))
    def launch(send_sems, recv_sems, local_sems):
        _handshake(_gather_peers())
        _gather_copies(x_refs, out_refs, send_sems, recv_sems, local_sems)

    launch()
    return [ref[...] for ref in out_refs]


def sc_all_to_all8(xs, name, collective_id):
    x_refs, out_refs = _hbm_refs(xs, [x.shape for x in xs])

    @_sequencer_kernel(name, collective_id, len(xs))
    def launch(send_sems, recv_sems, local_sems):
        _handshake(_exchange_peers())
        _exchange_copies(x_refs, out_refs, send_sems, recv_sems, local_sems)

    launch()
    return [ref[...] for ref in out_refs]


def norm_mod(x, w, sc, sh, name):
    b, s, d = x.shape
    tm = min(512, s)

    def body(x_ref, w_ref, sc_ref, sh_ref, h_ref):
        xv = x_ref[...]
        n = xv * _rms(xv)
        h_ref[...] = ((n * w_ref[...]) * (1.0 + sc_ref[...]) + sh_ref[...]).astype(BF16)

    return pl.pallas_call(
        body, name=name, grid=(b, s // tm),
        in_specs=[_row(tm, d), _full((1, d)), _bvec(d), _bvec(d)],
        out_specs=_row(tm, d), out_shape=SDS((b, s, d), BF16), compiler_params=_cparams(2))(x, w, sc, sh)


def ffn_up(h, wg_t, wu_t, name):
    b, s, d = h.shape
    f = wg_t.shape[0]
    tm, tn = min(512, s), f // 2

    def body(h_ref, wg_ref, wu_ref, s_ref, t_ref, a_ref):
        hv = h_ref[...]
        g = lax.dot_general(hv, wg_ref[...], NT_DIMS, preferred_element_type=F32)
        u = lax.dot_general(hv, wu_ref[...], NT_DIMS, preferred_element_type=F32)
        sg = _sigmoid(g)
        silu = g * sg
        s_ref[...] = silu.astype(s_ref.dtype)
        t_ref[...] = (u * (sg + silu * (1.0 - sg))).astype(t_ref.dtype)
        a_ref[...] = (silu * u).astype(BF16)

    hs = pl.BlockSpec((None, tm, d), lambda j, bb, i: (bb, i, 0))
    ws = pl.BlockSpec((tn, d), lambda j, bb, i: (j, 0))
    os_ = pl.BlockSpec((None, tm, tn), lambda j, bb, i: (bb, i, j))
    return pl.pallas_call(
        body, name=name, grid=(f // tn, b, s // tm),
        in_specs=[hs, ws, ws], out_specs=[os_, os_, os_],
        out_shape=[SDS((b, s, f), SAVED_ACT), SDS((b, s, f), SAVED_ACT), SDS((b, s, f), BF16)],
        compiler_params=_cparams(3))(h, wg_t, wu_t)


def _norm_mod_tile(xv, w_ref, sc_ref, sh_ref):
    return ((xv * _rms(xv) * w_ref[...]) * (1.0 + sc_ref[...]) + sh_ref[...]).astype(BF16)


def ffn_down(a, wd, x, gate, scale, name, above=None):
    b, s, f = a.shape
    d = wd.shape[1]
    tm = min(512, s)

    def body(a_ref, wd_ref, x_ref, g_ref, *rest):
        xn_ref, o_ref = rest[-3:-1] if above else rest
        o = jnp.dot(a_ref[...], wd_ref[...], preferred_element_type=F32)
        xn = x_ref[...] + (scale * g_ref[...]) * o
        xn_ref[...] = xn
        o_ref[...] = o.astype(BF16)
        if above:
            rest[-1][...] = _norm_mod_tile(xn, *rest[0:3])

    extra = above is not None
    return pl.pallas_call(
        body, name=name, grid=(b, s // tm),
        in_specs=[_row(tm, f), _full((f, d)), _row(tm, d), _bvec(d)] + ([_full((1, d)), _bvec(d), _bvec(d)] if extra else []),
        out_specs=[_row(tm, d), _row(tm, d)] + ([_row(tm, d)] if extra else []),
        out_shape=[SDS((b, s, d), F32), SDS((b, s, d), BF16)] + ([SDS((b, s, d), BF16)] if extra else []),
        compiler_params=_cparams(2))(a, wd, x, gate, *(above or ()))


def ffn_down_final(a, wd, x, gate, scale, w_final, tgt, name):
    b, s, f = a.shape
    d = wd.shape[1]
    tm = min(512, s)

    def body(a_ref, wd_ref, x_ref, g_ref, w_ref, t_ref, loss_ref, dx_ref, dw_ref, do_ref, dg_ref):
        @pl.when(_first_step())
        def _():
            loss_ref[...] = jnp.zeros_like(loss_ref)
            dw_ref[...] = jnp.zeros_like(dw_ref)

        @pl.when(pl.program_id(1) == 0)
        def _():
            dg_ref[...] = jnp.zeros_like(dg_ref)
        o = jnp.dot(a_ref[...], wd_ref[...], preferred_element_type=F32)
        sg = scale * g_ref[...]
        xv = x_ref[...] + sg * o
        r = _rms(xv)
        n = xv * r
        wv = w_ref[...]
        e = n * wv - t_ref[...]
        loss_ref[...] += jnp.sum(e * e) * (0.5 / d)
        dy = e * (1.0 / d)
        dw_ref[...] += jnp.sum(dy * n, axis=0, keepdims=True)
        dx = _rms_bwd(dy * wv, n, r)
        dx_ref[...] = dx
        do_ref[...] = (sg * dx).astype(BF16)
        dg_ref[...] += jnp.sum(scale * dx * o, axis=0, keepdims=True)

    return pl.pallas_call(
        body, name=name, grid=(b, s // tm),
        in_specs=[_row(tm, f), _full((f, d)), _row(tm, d), _bvec(d), _full((1, d)), _row(tm, d)],
        out_specs=[_full((1, LANES)), _row(tm, d), _full((1, d)), _row(tm, d), _bvec(d)],
        out_shape=[SDS((1, LANES), F32), SDS((b, s, d), F32), SDS((1, d), F32), SDS((b, s, d), BF16), SDS((b, 1, d), F32)],
        compiler_params=_cparams(2))(a, wd, x, gate, w_final, tgt)


def ffn_dact(do, wd, silu_g, u_dsilu, name):
    b, s, d = do.shape
    f = wd.shape[0]
    tm, tn = min(512, s), f // 2

    def body(do_ref, wd_ref, s_ref, t_ref, dg_ref, du_ref):
        da = lax.dot_general(do_ref[...], wd_ref[...], NT_DIMS, preferred_element_type=F32)
        dg_ref[...] = (da * t_ref[...].astype(F32)).astype(BF16)
        du_ref[...] = (da * s_ref[...].astype(F32)).astype(BF16)

    dos = pl.BlockSpec((None, tm, d), lambda j, bb, i: (bb, i, 0))
    ws = pl.BlockSpec((tn, d), lambda j, bb, i: (j, 0))
    es = pl.BlockSpec((None, tm, tn), lambda j, bb, i: (bb, i, j))
    return pl.pallas_call(
        body, name=name, grid=(f // tn, b, s // tm),
        in_specs=[dos, ws, es, es], out_specs=[es, es],
        out_shape=[SDS((b, s, f), BF16), SDS((b, s, f), BF16)], compiler_params=_cparams(3))(do, wd, silu_g, u_dsilu)


def mm_tn(a, bm, tma, tnb, name):
    b, s, ka = a.shape
    nb = bm.shape[2]
    tk = min(2048, s)
    nk = s // tk

    def body(a_ref, b_ref, o_ref, acc):
        first = (pl.program_id(2) == 0) & (pl.program_id(3) == 0)
        last = (pl.program_id(2) == b - 1) & (pl.program_id(3) == nk - 1)
        part = lax.dot_general(a_ref[...], b_ref[...], TN_DIMS, preferred_element_type=F32)

        @pl.when(first)
        def _():
            acc[...] = part

        @pl.when(jnp.logical_not(first))
        def _():
            acc[...] += part

        @pl.when(last)
        def _():
            o_ref[...] = acc[...].astype(BF16)

    return pl.pallas_call(
        body, name=name, grid=(ka // tma, nb // tnb, b, nk),
        in_specs=[pl.BlockSpec((None, tk, tma), lambda i, j, bb, k: (bb, k, i)),
                  pl.BlockSpec((None, tk, tnb), lambda i, j, bb, k: (bb, k, j))],
        out_specs=pl.BlockSpec((tma, tnb), lambda i, j, bb, k: (i, j)),
        out_shape=SDS((ka, nb), BF16), scratch_shapes=[pltpu.VMEM((tma, tnb), F32)],
        compiler_params=_cparams(4))(a, bm)


def mm_tn_blocks(a_blocks, bm, name):
    b, s, nb = bm.shape
    widths = [a.shape[2] for a in a_blocks]
    starts = [sum(widths[:k]) for k in range(len(widths))]
    tk = min(1024, s)
    nk = s // tk
    n = len(a_blocks)

    def body(*refs):
        a_refs, b_ref, o_ref, acc = refs[:n], refs[n], refs[n + 1], refs[n + 2]
        first = (pl.program_id(0) == 0) & (pl.program_id(1) == 0)
        last = (pl.program_id(0) == b - 1) & (pl.program_id(1) == nk - 1)

        @pl.when(first)
        def _():
            acc[...] = jnp.zeros_like(acc)
        bv = b_ref[...]
        for a_ref, st, wd in zip(a_refs, starts, widths):
            acc[st:st + wd, :] += lax.dot_general(a_ref[...], bv, TN_DIMS, preferred_element_type=F32)

        @pl.when(last)
        def _():
            o_ref[...] = acc[...].astype(BF16)

    return pl.pallas_call(
        body, name=name, grid=(b, nk),
        in_specs=[_row(tk, wd) for wd in widths] + [_row(tk, nb)],
        out_specs=_full((sum(widths), nb)), out_shape=SDS((sum(widths), nb), BF16),
        scratch_shapes=[pltpu.VMEM((sum(widths), nb), F32)], compiler_params=_cparams(2))(*a_blocks, bm)


def _gate_bwd_specs(tm, d, b, s):
    return ([_row(tm, d), _bvec(d)], [_row(tm, d), _bvec(d)], [SDS((b, s, d), BF16), SDS((b, 1, d), F32)])


def _gate_bwd_tile(dx, scale, o_ref, g_ref, do_ref, dg_ref):
    do_ref[...] = ((scale * g_ref[...]) * dx).astype(BF16)
    dg_ref[...] += jnp.sum(scale * dx * o_ref[...].astype(F32), axis=0, keepdims=True)


def n_in_bytes(arrs):
    return sum(a.size * a.dtype.itemsize for a in arrs)


def dh_norm_bwd(dys, wts, x, dxn, w, sc, name, below=None):
    b, s, d = x.shape
    tm = min(512 if n_in_bytes(wts) <= 8 * 1024 * 1024 else 256, s)
    n_in, n_w = len(dys), len(wts)
    extra_in, extra_out, extra_shape = _gate_bwd_specs(tm, d, b, s) if below else ([], [], [])
    starts = [sum(dy.shape[2] for dy in dys[:k]) for k in range(n_in)]

    def body(*refs):
        dy_refs, w_refs = refs[:n_in], refs[n_in:n_in + n_w]
        x_ref, dxn_ref, nw_ref, sc_ref = refs[n_in + n_w:n_in + n_w + 4]
        rest = refs[n_in + n_w + 4:]
        if below:
            o_ref, g_ref, dx_ref, dsc_ref, dsh_ref, dw_ref, do_ref, dg_ref = rest
        else:
            dx_ref, dsc_ref, dsh_ref, dw_ref = rest

        @pl.when(pl.program_id(1) == 0)
        def _():
            dsc_ref[...] = jnp.zeros_like(dsc_ref)
            dsh_ref[...] = jnp.zeros_like(dsh_ref)
            if below:
                dg_ref[...] = jnp.zeros_like(dg_ref)

        @pl.when(_first_step())
        def _():
            dw_ref[...] = jnp.zeros_like(dw_ref)

        def weight(k):
            return w_refs[k][...] if n_w == n_in else w_refs[0][starts[k]:starts[k] + dys[k].shape[2], :]

        dh = jnp.dot(dy_refs[0][...], weight(0), preferred_element_type=F32)
        for k in range(1, n_in):
            dh += jnp.dot(dy_refs[k][...], weight(k), preferred_element_type=F32)
        xv = x_ref[...]
        r = _rms(xv)
        n = xv * r
        nw = nw_ref[...]
        dsc_ref[...] += jnp.sum(dh * (n * nw), axis=0, keepdims=True)
        dsh_ref[...] += jnp.sum(dh, axis=0, keepdims=True)
        dhn = dh * (1.0 + sc_ref[...])
        dw_ref[...] += jnp.sum(dhn * n, axis=0, keepdims=True)
        dx = dxn_ref[...] + _rms_bwd(dhn * nw, n, r)
        dx_ref[...] = dx
        if below:
            _gate_bwd_tile(dx, below[2], o_ref, g_ref, do_ref, dg_ref)

    in_specs = [_row(tm, dy.shape[2]) for dy in dys] + [_full(wt.shape) for wt in wts]
    in_specs += [_row(tm, d), _row(tm, d), _full((1, d)), _bvec(d)] + extra_in
    return pl.pallas_call(
        body, name=name, grid=(b, s // tm), in_specs=in_specs,
        out_specs=[_row(tm, d), _bvec(d), _bvec(d), _full((1, d))] + extra_out,
        out_shape=[SDS((b, s, d), F32), SDS((b, 1, d), F32), SDS((b, 1, d), F32), SDS((1, d), F32)] + extra_shape,
        compiler_params=_cparams(2))(*dys, *wts, x, dxn, w, sc, *(below[:2] if below else ()))


def in_proj(h, win_t, name):
    b, s, d = h.shape
    tm = min(512, s)
    widths = (D_SSD, D_SSD + 2 * SSD_GROUPS * SSD_STATE, Q_LORA, KV_LORA, LANES)

    def body(h_ref, w_ref, *outs):
        p = lax.dot_general(h_ref[...], w_ref[...], NT_DIMS, preferred_element_type=F32)
        off = 0
        for o_ref, wd in zip(outs, widths):
            o_ref[...] = p[:, off:off + wd]
            off += wd

    return pl.pallas_call(
        body, name=name, grid=(b, s // tm),
        in_specs=[_row(tm, d), _full(win_t.shape)],
        out_specs=[_row(tm, wd) for wd in widths],
        out_shape=[SDS((b, s, wd), F32) for wd in widths], compiler_params=_cparams(2))(h, win_t)


def _halo_prev(ts, d):
    return pl.BlockSpec((None, 8, d), lambda b, i: (b, jnp.maximum(i * (ts // 8) - 1, 0), 0))


CONV_ROWS = 32


def _conv_head(head, u_ref, up_ref):
    head[0:8, :] = jnp.where(pl.program_id(1) > 0, up_ref[...], 0.0)
    head[8:8 + CONV_ROWS, :] = u_ref[0:CONV_ROWS, :]


def _conv_windows(u_ref, head, r0):
    if r0 == 0:
        return [head[5 + k:5 + k + CONV_ROWS, :] for k in range(4)]
    return [u_ref[r0 - 3 + k:r0 - 3 + k + CONV_ROWS, :] for k in range(4)]


def _fold8(t):
    acc = t[0:8, :]
    for r in range(8, CONV_ROWS, 8):
        acc += t[r:r + 8, :]
    return acc


def conv_fwd(u, cw, cb, name):
    b, s, dc = u.shape
    ts = min(512, s)
    widths = (D_SSD, SSD_GROUPS * SSD_STATE, SSD_GROUPS * SSD_STATE)

    def body(u_ref, up_ref, w_ref, b_ref, xs_ref, bm_ref, cm_ref, head):
        _conv_head(head, u_ref, up_ref)
        ws = [w_ref[k:k + 1, :] for k in range(4)]
        bias = b_ref[...]
        for r0 in range(0, ts, CONV_ROWS):
            taps = _conv_windows(u_ref, head, r0)
            v = bias + taps[0] * ws[0] + taps[1] * ws[1] + taps[2] * ws[2] + taps[3] * ws[3]
            y = v * _sigmoid(v)
            rs = slice(r0, r0 + CONV_ROWS)
            xs_ref[rs, :] = y[:, 0:D_SSD]
            bm_ref[rs, :] = y[:, D_SSD:D_SSD + 256]
            cm_ref[rs, :] = y[:, D_SSD + 256:D_SSD + 512]

    return pl.pallas_call(
        body, name=name, grid=(b, s // ts),
        in_specs=[_row(ts, dc), _halo_prev(ts, dc), _full((4, dc)), _full((1, dc))],
        out_specs=[_row(ts, wd) for wd in widths],
        out_shape=[SDS((b, s, wd), F32) for wd in widths],
        scratch_shapes=[pltpu.VMEM((8 + CONV_ROWS, dc), F32)], compiler_params=_cparams(2))(u, u, cw, cb)


def conv_bwd_a(dxs, dbm, dcm, u, cw, cb, name):
    b, s, dc = u.shape
    ts = min(512, s)

    def body(dxs_ref, dbm_ref, dcm_ref, u_ref, up_ref, w_ref, b_ref, dv_ref, dwb_ref, head):
        @pl.when(_first_step())
        def _():
            dwb_ref[...] = jnp.zeros_like(dwb_ref)
        _conv_head(head, u_ref, up_ref)
        ws = [w_ref[k:k + 1, :] for k in range(4)]
        bias = b_ref[...]
        for r0 in range(0, ts, CONV_ROWS):
            taps = _conv_windows(u_ref, head, r0)
            v = bias + taps[0] * ws[0] + taps[1] * ws[1] + taps[2] * ws[2] + taps[3] * ws[3]
            sg = _sigmoid(v)
            rs = slice(r0, r0 + CONV_ROWS)
            dy = jnp.concatenate([dxs_ref[rs, :], dbm_ref[rs, :], dcm_ref[rs, :]], axis=1)
            dv = dy * (sg * (1.0 + v * (1.0 - sg)))
            dv_ref[rs, :] = dv
            for k in range(4):
                dwb_ref[8 * k:8 * k + 8, :] += _fold8(dv * taps[k])
            dwb_ref[32:40, :] += _fold8(dv)

    return pl.pallas_call(
        body, name=name, grid=(b, s // ts),
        in_specs=[_row(ts, D_SSD), _row(ts, 256), _row(ts, 256), _row(ts, dc), _halo_prev(ts, dc),
                  _full((4, dc)), _full((1, dc))],
        out_specs=[_row(ts, dc), _full((40, dc))],
        out_shape=[SDS((b, s, dc), F32), SDS((40, dc), F32)],
        scratch_shapes=[pltpu.VMEM((8 + CONV_ROWS, dc), F32)], compiler_params=_cparams(2))(dxs, dbm, dcm, u, u, cw, cb)


def conv_grads_fold(x, name):
    c = x.shape[1]

    def body(x_ref, o_ref):
        o_ref[...] = jnp.zeros_like(o_ref)
        for k in range(5):
            o_ref[k:k + 1, :] = jnp.sum(x_ref[8 * k:8 * k + 8, :], axis=0, keepdims=True)

    return pl.pallas_call(body, name=name, out_shape=SDS((8, c), F32))(x)


def conv_bwd_b(dv, cw, name):
    b, s, dc = dv.shape
    ts = min(512, s)
    nt = s // ts

    def body(dv_ref, dn_ref, w_ref, du_ref, tail):
        tail[0:CONV_ROWS, :] = dv_ref[ts - CONV_ROWS:ts, :]
        tail[CONV_ROWS:CONV_ROWS + 8, :] = jnp.where(pl.program_id(1) < nt - 1, dn_ref[...], 0.0)
        ws = [w_ref[k:k + 1, :] for k in range(4)]
        for r0 in range(0, ts, CONV_ROWS):
            if r0 == ts - CONV_ROWS:
                win = [tail[3 - k:3 - k + CONV_ROWS, :] for k in range(4)]
            else:
                win = [dv_ref[r0 + 3 - k:r0 + 3 - k + CONV_ROWS, :] for k in range(4)]
            acc = win[0] * ws[0] + win[1] * ws[1] + win[2] * ws[2] + win[3] * ws[3]
            du_ref[r0:r0 + CONV_ROWS, :] = acc.astype(BF16)

    nxt = pl.BlockSpec((None, 8, dc), lambda bb, i: (bb, jnp.minimum((i + 1) * (ts // 8), s // 8 - 1), 0))
    return pl.pallas_call(
        body, name=name, grid=(b, nt),
        in_specs=[_row(ts, dc), nxt, _full((4, dc))],
        out_specs=_row(ts, dc), out_shape=SDS((b, s, dc), BF16),
        scratch_shapes=[pltpu.VMEM((CONV_ROWS + 8, dc), F32)], compiler_params=_cparams(2))(dv, dv, cw)


def _ssd_common(misc_ref, dtb_ref, alog_ref, e_ref):
    ln = CHUNK
    lane = lax.broadcasted_iota(I32, (ln, LANES), 1)
    lane1 = lax.broadcasted_iota(I32, (1, LANES), 1)
    pre = misc_ref[...] + dtb_ref[...]
    dt_s = jnp.where(lane < SSD_HEADS, _softplus(pre), 0.0)
    a_neg = jnp.where(lane1 < SSD_HEADS, -jnp.exp(alog_ref[...]), 0.0)
    ri = lax.broadcasted_iota(I32, (ln, ln), 0)
    ci = lax.broadcasted_iota(I32, (ln, ln), 1)
    tril = ci <= ri
    acum = jnp.dot(tril.astype(F32), dt_s * a_neg, preferred_element_type=F32, precision=HI)
    both_e = _dot_01(jnp.concatenate([dt_s, acum], axis=0), e_ref[...], 3)
    dt_e, acum_e = both_e[0:ln], both_e[ln:2 * ln]
    return dict(pre=pre, dt_s=dt_s, a_neg=a_neg, tril=tril, ri=ri, ci=ci, acum=acum, acum_t=acum.T,
                dt_e=dt_e, eac_e=jnp.exp(acum_e), del_e=jnp.exp(acum_e[ln - 1:ln, :] - acum_e))


def _dot_01(x, m01, terms):
    acc, rest = None, x
    for k in range(terms):
        part = rest.astype(BF16)
        if k + 1 < terms:
            rest = rest - part.astype(F32)
        d = jnp.dot(part, m01, preferred_element_type=F32)
        acc = d if acc is None else acc + d
    return acc


def _decay(cm, h):
    seg = cm["acum"][:, h:h + 1] - cm["acum_t"][h:h + 1, :]
    return jnp.exp(jnp.where(cm["tril"], seg, -jnp.inf))


def ssd_fwd(xs, bm, cm_, misc, z, dtb, alog, dskip_e, norm_w, e_mat, name):
    b, s, _ = xs.shape
    ln, nc = CHUNK, s // CHUNK
    gw = D_SSD // SSD_GROUPS
    hpg = SSD_HEADS // SSD_GROUPS

    def body(xs_ref, b_ref, c_ref, misc_ref, z_ref, dtb_ref, alog_ref, dsk_ref, nw_ref, e_ref,
             ys_ref, y_ref, p_ref, st, yd):
        @pl.when(pl.program_id(1) == 0)
        def _():
            st[...] = jnp.zeros_like(st)
        cm = _ssd_common(misc_ref, dtb_ref, alog_ref, e_ref)
        xsv = xs_ref[...]
        xdt = xsv * cm["dt_e"]
        xdt_b = xdt.astype(BF16)
        xd_b = (xdt * cm["del_e"]).astype(BF16)
        gam_e = cm["eac_e"][ln - 1:ln, :]
        p_ref[...] = st[...]
        groups = [slice(gw * g, gw * (g + 1)) for g in range(SSD_GROUPS)]
        heads = [slice(SSD_HEAD_DIM * h, SSD_HEAD_DIM * (h + 1)) for h in range(SSD_HEADS)]
        bgs = [b_ref[:, SSD_STATE * g:SSD_STATE * (g + 1)].astype(BF16) for g in range(SSD_GROUPS)]
        cgs = [c_ref[:, SSD_STATE * g:SSD_STATE * (g + 1)].astype(BF16) for g in range(SSD_GROUPS)]
        cbs = [lax.dot_general(cg, bg, NT_DIMS, preferred_element_type=F32) for cg, bg in zip(cgs, bgs)]
        sts = [st[:, gs] for gs in groups]
        yoff = [jnp.dot(cg, st_g.astype(BF16), preferred_element_type=F32) * cm["eac_e"][:, gs]
                for cg, st_g, gs in zip(cgs, sts, groups)]
        news = [lax.dot_general(bg, xd_b[:, gs], TN_DIMS, preferred_element_type=F32) for bg, gs in zip(bgs, groups)]
        for gs, st_g, new in zip(groups, sts, news):
            st[:, gs] = st_g * gam_e[:, gs] + new
        ms = [(cbs[h // hpg] * _decay(cm, h)).astype(BF16) for h in range(SSD_HEADS)]
        for h, hs in enumerate(heads):
            yd[:, hs] = jnp.dot(ms[h], xdt_b[:, hs], preferred_element_type=F32)
        y = yd[...] + jnp.concatenate(yoff, axis=1) + dsk_ref[...] * xsv
        y_ref[...] = y
        zz = z_ref[...]
        yg = y * (zz * _sigmoid(zz))
        outs = []
        for g in range(SSD_GROUPS):
            ygg = yg[:, gw * g:gw * (g + 1)]
            outs.append(ygg * _rms(ygg) * nw_ref[:, gw * g:gw * (g + 1)])
        ys_ref[...] = jnp.concatenate(outs, axis=1).astype(BF16)

    row = lambda d: pl.BlockSpec((None, ln, d), lambda bb, c: (bb, c, 0))
    return pl.pallas_call(
        body, name=name, grid=(b, nc),
        in_specs=[row(D_SSD), row(256), row(256), row(LANES), row(D_SSD), _full((1, LANES)), _full((1, LANES)),
                  _full((1, D_SSD)), _full((1, D_SSD)), _full((LANES, D_SSD))],
        out_specs=[row(D_SSD), row(D_SSD), pl.BlockSpec((None, None, SSD_STATE, D_SSD), lambda bb, c: (bb, c, 0, 0))],
        out_shape=[SDS((b, s, D_SSD), BF16), SDS((b, s, D_SSD), F32), SDS((b, nc, SSD_STATE, D_SSD), F32)],
        scratch_shapes=[pltpu.VMEM((SSD_STATE, D_SSD), F32), pltpu.VMEM((ln, D_SSD), F32)],
        compiler_params=_cparams(2))(xs, bm, cm_, misc, z, dtb, alog, dskip_e, norm_w, e_mat)


def ssd_bwd(dys, y, z, xs, bm, cm_, misc, prev, dtb, alog, dskip_e, norm_w, e_mat, et_mat, name):
    b, s, _ = xs.shape
    ln, nc = CHUNK, s // CHUNK
    gw = D_SSD // SSD_GROUPS
    hpg = SSD_HEADS // SSD_GROUPS

    def body(dys_ref, y_ref, z_ref, xs_ref, b_ref, c_ref, misc_ref, p_ref, dtb_ref, alog_ref, dsk_ref, nw_ref,
             e_ref, et_ref, dxs_ref, db_ref, dc_ref, dz_ref, ddt_ref, dnw_ref, ddsk_ref, ddtb_ref, dalog_ref,
             dst, dxd, dac_t):
        @pl.when(_first_step())
        def _():
            for r_ in (dnw_ref, ddsk_ref, ddtb_ref, dalog_ref):
                r_[...] = jnp.zeros_like(r_)

        @pl.when(pl.program_id(1) == 0)
        def _():
            dst[...] = jnp.zeros_like(dst)

        cm = _ssd_common(misc_ref, dtb_ref, alog_ref, e_ref)
        et = et_ref[...]
        squeeze = lambda t: _dot_01(t, et, 2)
        lane = lax.broadcasted_iota(I32, (ln, LANES), 1)
        sub = lax.broadcasted_iota(I32, (LANES, ln), 0)
        xsv = xs_ref[...]
        xdt = xsv * cm["dt_e"]
        xdt_b = xdt.astype(BF16)
        xd_b = (xdt * cm["del_e"]).astype(BF16)
        eac_e = cm["eac_e"]
        gam_e = eac_e[ln - 1:ln, :]

        yv, zz, dyo = y_ref[...], z_ref[...], dys_ref[...]
        sz = _sigmoid(zz)
        silu_z = zz * sz
        yg = yv * silu_z
        dyg, dnw = [], []
        for g in range(SSD_GROUPS):
            gs = slice(gw * g, gw * (g + 1))
            ygg = yg[:, gs]
            r = _rms(ygg)
            n = ygg * r
            dnw.append(jnp.sum(dyo[:, gs] * n, axis=0, keepdims=True))
            dyg.append(_rms_bwd(dyo[:, gs] * nw_ref[:, gs], n, r))
        dyg = jnp.concatenate(dyg, axis=1)
        dnw_ref[...] += jnp.concatenate(dnw, axis=1)
        dz_ref[...] = (dyg * yv * (sz * (1.0 + zz * (1.0 - sz)))).astype(BF16)
        dy = dyg * silu_z
        ddsk_ref[...] += jnp.sum(dy * xsv, axis=0, keepdims=True)
        dy_b = dy.astype(BF16)

        dacum = jnp.zeros((ln, LANES), F32)
        dac_t[...] = jnp.zeros_like(dac_t)
        w1, dgam = [], []
        for g in range(SSD_GROUPS):
            gs = slice(gw * g, gw * (g + 1))
            ss = slice(SSD_STATE * g, SSD_STATE * (g + 1))
            bg = b_ref[:, ss].astype(BF16)
            cg = c_ref[:, ss].astype(BF16)
            cb = lax.dot_general(cg, bg, NT_DIMS, preferred_element_type=F32)
            pt = p_ref[:, gs]
            pt_b = pt.astype(BF16)
            dst_g = dst[:, gs]
            dst_b = dst_g.astype(BF16)
            edy = (dy[:, gs] * eac_e[:, gs]).astype(BF16)
            dcg = lax.dot_general(edy, pt_b, NT_DIMS, preferred_element_type=F32)
            dpt = lax.dot_general(cg, edy, TN_DIMS, preferred_element_type=F32)
            yoff = jnp.dot(cg, pt_b, preferred_element_type=F32) * eac_e[:, gs]
            dxd_g = jnp.dot(bg, dst_b, preferred_element_type=F32)
            dbg = lax.dot_general(xd_b[:, gs], dst_b, NT_DIMS, preferred_element_type=F32)
            ddel = dxd_g * xdt[:, gs] * cm["del_e"][:, gs]
            w1.append(dy[:, gs] * yoff - ddel)
            dgam.append(jnp.sum(ddel, axis=0, keepdims=True) + jnp.sum(dst_g * pt, axis=0, keepdims=True) * gam_e[:, gs])
            dxd[:, gs] = dxd_g * cm["del_e"][:, gs]
            dst[:, gs] = dst_g * gam_e[:, gs] + dpt
            dcb = jnp.zeros((ln, ln), F32)
            for j in range(hpg):
                h = hpg * g + j
                hs = slice(SSD_HEAD_DIM * h, SSD_HEAD_DIM * (h + 1))
                lam = _decay(cm, h)
                m = cb * lam
                dm = lax.dot_general(dy_b[:, hs], xdt_b[:, hs], NT_DIMS, preferred_element_type=F32)
                dxd[:, hs] += lax.dot_general(m.astype(BF16), dy_b[:, hs], TN_DIMS, preferred_element_type=F32)
                dcb += dm * lam
                wl = dm * m
                dacum += jnp.where(lane == h, jnp.sum(wl, axis=1, keepdims=True), 0.0)
                dac_t[...] -= jnp.where(sub == h, jnp.sum(wl, axis=0, keepdims=True), 0.0)
            dcb_b = dcb.astype(BF16)
            dc_ref[:, ss] = dcg + jnp.dot(dcb_b, bg, preferred_element_type=F32)
            db_ref[:, ss] = dbg + lax.dot_general(dcb_b, cg, TN_DIMS, preferred_element_type=F32)

        dxdt = dxd[...]
        dxs_ref[...] = dy * dsk_ref[...] + dxdt * cm["dt_e"]
        dacum += squeeze(jnp.concatenate(w1, axis=1)) + dac_t[...].T
        dlast = squeeze(jnp.broadcast_to(jnp.concatenate(dgam, axis=1), (8, D_SSD)))[0:1, :]
        dacum += jnp.where(lax.broadcasted_iota(I32, (ln, LANES), 0) == ln - 1, dlast, 0.0)
        triu = (cm["ci"] >= cm["ri"]).astype(F32)
        da = jnp.dot(triu, dacum, preferred_element_type=F32, precision=HI)
        ddt = da * cm["a_neg"] + squeeze(dxdt * xsv)
        dalog_ref[...] += jnp.sum(da * cm["dt_s"], axis=0, keepdims=True) * cm["a_neg"]
        ddt_raw = jnp.where(lane < SSD_HEADS, ddt * _sigmoid(cm["pre"]), 0.0)
        ddt_ref[...] = ddt_raw
        ddtb_ref[...] += jnp.sum(ddt_raw, axis=0, keepdims=True)

    row = lambda d: pl.BlockSpec((None, ln, d), lambda bb, c: (bb, nc - 1 - c, 0))
    return pl.pallas_call(
        body, name=name, grid=(b, nc),
        in_specs=[row(D_SSD), row(D_SSD), row(D_SSD), row(D_SSD), row(256), row(256), row(LANES),
                  pl.BlockSpec((None, None, SSD_STATE, D_SSD), lambda bb, c: (bb, nc - 1 - c, 0, 0)),
                  _full((1, LANES)), _full((1, LANES)), _full((1, D_SSD)), _full((1, D_SSD)),
                  _full((LANES, D_SSD)), _full((D_SSD, LANES))],
        out_specs=[row(D_SSD), row(256), row(256), row(D_SSD), row(LANES),
                   _full((1, D_SSD)), _full((1, D_SSD)), _full((1, LANES)), _full((1, LANES))],
        out_shape=[SDS((b, s, D_SSD), F32), SDS((b, s, 256), F32), SDS((b, s, 256), F32), SDS((b, s, D_SSD), BF16),
                   SDS((b, s, LANES), F32), SDS((1, D_SSD), F32), SDS((1, D_SSD), F32), SDS((1, LANES), F32),
                   SDS((1, LANES), F32)],
        scratch_shapes=[pltpu.VMEM((SSD_STATE, D_SSD), F32), pltpu.VMEM((ln, D_SSD), F32), pltpu.VMEM((LANES, ln), F32)],
        compiler_params=_cparams(2))(dys, y, z, xs, bm, cm_, misc, prev, dtb, alog, dskip_e, norm_w, e_mat, et_mat)


def _rope(xv, cc, sp, sm):
    n = xv.shape[1]
    return xv * cc + pltpu.roll(xv, 16, 1) * sp + pltpu.roll(xv, n - 16, 1) * sm


def _rope_bwd(dy, cc, sp, sm):
    n = dy.shape[1]
    return dy * cc + pltpu.roll(dy * sp, n - 16, 1) + pltpu.roll(dy * sm, 16, 1)


def _tile8(t):
    return jnp.concatenate([t] * MLA_HEADS, axis=1)


def qkv_fwd(cq, ckv, misc, cc, sp, sm, qnw, kvnw, wuq_t, wukv_t, place, name):
    b, s, _ = cq.shape
    tm = _att_tile(s)
    hd = MLA_HEADS * HEAD_PAD

    def body(cq_ref, ckv_ref, misc_ref, cc_ref, sp_ref, sm_ref, qnw_ref, kvnw_ref, wq_ref, wkv_ref, pl_ref,
             q_ref, k_ref, v_ref, vt_ref, qn_ref, kvn_ref):
        cqv, ckvv = cq_ref[...], ckv_ref[...]
        qn = (cqv * _rms(cqv) * qnw_ref[...]).astype(BF16)
        kvn = (ckvv * _rms(ckvv) * kvnw_ref[...]).astype(BF16)
        qn_ref[...] = qn
        kvn_ref[...] = kvn
        cc1, sp1, sm1 = cc_ref[...], sp_ref[...], sm_ref[...]
        q = lax.dot_general(qn, wq_ref[...], NT_DIMS, preferred_element_type=F32)
        q_ref[...] = _rope(q, _tile8(cc1), _tile8(sp1), _tile8(sm1)).astype(BF16)
        kv = lax.dot_general(kvn, wkv_ref[...], NT_DIMS, preferred_element_type=F32)
        kr = jnp.dot(misc_ref[...], pl_ref[...], preferred_element_type=F32, precision=HI)
        kr = _rope(kr, cc1, sp1, sm1)
        k_ref[...] = (kv[:, 0:hd] + _tile8(kr)).astype(BF16)
        v_ref[...] = kv[:, hd:2 * hd].astype(BF16)
        for h in range(MLA_HEADS):
            vt_ref[h] = kv[:, hd + HEAD_PAD * h:hd + HEAD_PAD * (h + 1)].T.astype(BF16)

    return pl.pallas_call(
        body, name=name, grid=(b, s // tm),
        in_specs=[_row(tm, Q_LORA), _row(tm, KV_LORA), _row(tm, LANES), _row(tm, LANES), _row(tm, LANES), _row(tm, LANES),
                  _full((1, Q_LORA)), _full((1, KV_LORA)), _full(wuq_t.shape), _full(wukv_t.shape), _full((LANES, LANES))],
        out_specs=[_row(tm, hd), _row(tm, hd), _row(tm, hd),
                   pl.BlockSpec((None, MLA_HEADS, None, HEAD_PAD, tm), lambda bb, i: (bb, 0, i, 0, 0)),
                   _row(tm, Q_LORA), _row(tm, KV_LORA)],
        out_shape=[SDS((b, s, hd), BF16)] * 3 + [SDS((b, MLA_HEADS, s // tm, HEAD_PAD, tm), BF16),
                                                 SDS((b, s, Q_LORA), BF16), SDS((b, s, KV_LORA), BF16)],
        compiler_params=_cparams(2))(cq, ckv, misc, cc, sp, sm, qnw, kvnw, wuq_t, wukv_t, place)


def qkv_bwd(dq, dk, dv, ddt, cq, ckv, cc, sp, sm, qnw, kvnw, wuq_t, wukv_t, place_t, name):
    b, s, _ = cq.shape
    tm = min(512, s)
    hd = MLA_HEADS * HEAD_PAD

    def body(dq_ref, dk_ref, dv_ref, ddt_ref, cq_ref, ckv_ref, cc_ref, sp_ref, sm_ref, qnw_ref, kvnw_ref,
             wq_ref, wkv_ref, plt_ref, dcq_ref, dckv_ref, dmisc_ref, dqp_ref, dkv_ref, dqnw_ref, dkvnw_ref):
        @pl.when(_first_step())
        def _():
            dqnw_ref[...] = jnp.zeros_like(dqnw_ref)
            dkvnw_ref[...] = jnp.zeros_like(dkvnw_ref)
        cc1, sp1, sm1 = cc_ref[...], sp_ref[...], sm_ref[...]
        dqp = _rope_bwd(dq_ref[...].astype(F32), _tile8(cc1), _tile8(sp1), _tile8(sm1)).astype(BF16)
        dqp_ref[...] = dqp
        dkv_b = jnp.concatenate([dk_ref[...], dv_ref[...]], axis=1)
        dkf = dk_ref[...].astype(F32)
        dkv_ref[...] = dkv_b
        dkr = dkf[:, 0:HEAD_PAD]
        for h in range(1, MLA_HEADS):
            dkr += dkf[:, HEAD_PAD * h:HEAD_PAD * (h + 1)]
        dkr = _rope_bwd(dkr, cc1, sp1, sm1)
        dmisc_ref[...] = (jnp.dot(dkr, plt_ref[...], preferred_element_type=F32, precision=HI) + ddt_ref[...]).astype(BF16)

        def norm_bwd(dn_w, xv, w_ref, dw_ref, dx_ref):
            r = _rms(xv)
            n = xv * r
            dw_ref[...] += jnp.sum(dn_w * n, axis=0, keepdims=True)
            dx_ref[...] = _rms_bwd(dn_w * w_ref[...], n, r).astype(BF16)

        norm_bwd(jnp.dot(dqp, wq_ref[...], preferred_element_type=F32), cq_ref[...], qnw_ref, dqnw_ref, dcq_ref)
        norm_bwd(jnp.dot(dkv_b, wkv_ref[...], preferred_element_type=F32), ckv_ref[...], kvnw_ref, dkvnw_ref, dckv_ref)

    return pl.pallas_call(
        body, name=name, grid=(b, s // tm),
        in_specs=[_row(tm, hd), _row(tm, hd), _row(tm, hd), _row(tm, LANES), _row(tm, Q_LORA), _row(tm, KV_LORA),
                  _row(tm, LANES), _row(tm, LANES), _row(tm, LANES), _full((1, Q_LORA)), _full((1, KV_LORA)),
                  _full(wuq_t.shape), _full(wukv_t.shape), _full((LANES, LANES))],
        out_specs=[_row(tm, Q_LORA), _row(tm, KV_LORA), _row(tm, LANES), _row(tm, hd), _row(tm, 2 * hd),
                   _full((1, Q_LORA)), _full((1, KV_LORA))],
        out_shape=[SDS((b, s, Q_LORA), BF16), SDS((b, s, KV_LORA), BF16), SDS((b, s, LANES), BF16),
                   SDS((b, s, hd), BF16), SDS((b, s, 2 * hd), BF16), SDS((1, Q_LORA), F32), SDS((1, KV_LORA), F32)],
        compiler_params=_cparams(2))(dq, dk, dv, ddt, cq, ckv, cc, sp, sm, qnw, kvnw, wuq_t, wukv_t, place_t)


ATT_SCALE = 1.0 / math.sqrt(QK_DIM)
LOG2E = math.log2(math.e)
ATT_SCALE_LOG2E = ATT_SCALE * LOG2E


ATT_HEADS_PER_STEP = 4
ATT_HEADS_PER_STEP_BWD = 2


def _att_tile(s):
    return min(512, s)


def flash_fwd(q, k, vt, name):
    b, s, hd = q.shape
    t = _att_tile(s)
    nb = s // t
    th = t // 2

    hps = ATT_HEADS_PER_STEP
    hw = hps * HEAD_PAD

    def body(q_ref, k_ref, vt_ref, o_ref, lse_ref, m_s, l_s, acc):
        i = pl.program_id(2)
        m_s[...] = jnp.full_like(m_s, -jnp.inf)
        l_s[...] = jnp.zeros_like(l_s)
        acc[...] = jnp.zeros_like(acc)

        def update(j, diagonal):
            chains = [(hh, half) for hh in range(hps) for half in range(2)]
            lanes = lambda hh: slice(HEAD_PAD * hh, HEAD_PAD * (hh + 1))
            cols = lambda half: slice(th * half, th * (half + 1))
            sts = {}
            nkeys = lambda half: th if diagonal and half == 0 else t
            for hh, half in chains:
                kr = pl.ds(pl.multiple_of(j * t, t), nkeys(half))
                st = lax.dot_general(k_ref[kr, lanes(hh)], q_ref[cols(half), lanes(hh)], NT_DIMS,
                                     preferred_element_type=F32)
                if diagonal:
                    row = lax.broadcasted_iota(I32, (nkeys(half), th), 0)
                    col = lax.broadcasted_iota(I32, (nkeys(half), th), 1) + th * half
                    st = jnp.where(row <= col, st, -jnp.inf)
                sts[hh, half] = st
            pts, alphas = {}, {}
            for hh, half in chains:
                st, cs = sts[hh, half], cols(half)
                m_prev = m_s[hh, :, cs]
                m_new = jnp.maximum(m_prev, jnp.max(st, axis=0, keepdims=True))
                alpha = jnp.exp2((m_prev - m_new) * ATT_SCALE_LOG2E)
                pt = jnp.exp2((st - m_new) * ATT_SCALE_LOG2E)
                l_s[hh, :, cs] = alpha * l_s[hh, :, cs] + jnp.sum(pt, axis=0, keepdims=True)
                m_s[hh, :, cs] = m_new
                pts[hh, half], alphas[hh, half] = pt.astype(BF16), alpha
            for hh, half in chains:
                cs = cols(half)
                acc[hh, :, cs] = alphas[hh, half] * acc[hh, :, cs] + jnp.dot(
                    vt_ref[hh, j, :, 0:nkeys(half)], pts[hh, half], preferred_element_type=F32)

        def step(j, carry):
            update(j, False)
            return carry

        lax.fori_loop(0, i, step, 0)
        update(i, True)
        for hh in range(hps):
            o_ref[:, HEAD_PAD * hh:HEAD_PAD * (hh + 1)] = (acc[hh] / l_s[hh]).T
            lse_ref[hh] = m_s[hh] * ATT_SCALE + jnp.log(l_s[hh])

    qs = pl.BlockSpec((None, t, hw), lambda bb, h, i: (bb, i, h))
    ks = pl.BlockSpec((None, s, hw), lambda bb, h, i: (bb, 0, h))
    vs = pl.BlockSpec((None, hps, nb, HEAD_PAD, t), lambda bb, h, i: (bb, h, 0, 0, 0))
    ls = pl.BlockSpec((None, hps, None, 1, t), lambda bb, h, i: (bb, h, i, 0, 0))
    return pl.pallas_call(
        body, name=name, grid=(b, MLA_HEADS // hps, nb),
        in_specs=[qs, ks, vs], out_specs=[qs, ls],
        out_shape=[SDS((b, s, hd), F32), SDS((b, MLA_HEADS, nb, 1, t), F32)],
        scratch_shapes=[pltpu.VMEM((hps, 1, t), F32), pltpu.VMEM((hps, 1, t), F32), pltpu.VMEM((hps, HEAD_PAD, t), F32)],
        compiler_params=_cparams(3))(q, k, vt)


def flash_bwd(q, k, v, do, lse, dlt, name):
    b, s, hd = q.shape
    t = _att_tile(s)
    nb = s // t
    th = t // 2
    lse_r = lse
    dlt_r = dlt.reshape(b, MLA_HEADS, nb, 1, t)

    hps = ATT_HEADS_PER_STEP_BWD
    hw = hps * HEAD_PAD

    def body(q_ref, k_ref, v_ref, do_ref, lse_ref, dlt_ref, dq_ref, dk_ref, dv_ref, dq_s, dk_s, dv_s):
        dq_s[...] = jnp.zeros_like(dq_s)
        dk_s[...] = jnp.zeros_like(dk_s)
        dv_s[...] = jnp.zeros_like(dv_s)

        def tile(j, i, diagonal):
            chains = [(hh, half) for hh in range(hps) for half in range(2)]
            lanes = lambda hh: slice(HEAD_PAD * hh, HEAD_PAD * (hh + 1))
            keys = lambda half: pl.ds(pl.multiple_of(j * t + th * half, th), th)
            q0 = lambda half: th if diagonal and half == 1 else 0
            qsel = lambda half: pl.ds(pl.multiple_of(i * t + q0(half), th), t - q0(half))
            sts, dpts = {}, {}
            for hh, half in chains:
                ls_, ks, qs, nq = lanes(hh), keys(half), qsel(half), t - q0(half)
                st = lax.dot_general(k_ref[ks, ls_], q_ref[qs, ls_], NT_DIMS, preferred_element_type=F32)
                if diagonal:
                    row = lax.broadcasted_iota(I32, (th, nq), 0) + th * half
                    col = lax.broadcasted_iota(I32, (th, nq), 1) + q0(half)
                    st = jnp.where(row <= col, st, -jnp.inf)
                sts[hh, half] = st
                dpts[hh, half] = lax.dot_general(v_ref[ks, ls_], do_ref[qs, ls_], NT_DIMS, preferred_element_type=F32)
            pts, dsts = {}, {}
            for hh, half in chains:
                qcols = slice(q0(half), t)
                pt = jnp.exp2(sts[hh, half] * ATT_SCALE_LOG2E - lse_ref[hh, i][:, qcols] * LOG2E)
                pts[hh, half] = pt.astype(BF16)
                dsts[hh, half] = (pt * (dpts[hh, half] - dlt_ref[hh, i][:, qcols])).astype(BF16)
            for hh, half in chains:
                ls_, ks, qs = lanes(hh), keys(half), qsel(half)
                dv_s[ks, ls_] += jnp.dot(pts[hh, half], do_ref[qs, ls_], preferred_element_type=F32)
                dk_s[ks, ls_] += jnp.dot(dsts[hh, half], q_ref[qs, ls_], preferred_element_type=F32)
                dq_s[qs, ls_] += lax.dot_general(dsts[hh, half], k_ref[ks, ls_], TN_DIMS, preferred_element_type=F32)

        def key_tile(j, carry):
            tile(j, j, True)

            def query_tile(i, c2):
                tile(j, i, False)
                return c2

            lax.fori_loop(j + 1, nb, query_tile, 0)
            return carry

        lax.fori_loop(0, nb, key_tile, 0)
        dq_ref[...] = (dq_s[...] * ATT_SCALE).astype(BF16)
        dk_ref[...] = (dk_s[...] * ATT_SCALE).astype(BF16)
        dv_ref[...] = dv_s[...].astype(BF16)

    hs = pl.BlockSpec((None, s, hw), lambda bb, h: (bb, 0, h))
    ls = pl.BlockSpec((None, hps, nb, 1, t), lambda bb, h: (bb, h, 0, 0, 0))
    return pl.pallas_call(
        body, name=name, grid=(b, MLA_HEADS // hps),
        in_specs=[hs, hs, hs, hs, ls, ls], out_specs=[hs, hs, hs],
        out_shape=[SDS((b, s, hd), BF16)] * 3, scratch_shapes=[pltpu.VMEM((s, hw), F32)] * 3,
        compiler_params=_cparams(2))(q, k, v, do, lse_r, dlt_r)


def out_proj(ys, attn, mnw, wo, x, gate, above, name):
    b, s, d = x.shape
    tm = min(512, s)

    def body(ys_ref, at_ref, mnw_ref, wo_ref, x_ref, g_ref, nw_ref, sc_ref, sh_ref, xn_ref, o_ref, ym_ref, h_ref):
        av = at_ref[...]
        ym = (av * _rms(av) * mnw_ref[...]).astype(BF16)
        ym_ref[...] = ym
        o = jnp.dot(ys_ref[...], wo_ref[0:D_SSD, :], preferred_element_type=F32)
        o += jnp.dot(ym, wo_ref[D_SSD:2 * D_SSD, :], preferred_element_type=F32)
        xn = x_ref[...] + g_ref[...] * o
        xn_ref[...] = xn
        o_ref[...] = o.astype(BF16)
        h_ref[...] = _norm_mod_tile(xn, nw_ref, sc_ref, sh_ref)

    return pl.pallas_call(
        body, name=name, grid=(b, s // tm),
        in_specs=[_row(tm, D_SSD), _row(tm, D_SSD), _full((1, D_SSD)), _full(wo.shape), _row(tm, d), _bvec(d),
                  _full((1, d)), _bvec(d), _bvec(d)],
        out_specs=[_row(tm, d), _row(tm, d), _row(tm, D_SSD), _row(tm, d)],
        out_shape=[SDS((b, s, d), F32), SDS((b, s, d), BF16), SDS((b, s, D_SSD), BF16), SDS((b, s, d), BF16)],
        compiler_params=_cparams(2))(ys, attn, mnw, wo, x, gate, *above)


def out_proj_bwd(dout, attn, mnw, wo, name):
    b, s, d = dout.shape
    tm = min(512, s)

    def body(do_ref, at_ref, mnw_ref, wo_ref, dys_ref, dat_ref, dlt_ref, dw_ref):
        lane = lax.broadcasted_iota(I32, (tm, LANES), 1)
        @pl.when(_first_step())
        def _():
            dw_ref[...] = jnp.zeros_like(dw_ref)
        dov = do_ref[...]
        dys_ref[...] = lax.dot_general(dov, wo_ref[0:D_SSD, :], NT_DIMS, preferred_element_type=F32)
        dym = lax.dot_general(dov, wo_ref[D_SSD:2 * D_SSD, :], NT_DIMS, preferred_element_type=F32)
        av = at_ref[...]
        r = _rms(av)
        n = av * r
        dw_ref[...] += jnp.sum(dym * n, axis=0, keepdims=True)
        dat = _rms_bwd(dym * mnw_ref[...], n, r)
        dat_ref[...] = dat.astype(BF16)
        prod = dat * av
        cols = jnp.zeros((tm, LANES), F32)
        for h in range(MLA_HEADS):
            cols += jnp.where(lane == h, jnp.sum(prod[:, HEAD_PAD * h:HEAD_PAD * (h + 1)], axis=1, keepdims=True), 0.0)
        dlt_ref[...] = cols.T[0:MLA_HEADS, :]

    return pl.pallas_call(
        body, name=name, grid=(b, s // tm),
        in_specs=[_row(tm, d), _row(tm, D_SSD), _full((1, D_SSD)), _full(wo.shape)],
        out_specs=[_row(tm, D_SSD), _row(tm, D_SSD),
                   pl.BlockSpec((None, MLA_HEADS, tm), lambda bb, i: (bb, 0, i)), _full((1, D_SSD))],
        out_shape=[SDS((b, s, D_SSD), F32), SDS((b, s, D_SSD), BF16), SDS((b, MLA_HEADS, s), F32),
                   SDS((1, D_SSD), F32)],
        compiler_params=_cparams(2))(dout, attn, mnw, wo)


def adaln_fwd(c_all, w_ada, b_ada, name):
    nb, d = c_all.shape
    n = w_ada.shape[1]

    def body(c_ref, w_ref, b_ref, m_ref, ca_ref):
        cv = c_ref[...]
        ca = (cv * _sigmoid(cv)).astype(BF16)
        ca_ref[...] = ca
        m_ref[...] = jnp.dot(ca, w_ref[...].astype(BF16), preferred_element_type=F32) + b_ref[...]

    return pl.pallas_call(
        body, name=name, out_shape=[SDS((nb, n), F32), SDS((nb, d), BF16)],
        compiler_params=pltpu.CompilerParams(vmem_limit_bytes=VMEM_LIMIT))(c_all, w_ada, b_ada)


def adaln_bwd(c_act, dmod_cols, name):
    d, n = c_act.shape[1], dmod_cols.shape[1]

    def body(c_ref, dm_ref, gw_ref):
        gw_ref[...] = lax.dot_general(c_ref[...], dm_ref[...].astype(BF16), TN_DIMS, preferred_element_type=F32)

    return pl.pallas_call(
        body, name=name, out_shape=SDS((d, n), F32),
        compiler_params=pltpu.CompilerParams(vmem_limit_bytes=VMEM_LIMIT))(c_act, dmod_cols)


def sum_rows(x, name):
    def body(x_ref, o_ref):
        o_ref[...] = jnp.sum(x_ref[...], axis=0, keepdims=True)
    return pl.pallas_call(body, name=name, out_shape=SDS((1, x.shape[1]), F32))(x)


def squeeze_heads(x, et_mat, name):
    def body(x_ref, et_ref, o_ref):
        xv = jnp.broadcast_to(x_ref[...], (8, x.shape[1]))
        o_ref[...] = _dot_01(xv, et_ref[...], 3)[0:1, :]
    return pl.pallas_call(body, name=name, out_shape=SDS((1, LANES), F32))(x, et_mat)


def sum_blocks(x, name):
    n, r, c = x.shape
    tr = next(cand for cand in (256, 128, 64, 32, 16, 8) if r % cand == 0)

    def body(x_ref, o_ref):
        acc = x_ref[0].astype(F32)
        for k in range(1, n):
            acc += x_ref[k].astype(F32)
        o_ref[...] = acc

    return pl.pallas_call(
        body, name=name, grid=(r // tr,), in_specs=[pl.BlockSpec((n, tr, c), lambda i: (0, i, 0))],
        out_specs=pl.BlockSpec((tr, c), lambda i: (i, 0)), out_shape=SDS((r, c), F32),
        compiler_params=_cparams(1))(x)


def _adam_math(w, g, m, v):
    m = ADAM_B1 * m + (1.0 - ADAM_B1) * g
    v = ADAM_B2 * v + (1.0 - ADAM_B2) * (g * g)
    m_hat = m / (1.0 - ADAM_B1 ** ADAM_STEP)
    v_hat = v / (1.0 - ADAM_B2 ** ADAM_STEP)
    return -ADAM_LR * (m_hat / (jnp.sqrt(v_hat) + ADAM_EPS) + ADAM_WD * w), m, v


def adamw(w, g, m, v, name):
    r, c = w.shape[-2:]
    tr = r
    for cand in (512, 256, 128, 64, 32, 16, 8):
        if r % cand == 0 and cand * c * 4 <= 2 * 1024 * 1024:
            tr = cand
            break

    def body(w_ref, g_ref, m_ref, v_ref, d_ref, mo_ref, vo_ref):
        d_ref[...], mo_ref[...], vo_ref[...] = _adam_math(w_ref[...], g_ref[...], m_ref[...], v_ref[...])

    def spec(a):
        return pl.BlockSpec((tr, c), lambda i: (i, 0)) if a.ndim == 2 else pl.BlockSpec((None, tr, c), lambda i: (0, i, 0))

    return pl.pallas_call(
        body, name=name, grid=(r // tr,), in_specs=[spec(w), spec(g), spec(m), spec(v)], out_specs=[spec(w)] * 3,
        out_shape=[SDS(w.shape, F32)] * 3, compiler_params=_cparams(1))(w, g, m, v)


def adamw_blocks(w, blocks, m, v, name):
    r, c = w.shape
    tr = next((cand for cand in (128, 64, 32, 16, 8) if r % cand == 0), r)

    def body(w_ref, b_ref, m_ref, v_ref, g_ref, d_ref, mo_ref, vo_ref):
        g = b_ref[0].astype(F32)
        for k in range(1, N_DEV):
            g += b_ref[k].astype(F32)
        g_ref[...] = g
        d_ref[...], mo_ref[...], vo_ref[...] = _adam_math(w_ref[...], g, m_ref[...], v_ref[...])

    spec = pl.BlockSpec((tr, c), lambda i: (i, 0))
    return pl.pallas_call(
        body, name=name, grid=(r // tr,),
        in_specs=[spec, pl.BlockSpec((N_DEV, tr, c), lambda i: (0, i, 0)), spec, spec], out_specs=[spec] * 4,
        out_shape=[SDS((r, c), F32)] * 4, compiler_params=_cparams(1))(w, blocks, m, v)


def adamw_many(ws, gs, ms, vs, name):
    n = len(ws)

    def body(*refs):
        w_r, g_r, m_r, v_r = (refs[k * n:(k + 1) * n] for k in range(4))
        d_r, mo_r, vo_r = (refs[(4 + k) * n:(5 + k) * n] for k in range(3))
        for k in range(n):
            d_r[k][...], mo_r[k][...], vo_r[k][...] = _adam_math(w_r[k][...], g_r[k][...], m_r[k][...], v_r[k][...])

    shapes = [SDS(w.shape, F32) for w in ws]
    outs = pl.pallas_call(body, name=name, out_shape=shapes * 3)(*ws, *gs, *ms, *vs)
    return outs[:n], outs[n:2 * n], outs[2 * n:]


TRANSPOSED = ("ffn1_w_gate", "ffn1_w_up", "ffn2_w_gate", "ffn2_w_up", "w_in", "w_ukv", "w_uq")
GATHER_GROUPS = (("ffn1_w_gate", "ffn1_w_up"), ("ffn1_w_down",),
                 ("w_in", "w_ukv", "w_uq", "w_out", "ffn2_w_gate", "ffn2_w_up", "ffn2_w_down"))
GRAD_GROUPS = (("ffn2", ("ffn2_w_gate", "ffn2_w_up", "ffn2_w_down")), ("mixer", ("w_out", "w_in", "w_ukv", "w_uq")),
               ("ffn1_down", ("ffn1_w_down",)), ("ffn1_gate", ("ffn1_w_gate",)), ("ffn1_up", ("ffn1_w_up",)))


def _shard_view(name, w):
    return w[0].T if name in TRANSPOSED else w[0]


def _shard_unview(name, t):
    return t.T[None] if name in TRANSPOSED else t[None]


def _grad_blocks(name, gw):
    if name == "w_in":
        return _in_proj_rows_inv(gw).reshape(N_DEV, -1, D_MODEL)
    if name == "w_ukv":
        hd = MLA_HEADS * HEAD_PAD
        return jnp.concatenate([gw[:hd].reshape(MLA_HEADS, HEAD_PAD, KV_LORA)[:, :QK_NOPE],
                                gw[hd:].reshape(MLA_HEADS, V_HEAD, KV_LORA)], axis=1)
    if name == "w_uq":
        return gw.reshape(MLA_HEADS, HEAD_PAD, Q_LORA)[:, :QK_DIM]
    return gw.reshape(N_DEV, -1, D_MODEL)


def _pack_rows(arrs):
    parts = []
    for a in arrs:
        flat = a.reshape(-1).astype(F32)
        pad = (-flat.shape[0]) % D_MODEL
        if pad:
            flat = jnp.pad(flat, (0, pad))
        parts.append(flat.reshape(-1, D_MODEL))
    out = jnp.concatenate(parts, axis=0)
    pad = (-out.shape[0]) % 8
    if pad:
        out = jnp.pad(out, ((0, pad), (0, 0)))
    return out


def _unpack_rows(packed, shapes):
    out, row = [], 0
    for shp in shapes:
        n = math.prod(shp)
        nrow = -(-n // D_MODEL)
        out.append(packed[row:row + nrow].reshape(-1)[:n].reshape(shp))
        row += nrow
    return out


def _in_proj_rows(w_t):
    return jnp.concatenate([w_t[0:2560], w_t[2576:2960], w_t[2960:3216], w_t[2560:2576], w_t[3216:3248],
                            jnp.zeros((D_IN_PAD - D_IN, D_MODEL), w_t.dtype)], axis=0)


def _in_proj_rows_inv(d):
    return jnp.concatenate([d[0:2560], d[3200:3216], d[2560:2944], d[2944:3200], d[3216:3248]], axis=0)


def _rope_tables(positions):
    inv_freq = ROPE_THETA ** (-jnp.arange(0, QK_ROPE, 2, dtype=F32) / QK_ROPE)
    ang = positions[..., None].astype(F32) * inv_freq
    cos, sin = jnp.cos(ang), jnp.sin(ang)
    one = jnp.ones(ang.shape[:2] + (QK_NOPE,), F32)
    zero = jnp.zeros_like(one)
    z16, z32, o32 = zero[..., :16], zero[..., :32], one[..., :32]
    cc = jnp.concatenate([one, cos, cos, o32], axis=-1)
    sp = jnp.concatenate([zero, z16, sin, z32], axis=-1)
    sm = jnp.concatenate([zero, -sin, z16, z32], axis=-1)
    return cc, sp, sm


def weight_views(gathered):
    full = lambda name: gathered[name].reshape(-1, gathered[name].shape[2])
    ukv = full("w_ukv").reshape(MLA_HEADS, QK_NOPE + V_HEAD, KV_LORA)
    wukv_t = jnp.concatenate([jnp.pad(ukv[:, :QK_NOPE], ((0, 0), (0, HEAD_PAD - QK_NOPE), (0, 0))).reshape(-1, KV_LORA),
                              ukv[:, QK_NOPE:].reshape(-1, KV_LORA)], axis=0)
    uq = full("w_uq").reshape(MLA_HEADS, QK_DIM, Q_LORA)
    wuq_t = jnp.pad(uq, ((0, 0), (0, HEAD_PAD - QK_DIM), (0, 0))).reshape(-1, Q_LORA)
    return dict(wg1_t=full("ffn1_w_gate"), wu1_t=full("ffn1_w_up"), wd1=full("ffn1_w_down"),
                wg2_t=full("ffn2_w_gate"), wu2_t=full("ffn2_w_up"), wd2=full("ffn2_w_down"),
                wo=full("w_out"), win_t=_in_proj_rows(full("w_in")), wukv_t=wukv_t, wuq_t=wuq_t)


def _ffn_bwd(tag, dxn, do, dgate, x, h, gg, uu, a, sc, norm_w, wg_t, wu_t, wd, below):
    f2 = wd.shape[0] // 2
    dwd = mm_tn(a, do, f2, D_MODEL, tag + "_dwd")
    dgg, duu = ffn_dact(do, wd, gg, uu, tag + "_dact")
    dwg_t = mm_tn(dgg, h, f2, D_MODEL, tag + "_dwg")
    dwu_t = mm_tn(duu, h, f2, D_MODEL, tag + "_dwu")
    dx, dsc, dsh, dnw, *nxt = dh_norm_bwd([dgg, duu], [wg_t, wu_t], x, dxn, norm_w, sc, tag + "_dh", below)
    return dx, (dsh, dsc, dgate), dnw, (dwg_t, dwu_t, dwd), nxt


def local_step(x, tgt, positions, mod, wv, p):
    nb, s, d = x.shape
    sh1, sc1, g1, sh2, sc2, g2, sh3, sc3, g3 = mod
    cc, sp, sm = _rope_tables(positions)
    lane_head = jnp.arange(D_SSD, dtype=I32)[None, :] // SSD_HEAD_DIM
    e_mat = (lane_head == jnp.arange(LANES, dtype=I32)[:, None]).astype(BF16)
    et_mat = e_mat.T
    rr, cl = jnp.arange(LANES, dtype=I32)[:, None], jnp.arange(LANES, dtype=I32)[None, :]
    place = ((cl == rr + (QK_NOPE - SSD_HEADS)) & (rr >= SSD_HEADS) & (rr < SSD_HEADS + QK_ROPE)).astype(F32)
    dtb = jnp.pad(p["dt_bias"], ((0, 0), (0, LANES - SSD_HEADS)))
    alog = jnp.pad(p["a_log"], ((0, 0), (0, LANES - SSD_HEADS)))
    dskip_e = jnp.repeat(p["d_skip"], SSD_HEAD_DIM, axis=1)

    h1 = norm_mod(x, p["norm_ffn1"], sc1, sh1, "ffn1_norm")
    gg1, uu1, a1 = ffn_up(h1, wv["wg1_t"], wv["wu1_t"], "ffn1_up")
    x1, o1, h2 = ffn_down(a1, wv["wd1"], x, g1, 0.5, "ffn1_down", (p["norm_mix"], sc2, sh2))
    z, u, cq, ckv, misc = in_proj(h2, wv["win_t"], "in_proj")
    xs, bm, cm_ = conv_fwd(u, p["conv_w"], p["conv_b"], "conv_fwd")
    ys, y, prev = ssd_fwd(xs, bm, cm_, misc, z, dtb, alog, dskip_e, p["ssd_norm_w"], e_mat, "ssd_fwd")
    q, k, v, vt, qn, kvn = qkv_fwd(cq, ckv, misc, cc, sp, sm, p["q_norm_w"], p["kv_norm_w"], wv["wuq_t"], wv["wukv_t"],
                               place, "qkv_fwd")
    attn, lse = flash_fwd(q, k, vt, "flash_fwd")
    x2, o2, ym, h3 = out_proj(ys, attn, p["mla_norm_w"], wv["wo"], x1, g2, (p["norm_ffn2"], sc3, sh3), "out_proj")
    gg3, uu3, a3 = ffn_up(h3, wv["wg2_t"], wv["wu2_t"], "ffn2_up")
    loss, dx3, dnfin, do3, dg3 = ffn_down_final(a3, wv["wd2"], x2, g3, 0.5, p["norm_final"], tgt, "ffn2_down_loss")

    dx2, dmod3, dnf2, (dwg2, dwu2, dwd2), (dout, dg2) = _ffn_bwd(
        "ffn2", dx3, do3, dg3, x2, h3, gg3, uu3, a3, sc3, p["norm_ffn2"], wv["wg2_t"], wv["wu2_t"], wv["wd2"],
        (o2, g2, 1.0))
    dys, dattn, dlt, dmlan = out_proj_bwd(dout, attn, p["mla_norm_w"], wv["wo"], "out_proj_bwd")
    dwo = jnp.concatenate([mm_tn(ys, dout, D_SSD, D_MODEL, "dwo_ssd"), mm_tn(ym, dout, D_SSD, D_MODEL, "dwo_mla")], axis=0)
    dxs, dbm, dcm, dz, ddt, dssdn, ddsk_lane, ddtb, dalog = ssd_bwd(
        dys, y, z, xs, bm, cm_, misc, prev, dtb, alog, dskip_e, p["ssd_norm_w"], e_mat, et_mat, "ssd_bwd")
    dq, dk, dv = flash_bwd(q, k, v, dattn, lse, dlt, "flash_bwd")
    dcq, dckv, dmisc, dqp, dkvc, dqn, dkvn = qkv_bwd(dq, dk, dv, ddt, cq, ckv, cc, sp, sm, p["q_norm_w"], p["kv_norm_w"],
                                                     wv["wuq_t"], wv["wukv_t"], place.T, "qkv_bwd")
    dwuq = mm_tn(dqp, qn, MLA_HEADS * HEAD_PAD, Q_LORA, "dwuq")
    dwukv = mm_tn(dkvc, kvn, MLA_HEADS * HEAD_PAD, KV_LORA, "dwukv")
    dvv, dconv = conv_bwd_a(dxs, dbm, dcm, u, p["conv_w"], p["conv_b"], "conv_bwd_a")
    dconv = conv_grads_fold(dconv, "conv_grads_fold")
    du = conv_bwd_b(dvv, p["conv_w"], "conv_bwd_b")
    dproj = [dz, du, dcq, dckv, dmisc]
    dwin = mm_tn_blocks(dproj, h2, "dwin")
    dx1, dsc2, dsh2, dnmix, do1, dg1 = dh_norm_bwd(dproj, [wv["win_t"]], x1, dx2, p["norm_mix"], sc2, "mix_dh",
                                                   (o1, g1, 0.5))
    dx0, dmod1, dnf1, (dwg1, dwu1, dwd1), _ = _ffn_bwd(
        "ffn1", dx1, do1, dg1, x, h1, gg1, uu1, a1, sc1, p["norm_ffn1"], wv["wg1_t"], wv["wu1_t"], wv["wd1"], None)

    dmod = jnp.concatenate([*dmod1, dsh2, dsc2, dg2, *dmod3], axis=1).reshape(nb, N_MOD * d)
    return dict(
        loss=loss, dx=dx0, dmod=dmod, norm_ffn1=dnf1, norm_mix=dnmix, norm_ffn2=dnf2, norm_final=dnfin,
        ssd_norm_w=dssdn, mla_norm_w=dmlan, q_norm_w=dqn, kv_norm_w=dkvn,
        dt_bias=ddtb[:, :SSD_HEADS], a_log=dalog[:, :SSD_HEADS],
        d_skip=squeeze_heads(ddsk_lane, et_mat, "d_skip_heads")[:, :SSD_HEADS],
        conv_b=dconv[4:5], conv_w=dconv[0:4],
        gw=dict(ffn1_w_gate=dwg1, ffn1_w_up=dwu1, ffn1_w_down=dwd1, ffn2_w_gate=dwg2, ffn2_w_up=dwu2, ffn2_w_down=dwd2,
                w_out=dwo, w_in=dwin, w_ukv=dwukv, w_uq=dwuq))


def kernel(x, c, positions, w_ada, b_ada, norm_ffn1, ffn1_w_gate, ffn1_w_up, ffn1_w_down, norm_mix, w_in, conv_w, conv_b, dt_bias, a_log, d_skip, ssd_norm_w, q_norm_w, w_uq, kv_norm_w, w_ukv, mla_norm_w, w_out, norm_ffn2, ffn2_w_gate, ffn2_w_up, ffn2_w_down, norm_final, loss_target, m_w_ada, m_b_ada, m_norm_ffn1, m_ffn1_w_gate, m_ffn1_w_up, m_ffn1_w_down, m_norm_mix, m_w_in, m_conv_w, m_conv_b, m_dt_bias, m_a_log, m_d_skip, m_ssd_norm_w, m_q_norm_w, m_w_uq, m_kv_norm_w, m_w_ukv, m_mla_norm_w, m_w_out, m_norm_ffn2, m_ffn2_w_gate, m_ffn2_w_up, m_ffn2_w_down, m_norm_final, v_w_ada, v_b_ada, v_norm_ffn1, v_ffn1_w_gate, v_ffn1_w_up, v_ffn1_w_down, v_norm_mix, v_w_in, v_conv_w, v_conv_b, v_dt_bias, v_a_log, v_d_skip, v_ssd_norm_w, v_q_norm_w, v_w_uq, v_kv_norm_w, v_w_ukv, v_mla_norm_w, v_w_out, v_norm_ffn2, v_ffn2_w_gate, v_ffn2_w_up, v_ffn2_w_down, v_norm_final):
    names = ["w_ada", "b_ada", "norm_ffn1", "ffn1_w_gate", "ffn1_w_up", "ffn1_w_down", "norm_mix", "w_in", "conv_w",
             "conv_b", "dt_bias", "a_log", "d_skip", "ssd_norm_w", "q_norm_w", "w_uq", "kv_norm_w", "w_ukv",
             "mla_norm_w", "w_out", "norm_ffn2", "ffn2_w_gate", "ffn2_w_up", "ffn2_w_down", "norm_final"]
    W = dict(zip(names, (w_ada, b_ada, norm_ffn1, ffn1_w_gate, ffn1_w_up, ffn1_w_down, norm_mix, w_in, conv_w, conv_b, dt_bias, a_log, d_skip, ssd_norm_w, q_norm_w, w_uq, kv_norm_w, w_ukv, mla_norm_w, w_out, norm_ffn2, ffn2_w_gate, ffn2_w_up, ffn2_w_down, norm_final)))
    M = dict(zip(names, (m_w_ada, m_b_ada, m_norm_ffn1, m_ffn1_w_gate, m_ffn1_w_up, m_ffn1_w_down, m_norm_mix, m_w_in, m_conv_w, m_conv_b, m_dt_bias, m_a_log, m_d_skip, m_ssd_norm_w, m_q_norm_w, m_w_uq, m_kv_norm_w, m_w_ukv, m_mla_norm_w, m_w_out, m_norm_ffn2, m_ffn2_w_gate, m_ffn2_w_up, m_ffn2_w_down, m_norm_final)))
    V = dict(zip(names, (v_w_ada, v_b_ada, v_norm_ffn1, v_ffn1_w_gate, v_ffn1_w_up, v_ffn1_w_down, v_norm_mix, v_w_in, v_conv_w, v_conv_b, v_dt_bias, v_a_log, v_d_skip, v_ssd_norm_w, v_q_norm_w, v_w_uq, v_kv_norm_w, v_w_ukv, v_mla_norm_w, v_w_out, v_norm_ffn2, v_ffn2_w_gate, v_ffn2_w_up, v_ffn2_w_down, v_norm_final)))

    nb, s, d = x.shape
    me = 4 * lax.axis_index("x") + 2 * lax.axis_index("y") + lax.axis_index("c")
    n_ada = w_ada.shape[2]

    taps, n_cw = conv_w.shape[1:]
    (cg,) = all_gather8([_pack_rows([c, conv_w[0]])], "gather_c")
    c_all = cg[:, 0:nb].reshape(N_DEV * nb, d)
    conv_w_full = cg[:, nb, 0:taps * n_cw].reshape(N_DEV, taps, n_cw).transpose(1, 0, 2).reshape(taps, N_DEV * n_cw)
    shards = [[_shard_view(name, W[name]).astype(BF16) for name in group] for group in GATHER_GROUPS]
    gathered = dict(zip(GATHER_GROUPS[0], all_gather8(shards[0], "gather_w_ffn1")))

    b_ada_cols = lax.dynamic_slice(b_ada, (0, me * n_ada), (1, n_ada))
    mod_cols, c_act = adaln_fwd(c_all, w_ada[0], b_ada_cols, "adaln_fwd")
    (mod_g,) = all_gather8([mod_cols], "gather_mod")
    gathered, mod_g, shards = lax.optimization_barrier((gathered, mod_g, shards))
    gathered.update(zip(GATHER_GROUPS[1], sc_all_gather8(shards[1], "gather_w_ffn1_down", 1)))
    gathered.update(zip(GATHER_GROUPS[2], sc_all_gather8(shards[2], "gather_w_rest", 7)))
    wv = weight_views(gathered)
    mod = lax.dynamic_slice(mod_g, (0, me * nb, 0), (N_DEV, nb, n_ada)).transpose(1, 0, 2).reshape(nb, N_MOD, 1, d)
    mod = [mod[:, k] for k in range(N_MOD)]

    P = dict(W)
    P["conv_w"] = conv_w_full
    P["norm_final"] = norm_final.reshape(1, d)
    R = local_step(x, loss_target, positions, mod, wv, P)

    dmod = R["dmod"]
    partial_shapes = [(1,), (1, d), (1, d), (1, d), (1, d), (1, d), (1, d), (1, Q_LORA), (1, KV_LORA),
                      (1, SSD_HEADS), (1, SSD_HEADS), (1, SSD_HEADS), (1, D_CONV), (4, D_CONV), (1, N_MOD * d),
                      (nb, N_MOD * d)]
    partial = _pack_rows([R["loss"][0, :1], R["norm_ffn1"], R["norm_mix"], R["norm_ffn2"], R["norm_final"],
                          R["ssd_norm_w"], R["mla_norm_w"], R["q_norm_w"], R["kv_norm_w"],
                          R["dt_bias"], R["a_log"], R["d_skip"], R["conv_b"], R["conv_w"],
                          sum_rows(dmod, "dmod_rows"), dmod])
    (partial_g,) = all_gather8([partial], "gather_partials")
    (loss, g_nf1, g_nmix, g_nf2, g_nfin, g_ssdn, g_mlan, g_qn, g_kvn, g_dtb, g_alog, g_dskip, g_convb, g_convw,
     g_bada, _) = _unpack_rows(sum_blocks(partial_g, "sum_partials"), partial_shapes)
    dmod_row = sum(-(-math.prod(shp) // D_MODEL) for shp in partial_shapes[:-1])
    dmod_all = partial_g[:, dmod_row:dmod_row + nb * N_MOD].reshape(N_DEV * nb, N_MOD * d)
    g_wada = adaln_bwd(c_act, lax.dynamic_slice(dmod_all, (0, me * n_ada), (N_DEV * nb, n_ada)), "adaln_bwd")
    n_cw = conv_w.shape[2]
    G = {"w_ada": g_wada[None], "b_ada": g_bada, "norm_ffn1": g_nf1, "norm_mix": g_nmix, "norm_ffn2": g_nf2,
         "norm_final": g_nfin.reshape(d), "ssd_norm_w": g_ssdn, "mla_norm_w": g_mlan, "q_norm_w": g_qn,
         "kv_norm_w": g_kvn, "dt_bias": g_dtb, "a_log": g_alog, "d_skip": g_dskip, "conv_b": g_convb,
         "conv_w": lax.dynamic_slice(g_convw, (0, me * n_cw), (4, n_cw))[None]}

    DW, NM, NV = {}, {}, {}
    gw = R["gw"]
    for k, (tag, group) in enumerate(GRAD_GROUPS):
        send = [_grad_blocks(name, gw[name]).reshape(N_DEV, *_shard_view(name, W[name]).shape) for name in group]
        recv = sc_all_to_all8(send, "exchange_" + tag, 2 + k)
        for name, blocks in zip(group, recv):
            res = adamw_blocks(_shard_view(name, W[name]), blocks, _shard_view(name, M[name]), _shard_view(name, V[name]),
                               "adamw_" + name)
            G[name], DW[name], NM[name], NV[name] = [_shard_unview(name, t) for t in res]
    DW["w_ada"], NM["w_ada"], NV["w_ada"] = adamw(w_ada, g_wada, m_w_ada, v_w_ada, "adamw_w_ada")
    small = [n for n in names if n not in DW]
    as2d = lambda a: a.reshape(-1, a.shape[-1])
    outs = adamw_many([as2d(W[n]) for n in small], [as2d(G[n]) for n in small], [as2d(M[n]) for n in small],
                      [as2d(V[n]) for n in small], "adamw_small")
    for res, dst in zip(outs, (DW, NM, NV)):
        for n, t in zip(small, res):
            dst[n] = t.reshape(W[n].shape)
    return (loss.reshape(()), R["dx"], *[G[n] for n in names], *[DW[n] for n in names], *[NM[n] for n in names],
            *[NV[n] for n in names])
```

```python
import math

import jax
import jax.numpy as jnp
from jax import lax
from jax.experimental import pallas as pl
from jax.experimental.pallas import tpu as pltpu
from jax.experimental.pallas import tpu_sc as plsc

F32, BF16, I32 = jnp.float32, jnp.bfloat16, jnp.int32
HI = lax.Precision.HIGHEST
SDS = jax.ShapeDtypeStruct
MESH = pl.DeviceIdType.MESH

D_MODEL = 1024
D_FF = 2816
D_SSD = 1024
SSD_HEADS = 16
SSD_HEAD_DIM = 64
SSD_GROUPS = 2
SSD_STATE = 128
CHUNK = 128
MLA_HEADS = 8
QK_NOPE = 64
QK_ROPE = 32
QK_DIM = 96
V_HEAD = 128
Q_LORA = 384
KV_LORA = 256
ROPE_THETA = 10000.0
N_MOD = 9
EPS = 1e-6
D_CONV = 1536
D_IN = 3248
D_IN_PAD = 3328
HEAD_PAD = 128
N_DEV = 8
ADAM_LR, ADAM_B1, ADAM_B2, ADAM_EPS, ADAM_WD, ADAM_STEP = 0.001, 0.9, 0.999, 1e-08, 0.01, 10

SAVED_ACT = BF16
VMEM_LIMIT = 56 * 1024 * 1024
LANES = 128
NT_DIMS = (((1,), (1,)), ((), ()))
TN_DIMS = (((0,), (0,)), ((), ()))


def _cparams(n_axes):
    return pltpu.CompilerParams(dimension_semantics=("arbitrary",) * n_axes, vmem_limit_bytes=VMEM_LIMIT)


def _row(tm, d):
    return pl.BlockSpec((None, tm, d), lambda b, i: (b, i, 0))


def _bvec(d):
    return pl.BlockSpec((None, 1, d), lambda b, i: (b, 0, 0))


def _full(shape):
    n = len(shape)
    return pl.BlockSpec(shape, lambda *_: (0,) * n)


def _sigmoid(x):
    return 1.0 / (1.0 + jnp.exp(-x))


def _softplus(x):
    return jnp.maximum(x, 0.0) + jnp.log(1.0 + jnp.exp(-jnp.abs(x)))


def _rms(x):
    return lax.rsqrt(jnp.mean(x * x, axis=-1, keepdims=True) + EPS)


def _rms_bwd(dn, n, r):
    return r * (dn - n * jnp.mean(dn * n, axis=-1, keepdims=True))


def _first_step():
    return (pl.program_id(0) == 0) & (pl.program_id(1) == 0)


def _gather_copies(x_refs, out_refs, send_sems, recv_sems, local_sems):
    mx, my, mc = lax.axis_index("x"), lax.axis_index("y"), lax.axis_index("c")
    me, sibling = (mx, my, mc), (mx, my, 1 - mc)
    chips = [(1 - mx, my), (mx, 1 - my), (1 - mx, 1 - my)]

    def copy(a, k, block, to, src=None):
        rows = out_refs[a].at[4 * block[0] + 2 * block[1] + block[2]]
        return pltpu.make_async_remote_copy(
            src_ref=rows if src is None else src, dst_ref=rows,
            send_sem=send_sems.at[7 * a + k], recv_sem=recv_sems.at[7 * a + k], device_id=to, device_id_type=MESH)

    arrays = range(len(x_refs))
    mine = [pltpu.make_async_copy(x_refs[a], out_refs[a].at[4 * mx + 2 * my + mc], local_sems.at[a]) for a in arrays]
    first = [[copy(a, 0, me, sibling, src=x_refs[a])] + [copy(a, 1 + j, me, (*chip, mc), src=x_refs[a])
                                                          for j, chip in enumerate(chips)] for a in arrays]
    passed = [[copy(a, 4 + j, (*chip, mc), sibling) for j, chip in enumerate(chips)] for a in arrays]
    for a in arrays:
        mine[a].start()
        for cp in first[a]:
            cp.start()
    for a in arrays:
        for j, chip in enumerate(chips):
            copy(a, 1 + j, (*chip, mc), me).wait_recv()
            passed[a][j].start()
    for a in arrays:
        copy(a, 0, sibling, me).wait_recv()
        for j, chip in enumerate(chips):
            copy(a, 4 + j, (*chip, 1 - mc), me).wait_recv()
    for a in arrays:
        for cp in first[a] + passed[a]:
            cp.wait_send()
        mine[a].wait()


def _gather_peers():
    mx, my, mc = lax.axis_index("x"), lax.axis_index("y"), lax.axis_index("c")
    return [(mx, my, 1 - mc), (1 - mx, my, mc), (mx, 1 - my, mc), (1 - mx, 1 - my, mc)]


def _comm_scratch(n):
    return [pltpu.SemaphoreType.DMA((7 * n,)), pltpu.SemaphoreType.DMA((7 * n,)), pltpu.SemaphoreType.DMA((n,))]


def all_gather8(xs, name):
    n = len(xs)

    def body(*refs):
        _gather_copies(refs[:n], refs[n:2 * n], *refs[2 * n:])

    return pl.pallas_call(
        body, name=name,
        out_shape=[SDS((N_DEV, *x.shape), x.dtype) for x in xs],
        in_specs=[pl.BlockSpec(memory_space=pl.ANY)] * n,
        out_specs=[pl.BlockSpec(memory_space=pl.ANY)] * n,
        scratch_shapes=_comm_scratch(n),
    )(*xs)


def _exchange_peers():
    mx, my, mc = lax.axis_index("x"), lax.axis_index("y"), lax.axis_index("c")
    return [(1 - mx if rel & 4 else mx, 1 - my if rel & 2 else my, 1 - mc if rel & 1 else mc) for rel in range(1, N_DEV)]


def _exchange_copies(x_refs, out_refs, send_sems, recv_sems, local_sems):
    mx, my, mc = lax.axis_index("x"), lax.axis_index("y"), lax.axis_index("c")
    me = 4 * mx + 2 * my + mc
    copies = []
    for a, (x_ref, out_ref) in enumerate(zip(x_refs, out_refs)):
        mine = pltpu.make_async_copy(x_ref.at[me], out_ref.at[me], local_sems.at[a])
        mine.start()
        copies.append(mine)
        for k, (px, py, pc) in enumerate(_exchange_peers()):
            cp = pltpu.make_async_remote_copy(
                src_ref=x_ref.at[4 * px + 2 * py + pc], dst_ref=out_ref.at[me],
                send_sem=send_sems.at[7 * a + k], recv_sem=recv_sems.at[7 * a + k],
                device_id=(px, py, pc), device_id_type=MESH)
            cp.start()
            copies.append(cp)
    for cp in copies:
        cp.wait()


def all_to_all8(xs, name):
    n = len(xs)

    def body(*refs):
        _exchange_copies(refs[:n], refs[n:2 * n], *refs[2 * n:])

    return pl.pallas_call(
        body, name=name,
        out_shape=[SDS(x.shape, x.dtype) for x in xs],
        in_specs=[pl.BlockSpec(memory_space=pl.ANY)] * n,
        out_specs=[pl.BlockSpec(memory_space=pl.ANY)] * n,
        scratch_shapes=_comm_scratch(n),
    )(*xs)


def _sequencer_kernel(name, collective_id, n_arrays):
    return pl.kernel(
        mesh=plsc.ScalarSubcoreMesh(axis_name="seq", num_cores=1), name=name,
        scratch_types=tuple(_comm_scratch(n_arrays)),
        compiler_params=pltpu.CompilerParams(collective_id=collective_id))


def _handshake(peers):
    barrier = pltpu.get_barrier_semaphore()
    for peer in peers:
        pl.semaphore_signal(barrier, inc=1, device_id=peer, device_id_type=MESH)
    pl.semaphore_wait(barrier, len(peers))


def _hbm_refs(xs, out_shapes):
    x_refs = [jax.new_ref(x, memory_space=pltpu.MemorySpace.HBM) for x in xs]
    out_refs = [jax.empty_ref(SDS(shp, x.dtype), memory_space=pltpu.MemorySpace.HBM) for x, shp in zip(xs, out_shapes)]
    return x_refs, out_refs


def sc_all_gather8(xs, name, collective_id):
    x_refs, out_refs = _hbm_refs(xs, [(N_DEV, *x.shape) for x in xs])

    @_sequencer_kernel(name, collective_id, len(xs))
    def launch(send_sems, recv_sems, local_sems):
        _handshake(_gather_peers())
        _gather_copies(x_refs, out_refs, send_sems, recv_sems, local_sems)

    launch()
    return [ref[...] for ref in out_refs]


def sc_all_to_all8(xs, name, collective_id):
    x_refs, out_refs = _hbm_refs(xs, [x.shape for x in xs])

    @_sequencer_kernel(name, collective_id, len(xs))
    def launch(send_sems, recv_sems, local_sems):
        _handshake(_exchange_peers())
        _exchange_copies(x_refs, out_refs, send_sems, recv_sems, local_sems)

    launch()
    return [ref[...] for ref in out_refs]


def norm_mod(x, w, sc, sh, name):
    b, s, d = x.shape
    tm = min(512, s)

    def body(x_ref, w_ref, sc_ref, sh_ref, h_ref):
        xv = x_ref[...]
        n = xv * _rms(xv)
        h_ref[...] = ((n * w_ref[...]) * (1.0 + sc_ref[...]) + sh_ref[...]).astype(BF16)

    return pl.pallas_call(
        body, name=name, grid=(b, s // tm),
        in_specs=[_row(tm, d), _full((1, d)), _bvec(d), _bvec(d)],
        out_specs=_row(tm, d), out_shape=SDS((b, s, d), BF16), compiler_params=_cparams(2))(x, w, sc, sh)


def ffn_up(h, wg_t, wu_t, name):
    b, s, d = h.shape
    f = wg_t.shape[0]
    tm, tn = min(1024, s), f // 2

    def body(h_ref, wg_ref, wu_ref, s_ref, t_ref, a_ref):
        hv = h_ref[...]
        g = lax.dot_general(hv, wg_ref[...], NT_DIMS, preferred_element_type=F32)
        u = lax.dot_general(hv, wu_ref[...], NT_DIMS, preferred_element_type=F32)
        sg = _sigmoid(g)
        silu = g * sg
        s_ref[...] = silu.astype(s_ref.dtype)
        t_ref[...] = (u * (sg + silu * (1.0 - sg))).astype(t_ref.dtype)
        a_ref[...] = (silu * u).astype(BF16)

    hs = pl.BlockSpec((None, tm, d), lambda j, bb, i: (bb, i, 0))
    ws = pl.BlockSpec((tn, d), lambda j, bb, i: (j, 0))
    os_ = pl.BlockSpec((None, tm, tn), lambda j, bb, i: (bb, i, j))
    return pl.pallas_call(
        body, name=name, grid=(f // tn, b, s // tm),
        in_specs=[hs, ws, ws], out_specs=[os_, os_, os_],
        out_shape=[SDS((b, s, f), SAVED_ACT), SDS((b, s, f), SAVED_ACT), SDS((b, s, f), BF16)],
        compiler_params=_cparams(3))(h, wg_t, wu_t)


def _norm_mod_tile(xv, w_ref, sc_ref, sh_ref):
    return ((xv * _rms(xv) * w_ref[...]) * (1.0 + sc_ref[...]) + sh_ref[...]).astype(BF16)


def ffn_down(a, wd, x, gate, scale, name, above=None):
    b, s, f = a.shape
    d = wd.shape[1]
    tm = min(1024, s)

    def body(a_ref, wd_ref, x_ref, g_ref, *rest):
        xn_ref, o_ref = rest[-3:-1] if above else rest
        o = jnp.dot(a_ref[...], wd_ref[...], preferred_element_type=F32)
        xn = x_ref[...] + (scale * g_ref[...]) * o
        xn_ref[...] = xn
        o_ref[...] = o.astype(BF16)
        if above:
            rest[-1][...] = _norm_mod_tile(xn, *rest[0:3])

    extra = above is not None
    return pl.pallas_call(
        body, name=name, grid=(b, s // tm),
        in_specs=[_row(tm, f), _full((f, d)), _row(tm, d), _bvec(d)] + ([_full((1, d)), _bvec(d), _bvec(d)] if extra else []),
        out_specs=[_row(tm, d), _row(tm, d)] + ([_row(tm, d)] if extra else []),
        out_shape=[SDS((b, s, d), F32), SDS((b, s, d), BF16)] + ([SDS((b, s, d), BF16)] if extra else []),
        compiler_params=_cparams(2))(a, wd, x, gate, *(above or ()))


def ffn_down_final(a, wd, x, gate, scale, w_final, tgt, name):
    b, s, f = a.shape
    d = wd.shape[1]
    tm = min(1024, s)

    def body(a_ref, wd_ref, x_ref, g_ref, w_ref, t_ref, loss_ref, dx_ref, dw_ref, do_ref, dg_ref):
        @pl.when(_first_step())
        def _():
            loss_ref[...] = jnp.zeros_like(loss_ref)
            dw_ref[...] = jnp.zeros_like(dw_ref)

        @pl.when(pl.program_id(1) == 0)
        def _():
            dg_ref[...] = jnp.zeros_like(dg_ref)
        o = jnp.dot(a_ref[...], wd_ref[...], preferred_element_type=F32)
        sg = scale * g_ref[...]
        xv = x_ref[...] + sg * o
        r = _rms(xv)
        n = xv * r
        wv = w_ref[...]
        e = n * wv - t_ref[...]
        loss_ref[...] += jnp.sum(e * e) * (0.5 / d)
        dy = e * (1.0 / d)
        dw_ref[...] += jnp.sum(dy * n, axis=0, keepdims=True)
        dx = _rms_bwd(dy * wv, n, r)
        dx_ref[...] = dx
        do_ref[...] = (sg * dx).astype(BF16)
        dg_ref[...] += jnp.sum(scale * dx * o, axis=0, keepdims=True)

    return pl.pallas_call(
        body, name=name, grid=(b, s // tm),
        in_specs=[_row(tm, f), _full((f, d)), _row(tm, d), _bvec(d), _full((1, d)), _row(tm, d)],
        out_specs=[_full((1, LANES)), _row(tm, d), _full((1, d)), _row(tm, d), _bvec(d)],
        out_shape=[SDS((1, LANES), F32), SDS((b, s, d), F32), SDS((1, d), F32), SDS((b, s, d), BF16), SDS((b, 1, d), F32)],
        compiler_params=_cparams(2))(a, wd, x, gate, w_final, tgt)


def ffn_dact(do, wd, silu_g, u_dsilu, name):
    b, s, d = do.shape
    f = wd.shape[0]
    tm, tn = min(1024, s), f // 2

    def body(do_ref, wd_ref, s_ref, t_ref, dg_ref, du_ref):
        da = lax.dot_general(do_ref[...], wd_ref[...], NT_DIMS, preferred_element_type=F32)
        dg_ref[...] = (da * t_ref[...].astype(F32)).astype(BF16)
        du_ref[...] = (da * s_ref[...].astype(F32)).astype(BF16)

    dos = pl.BlockSpec((None, tm, d), lambda j, bb, i: (bb, i, 0))
    ws = pl.BlockSpec((tn, d), lambda j, bb, i: (j, 0))
    es = pl.BlockSpec((None, tm, tn), lambda j, bb, i: (bb, i, j))
    return pl.pallas_call(
        body, name=name, grid=(f // tn, b, s // tm),
        in_specs=[dos, ws, es, es], out_specs=[es, es],
        out_shape=[SDS((b, s, f), BF16), SDS((b, s, f), BF16)], compiler_params=_cparams(3))(do, wd, silu_g, u_dsilu)


def mm_tn(a, bm, tma, tnb, name):
    b, s, ka = a.shape
    nb = bm.shape[2]
    tk = min(2048, s)
    nk = s // tk

    def body(a_ref, b_ref, o_ref, acc):
        first = (pl.program_id(2) == 0) & (pl.program_id(3) == 0)
        last = (pl.program_id(2) == b - 1) & (pl.program_id(3) == nk - 1)
        part = lax.dot_general(a_ref[...], b_ref[...], TN_DIMS, preferred_element_type=F32)

        @pl.when(first)
        def _():
            acc[...] = part

        @pl.when(jnp.logical_not(first))
        def _():
            acc[...] += part

        @pl.when(last)
        def _():
            o_ref[...] = acc[...].astype(BF16)

    return pl.pallas_call(
        body, name=name, grid=(ka // tma, nb // tnb, b, nk),
        in_specs=[pl.BlockSpec((None, tk, tma), lambda i, j, bb, k: (bb, k, i)),
                  pl.BlockSpec((None, tk, tnb), lambda i, j, bb, k: (bb, k, j))],
        out_specs=pl.BlockSpec((tma, tnb), lambda i, j, bb, k: (i, j)),
        out_shape=SDS((ka, nb), BF16), scratch_shapes=[pltpu.VMEM((tma, tnb), F32)],
        compiler_params=_cparams(4))(a, bm)


def mm_tn_blocks(a_blocks, bm, name):
    b, s, nb = bm.shape
    widths = [a.shape[2] for a in a_blocks]
    starts = [sum(widths[:k]) for k in range(len(widths))]
    tk = min(1024, s)
    nk = s // tk
    n = len(a_blocks)

    def body(*refs):
        a_refs, b_ref, o_ref, acc = refs[:n], refs[n], refs[n + 1], refs[n + 2]
        first = (pl.program_id(0) == 0) & (pl.program_id(1) == 0)
        last = (pl.program_id(0) == b - 1) & (pl.program_id(1) == nk - 1)

        @pl.when(first)
        def _():
            acc[...] = jnp.zeros_like(acc)
        bv = b_ref[...]
        for a_ref, st, wd in zip(a_refs, starts, widths):
            acc[st:st + wd, :] += lax.dot_general(a_ref[...], bv, TN_DIMS, preferred_element_type=F32)

        @pl.when(last)
        def _():
            o_ref[...] = acc[...].astype(BF16)

    return pl.pallas_call(
        body, name=name, grid=(b, nk),
        in_specs=[_row(tk, wd) for wd in widths] + [_row(tk, nb)],
        out_specs=_full((sum(widths), nb)), out_shape=SDS((sum(widths), nb), BF16),
        scratch_shapes=[pltpu.VMEM((sum(widths), nb), F32)], compiler_params=_cparams(2))(*a_blocks, bm)


def _gate_bwd_specs(tm, d, b, s):
    return ([_row(tm, d), _bvec(d)], [_row(tm, d), _bvec(d)], [SDS((b, s, d), BF16), SDS((b, 1, d), F32)])


def _gate_bwd_tile(dx, scale, o_ref, g_ref, do_ref, dg_ref):
    do_ref[...] = ((scale * g_ref[...]) * dx).astype(BF16)
    dg_ref[...] += jnp.sum(scale * dx * o_ref[...].astype(F32), axis=0, keepdims=True)


def n_in_bytes(arrs):
    return sum(a.size * a.dtype.itemsize for a in arrs)


def dh_norm_bwd(dys, wts, x, dxn, w, sc, name, below=None):
    b, s, d = x.shape
    tm = min(512 if n_in_bytes(wts) <= 8 * 1024 * 1024 else 256, s)
    n_in, n_w = len(dys), len(wts)
    extra_in, extra_out, extra_shape = _gate_bwd_specs(tm, d, b, s) if below else ([], [], [])
    starts = [sum(dy.shape[2] for dy in dys[:k]) for k in range(n_in)]

    def body(*refs):
        dy_refs, w_refs = refs[:n_in], refs[n_in:n_in + n_w]
        x_ref, dxn_ref, nw_ref, sc_ref = refs[n_in + n_w:n_in + n_w + 4]
        rest = refs[n_in + n_w + 4:]
        if below:
            o_ref, g_ref, dx_ref, dsc_ref, dsh_ref, dw_ref, do_ref, dg_ref = rest
        else:
            dx_ref, dsc_ref, dsh_ref, dw_ref = rest

        @pl.when(pl.program_id(1) == 0)
        def _():
            dsc_ref[...] = jnp.zeros_like(dsc_ref)
            dsh_ref[...] = jnp.zeros_like(dsh_ref)
            if below:
                dg_ref[...] = jnp.zeros_like(dg_ref)

        @pl.when(_first_step())
        def _():
            dw_ref[...] = jnp.zeros_like(dw_ref)

        def weight(k):
            return w_refs[k][...] if n_w == n_in else w_refs[0][starts[k]:starts[k] + dys[k].shape[2], :]

        dh = jnp.dot(dy_refs[0][...], weight(0), preferred_element_type=F32)
        for k in range(1, n_in):
            dh += jnp.dot(dy_refs[k][...], weight(k), preferred_element_type=F32)
        xv = x_ref[...]
        r = _rms(xv)
        n = xv * r
        nw = nw_ref[...]
        dsc_ref[...] += jnp.sum(dh * (n * nw), axis=0, keepdims=True)
        dsh_ref[...] += jnp.sum(dh, axis=0, keepdims=True)
        dhn = dh * (1.0 + sc_ref[...])
        dw_ref[...] += jnp.sum(dhn * n, axis=0, keepdims=True)
        dx = dxn_ref[...] + _rms_bwd(dhn * nw, n, r)
        dx_ref[...] = dx
        if below:
            _gate_bwd_tile(dx, below[2], o_ref, g_ref, do_ref, dg_ref)

    in_specs = [_row(tm, dy.shape[2]) for dy in dys] + [_full(wt.shape) for wt in wts]
    in_specs += [_row(tm, d), _row(tm, d), _full((1, d)), _bvec(d)] + extra_in
    return pl.pallas_call(
        body, name=name, grid=(b, s // tm), in_specs=in_specs,
        out_specs=[_row(tm, d), _bvec(d), _bvec(d), _full((1, d))] + extra_out,
        out_shape=[SDS((b, s, d), F32), SDS((b, 1, d), F32), SDS((b, 1, d), F32), SDS((1, d), F32)] + extra_shape,
        compiler_params=_cparams(2))(*dys, *wts, x, dxn, w, sc, *(below[:2] if below else ()))


def in_proj(h, win_t, name):
    b, s, d = h.shape
    tm = min(512, s)
    widths = (D_SSD, D_SSD + 2 * SSD_GROUPS * SSD_STATE, Q_LORA, KV_LORA, LANES)

    def body(h_ref, w_ref, *outs):
        p = lax.dot_general(h_ref[...], w_ref[...], NT_DIMS, preferred_element_type=F32)
        off = 0
        for o_ref, wd in zip(outs, widths):
            o_ref[...] = p[:, off:off + wd]
            off += wd

    return pl.pallas_call(
        body, name=name, grid=(b, s // tm),
        in_specs=[_row(tm, d), _full(win_t.shape)],
        out_specs=[_row(tm, wd) for wd in widths],
        out_shape=[SDS((b, s, wd), F32) for wd in widths], compiler_params=_cparams(2))(h, win_t)


def _halo_prev(ts, d):
    return pl.BlockSpec((None, 8, d), lambda b, i: (b, jnp.maximum(i * (ts // 8) - 1, 0), 0))


CONV_ROWS = 32


def _conv_head(head, u_ref, up_ref):
    head[0:8, :] = jnp.where(pl.program_id(1) > 0, up_ref[...], 0.0)
    head[8:8 + CONV_ROWS, :] = u_ref[0:CONV_ROWS, :]


def _conv_windows(u_ref, head, r0):
    if r0 == 0:
        return [head[5 + k:5 + k + CONV_ROWS, :] for k in range(4)]
    return [u_ref[r0 - 3 + k:r0 - 3 + k + CONV_ROWS, :] for k in range(4)]


def _fold8(t):
    acc = t[0:8, :]
    for r in range(8, CONV_ROWS, 8):
        acc += t[r:r + 8, :]
    return acc


def conv_fwd(u, cw, cb, name):
    b, s, dc = u.shape
    ts = min(512, s)
    widths = (D_SSD, SSD_GROUPS * SSD_STATE, SSD_GROUPS * SSD_STATE)

    def body(u_ref, up_ref, w_ref, b_ref, xs_ref, bm_ref, cm_ref, head):
        _conv_head(head, u_ref, up_ref)
        ws = [w_ref[k:k + 1, :] for k in range(4)]
        bias = b_ref[...]
        for r0 in range(0, ts, CONV_ROWS):
            taps = _conv_windows(u_ref, head, r0)
            v = bias + taps[0] * ws[0] + taps[1] * ws[1] + taps[2] * ws[2] + taps[3] * ws[3]
            y = v * _sigmoid(v)
            rs = slice(r0, r0 + CONV_ROWS)
            xs_ref[rs, :] = y[:, 0:D_SSD]
            bm_ref[rs, :] = y[:, D_SSD:D_SSD + 256]
            cm_ref[rs, :] = y[:, D_SSD + 256:D_SSD + 512]

    return pl.pallas_call(
        body, name=name, grid=(b, s // ts),
        in_specs=[_row(ts, dc), _halo_prev(ts, dc), _full((4, dc)), _full((1, dc))],
        out_specs=[_row(ts, wd) for wd in widths],
        out_shape=[SDS((b, s, wd), F32) for wd in widths],
        scratch_shapes=[pltpu.VMEM((8 + CONV_ROWS, dc), F32)], compiler_params=_cparams(2))(u, u, cw, cb)


def conv_bwd_a(dxs, dbm, dcm, u, cw, cb, name):
    b, s, dc = u.shape
    ts = min(512, s)

    def body(dxs_ref, dbm_ref, dcm_ref, u_ref, up_ref, w_ref, b_ref, dv_ref, dwb_ref, head):
        @pl.when(_first_step())
        def _():
            dwb_ref[...] = jnp.zeros_like(dwb_ref)
        _conv_head(head, u_ref, up_ref)
        ws = [w_ref[k:k + 1, :] for k in range(4)]
        bias = b_ref[...]
        for r0 in range(0, ts, CONV_ROWS):
            taps = _conv_windows(u_ref, head, r0)
            v = bias + taps[0] * ws[0] + taps[1] * ws[1] + taps[2] * ws[2] + taps[3] * ws[3]
            sg = _sigmoid(v)
            rs = slice(r0, r0 + CONV_ROWS)
            dy = jnp.concatenate([dxs_ref[rs, :], dbm_ref[rs, :], dcm_ref[rs, :]], axis=1)
            dv = dy * (sg * (1.0 + v * (1.0 - sg)))
            dv_ref[rs, :] = dv
            for k in range(4):
                dwb_ref[8 * k:8 * k + 8, :] += _fold8(dv * taps[k])
            dwb_ref[32:40, :] += _fold8(dv)

    return pl.pallas_call(
        body, name=name, grid=(b, s // ts),
        in_specs=[_row(ts, D_SSD), _row(ts, 256), _row(ts, 256), _row(ts, dc), _halo_prev(ts, dc),
                  _full((4, dc)), _full((1, dc))],
        out_specs=[_row(ts, dc), _full((40, dc))],
        out_shape=[SDS((b, s, dc), F32), SDS((40, dc), F32)],
        scratch_shapes=[pltpu.VMEM((8 + CONV_ROWS, dc), F32)], compiler_params=_cparams(2))(dxs, dbm, dcm, u, u, cw, cb)


def conv_grads_fold(x, name):
    c = x.shape[1]

    def body(x_ref, o_ref):
        o_ref[...] = jnp.zeros_like(o_ref)
        for k in range(5):
            o_ref[k:k + 1, :] = jnp.sum(x_ref[8 * k:8 * k + 8, :], axis=0, keepdims=True)

    return pl.pallas_call(body, name=name, out_shape=SDS((8, c), F32))(x)


def conv_bwd_b(dv, cw, name):
    b, s, dc = dv.shape
    ts = min(512, s)
    nt = s // ts

    def body(dv_ref, dn_ref, w_ref, du_ref, tail):
        tail[0:CONV_ROWS, :] = dv_ref[ts - CONV_ROWS:ts, :]
        tail[CONV_ROWS:CONV_ROWS + 8, :] = jnp.where(pl.program_id(1) < nt - 1, dn_ref[...], 0.0)
        ws = [w_ref[k:k + 1, :] for k in range(4)]
        for r0 in range(0, ts, CONV_ROWS):
            if r0 == ts - CONV_ROWS:
                win = [tail[3 - k:3 - k + CONV_ROWS, :] for k in range(4)]
            else:
                win = [dv_ref[r0 + 3 - k:r0 + 3 - k + CONV_ROWS, :] for k in range(4)]
            acc = win[0] * ws[0] + win[1] * ws[1] + win[2] * ws[2] + win[3] * ws[3]
            du_ref[r0:r0 + CONV_ROWS, :] = acc.astype(BF16)

    nxt = pl.BlockSpec((None, 8, dc), lambda bb, i: (bb, jnp.minimum((i + 1) * (ts // 8), s // 8 - 1), 0))
    return pl.pallas_call(
        body, name=name, grid=(b, nt),
        in_specs=[_row(ts, dc), nxt, _full((4, dc))],
        out_specs=_row(ts, dc), out_shape=SDS((b, s, dc), BF16),
        scratch_shapes=[pltpu.VMEM((CONV_ROWS + 8, dc), F32)], compiler_params=_cparams(2))(dv, dv, cw)


def _ssd_common(misc_ref, dtb_ref, alog_ref, e_ref):
    ln = CHUNK
    lane = lax.broadcasted_iota(I32, (ln, LANES), 1)
    lane1 = lax.broadcasted_iota(I32, (1, LANES), 1)
    pre = misc_ref[...] + dtb_ref[...]
    dt_s = jnp.where(lane < SSD_HEADS, _softplus(pre), 0.0)
    a_neg = jnp.where(lane1 < SSD_HEADS, -jnp.exp(alog_ref[...]), 0.0)
    ri = lax.broadcasted_iota(I32, (ln, ln), 0)
    ci = lax.broadcasted_iota(I32, (ln, ln), 1)
    tril = ci <= ri
    acum = jnp.dot(tril.astype(F32), dt_s * a_neg, preferred_element_type=F32, precision=HI)
    both_e = _dot_01(jnp.concatenate([dt_s, acum], axis=0), e_ref[...], 3)
    dt_e, acum_e = both_e[0:ln], both_e[ln:2 * ln]
    return dict(pre=pre, dt_s=dt_s, a_neg=a_neg, tril=tril, ri=ri, ci=ci, acum=acum, acum_t=acum.T,
                dt_e=dt_e, eac_e=jnp.exp(acum_e), del_e=jnp.exp(acum_e[ln - 1:ln, :] - acum_e))


def _dot_01(x, m01, terms):
    acc, rest = None, x
    for k in range(terms):
        part = rest.astype(BF16)
        if k + 1 < terms:
            rest = rest - part.astype(F32)
        d = jnp.dot(part, m01, preferred_element_type=F32)
        acc = d if acc is None else acc + d
    return acc


def _decay(cm, h):
    seg = cm["acum"][:, h:h + 1] - cm["acum_t"][h:h + 1, :]
    return jnp.exp(jnp.where(cm["tril"], seg, -jnp.inf))


def ssd_fwd(xs, bm, cm_, misc, z, dtb, alog, dskip_e, norm_w, e_mat, name):
    b, s, _ = xs.shape
    ln, nc = CHUNK, s // CHUNK
    gw = D_SSD // SSD_GROUPS
    hpg = SSD_HEADS // SSD_GROUPS

    def body(xs_ref, b_ref, c_ref, misc_ref, z_ref, dtb_ref, alog_ref, dsk_ref, nw_ref, e_ref,
             ys_ref, y_ref, p_ref, st, yd):
        @pl.when(pl.program_id(1) == 0)
        def _():
            st[...] = jnp.zeros_like(st)
        cm = _ssd_common(misc_ref, dtb_ref, alog_ref, e_ref)
        xsv = xs_ref[...]
        xdt = xsv * cm["dt_e"]
        xdt_b = xdt.astype(BF16)
        xd_b = (xdt * cm["del_e"]).astype(BF16)
        gam_e = cm["eac_e"][ln - 1:ln, :]
        p_ref[...] = st[...]
        groups = [slice(gw * g, gw * (g + 1)) for g in range(SSD_GROUPS)]
        heads = [slice(SSD_HEAD_DIM * h, SSD_HEAD_DIM * (h + 1)) for h in range(SSD_HEADS)]
        bgs = [b_ref[:, SSD_STATE * g:SSD_STATE * (g + 1)].astype(BF16) for g in range(SSD_GROUPS)]
        cgs = [c_ref[:, SSD_STATE * g:SSD_STATE * (g + 1)].astype(BF16) for g in range(SSD_GROUPS)]
        cbs = [lax.dot_general(cg, bg, NT_DIMS, preferred_element_type=F32) for cg, bg in zip(cgs, bgs)]
        sts = [st[:, gs] for gs in groups]
        yoff = [jnp.dot(cg, st_g.astype(BF16), preferred_element_type=F32) * cm["eac_e"][:, gs]
                for cg, st_g, gs in zip(cgs, sts, groups)]
        news = [lax.dot_general(bg, xd_b[:, gs], TN_DIMS, preferred_element_type=F32) for bg, gs in zip(bgs, groups)]
        for gs, st_g, new in zip(groups, sts, news):
            st[:, gs] = st_g * gam_e[:, gs] + new
        ms = [(cbs[h // hpg] * _decay(cm, h)).astype(BF16) for h in range(SSD_HEADS)]
        for h, hs in enumerate(heads):
            yd[:, hs] = jnp.dot(ms[h], xdt_b[:, hs], preferred_element_type=F32)
        y = yd[...] + jnp.concatenate(yoff, axis=1) + dsk_ref[...] * xsv
        y_ref[...] = y
        zz = z_ref[...]
        yg = y * (zz * _sigmoid(zz))
        outs = []
        for g in range(SSD_GROUPS):
            ygg = yg[:, gw * g:gw * (g + 1)]
            outs.append(ygg * _rms(ygg) * nw_ref[:, gw * g:gw * (g + 1)])
        ys_ref[...] = jnp.concatenate(outs, axis=1).astype(BF16)

    row = lambda d: pl.BlockSpec((None, ln, d), lambda bb, c: (bb, c, 0))
    return pl.pallas_call(
        body, name=name, grid=(b, nc),
        in_specs=[row(D_SSD), row(256), row(256), row(LANES), row(D_SSD), _full((1, LANES)), _full((1, LANES)),
                  _full((1, D_SSD)), _full((1, D_SSD)), _full((LANES, D_SSD))],
        out_specs=[row(D_SSD), row(D_SSD), pl.BlockSpec((None, None, SSD_STATE, D_SSD), lambda bb, c: (bb, c, 0, 0))],
        out_shape=[SDS((b, s, D_SSD), BF16), SDS((b, s, D_SSD), F32), SDS((b, nc, SSD_STATE, D_SSD), F32)],
        scratch_shapes=[pltpu.VMEM((SSD_STATE, D_SSD), F32), pltpu.VMEM((ln, D_SSD), F32)],
        compiler_params=_cparams(2))(xs, bm, cm_, misc, z, dtb, alog, dskip_e, norm_w, e_mat)


def ssd_bwd(dys, y, z, xs, bm, cm_, misc, prev, dtb, alog, dskip_e, norm_w, e_mat, et_mat, name):
    b, s, _ = xs.shape
    ln, nc = CHUNK, s // CHUNK
    gw = D_SSD // SSD_GROUPS
    hpg = SSD_HEADS // SSD_GROUPS

    def body(dys_ref, y_ref, z_ref, xs_ref, b_ref, c_ref, misc_ref, p_ref, dtb_ref, alog_ref, dsk_ref, nw_ref,
             e_ref, et_ref, dxs_ref, db_ref, dc_ref, dz_ref, ddt_ref, dnw_ref, ddsk_ref, ddtb_ref, dalog_ref,
             dst, dxd, dac_t):
        @pl.when(_first_step())
        def _():
            for r_ in (dnw_ref, ddsk_ref, ddtb_ref, dalog_ref):
                r_[...] = jnp.zeros_like(r_)

        @pl.when(pl.program_id(1) == 0)
        def _():
            dst[...] = jnp.zeros_like(dst)

        cm = _ssd_common(misc_ref, dtb_ref, alog_ref, e_ref)
        et = et_ref[...]
        squeeze = lambda t: _dot_01(t, et, 2)
        lane = lax.broadcasted_iota(I32, (ln, LANES), 1)
        sub = lax.broadcasted_iota(I32, (LANES, ln), 0)
        xsv = xs_ref[...]
        xdt = xsv * cm["dt_e"]
        xdt_b = xdt.astype(BF16)
        xd_b = (xdt * cm["del_e"]).astype(BF16)
        eac_e = cm["eac_e"]
        gam_e = eac_e[ln - 1:ln, :]

        yv, zz, dyo = y_ref[...], z_ref[...], dys_ref[...]
        sz = _sigmoid(zz)
        silu_z = zz * sz
        yg = yv * silu_z
        dyg, dnw = [], []
        for g in range(SSD_GROUPS):
            gs = slice(gw * g, gw * (g + 1))
            ygg = yg[:, gs]
            r = _rms(ygg)
            n = ygg * r
            dnw.append(jnp.sum(dyo[:, gs] * n, axis=0, keepdims=True))
            dyg.append(_rms_bwd(dyo[:, gs] * nw_ref[:, gs], n, r))
        dyg = jnp.concatenate(dyg, axis=1)
        dnw_ref[...] += jnp.concatenate(dnw, axis=1)
        dz_ref[...] = (dyg * yv * (sz * (1.0 + zz * (1.0 - sz)))).astype(BF16)
        dy = dyg * silu_z
        ddsk_ref[...] += jnp.sum(dy * xsv, axis=0, keepdims=True)
        dy_b = dy.astype(BF16)

        dacum = jnp.zeros((ln, LANES), F32)
        dac_t[...] = jnp.zeros_like(dac_t)
        w1, dgam = [], []
        for g in range(SSD_GROUPS):
            gs = slice(gw * g, gw * (g + 1))
            ss = slice(SSD_STATE * g, SSD_STATE * (g + 1))
            bg = b_ref[:, ss].astype(BF16)
            cg = c_ref[:, ss].astype(BF16)
            cb = lax.dot_general(cg, bg, NT_DIMS, preferred_element_type=F32)
            pt = p_ref[:, gs]
            pt_b = pt.astype(BF16)
            dst_g = dst[:, gs]
            dst_b = dst_g.astype(BF16)
            edy = (dy[:, gs] * eac_e[:, gs]).astype(BF16)
            dcg = lax.dot_general(edy, pt_b, NT_DIMS, preferred_element_type=F32)
            dpt = lax.dot_general(cg, edy, TN_DIMS, preferred_element_type=F32)
            yoff = jnp.dot(cg, pt_b, preferred_element_type=F32) * eac_e[:, gs]
            dxd_g = jnp.dot(bg, dst_b, preferred_element_type=F32)
            dbg = lax.dot_general(xd_b[:, gs], dst_b, NT_DIMS, preferred_element_type=F32)
            ddel = dxd_g * xdt[:, gs] * cm["del_e"][:, gs]
            w1.append(dy[:, gs] * yoff - ddel)
            dgam.append(jnp.sum(ddel, axis=0, keepdims=True) + jnp.sum(dst_g * pt, axis=0, keepdims=True) * gam_e[:, gs])
            dxd[:, gs] = dxd_g * cm["del_e"][:, gs]
            dst[:, gs] = dst_g * gam_e[:, gs] + dpt
            dcb = jnp.zeros((ln, ln), F32)
            for j in range(hpg):
                h = hpg * g + j
                hs = slice(SSD_HEAD_DIM * h, SSD_HEAD_DIM * (h + 1))
                lam = _decay(cm, h)
                m = cb * lam
                dm = lax.dot_general(dy_b[:, hs], xdt_b[:, hs], NT_DIMS, preferred_element_type=F32)
                dxd[:, hs] += lax.dot_general(m.astype(BF16), dy_b[:, hs], TN_DIMS, preferred_element_type=F32)
                dcb += dm * lam
                wl = dm * m
                dacum += jnp.where(lane == h, jnp.sum(wl, axis=1, keepdims=True), 0.0)
                dac_t[...] -= jnp.where(sub == h, jnp.sum(wl, axis=0, keepdims=True), 0.0)
            dcb_b = dcb.astype(BF16)
            dc_ref[:, ss] = dcg + jnp.dot(dcb_b, bg, preferred_element_type=F32)
            db_ref[:, ss] = dbg + lax.dot_general(dcb_b, cg, TN_DIMS, preferred_element_type=F32)

        dxdt = dxd[...]
        dxs_ref[...] = dy * dsk_ref[...] + dxdt * cm["dt_e"]
        dacum += squeeze(jnp.concatenate(w1, axis=1)) + dac_t[...].T
        dlast = squeeze(jnp.broadcast_to(jnp.concatenate(dgam, axis=1), (8, D_SSD)))[0:1, :]
        dacum += jnp.where(lax.broadcasted_iota(I32, (ln, LANES), 0) == ln - 1, dlast, 0.0)
        triu = (cm["ci"] >= cm["ri"]).astype(F32)
        da = jnp.dot(triu, dacum, preferred_element_type=F32, precision=HI)
        ddt = da * cm["a_neg"] + squeeze(dxdt * xsv)
        dalog_ref[...] += jnp.sum(da * cm["dt_s"], axis=0, keepdims=True) * cm["a_neg"]
        ddt_raw = jnp.where(lane < SSD_HEADS, ddt * _sigmoid(cm["pre"]), 0.0)
        ddt_ref[...] = ddt_raw
        ddtb_ref[...] += jnp.sum(ddt_raw, axis=0, keepdims=True)

    row = lambda d: pl.BlockSpec((None, ln, d), lambda bb, c: (bb, nc - 1 - c, 0))
    return pl.pallas_call(
        body, name=name, grid=(b, nc),
        in_specs=[row(D_SSD), row(D_SSD), row(D_SSD), row(D_SSD), row(256), row(256), row(LANES),
                  pl.BlockSpec((None, None, SSD_STATE, D_SSD), lambda bb, c: (bb, nc - 1 - c, 0, 0)),
                  _full((1, LANES)), _full((1, LANES)), _full((1, D_SSD)), _full((1, D_SSD)),
                  _full((LANES, D_SSD)), _full((D_SSD, LANES))],
        out_specs=[row(D_SSD), row(256), row(256), row(D_SSD), row(LANES),
                   _full((1, D_SSD)), _full((1, D_SSD)), _full((1, LANES)), _full((1, LANES))],
        out_shape=[SDS((b, s, D_SSD), F32), SDS((b, s, 256), F32), SDS((b, s, 256), F32), SDS((b, s, D_SSD), BF16),
                   SDS((b, s, LANES), F32), SDS((1, D_SSD), F32), SDS((1, D_SSD), F32), SDS((1, LANES), F32),
                   SDS((1, LANES), F32)],
        scratch_shapes=[pltpu.VMEM((SSD_STATE, D_SSD), F32), pltpu.VMEM((ln, D_SSD), F32), pltpu.VMEM((LANES, ln), F32)],
        compiler_params=_cparams(2))(dys, y, z, xs, bm, cm_, misc, prev, dtb, alog, dskip_e, norm_w, e_mat, et_mat)


def _rope(xv, cc, sp, sm):
    n = xv.shape[1]
    return xv * cc + pltpu.roll(xv, 16, 1) * sp + pltpu.roll(xv, n - 16, 1) * sm


def _rope_bwd(dy, cc, sp, sm):
    n = dy.shape[1]
    return dy * cc + pltpu.roll(dy * sp, n - 16, 1) + pltpu.roll(dy * sm, 16, 1)


def _tile8(t):
    return jnp.concatenate([t] * MLA_HEADS, axis=1)


def qkv_fwd(cq, ckv, misc, cc, sp, sm, qnw, kvnw, wuq_t, wukv_t, place, name):
    b, s, _ = cq.shape
    tm = _att_tile(s)
    hd = MLA_HEADS * HEAD_PAD

    def body(cq_ref, ckv_ref, misc_ref, cc_ref, sp_ref, sm_ref, qnw_ref, kvnw_ref, wq_ref, wkv_ref, pl_ref,
             q_ref, k_ref, v_ref, vt_ref, qn_ref, kvn_ref):
        cqv, ckvv = cq_ref[...], ckv_ref[...]
        qn = (cqv * _rms(cqv) * qnw_ref[...]).astype(BF16)
        kvn = (ckvv * _rms(ckvv) * kvnw_ref[...]).astype(BF16)
        qn_ref[...] = qn
        kvn_ref[...] = kvn
        cc1, sp1, sm1 = cc_ref[...], sp_ref[...], sm_ref[...]
        q = lax.dot_general(qn, wq_ref[...], NT_DIMS, preferred_element_type=F32)
        q_ref[...] = _rope(q, _tile8(cc1), _tile8(sp1), _tile8(sm1)).astype(BF16)
        kv = lax.dot_general(kvn, wkv_ref[...], NT_DIMS, preferred_element_type=F32)
        kr = jnp.dot(misc_ref[...], pl_ref[...], preferred_element_type=F32, precision=HI)
        kr = _rope(kr, cc1, sp1, sm1)
        k_ref[...] = (kv[:, 0:hd] + _tile8(kr)).astype(BF16)
        v_ref[...] = kv[:, hd:2 * hd].astype(BF16)
        for h in range(MLA_HEADS):
            vt_ref[h] = kv[:, hd + HEAD_PAD * h:hd + HEAD_PAD * (h + 1)].T.astype(BF16)

    return pl.pallas_call(
        body, name=name, grid=(b, s // tm),
        in_specs=[_row(tm, Q_LORA), _row(tm, KV_LORA), _row(tm, LANES), _row(tm, LANES), _row(tm, LANES), _row(tm, LANES),
                  _full((1, Q_LORA)), _full((1, KV_LORA)), _full(wuq_t.shape), _full(wukv_t.shape), _full((LANES, LANES))],
        out_specs=[_row(tm, hd), _row(tm, hd), _row(tm, hd),
                   pl.BlockSpec((None, MLA_HEADS, None, HEAD_PAD, tm), lambda bb, i: (bb, 0, i, 0, 0)),
                   _row(tm, Q_LORA), _row(tm, KV_LORA)],
        out_shape=[SDS((b, s, hd), BF16)] * 3 + [SDS((b, MLA_HEADS, s // tm, HEAD_PAD, tm), BF16),
                                                 SDS((b, s, Q_LORA), BF16), SDS((b, s, KV_LORA), BF16)],
        compiler_params=_cparams(2))(cq, ckv, misc, cc, sp, sm, qnw, kvnw, wuq_t, wukv_t, place)


def qkv_bwd(dq, dk, dv, ddt, cq, ckv, cc, sp, sm, qnw, kvnw, wuq_t, wukv_t, place_t, name):
    b, s, _ = cq.shape
    tm = min(512, s)
    hd = MLA_HEADS * HEAD_PAD

    def body(dq_ref, dk_ref, dv_ref, ddt_ref, cq_ref, ckv_ref, cc_ref, sp_ref, sm_ref, qnw_ref, kvnw_ref,
             wq_ref, wkv_ref, plt_ref, dcq_ref, dckv_ref, dmisc_ref, dqp_ref, dkv_ref, dqnw_ref, dkvnw_ref):
        @pl.when(_first_step())
        def _():
            dqnw_ref[...] = jnp.zeros_like(dqnw_ref)
            dkvnw_ref[...] = jnp.zeros_like(dkvnw_ref)
        cc1, sp1, sm1 = cc_ref[...], sp_ref[...], sm_ref[...]
        dqp = _rope_bwd(dq_ref[...].astype(F32), _tile8(cc1), _tile8(sp1), _tile8(sm1)).astype(BF16)
        dqp_ref[...] = dqp
        dkv_b = jnp.concatenate([dk_ref[...], dv_ref[...]], axis=1)
        dkf = dk_ref[...].astype(F32)
        dkv_ref[...] = dkv_b
        dkr = dkf[:, 0:HEAD_PAD]
        for h in range(1, MLA_HEADS):
            dkr += dkf[:, HEAD_PAD * h:HEAD_PAD * (h + 1)]
        dkr = _rope_bwd(dkr, cc1, sp1, sm1)
        dmisc_ref[...] = (jnp.dot(dkr, plt_ref[...], preferred_element_type=F32, precision=HI) + ddt_ref[...]).astype(BF16)

        def norm_bwd(dn_w, xv, w_ref, dw_ref, dx_ref):
            r = _rms(xv)
            n = xv * r
            dw_ref[...] += jnp.sum(dn_w * n, axis=0, keepdims=True)
            dx_ref[...] = _rms_bwd(dn_w * w_ref[...], n, r).astype(BF16)

        norm_bwd(jnp.dot(dqp, wq_ref[...], preferred_element_type=F32), cq_ref[...], qnw_ref, dqnw_ref, dcq_ref)
        norm_bwd(jnp.dot(dkv_b, wkv_ref[...], preferred_element_type=F32), ckv_ref[...], kvnw_ref, dkvnw_ref, dckv_ref)

    return pl.pallas_call(
        body, name=name, grid=(b, s // tm),
        in_specs=[_row(tm, hd), _row(tm, hd), _row(tm, hd), _row(tm, LANES), _row(tm, Q_LORA), _row(tm, KV_LORA),
                  _row(tm, LANES), _row(tm, LANES), _row(tm, LANES), _full((1, Q_LORA)), _full((1, KV_LORA)),
                  _full(wuq_t.shape), _full(wukv_t.shape), _full((LANES, LANES))],
        out_specs=[_row(tm, Q_LORA), _row(tm, KV_LORA), _row(tm, LANES), _row(tm, hd), _row(tm, 2 * hd),
                   _full((1, Q_LORA)), _full((1, KV_LORA))],
        out_shape=[SDS((b, s, Q_LORA), BF16), SDS((b, s, KV_LORA), BF16), SDS((b, s, LANES), BF16),
                   SDS((b, s, hd), BF16), SDS((b, s, 2 * hd), BF16), SDS((1, Q_LORA), F32), SDS((1, KV_LORA), F32)],
        compiler_params=_cparams(2))(dq, dk, dv, ddt, cq, ckv, cc, sp, sm, qnw, kvnw, wuq_t, wukv_t, place_t)


ATT_SCALE = 1.0 / math.sqrt(QK_DIM)
LOG2E = math.log2(math.e)
ATT_SCALE_LOG2E = ATT_SCALE * LOG2E


ATT_HEADS_PER_STEP = 4
ATT_HEADS_PER_STEP_BWD = 2


def _att_tile(s):
    return min(512, s)


def flash_fwd(q, k, vt, name):
    b, s, hd = q.shape
    t = _att_tile(s)
    nb = s // t
    th = t // 2

    hps = ATT_HEADS_PER_STEP
    hw = hps * HEAD_PAD

    def body(q_ref, k_ref, vt_ref, o_ref, lse_ref, m_s, l_s, acc):
        i = pl.program_id(2)
        m_s[...] = jnp.full_like(m_s, -jnp.inf)
        l_s[...] = jnp.zeros_like(l_s)
        acc[...] = jnp.zeros_like(acc)

        def update(j, diagonal):
            chains = [(hh, half) for hh in range(hps) for half in range(2)]
            lanes = lambda hh: slice(HEAD_PAD * hh, HEAD_PAD * (hh + 1))
            cols = lambda half: slice(th * half, th * (half + 1))
            sts = {}
            nkeys = lambda half: th if diagonal and half == 0 else t
            for hh, half in chains:
                kr = pl.ds(pl.multiple_of(j * t, t), nkeys(half))
                st = lax.dot_general(k_ref[kr, lanes(hh)], q_ref[cols(half), lanes(hh)], NT_DIMS,
                                     preferred_element_type=F32)
                if diagonal:
                    row = lax.broadcasted_iota(I32, (nkeys(half), th), 0)
                    col = lax.broadcasted_iota(I32, (nkeys(half), th), 1) + th * half
                    st = jnp.where(row <= col, st, -jnp.inf)
                sts[hh, half] = st
            pts, alphas = {}, {}
            for hh, half in chains:
                st, cs = sts[hh, half], cols(half)
                m_prev = m_s[hh, :, cs]
                m_new = jnp.maximum(m_prev, jnp.max(st, axis=0, keepdims=True))
                alpha = jnp.exp2((m_prev - m_new) * ATT_SCALE_LOG2E)
                pt = jnp.exp2((st - m_new) * ATT_SCALE_LOG2E)
                l_s[hh, :, cs] = alpha * l_s[hh, :, cs] + jnp.sum(pt, axis=0, keepdims=True)
                m_s[hh, :, cs] = m_new
                pts[hh, half], alphas[hh, half] = pt.astype(BF16), alpha
            for hh, half in chains:
                cs = cols(half)
                acc[hh, :, cs] = alphas[hh, half] * acc[hh, :, cs] + jnp.dot(
                    vt_ref[hh, j, :, 0:nkeys(half)], pts[hh, half], preferred_element_type=F32)

        def step(j, carry):
            update(j, False)
            return carry

        lax.fori_loop(0, i, step, 0)
        update(i, True)
        for hh in range(hps):
            o_ref[:, HEAD_PAD * hh:HEAD_PAD * (hh + 1)] = (acc[hh] / l_s[hh]).T
            lse_ref[hh] = m_s[hh] * ATT_SCALE + jnp.log(l_s[hh])

    qs = pl.BlockSpec((None, t, hw), lambda bb, h, i: (bb, i, h))
    ks = pl.BlockSpec((None, s, hw), lambda bb, h, i: (bb, 0, h))
    vs = pl.BlockSpec((None, hps, nb, HEAD_PAD, t), lambda bb, h, i: (bb, h, 0, 0, 0))
    ls = pl.BlockSpec((None, hps, None, 1, t), lambda bb, h, i: (bb, h, i, 0, 0))
    return pl.pallas_call(
        body, name=name, grid=(b, MLA_HEADS // hps, nb),
        in_specs=[qs, ks, vs], out_specs=[qs, ls],
        out_shape=[SDS((b, s, hd), F32), SDS((b, MLA_HEADS, nb, 1, t), F32)],
        scratch_shapes=[pltpu.VMEM((hps, 1, t), F32), pltpu.VMEM((hps, 1, t), F32), pltpu.VMEM((hps, HEAD_PAD, t), F32)],
        compiler_params=_cparams(3))(q, k, vt)


def flash_bwd(q, k, v, do, lse, dlt, name):
    b, s, hd = q.shape
    t = _att_tile(s)
    nb = s // t
    th = t // 2
    lse_r = lse
    dlt_r = dlt.reshape(b, MLA_HEADS, nb, 1, t)

    hps = ATT_HEADS_PER_STEP_BWD
    hw = hps * HEAD_PAD

    def body(q_ref, k_ref, v_ref, do_ref, lse_ref, dlt_ref, dq_ref, dk_ref, dv_ref, dq_s, dk_s, dv_s):
        dq_s[...] = jnp.zeros_like(dq_s)
        dk_s[...] = jnp.zeros_like(dk_s)
        dv_s[...] = jnp.zeros_like(dv_s)

        def tile(j, i, diagonal):
            chains = [(hh, half) for hh in range(hps) for half in range(2)]
            lanes = lambda hh: slice(HEAD_PAD * hh, HEAD_PAD * (hh + 1))
            keys = lambda half: pl.ds(pl.multiple_of(j * t + th * half, th), th)
            q0 = lambda half: th if diagonal and half == 1 else 0
            qsel = lambda half: pl.ds(pl.multiple_of(i * t + q0(half), th), t - q0(half))
            sts, dpts = {}, {}
            for hh, half in chains:
                ls_, ks, qs, nq = lanes(hh), keys(half), qsel(half), t - q0(half)
                st = lax.dot_general(k_ref[ks, ls_], q_ref[qs, ls_], NT_DIMS, preferred_element_type=F32)
                if diagonal:
                    row = lax.broadcasted_iota(I32, (th, nq), 0) + th * half
                    col = lax.broadcasted_iota(I32, (th, nq), 1) + q0(half)
                    st = jnp.where(row <= col, st, -jnp.inf)
                sts[hh, half] = st
                dpts[hh, half] = lax.dot_general(v_ref[ks, ls_], do_ref[qs, ls_], NT_DIMS, preferred_element_type=F32)
            pts, dsts = {}, {}
            for hh, half in chains:
                qcols = slice(q0(half), t)
                pt = jnp.exp2(sts[hh, half] * ATT_SCALE_LOG2E - lse_ref[hh, i][:, qcols] * LOG2E)
                pts[hh, half] = pt.astype(BF16)
                dsts[hh, half] = (pt * (dpts[hh, half] - dlt_ref[hh, i][:, qcols])).astype(BF16)
            for hh, half in chains:
                ls_, ks, qs = lanes(hh), keys(half), qsel(half)
                dv_s[ks, ls_] += jnp.dot(pts[hh, half], do_ref[qs, ls_], preferred_element_type=F32)
                dk_s[ks, ls_] += jnp.dot(dsts[hh, half], q_ref[qs, ls_], preferred_element_type=F32)
                dq_s[qs, ls_] += lax.dot_general(dsts[hh, half], k_ref[ks, ls_], TN_DIMS, preferred_element_type=F32)

        def key_tile(j, carry):
            tile(j, j, True)

            def query_tile(i, c2):
                tile(j, i, False)
                return c2

            lax.fori_loop(j + 1, nb, query_tile, 0)
            return carry

        lax.fori_loop(0, nb, key_tile, 0)
        dq_ref[...] = (dq_s[...] * ATT_SCALE).astype(BF16)
        dk_ref[...] = (dk_s[...] * ATT_SCALE).astype(BF16)
        dv_ref[...] = dv_s[...].astype(BF16)

    hs = pl.BlockSpec((None, s, hw), lambda bb, h: (bb, 0, h))
    ls = pl.BlockSpec((None, hps, nb, 1, t), lambda bb, h: (bb, h, 0, 0, 0))
    return pl.pallas_call(
        body, name=name, grid=(b, MLA_HEADS // hps),
        in_specs=[hs, hs, hs, hs, ls, ls], out_specs=[hs, hs, hs],
        out_shape=[SDS((b, s, hd), BF16)] * 3, scratch_shapes=[pltpu.VMEM((s, hw), F32)] * 3,
        compiler_params=_cparams(2))(q, k, v, do, lse_r, dlt_r)


def out_proj(ys, attn, mnw, wo, x, gate, above, name):
    b, s, d = x.shape
    tm = min(512, s)

    def body(ys_ref, at_ref, mnw_ref, wo_ref, x_ref, g_ref, nw_ref, sc_ref, sh_ref, xn_ref, o_ref, ym_ref, h_ref):
        av = at_ref[...]
        ym = (av * _rms(av) * mnw_ref[...]).astype(BF16)
        ym_ref[...] = ym
        o = jnp.dot(ys_ref[...], wo_ref[0:D_SSD, :], preferred_element_type=F32)
        o += jnp.dot(ym, wo_ref[D_SSD:2 * D_SSD, :], preferred_element_type=F32)
        xn = x_ref[...] + g_ref[...] * o
        xn_ref[...] = xn
        o_ref[...] = o.astype(BF16)
        h_ref[...] = _norm_mod_tile(xn, nw_ref, sc_ref, sh_ref)

    return pl.pallas_call(
        body, name=name, grid=(b, s // tm),
        in_specs=[_row(tm, D_SSD), _row(tm, D_SSD), _full((1, D_SSD)), _full(wo.shape), _row(tm, d), _bvec(d),
                  _full((1, d)), _bvec(d), _bvec(d)],
        out_specs=[_row(tm, d), _row(tm, d), _row(tm, D_SSD), _row(tm, d)],
        out_shape=[SDS((b, s, d), F32), SDS((b, s, d), BF16), SDS((b, s, D_SSD), BF16), SDS((b, s, d), BF16)],
        compiler_params=_cparams(2))(ys, attn, mnw, wo, x, gate, *above)


def out_proj_bwd(dout, attn, mnw, wo, name):
    b, s, d = dout.shape
    tm = min(512, s)

    def body(do_ref, at_ref, mnw_ref, wo_ref, dys_ref, dat_ref, dlt_ref, dw_ref):
        lane = lax.broadcasted_iota(I32, (tm, LANES), 1)
        @pl.when(_first_step())
        def _():
            dw_ref[...] = jnp.zeros_like(dw_ref)
        dov = do_ref[...]
        dys_ref[...] = lax.dot_general(dov, wo_ref[0:D_SSD, :], NT_DIMS, preferred_element_type=F32)
        dym = lax.dot_general(dov, wo_ref[D_SSD:2 * D_SSD, :], NT_DIMS, preferred_element_type=F32)
        av = at_ref[...]
        r = _rms(av)
        n = av * r
        dw_ref[...] += jnp.sum(dym * n, axis=0, keepdims=True)
        dat = _rms_bwd(dym * mnw_ref[...], n, r)
        dat_ref[...] = dat.astype(BF16)
        prod = dat * av
        cols = jnp.zeros((tm, LANES), F32)
        for h in range(MLA_HEADS):
            cols += jnp.where(lane == h, jnp.sum(prod[:, HEAD_PAD * h:HEAD_PAD * (h + 1)], axis=1, keepdims=True), 0.0)
        dlt_ref[...] = cols.T[0:MLA_HEADS, :]

    return pl.pallas_call(
        body, name=name, grid=(b, s // tm),
        in_specs=[_row(tm, d), _row(tm, D_SSD), _full((1, D_SSD)), _full(wo.shape)],
        out_specs=[_row(tm, D_SSD), _row(tm, D_SSD),
                   pl.BlockSpec((None, MLA_HEADS, tm), lambda bb, i: (bb, 0, i)), _full((1, D_SSD))],
        out_shape=[SDS((b, s, D_SSD), F32), SDS((b, s, D_SSD), BF16), SDS((b, MLA_HEADS, s), F32),
                   SDS((1, D_SSD), F32)],
        compiler_params=_cparams(2))(dout, attn, mnw, wo)


def adaln_fwd(c_all, w_ada, b_ada, name):
    nb, d = c_all.shape
    n = w_ada.shape[1]

    def body(c_ref, w_ref, b_ref, m_ref, ca_ref):
        cv = c_ref[...]
        ca = (cv * _sigmoid(cv)).astype(BF16)
        ca_ref[...] = ca
        m_ref[...] = jnp.dot(ca, w_ref[...].astype(BF16), preferred_element_type=F32) + b_ref[...]

    return pl.pallas_call(
        body, name=name, out_shape=[SDS((nb, n), F32), SDS((nb, d), BF16)],
        compiler_params=pltpu.CompilerParams(vmem_limit_bytes=VMEM_LIMIT))(c_all, w_ada, b_ada)


def adaln_bwd(c_act, dmod_cols, name):
    d, n = c_act.shape[1], dmod_cols.shape[1]

    def body(c_ref, dm_ref, gw_ref):
        gw_ref[...] = lax.dot_general(c_ref[...], dm_ref[...].astype(BF16), TN_DIMS, preferred_element_type=F32)

    return pl.pallas_call(
        body, name=name, out_shape=SDS((d, n), F32),
        compiler_params=pltpu.CompilerParams(vmem_limit_bytes=VMEM_LIMIT))(c_act, dmod_cols)


def sum_rows(x, name):
    def body(x_ref, o_ref):
        o_ref[...] = jnp.sum(x_ref[...], axis=0, keepdims=True)
    return pl.pallas_call(body, name=name, out_shape=SDS((1, x.shape[1]), F32))(x)


def squeeze_heads(x, et_mat, name):
    def body(x_ref, et_ref, o_ref):
        xv = jnp.broadcast_to(x_ref[...], (8, x.shape[1]))
        o_ref[...] = _dot_01(xv, et_ref[...], 3)[0:1, :]
    return pl.pallas_call(body, name=name, out_shape=SDS((1, LANES), F32))(x, et_mat)


def sum_blocks(x, name):
    n, r, c = x.shape
    tr = next(cand for cand in (256, 128, 64, 32, 16, 8) if r % cand == 0)

    def body(x_ref, o_ref):
        acc = x_ref[0].astype(F32)
        for k in range(1, n):
            acc += x_ref[k].astype(F32)
        o_ref[...] = acc

    return pl.pallas_call(
        body, name=name, grid=(r // tr,), in_specs=[pl.BlockSpec((n, tr, c), lambda i: (0, i, 0))],
        out_specs=pl.BlockSpec((tr, c), lambda i: (i, 0)), out_shape=SDS((r, c), F32),
        compiler_params=_cparams(1))(x)


def _adam_math(w, g, m, v):
    m = ADAM_B1 * m + (1.0 - ADAM_B1) * g
    v = ADAM_B2 * v + (1.0 - ADAM_B2) * (g * g)
    m_hat = m / (1.0 - ADAM_B1 ** ADAM_STEP)
    v_hat = v / (1.0 - ADAM_B2 ** ADAM_STEP)
    return -ADAM_LR * (m_hat / (jnp.sqrt(v_hat) + ADAM_EPS) + ADAM_WD * w), m, v


def adamw(w, g, m, v, name):
    r, c = w.shape[-2:]
    tr = r
    for cand in (512, 256, 128, 64, 32, 16, 8):
        if r % cand == 0 and cand * c * 4 <= 2 * 1024 * 1024:
            tr = cand
            break

    def body(w_ref, g_ref, m_ref, v_ref, d_ref, mo_ref, vo_ref):
        d_ref[...], mo_ref[...], vo_ref[...] = _adam_math(w_ref[...], g_ref[...], m_ref[...], v_ref[...])

    def spec(a):
        return pl.BlockSpec((tr, c), lambda i: (i, 0)) if a.ndim == 2 else pl.BlockSpec((None, tr, c), lambda i: (0, i, 0))

    return pl.pallas_call(
        body, name=name, grid=(r // tr,), in_specs=[spec(w), spec(g), spec(m), spec(v)], out_specs=[spec(w)] * 3,
        out_shape=[SDS(w.shape, F32)] * 3, compiler_params=_cparams(1))(w, g, m, v)


def adamw_blocks(w, blocks, m, v, name):
    r, c = w.shape
    tr = next((cand for cand in (128, 64, 32, 16, 8) if r % cand == 0), r)

    def body(w_ref, b_ref, m_ref, v_ref, g_ref, d_ref, mo_ref, vo_ref):
        g = b_ref[0].astype(F32)
        for k in range(1, N_DEV):
            g += b_ref[k].astype(F32)
        g_ref[...] = g
        d_ref[...], mo_ref[...], vo_ref[...] = _adam_math(w_ref[...], g, m_ref[...], v_ref[...])

    spec = pl.BlockSpec((tr, c), lambda i: (i, 0))
    return pl.pallas_call(
        body, name=name, grid=(r // tr,),
        in_specs=[spec, pl.BlockSpec((N_DEV, tr, c), lambda i: (0, i, 0)), spec, spec], out_specs=[spec] * 4,
        out_shape=[SDS((r, c), F32)] * 4, compiler_params=_cparams(1))(w, blocks, m, v)


def adamw_many(ws, gs, ms, vs, name):
    n = len(ws)

    def body(*refs):
        w_r, g_r, m_r, v_r = (refs[k * n:(k + 1) * n] for k in range(4))
        d_r, mo_r, vo_r = (refs[(4 + k) * n:(5 + k) * n] for k in range(3))
        for k in range(n):
            d_r[k][...], mo_r[k][...], vo_r[k][...] = _adam_math(w_r[k][...], g_r[k][...], m_r[k][...], v_r[k][...])

    shapes = [SDS(w.shape, F32) for w in ws]
    outs = pl.pallas_call(body, name=name, out_shape=shapes * 3)(*ws, *gs, *ms, *vs)
    return outs[:n], outs[n:2 * n], outs[2 * n:]


TRANSPOSED = ("ffn1_w_gate", "ffn1_w_up", "ffn2_w_gate", "ffn2_w_up", "w_in", "w_ukv", "w_uq")
GATHER_GROUPS = (("ffn1_w_gate", "ffn1_w_up"), ("ffn1_w_down",),
                 ("w_in", "w_ukv", "w_uq", "w_out", "ffn2_w_gate", "ffn2_w_up", "ffn2_w_down"))
GRAD_GROUPS = (("ffn2", ("ffn2_w_gate", "ffn2_w_up", "ffn2_w_down")), ("mixer", ("w_out", "w_in", "w_ukv", "w_uq")),
               ("ffn1_down", ("ffn1_w_down",)), ("ffn1_gate", ("ffn1_w_gate",)), ("ffn1_up", ("ffn1_w_up",)))


def _shard_view(name, w):
    return w[0].T if name in TRANSPOSED else w[0]


def _shard_unview(name, t):
    return t.T[None] if name in TRANSPOSED else t[None]


def _grad_blocks(name, gw):
    if name == "w_in":
        return _in_proj_rows_inv(gw).reshape(N_DEV, -1, D_MODEL)
    if name == "w_ukv":
        hd = MLA_HEADS * HEAD_PAD
        return jnp.concatenate([gw[:hd].reshape(MLA_HEADS, HEAD_PAD, KV_LORA)[:, :QK_NOPE],
                                gw[hd:].reshape(MLA_HEADS, V_HEAD, KV_LORA)], axis=1)
    if name == "w_uq":
        return gw.reshape(MLA_HEADS, HEAD_PAD, Q_LORA)[:, :QK_DIM]
    return gw.reshape(N_DEV, -1, D_MODEL)


def _pack_rows(arrs):
    parts = []
    for a in arrs:
        flat = a.reshape(-1).astype(F32)
        pad = (-flat.shape[0]) % D_MODEL
        if pad:
            flat = jnp.pad(flat, (0, pad))
        parts.append(flat.reshape(-1, D_MODEL))
    out = jnp.concatenate(parts, axis=0)
    pad = (-out.shape[0]) % 8
    if pad:
        out = jnp.pad(out, ((0, pad), (0, 0)))
    return out


def _unpack_rows(packed, shapes):
    out, row = [], 0
    for shp in shapes:
        n = math.prod(shp)
        nrow = -(-n // D_MODEL)
        out.append(packed[row:row + nrow].reshape(-1)[:n].reshape(shp))
        row += nrow
    return out


def _in_proj_rows(w_t):
    return jnp.concatenate([w_t[0:2560], w_t[2576:2960], w_t[2960:3216], w_t[2560:2576], w_t[3216:3248],
                            jnp.zeros((D_IN_PAD - D_IN, D_MODEL), w_t.dtype)], axis=0)


def _in_proj_rows_inv(d):
    return jnp.concatenate([d[0:2560], d[3200:3216], d[2560:2944], d[2944:3200], d[3216:3248]], axis=0)


def _rope_tables(positions):
    inv_freq = ROPE_THETA ** (-jnp.arange(0, QK_ROPE, 2, dtype=F32) / QK_ROPE)
    ang = positions[..., None].astype(F32) * inv_freq
    cos, sin = jnp.cos(ang), jnp.sin(ang)
    one = jnp.ones(ang.shape[:2] + (QK_NOPE,), F32)
    zero = jnp.zeros_like(one)
    z16, z32, o32 = zero[..., :16], zero[..., :32], one[..., :32]
    cc = jnp.concatenate([one, cos, cos, o32], axis=-1)
    sp = jnp.concatenate([zero, z16, sin, z32], axis=-1)
    sm = jnp.concatenate([zero, -sin, z16, z32], axis=-1)
    return cc, sp, sm


def weight_views(gathered):
    full = lambda name: gathered[name].reshape(-1, gathered[name].shape[2])
    ukv = full("w_ukv").reshape(MLA_HEADS, QK_NOPE + V_HEAD, KV_LORA)
    wukv_t = jnp.concatenate([jnp.pad(ukv[:, :QK_NOPE], ((0, 0), (0, HEAD_PAD - QK_NOPE), (0, 0))).reshape(-1, KV_LORA),
                              ukv[:, QK_NOPE:].reshape(-1, KV_LORA)], axis=0)
    uq = full("w_uq").reshape(MLA_HEADS, QK_DIM, Q_LORA)
    wuq_t = jnp.pad(uq, ((0, 0), (0, HEAD_PAD - QK_DIM), (0, 0))).reshape(-1, Q_LORA)
    return dict(wg1_t=full("ffn1_w_gate"), wu1_t=full("ffn1_w_up"), wd1=full("ffn1_w_down"),
                wg2_t=full("ffn2_w_gate"), wu2_t=full("ffn2_w_up"), wd2=full("ffn2_w_down"),
                wo=full("w_out"), win_t=_in_proj_rows(full("w_in")), wukv_t=wukv_t, wuq_t=wuq_t)


def _ffn_bwd(tag, dxn, do, dgate, x, h, gg, uu, a, sc, norm_w, wg_t, wu_t, wd, below):
    f2 = wd.shape[0] // 2
    dwd = mm_tn(a, do, f2, D_MODEL, tag + "_dwd")
    dgg, duu = ffn_dact(do, wd, gg, uu, tag + "_dact")
    dwg_t = mm_tn(dgg, h, f2, D_MODEL, tag + "_dwg")
    dwu_t = mm_tn(duu, h, f2, D_MODEL, tag + "_dwu")
    dx, dsc, dsh, dnw, *nxt = dh_norm_bwd([dgg, duu], [wg_t, wu_t], x, dxn, norm_w, sc, tag + "_dh", below)
    return dx, (dsh, dsc, dgate), dnw, (dwg_t, dwu_t, dwd), nxt


def local_step(x, tgt, positions, mod, wv, p):
    nb, s, d = x.shape
    sh1, sc1, g1, sh2, sc2, g2, sh3, sc3, g3 = mod
    cc, sp, sm = _rope_tables(positions)
    lane_head = jnp.arange(D_SSD, dtype=I32)[None, :] // SSD_HEAD_DIM
    e_mat = (lane_head == jnp.arange(LANES, dtype=I32)[:, None]).astype(BF16)
    et_mat = e_mat.T
    rr, cl = jnp.arange(LANES, dtype=I32)[:, None], jnp.arange(LANES, dtype=I32)[None, :]
    place = ((cl == rr + (QK_NOPE - SSD_HEADS)) & (rr >= SSD_HEADS) & (rr < SSD_HEADS + QK_ROPE)).astype(F32)
    dtb = jnp.pad(p["dt_bias"], ((0, 0), (0, LANES - SSD_HEADS)))
    alog = jnp.pad(p["a_log"], ((0, 0), (0, LANES - SSD_HEADS)))
    dskip_e = jnp.repeat(p["d_skip"], SSD_HEAD_DIM, axis=1)

    h1 = norm_mod(x, p["norm_ffn1"], sc1, sh1, "ffn1_norm")
    gg1, uu1, a1 = ffn_up(h1, wv["wg1_t"], wv["wu1_t"], "ffn1_up")
    x1, o1, h2 = ffn_down(a1, wv["wd1"], x, g1, 0.5, "ffn1_down", (p["norm_mix"], sc2, sh2))
    z, u, cq, ckv, misc = in_proj(h2, wv["win_t"], "in_proj")
    xs, bm, cm_ = conv_fwd(u, p["conv_w"], p["conv_b"], "conv_fwd")
    ys, y, prev = ssd_fwd(xs, bm, cm_, misc, z, dtb, alog, dskip_e, p["ssd_norm_w"], e_mat, "ssd_fwd")
    q, k, v, vt, qn, kvn = qkv_fwd(cq, ckv, misc, cc, sp, sm, p["q_norm_w"], p["kv_norm_w"], wv["wuq_t"], wv["wukv_t"],
                               place, "qkv_fwd")
    attn, lse = flash_fwd(q, k, vt, "flash_fwd")
    x2, o2, ym, h3 = out_proj(ys, attn, p["mla_norm_w"], wv["wo"], x1, g2, (p["norm_ffn2"], sc3, sh3), "out_proj")
    gg3, uu3, a3 = ffn_up(h3, wv["wg2_t"], wv["wu2_t"], "ffn2_up")
    loss, dx3, dnfin, do3, dg3 = ffn_down_final(a3, wv["wd2"], x2, g3, 0.5, p["norm_final"], tgt, "ffn2_down_loss")

    dx2, dmod3, dnf2, (dwg2, dwu2, dwd2), (dout, dg2) = _ffn_bwd(
        "ffn2", dx3, do3, dg3, x2, h3, gg3, uu3, a3, sc3, p["norm_ffn2"], wv["wg2_t"], wv["wu2_t"], wv["wd2"],
        (o2, g2, 1.0))
    dys, dattn, dlt, dmlan = out_proj_bwd(dout, attn, p["mla_norm_w"], wv["wo"], "out_proj_bwd")
    dwo = jnp.concatenate([mm_tn(ys, dout, D_SSD, D_MODEL, "dwo_ssd"), mm_tn(ym, dout, D_SSD, D_MODEL, "dwo_mla")], axis=0)
    dxs, dbm, dcm, dz, ddt, dssdn, ddsk_lane, ddtb, dalog = ssd_bwd(
        dys, y, z, xs, bm, cm_, misc, prev, dtb, alog, dskip_e, p["ssd_norm_w"], e_mat, et_mat, "ssd_bwd")
    dq, dk, dv = flash_bwd(q, k, v, dattn, lse, dlt, "flash_bwd")
    dcq, dckv, dmisc, dqp, dkvc, dqn, dkvn = qkv_bwd(dq, dk, dv, ddt, cq, ckv, cc, sp, sm, p["q_norm_w"], p["kv_norm_w"],
                                                     wv["wuq_t"], wv["wukv_t"], place.T, "qkv_bwd")
    dwuq = mm_tn(dqp, qn, MLA_HEADS * HEAD_PAD, Q_LORA, "dwuq")
    dwukv = mm_tn(dkvc, kvn, MLA_HEADS * HEAD_PAD, KV_LORA, "dwukv")
    dvv, dconv = conv_bwd_a(dxs, dbm, dcm, u, p["conv_w"], p["conv_b"], "conv_bwd_a")
    dconv = conv_grads_fold(dconv, "conv_grads_fold")
    du = conv_bwd_b(dvv, p["conv_w"], "conv_bwd_b")
    dproj = [dz, du, dcq, dckv, dmisc]
    dwin = mm_tn_blocks(dproj, h2, "dwin")
    dx1, dsc2, dsh2, dnmix, do1, dg1 = dh_norm_bwd(dproj, [wv["win_t"]], x1, dx2, p["norm_mix"], sc2, "mix_dh",
                                                   (o1, g1, 0.5))
    dx0, dmod1, dnf1, (dwg1, dwu1, dwd1), _ = _ffn_bwd(
        "ffn1", dx1, do1, dg1, x, h1, gg1, uu1, a1, sc1, p["norm_ffn1"], wv["wg1_t"], wv["wu1_t"], wv["wd1"], None)

    dmod = jnp.concatenate([*dmod1, dsh2, dsc2, dg2, *dmod3], axis=1).reshape(nb, N_MOD * d)
    return dict(
        loss=loss, dx=dx0, dmod=dmod, norm_ffn1=dnf1, norm_mix=dnmix, norm_ffn2=dnf2, norm_final=dnfin,
        ssd_norm_w=dssdn, mla_norm_w=dmlan, q_norm_w=dqn, kv_norm_w=dkvn,
        dt_bias=ddtb[:, :SSD_HEADS], a_log=dalog[:, :SSD_HEADS],
        d_skip=squeeze_heads(ddsk_lane, et_mat, "d_skip_heads")[:, :SSD_HEADS],
        conv_b=dconv[4:5], conv_w=dconv[0:4],
        gw=dict(ffn1_w_gate=dwg1, ffn1_w_up=dwu1, ffn1_w_down=dwd1, ffn2_w_gate=dwg2, ffn2_w_up=dwu2, ffn2_w_down=dwd2,
                w_out=dwo, w_in=dwin, w_ukv=dwukv, w_uq=dwuq))


def kernel(x, c, positions, w_ada, b_ada, norm_ffn1, ffn1_w_gate, ffn1_w_up, ffn1_w_down, norm_mix, w_in, conv_w, conv_b, dt_bias, a_log, d_skip, ssd_norm_w, q_norm_w, w_uq, kv_norm_w, w_ukv, mla_norm_w, w_out, norm_ffn2, ffn2_w_gate, ffn2_w_up, ffn2_w_down, norm_final, loss_target, m_w_ada, m_b_ada, m_norm_ffn1, m_ffn1_w_gate, m_ffn1_w_up, m_ffn1_w_down, m_norm_mix, m_w_in, m_conv_w, m_conv_b, m_dt_bias, m_a_log, m_d_skip, m_ssd_norm_w, m_q_norm_w, m_w_uq, m_kv_norm_w, m_w_ukv, m_mla_norm_w, m_w_out, m_norm_ffn2, m_ffn2_w_gate, m_ffn2_w_up, m_ffn2_w_down, m_norm_final, v_w_ada, v_b_ada, v_norm_ffn1, v_ffn1_w_gate, v_ffn1_w_up, v_ffn1_w_down, v_norm_mix, v_w_in, v_conv_w, v_conv_b, v_dt_bias, v_a_log, v_d_skip, v_ssd_norm_w, v_q_norm_w, v_w_uq, v_kv_norm_w, v_w_ukv, v_mla_norm_w, v_w_out, v_norm_ffn2, v_ffn2_w_gate, v_ffn2_w_up, v_ffn2_w_down, v_norm_final):
    names = ["w_ada", "b_ada", "norm_ffn1", "ffn1_w_gate", "ffn1_w_up", "ffn1_w_down", "norm_mix", "w_in", "conv_w",
             "conv_b", "dt_bias", "a_log", "d_skip", "ssd_norm_w", "q_norm_w", "w_uq", "kv_norm_w", "w_ukv",
             "mla_norm_w", "w_out", "norm_ffn2", "ffn2_w_gate", "ffn2_w_up", "ffn2_w_down", "norm_final"]
    W = dict(zip(names, (w_ada, b_ada, norm_ffn1, ffn1_w_gate, ffn1_w_up, ffn1_w_down, norm_mix, w_in, conv_w, conv_b, dt_bias, a_log, d_skip, ssd_norm_w, q_norm_w, w_uq, kv_norm_w, w_ukv, mla_norm_w, w_out, norm_ffn2, ffn2_w_gate, ffn2_w_up, ffn2_w_down, norm_final)))
    M = dict(zip(names, (m_w_ada, m_b_ada, m_norm_ffn1, m_ffn1_w_gate, m_ffn1_w_up, m_ffn1_w_down, m_norm_mix, m_w_in, m_conv_w, m_conv_b, m_dt_bias, m_a_log, m_d_skip, m_ssd_norm_w, m_q_norm_w, m_w_uq, m_kv_norm_w, m_w_ukv, m_mla_norm_w, m_w_out, m_norm_ffn2, m_ffn2_w_gate, m_ffn2_w_up, m_ffn2_w_down, m_norm_final)))
    V = dict(zip(names, (v_w_ada, v_b_ada, v_norm_ffn1, v_ffn1_w_gate, v_ffn1_w_up, v_ffn1_w_down, v_norm_mix, v_w_in, v_conv_w, v_conv_b, v_dt_bias, v_a_log, v_d_skip, v_ssd_norm_w, v_q_norm_w, v_w_uq, v_kv_norm_w, v_w_ukv, v_mla_norm_w, v_w_out, v_norm_ffn2, v_ffn2_w_gate, v_ffn2_w_up, v_ffn2_w_down, v_norm_final)))

    nb, s, d = x.shape
    me = 4 * lax.axis_index("x") + 2 * lax.axis_index("y") + lax.axis_index("c")
    n_ada = w_ada.shape[2]

    taps, n_cw = conv_w.shape[1:]
    (cg,) = all_gather8([_pack_rows([c, conv_w[0]])], "gather_c")
    c_all = cg[:, 0:nb].reshape(N_DEV * nb, d)
    conv_w_full = cg[:, nb, 0:taps * n_cw].reshape(N_DEV, taps, n_cw).transpose(1, 0, 2).reshape(taps, N_DEV * n_cw)
    shards = [[_shard_view(name, W[name]).astype(BF16) for name in group] for group in GATHER_GROUPS]
    gathered = dict(zip(GATHER_GROUPS[0], all_gather8(shards[0], "gather_w_ffn1")))

    b_ada_cols = lax.dynamic_slice(b_ada, (0, me * n_ada), (1, n_ada))
    mod_cols, c_act = adaln_fwd(c_all, w_ada[0], b_ada_cols, "adaln_fwd")
    (mod_g,) = all_gather8([mod_cols], "gather_mod")
    gathered, mod_g, shards = lax.optimization_barrier((gathered, mod_g, shards))
    gathered.update(zip(GATHER_GROUPS[1], sc_all_gather8(shards[1], "gather_w_ffn1_down", 1)))
    gathered.update(zip(GATHER_GROUPS[2], sc_all_gather8(shards[2], "gather_w_rest", 7)))
    wv = weight_views(gathered)
    mod = lax.dynamic_slice(mod_g, (0, me * nb, 0), (N_DEV, nb, n_ada)).transpose(1, 0, 2).reshape(nb, N_MOD, 1, d)
    mod = [mod[:, k] for k in range(N_MOD)]

    P = dict(W)
    P["conv_w"] = conv_w_full
    P["norm_final"] = norm_final.reshape(1, d)
    R = local_step(x, loss_target, positions, mod, wv, P)

    dmod = R["dmod"]
    partial_shapes = [(1,), (1, d), (1, d), (1, d), (1, d), (1, d), (1, d), (1, Q_LORA), (1, KV_LORA),
                      (1, SSD_HEADS), (1, SSD_HEADS), (1, SSD_HEADS), (1, D_CONV), (4, D_CONV), (1, N_MOD * d),
                      (nb, N_MOD * d)]
    partial = _pack_rows([R["loss"][0, :1], R["norm_ffn1"], R["norm_mix"], R["norm_ffn2"], R["norm_final"],
                          R["ssd_norm_w"], R["mla_norm_w"], R["q_norm_w"], R["kv_norm_w"],
                          R["dt_bias"], R["a_log"], R["d_skip"], R["conv_b"], R["conv_w"],
                          sum_rows(dmod, "dmod_rows"), dmod])
    (partial_g,) = all_gather8([partial], "gather_partials")
    (loss, g_nf1, g_nmix, g_nf2, g_nfin, g_ssdn, g_mlan, g_qn, g_kvn, g_dtb, g_alog, g_dskip, g_convb, g_convw,
     g_bada, _) = _unpack_rows(sum_blocks(partial_g, "sum_partials"), partial_shapes)
    dmod_row = sum(-(-math.prod(shp) // D_MODEL) for shp in partial_shapes[:-1])
    dmod_all = partial_g[:, dmod_row:dmod_row + nb * N_MOD].reshape(N_DEV * nb, N_MOD * d)
    g_wada = adaln_bwd(c_act, lax.dynamic_slice(dmod_all, (0, me * n_ada), (N_DEV * nb, n_ada)), "adaln_bwd")
    n_cw = conv_w.shape[2]
    G = {"w_ada": g_wada[None], "b_ada": g_bada, "norm_ffn1": g_nf1, "norm_mix": g_nmix, "norm_ffn2": g_nf2,
         "norm_final": g_nfin.reshape(d), "ssd_norm_w": g_ssdn, "mla_norm_w": g_mlan, "q_norm_w": g_qn,
         "kv_norm_w": g_kvn, "dt_bias": g_dtb, "a_log": g_alog, "d_skip": g_dskip, "conv_b": g_convb,
         "conv_w": lax.dynamic_slice(g_convw, (0, me * n_cw), (4, n_cw))[None]}

    DW, NM, NV = {}, {}, {}
    gw = R["gw"]
    for k, (tag, group) in enumerate(GRAD_GROUPS):
        send = [_grad_blocks(name, gw[name]).reshape(N_DEV, *_shard_view(name, W[name]).shape) for name in group]
        recv = sc_all_to_all8(send, "exchange_" + tag, 2 + k)
        for name, blocks in zip(group, recv):
            res = adamw_blocks(_shard_view(name, W[name]), blocks, _shard_view(name, M[name]), _shard_view(name, V[name]),
                               "adamw_" + name)
            G[name], DW[name], NM[name], NV[name] = [_shard_unview(name, t) for t in res]
    DW["w_ada"], NM["w_ada"], NV["w_ada"] = adamw(w_ada, g_wada, m_w_ada, v_w_ada, "adamw_w_ada")
    small = [n for n in names if n not in DW]
    as2d = lambda a: a.reshape(-1, a.shape[-1])
    outs = adamw_many([as2d(W[n]) for n in small], [as2d(G[n]) for n in small], [as2d(M[n]) for n in small],
                      [as2d(V[n]) for n in small], "adamw_small")
    for res, dst in zip(outs, (DW, NM, NV)):
        for n, t in zip(small, res):
            dst[n] = t.reshape(W[n].shape)
    return (loss.reshape(()), R["dx"], *[G[n] for n in names], *[DW[n] for n in names], *[NM[n] for n in names],
            *[NV[n] for n in names])
```

```python
import math

import jax
import jax.numpy as jnp
from jax import lax
from jax.experimental import pallas as pl
from jax.experimental.pallas import tpu as pltpu
from jax.experimental.pallas import tpu_sc as plsc

F32, BF16, I32 = jnp.float32, jnp.bfloat16, jnp.int32
HI = lax.Precision.HIGHEST
SDS = jax.ShapeDtypeStruct
MESH = pl.DeviceIdType.MESH

D_MODEL = 1024
D_FF = 2816
D_SSD = 1024
SSD_HEADS = 16
SSD_HEAD_DIM = 64
SSD_GROUPS = 2
SSD_STATE = 128
CHUNK = 128
MLA_HEADS = 8
QK_NOPE = 64
QK_ROPE = 32
QK_DIM = 96
V_HEAD = 128
Q_LORA = 384
KV_LORA = 256
ROPE_THETA = 10000.0
N_MOD = 9
EPS = 1e-6
D_CONV = 1536
D_IN = 3248
D_IN_PAD = 3328
HEAD_PAD = 128
N_DEV = 8
ADAM_LR, ADAM_B1, ADAM_B2, ADAM_EPS, ADAM_WD, ADAM_STEP = 0.001, 0.9, 0.999, 1e-08, 0.01, 10

SAVED_ACT = BF16
VMEM_LIMIT = 56 * 1024 * 1024
LANES = 128
NT_DIMS = (((1,), (1,)), ((), ()))
TN_DIMS = (((0,), (0,)), ((), ()))


def _cparams(n_axes):
    return pltpu.CompilerParams(dimension_semantics=("arbitrary",) * n_axes, vmem_limit_bytes=VMEM_LIMIT)


def _row(tm, d):
    return pl.BlockSpec((None, tm, d), lambda b, i: (b, i, 0))


def _bvec(d):
    return pl.BlockSpec((None, 1, d), lambda b, i: (b, 0, 0))


def _full(shape):
    n = len(shape)
    return pl.BlockSpec(shape, lambda *_: (0,) * n)


def _sigmoid(x):
    return 1.0 / (1.0 + jnp.exp(-x))


def _softplus(x):
    return jnp.maximum(x, 0.0) + jnp.log(1.0 + jnp.exp(-jnp.abs(x)))


def _rms(x):
    return lax.rsqrt(jnp.mean(x * x, axis=-1, keepdims=True) + EPS)


def _rms_bwd(dn, n, r):
    return r * (dn - n * jnp.mean(dn * n, axis=-1, keepdims=True))


def _first_step():
    return (pl.program_id(0) == 0) & (pl.program_id(1) == 0)


def _gather_copies(x_refs, out_refs, send_sems, recv_sems, local_sems):
    mx, my, mc = lax.axis_index("x"), lax.axis_index("y"), lax.axis_index("c")
    me, sibling = (mx, my, mc), (mx, my, 1 - mc)
    chips = [(1 - mx, my), (mx, 1 - my), (1 - mx, 1 - my)]

    def copy(a, k, block, to, src=None):
        rows = out_refs[a].at[4 * block[0] + 2 * block[1] + block[2]]
        return pltpu.make_async_remote_copy(
            src_ref=rows if src is None else src, dst_ref=rows,
            send_sem=send_sems.at[7 * a + k], recv_sem=recv_sems.at[7 * a + k], device_id=to, device_id_type=MESH)

    arrays = range(len(x_refs))
    mine = [pltpu.make_async_copy(x_refs[a], out_refs[a].at[4 * mx + 2 * my + mc], local_sems.at[a]) for a in arrays]
    first = [[copy(a, 0, me, sibling, src=x_refs[a])] + [copy(a, 1 + j, me, (*chip, mc), src=x_refs[a])
                                                          for j, chip in enumerate(chips)] for a in arrays]
    passed = [[copy(a, 4 + j, (*chip, mc), sibling) for j, chip in enumerate(chips)] for a in arrays]
    for a in arrays:
        mine[a].start()
        for cp in first[a]:
            cp.start()
    for a in arrays:
        for j, chip in enumerate(chips):
            copy(a, 1 + j, (*chip, mc), me).wait_recv()
            passed[a][j].start()
    for a in arrays:
        copy(a, 0, sibling, me).wait_recv()
        for j, chip in enumerate(chips):
            copy(a, 4 + j, (*chip, 1 - mc), me).wait_recv()
    for a in arrays:
        for cp in first[a] + passed[a]:
            cp.wait_send()
        mine[a].wait()


def _gather_peers():
    mx, my, mc = lax.axis_index("x"), lax.axis_index("y"), lax.axis_index("c")
    return [(mx, my, 1 - mc), (1 - mx, my, mc), (mx, 1 - my, mc), (1 - mx, 1 - my, mc)]


def _comm_scratch(n):
    return [pltpu.SemaphoreType.DMA((7 * n,)), pltpu.SemaphoreType.DMA((7 * n,)), pltpu.SemaphoreType.DMA((n,))]


def all_gather8(xs, name):
    n = len(xs)

    def body(*refs):
        _gather_copies(refs[:n], refs[n:2 * n], *refs[2 * n:])

    return pl.pallas_call(
        body, name=name,
        out_shape=[SDS((N_DEV, *x.shape), x.dtype) for x in xs],
        in_specs=[pl.BlockSpec(memory_space=pl.ANY)] * n,
        out_specs=[pl.BlockSpec(memory_space=pl.ANY)] * n,
        scratch_shapes=_comm_scratch(n),
    )(*xs)


def _exchange_peers():
    mx, my, mc = lax.axis_index("x"), lax.axis_index("y"), lax.axis_index("c")
    return [(1 - mx if rel & 4 else mx, 1 - my if rel & 2 else my, 1 - mc if rel & 1 else mc) for rel in range(1, N_DEV)]


def _exchange_copies(x_refs, out_refs, send_sems, recv_sems, local_sems):
    mx, my, mc = lax.axis_index("x"), lax.axis_index("y"), lax.axis_index("c")
    me = 4 * mx + 2 * my + mc
    copies = []
    for a, (x_ref, out_ref) in enumerate(zip(x_refs, out_refs)):
        mine = pltpu.make_async_copy(x_ref.at[me], out_ref.at[me], local_sems.at[a])
        mine.start()
        copies.append(mine)
        for k, (px, py, pc) in enumerate(_exchange_peers()):
            cp = pltpu.make_async_remote_copy(
                src_ref=x_ref.at[4 * px + 2 * py + pc], dst_ref=out_ref.at[me],
                send_sem=send_sems.at[7 * a + k], recv_sem=recv_sems.at[7 * a + k],
                device_id=(px, py, pc), device_id_type=MESH)
            cp.start()
            copies.append(cp)
    for cp in copies:
        cp.wait()


def all_to_all8(xs, name):
    n = len(xs)

    def body(*refs):
        _exchange_copies(refs[:n], refs[n:2 * n], *refs[2 * n:])

    return pl.pallas_call(
        body, name=name,
        out_shape=[SDS(x.shape, x.dtype) for x in xs],
        in_specs=[pl.BlockSpec(memory_space=pl.ANY)] * n,
        out_specs=[pl.BlockSpec(memory_space=pl.ANY)] * n,
        scratch_shapes=_comm_scratch(n),
    )(*xs)


def _sequencer_kernel(name, collective_id, n_arrays):
    return pl.kernel(
        mesh=plsc.ScalarSubcoreMesh(axis_name="seq", num_cores=1), name=name,
        scratch_types=tuple(_comm_scratch(n_arrays)),
        compiler_params=pltpu.CompilerParams(collective_id=collective_id))


def _handshake(peers):
    barrier = pltpu.get_barrier_semaphore()
    for peer in peers:
        pl.semaphore_signal(barrier, inc=1, device_id=peer, device_id_type=MESH)
    pl.semaphore_wait(barrier, len(peers))


def _hbm_refs(xs, out_shapes):
    x_refs = [jax.new_ref(x, memory_space=pltpu.MemorySpace.HBM) for x in xs]
    out_refs = [jax.empty_ref(SDS(shp, x.dtype), memory_space=pltpu.MemorySpace.HBM) for x, shp in zip(xs, out_shapes)]
    return x_refs, out_refs


def sc_all_gather8(xs, name, collective_id):
    x_refs, out_refs = _hbm_refs(xs, [(N_DEV, *x.shape) for x in xs])

    @_sequencer_kernel(name, collective_id, len(xs))
    def launch(send_sems, recv_sems, local_sems):
        _handshake(_gather_peers())
        _gather_copies(x_refs, out_refs, send_sems, recv_sems, local_sems)

    launch()
    return [ref[...] for ref in out_refs]


def sc_all_to_all8(xs, name, collective_id):
    x_refs, out_refs = _hbm_refs(xs, [x.shape for x in xs])

    @_sequencer_kernel(name, collective_id, len(xs))
    def launch(send_sems, recv_sems, local_sems):
        _handshake(_exchange_peers())
        _exchange_copies(x_refs, out_refs, send_sems, recv_sems, local_sems)

    launch()
    return [ref[...] for ref in out_refs]


def norm_mod(x, w, sc, sh, name):
    b, s, d = x.shape
    tm = min(512, s)

    def body(x_ref, w_ref, sc_ref, sh_ref, h_ref):
        xv = x_ref[...]
        n = xv * _rms(xv)
        h_ref[...] = ((n * w_ref[...]) * (1.0 + sc_ref[...]) + sh_ref[...]).astype(BF16)

    return pl.pallas_call(
        body, name=name, grid=(b, s // tm),
        in_specs=[_row(tm, d), _full((1, d)), _bvec(d), _bvec(d)],
        out_specs=_row(tm, d), out_shape=SDS((b, s, d), BF16), compiler_params=_cparams(2))(x, w, sc, sh)


def ffn_up(h, wg_t, wu_t, name):
    b, s, d = h.shape
    f = wg_t.shape[0]
    tm, tn = min(1024, s), f // 2

    def body(h_ref, wg_ref, wu_ref, s_ref, t_ref, a_ref):
        hv = h_ref[...]
        g = lax.dot_general(hv, wg_ref[...], NT_DIMS, preferred_element_type=F32)
        u = lax.dot_general(hv, wu_ref[...], NT_DIMS, preferred_element_type=F32)
        sg = _sigmoid(g)
        silu = g * sg
        s_ref[...] = silu.astype(s_ref.dtype)
        t_ref[...] = (u * (sg + silu * (1.0 - sg))).astype(t_ref.dtype)
        a_ref[...] = (silu * u).astype(BF16)

    hs = pl.BlockSpec((None, tm, d), lambda j, bb, i: (bb, i, 0))
    ws = pl.BlockSpec((tn, d), lambda j, bb, i: (j, 0))
    os_ = pl.BlockSpec((None, tm, tn), lambda j, bb, i: (bb, i, j))
    return pl.pallas_call(
        body, name=name, grid=(f // tn, b, s // tm),
        in_specs=[hs, ws, ws], out_specs=[os_, os_, os_],
        out_shape=[SDS((b, s, f), SAVED_ACT), SDS((b, s, f), SAVED_ACT), SDS((b, s, f), BF16)],
        compiler_params=_cparams(3))(h, wg_t, wu_t)


def _norm_mod_tile(xv, w_ref, sc_ref, sh_ref):
    return ((xv * _rms(xv) * w_ref[...]) * (1.0 + sc_ref[...]) + sh_ref[...]).astype(BF16)


def ffn_down(a, wd, x, gate, scale, name, above=None):
    b, s, f = a.shape
    d = wd.shape[1]
    tm = min(1024, s)

    def body(a_ref, wd_ref, x_ref, g_ref, *rest):
        xn_ref, o_ref = rest[-3:-1] if above else rest
        o = jnp.dot(a_ref[...], wd_ref[...], preferred_element_type=F32)
        xn = x_ref[...] + (scale * g_ref[...]) * o
        xn_ref[...] = xn
        o_ref[...] = o.astype(BF16)
        if above:
            rest[-1][...] = _norm_mod_tile(xn, *rest[0:3])

    extra = above is not None
    return pl.pallas_call(
        body, name=name, grid=(b, s // tm),
        in_specs=[_row(tm, f), _full((f, d)), _row(tm, d), _bvec(d)] + ([_full((1, d)), _bvec(d), _bvec(d)] if extra else []),
        out_specs=[_row(tm, d), _row(tm, d)] + ([_row(tm, d)] if extra else []),
        out_shape=[SDS((b, s, d), F32), SDS((b, s, d), BF16)] + ([SDS((b, s, d), BF16)] if extra else []),
        compiler_params=_cparams(2))(a, wd, x, gate, *(above or ()))


def ffn_down_final(a, wd, x, gate, scale, w_final, tgt, name):
    b, s, f = a.shape
    d = wd.shape[1]
    tm = min(1024, s)

    def body(a_ref, wd_ref, x_ref, g_ref, w_ref, t_ref, loss_ref, dx_ref, dw_ref, do_ref, dg_ref):
        @pl.when(_first_step())
        def _():
            loss_ref[...] = jnp.zeros_like(loss_ref)
            dw_ref[...] = jnp.zeros_like(dw_ref)

        @pl.when(pl.program_id(1) == 0)
        def _():
            dg_ref[...] = jnp.zeros_like(dg_ref)
        o = jnp.dot(a_ref[...], wd_ref[...], preferred_element_type=F32)
        sg = scale * g_ref[...]
        xv = x_ref[...] + sg * o
        r = _rms(xv)
        n = xv * r
        wv = w_ref[...]
        e = n * wv - t_ref[...]
        loss_ref[...] += jnp.sum(e * e) * (0.5 / d)
        dy = e * (1.0 / d)
        dw_ref[...] += jnp.sum(dy * n, axis=0, keepdims=True)
        dx = _rms_bwd(dy * wv, n, r)
        dx_ref[...] = dx
        do_ref[...] = (sg * dx).astype(BF16)
        dg_ref[...] += jnp.sum(scale * dx * o, axis=0, keepdims=True)

    return pl.pallas_call(
        body, name=name, grid=(b, s // tm),
        in_specs=[_row(tm, f), _full((f, d)), _row(tm, d), _bvec(d), _full((1, d)), _row(tm, d)],
        out_specs=[_full((1, LANES)), _row(tm, d), _full((1, d)), _row(tm, d), _bvec(d)],
        out_shape=[SDS((1, LANES), F32), SDS((b, s, d), F32), SDS((1, d), F32), SDS((b, s, d), BF16), SDS((b, 1, d), F32)],
        compiler_params=_cparams(2))(a, wd, x, gate, w_final, tgt)


def ffn_dact(do, wd, silu_g, u_dsilu, name):
    b, s, d = do.shape
    f = wd.shape[0]
    tm, tn = min(1024, s), f // 2

    def body(do_ref, wd_ref, s_ref, t_ref, dg_ref, du_ref):
        da = lax.dot_general(do_ref[...], wd_ref[...], NT_DIMS, preferred_element_type=F32)
        dg_ref[...] = (da * t_ref[...].astype(F32)).astype(BF16)
        du_ref[...] = (da * s_ref[...].astype(F32)).astype(BF16)

    dos = pl.BlockSpec((None, tm, d), lambda j, bb, i: (bb, i, 0))
    ws = pl.BlockSpec((tn, d), lambda j, bb, i: (j, 0))
    es = pl.BlockSpec((None, tm, tn), lambda j, bb, i: (bb, i, j))
    return pl.pallas_call(
        body, name=name, grid=(f // tn, b, s // tm),
        in_specs=[dos, ws, es, es], out_specs=[es, es],
        out_shape=[SDS((b, s, f), BF16), SDS((b, s, f), BF16)], compiler_params=_cparams(3))(do, wd, silu_g, u_dsilu)


def mm_tn(a, bm, tma, tnb, name):
    b, s, ka = a.shape
    nb = bm.shape[2]
    tk = min(2048, s)
    nk = s // tk

    def body(a_ref, b_ref, o_ref, acc):
        first = (pl.program_id(2) == 0) & (pl.program_id(3) == 0)
        last = (pl.program_id(2) == b - 1) & (pl.program_id(3) == nk - 1)
        part = lax.dot_general(a_ref[...], b_ref[...], TN_DIMS, preferred_element_type=F32)

        @pl.when(first)
        def _():
            acc[...] = part

        @pl.when(jnp.logical_not(first))
        def _():
            acc[...] += part

        @pl.when(last)
        def _():
            o_ref[...] = acc[...].astype(BF16)

    return pl.pallas_call(
        body, name=name, grid=(ka // tma, nb // tnb, b, nk),
        in_specs=[pl.BlockSpec((None, tk, tma), lambda i, j, bb, k: (bb, k, i)),
                  pl.BlockSpec((None, tk, tnb), lambda i, j, bb, k: (bb, k, j))],
        out_specs=pl.BlockSpec((tma, tnb), lambda i, j, bb, k: (i, j)),
        out_shape=SDS((ka, nb), BF16), scratch_shapes=[pltpu.VMEM((tma, tnb), F32)],
        compiler_params=_cparams(4))(a, bm)


def mm_tn_blocks(a_blocks, bm, name):
    b, s, nb = bm.shape
    widths = [a.shape[2] for a in a_blocks]
    starts = [sum(widths[:k]) for k in range(len(widths))]
    tk = min(1024, s)
    nk = s // tk
    n = len(a_blocks)

    def body(*refs):
        a_refs, b_ref, o_ref, acc = refs[:n], refs[n], refs[n + 1], refs[n + 2]
        first = (pl.program_id(0) == 0) & (pl.program_id(1) == 0)
        last = (pl.program_id(0) == b - 1) & (pl.program_id(1) == nk - 1)

        @pl.when(first)
        def _():
            acc[...] = jnp.zeros_like(acc)
        bv = b_ref[...]
        for a_ref, st, wd in zip(a_refs, starts, widths):
            acc[st:st + wd, :] += lax.dot_general(a_ref[...], bv, TN_DIMS, preferred_element_type=F32)

        @pl.when(last)
        def _():
            o_ref[...] = acc[...].astype(BF16)

    return pl.pallas_call(
        body, name=name, grid=(b, nk),
        in_specs=[_row(tk, wd) for wd in widths] + [_row(tk, nb)],
        out_specs=_full((sum(widths), nb)), out_shape=SDS((sum(widths), nb), BF16),
        scratch_shapes=[pltpu.VMEM((sum(widths), nb), F32)], compiler_params=_cparams(2))(*a_blocks, bm)


def _gate_bwd_specs(tm, d, b, s):
    return ([_row(tm, d), _bvec(d)], [_row(tm, d), _bvec(d)], [SDS((b, s, d), BF16), SDS((b, 1, d), F32)])


def _gate_bwd_tile(dx, scale, o_ref, g_ref, do_ref, dg_ref):
    do_ref[...] = ((scale * g_ref[...]) * dx).astype(BF16)
    dg_ref[...] += jnp.sum(scale * dx * o_ref[...].astype(F32), axis=0, keepdims=True)


def dh_norm_bwd(dys, wts, x, dxn, w, sc, name, below=None):
    b, s, d = x.shape
    tm = min(512, s)
    n_in, n_w = len(dys), len(wts)
    extra_in, extra_out, extra_shape = _gate_bwd_specs(tm, d, b, s) if below else ([], [], [])
    starts = [sum(dy.shape[2] for dy in dys[:k]) for k in range(n_in)]

    def body(*refs):
        dy_refs, w_refs = refs[:n_in], refs[n_in:n_in + n_w]
        x_ref, dxn_ref, nw_ref, sc_ref = refs[n_in + n_w:n_in + n_w + 4]
        rest = refs[n_in + n_w + 4:]
        if below:
            o_ref, g_ref, dx_ref, dsc_ref, dsh_ref, dw_ref, do_ref, dg_ref = rest
        else:
            dx_ref, dsc_ref, dsh_ref, dw_ref = rest

        @pl.when(pl.program_id(1) == 0)
        def _():
            dsc_ref[...] = jnp.zeros_like(dsc_ref)
            dsh_ref[...] = jnp.zeros_like(dsh_ref)
            if below:
                dg_ref[...] = jnp.zeros_like(dg_ref)

        @pl.when(_first_step())
        def _():
            dw_ref[...] = jnp.zeros_like(dw_ref)

        def weight(k):
            return w_refs[k][...] if n_w == n_in else w_refs[0][starts[k]:starts[k] + dys[k].shape[2], :]

        dh = jnp.dot(dy_refs[0][...], weight(0), preferred_element_type=F32)
        for k in range(1, n_in):
            dh += jnp.dot(dy_refs[k][...], weight(k), preferred_element_type=F32)
        xv = x_ref[...]
        r = _rms(xv)
        n = xv * r
        nw = nw_ref[...]
        dsc_ref[...] += jnp.sum(dh * (n * nw), axis=0, keepdims=True)
        dsh_ref[...] += jnp.sum(dh, axis=0, keepdims=True)
        dhn = dh * (1.0 + sc_ref[...])
        dw_ref[...] += jnp.sum(dhn * n, axis=0, keepdims=True)
        dx = dxn_ref[...] + _rms_bwd(dhn * nw, n, r)
        dx_ref[...] = dx
        if below:
            _gate_bwd_tile(dx, below[2], o_ref, g_ref, do_ref, dg_ref)

    resident = lambda shape: pl.BlockSpec(shape, lambda *_: (0,) * len(shape), pipeline_mode=pl.Buffered(1))
    in_specs = [_row(tm, dy.shape[2]) for dy in dys] + [resident(wt.shape) for wt in wts]
    in_specs += [_row(tm, d), _row(tm, d), _full((1, d)), _bvec(d)] + extra_in
    return pl.pallas_call(
        body, name=name, grid=(b, s // tm), in_specs=in_specs,
        out_specs=[_row(tm, d), _bvec(d), _bvec(d), _full((1, d))] + extra_out,
        out_shape=[SDS((b, s, d), F32), SDS((b, 1, d), F32), SDS((b, 1, d), F32), SDS((1, d), F32)] + extra_shape,
        compiler_params=_cparams(2))(*dys, *wts, x, dxn, w, sc, *(below[:2] if below else ()))


def in_proj(h, win_t, name):
    b, s, d = h.shape
    tm = min(512, s)
    widths = (D_SSD, D_SSD + 2 * SSD_GROUPS * SSD_STATE, Q_LORA, KV_LORA, LANES)

    def body(h_ref, w_ref, *outs):
        p = lax.dot_general(h_ref[...], w_ref[...], NT_DIMS, preferred_element_type=F32)
        off = 0
        for o_ref, wd in zip(outs, widths):
            o_ref[...] = p[:, off:off + wd]
            off += wd

    return pl.pallas_call(
        body, name=name, grid=(b, s // tm),
        in_specs=[_row(tm, d), _full(win_t.shape)],
        out_specs=[_row(tm, wd) for wd in widths],
        out_shape=[SDS((b, s, wd), F32) for wd in widths], compiler_params=_cparams(2))(h, win_t)


def _halo_prev(ts, d):
    return pl.BlockSpec((None, 8, d), lambda b, i: (b, jnp.maximum(i * (ts // 8) - 1, 0), 0))


CONV_ROWS = 32


def _conv_head(head, u_ref, up_ref):
    head[0:8, :] = jnp.where(pl.program_id(1) > 0, up_ref[...], 0.0)
    head[8:8 + CONV_ROWS, :] = u_ref[0:CONV_ROWS, :]


def _conv_windows(u_ref, head, r0):
    if r0 == 0:
        return [head[5 + k:5 + k + CONV_ROWS, :] for k in range(4)]
    return [u_ref[r0 - 3 + k:r0 - 3 + k + CONV_ROWS, :] for k in range(4)]


def _fold8(t):
    acc = t[0:8, :]
    for r in range(8, CONV_ROWS, 8):
        acc += t[r:r + 8, :]
    return acc


def conv_fwd(u, cw, cb, name):
    b, s, dc = u.shape
    ts = min(512, s)
    widths = (D_SSD, SSD_GROUPS * SSD_STATE, SSD_GROUPS * SSD_STATE)

    def body(u_ref, up_ref, w_ref, b_ref, xs_ref, bm_ref, cm_ref, head):
        _conv_head(head, u_ref, up_ref)
        ws = [w_ref[k:k + 1, :] for k in range(4)]
        bias = b_ref[...]
        for r0 in range(0, ts, CONV_ROWS):
            taps = _conv_windows(u_ref, head, r0)
            v = bias + taps[0] * ws[0] + taps[1] * ws[1] + taps[2] * ws[2] + taps[3] * ws[3]
            y = v * _sigmoid(v)
            rs = slice(r0, r0 + CONV_ROWS)
            xs_ref[rs, :] = y[:, 0:D_SSD]
            bm_ref[rs, :] = y[:, D_SSD:D_SSD + 256]
            cm_ref[rs, :] = y[:, D_SSD + 256:D_SSD + 512]

    return pl.pallas_call(
        body, name=name, grid=(b, s // ts),
        in_specs=[_row(ts, dc), _halo_prev(ts, dc), _full((4, dc)), _full((1, dc))],
        out_specs=[_row(ts, wd) for wd in widths],
        out_shape=[SDS((b, s, wd), F32) for wd in widths],
        scratch_shapes=[pltpu.VMEM((8 + CONV_ROWS, dc), F32)], compiler_params=_cparams(2))(u, u, cw, cb)


def conv_bwd_a(dxs, dbm, dcm, u, cw, cb, name):
    b, s, dc = u.shape
    ts = min(512, s)

    def body(dxs_ref, dbm_ref, dcm_ref, u_ref, up_ref, w_ref, b_ref, dv_ref, dwb_ref, head):
        @pl.when(_first_step())
        def _():
            dwb_ref[...] = jnp.zeros_like(dwb_ref)
        _conv_head(head, u_ref, up_ref)
        ws = [w_ref[k:k + 1, :] for k in range(4)]
        bias = b_ref[...]
        for r0 in range(0, ts, CONV_ROWS):
            taps = _conv_windows(u_ref, head, r0)
            v = bias + taps[0] * ws[0] + taps[1] * ws[1] + taps[2] * ws[2] + taps[3] * ws[3]
            sg = _sigmoid(v)
            rs = slice(r0, r0 + CONV_ROWS)
            dy = jnp.concatenate([dxs_ref[rs, :], dbm_ref[rs, :], dcm_ref[rs, :]], axis=1)
            dv = dy * (sg * (1.0 + v * (1.0 - sg)))
            dv_ref[rs, :] = dv
            for k in range(4):
                dwb_ref[8 * k:8 * k + 8, :] += _fold8(dv * taps[k])
            dwb_ref[32:40, :] += _fold8(dv)

    return pl.pallas_call(
        body, name=name, grid=(b, s // ts),
        in_specs=[_row(ts, D_SSD), _row(ts, 256), _row(ts, 256), _row(ts, dc), _halo_prev(ts, dc),
                  _full((4, dc)), _full((1, dc))],
        out_specs=[_row(ts, dc), _full((40, dc))],
        out_shape=[SDS((b, s, dc), F32), SDS((40, dc), F32)],
        scratch_shapes=[pltpu.VMEM((8 + CONV_ROWS, dc), F32)], compiler_params=_cparams(2))(dxs, dbm, dcm, u, u, cw, cb)


def conv_grads_fold(x, name):
    c = x.shape[1]

    def body(x_ref, o_ref):
        o_ref[...] = jnp.zeros_like(o_ref)
        for k in range(5):
            o_ref[k:k + 1, :] = jnp.sum(x_ref[8 * k:8 * k + 8, :], axis=0, keepdims=True)

    return pl.pallas_call(body, name=name, out_shape=SDS((8, c), F32))(x)


def conv_bwd_b(dv, cw, name):
    b, s, dc = dv.shape
    ts = min(512, s)
    nt = s // ts

    def body(dv_ref, dn_ref, w_ref, du_ref, tail):
        tail[0:CONV_ROWS, :] = dv_ref[ts - CONV_ROWS:ts, :]
        tail[CONV_ROWS:CONV_ROWS + 8, :] = jnp.where(pl.program_id(1) < nt - 1, dn_ref[...], 0.0)
        ws = [w_ref[k:k + 1, :] for k in range(4)]
        for r0 in range(0, ts, CONV_ROWS):
            if r0 == ts - CONV_ROWS:
                win = [tail[3 - k:3 - k + CONV_ROWS, :] for k in range(4)]
            else:
                win = [dv_ref[r0 + 3 - k:r0 + 3 - k + CONV_ROWS, :] for k in range(4)]
            acc = win[0] * ws[0] + win[1] * ws[1] + win[2] * ws[2] + win[3] * ws[3]
            du_ref[r0:r0 + CONV_ROWS, :] = acc.astype(BF16)

    nxt = pl.BlockSpec((None, 8, dc), lambda bb, i: (bb, jnp.minimum((i + 1) * (ts // 8), s // 8 - 1), 0))
    return pl.pallas_call(
        body, name=name, grid=(b, nt),
        in_specs=[_row(ts, dc), nxt, _full((4, dc))],
        out_specs=_row(ts, dc), out_shape=SDS((b, s, dc), BF16),
        scratch_shapes=[pltpu.VMEM((CONV_ROWS + 8, dc), F32)], compiler_params=_cparams(2))(dv, dv, cw)


def _ssd_common(misc_ref, dtb_ref, alog_ref, e_ref):
    ln = CHUNK
    lane = lax.broadcasted_iota(I32, (ln, LANES), 1)
    lane1 = lax.broadcasted_iota(I32, (1, LANES), 1)
    pre = misc_ref[...] + dtb_ref[...]
    dt_s = jnp.where(lane < SSD_HEADS, _softplus(pre), 0.0)
    a_neg = jnp.where(lane1 < SSD_HEADS, -jnp.exp(alog_ref[...]), 0.0)
    ri = lax.broadcasted_iota(I32, (ln, ln), 0)
    ci = lax.broadcasted_iota(I32, (ln, ln), 1)
    tril = ci <= ri
    acum = jnp.dot(tril.astype(F32), dt_s * a_neg, preferred_element_type=F32, precision=HI)
    both_e = _dot_01(jnp.concatenate([dt_s, acum], axis=0), e_ref[...], 3)
    dt_e, acum_e = both_e[0:ln], both_e[ln:2 * ln]
    return dict(pre=pre, dt_s=dt_s, a_neg=a_neg, tril=tril, ri=ri, ci=ci, acum=acum, acum_t=acum.T,
                dt_e=dt_e, eac_e=jnp.exp(acum_e), del_e=jnp.exp(acum_e[ln - 1:ln, :] - acum_e))


def _dot_01(x, m01, terms):
    acc, rest = None, x
    for k in range(terms):
        part = rest.astype(BF16)
        if k + 1 < terms:
            rest = rest - part.astype(F32)
        d = jnp.dot(part, m01, preferred_element_type=F32)
        acc = d if acc is None else acc + d
    return acc


def _decay(cm, h):
    seg = cm["acum"][:, h:h + 1] - cm["acum_t"][h:h + 1, :]
    return jnp.exp(jnp.where(cm["tril"], seg, -jnp.inf))


def ssd_fwd(xs, bm, cm_, misc, z, dtb, alog, dskip_e, norm_w, e_mat, name):
    b, s, _ = xs.shape
    ln, nc = CHUNK, s // CHUNK
    gw = D_SSD // SSD_GROUPS
    hpg = SSD_HEADS // SSD_GROUPS

    def body(xs_ref, b_ref, c_ref, misc_ref, z_ref, dtb_ref, alog_ref, dsk_ref, nw_ref, e_ref,
             ys_ref, y_ref, p_ref, st, yd):
        @pl.when(pl.program_id(1) == 0)
        def _():
            st[...] = jnp.zeros_like(st)
        cm = _ssd_common(misc_ref, dtb_ref, alog_ref, e_ref)
        xsv = xs_ref[...]
        xdt = xsv * cm["dt_e"]
        xdt_b = xdt.astype(BF16)
        xd_b = (xdt * cm["del_e"]).astype(BF16)
        gam_e = cm["eac_e"][ln - 1:ln, :]
        p_ref[...] = st[...]
        groups = [slice(gw * g, gw * (g + 1)) for g in range(SSD_GROUPS)]
        heads = [slice(SSD_HEAD_DIM * h, SSD_HEAD_DIM * (h + 1)) for h in range(SSD_HEADS)]
        bgs = [b_ref[:, SSD_STATE * g:SSD_STATE * (g + 1)].astype(BF16) for g in range(SSD_GROUPS)]
        cgs = [c_ref[:, SSD_STATE * g:SSD_STATE * (g + 1)].astype(BF16) for g in range(SSD_GROUPS)]
        cbs = [lax.dot_general(cg, bg, NT_DIMS, preferred_element_type=F32) for cg, bg in zip(cgs, bgs)]
        sts = [st[:, gs] for gs in groups]
        yoff = [jnp.dot(cg, st_g.astype(BF16), preferred_element_type=F32) * cm["eac_e"][:, gs]
                for cg, st_g, gs in zip(cgs, sts, groups)]
        news = [lax.dot_general(bg, xd_b[:, gs], TN_DIMS, preferred_element_type=F32) for bg, gs in zip(bgs, groups)]
        for gs, st_g, new in zip(groups, sts, news):
            st[:, gs] = st_g * gam_e[:, gs] + new
        ms = [(cbs[h // hpg] * _decay(cm, h)).astype(BF16) for h in range(SSD_HEADS)]
        for h, hs in enumerate(heads):
            yd[:, hs] = jnp.dot(ms[h], xdt_b[:, hs], preferred_element_type=F32)
        y = yd[...] + jnp.concatenate(yoff, axis=1) + dsk_ref[...] * xsv
        y_ref[...] = y
        zz = z_ref[...]
        yg = y * (zz * _sigmoid(zz))
        outs = []
        for g in range(SSD_GROUPS):
            ygg = yg[:, gw * g:gw * (g + 1)]
            outs.append(ygg * _rms(ygg) * nw_ref[:, gw * g:gw * (g + 1)])
        ys_ref[...] = jnp.concatenate(outs, axis=1).astype(BF16)

    row = lambda d: pl.BlockSpec((None, ln, d), lambda bb, c: (bb, c, 0))
    return pl.pallas_call(
        body, name=name, grid=(b, nc),
        in_specs=[row(D_SSD), row(256), row(256), row(LANES), row(D_SSD), _full((1, LANES)), _full((1, LANES)),
                  _full((1, D_SSD)), _full((1, D_SSD)), _full((LANES, D_SSD))],
        out_specs=[row(D_SSD), row(D_SSD), pl.BlockSpec((None, None, SSD_STATE, D_SSD), lambda bb, c: (bb, c, 0, 0))],
        out_shape=[SDS((b, s, D_SSD), BF16), SDS((b, s, D_SSD), F32), SDS((b, nc, SSD_STATE, D_SSD), F32)],
        scratch_shapes=[pltpu.VMEM((SSD_STATE, D_SSD), F32), pltpu.VMEM((ln, D_SSD), F32)],
        compiler_params=_cparams(2))(xs, bm, cm_, misc, z, dtb, alog, dskip_e, norm_w, e_mat)


def ssd_bwd(dys, y, z, xs, bm, cm_, misc, prev, dtb, alog, dskip_e, norm_w, e_mat, et_mat, name):
    b, s, _ = xs.shape
    ln, nc = CHUNK, s // CHUNK
    gw = D_SSD // SSD_GROUPS
    hpg = SSD_HEADS // SSD_GROUPS

    def body(dys_ref, y_ref, z_ref, xs_ref, b_ref, c_ref, misc_ref, p_ref, dtb_ref, alog_ref, dsk_ref, nw_ref,
             e_ref, et_ref, dxs_ref, db_ref, dc_ref, dz_ref, ddt_ref, dnw_ref, ddsk_ref, ddtb_ref, dalog_ref,
             dst, dxd, dac_t):
        @pl.when(_first_step())
        def _():
            for r_ in (dnw_ref, ddsk_ref, ddtb_ref, dalog_ref):
                r_[...] = jnp.zeros_like(r_)

        @pl.when(pl.program_id(1) == 0)
        def _():
            dst[...] = jnp.zeros_like(dst)

        cm = _ssd_common(misc_ref, dtb_ref, alog_ref, e_ref)
        et = et_ref[...]
        squeeze = lambda t: _dot_01(t, et, 2)
        lane = lax.broadcasted_iota(I32, (ln, LANES), 1)
        sub = lax.broadcasted_iota(I32, (LANES, ln), 0)
        xsv = xs_ref[...]
        xdt = xsv * cm["dt_e"]
        xdt_b = xdt.astype(BF16)
        xd_b = (xdt * cm["del_e"]).astype(BF16)
        eac_e = cm["eac_e"]
        gam_e = eac_e[ln - 1:ln, :]

        yv, zz, dyo = y_ref[...], z_ref[...], dys_ref[...]
        sz = _sigmoid(zz)
        silu_z = zz * sz
        yg = yv * silu_z
        dyg, dnw = [], []
        for g in range(SSD_GROUPS):
            gs = slice(gw * g, gw * (g + 1))
            ygg = yg[:, gs]
            r = _rms(ygg)
            n = ygg * r
            dnw.append(jnp.sum(dyo[:, gs] * n, axis=0, keepdims=True))
            dyg.append(_rms_bwd(dyo[:, gs] * nw_ref[:, gs], n, r))
        dyg = jnp.concatenate(dyg, axis=1)
        dnw_ref[...] += jnp.concatenate(dnw, axis=1)
        dz_ref[...] = (dyg * yv * (sz * (1.0 + zz * (1.0 - sz)))).astype(BF16)
        dy = dyg * silu_z
        ddsk_ref[...] += jnp.sum(dy * xsv, axis=0, keepdims=True)
        dy_b = dy.astype(BF16)

        dacum = jnp.zeros((ln, LANES), F32)
        dac_t[...] = jnp.zeros_like(dac_t)
        w1, dgam = [], []
        for g in range(SSD_GROUPS):
            gs = slice(gw * g, gw * (g + 1))
            ss = slice(SSD_STATE * g, SSD_STATE * (g + 1))
            bg = b_ref[:, ss].astype(BF16)
            cg = c_ref[:, ss].astype(BF16)
            cb = lax.dot_general(cg, bg, NT_DIMS, preferred_element_type=F32)
            pt = p_ref[:, gs]
            pt_b = pt.astype(BF16)
            dst_g = dst[:, gs]
            dst_b = dst_g.astype(BF16)
            edy = (dy[:, gs] * eac_e[:, gs]).astype(BF16)
            dcg = lax.dot_general(edy, pt_b, NT_DIMS, preferred_element_type=F32)
            dpt = lax.dot_general(cg, edy, TN_DIMS, preferred_element_type=F32)
            yoff = jnp.dot(cg, pt_b, preferred_element_type=F32) * eac_e[:, gs]
            dxd_g = jnp.dot(bg, dst_b, preferred_element_type=F32)
            dbg = lax.dot_general(xd_b[:, gs], dst_b, NT_DIMS, preferred_element_type=F32)
            ddel = dxd_g * xdt[:, gs] * cm["del_e"][:, gs]
            w1.append(dy[:, gs] * yoff - ddel)
            dgam.append(jnp.sum(ddel, axis=0, keepdims=True) + jnp.sum(dst_g * pt, axis=0, keepdims=True) * gam_e[:, gs])
            dxd[:, gs] = dxd_g * cm["del_e"][:, gs]
            dst[:, gs] = dst_g * gam_e[:, gs] + dpt
            dcb = jnp.zeros((ln, ln), F32)
            for j in range(hpg):
                h = hpg * g + j
                hs = slice(SSD_HEAD_DIM * h, SSD_HEAD_DIM * (h + 1))
                lam = _decay(cm, h)
                m = cb * lam
                dm = lax.dot_general(dy_b[:, hs], xdt_b[:, hs], NT_DIMS, preferred_element_type=F32)
                dxd[:, hs] += lax.dot_general(m.astype(BF16), dy_b[:, hs], TN_DIMS, preferred_element_type=F32)
                dcb += dm * lam
                wl = dm * m
                dacum += jnp.where(lane == h, jnp.sum(wl, axis=1, keepdims=True), 0.0)
                dac_t[...] -= jnp.where(sub == h, jnp.sum(wl, axis=0, keepdims=True), 0.0)
            dcb_b = dcb.astype(BF16)
            dc_ref[:, ss] = dcg + jnp.dot(dcb_b, bg, preferred_element_type=F32)
            db_ref[:, ss] = dbg + lax.dot_general(dcb_b, cg, TN_DIMS, preferred_element_type=F32)

        dxdt = dxd[...]
        dxs_ref[...] = dy * dsk_ref[...] + dxdt * cm["dt_e"]
        dacum += squeeze(jnp.concatenate(w1, axis=1)) + dac_t[...].T
        dlast = squeeze(jnp.broadcast_to(jnp.concatenate(dgam, axis=1), (8, D_SSD)))[0:1, :]
        dacum += jnp.where(lax.broadcasted_iota(I32, (ln, LANES), 0) == ln - 1, dlast, 0.0)
        triu = (cm["ci"] >= cm["ri"]).astype(F32)
        da = jnp.dot(triu, dacum, preferred_element_type=F32, precision=HI)
        ddt = da * cm["a_neg"] + squeeze(dxdt * xsv)
        dalog_ref[...] += jnp.sum(da * cm["dt_s"], axis=0, keepdims=True) * cm["a_neg"]
        ddt_raw = jnp.where(lane < SSD_HEADS, ddt * _sigmoid(cm["pre"]), 0.0)
        ddt_ref[...] = ddt_raw
        ddtb_ref[...] += jnp.sum(ddt_raw, axis=0, keepdims=True)

    row = lambda d: pl.BlockSpec((None, ln, d), lambda bb, c: (bb, nc - 1 - c, 0))
    return pl.pallas_call(
        body, name=name, grid=(b, nc),
        in_specs=[row(D_SSD), row(D_SSD), row(D_SSD), row(D_SSD), row(256), row(256), row(LANES),
                  pl.BlockSpec((None, None, SSD_STATE, D_SSD), lambda bb, c: (bb, nc - 1 - c, 0, 0)),
                  _full((1, LANES)), _full((1, LANES)), _full((1, D_SSD)), _full((1, D_SSD)),
                  _full((LANES, D_SSD)), _full((D_SSD, LANES))],
        out_specs=[row(D_SSD), row(256), row(256), row(D_SSD), row(LANES),
                   _full((1, D_SSD)), _full((1, D_SSD)), _full((1, LANES)), _full((1, LANES))],
        out_shape=[SDS((b, s, D_SSD), F32), SDS((b, s, 256), F32), SDS((b, s, 256), F32), SDS((b, s, D_SSD), BF16),
                   SDS((b, s, LANES), F32), SDS((1, D_SSD), F32), SDS((1, D_SSD), F32), SDS((1, LANES), F32),
                   SDS((1, LANES), F32)],
        scratch_shapes=[pltpu.VMEM((SSD_STATE, D_SSD), F32), pltpu.VMEM((ln, D_SSD), F32), pltpu.VMEM((LANES, ln), F32)],
        compiler_params=_cparams(2))(dys, y, z, xs, bm, cm_, misc, prev, dtb, alog, dskip_e, norm_w, e_mat, et_mat)


def _rope(xv, cc, sp, sm):
    n = xv.shape[1]
    return xv * cc + pltpu.roll(xv, 16, 1) * sp + pltpu.roll(xv, n - 16, 1) * sm


def _rope_bwd(dy, cc, sp, sm):
    n = dy.shape[1]
    return dy * cc + pltpu.roll(dy * sp, n - 16, 1) + pltpu.roll(dy * sm, 16, 1)


def _tile8(t):
    return jnp.concatenate([t] * MLA_HEADS, axis=1)


def qkv_fwd(cq, ckv, misc, cc, sp, sm, qnw, kvnw, wuq_t, wukv_t, place, name):
    b, s, _ = cq.shape
    tm = _att_tile(s)
    hd = MLA_HEADS * HEAD_PAD

    def body(cq_ref, ckv_ref, misc_ref, cc_ref, sp_ref, sm_ref, qnw_ref, kvnw_ref, wq_ref, wkv_ref, pl_ref,
             q_ref, k_ref, v_ref, vt_ref, qn_ref, kvn_ref):
        cqv, ckvv = cq_ref[...], ckv_ref[...]
        qn = (cqv * _rms(cqv) * qnw_ref[...]).astype(BF16)
        kvn = (ckvv * _rms(ckvv) * kvnw_ref[...]).astype(BF16)
        qn_ref[...] = qn
        kvn_ref[...] = kvn
        cc1, sp1, sm1 = cc_ref[...], sp_ref[...], sm_ref[...]
        q = lax.dot_general(qn, wq_ref[...], NT_DIMS, preferred_element_type=F32)
        q_ref[...] = _rope(q, _tile8(cc1), _tile8(sp1), _tile8(sm1)).astype(BF16)
        kv = lax.dot_general(kvn, wkv_ref[...], NT_DIMS, preferred_element_type=F32)
        kr = jnp.dot(misc_ref[...], pl_ref[...], preferred_element_type=F32, precision=HI)
        kr = _rope(kr, cc1, sp1, sm1)
        k_ref[...] = (kv[:, 0:hd] + _tile8(kr)).astype(BF16)
        v_ref[...] = kv[:, hd:2 * hd].astype(BF16)
        for h in range(MLA_HEADS):
            vt_ref[h] = kv[:, hd + HEAD_PAD * h:hd + HEAD_PAD * (h + 1)].T.astype(BF16)

    return pl.pallas_call(
        body, name=name, grid=(b, s // tm),
        in_specs=[_row(tm, Q_LORA), _row(tm, KV_LORA), _row(tm, LANES), _row(tm, LANES), _row(tm, LANES), _row(tm, LANES),
                  _full((1, Q_LORA)), _full((1, KV_LORA)), _full(wuq_t.shape), _full(wukv_t.shape), _full((LANES, LANES))],
        out_specs=[_row(tm, hd), _row(tm, hd), _row(tm, hd),
                   pl.BlockSpec((None, MLA_HEADS, None, HEAD_PAD, tm), lambda bb, i: (bb, 0, i, 0, 0)),
                   _row(tm, Q_LORA), _row(tm, KV_LORA)],
        out_shape=[SDS((b, s, hd), BF16)] * 3 + [SDS((b, MLA_HEADS, s // tm, HEAD_PAD, tm), BF16),
                                                 SDS((b, s, Q_LORA), BF16), SDS((b, s, KV_LORA), BF16)],
        compiler_params=_cparams(2))(cq, ckv, misc, cc, sp, sm, qnw, kvnw, wuq_t, wukv_t, place)


def qkv_bwd(dq, dk, dv, ddt, cq, ckv, cc, sp, sm, qnw, kvnw, wuq_t, wukv_t, place_t, name):
    b, s, _ = cq.shape
    tm = min(512, s)
    hd = MLA_HEADS * HEAD_PAD

    def body(dq_ref, dk_ref, dv_ref, ddt_ref, cq_ref, ckv_ref, cc_ref, sp_ref, sm_ref, qnw_ref, kvnw_ref,
             wq_ref, wkv_ref, plt_ref, dcq_ref, dckv_ref, dmisc_ref, dqp_ref, dkv_ref, dqnw_ref, dkvnw_ref):
        @pl.when(_first_step())
        def _():
            dqnw_ref[...] = jnp.zeros_like(dqnw_ref)
            dkvnw_ref[...] = jnp.zeros_like(dkvnw_ref)
        cc1, sp1, sm1 = cc_ref[...], sp_ref[...], sm_ref[...]
        dqp = _rope_bwd(dq_ref[...].astype(F32), _tile8(cc1), _tile8(sp1), _tile8(sm1)).astype(BF16)
        dqp_ref[...] = dqp
        dkv_b = jnp.concatenate([dk_ref[...], dv_ref[...]], axis=1)
        dkf = dk_ref[...].astype(F32)
        dkv_ref[...] = dkv_b
        dkr = dkf[:, 0:HEAD_PAD]
        for h in range(1, MLA_HEADS):
            dkr += dkf[:, HEAD_PAD * h:HEAD_PAD * (h + 1)]
        dkr = _rope_bwd(dkr, cc1, sp1, sm1)
        dmisc_ref[...] = (jnp.dot(dkr, plt_ref[...], preferred_element_type=F32, precision=HI) + ddt_ref[...]).astype(BF16)

        def norm_bwd(dn_w, xv, w_ref, dw_ref, dx_ref):
            r = _rms(xv)
            n = xv * r
            dw_ref[...] += jnp.sum(dn_w * n, axis=0, keepdims=True)
            dx_ref[...] = _rms_bwd(dn_w * w_ref[...], n, r).astype(BF16)

        norm_bwd(jnp.dot(dqp, wq_ref[...], preferred_element_type=F32), cq_ref[...], qnw_ref, dqnw_ref, dcq_ref)
        norm_bwd(jnp.dot(dkv_b, wkv_ref[...], preferred_element_type=F32), ckv_ref[...], kvnw_ref, dkvnw_ref, dckv_ref)

    return pl.pallas_call(
        body, name=name, grid=(b, s // tm),
        in_specs=[_row(tm, hd), _row(tm, hd), _row(tm, hd), _row(tm, LANES), _row(tm, Q_LORA), _row(tm, KV_LORA),
                  _row(tm, LANES), _row(tm, LANES), _row(tm, LANES), _full((1, Q_LORA)), _full((1, KV_LORA)),
                  _full(wuq_t.shape), _full(wukv_t.shape), _full((LANES, LANES))],
        out_specs=[_row(tm, Q_LORA), _row(tm, KV_LORA), _row(tm, LANES), _row(tm, hd), _row(tm, 2 * hd),
                   _full((1, Q_LORA)), _full((1, KV_LORA))],
        out_shape=[SDS((b, s, Q_LORA), BF16), SDS((b, s, KV_LORA), BF16), SDS((b, s, LANES), BF16),
                   SDS((b, s, hd), BF16), SDS((b, s, 2 * hd), BF16), SDS((1, Q_LORA), F32), SDS((1, KV_LORA), F32)],
        compiler_params=_cparams(2))(dq, dk, dv, ddt, cq, ckv, cc, sp, sm, qnw, kvnw, wuq_t, wukv_t, place_t)


ATT_SCALE = 1.0 / math.sqrt(QK_DIM)
LOG2E = math.log2(math.e)
ATT_SCALE_LOG2E = ATT_SCALE * LOG2E


ATT_HEADS_PER_STEP = 4
ATT_HEADS_PER_STEP_BWD = 2


def _att_tile(s):
    return min(512, s)


def flash_fwd(q, k, vt, name):
    b, s, hd = q.shape
    t = _att_tile(s)
    nb = s // t
    th = t // 2

    hps = ATT_HEADS_PER_STEP
    hw = hps * HEAD_PAD

    def body(q_ref, k_ref, vt_ref, o_ref, lse_ref, m_s, l_s, acc):
        i = pl.program_id(2)
        m_s[...] = jnp.full_like(m_s, -jnp.inf)
        l_s[...] = jnp.zeros_like(l_s)
        acc[...] = jnp.zeros_like(acc)

        def update(j, diagonal):
            chains = [(hh, half) for hh in range(hps) for half in range(2)]
            lanes = lambda hh: slice(HEAD_PAD * hh, HEAD_PAD * (hh + 1))
            cols = lambda half: slice(th * half, th * (half + 1))
            sts = {}
            nkeys = lambda half: th if diagonal and half == 0 else t
            for hh, half in chains:
                kr = pl.ds(pl.multiple_of(j * t, t), nkeys(half))
                st = lax.dot_general(k_ref[kr, lanes(hh)], q_ref[cols(half), lanes(hh)], NT_DIMS,
                                     preferred_element_type=F32)
                if diagonal:
                    row = lax.broadcasted_iota(I32, (nkeys(half), th), 0)
                    col = lax.broadcasted_iota(I32, (nkeys(half), th), 1) + th * half
                    st = jnp.where(row <= col, st, -jnp.inf)
                sts[hh, half] = st
            pts, alphas = {}, {}
            for hh, half in chains:
                st, cs = sts[hh, half], cols(half)
                m_prev = m_s[hh, :, cs]
                m_new = jnp.maximum(m_prev, jnp.max(st, axis=0, keepdims=True))
                alpha = jnp.exp2((m_prev - m_new) * ATT_SCALE_LOG2E)
                pt = jnp.exp2((st - m_new) * ATT_SCALE_LOG2E)
                l_s[hh, :, cs] = alpha * l_s[hh, :, cs] + jnp.sum(pt, axis=0, keepdims=True)
                m_s[hh, :, cs] = m_new
                pts[hh, half], alphas[hh, half] = pt.astype(BF16), alpha
            for hh, half in chains:
                cs = cols(half)
                acc[hh, :, cs] = alphas[hh, half] * acc[hh, :, cs] + jnp.dot(
                    vt_ref[hh, j, :, 0:nkeys(half)], pts[hh, half], preferred_element_type=F32)

        def step(j, carry):
            update(j, False)
            return carry

        lax.fori_loop(0, i, step, 0)
        update(i, True)
        for hh in range(hps):
            o_ref[:, HEAD_PAD * hh:HEAD_PAD * (hh + 1)] = (acc[hh] / l_s[hh]).T
            lse_ref[hh] = m_s[hh] * ATT_SCALE + jnp.log(l_s[hh])

    qs = pl.BlockSpec((None, t, hw), lambda bb, h, i: (bb, i, h))
    ks = pl.BlockSpec((None, s, hw), lambda bb, h, i: (bb, 0, h))
    vs = pl.BlockSpec((None, hps, nb, HEAD_PAD, t), lambda bb, h, i: (bb, h, 0, 0, 0))
    ls = pl.BlockSpec((None, hps, None, 1, t), lambda bb, h, i: (bb, h, i, 0, 0))
    return pl.pallas_call(
        body, name=name, grid=(b, MLA_HEADS // hps, nb),
        in_specs=[qs, ks, vs], out_specs=[qs, ls],
        out_shape=[SDS((b, s, hd), F32), SDS((b, MLA_HEADS, nb, 1, t), F32)],
        scratch_shapes=[pltpu.VMEM((hps, 1, t), F32), pltpu.VMEM((hps, 1, t), F32), pltpu.VMEM((hps, HEAD_PAD, t), F32)],
        compiler_params=_cparams(3))(q, k, vt)


def flash_bwd(q, k, v, do, lse, dlt, name):
    b, s, hd = q.shape
    t = _att_tile(s)
    nb = s // t
    th = t // 2
    lse_r = lse
    dlt_r = dlt.reshape(b, MLA_HEADS, nb, 1, t)

    hps = ATT_HEADS_PER_STEP_BWD
    hw = hps * HEAD_PAD

    def body(q_ref, k_ref, v_ref, do_ref, lse_ref, dlt_ref, dq_ref, dk_ref, dv_ref, dq_s, dk_s, dv_s):
        dq_s[...] = jnp.zeros_like(dq_s)
        dk_s[...] = jnp.zeros_like(dk_s)
        dv_s[...] = jnp.zeros_like(dv_s)

        def tile(j, i, diagonal):
            chains = [(hh, half) for hh in range(hps) for half in range(2)]
            lanes = lambda hh: slice(HEAD_PAD * hh, HEAD_PAD * (hh + 1))
            keys = lambda half: pl.ds(pl.multiple_of(j * t + th * half, th), th)
            q0 = lambda half: th if diagonal and half == 1 else 0
            qsel = lambda half: pl.ds(pl.multiple_of(i * t + q0(half), th), t - q0(half))
            sts, dpts = {}, {}
            for hh, half in chains:
                ls_, ks, qs, nq = lanes(hh), keys(half), qsel(half), t - q0(half)
                st = lax.dot_general(k_ref[ks, ls_], q_ref[qs, ls_], NT_DIMS, preferred_element_type=F32)
                if diagonal:
                    row = lax.broadcasted_iota(I32, (th, nq), 0) + th * half
                    col = lax.broadcasted_iota(I32, (th, nq), 1) + q0(half)
                    st = jnp.where(row <= col, st, -jnp.inf)
                sts[hh, half] = st
                dpts[hh, half] = lax.dot_general(v_ref[ks, ls_], do_ref[qs, ls_], NT_DIMS, preferred_element_type=F32)
            pts, dsts = {}, {}
            for hh, half in chains:
                qcols = slice(q0(half), t)
                pt = jnp.exp2(sts[hh, half] * ATT_SCALE_LOG2E - lse_ref[hh, i][:, qcols] * LOG2E)
                pts[hh, half] = pt.astype(BF16)
                dsts[hh, half] = (pt * (dpts[hh, half] - dlt_ref[hh, i][:, qcols])).astype(BF16)
            for hh, half in chains:
                ls_, ks, qs = lanes(hh), keys(half), qsel(half)
                dv_s[ks, ls_] += jnp.dot(pts[hh, half], do_ref[qs, ls_], preferred_element_type=F32)
                dk_s[ks, ls_] += jnp.dot(dsts[hh, half], q_ref[qs, ls_], preferred_element_type=F32)
                dq_s[qs, ls_] += lax.dot_general(dsts[hh, half], k_ref[ks, ls_], TN_DIMS, preferred_element_type=F32)

        def key_tile(j, carry):
            tile(j, j, True)

            def query_tile(i, c2):
                tile(j, i, False)
                return c2

            lax.fori_loop(j + 1, nb, query_tile, 0)
            return carry

        lax.fori_loop(0, nb, key_tile, 0)
        dq_ref[...] = (dq_s[...] * ATT_SCALE).astype(BF16)
        dk_ref[...] = (dk_s[...] * ATT_SCALE).astype(BF16)
        dv_ref[...] = dv_s[...].astype(BF16)

    hs = pl.BlockSpec((None, s, hw), lambda bb, h: (bb, 0, h))
    ls = pl.BlockSpec((None, hps, nb, 1, t), lambda bb, h: (bb, h, 0, 0, 0))
    return pl.pallas_call(
        body, name=name, grid=(b, MLA_HEADS // hps),
        in_specs=[hs, hs, hs, hs, ls, ls], out_specs=[hs, hs, hs],
        out_shape=[SDS((b, s, hd), BF16)] * 3, scratch_shapes=[pltpu.VMEM((s, hw), F32)] * 3,
        compiler_params=_cparams(2))(q, k, v, do, lse_r, dlt_r)


def out_proj(ys, attn, mnw, wo, x, gate, above, name):
    b, s, d = x.shape
    tm = min(512, s)

    def body(ys_ref, at_ref, mnw_ref, wo_ref, x_ref, g_ref, nw_ref, sc_ref, sh_ref, xn_ref, o_ref, ym_ref, h_ref):
        av = at_ref[...]
        ym = (av * _rms(av) * mnw_ref[...]).astype(BF16)
        ym_ref[...] = ym
        o = jnp.dot(ys_ref[...], wo_ref[0:D_SSD, :], preferred_element_type=F32)
        o += jnp.dot(ym, wo_ref[D_SSD:2 * D_SSD, :], preferred_element_type=F32)
        xn = x_ref[...] + g_ref[...] * o
        xn_ref[...] = xn
        o_ref[...] = o.astype(BF16)
        h_ref[...] = _norm_mod_tile(xn, nw_ref, sc_ref, sh_ref)

    return pl.pallas_call(
        body, name=name, grid=(b, s // tm),
        in_specs=[_row(tm, D_SSD), _row(tm, D_SSD), _full((1, D_SSD)), _full(wo.shape), _row(tm, d), _bvec(d),
                  _full((1, d)), _bvec(d), _bvec(d)],
        out_specs=[_row(tm, d), _row(tm, d), _row(tm, D_SSD), _row(tm, d)],
        out_shape=[SDS((b, s, d), F32), SDS((b, s, d), BF16), SDS((b, s, D_SSD), BF16), SDS((b, s, d), BF16)],
        compiler_params=_cparams(2))(ys, attn, mnw, wo, x, gate, *above)


def out_proj_bwd(dout, attn, mnw, wo, name):
    b, s, d = dout.shape
    tm = min(512, s)

    def body(do_ref, at_ref, mnw_ref, wo_ref, dys_ref, dat_ref, dlt_ref, dw_ref):
        lane = lax.broadcasted_iota(I32, (tm, LANES), 1)
        @pl.when(_first_step())
        def _():
            dw_ref[...] = jnp.zeros_like(dw_ref)
        dov = do_ref[...]
        dys_ref[...] = lax.dot_general(dov, wo_ref[0:D_SSD, :], NT_DIMS, preferred_element_type=F32)
        dym = lax.dot_general(dov, wo_ref[D_SSD:2 * D_SSD, :], NT_DIMS, preferred_element_type=F32)
        av = at_ref[...]
        r = _rms(av)
        n = av * r
        dw_ref[...] += jnp.sum(dym * n, axis=0, keepdims=True)
        dat = _rms_bwd(dym * mnw_ref[...], n, r)
        dat_ref[...] = dat.astype(BF16)
        prod = dat * av
        cols = jnp.zeros((tm, LANES), F32)
        for h in range(MLA_HEADS):
            cols += jnp.where(lane == h, jnp.sum(prod[:, HEAD_PAD * h:HEAD_PAD * (h + 1)], axis=1, keepdims=True), 0.0)
        dlt_ref[...] = cols.T[0:MLA_HEADS, :]

    return pl.pallas_call(
        body, name=name, grid=(b, s // tm),
        in_specs=[_row(tm, d), _row(tm, D_SSD), _full((1, D_SSD)), _full(wo.shape)],
        out_specs=[_row(tm, D_SSD), _row(tm, D_SSD),
                   pl.BlockSpec((None, MLA_HEADS, tm), lambda bb, i: (bb, 0, i)), _full((1, D_SSD))],
        out_shape=[SDS((b, s, D_SSD), F32), SDS((b, s, D_SSD), BF16), SDS((b, MLA_HEADS, s), F32),
                   SDS((1, D_SSD), F32)],
        compiler_params=_cparams(2))(dout, attn, mnw, wo)


def adaln_fwd(c_all, w_ada, b_ada, name):
    nb, d = c_all.shape
    n = w_ada.shape[1]

    def body(c_ref, w_ref, b_ref, m_ref, ca_ref):
        cv = c_ref[...]
        ca = (cv * _sigmoid(cv)).astype(BF16)
        ca_ref[...] = ca
        m_ref[...] = jnp.dot(ca, w_ref[...].astype(BF16), preferred_element_type=F32) + b_ref[...]

    return pl.pallas_call(
        body, name=name, out_shape=[SDS((nb, n), F32), SDS((nb, d), BF16)],
        compiler_params=pltpu.CompilerParams(vmem_limit_bytes=VMEM_LIMIT))(c_all, w_ada, b_ada)


def adaln_bwd(c_act, dmod_cols, name):
    d, n = c_act.shape[1], dmod_cols.shape[1]

    def body(c_ref, dm_ref, gw_ref):
        gw_ref[...] = lax.dot_general(c_ref[...], dm_ref[...].astype(BF16), TN_DIMS, preferred_element_type=F32)

    return pl.pallas_call(
        body, name=name, out_shape=SDS((d, n), F32),
        compiler_params=pltpu.CompilerParams(vmem_limit_bytes=VMEM_LIMIT))(c_act, dmod_cols)


def sum_rows(x, name):
    def body(x_ref, o_ref):
        o_ref[...] = jnp.sum(x_ref[...], axis=0, keepdims=True)
    return pl.pallas_call(body, name=name, out_shape=SDS((1, x.shape[1]), F32))(x)


def squeeze_heads(x, et_mat, name):
    def body(x_ref, et_ref, o_ref):
        xv = jnp.broadcast_to(x_ref[...], (8, x.shape[1]))
        o_ref[...] = _dot_01(xv, et_ref[...], 3)[0:1, :]
    return pl.pallas_call(body, name=name, out_shape=SDS((1, LANES), F32))(x, et_mat)


def sum_blocks(x, name):
    n, r, c = x.shape
    tr = next(cand for cand in (256, 128, 64, 32, 16, 8) if r % cand == 0)

    def body(x_ref, o_ref):
        acc = x_ref[0].astype(F32)
        for k in range(1, n):
            acc += x_ref[k].astype(F32)
        o_ref[...] = acc

    return pl.pallas_call(
        body, name=name, grid=(r // tr,), in_specs=[pl.BlockSpec((n, tr, c), lambda i: (0, i, 0))],
        out_specs=pl.BlockSpec((tr, c), lambda i: (i, 0)), out_shape=SDS((r, c), F32),
        compiler_params=_cparams(1))(x)


def _adam_math(w, g, m, v):
    m = ADAM_B1 * m + (1.0 - ADAM_B1) * g
    v = ADAM_B2 * v + (1.0 - ADAM_B2) * (g * g)
    m_hat = m / (1.0 - ADAM_B1 ** ADAM_STEP)
    v_hat = v / (1.0 - ADAM_B2 ** ADAM_STEP)
    return -ADAM_LR * (m_hat / (jnp.sqrt(v_hat) + ADAM_EPS) + ADAM_WD * w), m, v


def adamw(w, g, m, v, name):
    r, c = w.shape[-2:]
    tr = r
    for cand in (512, 256, 128, 64, 32, 16, 8):
        if r % cand == 0 and cand * c * 4 <= 2 * 1024 * 1024:
            tr = cand
            break

    def body(w_ref, g_ref, m_ref, v_ref, d_ref, mo_ref, vo_ref):
        d_ref[...], mo_ref[...], vo_ref[...] = _adam_math(w_ref[...], g_ref[...], m_ref[...], v_ref[...])

    def spec(a):
        return pl.BlockSpec((tr, c), lambda i: (i, 0)) if a.ndim == 2 else pl.BlockSpec((None, tr, c), lambda i: (0, i, 0))

    return pl.pallas_call(
        body, name=name, grid=(r // tr,), in_specs=[spec(w), spec(g), spec(m), spec(v)], out_specs=[spec(w)] * 3,
        out_shape=[SDS(w.shape, F32)] * 3, compiler_params=_cparams(1))(w, g, m, v)


def adamw_blocks(w, blocks, m, v, name):
    r, c = w.shape
    tr = next((cand for cand in (128, 64, 32, 16, 8) if r % cand == 0), r)

    def body(w_ref, b_ref, m_ref, v_ref, g_ref, d_ref, mo_ref, vo_ref):
        g = b_ref[0].astype(F32)
        for k in range(1, N_DEV):
            g += b_ref[k].astype(F32)
        g_ref[...] = g
        d_ref[...], mo_ref[...], vo_ref[...] = _adam_math(w_ref[...], g, m_ref[...], v_ref[...])

    spec = pl.BlockSpec((tr, c), lambda i: (i, 0))
    return pl.pallas_call(
        body, name=name, grid=(r // tr,),
        in_specs=[spec, pl.BlockSpec((N_DEV, tr, c), lambda i: (0, i, 0)), spec, spec], out_specs=[spec] * 4,
        out_shape=[SDS((r, c), F32)] * 4, compiler_params=_cparams(1))(w, blocks, m, v)


def adamw_many(ws, gs, ms, vs, name):
    n = len(ws)

    def body(*refs):
        w_r, g_r, m_r, v_r = (refs[k * n:(k + 1) * n] for k in range(4))
        d_r, mo_r, vo_r = (refs[(4 + k) * n:(5 + k) * n] for k in range(3))
        for k in range(n):
            d_r[k][...], mo_r[k][...], vo_r[k][...] = _adam_math(w_r[k][...], g_r[k][...], m_r[k][...], v_r[k][...])

    shapes = [SDS(w.shape, F32) for w in ws]
    outs = pl.pallas_call(body, name=name, out_shape=shapes * 3)(*ws, *gs, *ms, *vs)
    return outs[:n], outs[n:2 * n], outs[2 * n:]


TRANSPOSED = ("ffn1_w_gate", "ffn1_w_up", "ffn2_w_gate", "ffn2_w_up", "w_in", "w_ukv", "w_uq")
GATHER_GROUPS = (("ffn1_w_gate", "ffn1_w_up"), ("ffn1_w_down",),
                 ("w_in", "w_ukv", "w_uq", "w_out", "ffn2_w_gate", "ffn2_w_up", "ffn2_w_down"))
GRAD_GROUPS = (("ffn2", ("ffn2_w_gate", "ffn2_w_up", "ffn2_w_down")), ("mixer", ("w_out", "w_in", "w_ukv", "w_uq")),
               ("ffn1_down", ("ffn1_w_down",)), ("ffn1_gate", ("ffn1_w_gate",)), ("ffn1_up", ("ffn1_w_up",)))


def _shard_view(name, w):
    return w[0].T if name in TRANSPOSED else w[0]


def _shard_unview(name, t):
    return t.T[None] if name in TRANSPOSED else t[None]


def _grad_blocks(name, gw):
    if name == "w_in":
        return _in_proj_rows_inv(gw).reshape(N_DEV, -1, D_MODEL)
    if name == "w_ukv":
        hd = MLA_HEADS * HEAD_PAD
        return jnp.concatenate([gw[:hd].reshape(MLA_HEADS, HEAD_PAD, KV_LORA)[:, :QK_NOPE],
                                gw[hd:].reshape(MLA_HEADS, V_HEAD, KV_LORA)], axis=1)
    if name == "w_uq":
        return gw.reshape(MLA_HEADS, HEAD_PAD, Q_LORA)[:, :QK_DIM]
    return gw.reshape(N_DEV, -1, D_MODEL)


def _pack_rows(arrs):
    parts = []
    for a in arrs:
        flat = a.reshape(-1).astype(F32)
        pad = (-flat.shape[0]) % D_MODEL
        if pad:
            flat = jnp.pad(flat, (0, pad))
        parts.append(flat.reshape(-1, D_MODEL))
    out = jnp.concatenate(parts, axis=0)
    pad = (-out.shape[0]) % 8
    if pad:
        out = jnp.pad(out, ((0, pad), (0, 0)))
    return out


def _unpack_rows(packed, shapes):
    out, row = [], 0
    for shp in shapes:
        n = math.prod(shp)
        nrow = -(-n // D_MODEL)
        out.append(packed[row:row + nrow].reshape(-1)[:n].reshape(shp))
        row += nrow
    return out


def _in_proj_rows(w_t):
    return jnp.concatenate([w_t[0:2560], w_t[2576:2960], w_t[2960:3216], w_t[2560:2576], w_t[3216:3248],
                            jnp.zeros((D_IN_PAD - D_IN, D_MODEL), w_t.dtype)], axis=0)


def _in_proj_rows_inv(d):
    return jnp.concatenate([d[0:2560], d[3200:3216], d[2560:2944], d[2944:3200], d[3216:3248]], axis=0)


def _rope_tables(positions):
    inv_freq = ROPE_THETA ** (-jnp.arange(0, QK_ROPE, 2, dtype=F32) / QK_ROPE)
    ang = positions[..., None].astype(F32) * inv_freq
    cos, sin = jnp.cos(ang), jnp.sin(ang)
    one = jnp.ones(ang.shape[:2] + (QK_NOPE,), F32)
    zero = jnp.zeros_like(one)
    z16, z32, o32 = zero[..., :16], zero[..., :32], one[..., :32]
    cc = jnp.concatenate([one, cos, cos, o32], axis=-1)
    sp = jnp.concatenate([zero, z16, sin, z32], axis=-1)
    sm = jnp.concatenate([zero, -sin, z16, z32], axis=-1)
    return cc, sp, sm


def weight_views(gathered):
    full = lambda name: gathered[name].reshape(-1, gathered[name].shape[2])
    ukv = full("w_ukv").reshape(MLA_HEADS, QK_NOPE + V_HEAD, KV_LORA)
    wukv_t = jnp.concatenate([jnp.pad(ukv[:, :QK_NOPE], ((0, 0), (0, HEAD_PAD - QK_NOPE), (0, 0))).reshape(-1, KV_LORA),
                              ukv[:, QK_NOPE:].reshape(-1, KV_LORA)], axis=0)
    uq = full("w_uq").reshape(MLA_HEADS, QK_DIM, Q_LORA)
    wuq_t = jnp.pad(uq, ((0, 0), (0, HEAD_PAD - QK_DIM), (0, 0))).reshape(-1, Q_LORA)
    return dict(wg1_t=full("ffn1_w_gate"), wu1_t=full("ffn1_w_up"), wd1=full("ffn1_w_down"),
                wg2_t=full("ffn2_w_gate"), wu2_t=full("ffn2_w_up"), wd2=full("ffn2_w_down"),
                wo=full("w_out"), win_t=_in_proj_rows(full("w_in")), wukv_t=wukv_t, wuq_t=wuq_t)


def _ffn_bwd(tag, dxn, do, dgate, x, h, gg, uu, a, sc, norm_w, wg_t, wu_t, wd, below):
    f2 = wd.shape[0] // 2
    dwd = mm_tn(a, do, f2, D_MODEL, tag + "_dwd")
    dgg, duu = ffn_dact(do, wd, gg, uu, tag + "_dact")
    dwg_t = mm_tn(dgg, h, f2, D_MODEL, tag + "_dwg")
    dwu_t = mm_tn(duu, h, f2, D_MODEL, tag + "_dwu")
    dx, dsc, dsh, dnw, *nxt = dh_norm_bwd([dgg, duu], [wg_t, wu_t], x, dxn, norm_w, sc, tag + "_dh", below)
    return dx, (dsh, dsc, dgate), dnw, (dwg_t, dwu_t, dwd), nxt


def local_step(x, tgt, positions, mod, wv, p):
    nb, s, d = x.shape
    sh1, sc1, g1, sh2, sc2, g2, sh3, sc3, g3 = mod
    cc, sp, sm = _rope_tables(positions)
    lane_head = jnp.arange(D_SSD, dtype=I32)[None, :] // SSD_HEAD_DIM
    e_mat = (lane_head == jnp.arange(LANES, dtype=I32)[:, None]).astype(BF16)
    et_mat = e_mat.T
    rr, cl = jnp.arange(LANES, dtype=I32)[:, None], jnp.arange(LANES, dtype=I32)[None, :]
    place = ((cl == rr + (QK_NOPE - SSD_HEADS)) & (rr >= SSD_HEADS) & (rr < SSD_HEADS + QK_ROPE)).astype(F32)
    dtb = jnp.pad(p["dt_bias"], ((0, 0), (0, LANES - SSD_HEADS)))
    alog = jnp.pad(p["a_log"], ((0, 0), (0, LANES - SSD_HEADS)))
    dskip_e = jnp.repeat(p["d_skip"], SSD_HEAD_DIM, axis=1)

    h1 = norm_mod(x, p["norm_ffn1"], sc1, sh1, "ffn1_norm")
    gg1, uu1, a1 = ffn_up(h1, wv["wg1_t"], wv["wu1_t"], "ffn1_up")
    x1, o1, h2 = ffn_down(a1, wv["wd1"], x, g1, 0.5, "ffn1_down", (p["norm_mix"], sc2, sh2))
    z, u, cq, ckv, misc = in_proj(h2, wv["win_t"], "in_proj")
    xs, bm, cm_ = conv_fwd(u, p["conv_w"], p["conv_b"], "conv_fwd")
    ys, y, prev = ssd_fwd(xs, bm, cm_, misc, z, dtb, alog, dskip_e, p["ssd_norm_w"], e_mat, "ssd_fwd")
    q, k, v, vt, qn, kvn = qkv_fwd(cq, ckv, misc, cc, sp, sm, p["q_norm_w"], p["kv_norm_w"], wv["wuq_t"], wv["wukv_t"],
                               place, "qkv_fwd")
    attn, lse = flash_fwd(q, k, vt, "flash_fwd")
    x2, o2, ym, h3 = out_proj(ys, attn, p["mla_norm_w"], wv["wo"], x1, g2, (p["norm_ffn2"], sc3, sh3), "out_proj")
    gg3, uu3, a3 = ffn_up(h3, wv["wg2_t"], wv["wu2_t"], "ffn2_up")
    loss, dx3, dnfin, do3, dg3 = ffn_down_final(a3, wv["wd2"], x2, g3, 0.5, p["norm_final"], tgt, "ffn2_down_loss")

    dx2, dmod3, dnf2, (dwg2, dwu2, dwd2), (dout, dg2) = _ffn_bwd(
        "ffn2", dx3, do3, dg3, x2, h3, gg3, uu3, a3, sc3, p["norm_ffn2"], wv["wg2_t"], wv["wu2_t"], wv["wd2"],
        (o2, g2, 1.0))
    dys, dattn, dlt, dmlan = out_proj_bwd(dout, attn, p["mla_norm_w"], wv["wo"], "out_proj_bwd")
    dwo = jnp.concatenate([mm_tn(ys, dout, D_SSD, D_MODEL, "dwo_ssd"), mm_tn(ym, dout, D_SSD, D_MODEL, "dwo_mla")], axis=0)
    dxs, dbm, dcm, dz, ddt, dssdn, ddsk_lane, ddtb, dalog = ssd_bwd(
        dys, y, z, xs, bm, cm_, misc, prev, dtb, alog, dskip_e, p["ssd_norm_w"], e_mat, et_mat, "ssd_bwd")
    dq, dk, dv = flash_bwd(q, k, v, dattn, lse, dlt, "flash_bwd")
    dcq, dckv, dmisc, dqp, dkvc, dqn, dkvn = qkv_bwd(dq, dk, dv, ddt, cq, ckv, cc, sp, sm, p["q_norm_w"], p["kv_norm_w"],
                                                     wv["wuq_t"], wv["wukv_t"], place.T, "qkv_bwd")
    dwuq = mm_tn(dqp, qn, MLA_HEADS * HEAD_PAD, Q_LORA, "dwuq")
    dwukv = mm_tn(dkvc, kvn, MLA_HEADS * HEAD_PAD, KV_LORA, "dwukv")
    dvv, dconv = conv_bwd_a(dxs, dbm, dcm, u, p["conv_w"], p["conv_b"], "conv_bwd_a")
    dconv = conv_grads_fold(dconv, "conv_grads_fold")
    du = conv_bwd_b(dvv, p["conv_w"], "conv_bwd_b")
    dproj = [dz, du, dcq, dckv, dmisc]
    dwin = mm_tn_blocks(dproj, h2, "dwin")
    dx1, dsc2, dsh2, dnmix, do1, dg1 = dh_norm_bwd(dproj, [wv["win_t"]], x1, dx2, p["norm_mix"], sc2, "mix_dh",
                                                   (o1, g1, 0.5))
    dx0, dmod1, dnf1, (dwg1, dwu1, dwd1), _ = _ffn_bwd(
        "ffn1", dx1, do1, dg1, x, h1, gg1, uu1, a1, sc1, p["norm_ffn1"], wv["wg1_t"], wv["wu1_t"], wv["wd1"], None)

    dmod = jnp.concatenate([*dmod1, dsh2, dsc2, dg2, *dmod3], axis=1).reshape(nb, N_MOD * d)
    return dict(
        loss=loss, dx=dx0, dmod=dmod, norm_ffn1=dnf1, norm_mix=dnmix, norm_ffn2=dnf2, norm_final=dnfin,
        ssd_norm_w=dssdn, mla_norm_w=dmlan, q_norm_w=dqn, kv_norm_w=dkvn,
        dt_bias=ddtb[:, :SSD_HEADS], a_log=dalog[:, :SSD_HEADS],
        d_skip=squeeze_heads(ddsk_lane, et_mat, "d_skip_heads")[:, :SSD_HEADS],
        conv_b=dconv[4:5], conv_w=dconv[0:4],
        gw=dict(ffn1_w_gate=dwg1, ffn1_w_up=dwu1, ffn1_w_down=dwd1, ffn2_w_gate=dwg2, ffn2_w_up=dwu2, ffn2_w_down=dwd2,
                w_out=dwo, w_in=dwin, w_ukv=dwukv, w_uq=dwuq))


def kernel(x, c, positions, w_ada, b_ada, norm_ffn1, ffn1_w_gate, ffn1_w_up, ffn1_w_down, norm_mix, w_in, conv_w, conv_b, dt_bias, a_log, d_skip, ssd_norm_w, q_norm_w, w_uq, kv_norm_w, w_ukv, mla_norm_w, w_out, norm_ffn2, ffn2_w_gate, ffn2_w_up, ffn2_w_down, norm_final, loss_target, m_w_ada, m_b_ada, m_norm_ffn1, m_ffn1_w_gate, m_ffn1_w_up, m_ffn1_w_down, m_norm_mix, m_w_in, m_conv_w, m_conv_b, m_dt_bias, m_a_log, m_d_skip, m_ssd_norm_w, m_q_norm_w, m_w_uq, m_kv_norm_w, m_w_ukv, m_mla_norm_w, m_w_out, m_norm_ffn2, m_ffn2_w_gate, m_ffn2_w_up, m_ffn2_w_down, m_norm_final, v_w_ada, v_b_ada, v_norm_ffn1, v_ffn1_w_gate, v_ffn1_w_up, v_ffn1_w_down, v_norm_mix, v_w_in, v_conv_w, v_conv_b, v_dt_bias, v_a_log, v_d_skip, v_ssd_norm_w, v_q_norm_w, v_w_uq, v_kv_norm_w, v_w_ukv, v_mla_norm_w, v_w_out, v_norm_ffn2, v_ffn2_w_gate, v_ffn2_w_up, v_ffn2_w_down, v_norm_final):
    names = ["w_ada", "b_ada", "norm_ffn1", "ffn1_w_gate", "ffn1_w_up", "ffn1_w_down", "norm_mix", "w_in", "conv_w",
             "conv_b", "dt_bias", "a_log", "d_skip", "ssd_norm_w", "q_norm_w", "w_uq", "kv_norm_w", "w_ukv",
             "mla_norm_w", "w_out", "norm_ffn2", "ffn2_w_gate", "ffn2_w_up", "ffn2_w_down", "norm_final"]
    W = dict(zip(names, (w_ada, b_ada, norm_ffn1, ffn1_w_gate, ffn1_w_up, ffn1_w_down, norm_mix, w_in, conv_w, conv_b, dt_bias, a_log, d_skip, ssd_norm_w, q_norm_w, w_uq, kv_norm_w, w_ukv, mla_norm_w, w_out, norm_ffn2, ffn2_w_gate, ffn2_w_up, ffn2_w_down, norm_final)))
    M = dict(zip(names, (m_w_ada, m_b_ada, m_norm_ffn1, m_ffn1_w_gate, m_ffn1_w_up, m_ffn1_w_down, m_norm_mix, m_w_in, m_conv_w, m_conv_b, m_dt_bias, m_a_log, m_d_skip, m_ssd_norm_w, m_q_norm_w, m_w_uq, m_kv_norm_w, m_w_ukv, m_mla_norm_w, m_w_out, m_norm_ffn2, m_ffn2_w_gate, m_ffn2_w_up, m_ffn2_w_down, m_norm_final)))
    V = dict(zip(names, (v_w_ada, v_b_ada, v_norm_ffn1, v_ffn1_w_gate, v_ffn1_w_up, v_ffn1_w_down, v_norm_mix, v_w_in, v_conv_w, v_conv_b, v_dt_bias, v_a_log, v_d_skip, v_ssd_norm_w, v_q_norm_w, v_w_uq, v_kv_norm_w, v_w_ukv, v_mla_norm_w, v_w_out, v_norm_ffn2, v_ffn2_w_gate, v_ffn2_w_up, v_ffn2_w_down, v_norm_final)))

    nb, s, d = x.shape
    me = 4 * lax.axis_index("x") + 2 * lax.axis_index("y") + lax.axis_index("c")
    n_ada = w_ada.shape[2]

    taps, n_cw = conv_w.shape[1:]
    (cg,) = all_gather8([_pack_rows([c, conv_w[0]])], "gather_c")
    c_all = cg[:, 0:nb].reshape(N_DEV * nb, d)
    conv_w_full = cg[:, nb, 0:taps * n_cw].reshape(N_DEV, taps, n_cw).transpose(1, 0, 2).reshape(taps, N_DEV * n_cw)
    shards = [[_shard_view(name, W[name]).astype(BF16) for name in group] for group in GATHER_GROUPS]
    gathered = dict(zip(GATHER_GROUPS[0], all_gather8(shards[0], "gather_w_ffn1")))

    b_ada_cols = lax.dynamic_slice(b_ada, (0, me * n_ada), (1, n_ada))
    mod_cols, c_act = adaln_fwd(c_all, w_ada[0], b_ada_cols, "adaln_fwd")
    (mod_g,) = all_gather8([mod_cols], "gather_mod")
    gathered, mod_g, shards = lax.optimization_barrier((gathered, mod_g, shards))
    gathered.update(zip(GATHER_GROUPS[1], sc_all_gather8(shards[1], "gather_w_ffn1_down", 1)))
    gathered.update(zip(GATHER_GROUPS[2], sc_all_gather8(shards[2], "gather_w_rest", 7)))
    wv = weight_views(gathered)
    mod = lax.dynamic_slice(mod_g, (0, me * nb, 0), (N_DEV, nb, n_ada)).transpose(1, 0, 2).reshape(nb, N_MOD, 1, d)
    mod = [mod[:, k] for k in range(N_MOD)]

    P = dict(W)
    P["conv_w"] = conv_w_full
    P["norm_final"] = norm_final.reshape(1, d)
    R = local_step(x, loss_target, positions, mod, wv, P)

    dmod = R["dmod"]
    partial_shapes = [(1,), (1, d), (1, d), (1, d), (1, d), (1, d), (1, d), (1, Q_LORA), (1, KV_LORA),
                      (1, SSD_HEADS), (1, SSD_HEADS), (1, SSD_HEADS), (1, D_CONV), (4, D_CONV), (1, N_MOD * d),
                      (nb, N_MOD * d)]
    partial = _pack_rows([R["loss"][0, :1], R["norm_ffn1"], R["norm_mix"], R["norm_ffn2"], R["norm_final"],
                          R["ssd_norm_w"], R["mla_norm_w"], R["q_norm_w"], R["kv_norm_w"],
                          R["dt_bias"], R["a_log"], R["d_skip"], R["conv_b"], R["conv_w"],
                          sum_rows(dmod, "dmod_rows"), dmod])
    (partial_g,) = all_gather8([partial], "gather_partials")
    (loss, g_nf1, g_nmix, g_nf2, g_nfin, g_ssdn, g_mlan, g_qn, g_kvn, g_dtb, g_alog, g_dskip, g_convb, g_convw,
     g_bada, _) = _unpack_rows(sum_blocks(partial_g, "sum_partials"), partial_shapes)
    dmod_row = sum(-(-math.prod(shp) // D_MODEL) for shp in partial_shapes[:-1])
    dmod_all = partial_g[:, dmod_row:dmod_row + nb * N_MOD].reshape(N_DEV * nb, N_MOD * d)
    g_wada = adaln_bwd(c_act, lax.dynamic_slice(dmod_all, (0, me * n_ada), (N_DEV * nb, n_ada)), "adaln_bwd")
    n_cw = conv_w.shape[2]
    G = {"w_ada": g_wada[None], "b_ada": g_bada, "norm_ffn1": g_nf1, "norm_mix": g_nmix, "norm_ffn2": g_nf2,
         "norm_final": g_nfin.reshape(d), "ssd_norm_w": g_ssdn, "mla_norm_w": g_mlan, "q_norm_w": g_qn,
         "kv_norm_w": g_kvn, "dt_bias": g_dtb, "a_log": g_alog, "d_skip": g_dskip, "conv_b": g_convb,
         "conv_w": lax.dynamic_slice(g_convw, (0, me * n_cw), (4, n_cw))[None]}

    DW, NM, NV = {}, {}, {}
    gw = R["gw"]
    for k, (tag, group) in enumerate(GRAD_GROUPS):
        send = [_grad_blocks(name, gw[name]).reshape(N_DEV, *_shard_view(name, W[name]).shape) for name in group]
        recv = sc_all_to_all8(send, "exchange_" + tag, 2 + k)
        for name, blocks in zip(group, recv):
            res = adamw_blocks(_shard_view(name, W[name]), blocks, _shard_view(name, M[name]), _shard_view(name, V[name]),
                               "adamw_" + name)
            G[name], DW[name], NM[name], NV[name] = [_shard_unview(name, t) for t in res]
    DW["w_ada"], NM["w_ada"], NV["w_ada"] = adamw(w_ada, g_wada, m_w_ada, v_w_ada, "adamw_w_ada")
    small = [n for n in names if n not in DW]
    as2d = lambda a: a.reshape(-1, a.shape[-1])
    outs = adamw_many([as2d(W[n]) for n in small], [as2d(G[n]) for n in small], [as2d(M[n]) for n in small],
                      [as2d(V[n]) for n in small], "adamw_small")
    for res, dst in zip(outs, (DW, NM, NV)):
        for n, t in zip(small, res):
            dst[n] = t.reshape(W[n].shape)
    return (loss.reshape(()), R["dx"], *[G[n] for n in names], *[DW[n] for n in names], *[NM[n] for n in names],
            *[NV[n] for n in names])
```

```python
import math

import jax
import jax.numpy as jnp
from jax import lax
from jax.experimental import pallas as pl
from jax.experimental.pallas import tpu as pltpu
from jax.experimental.pallas import tpu_sc as plsc

F32, BF16, I32 = jnp.float32, jnp.bfloat16, jnp.int32
HI = lax.Precision.HIGHEST
SDS = jax.ShapeDtypeStruct
MESH = pl.DeviceIdType.MESH

D_MODEL = 1024
D_FF = 2816
D_SSD = 1024
SSD_HEADS = 16
SSD_HEAD_DIM = 64
SSD_GROUPS = 2
SSD_STATE = 128
CHUNK = 128
MLA_HEADS = 8
QK_NOPE = 64
QK_ROPE = 32
QK_DIM = 96
V_HEAD = 128
Q_LORA = 384
KV_LORA = 256
ROPE_THETA = 10000.0
N_MOD = 9
EPS = 1e-6
D_CONV = 1536
D_IN = 3248
D_IN_PAD = 3328
HEAD_PAD = 128
N_DEV = 8
ADAM_LR, ADAM_B1, ADAM_B2, ADAM_EPS, ADAM_WD, ADAM_STEP = 0.001, 0.9, 0.999, 1e-08, 0.01, 10

SAVED_ACT = BF16
VMEM_LIMIT = 56 * 1024 * 1024
LANES = 128
NT_DIMS = (((1,), (1,)), ((), ()))
TN_DIMS = (((0,), (0,)), ((), ()))


def _cparams(n_axes):
    return pltpu.CompilerParams(dimension_semantics=("arbitrary",) * n_axes, vmem_limit_bytes=VMEM_LIMIT)


def _row(tm, d):
    return pl.BlockSpec((None, tm, d), lambda b, i: (b, i, 0))


def _bvec(d):
    return pl.BlockSpec((None, 1, d), lambda b, i: (b, 0, 0))


def _full(shape):
    n = len(shape)
    return pl.BlockSpec(shape, lambda *_: (0,) * n)


def _sigmoid(x):
    return 1.0 / (1.0 + jnp.exp(-x))


def _softplus(x):
    return jnp.maximum(x, 0.0) + jnp.log(1.0 + jnp.exp(-jnp.abs(x)))


def _rms(x):
    return lax.rsqrt(jnp.mean(x * x, axis=-1, keepdims=True) + EPS)


def _rms_bwd(dn, n, r):
    return r * (dn - n * jnp.mean(dn * n, axis=-1, keepdims=True))


def _first_step():
    return (pl.program_id(0) == 0) & (pl.program_id(1) == 0)


def _gather_copies(x_refs, out_refs, send_sems, recv_sems, local_sems):
    mx, my, mc = lax.axis_index("x"), lax.axis_index("y"), lax.axis_index("c")
    me, sibling = (mx, my, mc), (mx, my, 1 - mc)
    chips = [(1 - mx, my), (mx, 1 - my), (1 - mx, 1 - my)]

    def copy(a, k, block, to, src=None):
        rows = out_refs[a].at[4 * block[0] + 2 * block[1] + block[2]]
        return pltpu.make_async_remote_copy(
            src_ref=rows if src is None else src, dst_ref=rows,
            send_sem=send_sems.at[7 * a + k], recv_sem=recv_sems.at[7 * a + k], device_id=to, device_id_type=MESH)

    arrays = range(len(x_refs))
    mine = [pltpu.make_async_copy(x_refs[a], out_refs[a].at[4 * mx + 2 * my + mc], local_sems.at[a]) for a in arrays]
    first = [[copy(a, 0, me, sibling, src=x_refs[a])] + [copy(a, 1 + j, me, (*chip, mc), src=x_refs[a])
                                                          for j, chip in enumerate(chips)] for a in arrays]
    passed = [[copy(a, 4 + j, (*chip, mc), sibling) for j, chip in enumerate(chips)] for a in arrays]
    for a in arrays:
        mine[a].start()
        for cp in first[a]:
            cp.start()
    for a in arrays:
        for j, chip in enumerate(chips):
            copy(a, 1 + j, (*chip, mc), me).wait_recv()
            passed[a][j].start()
    for a in arrays:
        copy(a, 0, sibling, me).wait_recv()
        for j, chip in enumerate(chips):
            copy(a, 4 + j, (*chip, 1 - mc), me).wait_recv()
    for a in arrays:
        for cp in first[a] + passed[a]:
            cp.wait_send()
        mine[a].wait()


def _gather_peers():
    mx, my, mc = lax.axis_index("x"), lax.axis_index("y"), lax.axis_index("c")
    return [(mx, my, 1 - mc), (1 - mx, my, mc), (mx, 1 - my, mc), (1 - mx, 1 - my, mc)]


def _comm_scratch(n):
    return [pltpu.SemaphoreType.DMA((7 * n,)), pltpu.SemaphoreType.DMA((7 * n,)), pltpu.SemaphoreType.DMA((n,))]


def all_gather8(xs, name):
    n = len(xs)

    def body(*refs):
        _gather_copies(refs[:n], refs[n:2 * n], *refs[2 * n:])

    return pl.pallas_call(
        body, name=name,
        out_shape=[SDS((N_DEV, *x.shape), x.dtype) for x in xs],
        in_specs=[pl.BlockSpec(memory_space=pl.ANY)] * n,
        out_specs=[pl.BlockSpec(memory_space=pl.ANY)] * n,
        scratch_shapes=_comm_scratch(n),
    )(*xs)


def _exchange_peers():
    mx, my, mc = lax.axis_index("x"), lax.axis_index("y"), lax.axis_index("c")
    return [(1 - mx if rel & 4 else mx, 1 - my if rel & 2 else my, 1 - mc if rel & 1 else mc) for rel in range(1, N_DEV)]


def _exchange_copies(x_refs, out_refs, send_sems, recv_sems, local_sems):
    mx, my, mc = lax.axis_index("x"), lax.axis_index("y"), lax.axis_index("c")
    me = 4 * mx + 2 * my + mc
    copies = []
    for a, (x_ref, out_ref) in enumerate(zip(x_refs, out_refs)):
        mine = pltpu.make_async_copy(x_ref.at[me], out_ref.at[me], local_sems.at[a])
        mine.start()
        copies.append(mine)
        for k, (px, py, pc) in enumerate(_exchange_peers()):
            cp = pltpu.make_async_remote_copy(
                src_ref=x_ref.at[4 * px + 2 * py + pc], dst_ref=out_ref.at[me],
                send_sem=send_sems.at[7 * a + k], recv_sem=recv_sems.at[7 * a + k],
                device_id=(px, py, pc), device_id_type=MESH)
            cp.start()
            copies.append(cp)
    for cp in copies:
        cp.wait()


def all_to_all8(xs, name):
    n = len(xs)

    def body(*refs):
        _exchange_copies(refs[:n], refs[n:2 * n], *refs[2 * n:])

    return pl.pallas_call(
        body, name=name,
        out_shape=[SDS(x.shape, x.dtype) for x in xs],
        in_specs=[pl.BlockSpec(memory_space=pl.ANY)] * n,
        out_specs=[pl.BlockSpec(memory_space=pl.ANY)] * n,
        scratch_shapes=_comm_scratch(n),
    )(*xs)


def _sequencer_kernel(name, collective_id, n_arrays):
    return pl.kernel(
        mesh=plsc.ScalarSubcoreMesh(axis_name="seq", num_cores=1), name=name,
        scratch_types=tuple(_comm_scratch(n_arrays)),
        compiler_params=pltpu.CompilerParams(collective_id=collective_id))


def _handshake(peers):
    barrier = pltpu.get_barrier_semaphore()
    for peer in peers:
        pl.semaphore_signal(barrier, inc=1, device_id=peer, device_id_type=MESH)
    pl.semaphore_wait(barrier, len(peers))


def _hbm_refs(xs, out_shapes):
    x_refs = [jax.new_ref(x, memory_space=pltpu.MemorySpace.HBM) for x in xs]
    out_refs = [jax.empty_ref(SDS(shp, x.dtype), memory_space=pltpu.MemorySpace.HBM) for x, shp in zip(xs, out_shapes)]
    return x_refs, out_refs


def sc_all_gather8(xs, name, collective_id):
    x_refs, out_refs = _hbm_refs(xs, [(N_DEV, *x.shape) for x in xs])

    @_sequencer_kernel(name, collective_id, len(xs))
    def launch(send_sems, recv_sems, local_sems):
        _handshake(_gather_peers())
        _gather_copies(x_refs, out_refs, send_sems, recv_sems, local_sems)

    launch()
    return [ref[...] for ref in out_refs]


def sc_all_to_all8(xs, name, collective_id):
    x_refs, out_refs = _hbm_refs(xs, [x.shape for x in xs])

    @_sequencer_kernel(name, collective_id, len(xs))
    def launch(send_sems, recv_sems, local_sems):
        _handshake(_exchange_peers())
        _exchange_copies(x_refs, out_refs, send_sems, recv_sems, local_sems)

    launch()
    return [ref[...] for ref in out_refs]


def norm_mod(x, w, sc, sh, name):
    b, s, d = x.shape
    tm = min(512, s)

    def body(x_ref, w_ref, sc_ref, sh_ref, h_ref):
        xv = x_ref[...]
        n = xv * _rms(xv)
        h_ref[...] = ((n * w_ref[...]) * (1.0 + sc_ref[...]) + sh_ref[...]).astype(BF16)

    return pl.pallas_call(
        body, name=name, grid=(b, s // tm),
        in_specs=[_row(tm, d), _full((1, d)), _bvec(d), _bvec(d)],
        out_specs=_row(tm, d), out_shape=SDS((b, s, d), BF16), compiler_params=_cparams(2))(x, w, sc, sh)


def ffn_up(h, wg_t, wu_t, name):
    b, s, d = h.shape
    f = wg_t.shape[0]
    tm, tn = min(1024, s), f // 2

    def body(h_ref, wg_ref, wu_ref, s_ref, t_ref, a_ref):
        hv = h_ref[...]
        g = lax.dot_general(hv, wg_ref[...], NT_DIMS, preferred_element_type=F32)
        u = lax.dot_general(hv, wu_ref[...], NT_DIMS, preferred_element_type=F32)
        sg = _sigmoid(g)
        silu = g * sg
        s_ref[...] = silu.astype(s_ref.dtype)
        t_ref[...] = (u * (sg + silu * (1.0 - sg))).astype(t_ref.dtype)
        a_ref[...] = (silu * u).astype(BF16)

    hs = pl.BlockSpec((None, tm, d), lambda j, bb, i: (bb, i, 0))
    ws = pl.BlockSpec((tn, d), lambda j, bb, i: (j, 0))
    os_ = pl.BlockSpec((None, tm, tn), lambda j, bb, i: (bb, i, j))
    return pl.pallas_call(
        body, name=name, grid=(f // tn, b, s // tm),
        in_specs=[hs, ws, ws], out_specs=[os_, os_, os_],
        out_shape=[SDS((b, s, f), SAVED_ACT), SDS((b, s, f), SAVED_ACT), SDS((b, s, f), BF16)],
        compiler_params=_cparams(3))(h, wg_t, wu_t)


def _norm_mod_tile(xv, w_ref, sc_ref, sh_ref):
    return ((xv * _rms(xv) * w_ref[...]) * (1.0 + sc_ref[...]) + sh_ref[...]).astype(BF16)


def ffn_down(a, wd, x, gate, scale, name, above=None):
    b, s, f = a.shape
    d = wd.shape[1]
    tm = min(1024, s)

    def body(a_ref, wd_ref, x_ref, g_ref, *rest):
        xn_ref, o_ref = rest[-3:-1] if above else rest
        o = jnp.dot(a_ref[...], wd_ref[...], preferred_element_type=F32)
        xn = x_ref[...] + (scale * g_ref[...]) * o
        xn_ref[...] = xn
        o_ref[...] = o.astype(BF16)
        if above:
            rest[-1][...] = _norm_mod_tile(xn, *rest[0:3])

    extra = above is not None
    return pl.pallas_call(
        body, name=name, grid=(b, s // tm),
        in_specs=[_row(tm, f), _full((f, d)), _row(tm, d), _bvec(d)] + ([_full((1, d)), _bvec(d), _bvec(d)] if extra else []),
        out_specs=[_row(tm, d), _row(tm, d)] + ([_row(tm, d)] if extra else []),
        out_shape=[SDS((b, s, d), F32), SDS((b, s, d), BF16)] + ([SDS((b, s, d), BF16)] if extra else []),
        compiler_params=_cparams(2))(a, wd, x, gate, *(above or ()))


def ffn_down_final(a, wd, x, gate, scale, w_final, tgt, name):
    b, s, f = a.shape
    d = wd.shape[1]
    tm = min(1024, s)

    def body(a_ref, wd_ref, x_ref, g_ref, w_ref, t_ref, loss_ref, dx_ref, dw_ref, do_ref, dg_ref):
        @pl.when(_first_step())
        def _():
            loss_ref[...] = jnp.zeros_like(loss_ref)
            dw_ref[...] = jnp.zeros_like(dw_ref)

        @pl.when(pl.program_id(1) == 0)
        def _():
            dg_ref[...] = jnp.zeros_like(dg_ref)
        o = jnp.dot(a_ref[...], wd_ref[...], preferred_element_type=F32)
        sg = scale * g_ref[...]
        xv = x_ref[...] + sg * o
        r = _rms(xv)
        n = xv * r
        wv = w_ref[...]
        e = n * wv - t_ref[...]
        loss_ref[...] += jnp.sum(e * e) * (0.5 / d)
        dy = e * (1.0 / d)
        dw_ref[...] += jnp.sum(dy * n, axis=0, keepdims=True)
        dx = _rms_bwd(dy * wv, n, r)
        dx_ref[...] = dx
        do_ref[...] = (sg * dx).astype(BF16)
        dg_ref[...] += jnp.sum(scale * dx * o, axis=0, keepdims=True)

    return pl.pallas_call(
        body, name=name, grid=(b, s // tm),
        in_specs=[_row(tm, f), _full((f, d)), _row(tm, d), _bvec(d), _full((1, d)), _row(tm, d)],
        out_specs=[_full((1, LANES)), _row(tm, d), _full((1, d)), _row(tm, d), _bvec(d)],
        out_shape=[SDS((1, LANES), F32), SDS((b, s, d), F32), SDS((1, d), F32), SDS((b, s, d), BF16), SDS((b, 1, d), F32)],
        compiler_params=_cparams(2))(a, wd, x, gate, w_final, tgt)


def ffn_dact(do, wd, silu_g, u_dsilu, name):
    b, s, d = do.shape
    f = wd.shape[0]
    tm, tn = min(1024, s), f // 2

    def body(do_ref, wd_ref, s_ref, t_ref, dg_ref, du_ref):
        da = lax.dot_general(do_ref[...], wd_ref[...], NT_DIMS, preferred_element_type=F32)
        dg_ref[...] = (da * t_ref[...].astype(F32)).astype(BF16)
        du_ref[...] = (da * s_ref[...].astype(F32)).astype(BF16)

    dos = pl.BlockSpec((None, tm, d), lambda j, bb, i: (bb, i, 0))
    ws = pl.BlockSpec((tn, d), lambda j, bb, i: (j, 0))
    es = pl.BlockSpec((None, tm, tn), lambda j, bb, i: (bb, i, j))
    return pl.pallas_call(
        body, name=name, grid=(f // tn, b, s // tm),
        in_specs=[dos, ws, es, es], out_specs=[es, es],
        out_shape=[SDS((b, s, f), BF16), SDS((b, s, f), BF16)], compiler_params=_cparams(3))(do, wd, silu_g, u_dsilu)


def mm_tn(a, bm, tma, tnb, name):
    b, s, ka = a.shape
    nb = bm.shape[2]
    tk = min(2048, s)
    nk = s // tk

    def body(a_ref, b_ref, o_ref, acc):
        first = (pl.program_id(2) == 0) & (pl.program_id(3) == 0)
        last = (pl.program_id(2) == b - 1) & (pl.program_id(3) == nk - 1)
        part = lax.dot_general(a_ref[...], b_ref[...], TN_DIMS, preferred_element_type=F32)

        @pl.when(first)
        def _():
            acc[...] = part

        @pl.when(jnp.logical_not(first))
        def _():
            acc[...] += part

        @pl.when(last)
        def _():
            o_ref[...] = acc[...].astype(BF16)

    return pl.pallas_call(
        body, name=name, grid=(ka // tma, nb // tnb, b, nk),
        in_specs=[pl.BlockSpec((None, tk, tma), lambda i, j, bb, k: (bb, k, i)),
                  pl.BlockSpec((None, tk, tnb), lambda i, j, bb, k: (bb, k, j))],
        out_specs=pl.BlockSpec((tma, tnb), lambda i, j, bb, k: (i, j)),
        out_shape=SDS((ka, nb), BF16), scratch_shapes=[pltpu.VMEM((tma, tnb), F32)],
        compiler_params=_cparams(4))(a, bm)


def mm_tn_blocks(a_blocks, bm, name):
    b, s, nb = bm.shape
    widths = [a.shape[2] for a in a_blocks]
    starts = [sum(widths[:k]) for k in range(len(widths))]
    tk = min(1024, s)
    nk = s // tk
    n = len(a_blocks)

    def body(*refs):
        a_refs, b_ref, o_ref, acc = refs[:n], refs[n], refs[n + 1], refs[n + 2]
        first = (pl.program_id(0) == 0) & (pl.program_id(1) == 0)
        last = (pl.program_id(0) == b - 1) & (pl.program_id(1) == nk - 1)

        @pl.when(first)
        def _():
            acc[...] = jnp.zeros_like(acc)
        bv = b_ref[...]
        for a_ref, st, wd in zip(a_refs, starts, widths):
            acc[st:st + wd, :] += lax.dot_general(a_ref[...], bv, TN_DIMS, preferred_element_type=F32)

        @pl.when(last)
        def _():
            o_ref[...] = acc[...].astype(BF16)

    return pl.pallas_call(
        body, name=name, grid=(b, nk),
        in_specs=[_row(tk, wd) for wd in widths] + [_row(tk, nb)],
        out_specs=_full((sum(widths), nb)), out_shape=SDS((sum(widths), nb), BF16),
        scratch_shapes=[pltpu.VMEM((sum(widths), nb), F32)], compiler_params=_cparams(2))(*a_blocks, bm)


def _gate_bwd_specs(tm, d, b, s):
    return ([_row(tm, d), _bvec(d)], [_row(tm, d), _bvec(d)], [SDS((b, s, d), BF16), SDS((b, 1, d), F32)])


def _gate_bwd_tile(dx, scale, o_ref, g_ref, do_ref, dg_ref):
    do_ref[...] = ((scale * g_ref[...]) * dx).astype(BF16)
    dg_ref[...] += jnp.sum(scale * dx * o_ref[...].astype(F32), axis=0, keepdims=True)


def dh_norm_bwd(dys, wts, x, dxn, w, sc, name, below=None):
    b, s, d = x.shape
    tm = min(512, s)
    n_in, n_w = len(dys), len(wts)
    extra_in, extra_out, extra_shape = _gate_bwd_specs(tm, d, b, s) if below else ([], [], [])
    starts = [sum(dy.shape[2] for dy in dys[:k]) for k in range(n_in)]

    def body(*refs):
        dy_refs, w_refs = refs[:n_in], refs[n_in:n_in + n_w]
        x_ref, dxn_ref, nw_ref, sc_ref = refs[n_in + n_w:n_in + n_w + 4]
        rest = refs[n_in + n_w + 4:]
        if below:
            o_ref, g_ref, dx_ref, dsc_ref, dsh_ref, dw_ref, do_ref, dg_ref = rest
        else:
            dx_ref, dsc_ref, dsh_ref, dw_ref = rest

        @pl.when(pl.program_id(1) == 0)
        def _():
            dsc_ref[...] = jnp.zeros_like(dsc_ref)
            dsh_ref[...] = jnp.zeros_like(dsh_ref)
            if below:
                dg_ref[...] = jnp.zeros_like(dg_ref)

        @pl.when(_first_step())
        def _():
            dw_ref[...] = jnp.zeros_like(dw_ref)

        def weight(k):
            return w_refs[k][...] if n_w == n_in else w_refs[0][starts[k]:starts[k] + dys[k].shape[2], :]

        dh = jnp.dot(dy_refs[0][...], weight(0), preferred_element_type=F32)
        for k in range(1, n_in):
            dh += jnp.dot(dy_refs[k][...], weight(k), preferred_element_type=F32)
        xv = x_ref[...]
        r = _rms(xv)
        n = xv * r
        nw = nw_ref[...]
        dsc_ref[...] += jnp.sum(dh * (n * nw), axis=0, keepdims=True)
        dsh_ref[...] += jnp.sum(dh, axis=0, keepdims=True)
        dhn = dh * (1.0 + sc_ref[...])
        dw_ref[...] += jnp.sum(dhn * n, axis=0, keepdims=True)
        dx = dxn_ref[...] + _rms_bwd(dhn * nw, n, r)
        dx_ref[...] = dx
        if below:
            _gate_bwd_tile(dx, below[2], o_ref, g_ref, do_ref, dg_ref)

    resident = lambda shape: pl.BlockSpec(shape, lambda *_: (0,) * len(shape), pipeline_mode=pl.Buffered(1))
    in_specs = [_row(tm, dy.shape[2]) for dy in dys] + [resident(wt.shape) for wt in wts]
    in_specs += [_row(tm, d), _row(tm, d), _full((1, d)), _bvec(d)] + extra_in
    return pl.pallas_call(
        body, name=name, grid=(b, s // tm), in_specs=in_specs,
        out_specs=[_row(tm, d), _bvec(d), _bvec(d), _full((1, d))] + extra_out,
        out_shape=[SDS((b, s, d), F32), SDS((b, 1, d), F32), SDS((b, 1, d), F32), SDS((1, d), F32)] + extra_shape,
        compiler_params=_cparams(2))(*dys, *wts, x, dxn, w, sc, *(below[:2] if below else ()))


def in_proj(h, win_t, name):
    b, s, d = h.shape
    tm = min(512, s)
    widths = (D_SSD, D_SSD + 2 * SSD_GROUPS * SSD_STATE, Q_LORA, KV_LORA, LANES)

    def body(h_ref, w_ref, *outs):
        p = lax.dot_general(h_ref[...], w_ref[...], NT_DIMS, preferred_element_type=F32)
        off = 0
        for o_ref, wd in zip(outs, widths):
            o_ref[...] = p[:, off:off + wd]
            off += wd

    return pl.pallas_call(
        body, name=name, grid=(b, s // tm),
        in_specs=[_row(tm, d), _full(win_t.shape)],
        out_specs=[_row(tm, wd) for wd in widths],
        out_shape=[SDS((b, s, wd), F32) for wd in widths], compiler_params=_cparams(2))(h, win_t)


def _halo_prev(ts, d):
    return pl.BlockSpec((None, 8, d), lambda b, i: (b, jnp.maximum(i * (ts // 8) - 1, 0), 0))


CONV_ROWS = 32


def _conv_head(head, u_ref, up_ref, tile):
    head[0:8, :] = jnp.where(tile > 0, up_ref[...], 0.0)
    head[8:8 + CONV_ROWS, :] = u_ref[0:CONV_ROWS, :]


def _conv_windows(u_ref, head, r0):
    if r0 == 0:
        return [head[5 + k:5 + k + CONV_ROWS, :] for k in range(4)]
    return [u_ref[r0 - 3 + k:r0 - 3 + k + CONV_ROWS, :] for k in range(4)]


def _fold8(t):
    acc = t[0:8, :]
    for r in range(8, CONV_ROWS, 8):
        acc += t[r:r + 8, :]
    return acc


def conv_fwd(u, cw, cb, name):
    b, s, dc = u.shape
    ts = min(512, s)
    widths = (D_SSD, SSD_GROUPS * SSD_STATE, SSD_GROUPS * SSD_STATE)

    def body(u_ref, up_ref, w_ref, b_ref, xs_ref, bm_ref, cm_ref, head):
        _conv_head(head, u_ref, up_ref, pl.program_id(1))
        ws = [w_ref[k:k + 1, :] for k in range(4)]
        bias = b_ref[...]
        for r0 in range(0, ts, CONV_ROWS):
            taps = _conv_windows(u_ref, head, r0)
            v = bias + taps[0] * ws[0] + taps[1] * ws[1] + taps[2] * ws[2] + taps[3] * ws[3]
            y = v * _sigmoid(v)
            rs = slice(r0, r0 + CONV_ROWS)
            xs_ref[rs, :] = y[:, 0:D_SSD]
            bm_ref[rs, :] = y[:, D_SSD:D_SSD + 256]
            cm_ref[rs, :] = y[:, D_SSD + 256:D_SSD + 512]

    return pl.pallas_call(
        body, name=name, grid=(b, s // ts),
        in_specs=[_row(ts, dc), _halo_prev(ts, dc), _full((4, dc)), _full((1, dc))],
        out_specs=[_row(ts, wd) for wd in widths],
        out_shape=[SDS((b, s, wd), F32) for wd in widths],
        scratch_shapes=[pltpu.VMEM((8 + CONV_ROWS, dc), F32)], compiler_params=_cparams(2))(u, u, cw, cb)


def conv_bwd(dxs, dbm, dcm, u, cw, cb, name):
    b, s, dc = u.shape
    ts = min(512, s)
    nt = s // ts

    def body(dxs_ref, dbm_ref, dcm_ref, u_ref, up_ref, w_ref, b_ref, du_ref, dwb_ref, head, dvs):
        @pl.when(_first_step())
        def _():
            dwb_ref[...] = jnp.zeros_like(dwb_ref)

        @pl.when(pl.program_id(1) == 0)
        def _():
            dvs[ts:ts + 8, :] = jnp.zeros((8, dc), F32)
        _conv_head(head, u_ref, up_ref, nt - 1 - pl.program_id(1))
        ws = [w_ref[k:k + 1, :] for k in range(4)]
        bias = b_ref[...]
        for r0 in range(0, ts, CONV_ROWS):
            taps = _conv_windows(u_ref, head, r0)
            v = bias + taps[0] * ws[0] + taps[1] * ws[1] + taps[2] * ws[2] + taps[3] * ws[3]
            sg = _sigmoid(v)
            rs = slice(r0, r0 + CONV_ROWS)
            dy = jnp.concatenate([dxs_ref[rs, :], dbm_ref[rs, :], dcm_ref[rs, :]], axis=1)
            dv = dy * (sg * (1.0 + v * (1.0 - sg)))
            dvs[rs, :] = dv
            for k in range(4):
                dwb_ref[8 * k:8 * k + 8, :] += _fold8(dv * taps[k])
            dwb_ref[32:40, :] += _fold8(dv)
        for r0 in range(0, ts, CONV_ROWS):
            win = [dvs[r0 + 3 - k:r0 + 3 - k + CONV_ROWS, :] for k in range(4)]
            acc = win[0] * ws[0] + win[1] * ws[1] + win[2] * ws[2] + win[3] * ws[3]
            du_ref[r0:r0 + CONV_ROWS, :] = acc.astype(BF16)
        dvs[ts:ts + 8, :] = dvs[0:8, :]

    rows = lambda wd: pl.BlockSpec((None, ts, wd), lambda bb, i: (bb, nt - 1 - i, 0))
    prev = pl.BlockSpec((None, 8, dc), lambda bb, i: (bb, jnp.maximum((nt - 1 - i) * (ts // 8) - 1, 0), 0))
    return pl.pallas_call(
        body, name=name, grid=(b, nt),
        in_specs=[rows(D_SSD), rows(256), rows(256), rows(dc), prev, _full((4, dc)), _full((1, dc))],
        out_specs=[rows(dc), _full((40, dc))],
        out_shape=[SDS((b, s, dc), BF16), SDS((40, dc), F32)],
        scratch_shapes=[pltpu.VMEM((8 + CONV_ROWS, dc), F32), pltpu.VMEM((ts + 8, dc), F32)],
        compiler_params=_cparams(2))(dxs, dbm, dcm, u, u, cw, cb)


def conv_grads_fold(x, name):
    c = x.shape[1]

    def body(x_ref, o_ref):
        o_ref[...] = jnp.zeros_like(o_ref)
        for k in range(5):
            o_ref[k:k + 1, :] = jnp.sum(x_ref[8 * k:8 * k + 8, :], axis=0, keepdims=True)

    return pl.pallas_call(body, name=name, out_shape=SDS((8, c), F32))(x)


def _ssd_common(misc_ref, dtb_ref, alog_ref, e_ref):
    ln = CHUNK
    lane = lax.broadcasted_iota(I32, (ln, LANES), 1)
    lane1 = lax.broadcasted_iota(I32, (1, LANES), 1)
    pre = misc_ref[...] + dtb_ref[...]
    dt_s = jnp.where(lane < SSD_HEADS, _softplus(pre), 0.0)
    a_neg = jnp.where(lane1 < SSD_HEADS, -jnp.exp(alog_ref[...]), 0.0)
    ri = lax.broadcasted_iota(I32, (ln, ln), 0)
    ci = lax.broadcasted_iota(I32, (ln, ln), 1)
    tril = ci <= ri
    acum = jnp.dot(tril.astype(F32), dt_s * a_neg, preferred_element_type=F32, precision=HI)
    both_e = _dot_01(jnp.concatenate([dt_s, acum], axis=0), e_ref[...], 3)
    dt_e, acum_e = both_e[0:ln], both_e[ln:2 * ln]
    return dict(pre=pre, dt_s=dt_s, a_neg=a_neg, tril=tril, ri=ri, ci=ci, acum=acum, acum_t=acum.T,
                dt_e=dt_e, eac_e=jnp.exp(acum_e), del_e=jnp.exp(acum_e[ln - 1:ln, :] - acum_e))


def _dot_01(x, m01, terms):
    acc, rest = None, x
    for k in range(terms):
        part = rest.astype(BF16)
        if k + 1 < terms:
            rest = rest - part.astype(F32)
        d = jnp.dot(part, m01, preferred_element_type=F32)
        acc = d if acc is None else acc + d
    return acc


def _decay(cm, h):
    seg = cm["acum"][:, h:h + 1] - cm["acum_t"][h:h + 1, :]
    return jnp.exp(jnp.where(cm["tril"], seg, -jnp.inf))


def ssd_fwd(xs, bm, cm_, misc, z, dtb, alog, dskip_e, norm_w, e_mat, name):
    b, s, _ = xs.shape
    ln, nc = CHUNK, s // CHUNK
    gw = D_SSD // SSD_GROUPS
    hpg = SSD_HEADS // SSD_GROUPS

    def body(xs_ref, b_ref, c_ref, misc_ref, z_ref, dtb_ref, alog_ref, dsk_ref, nw_ref, e_ref,
             ys_ref, y_ref, p_ref, st, yd):
        @pl.when(pl.program_id(1) == 0)
        def _():
            st[...] = jnp.zeros_like(st)
        cm = _ssd_common(misc_ref, dtb_ref, alog_ref, e_ref)
        xsv = xs_ref[...]
        xdt = xsv * cm["dt_e"]
        xdt_b = xdt.astype(BF16)
        xd_b = (xdt * cm["del_e"]).astype(BF16)
        gam_e = cm["eac_e"][ln - 1:ln, :]
        p_ref[...] = st[...]
        groups = [slice(gw * g, gw * (g + 1)) for g in range(SSD_GROUPS)]
        heads = [slice(SSD_HEAD_DIM * h, SSD_HEAD_DIM * (h + 1)) for h in range(SSD_HEADS)]
        bgs = [b_ref[:, SSD_STATE * g:SSD_STATE * (g + 1)].astype(BF16) for g in range(SSD_GROUPS)]
        cgs = [c_ref[:, SSD_STATE * g:SSD_STATE * (g + 1)].astype(BF16) for g in range(SSD_GROUPS)]
        cbs = [lax.dot_general(cg, bg, NT_DIMS, preferred_element_type=F32) for cg, bg in zip(cgs, bgs)]
        sts = [st[:, gs] for gs in groups]
        yoff = [jnp.dot(cg, st_g.astype(BF16), preferred_element_type=F32) * cm["eac_e"][:, gs]
                for cg, st_g, gs in zip(cgs, sts, groups)]
        news = [lax.dot_general(bg, xd_b[:, gs], TN_DIMS, preferred_element_type=F32) for bg, gs in zip(bgs, groups)]
        for gs, st_g, new in zip(groups, sts, news):
            st[:, gs] = st_g * gam_e[:, gs] + new
        ms = [(cbs[h // hpg] * _decay(cm, h)).astype(BF16) for h in range(SSD_HEADS)]
        for h, hs in enumerate(heads):
            yd[:, hs] = jnp.dot(ms[h], xdt_b[:, hs], preferred_element_type=F32)
        y = yd[...] + jnp.concatenate(yoff, axis=1) + dsk_ref[...] * xsv
        y_ref[...] = y
        zz = z_ref[...]
        yg = y * (zz * _sigmoid(zz))
        outs = []
        for g in range(SSD_GROUPS):
            ygg = yg[:, gw * g:gw * (g + 1)]
            outs.append(ygg * _rms(ygg) * nw_ref[:, gw * g:gw * (g + 1)])
        ys_ref[...] = jnp.concatenate(outs, axis=1).astype(BF16)

    row = lambda d: pl.BlockSpec((None, ln, d), lambda bb, c: (bb, c, 0))
    return pl.pallas_call(
        body, name=name, grid=(b, nc),
        in_specs=[row(D_SSD), row(256), row(256), row(LANES), row(D_SSD), _full((1, LANES)), _full((1, LANES)),
                  _full((1, D_SSD)), _full((1, D_SSD)), _full((LANES, D_SSD))],
        out_specs=[row(D_SSD), row(D_SSD), pl.BlockSpec((None, None, SSD_STATE, D_SSD), lambda bb, c: (bb, c, 0, 0))],
        out_shape=[SDS((b, s, D_SSD), BF16), SDS((b, s, D_SSD), F32), SDS((b, nc, SSD_STATE, D_SSD), F32)],
        scratch_shapes=[pltpu.VMEM((SSD_STATE, D_SSD), F32), pltpu.VMEM((ln, D_SSD), F32)],
        compiler_params=_cparams(2))(xs, bm, cm_, misc, z, dtb, alog, dskip_e, norm_w, e_mat)


def ssd_bwd(dys, y, z, xs, bm, cm_, misc, prev, dtb, alog, dskip_e, norm_w, e_mat, et_mat, name):
    b, s, _ = xs.shape
    ln, nc = CHUNK, s // CHUNK
    gw = D_SSD // SSD_GROUPS
    hpg = SSD_HEADS // SSD_GROUPS

    def body(dys_ref, y_ref, z_ref, xs_ref, b_ref, c_ref, misc_ref, p_ref, dtb_ref, alog_ref, dsk_ref, nw_ref,
             e_ref, et_ref, dxs_ref, db_ref, dc_ref, dz_ref, ddt_ref, dnw_ref, ddsk_ref, ddtb_ref, dalog_ref,
             dst, dxd, dac_t):
        @pl.when(_first_step())
        def _():
            for r_ in (dnw_ref, ddsk_ref, ddtb_ref, dalog_ref):
                r_[...] = jnp.zeros_like(r_)

        @pl.when(pl.program_id(1) == 0)
        def _():
            dst[...] = jnp.zeros_like(dst)

        cm = _ssd_common(misc_ref, dtb_ref, alog_ref, e_ref)
        et = et_ref[...]
        squeeze = lambda t: _dot_01(t, et, 2)
        lane = lax.broadcasted_iota(I32, (ln, LANES), 1)
        sub = lax.broadcasted_iota(I32, (LANES, ln), 0)
        xsv = xs_ref[...]
        xdt = xsv * cm["dt_e"]
        xdt_b = xdt.astype(BF16)
        xd_b = (xdt * cm["del_e"]).astype(BF16)
        eac_e = cm["eac_e"]
        gam_e = eac_e[ln - 1:ln, :]

        yv, zz, dyo = y_ref[...], z_ref[...], dys_ref[...]
        sz = _sigmoid(zz)
        silu_z = zz * sz
        yg = yv * silu_z
        dyg, dnw = [], []
        for g in range(SSD_GROUPS):
            gs = slice(gw * g, gw * (g + 1))
            ygg = yg[:, gs]
            r = _rms(ygg)
            n = ygg * r
            dnw.append(jnp.sum(dyo[:, gs] * n, axis=0, keepdims=True))
            dyg.append(_rms_bwd(dyo[:, gs] * nw_ref[:, gs], n, r))
        dyg = jnp.concatenate(dyg, axis=1)
        dnw_ref[...] += jnp.concatenate(dnw, axis=1)
        dz_ref[...] = (dyg * yv * (sz * (1.0 + zz * (1.0 - sz)))).astype(BF16)
        dy = dyg * silu_z
        ddsk_ref[...] += jnp.sum(dy * xsv, axis=0, keepdims=True)
        dy_b = dy.astype(BF16)

        dacum = jnp.zeros((ln, LANES), F32)
        dac_t[...] = jnp.zeros_like(dac_t)
        w1, dgam = [], []
        for g in range(SSD_GROUPS):
            gs = slice(gw * g, gw * (g + 1))
            ss = slice(SSD_STATE * g, SSD_STATE * (g + 1))
            bg = b_ref[:, ss].astype(BF16)
            cg = c_ref[:, ss].astype(BF16)
            cb = lax.dot_general(cg, bg, NT_DIMS, preferred_element_type=F32)
            pt = p_ref[:, gs]
            pt_b = pt.astype(BF16)
            dst_g = dst[:, gs]
            dst_b = dst_g.astype(BF16)
            edy = (dy[:, gs] * eac_e[:, gs]).astype(BF16)
            dcg = lax.dot_general(edy, pt_b, NT_DIMS, preferred_element_type=F32)
            dpt = lax.dot_general(cg, edy, TN_DIMS, preferred_element_type=F32)
            yoff = jnp.dot(cg, pt_b, preferred_element_type=F32) * eac_e[:, gs]
            dxd_g = jnp.dot(bg, dst_b, preferred_element_type=F32)
            dbg = lax.dot_general(xd_b[:, gs], dst_b, NT_DIMS, preferred_element_type=F32)
            ddel = dxd_g * xdt[:, gs] * cm["del_e"][:, gs]
            w1.append(dy[:, gs] * yoff - ddel)
            dgam.append(jnp.sum(ddel, axis=0, keepdims=True) + jnp.sum(dst_g * pt, axis=0, keepdims=True) * gam_e[:, gs])
            dxd[:, gs] = dxd_g * cm["del_e"][:, gs]
            dst[:, gs] = dst_g * gam_e[:, gs] + dpt
            dcb = jnp.zeros((ln, ln), F32)
            for j in range(hpg):
                h = hpg * g + j
                hs = slice(SSD_HEAD_DIM * h, SSD_HEAD_DIM * (h + 1))
                lam = _decay(cm, h)
                m = cb * lam
                dm = lax.dot_general(dy_b[:, hs], xdt_b[:, hs], NT_DIMS, preferred_element_type=F32)
                dxd[:, hs] += lax.dot_general(m.astype(BF16), dy_b[:, hs], TN_DIMS, preferred_element_type=F32)
                dcb += dm * lam
                wl = dm * m
                dacum += jnp.where(lane == h, jnp.sum(wl, axis=1, keepdims=True), 0.0)
                dac_t[...] -= jnp.where(sub == h, jnp.sum(wl, axis=0, keepdims=True), 0.0)
            dcb_b = dcb.astype(BF16)
            dc_ref[:, ss] = dcg + jnp.dot(dcb_b, bg, preferred_element_type=F32)
            db_ref[:, ss] = dbg + lax.dot_general(dcb_b, cg, TN_DIMS, preferred_element_type=F32)

        dxdt = dxd[...]
        dxs_ref[...] = dy * dsk_ref[...] + dxdt * cm["dt_e"]
        dacum += squeeze(jnp.concatenate(w1, axis=1)) + dac_t[...].T
        dlast = squeeze(jnp.broadcast_to(jnp.concatenate(dgam, axis=1), (8, D_SSD)))[0:1, :]
        dacum += jnp.where(lax.broadcasted_iota(I32, (ln, LANES), 0) == ln - 1, dlast, 0.0)
        triu = (cm["ci"] >= cm["ri"]).astype(F32)
        da = jnp.dot(triu, dacum, preferred_element_type=F32, precision=HI)
        ddt = da * cm["a_neg"] + squeeze(dxdt * xsv)
        dalog_ref[...] += jnp.sum(da * cm["dt_s"], axis=0, keepdims=True) * cm["a_neg"]
        ddt_raw = jnp.where(lane < SSD_HEADS, ddt * _sigmoid(cm["pre"]), 0.0)
        ddt_ref[...] = ddt_raw
        ddtb_ref[...] += jnp.sum(ddt_raw, axis=0, keepdims=True)

    row = lambda d: pl.BlockSpec((None, ln, d), lambda bb, c: (bb, nc - 1 - c, 0))
    return pl.pallas_call(
        body, name=name, grid=(b, nc),
        in_specs=[row(D_SSD), row(D_SSD), row(D_SSD), row(D_SSD), row(256), row(256), row(LANES),
                  pl.BlockSpec((None, None, SSD_STATE, D_SSD), lambda bb, c: (bb, nc - 1 - c, 0, 0)),
                  _full((1, LANES)), _full((1, LANES)), _full((1, D_SSD)), _full((1, D_SSD)),
                  _full((LANES, D_SSD)), _full((D_SSD, LANES))],
        out_specs=[row(D_SSD), row(256), row(256), row(D_SSD), row(LANES),
                   _full((1, D_SSD)), _full((1, D_SSD)), _full((1, LANES)), _full((1, LANES))],
        out_shape=[SDS((b, s, D_SSD), F32), SDS((b, s, 256), F32), SDS((b, s, 256), F32), SDS((b, s, D_SSD), BF16),
                   SDS((b, s, LANES), F32), SDS((1, D_SSD), F32), SDS((1, D_SSD), F32), SDS((1, LANES), F32),
                   SDS((1, LANES), F32)],
        scratch_shapes=[pltpu.VMEM((SSD_STATE, D_SSD), F32), pltpu.VMEM((ln, D_SSD), F32), pltpu.VMEM((LANES, ln), F32)],
        compiler_params=_cparams(2))(dys, y, z, xs, bm, cm_, misc, prev, dtb, alog, dskip_e, norm_w, e_mat, et_mat)


def _rope(xv, cc, sp, sm):
    n = xv.shape[1]
    return xv * cc + pltpu.roll(xv, 16, 1) * sp + pltpu.roll(xv, n - 16, 1) * sm


def _rope_bwd(dy, cc, sp, sm):
    n = dy.shape[1]
    return dy * cc + pltpu.roll(dy * sp, n - 16, 1) + pltpu.roll(dy * sm, 16, 1)


def _tile8(t):
    return jnp.concatenate([t] * MLA_HEADS, axis=1)


def qkv_fwd(cq, ckv, misc, cc, sp, sm, qnw, kvnw, wuq_t, wukv_t, place, name):
    b, s, _ = cq.shape
    tm = _att_tile(s)
    hd = MLA_HEADS * HEAD_PAD

    def body(cq_ref, ckv_ref, misc_ref, cc_ref, sp_ref, sm_ref, qnw_ref, kvnw_ref, wq_ref, wkv_ref, pl_ref,
             q_ref, k_ref, v_ref, vt_ref, qn_ref, kvn_ref):
        cqv, ckvv = cq_ref[...], ckv_ref[...]
        qn = (cqv * _rms(cqv) * qnw_ref[...]).astype(BF16)
        kvn = (ckvv * _rms(ckvv) * kvnw_ref[...]).astype(BF16)
        qn_ref[...] = qn
        kvn_ref[...] = kvn
        cc1, sp1, sm1 = cc_ref[...], sp_ref[...], sm_ref[...]
        q = lax.dot_general(qn, wq_ref[...], NT_DIMS, preferred_element_type=F32)
        q_ref[...] = _rope(q, _tile8(cc1), _tile8(sp1), _tile8(sm1)).astype(BF16)
        kv = lax.dot_general(kvn, wkv_ref[...], NT_DIMS, preferred_element_type=F32)
        kr = jnp.dot(misc_ref[...], pl_ref[...], preferred_element_type=F32, precision=HI)
        kr = _rope(kr, cc1, sp1, sm1)
        k_ref[...] = (kv[:, 0:hd] + _tile8(kr)).astype(BF16)
        v_ref[...] = kv[:, hd:2 * hd].astype(BF16)
        for h in range(MLA_HEADS):
            vt_ref[h] = kv[:, hd + HEAD_PAD * h:hd + HEAD_PAD * (h + 1)].T.astype(BF16)

    return pl.pallas_call(
        body, name=name, grid=(b, s // tm),
        in_specs=[_row(tm, Q_LORA), _row(tm, KV_LORA), _row(tm, LANES), _row(tm, LANES), _row(tm, LANES), _row(tm, LANES),
                  _full((1, Q_LORA)), _full((1, KV_LORA)), _full(wuq_t.shape), _full(wukv_t.shape), _full((LANES, LANES))],
        out_specs=[_row(tm, hd), _row(tm, hd), _row(tm, hd),
                   pl.BlockSpec((None, MLA_HEADS, None, HEAD_PAD, tm), lambda bb, i: (bb, 0, i, 0, 0)),
                   _row(tm, Q_LORA), _row(tm, KV_LORA)],
        out_shape=[SDS((b, s, hd), BF16)] * 3 + [SDS((b, MLA_HEADS, s // tm, HEAD_PAD, tm), BF16),
                                                 SDS((b, s, Q_LORA), BF16), SDS((b, s, KV_LORA), BF16)],
        compiler_params=_cparams(2))(cq, ckv, misc, cc, sp, sm, qnw, kvnw, wuq_t, wukv_t, place)


def qkv_bwd(dq, dk, dv, ddt, cq, ckv, cc, sp, sm, qnw, kvnw, wuq_t, wukv_t, place_t, name):
    b, s, _ = cq.shape
    tm = min(512, s)
    hd = MLA_HEADS * HEAD_PAD

    def body(dq_ref, dk_ref, dv_ref, ddt_ref, cq_ref, ckv_ref, cc_ref, sp_ref, sm_ref, qnw_ref, kvnw_ref,
             wq_ref, wkv_ref, plt_ref, dcq_ref, dckv_ref, dmisc_ref, dqp_ref, dkv_ref, dqnw_ref, dkvnw_ref):
        @pl.when(_first_step())
        def _():
            dqnw_ref[...] = jnp.zeros_like(dqnw_ref)
            dkvnw_ref[...] = jnp.zeros_like(dkvnw_ref)
        cc1, sp1, sm1 = cc_ref[...], sp_ref[...], sm_ref[...]
        dqp = _rope_bwd(dq_ref[...].astype(F32), _tile8(cc1), _tile8(sp1), _tile8(sm1)).astype(BF16)
        dqp_ref[...] = dqp
        dkv_b = jnp.concatenate([dk_ref[...], dv_ref[...]], axis=1)
        dkf = dk_ref[...].astype(F32)
        dkv_ref[...] = dkv_b
        dkr = dkf[:, 0:HEAD_PAD]
        for h in range(1, MLA_HEADS):
            dkr += dkf[:, HEAD_PAD * h:HEAD_PAD * (h + 1)]
        dkr = _rope_bwd(dkr, cc1, sp1, sm1)
        dmisc_ref[...] = (jnp.dot(dkr, plt_ref[...], preferred_element_type=F32, precision=HI) + ddt_ref[...]).astype(BF16)

        def norm_bwd(dn_w, xv, w_ref, dw_ref, dx_ref):
            r = _rms(xv)
            n = xv * r
            dw_ref[...] += jnp.sum(dn_w * n, axis=0, keepdims=True)
            dx_ref[...] = _rms_bwd(dn_w * w_ref[...], n, r).astype(BF16)

        norm_bwd(jnp.dot(dqp, wq_ref[...], preferred_element_type=F32), cq_ref[...], qnw_ref, dqnw_ref, dcq_ref)
        norm_bwd(jnp.dot(dkv_b, wkv_ref[...], preferred_element_type=F32), ckv_ref[...], kvnw_ref, dkvnw_ref, dckv_ref)

    return pl.pallas_call(
        body, name=name, grid=(b, s // tm),
        in_specs=[_row(tm, hd), _row(tm, hd), _row(tm, hd), _row(tm, LANES), _row(tm, Q_LORA), _row(tm, KV_LORA),
                  _row(tm, LANES), _row(tm, LANES), _row(tm, LANES), _full((1, Q_LORA)), _full((1, KV_LORA)),
                  _full(wuq_t.shape), _full(wukv_t.shape), _full((LANES, LANES))],
        out_specs=[_row(tm, Q_LORA), _row(tm, KV_LORA), _row(tm, LANES), _row(tm, hd), _row(tm, 2 * hd),
                   _full((1, Q_LORA)), _full((1, KV_LORA))],
        out_shape=[SDS((b, s, Q_LORA), BF16), SDS((b, s, KV_LORA), BF16), SDS((b, s, LANES), BF16),
                   SDS((b, s, hd), BF16), SDS((b, s, 2 * hd), BF16), SDS((1, Q_LORA), F32), SDS((1, KV_LORA), F32)],
        compiler_params=_cparams(2))(dq, dk, dv, ddt, cq, ckv, cc, sp, sm, qnw, kvnw, wuq_t, wukv_t, place_t)


ATT_SCALE = 1.0 / math.sqrt(QK_DIM)
LOG2E = math.log2(math.e)
ATT_SCALE_LOG2E = ATT_SCALE * LOG2E


ATT_HEADS_PER_STEP = 4
ATT_HEADS_PER_STEP_BWD = 2


def _att_tile(s):
    return min(512, s)


def flash_fwd(q, k, vt, name):
    b, s, hd = q.shape
    t = _att_tile(s)
    nb = s // t
    th = t // 2

    hps = ATT_HEADS_PER_STEP
    hw = hps * HEAD_PAD

    def body(q_ref, k_ref, vt_ref, o_ref, lse_ref, m_s, l_s, acc):
        i = pl.program_id(2)
        m_s[...] = jnp.full_like(m_s, -jnp.inf)
        l_s[...] = jnp.zeros_like(l_s)
        acc[...] = jnp.zeros_like(acc)

        def update(j, diagonal):
            chains = [(hh, half) for hh in range(hps) for half in range(2)]
            lanes = lambda hh: slice(HEAD_PAD * hh, HEAD_PAD * (hh + 1))
            cols = lambda half: slice(th * half, th * (half + 1))
            sts = {}
            nkeys = lambda half: th if diagonal and half == 0 else t
            for hh, half in chains:
                kr = pl.ds(pl.multiple_of(j * t, t), nkeys(half))
                st = lax.dot_general(k_ref[kr, lanes(hh)], q_ref[cols(half), lanes(hh)], NT_DIMS,
                                     preferred_element_type=F32)
                if diagonal:
                    row = lax.broadcasted_iota(I32, (nkeys(half), th), 0)
                    col = lax.broadcasted_iota(I32, (nkeys(half), th), 1) + th * half
                    st = jnp.where(row <= col, st, -jnp.inf)
                sts[hh, half] = st
            pts, alphas = {}, {}
            for hh, half in chains:
                st, cs = sts[hh, half], cols(half)
                m_prev = m_s[hh, :, cs]
                m_new = jnp.maximum(m_prev, jnp.max(st, axis=0, keepdims=True))
                alpha = jnp.exp2((m_prev - m_new) * ATT_SCALE_LOG2E)
                pt = jnp.exp2((st - m_new) * ATT_SCALE_LOG2E)
                l_s[hh, :, cs] = alpha * l_s[hh, :, cs] + jnp.sum(pt, axis=0, keepdims=True)
                m_s[hh, :, cs] = m_new
                pts[hh, half], alphas[hh, half] = pt.astype(BF16), alpha
            for hh, half in chains:
                cs = cols(half)
                acc[hh, :, cs] = alphas[hh, half] * acc[hh, :, cs] + jnp.dot(
                    vt_ref[hh, j, :, 0:nkeys(half)], pts[hh, half], preferred_element_type=F32)

        def step(j, carry):
            update(j, False)
            return carry

        lax.fori_loop(0, i, step, 0)
        update(i, True)
        for hh in range(hps):
            o_ref[:, HEAD_PAD * hh:HEAD_PAD * (hh + 1)] = (acc[hh] / l_s[hh]).T
            lse_ref[hh] = m_s[hh] * ATT_SCALE + jnp.log(l_s[hh])

    qs = pl.BlockSpec((None, t, hw), lambda bb, h, i: (bb, i, h))
    ks = pl.BlockSpec((None, s, hw), lambda bb, h, i: (bb, 0, h))
    vs = pl.BlockSpec((None, hps, nb, HEAD_PAD, t), lambda bb, h, i: (bb, h, 0, 0, 0))
    ls = pl.BlockSpec((None, hps, None, 1, t), lambda bb, h, i: (bb, h, i, 0, 0))
    return pl.pallas_call(
        body, name=name, grid=(b, MLA_HEADS // hps, nb),
        in_specs=[qs, ks, vs], out_specs=[qs, ls],
        out_shape=[SDS((b, s, hd), F32), SDS((b, MLA_HEADS, nb, 1, t), F32)],
        scratch_shapes=[pltpu.VMEM((hps, 1, t), F32), pltpu.VMEM((hps, 1, t), F32), pltpu.VMEM((hps, HEAD_PAD, t), F32)],
        compiler_params=_cparams(3))(q, k, vt)


def flash_bwd(q, k, v, do, lse, dlt, name):
    b, s, hd = q.shape
    t = _att_tile(s)
    nb = s // t
    th = t // 2
    lse_r = lse
    dlt_r = dlt.reshape(b, MLA_HEADS, nb, 1, t)

    hps = ATT_HEADS_PER_STEP_BWD
    hw = hps * HEAD_PAD

    def body(q_ref, k_ref, v_ref, do_ref, lse_ref, dlt_ref, dq_ref, dk_ref, dv_ref, dq_s, dk_s, dv_s):
        dq_s[...] = jnp.zeros_like(dq_s)
        dk_s[...] = jnp.zeros_like(dk_s)
        dv_s[...] = jnp.zeros_like(dv_s)

        def tile(j, i, diagonal):
            chains = [(hh, half) for hh in range(hps) for half in range(2)]
            lanes = lambda hh: slice(HEAD_PAD * hh, HEAD_PAD * (hh + 1))
            keys = lambda half: pl.ds(pl.multiple_of(j * t + th * half, th), th)
            q0 = lambda half: th if diagonal and half == 1 else 0
            qsel = lambda half: pl.ds(pl.multiple_of(i * t + q0(half), th), t - q0(half))
            sts, dpts = {}, {}
            for hh, half in chains:
                ls_, ks, qs, nq = lanes(hh), keys(half), qsel(half), t - q0(half)
                st = lax.dot_general(k_ref[ks, ls_], q_ref[qs, ls_], NT_DIMS, preferred_element_type=F32)
                if diagonal:
                    row = lax.broadcasted_iota(I32, (th, nq), 0) + th * half
                    col = lax.broadcasted_iota(I32, (th, nq), 1) + q0(half)
                    st = jnp.where(row <= col, st, -jnp.inf)
                sts[hh, half] = st
                dpts[hh, half] = lax.dot_general(v_ref[ks, ls_], do_ref[qs, ls_], NT_DIMS, preferred_element_type=F32)
            pts, dsts = {}, {}
            for hh, half in chains:
                qcols = slice(q0(half), t)
                pt = jnp.exp2(sts[hh, half] * ATT_SCALE_LOG2E - lse_ref[hh, i][:, qcols] * LOG2E)
                pts[hh, half] = pt.astype(BF16)
                dsts[hh, half] = (pt * (dpts[hh, half] - dlt_ref[hh, i][:, qcols])).astype(BF16)
            for hh, half in chains:
                ls_, ks, qs = lanes(hh), keys(half), qsel(half)
                dv_s[ks, ls_] += jnp.dot(pts[hh, half], do_ref[qs, ls_], preferred_element_type=F32)
                dk_s[ks, ls_] += jnp.dot(dsts[hh, half], q_ref[qs, ls_], preferred_element_type=F32)
                dq_s[qs, ls_] += lax.dot_general(dsts[hh, half], k_ref[ks, ls_], TN_DIMS, preferred_element_type=F32)

        def key_tile(j, carry):
            tile(j, j, True)

            def query_tile(i, c2):
                tile(j, i, False)
                return c2

            lax.fori_loop(j + 1, nb, query_tile, 0)
            return carry

        lax.fori_loop(0, nb, key_tile, 0)
        dq_ref[...] = (dq_s[...] * ATT_SCALE).astype(BF16)
        dk_ref[...] = (dk_s[...] * ATT_SCALE).astype(BF16)
        dv_ref[...] = dv_s[...].astype(BF16)

    hs = pl.BlockSpec((None, s, hw), lambda bb, h: (bb, 0, h))
    ls = pl.BlockSpec((None, hps, nb, 1, t), lambda bb, h: (bb, h, 0, 0, 0))
    return pl.pallas_call(
        body, name=name, grid=(b, MLA_HEADS // hps),
        in_specs=[hs, hs, hs, hs, ls, ls], out_specs=[hs, hs, hs],
        out_shape=[SDS((b, s, hd), BF16)] * 3, scratch_shapes=[pltpu.VMEM((s, hw), F32)] * 3,
        compiler_params=_cparams(2))(q, k, v, do, lse_r, dlt_r)


def out_proj(ys, attn, mnw, wo, x, gate, above, name):
    b, s, d = x.shape
    tm = min(512, s)

    def body(ys_ref, at_ref, mnw_ref, wo_ref, x_ref, g_ref, nw_ref, sc_ref, sh_ref, xn_ref, o_ref, ym_ref, h_ref):
        av = at_ref[...]
        ym = (av * _rms(av) * mnw_ref[...]).astype(BF16)
        ym_ref[...] = ym
        o = jnp.dot(ys_ref[...], wo_ref[0:D_SSD, :], preferred_element_type=F32)
        o += jnp.dot(ym, wo_ref[D_SSD:2 * D_SSD, :], preferred_element_type=F32)
        xn = x_ref[...] + g_ref[...] * o
        xn_ref[...] = xn
        o_ref[...] = o.astype(BF16)
        h_ref[...] = _norm_mod_tile(xn, nw_ref, sc_ref, sh_ref)

    return pl.pallas_call(
        body, name=name, grid=(b, s // tm),
        in_specs=[_row(tm, D_SSD), _row(tm, D_SSD), _full((1, D_SSD)), _full(wo.shape), _row(tm, d), _bvec(d),
                  _full((1, d)), _bvec(d), _bvec(d)],
        out_specs=[_row(tm, d), _row(tm, d), _row(tm, D_SSD), _row(tm, d)],
        out_shape=[SDS((b, s, d), F32), SDS((b, s, d), BF16), SDS((b, s, D_SSD), BF16), SDS((b, s, d), BF16)],
        compiler_params=_cparams(2))(ys, attn, mnw, wo, x, gate, *above)


def out_proj_bwd(dout, attn, mnw, wo, name):
    b, s, d = dout.shape
    tm = min(512, s)

    def body(do_ref, at_ref, mnw_ref, wo_ref, dys_ref, dat_ref, dlt_ref, dw_ref):
        lane = lax.broadcasted_iota(I32, (tm, LANES), 1)
        @pl.when(_first_step())
        def _():
            dw_ref[...] = jnp.zeros_like(dw_ref)
        dov = do_ref[...]
        dys_ref[...] = lax.dot_general(dov, wo_ref[0:D_SSD, :], NT_DIMS, preferred_element_type=F32)
        dym = lax.dot_general(dov, wo_ref[D_SSD:2 * D_SSD, :], NT_DIMS, preferred_element_type=F32)
        av = at_ref[...]
        r = _rms(av)
        n = av * r
        dw_ref[...] += jnp.sum(dym * n, axis=0, keepdims=True)
        dat = _rms_bwd(dym * mnw_ref[...], n, r)
        dat_ref[...] = dat.astype(BF16)
        prod = dat * av
        cols = jnp.zeros((tm, LANES), F32)
        for h in range(MLA_HEADS):
            cols += jnp.where(lane == h, jnp.sum(prod[:, HEAD_PAD * h:HEAD_PAD * (h + 1)], axis=1, keepdims=True), 0.0)
        dlt_ref[...] = cols.T[0:MLA_HEADS, :]

    return pl.pallas_call(
        body, name=name, grid=(b, s // tm),
        in_specs=[_row(tm, d), _row(tm, D_SSD), _full((1, D_SSD)), _full(wo.shape)],
        out_specs=[_row(tm, D_SSD), _row(tm, D_SSD),
                   pl.BlockSpec((None, MLA_HEADS, tm), lambda bb, i: (bb, 0, i)), _full((1, D_SSD))],
        out_shape=[SDS((b, s, D_SSD), F32), SDS((b, s, D_SSD), BF16), SDS((b, MLA_HEADS, s), F32),
                   SDS((1, D_SSD), F32)],
        compiler_params=_cparams(2))(dout, attn, mnw, wo)


def adaln_fwd(c_all, w_ada, b_ada, name):
    nb, d = c_all.shape
    n = w_ada.shape[1]

    def body(c_ref, w_ref, b_ref, m_ref, ca_ref):
        cv = c_ref[...]
        ca = (cv * _sigmoid(cv)).astype(BF16)
        ca_ref[...] = ca
        m_ref[...] = jnp.dot(ca, w_ref[...].astype(BF16), preferred_element_type=F32) + b_ref[...]

    return pl.pallas_call(
        body, name=name, out_shape=[SDS((nb, n), F32), SDS((nb, d), BF16)],
        compiler_params=pltpu.CompilerParams(vmem_limit_bytes=VMEM_LIMIT))(c_all, w_ada, b_ada)


def adaln_bwd(c_act, dmod_cols, name):
    d, n = c_act.shape[1], dmod_cols.shape[1]

    def body(c_ref, dm_ref, gw_ref):
        gw_ref[...] = lax.dot_general(c_ref[...], dm_ref[...].astype(BF16), TN_DIMS, preferred_element_type=F32)

    return pl.pallas_call(
        body, name=name, out_shape=SDS((d, n), F32),
        compiler_params=pltpu.CompilerParams(vmem_limit_bytes=VMEM_LIMIT))(c_act, dmod_cols)


def sum_rows(x, name):
    def body(x_ref, o_ref):
        o_ref[...] = jnp.sum(x_ref[...], axis=0, keepdims=True)
    return pl.pallas_call(body, name=name, out_shape=SDS((1, x.shape[1]), F32))(x)


def squeeze_heads(x, et_mat, name):
    def body(x_ref, et_ref, o_ref):
        xv = jnp.broadcast_to(x_ref[...], (8, x.shape[1]))
        o_ref[...] = _dot_01(xv, et_ref[...], 3)[0:1, :]
    return pl.pallas_call(body, name=name, out_shape=SDS((1, LANES), F32))(x, et_mat)


def sum_blocks(x, name):
    n, r, c = x.shape
    tr = next(cand for cand in (256, 128, 64, 32, 16, 8) if r % cand == 0)

    def body(x_ref, o_ref):
        acc = x_ref[0].astype(F32)
        for k in range(1, n):
            acc += x_ref[k].astype(F32)
        o_ref[...] = acc

    return pl.pallas_call(
        body, name=name, grid=(r // tr,), in_specs=[pl.BlockSpec((n, tr, c), lambda i: (0, i, 0))],
        out_specs=pl.BlockSpec((tr, c), lambda i: (i, 0)), out_shape=SDS((r, c), F32),
        compiler_params=_cparams(1))(x)


def _adam_math(w, g, m, v):
    m = ADAM_B1 * m + (1.0 - ADAM_B1) * g
    v = ADAM_B2 * v + (1.0 - ADAM_B2) * (g * g)
    m_hat = m / (1.0 - ADAM_B1 ** ADAM_STEP)
    v_hat = v / (1.0 - ADAM_B2 ** ADAM_STEP)
    return -ADAM_LR * (m_hat / (jnp.sqrt(v_hat) + ADAM_EPS) + ADAM_WD * w), m, v


def adamw(w, g, m, v, name):
    r, c = w.shape[-2:]
    tr = r
    for cand in (512, 256, 128, 64, 32, 16, 8):
        if r % cand == 0 and cand * c * 4 <= 2 * 1024 * 1024:
            tr = cand
            break

    def body(w_ref, g_ref, m_ref, v_ref, d_ref, mo_ref, vo_ref):
        d_ref[...], mo_ref[...], vo_ref[...] = _adam_math(w_ref[...], g_ref[...], m_ref[...], v_ref[...])

    def spec(a):
        return pl.BlockSpec((tr, c), lambda i: (i, 0)) if a.ndim == 2 else pl.BlockSpec((None, tr, c), lambda i: (0, i, 0))

    return pl.pallas_call(
        body, name=name, grid=(r // tr,), in_specs=[spec(w), spec(g), spec(m), spec(v)], out_specs=[spec(w)] * 3,
        out_shape=[SDS(w.shape, F32)] * 3, compiler_params=_cparams(1))(w, g, m, v)


def adamw_blocks(w, blocks, m, v, name):
    r, c = w.shape
    tr = next((cand for cand in (128, 64, 32, 16, 8) if r % cand == 0), r)

    def body(w_ref, b_ref, m_ref, v_ref, g_ref, d_ref, mo_ref, vo_ref):
        g = b_ref[0].astype(F32)
        for k in range(1, N_DEV):
            g += b_ref[k].astype(F32)
        g_ref[...] = g
        d_ref[...], mo_ref[...], vo_ref[...] = _adam_math(w_ref[...], g, m_ref[...], v_ref[...])

    spec = pl.BlockSpec((tr, c), lambda i: (i, 0))
    return pl.pallas_call(
        body, name=name, grid=(r // tr,),
        in_specs=[spec, pl.BlockSpec((N_DEV, tr, c), lambda i: (0, i, 0)), spec, spec], out_specs=[spec] * 4,
        out_shape=[SDS((r, c), F32)] * 4, compiler_params=_cparams(1))(w, blocks, m, v)


def adamw_many(ws, gs, ms, vs, name):
    n = len(ws)

    def body(*refs):
        w_r, g_r, m_r, v_r = (refs[k * n:(k + 1) * n] for k in range(4))
        d_r, mo_r, vo_r = (refs[(4 + k) * n:(5 + k) * n] for k in range(3))
        for k in range(n):
            d_r[k][...], mo_r[k][...], vo_r[k][...] = _adam_math(w_r[k][...], g_r[k][...], m_r[k][...], v_r[k][...])

    shapes = [SDS(w.shape, F32) for w in ws]
    outs = pl.pallas_call(body, name=name, out_shape=shapes * 3)(*ws, *gs, *ms, *vs)
    return outs[:n], outs[n:2 * n], outs[2 * n:]


TRANSPOSED = ("ffn1_w_gate", "ffn1_w_up", "ffn2_w_gate", "ffn2_w_up", "w_in", "w_ukv", "w_uq")
GATHER_GROUPS = (("ffn1_w_gate", "ffn1_w_up"), ("ffn1_w_down",),
                 ("w_in", "w_ukv", "w_uq", "w_out", "ffn2_w_gate", "ffn2_w_up", "ffn2_w_down"))
GRAD_GROUPS = (("ffn2", ("ffn2_w_gate", "ffn2_w_up", "ffn2_w_down")), ("mixer", ("w_out", "w_in", "w_ukv", "w_uq")),
               ("ffn1_down", ("ffn1_w_down",)), ("ffn1_gate", ("ffn1_w_gate",)), ("ffn1_up", ("ffn1_w_up",)))


def _shard_view(name, w):
    return w[0].T if name in TRANSPOSED else w[0]


def _shard_unview(name, t):
    return t.T[None] if name in TRANSPOSED else t[None]


def _grad_blocks(name, gw):
    if name == "w_in":
        return _in_proj_rows_inv(gw).reshape(N_DEV, -1, D_MODEL)
    if name == "w_ukv":
        hd = MLA_HEADS * HEAD_PAD
        return jnp.concatenate([gw[:hd].reshape(MLA_HEADS, HEAD_PAD, KV_LORA)[:, :QK_NOPE],
                                gw[hd:].reshape(MLA_HEADS, V_HEAD, KV_LORA)], axis=1)
    if name == "w_uq":
        return gw.reshape(MLA_HEADS, HEAD_PAD, Q_LORA)[:, :QK_DIM]
    return gw.reshape(N_DEV, -1, D_MODEL)


def _pack_rows(arrs):
    parts = []
    for a in arrs:
        flat = a.reshape(-1).astype(F32)
        pad = (-flat.shape[0]) % D_MODEL
        if pad:
            flat = jnp.pad(flat, (0, pad))
        parts.append(flat.reshape(-1, D_MODEL))
    out = jnp.concatenate(parts, axis=0)
    pad = (-out.shape[0]) % 8
    if pad:
        out = jnp.pad(out, ((0, pad), (0, 0)))
    return out


def _unpack_rows(packed, shapes):
    out, row = [], 0
    for shp in shapes:
        n = math.prod(shp)
        nrow = -(-n // D_MODEL)
        out.append(packed[row:row + nrow].reshape(-1)[:n].reshape(shp))
        row += nrow
    return out


def _in_proj_rows(w_t):
    return jnp.concatenate([w_t[0:2560], w_t[2576:2960], w_t[2960:3216], w_t[2560:2576], w_t[3216:3248],
                            jnp.zeros((D_IN_PAD - D_IN, D_MODEL), w_t.dtype)], axis=0)


def _in_proj_rows_inv(d):
    return jnp.concatenate([d[0:2560], d[3200:3216], d[2560:2944], d[2944:3200], d[3216:3248]], axis=0)


def _rope_tables(positions):
    inv_freq = ROPE_THETA ** (-jnp.arange(0, QK_ROPE, 2, dtype=F32) / QK_ROPE)
    ang = positions[..., None].astype(F32) * inv_freq
    cos, sin = jnp.cos(ang), jnp.sin(ang)
    one = jnp.ones(ang.shape[:2] + (QK_NOPE,), F32)
    zero = jnp.zeros_like(one)
    z16, z32, o32 = zero[..., :16], zero[..., :32], one[..., :32]
    cc = jnp.concatenate([one, cos, cos, o32], axis=-1)
    sp = jnp.concatenate([zero, z16, sin, z32], axis=-1)
    sm = jnp.concatenate([zero, -sin, z16, z32], axis=-1)
    return cc, sp, sm


def weight_views(gathered):
    full = lambda name: gathered[name].reshape(-1, gathered[name].shape[2])
    ukv = full("w_ukv").reshape(MLA_HEADS, QK_NOPE + V_HEAD, KV_LORA)
    wukv_t = jnp.concatenate([jnp.pad(ukv[:, :QK_NOPE], ((0, 0), (0, HEAD_PAD - QK_NOPE), (0, 0))).reshape(-1, KV_LORA),
                              ukv[:, QK_NOPE:].reshape(-1, KV_LORA)], axis=0)
    uq = full("w_uq").reshape(MLA_HEADS, QK_DIM, Q_LORA)
    wuq_t = jnp.pad(uq, ((0, 0), (0, HEAD_PAD - QK_DIM), (0, 0))).reshape(-1, Q_LORA)
    return dict(wg1_t=full("ffn1_w_gate"), wu1_t=full("ffn1_w_up"), wd1=full("ffn1_w_down"),
                wg2_t=full("ffn2_w_gate"), wu2_t=full("ffn2_w_up"), wd2=full("ffn2_w_down"),
                wo=full("w_out"), win_t=_in_proj_rows(full("w_in")), wukv_t=wukv_t, wuq_t=wuq_t)


def _ffn_bwd(tag, dxn, do, dgate, x, h, gg, uu, a, sc, norm_w, wg_t, wu_t, wd, below):
    f2 = wd.shape[0] // 2
    dwd = mm_tn(a, do, f2, D_MODEL, tag + "_dwd")
    dgg, duu = ffn_dact(do, wd, gg, uu, tag + "_dact")
    dwg_t = mm_tn(dgg, h, f2, D_MODEL, tag + "_dwg")
    dwu_t = mm_tn(duu, h, f2, D_MODEL, tag + "_dwu")
    dx, dsc, dsh, dnw, *nxt = dh_norm_bwd([dgg, duu], [wg_t, wu_t], x, dxn, norm_w, sc, tag + "_dh", below)
    return dx, (dsh, dsc, dgate), dnw, (dwg_t, dwu_t, dwd), nxt


def local_step(x, tgt, positions, mod, wv, p):
    nb, s, d = x.shape
    sh1, sc1, g1, sh2, sc2, g2, sh3, sc3, g3 = mod
    cc, sp, sm = _rope_tables(positions)
    lane_head = jnp.arange(D_SSD, dtype=I32)[None, :] // SSD_HEAD_DIM
    e_mat = (lane_head == jnp.arange(LANES, dtype=I32)[:, None]).astype(BF16)
    et_mat = e_mat.T
    rr, cl = jnp.arange(LANES, dtype=I32)[:, None], jnp.arange(LANES, dtype=I32)[None, :]
    place = ((cl == rr + (QK_NOPE - SSD_HEADS)) & (rr >= SSD_HEADS) & (rr < SSD_HEADS + QK_ROPE)).astype(F32)
    dtb = jnp.pad(p["dt_bias"], ((0, 0), (0, LANES - SSD_HEADS)))
    alog = jnp.pad(p["a_log"], ((0, 0), (0, LANES - SSD_HEADS)))
    dskip_e = jnp.repeat(p["d_skip"], SSD_HEAD_DIM, axis=1)

    h1 = norm_mod(x, p["norm_ffn1"], sc1, sh1, "ffn1_norm")
    gg1, uu1, a1 = ffn_up(h1, wv["wg1_t"], wv["wu1_t"], "ffn1_up")
    x1, o1, h2 = ffn_down(a1, wv["wd1"], x, g1, 0.5, "ffn1_down", (p["norm_mix"], sc2, sh2))
    z, u, cq, ckv, misc = in_proj(h2, wv["win_t"], "in_proj")
    xs, bm, cm_ = conv_fwd(u, p["conv_w"], p["conv_b"], "conv_fwd")
    ys, y, prev = ssd_fwd(xs, bm, cm_, misc, z, dtb, alog, dskip_e, p["ssd_norm_w"], e_mat, "ssd_fwd")
    q, k, v, vt, qn, kvn = qkv_fwd(cq, ckv, misc, cc, sp, sm, p["q_norm_w"], p["kv_norm_w"], wv["wuq_t"], wv["wukv_t"],
                               place, "qkv_fwd")
    attn, lse = flash_fwd(q, k, vt, "flash_fwd")
    x2, o2, ym, h3 = out_proj(ys, attn, p["mla_norm_w"], wv["wo"], x1, g2, (p["norm_ffn2"], sc3, sh3), "out_proj")
    gg3, uu3, a3 = ffn_up(h3, wv["wg2_t"], wv["wu2_t"], "ffn2_up")
    loss, dx3, dnfin, do3, dg3 = ffn_down_final(a3, wv["wd2"], x2, g3, 0.5, p["norm_final"], tgt, "ffn2_down_loss")

    dx2, dmod3, dnf2, (dwg2, dwu2, dwd2), (dout, dg2) = _ffn_bwd(
        "ffn2", dx3, do3, dg3, x2, h3, gg3, uu3, a3, sc3, p["norm_ffn2"], wv["wg2_t"], wv["wu2_t"], wv["wd2"],
        (o2, g2, 1.0))
    dys, dattn, dlt, dmlan = out_proj_bwd(dout, attn, p["mla_norm_w"], wv["wo"], "out_proj_bwd")
    dwo = jnp.concatenate([mm_tn(ys, dout, D_SSD, D_MODEL, "dwo_ssd"), mm_tn(ym, dout, D_SSD, D_MODEL, "dwo_mla")], axis=0)
    dxs, dbm, dcm, dz, ddt, dssdn, ddsk_lane, ddtb, dalog = ssd_bwd(
        dys, y, z, xs, bm, cm_, misc, prev, dtb, alog, dskip_e, p["ssd_norm_w"], e_mat, et_mat, "ssd_bwd")
    dq, dk, dv = flash_bwd(q, k, v, dattn, lse, dlt, "flash_bwd")
    dcq, dckv, dmisc, dqp, dkvc, dqn, dkvn = qkv_bwd(dq, dk, dv, ddt, cq, ckv, cc, sp, sm, p["q_norm_w"], p["kv_norm_w"],
                                                     wv["wuq_t"], wv["wukv_t"], place.T, "qkv_bwd")
    dwuq = mm_tn(dqp, qn, MLA_HEADS * HEAD_PAD, Q_LORA, "dwuq")
    dwukv = mm_tn(dkvc, kvn, MLA_HEADS * HEAD_PAD, KV_LORA, "dwukv")
    du, dconv = conv_bwd(dxs, dbm, dcm, u, p["conv_w"], p["conv_b"], "conv_bwd")
    dconv = conv_grads_fold(dconv, "conv_grads_fold")
    dproj = [dz, du, dcq, dckv, dmisc]
    dwin = mm_tn_blocks(dproj, h2, "dwin")
    dx1, dsc2, dsh2, dnmix, do1, dg1 = dh_norm_bwd(dproj, [wv["win_t"]], x1, dx2, p["norm_mix"], sc2, "mix_dh",
                                                   (o1, g1, 0.5))
    dx0, dmod1, dnf1, (dwg1, dwu1, dwd1), _ = _ffn_bwd(
        "ffn1", dx1, do1, dg1, x, h1, gg1, uu1, a1, sc1, p["norm_ffn1"], wv["wg1_t"], wv["wu1_t"], wv["wd1"], None)

    dmod = jnp.concatenate([*dmod1, dsh2, dsc2, dg2, *dmod3], axis=1).reshape(nb, N_MOD * d)
    return dict(
        loss=loss, dx=dx0, dmod=dmod, norm_ffn1=dnf1, norm_mix=dnmix, norm_ffn2=dnf2, norm_final=dnfin,
        ssd_norm_w=dssdn, mla_norm_w=dmlan, q_norm_w=dqn, kv_norm_w=dkvn,
        dt_bias=ddtb[:, :SSD_HEADS], a_log=dalog[:, :SSD_HEADS],
        d_skip=squeeze_heads(ddsk_lane, et_mat, "d_skip_heads")[:, :SSD_HEADS],
        conv_b=dconv[4:5], conv_w=dconv[0:4],
        gw=dict(ffn1_w_gate=dwg1, ffn1_w_up=dwu1, ffn1_w_down=dwd1, ffn2_w_gate=dwg2, ffn2_w_up=dwu2, ffn2_w_down=dwd2,
                w_out=dwo, w_in=dwin, w_ukv=dwukv, w_uq=dwuq))


def kernel(x, c, positions, w_ada, b_ada, norm_ffn1, ffn1_w_gate, ffn1_w_up, ffn1_w_down, norm_mix, w_in, conv_w, conv_b, dt_bias, a_log, d_skip, ssd_norm_w, q_norm_w, w_uq, kv_norm_w, w_ukv, mla_norm_w, w_out, norm_ffn2, ffn2_w_gate, ffn2_w_up, ffn2_w_down, norm_final, loss_target, m_w_ada, m_b_ada, m_norm_ffn1, m_ffn1_w_gate, m_ffn1_w_up, m_ffn1_w_down, m_norm_mix, m_w_in, m_conv_w, m_conv_b, m_dt_bias, m_a_log, m_d_skip, m_ssd_norm_w, m_q_norm_w, m_w_uq, m_kv_norm_w, m_w_ukv, m_mla_norm_w, m_w_out, m_norm_ffn2, m_ffn2_w_gate, m_ffn2_w_up, m_ffn2_w_down, m_norm_final, v_w_ada, v_b_ada, v_norm_ffn1, v_ffn1_w_gate, v_ffn1_w_up, v_ffn1_w_down, v_norm_mix, v_w_in, v_conv_w, v_conv_b, v_dt_bias, v_a_log, v_d_skip, v_ssd_norm_w, v_q_norm_w, v_w_uq, v_kv_norm_w, v_w_ukv, v_mla_norm_w, v_w_out, v_norm_ffn2, v_ffn2_w_gate, v_ffn2_w_up, v_ffn2_w_down, v_norm_final):
    names = ["w_ada", "b_ada", "norm_ffn1", "ffn1_w_gate", "ffn1_w_up", "ffn1_w_down", "norm_mix", "w_in", "conv_w",
             "conv_b", "dt_bias", "a_log", "d_skip", "ssd_norm_w", "q_norm_w", "w_uq", "kv_norm_w", "w_ukv",
             "mla_norm_w", "w_out", "norm_ffn2", "ffn2_w_gate", "ffn2_w_up", "ffn2_w_down", "norm_final"]
    W = dict(zip(names, (w_ada, b_ada, norm_ffn1, ffn1_w_gate, ffn1_w_up, ffn1_w_down, norm_mix, w_in, conv_w, conv_b, dt_bias, a_log, d_skip, ssd_norm_w, q_norm_w, w_uq, kv_norm_w, w_ukv, mla_norm_w, w_out, norm_ffn2, ffn2_w_gate, ffn2_w_up, ffn2_w_down, norm_final)))
    M = dict(zip(names, (m_w_ada, m_b_ada, m_norm_ffn1, m_ffn1_w_gate, m_ffn1_w_up, m_ffn1_w_down, m_norm_mix, m_w_in, m_conv_w, m_conv_b, m_dt_bias, m_a_log, m_d_skip, m_ssd_norm_w, m_q_norm_w, m_w_uq, m_kv_norm_w, m_w_ukv, m_mla_norm_w, m_w_out, m_norm_ffn2, m_ffn2_w_gate, m_ffn2_w_up, m_ffn2_w_down, m_norm_final)))
    V = dict(zip(names, (v_w_ada, v_b_ada, v_norm_ffn1, v_ffn1_w_gate, v_ffn1_w_up, v_ffn1_w_down, v_norm_mix, v_w_in, v_conv_w, v_conv_b, v_dt_bias, v_a_log, v_d_skip, v_ssd_norm_w, v_q_norm_w, v_w_uq, v_kv_norm_w, v_w_ukv, v_mla_norm_w, v_w_out, v_norm_ffn2, v_ffn2_w_gate, v_ffn2_w_up, v_ffn2_w_down, v_norm_final)))

    nb, s, d = x.shape
    me = 4 * lax.axis_index("x") + 2 * lax.axis_index("y") + lax.axis_index("c")
    n_ada = w_ada.shape[2]

    taps, n_cw = conv_w.shape[1:]
    (cg,) = all_gather8([_pack_rows([c, conv_w[0]])], "gather_c")
    c_all = cg[:, 0:nb].reshape(N_DEV * nb, d)
    conv_w_full = cg[:, nb, 0:taps * n_cw].reshape(N_DEV, taps, n_cw).transpose(1, 0, 2).reshape(taps, N_DEV * n_cw)
    shards = [[_shard_view(name, W[name]).astype(BF16) for name in group] for group in GATHER_GROUPS]
    gathered = dict(zip(GATHER_GROUPS[0], all_gather8(shards[0], "gather_w_ffn1")))

    b_ada_cols = lax.dynamic_slice(b_ada, (0, me * n_ada), (1, n_ada))
    mod_cols, c_act = adaln_fwd(c_all, w_ada[0], b_ada_cols, "adaln_fwd")
    (mod_g,) = all_gather8([mod_cols], "gather_mod")
    gathered, mod_g, shards = lax.optimization_barrier((gathered, mod_g, shards))
    gathered.update(zip(GATHER_GROUPS[1], sc_all_gather8(shards[1], "gather_w_ffn1_down", 1)))
    gathered.update(zip(GATHER_GROUPS[2], sc_all_gather8(shards[2], "gather_w_rest", 7)))
    wv = weight_views(gathered)
    mod = lax.dynamic_slice(mod_g, (0, me * nb, 0), (N_DEV, nb, n_ada)).transpose(1, 0, 2).reshape(nb, N_MOD, 1, d)
    mod = [mod[:, k] for k in range(N_MOD)]

    P = dict(W)
    P["conv_w"] = conv_w_full
    P["norm_final"] = norm_final.reshape(1, d)
    R = local_step(x, loss_target, positions, mod, wv, P)

    dmod = R["dmod"]
    partial_shapes = [(1,), (1, d), (1, d), (1, d), (1, d), (1, d), (1, d), (1, Q_LORA), (1, KV_LORA),
                      (1, SSD_HEADS), (1, SSD_HEADS), (1, SSD_HEADS), (1, D_CONV), (4, D_CONV), (1, N_MOD * d),
                      (nb, N_MOD * d)]
    partial = _pack_rows([R["loss"][0, :1], R["norm_ffn1"], R["norm_mix"], R["norm_ffn2"], R["norm_final"],
                          R["ssd_norm_w"], R["mla_norm_w"], R["q_norm_w"], R["kv_norm_w"],
                          R["dt_bias"], R["a_log"], R["d_skip"], R["conv_b"], R["conv_w"],
                          sum_rows(dmod, "dmod_rows"), dmod])
    (partial_g,) = all_gather8([partial], "gather_partials")
    (loss, g_nf1, g_nmix, g_nf2, g_nfin, g_ssdn, g_mlan, g_qn, g_kvn, g_dtb, g_alog, g_dskip, g_convb, g_convw,
     g_bada, _) = _unpack_rows(sum_blocks(partial_g, "sum_partials"), partial_shapes)
    dmod_row = sum(-(-math.prod(shp) // D_MODEL) for shp in partial_shapes[:-1])
    dmod_all = partial_g[:, dmod_row:dmod_row + nb * N_MOD].reshape(N_DEV * nb, N_MOD * d)
    g_wada = adaln_bwd(c_act, lax.dynamic_slice(dmod_all, (0, me * n_ada), (N_DEV * nb, n_ada)), "adaln_bwd")
    n_cw = conv_w.shape[2]
    G = {"w_ada": g_wada[None], "b_ada": g_bada, "norm_ffn1": g_nf1, "norm_mix": g_nmix, "norm_ffn2": g_nf2,
         "norm_final": g_nfin.reshape(d), "ssd_norm_w": g_ssdn, "mla_norm_w": g_mlan, "q_norm_w": g_qn,
         "kv_norm_w": g_kvn, "dt_bias": g_dtb, "a_log": g_alog, "d_skip": g_dskip, "conv_b": g_convb,
         "conv_w": lax.dynamic_slice(g_convw, (0, me * n_cw), (4, n_cw))[None]}

    DW, NM, NV = {}, {}, {}
    gw = R["gw"]
    for k, (tag, group) in enumerate(GRAD_GROUPS):
        send = [_grad_blocks(name, gw[name]).reshape(N_DEV, *_shard_view(name, W[name]).shape) for name in group]
        recv = sc_all_to_all8(send, "exchange_" + tag, 2 + k)
        for name, blocks in zip(group, recv):
            res = adamw_blocks(_shard_view(name, W[name]), blocks, _shard_view(name, M[name]), _shard_view(name, V[name]),
                               "adamw_" + name)
            G[name], DW[name], NM[name], NV[name] = [_shard_unview(name, t) for t in res]
    DW["w_ada"], NM["w_ada"], NV["w_ada"] = adamw(w_ada, g_wada, m_w_ada, v_w_ada, "adamw_w_ada")
    small = [n for n in names if n not in DW]
    as2d = lambda a: a.reshape(-1, a.shape[-1])
    outs = adamw_many([as2d(W[n]) for n in small], [as2d(G[n]) for n in small], [as2d(M[n]) for n in small],
                      [as2d(V[n]) for n in small], "adamw_small")
    for res, dst in zip(outs, (DW, NM, NV)):
        for n, t in zip(small, res):
            dst[n] = t.reshape(W[n].shape)
    return (loss.reshape(()), R["dx"], *[G[n] for n in names], *[DW[n] for n in names], *[NM[n] for n in names],
            *[NV[n] for n in names])
```

```python
import math

import jax
import jax.numpy as jnp
from jax import lax
from jax.experimental import pallas as pl
from jax.experimental.pallas import tpu as pltpu
from jax.experimental.pallas import tpu_sc as plsc

F32, BF16, I32 = jnp.float32, jnp.bfloat16, jnp.int32
HI = lax.Precision.HIGHEST
SDS = jax.ShapeDtypeStruct
MESH = pl.DeviceIdType.MESH

D_MODEL = 1024
D_FF = 2816
D_SSD = 1024
SSD_HEADS = 16
SSD_HEAD_DIM = 64
SSD_GROUPS = 2
SSD_STATE = 128
CHUNK = 128
MLA_HEADS = 8
QK_NOPE = 64
QK_ROPE = 32
QK_DIM = 96
V_HEAD = 128
Q_LORA = 384
KV_LORA = 256
ROPE_THETA = 10000.0
N_MOD = 9
EPS = 1e-6
D_CONV = 1536
D_IN = 3248
D_IN_PAD = 3328
HEAD_PAD = 128
N_DEV = 8
ADAM_LR, ADAM_B1, ADAM_B2, ADAM_EPS, ADAM_WD, ADAM_STEP = 0.001, 0.9, 0.999, 1e-08, 0.01, 10

SAVED_ACT = BF16
VMEM_LIMIT = 56 * 1024 * 1024
LANES = 128
NT_DIMS = (((1,), (1,)), ((), ()))
TN_DIMS = (((0,), (0,)), ((), ()))


def _cparams(n_axes):
    return pltpu.CompilerParams(dimension_semantics=("arbitrary",) * n_axes, vmem_limit_bytes=VMEM_LIMIT)


def _row(tm, d):
    return pl.BlockSpec((None, tm, d), lambda b, i: (b, i, 0))


def _bvec(d):
    return pl.BlockSpec((None, 1, d), lambda b, i: (b, 0, 0))


def _full(shape):
    n = len(shape)
    return pl.BlockSpec(shape, lambda *_: (0,) * n)


def _sigmoid(x):
    return 1.0 / (1.0 + jnp.exp(-x))


def _softplus(x):
    return jnp.maximum(x, 0.0) + jnp.log(1.0 + jnp.exp(-jnp.abs(x)))


def _rms(x):
    return lax.rsqrt(jnp.mean(x * x, axis=-1, keepdims=True) + EPS)


def _rms_bwd(dn, n, r):
    return r * (dn - n * jnp.mean(dn * n, axis=-1, keepdims=True))


def _first_step():
    return (pl.program_id(0) == 0) & (pl.program_id(1) == 0)


def _gather_copies(x_refs, out_refs, send_sems, recv_sems, local_sems):
    mx, my, mc = lax.axis_index("x"), lax.axis_index("y"), lax.axis_index("c")
    me, sibling = (mx, my, mc), (mx, my, 1 - mc)
    chips = [(1 - mx, my), (mx, 1 - my), (1 - mx, 1 - my)]

    def copy(a, k, block, to, src=None):
        rows = out_refs[a].at[4 * block[0] + 2 * block[1] + block[2]]
        return pltpu.make_async_remote_copy(
            src_ref=rows if src is None else src, dst_ref=rows,
            send_sem=send_sems.at[7 * a + k], recv_sem=recv_sems.at[7 * a + k], device_id=to, device_id_type=MESH)

    arrays = range(len(x_refs))
    mine = [pltpu.make_async_copy(x_refs[a], out_refs[a].at[4 * mx + 2 * my + mc], local_sems.at[a]) for a in arrays]
    first = [[copy(a, 0, me, sibling, src=x_refs[a])] + [copy(a, 1 + j, me, (*chip, mc), src=x_refs[a])
                                                          for j, chip in enumerate(chips)] for a in arrays]
    passed = [[copy(a, 4 + j, (*chip, mc), sibling) for j, chip in enumerate(chips)] for a in arrays]
    for a in arrays:
        mine[a].start()
        for cp in first[a]:
            cp.start()
    for a in arrays:
        for j, chip in enumerate(chips):
            copy(a, 1 + j, (*chip, mc), me).wait_recv()
            passed[a][j].start()
    for a in arrays:
        copy(a, 0, sibling, me).wait_recv()
        for j, chip in enumerate(chips):
            copy(a, 4 + j, (*chip, 1 - mc), me).wait_recv()
    for a in arrays:
        for cp in first[a] + passed[a]:
            cp.wait_send()
        mine[a].wait()


def _gather_peers():
    mx, my, mc = lax.axis_index("x"), lax.axis_index("y"), lax.axis_index("c")
    return [(mx, my, 1 - mc), (1 - mx, my, mc), (mx, 1 - my, mc), (1 - mx, 1 - my, mc)]


def _comm_scratch(n):
    return [pltpu.SemaphoreType.DMA((7 * n,)), pltpu.SemaphoreType.DMA((7 * n,)), pltpu.SemaphoreType.DMA((n,))]


def all_gather8(xs, name):
    n = len(xs)

    def body(*refs):
        _gather_copies(refs[:n], refs[n:2 * n], *refs[2 * n:])

    return pl.pallas_call(
        body, name=name,
        out_shape=[SDS((N_DEV, *x.shape), x.dtype) for x in xs],
        in_specs=[pl.BlockSpec(memory_space=pl.ANY)] * n,
        out_specs=[pl.BlockSpec(memory_space=pl.ANY)] * n,
        scratch_shapes=_comm_scratch(n),
    )(*xs)


def _exchange_peers():
    mx, my, mc = lax.axis_index("x"), lax.axis_index("y"), lax.axis_index("c")
    return [(1 - mx if rel & 4 else mx, 1 - my if rel & 2 else my, 1 - mc if rel & 1 else mc) for rel in range(1, N_DEV)]


def _exchange_copies(x_refs, out_refs, send_sems, recv_sems, local_sems):
    mx, my, mc = lax.axis_index("x"), lax.axis_index("y"), lax.axis_index("c")
    me = 4 * mx + 2 * my + mc
    copies = []
    for a, (x_ref, out_ref) in enumerate(zip(x_refs, out_refs)):
        mine = pltpu.make_async_copy(x_ref.at[me], out_ref.at[me], local_sems.at[a])
        mine.start()
        copies.append(mine)
        for k, (px, py, pc) in enumerate(_exchange_peers()):
            cp = pltpu.make_async_remote_copy(
                src_ref=x_ref.at[4 * px + 2 * py + pc], dst_ref=out_ref.at[me],
                send_sem=send_sems.at[7 * a + k], recv_sem=recv_sems.at[7 * a + k],
                device_id=(px, py, pc), device_id_type=MESH)
            cp.start()
            copies.append(cp)
    for cp in copies:
        cp.wait()


def all_to_all8(xs, name):
    n = len(xs)

    def body(*refs):
        _exchange_copies(refs[:n], refs[n:2 * n], *refs[2 * n:])

    return pl.pallas_call(
        body, name=name,
        out_shape=[SDS(x.shape, x.dtype) for x in xs],
        in_specs=[pl.BlockSpec(memory_space=pl.ANY)] * n,
        out_specs=[pl.BlockSpec(memory_space=pl.ANY)] * n,
        scratch_shapes=_comm_scratch(n),
    )(*xs)


def _sequencer_kernel(name, collective_id, n_arrays):
    return pl.kernel(
        mesh=plsc.ScalarSubcoreMesh(axis_name="seq", num_cores=1), name=name,
        scratch_types=tuple(_comm_scratch(n_arrays)),
        compiler_params=pltpu.CompilerParams(collective_id=collective_id))


def _handshake(peers):
    barrier = pltpu.get_barrier_semaphore()
    for peer in peers:
        pl.semaphore_signal(barrier, inc=1, device_id=peer, device_id_type=MESH)
    pl.semaphore_wait(barrier, len(peers))


def _hbm_refs(xs, out_shapes):
    x_refs = [jax.new_ref(x, memory_space=pltpu.MemorySpace.HBM) for x in xs]
    out_refs = [jax.empty_ref(SDS(shp, x.dtype), memory_space=pltpu.MemorySpace.HBM) for x, shp in zip(xs, out_shapes)]
    return x_refs, out_refs


def sc_all_gather8(xs, name, collective_id):
    x_refs, out_refs = _hbm_refs(xs, [(N_DEV, *x.shape) for x in xs])

    @_sequencer_kernel(name, collective_id, len(xs))
    def launch(send_sems, recv_sems, local_sems):
        _handshake(_gather_peers())
        _gather_copies(x_refs, out_refs, send_sems, recv_sems, local_sems)

    launch()
    return [ref[...] for ref in out_refs]


def sc_all_to_all8(xs, name, collective_id):
    x_refs, out_refs = _hbm_refs(xs, [x.shape for x in xs])

    @_sequencer_kernel(name, collective_id, len(xs))
    def launch(send_sems, recv_sems, local_sems):
        _handshake(_exchange_peers())
        _exchange_copies(x_refs, out_refs, send_sems, recv_sems, local_sems)

    launch()
    return [ref[...] for ref in out_refs]


def norm_mod(x, w, sc, sh, name):
    b, s, d = x.shape
    tm = min(512, s)

    def body(x_ref, w_ref, sc_ref, sh_ref, h_ref):
        xv = x_ref[...]
        n = xv * _rms(xv)
        h_ref[...] = ((n * w_ref[...]) * (1.0 + sc_ref[...]) + sh_ref[...]).astype(BF16)

    return pl.pallas_call(
        body, name=name, grid=(b, s // tm),
        in_specs=[_row(tm, d), _full((1, d)), _bvec(d), _bvec(d)],
        out_specs=_row(tm, d), out_shape=SDS((b, s, d), BF16), compiler_params=_cparams(2))(x, w, sc, sh)


def ffn_up(h, wg_t, wu_t, name):
    b, s, d = h.shape
    f = wg_t.shape[0]
    tm, tn = min(1024, s), f // 2

    def body(h_ref, wg_ref, wu_ref, s_ref, t_ref, a_ref):
        hv = h_ref[...]
        g = lax.dot_general(hv, wg_ref[...], NT_DIMS, preferred_element_type=F32)
        u = lax.dot_general(hv, wu_ref[...], NT_DIMS, preferred_element_type=F32)
        sg = _sigmoid(g)
        silu = g * sg
        s_ref[...] = silu.astype(s_ref.dtype)
        t_ref[...] = (u * (sg + silu * (1.0 - sg))).astype(t_ref.dtype)
        a_ref[...] = (silu * u).astype(BF16)

    hs = pl.BlockSpec((None, tm, d), lambda j, bb, i: (bb, i, 0))
    ws = pl.BlockSpec((tn, d), lambda j, bb, i: (j, 0))
    os_ = pl.BlockSpec((None, tm, tn), lambda j, bb, i: (bb, i, j))
    return pl.pallas_call(
        body, name=name, grid=(f // tn, b, s // tm),
        in_specs=[hs, ws, ws], out_specs=[os_, os_, os_],
        out_shape=[SDS((b, s, f), SAVED_ACT), SDS((b, s, f), SAVED_ACT), SDS((b, s, f), BF16)],
        compiler_params=_cparams(3))(h, wg_t, wu_t)


def _norm_mod_tile(xv, w_ref, sc_ref, sh_ref):
    return ((xv * _rms(xv) * w_ref[...]) * (1.0 + sc_ref[...]) + sh_ref[...]).astype(BF16)


def ffn_down(a, wd, x, gate, scale, name, above=None):
    b, s, f = a.shape
    d = wd.shape[1]
    tm = min(1024, s)

    def body(a_ref, wd_ref, x_ref, g_ref, *rest):
        xn_ref, o_ref = rest[-3:-1] if above else rest
        o = jnp.dot(a_ref[...], wd_ref[...], preferred_element_type=F32)
        xn = x_ref[...] + (scale * g_ref[...]) * o
        xn_ref[...] = xn
        o_ref[...] = o.astype(BF16)
        if above:
            rest[-1][...] = _norm_mod_tile(xn, *rest[0:3])

    extra = above is not None
    return pl.pallas_call(
        body, name=name, grid=(b, s // tm),
        in_specs=[_row(tm, f), _full((f, d)), _row(tm, d), _bvec(d)] + ([_full((1, d)), _bvec(d), _bvec(d)] if extra else []),
        out_specs=[_row(tm, d), _row(tm, d)] + ([_row(tm, d)] if extra else []),
        out_shape=[SDS((b, s, d), F32), SDS((b, s, d), BF16)] + ([SDS((b, s, d), BF16)] if extra else []),
        compiler_params=_cparams(2))(a, wd, x, gate, *(above or ()))


def ffn_down_final(a, wd, x, gate, scale, w_final, tgt, name):
    b, s, f = a.shape
    d = wd.shape[1]
    tm = min(1024, s)

    def body(a_ref, wd_ref, x_ref, g_ref, w_ref, t_ref, loss_ref, dx_ref, dw_ref, do_ref, dg_ref):
        @pl.when(_first_step())
        def _():
            loss_ref[...] = jnp.zeros_like(loss_ref)
            dw_ref[...] = jnp.zeros_like(dw_ref)

        @pl.when(pl.program_id(1) == 0)
        def _():
            dg_ref[...] = jnp.zeros_like(dg_ref)
        o = jnp.dot(a_ref[...], wd_ref[...], preferred_element_type=F32)
        sg = scale * g_ref[...]
        xv = x_ref[...] + sg * o
        r = _rms(xv)
        n = xv * r
        wv = w_ref[...]
        e = n * wv - t_ref[...]
        loss_ref[...] += jnp.sum(e * e) * (0.5 / d)
        dy = e * (1.0 / d)
        dw_ref[...] += jnp.sum(dy * n, axis=0, keepdims=True)
        dx = _rms_bwd(dy * wv, n, r)
        dx_ref[...] = dx
        do_ref[...] = (sg * dx).astype(BF16)
        dg_ref[...] += jnp.sum(scale * dx * o, axis=0, keepdims=True)

    return pl.pallas_call(
        body, name=name, grid=(b, s // tm),
        in_specs=[_row(tm, f), _full((f, d)), _row(tm, d), _bvec(d), _full((1, d)), _row(tm, d)],
        out_specs=[_full((1, LANES)), _row(tm, d), _full((1, d)), _row(tm, d), _bvec(d)],
        out_shape=[SDS((1, LANES), F32), SDS((b, s, d), F32), SDS((1, d), F32), SDS((b, s, d), BF16), SDS((b, 1, d), F32)],
        compiler_params=_cparams(2))(a, wd, x, gate, w_final, tgt)


def ffn_dact(do, wd, silu_g, u_dsilu, name):
    b, s, d = do.shape
    f = wd.shape[0]
    tm, tn = min(1024, s), f // 2

    def body(do_ref, wd_ref, s_ref, t_ref, dg_ref, du_ref):
        da = lax.dot_general(do_ref[...], wd_ref[...], NT_DIMS, preferred_element_type=F32)
        dg_ref[...] = (da * t_ref[...].astype(F32)).astype(BF16)
        du_ref[...] = (da * s_ref[...].astype(F32)).astype(BF16)

    dos = pl.BlockSpec((None, tm, d), lambda j, bb, i: (bb, i, 0))
    ws = pl.BlockSpec((tn, d), lambda j, bb, i: (j, 0))
    es = pl.BlockSpec((None, tm, tn), lambda j, bb, i: (bb, i, j))
    return pl.pallas_call(
        body, name=name, grid=(f // tn, b, s // tm),
        in_specs=[dos, ws, es, es], out_specs=[es, es],
        out_shape=[SDS((b, s, f), BF16), SDS((b, s, f), BF16)], compiler_params=_cparams(3))(do, wd, silu_g, u_dsilu)


def mm_tn(a, bm, tma, tnb, name):
    b, s, ka = a.shape
    nb = bm.shape[2]
    tk = min(2048, s)
    nk = s // tk

    def body(a_ref, b_ref, o_ref, acc):
        first = (pl.program_id(2) == 0) & (pl.program_id(3) == 0)
        last = (pl.program_id(2) == b - 1) & (pl.program_id(3) == nk - 1)
        part = lax.dot_general(a_ref[...], b_ref[...], TN_DIMS, preferred_element_type=F32)

        @pl.when(first)
        def _():
            acc[...] = part

        @pl.when(jnp.logical_not(first))
        def _():
            acc[...] += part

        @pl.when(last)
        def _():
            o_ref[...] = acc[...].astype(BF16)

    return pl.pallas_call(
        body, name=name, grid=(ka // tma, nb // tnb, b, nk),
        in_specs=[pl.BlockSpec((None, tk, tma), lambda i, j, bb, k: (bb, k, i)),
                  pl.BlockSpec((None, tk, tnb), lambda i, j, bb, k: (bb, k, j))],
        out_specs=pl.BlockSpec((tma, tnb), lambda i, j, bb, k: (i, j)),
        out_shape=SDS((ka, nb), BF16), scratch_shapes=[pltpu.VMEM((tma, tnb), F32)],
        compiler_params=_cparams(4))(a, bm)


def mm_tn_blocks(a_blocks, bm, name):
    b, s, nb = bm.shape
    widths = [a.shape[2] for a in a_blocks]
    starts = [sum(widths[:k]) for k in range(len(widths))]
    tk = min(1024, s)
    nk = s // tk
    n = len(a_blocks)

    def body(*refs):
        a_refs, b_ref, o_ref, acc = refs[:n], refs[n], refs[n + 1], refs[n + 2]
        first = (pl.program_id(0) == 0) & (pl.program_id(1) == 0)
        last = (pl.program_id(0) == b - 1) & (pl.program_id(1) == nk - 1)

        @pl.when(first)
        def _():
            acc[...] = jnp.zeros_like(acc)
        bv = b_ref[...]
        for a_ref, st, wd in zip(a_refs, starts, widths):
            acc[st:st + wd, :] += lax.dot_general(a_ref[...], bv, TN_DIMS, preferred_element_type=F32)

        @pl.when(last)
        def _():
            o_ref[...] = acc[...].astype(BF16)

    return pl.pallas_call(
        body, name=name, grid=(b, nk),
        in_specs=[_row(tk, wd) for wd in widths] + [_row(tk, nb)],
        out_specs=_full((sum(widths), nb)), out_shape=SDS((sum(widths), nb), BF16),
        scratch_shapes=[pltpu.VMEM((sum(widths), nb), F32)], compiler_params=_cparams(2))(*a_blocks, bm)


def _gate_bwd_specs(tm, d, b, s):
    return ([_row(tm, d), _bvec(d)], [_row(tm, d), _bvec(d)], [SDS((b, s, d), BF16), SDS((b, 1, d), F32)])


def _gate_bwd_tile(dx, scale, o_ref, g_ref, do_ref, dg_ref):
    do_ref[...] = ((scale * g_ref[...]) * dx).astype(BF16)
    dg_ref[...] += jnp.sum(scale * dx * o_ref[...].astype(F32), axis=0, keepdims=True)


def dh_norm_bwd(dys, wts, x, dxn, w, sc, name, below=None):
    b, s, d = x.shape
    tm = min(512, s)
    n_in, n_w = len(dys), len(wts)
    extra_in, extra_out, extra_shape = _gate_bwd_specs(tm, d, b, s) if below else ([], [], [])
    starts = [sum(dy.shape[2] for dy in dys[:k]) for k in range(n_in)]

    def body(*refs):
        dy_refs, w_refs = refs[:n_in], refs[n_in:n_in + n_w]
        x_ref, dxn_ref, nw_ref, sc_ref = refs[n_in + n_w:n_in + n_w + 4]
        rest = refs[n_in + n_w + 4:]
        if below:
            o_ref, g_ref, dx_ref, dsc_ref, dsh_ref, dw_ref, do_ref, dg_ref = rest
        else:
            dx_ref, dsc_ref, dsh_ref, dw_ref = rest

        @pl.when(pl.program_id(1) == 0)
        def _():
            dsc_ref[...] = jnp.zeros_like(dsc_ref)
            dsh_ref[...] = jnp.zeros_like(dsh_ref)
            if below:
                dg_ref[...] = jnp.zeros_like(dg_ref)

        @pl.when(_first_step())
        def _():
            dw_ref[...] = jnp.zeros_like(dw_ref)

        def weight(k):
            return w_refs[k][...] if n_w == n_in else w_refs[0][starts[k]:starts[k] + dys[k].shape[2], :]

        dh = jnp.dot(dy_refs[0][...], weight(0), preferred_element_type=F32)
        for k in range(1, n_in):
            dh += jnp.dot(dy_refs[k][...], weight(k), preferred_element_type=F32)
        xv = x_ref[...]
        r = _rms(xv)
        n = xv * r
        nw = nw_ref[...]
        dsc_ref[...] += jnp.sum(dh * (n * nw), axis=0, keepdims=True)
        dsh_ref[...] += jnp.sum(dh, axis=0, keepdims=True)
        dhn = dh * (1.0 + sc_ref[...])
        dw_ref[...] += jnp.sum(dhn * n, axis=0, keepdims=True)
        dx = dxn_ref[...] + _rms_bwd(dhn * nw, n, r)
        dx_ref[...] = dx
        if below:
            _gate_bwd_tile(dx, below[2], o_ref, g_ref, do_ref, dg_ref)

    resident = lambda shape: pl.BlockSpec(shape, lambda *_: (0,) * len(shape), pipeline_mode=pl.Buffered(1))
    in_specs = [_row(tm, dy.shape[2]) for dy in dys] + [resident(wt.shape) for wt in wts]
    in_specs += [_row(tm, d), _row(tm, d), _full((1, d)), _bvec(d)] + extra_in
    return pl.pallas_call(
        body, name=name, grid=(b, s // tm), in_specs=in_specs,
        out_specs=[_row(tm, d), _bvec(d), _bvec(d), _full((1, d))] + extra_out,
        out_shape=[SDS((b, s, d), F32), SDS((b, 1, d), F32), SDS((b, 1, d), F32), SDS((1, d), F32)] + extra_shape,
        compiler_params=_cparams(2))(*dys, *wts, x, dxn, w, sc, *(below[:2] if below else ()))


def in_proj(h, win_t, name):
    b, s, d = h.shape
    tm = min(512, s)
    widths = (D_SSD, D_SSD + 2 * SSD_GROUPS * SSD_STATE, Q_LORA, KV_LORA, LANES)

    def body(h_ref, w_ref, *outs):
        p = lax.dot_general(h_ref[...], w_ref[...], NT_DIMS, preferred_element_type=F32)
        off = 0
        for o_ref, wd in zip(outs, widths):
            o_ref[...] = p[:, off:off + wd]
            off += wd

    return pl.pallas_call(
        body, name=name, grid=(b, s // tm),
        in_specs=[_row(tm, d), _full(win_t.shape)],
        out_specs=[_row(tm, wd) for wd in widths],
        out_shape=[SDS((b, s, wd), F32) for wd in widths], compiler_params=_cparams(2))(h, win_t)


def _halo_prev(ts, d):
    return pl.BlockSpec((None, 8, d), lambda b, i: (b, jnp.maximum(i * (ts // 8) - 1, 0), 0))


CONV_ROWS = 32


def _conv_head(head, u_ref, up_ref, tile):
    head[0:8, :] = jnp.where(tile > 0, up_ref[...], 0.0)
    head[8:8 + CONV_ROWS, :] = u_ref[0:CONV_ROWS, :]


def _conv_windows(u_ref, head, r0):
    if r0 == 0:
        return [head[5 + k:5 + k + CONV_ROWS, :] for k in range(4)]
    return [u_ref[r0 - 3 + k:r0 - 3 + k + CONV_ROWS, :] for k in range(4)]


def _fold8(t):
    acc = t[0:8, :]
    for r in range(8, CONV_ROWS, 8):
        acc += t[r:r + 8, :]
    return acc


def conv_fwd(u, cw, cb, name):
    b, s, dc = u.shape
    ts = min(512, s)
    widths = (D_SSD, SSD_GROUPS * SSD_STATE, SSD_GROUPS * SSD_STATE)

    def body(u_ref, up_ref, w_ref, b_ref, xs_ref, bm_ref, cm_ref, head):
        _conv_head(head, u_ref, up_ref, pl.program_id(1))
        ws = [w_ref[k:k + 1, :] for k in range(4)]
        bias = b_ref[...]
        for r0 in range(0, ts, CONV_ROWS):
            taps = _conv_windows(u_ref, head, r0)
            v = bias + taps[0] * ws[0] + taps[1] * ws[1] + taps[2] * ws[2] + taps[3] * ws[3]
            y = v * _sigmoid(v)
            rs = slice(r0, r0 + CONV_ROWS)
            xs_ref[rs, :] = y[:, 0:D_SSD]
            bm_ref[rs, :] = y[:, D_SSD:D_SSD + 256]
            cm_ref[rs, :] = y[:, D_SSD + 256:D_SSD + 512]

    return pl.pallas_call(
        body, name=name, grid=(b, s // ts),
        in_specs=[_row(ts, dc), _halo_prev(ts, dc), _full((4, dc)), _full((1, dc))],
        out_specs=[_row(ts, wd) for wd in widths],
        out_shape=[SDS((b, s, wd), F32) for wd in widths],
        scratch_shapes=[pltpu.VMEM((8 + CONV_ROWS, dc), F32)], compiler_params=_cparams(2))(u, u, cw, cb)


def conv_bwd(dxs, dbm, dcm, u, cw, cb, name):
    b, s, dc = u.shape
    ts = min(512, s)
    nt = s // ts

    def body(dxs_ref, dbm_ref, dcm_ref, u_ref, up_ref, w_ref, b_ref, du_ref, dwb_ref, head, dvs):
        @pl.when(_first_step())
        def _():
            dwb_ref[...] = jnp.zeros_like(dwb_ref)

        @pl.when(pl.program_id(1) == 0)
        def _():
            dvs[ts:ts + 8, :] = jnp.zeros((8, dc), F32)
        _conv_head(head, u_ref, up_ref, nt - 1 - pl.program_id(1))
        ws = [w_ref[k:k + 1, :] for k in range(4)]
        bias = b_ref[...]
        for r0 in range(0, ts, CONV_ROWS):
            taps = _conv_windows(u_ref, head, r0)
            v = bias + taps[0] * ws[0] + taps[1] * ws[1] + taps[2] * ws[2] + taps[3] * ws[3]
            sg = _sigmoid(v)
            rs = slice(r0, r0 + CONV_ROWS)
            dy = jnp.concatenate([dxs_ref[rs, :], dbm_ref[rs, :], dcm_ref[rs, :]], axis=1)
            dv = dy * (sg * (1.0 + v * (1.0 - sg)))
            dvs[rs, :] = dv
            for k in range(4):
                dwb_ref[8 * k:8 * k + 8, :] += _fold8(dv * taps[k])
            dwb_ref[32:40, :] += _fold8(dv)
        for r0 in range(0, ts, CONV_ROWS):
            win = [dvs[r0 + 3 - k:r0 + 3 - k + CONV_ROWS, :] for k in range(4)]
            acc = win[0] * ws[0] + win[1] * ws[1] + win[2] * ws[2] + win[3] * ws[3]
            du_ref[r0:r0 + CONV_ROWS, :] = acc.astype(BF16)
        dvs[ts:ts + 8, :] = dvs[0:8, :]

    rows = lambda wd: pl.BlockSpec((None, ts, wd), lambda bb, i: (bb, nt - 1 - i, 0))
    prev = pl.BlockSpec((None, 8, dc), lambda bb, i: (bb, jnp.maximum((nt - 1 - i) * (ts // 8) - 1, 0), 0))
    return pl.pallas_call(
        body, name=name, grid=(b, nt),
        in_specs=[rows(D_SSD), rows(256), rows(256), rows(dc), prev, _full((4, dc)), _full((1, dc))],
        out_specs=[rows(dc), _full((40, dc))],
        out_shape=[SDS((b, s, dc), BF16), SDS((40, dc), F32)],
        scratch_shapes=[pltpu.VMEM((8 + CONV_ROWS, dc), F32), pltpu.VMEM((ts + 8, dc), F32)],
        compiler_params=_cparams(2))(dxs, dbm, dcm, u, u, cw, cb)


def conv_grads_fold(x, name):
    c = x.shape[1]

    def body(x_ref, o_ref):
        o_ref[...] = jnp.zeros_like(o_ref)
        for k in range(5):
            o_ref[k:k + 1, :] = jnp.sum(x_ref[8 * k:8 * k + 8, :], axis=0, keepdims=True)

    return pl.pallas_call(body, name=name, out_shape=SDS((8, c), F32))(x)


def _ssd_common(misc_ref, dtb_ref, alog_ref, e_ref):
    ln = CHUNK
    lane = lax.broadcasted_iota(I32, (ln, LANES), 1)
    lane1 = lax.broadcasted_iota(I32, (1, LANES), 1)
    pre = misc_ref[...] + dtb_ref[...]
    dt_s = jnp.where(lane < SSD_HEADS, _softplus(pre), 0.0)
    a_neg = jnp.where(lane1 < SSD_HEADS, -jnp.exp(alog_ref[...]), 0.0)
    ri = lax.broadcasted_iota(I32, (ln, ln), 0)
    ci = lax.broadcasted_iota(I32, (ln, ln), 1)
    tril = ci <= ri
    acum = jnp.dot(tril.astype(F32), dt_s * a_neg, preferred_element_type=F32, precision=HI)
    both_e = _dot_01(jnp.concatenate([dt_s, acum], axis=0), e_ref[...], 3)
    dt_e, acum_e = both_e[0:ln], both_e[ln:2 * ln]
    return dict(pre=pre, dt_s=dt_s, a_neg=a_neg, tril=tril, ri=ri, ci=ci, acum=acum, acum_t=acum.T,
                dt_e=dt_e, eac_e=jnp.exp(acum_e), del_e=jnp.exp(acum_e[ln - 1:ln, :] - acum_e))


def _dot_01(x, m01, terms, dims=(((1,), (0,)), ((), ()))):
    acc, rest = None, x
    for k in range(terms):
        part = rest.astype(BF16)
        if k + 1 < terms:
            rest = rest - part.astype(F32)
        d = lax.dot_general(part, m01, dims, preferred_element_type=F32)
        acc = d if acc is None else acc + d
    return acc


def _decay(cm, h):
    seg = cm["acum"][:, h:h + 1] - cm["acum_t"][h:h + 1, :]
    return jnp.exp(jnp.where(cm["tril"], seg, -jnp.inf))


def ssd_fwd(xs, bm, cm_, misc, z, dtb, alog, dskip_e, norm_w, e_mat, name):
    b, s, _ = xs.shape
    ln, nc = CHUNK, s // CHUNK
    gw = D_SSD // SSD_GROUPS
    hpg = SSD_HEADS // SSD_GROUPS

    def body(xs_ref, b_ref, c_ref, misc_ref, z_ref, dtb_ref, alog_ref, dsk_ref, nw_ref, e_ref,
             ys_ref, y_ref, p_ref, st, yd):
        @pl.when(pl.program_id(1) == 0)
        def _():
            st[...] = jnp.zeros_like(st)
        cm = _ssd_common(misc_ref, dtb_ref, alog_ref, e_ref)
        xsv = xs_ref[...]
        xdt = xsv * cm["dt_e"]
        xdt_b = xdt.astype(BF16)
        xd_b = (xdt * cm["del_e"]).astype(BF16)
        gam_e = cm["eac_e"][ln - 1:ln, :]
        p_ref[...] = st[...]
        groups = [slice(gw * g, gw * (g + 1)) for g in range(SSD_GROUPS)]
        heads = [slice(SSD_HEAD_DIM * h, SSD_HEAD_DIM * (h + 1)) for h in range(SSD_HEADS)]
        bgs = [b_ref[:, SSD_STATE * g:SSD_STATE * (g + 1)].astype(BF16) for g in range(SSD_GROUPS)]
        cgs = [c_ref[:, SSD_STATE * g:SSD_STATE * (g + 1)].astype(BF16) for g in range(SSD_GROUPS)]
        cbs = [lax.dot_general(cg, bg, NT_DIMS, preferred_element_type=F32) for cg, bg in zip(cgs, bgs)]
        sts = [st[:, gs] for gs in groups]
        yoff = [jnp.dot(cg, st_g.astype(BF16), preferred_element_type=F32) * cm["eac_e"][:, gs]
                for cg, st_g, gs in zip(cgs, sts, groups)]
        news = [lax.dot_general(bg, xd_b[:, gs], TN_DIMS, preferred_element_type=F32) for bg, gs in zip(bgs, groups)]
        for gs, st_g, new in zip(groups, sts, news):
            st[:, gs] = st_g * gam_e[:, gs] + new
        ms = [(cbs[h // hpg] * _decay(cm, h)).astype(BF16) for h in range(SSD_HEADS)]
        for h, hs in enumerate(heads):
            yd[:, hs] = jnp.dot(ms[h], xdt_b[:, hs], preferred_element_type=F32)
        y = yd[...] + jnp.concatenate(yoff, axis=1) + dsk_ref[...] * xsv
        y_ref[...] = y
        zz = z_ref[...]
        yg = y * (zz * _sigmoid(zz))
        outs = []
        for g in range(SSD_GROUPS):
            ygg = yg[:, gw * g:gw * (g + 1)]
            outs.append(ygg * _rms(ygg) * nw_ref[:, gw * g:gw * (g + 1)])
        ys_ref[...] = jnp.concatenate(outs, axis=1).astype(BF16)

    row = lambda d: pl.BlockSpec((None, ln, d), lambda bb, c: (bb, c, 0))
    return pl.pallas_call(
        body, name=name, grid=(b, nc),
        in_specs=[row(D_SSD), row(256), row(256), row(LANES), row(D_SSD), _full((1, LANES)), _full((1, LANES)),
                  _full((1, D_SSD)), _full((1, D_SSD)), _full((LANES, D_SSD))],
        out_specs=[row(D_SSD), row(D_SSD), pl.BlockSpec((None, None, SSD_STATE, D_SSD), lambda bb, c: (bb, c, 0, 0))],
        out_shape=[SDS((b, s, D_SSD), BF16), SDS((b, s, D_SSD), F32), SDS((b, nc, SSD_STATE, D_SSD), F32)],
        scratch_shapes=[pltpu.VMEM((SSD_STATE, D_SSD), F32), pltpu.VMEM((ln, D_SSD), F32)],
        compiler_params=_cparams(2))(xs, bm, cm_, misc, z, dtb, alog, dskip_e, norm_w, e_mat)


def ssd_bwd(dys, y, z, xs, bm, cm_, misc, prev, dtb, alog, dskip_e, norm_w, e_mat, et_mat, name):
    b, s, _ = xs.shape
    ln, nc = CHUNK, s // CHUNK
    gw = D_SSD // SSD_GROUPS
    hpg = SSD_HEADS // SSD_GROUPS

    def body(dys_ref, y_ref, z_ref, xs_ref, b_ref, c_ref, misc_ref, p_ref, dtb_ref, alog_ref, dsk_ref, nw_ref,
             e_ref, et_ref, dxs_ref, db_ref, dc_ref, dz_ref, ddt_ref, dnw_ref, ddsk_ref, ddtb_ref, dalog_ref,
             dst, dxd, dac_t):
        @pl.when(_first_step())
        def _():
            for r_ in (dnw_ref, ddsk_ref, ddtb_ref, dalog_ref):
                r_[...] = jnp.zeros_like(r_)

        @pl.when(pl.program_id(1) == 0)
        def _():
            dst[...] = jnp.zeros_like(dst)

        cm = _ssd_common(misc_ref, dtb_ref, alog_ref, e_ref)
        et = et_ref[...]
        squeeze = lambda t: _dot_01(t, et, 2)
        lane = lax.broadcasted_iota(I32, (ln, LANES), 1)
        sub = lax.broadcasted_iota(I32, (LANES, ln), 0)
        xsv = xs_ref[...]
        xdt = xsv * cm["dt_e"]
        xdt_b = xdt.astype(BF16)
        xd_b = (xdt * cm["del_e"]).astype(BF16)
        eac_e = cm["eac_e"]
        gam_e = eac_e[ln - 1:ln, :]

        yv, zz, dyo = y_ref[...], z_ref[...], dys_ref[...]
        sz = _sigmoid(zz)
        silu_z = zz * sz
        yg = yv * silu_z
        dyg, dnw = [], []
        for g in range(SSD_GROUPS):
            gs = slice(gw * g, gw * (g + 1))
            ygg = yg[:, gs]
            r = _rms(ygg)
            n = ygg * r
            dnw.append(jnp.sum(dyo[:, gs] * n, axis=0, keepdims=True))
            dyg.append(_rms_bwd(dyo[:, gs] * nw_ref[:, gs], n, r))
        dyg = jnp.concatenate(dyg, axis=1)
        dnw_ref[...] += jnp.concatenate(dnw, axis=1)
        dz_ref[...] = (dyg * yv * (sz * (1.0 + zz * (1.0 - sz)))).astype(BF16)
        dy = dyg * silu_z
        ddsk_ref[...] += jnp.sum(dy * xsv, axis=0, keepdims=True)
        dy_b = dy.astype(BF16)

        dacum = jnp.zeros((ln, LANES), F32)
        dac_t[...] = jnp.zeros_like(dac_t)
        w1, dgam = [], []
        for g in range(SSD_GROUPS):
            gs = slice(gw * g, gw * (g + 1))
            ss = slice(SSD_STATE * g, SSD_STATE * (g + 1))
            bg = b_ref[:, ss].astype(BF16)
            cg = c_ref[:, ss].astype(BF16)
            cb = lax.dot_general(cg, bg, NT_DIMS, preferred_element_type=F32)
            pt = p_ref[:, gs]
            pt_b = pt.astype(BF16)
            dst_g = dst[:, gs]
            dst_b = dst_g.astype(BF16)
            edy = (dy[:, gs] * eac_e[:, gs]).astype(BF16)
            dcg = lax.dot_general(edy, pt_b, NT_DIMS, preferred_element_type=F32)
            dpt = lax.dot_general(cg, edy, TN_DIMS, preferred_element_type=F32)
            yoff = jnp.dot(cg, pt_b, preferred_element_type=F32) * eac_e[:, gs]
            dxd_g = jnp.dot(bg, dst_b, preferred_element_type=F32)
            dbg = lax.dot_general(xd_b[:, gs], dst_b, NT_DIMS, preferred_element_type=F32)
            ddel = dxd_g * xdt[:, gs] * cm["del_e"][:, gs]
            w1.append(dy[:, gs] * yoff - ddel)
            dgam.append(jnp.sum(ddel, axis=0, keepdims=True) + jnp.sum(dst_g * pt, axis=0, keepdims=True) * gam_e[:, gs])
            dxd[:, gs] = dxd_g * cm["del_e"][:, gs]
            dst[:, gs] = dst_g * gam_e[:, gs] + dpt
            dcb = jnp.zeros((ln, ln), F32)
            for j in range(hpg):
                h = hpg * g + j
                hs = slice(SSD_HEAD_DIM * h, SSD_HEAD_DIM * (h + 1))
                lam = _decay(cm, h)
                m = cb * lam
                dm = lax.dot_general(dy_b[:, hs], xdt_b[:, hs], NT_DIMS, preferred_element_type=F32)
                dxd[:, hs] += lax.dot_general(m.astype(BF16), dy_b[:, hs], TN_DIMS, preferred_element_type=F32)
                dcb += dm * lam
                wl = dm * m
                dacum += jnp.where(lane == h, jnp.sum(wl, axis=1, keepdims=True), 0.0)
                dac_t[...] -= jnp.where(sub == h, jnp.sum(wl, axis=0, keepdims=True), 0.0)
            dcb_b = dcb.astype(BF16)
            dc_ref[:, ss] = dcg + jnp.dot(dcb_b, bg, preferred_element_type=F32)
            db_ref[:, ss] = dbg + lax.dot_general(dcb_b, cg, TN_DIMS, preferred_element_type=F32)

        dxdt = dxd[...]
        dxs_ref[...] = dy * dsk_ref[...] + dxdt * cm["dt_e"]
        dacum += squeeze(jnp.concatenate(w1, axis=1)) + dac_t[...].T
        dlast = squeeze(jnp.broadcast_to(jnp.concatenate(dgam, axis=1), (8, D_SSD)))[0:1, :]
        dacum += jnp.where(lax.broadcasted_iota(I32, (ln, LANES), 0) == ln - 1, dlast, 0.0)
        triu = (cm["ci"] >= cm["ri"]).astype(F32)
        da = jnp.dot(triu, dacum, preferred_element_type=F32, precision=HI)
        ddt = da * cm["a_neg"] + squeeze(dxdt * xsv)
        dalog_ref[...] += jnp.sum(da * cm["dt_s"], axis=0, keepdims=True) * cm["a_neg"]
        ddt_raw = jnp.where(lane < SSD_HEADS, ddt * _sigmoid(cm["pre"]), 0.0)
        ddt_ref[...] = ddt_raw
        ddtb_ref[...] += jnp.sum(ddt_raw, axis=0, keepdims=True)

    row = lambda d: pl.BlockSpec((None, ln, d), lambda bb, c: (bb, nc - 1 - c, 0))
    return pl.pallas_call(
        body, name=name, grid=(b, nc),
        in_specs=[row(D_SSD), row(D_SSD), row(D_SSD), row(D_SSD), row(256), row(256), row(LANES),
                  pl.BlockSpec((None, None, SSD_STATE, D_SSD), lambda bb, c: (bb, nc - 1 - c, 0, 0)),
                  _full((1, LANES)), _full((1, LANES)), _full((1, D_SSD)), _full((1, D_SSD)),
                  _full((LANES, D_SSD)), _full((D_SSD, LANES))],
        out_specs=[row(D_SSD), row(256), row(256), row(D_SSD), row(LANES),
                   _full((1, D_SSD)), _full((1, D_SSD)), _full((1, LANES)), _full((1, LANES))],
        out_shape=[SDS((b, s, D_SSD), F32), SDS((b, s, 256), F32), SDS((b, s, 256), F32), SDS((b, s, D_SSD), BF16),
                   SDS((b, s, LANES), F32), SDS((1, D_SSD), F32), SDS((1, D_SSD), F32), SDS((1, LANES), F32),
                   SDS((1, LANES), F32)],
        scratch_shapes=[pltpu.VMEM((SSD_STATE, D_SSD), F32), pltpu.VMEM((ln, D_SSD), F32), pltpu.VMEM((LANES, ln), F32)],
        compiler_params=_cparams(2))(dys, y, z, xs, bm, cm_, misc, prev, dtb, alog, dskip_e, norm_w, e_mat, et_mat)


def _rope(xv, cc, sp, sm):
    n = xv.shape[1]
    return xv * cc + pltpu.roll(xv, 16, 1) * sp + pltpu.roll(xv, n - 16, 1) * sm


def _rope_bwd(dy, cc, sp, sm):
    n = dy.shape[1]
    return dy * cc + pltpu.roll(dy * sp, n - 16, 1) + pltpu.roll(dy * sm, 16, 1)


def _tile8(t):
    return jnp.concatenate([t] * MLA_HEADS, axis=1)


def qkv_fwd(cq, ckv, misc, cc, sp, sm, qnw, kvnw, wuq_t, wukv_t, place, name):
    b, s, _ = cq.shape
    tm = _att_tile(s)
    hd = MLA_HEADS * HEAD_PAD

    def body(cq_ref, ckv_ref, misc_ref, cc_ref, sp_ref, sm_ref, qnw_ref, kvnw_ref, wq_ref, wkv_ref, pl_ref,
             q_ref, k_ref, v_ref, vt_ref, qn_ref, kvn_ref):
        cqv, ckvv = cq_ref[...], ckv_ref[...]
        qn = (cqv * _rms(cqv) * qnw_ref[...]).astype(BF16)
        kvn = (ckvv * _rms(ckvv) * kvnw_ref[...]).astype(BF16)
        qn_ref[...] = qn
        kvn_ref[...] = kvn
        cc1, sp1, sm1 = cc_ref[...], sp_ref[...], sm_ref[...]
        q = lax.dot_general(qn, wq_ref[...], NT_DIMS, preferred_element_type=F32)
        q_ref[...] = _rope(q, _tile8(cc1), _tile8(sp1), _tile8(sm1)).astype(BF16)
        kv = lax.dot_general(kvn, wkv_ref[...], NT_DIMS, preferred_element_type=F32)
        kr = jnp.dot(misc_ref[...], pl_ref[...], preferred_element_type=F32, precision=HI)
        kr = _rope(kr, cc1, sp1, sm1)
        k_ref[...] = (kv[:, 0:hd] + _tile8(kr)).astype(BF16)
        v_ref[...] = kv[:, hd:2 * hd].astype(BF16)
        for h in range(MLA_HEADS):
            vt_ref[h] = kv[:, hd + HEAD_PAD * h:hd + HEAD_PAD * (h + 1)].T.astype(BF16)

    return pl.pallas_call(
        body, name=name, grid=(b, s // tm),
        in_specs=[_row(tm, Q_LORA), _row(tm, KV_LORA), _row(tm, LANES), _row(tm, LANES), _row(tm, LANES), _row(tm, LANES),
                  _full((1, Q_LORA)), _full((1, KV_LORA)), _full(wuq_t.shape), _full(wukv_t.shape), _full((LANES, LANES))],
        out_specs=[_row(tm, hd), _row(tm, hd), _row(tm, hd),
                   pl.BlockSpec((None, MLA_HEADS, None, HEAD_PAD, tm), lambda bb, i: (bb, 0, i, 0, 0)),
                   _row(tm, Q_LORA), _row(tm, KV_LORA)],
        out_shape=[SDS((b, s, hd), BF16)] * 3 + [SDS((b, MLA_HEADS, s // tm, HEAD_PAD, tm), BF16),
                                                 SDS((b, s, Q_LORA), BF16), SDS((b, s, KV_LORA), BF16)],
        compiler_params=_cparams(2))(cq, ckv, misc, cc, sp, sm, qnw, kvnw, wuq_t, wukv_t, place)


def qkv_bwd(dq, dk, dv, ddt, cq, ckv, cc, sp, sm, qnw, kvnw, wuq_t, wukv_t, place_t, name):
    b, s, _ = cq.shape
    tm = min(512, s)
    hd = MLA_HEADS * HEAD_PAD

    def body(dq_ref, dk_ref, dv_ref, ddt_ref, cq_ref, ckv_ref, cc_ref, sp_ref, sm_ref, qnw_ref, kvnw_ref,
             wq_ref, wkv_ref, plt_ref, dcq_ref, dckv_ref, dmisc_ref, dqp_ref, dkv_ref, dqnw_ref, dkvnw_ref):
        @pl.when(_first_step())
        def _():
            dqnw_ref[...] = jnp.zeros_like(dqnw_ref)
            dkvnw_ref[...] = jnp.zeros_like(dkvnw_ref)
        cc1, sp1, sm1 = cc_ref[...], sp_ref[...], sm_ref[...]
        dqp = _rope_bwd(dq_ref[...].astype(F32), _tile8(cc1), _tile8(sp1), _tile8(sm1)).astype(BF16)
        dqp_ref[...] = dqp
        dkv_b = jnp.concatenate([dk_ref[...], dv_ref[...]], axis=1)
        dkf = dk_ref[...].astype(F32)
        dkv_ref[...] = dkv_b
        dkr = dkf[:, 0:HEAD_PAD]
        for h in range(1, MLA_HEADS):
            dkr += dkf[:, HEAD_PAD * h:HEAD_PAD * (h + 1)]
        dkr = _rope_bwd(dkr, cc1, sp1, sm1)
        dmisc_ref[...] = (jnp.dot(dkr, plt_ref[...], preferred_element_type=F32, precision=HI) + ddt_ref[...]).astype(BF16)

        def norm_bwd(dn_w, xv, w_ref, dw_ref, dx_ref):
            r = _rms(xv)
            n = xv * r
            dw_ref[...] += jnp.sum(dn_w * n, axis=0, keepdims=True)
            dx_ref[...] = _rms_bwd(dn_w * w_ref[...], n, r).astype(BF16)

        norm_bwd(jnp.dot(dqp, wq_ref[...], preferred_element_type=F32), cq_ref[...], qnw_ref, dqnw_ref, dcq_ref)
        norm_bwd(jnp.dot(dkv_b, wkv_ref[...], preferred_element_type=F32), ckv_ref[...], kvnw_ref, dkvnw_ref, dckv_ref)

    return pl.pallas_call(
        body, name=name, grid=(b, s // tm),
        in_specs=[_row(tm, hd), _row(tm, hd), _row(tm, hd), _row(tm, LANES), _row(tm, Q_LORA), _row(tm, KV_LORA),
                  _row(tm, LANES), _row(tm, LANES), _row(tm, LANES), _full((1, Q_LORA)), _full((1, KV_LORA)),
                  _full(wuq_t.shape), _full(wukv_t.shape), _full((LANES, LANES))],
        out_specs=[_row(tm, Q_LORA), _row(tm, KV_LORA), _row(tm, LANES), _row(tm, hd), _row(tm, 2 * hd),
                   _full((1, Q_LORA)), _full((1, KV_LORA))],
        out_shape=[SDS((b, s, Q_LORA), BF16), SDS((b, s, KV_LORA), BF16), SDS((b, s, LANES), BF16),
                   SDS((b, s, hd), BF16), SDS((b, s, 2 * hd), BF16), SDS((1, Q_LORA), F32), SDS((1, KV_LORA), F32)],
        compiler_params=_cparams(2))(dq, dk, dv, ddt, cq, ckv, cc, sp, sm, qnw, kvnw, wuq_t, wukv_t, place_t)


ATT_SCALE = 1.0 / math.sqrt(QK_DIM)
LOG2E = math.log2(math.e)
ATT_SCALE_LOG2E = ATT_SCALE * LOG2E


ATT_HEADS_PER_STEP = 4
ATT_HEADS_PER_STEP_BWD = 2


def _att_tile(s):
    return min(512, s)


def flash_fwd(q, k, vt, name):
    b, s, hd = q.shape
    t = _att_tile(s)
    nb = s // t
    th = t // 2

    hps = ATT_HEADS_PER_STEP
    hw = hps * HEAD_PAD

    def body(q_ref, k_ref, vt_ref, o_ref, lse_ref, m_s, l_s, acc):
        i = pl.program_id(2)
        m_s[...] = jnp.full_like(m_s, -jnp.inf)
        l_s[...] = jnp.zeros_like(l_s)
        acc[...] = jnp.zeros_like(acc)

        def update(j, diagonal):
            chains = [(hh, half) for hh in range(hps) for half in range(2)]
            lanes = lambda hh: slice(HEAD_PAD * hh, HEAD_PAD * (hh + 1))
            cols = lambda half: slice(th * half, th * (half + 1))
            sts = {}
            nkeys = lambda half: th if diagonal and half == 0 else t
            for hh, half in chains:
                kr = pl.ds(pl.multiple_of(j * t, t), nkeys(half))
                st = lax.dot_general(k_ref[kr, lanes(hh)], q_ref[cols(half), lanes(hh)], NT_DIMS,
                                     preferred_element_type=F32)
                if diagonal:
                    row = lax.broadcasted_iota(I32, (nkeys(half), th), 0)
                    col = lax.broadcasted_iota(I32, (nkeys(half), th), 1) + th * half
                    st = jnp.where(row <= col, st, -jnp.inf)
                sts[hh, half] = st
            pts, alphas = {}, {}
            for hh, half in chains:
                st, cs = sts[hh, half], cols(half)
                m_prev = m_s[hh, :, cs]
                m_new = jnp.maximum(m_prev, jnp.max(st, axis=0, keepdims=True))
                alpha = jnp.exp2((m_prev - m_new) * ATT_SCALE_LOG2E)
                pt = jnp.exp2((st - m_new) * ATT_SCALE_LOG2E)
                l_s[hh, :, cs] = alpha * l_s[hh, :, cs] + jnp.sum(pt, axis=0, keepdims=True)
                m_s[hh, :, cs] = m_new
                pts[hh, half], alphas[hh, half] = pt.astype(BF16), alpha
            for hh, half in chains:
                cs = cols(half)
                acc[hh, :, cs] = alphas[hh, half] * acc[hh, :, cs] + jnp.dot(
                    vt_ref[hh, j, :, 0:nkeys(half)], pts[hh, half], preferred_element_type=F32)

        def step(j, carry):
            update(j, False)
            return carry

        lax.fori_loop(0, i, step, 0)
        update(i, True)
        for hh in range(hps):
            o_ref[:, HEAD_PAD * hh:HEAD_PAD * (hh + 1)] = (acc[hh] / l_s[hh]).T
            lse_ref[hh] = m_s[hh] * ATT_SCALE + jnp.log(l_s[hh])

    qs = pl.BlockSpec((None, t, hw), lambda bb, h, i: (bb, i, h))
    ks = pl.BlockSpec((None, s, hw), lambda bb, h, i: (bb, 0, h))
    vs = pl.BlockSpec((None, hps, nb, HEAD_PAD, t), lambda bb, h, i: (bb, h, 0, 0, 0))
    ls = pl.BlockSpec((None, hps, None, 1, t), lambda bb, h, i: (bb, h, i, 0, 0))
    return pl.pallas_call(
        body, name=name, grid=(b, MLA_HEADS // hps, nb),
        in_specs=[qs, ks, vs], out_specs=[qs, ls],
        out_shape=[SDS((b, s, hd), F32), SDS((b, MLA_HEADS, nb, 1, t), F32)],
        scratch_shapes=[pltpu.VMEM((hps, 1, t), F32), pltpu.VMEM((hps, 1, t), F32), pltpu.VMEM((hps, HEAD_PAD, t), F32)],
        compiler_params=_cparams(3))(q, k, vt)


def flash_bwd(q, k, v, do, lse, dlt, name):
    b, s, hd = q.shape
    t = _att_tile(s)
    nb = s // t
    th = t // 2
    lse_r = lse
    dlt_r = dlt.reshape(b, MLA_HEADS, nb, 1, t)

    hps = ATT_HEADS_PER_STEP_BWD
    hw = hps * HEAD_PAD

    def body(q_ref, k_ref, v_ref, do_ref, lse_ref, dlt_ref, dq_ref, dk_ref, dv_ref, dq_s, dk_s, dv_s):
        dq_s[...] = jnp.zeros_like(dq_s)
        dk_s[...] = jnp.zeros_like(dk_s)
        dv_s[...] = jnp.zeros_like(dv_s)

        def tile(j, i, diagonal):
            chains = [(hh, half) for hh in range(hps) for half in range(2)]
            lanes = lambda hh: slice(HEAD_PAD * hh, HEAD_PAD * (hh + 1))
            keys = lambda half: pl.ds(pl.multiple_of(j * t + th * half, th), th)
            q0 = lambda half: th if diagonal and half == 1 else 0
            qsel = lambda half: pl.ds(pl.multiple_of(i * t + q0(half), th), t - q0(half))
            sts, dpts = {}, {}
            for hh, half in chains:
                ls_, ks, qs, nq = lanes(hh), keys(half), qsel(half), t - q0(half)
                st = lax.dot_general(k_ref[ks, ls_], q_ref[qs, ls_], NT_DIMS, preferred_element_type=F32)
                if diagonal:
                    row = lax.broadcasted_iota(I32, (th, nq), 0) + th * half
                    col = lax.broadcasted_iota(I32, (th, nq), 1) + q0(half)
                    st = jnp.where(row <= col, st, -jnp.inf)
                sts[hh, half] = st
                dpts[hh, half] = lax.dot_general(v_ref[ks, ls_], do_ref[qs, ls_], NT_DIMS, preferred_element_type=F32)
            pts, dsts = {}, {}
            for hh, half in chains:
                qcols = slice(q0(half), t)
                pt = jnp.exp2(sts[hh, half] * ATT_SCALE_LOG2E - lse_ref[hh, i][:, qcols] * LOG2E)
                pts[hh, half] = pt.astype(BF16)
                dsts[hh, half] = (pt * (dpts[hh, half] - dlt_ref[hh, i][:, qcols])).astype(BF16)
            for hh, half in chains:
                ls_, ks, qs = lanes(hh), keys(half), qsel(half)
                dv_s[ks, ls_] += jnp.dot(pts[hh, half], do_ref[qs, ls_], preferred_element_type=F32)
                dk_s[ks, ls_] += jnp.dot(dsts[hh, half], q_ref[qs, ls_], preferred_element_type=F32)
                dq_s[qs, ls_] += lax.dot_general(dsts[hh, half], k_ref[ks, ls_], TN_DIMS, preferred_element_type=F32)

        def key_tile(j, carry):
            tile(j, j, True)

            def query_tile(i, c2):
                tile(j, i, False)
                return c2

            lax.fori_loop(j + 1, nb, query_tile, 0)
            return carry

        lax.fori_loop(0, nb, key_tile, 0)
        dq_ref[...] = (dq_s[...] * ATT_SCALE).astype(BF16)
        dk_ref[...] = (dk_s[...] * ATT_SCALE).astype(BF16)
        dv_ref[...] = dv_s[...].astype(BF16)

    hs = pl.BlockSpec((None, s, hw), lambda bb, h: (bb, 0, h))
    ls = pl.BlockSpec((None, hps, nb, 1, t), lambda bb, h: (bb, h, 0, 0, 0))
    return pl.pallas_call(
        body, name=name, grid=(b, MLA_HEADS // hps),
        in_specs=[hs, hs, hs, hs, ls, ls], out_specs=[hs, hs, hs],
        out_shape=[SDS((b, s, hd), BF16)] * 3, scratch_shapes=[pltpu.VMEM((s, hw), F32)] * 3,
        compiler_params=_cparams(2))(q, k, v, do, lse_r, dlt_r)


def out_proj(ys, attn, mnw, wo, x, gate, above, name):
    b, s, d = x.shape
    tm = min(512, s)

    def body(ys_ref, at_ref, mnw_ref, wo_ref, x_ref, g_ref, nw_ref, sc_ref, sh_ref, xn_ref, o_ref, ym_ref, h_ref):
        av = at_ref[...]
        ym = (av * _rms(av) * mnw_ref[...]).astype(BF16)
        ym_ref[...] = ym
        o = jnp.dot(ys_ref[...], wo_ref[0:D_SSD, :], preferred_element_type=F32)
        o += jnp.dot(ym, wo_ref[D_SSD:2 * D_SSD, :], preferred_element_type=F32)
        xn = x_ref[...] + g_ref[...] * o
        xn_ref[...] = xn
        o_ref[...] = o.astype(BF16)
        h_ref[...] = _norm_mod_tile(xn, nw_ref, sc_ref, sh_ref)

    return pl.pallas_call(
        body, name=name, grid=(b, s // tm),
        in_specs=[_row(tm, D_SSD), _row(tm, D_SSD), _full((1, D_SSD)), _full(wo.shape), _row(tm, d), _bvec(d),
                  _full((1, d)), _bvec(d), _bvec(d)],
        out_specs=[_row(tm, d), _row(tm, d), _row(tm, D_SSD), _row(tm, d)],
        out_shape=[SDS((b, s, d), F32), SDS((b, s, d), BF16), SDS((b, s, D_SSD), BF16), SDS((b, s, d), BF16)],
        compiler_params=_cparams(2))(ys, attn, mnw, wo, x, gate, *above)


def out_proj_bwd(dout, attn, mnw, wo, name):
    b, s, d = dout.shape
    tm = min(512, s)

    def body(do_ref, at_ref, mnw_ref, wo_ref, dys_ref, dat_ref, dlt_ref, dw_ref):
        lane = lax.broadcasted_iota(I32, (tm, LANES), 1)
        @pl.when(_first_step())
        def _():
            dw_ref[...] = jnp.zeros_like(dw_ref)
        dov = do_ref[...]
        dys_ref[...] = lax.dot_general(dov, wo_ref[0:D_SSD, :], NT_DIMS, preferred_element_type=F32)
        dym = lax.dot_general(dov, wo_ref[D_SSD:2 * D_SSD, :], NT_DIMS, preferred_element_type=F32)
        av = at_ref[...]
        r = _rms(av)
        n = av * r
        dw_ref[...] += jnp.sum(dym * n, axis=0, keepdims=True)
        dat = _rms_bwd(dym * mnw_ref[...], n, r)
        dat_ref[...] = dat.astype(BF16)
        prod = dat * av
        cols = jnp.zeros((tm, LANES), F32)
        for h in range(MLA_HEADS):
            cols += jnp.where(lane == h, jnp.sum(prod[:, HEAD_PAD * h:HEAD_PAD * (h + 1)], axis=1, keepdims=True), 0.0)
        dlt_ref[...] = cols.T[0:MLA_HEADS, :]

    return pl.pallas_call(
        body, name=name, grid=(b, s // tm),
        in_specs=[_row(tm, d), _row(tm, D_SSD), _full((1, D_SSD)), _full(wo.shape)],
        out_specs=[_row(tm, D_SSD), _row(tm, D_SSD),
                   pl.BlockSpec((None, MLA_HEADS, tm), lambda bb, i: (bb, 0, i)), _full((1, D_SSD))],
        out_shape=[SDS((b, s, D_SSD), F32), SDS((b, s, D_SSD), BF16), SDS((b, MLA_HEADS, s), F32),
                   SDS((1, D_SSD), F32)],
        compiler_params=_cparams(2))(dout, attn, mnw, wo)


def adaln_fwd(c_all, w_ada, b_ada, name):
    nb, d = c_all.shape
    n = w_ada.shape[1]

    def body(c_ref, w_ref, b_ref, m_ref, ca_ref):
        cv = c_ref[...]
        ca = (cv * _sigmoid(cv)).astype(BF16)
        ca_ref[...] = ca
        m_ref[...] = jnp.dot(ca, w_ref[...].astype(BF16), preferred_element_type=F32) + b_ref[...]

    return pl.pallas_call(
        body, name=name, out_shape=[SDS((nb, n), F32), SDS((nb, d), BF16)],
        compiler_params=pltpu.CompilerParams(vmem_limit_bytes=VMEM_LIMIT))(c_all, w_ada, b_ada)


def adaln_bwd(c_act, dmod_cols, name):
    d, n = c_act.shape[1], dmod_cols.shape[1]

    def body(c_ref, dm_ref, gw_ref):
        gw_ref[...] = lax.dot_general(c_ref[...], dm_ref[...].astype(BF16), TN_DIMS, preferred_element_type=F32)

    return pl.pallas_call(
        body, name=name, out_shape=SDS((d, n), F32),
        compiler_params=pltpu.CompilerParams(vmem_limit_bytes=VMEM_LIMIT))(c_act, dmod_cols)


def sum_rows(x, name):
    def body(x_ref, o_ref):
        o_ref[...] = jnp.sum(x_ref[...], axis=0, keepdims=True)
    return pl.pallas_call(body, name=name, out_shape=SDS((1, x.shape[1]), F32))(x)


def squeeze_heads(x, et_mat, name):
    def body(x_ref, et_ref, o_ref):
        xv = jnp.broadcast_to(x_ref[...], (8, x.shape[1]))
        o_ref[...] = _dot_01(xv, et_ref[...], 3)[0:1, :]
    return pl.pallas_call(body, name=name, out_shape=SDS((1, LANES), F32))(x, et_mat)


def sum_blocks(x, name):
    n, r, c = x.shape
    tr = next(cand for cand in (256, 128, 64, 32, 16, 8) if r % cand == 0)

    def body(x_ref, o_ref):
        acc = x_ref[0].astype(F32)
        for k in range(1, n):
            acc += x_ref[k].astype(F32)
        o_ref[...] = acc

    return pl.pallas_call(
        body, name=name, grid=(r // tr,), in_specs=[pl.BlockSpec((n, tr, c), lambda i: (0, i, 0))],
        out_specs=pl.BlockSpec((tr, c), lambda i: (i, 0)), out_shape=SDS((r, c), F32),
        compiler_params=_cparams(1))(x)


def _adam_math(w, g, m, v):
    m = ADAM_B1 * m + (1.0 - ADAM_B1) * g
    v = ADAM_B2 * v + (1.0 - ADAM_B2) * (g * g)
    m_hat = m / (1.0 - ADAM_B1 ** ADAM_STEP)
    v_hat = v / (1.0 - ADAM_B2 ** ADAM_STEP)
    return -ADAM_LR * (m_hat / (jnp.sqrt(v_hat) + ADAM_EPS) + ADAM_WD * w), m, v


def adamw(w, g, m, v, name):
    r, c = w.shape[-2:]
    tr = r
    for cand in (512, 256, 128, 64, 32, 16, 8):
        if r % cand == 0 and cand * c * 4 <= 2 * 1024 * 1024:
            tr = cand
            break

    def body(w_ref, g_ref, m_ref, v_ref, d_ref, mo_ref, vo_ref):
        d_ref[...], mo_ref[...], vo_ref[...] = _adam_math(w_ref[...], g_ref[...], m_ref[...], v_ref[...])

    def spec(a):
        return pl.BlockSpec((tr, c), lambda i: (i, 0)) if a.ndim == 2 else pl.BlockSpec((None, tr, c), lambda i: (0, i, 0))

    return pl.pallas_call(
        body, name=name, grid=(r // tr,), in_specs=[spec(w), spec(g), spec(m), spec(v)], out_specs=[spec(w)] * 3,
        out_shape=[SDS(w.shape, F32)] * 3, compiler_params=_cparams(1))(w, g, m, v)


def adamw_blocks(w, blocks, m, v, name):
    r, c = w.shape
    tr = next((cand for cand in (128, 64, 32, 16, 8) if r % cand == 0), r)

    def body(w_ref, b_ref, m_ref, v_ref, g_ref, d_ref, mo_ref, vo_ref):
        g = b_ref[0].astype(F32)
        for k in range(1, N_DEV):
            g += b_ref[k].astype(F32)
        g_ref[...] = g
        d_ref[...], mo_ref[...], vo_ref[...] = _adam_math(w_ref[...], g, m_ref[...], v_ref[...])

    spec = pl.BlockSpec((tr, c), lambda i: (i, 0))
    return pl.pallas_call(
        body, name=name, grid=(r // tr,),
        in_specs=[spec, pl.BlockSpec((N_DEV, tr, c), lambda i: (0, i, 0)), spec, spec], out_specs=[spec] * 4,
        out_shape=[SDS((r, c), F32)] * 4, compiler_params=_cparams(1))(w, blocks, m, v)


def adamw_many(ws, gs, ms, vs, name):
    n = len(ws)

    def body(*refs):
        w_r, g_r, m_r, v_r = (refs[k * n:(k + 1) * n] for k in range(4))
        d_r, mo_r, vo_r = (refs[(4 + k) * n:(5 + k) * n] for k in range(3))
        for k in range(n):
            d_r[k][...], mo_r[k][...], vo_r[k][...] = _adam_math(w_r[k][...], g_r[k][...], m_r[k][...], v_r[k][...])

    shapes = [SDS(w.shape, F32) for w in ws]
    outs = pl.pallas_call(body, name=name, out_shape=shapes * 3)(*ws, *gs, *ms, *vs)
    return outs[:n], outs[n:2 * n], outs[2 * n:]


TRANSPOSED = ("ffn1_w_gate", "ffn1_w_up", "ffn2_w_gate", "ffn2_w_up", "w_in", "w_ukv", "w_uq")
GATHER_GROUPS = (("ffn1_w_gate", "ffn1_w_up"), ("ffn1_w_down",),
                 ("w_in", "w_ukv", "w_uq", "w_out", "ffn2_w_gate", "ffn2_w_up", "ffn2_w_down"))
GRAD_GROUPS = (("ffn2", ("ffn2_w_gate", "ffn2_w_up", "ffn2_w_down")), ("mixer", ("w_out", "w_in", "w_ukv", "w_uq")),
               ("ffn1_down", ("ffn1_w_down",)), ("ffn1_gate", ("ffn1_w_gate",)), ("ffn1_up", ("ffn1_w_up",)))


def _shard_view(name, w):
    return w[0].T if name in TRANSPOSED else w[0]


def _shard_unview(name, t):
    return t.T[None] if name in TRANSPOSED else t[None]


def _grad_blocks(name, gw):
    if name == "w_in":
        return _in_proj_rows_inv(gw).reshape(N_DEV, -1, D_MODEL)
    if name == "w_ukv":
        hd = MLA_HEADS * HEAD_PAD
        return jnp.concatenate([gw[:hd].reshape(MLA_HEADS, HEAD_PAD, KV_LORA)[:, :QK_NOPE],
                                gw[hd:].reshape(MLA_HEADS, V_HEAD, KV_LORA)], axis=1)
    if name == "w_uq":
        return gw.reshape(MLA_HEADS, HEAD_PAD, Q_LORA)[:, :QK_DIM]
    return gw.reshape(N_DEV, -1, D_MODEL)


def _pack_rows(arrs):
    parts = []
    for a in arrs:
        flat = a.reshape(-1).astype(F32)
        pad = (-flat.shape[0]) % D_MODEL
        if pad:
            flat = jnp.pad(flat, (0, pad))
        parts.append(flat.reshape(-1, D_MODEL))
    out = jnp.concatenate(parts, axis=0)
    pad = (-out.shape[0]) % 8
    if pad:
        out = jnp.pad(out, ((0, pad), (0, 0)))
    return out


def _unpack_rows(packed, shapes):
    out, row = [], 0
    for shp in shapes:
        n = math.prod(shp)
        nrow = -(-n // D_MODEL)
        out.append(packed[row:row + nrow].reshape(-1)[:n].reshape(shp))
        row += nrow
    return out


def _in_proj_rows(w_t):
    return jnp.concatenate([w_t[0:2560], w_t[2576:2960], w_t[2960:3216], w_t[2560:2576], w_t[3216:3248],
                            jnp.zeros((D_IN_PAD - D_IN, D_MODEL), w_t.dtype)], axis=0)


def _in_proj_rows_inv(d):
    return jnp.concatenate([d[0:2560], d[3200:3216], d[2560:2944], d[2944:3200], d[3216:3248]], axis=0)


def _rope_tables(positions):
    half = QK_ROPE // 2
    inv_freq = ROPE_THETA ** (-jnp.arange(0, QK_ROPE, 2, dtype=F32) / QK_ROPE)
    ang_t = positions[:, None, :].astype(F32) * inv_freq[:, None]
    cos_t, sin_t = jnp.cos(ang_t), jnp.sin(ang_t)
    b, _, s = ang_t.shape
    ts = min(512, s)

    def body(c_ref, s_ref, cc_ref, sp_ref, sm_ref):
        row = lax.broadcasted_iota(I32, (half, LANES), 0)
        lane = lax.broadcasted_iota(I32, (half, LANES), 1)
        first, second = lane == QK_NOPE + row, lane == QK_NOPE + half + row

        spread = lambda x, where: _dot_01(x, where.astype(BF16), 3, TN_DIMS)
        lane1 = lax.broadcasted_iota(I32, (1, LANES), 1)
        ones = jnp.where((lane1 < QK_NOPE) | (lane1 >= QK_NOPE + QK_ROPE), 1.0, 0.0)
        cc_ref[...] = spread(c_ref[...], first | second) + ones
        sp_ref[...] = spread(s_ref[...], second)
        sm_ref[...] = -spread(s_ref[...], first)

    src = pl.BlockSpec((None, half, ts), lambda bb, i: (bb, 0, i))
    return pl.pallas_call(
        body, name="rope_tables", grid=(b, s // ts), in_specs=[src, src], out_specs=[_row(ts, LANES)] * 3,
        out_shape=[SDS((b, s, LANES), F32)] * 3, compiler_params=_cparams(2))(cos_t, sin_t)


def weight_views(gathered):
    full = lambda name: gathered[name].reshape(-1, gathered[name].shape[2])
    ukv = full("w_ukv").reshape(MLA_HEADS, QK_NOPE + V_HEAD, KV_LORA)
    wukv_t = jnp.concatenate([jnp.pad(ukv[:, :QK_NOPE], ((0, 0), (0, HEAD_PAD - QK_NOPE), (0, 0))).reshape(-1, KV_LORA),
                              ukv[:, QK_NOPE:].reshape(-1, KV_LORA)], axis=0)
    uq = full("w_uq").reshape(MLA_HEADS, QK_DIM, Q_LORA)
    wuq_t = jnp.pad(uq, ((0, 0), (0, HEAD_PAD - QK_DIM), (0, 0))).reshape(-1, Q_LORA)
    return dict(wg1_t=full("ffn1_w_gate"), wu1_t=full("ffn1_w_up"), wd1=full("ffn1_w_down"),
                wg2_t=full("ffn2_w_gate"), wu2_t=full("ffn2_w_up"), wd2=full("ffn2_w_down"),
                wo=full("w_out"), win_t=_in_proj_rows(full("w_in")), wukv_t=wukv_t, wuq_t=wuq_t)


def _ffn_bwd(tag, dxn, do, dgate, x, h, gg, uu, a, sc, norm_w, wg_t, wu_t, wd, below):
    f2 = wd.shape[0] // 2
    dwd = mm_tn(a, do, f2, D_MODEL, tag + "_dwd")
    dgg, duu = ffn_dact(do, wd, gg, uu, tag + "_dact")
    dwg_t = mm_tn(dgg, h, f2, D_MODEL, tag + "_dwg")
    dwu_t = mm_tn(duu, h, f2, D_MODEL, tag + "_dwu")
    dx, dsc, dsh, dnw, *nxt = dh_norm_bwd([dgg, duu], [wg_t, wu_t], x, dxn, norm_w, sc, tag + "_dh", below)
    return dx, (dsh, dsc, dgate), dnw, (dwg_t, dwu_t, dwd), nxt


def local_step(x, tgt, positions, mod, wv, p):
    nb, s, d = x.shape
    sh1, sc1, g1, sh2, sc2, g2, sh3, sc3, g3 = mod
    cc, sp, sm = _rope_tables(positions)
    lane_head = jnp.arange(D_SSD, dtype=I32)[None, :] // SSD_HEAD_DIM
    e_mat = (lane_head == jnp.arange(LANES, dtype=I32)[:, None]).astype(BF16)
    et_mat = e_mat.T
    rr, cl = jnp.arange(LANES, dtype=I32)[:, None], jnp.arange(LANES, dtype=I32)[None, :]
    place = ((cl == rr + (QK_NOPE - SSD_HEADS)) & (rr >= SSD_HEADS) & (rr < SSD_HEADS + QK_ROPE)).astype(F32)
    dtb = jnp.pad(p["dt_bias"], ((0, 0), (0, LANES - SSD_HEADS)))
    alog = jnp.pad(p["a_log"], ((0, 0), (0, LANES - SSD_HEADS)))
    dskip_e = jnp.repeat(p["d_skip"], SSD_HEAD_DIM, axis=1)

    h1 = norm_mod(x, p["norm_ffn1"], sc1, sh1, "ffn1_norm")
    gg1, uu1, a1 = ffn_up(h1, wv["wg1_t"], wv["wu1_t"], "ffn1_up")
    x1, o1, h2 = ffn_down(a1, wv["wd1"], x, g1, 0.5, "ffn1_down", (p["norm_mix"], sc2, sh2))
    z, u, cq, ckv, misc = in_proj(h2, wv["win_t"], "in_proj")
    xs, bm, cm_ = conv_fwd(u, p["conv_w"], p["conv_b"], "conv_fwd")
    ys, y, prev = ssd_fwd(xs, bm, cm_, misc, z, dtb, alog, dskip_e, p["ssd_norm_w"], e_mat, "ssd_fwd")
    q, k, v, vt, qn, kvn = qkv_fwd(cq, ckv, misc, cc, sp, sm, p["q_norm_w"], p["kv_norm_w"], wv["wuq_t"], wv["wukv_t"],
                               place, "qkv_fwd")
    attn, lse = flash_fwd(q, k, vt, "flash_fwd")
    x2, o2, ym, h3 = out_proj(ys, attn, p["mla_norm_w"], wv["wo"], x1, g2, (p["norm_ffn2"], sc3, sh3), "out_proj")
    gg3, uu3, a3 = ffn_up(h3, wv["wg2_t"], wv["wu2_t"], "ffn2_up")
    loss, dx3, dnfin, do3, dg3 = ffn_down_final(a3, wv["wd2"], x2, g3, 0.5, p["norm_final"], tgt, "ffn2_down_loss")

    dx2, dmod3, dnf2, (dwg2, dwu2, dwd2), (dout, dg2) = _ffn_bwd(
        "ffn2", dx3, do3, dg3, x2, h3, gg3, uu3, a3, sc3, p["norm_ffn2"], wv["wg2_t"], wv["wu2_t"], wv["wd2"],
        (o2, g2, 1.0))
    dys, dattn, dlt, dmlan = out_proj_bwd(dout, attn, p["mla_norm_w"], wv["wo"], "out_proj_bwd")
    dwo = jnp.concatenate([mm_tn(ys, dout, D_SSD, D_MODEL, "dwo_ssd"), mm_tn(ym, dout, D_SSD, D_MODEL, "dwo_mla")], axis=0)
    dxs, dbm, dcm, dz, ddt, dssdn, ddsk_lane, ddtb, dalog = ssd_bwd(
        dys, y, z, xs, bm, cm_, misc, prev, dtb, alog, dskip_e, p["ssd_norm_w"], e_mat, et_mat, "ssd_bwd")
    dq, dk, dv = flash_bwd(q, k, v, dattn, lse, dlt, "flash_bwd")
    dcq, dckv, dmisc, dqp, dkvc, dqn, dkvn = qkv_bwd(dq, dk, dv, ddt, cq, ckv, cc, sp, sm, p["q_norm_w"], p["kv_norm_w"],
                                                     wv["wuq_t"], wv["wukv_t"], place.T, "qkv_bwd")
    dwuq = mm_tn(dqp, qn, MLA_HEADS * HEAD_PAD, Q_LORA, "dwuq")
    dwukv = mm_tn(dkvc, kvn, MLA_HEADS * HEAD_PAD, KV_LORA, "dwukv")
    du, dconv = conv_bwd(dxs, dbm, dcm, u, p["conv_w"], p["conv_b"], "conv_bwd")
    dconv = conv_grads_fold(dconv, "conv_grads_fold")
    dproj = [dz, du, dcq, dckv, dmisc]
    dwin = mm_tn_blocks(dproj, h2, "dwin")
    dx1, dsc2, dsh2, dnmix, do1, dg1 = dh_norm_bwd(dproj, [wv["win_t"]], x1, dx2, p["norm_mix"], sc2, "mix_dh",
                                                   (o1, g1, 0.5))
    dx0, dmod1, dnf1, (dwg1, dwu1, dwd1), _ = _ffn_bwd(
        "ffn1", dx1, do1, dg1, x, h1, gg1, uu1, a1, sc1, p["norm_ffn1"], wv["wg1_t"], wv["wu1_t"], wv["wd1"], None)

    dmod = jnp.concatenate([*dmod1, dsh2, dsc2, dg2, *dmod3], axis=1).reshape(nb, N_MOD * d)
    return dict(
        loss=loss, dx=dx0, dmod=dmod, norm_ffn1=dnf1, norm_mix=dnmix, norm_ffn2=dnf2, norm_final=dnfin,
        ssd_norm_w=dssdn, mla_norm_w=dmlan, q_norm_w=dqn, kv_norm_w=dkvn,
        dt_bias=ddtb[:, :SSD_HEADS], a_log=dalog[:, :SSD_HEADS],
        d_skip=squeeze_heads(ddsk_lane, et_mat, "d_skip_heads")[:, :SSD_HEADS],
        conv_b=dconv[4:5], conv_w=dconv[0:4],
        gw=dict(ffn1_w_gate=dwg1, ffn1_w_up=dwu1, ffn1_w_down=dwd1, ffn2_w_gate=dwg2, ffn2_w_up=dwu2, ffn2_w_down=dwd2,
                w_out=dwo, w_in=dwin, w_ukv=dwukv, w_uq=dwuq))


def kernel(x, c, positions, w_ada, b_ada, norm_ffn1, ffn1_w_gate, ffn1_w_up, ffn1_w_down, norm_mix, w_in, conv_w, conv_b, dt_bias, a_log, d_skip, ssd_norm_w, q_norm_w, w_uq, kv_norm_w, w_ukv, mla_norm_w, w_out, norm_ffn2, ffn2_w_gate, ffn2_w_up, ffn2_w_down, norm_final, loss_target, m_w_ada, m_b_ada, m_norm_ffn1, m_ffn1_w_gate, m_ffn1_w_up, m_ffn1_w_down, m_norm_mix, m_w_in, m_conv_w, m_conv_b, m_dt_bias, m_a_log, m_d_skip, m_ssd_norm_w, m_q_norm_w, m_w_uq, m_kv_norm_w, m_w_ukv, m_mla_norm_w, m_w_out, m_norm_ffn2, m_ffn2_w_gate, m_ffn2_w_up, m_ffn2_w_down, m_norm_final, v_w_ada, v_b_ada, v_norm_ffn1, v_ffn1_w_gate, v_ffn1_w_up, v_ffn1_w_down, v_norm_mix, v_w_in, v_conv_w, v_conv_b, v_dt_bias, v_a_log, v_d_skip, v_ssd_norm_w, v_q_norm_w, v_w_uq, v_kv_norm_w, v_w_ukv, v_mla_norm_w, v_w_out, v_norm_ffn2, v_ffn2_w_gate, v_ffn2_w_up, v_ffn2_w_down, v_norm_final):
    names = ["w_ada", "b_ada", "norm_ffn1", "ffn1_w_gate", "ffn1_w_up", "ffn1_w_down", "norm_mix", "w_in", "conv_w",
             "conv_b", "dt_bias", "a_log", "d_skip", "ssd_norm_w", "q_norm_w", "w_uq", "kv_norm_w", "w_ukv",
             "mla_norm_w", "w_out", "norm_ffn2", "ffn2_w_gate", "ffn2_w_up", "ffn2_w_down", "norm_final"]
    W = dict(zip(names, (w_ada, b_ada, norm_ffn1, ffn1_w_gate, ffn1_w_up, ffn1_w_down, norm_mix, w_in, conv_w, conv_b, dt_bias, a_log, d_skip, ssd_norm_w, q_norm_w, w_uq, kv_norm_w, w_ukv, mla_norm_w, w_out, norm_ffn2, ffn2_w_gate, ffn2_w_up, ffn2_w_down, norm_final)))
    M = dict(zip(names, (m_w_ada, m_b_ada, m_norm_ffn1, m_ffn1_w_gate, m_ffn1_w_up, m_ffn1_w_down, m_norm_mix, m_w_in, m_conv_w, m_conv_b, m_dt_bias, m_a_log, m_d_skip, m_ssd_norm_w, m_q_norm_w, m_w_uq, m_kv_norm_w, m_w_ukv, m_mla_norm_w, m_w_out, m_norm_ffn2, m_ffn2_w_gate, m_ffn2_w_up, m_ffn2_w_down, m_norm_final)))
    V = dict(zip(names, (v_w_ada, v_b_ada, v_norm_ffn1, v_ffn1_w_gate, v_ffn1_w_up, v_ffn1_w_down, v_norm_mix, v_w_in, v_conv_w, v_conv_b, v_dt_bias, v_a_log, v_d_skip, v_ssd_norm_w, v_q_norm_w, v_w_uq, v_kv_norm_w, v_w_ukv, v_mla_norm_w, v_w_out, v_norm_ffn2, v_ffn2_w_gate, v_ffn2_w_up, v_ffn2_w_down, v_norm_final)))

    nb, s, d = x.shape
    me = 4 * lax.axis_index("x") + 2 * lax.axis_index("y") + lax.axis_index("c")
    n_ada = w_ada.shape[2]

    taps, n_cw = conv_w.shape[1:]
    (cg,) = all_gather8([_pack_rows([c, conv_w[0]])], "gather_c")
    c_all = cg[:, 0:nb].reshape(N_DEV * nb, d)
    conv_w_full = cg[:, nb, 0:taps * n_cw].reshape(N_DEV, taps, n_cw).transpose(1, 0, 2).reshape(taps, N_DEV * n_cw)
    shards = [[_shard_view(name, W[name]).astype(BF16) for name in group] for group in GATHER_GROUPS]
    gathered = dict(zip(GATHER_GROUPS[0], all_gather8(shards[0], "gather_w_ffn1")))

    b_ada_cols = lax.dynamic_slice(b_ada, (0, me * n_ada), (1, n_ada))
    mod_cols, c_act = adaln_fwd(c_all, w_ada[0], b_ada_cols, "adaln_fwd")
    (mod_g,) = all_gather8([mod_cols], "gather_mod")
    gathered, mod_g, shards = lax.optimization_barrier((gathered, mod_g, shards))
    gathered.update(zip(GATHER_GROUPS[1], sc_all_gather8(shards[1], "gather_w_ffn1_down", 1)))
    gathered.update(zip(GATHER_GROUPS[2], sc_all_gather8(shards[2], "gather_w_rest", 7)))
    wv = weight_views(gathered)
    mod = lax.dynamic_slice(mod_g, (0, me * nb, 0), (N_DEV, nb, n_ada)).transpose(1, 0, 2).reshape(nb, N_MOD, 1, d)
    mod = [mod[:, k] for k in range(N_MOD)]

    P = dict(W)
    P["conv_w"] = conv_w_full
    P["norm_final"] = norm_final.reshape(1, d)
    R = local_step(x, loss_target, positions, mod, wv, P)

    dmod = R["dmod"]
    partial_shapes = [(1,), (1, d), (1, d), (1, d), (1, d), (1, d), (1, d), (1, Q_LORA), (1, KV_LORA),
                      (1, SSD_HEADS), (1, SSD_HEADS), (1, SSD_HEADS), (1, D_CONV), (4, D_CONV), (1, N_MOD * d),
                      (nb, N_MOD * d)]
    partial = _pack_rows([R["loss"][0, :1], R["norm_ffn1"], R["norm_mix"], R["norm_ffn2"], R["norm_final"],
                          R["ssd_norm_w"], R["mla_norm_w"], R["q_norm_w"], R["kv_norm_w"],
                          R["dt_bias"], R["a_log"], R["d_skip"], R["conv_b"], R["conv_w"],
                          sum_rows(dmod, "dmod_rows"), dmod])
    (partial_g,) = all_gather8([partial], "gather_partials")
    (loss, g_nf1, g_nmix, g_nf2, g_nfin, g_ssdn, g_mlan, g_qn, g_kvn, g_dtb, g_alog, g_dskip, g_convb, g_convw,
     g_bada, _) = _unpack_rows(sum_blocks(partial_g, "sum_partials"), partial_shapes)
    dmod_row = sum(-(-math.prod(shp) // D_MODEL) for shp in partial_shapes[:-1])
    dmod_all = partial_g[:, dmod_row:dmod_row + nb * N_MOD].reshape(N_DEV * nb, N_MOD * d)
    g_wada = adaln_bwd(c_act, lax.dynamic_slice(dmod_all, (0, me * n_ada), (N_DEV * nb, n_ada)), "adaln_bwd")
    n_cw = conv_w.shape[2]
    G = {"w_ada": g_wada[None], "b_ada": g_bada, "norm_ffn1": g_nf1, "norm_mix": g_nmix, "norm_ffn2": g_nf2,
         "norm_final": g_nfin.reshape(d), "ssd_norm_w": g_ssdn, "mla_norm_w": g_mlan, "q_norm_w": g_qn,
         "kv_norm_w": g_kvn, "dt_bias": g_dtb, "a_log": g_alog, "d_skip": g_dskip, "conv_b": g_convb,
         "conv_w": lax.dynamic_slice(g_convw, (0, me * n_cw), (4, n_cw))[None]}

    DW, NM, NV = {}, {}, {}
    gw = R["gw"]
    for k, (tag, group) in enumerate(GRAD_GROUPS):
        send = [_grad_blocks(name, gw[name]).reshape(N_DEV, *_shard_view(name, W[name]).shape) for name in group]
        recv = sc_all_to_all8(send, "exchange_" + tag, 2 + k)
        for name, blocks in zip(group, recv):
            res = adamw_blocks(_shard_view(name, W[name]), blocks, _shard_view(name, M[name]), _shard_view(name, V[name]),
                               "adamw_" + name)
            G[name], DW[name], NM[name], NV[name] = [_shard_unview(name, t) for t in res]
    DW["w_ada"], NM["w_ada"], NV["w_ada"] = adamw(w_ada, g_wada, m_w_ada, v_w_ada, "adamw_w_ada")
    small = [n for n in names if n not in DW]
    as2d = lambda a: a.reshape(-1, a.shape[-1])
    outs = adamw_many([as2d(W[n]) for n in small], [as2d(G[n]) for n in small], [as2d(M[n]) for n in small],
                      [as2d(V[n]) for n in small], "adamw_small")
    for res, dst in zip(outs, (DW, NM, NV)):
        for n, t in zip(small, res):
            dst[n] = t.reshape(W[n].shape)
    return (loss.reshape(()), R["dx"], *[G[n] for n in names], *[DW[n] for n in names], *[NM[n] for n in names],
            *[NV[n] for n in names])
```

```python
import math

import jax
import jax.numpy as jnp
from jax import lax
from jax.experimental import pallas as pl
from jax.experimental.pallas import tpu as pltpu
from jax.experimental.pallas import tpu_sc as plsc

F32, BF16, I32 = jnp.float32, jnp.bfloat16, jnp.int32
HI = lax.Precision.HIGHEST
SDS = jax.ShapeDtypeStruct
MESH = pl.DeviceIdType.MESH

D_MODEL = 1024
D_FF = 2816
D_SSD = 1024
SSD_HEADS = 16
SSD_HEAD_DIM = 64
SSD_GROUPS = 2
SSD_STATE = 128
CHUNK = 128
MLA_HEADS = 8
QK_NOPE = 64
QK_ROPE = 32
QK_DIM = 96
V_HEAD = 128
Q_LORA = 384
KV_LORA = 256
ROPE_THETA = 10000.0
N_MOD = 9
EPS = 1e-6
D_CONV = 1536
D_IN = 3248
D_IN_PAD = 3328
HEAD_PAD = 128
N_DEV = 8
ADAM_LR, ADAM_B1, ADAM_B2, ADAM_EPS, ADAM_WD, ADAM_STEP = 0.001, 0.9, 0.999, 1e-08, 0.01, 10

SAVED_ACT = BF16
VMEM_LIMIT = 56 * 1024 * 1024
LANES = 128
NT_DIMS = (((1,), (1,)), ((), ()))
TN_DIMS = (((0,), (0,)), ((), ()))


def _cparams(n_axes):
    return pltpu.CompilerParams(dimension_semantics=("arbitrary",) * n_axes, vmem_limit_bytes=VMEM_LIMIT)


def _row(tm, d):
    return pl.BlockSpec((None, tm, d), lambda b, i: (b, i, 0))


def _bvec(d):
    return pl.BlockSpec((None, 1, d), lambda b, i: (b, 0, 0))


def _full(shape):
    n = len(shape)
    return pl.BlockSpec(shape, lambda *_: (0,) * n)


def _sigmoid(x):
    return 1.0 / (1.0 + jnp.exp(-x))


def _softplus(x):
    return jnp.maximum(x, 0.0) + jnp.log(1.0 + jnp.exp(-jnp.abs(x)))


def _rms(x):
    return lax.rsqrt(jnp.mean(x * x, axis=-1, keepdims=True) + EPS)


def _rms_bwd(dn, n, r):
    return r * (dn - n * jnp.mean(dn * n, axis=-1, keepdims=True))


def _first_step():
    return (pl.program_id(0) == 0) & (pl.program_id(1) == 0)


def _gather_copies(x_refs, out_refs, send_sems, recv_sems, local_sems):
    mx, my, mc = lax.axis_index("x"), lax.axis_index("y"), lax.axis_index("c")
    me, sibling = (mx, my, mc), (mx, my, 1 - mc)
    chips = [(1 - mx, my), (mx, 1 - my), (1 - mx, 1 - my)]

    def copy(a, k, block, to, src=None):
        rows = out_refs[a].at[4 * block[0] + 2 * block[1] + block[2]]
        return pltpu.make_async_remote_copy(
            src_ref=rows if src is None else src, dst_ref=rows,
            send_sem=send_sems.at[7 * a + k], recv_sem=recv_sems.at[7 * a + k], device_id=to, device_id_type=MESH)

    arrays = range(len(x_refs))
    mine = [pltpu.make_async_copy(x_refs[a], out_refs[a].at[4 * mx + 2 * my + mc], local_sems.at[a]) for a in arrays]
    first = [[copy(a, 0, me, sibling, src=x_refs[a])] + [copy(a, 1 + j, me, (*chip, mc), src=x_refs[a])
                                                          for j, chip in enumerate(chips)] for a in arrays]
    passed = [[copy(a, 4 + j, (*chip, mc), sibling) for j, chip in enumerate(chips)] for a in arrays]
    for a in arrays:
        mine[a].start()
        for cp in first[a]:
            cp.start()
    for a in arrays:
        for j, chip in enumerate(chips):
            copy(a, 1 + j, (*chip, mc), me).wait_recv()
            passed[a][j].start()
    for a in arrays:
        copy(a, 0, sibling, me).wait_recv()
        for j, chip in enumerate(chips):
            copy(a, 4 + j, (*chip, 1 - mc), me).wait_recv()
    for a in arrays:
        for cp in first[a] + passed[a]:
            cp.wait_send()
        mine[a].wait()


def _gather_peers():
    mx, my, mc = lax.axis_index("x"), lax.axis_index("y"), lax.axis_index("c")
    return [(mx, my, 1 - mc), (1 - mx, my, mc), (mx, 1 - my, mc), (1 - mx, 1 - my, mc)]


def _comm_scratch(n):
    return [pltpu.SemaphoreType.DMA((7 * n,)), pltpu.SemaphoreType.DMA((7 * n,)), pltpu.SemaphoreType.DMA((n,))]


def all_gather8(xs, name):
    n = len(xs)

    def body(*refs):
        _gather_copies(refs[:n], refs[n:2 * n], *refs[2 * n:])

    return pl.pallas_call(
        body, name=name,
        out_shape=[SDS((N_DEV, *x.shape), x.dtype) for x in xs],
        in_specs=[pl.BlockSpec(memory_space=pl.ANY)] * n,
        out_specs=[pl.BlockSpec(memory_space=pl.ANY)] * n,
        scratch_shapes=_comm_scratch(n),
    )(*xs)


def _exchange_peers():
    mx, my, mc = lax.axis_index("x"), lax.axis_index("y"), lax.axis_index("c")
    return [(1 - mx if rel & 4 else mx, 1 - my if rel & 2 else my, 1 - mc if rel & 1 else mc) for rel in range(1, N_DEV)]


def _exchange_copies(x_refs, out_refs, send_sems, recv_sems, local_sems):
    mx, my, mc = lax.axis_index("x"), lax.axis_index("y"), lax.axis_index("c")
    me = 4 * mx + 2 * my + mc
    copies = []
    for a, (x_ref, out_ref) in enumerate(zip(x_refs, out_refs)):
        mine = pltpu.make_async_copy(x_ref.at[me], out_ref.at[me], local_sems.at[a])
        mine.start()
        copies.append(mine)
        for k, (px, py, pc) in enumerate(_exchange_peers()):
            cp = pltpu.make_async_remote_copy(
                src_ref=x_ref.at[4 * px + 2 * py + pc], dst_ref=out_ref.at[me],
                send_sem=send_sems.at[7 * a + k], recv_sem=recv_sems.at[7 * a + k],
                device_id=(px, py, pc), device_id_type=MESH)
            cp.start()
            copies.append(cp)
    for cp in copies:
        cp.wait()


def all_to_all8(xs, name):
    n = len(xs)

    def body(*refs):
        _exchange_copies(refs[:n], refs[n:2 * n], *refs[2 * n:])

    return pl.pallas_call(
        body, name=name,
        out_shape=[SDS(x.shape, x.dtype) for x in xs],
        in_specs=[pl.BlockSpec(memory_space=pl.ANY)] * n,
        out_specs=[pl.BlockSpec(memory_space=pl.ANY)] * n,
        scratch_shapes=_comm_scratch(n),
    )(*xs)


def _sequencer_kernel(name, collective_id, n_arrays):
    return pl.kernel(
        mesh=plsc.ScalarSubcoreMesh(axis_name="seq", num_cores=1), name=name,
        scratch_types=tuple(_comm_scratch(n_arrays)),
        compiler_params=pltpu.CompilerParams(collective_id=collective_id))


def _handshake(peers):
    barrier = pltpu.get_barrier_semaphore()
    for peer in peers:
        pl.semaphore_signal(barrier, inc=1, device_id=peer, device_id_type=MESH)
    pl.semaphore_wait(barrier, len(peers))


def _hbm_refs(xs, out_shapes):
    x_refs = [jax.new_ref(x, memory_space=pltpu.MemorySpace.HBM) for x in xs]
    out_refs = [jax.empty_ref(SDS(shp, x.dtype), memory_space=pltpu.MemorySpace.HBM) for x, shp in zip(xs, out_shapes)]
    return x_refs, out_refs


def sc_all_gather8(xs, name, collective_id):
    x_refs, out_refs = _hbm_refs(xs, [(N_DEV, *x.shape) for x in xs])

    @_sequencer_kernel(name, collective_id, len(xs))
    def launch(send_sems, recv_sems, local_sems):
        _handshake(_gather_peers())
        _gather_copies(x_refs, out_refs, send_sems, recv_sems, local_sems)

    launch()
    return [ref[...] for ref in out_refs]


def sc_all_to_all8(xs, name, collective_id):
    x_refs, out_refs = _hbm_refs(xs, [x.shape for x in xs])

    @_sequencer_kernel(name, collective_id, len(xs))
    def launch(send_sems, recv_sems, local_sems):
        _handshake(_exchange_peers())
        _exchange_copies(x_refs, out_refs, send_sems, recv_sems, local_sems)

    launch()
    return [ref[...] for ref in out_refs]


def norm_mod(x, w, sc, sh, name):
    b, s, d = x.shape
    tm = min(512, s)

    def body(x_ref, w_ref, sc_ref, sh_ref, h_ref):
        xv = x_ref[...]
        n = xv * _rms(xv)
        h_ref[...] = ((n * w_ref[...]) * (1.0 + sc_ref[...]) + sh_ref[...]).astype(BF16)

    return pl.pallas_call(
        body, name=name, grid=(b, s // tm),
        in_specs=[_row(tm, d), _full((1, d)), _bvec(d), _bvec(d)],
        out_specs=_row(tm, d), out_shape=SDS((b, s, d), BF16), compiler_params=_cparams(2))(x, w, sc, sh)


def ffn_up(h, wg_t, wu_t, name):
    b, s, d = h.shape
    f = wg_t.shape[0]
    tm, tn = min(1024, s), f // 2

    def body(h_ref, wg_ref, wu_ref, s_ref, t_ref, a_ref):
        hv = h_ref[...]
        g = lax.dot_general(hv, wg_ref[...], NT_DIMS, preferred_element_type=F32)
        u = lax.dot_general(hv, wu_ref[...], NT_DIMS, preferred_element_type=F32)
        sg = _sigmoid(g)
        silu = g * sg
        s_ref[...] = silu.astype(s_ref.dtype)
        t_ref[...] = (u * (sg + silu * (1.0 - sg))).astype(t_ref.dtype)
        a_ref[...] = (silu * u).astype(BF16)

    hs = pl.BlockSpec((None, tm, d), lambda j, bb, i: (bb, i, 0))
    ws = pl.BlockSpec((tn, d), lambda j, bb, i: (j, 0))
    os_ = pl.BlockSpec((None, tm, tn), lambda j, bb, i: (bb, i, j))
    return pl.pallas_call(
        body, name=name, grid=(f // tn, b, s // tm),
        in_specs=[hs, ws, ws], out_specs=[os_, os_, os_],
        out_shape=[SDS((b, s, f), SAVED_ACT), SDS((b, s, f), SAVED_ACT), SDS((b, s, f), BF16)],
        compiler_params=_cparams(3))(h, wg_t, wu_t)


def _norm_mod_tile(xv, w_ref, sc_ref, sh_ref):
    return ((xv * _rms(xv) * w_ref[...]) * (1.0 + sc_ref[...]) + sh_ref[...]).astype(BF16)


def ffn_down(a, wd, x, gate, scale, name, above=None):
    b, s, f = a.shape
    d = wd.shape[1]
    tm = min(1024, s)

    def body(a_ref, wd_ref, x_ref, g_ref, *rest):
        xn_ref, o_ref = rest[-3:-1] if above else rest
        o = jnp.dot(a_ref[...], wd_ref[...], preferred_element_type=F32)
        xn = x_ref[...] + (scale * g_ref[...]) * o
        xn_ref[...] = xn
        o_ref[...] = o.astype(BF16)
        if above:
            rest[-1][...] = _norm_mod_tile(xn, *rest[0:3])

    extra = above is not None
    return pl.pallas_call(
        body, name=name, grid=(b, s // tm),
        in_specs=[_row(tm, f), _full((f, d)), _row(tm, d), _bvec(d)] + ([_full((1, d)), _bvec(d), _bvec(d)] if extra else []),
        out_specs=[_row(tm, d), _row(tm, d)] + ([_row(tm, d)] if extra else []),
        out_shape=[SDS((b, s, d), F32), SDS((b, s, d), BF16)] + ([SDS((b, s, d), BF16)] if extra else []),
        compiler_params=_cparams(2))(a, wd, x, gate, *(above or ()))


def ffn_down_final(a, wd, x, gate, scale, w_final, tgt, name):
    b, s, f = a.shape
    d = wd.shape[1]
    tm = min(1024, s)

    def body(a_ref, wd_ref, x_ref, g_ref, w_ref, t_ref, loss_ref, dx_ref, dw_ref, do_ref, dg_ref):
        @pl.when(_first_step())
        def _():
            loss_ref[...] = jnp.zeros_like(loss_ref)
            dw_ref[...] = jnp.zeros_like(dw_ref)

        @pl.when(pl.program_id(1) == 0)
        def _():
            dg_ref[...] = jnp.zeros_like(dg_ref)
        o = jnp.dot(a_ref[...], wd_ref[...], preferred_element_type=F32)
        sg = scale * g_ref[...]
        xv = x_ref[...] + sg * o
        r = _rms(xv)
        n = xv * r
        wv = w_ref[...]
        e = n * wv - t_ref[...]
        loss_ref[...] += jnp.sum(e * e) * (0.5 / d)
        dy = e * (1.0 / d)
        dw_ref[...] += jnp.sum(dy * n, axis=0, keepdims=True)
        dx = _rms_bwd(dy * wv, n, r)
        dx_ref[...] = dx
        do_ref[...] = (sg * dx).astype(BF16)
        dg_ref[...] += jnp.sum(scale * dx * o, axis=0, keepdims=True)

    return pl.pallas_call(
        body, name=name, grid=(b, s // tm),
        in_specs=[_row(tm, f), _full((f, d)), _row(tm, d), _bvec(d), _full((1, d)), _row(tm, d)],
        out_specs=[_full((1, LANES)), _row(tm, d), _full((1, d)), _row(tm, d), _bvec(d)],
        out_shape=[SDS((1, LANES), F32), SDS((b, s, d), F32), SDS((1, d), F32), SDS((b, s, d), BF16), SDS((b, 1, d), F32)],
        compiler_params=_cparams(2))(a, wd, x, gate, w_final, tgt)


def ffn_dact(do, wd, silu_g, u_dsilu, name):
    b, s, d = do.shape
    f = wd.shape[0]
    tm, tn = min(1024, s), f // 2

    def body(do_ref, wd_ref, s_ref, t_ref, dg_ref, du_ref):
        da = lax.dot_general(do_ref[...], wd_ref[...], NT_DIMS, preferred_element_type=F32)
        dg_ref[...] = (da * t_ref[...].astype(F32)).astype(BF16)
        du_ref[...] = (da * s_ref[...].astype(F32)).astype(BF16)

    dos = pl.BlockSpec((None, tm, d), lambda j, bb, i: (bb, i, 0))
    ws = pl.BlockSpec((tn, d), lambda j, bb, i: (j, 0))
    es = pl.BlockSpec((None, tm, tn), lambda j, bb, i: (bb, i, j))
    return pl.pallas_call(
        body, name=name, grid=(f // tn, b, s // tm),
        in_specs=[dos, ws, es, es], out_specs=[es, es],
        out_shape=[SDS((b, s, f), BF16), SDS((b, s, f), BF16)], compiler_params=_cparams(3))(do, wd, silu_g, u_dsilu)


def mm_tn(a, bm, tma, tnb, name):
    b, s, ka = a.shape
    nb = bm.shape[2]
    tk = min(2048, s)
    nk = s // tk

    def body(a_ref, b_ref, o_ref, acc):
        first = (pl.program_id(2) == 0) & (pl.program_id(3) == 0)
        last = (pl.program_id(2) == b - 1) & (pl.program_id(3) == nk - 1)
        part = lax.dot_general(a_ref[...], b_ref[...], TN_DIMS, preferred_element_type=F32)

        @pl.when(first)
        def _():
            acc[...] = part

        @pl.when(jnp.logical_not(first))
        def _():
            acc[...] += part

        @pl.when(last)
        def _():
            o_ref[...] = acc[...].astype(BF16)

    return pl.pallas_call(
        body, name=name, grid=(ka // tma, nb // tnb, b, nk),
        in_specs=[pl.BlockSpec((None, tk, tma), lambda i, j, bb, k: (bb, k, i)),
                  pl.BlockSpec((None, tk, tnb), lambda i, j, bb, k: (bb, k, j))],
        out_specs=pl.BlockSpec((tma, tnb), lambda i, j, bb, k: (i, j)),
        out_shape=SDS((ka, nb), BF16), scratch_shapes=[pltpu.VMEM((tma, tnb), F32)],
        compiler_params=_cparams(4))(a, bm)


def mm_tn_blocks(a_blocks, bm, name):
    b, s, nb = bm.shape
    widths = [a.shape[2] for a in a_blocks]
    starts = [sum(widths[:k]) for k in range(len(widths))]
    tk = min(1024, s)
    nk = s // tk
    n = len(a_blocks)

    def body(*refs):
        a_refs, b_ref, o_ref, acc = refs[:n], refs[n], refs[n + 1], refs[n + 2]
        first = (pl.program_id(0) == 0) & (pl.program_id(1) == 0)
        last = (pl.program_id(0) == b - 1) & (pl.program_id(1) == nk - 1)

        @pl.when(first)
        def _():
            acc[...] = jnp.zeros_like(acc)
        bv = b_ref[...]
        for a_ref, st, wd in zip(a_refs, starts, widths):
            acc[st:st + wd, :] += lax.dot_general(a_ref[...], bv, TN_DIMS, preferred_element_type=F32)

        @pl.when(last)
        def _():
            o_ref[...] = acc[...].astype(BF16)

    return pl.pallas_call(
        body, name=name, grid=(b, nk),
        in_specs=[_row(tk, wd) for wd in widths] + [_row(tk, nb)],
        out_specs=_full((sum(widths), nb)), out_shape=SDS((sum(widths), nb), BF16),
        scratch_shapes=[pltpu.VMEM((sum(widths), nb), F32)], compiler_params=_cparams(2))(*a_blocks, bm)


def _gate_bwd_specs(tm, d, b, s):
    return ([_row(tm, d), _bvec(d)], [_row(tm, d), _bvec(d)], [SDS((b, s, d), BF16), SDS((b, 1, d), F32)])


def _gate_bwd_tile(dx, scale, o_ref, g_ref, do_ref, dg_ref):
    do_ref[...] = ((scale * g_ref[...]) * dx).astype(BF16)
    dg_ref[...] += jnp.sum(scale * dx * o_ref[...].astype(F32), axis=0, keepdims=True)


def dh_norm_bwd(dys, wts, x, dxn, w, sc, name, below=None):
    b, s, d = x.shape
    tm = min(512, s)
    n_in, n_w = len(dys), len(wts)
    extra_in, extra_out, extra_shape = _gate_bwd_specs(tm, d, b, s) if below else ([], [], [])
    starts = [sum(dy.shape[2] for dy in dys[:k]) for k in range(n_in)]

    def body(*refs):
        dy_refs, w_refs = refs[:n_in], refs[n_in:n_in + n_w]
        x_ref, dxn_ref, nw_ref, sc_ref = refs[n_in + n_w:n_in + n_w + 4]
        rest = refs[n_in + n_w + 4:]
        if below:
            o_ref, g_ref, dx_ref, dsc_ref, dsh_ref, dw_ref, do_ref, dg_ref = rest
        else:
            dx_ref, dsc_ref, dsh_ref, dw_ref = rest

        @pl.when(pl.program_id(1) == 0)
        def _():
            dsc_ref[...] = jnp.zeros_like(dsc_ref)
            dsh_ref[...] = jnp.zeros_like(dsh_ref)
            if below:
                dg_ref[...] = jnp.zeros_like(dg_ref)

        @pl.when(_first_step())
        def _():
            dw_ref[...] = jnp.zeros_like(dw_ref)

        def weight(k):
            return w_refs[k][...] if n_w == n_in else w_refs[0][starts[k]:starts[k] + dys[k].shape[2], :]

        dh = jnp.dot(dy_refs[0][...], weight(0), preferred_element_type=F32)
        for k in range(1, n_in):
            dh += jnp.dot(dy_refs[k][...], weight(k), preferred_element_type=F32)
        xv = x_ref[...]
        r = _rms(xv)
        n = xv * r
        nw = nw_ref[...]
        dsc_ref[...] += jnp.sum(dh * (n * nw), axis=0, keepdims=True)
        dsh_ref[...] += jnp.sum(dh, axis=0, keepdims=True)
        dhn = dh * (1.0 + sc_ref[...])
        dw_ref[...] += jnp.sum(dhn * n, axis=0, keepdims=True)
        dx = dxn_ref[...] + _rms_bwd(dhn * nw, n, r)
        dx_ref[...] = dx
        if below:
            _gate_bwd_tile(dx, below[2], o_ref, g_ref, do_ref, dg_ref)

    resident = lambda shape: pl.BlockSpec(shape, lambda *_: (0,) * len(shape), pipeline_mode=pl.Buffered(1))
    in_specs = [_row(tm, dy.shape[2]) for dy in dys] + [resident(wt.shape) for wt in wts]
    in_specs += [_row(tm, d), _row(tm, d), _full((1, d)), _bvec(d)] + extra_in
    return pl.pallas_call(
        body, name=name, grid=(b, s // tm), in_specs=in_specs,
        out_specs=[_row(tm, d), _bvec(d), _bvec(d), _full((1, d))] + extra_out,
        out_shape=[SDS((b, s, d), F32), SDS((b, 1, d), F32), SDS((b, 1, d), F32), SDS((1, d), F32)] + extra_shape,
        compiler_params=_cparams(2))(*dys, *wts, x, dxn, w, sc, *(below[:2] if below else ()))


def in_proj(h, win_t, name):
    b, s, d = h.shape
    tm = min(512, s)
    widths = (D_SSD, D_SSD + 2 * SSD_GROUPS * SSD_STATE, Q_LORA, KV_LORA, LANES)

    def body(h_ref, w_ref, *outs):
        p = lax.dot_general(h_ref[...], w_ref[...], NT_DIMS, preferred_element_type=F32)
        off = 0
        for o_ref, wd in zip(outs, widths):
            o_ref[...] = p[:, off:off + wd]
            off += wd

    return pl.pallas_call(
        body, name=name, grid=(b, s // tm),
        in_specs=[_row(tm, d), _full(win_t.shape)],
        out_specs=[_row(tm, wd) for wd in widths],
        out_shape=[SDS((b, s, wd), F32) for wd in widths], compiler_params=_cparams(2))(h, win_t)


def _halo_prev(ts, d):
    return pl.BlockSpec((None, 8, d), lambda b, i: (b, jnp.maximum(i * (ts // 8) - 1, 0), 0))


CONV_ROWS = 32


def _conv_head(head, u_ref, up_ref, tile):
    head[0:8, :] = jnp.where(tile > 0, up_ref[...], 0.0)
    head[8:8 + CONV_ROWS, :] = u_ref[0:CONV_ROWS, :]


def _conv_windows(u_ref, head, r0):
    if r0 == 0:
        return [head[5 + k:5 + k + CONV_ROWS, :] for k in range(4)]
    return [u_ref[r0 - 3 + k:r0 - 3 + k + CONV_ROWS, :] for k in range(4)]


def _fold8(t):
    acc = t[0:8, :]
    for r in range(8, CONV_ROWS, 8):
        acc += t[r:r + 8, :]
    return acc


def conv_fwd(u, cw, cb, name):
    b, s, dc = u.shape
    ts = min(512, s)
    widths = (D_SSD, SSD_GROUPS * SSD_STATE, SSD_GROUPS * SSD_STATE)

    def body(u_ref, up_ref, w_ref, b_ref, xs_ref, bm_ref, cm_ref, head):
        _conv_head(head, u_ref, up_ref, pl.program_id(1))
        ws = [w_ref[k:k + 1, :] for k in range(4)]
        bias = b_ref[...]
        for r0 in range(0, ts, CONV_ROWS):
            taps = _conv_windows(u_ref, head, r0)
            v = bias + taps[0] * ws[0] + taps[1] * ws[1] + taps[2] * ws[2] + taps[3] * ws[3]
            y = v * _sigmoid(v)
            rs = slice(r0, r0 + CONV_ROWS)
            xs_ref[rs, :] = y[:, 0:D_SSD]
            bm_ref[rs, :] = y[:, D_SSD:D_SSD + 256]
            cm_ref[rs, :] = y[:, D_SSD + 256:D_SSD + 512]

    return pl.pallas_call(
        body, name=name, grid=(b, s // ts),
        in_specs=[_row(ts, dc), _halo_prev(ts, dc), _full((4, dc)), _full((1, dc))],
        out_specs=[_row(ts, wd) for wd in widths],
        out_shape=[SDS((b, s, wd), F32) for wd in widths],
        scratch_shapes=[pltpu.VMEM((8 + CONV_ROWS, dc), F32)], compiler_params=_cparams(2))(u, u, cw, cb)


def conv_bwd(dxs, dbm, dcm, u, cw, cb, name):
    b, s, dc = u.shape
    ts = min(512, s)
    nt = s // ts

    def body(dxs_ref, dbm_ref, dcm_ref, u_ref, up_ref, w_ref, b_ref, du_ref, dwb_ref, head, dvs):
        @pl.when(_first_step())
        def _():
            dwb_ref[...] = jnp.zeros_like(dwb_ref)

        @pl.when(pl.program_id(1) == 0)
        def _():
            dvs[ts:ts + 8, :] = jnp.zeros((8, dc), F32)
        _conv_head(head, u_ref, up_ref, nt - 1 - pl.program_id(1))
        ws = [w_ref[k:k + 1, :] for k in range(4)]
        bias = b_ref[...]
        for r0 in range(0, ts, CONV_ROWS):
            taps = _conv_windows(u_ref, head, r0)
            v = bias + taps[0] * ws[0] + taps[1] * ws[1] + taps[2] * ws[2] + taps[3] * ws[3]
            sg = _sigmoid(v)
            rs = slice(r0, r0 + CONV_ROWS)
            dy = jnp.concatenate([dxs_ref[rs, :], dbm_ref[rs, :], dcm_ref[rs, :]], axis=1)
            dv = dy * (sg * (1.0 + v * (1.0 - sg)))
            dvs[rs, :] = dv
            for k in range(4):
                dwb_ref[8 * k:8 * k + 8, :] += _fold8(dv * taps[k])
            dwb_ref[32:40, :] += _fold8(dv)
        for r0 in range(0, ts, CONV_ROWS):
            win = [dvs[r0 + 3 - k:r0 + 3 - k + CONV_ROWS, :] for k in range(4)]
            acc = win[0] * ws[0] + win[1] * ws[1] + win[2] * ws[2] + win[3] * ws[3]
            du_ref[r0:r0 + CONV_ROWS, :] = acc.astype(BF16)
        dvs[ts:ts + 8, :] = dvs[0:8, :]

    rows = lambda wd: pl.BlockSpec((None, ts, wd), lambda bb, i: (bb, nt - 1 - i, 0))
    prev = pl.BlockSpec((None, 8, dc), lambda bb, i: (bb, jnp.maximum((nt - 1 - i) * (ts // 8) - 1, 0), 0))
    return pl.pallas_call(
        body, name=name, grid=(b, nt),
        in_specs=[rows(D_SSD), rows(256), rows(256), rows(dc), prev, _full((4, dc)), _full((1, dc))],
        out_specs=[rows(dc), _full((40, dc))],
        out_shape=[SDS((b, s, dc), BF16), SDS((40, dc), F32)],
        scratch_shapes=[pltpu.VMEM((8 + CONV_ROWS, dc), F32), pltpu.VMEM((ts + 8, dc), F32)],
        compiler_params=_cparams(2))(dxs, dbm, dcm, u, u, cw, cb)


def conv_grads_fold(x, name):
    c = x.shape[1]

    def body(x_ref, o_ref):
        o_ref[...] = jnp.zeros_like(o_ref)
        for k in range(5):
            o_ref[k:k + 1, :] = jnp.sum(x_ref[8 * k:8 * k + 8, :], axis=0, keepdims=True)

    return pl.pallas_call(body, name=name, out_shape=SDS((8, c), F32))(x)


def _ssd_common(misc_ref, dtb_ref, alog_ref, e_ref):
    ln = CHUNK
    lane = lax.broadcasted_iota(I32, (ln, LANES), 1)
    lane1 = lax.broadcasted_iota(I32, (1, LANES), 1)
    pre = misc_ref[...] + dtb_ref[...]
    dt_s = jnp.where(lane < SSD_HEADS, _softplus(pre), 0.0)
    a_neg = jnp.where(lane1 < SSD_HEADS, -jnp.exp(alog_ref[...]), 0.0)
    ri = lax.broadcasted_iota(I32, (ln, ln), 0)
    ci = lax.broadcasted_iota(I32, (ln, ln), 1)
    tril = ci <= ri
    acum = jnp.dot(tril.astype(F32), dt_s * a_neg, preferred_element_type=F32, precision=HI)
    both_e = _dot_01(jnp.concatenate([dt_s, acum], axis=0), e_ref[...], 3)
    dt_e, acum_e = both_e[0:ln], both_e[ln:2 * ln]
    return dict(pre=pre, dt_s=dt_s, a_neg=a_neg, tril=tril, ri=ri, ci=ci, acum=acum, acum_t=acum.T,
                dt_e=dt_e, eac_e=jnp.exp(acum_e), del_e=jnp.exp(acum_e[ln - 1:ln, :] - acum_e))


def _dot_01(x, m01, terms, dims=(((1,), (0,)), ((), ()))):
    acc, rest = None, x
    for k in range(terms):
        part = rest.astype(BF16)
        if k + 1 < terms:
            rest = rest - part.astype(F32)
        d = lax.dot_general(part, m01, dims, preferred_element_type=F32)
        acc = d if acc is None else acc + d
    return acc


def _decay(cm, h):
    seg = cm["acum"][:, h:h + 1] - cm["acum_t"][h:h + 1, :]
    return jnp.exp(jnp.where(cm["tril"], seg, -jnp.inf))


def ssd_fwd(xs, bm, cm_, misc, z, dtb, alog, dskip_e, norm_w, e_mat, name):
    b, s, _ = xs.shape
    ln, nc = CHUNK, s // CHUNK
    gw = D_SSD // SSD_GROUPS
    hpg = SSD_HEADS // SSD_GROUPS

    def body(xs_ref, b_ref, c_ref, misc_ref, z_ref, dtb_ref, alog_ref, dsk_ref, nw_ref, e_ref,
             ys_ref, y_ref, p_ref, st, yd):
        @pl.when(pl.program_id(1) == 0)
        def _():
            st[...] = jnp.zeros_like(st)
        cm = _ssd_common(misc_ref, dtb_ref, alog_ref, e_ref)
        xsv = xs_ref[...]
        xdt = xsv * cm["dt_e"]
        xdt_b = xdt.astype(BF16)
        xd_b = (xdt * cm["del_e"]).astype(BF16)
        gam_e = cm["eac_e"][ln - 1:ln, :]
        p_ref[...] = st[...]
        groups = [slice(gw * g, gw * (g + 1)) for g in range(SSD_GROUPS)]
        heads = [slice(SSD_HEAD_DIM * h, SSD_HEAD_DIM * (h + 1)) for h in range(SSD_HEADS)]
        bgs = [b_ref[:, SSD_STATE * g:SSD_STATE * (g + 1)].astype(BF16) for g in range(SSD_GROUPS)]
        cgs = [c_ref[:, SSD_STATE * g:SSD_STATE * (g + 1)].astype(BF16) for g in range(SSD_GROUPS)]
        cbs = [lax.dot_general(cg, bg, NT_DIMS, preferred_element_type=F32) for cg, bg in zip(cgs, bgs)]
        sts = [st[:, gs] for gs in groups]
        yoff = [jnp.dot(cg, st_g.astype(BF16), preferred_element_type=F32) * cm["eac_e"][:, gs]
                for cg, st_g, gs in zip(cgs, sts, groups)]
        news = [lax.dot_general(bg, xd_b[:, gs], TN_DIMS, preferred_element_type=F32) for bg, gs in zip(bgs, groups)]
        for gs, st_g, new in zip(groups, sts, news):
            st[:, gs] = st_g * gam_e[:, gs] + new
        ms = [(cbs[h // hpg] * _decay(cm, h)).astype(BF16) for h in range(SSD_HEADS)]
        for h, hs in enumerate(heads):
            yd[:, hs] = jnp.dot(ms[h], xdt_b[:, hs], preferred_element_type=F32)
        y = yd[...] + jnp.concatenate(yoff, axis=1) + dsk_ref[...] * xsv
        y_ref[...] = y
        zz = z_ref[...]
        yg = y * (zz * _sigmoid(zz))
        outs = []
        for g in range(SSD_GROUPS):
            ygg = yg[:, gw * g:gw * (g + 1)]
            outs.append(ygg * _rms(ygg) * nw_ref[:, gw * g:gw * (g + 1)])
        ys_ref[...] = jnp.concatenate(outs, axis=1).astype(BF16)

    row = lambda d: pl.BlockSpec((None, ln, d), lambda bb, c: (bb, c, 0))
    return pl.pallas_call(
        body, name=name, grid=(b, nc),
        in_specs=[row(D_SSD), row(256), row(256), row(LANES), row(D_SSD), _full((1, LANES)), _full((1, LANES)),
                  _full((1, D_SSD)), _full((1, D_SSD)), _full((LANES, D_SSD))],
        out_specs=[row(D_SSD), row(D_SSD), pl.BlockSpec((None, None, SSD_STATE, D_SSD), lambda bb, c: (bb, c, 0, 0))],
        out_shape=[SDS((b, s, D_SSD), BF16), SDS((b, s, D_SSD), F32), SDS((b, nc, SSD_STATE, D_SSD), F32)],
        scratch_shapes=[pltpu.VMEM((SSD_STATE, D_SSD), F32), pltpu.VMEM((ln, D_SSD), F32)],
        compiler_params=_cparams(2))(xs, bm, cm_, misc, z, dtb, alog, dskip_e, norm_w, e_mat)


def ssd_bwd(dys, y, z, xs, bm, cm_, misc, prev, dtb, alog, dskip_e, norm_w, e_mat, et_mat, name):
    b, s, _ = xs.shape
    ln, nc = CHUNK, s // CHUNK
    gw = D_SSD // SSD_GROUPS
    hpg = SSD_HEADS // SSD_GROUPS

    def body(dys_ref, y_ref, z_ref, xs_ref, b_ref, c_ref, misc_ref, p_ref, dtb_ref, alog_ref, dsk_ref, nw_ref,
             e_ref, et_ref, dxs_ref, db_ref, dc_ref, dz_ref, ddt_ref, dnw_ref, ddsk_ref, ddtb_ref, dalog_ref,
             dst, dxd, dac_t):
        @pl.when(_first_step())
        def _():
            for r_ in (dnw_ref, ddsk_ref, ddtb_ref, dalog_ref):
                r_[...] = jnp.zeros_like(r_)

        @pl.when(pl.program_id(1) == 0)
        def _():
            dst[...] = jnp.zeros_like(dst)

        cm = _ssd_common(misc_ref, dtb_ref, alog_ref, e_ref)
        et = et_ref[...]
        squeeze = lambda t: _dot_01(t, et, 2)
        lane = lax.broadcasted_iota(I32, (ln, LANES), 1)
        sub = lax.broadcasted_iota(I32, (LANES, ln), 0)
        xsv = xs_ref[...]
        xdt = xsv * cm["dt_e"]
        xdt_b = xdt.astype(BF16)
        xd_b = (xdt * cm["del_e"]).astype(BF16)
        eac_e = cm["eac_e"]
        gam_e = eac_e[ln - 1:ln, :]

        yv, zz, dyo = y_ref[...], z_ref[...], dys_ref[...]
        sz = _sigmoid(zz)
        silu_z = zz * sz
        yg = yv * silu_z
        dyg, dnw = [], []
        for g in range(SSD_GROUPS):
            gs = slice(gw * g, gw * (g + 1))
            ygg = yg[:, gs]
            r = _rms(ygg)
            n = ygg * r
            dnw.append(jnp.sum(dyo[:, gs] * n, axis=0, keepdims=True))
            dyg.append(_rms_bwd(dyo[:, gs] * nw_ref[:, gs], n, r))
        dyg = jnp.concatenate(dyg, axis=1)
        dnw_ref[...] += jnp.concatenate(dnw, axis=1)
        dz_ref[...] = (dyg * yv * (sz * (1.0 + zz * (1.0 - sz)))).astype(BF16)
        dy = dyg * silu_z
        ddsk_ref[...] += jnp.sum(dy * xsv, axis=0, keepdims=True)
        dy_b = dy.astype(BF16)

        dacum = jnp.zeros((ln, LANES), F32)
        dac_t[...] = jnp.zeros_like(dac_t)
        w1, dgam = [], []
        for g in range(SSD_GROUPS):
            gs = slice(gw * g, gw * (g + 1))
            ss = slice(SSD_STATE * g, SSD_STATE * (g + 1))
            bg = b_ref[:, ss].astype(BF16)
            cg = c_ref[:, ss].astype(BF16)
            cb = lax.dot_general(cg, bg, NT_DIMS, preferred_element_type=F32)
            pt = p_ref[:, gs]
            pt_b = pt.astype(BF16)
            dst_g = dst[:, gs]
            dst_b = dst_g.astype(BF16)
            edy = (dy[:, gs] * eac_e[:, gs]).astype(BF16)
            dcg = lax.dot_general(edy, pt_b, NT_DIMS, preferred_element_type=F32)
            dpt = lax.dot_general(cg, edy, TN_DIMS, preferred_element_type=F32)
            yoff = jnp.dot(cg, pt_b, preferred_element_type=F32) * eac_e[:, gs]
            dxd_g = jnp.dot(bg, dst_b, preferred_element_type=F32)
            dbg = lax.dot_general(xd_b[:, gs], dst_b, NT_DIMS, preferred_element_type=F32)
            ddel = dxd_g * xdt[:, gs] * cm["del_e"][:, gs]
            w1.append(dy[:, gs] * yoff - ddel)
            dgam.append(jnp.sum(ddel, axis=0, keepdims=True) + jnp.sum(dst_g * pt, axis=0, keepdims=True) * gam_e[:, gs])
            dxd[:, gs] = dxd_g * cm["del_e"][:, gs]
            dst[:, gs] = dst_g * gam_e[:, gs] + dpt
            dcb = jnp.zeros((ln, ln), F32)
            for j in range(hpg):
                h = hpg * g + j
                hs = slice(SSD_HEAD_DIM * h, SSD_HEAD_DIM * (h + 1))
                lam = _decay(cm, h)
                m = cb * lam
                dm = lax.dot_general(dy_b[:, hs], xdt_b[:, hs], NT_DIMS, preferred_element_type=F32)
                dxd[:, hs] += lax.dot_general(m.astype(BF16), dy_b[:, hs], TN_DIMS, preferred_element_type=F32)
                dcb += dm * lam
                wl = dm * m
                dacum += jnp.where(lane == h, jnp.sum(wl, axis=1, keepdims=True), 0.0)
                dac_t[...] -= jnp.where(sub == h, jnp.sum(wl, axis=0, keepdims=True), 0.0)
            dcb_b = dcb.astype(BF16)
            dc_ref[:, ss] = dcg + jnp.dot(dcb_b, bg, preferred_element_type=F32)
            db_ref[:, ss] = dbg + lax.dot_general(dcb_b, cg, TN_DIMS, preferred_element_type=F32)

        dxdt = dxd[...]
        dxs_ref[...] = dy * dsk_ref[...] + dxdt * cm["dt_e"]
        dacum += squeeze(jnp.concatenate(w1, axis=1)) + dac_t[...].T
        dlast = squeeze(jnp.broadcast_to(jnp.concatenate(dgam, axis=1), (8, D_SSD)))[0:1, :]
        dacum += jnp.where(lax.broadcasted_iota(I32, (ln, LANES), 0) == ln - 1, dlast, 0.0)
        triu = (cm["ci"] >= cm["ri"]).astype(F32)
        da = jnp.dot(triu, dacum, preferred_element_type=F32, precision=HI)
        ddt = da * cm["a_neg"] + squeeze(dxdt * xsv)
        dalog_ref[...] += jnp.sum(da * cm["dt_s"], axis=0, keepdims=True) * cm["a_neg"]
        ddt_raw = jnp.where(lane < SSD_HEADS, ddt * _sigmoid(cm["pre"]), 0.0)
        ddt_ref[...] = ddt_raw
        ddtb_ref[...] += jnp.sum(ddt_raw, axis=0, keepdims=True)

    row = lambda d: pl.BlockSpec((None, ln, d), lambda bb, c: (bb, nc - 1 - c, 0))
    return pl.pallas_call(
        body, name=name, grid=(b, nc),
        in_specs=[row(D_SSD), row(D_SSD), row(D_SSD), row(D_SSD), row(256), row(256), row(LANES),
                  pl.BlockSpec((None, None, SSD_STATE, D_SSD), lambda bb, c: (bb, nc - 1 - c, 0, 0)),
                  _full((1, LANES)), _full((1, LANES)), _full((1, D_SSD)), _full((1, D_SSD)),
                  _full((LANES, D_SSD)), _full((D_SSD, LANES))],
        out_specs=[row(D_SSD), row(256), row(256), row(D_SSD), row(LANES),
                   _full((1, D_SSD)), _full((1, D_SSD)), _full((1, LANES)), _full((1, LANES))],
        out_shape=[SDS((b, s, D_SSD), F32), SDS((b, s, 256), F32), SDS((b, s, 256), F32), SDS((b, s, D_SSD), BF16),
                   SDS((b, s, LANES), F32), SDS((1, D_SSD), F32), SDS((1, D_SSD), F32), SDS((1, LANES), F32),
                   SDS((1, LANES), F32)],
        scratch_shapes=[pltpu.VMEM((SSD_STATE, D_SSD), F32), pltpu.VMEM((ln, D_SSD), F32), pltpu.VMEM((LANES, ln), F32)],
        compiler_params=_cparams(2))(dys, y, z, xs, bm, cm_, misc, prev, dtb, alog, dskip_e, norm_w, e_mat, et_mat)


def _rope(xv, cc, sp, sm):
    n = xv.shape[1]
    return xv * cc + pltpu.roll(xv, 16, 1) * sp + pltpu.roll(xv, n - 16, 1) * sm


def _rope_bwd(dy, cc, sp, sm):
    n = dy.shape[1]
    return dy * cc + pltpu.roll(dy * sp, n - 16, 1) + pltpu.roll(dy * sm, 16, 1)


def _tile8(t):
    return jnp.concatenate([t] * MLA_HEADS, axis=1)


def qkv_fwd(cq, ckv, misc, cc, sp, sm, qnw, kvnw, wuq_t, wukv_t, place, name):
    b, s, _ = cq.shape
    tm = _att_tile(s)
    hd = MLA_HEADS * HEAD_PAD

    def body(cq_ref, ckv_ref, misc_ref, cc_ref, sp_ref, sm_ref, qnw_ref, kvnw_ref, wq_ref, wkv_ref, pl_ref,
             q_ref, k_ref, v_ref, vt_ref, qn_ref, kvn_ref):
        cqv, ckvv = cq_ref[...], ckv_ref[...]
        qn = (cqv * _rms(cqv) * qnw_ref[...]).astype(BF16)
        kvn = (ckvv * _rms(ckvv) * kvnw_ref[...]).astype(BF16)
        qn_ref[...] = qn
        kvn_ref[...] = kvn
        cc1, sp1, sm1 = cc_ref[...], sp_ref[...], sm_ref[...]
        q = lax.dot_general(qn, wq_ref[...], NT_DIMS, preferred_element_type=F32)
        q_ref[...] = _rope(q, _tile8(cc1), _tile8(sp1), _tile8(sm1)).astype(BF16)
        kv = lax.dot_general(kvn, wkv_ref[...], NT_DIMS, preferred_element_type=F32)
        kr = jnp.dot(misc_ref[...], pl_ref[...], preferred_element_type=F32, precision=HI)
        kr = _rope(kr, cc1, sp1, sm1)
        k_ref[...] = (kv[:, 0:hd] + _tile8(kr)).astype(BF16)
        v_ref[...] = kv[:, hd:2 * hd].astype(BF16)
        for h in range(MLA_HEADS):
            vt_ref[h] = kv[:, hd + HEAD_PAD * h:hd + HEAD_PAD * (h + 1)].T.astype(BF16)

    return pl.pallas_call(
        body, name=name, grid=(b, s // tm),
        in_specs=[_row(tm, Q_LORA), _row(tm, KV_LORA), _row(tm, LANES), _row(tm, LANES), _row(tm, LANES), _row(tm, LANES),
                  _full((1, Q_LORA)), _full((1, KV_LORA)), _full(wuq_t.shape), _full(wukv_t.shape), _full((LANES, LANES))],
        out_specs=[_row(tm, hd), _row(tm, hd), _row(tm, hd),
                   pl.BlockSpec((None, MLA_HEADS, None, HEAD_PAD, tm), lambda bb, i: (bb, 0, i, 0, 0)),
                   _row(tm, Q_LORA), _row(tm, KV_LORA)],
        out_shape=[SDS((b, s, hd), BF16)] * 3 + [SDS((b, MLA_HEADS, s // tm, HEAD_PAD, tm), BF16),
                                                 SDS((b, s, Q_LORA), BF16), SDS((b, s, KV_LORA), BF16)],
        compiler_params=_cparams(2))(cq, ckv, misc, cc, sp, sm, qnw, kvnw, wuq_t, wukv_t, place)


def qkv_bwd(dq, dk, dv, ddt, cq, ckv, cc, sp, sm, qnw, kvnw, wuq_t, wukv_t, place_t, name):
    b, s, _ = cq.shape
    tm = min(512, s)
    hd = MLA_HEADS * HEAD_PAD

    def body(dq_ref, dk_ref, dv_ref, ddt_ref, cq_ref, ckv_ref, cc_ref, sp_ref, sm_ref, qnw_ref, kvnw_ref,
             wq_ref, wkv_ref, plt_ref, dcq_ref, dckv_ref, dmisc_ref, dqp_ref, dkv_ref, dqnw_ref, dkvnw_ref):
        @pl.when(_first_step())
        def _():
            dqnw_ref[...] = jnp.zeros_like(dqnw_ref)
            dkvnw_ref[...] = jnp.zeros_like(dkvnw_ref)
        cc1, sp1, sm1 = cc_ref[...], sp_ref[...], sm_ref[...]
        dqp = _rope_bwd(dq_ref[...].astype(F32), _tile8(cc1), _tile8(sp1), _tile8(sm1)).astype(BF16)
        dqp_ref[...] = dqp
        dkv_b = jnp.concatenate([dk_ref[...], dv_ref[...]], axis=1)
        dkf = dk_ref[...].astype(F32)
        dkv_ref[...] = dkv_b
        dkr = dkf[:, 0:HEAD_PAD]
        for h in range(1, MLA_HEADS):
            dkr += dkf[:, HEAD_PAD * h:HEAD_PAD * (h + 1)]
        dkr = _rope_bwd(dkr, cc1, sp1, sm1)
        dmisc_ref[...] = (jnp.dot(dkr, plt_ref[...], preferred_element_type=F32, precision=HI) + ddt_ref[...]).astype(BF16)

        def norm_bwd(dn_w, xv, w_ref, dw_ref, dx_ref):
            r = _rms(xv)
            n = xv * r
            dw_ref[...] += jnp.sum(dn_w * n, axis=0, keepdims=True)
            dx_ref[...] = _rms_bwd(dn_w * w_ref[...], n, r).astype(BF16)

        norm_bwd(jnp.dot(dqp, wq_ref[...], preferred_element_type=F32), cq_ref[...], qnw_ref, dqnw_ref, dcq_ref)
        norm_bwd(jnp.dot(dkv_b, wkv_ref[...], preferred_element_type=F32), ckv_ref[...], kvnw_ref, dkvnw_ref, dckv_ref)

    return pl.pallas_call(
        body, name=name, grid=(b, s // tm),
        in_specs=[_row(tm, hd), _row(tm, hd), _row(tm, hd), _row(tm, LANES), _row(tm, Q_LORA), _row(tm, KV_LORA),
                  _row(tm, LANES), _row(tm, LANES), _row(tm, LANES), _full((1, Q_LORA)), _full((1, KV_LORA)),
                  _full(wuq_t.shape), _full(wukv_t.shape), _full((LANES, LANES))],
        out_specs=[_row(tm, Q_LORA), _row(tm, KV_LORA), _row(tm, LANES), _row(tm, hd), _row(tm, 2 * hd),
                   _full((1, Q_LORA)), _full((1, KV_LORA))],
        out_shape=[SDS((b, s, Q_LORA), BF16), SDS((b, s, KV_LORA), BF16), SDS((b, s, LANES), BF16),
                   SDS((b, s, hd), BF16), SDS((b, s, 2 * hd), BF16), SDS((1, Q_LORA), F32), SDS((1, KV_LORA), F32)],
        compiler_params=_cparams(2))(dq, dk, dv, ddt, cq, ckv, cc, sp, sm, qnw, kvnw, wuq_t, wukv_t, place_t)


ATT_SCALE = 1.0 / math.sqrt(QK_DIM)
LOG2E = math.log2(math.e)
ATT_SCALE_LOG2E = ATT_SCALE * LOG2E


ATT_HEADS_PER_STEP = 4
ATT_HEADS_PER_STEP_BWD = 2


def _att_tile(s):
    return min(512, s)


def flash_fwd(q, k, vt, name):
    b, s, hd = q.shape
    t = _att_tile(s)
    nb = s // t
    th = t // 2

    hps = ATT_HEADS_PER_STEP
    hw = hps * HEAD_PAD

    def body(q_ref, k_ref, vt_ref, o_ref, lse_ref, m_s, l_s, acc):
        i = pl.program_id(2)
        m_s[...] = jnp.full_like(m_s, -jnp.inf)
        l_s[...] = jnp.zeros_like(l_s)
        acc[...] = jnp.zeros_like(acc)

        def update(j, diagonal):
            chains = [(hh, half) for hh in range(hps) for half in range(2)]
            lanes = lambda hh: slice(HEAD_PAD * hh, HEAD_PAD * (hh + 1))
            cols = lambda half: slice(th * half, th * (half + 1))
            sts = {}
            nkeys = lambda half: th if diagonal and half == 0 else t
            for hh, half in chains:
                kr = pl.ds(pl.multiple_of(j * t, t), nkeys(half))
                st = lax.dot_general(k_ref[kr, lanes(hh)], q_ref[cols(half), lanes(hh)], NT_DIMS,
                                     preferred_element_type=F32)
                if diagonal:
                    row = lax.broadcasted_iota(I32, (nkeys(half), th), 0)
                    col = lax.broadcasted_iota(I32, (nkeys(half), th), 1) + th * half
                    st = jnp.where(row <= col, st, -jnp.inf)
                sts[hh, half] = st
            pts, alphas = {}, {}
            for hh, half in chains:
                st, cs = sts[hh, half], cols(half)
                m_prev = m_s[hh, :, cs]
                m_new = jnp.maximum(m_prev, jnp.max(st, axis=0, keepdims=True))
                alpha = jnp.exp2((m_prev - m_new) * ATT_SCALE_LOG2E)
                pt = jnp.exp2((st - m_new) * ATT_SCALE_LOG2E)
                l_s[hh, :, cs] = alpha * l_s[hh, :, cs] + jnp.sum(pt, axis=0, keepdims=True)
                m_s[hh, :, cs] = m_new
                pts[hh, half], alphas[hh, half] = pt.astype(BF16), alpha
            for hh, half in chains:
                cs = cols(half)
                acc[hh, :, cs] = alphas[hh, half] * acc[hh, :, cs] + jnp.dot(
                    vt_ref[hh, j, :, 0:nkeys(half)], pts[hh, half], preferred_element_type=F32)

        def step(j, carry):
            update(j, False)
            return carry

        lax.fori_loop(0, i, step, 0)
        update(i, True)
        for hh in range(hps):
            o_ref[:, HEAD_PAD * hh:HEAD_PAD * (hh + 1)] = (acc[hh] / l_s[hh]).T
            lse_ref[hh] = m_s[hh] * ATT_SCALE + jnp.log(l_s[hh])

    qs = pl.BlockSpec((None, t, hw), lambda bb, h, i: (bb, i, h))
    ks = pl.BlockSpec((None, s, hw), lambda bb, h, i: (bb, 0, h))
    vs = pl.BlockSpec((None, hps, nb, HEAD_PAD, t), lambda bb, h, i: (bb, h, 0, 0, 0))
    ls = pl.BlockSpec((None, hps, None, 1, t), lambda bb, h, i: (bb, h, i, 0, 0))
    return pl.pallas_call(
        body, name=name, grid=(b, MLA_HEADS // hps, nb),
        in_specs=[qs, ks, vs], out_specs=[qs, ls],
        out_shape=[SDS((b, s, hd), F32), SDS((b, MLA_HEADS, nb, 1, t), F32)],
        scratch_shapes=[pltpu.VMEM((hps, 1, t), F32), pltpu.VMEM((hps, 1, t), F32), pltpu.VMEM((hps, HEAD_PAD, t), F32)],
        compiler_params=_cparams(3))(q, k, vt)


def flash_bwd(q, k, v, do, lse, dlt, name):
    b, s, hd = q.shape
    t = _att_tile(s)
    nb = s // t
    th = t // 2
    lse_r = lse
    dlt_r = dlt.reshape(b, MLA_HEADS, nb, 1, t)

    hps = ATT_HEADS_PER_STEP_BWD
    hw = hps * HEAD_PAD

    def body(q_ref, k_ref, v_ref, do_ref, lse_ref, dlt_ref, dq_ref, dk_ref, dv_ref, dq_s, dk_s, dv_s):
        dq_s[...] = jnp.zeros_like(dq_s)
        dk_s[...] = jnp.zeros_like(dk_s)
        dv_s[...] = jnp.zeros_like(dv_s)

        def tile(j, i, diagonal):
            chains = [(hh, half) for hh in range(hps) for half in range(2)]
            lanes = lambda hh: slice(HEAD_PAD * hh, HEAD_PAD * (hh + 1))
            keys = lambda half: pl.ds(pl.multiple_of(j * t + th * half, th), th)
            q0 = lambda half: th if diagonal and half == 1 else 0
            qsel = lambda half: pl.ds(pl.multiple_of(i * t + q0(half), th), t - q0(half))
            sts, dpts = {}, {}
            for hh, half in chains:
                ls_, ks, qs, nq = lanes(hh), keys(half), qsel(half), t - q0(half)
                st = lax.dot_general(k_ref[ks, ls_], q_ref[qs, ls_], NT_DIMS, preferred_element_type=F32)
                if diagonal:
                    row = lax.broadcasted_iota(I32, (th, nq), 0) + th * half
                    col = lax.broadcasted_iota(I32, (th, nq), 1) + q0(half)
                    st = jnp.where(row <= col, st, -jnp.inf)
                sts[hh, half] = st
                dpts[hh, half] = lax.dot_general(v_ref[ks, ls_], do_ref[qs, ls_], NT_DIMS, preferred_element_type=F32)
            pts, dsts = {}, {}
            for hh, half in chains:
                qcols = slice(q0(half), t)
                pt = jnp.exp2(sts[hh, half] * ATT_SCALE_LOG2E - lse_ref[hh, i][:, qcols] * LOG2E)
                pts[hh, half] = pt.astype(BF16)
                dsts[hh, half] = (pt * (dpts[hh, half] - dlt_ref[hh, i][:, qcols])).astype(BF16)
            for hh, half in chains:
                ls_, ks, qs = lanes(hh), keys(half), qsel(half)
                dv_s[ks, ls_] += jnp.dot(pts[hh, half], do_ref[qs, ls_], preferred_element_type=F32)
                dk_s[ks, ls_] += jnp.dot(dsts[hh, half], q_ref[qs, ls_], preferred_element_type=F32)
                dq_s[qs, ls_] += lax.dot_general(dsts[hh, half], k_ref[ks, ls_], TN_DIMS, preferred_element_type=F32)

        def key_tile(j, carry):
            tile(j, j, True)

            def query_tile(i, c2):
                tile(j, i, False)
                return c2

            lax.fori_loop(j + 1, nb, query_tile, 0)
            return carry

        lax.fori_loop(0, nb, key_tile, 0)
        dq_ref[...] = (dq_s[...] * ATT_SCALE).astype(BF16)
        dk_ref[...] = (dk_s[...] * ATT_SCALE).astype(BF16)
        dv_ref[...] = dv_s[...].astype(BF16)

    hs = pl.BlockSpec((None, s, hw), lambda bb, h: (bb, 0, h))
    ls = pl.BlockSpec((None, hps, nb, 1, t), lambda bb, h: (bb, h, 0, 0, 0))
    return pl.pallas_call(
        body, name=name, grid=(b, MLA_HEADS // hps),
        in_specs=[hs, hs, hs, hs, ls, ls], out_specs=[hs, hs, hs],
        out_shape=[SDS((b, s, hd), BF16)] * 3, scratch_shapes=[pltpu.VMEM((s, hw), F32)] * 3,
        compiler_params=_cparams(2))(q, k, v, do, lse_r, dlt_r)


def out_proj(ys, attn, mnw, wo, x, gate, above, name):
    b, s, d = x.shape
    tm = min(512, s)

    def body(ys_ref, at_ref, mnw_ref, wo_ref, x_ref, g_ref, nw_ref, sc_ref, sh_ref, xn_ref, o_ref, ym_ref, h_ref):
        av = at_ref[...]
        ym = (av * _rms(av) * mnw_ref[...]).astype(BF16)
        ym_ref[...] = ym
        o = jnp.dot(ys_ref[...], wo_ref[0:D_SSD, :], preferred_element_type=F32)
        o += jnp.dot(ym, wo_ref[D_SSD:2 * D_SSD, :], preferred_element_type=F32)
        xn = x_ref[...] + g_ref[...] * o
        xn_ref[...] = xn
        o_ref[...] = o.astype(BF16)
        h_ref[...] = _norm_mod_tile(xn, nw_ref, sc_ref, sh_ref)

    return pl.pallas_call(
        body, name=name, grid=(b, s // tm),
        in_specs=[_row(tm, D_SSD), _row(tm, D_SSD), _full((1, D_SSD)), _full(wo.shape), _row(tm, d), _bvec(d),
                  _full((1, d)), _bvec(d), _bvec(d)],
        out_specs=[_row(tm, d), _row(tm, d), _row(tm, D_SSD), _row(tm, d)],
        out_shape=[SDS((b, s, d), F32), SDS((b, s, d), BF16), SDS((b, s, D_SSD), BF16), SDS((b, s, d), BF16)],
        compiler_params=_cparams(2))(ys, attn, mnw, wo, x, gate, *above)


def out_proj_bwd(dout, attn, mnw, wo, name):
    b, s, d = dout.shape
    tm = min(512, s)

    def body(do_ref, at_ref, mnw_ref, wo_ref, dys_ref, dat_ref, dlt_ref, dw_ref):
        lane = lax.broadcasted_iota(I32, (tm, LANES), 1)
        @pl.when(_first_step())
        def _():
            dw_ref[...] = jnp.zeros_like(dw_ref)
        dov = do_ref[...]
        dys_ref[...] = lax.dot_general(dov, wo_ref[0:D_SSD, :], NT_DIMS, preferred_element_type=F32)
        dym = lax.dot_general(dov, wo_ref[D_SSD:2 * D_SSD, :], NT_DIMS, preferred_element_type=F32)
        av = at_ref[...]
        r = _rms(av)
        n = av * r
        dw_ref[...] += jnp.sum(dym * n, axis=0, keepdims=True)
        dat = _rms_bwd(dym * mnw_ref[...], n, r)
        dat_ref[...] = dat.astype(BF16)
        prod = dat * av
        cols = jnp.zeros((tm, LANES), F32)
        for h in range(MLA_HEADS):
            cols += jnp.where(lane == h, jnp.sum(prod[:, HEAD_PAD * h:HEAD_PAD * (h + 1)], axis=1, keepdims=True), 0.0)
        dlt_ref[...] = cols.T[0:MLA_HEADS, :]

    return pl.pallas_call(
        body, name=name, grid=(b, s // tm),
        in_specs=[_row(tm, d), _row(tm, D_SSD), _full((1, D_SSD)), _full(wo.shape)],
        out_specs=[_row(tm, D_SSD), _row(tm, D_SSD),
                   pl.BlockSpec((None, MLA_HEADS, tm), lambda bb, i: (bb, 0, i)), _full((1, D_SSD))],
        out_shape=[SDS((b, s, D_SSD), F32), SDS((b, s, D_SSD), BF16), SDS((b, MLA_HEADS, s), F32),
                   SDS((1, D_SSD), F32)],
        compiler_params=_cparams(2))(dout, attn, mnw, wo)


def adaln_fwd(c_all, w_ada, b_ada, name):
    nb, d = c_all.shape
    n = w_ada.shape[1]

    def body(c_ref, w_ref, b_ref, m_ref, ca_ref):
        cv = c_ref[...]
        ca = (cv * _sigmoid(cv)).astype(BF16)
        ca_ref[...] = ca
        m_ref[...] = jnp.dot(ca, w_ref[...].astype(BF16), preferred_element_type=F32) + b_ref[...]

    return pl.pallas_call(
        body, name=name, out_shape=[SDS((nb, n), F32), SDS((nb, d), BF16)],
        compiler_params=pltpu.CompilerParams(vmem_limit_bytes=VMEM_LIMIT))(c_all, w_ada, b_ada)


def adaln_bwd(c_act, dmod_cols, name):
    d, n = c_act.shape[1], dmod_cols.shape[1]

    def body(c_ref, dm_ref, gw_ref):
        gw_ref[...] = lax.dot_general(c_ref[...], dm_ref[...].astype(BF16), TN_DIMS, preferred_element_type=F32)

    return pl.pallas_call(
        body, name=name, out_shape=SDS((d, n), F32),
        compiler_params=pltpu.CompilerParams(vmem_limit_bytes=VMEM_LIMIT))(c_act, dmod_cols)


def sum_rows(x, name):
    def body(x_ref, o_ref):
        o_ref[...] = jnp.sum(x_ref[...], axis=0, keepdims=True)
    return pl.pallas_call(body, name=name, out_shape=SDS((1, x.shape[1]), F32))(x)


def squeeze_heads(x, et_mat, name):
    def body(x_ref, et_ref, o_ref):
        xv = jnp.broadcast_to(x_ref[...], (8, x.shape[1]))
        o_ref[...] = _dot_01(xv, et_ref[...], 3)[0:1, :]
    return pl.pallas_call(body, name=name, out_shape=SDS((1, LANES), F32))(x, et_mat)


def sum_blocks(x, name):
    n, r, c = x.shape
    tr = next(cand for cand in (256, 128, 64, 32, 16, 8) if r % cand == 0)

    def body(x_ref, o_ref):
        acc = x_ref[0].astype(F32)
        for k in range(1, n):
            acc += x_ref[k].astype(F32)
        o_ref[...] = acc

    return pl.pallas_call(
        body, name=name, grid=(r // tr,), in_specs=[pl.BlockSpec((n, tr, c), lambda i: (0, i, 0))],
        out_specs=pl.BlockSpec((tr, c), lambda i: (i, 0)), out_shape=SDS((r, c), F32),
        compiler_params=_cparams(1))(x)


def _adam_math(w, g, m, v):
    m = ADAM_B1 * m + (1.0 - ADAM_B1) * g
    v = ADAM_B2 * v + (1.0 - ADAM_B2) * (g * g)
    m_hat = m / (1.0 - ADAM_B1 ** ADAM_STEP)
    v_hat = v / (1.0 - ADAM_B2 ** ADAM_STEP)
    return -ADAM_LR * (m_hat / (jnp.sqrt(v_hat) + ADAM_EPS) + ADAM_WD * w), m, v


def adamw(w, g, m, v, name):
    r, c = w.shape[-2:]
    tr = r
    for cand in (512, 256, 128, 64, 32, 16, 8):
        if r % cand == 0 and cand * c * 4 <= 2 * 1024 * 1024:
            tr = cand
            break

    def body(w_ref, g_ref, m_ref, v_ref, d_ref, mo_ref, vo_ref):
        d_ref[...], mo_ref[...], vo_ref[...] = _adam_math(w_ref[...], g_ref[...], m_ref[...], v_ref[...])

    def spec(a):
        return pl.BlockSpec((tr, c), lambda i: (i, 0)) if a.ndim == 2 else pl.BlockSpec((None, tr, c), lambda i: (0, i, 0))

    return pl.pallas_call(
        body, name=name, grid=(r // tr,), in_specs=[spec(w), spec(g), spec(m), spec(v)], out_specs=[spec(w)] * 3,
        out_shape=[SDS(w.shape, F32)] * 3, compiler_params=_cparams(1))(w, g, m, v)


def adamw_blocks(w, blocks, m, v, name):
    r, c = w.shape
    tr = next((cand for cand in range(r // 32 * 16, 0, -16) if r % cand == 0), r)

    def body(w_ref, b_ref, m_ref, v_ref, g_ref, d_ref, mo_ref, vo_ref):
        g = b_ref[0].astype(F32)
        for k in range(1, N_DEV):
            g += b_ref[k].astype(F32)
        g_ref[...] = g
        d_ref[...], mo_ref[...], vo_ref[...] = _adam_math(w_ref[...], g, m_ref[...], v_ref[...])

    spec = pl.BlockSpec((tr, c), lambda i: (i, 0))
    return pl.pallas_call(
        body, name=name, grid=(r // tr,),
        in_specs=[spec, pl.BlockSpec((N_DEV, tr, c), lambda i: (0, i, 0)), spec, spec], out_specs=[spec] * 4,
        out_shape=[SDS((r, c), F32)] * 4, compiler_params=_cparams(1))(w, blocks, m, v)


def adamw_many(ws, gs, ms, vs, name):
    n = len(ws)

    def body(*refs):
        w_r, g_r, m_r, v_r = (refs[k * n:(k + 1) * n] for k in range(4))
        d_r, mo_r, vo_r = (refs[(4 + k) * n:(5 + k) * n] for k in range(3))
        for k in range(n):
            d_r[k][...], mo_r[k][...], vo_r[k][...] = _adam_math(w_r[k][...], g_r[k][...], m_r[k][...], v_r[k][...])

    shapes = [SDS(w.shape, F32) for w in ws]
    outs = pl.pallas_call(body, name=name, out_shape=shapes * 3)(*ws, *gs, *ms, *vs)
    return outs[:n], outs[n:2 * n], outs[2 * n:]


TRANSPOSED = ("ffn1_w_gate", "ffn1_w_up", "ffn2_w_gate", "ffn2_w_up", "w_in", "w_ukv", "w_uq")
GATHER_GROUPS = (("ffn1_w_gate", "ffn1_w_up"), ("ffn1_w_down",),
                 ("w_in", "w_ukv", "w_uq", "w_out", "ffn2_w_gate", "ffn2_w_up", "ffn2_w_down"))
GRAD_GROUPS = (("ffn2", ("ffn2_w_gate", "ffn2_w_up", "ffn2_w_down")), ("mixer", ("w_out", "w_in", "w_ukv", "w_uq")),
               ("ffn1_down", ("ffn1_w_down",)), ("ffn1_gate", ("ffn1_w_gate",)), ("ffn1_up", ("ffn1_w_up",)))


def _shard_view(name, w):
    return w[0].T if name in TRANSPOSED else w[0]


def _shard_unview(name, t):
    return t.T[None] if name in TRANSPOSED else t[None]


def _grad_blocks(name, gw):
    if name == "w_in":
        return _in_proj_rows_inv(gw).reshape(N_DEV, -1, D_MODEL)
    if name == "w_ukv":
        hd = MLA_HEADS * HEAD_PAD
        return jnp.concatenate([gw[:hd].reshape(MLA_HEADS, HEAD_PAD, KV_LORA)[:, :QK_NOPE],
                                gw[hd:].reshape(MLA_HEADS, V_HEAD, KV_LORA)], axis=1)
    if name == "w_uq":
        return gw.reshape(MLA_HEADS, HEAD_PAD, Q_LORA)[:, :QK_DIM]
    return gw.reshape(N_DEV, -1, D_MODEL)


def _pack_rows(arrs):
    parts = []
    for a in arrs:
        flat = a.reshape(-1).astype(F32)
        pad = (-flat.shape[0]) % D_MODEL
        if pad:
            flat = jnp.pad(flat, (0, pad))
        parts.append(flat.reshape(-1, D_MODEL))
    out = jnp.concatenate(parts, axis=0)
    pad = (-out.shape[0]) % 8
    if pad:
        out = jnp.pad(out, ((0, pad), (0, 0)))
    return out


def _unpack_rows(packed, shapes):
    out, row = [], 0
    for shp in shapes:
        n = math.prod(shp)
        nrow = -(-n // D_MODEL)
        out.append(packed[row:row + nrow].reshape(-1)[:n].reshape(shp))
        row += nrow
    return out


def _in_proj_rows(w_t):
    return jnp.concatenate([w_t[0:2560], w_t[2576:2960], w_t[2960:3216], w_t[2560:2576], w_t[3216:3248],
                            jnp.zeros((D_IN_PAD - D_IN, D_MODEL), w_t.dtype)], axis=0)


def _in_proj_rows_inv(d):
    return jnp.concatenate([d[0:2560], d[3200:3216], d[2560:2944], d[2944:3200], d[3216:3248]], axis=0)


def _rope_tables(positions):
    half = QK_ROPE // 2
    inv_freq = ROPE_THETA ** (-jnp.arange(0, QK_ROPE, 2, dtype=F32) / QK_ROPE)
    ang_t = positions[:, None, :].astype(F32) * inv_freq[:, None]
    cos_t, sin_t = jnp.cos(ang_t), jnp.sin(ang_t)
    b, _, s = ang_t.shape
    ts = min(2048, s)

    def body(c_ref, s_ref, cc_ref, sp_ref, sm_ref):
        row = lax.broadcasted_iota(I32, (half, LANES), 0)
        lane = lax.broadcasted_iota(I32, (half, LANES), 1)
        first, second = lane == QK_NOPE + row, lane == QK_NOPE + half + row

        spread = lambda x, where: _dot_01(x, where.astype(BF16), 3, TN_DIMS)
        lane1 = lax.broadcasted_iota(I32, (1, LANES), 1)
        ones = jnp.where((lane1 < QK_NOPE) | (lane1 >= QK_NOPE + QK_ROPE), 1.0, 0.0)
        cc_ref[...] = spread(c_ref[...], first | second) + ones
        sp_ref[...] = spread(s_ref[...], second)
        sm_ref[...] = -spread(s_ref[...], first)

    src = pl.BlockSpec((None, half, ts), lambda bb, i: (bb, 0, i))
    return pl.pallas_call(
        body, name="rope_tables", grid=(b, s // ts), in_specs=[src, src], out_specs=[_row(ts, LANES)] * 3,
        out_shape=[SDS((b, s, LANES), F32)] * 3, compiler_params=_cparams(2))(cos_t, sin_t)


def weight_views(gathered):
    full = lambda name: gathered[name].reshape(-1, gathered[name].shape[2])
    ukv = full("w_ukv").reshape(MLA_HEADS, QK_NOPE + V_HEAD, KV_LORA)
    wukv_t = jnp.concatenate([jnp.pad(ukv[:, :QK_NOPE], ((0, 0), (0, HEAD_PAD - QK_NOPE), (0, 0))).reshape(-1, KV_LORA),
                              ukv[:, QK_NOPE:].reshape(-1, KV_LORA)], axis=0)
    uq = full("w_uq").reshape(MLA_HEADS, QK_DIM, Q_LORA)
    wuq_t = jnp.pad(uq, ((0, 0), (0, HEAD_PAD - QK_DIM), (0, 0))).reshape(-1, Q_LORA)
    return dict(wg1_t=full("ffn1_w_gate"), wu1_t=full("ffn1_w_up"), wd1=full("ffn1_w_down"),
                wg2_t=full("ffn2_w_gate"), wu2_t=full("ffn2_w_up"), wd2=full("ffn2_w_down"),
                wo=full("w_out"), win_t=_in_proj_rows(full("w_in")), wukv_t=wukv_t, wuq_t=wuq_t)


def _ffn_bwd(tag, dxn, do, dgate, x, h, gg, uu, a, sc, norm_w, wg_t, wu_t, wd, below):
    f2 = wd.shape[0] // 2
    dwd = mm_tn(a, do, f2, D_MODEL, tag + "_dwd")
    dgg, duu = ffn_dact(do, wd, gg, uu, tag + "_dact")
    dwg_t = mm_tn(dgg, h, f2, D_MODEL, tag + "_dwg")
    dwu_t = mm_tn(duu, h, f2, D_MODEL, tag + "_dwu")
    dx, dsc, dsh, dnw, *nxt = dh_norm_bwd([dgg, duu], [wg_t, wu_t], x, dxn, norm_w, sc, tag + "_dh", below)
    return dx, (dsh, dsc, dgate), dnw, (dwg_t, dwu_t, dwd), nxt


def local_step(x, tgt, positions, mod, wv, p):
    nb, s, d = x.shape
    sh1, sc1, g1, sh2, sc2, g2, sh3, sc3, g3 = mod
    cc, sp, sm = _rope_tables(positions)
    lane_head = jnp.arange(D_SSD, dtype=I32)[None, :] // SSD_HEAD_DIM
    e_mat = (lane_head == jnp.arange(LANES, dtype=I32)[:, None]).astype(BF16)
    et_mat = e_mat.T
    rr, cl = jnp.arange(LANES, dtype=I32)[:, None], jnp.arange(LANES, dtype=I32)[None, :]
    place = ((cl == rr + (QK_NOPE - SSD_HEADS)) & (rr >= SSD_HEADS) & (rr < SSD_HEADS + QK_ROPE)).astype(F32)
    dtb = jnp.pad(p["dt_bias"], ((0, 0), (0, LANES - SSD_HEADS)))
    alog = jnp.pad(p["a_log"], ((0, 0), (0, LANES - SSD_HEADS)))
    dskip_e = jnp.repeat(p["d_skip"], SSD_HEAD_DIM, axis=1)

    h1 = norm_mod(x, p["norm_ffn1"], sc1, sh1, "ffn1_norm")
    gg1, uu1, a1 = ffn_up(h1, wv["wg1_t"], wv["wu1_t"], "ffn1_up")
    x1, o1, h2 = ffn_down(a1, wv["wd1"], x, g1, 0.5, "ffn1_down", (p["norm_mix"], sc2, sh2))
    z, u, cq, ckv, misc = in_proj(h2, wv["win_t"], "in_proj")
    xs, bm, cm_ = conv_fwd(u, p["conv_w"], p["conv_b"], "conv_fwd")
    ys, y, prev = ssd_fwd(xs, bm, cm_, misc, z, dtb, alog, dskip_e, p["ssd_norm_w"], e_mat, "ssd_fwd")
    q, k, v, vt, qn, kvn = qkv_fwd(cq, ckv, misc, cc, sp, sm, p["q_norm_w"], p["kv_norm_w"], wv["wuq_t"], wv["wukv_t"],
                               place, "qkv_fwd")
    attn, lse = flash_fwd(q, k, vt, "flash_fwd")
    x2, o2, ym, h3 = out_proj(ys, attn, p["mla_norm_w"], wv["wo"], x1, g2, (p["norm_ffn2"], sc3, sh3), "out_proj")
    gg3, uu3, a3 = ffn_up(h3, wv["wg2_t"], wv["wu2_t"], "ffn2_up")
    loss, dx3, dnfin, do3, dg3 = ffn_down_final(a3, wv["wd2"], x2, g3, 0.5, p["norm_final"], tgt, "ffn2_down_loss")

    dx2, dmod3, dnf2, (dwg2, dwu2, dwd2), (dout, dg2) = _ffn_bwd(
        "ffn2", dx3, do3, dg3, x2, h3, gg3, uu3, a3, sc3, p["norm_ffn2"], wv["wg2_t"], wv["wu2_t"], wv["wd2"],
        (o2, g2, 1.0))
    dys, dattn, dlt, dmlan = out_proj_bwd(dout, attn, p["mla_norm_w"], wv["wo"], "out_proj_bwd")
    dwo = jnp.concatenate([mm_tn(ys, dout, D_SSD, D_MODEL, "dwo_ssd"), mm_tn(ym, dout, D_SSD, D_MODEL, "dwo_mla")], axis=0)
    dxs, dbm, dcm, dz, ddt, dssdn, ddsk_lane, ddtb, dalog = ssd_bwd(
        dys, y, z, xs, bm, cm_, misc, prev, dtb, alog, dskip_e, p["ssd_norm_w"], e_mat, et_mat, "ssd_bwd")
    dq, dk, dv = flash_bwd(q, k, v, dattn, lse, dlt, "flash_bwd")
    dcq, dckv, dmisc, dqp, dkvc, dqn, dkvn = qkv_bwd(dq, dk, dv, ddt, cq, ckv, cc, sp, sm, p["q_norm_w"], p["kv_norm_w"],
                                                     wv["wuq_t"], wv["wukv_t"], place.T, "qkv_bwd")
    dwuq = mm_tn(dqp, qn, MLA_HEADS * HEAD_PAD, Q_LORA, "dwuq")
    dwukv = mm_tn(dkvc, kvn, MLA_HEADS * HEAD_PAD, KV_LORA, "dwukv")
    du, dconv = conv_bwd(dxs, dbm, dcm, u, p["conv_w"], p["conv_b"], "conv_bwd")
    dconv = conv_grads_fold(dconv, "conv_grads_fold")
    dproj = [dz, du, dcq, dckv, dmisc]
    dwin = mm_tn_blocks(dproj, h2, "dwin")
    dx1, dsc2, dsh2, dnmix, do1, dg1 = dh_norm_bwd(dproj, [wv["win_t"]], x1, dx2, p["norm_mix"], sc2, "mix_dh",
                                                   (o1, g1, 0.5))
    dx0, dmod1, dnf1, (dwg1, dwu1, dwd1), _ = _ffn_bwd(
        "ffn1", dx1, do1, dg1, x, h1, gg1, uu1, a1, sc1, p["norm_ffn1"], wv["wg1_t"], wv["wu1_t"], wv["wd1"], None)

    dmod = jnp.concatenate([*dmod1, dsh2, dsc2, dg2, *dmod3], axis=1).reshape(nb, N_MOD * d)
    return dict(
        loss=loss, dx=dx0, dmod=dmod, norm_ffn1=dnf1, norm_mix=dnmix, norm_ffn2=dnf2, norm_final=dnfin,
        ssd_norm_w=dssdn, mla_norm_w=dmlan, q_norm_w=dqn, kv_norm_w=dkvn,
        dt_bias=ddtb[:, :SSD_HEADS], a_log=dalog[:, :SSD_HEADS],
        d_skip=squeeze_heads(ddsk_lane, et_mat, "d_skip_heads")[:, :SSD_HEADS],
        conv_b=dconv[4:5], conv_w=dconv[0:4],
        gw=dict(ffn1_w_gate=dwg1, ffn1_w_up=dwu1, ffn1_w_down=dwd1, ffn2_w_gate=dwg2, ffn2_w_up=dwu2, ffn2_w_down=dwd2,
                w_out=dwo, w_in=dwin, w_ukv=dwukv, w_uq=dwuq))


def kernel(x, c, positions, w_ada, b_ada, norm_ffn1, ffn1_w_gate, ffn1_w_up, ffn1_w_down, norm_mix, w_in, conv_w, conv_b, dt_bias, a_log, d_skip, ssd_norm_w, q_norm_w, w_uq, kv_norm_w, w_ukv, mla_norm_w, w_out, norm_ffn2, ffn2_w_gate, ffn2_w_up, ffn2_w_down, norm_final, loss_target, m_w_ada, m_b_ada, m_norm_ffn1, m_ffn1_w_gate, m_ffn1_w_up, m_ffn1_w_down, m_norm_mix, m_w_in, m_conv_w, m_conv_b, m_dt_bias, m_a_log, m_d_skip, m_ssd_norm_w, m_q_norm_w, m_w_uq, m_kv_norm_w, m_w_ukv, m_mla_norm_w, m_w_out, m_norm_ffn2, m_ffn2_w_gate, m_ffn2_w_up, m_ffn2_w_down, m_norm_final, v_w_ada, v_b_ada, v_norm_ffn1, v_ffn1_w_gate, v_ffn1_w_up, v_ffn1_w_down, v_norm_mix, v_w_in, v_conv_w, v_conv_b, v_dt_bias, v_a_log, v_d_skip, v_ssd_norm_w, v_q_norm_w, v_w_uq, v_kv_norm_w, v_w_ukv, v_mla_norm_w, v_w_out, v_norm_ffn2, v_ffn2_w_gate, v_ffn2_w_up, v_ffn2_w_down, v_norm_final):
    names = ["w_ada", "b_ada", "norm_ffn1", "ffn1_w_gate", "ffn1_w_up", "ffn1_w_down", "norm_mix", "w_in", "conv_w",
             "conv_b", "dt_bias", "a_log", "d_skip", "ssd_norm_w", "q_norm_w", "w_uq", "kv_norm_w", "w_ukv",
             "mla_norm_w", "w_out", "norm_ffn2", "ffn2_w_gate", "ffn2_w_up", "ffn2_w_down", "norm_final"]
    W = dict(zip(names, (w_ada, b_ada, norm_ffn1, ffn1_w_gate, ffn1_w_up, ffn1_w_down, norm_mix, w_in, conv_w, conv_b, dt_bias, a_log, d_skip, ssd_norm_w, q_norm_w, w_uq, kv_norm_w, w_ukv, mla_norm_w, w_out, norm_ffn2, ffn2_w_gate, ffn2_w_up, ffn2_w_down, norm_final)))
    M = dict(zip(names, (m_w_ada, m_b_ada, m_norm_ffn1, m_ffn1_w_gate, m_ffn1_w_up, m_ffn1_w_down, m_norm_mix, m_w_in, m_conv_w, m_conv_b, m_dt_bias, m_a_log, m_d_skip, m_ssd_norm_w, m_q_norm_w, m_w_uq, m_kv_norm_w, m_w_ukv, m_mla_norm_w, m_w_out, m_norm_ffn2, m_ffn2_w_gate, m_ffn2_w_up, m_ffn2_w_down, m_norm_final)))
    V = dict(zip(names, (v_w_ada, v_b_ada, v_norm_ffn1, v_ffn1_w_gate, v_ffn1_w_up, v_ffn1_w_down, v_norm_mix, v_w_in, v_conv_w, v_conv_b, v_dt_bias, v_a_log, v_d_skip, v_ssd_norm_w, v_q_norm_w, v_w_uq, v_kv_norm_w, v_w_ukv, v_mla_norm_w, v_w_out, v_norm_ffn2, v_ffn2_w_gate, v_ffn2_w_up, v_ffn2_w_down, v_norm_final)))

    nb, s, d = x.shape
    me = 4 * lax.axis_index("x") + 2 * lax.axis_index("y") + lax.axis_index("c")
    n_ada = w_ada.shape[2]

    taps, n_cw = conv_w.shape[1:]
    (cg,) = all_gather8([_pack_rows([c, conv_w[0]])], "gather_c")
    c_all = cg[:, 0:nb].reshape(N_DEV * nb, d)
    conv_w_full = cg[:, nb, 0:taps * n_cw].reshape(N_DEV, taps, n_cw).transpose(1, 0, 2).reshape(taps, N_DEV * n_cw)
    shards = [[_shard_view(name, W[name]).astype(BF16) for name in group] for group in GATHER_GROUPS]
    gathered = dict(zip(GATHER_GROUPS[0], all_gather8(shards[0], "gather_w_ffn1")))

    b_ada_cols = lax.dynamic_slice(b_ada, (0, me * n_ada), (1, n_ada))
    mod_cols, c_act = adaln_fwd(c_all, w_ada[0], b_ada_cols, "adaln_fwd")
    (mod_g,) = all_gather8([mod_cols], "gather_mod")
    gathered, mod_g, shards = lax.optimization_barrier((gathered, mod_g, shards))
    gathered.update(zip(GATHER_GROUPS[1], sc_all_gather8(shards[1], "gather_w_ffn1_down", 1)))
    gathered.update(zip(GATHER_GROUPS[2], sc_all_gather8(shards[2], "gather_w_rest", 7)))
    wv = weight_views(gathered)
    mod = lax.dynamic_slice(mod_g, (0, me * nb, 0), (N_DEV, nb, n_ada)).transpose(1, 0, 2).reshape(nb, N_MOD, 1, d)
    mod = [mod[:, k] for k in range(N_MOD)]

    P = dict(W)
    P["conv_w"] = conv_w_full
    P["norm_final"] = norm_final.reshape(1, d)
    R = local_step(x, loss_target, positions, mod, wv, P)

    dmod = R["dmod"]
    partial_shapes = [(1,), (1, d), (1, d), (1, d), (1, d), (1, d), (1, d), (1, Q_LORA), (1, KV_LORA),
                      (1, SSD_HEADS), (1, SSD_HEADS), (1, SSD_HEADS), (1, D_CONV), (4, D_CONV), (1, N_MOD * d),
                      (nb, N_MOD * d)]
    partial = _pack_rows([R["loss"][0, :1], R["norm_ffn1"], R["norm_mix"], R["norm_ffn2"], R["norm_final"],
                          R["ssd_norm_w"], R["mla_norm_w"], R["q_norm_w"], R["kv_norm_w"],
                          R["dt_bias"], R["a_log"], R["d_skip"], R["conv_b"], R["conv_w"],
                          sum_rows(dmod, "dmod_rows"), dmod])
    (partial_g,) = all_gather8([partial], "gather_partials")
    (loss, g_nf1, g_nmix, g_nf2, g_nfin, g_ssdn, g_mlan, g_qn, g_kvn, g_dtb, g_alog, g_dskip, g_convb, g_convw,
     g_bada, _) = _unpack_rows(sum_blocks(partial_g, "sum_partials"), partial_shapes)
    dmod_row = sum(-(-math.prod(shp) // D_MODEL) for shp in partial_shapes[:-1])
    dmod_all = partial_g[:, dmod_row:dmod_row + nb * N_MOD].reshape(N_DEV * nb, N_MOD * d)
    g_wada = adaln_bwd(c_act, lax.dynamic_slice(dmod_all, (0, me * n_ada), (N_DEV * nb, n_ada)), "adaln_bwd")
    n_cw = conv_w.shape[2]
    G = {"w_ada": g_wada[None], "b_ada": g_bada, "norm_ffn1": g_nf1, "norm_mix": g_nmix, "norm_ffn2": g_nf2,
         "norm_final": g_nfin.reshape(d), "ssd_norm_w": g_ssdn, "mla_norm_w": g_mlan, "q_norm_w": g_qn,
         "kv_norm_w": g_kvn, "dt_bias": g_dtb, "a_log": g_alog, "d_skip": g_dskip, "conv_b": g_convb,
         "conv_w": lax.dynamic_slice(g_convw, (0, me * n_cw), (4, n_cw))[None]}

    DW, NM, NV = {}, {}, {}
    gw = R["gw"]
    for k, (tag, group) in enumerate(GRAD_GROUPS):
        send = [_grad_blocks(name, gw[name]).reshape(N_DEV, *_shard_view(name, W[name]).shape) for name in group]
        recv = sc_all_to_all8(send, "exchange_" + tag, 2 + k)
        for name, blocks in zip(group, recv):
            res = adamw_blocks(_shard_view(name, W[name]), blocks, _shard_view(name, M[name]), _shard_view(name, V[name]),
                               "adamw_" + name)
            G[name], DW[name], NM[name], NV[name] = [_shard_unview(name, t) for t in res]
    DW["w_ada"], NM["w_ada"], NV["w_ada"] = adamw(w_ada, g_wada, m_w_ada, v_w_ada, "adamw_w_ada")
    small = [n for n in names if n not in DW]
    as2d = lambda a: a.reshape(-1, a.shape[-1])
    outs = adamw_many([as2d(W[n]) for n in small], [as2d(G[n]) for n in small], [as2d(M[n]) for n in small],
                      [as2d(V[n]) for n in small], "adamw_small")
    for res, dst in zip(outs, (DW, NM, NV)):
        for n, t in zip(small, res):
            dst[n] = t.reshape(W[n].shape)
    return (loss.reshape(()), R["dx"], *[G[n] for n in names], *[DW[n] for n in names], *[NM[n] for n in names],
            *[NV[n] for n in names])
```

```python
import math

import jax
import jax.numpy as jnp
from jax import lax
from jax.experimental import pallas as pl
from jax.experimental.pallas import tpu as pltpu
from jax.experimental.pallas import tpu_sc as plsc

F32, BF16, I32 = jnp.float32, jnp.bfloat16, jnp.int32
HI = lax.Precision.HIGHEST
SDS = jax.ShapeDtypeStruct
MESH = pl.DeviceIdType.MESH

D_MODEL = 1024
D_FF = 2816
D_SSD = 1024
SSD_HEADS = 16
SSD_HEAD_DIM = 64
SSD_GROUPS = 2
SSD_STATE = 128
CHUNK = 128
MLA_HEADS = 8
QK_NOPE = 64
QK_ROPE = 32
QK_DIM = 96
V_HEAD = 128
Q_LORA = 384
KV_LORA = 256
ROPE_THETA = 10000.0
N_MOD = 9
EPS = 1e-6
D_CONV = 1536
D_IN = 3248
D_IN_PAD = 3328
HEAD_PAD = 128
N_DEV = 8
ADAM_LR, ADAM_B1, ADAM_B2, ADAM_EPS, ADAM_WD, ADAM_STEP = 0.001, 0.9, 0.999, 1e-08, 0.01, 10

SAVED_ACT = BF16
VMEM_LIMIT = 56 * 1024 * 1024
LANES = 128
NT_DIMS = (((1,), (1,)), ((), ()))
TN_DIMS = (((0,), (0,)), ((), ()))


def _cparams(n_axes):
    return pltpu.CompilerParams(dimension_semantics=("arbitrary",) * n_axes, vmem_limit_bytes=VMEM_LIMIT)


def _row(tm, d):
    return pl.BlockSpec((None, tm, d), lambda b, i: (b, i, 0))


def _bvec(d):
    return pl.BlockSpec((None, 1, d), lambda b, i: (b, 0, 0))


def _full(shape):
    n = len(shape)
    return pl.BlockSpec(shape, lambda *_: (0,) * n)


def _sigmoid(x):
    return 1.0 / (1.0 + jnp.exp(-x))


def _softplus(x):
    return jnp.maximum(x, 0.0) + jnp.log(1.0 + jnp.exp(-jnp.abs(x)))


def _rms(x):
    return lax.rsqrt(jnp.mean(x * x, axis=-1, keepdims=True) + EPS)


def _rms_bwd(dn, n, r):
    return r * (dn - n * jnp.mean(dn * n, axis=-1, keepdims=True))


def _first_step():
    return (pl.program_id(0) == 0) & (pl.program_id(1) == 0)


def _gather_copies(x_refs, out_refs, send_sems, recv_sems, local_sems):
    mx, my, mc = lax.axis_index("x"), lax.axis_index("y"), lax.axis_index("c")
    me, sibling = (mx, my, mc), (mx, my, 1 - mc)
    chips = [(1 - mx, my), (mx, 1 - my), (1 - mx, 1 - my)]

    def copy(a, k, block, to, src=None):
        rows = out_refs[a].at[4 * block[0] + 2 * block[1] + block[2]]
        return pltpu.make_async_remote_copy(
            src_ref=rows if src is None else src, dst_ref=rows,
            send_sem=send_sems.at[7 * a + k], recv_sem=recv_sems.at[7 * a + k], device_id=to, device_id_type=MESH)

    arrays = range(len(x_refs))
    mine = [pltpu.make_async_copy(x_refs[a], out_refs[a].at[4 * mx + 2 * my + mc], local_sems.at[a]) for a in arrays]
    first = [[copy(a, 0, me, sibling, src=x_refs[a])] + [copy(a, 1 + j, me, (*chip, mc), src=x_refs[a])
                                                          for j, chip in enumerate(chips)] for a in arrays]
    passed = [[copy(a, 4 + j, (*chip, mc), sibling) for j, chip in enumerate(chips)] for a in arrays]
    for a in arrays:
        mine[a].start()
        for cp in first[a]:
            cp.start()
    for a in arrays:
        for j, chip in enumerate(chips):
            copy(a, 1 + j, (*chip, mc), me).wait_recv()
            passed[a][j].start()
    for a in arrays:
        copy(a, 0, sibling, me).wait_recv()
        for j, chip in enumerate(chips):
            copy(a, 4 + j, (*chip, 1 - mc), me).wait_recv()
    for a in arrays:
        for cp in first[a] + passed[a]:
            cp.wait_send()
        mine[a].wait()


def _gather_peers():
    mx, my, mc = lax.axis_index("x"), lax.axis_index("y"), lax.axis_index("c")
    return [(mx, my, 1 - mc), (1 - mx, my, mc), (mx, 1 - my, mc), (1 - mx, 1 - my, mc)]


def _comm_scratch(n):
    return [pltpu.SemaphoreType.DMA((7 * n,)), pltpu.SemaphoreType.DMA((7 * n,)), pltpu.SemaphoreType.DMA((n,))]


def all_gather8(xs, name):
    n = len(xs)

    def body(*refs):
        _gather_copies(refs[:n], refs[n:2 * n], *refs[2 * n:])

    return pl.pallas_call(
        body, name=name,
        out_shape=[SDS((N_DEV, *x.shape), x.dtype) for x in xs],
        in_specs=[pl.BlockSpec(memory_space=pl.ANY)] * n,
        out_specs=[pl.BlockSpec(memory_space=pl.ANY)] * n,
        scratch_shapes=_comm_scratch(n),
    )(*xs)


def _exchange_peers():
    mx, my, mc = lax.axis_index("x"), lax.axis_index("y"), lax.axis_index("c")
    return [(1 - mx if rel & 4 else mx, 1 - my if rel & 2 else my, 1 - mc if rel & 1 else mc) for rel in range(1, N_DEV)]


def _exchange_copies(x_refs, out_refs, send_sems, recv_sems, local_sems):
    mx, my, mc = lax.axis_index("x"), lax.axis_index("y"), lax.axis_index("c")
    me = 4 * mx + 2 * my + mc
    copies = []
    for a, (x_ref, out_ref) in enumerate(zip(x_refs, out_refs)):
        mine = pltpu.make_async_copy(x_ref.at[me], out_ref.at[me], local_sems.at[a])
        mine.start()
        copies.append(mine)
        for k, (px, py, pc) in enumerate(_exchange_peers()):
            cp = pltpu.make_async_remote_copy(
                src_ref=x_ref.at[4 * px + 2 * py + pc], dst_ref=out_ref.at[me],
                send_sem=send_sems.at[7 * a + k], recv_sem=recv_sems.at[7 * a + k],
                device_id=(px, py, pc), device_id_type=MESH)
            cp.start()
            copies.append(cp)
    for cp in copies:
        cp.wait()


def all_to_all8(xs, name):
    n = len(xs)

    def body(*refs):
        _exchange_copies(refs[:n], refs[n:2 * n], *refs[2 * n:])

    return pl.pallas_call(
        body, name=name,
        out_shape=[SDS(x.shape, x.dtype) for x in xs],
        in_specs=[pl.BlockSpec(memory_space=pl.ANY)] * n,
        out_specs=[pl.BlockSpec(memory_space=pl.ANY)] * n,
        scratch_shapes=_comm_scratch(n),
    )(*xs)


def _sequencer_kernel(name, collective_id, n_arrays):
    return pl.kernel(
        mesh=plsc.ScalarSubcoreMesh(axis_name="seq", num_cores=1), name=name,
        scratch_types=tuple(_comm_scratch(n_arrays)),
        compiler_params=pltpu.CompilerParams(collective_id=collective_id))


def _handshake(peers):
    barrier = pltpu.get_barrier_semaphore()
    for peer in peers:
        pl.semaphore_signal(barrier, inc=1, device_id=peer, device_id_type=MESH)
    pl.semaphore_wait(barrier, len(peers))


def _hbm_refs(xs, out_shapes):
    x_refs = [jax.new_ref(x, memory_space=pltpu.MemorySpace.HBM) for x in xs]
    out_refs = [jax.empty_ref(SDS(shp, x.dtype), memory_space=pltpu.MemorySpace.HBM) for x, shp in zip(xs, out_shapes)]
    return x_refs, out_refs


def sc_all_gather8(xs, name, collective_id):
    x_refs, out_refs = _hbm_refs(xs, [(N_DEV, *x.shape) for x in xs])

    @_sequencer_kernel(name, collective_id, len(xs))
    def launch(send_sems, recv_sems, local_sems):
        _handshake(_gather_peers())
        _gather_copies(x_refs, out_refs, send_sems, recv_sems, local_sems)

    launch()
    return [ref[...] for ref in out_refs]


def sc_all_to_all8(xs, name, collective_id):
    x_refs, out_refs = _hbm_refs(xs, [x.shape for x in xs])

    @_sequencer_kernel(name, collective_id, len(xs))
    def launch(send_sems, recv_sems, local_sems):
        _handshake(_exchange_peers())
        _exchange_copies(x_refs, out_refs, send_sems, recv_sems, local_sems)

    launch()
    return [ref[...] for ref in out_refs]


def norm_mod(x, w, sc, sh, name):
    b, s, d = x.shape
    tm = min(512, s)

    def body(x_ref, w_ref, sc_ref, sh_ref, h_ref):
        xv = x_ref[...]
        n = xv * _rms(xv)
        h_ref[...] = ((n * w_ref[...]) * (1.0 + sc_ref[...]) + sh_ref[...]).astype(BF16)

    return pl.pallas_call(
        body, name=name, grid=(b, s // tm),
        in_specs=[_row(tm, d), _full((1, d)), _bvec(d), _bvec(d)],
        out_specs=_row(tm, d), out_shape=SDS((b, s, d), BF16), compiler_params=_cparams(2))(x, w, sc, sh)


def ffn_up(h, wg_t, wu_t, name):
    b, s, d = h.shape
    f = wg_t.shape[0]
    tm, tn = min(1024, s), f // 2

    def body(h_ref, wg_ref, wu_ref, s_ref, t_ref, a_ref):
        hv = h_ref[...]
        g = lax.dot_general(hv, wg_ref[...], NT_DIMS, preferred_element_type=F32)
        u = lax.dot_general(hv, wu_ref[...], NT_DIMS, preferred_element_type=F32)
        sg = _sigmoid(g)
        silu = g * sg
        s_ref[...] = silu.astype(s_ref.dtype)
        t_ref[...] = (u * (sg + silu * (1.0 - sg))).astype(t_ref.dtype)
        a_ref[...] = (silu * u).astype(BF16)

    hs = pl.BlockSpec((None, tm, d), lambda j, bb, i: (bb, i, 0))
    ws = pl.BlockSpec((tn, d), lambda j, bb, i: (j, 0))
    os_ = pl.BlockSpec((None, tm, tn), lambda j, bb, i: (bb, i, j))
    return pl.pallas_call(
        body, name=name, grid=(f // tn, b, s // tm),
        in_specs=[hs, ws, ws], out_specs=[os_, os_, os_],
        out_shape=[SDS((b, s, f), SAVED_ACT), SDS((b, s, f), SAVED_ACT), SDS((b, s, f), BF16)],
        compiler_params=_cparams(3))(h, wg_t, wu_t)


def _norm_mod_tile(xv, w_ref, sc_ref, sh_ref):
    return ((xv * _rms(xv) * w_ref[...]) * (1.0 + sc_ref[...]) + sh_ref[...]).astype(BF16)


def ffn_down(a, wd, x, gate, scale, name, above=None):
    b, s, f = a.shape
    d = wd.shape[1]
    tm = min(1024, s)

    def body(a_ref, wd_ref, x_ref, g_ref, *rest):
        xn_ref, o_ref = rest[-3:-1] if above else rest
        o = jnp.dot(a_ref[...], wd_ref[...], preferred_element_type=F32)
        xn = x_ref[...] + (scale * g_ref[...]) * o
        xn_ref[...] = xn
        o_ref[...] = o.astype(BF16)
        if above:
            rest[-1][...] = _norm_mod_tile(xn, *rest[0:3])

    extra = above is not None
    return pl.pallas_call(
        body, name=name, grid=(b, s // tm),
        in_specs=[_row(tm, f), _full((f, d)), _row(tm, d), _bvec(d)] + ([_full((1, d)), _bvec(d), _bvec(d)] if extra else []),
        out_specs=[_row(tm, d), _row(tm, d)] + ([_row(tm, d)] if extra else []),
        out_shape=[SDS((b, s, d), F32), SDS((b, s, d), BF16)] + ([SDS((b, s, d), BF16)] if extra else []),
        compiler_params=_cparams(2))(a, wd, x, gate, *(above or ()))


def ffn_down_final(a, wd, x, gate, scale, w_final, tgt, name):
    b, s, f = a.shape
    d = wd.shape[1]
    tm = min(1024, s)

    def body(a_ref, wd_ref, x_ref, g_ref, w_ref, t_ref, loss_ref, dx_ref, dw_ref, do_ref, dg_ref):
        @pl.when(_first_step())
        def _():
            loss_ref[...] = jnp.zeros_like(loss_ref)
            dw_ref[...] = jnp.zeros_like(dw_ref)

        @pl.when(pl.program_id(1) == 0)
        def _():
            dg_ref[...] = jnp.zeros_like(dg_ref)
        o = jnp.dot(a_ref[...], wd_ref[...], preferred_element_type=F32)
        sg = scale * g_ref[...]
        xv = x_ref[...] + sg * o
        r = _rms(xv)
        n = xv * r
        wv = w_ref[...]
        e = n * wv - t_ref[...]
        loss_ref[...] += jnp.sum(e * e) * (0.5 / d)
        dy = e * (1.0 / d)
        dw_ref[...] += jnp.sum(dy * n, axis=0, keepdims=True)
        dx = _rms_bwd(dy * wv, n, r)
        dx_ref[...] = dx
        do_ref[...] = (sg * dx).astype(BF16)
        dg_ref[...] += jnp.sum(scale * dx * o, axis=0, keepdims=True)

    return pl.pallas_call(
        body, name=name, grid=(b, s // tm),
        in_specs=[_row(tm, f), _full((f, d)), _row(tm, d), _bvec(d), _full((1, d)), _row(tm, d)],
        out_specs=[_full((1, LANES)), _row(tm, d), _full((1, d)), _row(tm, d), _bvec(d)],
        out_shape=[SDS((1, LANES), F32), SDS((b, s, d), F32), SDS((1, d), F32), SDS((b, s, d), BF16), SDS((b, 1, d), F32)],
        compiler_params=_cparams(2))(a, wd, x, gate, w_final, tgt)


def ffn_dact(do, wd, silu_g, u_dsilu, name):
    b, s, d = do.shape
    f = wd.shape[0]
    tm, tn = min(1024, s), f // 2

    def body(do_ref, wd_ref, s_ref, t_ref, dg_ref, du_ref):
        da = lax.dot_general(do_ref[...], wd_ref[...], NT_DIMS, preferred_element_type=F32)
        dg_ref[...] = (da * t_ref[...].astype(F32)).astype(BF16)
        du_ref[...] = (da * s_ref[...].astype(F32)).astype(BF16)

    dos = pl.BlockSpec((None, tm, d), lambda j, bb, i: (bb, i, 0))
    ws = pl.BlockSpec((tn, d), lambda j, bb, i: (j, 0))
    es = pl.BlockSpec((None, tm, tn), lambda j, bb, i: (bb, i, j))
    return pl.pallas_call(
        body, name=name, grid=(f // tn, b, s // tm),
        in_specs=[dos, ws, es, es], out_specs=[es, es],
        out_shape=[SDS((b, s, f), BF16), SDS((b, s, f), BF16)], compiler_params=_cparams(3))(do, wd, silu_g, u_dsilu)


def mm_tn(a, bm, tma, tnb, name):
    b, s, ka = a.shape
    nb = bm.shape[2]
    tk = min(2048, s)
    nk = s // tk

    def body(a_ref, b_ref, o_ref, acc):
        first = (pl.program_id(2) == 0) & (pl.program_id(3) == 0)
        last = (pl.program_id(2) == b - 1) & (pl.program_id(3) == nk - 1)
        part = lax.dot_general(a_ref[...], b_ref[...], TN_DIMS, preferred_element_type=F32)

        @pl.when(first)
        def _():
            acc[...] = part

        @pl.when(jnp.logical_not(first))
        def _():
            acc[...] += part

        @pl.when(last)
        def _():
            o_ref[...] = acc[...].astype(BF16)

    return pl.pallas_call(
        body, name=name, grid=(ka // tma, nb // tnb, b, nk),
        in_specs=[pl.BlockSpec((None, tk, tma), lambda i, j, bb, k: (bb, k, i)),
                  pl.BlockSpec((None, tk, tnb), lambda i, j, bb, k: (bb, k, j))],
        out_specs=pl.BlockSpec((tma, tnb), lambda i, j, bb, k: (i, j)),
        out_shape=SDS((ka, nb), BF16), scratch_shapes=[pltpu.VMEM((tma, tnb), F32)],
        compiler_params=_cparams(4))(a, bm)


def mm_tn_blocks(a_blocks, bm, name):
    b, s, nb = bm.shape
    widths = [a.shape[2] for a in a_blocks]
    starts = [sum(widths[:k]) for k in range(len(widths))]
    tk = min(2048 if sum(widths) <= 2048 else 1024, s)
    nk = s // tk
    n = len(a_blocks)

    def body(*refs):
        a_refs, b_ref, o_ref, acc = refs[:n], refs[n], refs[n + 1], refs[n + 2]
        first = (pl.program_id(0) == 0) & (pl.program_id(1) == 0)
        last = (pl.program_id(0) == b - 1) & (pl.program_id(1) == nk - 1)

        @pl.when(first)
        def _():
            acc[...] = jnp.zeros_like(acc)
        bv = b_ref[...]
        for a_ref, st, wd in zip(a_refs, starts, widths):
            acc[st:st + wd, :] += lax.dot_general(a_ref[...], bv, TN_DIMS, preferred_element_type=F32)

        @pl.when(last)
        def _():
            o_ref[...] = acc[...].astype(BF16)

    return pl.pallas_call(
        body, name=name, grid=(b, nk),
        in_specs=[_row(tk, wd) for wd in widths] + [_row(tk, nb)],
        out_specs=_full((sum(widths), nb)), out_shape=SDS((sum(widths), nb), BF16),
        scratch_shapes=[pltpu.VMEM((sum(widths), nb), F32)], compiler_params=_cparams(2))(*a_blocks, bm)


def _gate_bwd_specs(tm, d, b, s):
    return ([_row(tm, d), _bvec(d)], [_row(tm, d), _bvec(d)], [SDS((b, s, d), BF16), SDS((b, 1, d), F32)])


def _gate_bwd_tile(dx, scale, o_ref, g_ref, do_ref, dg_ref):
    do_ref[...] = ((scale * g_ref[...]) * dx).astype(BF16)
    dg_ref[...] += jnp.sum(scale * dx * o_ref[...].astype(F32), axis=0, keepdims=True)


def dh_norm_bwd(dys, wts, x, dxn, w, sc, name, below=None):
    b, s, d = x.shape
    tm = min(512, s)
    n_in, n_w = len(dys), len(wts)
    extra_in, extra_out, extra_shape = _gate_bwd_specs(tm, d, b, s) if below else ([], [], [])
    starts = [sum(dy.shape[2] for dy in dys[:k]) for k in range(n_in)]

    def body(*refs):
        dy_refs, w_refs = refs[:n_in], refs[n_in:n_in + n_w]
        x_ref, dxn_ref, nw_ref, sc_ref = refs[n_in + n_w:n_in + n_w + 4]
        rest = refs[n_in + n_w + 4:]
        if below:
            o_ref, g_ref, dx_ref, dsc_ref, dsh_ref, dw_ref, do_ref, dg_ref = rest
        else:
            dx_ref, dsc_ref, dsh_ref, dw_ref = rest

        @pl.when(pl.program_id(1) == 0)
        def _():
            dsc_ref[...] = jnp.zeros_like(dsc_ref)
            dsh_ref[...] = jnp.zeros_like(dsh_ref)
            if below:
                dg_ref[...] = jnp.zeros_like(dg_ref)

        @pl.when(_first_step())
        def _():
            dw_ref[...] = jnp.zeros_like(dw_ref)

        def weight(k):
            return w_refs[k][...] if n_w == n_in else w_refs[0][starts[k]:starts[k] + dys[k].shape[2], :]

        dh = jnp.dot(dy_refs[0][...], weight(0), preferred_element_type=F32)
        for k in range(1, n_in):
            dh += jnp.dot(dy_refs[k][...], weight(k), preferred_element_type=F32)
        xv = x_ref[...]
        r = _rms(xv)
        n = xv * r
        nw = nw_ref[...]
        dsc_ref[...] += jnp.sum(dh * (n * nw), axis=0, keepdims=True)
        dsh_ref[...] += jnp.sum(dh, axis=0, keepdims=True)
        dhn = dh * (1.0 + sc_ref[...])
        dw_ref[...] += jnp.sum(dhn * n, axis=0, keepdims=True)
        dx = dxn_ref[...] + _rms_bwd(dhn * nw, n, r)
        dx_ref[...] = dx
        if below:
            _gate_bwd_tile(dx, below[2], o_ref, g_ref, do_ref, dg_ref)

    resident = lambda shape: pl.BlockSpec(shape, lambda *_: (0,) * len(shape), pipeline_mode=pl.Buffered(1))
    in_specs = [_row(tm, dy.shape[2]) for dy in dys] + [resident(wt.shape) for wt in wts]
    in_specs += [_row(tm, d), _row(tm, d), _full((1, d)), _bvec(d)] + extra_in
    return pl.pallas_call(
        body, name=name, grid=(b, s // tm), in_specs=in_specs,
        out_specs=[_row(tm, d), _bvec(d), _bvec(d), _full((1, d))] + extra_out,
        out_shape=[SDS((b, s, d), F32), SDS((b, 1, d), F32), SDS((b, 1, d), F32), SDS((1, d), F32)] + extra_shape,
        compiler_params=_cparams(2))(*dys, *wts, x, dxn, w, sc, *(below[:2] if below else ()))


def in_proj(h, win_t, name):
    b, s, d = h.shape
    tm = min(512, s)
    widths = (D_SSD, D_SSD + 2 * SSD_GROUPS * SSD_STATE, Q_LORA, KV_LORA, LANES)

    def body(h_ref, w_ref, *outs):
        p = lax.dot_general(h_ref[...], w_ref[...], NT_DIMS, preferred_element_type=F32)
        off = 0
        for o_ref, wd in zip(outs, widths):
            o_ref[...] = p[:, off:off + wd]
            off += wd

    return pl.pallas_call(
        body, name=name, grid=(b, s // tm),
        in_specs=[_row(tm, d), _full(win_t.shape)],
        out_specs=[_row(tm, wd) for wd in widths],
        out_shape=[SDS((b, s, wd), F32) for wd in widths], compiler_params=_cparams(2))(h, win_t)


def _halo_prev(ts, d):
    return pl.BlockSpec((None, 8, d), lambda b, i: (b, jnp.maximum(i * (ts // 8) - 1, 0), 0))


CONV_ROWS = 32


def _conv_head(head, u_ref, up_ref, tile):
    head[0:8, :] = jnp.where(tile > 0, up_ref[...], 0.0)
    head[8:8 + CONV_ROWS, :] = u_ref[0:CONV_ROWS, :]


def _conv_windows(u_ref, head, r0):
    if r0 == 0:
        return [head[5 + k:5 + k + CONV_ROWS, :] for k in range(4)]
    return [u_ref[r0 - 3 + k:r0 - 3 + k + CONV_ROWS, :] for k in range(4)]


def _fold8(t):
    acc = t[0:8, :]
    for r in range(8, CONV_ROWS, 8):
        acc += t[r:r + 8, :]
    return acc


def conv_fwd(u, cw, cb, name):
    b, s, dc = u.shape
    ts = min(512, s)
    widths = (D_SSD, SSD_GROUPS * SSD_STATE, SSD_GROUPS * SSD_STATE)

    def body(u_ref, up_ref, w_ref, b_ref, xs_ref, bm_ref, cm_ref, head):
        _conv_head(head, u_ref, up_ref, pl.program_id(1))
        ws = [w_ref[k:k + 1, :] for k in range(4)]
        bias = b_ref[...]
        for r0 in range(0, ts, CONV_ROWS):
            taps = _conv_windows(u_ref, head, r0)
            v = bias + taps[0] * ws[0] + taps[1] * ws[1] + taps[2] * ws[2] + taps[3] * ws[3]
            y = v * _sigmoid(v)
            rs = slice(r0, r0 + CONV_ROWS)
            xs_ref[rs, :] = y[:, 0:D_SSD]
            bm_ref[rs, :] = y[:, D_SSD:D_SSD + 256]
            cm_ref[rs, :] = y[:, D_SSD + 256:D_SSD + 512]

    return pl.pallas_call(
        body, name=name, grid=(b, s // ts),
        in_specs=[_row(ts, dc), _halo_prev(ts, dc), _full((4, dc)), _full((1, dc))],
        out_specs=[_row(ts, wd) for wd in widths],
        out_shape=[SDS((b, s, wd), F32) for wd in widths],
        scratch_shapes=[pltpu.VMEM((8 + CONV_ROWS, dc), F32)], compiler_params=_cparams(2))(u, u, cw, cb)


def conv_bwd(dxs, dbm, dcm, u, cw, cb, name):
    b, s, dc = u.shape
    ts = min(512, s)
    nt = s // ts

    def body(dxs_ref, dbm_ref, dcm_ref, u_ref, up_ref, w_ref, b_ref, du_ref, dwb_ref, head, dvs):
        @pl.when(_first_step())
        def _():
            dwb_ref[...] = jnp.zeros_like(dwb_ref)

        @pl.when(pl.program_id(1) == 0)
        def _():
            dvs[ts:ts + 8, :] = jnp.zeros((8, dc), F32)
        _conv_head(head, u_ref, up_ref, nt - 1 - pl.program_id(1))
        ws = [w_ref[k:k + 1, :] for k in range(4)]
        bias = b_ref[...]
        for r0 in range(0, ts, CONV_ROWS):
            taps = _conv_windows(u_ref, head, r0)
            v = bias + taps[0] * ws[0] + taps[1] * ws[1] + taps[2] * ws[2] + taps[3] * ws[3]
            sg = _sigmoid(v)
            rs = slice(r0, r0 + CONV_ROWS)
            dy = jnp.concatenate([dxs_ref[rs, :], dbm_ref[rs, :], dcm_ref[rs, :]], axis=1)
            dv = dy * (sg * (1.0 + v * (1.0 - sg)))
            dvs[rs, :] = dv
            for k in range(4):
                dwb_ref[8 * k:8 * k + 8, :] += _fold8(dv * taps[k])
            dwb_ref[32:40, :] += _fold8(dv)
        for r0 in range(0, ts, CONV_ROWS):
            win = [dvs[r0 + 3 - k:r0 + 3 - k + CONV_ROWS, :] for k in range(4)]
            acc = win[0] * ws[0] + win[1] * ws[1] + win[2] * ws[2] + win[3] * ws[3]
            du_ref[r0:r0 + CONV_ROWS, :] = acc.astype(BF16)
        dvs[ts:ts + 8, :] = dvs[0:8, :]

    rows = lambda wd: pl.BlockSpec((None, ts, wd), lambda bb, i: (bb, nt - 1 - i, 0))
    prev = pl.BlockSpec((None, 8, dc), lambda bb, i: (bb, jnp.maximum((nt - 1 - i) * (ts // 8) - 1, 0), 0))
    return pl.pallas_call(
        body, name=name, grid=(b, nt),
        in_specs=[rows(D_SSD), rows(256), rows(256), rows(dc), prev, _full((4, dc)), _full((1, dc))],
        out_specs=[rows(dc), _full((40, dc))],
        out_shape=[SDS((b, s, dc), BF16), SDS((40, dc), F32)],
        scratch_shapes=[pltpu.VMEM((8 + CONV_ROWS, dc), F32), pltpu.VMEM((ts + 8, dc), F32)],
        compiler_params=_cparams(2))(dxs, dbm, dcm, u, u, cw, cb)


def conv_grads_fold(x, name):
    c = x.shape[1]

    def body(x_ref, o_ref):
        o_ref[...] = jnp.zeros_like(o_ref)
        for k in range(5):
            o_ref[k:k + 1, :] = jnp.sum(x_ref[8 * k:8 * k + 8, :], axis=0, keepdims=True)

    return pl.pallas_call(body, name=name, out_shape=SDS((8, c), F32))(x)


def _ssd_common(misc_ref, dtb_ref, alog_ref, e_ref):
    ln = CHUNK
    lane = lax.broadcasted_iota(I32, (ln, LANES), 1)
    lane1 = lax.broadcasted_iota(I32, (1, LANES), 1)
    pre = misc_ref[...] + dtb_ref[...]
    dt_s = jnp.where(lane < SSD_HEADS, _softplus(pre), 0.0)
    a_neg = jnp.where(lane1 < SSD_HEADS, -jnp.exp(alog_ref[...]), 0.0)
    ri = lax.broadcasted_iota(I32, (ln, ln), 0)
    ci = lax.broadcasted_iota(I32, (ln, ln), 1)
    tril = ci <= ri
    acum = jnp.dot(tril.astype(F32), dt_s * a_neg, preferred_element_type=F32, precision=HI)
    both_e = _dot_01(jnp.concatenate([dt_s, acum], axis=0), e_ref[...], 3)
    dt_e, acum_e = both_e[0:ln], both_e[ln:2 * ln]
    return dict(pre=pre, dt_s=dt_s, a_neg=a_neg, tril=tril, ri=ri, ci=ci, acum=acum, acum_t=acum.T,
                dt_e=dt_e, eac_e=jnp.exp(acum_e), del_e=jnp.exp(acum_e[ln - 1:ln, :] - acum_e))


def _dot_01(x, m01, terms, dims=(((1,), (0,)), ((), ()))):
    acc, rest = None, x
    for k in range(terms):
        part = rest.astype(BF16)
        if k + 1 < terms:
            rest = rest - part.astype(F32)
        d = lax.dot_general(part, m01, dims, preferred_element_type=F32)
        acc = d if acc is None else acc + d
    return acc


def _decay(cm, h):
    seg = cm["acum"][:, h:h + 1] - cm["acum_t"][h:h + 1, :]
    return jnp.exp(jnp.where(cm["tril"], seg, -jnp.inf))


def ssd_fwd(xs, bm, cm_, misc, z, dtb, alog, dskip_e, norm_w, e_mat, name):
    b, s, _ = xs.shape
    ln, nc = CHUNK, s // CHUNK
    gw = D_SSD // SSD_GROUPS
    hpg = SSD_HEADS // SSD_GROUPS

    def body(xs_ref, b_ref, c_ref, misc_ref, z_ref, dtb_ref, alog_ref, dsk_ref, nw_ref, e_ref,
             ys_ref, y_ref, p_ref, st, yd):
        @pl.when(pl.program_id(1) == 0)
        def _():
            st[...] = jnp.zeros_like(st)
        cm = _ssd_common(misc_ref, dtb_ref, alog_ref, e_ref)
        xsv = xs_ref[...]
        xdt = xsv * cm["dt_e"]
        xdt_b = xdt.astype(BF16)
        xd_b = (xdt * cm["del_e"]).astype(BF16)
        gam_e = cm["eac_e"][ln - 1:ln, :]
        p_ref[...] = st[...]
        groups = [slice(gw * g, gw * (g + 1)) for g in range(SSD_GROUPS)]
        heads = [slice(SSD_HEAD_DIM * h, SSD_HEAD_DIM * (h + 1)) for h in range(SSD_HEADS)]
        bgs = [b_ref[:, SSD_STATE * g:SSD_STATE * (g + 1)].astype(BF16) for g in range(SSD_GROUPS)]
        cgs = [c_ref[:, SSD_STATE * g:SSD_STATE * (g + 1)].astype(BF16) for g in range(SSD_GROUPS)]
        cbs = [lax.dot_general(cg, bg, NT_DIMS, preferred_element_type=F32) for cg, bg in zip(cgs, bgs)]
        sts = [st[:, gs] for gs in groups]
        yoff = [jnp.dot(cg, st_g.astype(BF16), preferred_element_type=F32) * cm["eac_e"][:, gs]
                for cg, st_g, gs in zip(cgs, sts, groups)]
        news = [lax.dot_general(bg, xd_b[:, gs], TN_DIMS, preferred_element_type=F32) for bg, gs in zip(bgs, groups)]
        for gs, st_g, new in zip(groups, sts, news):
            st[:, gs] = st_g * gam_e[:, gs] + new
        ms = [(cbs[h // hpg] * _decay(cm, h)).astype(BF16) for h in range(SSD_HEADS)]
        for h, hs in enumerate(heads):
            yd[:, hs] = jnp.dot(ms[h], xdt_b[:, hs], preferred_element_type=F32)
        y = yd[...] + jnp.concatenate(yoff, axis=1) + dsk_ref[...] * xsv
        y_ref[...] = y
        zz = z_ref[...]
        yg = y * (zz * _sigmoid(zz))
        outs = []
        for g in range(SSD_GROUPS):
            ygg = yg[:, gw * g:gw * (g + 1)]
            outs.append(ygg * _rms(ygg) * nw_ref[:, gw * g:gw * (g + 1)])
        ys_ref[...] = jnp.concatenate(outs, axis=1).astype(BF16)

    row = lambda d: pl.BlockSpec((None, ln, d), lambda bb, c: (bb, c, 0))
    return pl.pallas_call(
        body, name=name, grid=(b, nc),
        in_specs=[row(D_SSD), row(256), row(256), row(LANES), row(D_SSD), _full((1, LANES)), _full((1, LANES)),
                  _full((1, D_SSD)), _full((1, D_SSD)), _full((LANES, D_SSD))],
        out_specs=[row(D_SSD), row(D_SSD), pl.BlockSpec((None, None, SSD_STATE, D_SSD), lambda bb, c: (bb, c, 0, 0))],
        out_shape=[SDS((b, s, D_SSD), BF16), SDS((b, s, D_SSD), F32), SDS((b, nc, SSD_STATE, D_SSD), F32)],
        scratch_shapes=[pltpu.VMEM((SSD_STATE, D_SSD), F32), pltpu.VMEM((ln, D_SSD), F32)],
        compiler_params=_cparams(2))(xs, bm, cm_, misc, z, dtb, alog, dskip_e, norm_w, e_mat)


def ssd_bwd(dys, y, z, xs, bm, cm_, misc, prev, dtb, alog, dskip_e, norm_w, e_mat, et_mat, name):
    b, s, _ = xs.shape
    ln, nc = CHUNK, s // CHUNK
    gw = D_SSD // SSD_GROUPS
    hpg = SSD_HEADS // SSD_GROUPS

    def body(dys_ref, y_ref, z_ref, xs_ref, b_ref, c_ref, misc_ref, p_ref, dtb_ref, alog_ref, dsk_ref, nw_ref,
             e_ref, et_ref, dxs_ref, db_ref, dc_ref, dz_ref, ddt_ref, dnw_ref, ddsk_ref, ddtb_ref, dalog_ref,
             dst, dxd, dac_t):
        @pl.when(_first_step())
        def _():
            for r_ in (dnw_ref, ddsk_ref, ddtb_ref, dalog_ref):
                r_[...] = jnp.zeros_like(r_)

        @pl.when(pl.program_id(1) == 0)
        def _():
            dst[...] = jnp.zeros_like(dst)

        cm = _ssd_common(misc_ref, dtb_ref, alog_ref, e_ref)
        et = et_ref[...]
        squeeze = lambda t: _dot_01(t, et, 2)
        lane = lax.broadcasted_iota(I32, (ln, LANES), 1)
        sub = lax.broadcasted_iota(I32, (LANES, ln), 0)
        xsv = xs_ref[...]
        xdt = xsv * cm["dt_e"]
        xdt_b = xdt.astype(BF16)
        xd_b = (xdt * cm["del_e"]).astype(BF16)
        eac_e = cm["eac_e"]
        gam_e = eac_e[ln - 1:ln, :]

        yv, zz, dyo = y_ref[...], z_ref[...], dys_ref[...]
        sz = _sigmoid(zz)
        silu_z = zz * sz
        yg = yv * silu_z
        dyg, dnw = [], []
        for g in range(SSD_GROUPS):
            gs = slice(gw * g, gw * (g + 1))
            ygg = yg[:, gs]
            r = _rms(ygg)
            n = ygg * r
            dnw.append(jnp.sum(dyo[:, gs] * n, axis=0, keepdims=True))
            dyg.append(_rms_bwd(dyo[:, gs] * nw_ref[:, gs], n, r))
        dyg = jnp.concatenate(dyg, axis=1)
        dnw_ref[...] += jnp.concatenate(dnw, axis=1)
        dz_ref[...] = (dyg * yv * (sz * (1.0 + zz * (1.0 - sz)))).astype(BF16)
        dy = dyg * silu_z
        ddsk_ref[...] += jnp.sum(dy * xsv, axis=0, keepdims=True)
        dy_b = dy.astype(BF16)

        dacum = jnp.zeros((ln, LANES), F32)
        dac_t[...] = jnp.zeros_like(dac_t)
        w1, dgam = [], []
        for g in range(SSD_GROUPS):
            gs = slice(gw * g, gw * (g + 1))
            ss = slice(SSD_STATE * g, SSD_STATE * (g + 1))
            bg = b_ref[:, ss].astype(BF16)
            cg = c_ref[:, ss].astype(BF16)
            cb = lax.dot_general(cg, bg, NT_DIMS, preferred_element_type=F32)
            pt = p_ref[:, gs]
            pt_b = pt.astype(BF16)
            dst_g = dst[:, gs]
            dst_b = dst_g.astype(BF16)
            edy = (dy[:, gs] * eac_e[:, gs]).astype(BF16)
            dcg = lax.dot_general(edy, pt_b, NT_DIMS, preferred_element_type=F32)
            dpt = lax.dot_general(cg, edy, TN_DIMS, preferred_element_type=F32)
            yoff = jnp.dot(cg, pt_b, preferred_element_type=F32) * eac_e[:, gs]
            dxd_g = jnp.dot(bg, dst_b, preferred_element_type=F32)
            dbg = lax.dot_general(xd_b[:, gs], dst_b, NT_DIMS, preferred_element_type=F32)
            ddel = dxd_g * xdt[:, gs] * cm["del_e"][:, gs]
            w1.append(dy[:, gs] * yoff - ddel)
            dgam.append(jnp.sum(ddel, axis=0, keepdims=True) + jnp.sum(dst_g * pt, axis=0, keepdims=True) * gam_e[:, gs])
            dxd[:, gs] = dxd_g * cm["del_e"][:, gs]
            dst[:, gs] = dst_g * gam_e[:, gs] + dpt
            dcb = jnp.zeros((ln, ln), F32)
            for j in range(hpg):
                h = hpg * g + j
                hs = slice(SSD_HEAD_DIM * h, SSD_HEAD_DIM * (h + 1))
                lam = _decay(cm, h)
                m = cb * lam
                dm = lax.dot_general(dy_b[:, hs], xdt_b[:, hs], NT_DIMS, preferred_element_type=F32)
                dxd[:, hs] += lax.dot_general(m.astype(BF16), dy_b[:, hs], TN_DIMS, preferred_element_type=F32)
                dcb += dm * lam
                wl = dm * m
                dacum += jnp.where(lane == h, jnp.sum(wl, axis=1, keepdims=True), 0.0)
                dac_t[...] -= jnp.where(sub == h, jnp.sum(wl, axis=0, keepdims=True), 0.0)
            dcb_b = dcb.astype(BF16)
            dc_ref[:, ss] = dcg + jnp.dot(dcb_b, bg, preferred_element_type=F32)
            db_ref[:, ss] = dbg + lax.dot_general(dcb_b, cg, TN_DIMS, preferred_element_type=F32)

        dxdt = dxd[...]
        dxs_ref[...] = dy * dsk_ref[...] + dxdt * cm["dt_e"]
        dacum += squeeze(jnp.concatenate(w1, axis=1)) + dac_t[...].T
        dlast = squeeze(jnp.broadcast_to(jnp.concatenate(dgam, axis=1), (8, D_SSD)))[0:1, :]
        dacum += jnp.where(lax.broadcasted_iota(I32, (ln, LANES), 0) == ln - 1, dlast, 0.0)
        triu = (cm["ci"] >= cm["ri"]).astype(F32)
        da = jnp.dot(triu, dacum, preferred_element_type=F32, precision=HI)
        ddt = da * cm["a_neg"] + squeeze(dxdt * xsv)
        dalog_ref[...] += jnp.sum(da * cm["dt_s"], axis=0, keepdims=True) * cm["a_neg"]
        ddt_raw = jnp.where(lane < SSD_HEADS, ddt * _sigmoid(cm["pre"]), 0.0)
        ddt_ref[...] = ddt_raw
        ddtb_ref[...] += jnp.sum(ddt_raw, axis=0, keepdims=True)

    row = lambda d: pl.BlockSpec((None, ln, d), lambda bb, c: (bb, nc - 1 - c, 0))
    return pl.pallas_call(
        body, name=name, grid=(b, nc),
        in_specs=[row(D_SSD), row(D_SSD), row(D_SSD), row(D_SSD), row(256), row(256), row(LANES),
                  pl.BlockSpec((None, None, SSD_STATE, D_SSD), lambda bb, c: (bb, nc - 1 - c, 0, 0)),
                  _full((1, LANES)), _full((1, LANES)), _full((1, D_SSD)), _full((1, D_SSD)),
                  _full((LANES, D_SSD)), _full((D_SSD, LANES))],
        out_specs=[row(D_SSD), row(256), row(256), row(D_SSD), row(LANES),
                   _full((1, D_SSD)), _full((1, D_SSD)), _full((1, LANES)), _full((1, LANES))],
        out_shape=[SDS((b, s, D_SSD), F32), SDS((b, s, 256), F32), SDS((b, s, 256), F32), SDS((b, s, D_SSD), BF16),
                   SDS((b, s, LANES), F32), SDS((1, D_SSD), F32), SDS((1, D_SSD), F32), SDS((1, LANES), F32),
                   SDS((1, LANES), F32)],
        scratch_shapes=[pltpu.VMEM((SSD_STATE, D_SSD), F32), pltpu.VMEM((ln, D_SSD), F32), pltpu.VMEM((LANES, ln), F32)],
        compiler_params=_cparams(2))(dys, y, z, xs, bm, cm_, misc, prev, dtb, alog, dskip_e, norm_w, e_mat, et_mat)


def _rope(xv, cc, sp, sm):
    n = xv.shape[1]
    return xv * cc + pltpu.roll(xv, 16, 1) * sp + pltpu.roll(xv, n - 16, 1) * sm


def _rope_bwd(dy, cc, sp, sm):
    n = dy.shape[1]
    return dy * cc + pltpu.roll(dy * sp, n - 16, 1) + pltpu.roll(dy * sm, 16, 1)


def _tile8(t):
    return jnp.concatenate([t] * MLA_HEADS, axis=1)


def qkv_fwd(cq, ckv, misc, cc, sp, sm, qnw, kvnw, wuq_t, wukv_t, place, name):
    b, s, _ = cq.shape
    tm = _att_tile(s)
    hd = MLA_HEADS * HEAD_PAD

    def body(cq_ref, ckv_ref, misc_ref, cc_ref, sp_ref, sm_ref, qnw_ref, kvnw_ref, wq_ref, wkv_ref, pl_ref,
             q_ref, k_ref, v_ref, vt_ref, qn_ref, kvn_ref):
        cqv, ckvv = cq_ref[...], ckv_ref[...]
        qn = (cqv * _rms(cqv) * qnw_ref[...]).astype(BF16)
        kvn = (ckvv * _rms(ckvv) * kvnw_ref[...]).astype(BF16)
        qn_ref[...] = qn
        kvn_ref[...] = kvn
        cc1, sp1, sm1 = cc_ref[...], sp_ref[...], sm_ref[...]
        q = lax.dot_general(qn, wq_ref[...], NT_DIMS, preferred_element_type=F32)
        q_ref[...] = _rope(q, _tile8(cc1), _tile8(sp1), _tile8(sm1)).astype(BF16)
        kv = lax.dot_general(kvn, wkv_ref[...], NT_DIMS, preferred_element_type=F32)
        kr = jnp.dot(misc_ref[...], pl_ref[...], preferred_element_type=F32, precision=HI)
        kr = _rope(kr, cc1, sp1, sm1)
        k_ref[...] = (kv[:, 0:hd] + _tile8(kr)).astype(BF16)
        v_ref[...] = kv[:, hd:2 * hd].astype(BF16)
        for h in range(MLA_HEADS):
            vt_ref[h] = kv[:, hd + HEAD_PAD * h:hd + HEAD_PAD * (h + 1)].T.astype(BF16)

    return pl.pallas_call(
        body, name=name, grid=(b, s // tm),
        in_specs=[_row(tm, Q_LORA), _row(tm, KV_LORA), _row(tm, LANES), _row(tm, LANES), _row(tm, LANES), _row(tm, LANES),
                  _full((1, Q_LORA)), _full((1, KV_LORA)), _full(wuq_t.shape), _full(wukv_t.shape), _full((LANES, LANES))],
        out_specs=[_row(tm, hd), _row(tm, hd), _row(tm, hd),
                   pl.BlockSpec((None, MLA_HEADS, None, HEAD_PAD, tm), lambda bb, i: (bb, 0, i, 0, 0)),
                   _row(tm, Q_LORA), _row(tm, KV_LORA)],
        out_shape=[SDS((b, s, hd), BF16)] * 3 + [SDS((b, MLA_HEADS, s // tm, HEAD_PAD, tm), BF16),
                                                 SDS((b, s, Q_LORA), BF16), SDS((b, s, KV_LORA), BF16)],
        compiler_params=_cparams(2))(cq, ckv, misc, cc, sp, sm, qnw, kvnw, wuq_t, wukv_t, place)


def qkv_bwd(dq, dk, dv, ddt, cq, ckv, cc, sp, sm, qnw, kvnw, wuq_t, wukv_t, place_t, name):
    b, s, _ = cq.shape
    tm = min(512, s)
    hd = MLA_HEADS * HEAD_PAD

    def body(dq_ref, dk_ref, dv_ref, ddt_ref, cq_ref, ckv_ref, cc_ref, sp_ref, sm_ref, qnw_ref, kvnw_ref,
             wq_ref, wkv_ref, plt_ref, dcq_ref, dckv_ref, dmisc_ref, dqp_ref, dkv_ref, dqnw_ref, dkvnw_ref):
        @pl.when(_first_step())
        def _():
            dqnw_ref[...] = jnp.zeros_like(dqnw_ref)
            dkvnw_ref[...] = jnp.zeros_like(dkvnw_ref)
        cc1, sp1, sm1 = cc_ref[...], sp_ref[...], sm_ref[...]
        dqp = _rope_bwd(dq_ref[...].astype(F32), _tile8(cc1), _tile8(sp1), _tile8(sm1)).astype(BF16)
        dqp_ref[...] = dqp
        dkv_b = jnp.concatenate([dk_ref[...], dv_ref[...]], axis=1)
        dkf = dk_ref[...].astype(F32)
        dkv_ref[...] = dkv_b
        dkr = dkf[:, 0:HEAD_PAD]
        for h in range(1, MLA_HEADS):
            dkr += dkf[:, HEAD_PAD * h:HEAD_PAD * (h + 1)]
        dkr = _rope_bwd(dkr, cc1, sp1, sm1)
        dmisc_ref[...] = (jnp.dot(dkr, plt_ref[...], preferred_element_type=F32, precision=HI) + ddt_ref[...]).astype(BF16)

        def norm_bwd(dn_w, xv, w_ref, dw_ref, dx_ref):
            r = _rms(xv)
            n = xv * r
            dw_ref[...] += jnp.sum(dn_w * n, axis=0, keepdims=True)
            dx_ref[...] = _rms_bwd(dn_w * w_ref[...], n, r).astype(BF16)

        norm_bwd(jnp.dot(dqp, wq_ref[...], preferred_element_type=F32), cq_ref[...], qnw_ref, dqnw_ref, dcq_ref)
        norm_bwd(jnp.dot(dkv_b, wkv_ref[...], preferred_element_type=F32), ckv_ref[...], kvnw_ref, dkvnw_ref, dckv_ref)

    return pl.pallas_call(
        body, name=name, grid=(b, s // tm),
        in_specs=[_row(tm, hd), _row(tm, hd), _row(tm, hd), _row(tm, LANES), _row(tm, Q_LORA), _row(tm, KV_LORA),
                  _row(tm, LANES), _row(tm, LANES), _row(tm, LANES), _full((1, Q_LORA)), _full((1, KV_LORA)),
                  _full(wuq_t.shape), _full(wukv_t.shape), _full((LANES, LANES))],
        out_specs=[_row(tm, Q_LORA), _row(tm, KV_LORA), _row(tm, LANES), _row(tm, hd), _row(tm, 2 * hd),
                   _full((1, Q_LORA)), _full((1, KV_LORA))],
        out_shape=[SDS((b, s, Q_LORA), BF16), SDS((b, s, KV_LORA), BF16), SDS((b, s, LANES), BF16),
                   SDS((b, s, hd), BF16), SDS((b, s, 2 * hd), BF16), SDS((1, Q_LORA), F32), SDS((1, KV_LORA), F32)],
        compiler_params=_cparams(2))(dq, dk, dv, ddt, cq, ckv, cc, sp, sm, qnw, kvnw, wuq_t, wukv_t, place_t)


ATT_SCALE = 1.0 / math.sqrt(QK_DIM)
LOG2E = math.log2(math.e)
ATT_SCALE_LOG2E = ATT_SCALE * LOG2E


ATT_HEADS_PER_STEP = 4
ATT_HEADS_PER_STEP_BWD = 2


def _att_tile(s):
    return min(512, s)


def flash_fwd(q, k, vt, name):
    b, s, hd = q.shape
    t = _att_tile(s)
    nb = s // t
    th = t // 2

    hps = ATT_HEADS_PER_STEP
    hw = hps * HEAD_PAD

    def body(q_ref, k_ref, vt_ref, o_ref, lse_ref, m_s, l_s, acc):
        i = pl.program_id(2)
        m_s[...] = jnp.full_like(m_s, -jnp.inf)
        l_s[...] = jnp.zeros_like(l_s)
        acc[...] = jnp.zeros_like(acc)

        def update(j, diagonal):
            chains = [(hh, half) for hh in range(hps) for half in range(2)]
            lanes = lambda hh: slice(HEAD_PAD * hh, HEAD_PAD * (hh + 1))
            cols = lambda half: slice(th * half, th * (half + 1))
            sts = {}
            nkeys = lambda half: th if diagonal and half == 0 else t
            for hh, half in chains:
                kr = pl.ds(pl.multiple_of(j * t, t), nkeys(half))
                st = lax.dot_general(k_ref[kr, lanes(hh)], q_ref[cols(half), lanes(hh)], NT_DIMS,
                                     preferred_element_type=F32)
                if diagonal:
                    row = lax.broadcasted_iota(I32, (nkeys(half), th), 0)
                    col = lax.broadcasted_iota(I32, (nkeys(half), th), 1) + th * half
                    st = jnp.where(row <= col, st, -jnp.inf)
                sts[hh, half] = st
            pts, alphas = {}, {}
            for hh, half in chains:
                st, cs = sts[hh, half], cols(half)
                m_prev = m_s[hh, :, cs]
                m_new = jnp.maximum(m_prev, jnp.max(st, axis=0, keepdims=True))
                alpha = jnp.exp2((m_prev - m_new) * ATT_SCALE_LOG2E)
                pt = jnp.exp2((st - m_new) * ATT_SCALE_LOG2E)
                l_s[hh, :, cs] = alpha * l_s[hh, :, cs] + jnp.sum(pt, axis=0, keepdims=True)
                m_s[hh, :, cs] = m_new
                pts[hh, half], alphas[hh, half] = pt.astype(BF16), alpha
            for hh, half in chains:
                cs = cols(half)
                acc[hh, :, cs] = alphas[hh, half] * acc[hh, :, cs] + jnp.dot(
                    vt_ref[hh, j, :, 0:nkeys(half)], pts[hh, half], preferred_element_type=F32)

        def step(j, carry):
            update(j, False)
            return carry

        lax.fori_loop(0, i, step, 0)
        update(i, True)
        for hh in range(hps):
            o_ref[:, HEAD_PAD * hh:HEAD_PAD * (hh + 1)] = (acc[hh] / l_s[hh]).T
            lse_ref[hh] = m_s[hh] * ATT_SCALE + jnp.log(l_s[hh])

    qs = pl.BlockSpec((None, t, hw), lambda bb, h, i: (bb, i, h))
    ks = pl.BlockSpec((None, s, hw), lambda bb, h, i: (bb, 0, h))
    vs = pl.BlockSpec((None, hps, nb, HEAD_PAD, t), lambda bb, h, i: (bb, h, 0, 0, 0))
    ls = pl.BlockSpec((None, hps, None, 1, t), lambda bb, h, i: (bb, h, i, 0, 0))
    return pl.pallas_call(
        body, name=name, grid=(b, MLA_HEADS // hps, nb),
        in_specs=[qs, ks, vs], out_specs=[qs, ls],
        out_shape=[SDS((b, s, hd), F32), SDS((b, MLA_HEADS, nb, 1, t), F32)],
        scratch_shapes=[pltpu.VMEM((hps, 1, t), F32), pltpu.VMEM((hps, 1, t), F32), pltpu.VMEM((hps, HEAD_PAD, t), F32)],
        compiler_params=_cparams(3))(q, k, vt)


def flash_bwd(q, k, v, do, lse, dlt, name):
    b, s, hd = q.shape
    t = _att_tile(s)
    nb = s // t
    th = t // 2
    lse_r = lse
    dlt_r = dlt.reshape(b, MLA_HEADS, nb, 1, t)

    hps = ATT_HEADS_PER_STEP_BWD
    hw = hps * HEAD_PAD

    def body(q_ref, k_ref, v_ref, do_ref, lse_ref, dlt_ref, dq_ref, dk_ref, dv_ref, dq_s, dk_s, dv_s):
        dq_s[...] = jnp.zeros_like(dq_s)
        dk_s[...] = jnp.zeros_like(dk_s)
        dv_s[...] = jnp.zeros_like(dv_s)

        def tile(j, i, diagonal):
            chains = [(hh, half) for hh in range(hps) for half in range(2)]
            lanes = lambda hh: slice(HEAD_PAD * hh, HEAD_PAD * (hh + 1))
            keys = lambda half: pl.ds(pl.multiple_of(j * t + th * half, th), th)
            q0 = lambda half: th if diagonal and half == 1 else 0
            qsel = lambda half: pl.ds(pl.multiple_of(i * t + q0(half), th), t - q0(half))
            sts, dpts = {}, {}
            for hh, half in chains:
                ls_, ks, qs, nq = lanes(hh), keys(half), qsel(half), t - q0(half)
                st = lax.dot_general(k_ref[ks, ls_], q_ref[qs, ls_], NT_DIMS, preferred_element_type=F32)
                if diagonal:
                    row = lax.broadcasted_iota(I32, (th, nq), 0) + th * half
                    col = lax.broadcasted_iota(I32, (th, nq), 1) + q0(half)
                    st = jnp.where(row <= col, st, -jnp.inf)
                sts[hh, half] = st
                dpts[hh, half] = lax.dot_general(v_ref[ks, ls_], do_ref[qs, ls_], NT_DIMS, preferred_element_type=F32)
            pts, dsts = {}, {}
            for hh, half in chains:
                qcols = slice(q0(half), t)
                pt = jnp.exp2(sts[hh, half] * ATT_SCALE_LOG2E - lse_ref[hh, i][:, qcols] * LOG2E)
                pts[hh, half] = pt.astype(BF16)
                dsts[hh, half] = (pt * (dpts[hh, half] - dlt_ref[hh, i][:, qcols])).astype(BF16)
            for hh, half in chains:
                ls_, ks, qs = lanes(hh), keys(half), qsel(half)
                dv_s[ks, ls_] += jnp.dot(pts[hh, half], do_ref[qs, ls_], preferred_element_type=F32)
                dk_s[ks, ls_] += jnp.dot(dsts[hh, half], q_ref[qs, ls_], preferred_element_type=F32)
                dq_s[qs, ls_] += lax.dot_general(dsts[hh, half], k_ref[ks, ls_], TN_DIMS, preferred_element_type=F32)

        def key_tile(j, carry):
            tile(j, j, True)

            def query_tile(i, c2):
                tile(j, i, False)
                return c2

            lax.fori_loop(j + 1, nb, query_tile, 0)
            return carry

        lax.fori_loop(0, nb, key_tile, 0)
        dq_ref[...] = (dq_s[...] * ATT_SCALE).astype(BF16)
        dk_ref[...] = (dk_s[...] * ATT_SCALE).astype(BF16)
        dv_ref[...] = dv_s[...].astype(BF16)

    hs = pl.BlockSpec((None, s, hw), lambda bb, h: (bb, 0, h))
    ls = pl.BlockSpec((None, hps, nb, 1, t), lambda bb, h: (bb, h, 0, 0, 0))
    return pl.pallas_call(
        body, name=name, grid=(b, MLA_HEADS // hps),
        in_specs=[hs, hs, hs, hs, ls, ls], out_specs=[hs, hs, hs],
        out_shape=[SDS((b, s, hd), BF16)] * 3, scratch_shapes=[pltpu.VMEM((s, hw), F32)] * 3,
        compiler_params=_cparams(2))(q, k, v, do, lse_r, dlt_r)


def out_proj(ys, attn, mnw, wo, x, gate, above, name):
    b, s, d = x.shape
    tm = min(512, s)

    def body(ys_ref, at_ref, mnw_ref, wo_ref, x_ref, g_ref, nw_ref, sc_ref, sh_ref, xn_ref, o_ref, ym_ref, h_ref):
        av = at_ref[...]
        ym = (av * _rms(av) * mnw_ref[...]).astype(BF16)
        ym_ref[...] = ym
        o = jnp.dot(ys_ref[...], wo_ref[0:D_SSD, :], preferred_element_type=F32)
        o += jnp.dot(ym, wo_ref[D_SSD:2 * D_SSD, :], preferred_element_type=F32)
        xn = x_ref[...] + g_ref[...] * o
        xn_ref[...] = xn
        o_ref[...] = o.astype(BF16)
        h_ref[...] = _norm_mod_tile(xn, nw_ref, sc_ref, sh_ref)

    return pl.pallas_call(
        body, name=name, grid=(b, s // tm),
        in_specs=[_row(tm, D_SSD), _row(tm, D_SSD), _full((1, D_SSD)), _full(wo.shape), _row(tm, d), _bvec(d),
                  _full((1, d)), _bvec(d), _bvec(d)],
        out_specs=[_row(tm, d), _row(tm, d), _row(tm, D_SSD), _row(tm, d)],
        out_shape=[SDS((b, s, d), F32), SDS((b, s, d), BF16), SDS((b, s, D_SSD), BF16), SDS((b, s, d), BF16)],
        compiler_params=_cparams(2))(ys, attn, mnw, wo, x, gate, *above)


def out_proj_bwd(dout, attn, mnw, wo, name):
    b, s, d = dout.shape
    tm = min(512, s)

    def body(do_ref, at_ref, mnw_ref, wo_ref, dys_ref, dat_ref, dlt_ref, dw_ref):
        lane = lax.broadcasted_iota(I32, (tm, LANES), 1)
        @pl.when(_first_step())
        def _():
            dw_ref[...] = jnp.zeros_like(dw_ref)
        dov = do_ref[...]
        dys_ref[...] = lax.dot_general(dov, wo_ref[0:D_SSD, :], NT_DIMS, preferred_element_type=F32)
        dym = lax.dot_general(dov, wo_ref[D_SSD:2 * D_SSD, :], NT_DIMS, preferred_element_type=F32)
        av = at_ref[...]
        r = _rms(av)
        n = av * r
        dw_ref[...] += jnp.sum(dym * n, axis=0, keepdims=True)
        dat = _rms_bwd(dym * mnw_ref[...], n, r)
        dat_ref[...] = dat.astype(BF16)
        prod = dat * av
        cols = jnp.zeros((tm, LANES), F32)
        for h in range(MLA_HEADS):
            cols += jnp.where(lane == h, jnp.sum(prod[:, HEAD_PAD * h:HEAD_PAD * (h + 1)], axis=1, keepdims=True), 0.0)
        dlt_ref[...] = cols.T[0:MLA_HEADS, :]

    return pl.pallas_call(
        body, name=name, grid=(b, s // tm),
        in_specs=[_row(tm, d), _row(tm, D_SSD), _full((1, D_SSD)), _full(wo.shape)],
        out_specs=[_row(tm, D_SSD), _row(tm, D_SSD),
                   pl.BlockSpec((None, MLA_HEADS, tm), lambda bb, i: (bb, 0, i)), _full((1, D_SSD))],
        out_shape=[SDS((b, s, D_SSD), F32), SDS((b, s, D_SSD), BF16), SDS((b, MLA_HEADS, s), F32),
                   SDS((1, D_SSD), F32)],
        compiler_params=_cparams(2))(dout, attn, mnw, wo)


def adaln_fwd(c_all, w_ada, b_ada, name):
    nb, d = c_all.shape
    n = w_ada.shape[1]

    def body(c_ref, w_ref, b_ref, m_ref, ca_ref):
        cv = c_ref[...]
        ca = (cv * _sigmoid(cv)).astype(BF16)
        ca_ref[...] = ca
        m_ref[...] = jnp.dot(ca, w_ref[...].astype(BF16), preferred_element_type=F32) + b_ref[...]

    return pl.pallas_call(
        body, name=name, out_shape=[SDS((nb, n), F32), SDS((nb, d), BF16)],
        compiler_params=pltpu.CompilerParams(vmem_limit_bytes=VMEM_LIMIT))(c_all, w_ada, b_ada)


def adaln_bwd(c_act, dmod_cols, name):
    d, n = c_act.shape[1], dmod_cols.shape[1]

    def body(c_ref, dm_ref, gw_ref):
        gw_ref[...] = lax.dot_general(c_ref[...], dm_ref[...].astype(BF16), TN_DIMS, preferred_element_type=F32)

    return pl.pallas_call(
        body, name=name, out_shape=SDS((d, n), F32),
        compiler_params=pltpu.CompilerParams(vmem_limit_bytes=VMEM_LIMIT))(c_act, dmod_cols)


def sum_rows(x, name):
    def body(x_ref, o_ref):
        o_ref[...] = jnp.sum(x_ref[...], axis=0, keepdims=True)
    return pl.pallas_call(body, name=name, out_shape=SDS((1, x.shape[1]), F32))(x)


def squeeze_heads(x, et_mat, name):
    def body(x_ref, et_ref, o_ref):
        xv = jnp.broadcast_to(x_ref[...], (8, x.shape[1]))
        o_ref[...] = _dot_01(xv, et_ref[...], 3)[0:1, :]
    return pl.pallas_call(body, name=name, out_shape=SDS((1, LANES), F32))(x, et_mat)


def sum_blocks(x, name):
    n, r, c = x.shape
    tr = next(cand for cand in (256, 128, 64, 32, 16, 8) if r % cand == 0)

    def body(x_ref, o_ref):
        acc = x_ref[0].astype(F32)
        for k in range(1, n):
            acc += x_ref[k].astype(F32)
        o_ref[...] = acc

    return pl.pallas_call(
        body, name=name, grid=(r // tr,), in_specs=[pl.BlockSpec((n, tr, c), lambda i: (0, i, 0))],
        out_specs=pl.BlockSpec((tr, c), lambda i: (i, 0)), out_shape=SDS((r, c), F32),
        compiler_params=_cparams(1))(x)


def _adam_math(w, g, m, v):
    m = ADAM_B1 * m + (1.0 - ADAM_B1) * g
    v = ADAM_B2 * v + (1.0 - ADAM_B2) * (g * g)
    m_hat = m / (1.0 - ADAM_B1 ** ADAM_STEP)
    v_hat = v / (1.0 - ADAM_B2 ** ADAM_STEP)
    return -ADAM_LR * (m_hat / (jnp.sqrt(v_hat) + ADAM_EPS) + ADAM_WD * w), m, v


def adamw(w, g, m, v, name):
    r, c = w.shape[-2:]
    tr = r
    for cand in (512, 256, 128, 64, 32, 16, 8):
        if r % cand == 0 and cand * c * 4 <= 2 * 1024 * 1024:
            tr = cand
            break

    def body(w_ref, g_ref, m_ref, v_ref, d_ref, mo_ref, vo_ref):
        d_ref[...], mo_ref[...], vo_ref[...] = _adam_math(w_ref[...], g_ref[...], m_ref[...], v_ref[...])

    def spec(a):
        return pl.BlockSpec((tr, c), lambda i: (i, 0)) if a.ndim == 2 else pl.BlockSpec((None, tr, c), lambda i: (0, i, 0))

    return pl.pallas_call(
        body, name=name, grid=(r // tr,), in_specs=[spec(w), spec(g), spec(m), spec(v)], out_specs=[spec(w)] * 3,
        out_shape=[SDS(w.shape, F32)] * 3, compiler_params=_cparams(1))(w, g, m, v)


def adamw_blocks(w, blocks, m, v, name):
    r, c = w.shape
    tr = next((cand for cand in range(r // 32 * 16, 0, -16) if r % cand == 0), r)

    def body(w_ref, b_ref, m_ref, v_ref, g_ref, d_ref, mo_ref, vo_ref):
        g = b_ref[0].astype(F32)
        for k in range(1, N_DEV):
            g += b_ref[k].astype(F32)
        g_ref[...] = g
        d_ref[...], mo_ref[...], vo_ref[...] = _adam_math(w_ref[...], g, m_ref[...], v_ref[...])

    spec = pl.BlockSpec((tr, c), lambda i: (i, 0))
    return pl.pallas_call(
        body, name=name, grid=(r // tr,),
        in_specs=[spec, pl.BlockSpec((N_DEV, tr, c), lambda i: (0, i, 0)), spec, spec], out_specs=[spec] * 4,
        out_shape=[SDS((r, c), F32)] * 4, compiler_params=_cparams(1))(w, blocks, m, v)


def adamw_many(ws, gs, ms, vs, name):
    n = len(ws)

    def body(*refs):
        w_r, g_r, m_r, v_r = (refs[k * n:(k + 1) * n] for k in range(4))
        d_r, mo_r, vo_r = (refs[(4 + k) * n:(5 + k) * n] for k in range(3))
        for k in range(n):
            d_r[k][...], mo_r[k][...], vo_r[k][...] = _adam_math(w_r[k][...], g_r[k][...], m_r[k][...], v_r[k][...])

    shapes = [SDS(w.shape, F32) for w in ws]
    outs = pl.pallas_call(body, name=name, out_shape=shapes * 3)(*ws, *gs, *ms, *vs)
    return outs[:n], outs[n:2 * n], outs[2 * n:]


TRANSPOSED = ("ffn1_w_gate", "ffn1_w_up", "ffn2_w_gate", "ffn2_w_up", "w_in", "w_ukv", "w_uq")
GATHER_GROUPS = (("ffn1_w_gate", "ffn1_w_up"), ("ffn2_w_gate", "ffn2_w_up", "ffn2_w_down"),
                 ("ffn1_w_down", "w_in", "w_ukv", "w_uq", "w_out"))
GRAD_GROUPS = (("ffn2", ("ffn2_w_gate", "ffn2_w_up", "ffn2_w_down")), ("mixer", ("w_out", "w_in", "w_ukv", "w_uq")),
               ("ffn1_down", ("ffn1_w_down",)), ("ffn1_gate", ("ffn1_w_gate",)), ("ffn1_up", ("ffn1_w_up",)))


def _shard_view(name, w):
    return w[0].T if name in TRANSPOSED else w[0]


def _shard_unview(name, t):
    return t.T[None] if name in TRANSPOSED else t[None]


def _grad_blocks(name, gw):
    if name == "w_in":
        return _in_proj_rows_inv(gw).reshape(N_DEV, -1, D_MODEL)
    if name == "w_ukv":
        hd = MLA_HEADS * HEAD_PAD
        return jnp.concatenate([gw[:hd].reshape(MLA_HEADS, HEAD_PAD, KV_LORA)[:, :QK_NOPE],
                                gw[hd:].reshape(MLA_HEADS, V_HEAD, KV_LORA)], axis=1)
    if name == "w_uq":
        return gw.reshape(MLA_HEADS, HEAD_PAD, Q_LORA)[:, :QK_DIM]
    return gw.reshape(N_DEV, -1, D_MODEL)


def _pack_rows(arrs):
    parts = []
    for a in arrs:
        flat = a.reshape(-1).astype(F32)
        pad = (-flat.shape[0]) % D_MODEL
        if pad:
            flat = jnp.pad(flat, (0, pad))
        parts.append(flat.reshape(-1, D_MODEL))
    out = jnp.concatenate(parts, axis=0)
    pad = (-out.shape[0]) % 8
    if pad:
        out = jnp.pad(out, ((0, pad), (0, 0)))
    return out


def _unpack_rows(packed, shapes):
    out, row = [], 0
    for shp in shapes:
        n = math.prod(shp)
        nrow = -(-n // D_MODEL)
        out.append(packed[row:row + nrow].reshape(-1)[:n].reshape(shp))
        row += nrow
    return out


def _in_proj_rows(w_t):
    return jnp.concatenate([w_t[0:2560], w_t[2576:2960], w_t[2960:3216], w_t[2560:2576], w_t[3216:3248],
                            jnp.zeros((D_IN_PAD - D_IN, D_MODEL), w_t.dtype)], axis=0)


def _in_proj_rows_inv(d):
    return jnp.concatenate([d[0:2560], d[3200:3216], d[2560:2944], d[2944:3200], d[3216:3248]], axis=0)


def _rope_tables(positions):
    half = QK_ROPE // 2
    inv_freq = ROPE_THETA ** (-jnp.arange(0, QK_ROPE, 2, dtype=F32) / QK_ROPE)
    ang_t = positions[:, None, :].astype(F32) * inv_freq[:, None]
    cos_t, sin_t = jnp.cos(ang_t), jnp.sin(ang_t)
    b, _, s = ang_t.shape
    ts = min(2048, s)

    def body(c_ref, s_ref, cc_ref, sp_ref, sm_ref):
        row = lax.broadcasted_iota(I32, (half, LANES), 0)
        lane = lax.broadcasted_iota(I32, (half, LANES), 1)
        first, second = lane == QK_NOPE + row, lane == QK_NOPE + half + row

        spread = lambda x, where: _dot_01(x, where.astype(BF16), 3, TN_DIMS)
        lane1 = lax.broadcasted_iota(I32, (1, LANES), 1)
        ones = jnp.where((lane1 < QK_NOPE) | (lane1 >= QK_NOPE + QK_ROPE), 1.0, 0.0)
        cc_ref[...] = spread(c_ref[...], first | second) + ones
        sp_ref[...] = spread(s_ref[...], second)
        sm_ref[...] = -spread(s_ref[...], first)

    src = pl.BlockSpec((None, half, ts), lambda bb, i: (bb, 0, i))
    return pl.pallas_call(
        body, name="rope_tables", grid=(b, s // ts), in_specs=[src, src], out_specs=[_row(ts, LANES)] * 3,
        out_shape=[SDS((b, s, LANES), F32)] * 3, compiler_params=_cparams(2))(cos_t, sin_t)


def weight_views(gathered):
    full = lambda name: gathered[name].reshape(-1, gathered[name].shape[2])
    ukv = full("w_ukv").reshape(MLA_HEADS, QK_NOPE + V_HEAD, KV_LORA)
    wukv_t = jnp.concatenate([jnp.pad(ukv[:, :QK_NOPE], ((0, 0), (0, HEAD_PAD - QK_NOPE), (0, 0))).reshape(-1, KV_LORA),
                              ukv[:, QK_NOPE:].reshape(-1, KV_LORA)], axis=0)
    uq = full("w_uq").reshape(MLA_HEADS, QK_DIM, Q_LORA)
    wuq_t = jnp.pad(uq, ((0, 0), (0, HEAD_PAD - QK_DIM), (0, 0))).reshape(-1, Q_LORA)
    return dict(wg1_t=full("ffn1_w_gate"), wu1_t=full("ffn1_w_up"), wd1=full("ffn1_w_down"),
                wg2_t=full("ffn2_w_gate"), wu2_t=full("ffn2_w_up"), wd2=full("ffn2_w_down"),
                wo=full("w_out"), win_t=_in_proj_rows(full("w_in")), wukv_t=wukv_t, wuq_t=wuq_t)


def _ffn_bwd(tag, dxn, do, dgate, x, h, gg, uu, a, sc, norm_w, wg_t, wu_t, wd, below):
    f2 = wd.shape[0] // 2
    dwd = mm_tn(a, do, f2, D_MODEL, tag + "_dwd")
    dgg, duu = ffn_dact(do, wd, gg, uu, tag + "_dact")
    dwg_t = mm_tn(dgg, h, f2, D_MODEL, tag + "_dwg")
    dwu_t = mm_tn(duu, h, f2, D_MODEL, tag + "_dwu")
    dx, dsc, dsh, dnw, *nxt = dh_norm_bwd([dgg, duu], [wg_t, wu_t], x, dxn, norm_w, sc, tag + "_dh", below)
    return dx, (dsh, dsc, dgate), dnw, (dwg_t, dwu_t, dwd), nxt


def local_step(x, tgt, positions, mod, wv, p):
    nb, s, d = x.shape
    sh1, sc1, g1, sh2, sc2, g2, sh3, sc3, g3 = mod
    cc, sp, sm = _rope_tables(positions)
    lane_head = jnp.arange(D_SSD, dtype=I32)[None, :] // SSD_HEAD_DIM
    e_mat = (lane_head == jnp.arange(LANES, dtype=I32)[:, None]).astype(BF16)
    et_mat = e_mat.T
    rr, cl = jnp.arange(LANES, dtype=I32)[:, None], jnp.arange(LANES, dtype=I32)[None, :]
    place = ((cl == rr + (QK_NOPE - SSD_HEADS)) & (rr >= SSD_HEADS) & (rr < SSD_HEADS + QK_ROPE)).astype(F32)
    dtb = jnp.pad(p["dt_bias"], ((0, 0), (0, LANES - SSD_HEADS)))
    alog = jnp.pad(p["a_log"], ((0, 0), (0, LANES - SSD_HEADS)))
    dskip_e = jnp.repeat(p["d_skip"], SSD_HEAD_DIM, axis=1)

    h1 = norm_mod(x, p["norm_ffn1"], sc1, sh1, "ffn1_norm")
    gg1, uu1, a1 = ffn_up(h1, wv["wg1_t"], wv["wu1_t"], "ffn1_up")
    x1, o1, h2 = ffn_down(a1, wv["wd1"], x, g1, 0.5, "ffn1_down", (p["norm_mix"], sc2, sh2))
    z, u, cq, ckv, misc = in_proj(h2, wv["win_t"], "in_proj")
    xs, bm, cm_ = conv_fwd(u, p["conv_w"], p["conv_b"], "conv_fwd")
    ys, y, prev = ssd_fwd(xs, bm, cm_, misc, z, dtb, alog, dskip_e, p["ssd_norm_w"], e_mat, "ssd_fwd")
    q, k, v, vt, qn, kvn = qkv_fwd(cq, ckv, misc, cc, sp, sm, p["q_norm_w"], p["kv_norm_w"], wv["wuq_t"], wv["wukv_t"],
                               place, "qkv_fwd")
    attn, lse = flash_fwd(q, k, vt, "flash_fwd")
    x2, o2, ym, h3 = out_proj(ys, attn, p["mla_norm_w"], wv["wo"], x1, g2, (p["norm_ffn2"], sc3, sh3), "out_proj")
    gg3, uu3, a3 = ffn_up(h3, wv["wg2_t"], wv["wu2_t"], "ffn2_up")
    loss, dx3, dnfin, do3, dg3 = ffn_down_final(a3, wv["wd2"], x2, g3, 0.5, p["norm_final"], tgt, "ffn2_down_loss")

    dx2, dmod3, dnf2, (dwg2, dwu2, dwd2), (dout, dg2) = _ffn_bwd(
        "ffn2", dx3, do3, dg3, x2, h3, gg3, uu3, a3, sc3, p["norm_ffn2"], wv["wg2_t"], wv["wu2_t"], wv["wd2"],
        (o2, g2, 1.0))
    dys, dattn, dlt, dmlan = out_proj_bwd(dout, attn, p["mla_norm_w"], wv["wo"], "out_proj_bwd")
    dwo = mm_tn_blocks([ys, ym], dout, "dwo")
    dxs, dbm, dcm, dz, ddt, dssdn, ddsk_lane, ddtb, dalog = ssd_bwd(
        dys, y, z, xs, bm, cm_, misc, prev, dtb, alog, dskip_e, p["ssd_norm_w"], e_mat, et_mat, "ssd_bwd")
    dq, dk, dv = flash_bwd(q, k, v, dattn, lse, dlt, "flash_bwd")
    dcq, dckv, dmisc, dqp, dkvc, dqn, dkvn = qkv_bwd(dq, dk, dv, ddt, cq, ckv, cc, sp, sm, p["q_norm_w"], p["kv_norm_w"],
                                                     wv["wuq_t"], wv["wukv_t"], place.T, "qkv_bwd")
    dwuq = mm_tn(dqp, qn, MLA_HEADS * HEAD_PAD, Q_LORA, "dwuq")
    dwukv = mm_tn(dkvc, kvn, MLA_HEADS * HEAD_PAD, KV_LORA, "dwukv")
    du, dconv = conv_bwd(dxs, dbm, dcm, u, p["conv_w"], p["conv_b"], "conv_bwd")
    dconv = conv_grads_fold(dconv, "conv_grads_fold")
    dproj = [dz, du, dcq, dckv, dmisc]
    dwin = mm_tn_blocks(dproj, h2, "dwin")
    dx1, dsc2, dsh2, dnmix, do1, dg1 = dh_norm_bwd(dproj, [wv["win_t"]], x1, dx2, p["norm_mix"], sc2, "mix_dh",
                                                   (o1, g1, 0.5))
    dx0, dmod1, dnf1, (dwg1, dwu1, dwd1), _ = _ffn_bwd(
        "ffn1", dx1, do1, dg1, x, h1, gg1, uu1, a1, sc1, p["norm_ffn1"], wv["wg1_t"], wv["wu1_t"], wv["wd1"], None)

    dmod = jnp.concatenate([*dmod1, dsh2, dsc2, dg2, *dmod3], axis=1).reshape(nb, N_MOD * d)
    return dict(
        loss=loss, dx=dx0, dmod=dmod, norm_ffn1=dnf1, norm_mix=dnmix, norm_ffn2=dnf2, norm_final=dnfin,
        ssd_norm_w=dssdn, mla_norm_w=dmlan, q_norm_w=dqn, kv_norm_w=dkvn,
        dt_bias=ddtb[:, :SSD_HEADS], a_log=dalog[:, :SSD_HEADS],
        d_skip=squeeze_heads(ddsk_lane, et_mat, "d_skip_heads")[:, :SSD_HEADS],
        conv_b=dconv[4:5], conv_w=dconv[0:4],
        gw=dict(ffn1_w_gate=dwg1, ffn1_w_up=dwu1, ffn1_w_down=dwd1, ffn2_w_gate=dwg2, ffn2_w_up=dwu2, ffn2_w_down=dwd2,
                w_out=dwo, w_in=dwin, w_ukv=dwukv, w_uq=dwuq))


def kernel(x, c, positions, w_ada, b_ada, norm_ffn1, ffn1_w_gate, ffn1_w_up, ffn1_w_down, norm_mix, w_in, conv_w, conv_b, dt_bias, a_log, d_skip, ssd_norm_w, q_norm_w, w_uq, kv_norm_w, w_ukv, mla_norm_w, w_out, norm_ffn2, ffn2_w_gate, ffn2_w_up, ffn2_w_down, norm_final, loss_target, m_w_ada, m_b_ada, m_norm_ffn1, m_ffn1_w_gate, m_ffn1_w_up, m_ffn1_w_down, m_norm_mix, m_w_in, m_conv_w, m_conv_b, m_dt_bias, m_a_log, m_d_skip, m_ssd_norm_w, m_q_norm_w, m_w_uq, m_kv_norm_w, m_w_ukv, m_mla_norm_w, m_w_out, m_norm_ffn2, m_ffn2_w_gate, m_ffn2_w_up, m_ffn2_w_down, m_norm_final, v_w_ada, v_b_ada, v_norm_ffn1, v_ffn1_w_gate, v_ffn1_w_up, v_ffn1_w_down, v_norm_mix, v_w_in, v_conv_w, v_conv_b, v_dt_bias, v_a_log, v_d_skip, v_ssd_norm_w, v_q_norm_w, v_w_uq, v_kv_norm_w, v_w_ukv, v_mla_norm_w, v_w_out, v_norm_ffn2, v_ffn2_w_gate, v_ffn2_w_up, v_ffn2_w_down, v_norm_final):
    names = ["w_ada", "b_ada", "norm_ffn1", "ffn1_w_gate", "ffn1_w_up", "ffn1_w_down", "norm_mix", "w_in", "conv_w",
             "conv_b", "dt_bias", "a_log", "d_skip", "ssd_norm_w", "q_norm_w", "w_uq", "kv_norm_w", "w_ukv",
             "mla_norm_w", "w_out", "norm_ffn2", "ffn2_w_gate", "ffn2_w_up", "ffn2_w_down", "norm_final"]
    W = dict(zip(names, (w_ada, b_ada, norm_ffn1, ffn1_w_gate, ffn1_w_up, ffn1_w_down, norm_mix, w_in, conv_w, conv_b, dt_bias, a_log, d_skip, ssd_norm_w, q_norm_w, w_uq, kv_norm_w, w_ukv, mla_norm_w, w_out, norm_ffn2, ffn2_w_gate, ffn2_w_up, ffn2_w_down, norm_final)))
    M = dict(zip(names, (m_w_ada, m_b_ada, m_norm_ffn1, m_ffn1_w_gate, m_ffn1_w_up, m_ffn1_w_down, m_norm_mix, m_w_in, m_conv_w, m_conv_b, m_dt_bias, m_a_log, m_d_skip, m_ssd_norm_w, m_q_norm_w, m_w_uq, m_kv_norm_w, m_w_ukv, m_mla_norm_w, m_w_out, m_norm_ffn2, m_ffn2_w_gate, m_ffn2_w_up, m_ffn2_w_down, m_norm_final)))
    V = dict(zip(names, (v_w_ada, v_b_ada, v_norm_ffn1, v_ffn1_w_gate, v_ffn1_w_up, v_ffn1_w_down, v_norm_mix, v_w_in, v_conv_w, v_conv_b, v_dt_bias, v_a_log, v_d_skip, v_ssd_norm_w, v_q_norm_w, v_w_uq, v_kv_norm_w, v_w_ukv, v_mla_norm_w, v_w_out, v_norm_ffn2, v_ffn2_w_gate, v_ffn2_w_up, v_ffn2_w_down, v_norm_final)))

    nb, s, d = x.shape
    me = 4 * lax.axis_index("x") + 2 * lax.axis_index("y") + lax.axis_index("c")
    n_ada = w_ada.shape[2]

    taps, n_cw = conv_w.shape[1:]
    (cg,) = all_gather8([_pack_rows([c, conv_w[0]])], "gather_c")
    c_all = cg[:, 0:nb].reshape(N_DEV * nb, d)
    conv_w_full = cg[:, nb, 0:taps * n_cw].reshape(N_DEV, taps, n_cw).transpose(1, 0, 2).reshape(taps, N_DEV * n_cw)
    shards = [[_shard_view(name, W[name]).astype(BF16) for name in group] for group in GATHER_GROUPS]
    gathered = dict(zip(GATHER_GROUPS[0], all_gather8(shards[0], "gather_w_ffn1")))

    b_ada_cols = lax.dynamic_slice(b_ada, (0, me * n_ada), (1, n_ada))
    mod_cols, c_act = adaln_fwd(c_all, w_ada[0], b_ada_cols, "adaln_fwd")
    (mod_g,) = all_gather8([mod_cols], "gather_mod")
    gathered, mod_g, shards = lax.optimization_barrier((gathered, mod_g, shards))
    gathered.update(zip(GATHER_GROUPS[1], sc_all_gather8(shards[1], "gather_w_ffn2", 1)))
    gathered.update(zip(GATHER_GROUPS[2], sc_all_gather8(shards[2], "gather_w_mixer", 7)))
    wv = weight_views(gathered)
    mod = lax.dynamic_slice(mod_g, (0, me * nb, 0), (N_DEV, nb, n_ada)).transpose(1, 0, 2).reshape(nb, N_MOD, 1, d)
    mod = [mod[:, k] for k in range(N_MOD)]

    P = dict(W)
    P["conv_w"] = conv_w_full
    P["norm_final"] = norm_final.reshape(1, d)
    R = local_step(x, loss_target, positions, mod, wv, P)

    dmod = R["dmod"]
    partial_shapes = [(1,), (1, d), (1, d), (1, d), (1, d), (1, d), (1, d), (1, Q_LORA), (1, KV_LORA),
                      (1, SSD_HEADS), (1, SSD_HEADS), (1, SSD_HEADS), (1, D_CONV), (4, D_CONV), (1, N_MOD * d),
                      (nb, N_MOD * d)]
    partial = _pack_rows([R["loss"][0, :1], R["norm_ffn1"], R["norm_mix"], R["norm_ffn2"], R["norm_final"],
                          R["ssd_norm_w"], R["mla_norm_w"], R["q_norm_w"], R["kv_norm_w"],
                          R["dt_bias"], R["a_log"], R["d_skip"], R["conv_b"], R["conv_w"],
                          sum_rows(dmod, "dmod_rows"), dmod])
    (partial_g,) = all_gather8([partial], "gather_partials")
    (loss, g_nf1, g_nmix, g_nf2, g_nfin, g_ssdn, g_mlan, g_qn, g_kvn, g_dtb, g_alog, g_dskip, g_convb, g_convw,
     g_bada, _) = _unpack_rows(sum_blocks(partial_g, "sum_partials"), partial_shapes)
    dmod_row = sum(-(-math.prod(shp) // D_MODEL) for shp in partial_shapes[:-1])
    dmod_all = partial_g[:, dmod_row:dmod_row + nb * N_MOD].reshape(N_DEV * nb, N_MOD * d)
    g_wada = adaln_bwd(c_act, lax.dynamic_slice(dmod_all, (0, me * n_ada), (N_DEV * nb, n_ada)), "adaln_bwd")
    n_cw = conv_w.shape[2]
    G = {"w_ada": g_wada[None], "b_ada": g_bada, "norm_ffn1": g_nf1, "norm_mix": g_nmix, "norm_ffn2": g_nf2,
         "norm_final": g_nfin.reshape(d), "ssd_norm_w": g_ssdn, "mla_norm_w": g_mlan, "q_norm_w": g_qn,
         "kv_norm_w": g_kvn, "dt_bias": g_dtb, "a_log": g_alog, "d_skip": g_dskip, "conv_b": g_convb,
         "conv_w": lax.dynamic_slice(g_convw, (0, me * n_cw), (4, n_cw))[None]}

    DW, NM, NV = {}, {}, {}
    gw = R["gw"]
    for k, (tag, group) in enumerate(GRAD_GROUPS):
        send = [_grad_blocks(name, gw[name]).reshape(N_DEV, *_shard_view(name, W[name]).shape) for name in group]
        recv = sc_all_to_all8(send, "exchange_" + tag, 2 + k)
        for name, blocks in zip(group, recv):
            res = adamw_blocks(_shard_view(name, W[name]), blocks, _shard_view(name, M[name]), _shard_view(name, V[name]),
                               "adamw_" + name)
            G[name], DW[name], NM[name], NV[name] = [_shard_unview(name, t) for t in res]
    DW["w_ada"], NM["w_ada"], NV["w_ada"] = adamw(w_ada, g_wada, m_w_ada, v_w_ada, "adamw_w_ada")
    small = [n for n in names if n not in DW]
    as2d = lambda a: a.reshape(-1, a.shape[-1])
    outs = adamw_many([as2d(W[n]) for n in small], [as2d(G[n]) for n in small], [as2d(M[n]) for n in small],
                      [as2d(V[n]) for n in small], "adamw_small")
    for res, dst in zip(outs, (DW, NM, NV)):
        for n, t in zip(small, res):
            dst[n] = t.reshape(W[n].shape)
    return (loss.reshape(()), R["dx"], *[G[n] for n in names], *[DW[n] for n in names], *[NM[n] for n in names],
            *[NV[n] for n in names])
```

```python
import math

import jax
import jax.numpy as jnp
from jax import lax
from jax.experimental import pallas as pl
from jax.experimental.pallas import tpu as pltpu
from jax.experimental.pallas import tpu_sc as plsc

F32, BF16, I32 = jnp.float32, jnp.bfloat16, jnp.int32
HI = lax.Precision.HIGHEST
SDS = jax.ShapeDtypeStruct
MESH = pl.DeviceIdType.MESH

D_MODEL = 1024
D_FF = 2816
D_SSD = 1024
SSD_HEADS = 16
SSD_HEAD_DIM = 64
SSD_GROUPS = 2
SSD_STATE = 128
CHUNK = 128
MLA_HEADS = 8
QK_NOPE = 64
QK_ROPE = 32
QK_DIM = 96
V_HEAD = 128
Q_LORA = 384
KV_LORA = 256
ROPE_THETA = 10000.0
N_MOD = 9
EPS = 1e-6
D_CONV = 1536
D_IN = 3248
D_IN_PAD = 3328
HEAD_PAD = 128
N_DEV = 8
ADAM_LR, ADAM_B1, ADAM_B2, ADAM_EPS, ADAM_WD, ADAM_STEP = 0.001, 0.9, 0.999, 1e-08, 0.01, 10

SAVED_ACT = BF16
VMEM_LIMIT = 56 * 1024 * 1024
LANES = 128
NT_DIMS = (((1,), (1,)), ((), ()))
TN_DIMS = (((0,), (0,)), ((), ()))


def _cparams(n_axes):
    return pltpu.CompilerParams(dimension_semantics=("arbitrary",) * n_axes, vmem_limit_bytes=VMEM_LIMIT)


def _row(tm, d):
    return pl.BlockSpec((None, tm, d), lambda b, i: (b, i, 0))


def _bvec(d):
    return pl.BlockSpec((None, 1, d), lambda b, i: (b, 0, 0))


def _full(shape):
    n = len(shape)
    return pl.BlockSpec(shape, lambda *_: (0,) * n)


def _sigmoid(x):
    return 1.0 / (1.0 + jnp.exp(-x))


def _softplus(x):
    return jnp.maximum(x, 0.0) + jnp.log(1.0 + jnp.exp(-jnp.abs(x)))


def _rms(x):
    return lax.rsqrt(jnp.mean(x * x, axis=-1, keepdims=True) + EPS)


def _rms_bwd(dn, n, r):
    return r * (dn - n * jnp.mean(dn * n, axis=-1, keepdims=True))


def _first_step():
    return (pl.program_id(0) == 0) & (pl.program_id(1) == 0)


def _gather_copies(x_refs, out_refs, send_sems, recv_sems, local_sems, part=None):
    mx, my, mc = lax.axis_index("x"), lax.axis_index("y"), lax.axis_index("c")
    me, sibling = (mx, my, mc), (mx, my, 1 - mc)
    chips = [(1 - mx, my), (mx, 1 - my), (1 - mx, 1 - my)]

    def copy(a, k, block, to, src=None):
        rows = out_refs[a].at[4 * block[0] + 2 * block[1] + block[2]]
        return pltpu.make_async_remote_copy(
            src_ref=rows if src is None else src, dst_ref=rows,
            send_sem=send_sems.at[7 * a + k], recv_sem=recv_sems.at[7 * a + k], device_id=to, device_id_type=MESH)

    arrays = range(len(x_refs))
    mine = [pltpu.make_async_copy(x_refs[a], out_refs[a].at[4 * mx + 2 * my + mc], local_sems.at[a]) for a in arrays]
    first = [[copy(a, 0, me, sibling, src=x_refs[a])] + [copy(a, 1 + j, me, (*chip, mc), src=x_refs[a])
                                                          for j, chip in enumerate(chips)] for a in arrays]
    passed = [[copy(a, 4 + j, (*chip, mc), sibling) for j, chip in enumerate(chips)] for a in arrays]
    if part != "finish":
        for a in arrays:
            mine[a].start()
            for cp in first[a]:
                cp.start()
    if part == "start":
        return
    for a in arrays:
        for j, chip in enumerate(chips):
            copy(a, 1 + j, (*chip, mc), me).wait_recv()
            passed[a][j].start()
    for a in arrays:
        copy(a, 0, sibling, me).wait_recv()
        for j, chip in enumerate(chips):
            copy(a, 4 + j, (*chip, 1 - mc), me).wait_recv()
    for a in arrays:
        for cp in first[a] + passed[a]:
            cp.wait_send()
        mine[a].wait()


def _gather_peers():
    mx, my, mc = lax.axis_index("x"), lax.axis_index("y"), lax.axis_index("c")
    return [(mx, my, 1 - mc), (1 - mx, my, mc), (mx, 1 - my, mc), (1 - mx, 1 - my, mc)]


def _comm_scratch(n):
    return [pltpu.SemaphoreType.DMA((7 * n,)), pltpu.SemaphoreType.DMA((7 * n,)), pltpu.SemaphoreType.DMA((n,))]


def all_gather8(xs, name):
    n = len(xs)

    def body(*refs):
        _gather_copies(refs[:n], refs[n:2 * n], *refs[2 * n:])

    return pl.pallas_call(
        body, name=name,
        out_shape=[SDS((N_DEV, *x.shape), x.dtype) for x in xs],
        in_specs=[pl.BlockSpec(memory_space=pl.ANY)] * n,
        out_specs=[pl.BlockSpec(memory_space=pl.ANY)] * n,
        scratch_shapes=_comm_scratch(n),
    )(*xs)


def _exchange_peers():
    mx, my, mc = lax.axis_index("x"), lax.axis_index("y"), lax.axis_index("c")
    return [(1 - mx if rel & 4 else mx, 1 - my if rel & 2 else my, 1 - mc if rel & 1 else mc) for rel in range(1, N_DEV)]


def _exchange_copies(x_refs, out_refs, send_sems, recv_sems, local_sems):
    mx, my, mc = lax.axis_index("x"), lax.axis_index("y"), lax.axis_index("c")
    me = 4 * mx + 2 * my + mc
    copies = []
    for a, (x_ref, out_ref) in enumerate(zip(x_refs, out_refs)):
        mine = pltpu.make_async_copy(x_ref.at[me], out_ref.at[me], local_sems.at[a])
        mine.start()
        copies.append(mine)
        for k, (px, py, pc) in enumerate(_exchange_peers()):
            cp = pltpu.make_async_remote_copy(
                src_ref=x_ref.at[4 * px + 2 * py + pc], dst_ref=out_ref.at[me],
                send_sem=send_sems.at[7 * a + k], recv_sem=recv_sems.at[7 * a + k],
                device_id=(px, py, pc), device_id_type=MESH)
            cp.start()
            copies.append(cp)
    for cp in copies:
        cp.wait()


def all_to_all8(xs, name):
    n = len(xs)

    def body(*refs):
        _exchange_copies(refs[:n], refs[n:2 * n], *refs[2 * n:])

    return pl.pallas_call(
        body, name=name,
        out_shape=[SDS(x.shape, x.dtype) for x in xs],
        in_specs=[pl.BlockSpec(memory_space=pl.ANY)] * n,
        out_specs=[pl.BlockSpec(memory_space=pl.ANY)] * n,
        scratch_shapes=_comm_scratch(n),
    )(*xs)


def _sequencer_kernel(name, collective_id, n_arrays):
    return pl.kernel(
        mesh=plsc.ScalarSubcoreMesh(axis_name="seq", num_cores=1), name=name,
        scratch_types=tuple(_comm_scratch(n_arrays)),
        compiler_params=pltpu.CompilerParams(collective_id=collective_id))


def _handshake(peers):
    barrier = pltpu.get_barrier_semaphore()
    for peer in peers:
        pl.semaphore_signal(barrier, inc=1, device_id=peer, device_id_type=MESH)
    pl.semaphore_wait(barrier, len(peers))


def _hbm_refs(xs, out_shapes):
    x_refs = [jax.new_ref(x, memory_space=pltpu.MemorySpace.HBM) for x in xs]
    out_refs = [jax.empty_ref(SDS(shp, x.dtype), memory_space=pltpu.MemorySpace.HBM) for x, shp in zip(xs, out_shapes)]
    return x_refs, out_refs


def sc_all_gather8(xs, name, collective_id):
    x_refs, out_refs = _hbm_refs(xs, [(N_DEV, *x.shape) for x in xs])

    @_sequencer_kernel(name, collective_id, len(xs))
    def launch(send_sems, recv_sems, local_sems):
        _handshake(_gather_peers())
        _gather_copies(x_refs, out_refs, send_sems, recv_sems, local_sems)

    launch()
    return [ref[...] for ref in out_refs]


def sc_all_to_all8(xs, name, collective_id):
    x_refs, out_refs = _hbm_refs(xs, [x.shape for x in xs])

    @_sequencer_kernel(name, collective_id, len(xs))
    def launch(send_sems, recv_sems, local_sems):
        _handshake(_exchange_peers())
        _exchange_copies(x_refs, out_refs, send_sems, recv_sems, local_sems)

    launch()
    return [ref[...] for ref in out_refs]


def norm_mod(x, w, sc, sh, name, gather=()):
    b, s, d = x.shape
    tm = min(512, s)
    n = len(gather)
    last = (b - 1, s // tm - 1)

    def body(x_ref, w_ref, sc_ref, sh_ref, *refs):
        h_ref = refs[n]
        if n:
            comm = (refs[:n], refs[n + 1:2 * n + 1], *refs[2 * n + 1:])

            @pl.when(_first_step())
            def _():
                _gather_copies(*comm, part="start")
        xv = x_ref[...]
        nv = xv * _rms(xv)
        h_ref[...] = ((nv * w_ref[...]) * (1.0 + sc_ref[...]) + sh_ref[...]).astype(BF16)
        if n:
            @pl.when((pl.program_id(0) == last[0]) & (pl.program_id(1) == last[1]))
            def _():
                _gather_copies(*comm, part="finish")

    hbm = [pl.BlockSpec(memory_space=pl.ANY)] * n
    res = pl.pallas_call(
        body, name=name, grid=(b, s // tm),
        in_specs=[_row(tm, d), _full((1, d)), _bvec(d), _bvec(d)] + hbm,
        out_specs=[_row(tm, d)] + hbm,
        out_shape=[SDS((b, s, d), BF16)] + [SDS((N_DEV, *g.shape), g.dtype) for g in gather],
        scratch_shapes=_comm_scratch(n) if n else [], compiler_params=_cparams(2))(x, w, sc, sh, *gather)
    return res if n else res[0]


def ffn_up(h, wg_t, wu_t, name):
    b, s, d = h.shape
    f = wg_t.shape[0]
    tm, tn = min(1024, s), f // 2

    def body(h_ref, wg_ref, wu_ref, s_ref, t_ref, a_ref):
        hv = h_ref[...]
        g = lax.dot_general(hv, wg_ref[...], NT_DIMS, preferred_element_type=F32)
        u = lax.dot_general(hv, wu_ref[...], NT_DIMS, preferred_element_type=F32)
        sg = _sigmoid(g)
        silu = g * sg
        s_ref[...] = silu.astype(s_ref.dtype)
        t_ref[...] = (u * (sg + silu * (1.0 - sg))).astype(t_ref.dtype)
        a_ref[...] = (silu * u).astype(BF16)

    hs = pl.BlockSpec((None, tm, d), lambda j, bb, i: (bb, i, 0))
    ws = pl.BlockSpec((tn, d), lambda j, bb, i: (j, 0))
    os_ = pl.BlockSpec((None, tm, tn), lambda j, bb, i: (bb, i, j))
    return pl.pallas_call(
        body, name=name, grid=(f // tn, b, s // tm),
        in_specs=[hs, ws, ws], out_specs=[os_, os_, os_],
        out_shape=[SDS((b, s, f), SAVED_ACT), SDS((b, s, f), SAVED_ACT), SDS((b, s, f), BF16)],
        compiler_params=_cparams(3))(h, wg_t, wu_t)


def _norm_mod_tile(xv, w_ref, sc_ref, sh_ref):
    return ((xv * _rms(xv) * w_ref[...]) * (1.0 + sc_ref[...]) + sh_ref[...]).astype(BF16)


def ffn_down(a, wd, x, gate, scale, name, above=None):
    b, s, f = a.shape
    d = wd.shape[1]
    tm = min(1024, s)

    def body(a_ref, wd_ref, x_ref, g_ref, *rest):
        xn_ref, o_ref = rest[-3:-1] if above else rest
        o = jnp.dot(a_ref[...], wd_ref[...], preferred_element_type=F32)
        xn = x_ref[...] + (scale * g_ref[...]) * o
        xn_ref[...] = xn
        o_ref[...] = o.astype(BF16)
        if above:
            rest[-1][...] = _norm_mod_tile(xn, *rest[0:3])

    extra = above is not None
    return pl.pallas_call(
        body, name=name, grid=(b, s // tm),
        in_specs=[_row(tm, f), _full((f, d)), _row(tm, d), _bvec(d)] + ([_full((1, d)), _bvec(d), _bvec(d)] if extra else []),
        out_specs=[_row(tm, d), _row(tm, d)] + ([_row(tm, d)] if extra else []),
        out_shape=[SDS((b, s, d), F32), SDS((b, s, d), BF16)] + ([SDS((b, s, d), BF16)] if extra else []),
        compiler_params=_cparams(2))(a, wd, x, gate, *(above or ()))


def ffn_down_final(a, wd, x, gate, scale, w_final, tgt, name):
    b, s, f = a.shape
    d = wd.shape[1]
    tm = min(1024, s)

    def body(a_ref, wd_ref, x_ref, g_ref, w_ref, t_ref, loss_ref, dx_ref, dw_ref, do_ref, dg_ref):
        @pl.when(_first_step())
        def _():
            loss_ref[...] = jnp.zeros_like(loss_ref)
            dw_ref[...] = jnp.zeros_like(dw_ref)

        @pl.when(pl.program_id(1) == 0)
        def _():
            dg_ref[...] = jnp.zeros_like(dg_ref)
        o = jnp.dot(a_ref[...], wd_ref[...], preferred_element_type=F32)
        sg = scale * g_ref[...]
        xv = x_ref[...] + sg * o
        r = _rms(xv)
        n = xv * r
        wv = w_ref[...]
        e = n * wv - t_ref[...]
        loss_ref[...] += jnp.sum(e * e) * (0.5 / d)
        dy = e * (1.0 / d)
        dw_ref[...] += jnp.sum(dy * n, axis=0, keepdims=True)
        dx = _rms_bwd(dy * wv, n, r)
        dx_ref[...] = dx
        do_ref[...] = (sg * dx).astype(BF16)
        dg_ref[...] += jnp.sum(scale * dx * o, axis=0, keepdims=True)

    return pl.pallas_call(
        body, name=name, grid=(b, s // tm),
        in_specs=[_row(tm, f), _full((f, d)), _row(tm, d), _bvec(d), _full((1, d)), _row(tm, d)],
        out_specs=[_full((1, LANES)), _row(tm, d), _full((1, d)), _row(tm, d), _bvec(d)],
        out_shape=[SDS((1, LANES), F32), SDS((b, s, d), F32), SDS((1, d), F32), SDS((b, s, d), BF16), SDS((b, 1, d), F32)],
        compiler_params=_cparams(2))(a, wd, x, gate, w_final, tgt)


def ffn_dact(do, wd, silu_g, u_dsilu, name):
    b, s, d = do.shape
    f = wd.shape[0]
    tm, tn = min(1024, s), f // 2

    def body(do_ref, wd_ref, s_ref, t_ref, dg_ref, du_ref):
        da = lax.dot_general(do_ref[...], wd_ref[...], NT_DIMS, preferred_element_type=F32)
        dg_ref[...] = (da * t_ref[...].astype(F32)).astype(BF16)
        du_ref[...] = (da * s_ref[...].astype(F32)).astype(BF16)

    dos = pl.BlockSpec((None, tm, d), lambda j, bb, i: (bb, i, 0))
    ws = pl.BlockSpec((tn, d), lambda j, bb, i: (j, 0))
    es = pl.BlockSpec((None, tm, tn), lambda j, bb, i: (bb, i, j))
    return pl.pallas_call(
        body, name=name, grid=(f // tn, b, s // tm),
        in_specs=[dos, ws, es, es], out_specs=[es, es],
        out_shape=[SDS((b, s, f), BF16), SDS((b, s, f), BF16)], compiler_params=_cparams(3))(do, wd, silu_g, u_dsilu)


def mm_tn(a, bm, tma, tnb, name):
    b, s, ka = a.shape
    nb = bm.shape[2]
    tk = min(2048, s)
    nk = s // tk

    def body(a_ref, b_ref, o_ref, acc):
        first = (pl.program_id(2) == 0) & (pl.program_id(3) == 0)
        last = (pl.program_id(2) == b - 1) & (pl.program_id(3) == nk - 1)
        part = lax.dot_general(a_ref[...], b_ref[...], TN_DIMS, preferred_element_type=F32)

        @pl.when(first)
        def _():
            acc[...] = part

        @pl.when(jnp.logical_not(first))
        def _():
            acc[...] += part

        @pl.when(last)
        def _():
            o_ref[...] = acc[...].astype(BF16)

    return pl.pallas_call(
        body, name=name, grid=(ka // tma, nb // tnb, b, nk),
        in_specs=[pl.BlockSpec((None, tk, tma), lambda i, j, bb, k: (bb, k, i)),
                  pl.BlockSpec((None, tk, tnb), lambda i, j, bb, k: (bb, k, j))],
        out_specs=pl.BlockSpec((tma, tnb), lambda i, j, bb, k: (i, j)),
        out_shape=SDS((ka, nb), BF16), scratch_shapes=[pltpu.VMEM((tma, tnb), F32)],
        compiler_params=_cparams(4))(a, bm)


def mm_tn_blocks(a_blocks, bm, name):
    b, s, nb = bm.shape
    widths = [a.shape[2] for a in a_blocks]
    starts = [sum(widths[:k]) for k in range(len(widths))]
    tk = min(2048 if sum(widths) <= 2048 else 1024, s)
    nk = s // tk
    n = len(a_blocks)

    def body(*refs):
        a_refs, b_ref, o_ref, acc = refs[:n], refs[n], refs[n + 1], refs[n + 2]
        first = (pl.program_id(0) == 0) & (pl.program_id(1) == 0)
        last = (pl.program_id(0) == b - 1) & (pl.program_id(1) == nk - 1)

        @pl.when(first)
        def _():
            acc[...] = jnp.zeros_like(acc)
        bv = b_ref[...]
        for a_ref, st, wd in zip(a_refs, starts, widths):
            acc[st:st + wd, :] += lax.dot_general(a_ref[...], bv, TN_DIMS, preferred_element_type=F32)

        @pl.when(last)
        def _():
            o_ref[...] = acc[...].astype(BF16)

    return pl.pallas_call(
        body, name=name, grid=(b, nk),
        in_specs=[_row(tk, wd) for wd in widths] + [_row(tk, nb)],
        out_specs=_full((sum(widths), nb)), out_shape=SDS((sum(widths), nb), BF16),
        scratch_shapes=[pltpu.VMEM((sum(widths), nb), F32)], compiler_params=_cparams(2))(*a_blocks, bm)


def _gate_bwd_specs(tm, d, b, s):
    return ([_row(tm, d), _bvec(d)], [_row(tm, d), _bvec(d)], [SDS((b, s, d), BF16), SDS((b, 1, d), F32)])


def _gate_bwd_tile(dx, scale, o_ref, g_ref, do_ref, dg_ref):
    do_ref[...] = ((scale * g_ref[...]) * dx).astype(BF16)
    dg_ref[...] += jnp.sum(scale * dx * o_ref[...].astype(F32), axis=0, keepdims=True)


def dh_norm_bwd(dys, wts, x, dxn, w, sc, name, below=None):
    b, s, d = x.shape
    tm = min(512, s)
    n_in, n_w = len(dys), len(wts)
    extra_in, extra_out, extra_shape = _gate_bwd_specs(tm, d, b, s) if below else ([], [], [])
    starts = [sum(dy.shape[2] for dy in dys[:k]) for k in range(n_in)]

    def body(*refs):
        dy_refs, w_refs = refs[:n_in], refs[n_in:n_in + n_w]
        x_ref, dxn_ref, nw_ref, sc_ref = refs[n_in + n_w:n_in + n_w + 4]
        rest = refs[n_in + n_w + 4:]
        if below:
            o_ref, g_ref, dx_ref, dsc_ref, dsh_ref, dw_ref, do_ref, dg_ref = rest
        else:
            dx_ref, dsc_ref, dsh_ref, dw_ref = rest

        @pl.when(pl.program_id(1) == 0)
        def _():
            dsc_ref[...] = jnp.zeros_like(dsc_ref)
            dsh_ref[...] = jnp.zeros_like(dsh_ref)
            if below:
                dg_ref[...] = jnp.zeros_like(dg_ref)

        @pl.when(_first_step())
        def _():
            dw_ref[...] = jnp.zeros_like(dw_ref)

        def weight(k):
            return w_refs[k][...] if n_w == n_in else w_refs[0][starts[k]:starts[k] + dys[k].shape[2], :]

        dh = jnp.dot(dy_refs[0][...], weight(0), preferred_element_type=F32)
        for k in range(1, n_in):
            dh += jnp.dot(dy_refs[k][...], weight(k), preferred_element_type=F32)
        xv = x_ref[...]
        r = _rms(xv)
        n = xv * r
        nw = nw_ref[...]
        dsc_ref[...] += jnp.sum(dh * (n * nw), axis=0, keepdims=True)
        dsh_ref[...] += jnp.sum(dh, axis=0, keepdims=True)
        dhn = dh * (1.0 + sc_ref[...])
        dw_ref[...] += jnp.sum(dhn * n, axis=0, keepdims=True)
        dx = dxn_ref[...] + _rms_bwd(dhn * nw, n, r)
        dx_ref[...] = dx
        if below:
            _gate_bwd_tile(dx, below[2], o_ref, g_ref, do_ref, dg_ref)

    resident = lambda shape: pl.BlockSpec(shape, lambda *_: (0,) * len(shape), pipeline_mode=pl.Buffered(1))
    in_specs = [_row(tm, dy.shape[2]) for dy in dys] + [resident(wt.shape) for wt in wts]
    in_specs += [_row(tm, d), _row(tm, d), _full((1, d)), _bvec(d)] + extra_in
    return pl.pallas_call(
        body, name=name, grid=(b, s // tm), in_specs=in_specs,
        out_specs=[_row(tm, d), _bvec(d), _bvec(d), _full((1, d))] + extra_out,
        out_shape=[SDS((b, s, d), F32), SDS((b, 1, d), F32), SDS((b, 1, d), F32), SDS((1, d), F32)] + extra_shape,
        compiler_params=_cparams(2))(*dys, *wts, x, dxn, w, sc, *(below[:2] if below else ()))


def in_proj(h, win_t, name):
    b, s, d = h.shape
    tm = min(512, s)
    widths = (D_SSD, D_SSD + 2 * SSD_GROUPS * SSD_STATE, Q_LORA, KV_LORA, LANES)

    def body(h_ref, w_ref, *outs):
        p = lax.dot_general(h_ref[...], w_ref[...], NT_DIMS, preferred_element_type=F32)
        off = 0
        for o_ref, wd in zip(outs, widths):
            o_ref[...] = p[:, off:off + wd]
            off += wd

    return pl.pallas_call(
        body, name=name, grid=(b, s // tm),
        in_specs=[_row(tm, d), _full(win_t.shape)],
        out_specs=[_row(tm, wd) for wd in widths],
        out_shape=[SDS((b, s, wd), F32) for wd in widths], compiler_params=_cparams(2))(h, win_t)


def _halo_prev(ts, d):
    return pl.BlockSpec((None, 8, d), lambda b, i: (b, jnp.maximum(i * (ts // 8) - 1, 0), 0))


CONV_ROWS = 32


def _conv_head(head, u_ref, up_ref, tile):
    head[0:8, :] = jnp.where(tile > 0, up_ref[...], 0.0)
    head[8:8 + CONV_ROWS, :] = u_ref[0:CONV_ROWS, :]


def _conv_windows(u_ref, head, r0):
    if r0 == 0:
        return [head[5 + k:5 + k + CONV_ROWS, :] for k in range(4)]
    return [u_ref[r0 - 3 + k:r0 - 3 + k + CONV_ROWS, :] for k in range(4)]


def _fold8(t):
    acc = t[0:8, :]
    for r in range(8, CONV_ROWS, 8):
        acc += t[r:r + 8, :]
    return acc


def conv_fwd(u, cw, cb, name):
    b, s, dc = u.shape
    ts = min(512, s)
    widths = (D_SSD, SSD_GROUPS * SSD_STATE, SSD_GROUPS * SSD_STATE)

    def body(u_ref, up_ref, w_ref, b_ref, xs_ref, bm_ref, cm_ref, head):
        _conv_head(head, u_ref, up_ref, pl.program_id(1))
        ws = [w_ref[k:k + 1, :] for k in range(4)]
        bias = b_ref[...]
        for r0 in range(0, ts, CONV_ROWS):
            taps = _conv_windows(u_ref, head, r0)
            v = bias + taps[0] * ws[0] + taps[1] * ws[1] + taps[2] * ws[2] + taps[3] * ws[3]
            y = v * _sigmoid(v)
            rs = slice(r0, r0 + CONV_ROWS)
            xs_ref[rs, :] = y[:, 0:D_SSD]
            bm_ref[rs, :] = y[:, D_SSD:D_SSD + 256]
            cm_ref[rs, :] = y[:, D_SSD + 256:D_SSD + 512]

    return pl.pallas_call(
        body, name=name, grid=(b, s // ts),
        in_specs=[_row(ts, dc), _halo_prev(ts, dc), _full((4, dc)), _full((1, dc))],
        out_specs=[_row(ts, wd) for wd in widths],
        out_shape=[SDS((b, s, wd), F32) for wd in widths],
        scratch_shapes=[pltpu.VMEM((8 + CONV_ROWS, dc), F32)], compiler_params=_cparams(2))(u, u, cw, cb)


def conv_bwd(dxs, dbm, dcm, u, cw, cb, name):
    b, s, dc = u.shape
    ts = min(512, s)
    nt = s // ts

    def body(dxs_ref, dbm_ref, dcm_ref, u_ref, up_ref, w_ref, b_ref, du_ref, dwb_ref, head, dvs):
        @pl.when(_first_step())
        def _():
            dwb_ref[...] = jnp.zeros_like(dwb_ref)

        @pl.when(pl.program_id(1) == 0)
        def _():
            dvs[ts:ts + 8, :] = jnp.zeros((8, dc), F32)
        _conv_head(head, u_ref, up_ref, nt - 1 - pl.program_id(1))
        ws = [w_ref[k:k + 1, :] for k in range(4)]
        bias = b_ref[...]
        for r0 in range(0, ts, CONV_ROWS):
            taps = _conv_windows(u_ref, head, r0)
            v = bias + taps[0] * ws[0] + taps[1] * ws[1] + taps[2] * ws[2] + taps[3] * ws[3]
            sg = _sigmoid(v)
            rs = slice(r0, r0 + CONV_ROWS)
            dy = jnp.concatenate([dxs_ref[rs, :], dbm_ref[rs, :], dcm_ref[rs, :]], axis=1)
            dv = dy * (sg * (1.0 + v * (1.0 - sg)))
            dvs[rs, :] = dv
            for k in range(4):
                dwb_ref[8 * k:8 * k + 8, :] += _fold8(dv * taps[k])
            dwb_ref[32:40, :] += _fold8(dv)
        for r0 in range(0, ts, CONV_ROWS):
            win = [dvs[r0 + 3 - k:r0 + 3 - k + CONV_ROWS, :] for k in range(4)]
            acc = win[0] * ws[0] + win[1] * ws[1] + win[2] * ws[2] + win[3] * ws[3]
            du_ref[r0:r0 + CONV_ROWS, :] = acc.astype(BF16)
        dvs[ts:ts + 8, :] = dvs[0:8, :]

    rows = lambda wd: pl.BlockSpec((None, ts, wd), lambda bb, i: (bb, nt - 1 - i, 0))
    prev = pl.BlockSpec((None, 8, dc), lambda bb, i: (bb, jnp.maximum((nt - 1 - i) * (ts // 8) - 1, 0), 0))
    return pl.pallas_call(
        body, name=name, grid=(b, nt),
        in_specs=[rows(D_SSD), rows(256), rows(256), rows(dc), prev, _full((4, dc)), _full((1, dc))],
        out_specs=[rows(dc), _full((40, dc))],
        out_shape=[SDS((b, s, dc), BF16), SDS((40, dc), F32)],
        scratch_shapes=[pltpu.VMEM((8 + CONV_ROWS, dc), F32), pltpu.VMEM((ts + 8, dc), F32)],
        compiler_params=_cparams(2))(dxs, dbm, dcm, u, u, cw, cb)


def conv_grads_fold(x, name):
    c = x.shape[1]

    def body(x_ref, o_ref):
        o_ref[...] = jnp.zeros_like(o_ref)
        for k in range(5):
            o_ref[k:k + 1, :] = jnp.sum(x_ref[8 * k:8 * k + 8, :], axis=0, keepdims=True)

    return pl.pallas_call(body, name=name, out_shape=SDS((8, c), F32))(x)


def _ssd_common(misc_ref, dtb_ref, alog_ref, e_ref):
    ln = CHUNK
    lane = lax.broadcasted_iota(I32, (ln, LANES), 1)
    lane1 = lax.broadcasted_iota(I32, (1, LANES), 1)
    pre = misc_ref[...] + dtb_ref[...]
    dt_s = jnp.where(lane < SSD_HEADS, _softplus(pre), 0.0)
    a_neg = jnp.where(lane1 < SSD_HEADS, -jnp.exp(alog_ref[...]), 0.0)
    ri = lax.broadcasted_iota(I32, (ln, ln), 0)
    ci = lax.broadcasted_iota(I32, (ln, ln), 1)
    tril = ci <= ri
    acum = jnp.dot(tril.astype(F32), dt_s * a_neg, preferred_element_type=F32, precision=HI)
    both_e = _dot_01(jnp.concatenate([dt_s, acum], axis=0), e_ref[...], 3)
    dt_e, acum_e = both_e[0:ln], both_e[ln:2 * ln]
    return dict(pre=pre, dt_s=dt_s, a_neg=a_neg, tril=tril, ri=ri, ci=ci, acum=acum, acum_t=acum.T,
                dt_e=dt_e, eac_e=jnp.exp(acum_e), del_e=jnp.exp(acum_e[ln - 1:ln, :] - acum_e))


def _dot_01(x, m01, terms, dims=(((1,), (0,)), ((), ()))):
    acc, rest = None, x
    for k in range(terms):
        part = rest.astype(BF16)
        if k + 1 < terms:
            rest = rest - part.astype(F32)
        d = lax.dot_general(part, m01, dims, preferred_element_type=F32)
        acc = d if acc is None else acc + d
    return acc


def _decay(cm, h):
    seg = cm["acum"][:, h:h + 1] - cm["acum_t"][h:h + 1, :]
    return jnp.exp(jnp.where(cm["tril"], seg, -jnp.inf))


def ssd_fwd(xs, bm, cm_, misc, z, dtb, alog, dskip_e, norm_w, e_mat, name):
    b, s, _ = xs.shape
    ln, nc = CHUNK, s // CHUNK
    gw = D_SSD // SSD_GROUPS
    hpg = SSD_HEADS // SSD_GROUPS

    def body(xs_ref, b_ref, c_ref, misc_ref, z_ref, dtb_ref, alog_ref, dsk_ref, nw_ref, e_ref,
             ys_ref, y_ref, p_ref, st, yd):
        @pl.when(pl.program_id(1) == 0)
        def _():
            st[...] = jnp.zeros_like(st)
        cm = _ssd_common(misc_ref, dtb_ref, alog_ref, e_ref)
        xsv = xs_ref[...]
        xdt = xsv * cm["dt_e"]
        xdt_b = xdt.astype(BF16)
        xd_b = (xdt * cm["del_e"]).astype(BF16)
        gam_e = cm["eac_e"][ln - 1:ln, :]
        p_ref[...] = st[...]
        groups = [slice(gw * g, gw * (g + 1)) for g in range(SSD_GROUPS)]
        heads = [slice(SSD_HEAD_DIM * h, SSD_HEAD_DIM * (h + 1)) for h in range(SSD_HEADS)]
        bgs = [b_ref[:, SSD_STATE * g:SSD_STATE * (g + 1)].astype(BF16) for g in range(SSD_GROUPS)]
        cgs = [c_ref[:, SSD_STATE * g:SSD_STATE * (g + 1)].astype(BF16) for g in range(SSD_GROUPS)]
        cbs = [lax.dot_general(cg, bg, NT_DIMS, preferred_element_type=F32) for cg, bg in zip(cgs, bgs)]
        sts = [st[:, gs] for gs in groups]
        yoff = [jnp.dot(cg, st_g.astype(BF16), preferred_element_type=F32) * cm["eac_e"][:, gs]
                for cg, st_g, gs in zip(cgs, sts, groups)]
        news = [lax.dot_general(bg, xd_b[:, gs], TN_DIMS, preferred_element_type=F32) for bg, gs in zip(bgs, groups)]
        for gs, st_g, new in zip(groups, sts, news):
            st[:, gs] = st_g * gam_e[:, gs] + new
        ms = [(cbs[h // hpg] * _decay(cm, h)).astype(BF16) for h in range(SSD_HEADS)]
        for h, hs in enumerate(heads):
            yd[:, hs] = jnp.dot(ms[h], xdt_b[:, hs], preferred_element_type=F32)
        y = yd[...] + jnp.concatenate(yoff, axis=1) + dsk_ref[...] * xsv
        y_ref[...] = y
        zz = z_ref[...]
        yg = y * (zz * _sigmoid(zz))
        outs = []
        for g in range(SSD_GROUPS):
            ygg = yg[:, gw * g:gw * (g + 1)]
            outs.append(ygg * _rms(ygg) * nw_ref[:, gw * g:gw * (g + 1)])
        ys_ref[...] = jnp.concatenate(outs, axis=1).astype(BF16)

    row = lambda d: pl.BlockSpec((None, ln, d), lambda bb, c: (bb, c, 0))
    return pl.pallas_call(
        body, name=name, grid=(b, nc),
        in_specs=[row(D_SSD), row(256), row(256), row(LANES), row(D_SSD), _full((1, LANES)), _full((1, LANES)),
                  _full((1, D_SSD)), _full((1, D_SSD)), _full((LANES, D_SSD))],
        out_specs=[row(D_SSD), row(D_SSD), pl.BlockSpec((None, None, SSD_STATE, D_SSD), lambda bb, c: (bb, c, 0, 0))],
        out_shape=[SDS((b, s, D_SSD), BF16), SDS((b, s, D_SSD), F32), SDS((b, nc, SSD_STATE, D_SSD), F32)],
        scratch_shapes=[pltpu.VMEM((SSD_STATE, D_SSD), F32), pltpu.VMEM((ln, D_SSD), F32)],
        compiler_params=_cparams(2))(xs, bm, cm_, misc, z, dtb, alog, dskip_e, norm_w, e_mat)


def ssd_bwd(dys, y, z, xs, bm, cm_, misc, prev, dtb, alog, dskip_e, norm_w, e_mat, et_mat, name):
    b, s, _ = xs.shape
    ln, nc = CHUNK, s // CHUNK
    gw = D_SSD // SSD_GROUPS
    hpg = SSD_HEADS // SSD_GROUPS

    def body(dys_ref, y_ref, z_ref, xs_ref, b_ref, c_ref, misc_ref, p_ref, dtb_ref, alog_ref, dsk_ref, nw_ref,
             e_ref, et_ref, dxs_ref, db_ref, dc_ref, dz_ref, ddt_ref, dnw_ref, ddsk_ref, ddtb_ref, dalog_ref,
             dst, dxd, dac_t):
        @pl.when(_first_step())
        def _():
            for r_ in (dnw_ref, ddsk_ref, ddtb_ref, dalog_ref):
                r_[...] = jnp.zeros_like(r_)

        @pl.when(pl.program_id(1) == 0)
        def _():
            dst[...] = jnp.zeros_like(dst)

        cm = _ssd_common(misc_ref, dtb_ref, alog_ref, e_ref)
        et = et_ref[...]
        squeeze = lambda t: _dot_01(t, et, 2)
        lane = lax.broadcasted_iota(I32, (ln, LANES), 1)
        sub = lax.broadcasted_iota(I32, (LANES, ln), 0)
        xsv = xs_ref[...]
        xdt = xsv * cm["dt_e"]
        xdt_b = xdt.astype(BF16)
        xd_b = (xdt * cm["del_e"]).astype(BF16)
        eac_e = cm["eac_e"]
        gam_e = eac_e[ln - 1:ln, :]

        yv, zz, dyo = y_ref[...], z_ref[...], dys_ref[...]
        sz = _sigmoid(zz)
        silu_z = zz * sz
        yg = yv * silu_z
        dyg, dnw = [], []
        for g in range(SSD_GROUPS):
            gs = slice(gw * g, gw * (g + 1))
            ygg = yg[:, gs]
            r = _rms(ygg)
            n = ygg * r
            dnw.append(jnp.sum(dyo[:, gs] * n, axis=0, keepdims=True))
            dyg.append(_rms_bwd(dyo[:, gs] * nw_ref[:, gs], n, r))
        dyg = jnp.concatenate(dyg, axis=1)
        dnw_ref[...] += jnp.concatenate(dnw, axis=1)
        dz_ref[...] = (dyg * yv * (sz * (1.0 + zz * (1.0 - sz)))).astype(BF16)
        dy = dyg * silu_z
        ddsk_ref[...] += jnp.sum(dy * xsv, axis=0, keepdims=True)
        dy_b = dy.astype(BF16)

        dacum = jnp.zeros((ln, LANES), F32)
        dac_t[...] = jnp.zeros_like(dac_t)
        w1, dgam = [], []
        for g in range(SSD_GROUPS):
            gs = slice(gw * g, gw * (g + 1))
            ss = slice(SSD_STATE * g, SSD_STATE * (g + 1))
            bg = b_ref[:, ss].astype(BF16)
            cg = c_ref[:, ss].astype(BF16)
            cb = lax.dot_general(cg, bg, NT_DIMS, preferred_element_type=F32)
            pt = p_ref[:, gs]
            pt_b = pt.astype(BF16)
            dst_g = dst[:, gs]
            dst_b = dst_g.astype(BF16)
            edy = (dy[:, gs] * eac_e[:, gs]).astype(BF16)
            dcg = lax.dot_general(edy, pt_b, NT_DIMS, preferred_element_type=F32)
            dpt = lax.dot_general(cg, edy, TN_DIMS, preferred_element_type=F32)
            yoff = jnp.dot(cg, pt_b, preferred_element_type=F32) * eac_e[:, gs]
            dxd_g = jnp.dot(bg, dst_b, preferred_element_type=F32)
            dbg = lax.dot_general(xd_b[:, gs], dst_b, NT_DIMS, preferred_element_type=F32)
            ddel = dxd_g * xdt[:, gs] * cm["del_e"][:, gs]
            w1.append(dy[:, gs] * yoff - ddel)
            dgam.append(jnp.sum(ddel, axis=0, keepdims=True) + jnp.sum(dst_g * pt, axis=0, keepdims=True) * gam_e[:, gs])
            dxd[:, gs] = dxd_g * cm["del_e"][:, gs]
            dst[:, gs] = dst_g * gam_e[:, gs] + dpt
            dcb = jnp.zeros((ln, ln), F32)
            for j in range(hpg):
                h = hpg * g + j
                hs = slice(SSD_HEAD_DIM * h, SSD_HEAD_DIM * (h + 1))
                lam = _decay(cm, h)
                m = cb * lam
                dm = lax.dot_general(dy_b[:, hs], xdt_b[:, hs], NT_DIMS, preferred_element_type=F32)
                dxd[:, hs] += lax.dot_general(m.astype(BF16), dy_b[:, hs], TN_DIMS, preferred_element_type=F32)
                dcb += dm * lam
                wl = dm * m
                dacum += jnp.where(lane == h, jnp.sum(wl, axis=1, keepdims=True), 0.0)
                dac_t[...] -= jnp.where(sub == h, jnp.sum(wl, axis=0, keepdims=True), 0.0)
            dcb_b = dcb.astype(BF16)
            dc_ref[:, ss] = dcg + jnp.dot(dcb_b, bg, preferred_element_type=F32)
            db_ref[:, ss] = dbg + lax.dot_general(dcb_b, cg, TN_DIMS, preferred_element_type=F32)

        dxdt = dxd[...]
        dxs_ref[...] = dy * dsk_ref[...] + dxdt * cm["dt_e"]
        dacum += squeeze(jnp.concatenate(w1, axis=1)) + dac_t[...].T
        dlast = squeeze(jnp.broadcast_to(jnp.concatenate(dgam, axis=1), (8, D_SSD)))[0:1, :]
        dacum += jnp.where(lax.broadcasted_iota(I32, (ln, LANES), 0) == ln - 1, dlast, 0.0)
        triu = (cm["ci"] >= cm["ri"]).astype(F32)
        da = jnp.dot(triu, dacum, preferred_element_type=F32, precision=HI)
        ddt = da * cm["a_neg"] + squeeze(dxdt * xsv)
        dalog_ref[...] += jnp.sum(da * cm["dt_s"], axis=0, keepdims=True) * cm["a_neg"]
        ddt_raw = jnp.where(lane < SSD_HEADS, ddt * _sigmoid(cm["pre"]), 0.0)
        ddt_ref[...] = ddt_raw
        ddtb_ref[...] += jnp.sum(ddt_raw, axis=0, keepdims=True)

    row = lambda d: pl.BlockSpec((None, ln, d), lambda bb, c: (bb, nc - 1 - c, 0))
    return pl.pallas_call(
        body, name=name, grid=(b, nc),
        in_specs=[row(D_SSD), row(D_SSD), row(D_SSD), row(D_SSD), row(256), row(256), row(LANES),
                  pl.BlockSpec((None, None, SSD_STATE, D_SSD), lambda bb, c: (bb, nc - 1 - c, 0, 0)),
                  _full((1, LANES)), _full((1, LANES)), _full((1, D_SSD)), _full((1, D_SSD)),
                  _full((LANES, D_SSD)), _full((D_SSD, LANES))],
        out_specs=[row(D_SSD), row(256), row(256), row(D_SSD), row(LANES),
                   _full((1, D_SSD)), _full((1, D_SSD)), _full((1, LANES)), _full((1, LANES))],
        out_shape=[SDS((b, s, D_SSD), F32), SDS((b, s, 256), F32), SDS((b, s, 256), F32), SDS((b, s, D_SSD), BF16),
                   SDS((b, s, LANES), F32), SDS((1, D_SSD), F32), SDS((1, D_SSD), F32), SDS((1, LANES), F32),
                   SDS((1, LANES), F32)],
        scratch_shapes=[pltpu.VMEM((SSD_STATE, D_SSD), F32), pltpu.VMEM((ln, D_SSD), F32), pltpu.VMEM((LANES, ln), F32)],
        compiler_params=_cparams(2))(dys, y, z, xs, bm, cm_, misc, prev, dtb, alog, dskip_e, norm_w, e_mat, et_mat)


def _rope(xv, cc, sp, sm):
    n = xv.shape[1]
    return xv * cc + pltpu.roll(xv, 16, 1) * sp + pltpu.roll(xv, n - 16, 1) * sm


def _rope_bwd(dy, cc, sp, sm):
    n = dy.shape[1]
    return dy * cc + pltpu.roll(dy * sp, n - 16, 1) + pltpu.roll(dy * sm, 16, 1)


def _tile8(t):
    return jnp.concatenate([t] * MLA_HEADS, axis=1)


def qkv_fwd(cq, ckv, misc, cc, sp, sm, qnw, kvnw, wuq_t, wukv_t, place, name):
    b, s, _ = cq.shape
    tm = _att_tile(s)
    hd = MLA_HEADS * HEAD_PAD

    def body(cq_ref, ckv_ref, misc_ref, cc_ref, sp_ref, sm_ref, qnw_ref, kvnw_ref, wq_ref, wkv_ref, pl_ref,
             q_ref, k_ref, v_ref, vt_ref, qn_ref, kvn_ref):
        cqv, ckvv = cq_ref[...], ckv_ref[...]
        qn = (cqv * _rms(cqv) * qnw_ref[...]).astype(BF16)
        kvn = (ckvv * _rms(ckvv) * kvnw_ref[...]).astype(BF16)
        qn_ref[...] = qn
        kvn_ref[...] = kvn
        cc1, sp1, sm1 = cc_ref[...], sp_ref[...], sm_ref[...]
        q = lax.dot_general(qn, wq_ref[...], NT_DIMS, preferred_element_type=F32)
        q_ref[...] = _rope(q, _tile8(cc1), _tile8(sp1), _tile8(sm1)).astype(BF16)
        kv = lax.dot_general(kvn, wkv_ref[...], NT_DIMS, preferred_element_type=F32)
        kr = jnp.dot(misc_ref[...], pl_ref[...], preferred_element_type=F32, precision=HI)
        kr = _rope(kr, cc1, sp1, sm1)
        k_ref[...] = (kv[:, 0:hd] + _tile8(kr)).astype(BF16)
        v_ref[...] = kv[:, hd:2 * hd].astype(BF16)
        for h in range(MLA_HEADS):
            vt_ref[h] = kv[:, hd + HEAD_PAD * h:hd + HEAD_PAD * (h + 1)].T.astype(BF16)

    return pl.pallas_call(
        body, name=name, grid=(b, s // tm),
        in_specs=[_row(tm, Q_LORA), _row(tm, KV_LORA), _row(tm, LANES), _row(tm, LANES), _row(tm, LANES), _row(tm, LANES),
                  _full((1, Q_LORA)), _full((1, KV_LORA)), _full(wuq_t.shape), _full(wukv_t.shape), _full((LANES, LANES))],
        out_specs=[_row(tm, hd), _row(tm, hd), _row(tm, hd),
                   pl.BlockSpec((None, MLA_HEADS, None, HEAD_PAD, tm), lambda bb, i: (bb, 0, i, 0, 0)),
                   _row(tm, Q_LORA), _row(tm, KV_LORA)],
        out_shape=[SDS((b, s, hd), BF16)] * 3 + [SDS((b, MLA_HEADS, s // tm, HEAD_PAD, tm), BF16),
                                                 SDS((b, s, Q_LORA), BF16), SDS((b, s, KV_LORA), BF16)],
        compiler_params=_cparams(2))(cq, ckv, misc, cc, sp, sm, qnw, kvnw, wuq_t, wukv_t, place)


def qkv_bwd(dq, dk, dv, ddt, cq, ckv, cc, sp, sm, qnw, kvnw, wuq_t, wukv_t, place_t, name):
    b, s, _ = cq.shape
    tm = min(512, s)
    hd = MLA_HEADS * HEAD_PAD

    def body(dq_ref, dk_ref, dv_ref, ddt_ref, cq_ref, ckv_ref, cc_ref, sp_ref, sm_ref, qnw_ref, kvnw_ref,
             wq_ref, wkv_ref, plt_ref, dcq_ref, dckv_ref, dmisc_ref, dqp_ref, dkv_ref, dqnw_ref, dkvnw_ref):
        @pl.when(_first_step())
        def _():
            dqnw_ref[...] = jnp.zeros_like(dqnw_ref)
            dkvnw_ref[...] = jnp.zeros_like(dkvnw_ref)
        cc1, sp1, sm1 = cc_ref[...], sp_ref[...], sm_ref[...]
        dqp = _rope_bwd(dq_ref[...].astype(F32), _tile8(cc1), _tile8(sp1), _tile8(sm1)).astype(BF16)
        dqp_ref[...] = dqp
        dkv_b = jnp.concatenate([dk_ref[...], dv_ref[...]], axis=1)
        dkf = dk_ref[...].astype(F32)
        dkv_ref[...] = dkv_b
        dkr = dkf[:, 0:HEAD_PAD]
        for h in range(1, MLA_HEADS):
            dkr += dkf[:, HEAD_PAD * h:HEAD_PAD * (h + 1)]
        dkr = _rope_bwd(dkr, cc1, sp1, sm1)
        dmisc_ref[...] = (jnp.dot(dkr, plt_ref[...], preferred_element_type=F32, precision=HI) + ddt_ref[...]).astype(BF16)

        def norm_bwd(dn_w, xv, w_ref, dw_ref, dx_ref):
            r = _rms(xv)
            n = xv * r
            dw_ref[...] += jnp.sum(dn_w * n, axis=0, keepdims=True)
            dx_ref[...] = _rms_bwd(dn_w * w_ref[...], n, r).astype(BF16)

        norm_bwd(jnp.dot(dqp, wq_ref[...], preferred_element_type=F32), cq_ref[...], qnw_ref, dqnw_ref, dcq_ref)
        norm_bwd(jnp.dot(dkv_b, wkv_ref[...], preferred_element_type=F32), ckv_ref[...], kvnw_ref, dkvnw_ref, dckv_ref)

    return pl.pallas_call(
        body, name=name, grid=(b, s // tm),
        in_specs=[_row(tm, hd), _row(tm, hd), _row(tm, hd), _row(tm, LANES), _row(tm, Q_LORA), _row(tm, KV_LORA),
                  _row(tm, LANES), _row(tm, LANES), _row(tm, LANES), _full((1, Q_LORA)), _full((1, KV_LORA)),
                  _full(wuq_t.shape), _full(wukv_t.shape), _full((LANES, LANES))],
        out_specs=[_row(tm, Q_LORA), _row(tm, KV_LORA), _row(tm, LANES), _row(tm, hd), _row(tm, 2 * hd),
                   _full((1, Q_LORA)), _full((1, KV_LORA))],
        out_shape=[SDS((b, s, Q_LORA), BF16), SDS((b, s, KV_LORA), BF16), SDS((b, s, LANES), BF16),
                   SDS((b, s, hd), BF16), SDS((b, s, 2 * hd), BF16), SDS((1, Q_LORA), F32), SDS((1, KV_LORA), F32)],
        compiler_params=_cparams(2))(dq, dk, dv, ddt, cq, ckv, cc, sp, sm, qnw, kvnw, wuq_t, wukv_t, place_t)


ATT_SCALE = 1.0 / math.sqrt(QK_DIM)
LOG2E = math.log2(math.e)
ATT_SCALE_LOG2E = ATT_SCALE * LOG2E


ATT_HEADS_PER_STEP = 4
ATT_HEADS_PER_STEP_BWD = 2


def _att_tile(s):
    return min(512, s)


def flash_fwd(q, k, vt, name):
    b, s, hd = q.shape
    t = _att_tile(s)
    nb = s // t
    th = t // 2

    hps = ATT_HEADS_PER_STEP
    hw = hps * HEAD_PAD

    def body(q_ref, k_ref, vt_ref, o_ref, lse_ref, m_s, l_s, acc):
        i = pl.program_id(2)
        m_s[...] = jnp.full_like(m_s, -jnp.inf)
        l_s[...] = jnp.zeros_like(l_s)
        acc[...] = jnp.zeros_like(acc)

        def update(j, diagonal):
            chains = [(hh, half) for hh in range(hps) for half in range(2)]
            lanes = lambda hh: slice(HEAD_PAD * hh, HEAD_PAD * (hh + 1))
            cols = lambda half: slice(th * half, th * (half + 1))
            sts = {}
            nkeys = lambda half: th if diagonal and half == 0 else t
            for hh, half in chains:
                kr = pl.ds(pl.multiple_of(j * t, t), nkeys(half))
                st = lax.dot_general(k_ref[kr, lanes(hh)], q_ref[cols(half), lanes(hh)], NT_DIMS,
                                     preferred_element_type=F32)
                if diagonal:
                    row = lax.broadcasted_iota(I32, (nkeys(half), th), 0)
                    col = lax.broadcasted_iota(I32, (nkeys(half), th), 1) + th * half
                    st = jnp.where(row <= col, st, -jnp.inf)
                sts[hh, half] = st
            pts, alphas = {}, {}
            for hh, half in chains:
                st, cs = sts[hh, half], cols(half)
                m_prev = m_s[hh, :, cs]
                m_new = jnp.maximum(m_prev, jnp.max(st, axis=0, keepdims=True))
                alpha = jnp.exp2((m_prev - m_new) * ATT_SCALE_LOG2E)
                pt = jnp.exp2((st - m_new) * ATT_SCALE_LOG2E)
                l_s[hh, :, cs] = alpha * l_s[hh, :, cs] + jnp.sum(pt, axis=0, keepdims=True)
                m_s[hh, :, cs] = m_new
                pts[hh, half], alphas[hh, half] = pt.astype(BF16), alpha
            for hh, half in chains:
                cs = cols(half)
                acc[hh, :, cs] = alphas[hh, half] * acc[hh, :, cs] + jnp.dot(
                    vt_ref[hh, j, :, 0:nkeys(half)], pts[hh, half], preferred_element_type=F32)

        def step(j, carry):
            update(j, False)
            return carry

        lax.fori_loop(0, i, step, 0)
        update(i, True)
        for hh in range(hps):
            o_ref[:, HEAD_PAD * hh:HEAD_PAD * (hh + 1)] = (acc[hh] / l_s[hh]).T
            lse_ref[hh] = m_s[hh] * ATT_SCALE + jnp.log(l_s[hh])

    qs = pl.BlockSpec((None, t, hw), lambda bb, h, i: (bb, i, h))
    ks = pl.BlockSpec((None, s, hw), lambda bb, h, i: (bb, 0, h))
    vs = pl.BlockSpec((None, hps, nb, HEAD_PAD, t), lambda bb, h, i: (bb, h, 0, 0, 0))
    ls = pl.BlockSpec((None, hps, None, 1, t), lambda bb, h, i: (bb, h, i, 0, 0))
    return pl.pallas_call(
        body, name=name, grid=(b, MLA_HEADS // hps, nb),
        in_specs=[qs, ks, vs], out_specs=[qs, ls],
        out_shape=[SDS((b, s, hd), F32), SDS((b, MLA_HEADS, nb, 1, t), F32)],
        scratch_shapes=[pltpu.VMEM((hps, 1, t), F32), pltpu.VMEM((hps, 1, t), F32), pltpu.VMEM((hps, HEAD_PAD, t), F32)],
        compiler_params=_cparams(3))(q, k, vt)


def flash_bwd(q, k, v, do, lse, dlt, name):
    b, s, hd = q.shape
    t = _att_tile(s)
    nb = s // t
    th = t // 2
    lse_r = lse
    dlt_r = dlt.reshape(b, MLA_HEADS, nb, 1, t)

    hps = ATT_HEADS_PER_STEP_BWD
    hw = hps * HEAD_PAD

    def body(q_ref, k_ref, v_ref, do_ref, lse_ref, dlt_ref, dq_ref, dk_ref, dv_ref, dq_s, dk_s, dv_s):
        dq_s[...] = jnp.zeros_like(dq_s)
        dk_s[...] = jnp.zeros_like(dk_s)
        dv_s[...] = jnp.zeros_like(dv_s)

        def tile(j, i, diagonal):
            chains = [(hh, half) for hh in range(hps) for half in range(2)]
            lanes = lambda hh: slice(HEAD_PAD * hh, HEAD_PAD * (hh + 1))
            keys = lambda half: pl.ds(pl.multiple_of(j * t + th * half, th), th)
            q0 = lambda half: th if diagonal and half == 1 else 0
            qsel = lambda half: pl.ds(pl.multiple_of(i * t + q0(half), th), t - q0(half))
            sts, dpts = {}, {}
            for hh, half in chains:
                ls_, ks, qs, nq = lanes(hh), keys(half), qsel(half), t - q0(half)
                st = lax.dot_general(k_ref[ks, ls_], q_ref[qs, ls_], NT_DIMS, preferred_element_type=F32)
                if diagonal:
                    row = lax.broadcasted_iota(I32, (th, nq), 0) + th * half
                    col = lax.broadcasted_iota(I32, (th, nq), 1) + q0(half)
                    st = jnp.where(row <= col, st, -jnp.inf)
                sts[hh, half] = st
                dpts[hh, half] = lax.dot_general(v_ref[ks, ls_], do_ref[qs, ls_], NT_DIMS, preferred_element_type=F32)
            pts, dsts = {}, {}
            for hh, half in chains:
                qcols = slice(q0(half), t)
                pt = jnp.exp2(sts[hh, half] * ATT_SCALE_LOG2E - lse_ref[hh, i][:, qcols] * LOG2E)
                pts[hh, half] = pt.astype(BF16)
                dsts[hh, half] = (pt * (dpts[hh, half] - dlt_ref[hh, i][:, qcols])).astype(BF16)
            for hh, half in chains:
                ls_, ks, qs = lanes(hh), keys(half), qsel(half)
                dv_s[ks, ls_] += jnp.dot(pts[hh, half], do_ref[qs, ls_], preferred_element_type=F32)
                dk_s[ks, ls_] += jnp.dot(dsts[hh, half], q_ref[qs, ls_], preferred_element_type=F32)
                dq_s[qs, ls_] += lax.dot_general(dsts[hh, half], k_ref[ks, ls_], TN_DIMS, preferred_element_type=F32)

        def key_tile(j, carry):
            tile(j, j, True)

            def query_tile(i, c2):
                tile(j, i, False)
                return c2

            lax.fori_loop(j + 1, nb, query_tile, 0)
            return carry

        lax.fori_loop(0, nb, key_tile, 0)
        dq_ref[...] = (dq_s[...] * ATT_SCALE).astype(BF16)
        dk_ref[...] = (dk_s[...] * ATT_SCALE).astype(BF16)
        dv_ref[...] = dv_s[...].astype(BF16)

    hs = pl.BlockSpec((None, s, hw), lambda bb, h: (bb, 0, h))
    ls = pl.BlockSpec((None, hps, nb, 1, t), lambda bb, h: (bb, h, 0, 0, 0))
    return pl.pallas_call(
        body, name=name, grid=(b, MLA_HEADS // hps),
        in_specs=[hs, hs, hs, hs, ls, ls], out_specs=[hs, hs, hs],
        out_shape=[SDS((b, s, hd), BF16)] * 3, scratch_shapes=[pltpu.VMEM((s, hw), F32)] * 3,
        compiler_params=_cparams(2))(q, k, v, do, lse_r, dlt_r)


def out_proj(ys, attn, mnw, wo, x, gate, above, name):
    b, s, d = x.shape
    tm = min(512, s)

    def body(ys_ref, at_ref, mnw_ref, wo_ref, x_ref, g_ref, nw_ref, sc_ref, sh_ref, xn_ref, o_ref, ym_ref, h_ref):
        av = at_ref[...]
        ym = (av * _rms(av) * mnw_ref[...]).astype(BF16)
        ym_ref[...] = ym
        o = jnp.dot(ys_ref[...], wo_ref[0:D_SSD, :], preferred_element_type=F32)
        o += jnp.dot(ym, wo_ref[D_SSD:2 * D_SSD, :], preferred_element_type=F32)
        xn = x_ref[...] + g_ref[...] * o
        xn_ref[...] = xn
        o_ref[...] = o.astype(BF16)
        h_ref[...] = _norm_mod_tile(xn, nw_ref, sc_ref, sh_ref)

    return pl.pallas_call(
        body, name=name, grid=(b, s // tm),
        in_specs=[_row(tm, D_SSD), _row(tm, D_SSD), _full((1, D_SSD)), _full(wo.shape), _row(tm, d), _bvec(d),
                  _full((1, d)), _bvec(d), _bvec(d)],
        out_specs=[_row(tm, d), _row(tm, d), _row(tm, D_SSD), _row(tm, d)],
        out_shape=[SDS((b, s, d), F32), SDS((b, s, d), BF16), SDS((b, s, D_SSD), BF16), SDS((b, s, d), BF16)],
        compiler_params=_cparams(2))(ys, attn, mnw, wo, x, gate, *above)


def out_proj_bwd(dout, attn, mnw, wo, name):
    b, s, d = dout.shape
    tm = min(512, s)

    def body(do_ref, at_ref, mnw_ref, wo_ref, dys_ref, dat_ref, dlt_ref, dw_ref):
        lane = lax.broadcasted_iota(I32, (tm, LANES), 1)
        @pl.when(_first_step())
        def _():
            dw_ref[...] = jnp.zeros_like(dw_ref)
        dov = do_ref[...]
        dys_ref[...] = lax.dot_general(dov, wo_ref[0:D_SSD, :], NT_DIMS, preferred_element_type=F32)
        dym = lax.dot_general(dov, wo_ref[D_SSD:2 * D_SSD, :], NT_DIMS, preferred_element_type=F32)
        av = at_ref[...]
        r = _rms(av)
        n = av * r
        dw_ref[...] += jnp.sum(dym * n, axis=0, keepdims=True)
        dat = _rms_bwd(dym * mnw_ref[...], n, r)
        dat_ref[...] = dat.astype(BF16)
        prod = dat * av
        cols = jnp.zeros((tm, LANES), F32)
        for h in range(MLA_HEADS):
            cols += jnp.where(lane == h, jnp.sum(prod[:, HEAD_PAD * h:HEAD_PAD * (h + 1)], axis=1, keepdims=True), 0.0)
        dlt_ref[...] = cols.T[0:MLA_HEADS, :]

    return pl.pallas_call(
        body, name=name, grid=(b, s // tm),
        in_specs=[_row(tm, d), _row(tm, D_SSD), _full((1, D_SSD)), _full(wo.shape)],
        out_specs=[_row(tm, D_SSD), _row(tm, D_SSD),
                   pl.BlockSpec((None, MLA_HEADS, tm), lambda bb, i: (bb, 0, i)), _full((1, D_SSD))],
        out_shape=[SDS((b, s, D_SSD), F32), SDS((b, s, D_SSD), BF16), SDS((b, MLA_HEADS, s), F32),
                   SDS((1, D_SSD), F32)],
        compiler_params=_cparams(2))(dout, attn, mnw, wo)


def adaln_fwd(c_all, w_ada, b_ada, name):
    nb, d = c_all.shape
    n = w_ada.shape[1]

    def body(c_ref, w_ref, b_ref, m_ref, ca_ref):
        cv = c_ref[...]
        ca = (cv * _sigmoid(cv)).astype(BF16)
        ca_ref[...] = ca
        m_ref[...] = jnp.dot(ca, w_ref[...].astype(BF16), preferred_element_type=F32) + b_ref[...]

    return pl.pallas_call(
        body, name=name, out_shape=[SDS((nb, n), F32), SDS((nb, d), BF16)],
        compiler_params=pltpu.CompilerParams(vmem_limit_bytes=VMEM_LIMIT))(c_all, w_ada, b_ada)


def adaln_bwd(c_act, dmod_cols, name):
    d, n = c_act.shape[1], dmod_cols.shape[1]

    def body(c_ref, dm_ref, gw_ref):
        gw_ref[...] = lax.dot_general(c_ref[...], dm_ref[...].astype(BF16), TN_DIMS, preferred_element_type=F32)

    return pl.pallas_call(
        body, name=name, out_shape=SDS((d, n), F32),
        compiler_params=pltpu.CompilerParams(vmem_limit_bytes=VMEM_LIMIT))(c_act, dmod_cols)


def sum_rows(x, name):
    def body(x_ref, o_ref):
        o_ref[...] = jnp.sum(x_ref[...], axis=0, keepdims=True)
    return pl.pallas_call(body, name=name, out_shape=SDS((1, x.shape[1]), F32))(x)


def squeeze_heads(x, et_mat, name):
    def body(x_ref, et_ref, o_ref):
        xv = jnp.broadcast_to(x_ref[...], (8, x.shape[1]))
        o_ref[...] = _dot_01(xv, et_ref[...], 3)[0:1, :]
    return pl.pallas_call(body, name=name, out_shape=SDS((1, LANES), F32))(x, et_mat)


def sum_blocks(x, name):
    n, r, c = x.shape
    tr = next(cand for cand in (256, 128, 64, 32, 16, 8) if r % cand == 0)

    def body(x_ref, o_ref):
        acc = x_ref[0].astype(F32)
        for k in range(1, n):
            acc += x_ref[k].astype(F32)
        o_ref[...] = acc

    return pl.pallas_call(
        body, name=name, grid=(r // tr,), in_specs=[pl.BlockSpec((n, tr, c), lambda i: (0, i, 0))],
        out_specs=pl.BlockSpec((tr, c), lambda i: (i, 0)), out_shape=SDS((r, c), F32),
        compiler_params=_cparams(1))(x)


def _adam_math(w, g, m, v):
    m = ADAM_B1 * m + (1.0 - ADAM_B1) * g
    v = ADAM_B2 * v + (1.0 - ADAM_B2) * (g * g)
    m_hat = m / (1.0 - ADAM_B1 ** ADAM_STEP)
    v_hat = v / (1.0 - ADAM_B2 ** ADAM_STEP)
    return -ADAM_LR * (m_hat / (jnp.sqrt(v_hat) + ADAM_EPS) + ADAM_WD * w), m, v


def adamw(w, g, m, v, name):
    r, c = w.shape[-2:]
    tr = r
    for cand in (512, 256, 128, 64, 32, 16, 8):
        if r % cand == 0 and cand * c * 4 <= 2 * 1024 * 1024:
            tr = cand
            break

    def body(w_ref, g_ref, m_ref, v_ref, d_ref, mo_ref, vo_ref):
        d_ref[...], mo_ref[...], vo_ref[...] = _adam_math(w_ref[...], g_ref[...], m_ref[...], v_ref[...])

    def spec(a):
        return pl.BlockSpec((tr, c), lambda i: (i, 0)) if a.ndim == 2 else pl.BlockSpec((None, tr, c), lambda i: (0, i, 0))

    return pl.pallas_call(
        body, name=name, grid=(r // tr,), in_specs=[spec(w), spec(g), spec(m), spec(v)], out_specs=[spec(w)] * 3,
        out_shape=[SDS(w.shape, F32)] * 3, compiler_params=_cparams(1))(w, g, m, v)


def adamw_blocks(w, blocks, m, v, name):
    r, c = w.shape
    tr = next((cand for cand in range(r // 32 * 16, 0, -16) if r % cand == 0), r)

    def body(w_ref, b_ref, m_ref, v_ref, g_ref, d_ref, mo_ref, vo_ref):
        g = b_ref[0].astype(F32)
        for k in range(1, N_DEV):
            g += b_ref[k].astype(F32)
        g_ref[...] = g
        d_ref[...], mo_ref[...], vo_ref[...] = _adam_math(w_ref[...], g, m_ref[...], v_ref[...])

    spec = pl.BlockSpec((tr, c), lambda i: (i, 0))
    return pl.pallas_call(
        body, name=name, grid=(r // tr,),
        in_specs=[spec, pl.BlockSpec((N_DEV, tr, c), lambda i: (0, i, 0)), spec, spec], out_specs=[spec] * 4,
        out_shape=[SDS((r, c), F32)] * 4, compiler_params=_cparams(1))(w, blocks, m, v)


def adamw_many(ws, gs, ms, vs, name):
    n = len(ws)

    def body(*refs):
        w_r, g_r, m_r, v_r = (refs[k * n:(k + 1) * n] for k in range(4))
        d_r, mo_r, vo_r = (refs[(4 + k) * n:(5 + k) * n] for k in range(3))
        for k in range(n):
            d_r[k][...], mo_r[k][...], vo_r[k][...] = _adam_math(w_r[k][...], g_r[k][...], m_r[k][...], v_r[k][...])

    shapes = [SDS(w.shape, F32) for w in ws]
    outs = pl.pallas_call(body, name=name, out_shape=shapes * 3)(*ws, *gs, *ms, *vs)
    return outs[:n], outs[n:2 * n], outs[2 * n:]


TRANSPOSED = ("ffn1_w_gate", "ffn1_w_up", "ffn2_w_gate", "ffn2_w_up", "w_in", "w_ukv", "w_uq")
GATHER_GROUPS = (("ffn1_w_gate", "ffn1_w_up"), ("ffn2_w_gate", "ffn2_w_up", "ffn2_w_down"),
                 ("ffn1_w_down", "w_in", "w_ukv", "w_uq", "w_out"))
GRAD_GROUPS = (("ffn2", ("ffn2_w_gate", "ffn2_w_up", "ffn2_w_down")), ("mixer", ("w_out", "w_in", "w_ukv", "w_uq")),
               ("ffn1_down", ("ffn1_w_down",)), ("ffn1_gate", ("ffn1_w_gate",)), ("ffn1_up", ("ffn1_w_up",)))


def _shard_view(name, w):
    return w[0].T if name in TRANSPOSED else w[0]


def _shard_unview(name, t):
    return t.T[None] if name in TRANSPOSED else t[None]


def _grad_blocks(name, gw):
    if name == "w_in":
        return _in_proj_rows_inv(gw).reshape(N_DEV, -1, D_MODEL)
    if name == "w_ukv":
        hd = MLA_HEADS * HEAD_PAD
        return jnp.concatenate([gw[:hd].reshape(MLA_HEADS, HEAD_PAD, KV_LORA)[:, :QK_NOPE],
                                gw[hd:].reshape(MLA_HEADS, V_HEAD, KV_LORA)], axis=1)
    if name == "w_uq":
        return gw.reshape(MLA_HEADS, HEAD_PAD, Q_LORA)[:, :QK_DIM]
    return gw.reshape(N_DEV, -1, D_MODEL)


def _pack_rows(arrs):
    parts = []
    for a in arrs:
        flat = a.reshape(-1).astype(F32)
        pad = (-flat.shape[0]) % D_MODEL
        if pad:
            flat = jnp.pad(flat, (0, pad))
        parts.append(flat.reshape(-1, D_MODEL))
    out = jnp.concatenate(parts, axis=0)
    pad = (-out.shape[0]) % 8
    if pad:
        out = jnp.pad(out, ((0, pad), (0, 0)))
    return out


def _unpack_rows(packed, shapes):
    out, row = [], 0
    for shp in shapes:
        n = math.prod(shp)
        nrow = -(-n // D_MODEL)
        out.append(packed[row:row + nrow].reshape(-1)[:n].reshape(shp))
        row += nrow
    return out


def _in_proj_rows(w_t):
    return jnp.concatenate([w_t[0:2560], w_t[2576:2960], w_t[2960:3216], w_t[2560:2576], w_t[3216:3248],
                            jnp.zeros((D_IN_PAD - D_IN, D_MODEL), w_t.dtype)], axis=0)


def _in_proj_rows_inv(d):
    return jnp.concatenate([d[0:2560], d[3200:3216], d[2560:2944], d[2944:3200], d[3216:3248]], axis=0)


def _rope_tables(positions):
    half = QK_ROPE // 2
    inv_freq = ROPE_THETA ** (-jnp.arange(0, QK_ROPE, 2, dtype=F32) / QK_ROPE)
    ang_t = positions[:, None, :].astype(F32) * inv_freq[:, None]
    cos_t, sin_t = jnp.cos(ang_t), jnp.sin(ang_t)
    b, _, s = ang_t.shape
    ts = min(2048, s)

    def body(c_ref, s_ref, cc_ref, sp_ref, sm_ref):
        row = lax.broadcasted_iota(I32, (half, LANES), 0)
        lane = lax.broadcasted_iota(I32, (half, LANES), 1)
        first, second = lane == QK_NOPE + row, lane == QK_NOPE + half + row

        spread = lambda x, where: _dot_01(x, where.astype(BF16), 3, TN_DIMS)
        lane1 = lax.broadcasted_iota(I32, (1, LANES), 1)
        ones = jnp.where((lane1 < QK_NOPE) | (lane1 >= QK_NOPE + QK_ROPE), 1.0, 0.0)
        cc_ref[...] = spread(c_ref[...], first | second) + ones
        sp_ref[...] = spread(s_ref[...], second)
        sm_ref[...] = -spread(s_ref[...], first)

    src = pl.BlockSpec((None, half, ts), lambda bb, i: (bb, 0, i))
    return pl.pallas_call(
        body, name="rope_tables", grid=(b, s // ts), in_specs=[src, src], out_specs=[_row(ts, LANES)] * 3,
        out_shape=[SDS((b, s, LANES), F32)] * 3, compiler_params=_cparams(2))(cos_t, sin_t)


def weight_views(gathered):
    full = lambda name: gathered[name].reshape(-1, gathered[name].shape[2])
    ukv = full("w_ukv").reshape(MLA_HEADS, QK_NOPE + V_HEAD, KV_LORA)
    wukv_t = jnp.concatenate([jnp.pad(ukv[:, :QK_NOPE], ((0, 0), (0, HEAD_PAD - QK_NOPE), (0, 0))).reshape(-1, KV_LORA),
                              ukv[:, QK_NOPE:].reshape(-1, KV_LORA)], axis=0)
    uq = full("w_uq").reshape(MLA_HEADS, QK_DIM, Q_LORA)
    wuq_t = jnp.pad(uq, ((0, 0), (0, HEAD_PAD - QK_DIM), (0, 0))).reshape(-1, Q_LORA)
    return dict(wg1_t=full("ffn1_w_gate"), wu1_t=full("ffn1_w_up"), wd1=full("ffn1_w_down"),
                wg2_t=full("ffn2_w_gate"), wu2_t=full("ffn2_w_up"), wd2=full("ffn2_w_down"),
                wo=full("w_out"), win_t=_in_proj_rows(full("w_in")), wukv_t=wukv_t, wuq_t=wuq_t)


def _ffn_bwd(tag, dxn, do, dgate, x, h, gg, uu, a, sc, norm_w, wg_t, wu_t, wd, below):
    f2 = wd.shape[0] // 2
    dwd = mm_tn(a, do, f2, D_MODEL, tag + "_dwd")
    dgg, duu = ffn_dact(do, wd, gg, uu, tag + "_dact")
    dwg_t = mm_tn(dgg, h, f2, D_MODEL, tag + "_dwg")
    dwu_t = mm_tn(duu, h, f2, D_MODEL, tag + "_dwu")
    dx, dsc, dsh, dnw, *nxt = dh_norm_bwd([dgg, duu], [wg_t, wu_t], x, dxn, norm_w, sc, tag + "_dh", below)
    return dx, (dsh, dsc, dgate), dnw, (dwg_t, dwu_t, dwd), nxt


def local_step(x, tgt, positions, mod, wv, p, h1=None):
    nb, s, d = x.shape
    sh1, sc1, g1, sh2, sc2, g2, sh3, sc3, g3 = mod
    cc, sp, sm = _rope_tables(positions)
    lane_head = jnp.arange(D_SSD, dtype=I32)[None, :] // SSD_HEAD_DIM
    e_mat = (lane_head == jnp.arange(LANES, dtype=I32)[:, None]).astype(BF16)
    et_mat = e_mat.T
    rr, cl = jnp.arange(LANES, dtype=I32)[:, None], jnp.arange(LANES, dtype=I32)[None, :]
    place = ((cl == rr + (QK_NOPE - SSD_HEADS)) & (rr >= SSD_HEADS) & (rr < SSD_HEADS + QK_ROPE)).astype(F32)
    dtb = jnp.pad(p["dt_bias"], ((0, 0), (0, LANES - SSD_HEADS)))
    alog = jnp.pad(p["a_log"], ((0, 0), (0, LANES - SSD_HEADS)))
    dskip_e = jnp.repeat(p["d_skip"], SSD_HEAD_DIM, axis=1)

    if h1 is None:
        h1 = norm_mod(x, p["norm_ffn1"], sc1, sh1, "ffn1_norm")
    gg1, uu1, a1 = ffn_up(h1, wv["wg1_t"], wv["wu1_t"], "ffn1_up")
    x1, o1, h2 = ffn_down(a1, wv["wd1"], x, g1, 0.5, "ffn1_down", (p["norm_mix"], sc2, sh2))
    z, u, cq, ckv, misc = in_proj(h2, wv["win_t"], "in_proj")
    xs, bm, cm_ = conv_fwd(u, p["conv_w"], p["conv_b"], "conv_fwd")
    ys, y, prev = ssd_fwd(xs, bm, cm_, misc, z, dtb, alog, dskip_e, p["ssd_norm_w"], e_mat, "ssd_fwd")
    q, k, v, vt, qn, kvn = qkv_fwd(cq, ckv, misc, cc, sp, sm, p["q_norm_w"], p["kv_norm_w"], wv["wuq_t"], wv["wukv_t"],
                               place, "qkv_fwd")
    attn, lse = flash_fwd(q, k, vt, "flash_fwd")
    x2, o2, ym, h3 = out_proj(ys, attn, p["mla_norm_w"], wv["wo"], x1, g2, (p["norm_ffn2"], sc3, sh3), "out_proj")
    gg3, uu3, a3 = ffn_up(h3, wv["wg2_t"], wv["wu2_t"], "ffn2_up")
    loss, dx3, dnfin, do3, dg3 = ffn_down_final(a3, wv["wd2"], x2, g3, 0.5, p["norm_final"], tgt, "ffn2_down_loss")

    dx2, dmod3, dnf2, (dwg2, dwu2, dwd2), (dout, dg2) = _ffn_bwd(
        "ffn2", dx3, do3, dg3, x2, h3, gg3, uu3, a3, sc3, p["norm_ffn2"], wv["wg2_t"], wv["wu2_t"], wv["wd2"],
        (o2, g2, 1.0))
    dys, dattn, dlt, dmlan = out_proj_bwd(dout, attn, p["mla_norm_w"], wv["wo"], "out_proj_bwd")
    dwo = mm_tn_blocks([ys, ym], dout, "dwo")
    dxs, dbm, dcm, dz, ddt, dssdn, ddsk_lane, ddtb, dalog = ssd_bwd(
        dys, y, z, xs, bm, cm_, misc, prev, dtb, alog, dskip_e, p["ssd_norm_w"], e_mat, et_mat, "ssd_bwd")
    dq, dk, dv = flash_bwd(q, k, v, dattn, lse, dlt, "flash_bwd")
    dcq, dckv, dmisc, dqp, dkvc, dqn, dkvn = qkv_bwd(dq, dk, dv, ddt, cq, ckv, cc, sp, sm, p["q_norm_w"], p["kv_norm_w"],
                                                     wv["wuq_t"], wv["wukv_t"], place.T, "qkv_bwd")
    dwuq = mm_tn(dqp, qn, MLA_HEADS * HEAD_PAD, Q_LORA, "dwuq")
    dwukv = mm_tn(dkvc, kvn, MLA_HEADS * HEAD_PAD, KV_LORA, "dwukv")
    du, dconv = conv_bwd(dxs, dbm, dcm, u, p["conv_w"], p["conv_b"], "conv_bwd")
    dconv = conv_grads_fold(dconv, "conv_grads_fold")
    dproj = [dz, du, dcq, dckv, dmisc]
    dwin = mm_tn_blocks(dproj, h2, "dwin")
    dx1, dsc2, dsh2, dnmix, do1, dg1 = dh_norm_bwd(dproj, [wv["win_t"]], x1, dx2, p["norm_mix"], sc2, "mix_dh",
                                                   (o1, g1, 0.5))
    dx0, dmod1, dnf1, (dwg1, dwu1, dwd1), _ = _ffn_bwd(
        "ffn1", dx1, do1, dg1, x, h1, gg1, uu1, a1, sc1, p["norm_ffn1"], wv["wg1_t"], wv["wu1_t"], wv["wd1"], None)

    dmod = jnp.concatenate([*dmod1, dsh2, dsc2, dg2, *dmod3], axis=1).reshape(nb, N_MOD * d)
    return dict(
        loss=loss, dx=dx0, dmod=dmod, norm_ffn1=dnf1, norm_mix=dnmix, norm_ffn2=dnf2, norm_final=dnfin,
        ssd_norm_w=dssdn, mla_norm_w=dmlan, q_norm_w=dqn, kv_norm_w=dkvn,
        dt_bias=ddtb[:, :SSD_HEADS], a_log=dalog[:, :SSD_HEADS],
        d_skip=squeeze_heads(ddsk_lane, et_mat, "d_skip_heads")[:, :SSD_HEADS],
        conv_b=dconv[4:5], conv_w=dconv[0:4],
        gw=dict(ffn1_w_gate=dwg1, ffn1_w_up=dwu1, ffn1_w_down=dwd1, ffn2_w_gate=dwg2, ffn2_w_up=dwu2, ffn2_w_down=dwd2,
                w_out=dwo, w_in=dwin, w_ukv=dwukv, w_uq=dwuq))


def kernel(x, c, positions, w_ada, b_ada, norm_ffn1, ffn1_w_gate, ffn1_w_up, ffn1_w_down, norm_mix, w_in, conv_w, conv_b, dt_bias, a_log, d_skip, ssd_norm_w, q_norm_w, w_uq, kv_norm_w, w_ukv, mla_norm_w, w_out, norm_ffn2, ffn2_w_gate, ffn2_w_up, ffn2_w_down, norm_final, loss_target, m_w_ada, m_b_ada, m_norm_ffn1, m_ffn1_w_gate, m_ffn1_w_up, m_ffn1_w_down, m_norm_mix, m_w_in, m_conv_w, m_conv_b, m_dt_bias, m_a_log, m_d_skip, m_ssd_norm_w, m_q_norm_w, m_w_uq, m_kv_norm_w, m_w_ukv, m_mla_norm_w, m_w_out, m_norm_ffn2, m_ffn2_w_gate, m_ffn2_w_up, m_ffn2_w_down, m_norm_final, v_w_ada, v_b_ada, v_norm_ffn1, v_ffn1_w_gate, v_ffn1_w_up, v_ffn1_w_down, v_norm_mix, v_w_in, v_conv_w, v_conv_b, v_dt_bias, v_a_log, v_d_skip, v_ssd_norm_w, v_q_norm_w, v_w_uq, v_kv_norm_w, v_w_ukv, v_mla_norm_w, v_w_out, v_norm_ffn2, v_ffn2_w_gate, v_ffn2_w_up, v_ffn2_w_down, v_norm_final):
    names = ["w_ada", "b_ada", "norm_ffn1", "ffn1_w_gate", "ffn1_w_up", "ffn1_w_down", "norm_mix", "w_in", "conv_w",
             "conv_b", "dt_bias", "a_log", "d_skip", "ssd_norm_w", "q_norm_w", "w_uq", "kv_norm_w", "w_ukv",
             "mla_norm_w", "w_out", "norm_ffn2", "ffn2_w_gate", "ffn2_w_up", "ffn2_w_down", "norm_final"]
    W = dict(zip(names, (w_ada, b_ada, norm_ffn1, ffn1_w_gate, ffn1_w_up, ffn1_w_down, norm_mix, w_in, conv_w, conv_b, dt_bias, a_log, d_skip, ssd_norm_w, q_norm_w, w_uq, kv_norm_w, w_ukv, mla_norm_w, w_out, norm_ffn2, ffn2_w_gate, ffn2_w_up, ffn2_w_down, norm_final)))
    M = dict(zip(names, (m_w_ada, m_b_ada, m_norm_ffn1, m_ffn1_w_gate, m_ffn1_w_up, m_ffn1_w_down, m_norm_mix, m_w_in, m_conv_w, m_conv_b, m_dt_bias, m_a_log, m_d_skip, m_ssd_norm_w, m_q_norm_w, m_w_uq, m_kv_norm_w, m_w_ukv, m_mla_norm_w, m_w_out, m_norm_ffn2, m_ffn2_w_gate, m_ffn2_w_up, m_ffn2_w_down, m_norm_final)))
    V = dict(zip(names, (v_w_ada, v_b_ada, v_norm_ffn1, v_ffn1_w_gate, v_ffn1_w_up, v_ffn1_w_down, v_norm_mix, v_w_in, v_conv_w, v_conv_b, v_dt_bias, v_a_log, v_d_skip, v_ssd_norm_w, v_q_norm_w, v_w_uq, v_kv_norm_w, v_w_ukv, v_mla_norm_w, v_w_out, v_norm_ffn2, v_ffn2_w_gate, v_ffn2_w_up, v_ffn2_w_down, v_norm_final)))

    nb, s, d = x.shape
    me = 4 * lax.axis_index("x") + 2 * lax.axis_index("y") + lax.axis_index("c")
    n_ada = w_ada.shape[2]

    taps, n_cw = conv_w.shape[1:]
    (cg,) = all_gather8([_pack_rows([c, conv_w[0]])], "gather_c")
    c_all = cg[:, 0:nb].reshape(N_DEV * nb, d)
    conv_w_full = cg[:, nb, 0:taps * n_cw].reshape(N_DEV, taps, n_cw).transpose(1, 0, 2).reshape(taps, N_DEV * n_cw)
    shards = [[_shard_view(name, W[name]).astype(BF16) for name in group] for group in GATHER_GROUPS]

    b_ada_cols = lax.dynamic_slice(b_ada, (0, me * n_ada), (1, n_ada))
    mod_cols, c_act = adaln_fwd(c_all, w_ada[0], b_ada_cols, "adaln_fwd")
    (mod_g,) = all_gather8([mod_cols], "gather_mod")
    mod = lax.dynamic_slice(mod_g, (0, me * nb, 0), (N_DEV, nb, n_ada)).transpose(1, 0, 2).reshape(nb, N_MOD, 1, d)
    mod = [mod[:, k] for k in range(N_MOD)]
    h1, *ffn1_w = norm_mod(x, norm_ffn1, mod[1], mod[0], "ffn1_norm", gather=shards[0])
    gathered = dict(zip(GATHER_GROUPS[0], ffn1_w))
    gathered, h1, shards = lax.optimization_barrier((gathered, h1, shards))
    gathered.update(zip(GATHER_GROUPS[1], sc_all_gather8(shards[1], "gather_w_ffn2", 1)))
    gathered.update(zip(GATHER_GROUPS[2], sc_all_gather8(shards[2], "gather_w_mixer", 7)))
    wv = weight_views(gathered)

    P = dict(W)
    P["conv_w"] = conv_w_full
    P["norm_final"] = norm_final.reshape(1, d)
    R = local_step(x, loss_target, positions, mod, wv, P, h1)

    dmod = R["dmod"]
    partial_shapes = [(1,), (1, d), (1, d), (1, d), (1, d), (1, d), (1, d), (1, Q_LORA), (1, KV_LORA),
                      (1, SSD_HEADS), (1, SSD_HEADS), (1, SSD_HEADS), (1, D_CONV), (4, D_CONV), (1, N_MOD * d),
                      (nb, N_MOD * d)]
    partial = _pack_rows([R["loss"][0, :1], R["norm_ffn1"], R["norm_mix"], R["norm_ffn2"], R["norm_final"],
                          R["ssd_norm_w"], R["mla_norm_w"], R["q_norm_w"], R["kv_norm_w"],
                          R["dt_bias"], R["a_log"], R["d_skip"], R["conv_b"], R["conv_w"],
                          sum_rows(dmod, "dmod_rows"), dmod])
    (partial_g,) = all_gather8([partial], "gather_partials")
    (loss, g_nf1, g_nmix, g_nf2, g_nfin, g_ssdn, g_mlan, g_qn, g_kvn, g_dtb, g_alog, g_dskip, g_convb, g_convw,
     g_bada, _) = _unpack_rows(sum_blocks(partial_g, "sum_partials"), partial_shapes)
    dmod_row = sum(-(-math.prod(shp) // D_MODEL) for shp in partial_shapes[:-1])
    dmod_all = partial_g[:, dmod_row:dmod_row + nb * N_MOD].reshape(N_DEV * nb, N_MOD * d)
    g_wada = adaln_bwd(c_act, lax.dynamic_slice(dmod_all, (0, me * n_ada), (N_DEV * nb, n_ada)), "adaln_bwd")
    n_cw = conv_w.shape[2]
    G = {"w_ada": g_wada[None], "b_ada": g_bada, "norm_ffn1": g_nf1, "norm_mix": g_nmix, "norm_ffn2": g_nf2,
         "norm_final": g_nfin.reshape(d), "ssd_norm_w": g_ssdn, "mla_norm_w": g_mlan, "q_norm_w": g_qn,
         "kv_norm_w": g_kvn, "dt_bias": g_dtb, "a_log": g_alog, "d_skip": g_dskip, "conv_b": g_convb,
         "conv_w": lax.dynamic_slice(g_convw, (0, me * n_cw), (4, n_cw))[None]}

    DW, NM, NV = {}, {}, {}
    gw = R["gw"]
    for k, (tag, group) in enumerate(GRAD_GROUPS):
        send = [_grad_blocks(name, gw[name]).reshape(N_DEV, *_shard_view(name, W[name]).shape) for name in group]
        recv = sc_all_to_all8(send, "exchange_" + tag, 2 + k)
        for name, blocks in zip(group, recv):
            res = adamw_blocks(_shard_view(name, W[name]), blocks, _shard_view(name, M[name]), _shard_view(name, V[name]),
                               "adamw_" + name)
            G[name], DW[name], NM[name], NV[name] = [_shard_unview(name, t) for t in res]
    DW["w_ada"], NM["w_ada"], NV["w_ada"] = adamw(w_ada, g_wada, m_w_ada, v_w_ada, "adamw_w_ada")
    small = [n for n in names if n not in DW]
    as2d = lambda a: a.reshape(-1, a.shape[-1])
    outs = adamw_many([as2d(W[n]) for n in small], [as2d(G[n]) for n in small], [as2d(M[n]) for n in small],
                      [as2d(V[n]) for n in small], "adamw_small")
    for res, dst in zip(outs, (DW, NM, NV)):
        for n, t in zip(small, res):
            dst[n] = t.reshape(W[n].shape)
    return (loss.reshape(()), R["dx"], *[G[n] for n in names], *[DW[n] for n in names], *[NM[n] for n in names],
            *[NV[n] for n in names])
```

```python
import math

import jax
import jax.numpy as jnp
from jax import lax
from jax.experimental import pallas as pl
from jax.experimental.pallas import tpu as pltpu
from jax.experimental.pallas import tpu_sc as plsc

F32, BF16, I32 = jnp.float32, jnp.bfloat16, jnp.int32
HI = lax.Precision.HIGHEST
SDS = jax.ShapeDtypeStruct
MESH = pl.DeviceIdType.MESH

D_MODEL = 1024
D_FF = 2816
D_SSD = 1024
SSD_HEADS = 16
SSD_HEAD_DIM = 64
SSD_GROUPS = 2
SSD_STATE = 128
CHUNK = 128
MLA_HEADS = 8
QK_NOPE = 64
QK_ROPE = 32
QK_DIM = 96
V_HEAD = 128
Q_LORA = 384
KV_LORA = 256
ROPE_THETA = 10000.0
N_MOD = 9
EPS = 1e-6
D_CONV = 1536
D_IN = 3248
D_IN_PAD = 3328
HEAD_PAD = 128
N_DEV = 8
ADAM_LR, ADAM_B1, ADAM_B2, ADAM_EPS, ADAM_WD, ADAM_STEP = 0.001, 0.9, 0.999, 1e-08, 0.01, 10

SAVED_ACT = BF16
VMEM_LIMIT = 56 * 1024 * 1024
LANES = 128
NT_DIMS = (((1,), (1,)), ((), ()))
TN_DIMS = (((0,), (0,)), ((), ()))


def _cparams(n_axes):
    return pltpu.CompilerParams(dimension_semantics=("arbitrary",) * n_axes, vmem_limit_bytes=VMEM_LIMIT)


def _row(tm, d):
    return pl.BlockSpec((None, tm, d), lambda b, i: (b, i, 0))


def _bvec(d):
    return pl.BlockSpec((None, 1, d), lambda b, i: (b, 0, 0))


def _full(shape):
    n = len(shape)
    return pl.BlockSpec(shape, lambda *_: (0,) * n)


def _sigmoid(x):
    return 1.0 / (1.0 + jnp.exp(-x))


def _softplus(x):
    return jnp.maximum(x, 0.0) + jnp.log(1.0 + jnp.exp(-jnp.abs(x)))


def _rms(x):
    return lax.rsqrt(jnp.mean(x * x, axis=-1, keepdims=True) + EPS)


def _rms_bwd(dn, n, r):
    return r * (dn - n * jnp.mean(dn * n, axis=-1, keepdims=True))


def _first_step():
    return (pl.program_id(0) == 0) & (pl.program_id(1) == 0)


def _gather_copies(x_refs, out_refs, send_sems, recv_sems, local_sems, part=None):
    mx, my, mc = lax.axis_index("x"), lax.axis_index("y"), lax.axis_index("c")
    me, sibling = (mx, my, mc), (mx, my, 1 - mc)
    chips = [(1 - mx, my), (mx, 1 - my), (1 - mx, 1 - my)]

    def copy(a, k, block, to, src=None):
        rows = out_refs[a].at[4 * block[0] + 2 * block[1] + block[2]]
        return pltpu.make_async_remote_copy(
            src_ref=rows if src is None else src, dst_ref=rows,
            send_sem=send_sems.at[7 * a + k], recv_sem=recv_sems.at[7 * a + k], device_id=to, device_id_type=MESH)

    arrays = range(len(x_refs))
    mine = [pltpu.make_async_copy(x_refs[a], out_refs[a].at[4 * mx + 2 * my + mc], local_sems.at[a]) for a in arrays]
    first = [[copy(a, 0, me, sibling, src=x_refs[a])] + [copy(a, 1 + j, me, (*chip, mc), src=x_refs[a])
                                                          for j, chip in enumerate(chips)] for a in arrays]
    passed = [[copy(a, 4 + j, (*chip, mc), sibling) for j, chip in enumerate(chips)] for a in arrays]
    if part != "finish":
        for a in arrays:
            mine[a].start()
            for cp in first[a]:
                cp.start()
    if part == "start":
        return
    for a in arrays:
        for j, chip in enumerate(chips):
            copy(a, 1 + j, (*chip, mc), me).wait_recv()
            passed[a][j].start()
    for a in arrays:
        copy(a, 0, sibling, me).wait_recv()
        for j, chip in enumerate(chips):
            copy(a, 4 + j, (*chip, 1 - mc), me).wait_recv()
    for a in arrays:
        for cp in first[a] + passed[a]:
            cp.wait_send()
        mine[a].wait()


def _gather_peers():
    mx, my, mc = lax.axis_index("x"), lax.axis_index("y"), lax.axis_index("c")
    return [(mx, my, 1 - mc), (1 - mx, my, mc), (mx, 1 - my, mc), (1 - mx, 1 - my, mc)]


def _comm_scratch(n):
    return [pltpu.SemaphoreType.DMA((7 * n,)), pltpu.SemaphoreType.DMA((7 * n,)), pltpu.SemaphoreType.DMA((n,))]


def all_gather8(xs, name):
    n = len(xs)

    def body(*refs):
        _gather_copies(refs[:n], refs[n:2 * n], *refs[2 * n:])

    return pl.pallas_call(
        body, name=name,
        out_shape=[SDS((N_DEV, *x.shape), x.dtype) for x in xs],
        in_specs=[pl.BlockSpec(memory_space=pl.ANY)] * n,
        out_specs=[pl.BlockSpec(memory_space=pl.ANY)] * n,
        scratch_shapes=_comm_scratch(n),
    )(*xs)


def _exchange_peers():
    mx, my, mc = lax.axis_index("x"), lax.axis_index("y"), lax.axis_index("c")
    return [(1 - mx if rel & 4 else mx, 1 - my if rel & 2 else my, 1 - mc if rel & 1 else mc) for rel in range(1, N_DEV)]


def _exchange_copies(x_refs, out_refs, send_sems, recv_sems, local_sems):
    mx, my, mc = lax.axis_index("x"), lax.axis_index("y"), lax.axis_index("c")
    me = 4 * mx + 2 * my + mc
    copies = []
    for a, (x_ref, out_ref) in enumerate(zip(x_refs, out_refs)):
        mine = pltpu.make_async_copy(x_ref.at[me], out_ref.at[me], local_sems.at[a])
        mine.start()
        copies.append(mine)
        for k, (px, py, pc) in enumerate(_exchange_peers()):
            cp = pltpu.make_async_remote_copy(
                src_ref=x_ref.at[4 * px + 2 * py + pc], dst_ref=out_ref.at[me],
                send_sem=send_sems.at[7 * a + k], recv_sem=recv_sems.at[7 * a + k],
                device_id=(px, py, pc), device_id_type=MESH)
            cp.start()
            copies.append(cp)
    for cp in copies:
        cp.wait()


def all_to_all8(xs, name):
    n = len(xs)

    def body(*refs):
        _exchange_copies(refs[:n], refs[n:2 * n], *refs[2 * n:])

    return pl.pallas_call(
        body, name=name,
        out_shape=[SDS(x.shape, x.dtype) for x in xs],
        in_specs=[pl.BlockSpec(memory_space=pl.ANY)] * n,
        out_specs=[pl.BlockSpec(memory_space=pl.ANY)] * n,
        scratch_shapes=_comm_scratch(n),
    )(*xs)


def _sequencer_kernel(name, collective_id, n_arrays):
    return pl.kernel(
        mesh=plsc.ScalarSubcoreMesh(axis_name="seq", num_cores=1), name=name,
        scratch_types=tuple(_comm_scratch(n_arrays)),
        compiler_params=pltpu.CompilerParams(collective_id=collective_id))


def _handshake(peers):
    barrier = pltpu.get_barrier_semaphore()
    for peer in peers:
        pl.semaphore_signal(barrier, inc=1, device_id=peer, device_id_type=MESH)
    pl.semaphore_wait(barrier, len(peers))


def _hbm_refs(xs, out_shapes):
    x_refs = [jax.new_ref(x, memory_space=pltpu.MemorySpace.HBM) for x in xs]
    out_refs = [jax.empty_ref(SDS(shp, x.dtype), memory_space=pltpu.MemorySpace.HBM) for x, shp in zip(xs, out_shapes)]
    return x_refs, out_refs


def sc_all_gather8(xs, name, collective_id):
    x_refs, out_refs = _hbm_refs(xs, [(N_DEV, *x.shape) for x in xs])

    @_sequencer_kernel(name, collective_id, len(xs))
    def launch(send_sems, recv_sems, local_sems):
        _handshake(_gather_peers())
        _gather_copies(x_refs, out_refs, send_sems, recv_sems, local_sems)

    launch()
    return [ref[...] for ref in out_refs]


def sc_all_to_all8(xs, name, collective_id):
    x_refs, out_refs = _hbm_refs(xs, [x.shape for x in xs])

    @_sequencer_kernel(name, collective_id, len(xs))
    def launch(send_sems, recv_sems, local_sems):
        _handshake(_exchange_peers())
        _exchange_copies(x_refs, out_refs, send_sems, recv_sems, local_sems)

    launch()
    return [ref[...] for ref in out_refs]


def norm_mod(x, w, sc, sh, name, gather=()):
    b, s, d = x.shape
    tm = min(512, s)
    n = len(gather)
    last = (b - 1, s // tm - 1)

    def body(x_ref, w_ref, sc_ref, sh_ref, *refs):
        h_ref = refs[n]
        if n:
            comm = (refs[:n], refs[n + 1:2 * n + 1], *refs[2 * n + 1:])

            @pl.when(_first_step())
            def _():
                _gather_copies(*comm, part="start")
        xv = x_ref[...]
        nv = xv * _rms(xv)
        h_ref[...] = ((nv * w_ref[...]) * (1.0 + sc_ref[...]) + sh_ref[...]).astype(BF16)
        if n:
            @pl.when((pl.program_id(0) == last[0]) & (pl.program_id(1) == last[1]))
            def _():
                _gather_copies(*comm, part="finish")

    hbm = [pl.BlockSpec(memory_space=pl.ANY)] * n
    res = pl.pallas_call(
        body, name=name, grid=(b, s // tm),
        in_specs=[_row(tm, d), _full((1, d)), _bvec(d), _bvec(d)] + hbm,
        out_specs=[_row(tm, d)] + hbm,
        out_shape=[SDS((b, s, d), BF16)] + [SDS((N_DEV, *g.shape), g.dtype) for g in gather],
        scratch_shapes=_comm_scratch(n) if n else [], compiler_params=_cparams(2))(x, w, sc, sh, *gather)
    return res if n else res[0]


def ffn_up(h, wg_t, wu_t, name):
    b, s, d = h.shape
    f = wg_t.shape[0]
    tm, tn = min(1024, s), f // 2

    def body(h_ref, wg_ref, wu_ref, s_ref, t_ref, a_ref):
        hv = h_ref[...]
        g = lax.dot_general(hv, wg_ref[...], NT_DIMS, preferred_element_type=F32)
        u = lax.dot_general(hv, wu_ref[...], NT_DIMS, preferred_element_type=F32)
        sg = _sigmoid(g)
        silu = g * sg
        s_ref[...] = silu.astype(s_ref.dtype)
        t_ref[...] = (u * (sg + silu * (1.0 - sg))).astype(t_ref.dtype)
        a_ref[...] = (silu * u).astype(BF16)

    hs = pl.BlockSpec((None, tm, d), lambda j, bb, i: (bb, i, 0))
    ws = pl.BlockSpec((tn, d), lambda j, bb, i: (j, 0))
    os_ = pl.BlockSpec((None, tm, tn), lambda j, bb, i: (bb, i, j))
    return pl.pallas_call(
        body, name=name, grid=(f // tn, b, s // tm),
        in_specs=[hs, ws, ws], out_specs=[os_, os_, os_],
        out_shape=[SDS((b, s, f), SAVED_ACT), SDS((b, s, f), SAVED_ACT), SDS((b, s, f), BF16)],
        compiler_params=_cparams(3))(h, wg_t, wu_t)


def _norm_mod_tile(xv, w_ref, sc_ref, sh_ref):
    return ((xv * _rms(xv) * w_ref[...]) * (1.0 + sc_ref[...]) + sh_ref[...]).astype(BF16)


def ffn_down(a, wd, x, gate, scale, name, above=None):
    b, s, f = a.shape
    d = wd.shape[1]
    tm = min(1024, s)

    def body(a_ref, wd_ref, x_ref, g_ref, *rest):
        xn_ref, o_ref = rest[-3:-1] if above else rest
        o = jnp.dot(a_ref[...], wd_ref[...], preferred_element_type=F32)
        xn = x_ref[...] + (scale * g_ref[...]) * o
        xn_ref[...] = xn
        o_ref[...] = o.astype(BF16)
        if above:
            rest[-1][...] = _norm_mod_tile(xn, *rest[0:3])

    extra = above is not None
    return pl.pallas_call(
        body, name=name, grid=(b, s // tm),
        in_specs=[_row(tm, f), _full((f, d)), _row(tm, d), _bvec(d)] + ([_full((1, d)), _bvec(d), _bvec(d)] if extra else []),
        out_specs=[_row(tm, d), _row(tm, d)] + ([_row(tm, d)] if extra else []),
        out_shape=[SDS((b, s, d), F32), SDS((b, s, d), BF16)] + ([SDS((b, s, d), BF16)] if extra else []),
        compiler_params=_cparams(2))(a, wd, x, gate, *(above or ()))


def ffn_down_final(a, wd, x, gate, scale, w_final, tgt, name):
    b, s, f = a.shape
    d = wd.shape[1]
    tm = min(1024, s)

    def body(a_ref, wd_ref, x_ref, g_ref, w_ref, t_ref, loss_ref, dx_ref, dw_ref, do_ref, dg_ref):
        @pl.when(_first_step())
        def _():
            loss_ref[...] = jnp.zeros_like(loss_ref)
            dw_ref[...] = jnp.zeros_like(dw_ref)

        @pl.when(pl.program_id(1) == 0)
        def _():
            dg_ref[...] = jnp.zeros_like(dg_ref)
        o = jnp.dot(a_ref[...], wd_ref[...], preferred_element_type=F32)
        sg = scale * g_ref[...]
        xv = x_ref[...] + sg * o
        r = _rms(xv)
        n = xv * r
        wv = w_ref[...]
        e = n * wv - t_ref[...]
        loss_ref[...] += jnp.sum(e * e) * (0.5 / d)
        dy = e * (1.0 / d)
        dw_ref[...] += jnp.sum(dy * n, axis=0, keepdims=True)
        dx = _rms_bwd(dy * wv, n, r)
        dx_ref[...] = dx
        do_ref[...] = (sg * dx).astype(BF16)
        dg_ref[...] += jnp.sum(scale * dx * o, axis=0, keepdims=True)

    return pl.pallas_call(
        body, name=name, grid=(b, s // tm),
        in_specs=[_row(tm, f), _full((f, d)), _row(tm, d), _bvec(d), _full((1, d)), _row(tm, d)],
        out_specs=[_full((1, LANES)), _row(tm, d), _full((1, d)), _row(tm, d), _bvec(d)],
        out_shape=[SDS((1, LANES), F32), SDS((b, s, d), F32), SDS((1, d), F32), SDS((b, s, d), BF16), SDS((b, 1, d), F32)],
        compiler_params=_cparams(2))(a, wd, x, gate, w_final, tgt)


def ffn_dact(do, wd, silu_g, u_dsilu, name):
    b, s, d = do.shape
    f = wd.shape[0]
    tm, tn = min(1024, s), f // 2

    def body(do_ref, wd_ref, s_ref, t_ref, dg_ref, du_ref):
        da = lax.dot_general(do_ref[...], wd_ref[...], NT_DIMS, preferred_element_type=F32)
        dg_ref[...] = (da * t_ref[...].astype(F32)).astype(BF16)
        du_ref[...] = (da * s_ref[...].astype(F32)).astype(BF16)

    dos = pl.BlockSpec((None, tm, d), lambda j, bb, i: (bb, i, 0))
    ws = pl.BlockSpec((tn, d), lambda j, bb, i: (j, 0))
    es = pl.BlockSpec((None, tm, tn), lambda j, bb, i: (bb, i, j))
    return pl.pallas_call(
        body, name=name, grid=(f // tn, b, s // tm),
        in_specs=[dos, ws, es, es], out_specs=[es, es],
        out_shape=[SDS((b, s, f), BF16), SDS((b, s, f), BF16)], compiler_params=_cparams(3))(do, wd, silu_g, u_dsilu)


def mm_tn(a, bm, tma, tnb, name):
    b, s, ka = a.shape
    nb = bm.shape[2]
    tk = min(2048, s)
    nk = s // tk

    def body(a_ref, b_ref, o_ref, acc):
        first = (pl.program_id(2) == 0) & (pl.program_id(3) == 0)
        last = (pl.program_id(2) == b - 1) & (pl.program_id(3) == nk - 1)
        part = lax.dot_general(a_ref[...], b_ref[...], TN_DIMS, preferred_element_type=F32)

        @pl.when(first)
        def _():
            acc[...] = part

        @pl.when(jnp.logical_not(first))
        def _():
            acc[...] += part

        @pl.when(last)
        def _():
            o_ref[...] = acc[...].astype(BF16)

    return pl.pallas_call(
        body, name=name, grid=(ka // tma, nb // tnb, b, nk),
        in_specs=[pl.BlockSpec((None, tk, tma), lambda i, j, bb, k: (bb, k, i)),
                  pl.BlockSpec((None, tk, tnb), lambda i, j, bb, k: (bb, k, j))],
        out_specs=pl.BlockSpec((tma, tnb), lambda i, j, bb, k: (i, j)),
        out_shape=SDS((ka, nb), BF16), scratch_shapes=[pltpu.VMEM((tma, tnb), F32)],
        compiler_params=_cparams(4))(a, bm)


def mm_tn_blocks(a_blocks, bm, name, out_rows=None):
    b, s, nb = bm.shape
    widths = [a.shape[2] for a in a_blocks]
    starts = [sum(widths[:k]) for k in range(len(widths))]
    out_rows = out_rows or ((0, 0, sum(widths)),)
    n_out = sum(seg[2] for seg in out_rows)
    tk = min(2048 if sum(widths) <= 2048 else 1024, s)
    nk = s // tk
    n = len(a_blocks)

    def body(*refs):
        a_refs, b_ref, o_ref, acc = refs[:n], refs[n], refs[n + 1], refs[n + 2]
        first = (pl.program_id(0) == 0) & (pl.program_id(1) == 0)
        last = (pl.program_id(0) == b - 1) & (pl.program_id(1) == nk - 1)

        @pl.when(first)
        def _():
            acc[...] = jnp.zeros_like(acc)
        bv = b_ref[...]
        for a_ref, st, wd in zip(a_refs, starts, widths):
            acc[st:st + wd, :] += lax.dot_general(a_ref[...], bv, TN_DIMS, preferred_element_type=F32)

        @pl.when(last)
        def _():
            for dst, src, rows in out_rows:
                o_ref[dst:dst + rows, :] = acc[src:src + rows, :].astype(BF16)

    return pl.pallas_call(
        body, name=name, grid=(b, nk),
        in_specs=[_row(tk, wd) for wd in widths] + [_row(tk, nb)],
        out_specs=_full((n_out, nb)), out_shape=SDS((n_out, nb), BF16),
        scratch_shapes=[pltpu.VMEM((sum(widths), nb), F32)], compiler_params=_cparams(2))(*a_blocks, bm)


def _gate_bwd_specs(tm, d, b, s):
    return ([_row(tm, d), _bvec(d)], [_row(tm, d), _bvec(d)], [SDS((b, s, d), BF16), SDS((b, 1, d), F32)])


def _gate_bwd_tile(dx, scale, o_ref, g_ref, do_ref, dg_ref):
    do_ref[...] = ((scale * g_ref[...]) * dx).astype(BF16)
    dg_ref[...] += jnp.sum(scale * dx * o_ref[...].astype(F32), axis=0, keepdims=True)


def dh_norm_bwd(dys, wts, x, dxn, w, sc, name, below=None):
    b, s, d = x.shape
    tm = min(512, s)
    n_in, n_w = len(dys), len(wts)
    extra_in, extra_out, extra_shape = _gate_bwd_specs(tm, d, b, s) if below else ([], [], [])
    starts = [sum(dy.shape[2] for dy in dys[:k]) for k in range(n_in)]

    def body(*refs):
        dy_refs, w_refs = refs[:n_in], refs[n_in:n_in + n_w]
        x_ref, dxn_ref, nw_ref, sc_ref = refs[n_in + n_w:n_in + n_w + 4]
        rest = refs[n_in + n_w + 4:]
        if below:
            o_ref, g_ref, dx_ref, dsc_ref, dsh_ref, dw_ref, do_ref, dg_ref = rest
        else:
            dx_ref, dsc_ref, dsh_ref, dw_ref = rest

        @pl.when(pl.program_id(1) == 0)
        def _():
            dsc_ref[...] = jnp.zeros_like(dsc_ref)
            dsh_ref[...] = jnp.zeros_like(dsh_ref)
            if below:
                dg_ref[...] = jnp.zeros_like(dg_ref)

        @pl.when(_first_step())
        def _():
            dw_ref[...] = jnp.zeros_like(dw_ref)

        def weight(k):
            return w_refs[k][...] if n_w == n_in else w_refs[0][starts[k]:starts[k] + dys[k].shape[2], :]

        dh = jnp.dot(dy_refs[0][...], weight(0), preferred_element_type=F32)
        for k in range(1, n_in):
            dh += jnp.dot(dy_refs[k][...], weight(k), preferred_element_type=F32)
        xv = x_ref[...]
        r = _rms(xv)
        n = xv * r
        nw = nw_ref[...]
        dsc_ref[...] += jnp.sum(dh * (n * nw), axis=0, keepdims=True)
        dsh_ref[...] += jnp.sum(dh, axis=0, keepdims=True)
        dhn = dh * (1.0 + sc_ref[...])
        dw_ref[...] += jnp.sum(dhn * n, axis=0, keepdims=True)
        dx = dxn_ref[...] + _rms_bwd(dhn * nw, n, r)
        dx_ref[...] = dx
        if below:
            _gate_bwd_tile(dx, below[2], o_ref, g_ref, do_ref, dg_ref)

    resident = lambda shape: pl.BlockSpec(shape, lambda *_: (0,) * len(shape), pipeline_mode=pl.Buffered(1))
    in_specs = [_row(tm, dy.shape[2]) for dy in dys] + [resident(wt.shape) for wt in wts]
    in_specs += [_row(tm, d), _row(tm, d), _full((1, d)), _bvec(d)] + extra_in
    return pl.pallas_call(
        body, name=name, grid=(b, s // tm), in_specs=in_specs,
        out_specs=[_row(tm, d), _bvec(d), _bvec(d), _full((1, d))] + extra_out,
        out_shape=[SDS((b, s, d), F32), SDS((b, 1, d), F32), SDS((b, 1, d), F32), SDS((1, d), F32)] + extra_shape,
        compiler_params=_cparams(2))(*dys, *wts, x, dxn, w, sc, *(below[:2] if below else ()))


def in_proj(h, win_t, name):
    b, s, d = h.shape
    tm = min(512, s)
    widths = (D_SSD, D_SSD + 2 * SSD_GROUPS * SSD_STATE, Q_LORA, KV_LORA, LANES)

    def body(h_ref, w_ref, *outs):
        p = lax.dot_general(h_ref[...], w_ref[...], NT_DIMS, preferred_element_type=F32)
        off = 0
        for o_ref, wd in zip(outs, widths):
            o_ref[...] = p[:, off:off + wd]
            off += wd

    return pl.pallas_call(
        body, name=name, grid=(b, s // tm),
        in_specs=[_row(tm, d), _full(win_t.shape)],
        out_specs=[_row(tm, wd) for wd in widths],
        out_shape=[SDS((b, s, wd), F32) for wd in widths], compiler_params=_cparams(2))(h, win_t)


def _halo_prev(ts, d):
    return pl.BlockSpec((None, 8, d), lambda b, i: (b, jnp.maximum(i * (ts // 8) - 1, 0), 0))


CONV_ROWS = 32


def _conv_head(head, u_ref, up_ref, tile):
    head[0:8, :] = jnp.where(tile > 0, up_ref[...], 0.0)
    head[8:8 + CONV_ROWS, :] = u_ref[0:CONV_ROWS, :]


def _conv_windows(u_ref, head, r0):
    if r0 == 0:
        return [head[5 + k:5 + k + CONV_ROWS, :] for k in range(4)]
    return [u_ref[r0 - 3 + k:r0 - 3 + k + CONV_ROWS, :] for k in range(4)]


def _fold8(t):
    acc = t[0:8, :]
    for r in range(8, CONV_ROWS, 8):
        acc += t[r:r + 8, :]
    return acc


def conv_fwd(u, cw, cb, name):
    b, s, dc = u.shape
    ts = min(512, s)
    widths = (D_SSD, SSD_GROUPS * SSD_STATE, SSD_GROUPS * SSD_STATE)

    def body(u_ref, up_ref, w_ref, b_ref, xs_ref, bm_ref, cm_ref, head):
        _conv_head(head, u_ref, up_ref, pl.program_id(1))
        ws = [w_ref[k:k + 1, :] for k in range(4)]
        bias = b_ref[...]
        for r0 in range(0, ts, CONV_ROWS):
            taps = _conv_windows(u_ref, head, r0)
            v = bias + taps[0] * ws[0] + taps[1] * ws[1] + taps[2] * ws[2] + taps[3] * ws[3]
            y = v * _sigmoid(v)
            rs = slice(r0, r0 + CONV_ROWS)
            xs_ref[rs, :] = y[:, 0:D_SSD]
            bm_ref[rs, :] = y[:, D_SSD:D_SSD + 256]
            cm_ref[rs, :] = y[:, D_SSD + 256:D_SSD + 512]

    return pl.pallas_call(
        body, name=name, grid=(b, s // ts),
        in_specs=[_row(ts, dc), _halo_prev(ts, dc), _full((4, dc)), _full((1, dc))],
        out_specs=[_row(ts, wd) for wd in widths],
        out_shape=[SDS((b, s, wd), F32) for wd in widths],
        scratch_shapes=[pltpu.VMEM((8 + CONV_ROWS, dc), F32)], compiler_params=_cparams(2))(u, u, cw, cb)


def conv_bwd(dxs, dbm, dcm, u, cw, cb, name):
    b, s, dc = u.shape
    ts = min(512, s)
    nt = s // ts

    def body(dxs_ref, dbm_ref, dcm_ref, u_ref, up_ref, w_ref, b_ref, du_ref, dwb_ref, head, dvs):
        @pl.when(_first_step())
        def _():
            dwb_ref[...] = jnp.zeros_like(dwb_ref)

        @pl.when(pl.program_id(1) == 0)
        def _():
            dvs[ts:ts + 8, :] = jnp.zeros((8, dc), F32)
        _conv_head(head, u_ref, up_ref, nt - 1 - pl.program_id(1))
        ws = [w_ref[k:k + 1, :] for k in range(4)]
        bias = b_ref[...]
        for r0 in range(0, ts, CONV_ROWS):
            taps = _conv_windows(u_ref, head, r0)
            v = bias + taps[0] * ws[0] + taps[1] * ws[1] + taps[2] * ws[2] + taps[3] * ws[3]
            sg = _sigmoid(v)
            rs = slice(r0, r0 + CONV_ROWS)
            dy = jnp.concatenate([dxs_ref[rs, :], dbm_ref[rs, :], dcm_ref[rs, :]], axis=1)
            dv = dy * (sg * (1.0 + v * (1.0 - sg)))
            dvs[rs, :] = dv
            for k in range(4):
                dwb_ref[8 * k:8 * k + 8, :] += _fold8(dv * taps[k])
            dwb_ref[32:40, :] += _fold8(dv)
        for r0 in range(0, ts, CONV_ROWS):
            win = [dvs[r0 + 3 - k:r0 + 3 - k + CONV_ROWS, :] for k in range(4)]
            acc = win[0] * ws[0] + win[1] * ws[1] + win[2] * ws[2] + win[3] * ws[3]
            du_ref[r0:r0 + CONV_ROWS, :] = acc.astype(BF16)
        dvs[ts:ts + 8, :] = dvs[0:8, :]

    rows = lambda wd: pl.BlockSpec((None, ts, wd), lambda bb, i: (bb, nt - 1 - i, 0))
    prev = pl.BlockSpec((None, 8, dc), lambda bb, i: (bb, jnp.maximum((nt - 1 - i) * (ts // 8) - 1, 0), 0))
    return pl.pallas_call(
        body, name=name, grid=(b, nt),
        in_specs=[rows(D_SSD), rows(256), rows(256), rows(dc), prev, _full((4, dc)), _full((1, dc))],
        out_specs=[rows(dc), _full((40, dc))],
        out_shape=[SDS((b, s, dc), BF16), SDS((40, dc), F32)],
        scratch_shapes=[pltpu.VMEM((8 + CONV_ROWS, dc), F32), pltpu.VMEM((ts + 8, dc), F32)],
        compiler_params=_cparams(2))(dxs, dbm, dcm, u, u, cw, cb)


def conv_grads_fold(x, name):
    c = x.shape[1]

    def body(x_ref, o_ref):
        o_ref[...] = jnp.zeros_like(o_ref)
        for k in range(5):
            o_ref[k:k + 1, :] = jnp.sum(x_ref[8 * k:8 * k + 8, :], axis=0, keepdims=True)

    return pl.pallas_call(body, name=name, out_shape=SDS((8, c), F32))(x)


def _ssd_common(misc_ref, dtb_ref, alog_ref, e_ref):
    ln = CHUNK
    lane = lax.broadcasted_iota(I32, (ln, LANES), 1)
    lane1 = lax.broadcasted_iota(I32, (1, LANES), 1)
    pre = misc_ref[...] + dtb_ref[...]
    dt_s = jnp.where(lane < SSD_HEADS, _softplus(pre), 0.0)
    a_neg = jnp.where(lane1 < SSD_HEADS, -jnp.exp(alog_ref[...]), 0.0)
    ri = lax.broadcasted_iota(I32, (ln, ln), 0)
    ci = lax.broadcasted_iota(I32, (ln, ln), 1)
    tril = ci <= ri
    acum = jnp.dot(tril.astype(F32), dt_s * a_neg, preferred_element_type=F32, precision=HI)
    both_e = _dot_01(jnp.concatenate([dt_s, acum], axis=0), e_ref[...], 3)
    dt_e, acum_e = both_e[0:ln], both_e[ln:2 * ln]
    return dict(pre=pre, dt_s=dt_s, a_neg=a_neg, tril=tril, ri=ri, ci=ci, acum=acum, acum_t=acum.T,
                dt_e=dt_e, eac_e=jnp.exp(acum_e), del_e=jnp.exp(acum_e[ln - 1:ln, :] - acum_e))


def _dot_01(x, m01, terms, dims=(((1,), (0,)), ((), ()))):
    acc, rest = None, x
    for k in range(terms):
        part = rest.astype(BF16)
        if k + 1 < terms:
            rest = rest - part.astype(F32)
        d = lax.dot_general(part, m01, dims, preferred_element_type=F32)
        acc = d if acc is None else acc + d
    return acc


def _decay(cm, h):
    seg = cm["acum"][:, h:h + 1] - cm["acum_t"][h:h + 1, :]
    return jnp.exp(jnp.where(cm["tril"], seg, -jnp.inf))


def ssd_fwd(xs, bm, cm_, misc, z, dtb, alog, dskip_e, norm_w, e_mat, name):
    b, s, _ = xs.shape
    ln, nc = CHUNK, s // CHUNK
    gw = D_SSD // SSD_GROUPS
    hpg = SSD_HEADS // SSD_GROUPS

    def body(xs_ref, b_ref, c_ref, misc_ref, z_ref, dtb_ref, alog_ref, dsk_ref, nw_ref, e_ref,
             ys_ref, y_ref, p_ref, st, yd):
        @pl.when(pl.program_id(1) == 0)
        def _():
            st[...] = jnp.zeros_like(st)
        cm = _ssd_common(misc_ref, dtb_ref, alog_ref, e_ref)
        xsv = xs_ref[...]
        xdt = xsv * cm["dt_e"]
        xdt_b = xdt.astype(BF16)
        xd_b = (xdt * cm["del_e"]).astype(BF16)
        gam_e = cm["eac_e"][ln - 1:ln, :]
        p_ref[...] = st[...]
        groups = [slice(gw * g, gw * (g + 1)) for g in range(SSD_GROUPS)]
        heads = [slice(SSD_HEAD_DIM * h, SSD_HEAD_DIM * (h + 1)) for h in range(SSD_HEADS)]
        bgs = [b_ref[:, SSD_STATE * g:SSD_STATE * (g + 1)].astype(BF16) for g in range(SSD_GROUPS)]
        cgs = [c_ref[:, SSD_STATE * g:SSD_STATE * (g + 1)].astype(BF16) for g in range(SSD_GROUPS)]
        cbs = [lax.dot_general(cg, bg, NT_DIMS, preferred_element_type=F32) for cg, bg in zip(cgs, bgs)]
        sts = [st[:, gs] for gs in groups]
        yoff = [jnp.dot(cg, st_g.astype(BF16), preferred_element_type=F32) * cm["eac_e"][:, gs]
                for cg, st_g, gs in zip(cgs, sts, groups)]
        news = [lax.dot_general(bg, xd_b[:, gs], TN_DIMS, preferred_element_type=F32) for bg, gs in zip(bgs, groups)]
        for gs, st_g, new in zip(groups, sts, news):
            st[:, gs] = st_g * gam_e[:, gs] + new
        ms = [(cbs[h // hpg] * _decay(cm, h)).astype(BF16) for h in range(SSD_HEADS)]
        for h, hs in enumerate(heads):
            yd[:, hs] = jnp.dot(ms[h], xdt_b[:, hs], preferred_element_type=F32)
        y = yd[...] + jnp.concatenate(yoff, axis=1) + dsk_ref[...] * xsv
        y_ref[...] = y
        zz = z_ref[...]
        yg = y * (zz * _sigmoid(zz))
        outs = []
        for g in range(SSD_GROUPS):
            ygg = yg[:, gw * g:gw * (g + 1)]
            outs.append(ygg * _rms(ygg) * nw_ref[:, gw * g:gw * (g + 1)])
        ys_ref[...] = jnp.concatenate(outs, axis=1).astype(BF16)

    row = lambda d: pl.BlockSpec((None, ln, d), lambda bb, c: (bb, c, 0))
    return pl.pallas_call(
        body, name=name, grid=(b, nc),
        in_specs=[row(D_SSD), row(256), row(256), row(LANES), row(D_SSD), _full((1, LANES)), _full((1, LANES)),
                  _full((1, D_SSD)), _full((1, D_SSD)), _full((LANES, D_SSD))],
        out_specs=[row(D_SSD), row(D_SSD), pl.BlockSpec((None, None, SSD_STATE, D_SSD), lambda bb, c: (bb, c, 0, 0))],
        out_shape=[SDS((b, s, D_SSD), BF16), SDS((b, s, D_SSD), F32), SDS((b, nc, SSD_STATE, D_SSD), F32)],
        scratch_shapes=[pltpu.VMEM((SSD_STATE, D_SSD), F32), pltpu.VMEM((ln, D_SSD), F32)],
        compiler_params=_cparams(2))(xs, bm, cm_, misc, z, dtb, alog, dskip_e, norm_w, e_mat)


def ssd_bwd(dys, y, z, xs, bm, cm_, misc, prev, dtb, alog, dskip_e, norm_w, e_mat, et_mat, name):
    b, s, _ = xs.shape
    ln, nc = CHUNK, s // CHUNK
    gw = D_SSD // SSD_GROUPS
    hpg = SSD_HEADS // SSD_GROUPS

    def body(dys_ref, y_ref, z_ref, xs_ref, b_ref, c_ref, misc_ref, p_ref, dtb_ref, alog_ref, dsk_ref, nw_ref,
             e_ref, et_ref, dxs_ref, db_ref, dc_ref, dz_ref, ddt_ref, dnw_ref, ddsk_ref, ddtb_ref, dalog_ref,
             dst, dxd, dac_t):
        @pl.when(_first_step())
        def _():
            for r_ in (dnw_ref, ddsk_ref, ddtb_ref, dalog_ref):
                r_[...] = jnp.zeros_like(r_)

        @pl.when(pl.program_id(1) == 0)
        def _():
            dst[...] = jnp.zeros_like(dst)

        cm = _ssd_common(misc_ref, dtb_ref, alog_ref, e_ref)
        et = et_ref[...]
        squeeze = lambda t: _dot_01(t, et, 2)
        lane = lax.broadcasted_iota(I32, (ln, LANES), 1)
        sub = lax.broadcasted_iota(I32, (LANES, ln), 0)
        xsv = xs_ref[...]
        xdt = xsv * cm["dt_e"]
        xdt_b = xdt.astype(BF16)
        xd_b = (xdt * cm["del_e"]).astype(BF16)
        eac_e = cm["eac_e"]
        gam_e = eac_e[ln - 1:ln, :]

        yv, zz, dyo = y_ref[...], z_ref[...], dys_ref[...]
        sz = _sigmoid(zz)
        silu_z = zz * sz
        yg = yv * silu_z
        dyg, dnw = [], []
        for g in range(SSD_GROUPS):
            gs = slice(gw * g, gw * (g + 1))
            ygg = yg[:, gs]
            r = _rms(ygg)
            n = ygg * r
            dnw.append(jnp.sum(dyo[:, gs] * n, axis=0, keepdims=True))
            dyg.append(_rms_bwd(dyo[:, gs] * nw_ref[:, gs], n, r))
        dyg = jnp.concatenate(dyg, axis=1)
        dnw_ref[...] += jnp.concatenate(dnw, axis=1)
        dz_ref[...] = (dyg * yv * (sz * (1.0 + zz * (1.0 - sz)))).astype(BF16)
        dy = dyg * silu_z
        ddsk_ref[...] += jnp.sum(dy * xsv, axis=0, keepdims=True)
        dy_b = dy.astype(BF16)

        dacum = jnp.zeros((ln, LANES), F32)
        dac_t[...] = jnp.zeros_like(dac_t)
        w1, dgam = [], []
        for g in range(SSD_GROUPS):
            gs = slice(gw * g, gw * (g + 1))
            ss = slice(SSD_STATE * g, SSD_STATE * (g + 1))
            bg = b_ref[:, ss].astype(BF16)
            cg = c_ref[:, ss].astype(BF16)
            cb = lax.dot_general(cg, bg, NT_DIMS, preferred_element_type=F32)
            pt = p_ref[:, gs]
            pt_b = pt.astype(BF16)
            dst_g = dst[:, gs]
            dst_b = dst_g.astype(BF16)
            edy = (dy[:, gs] * eac_e[:, gs]).astype(BF16)
            dcg = lax.dot_general(edy, pt_b, NT_DIMS, preferred_element_type=F32)
            dpt = lax.dot_general(cg, edy, TN_DIMS, preferred_element_type=F32)
            yoff = jnp.dot(cg, pt_b, preferred_element_type=F32) * eac_e[:, gs]
            dxd_g = jnp.dot(bg, dst_b, preferred_element_type=F32)
            dbg = lax.dot_general(xd_b[:, gs], dst_b, NT_DIMS, preferred_element_type=F32)
            ddel = dxd_g * xdt[:, gs] * cm["del_e"][:, gs]
            w1.append(dy[:, gs] * yoff - ddel)
            dgam.append(jnp.sum(ddel, axis=0, keepdims=True) + jnp.sum(dst_g * pt, axis=0, keepdims=True) * gam_e[:, gs])
            dxd[:, gs] = dxd_g * cm["del_e"][:, gs]
            dst[:, gs] = dst_g * gam_e[:, gs] + dpt
            dcb = jnp.zeros((ln, ln), F32)
            for j in range(hpg):
                h = hpg * g + j
                hs = slice(SSD_HEAD_DIM * h, SSD_HEAD_DIM * (h + 1))
                lam = _decay(cm, h)
                m = cb * lam
                dm = lax.dot_general(dy_b[:, hs], xdt_b[:, hs], NT_DIMS, preferred_element_type=F32)
                dxd[:, hs] += lax.dot_general(m.astype(BF16), dy_b[:, hs], TN_DIMS, preferred_element_type=F32)
                dcb += dm * lam
                wl = dm * m
                dacum += jnp.where(lane == h, jnp.sum(wl, axis=1, keepdims=True), 0.0)
                dac_t[...] -= jnp.where(sub == h, jnp.sum(wl, axis=0, keepdims=True), 0.0)
            dcb_b = dcb.astype(BF16)
            dc_ref[:, ss] = dcg + jnp.dot(dcb_b, bg, preferred_element_type=F32)
            db_ref[:, ss] = dbg + lax.dot_general(dcb_b, cg, TN_DIMS, preferred_element_type=F32)

        dxdt = dxd[...]
        dxs_ref[...] = dy * dsk_ref[...] + dxdt * cm["dt_e"]
        dacum += squeeze(jnp.concatenate(w1, axis=1)) + dac_t[...].T
        dlast = squeeze(jnp.broadcast_to(jnp.concatenate(dgam, axis=1), (8, D_SSD)))[0:1, :]
        dacum += jnp.where(lax.broadcasted_iota(I32, (ln, LANES), 0) == ln - 1, dlast, 0.0)
        triu = (cm["ci"] >= cm["ri"]).astype(F32)
        da = jnp.dot(triu, dacum, preferred_element_type=F32, precision=HI)
        ddt = da * cm["a_neg"] + squeeze(dxdt * xsv)
        dalog_ref[...] += jnp.sum(da * cm["dt_s"], axis=0, keepdims=True) * cm["a_neg"]
        ddt_raw = jnp.where(lane < SSD_HEADS, ddt * _sigmoid(cm["pre"]), 0.0)
        ddt_ref[...] = ddt_raw
        ddtb_ref[...] += jnp.sum(ddt_raw, axis=0, keepdims=True)

    row = lambda d: pl.BlockSpec((None, ln, d), lambda bb, c: (bb, nc - 1 - c, 0))
    return pl.pallas_call(
        body, name=name, grid=(b, nc),
        in_specs=[row(D_SSD), row(D_SSD), row(D_SSD), row(D_SSD), row(256), row(256), row(LANES),
                  pl.BlockSpec((None, None, SSD_STATE, D_SSD), lambda bb, c: (bb, nc - 1 - c, 0, 0)),
                  _full((1, LANES)), _full((1, LANES)), _full((1, D_SSD)), _full((1, D_SSD)),
                  _full((LANES, D_SSD)), _full((D_SSD, LANES))],
        out_specs=[row(D_SSD), row(256), row(256), row(D_SSD), row(LANES),
                   _full((1, D_SSD)), _full((1, D_SSD)), _full((1, LANES)), _full((1, LANES))],
        out_shape=[SDS((b, s, D_SSD), F32), SDS((b, s, 256), F32), SDS((b, s, 256), F32), SDS((b, s, D_SSD), BF16),
                   SDS((b, s, LANES), F32), SDS((1, D_SSD), F32), SDS((1, D_SSD), F32), SDS((1, LANES), F32),
                   SDS((1, LANES), F32)],
        scratch_shapes=[pltpu.VMEM((SSD_STATE, D_SSD), F32), pltpu.VMEM((ln, D_SSD), F32), pltpu.VMEM((LANES, ln), F32)],
        compiler_params=_cparams(2))(dys, y, z, xs, bm, cm_, misc, prev, dtb, alog, dskip_e, norm_w, e_mat, et_mat)


def _rope(xv, cc, sp, sm):
    n = xv.shape[1]
    return xv * cc + pltpu.roll(xv, 16, 1) * sp + pltpu.roll(xv, n - 16, 1) * sm


def _rope_bwd(dy, cc, sp, sm):
    n = dy.shape[1]
    return dy * cc + pltpu.roll(dy * sp, n - 16, 1) + pltpu.roll(dy * sm, 16, 1)


def _tile8(t):
    return jnp.concatenate([t] * MLA_HEADS, axis=1)


def qkv_fwd(cq, ckv, misc, cc, sp, sm, qnw, kvnw, wuq_t, wukv_t, place, name):
    b, s, _ = cq.shape
    tm = _att_tile(s)
    hd = MLA_HEADS * HEAD_PAD

    def body(cq_ref, ckv_ref, misc_ref, cc_ref, sp_ref, sm_ref, qnw_ref, kvnw_ref, wq_ref, wkv_ref, pl_ref,
             q_ref, k_ref, v_ref, vt_ref, qn_ref, kvn_ref):
        cqv, ckvv = cq_ref[...], ckv_ref[...]
        qn = (cqv * _rms(cqv) * qnw_ref[...]).astype(BF16)
        kvn = (ckvv * _rms(ckvv) * kvnw_ref[...]).astype(BF16)
        qn_ref[...] = qn
        kvn_ref[...] = kvn
        cc1, sp1, sm1 = cc_ref[...], sp_ref[...], sm_ref[...]
        q = lax.dot_general(qn, wq_ref[...], NT_DIMS, preferred_element_type=F32)
        q_ref[...] = _rope(q, _tile8(cc1), _tile8(sp1), _tile8(sm1)).astype(BF16)
        kv = lax.dot_general(kvn, wkv_ref[...], NT_DIMS, preferred_element_type=F32)
        kr = jnp.dot(misc_ref[...], pl_ref[...], preferred_element_type=F32, precision=HI)
        kr = _rope(kr, cc1, sp1, sm1)
        k_ref[...] = (kv[:, 0:hd] + _tile8(kr)).astype(BF16)
        v_ref[...] = kv[:, hd:2 * hd].astype(BF16)
        for h in range(MLA_HEADS):
            vt_ref[h] = kv[:, hd + HEAD_PAD * h:hd + HEAD_PAD * (h + 1)].T.astype(BF16)

    return pl.pallas_call(
        body, name=name, grid=(b, s // tm),
        in_specs=[_row(tm, Q_LORA), _row(tm, KV_LORA), _row(tm, LANES), _row(tm, LANES), _row(tm, LANES), _row(tm, LANES),
                  _full((1, Q_LORA)), _full((1, KV_LORA)), _full(wuq_t.shape), _full(wukv_t.shape), _full((LANES, LANES))],
        out_specs=[_row(tm, hd), _row(tm, hd), _row(tm, hd),
                   pl.BlockSpec((None, MLA_HEADS, None, HEAD_PAD, tm), lambda bb, i: (bb, 0, i, 0, 0)),
                   _row(tm, Q_LORA), _row(tm, KV_LORA)],
        out_shape=[SDS((b, s, hd), BF16)] * 3 + [SDS((b, MLA_HEADS, s // tm, HEAD_PAD, tm), BF16),
                                                 SDS((b, s, Q_LORA), BF16), SDS((b, s, KV_LORA), BF16)],
        compiler_params=_cparams(2))(cq, ckv, misc, cc, sp, sm, qnw, kvnw, wuq_t, wukv_t, place)


def qkv_bwd(dq, dk, dv, ddt, cq, ckv, cc, sp, sm, qnw, kvnw, wuq_t, wukv_t, place_t, name):
    b, s, _ = cq.shape
    tm = min(512, s)
    hd = MLA_HEADS * HEAD_PAD

    def body(dq_ref, dk_ref, dv_ref, ddt_ref, cq_ref, ckv_ref, cc_ref, sp_ref, sm_ref, qnw_ref, kvnw_ref,
             wq_ref, wkv_ref, plt_ref, dcq_ref, dckv_ref, dmisc_ref, dqp_ref, dkv_ref, dqnw_ref, dkvnw_ref):
        @pl.when(_first_step())
        def _():
            dqnw_ref[...] = jnp.zeros_like(dqnw_ref)
            dkvnw_ref[...] = jnp.zeros_like(dkvnw_ref)
        cc1, sp1, sm1 = cc_ref[...], sp_ref[...], sm_ref[...]
        dqp = _rope_bwd(dq_ref[...].astype(F32), _tile8(cc1), _tile8(sp1), _tile8(sm1)).astype(BF16)
        dqp_ref[...] = dqp
        dkv_b = jnp.concatenate([dk_ref[...], dv_ref[...]], axis=1)
        dkf = dk_ref[...].astype(F32)
        dkv_ref[...] = dkv_b
        dkr = dkf[:, 0:HEAD_PAD]
        for h in range(1, MLA_HEADS):
            dkr += dkf[:, HEAD_PAD * h:HEAD_PAD * (h + 1)]
        dkr = _rope_bwd(dkr, cc1, sp1, sm1)
        dmisc_ref[...] = (jnp.dot(dkr, plt_ref[...], preferred_element_type=F32, precision=HI) + ddt_ref[...]).astype(BF16)

        def norm_bwd(dn_w, xv, w_ref, dw_ref, dx_ref):
            r = _rms(xv)
            n = xv * r
            dw_ref[...] += jnp.sum(dn_w * n, axis=0, keepdims=True)
            dx_ref[...] = _rms_bwd(dn_w * w_ref[...], n, r).astype(BF16)

        norm_bwd(jnp.dot(dqp, wq_ref[...], preferred_element_type=F32), cq_ref[...], qnw_ref, dqnw_ref, dcq_ref)
        norm_bwd(jnp.dot(dkv_b, wkv_ref[...], preferred_element_type=F32), ckv_ref[...], kvnw_ref, dkvnw_ref, dckv_ref)

    return pl.pallas_call(
        body, name=name, grid=(b, s // tm),
        in_specs=[_row(tm, hd), _row(tm, hd), _row(tm, hd), _row(tm, LANES), _row(tm, Q_LORA), _row(tm, KV_LORA),
                  _row(tm, LANES), _row(tm, LANES), _row(tm, LANES), _full((1, Q_LORA)), _full((1, KV_LORA)),
                  _full(wuq_t.shape), _full(wukv_t.shape), _full((LANES, LANES))],
        out_specs=[_row(tm, Q_LORA), _row(tm, KV_LORA), _row(tm, LANES), _row(tm, hd), _row(tm, 2 * hd),
                   _full((1, Q_LORA)), _full((1, KV_LORA))],
        out_shape=[SDS((b, s, Q_LORA), BF16), SDS((b, s, KV_LORA), BF16), SDS((b, s, LANES), BF16),
                   SDS((b, s, hd), BF16), SDS((b, s, 2 * hd), BF16), SDS((1, Q_LORA), F32), SDS((1, KV_LORA), F32)],
        compiler_params=_cparams(2))(dq, dk, dv, ddt, cq, ckv, cc, sp, sm, qnw, kvnw, wuq_t, wukv_t, place_t)


ATT_SCALE = 1.0 / math.sqrt(QK_DIM)
LOG2E = math.log2(math.e)
ATT_SCALE_LOG2E = ATT_SCALE * LOG2E


ATT_HEADS_PER_STEP = 4
ATT_HEADS_PER_STEP_BWD = 2


def _att_tile(s):
    return min(512, s)


def flash_fwd(q, k, vt, name):
    b, s, hd = q.shape
    t = _att_tile(s)
    nb = s // t
    th = t // 2

    hps = ATT_HEADS_PER_STEP
    hw = hps * HEAD_PAD

    def body(q_ref, k_ref, vt_ref, o_ref, lse_ref, m_s, l_s, acc):
        i = pl.program_id(2)
        m_s[...] = jnp.full_like(m_s, -jnp.inf)
        l_s[...] = jnp.zeros_like(l_s)
        acc[...] = jnp.zeros_like(acc)

        def update(j, diagonal):
            chains = [(hh, half) for hh in range(hps) for half in range(2)]
            lanes = lambda hh: slice(HEAD_PAD * hh, HEAD_PAD * (hh + 1))
            cols = lambda half: slice(th * half, th * (half + 1))
            sts = {}
            nkeys = lambda half: th if diagonal and half == 0 else t
            for hh, half in chains:
                kr = pl.ds(pl.multiple_of(j * t, t), nkeys(half))
                st = lax.dot_general(k_ref[kr, lanes(hh)], q_ref[cols(half), lanes(hh)], NT_DIMS,
                                     preferred_element_type=F32)
                if diagonal:
                    row = lax.broadcasted_iota(I32, (nkeys(half), th), 0)
                    col = lax.broadcasted_iota(I32, (nkeys(half), th), 1) + th * half
                    st = jnp.where(row <= col, st, -jnp.inf)
                sts[hh, half] = st
            pts, alphas = {}, {}
            for hh, half in chains:
                st, cs = sts[hh, half], cols(half)
                m_prev = m_s[hh, :, cs]
                m_new = jnp.maximum(m_prev, jnp.max(st, axis=0, keepdims=True))
                alpha = jnp.exp2((m_prev - m_new) * ATT_SCALE_LOG2E)
                pt = jnp.exp2((st - m_new) * ATT_SCALE_LOG2E)
                l_s[hh, :, cs] = alpha * l_s[hh, :, cs] + jnp.sum(pt, axis=0, keepdims=True)
                m_s[hh, :, cs] = m_new
                pts[hh, half], alphas[hh, half] = pt.astype(BF16), alpha
            for hh, half in chains:
                cs = cols(half)
                acc[hh, :, cs] = alphas[hh, half] * acc[hh, :, cs] + jnp.dot(
                    vt_ref[hh, j, :, 0:nkeys(half)], pts[hh, half], preferred_element_type=F32)

        def step(j, carry):
            update(j, False)
            return carry

        lax.fori_loop(0, i, step, 0)
        update(i, True)
        for hh in range(hps):
            o_ref[:, HEAD_PAD * hh:HEAD_PAD * (hh + 1)] = (acc[hh] / l_s[hh]).T
            lse_ref[hh] = m_s[hh] * ATT_SCALE + jnp.log(l_s[hh])

    qs = pl.BlockSpec((None, t, hw), lambda bb, h, i: (bb, i, h))
    ks = pl.BlockSpec((None, s, hw), lambda bb, h, i: (bb, 0, h))
    vs = pl.BlockSpec((None, hps, nb, HEAD_PAD, t), lambda bb, h, i: (bb, h, 0, 0, 0))
    ls = pl.BlockSpec((None, hps, None, 1, t), lambda bb, h, i: (bb, h, i, 0, 0))
    return pl.pallas_call(
        body, name=name, grid=(b, MLA_HEADS // hps, nb),
        in_specs=[qs, ks, vs], out_specs=[qs, ls],
        out_shape=[SDS((b, s, hd), F32), SDS((b, MLA_HEADS, nb, 1, t), F32)],
        scratch_shapes=[pltpu.VMEM((hps, 1, t), F32), pltpu.VMEM((hps, 1, t), F32), pltpu.VMEM((hps, HEAD_PAD, t), F32)],
        compiler_params=_cparams(3))(q, k, vt)


def flash_bwd(q, k, v, do, lse, dlt, name):
    b, s, hd = q.shape
    t = _att_tile(s)
    nb = s // t
    th = t // 2
    lse_r = lse
    dlt_r = dlt.reshape(b, MLA_HEADS, nb, 1, t)

    hps = ATT_HEADS_PER_STEP_BWD
    hw = hps * HEAD_PAD

    def body(q_ref, k_ref, v_ref, do_ref, lse_ref, dlt_ref, dq_ref, dk_ref, dv_ref, dq_s, dk_s, dv_s):
        dq_s[...] = jnp.zeros_like(dq_s)
        dk_s[...] = jnp.zeros_like(dk_s)
        dv_s[...] = jnp.zeros_like(dv_s)

        def tile(j, i, diagonal):
            chains = [(hh, half) for hh in range(hps) for half in range(2)]
            lanes = lambda hh: slice(HEAD_PAD * hh, HEAD_PAD * (hh + 1))
            keys = lambda half: pl.ds(pl.multiple_of(j * t + th * half, th), th)
            q0 = lambda half: th if diagonal and half == 1 else 0
            qsel = lambda half: pl.ds(pl.multiple_of(i * t + q0(half), th), t - q0(half))
            sts, dpts = {}, {}
            for hh, half in chains:
                ls_, ks, qs, nq = lanes(hh), keys(half), qsel(half), t - q0(half)
                st = lax.dot_general(k_ref[ks, ls_], q_ref[qs, ls_], NT_DIMS, preferred_element_type=F32)
                if diagonal:
                    row = lax.broadcasted_iota(I32, (th, nq), 0) + th * half
                    col = lax.broadcasted_iota(I32, (th, nq), 1) + q0(half)
                    st = jnp.where(row <= col, st, -jnp.inf)
                sts[hh, half] = st
                dpts[hh, half] = lax.dot_general(v_ref[ks, ls_], do_ref[qs, ls_], NT_DIMS, preferred_element_type=F32)
            pts, dsts = {}, {}
            for hh, half in chains:
                qcols = slice(q0(half), t)
                pt = jnp.exp2(sts[hh, half] * ATT_SCALE_LOG2E - lse_ref[hh, i][:, qcols] * LOG2E)
                pts[hh, half] = pt.astype(BF16)
                dsts[hh, half] = (pt * (dpts[hh, half] - dlt_ref[hh, i][:, qcols])).astype(BF16)
            for hh, half in chains:
                ls_, ks, qs = lanes(hh), keys(half), qsel(half)
                dv_s[ks, ls_] += jnp.dot(pts[hh, half], do_ref[qs, ls_], preferred_element_type=F32)
                dk_s[ks, ls_] += jnp.dot(dsts[hh, half], q_ref[qs, ls_], preferred_element_type=F32)
                dq_s[qs, ls_] += lax.dot_general(dsts[hh, half], k_ref[ks, ls_], TN_DIMS, preferred_element_type=F32)

        def key_tile(j, carry):
            tile(j, j, True)

            def query_tile(i, c2):
                tile(j, i, False)
                return c2

            lax.fori_loop(j + 1, nb, query_tile, 0)
            return carry

        lax.fori_loop(0, nb, key_tile, 0)
        dq_ref[...] = (dq_s[...] * ATT_SCALE).astype(BF16)
        dk_ref[...] = (dk_s[...] * ATT_SCALE).astype(BF16)
        dv_ref[...] = dv_s[...].astype(BF16)

    hs = pl.BlockSpec((None, s, hw), lambda bb, h: (bb, 0, h))
    ls = pl.BlockSpec((None, hps, nb, 1, t), lambda bb, h: (bb, h, 0, 0, 0))
    return pl.pallas_call(
        body, name=name, grid=(b, MLA_HEADS // hps),
        in_specs=[hs, hs, hs, hs, ls, ls], out_specs=[hs, hs, hs],
        out_shape=[SDS((b, s, hd), BF16)] * 3, scratch_shapes=[pltpu.VMEM((s, hw), F32)] * 3,
        compiler_params=_cparams(2))(q, k, v, do, lse_r, dlt_r)


def out_proj(ys, attn, mnw, wo, x, gate, above, name):
    b, s, d = x.shape
    tm = min(512, s)

    def body(ys_ref, at_ref, mnw_ref, wo_ref, x_ref, g_ref, nw_ref, sc_ref, sh_ref, xn_ref, o_ref, ym_ref, h_ref):
        av = at_ref[...]
        ym = (av * _rms(av) * mnw_ref[...]).astype(BF16)
        ym_ref[...] = ym
        o = jnp.dot(ys_ref[...], wo_ref[0:D_SSD, :], preferred_element_type=F32)
        o += jnp.dot(ym, wo_ref[D_SSD:2 * D_SSD, :], preferred_element_type=F32)
        xn = x_ref[...] + g_ref[...] * o
        xn_ref[...] = xn
        o_ref[...] = o.astype(BF16)
        h_ref[...] = _norm_mod_tile(xn, nw_ref, sc_ref, sh_ref)

    return pl.pallas_call(
        body, name=name, grid=(b, s // tm),
        in_specs=[_row(tm, D_SSD), _row(tm, D_SSD), _full((1, D_SSD)), _full(wo.shape), _row(tm, d), _bvec(d),
                  _full((1, d)), _bvec(d), _bvec(d)],
        out_specs=[_row(tm, d), _row(tm, d), _row(tm, D_SSD), _row(tm, d)],
        out_shape=[SDS((b, s, d), F32), SDS((b, s, d), BF16), SDS((b, s, D_SSD), BF16), SDS((b, s, d), BF16)],
        compiler_params=_cparams(2))(ys, attn, mnw, wo, x, gate, *above)


def out_proj_bwd(dout, attn, mnw, wo, name):
    b, s, d = dout.shape
    tm = min(512, s)

    def body(do_ref, at_ref, mnw_ref, wo_ref, dys_ref, dat_ref, dlt_ref, dw_ref):
        lane = lax.broadcasted_iota(I32, (tm, LANES), 1)
        @pl.when(_first_step())
        def _():
            dw_ref[...] = jnp.zeros_like(dw_ref)
        dov = do_ref[...]
        dys_ref[...] = lax.dot_general(dov, wo_ref[0:D_SSD, :], NT_DIMS, preferred_element_type=F32)
        dym = lax.dot_general(dov, wo_ref[D_SSD:2 * D_SSD, :], NT_DIMS, preferred_element_type=F32)
        av = at_ref[...]
        r = _rms(av)
        n = av * r
        dw_ref[...] += jnp.sum(dym * n, axis=0, keepdims=True)
        dat = _rms_bwd(dym * mnw_ref[...], n, r)
        dat_ref[...] = dat.astype(BF16)
        prod = dat * av
        cols = jnp.zeros((tm, LANES), F32)
        for h in range(MLA_HEADS):
            cols += jnp.where(lane == h, jnp.sum(prod[:, HEAD_PAD * h:HEAD_PAD * (h + 1)], axis=1, keepdims=True), 0.0)
        dlt_ref[...] = cols.T[0:MLA_HEADS, :]

    return pl.pallas_call(
        body, name=name, grid=(b, s // tm),
        in_specs=[_row(tm, d), _row(tm, D_SSD), _full((1, D_SSD)), _full(wo.shape)],
        out_specs=[_row(tm, D_SSD), _row(tm, D_SSD),
                   pl.BlockSpec((None, MLA_HEADS, tm), lambda bb, i: (bb, 0, i)), _full((1, D_SSD))],
        out_shape=[SDS((b, s, D_SSD), F32), SDS((b, s, D_SSD), BF16), SDS((b, MLA_HEADS, s), F32),
                   SDS((1, D_SSD), F32)],
        compiler_params=_cparams(2))(dout, attn, mnw, wo)


def adaln_fwd(c_all, w_ada, b_ada, name):
    nb, d = c_all.shape
    n = w_ada.shape[1]

    def body(c_ref, w_ref, b_ref, m_ref, ca_ref):
        cv = c_ref[...]
        ca = (cv * _sigmoid(cv)).astype(BF16)
        ca_ref[...] = ca
        m_ref[...] = jnp.dot(ca, w_ref[...].astype(BF16), preferred_element_type=F32) + b_ref[...]

    return pl.pallas_call(
        body, name=name, out_shape=[SDS((nb, n), F32), SDS((nb, d), BF16)],
        compiler_params=pltpu.CompilerParams(vmem_limit_bytes=VMEM_LIMIT))(c_all, w_ada, b_ada)


def adaln_bwd(c_act, dmod_cols, name):
    d, n = c_act.shape[1], dmod_cols.shape[1]

    def body(c_ref, dm_ref, gw_ref):
        gw_ref[...] = lax.dot_general(c_ref[...], dm_ref[...].astype(BF16), TN_DIMS, preferred_element_type=F32)

    return pl.pallas_call(
        body, name=name, out_shape=SDS((d, n), F32),
        compiler_params=pltpu.CompilerParams(vmem_limit_bytes=VMEM_LIMIT))(c_act, dmod_cols)


def sum_rows(x, name):
    def body(x_ref, o_ref):
        o_ref[...] = jnp.sum(x_ref[...], axis=0, keepdims=True)
    return pl.pallas_call(body, name=name, out_shape=SDS((1, x.shape[1]), F32))(x)


def squeeze_heads(x, et_mat, name):
    def body(x_ref, et_ref, o_ref):
        xv = jnp.broadcast_to(x_ref[...], (8, x.shape[1]))
        o_ref[...] = _dot_01(xv, et_ref[...], 3)[0:1, :]
    return pl.pallas_call(body, name=name, out_shape=SDS((1, LANES), F32))(x, et_mat)


def sum_blocks(x, name):
    n, r, c = x.shape
    tr = next(cand for cand in (256, 128, 64, 32, 16, 8) if r % cand == 0)

    def body(x_ref, o_ref):
        acc = x_ref[0].astype(F32)
        for k in range(1, n):
            acc += x_ref[k].astype(F32)
        o_ref[...] = acc

    return pl.pallas_call(
        body, name=name, grid=(r // tr,), in_specs=[pl.BlockSpec((n, tr, c), lambda i: (0, i, 0))],
        out_specs=pl.BlockSpec((tr, c), lambda i: (i, 0)), out_shape=SDS((r, c), F32),
        compiler_params=_cparams(1))(x)


def _adam_math(w, g, m, v):
    m = ADAM_B1 * m + (1.0 - ADAM_B1) * g
    v = ADAM_B2 * v + (1.0 - ADAM_B2) * (g * g)
    m_hat = m / (1.0 - ADAM_B1 ** ADAM_STEP)
    v_hat = v / (1.0 - ADAM_B2 ** ADAM_STEP)
    return -ADAM_LR * (m_hat / (jnp.sqrt(v_hat) + ADAM_EPS) + ADAM_WD * w), m, v


def adamw(w, g, m, v, name):
    r, c = w.shape[-2:]
    tr = r
    for cand in (512, 256, 128, 64, 32, 16, 8):
        if r % cand == 0 and cand * c * 4 <= 2 * 1024 * 1024:
            tr = cand
            break

    def body(w_ref, g_ref, m_ref, v_ref, d_ref, mo_ref, vo_ref):
        d_ref[...], mo_ref[...], vo_ref[...] = _adam_math(w_ref[...], g_ref[...], m_ref[...], v_ref[...])

    def spec(a):
        return pl.BlockSpec((tr, c), lambda i: (i, 0)) if a.ndim == 2 else pl.BlockSpec((None, tr, c), lambda i: (0, i, 0))

    return pl.pallas_call(
        body, name=name, grid=(r // tr,), in_specs=[spec(w), spec(g), spec(m), spec(v)], out_specs=[spec(w)] * 3,
        out_shape=[SDS(w.shape, F32)] * 3, compiler_params=_cparams(1))(w, g, m, v)


def adamw_blocks(w, blocks, m, v, name):
    r, c = w.shape
    tr = next((cand for cand in range(r // 32 * 16, 0, -16) if r % cand == 0), r)

    def body(w_ref, b_ref, m_ref, v_ref, g_ref, d_ref, mo_ref, vo_ref):
        g = b_ref[0].astype(F32)
        for k in range(1, N_DEV):
            g += b_ref[k].astype(F32)
        g_ref[...] = g
        d_ref[...], mo_ref[...], vo_ref[...] = _adam_math(w_ref[...], g, m_ref[...], v_ref[...])

    spec = pl.BlockSpec((tr, c), lambda i: (i, 0))
    return pl.pallas_call(
        body, name=name, grid=(r // tr,),
        in_specs=[spec, pl.BlockSpec((N_DEV, tr, c), lambda i: (0, i, 0)), spec, spec], out_specs=[spec] * 4,
        out_shape=[SDS((r, c), F32)] * 4, compiler_params=_cparams(1))(w, blocks, m, v)


def adamw_many(ws, gs, ms, vs, name):
    n = len(ws)

    def body(*refs):
        w_r, g_r, m_r, v_r = (refs[k * n:(k + 1) * n] for k in range(4))
        d_r, mo_r, vo_r = (refs[(4 + k) * n:(5 + k) * n] for k in range(3))
        for k in range(n):
            d_r[k][...], mo_r[k][...], vo_r[k][...] = _adam_math(w_r[k][...], g_r[k][...], m_r[k][...], v_r[k][...])

    shapes = [SDS(w.shape, F32) for w in ws]
    outs = pl.pallas_call(body, name=name, out_shape=shapes * 3)(*ws, *gs, *ms, *vs)
    return outs[:n], outs[n:2 * n], outs[2 * n:]


TRANSPOSED = ("ffn1_w_gate", "ffn1_w_up", "ffn2_w_gate", "ffn2_w_up", "w_in", "w_ukv", "w_uq")
GATHER_GROUPS = (("ffn1_w_gate", "ffn1_w_up"), ("ffn2_w_gate", "ffn2_w_up", "ffn2_w_down"),
                 ("ffn1_w_down", "w_in", "w_ukv", "w_uq", "w_out"))
GRAD_GROUPS = (("ffn2", ("ffn2_w_gate", "ffn2_w_up", "ffn2_w_down")), ("mixer", ("w_out", "w_in", "w_ukv", "w_uq")),
               ("ffn1_down", ("ffn1_w_down",)), ("ffn1_gate", ("ffn1_w_gate",)), ("ffn1_up", ("ffn1_w_up",)))


def _shard_view(name, w):
    return w[0].T if name in TRANSPOSED else w[0]


def _shard_unview(name, t):
    return t.T[None] if name in TRANSPOSED else t[None]


def _grad_blocks(name, gw):
    if name == "w_ukv":
        hd = MLA_HEADS * HEAD_PAD
        return jnp.concatenate([gw[:hd].reshape(MLA_HEADS, HEAD_PAD, KV_LORA)[:, :QK_NOPE],
                                gw[hd:].reshape(MLA_HEADS, V_HEAD, KV_LORA)], axis=1)
    if name == "w_uq":
        return gw.reshape(MLA_HEADS, HEAD_PAD, Q_LORA)[:, :QK_DIM]
    return gw.reshape(N_DEV, -1, D_MODEL)


def _pack_rows(arrs):
    parts = []
    for a in arrs:
        flat = a.reshape(-1).astype(F32)
        pad = (-flat.shape[0]) % D_MODEL
        if pad:
            flat = jnp.pad(flat, (0, pad))
        parts.append(flat.reshape(-1, D_MODEL))
    out = jnp.concatenate(parts, axis=0)
    pad = (-out.shape[0]) % 8
    if pad:
        out = jnp.pad(out, ((0, pad), (0, 0)))
    return out


def _unpack_rows(packed, shapes):
    out, row = [], 0
    for shp in shapes:
        n = math.prod(shp)
        nrow = -(-n // D_MODEL)
        out.append(packed[row:row + nrow].reshape(-1)[:n].reshape(shp))
        row += nrow
    return out


IN_PROJ_ROWS = ((0, 0, 2560), (2560, 3200, 16), (2576, 2560, 384), (2960, 2944, 256), (3216, 3216, 32))


def _in_proj_rows(w_t):
    parts = [w_t[src:src + rows] for src, _, rows in sorted(IN_PROJ_ROWS, key=lambda seg: seg[1])]
    return jnp.concatenate(parts + [jnp.zeros((D_IN_PAD - D_IN, D_MODEL), w_t.dtype)], axis=0)


def _rope_tables(positions):
    half = QK_ROPE // 2
    inv_freq = ROPE_THETA ** (-jnp.arange(0, QK_ROPE, 2, dtype=F32) / QK_ROPE)
    ang_t = positions[:, None, :].astype(F32) * inv_freq[:, None]
    cos_t, sin_t = jnp.cos(ang_t), jnp.sin(ang_t)
    b, _, s = ang_t.shape
    ts = min(2048, s)

    def body(c_ref, s_ref, cc_ref, sp_ref, sm_ref):
        row = lax.broadcasted_iota(I32, (half, LANES), 0)
        lane = lax.broadcasted_iota(I32, (half, LANES), 1)
        first, second = lane == QK_NOPE + row, lane == QK_NOPE + half + row

        spread = lambda x, where: _dot_01(x, where.astype(BF16), 3, TN_DIMS)
        lane1 = lax.broadcasted_iota(I32, (1, LANES), 1)
        ones = jnp.where((lane1 < QK_NOPE) | (lane1 >= QK_NOPE + QK_ROPE), 1.0, 0.0)
        cc_ref[...] = spread(c_ref[...], first | second) + ones
        sp_ref[...] = spread(s_ref[...], second)
        sm_ref[...] = -spread(s_ref[...], first)

    src = pl.BlockSpec((None, half, ts), lambda bb, i: (bb, 0, i))
    return pl.pallas_call(
        body, name="rope_tables", grid=(b, s // ts), in_specs=[src, src], out_specs=[_row(ts, LANES)] * 3,
        out_shape=[SDS((b, s, LANES), F32)] * 3, compiler_params=_cparams(2))(cos_t, sin_t)


def weight_views(gathered):
    full = lambda name: gathered[name].reshape(-1, gathered[name].shape[2])
    ukv = full("w_ukv").reshape(MLA_HEADS, QK_NOPE + V_HEAD, KV_LORA)
    wukv_t = jnp.concatenate([jnp.pad(ukv[:, :QK_NOPE], ((0, 0), (0, HEAD_PAD - QK_NOPE), (0, 0))).reshape(-1, KV_LORA),
                              ukv[:, QK_NOPE:].reshape(-1, KV_LORA)], axis=0)
    uq = full("w_uq").reshape(MLA_HEADS, QK_DIM, Q_LORA)
    wuq_t = jnp.pad(uq, ((0, 0), (0, HEAD_PAD - QK_DIM), (0, 0))).reshape(-1, Q_LORA)
    return dict(wg1_t=full("ffn1_w_gate"), wu1_t=full("ffn1_w_up"), wd1=full("ffn1_w_down"),
                wg2_t=full("ffn2_w_gate"), wu2_t=full("ffn2_w_up"), wd2=full("ffn2_w_down"),
                wo=full("w_out"), win_t=_in_proj_rows(full("w_in")), wukv_t=wukv_t, wuq_t=wuq_t)


def _ffn_bwd(tag, dxn, do, dgate, x, h, gg, uu, a, sc, norm_w, wg_t, wu_t, wd, below):
    f2 = wd.shape[0] // 2
    dwd = mm_tn(a, do, f2, D_MODEL, tag + "_dwd")
    dgg, duu = ffn_dact(do, wd, gg, uu, tag + "_dact")
    dwg_t = mm_tn(dgg, h, f2, D_MODEL, tag + "_dwg")
    dwu_t = mm_tn(duu, h, f2, D_MODEL, tag + "_dwu")
    dx, dsc, dsh, dnw, *nxt = dh_norm_bwd([dgg, duu], [wg_t, wu_t], x, dxn, norm_w, sc, tag + "_dh", below)
    return dx, (dsh, dsc, dgate), dnw, (dwg_t, dwu_t, dwd), nxt


def local_step(x, tgt, positions, mod, wv, p, h1=None):
    nb, s, d = x.shape
    sh1, sc1, g1, sh2, sc2, g2, sh3, sc3, g3 = mod
    cc, sp, sm = _rope_tables(positions)
    lane_head = jnp.arange(D_SSD, dtype=I32)[None, :] // SSD_HEAD_DIM
    e_mat = (lane_head == jnp.arange(LANES, dtype=I32)[:, None]).astype(BF16)
    et_mat = e_mat.T
    rr, cl = jnp.arange(LANES, dtype=I32)[:, None], jnp.arange(LANES, dtype=I32)[None, :]
    place = ((cl == rr + (QK_NOPE - SSD_HEADS)) & (rr >= SSD_HEADS) & (rr < SSD_HEADS + QK_ROPE)).astype(F32)
    dtb = jnp.pad(p["dt_bias"], ((0, 0), (0, LANES - SSD_HEADS)))
    alog = jnp.pad(p["a_log"], ((0, 0), (0, LANES - SSD_HEADS)))
    dskip_e = jnp.repeat(p["d_skip"], SSD_HEAD_DIM, axis=1)

    if h1 is None:
        h1 = norm_mod(x, p["norm_ffn1"], sc1, sh1, "ffn1_norm")
    gg1, uu1, a1 = ffn_up(h1, wv["wg1_t"], wv["wu1_t"], "ffn1_up")
    x1, o1, h2 = ffn_down(a1, wv["wd1"], x, g1, 0.5, "ffn1_down", (p["norm_mix"], sc2, sh2))
    z, u, cq, ckv, misc = in_proj(h2, wv["win_t"], "in_proj")
    xs, bm, cm_ = conv_fwd(u, p["conv_w"], p["conv_b"], "conv_fwd")
    ys, y, prev = ssd_fwd(xs, bm, cm_, misc, z, dtb, alog, dskip_e, p["ssd_norm_w"], e_mat, "ssd_fwd")
    q, k, v, vt, qn, kvn = qkv_fwd(cq, ckv, misc, cc, sp, sm, p["q_norm_w"], p["kv_norm_w"], wv["wuq_t"], wv["wukv_t"],
                               place, "qkv_fwd")
    attn, lse = flash_fwd(q, k, vt, "flash_fwd")
    x2, o2, ym, h3 = out_proj(ys, attn, p["mla_norm_w"], wv["wo"], x1, g2, (p["norm_ffn2"], sc3, sh3), "out_proj")
    gg3, uu3, a3 = ffn_up(h3, wv["wg2_t"], wv["wu2_t"], "ffn2_up")
    loss, dx3, dnfin, do3, dg3 = ffn_down_final(a3, wv["wd2"], x2, g3, 0.5, p["norm_final"], tgt, "ffn2_down_loss")

    dx2, dmod3, dnf2, (dwg2, dwu2, dwd2), (dout, dg2) = _ffn_bwd(
        "ffn2", dx3, do3, dg3, x2, h3, gg3, uu3, a3, sc3, p["norm_ffn2"], wv["wg2_t"], wv["wu2_t"], wv["wd2"],
        (o2, g2, 1.0))
    dys, dattn, dlt, dmlan = out_proj_bwd(dout, attn, p["mla_norm_w"], wv["wo"], "out_proj_bwd")
    dwo = mm_tn_blocks([ys, ym], dout, "dwo")
    dxs, dbm, dcm, dz, ddt, dssdn, ddsk_lane, ddtb, dalog = ssd_bwd(
        dys, y, z, xs, bm, cm_, misc, prev, dtb, alog, dskip_e, p["ssd_norm_w"], e_mat, et_mat, "ssd_bwd")
    dq, dk, dv = flash_bwd(q, k, v, dattn, lse, dlt, "flash_bwd")
    dcq, dckv, dmisc, dqp, dkvc, dqn, dkvn = qkv_bwd(dq, dk, dv, ddt, cq, ckv, cc, sp, sm, p["q_norm_w"], p["kv_norm_w"],
                                                     wv["wuq_t"], wv["wukv_t"], place.T, "qkv_bwd")
    dwuq = mm_tn(dqp, qn, MLA_HEADS * HEAD_PAD, Q_LORA, "dwuq")
    dwukv = mm_tn(dkvc, kvn, MLA_HEADS * HEAD_PAD, KV_LORA, "dwukv")
    du, dconv = conv_bwd(dxs, dbm, dcm, u, p["conv_w"], p["conv_b"], "conv_bwd")
    dconv = conv_grads_fold(dconv, "conv_grads_fold")
    dproj = [dz, du, dcq, dckv, dmisc]
    dwin = mm_tn_blocks(dproj, h2, "dwin", IN_PROJ_ROWS)
    dx1, dsc2, dsh2, dnmix, do1, dg1 = dh_norm_bwd(dproj, [wv["win_t"]], x1, dx2, p["norm_mix"], sc2, "mix_dh",
                                                   (o1, g1, 0.5))
    dx0, dmod1, dnf1, (dwg1, dwu1, dwd1), _ = _ffn_bwd(
        "ffn1", dx1, do1, dg1, x, h1, gg1, uu1, a1, sc1, p["norm_ffn1"], wv["wg1_t"], wv["wu1_t"], wv["wd1"], None)

    dmod = jnp.concatenate([*dmod1, dsh2, dsc2, dg2, *dmod3], axis=1).reshape(nb, N_MOD * d)
    return dict(
        loss=loss, dx=dx0, dmod=dmod, norm_ffn1=dnf1, norm_mix=dnmix, norm_ffn2=dnf2, norm_final=dnfin,
        ssd_norm_w=dssdn, mla_norm_w=dmlan, q_norm_w=dqn, kv_norm_w=dkvn,
        dt_bias=ddtb[:, :SSD_HEADS], a_log=dalog[:, :SSD_HEADS],
        d_skip=squeeze_heads(ddsk_lane, et_mat, "d_skip_heads")[:, :SSD_HEADS],
        conv_b=dconv[4:5], conv_w=dconv[0:4],
        gw=dict(ffn1_w_gate=dwg1, ffn1_w_up=dwu1, ffn1_w_down=dwd1, ffn2_w_gate=dwg2, ffn2_w_up=dwu2, ffn2_w_down=dwd2,
                w_out=dwo, w_in=dwin, w_ukv=dwukv, w_uq=dwuq))


def kernel(x, c, positions, w_ada, b_ada, norm_ffn1, ffn1_w_gate, ffn1_w_up, ffn1_w_down, norm_mix, w_in, conv_w, conv_b, dt_bias, a_log, d_skip, ssd_norm_w, q_norm_w, w_uq, kv_norm_w, w_ukv, mla_norm_w, w_out, norm_ffn2, ffn2_w_gate, ffn2_w_up, ffn2_w_down, norm_final, loss_target, m_w_ada, m_b_ada, m_norm_ffn1, m_ffn1_w_gate, m_ffn1_w_up, m_ffn1_w_down, m_norm_mix, m_w_in, m_conv_w, m_conv_b, m_dt_bias, m_a_log, m_d_skip, m_ssd_norm_w, m_q_norm_w, m_w_uq, m_kv_norm_w, m_w_ukv, m_mla_norm_w, m_w_out, m_norm_ffn2, m_ffn2_w_gate, m_ffn2_w_up, m_ffn2_w_down, m_norm_final, v_w_ada, v_b_ada, v_norm_ffn1, v_ffn1_w_gate, v_ffn1_w_up, v_ffn1_w_down, v_norm_mix, v_w_in, v_conv_w, v_conv_b, v_dt_bias, v_a_log, v_d_skip, v_ssd_norm_w, v_q_norm_w, v_w_uq, v_kv_norm_w, v_w_ukv, v_mla_norm_w, v_w_out, v_norm_ffn2, v_ffn2_w_gate, v_ffn2_w_up, v_ffn2_w_down, v_norm_final):
    names = ["w_ada", "b_ada", "norm_ffn1", "ffn1_w_gate", "ffn1_w_up", "ffn1_w_down", "norm_mix", "w_in", "conv_w",
             "conv_b", "dt_bias", "a_log", "d_skip", "ssd_norm_w", "q_norm_w", "w_uq", "kv_norm_w", "w_ukv",
             "mla_norm_w", "w_out", "norm_ffn2", "ffn2_w_gate", "ffn2_w_up", "ffn2_w_down", "norm_final"]
    W = dict(zip(names, (w_ada, b_ada, norm_ffn1, ffn1_w_gate, ffn1_w_up, ffn1_w_down, norm_mix, w_in, conv_w, conv_b, dt_bias, a_log, d_skip, ssd_norm_w, q_norm_w, w_uq, kv_norm_w, w_ukv, mla_norm_w, w_out, norm_ffn2, ffn2_w_gate, ffn2_w_up, ffn2_w_down, norm_final)))
    M = dict(zip(names, (m_w_ada, m_b_ada, m_norm_ffn1, m_ffn1_w_gate, m_ffn1_w_up, m_ffn1_w_down, m_norm_mix, m_w_in, m_conv_w, m_conv_b, m_dt_bias, m_a_log, m_d_skip, m_ssd_norm_w, m_q_norm_w, m_w_uq, m_kv_norm_w, m_w_ukv, m_mla_norm_w, m_w_out, m_norm_ffn2, m_ffn2_w_gate, m_ffn2_w_up, m_ffn2_w_down, m_norm_final)))
    V = dict(zip(names, (v_w_ada, v_b_ada, v_norm_ffn1, v_ffn1_w_gate, v_ffn1_w_up, v_ffn1_w_down, v_norm_mix, v_w_in, v_conv_w, v_conv_b, v_dt_bias, v_a_log, v_d_skip, v_ssd_norm_w, v_q_norm_w, v_w_uq, v_kv_norm_w, v_w_ukv, v_mla_norm_w, v_w_out, v_norm_ffn2, v_ffn2_w_gate, v_ffn2_w_up, v_ffn2_w_down, v_norm_final)))

    nb, s, d = x.shape
    me = 4 * lax.axis_index("x") + 2 * lax.axis_index("y") + lax.axis_index("c")
    n_ada = w_ada.shape[2]

    taps, n_cw = conv_w.shape[1:]
    (cg,) = all_gather8([_pack_rows([c, conv_w[0]])], "gather_c")
    c_all = cg[:, 0:nb].reshape(N_DEV * nb, d)
    conv_w_full = cg[:, nb, 0:taps * n_cw].reshape(N_DEV, taps, n_cw).transpose(1, 0, 2).reshape(taps, N_DEV * n_cw)
    shards = [[_shard_view(name, W[name]).astype(BF16) for name in group] for group in GATHER_GROUPS]

    b_ada_cols = lax.dynamic_slice(b_ada, (0, me * n_ada), (1, n_ada))
    mod_cols, c_act = adaln_fwd(c_all, w_ada[0], b_ada_cols, "adaln_fwd")
    (mod_g,) = all_gather8([mod_cols], "gather_mod")
    mod = lax.dynamic_slice(mod_g, (0, me * nb, 0), (N_DEV, nb, n_ada)).transpose(1, 0, 2).reshape(nb, N_MOD, 1, d)
    mod = [mod[:, k] for k in range(N_MOD)]
    h1, *ffn1_w = norm_mod(x, norm_ffn1, mod[1], mod[0], "ffn1_norm", gather=shards[0])
    gathered = dict(zip(GATHER_GROUPS[0], ffn1_w))
    gathered, h1, shards = lax.optimization_barrier((gathered, h1, shards))
    gathered.update(zip(GATHER_GROUPS[1], sc_all_gather8(shards[1], "gather_w_ffn2", 1)))
    gathered.update(zip(GATHER_GROUPS[2], sc_all_gather8(shards[2], "gather_w_mixer", 7)))
    wv = weight_views(gathered)

    P = dict(W)
    P["conv_w"] = conv_w_full
    P["norm_final"] = norm_final.reshape(1, d)
    R = local_step(x, loss_target, positions, mod, wv, P, h1)

    dmod = R["dmod"]
    partial_shapes = [(1,), (1, d), (1, d), (1, d), (1, d), (1, d), (1, d), (1, Q_LORA), (1, KV_LORA),
                      (1, SSD_HEADS), (1, SSD_HEADS), (1, SSD_HEADS), (1, D_CONV), (4, D_CONV), (1, N_MOD * d),
                      (nb, N_MOD * d)]
    partial = _pack_rows([R["loss"][0, :1], R["norm_ffn1"], R["norm_mix"], R["norm_ffn2"], R["norm_final"],
                          R["ssd_norm_w"], R["mla_norm_w"], R["q_norm_w"], R["kv_norm_w"],
                          R["dt_bias"], R["a_log"], R["d_skip"], R["conv_b"], R["conv_w"],
                          sum_rows(dmod, "dmod_rows"), dmod])
    (partial_g,) = all_gather8([partial], "gather_partials")
    (loss, g_nf1, g_nmix, g_nf2, g_nfin, g_ssdn, g_mlan, g_qn, g_kvn, g_dtb, g_alog, g_dskip, g_convb, g_convw,
     g_bada, _) = _unpack_rows(sum_blocks(partial_g, "sum_partials"), partial_shapes)
    dmod_row = sum(-(-math.prod(shp) // D_MODEL) for shp in partial_shapes[:-1])
    dmod_all = partial_g[:, dmod_row:dmod_row + nb * N_MOD].reshape(N_DEV * nb, N_MOD * d)
    g_wada = adaln_bwd(c_act, lax.dynamic_slice(dmod_all, (0, me * n_ada), (N_DEV * nb, n_ada)), "adaln_bwd")
    n_cw = conv_w.shape[2]
    G = {"w_ada": g_wada[None], "b_ada": g_bada, "norm_ffn1": g_nf1, "norm_mix": g_nmix, "norm_ffn2": g_nf2,
         "norm_final": g_nfin.reshape(d), "ssd_norm_w": g_ssdn, "mla_norm_w": g_mlan, "q_norm_w": g_qn,
         "kv_norm_w": g_kvn, "dt_bias": g_dtb, "a_log": g_alog, "d_skip": g_dskip, "conv_b": g_convb,
         "conv_w": lax.dynamic_slice(g_convw, (0, me * n_cw), (4, n_cw))[None]}

    DW, NM, NV = {}, {}, {}
    gw = R["gw"]
    for k, (tag, group) in enumerate(GRAD_GROUPS):
        send = [_grad_blocks(name, gw[name]).reshape(N_DEV, *_shard_view(name, W[name]).shape) for name in group]
        recv = sc_all_to_all8(send, "exchange_" + tag, 2 + k)
        for name, blocks in zip(group, recv):
            res = adamw_blocks(_shard_view(name, W[name]), blocks, _shard_view(name, M[name]), _shard_view(name, V[name]),
                               "adamw_" + name)
            G[name], DW[name], NM[name], NV[name] = [_shard_unview(name, t) for t in res]
    DW["w_ada"], NM["w_ada"], NV["w_ada"] = adamw(w_ada, g_wada, m_w_ada, v_w_ada, "adamw_w_ada")
    small = [n for n in names if n not in DW]
    as2d = lambda a: a.reshape(-1, a.shape[-1])
    outs = adamw_many([as2d(W[n]) for n in small], [as2d(G[n]) for n in small], [as2d(M[n]) for n in small],
                      [as2d(V[n]) for n in small], "adamw_small")
    for res, dst in zip(outs, (DW, NM, NV)):
        for n, t in zip(small, res):
            dst[n] = t.reshape(W[n].shape)
    return (loss.reshape(()), R["dx"], *[G[n] for n in names], *[DW[n] for n in names], *[NM[n] for n in names],
            *[NV[n] for n in names])
```

```python
import math

import jax
import jax.numpy as jnp
from jax import lax
from jax.experimental import pallas as pl
from jax.experimental.pallas import tpu as pltpu
from jax.experimental.pallas import tpu_sc as plsc

F32, BF16, I32 = jnp.float32, jnp.bfloat16, jnp.int32
HI = lax.Precision.HIGHEST
SDS = jax.ShapeDtypeStruct
MESH = pl.DeviceIdType.MESH

D_MODEL = 1024
D_FF = 2816
D_SSD = 1024
SSD_HEADS = 16
SSD_HEAD_DIM = 64
SSD_GROUPS = 2
SSD_STATE = 128
CHUNK = 128
MLA_HEADS = 8
QK_NOPE = 64
QK_ROPE = 32
QK_DIM = 96
V_HEAD = 128
Q_LORA = 384
KV_LORA = 256
ROPE_THETA = 10000.0
N_MOD = 9
EPS = 1e-6
D_CONV = 1536
D_IN = 3248
D_IN_PAD = 3328
HEAD_PAD = 128
N_DEV = 8
ADAM_LR, ADAM_B1, ADAM_B2, ADAM_EPS, ADAM_WD, ADAM_STEP = 0.001, 0.9, 0.999, 1e-08, 0.01, 10

SAVED_ACT = BF16
VMEM_LIMIT = 56 * 1024 * 1024
LANES = 128
NT_DIMS = (((1,), (1,)), ((), ()))
TN_DIMS = (((0,), (0,)), ((), ()))


def _cparams(n_axes):
    return pltpu.CompilerParams(dimension_semantics=("arbitrary",) * n_axes, vmem_limit_bytes=VMEM_LIMIT)


def _row(tm, d):
    return pl.BlockSpec((None, tm, d), lambda b, i: (b, i, 0))


def _bvec(d):
    return pl.BlockSpec((None, 1, d), lambda b, i: (b, 0, 0))


def _full(shape):
    n = len(shape)
    return pl.BlockSpec(shape, lambda *_: (0,) * n)


def _sigmoid(x):
    return 1.0 / (1.0 + jnp.exp(-x))


def _softplus(x):
    return jnp.maximum(x, 0.0) + jnp.log(1.0 + jnp.exp(-jnp.abs(x)))


def _rms(x):
    return lax.rsqrt(jnp.mean(x * x, axis=-1, keepdims=True) + EPS)


def _rms_bwd(dn, n, r):
    return r * (dn - n * jnp.mean(dn * n, axis=-1, keepdims=True))


def _first_step():
    return (pl.program_id(0) == 0) & (pl.program_id(1) == 0)


def _gather_copies(x_refs, out_refs, send_sems, recv_sems, local_sems, part=None):
    mx, my, mc = lax.axis_index("x"), lax.axis_index("y"), lax.axis_index("c")
    me, sibling = (mx, my, mc), (mx, my, 1 - mc)
    chips = [(1 - mx, my), (mx, 1 - my), (1 - mx, 1 - my)]

    def copy(a, k, block, to, src=None):
        rows = out_refs[a].at[4 * block[0] + 2 * block[1] + block[2]]
        return pltpu.make_async_remote_copy(
            src_ref=rows if src is None else src, dst_ref=rows,
            send_sem=send_sems.at[7 * a + k], recv_sem=recv_sems.at[7 * a + k], device_id=to, device_id_type=MESH)

    arrays = range(len(x_refs))
    mine = [pltpu.make_async_copy(x_refs[a], out_refs[a].at[4 * mx + 2 * my + mc], local_sems.at[a]) for a in arrays]
    first = [[copy(a, 0, me, sibling, src=x_refs[a])] + [copy(a, 1 + j, me, (*chip, mc), src=x_refs[a])
                                                          for j, chip in enumerate(chips)] for a in arrays]
    passed = [[copy(a, 4 + j, (*chip, mc), sibling) for j, chip in enumerate(chips)] for a in arrays]
    if part != "finish":
        for a in arrays:
            mine[a].start()
            for cp in first[a]:
                cp.start()
    if part == "start":
        return
    for a in arrays:
        for j, chip in enumerate(chips):
            copy(a, 1 + j, (*chip, mc), me).wait_recv()
            passed[a][j].start()
    for a in arrays:
        copy(a, 0, sibling, me).wait_recv()
        for j, chip in enumerate(chips):
            copy(a, 4 + j, (*chip, 1 - mc), me).wait_recv()
    for a in arrays:
        for cp in first[a] + passed[a]:
            cp.wait_send()
        mine[a].wait()


def _gather_peers():
    mx, my, mc = lax.axis_index("x"), lax.axis_index("y"), lax.axis_index("c")
    return [(mx, my, 1 - mc), (1 - mx, my, mc), (mx, 1 - my, mc), (1 - mx, 1 - my, mc)]


def _comm_scratch(n):
    return [pltpu.SemaphoreType.DMA((7 * n,)), pltpu.SemaphoreType.DMA((7 * n,)), pltpu.SemaphoreType.DMA((n,))]


def all_gather8(xs, name):
    n = len(xs)

    def body(*refs):
        _gather_copies(refs[:n], refs[n:2 * n], *refs[2 * n:])

    return pl.pallas_call(
        body, name=name,
        out_shape=[SDS((N_DEV, *x.shape), x.dtype) for x in xs],
        in_specs=[pl.BlockSpec(memory_space=pl.ANY)] * n,
        out_specs=[pl.BlockSpec(memory_space=pl.ANY)] * n,
        scratch_shapes=_comm_scratch(n),
    )(*xs)


def _exchange_peers():
    mx, my, mc = lax.axis_index("x"), lax.axis_index("y"), lax.axis_index("c")
    return [(1 - mx if rel & 4 else mx, 1 - my if rel & 2 else my, 1 - mc if rel & 1 else mc) for rel in range(1, N_DEV)]


def _exchange_copies(x_refs, out_refs, send_sems, recv_sems, local_sems):
    mx, my, mc = lax.axis_index("x"), lax.axis_index("y"), lax.axis_index("c")
    me = 4 * mx + 2 * my + mc
    copies = []
    for a, (x_ref, out_ref) in enumerate(zip(x_refs, out_refs)):
        mine = pltpu.make_async_copy(x_ref.at[me], out_ref.at[me], local_sems.at[a])
        mine.start()
        copies.append(mine)
        for k, (px, py, pc) in enumerate(_exchange_peers()):
            cp = pltpu.make_async_remote_copy(
                src_ref=x_ref.at[4 * px + 2 * py + pc], dst_ref=out_ref.at[me],
                send_sem=send_sems.at[7 * a + k], recv_sem=recv_sems.at[7 * a + k],
                device_id=(px, py, pc), device_id_type=MESH)
            cp.start()
            copies.append(cp)
    for cp in copies:
        cp.wait()


def all_to_all8(xs, name):
    n = len(xs)

    def body(*refs):
        _exchange_copies(refs[:n], refs[n:2 * n], *refs[2 * n:])

    return pl.pallas_call(
        body, name=name,
        out_shape=[SDS(x.shape, x.dtype) for x in xs],
        in_specs=[pl.BlockSpec(memory_space=pl.ANY)] * n,
        out_specs=[pl.BlockSpec(memory_space=pl.ANY)] * n,
        scratch_shapes=_comm_scratch(n),
    )(*xs)


def _sequencer_kernel(name, collective_id, n_arrays):
    return pl.kernel(
        mesh=plsc.ScalarSubcoreMesh(axis_name="seq", num_cores=1), name=name,
        scratch_types=tuple(_comm_scratch(n_arrays)),
        compiler_params=pltpu.CompilerParams(collective_id=collective_id))


def _handshake(peers):
    barrier = pltpu.get_barrier_semaphore()
    for peer in peers:
        pl.semaphore_signal(barrier, inc=1, device_id=peer, device_id_type=MESH)
    pl.semaphore_wait(barrier, len(peers))


def _hbm_refs(xs, out_shapes):
    x_refs = [jax.new_ref(x, memory_space=pltpu.MemorySpace.HBM) for x in xs]
    out_refs = [jax.empty_ref(SDS(shp, x.dtype), memory_space=pltpu.MemorySpace.HBM) for x, shp in zip(xs, out_shapes)]
    return x_refs, out_refs


def sc_all_gather8(xs, name, collective_id):
    x_refs, out_refs = _hbm_refs(xs, [(N_DEV, *x.shape) for x in xs])

    @_sequencer_kernel(name, collective_id, len(xs))
    def launch(send_sems, recv_sems, local_sems):
        _handshake(_gather_peers())
        _gather_copies(x_refs, out_refs, send_sems, recv_sems, local_sems)

    launch()
    return [ref[...] for ref in out_refs]


def sc_all_to_all8(xs, name, collective_id):
    x_refs, out_refs = _hbm_refs(xs, [x.shape for x in xs])

    @_sequencer_kernel(name, collective_id, len(xs))
    def launch(send_sems, recv_sems, local_sems):
        _handshake(_exchange_peers())
        _exchange_copies(x_refs, out_refs, send_sems, recv_sems, local_sems)

    launch()
    return [ref[...] for ref in out_refs]


def norm_mod(x, w, sc, sh, name, gather=()):
    b, s, d = x.shape
    tm = min(512, s)
    n = len(gather)
    last = (b - 1, s // tm - 1)

    def body(x_ref, w_ref, sc_ref, sh_ref, *refs):
        h_ref = refs[n]
        if n:
            comm = (refs[:n], refs[n + 1:2 * n + 1], *refs[2 * n + 1:])

            @pl.when(_first_step())
            def _():
                _gather_copies(*comm, part="start")
        xv = x_ref[...]
        nv = xv * _rms(xv)
        h_ref[...] = ((nv * w_ref[...]) * (1.0 + sc_ref[...]) + sh_ref[...]).astype(BF16)
        if n:
            @pl.when((pl.program_id(0) == last[0]) & (pl.program_id(1) == last[1]))
            def _():
                _gather_copies(*comm, part="finish")

    hbm = [pl.BlockSpec(memory_space=pl.ANY)] * n
    res = pl.pallas_call(
        body, name=name, grid=(b, s // tm),
        in_specs=[_row(tm, d), _full((1, d)), _bvec(d), _bvec(d)] + hbm,
        out_specs=[_row(tm, d)] + hbm,
        out_shape=[SDS((b, s, d), BF16)] + [SDS((N_DEV, *g.shape), g.dtype) for g in gather],
        scratch_shapes=_comm_scratch(n) if n else [], compiler_params=_cparams(2))(x, w, sc, sh, *gather)
    return res if n else res[0]


def ffn_up(h, wg_t, wu_t, name):
    b, s, d = h.shape
    f = wg_t.shape[0]
    tm, tn = min(1024, s), f // 2

    def body(h_ref, wg_ref, wu_ref, s_ref, t_ref, a_ref):
        hv = h_ref[...]
        g = lax.dot_general(hv, wg_ref[...], NT_DIMS, preferred_element_type=F32)
        u = lax.dot_general(hv, wu_ref[...], NT_DIMS, preferred_element_type=F32)
        sg = _sigmoid(g)
        silu = g * sg
        s_ref[...] = silu.astype(s_ref.dtype)
        t_ref[...] = (u * (sg + silu * (1.0 - sg))).astype(t_ref.dtype)
        a_ref[...] = (silu * u).astype(BF16)

    hs = pl.BlockSpec((None, tm, d), lambda j, bb, i: (bb, i, 0))
    ws = pl.BlockSpec((tn, d), lambda j, bb, i: (j, 0))
    os_ = pl.BlockSpec((None, tm, tn), lambda j, bb, i: (bb, i, j))
    return pl.pallas_call(
        body, name=name, grid=(f // tn, b, s // tm),
        in_specs=[hs, ws, ws], out_specs=[os_, os_, os_],
        out_shape=[SDS((b, s, f), SAVED_ACT), SDS((b, s, f), SAVED_ACT), SDS((b, s, f), BF16)],
        compiler_params=_cparams(3))(h, wg_t, wu_t)


def _norm_mod_tile(xv, w_ref, sc_ref, sh_ref):
    return ((xv * _rms(xv) * w_ref[...]) * (1.0 + sc_ref[...]) + sh_ref[...]).astype(BF16)


def ffn_down(a, wd, x, gate, scale, name, above=None):
    b, s, f = a.shape
    d = wd.shape[1]
    tm = min(1024, s)

    def body(a_ref, wd_ref, x_ref, g_ref, *rest):
        xn_ref, o_ref = rest[-3:-1] if above else rest
        o = jnp.dot(a_ref[...], wd_ref[...], preferred_element_type=F32)
        xn = x_ref[...] + (scale * g_ref[...]) * o
        xn_ref[...] = xn
        o_ref[...] = o.astype(BF16)
        if above:
            rest[-1][...] = _norm_mod_tile(xn, *rest[0:3])

    extra = above is not None
    return pl.pallas_call(
        body, name=name, grid=(b, s // tm),
        in_specs=[_row(tm, f), _full((f, d)), _row(tm, d), _bvec(d)] + ([_full((1, d)), _bvec(d), _bvec(d)] if extra else []),
        out_specs=[_row(tm, d), _row(tm, d)] + ([_row(tm, d)] if extra else []),
        out_shape=[SDS((b, s, d), F32), SDS((b, s, d), BF16)] + ([SDS((b, s, d), BF16)] if extra else []),
        compiler_params=_cparams(2))(a, wd, x, gate, *(above or ()))


def ffn_down_final(a, wd, x, gate, scale, w_final, tgt, name):
    b, s, f = a.shape
    d = wd.shape[1]
    tm = min(1024, s)

    def body(a_ref, wd_ref, x_ref, g_ref, w_ref, t_ref, loss_ref, dx_ref, dw_ref, do_ref, dg_ref):
        @pl.when(_first_step())
        def _():
            loss_ref[...] = jnp.zeros_like(loss_ref)
            dw_ref[...] = jnp.zeros_like(dw_ref)

        @pl.when(pl.program_id(1) == 0)
        def _():
            dg_ref[...] = jnp.zeros_like(dg_ref)
        o = jnp.dot(a_ref[...], wd_ref[...], preferred_element_type=F32)
        sg = scale * g_ref[...]
        xv = x_ref[...] + sg * o
        r = _rms(xv)
        n = xv * r
        wv = w_ref[...]
        e = n * wv - t_ref[...]
        loss_ref[...] += jnp.sum(e * e) * (0.5 / d)
        dy = e * (1.0 / d)
        dw_ref[...] += jnp.sum(dy * n, axis=0, keepdims=True)
        dx = _rms_bwd(dy * wv, n, r)
        dx_ref[...] = dx
        do_ref[...] = (sg * dx).astype(BF16)
        dg_ref[...] += jnp.sum(scale * dx * o, axis=0, keepdims=True)

    return pl.pallas_call(
        body, name=name, grid=(b, s // tm),
        in_specs=[_row(tm, f), _full((f, d)), _row(tm, d), _bvec(d), _full((1, d)), _row(tm, d)],
        out_specs=[_full((1, LANES)), _row(tm, d), _full((1, d)), _row(tm, d), _bvec(d)],
        out_shape=[SDS((1, LANES), F32), SDS((b, s, d), F32), SDS((1, d), F32), SDS((b, s, d), BF16), SDS((b, 1, d), F32)],
        compiler_params=_cparams(2))(a, wd, x, gate, w_final, tgt)


def ffn_dact(do, wd, silu_g, u_dsilu, name):
    b, s, d = do.shape
    f = wd.shape[0]
    tm, tn = min(1024, s), f // 2

    def body(do_ref, wd_ref, s_ref, t_ref, dg_ref, du_ref):
        da = lax.dot_general(do_ref[...], wd_ref[...], NT_DIMS, preferred_element_type=F32)
        dg_ref[...] = (da * t_ref[...].astype(F32)).astype(BF16)
        du_ref[...] = (da * s_ref[...].astype(F32)).astype(BF16)

    dos = pl.BlockSpec((None, tm, d), lambda j, bb, i: (bb, i, 0))
    ws = pl.BlockSpec((tn, d), lambda j, bb, i: (j, 0))
    es = pl.BlockSpec((None, tm, tn), lambda j, bb, i: (bb, i, j))
    return pl.pallas_call(
        body, name=name, grid=(f // tn, b, s // tm),
        in_specs=[dos, ws, es, es], out_specs=[es, es],
        out_shape=[SDS((b, s, f), BF16), SDS((b, s, f), BF16)], compiler_params=_cparams(3))(do, wd, silu_g, u_dsilu)


def mm_tn(a, bm, tma, tnb, name):
    b, s, ka = a.shape
    nb = bm.shape[2]
    tk = min(2048, s)
    nk = s // tk

    def body(a_ref, b_ref, o_ref, acc):
        first = (pl.program_id(2) == 0) & (pl.program_id(3) == 0)
        last = (pl.program_id(2) == b - 1) & (pl.program_id(3) == nk - 1)
        part = lax.dot_general(a_ref[...], b_ref[...], TN_DIMS, preferred_element_type=F32)

        @pl.when(first)
        def _():
            acc[...] = part

        @pl.when(jnp.logical_not(first))
        def _():
            acc[...] += part

        @pl.when(last)
        def _():
            o_ref[...] = acc[...].astype(BF16)

    return pl.pallas_call(
        body, name=name, grid=(ka // tma, nb // tnb, b, nk),
        in_specs=[pl.BlockSpec((None, tk, tma), lambda i, j, bb, k: (bb, k, i)),
                  pl.BlockSpec((None, tk, tnb), lambda i, j, bb, k: (bb, k, j))],
        out_specs=pl.BlockSpec((tma, tnb), lambda i, j, bb, k: (i, j)),
        out_shape=SDS((ka, nb), BF16), scratch_shapes=[pltpu.VMEM((tma, tnb), F32)],
        compiler_params=_cparams(4))(a, bm)


def mm_tn_blocks(a_blocks, bm, name, out_rows=None):
    b, s, nb = bm.shape
    widths = [a.shape[2] for a in a_blocks]
    starts = [sum(widths[:k]) for k in range(len(widths))]
    n_out = sum(seg[2] for seg in out_rows) if out_rows else sum(widths)
    per = n_out // N_DEV
    pieces = []
    for dst, src, rows in out_rows or ():
        while rows:
            n = min(rows, per - dst % per)
            pieces.append((dst // per, dst % per, src, n))
            dst, src, rows = dst + n, src + n, rows - n
    tk = min(2048 if sum(widths) <= 2048 else 1024, s)
    nk = s // tk
    n = len(a_blocks)

    def body(*refs):
        a_refs, b_ref, o_ref, acc = refs[:n], refs[n], refs[n + 1], refs[n + 2]
        first = (pl.program_id(0) == 0) & (pl.program_id(1) == 0)
        last = (pl.program_id(0) == b - 1) & (pl.program_id(1) == nk - 1)

        @pl.when(first)
        def _():
            acc[...] = jnp.zeros_like(acc)
        bv = b_ref[...]
        for a_ref, st, wd in zip(a_refs, starts, widths):
            acc[st:st + wd, :] += lax.dot_general(a_ref[...], bv, TN_DIMS, preferred_element_type=F32)

        @pl.when(last)
        def _():
            if not out_rows:
                o_ref[...] = acc[...].astype(BF16)
            for blk, dst, src, rows in pieces:
                o_ref[blk, dst:dst + rows, :] = acc[src:src + rows, :].astype(BF16)

    out_shape = (N_DEV, per, nb) if out_rows else (n_out, nb)
    return pl.pallas_call(
        body, name=name, grid=(b, nk),
        in_specs=[_row(tk, wd) for wd in widths] + [_row(tk, nb)],
        out_specs=_full(out_shape), out_shape=SDS(out_shape, BF16),
        scratch_shapes=[pltpu.VMEM((sum(widths), nb), F32)], compiler_params=_cparams(2))(*a_blocks, bm)


def _gate_bwd_specs(tm, d, b, s):
    return ([_row(tm, d), _bvec(d)], [_row(tm, d), _bvec(d)], [SDS((b, s, d), BF16), SDS((b, 1, d), F32)])


def _gate_bwd_tile(dx, scale, o_ref, g_ref, do_ref, dg_ref):
    do_ref[...] = ((scale * g_ref[...]) * dx).astype(BF16)
    dg_ref[...] += jnp.sum(scale * dx * o_ref[...].astype(F32), axis=0, keepdims=True)


def dh_norm_bwd(dys, wts, x, dxn, w, sc, name, below=None):
    b, s, d = x.shape
    tm = min(512, s)
    n_in, n_w = len(dys), len(wts)
    extra_in, extra_out, extra_shape = _gate_bwd_specs(tm, d, b, s) if below else ([], [], [])
    starts = [sum(dy.shape[2] for dy in dys[:k]) for k in range(n_in)]

    def body(*refs):
        dy_refs, w_refs = refs[:n_in], refs[n_in:n_in + n_w]
        x_ref, dxn_ref, nw_ref, sc_ref = refs[n_in + n_w:n_in + n_w + 4]
        rest = refs[n_in + n_w + 4:]
        if below:
            o_ref, g_ref, dx_ref, dsc_ref, dsh_ref, dw_ref, do_ref, dg_ref = rest
        else:
            dx_ref, dsc_ref, dsh_ref, dw_ref = rest

        @pl.when(pl.program_id(1) == 0)
        def _():
            dsc_ref[...] = jnp.zeros_like(dsc_ref)
            dsh_ref[...] = jnp.zeros_like(dsh_ref)
            if below:
                dg_ref[...] = jnp.zeros_like(dg_ref)

        @pl.when(_first_step())
        def _():
            dw_ref[...] = jnp.zeros_like(dw_ref)

        def weight(k):
            return w_refs[k][...] if n_w == n_in else w_refs[0][starts[k]:starts[k] + dys[k].shape[2], :]

        dh = jnp.dot(dy_refs[0][...], weight(0), preferred_element_type=F32)
        for k in range(1, n_in):
            dh += jnp.dot(dy_refs[k][...], weight(k), preferred_element_type=F32)
        xv = x_ref[...]
        r = _rms(xv)
        n = xv * r
        nw = nw_ref[...]
        dsc_ref[...] += jnp.sum(dh * (n * nw), axis=0, keepdims=True)
        dsh_ref[...] += jnp.sum(dh, axis=0, keepdims=True)
        dhn = dh * (1.0 + sc_ref[...])
        dw_ref[...] += jnp.sum(dhn * n, axis=0, keepdims=True)
        dx = dxn_ref[...] + _rms_bwd(dhn * nw, n, r)
        dx_ref[...] = dx
        if below:
            _gate_bwd_tile(dx, below[2], o_ref, g_ref, do_ref, dg_ref)

    resident = lambda shape: pl.BlockSpec(shape, lambda *_: (0,) * len(shape), pipeline_mode=pl.Buffered(1))
    in_specs = [_row(tm, dy.shape[2]) for dy in dys] + [resident(wt.shape) for wt in wts]
    in_specs += [_row(tm, d), _row(tm, d), _full((1, d)), _bvec(d)] + extra_in
    return pl.pallas_call(
        body, name=name, grid=(b, s // tm), in_specs=in_specs,
        out_specs=[_row(tm, d), _bvec(d), _bvec(d), _full((1, d))] + extra_out,
        out_shape=[SDS((b, s, d), F32), SDS((b, 1, d), F32), SDS((b, 1, d), F32), SDS((1, d), F32)] + extra_shape,
        compiler_params=_cparams(2))(*dys, *wts, x, dxn, w, sc, *(below[:2] if below else ()))


def in_proj(h, win_t, name):
    b, s, d = h.shape
    tm = min(512, s)
    widths = (D_SSD, D_SSD + 2 * SSD_GROUPS * SSD_STATE, Q_LORA, KV_LORA, LANES)

    def body(h_ref, w_ref, *outs):
        p = lax.dot_general(h_ref[...], w_ref[...], NT_DIMS, preferred_element_type=F32)
        off = 0
        for o_ref, wd in zip(outs, widths):
            o_ref[...] = p[:, off:off + wd]
            off += wd

    return pl.pallas_call(
        body, name=name, grid=(b, s // tm),
        in_specs=[_row(tm, d), _full(win_t.shape)],
        out_specs=[_row(tm, wd) for wd in widths],
        out_shape=[SDS((b, s, wd), F32) for wd in widths], compiler_params=_cparams(2))(h, win_t)


def _halo_prev(ts, d):
    return pl.BlockSpec((None, 8, d), lambda b, i: (b, jnp.maximum(i * (ts // 8) - 1, 0), 0))


CONV_ROWS = 32


def _conv_head(head, u_ref, up_ref, tile):
    head[0:8, :] = jnp.where(tile > 0, up_ref[...], 0.0)
    head[8:8 + CONV_ROWS, :] = u_ref[0:CONV_ROWS, :]


def _conv_windows(u_ref, head, r0):
    if r0 == 0:
        return [head[5 + k:5 + k + CONV_ROWS, :] for k in range(4)]
    return [u_ref[r0 - 3 + k:r0 - 3 + k + CONV_ROWS, :] for k in range(4)]


def _fold8(t):
    acc = t[0:8, :]
    for r in range(8, CONV_ROWS, 8):
        acc += t[r:r + 8, :]
    return acc


def conv_fwd(u, cw, cb, name):
    b, s, dc = u.shape
    ts = min(512, s)
    widths = (D_SSD, SSD_GROUPS * SSD_STATE, SSD_GROUPS * SSD_STATE)

    def body(u_ref, up_ref, w_ref, b_ref, xs_ref, bm_ref, cm_ref, head):
        _conv_head(head, u_ref, up_ref, pl.program_id(1))
        ws = [w_ref[k:k + 1, :] for k in range(4)]
        bias = b_ref[...]
        for r0 in range(0, ts, CONV_ROWS):
            taps = _conv_windows(u_ref, head, r0)
            v = bias + taps[0] * ws[0] + taps[1] * ws[1] + taps[2] * ws[2] + taps[3] * ws[3]
            y = v * _sigmoid(v)
            rs = slice(r0, r0 + CONV_ROWS)
            xs_ref[rs, :] = y[:, 0:D_SSD]
            bm_ref[rs, :] = y[:, D_SSD:D_SSD + 256]
            cm_ref[rs, :] = y[:, D_SSD + 256:D_SSD + 512]

    return pl.pallas_call(
        body, name=name, grid=(b, s // ts),
        in_specs=[_row(ts, dc), _halo_prev(ts, dc), _full((4, dc)), _full((1, dc))],
        out_specs=[_row(ts, wd) for wd in widths],
        out_shape=[SDS((b, s, wd), F32) for wd in widths],
        scratch_shapes=[pltpu.VMEM((8 + CONV_ROWS, dc), F32)], compiler_params=_cparams(2))(u, u, cw, cb)


def conv_bwd(dxs, dbm, dcm, u, cw, cb, name):
    b, s, dc = u.shape
    ts = min(512, s)
    nt = s // ts

    def body(dxs_ref, dbm_ref, dcm_ref, u_ref, up_ref, w_ref, b_ref, du_ref, dwb_ref, head, dvs):
        @pl.when(_first_step())
        def _():
            dwb_ref[...] = jnp.zeros_like(dwb_ref)

        @pl.when(pl.program_id(1) == 0)
        def _():
            dvs[ts:ts + 8, :] = jnp.zeros((8, dc), F32)
        _conv_head(head, u_ref, up_ref, nt - 1 - pl.program_id(1))
        ws = [w_ref[k:k + 1, :] for k in range(4)]
        bias = b_ref[...]
        for r0 in range(0, ts, CONV_ROWS):
            taps = _conv_windows(u_ref, head, r0)
            v = bias + taps[0] * ws[0] + taps[1] * ws[1] + taps[2] * ws[2] + taps[3] * ws[3]
            sg = _sigmoid(v)
            rs = slice(r0, r0 + CONV_ROWS)
            dy = jnp.concatenate([dxs_ref[rs, :], dbm_ref[rs, :], dcm_ref[rs, :]], axis=1)
            dv = dy * (sg * (1.0 + v * (1.0 - sg)))
            dvs[rs, :] = dv
            for k in range(4):
                dwb_ref[8 * k:8 * k + 8, :] += _fold8(dv * taps[k])
            dwb_ref[32:40, :] += _fold8(dv)
        for r0 in range(0, ts, CONV_ROWS):
            win = [dvs[r0 + 3 - k:r0 + 3 - k + CONV_ROWS, :] for k in range(4)]
            acc = win[0] * ws[0] + win[1] * ws[1] + win[2] * ws[2] + win[3] * ws[3]
            du_ref[r0:r0 + CONV_ROWS, :] = acc.astype(BF16)
        dvs[ts:ts + 8, :] = dvs[0:8, :]

    rows = lambda wd: pl.BlockSpec((None, ts, wd), lambda bb, i: (bb, nt - 1 - i, 0))
    prev = pl.BlockSpec((None, 8, dc), lambda bb, i: (bb, jnp.maximum((nt - 1 - i) * (ts // 8) - 1, 0), 0))
    return pl.pallas_call(
        body, name=name, grid=(b, nt),
        in_specs=[rows(D_SSD), rows(256), rows(256), rows(dc), prev, _full((4, dc)), _full((1, dc))],
        out_specs=[rows(dc), _full((40, dc))],
        out_shape=[SDS((b, s, dc), BF16), SDS((40, dc), F32)],
        scratch_shapes=[pltpu.VMEM((8 + CONV_ROWS, dc), F32), pltpu.VMEM((ts + 8, dc), F32)],
        compiler_params=_cparams(2))(dxs, dbm, dcm, u, u, cw, cb)


def conv_grads_fold(x, name):
    c = x.shape[1]

    def body(x_ref, o_ref):
        o_ref[...] = jnp.zeros_like(o_ref)
        for k in range(5):
            o_ref[k:k + 1, :] = jnp.sum(x_ref[8 * k:8 * k + 8, :], axis=0, keepdims=True)

    return pl.pallas_call(body, name=name, out_shape=SDS((8, c), F32))(x)


def _ssd_common(misc_ref, dtb_ref, alog_ref, e_ref):
    ln = CHUNK
    lane = lax.broadcasted_iota(I32, (ln, LANES), 1)
    lane1 = lax.broadcasted_iota(I32, (1, LANES), 1)
    pre = misc_ref[...] + dtb_ref[...]
    dt_s = jnp.where(lane < SSD_HEADS, _softplus(pre), 0.0)
    a_neg = jnp.where(lane1 < SSD_HEADS, -jnp.exp(alog_ref[...]), 0.0)
    ri = lax.broadcasted_iota(I32, (ln, ln), 0)
    ci = lax.broadcasted_iota(I32, (ln, ln), 1)
    tril = ci <= ri
    acum = jnp.dot(tril.astype(F32), dt_s * a_neg, preferred_element_type=F32, precision=HI)
    both_e = _dot_01(jnp.concatenate([dt_s, acum], axis=0), e_ref[...], 3)
    dt_e, acum_e = both_e[0:ln], both_e[ln:2 * ln]
    return dict(pre=pre, dt_s=dt_s, a_neg=a_neg, tril=tril, ri=ri, ci=ci, acum=acum, acum_t=acum.T,
                dt_e=dt_e, eac_e=jnp.exp(acum_e), del_e=jnp.exp(acum_e[ln - 1:ln, :] - acum_e))


def _dot_01(x, m01, terms, dims=(((1,), (0,)), ((), ()))):
    acc, rest = None, x
    for k in range(terms):
        part = rest.astype(BF16)
        if k + 1 < terms:
            rest = rest - part.astype(F32)
        d = lax.dot_general(part, m01, dims, preferred_element_type=F32)
        acc = d if acc is None else acc + d
    return acc


def _decay(cm, h):
    seg = cm["acum"][:, h:h + 1] - cm["acum_t"][h:h + 1, :]
    return jnp.exp(jnp.where(cm["tril"], seg, -jnp.inf))


def ssd_fwd(xs, bm, cm_, misc, z, dtb, alog, dskip_e, norm_w, e_mat, name):
    b, s, _ = xs.shape
    ln, nc = CHUNK, s // CHUNK
    gw = D_SSD // SSD_GROUPS
    hpg = SSD_HEADS // SSD_GROUPS

    def body(xs_ref, b_ref, c_ref, misc_ref, z_ref, dtb_ref, alog_ref, dsk_ref, nw_ref, e_ref,
             ys_ref, y_ref, p_ref, st, yd):
        @pl.when(pl.program_id(1) == 0)
        def _():
            st[...] = jnp.zeros_like(st)
        cm = _ssd_common(misc_ref, dtb_ref, alog_ref, e_ref)
        xsv = xs_ref[...]
        xdt = xsv * cm["dt_e"]
        xdt_b = xdt.astype(BF16)
        xd_b = (xdt * cm["del_e"]).astype(BF16)
        gam_e = cm["eac_e"][ln - 1:ln, :]
        p_ref[...] = st[...]
        groups = [slice(gw * g, gw * (g + 1)) for g in range(SSD_GROUPS)]
        heads = [slice(SSD_HEAD_DIM * h, SSD_HEAD_DIM * (h + 1)) for h in range(SSD_HEADS)]
        bgs = [b_ref[:, SSD_STATE * g:SSD_STATE * (g + 1)].astype(BF16) for g in range(SSD_GROUPS)]
        cgs = [c_ref[:, SSD_STATE * g:SSD_STATE * (g + 1)].astype(BF16) for g in range(SSD_GROUPS)]
        cbs = [lax.dot_general(cg, bg, NT_DIMS, preferred_element_type=F32) for cg, bg in zip(cgs, bgs)]
        sts = [st[:, gs] for gs in groups]
        yoff = [jnp.dot(cg, st_g.astype(BF16), preferred_element_type=F32) * cm["eac_e"][:, gs]
                for cg, st_g, gs in zip(cgs, sts, groups)]
        news = [lax.dot_general(bg, xd_b[:, gs], TN_DIMS, preferred_element_type=F32) for bg, gs in zip(bgs, groups)]
        for gs, st_g, new in zip(groups, sts, news):
            st[:, gs] = st_g * gam_e[:, gs] + new
        ms = [(cbs[h // hpg] * _decay(cm, h)).astype(BF16) for h in range(SSD_HEADS)]
        for h, hs in enumerate(heads):
            yd[:, hs] = jnp.dot(ms[h], xdt_b[:, hs], preferred_element_type=F32)
        y = yd[...] + jnp.concatenate(yoff, axis=1) + dsk_ref[...] * xsv
        y_ref[...] = y
        zz = z_ref[...]
        yg = y * (zz * _sigmoid(zz))
        outs = []
        for g in range(SSD_GROUPS):
            ygg = yg[:, gw * g:gw * (g + 1)]
            outs.append(ygg * _rms(ygg) * nw_ref[:, gw * g:gw * (g + 1)])
        ys_ref[...] = jnp.concatenate(outs, axis=1).astype(BF16)

    row = lambda d: pl.BlockSpec((None, ln, d), lambda bb, c: (bb, c, 0))
    return pl.pallas_call(
        body, name=name, grid=(b, nc),
        in_specs=[row(D_SSD), row(256), row(256), row(LANES), row(D_SSD), _full((1, LANES)), _full((1, LANES)),
                  _full((1, D_SSD)), _full((1, D_SSD)), _full((LANES, D_SSD))],
        out_specs=[row(D_SSD), row(D_SSD), pl.BlockSpec((None, None, SSD_STATE, D_SSD), lambda bb, c: (bb, c, 0, 0))],
        out_shape=[SDS((b, s, D_SSD), BF16), SDS((b, s, D_SSD), F32), SDS((b, nc, SSD_STATE, D_SSD), F32)],
        scratch_shapes=[pltpu.VMEM((SSD_STATE, D_SSD), F32), pltpu.VMEM((ln, D_SSD), F32)],
        compiler_params=_cparams(2))(xs, bm, cm_, misc, z, dtb, alog, dskip_e, norm_w, e_mat)


def ssd_bwd(dys, y, z, xs, bm, cm_, misc, prev, dtb, alog, dskip_e, norm_w, e_mat, et_mat, name):
    b, s, _ = xs.shape
    ln, nc = CHUNK, s // CHUNK
    gw = D_SSD // SSD_GROUPS
    hpg = SSD_HEADS // SSD_GROUPS

    def body(dys_ref, y_ref, z_ref, xs_ref, b_ref, c_ref, misc_ref, p_ref, dtb_ref, alog_ref, dsk_ref, nw_ref,
             e_ref, et_ref, dxs_ref, db_ref, dc_ref, dz_ref, ddt_ref, dnw_ref, ddsk_ref, ddtb_ref, dalog_ref,
             dst, dxd, dac_t):
        @pl.when(_first_step())
        def _():
            for r_ in (dnw_ref, ddsk_ref, ddtb_ref, dalog_ref):
                r_[...] = jnp.zeros_like(r_)

        @pl.when(pl.program_id(1) == 0)
        def _():
            dst[...] = jnp.zeros_like(dst)

        cm = _ssd_common(misc_ref, dtb_ref, alog_ref, e_ref)
        et = et_ref[...]
        squeeze = lambda t: _dot_01(t, et, 2)
        lane = lax.broadcasted_iota(I32, (ln, LANES), 1)
        sub = lax.broadcasted_iota(I32, (LANES, ln), 0)
        xsv = xs_ref[...]
        xdt = xsv * cm["dt_e"]
        xdt_b = xdt.astype(BF16)
        xd_b = (xdt * cm["del_e"]).astype(BF16)
        eac_e = cm["eac_e"]
        gam_e = eac_e[ln - 1:ln, :]

        yv, zz, dyo = y_ref[...], z_ref[...], dys_ref[...]
        sz = _sigmoid(zz)
        silu_z = zz * sz
        yg = yv * silu_z
        dyg, dnw = [], []
        for g in range(SSD_GROUPS):
            gs = slice(gw * g, gw * (g + 1))
            ygg = yg[:, gs]
            r = _rms(ygg)
            n = ygg * r
            dnw.append(jnp.sum(dyo[:, gs] * n, axis=0, keepdims=True))
            dyg.append(_rms_bwd(dyo[:, gs] * nw_ref[:, gs], n, r))
        dyg = jnp.concatenate(dyg, axis=1)
        dnw_ref[...] += jnp.concatenate(dnw, axis=1)
        dz_ref[...] = (dyg * yv * (sz * (1.0 + zz * (1.0 - sz)))).astype(BF16)
        dy = dyg * silu_z
        ddsk_ref[...] += jnp.sum(dy * xsv, axis=0, keepdims=True)
        dy_b = dy.astype(BF16)

        dacum = jnp.zeros((ln, LANES), F32)
        dac_t[...] = jnp.zeros_like(dac_t)
        w1, dgam = [], []
        for g in range(SSD_GROUPS):
            gs = slice(gw * g, gw * (g + 1))
            ss = slice(SSD_STATE * g, SSD_STATE * (g + 1))
            bg = b_ref[:, ss].astype(BF16)
            cg = c_ref[:, ss].astype(BF16)
            cb = lax.dot_general(cg, bg, NT_DIMS, preferred_element_type=F32)
            pt = p_ref[:, gs]
            pt_b = pt.astype(BF16)
            dst_g = dst[:, gs]
            dst_b = dst_g.astype(BF16)
            edy = (dy[:, gs] * eac_e[:, gs]).astype(BF16)
            dcg = lax.dot_general(edy, pt_b, NT_DIMS, preferred_element_type=F32)
            dpt = lax.dot_general(cg, edy, TN_DIMS, preferred_element_type=F32)
            yoff = jnp.dot(cg, pt_b, preferred_element_type=F32) * eac_e[:, gs]
            dxd_g = jnp.dot(bg, dst_b, preferred_element_type=F32)
            dbg = lax.dot_general(xd_b[:, gs], dst_b, NT_DIMS, preferred_element_type=F32)
            ddel = dxd_g * xdt[:, gs] * cm["del_e"][:, gs]
            w1.append(dy[:, gs] * yoff - ddel)
            dgam.append(jnp.sum(ddel, axis=0, keepdims=True) + jnp.sum(dst_g * pt, axis=0, keepdims=True) * gam_e[:, gs])
            dxd[:, gs] = dxd_g * cm["del_e"][:, gs]
            dst[:, gs] = dst_g * gam_e[:, gs] + dpt
            dcb = jnp.zeros((ln, ln), F32)
            for j in range(hpg):
                h = hpg * g + j
                hs = slice(SSD_HEAD_DIM * h, SSD_HEAD_DIM * (h + 1))
                lam = _decay(cm, h)
                m = cb * lam
                dm = lax.dot_general(dy_b[:, hs], xdt_b[:, hs], NT_DIMS, preferred_element_type=F32)
                dxd[:, hs] += lax.dot_general(m.astype(BF16), dy_b[:, hs], TN_DIMS, preferred_element_type=F32)
                dcb += dm * lam
                wl = dm * m
                dacum += jnp.where(lane == h, jnp.sum(wl, axis=1, keepdims=True), 0.0)
                dac_t[...] -= jnp.where(sub == h, jnp.sum(wl, axis=0, keepdims=True), 0.0)
            dcb_b = dcb.astype(BF16)
            dc_ref[:, ss] = dcg + jnp.dot(dcb_b, bg, preferred_element_type=F32)
            db_ref[:, ss] = dbg + lax.dot_general(dcb_b, cg, TN_DIMS, preferred_element_type=F32)

        dxdt = dxd[...]
        dxs_ref[...] = dy * dsk_ref[...] + dxdt * cm["dt_e"]
        dacum += squeeze(jnp.concatenate(w1, axis=1)) + dac_t[...].T
        dlast = squeeze(jnp.broadcast_to(jnp.concatenate(dgam, axis=1), (8, D_SSD)))[0:1, :]
        dacum += jnp.where(lax.broadcasted_iota(I32, (ln, LANES), 0) == ln - 1, dlast, 0.0)
        triu = (cm["ci"] >= cm["ri"]).astype(F32)
        da = jnp.dot(triu, dacum, preferred_element_type=F32, precision=HI)
        ddt = da * cm["a_neg"] + squeeze(dxdt * xsv)
        dalog_ref[...] += jnp.sum(da * cm["dt_s"], axis=0, keepdims=True) * cm["a_neg"]
        ddt_raw = jnp.where(lane < SSD_HEADS, ddt * _sigmoid(cm["pre"]), 0.0)
        ddt_ref[...] = ddt_raw
        ddtb_ref[...] += jnp.sum(ddt_raw, axis=0, keepdims=True)

    row = lambda d: pl.BlockSpec((None, ln, d), lambda bb, c: (bb, nc - 1 - c, 0))
    return pl.pallas_call(
        body, name=name, grid=(b, nc),
        in_specs=[row(D_SSD), row(D_SSD), row(D_SSD), row(D_SSD), row(256), row(256), row(LANES),
                  pl.BlockSpec((None, None, SSD_STATE, D_SSD), lambda bb, c: (bb, nc - 1 - c, 0, 0)),
                  _full((1, LANES)), _full((1, LANES)), _full((1, D_SSD)), _full((1, D_SSD)),
                  _full((LANES, D_SSD)), _full((D_SSD, LANES))],
        out_specs=[row(D_SSD), row(256), row(256), row(D_SSD), row(LANES),
                   _full((1, D_SSD)), _full((1, D_SSD)), _full((1, LANES)), _full((1, LANES))],
        out_shape=[SDS((b, s, D_SSD), F32), SDS((b, s, 256), F32), SDS((b, s, 256), F32), SDS((b, s, D_SSD), BF16),
                   SDS((b, s, LANES), F32), SDS((1, D_SSD), F32), SDS((1, D_SSD), F32), SDS((1, LANES), F32),
                   SDS((1, LANES), F32)],
        scratch_shapes=[pltpu.VMEM((SSD_STATE, D_SSD), F32), pltpu.VMEM((ln, D_SSD), F32), pltpu.VMEM((LANES, ln), F32)],
        compiler_params=_cparams(2))(dys, y, z, xs, bm, cm_, misc, prev, dtb, alog, dskip_e, norm_w, e_mat, et_mat)


def _rope(xv, cc, sp, sm):
    n = xv.shape[1]
    return xv * cc + pltpu.roll(xv, 16, 1) * sp + pltpu.roll(xv, n - 16, 1) * sm


def _rope_bwd(dy, cc, sp, sm):
    n = dy.shape[1]
    return dy * cc + pltpu.roll(dy * sp, n - 16, 1) + pltpu.roll(dy * sm, 16, 1)


def _tile8(t):
    return jnp.concatenate([t] * MLA_HEADS, axis=1)


def qkv_fwd(cq, ckv, misc, cc, sp, sm, qnw, kvnw, wuq_t, wukv_t, place, name):
    b, s, _ = cq.shape
    tm = _att_tile(s)
    hd = MLA_HEADS * HEAD_PAD

    def body(cq_ref, ckv_ref, misc_ref, cc_ref, sp_ref, sm_ref, qnw_ref, kvnw_ref, wq_ref, wkv_ref, pl_ref,
             q_ref, k_ref, v_ref, vt_ref, qn_ref, kvn_ref):
        cqv, ckvv = cq_ref[...], ckv_ref[...]
        qn = (cqv * _rms(cqv) * qnw_ref[...]).astype(BF16)
        kvn = (ckvv * _rms(ckvv) * kvnw_ref[...]).astype(BF16)
        qn_ref[...] = qn
        kvn_ref[...] = kvn
        cc1, sp1, sm1 = cc_ref[...], sp_ref[...], sm_ref[...]
        q = lax.dot_general(qn, wq_ref[...], NT_DIMS, preferred_element_type=F32)
        q_ref[...] = _rope(q, _tile8(cc1), _tile8(sp1), _tile8(sm1)).astype(BF16)
        kv = lax.dot_general(kvn, wkv_ref[...], NT_DIMS, preferred_element_type=F32)
        kr = jnp.dot(misc_ref[...], pl_ref[...], preferred_element_type=F32, precision=HI)
        kr = _rope(kr, cc1, sp1, sm1)
        k_ref[...] = (kv[:, 0:hd] + _tile8(kr)).astype(BF16)
        v_ref[...] = kv[:, hd:2 * hd].astype(BF16)
        for h in range(MLA_HEADS):
            vt_ref[h] = kv[:, hd + HEAD_PAD * h:hd + HEAD_PAD * (h + 1)].T.astype(BF16)

    return pl.pallas_call(
        body, name=name, grid=(b, s // tm),
        in_specs=[_row(tm, Q_LORA), _row(tm, KV_LORA), _row(tm, LANES), _row(tm, LANES), _row(tm, LANES), _row(tm, LANES),
                  _full((1, Q_LORA)), _full((1, KV_LORA)), _full(wuq_t.shape), _full(wukv_t.shape), _full((LANES, LANES))],
        out_specs=[_row(tm, hd), _row(tm, hd), _row(tm, hd),
                   pl.BlockSpec((None, MLA_HEADS, None, HEAD_PAD, tm), lambda bb, i: (bb, 0, i, 0, 0)),
                   _row(tm, Q_LORA), _row(tm, KV_LORA)],
        out_shape=[SDS((b, s, hd), BF16)] * 3 + [SDS((b, MLA_HEADS, s // tm, HEAD_PAD, tm), BF16),
                                                 SDS((b, s, Q_LORA), BF16), SDS((b, s, KV_LORA), BF16)],
        compiler_params=_cparams(2))(cq, ckv, misc, cc, sp, sm, qnw, kvnw, wuq_t, wukv_t, place)


def qkv_bwd(dq, dk, dv, ddt, cq, ckv, cc, sp, sm, qnw, kvnw, wuq_t, wukv_t, place_t, name):
    b, s, _ = cq.shape
    tm = min(512, s)
    hd = MLA_HEADS * HEAD_PAD

    def body(dq_ref, dk_ref, dv_ref, ddt_ref, cq_ref, ckv_ref, cc_ref, sp_ref, sm_ref, qnw_ref, kvnw_ref,
             wq_ref, wkv_ref, plt_ref, dcq_ref, dckv_ref, dmisc_ref, dqp_ref, dkv_ref, dqnw_ref, dkvnw_ref):
        @pl.when(_first_step())
        def _():
            dqnw_ref[...] = jnp.zeros_like(dqnw_ref)
            dkvnw_ref[...] = jnp.zeros_like(dkvnw_ref)
        cc1, sp1, sm1 = cc_ref[...], sp_ref[...], sm_ref[...]
        dqp = _rope_bwd(dq_ref[...].astype(F32), _tile8(cc1), _tile8(sp1), _tile8(sm1)).astype(BF16)
        dqp_ref[...] = dqp
        dkv_b = jnp.concatenate([dk_ref[...], dv_ref[...]], axis=1)
        dkf = dk_ref[...].astype(F32)
        dkv_ref[...] = dkv_b
        dkr = dkf[:, 0:HEAD_PAD]
        for h in range(1, MLA_HEADS):
            dkr += dkf[:, HEAD_PAD * h:HEAD_PAD * (h + 1)]
        dkr = _rope_bwd(dkr, cc1, sp1, sm1)
        dmisc_ref[...] = (jnp.dot(dkr, plt_ref[...], preferred_element_type=F32, precision=HI) + ddt_ref[...]).astype(BF16)

        def norm_bwd(dn_w, xv, w_ref, dw_ref, dx_ref):
            r = _rms(xv)
            n = xv * r
            dw_ref[...] += jnp.sum(dn_w * n, axis=0, keepdims=True)
            dx_ref[...] = _rms_bwd(dn_w * w_ref[...], n, r).astype(BF16)

        norm_bwd(jnp.dot(dqp, wq_ref[...], preferred_element_type=F32), cq_ref[...], qnw_ref, dqnw_ref, dcq_ref)
        norm_bwd(jnp.dot(dkv_b, wkv_ref[...], preferred_element_type=F32), ckv_ref[...], kvnw_ref, dkvnw_ref, dckv_ref)

    return pl.pallas_call(
        body, name=name, grid=(b, s // tm),
        in_specs=[_row(tm, hd), _row(tm, hd), _row(tm, hd), _row(tm, LANES), _row(tm, Q_LORA), _row(tm, KV_LORA),
                  _row(tm, LANES), _row(tm, LANES), _row(tm, LANES), _full((1, Q_LORA)), _full((1, KV_LORA)),
                  _full(wuq_t.shape), _full(wukv_t.shape), _full((LANES, LANES))],
        out_specs=[_row(tm, Q_LORA), _row(tm, KV_LORA), _row(tm, LANES), _row(tm, hd), _row(tm, 2 * hd),
                   _full((1, Q_LORA)), _full((1, KV_LORA))],
        out_shape=[SDS((b, s, Q_LORA), BF16), SDS((b, s, KV_LORA), BF16), SDS((b, s, LANES), BF16),
                   SDS((b, s, hd), BF16), SDS((b, s, 2 * hd), BF16), SDS((1, Q_LORA), F32), SDS((1, KV_LORA), F32)],
        compiler_params=_cparams(2))(dq, dk, dv, ddt, cq, ckv, cc, sp, sm, qnw, kvnw, wuq_t, wukv_t, place_t)


ATT_SCALE = 1.0 / math.sqrt(QK_DIM)
LOG2E = math.log2(math.e)
ATT_SCALE_LOG2E = ATT_SCALE * LOG2E


ATT_HEADS_PER_STEP = 4
ATT_HEADS_PER_STEP_BWD = 2


def _att_tile(s):
    return min(512, s)


def flash_fwd(q, k, vt, name):
    b, s, hd = q.shape
    t = _att_tile(s)
    nb = s // t
    th = t // 2

    hps = ATT_HEADS_PER_STEP
    hw = hps * HEAD_PAD

    def body(q_ref, k_ref, vt_ref, o_ref, lse_ref, m_s, l_s, acc):
        i = pl.program_id(2)
        m_s[...] = jnp.full_like(m_s, -jnp.inf)
        l_s[...] = jnp.zeros_like(l_s)
        acc[...] = jnp.zeros_like(acc)

        def update(j, diagonal):
            chains = [(hh, half) for hh in range(hps) for half in range(2)]
            lanes = lambda hh: slice(HEAD_PAD * hh, HEAD_PAD * (hh + 1))
            cols = lambda half: slice(th * half, th * (half + 1))
            sts = {}
            nkeys = lambda half: th if diagonal and half == 0 else t
            for hh, half in chains:
                kr = pl.ds(pl.multiple_of(j * t, t), nkeys(half))
                st = lax.dot_general(k_ref[kr, lanes(hh)], q_ref[cols(half), lanes(hh)], NT_DIMS,
                                     preferred_element_type=F32)
                if diagonal:
                    row = lax.broadcasted_iota(I32, (nkeys(half), th), 0)
                    col = lax.broadcasted_iota(I32, (nkeys(half), th), 1) + th * half
                    st = jnp.where(row <= col, st, -jnp.inf)
                sts[hh, half] = st
            pts, alphas = {}, {}
            for hh, half in chains:
                st, cs = sts[hh, half], cols(half)
                m_prev = m_s[hh, :, cs]
                m_new = jnp.maximum(m_prev, jnp.max(st, axis=0, keepdims=True))
                alpha = jnp.exp2((m_prev - m_new) * ATT_SCALE_LOG2E)
                pt = jnp.exp2((st - m_new) * ATT_SCALE_LOG2E)
                l_s[hh, :, cs] = alpha * l_s[hh, :, cs] + jnp.sum(pt, axis=0, keepdims=True)
                m_s[hh, :, cs] = m_new
                pts[hh, half], alphas[hh, half] = pt.astype(BF16), alpha
            for hh, half in chains:
                cs = cols(half)
                acc[hh, :, cs] = alphas[hh, half] * acc[hh, :, cs] + jnp.dot(
                    vt_ref[hh, j, :, 0:nkeys(half)], pts[hh, half], preferred_element_type=F32)

        def step(j, carry):
            update(j, False)
            return carry

        lax.fori_loop(0, i, step, 0)
        update(i, True)
        for hh in range(hps):
            o_ref[:, HEAD_PAD * hh:HEAD_PAD * (hh + 1)] = (acc[hh] / l_s[hh]).T
            lse_ref[hh] = m_s[hh] * ATT_SCALE + jnp.log(l_s[hh])

    qs = pl.BlockSpec((None, t, hw), lambda bb, h, i: (bb, i, h))
    ks = pl.BlockSpec((None, s, hw), lambda bb, h, i: (bb, 0, h))
    vs = pl.BlockSpec((None, hps, nb, HEAD_PAD, t), lambda bb, h, i: (bb, h, 0, 0, 0))
    ls = pl.BlockSpec((None, hps, None, 1, t), lambda bb, h, i: (bb, h, i, 0, 0))
    return pl.pallas_call(
        body, name=name, grid=(b, MLA_HEADS // hps, nb),
        in_specs=[qs, ks, vs], out_specs=[qs, ls],
        out_shape=[SDS((b, s, hd), F32), SDS((b, MLA_HEADS, nb, 1, t), F32)],
        scratch_shapes=[pltpu.VMEM((hps, 1, t), F32), pltpu.VMEM((hps, 1, t), F32), pltpu.VMEM((hps, HEAD_PAD, t), F32)],
        compiler_params=_cparams(3))(q, k, vt)


def flash_bwd(q, k, v, do, lse, dlt, name):
    b, s, hd = q.shape
    t = _att_tile(s)
    nb = s // t
    th = t // 2
    lse_r = lse
    dlt_r = dlt.reshape(b, MLA_HEADS, nb, 1, t)

    hps = ATT_HEADS_PER_STEP_BWD
    hw = hps * HEAD_PAD

    def body(q_ref, k_ref, v_ref, do_ref, lse_ref, dlt_ref, dq_ref, dk_ref, dv_ref, dq_s, dk_s, dv_s):
        dq_s[...] = jnp.zeros_like(dq_s)
        dk_s[...] = jnp.zeros_like(dk_s)
        dv_s[...] = jnp.zeros_like(dv_s)

        def tile(j, i, diagonal):
            chains = [(hh, half) for hh in range(hps) for half in range(2)]
            lanes = lambda hh: slice(HEAD_PAD * hh, HEAD_PAD * (hh + 1))
            keys = lambda half: pl.ds(pl.multiple_of(j * t + th * half, th), th)
            q0 = lambda half: th if diagonal and half == 1 else 0
            qsel = lambda half: pl.ds(pl.multiple_of(i * t + q0(half), th), t - q0(half))
            sts, dpts = {}, {}
            for hh, half in chains:
                ls_, ks, qs, nq = lanes(hh), keys(half), qsel(half), t - q0(half)
                st = lax.dot_general(k_ref[ks, ls_], q_ref[qs, ls_], NT_DIMS, preferred_element_type=F32)
                if diagonal:
                    row = lax.broadcasted_iota(I32, (th, nq), 0) + th * half
                    col = lax.broadcasted_iota(I32, (th, nq), 1) + q0(half)
                    st = jnp.where(row <= col, st, -jnp.inf)
                sts[hh, half] = st
                dpts[hh, half] = lax.dot_general(v_ref[ks, ls_], do_ref[qs, ls_], NT_DIMS, preferred_element_type=F32)
            pts, dsts = {}, {}
            for hh, half in chains:
                qcols = slice(q0(half), t)
                pt = jnp.exp2(sts[hh, half] * ATT_SCALE_LOG2E - lse_ref[hh, i][:, qcols] * LOG2E)
                pts[hh, half] = pt.astype(BF16)
                dsts[hh, half] = (pt * (dpts[hh, half] - dlt_ref[hh, i][:, qcols])).astype(BF16)
            for hh, half in chains:
                ls_, ks, qs = lanes(hh), keys(half), qsel(half)
                dv_s[ks, ls_] += jnp.dot(pts[hh, half], do_ref[qs, ls_], preferred_element_type=F32)
                dk_s[ks, ls_] += jnp.dot(dsts[hh, half], q_ref[qs, ls_], preferred_element_type=F32)
                dq_s[qs, ls_] += lax.dot_general(dsts[hh, half], k_ref[ks, ls_], TN_DIMS, preferred_element_type=F32)

        def key_tile(j, carry):
            tile(j, j, True)

            def query_tile(i, c2):
                tile(j, i, False)
                return c2

            lax.fori_loop(j + 1, nb, query_tile, 0)
            return carry

        lax.fori_loop(0, nb, key_tile, 0)
        dq_ref[...] = (dq_s[...] * ATT_SCALE).astype(BF16)
        dk_ref[...] = (dk_s[...] * ATT_SCALE).astype(BF16)
        dv_ref[...] = dv_s[...].astype(BF16)

    hs = pl.BlockSpec((None, s, hw), lambda bb, h: (bb, 0, h))
    ls = pl.BlockSpec((None, hps, nb, 1, t), lambda bb, h: (bb, h, 0, 0, 0))
    return pl.pallas_call(
        body, name=name, grid=(b, MLA_HEADS // hps),
        in_specs=[hs, hs, hs, hs, ls, ls], out_specs=[hs, hs, hs],
        out_shape=[SDS((b, s, hd), BF16)] * 3, scratch_shapes=[pltpu.VMEM((s, hw), F32)] * 3,
        compiler_params=_cparams(2))(q, k, v, do, lse_r, dlt_r)


def out_proj(ys, attn, mnw, wo, x, gate, above, name):
    b, s, d = x.shape
    tm = min(512, s)

    def body(ys_ref, at_ref, mnw_ref, wo_ref, x_ref, g_ref, nw_ref, sc_ref, sh_ref, xn_ref, o_ref, ym_ref, h_ref):
        av = at_ref[...]
        ym = (av * _rms(av) * mnw_ref[...]).astype(BF16)
        ym_ref[...] = ym
        o = jnp.dot(ys_ref[...], wo_ref[0:D_SSD, :], preferred_element_type=F32)
        o += jnp.dot(ym, wo_ref[D_SSD:2 * D_SSD, :], preferred_element_type=F32)
        xn = x_ref[...] + g_ref[...] * o
        xn_ref[...] = xn
        o_ref[...] = o.astype(BF16)
        h_ref[...] = _norm_mod_tile(xn, nw_ref, sc_ref, sh_ref)

    return pl.pallas_call(
        body, name=name, grid=(b, s // tm),
        in_specs=[_row(tm, D_SSD), _row(tm, D_SSD), _full((1, D_SSD)), _full(wo.shape), _row(tm, d), _bvec(d),
                  _full((1, d)), _bvec(d), _bvec(d)],
        out_specs=[_row(tm, d), _row(tm, d), _row(tm, D_SSD), _row(tm, d)],
        out_shape=[SDS((b, s, d), F32), SDS((b, s, d), BF16), SDS((b, s, D_SSD), BF16), SDS((b, s, d), BF16)],
        compiler_params=_cparams(2))(ys, attn, mnw, wo, x, gate, *above)


def out_proj_bwd(dout, attn, mnw, wo, name):
    b, s, d = dout.shape
    tm = min(512, s)

    def body(do_ref, at_ref, mnw_ref, wo_ref, dys_ref, dat_ref, dlt_ref, dw_ref):
        lane = lax.broadcasted_iota(I32, (tm, LANES), 1)
        @pl.when(_first_step())
        def _():
            dw_ref[...] = jnp.zeros_like(dw_ref)
        dov = do_ref[...]
        dys_ref[...] = lax.dot_general(dov, wo_ref[0:D_SSD, :], NT_DIMS, preferred_element_type=F32)
        dym = lax.dot_general(dov, wo_ref[D_SSD:2 * D_SSD, :], NT_DIMS, preferred_element_type=F32)
        av = at_ref[...]
        r = _rms(av)
        n = av * r
        dw_ref[...] += jnp.sum(dym * n, axis=0, keepdims=True)
        dat = _rms_bwd(dym * mnw_ref[...], n, r)
        dat_ref[...] = dat.astype(BF16)
        prod = dat * av
        cols = jnp.zeros((tm, LANES), F32)
        for h in range(MLA_HEADS):
            cols += jnp.where(lane == h, jnp.sum(prod[:, HEAD_PAD * h:HEAD_PAD * (h + 1)], axis=1, keepdims=True), 0.0)
        dlt_ref[...] = cols.T[0:MLA_HEADS, :]

    return pl.pallas_call(
        body, name=name, grid=(b, s // tm),
        in_specs=[_row(tm, d), _row(tm, D_SSD), _full((1, D_SSD)), _full(wo.shape)],
        out_specs=[_row(tm, D_SSD), _row(tm, D_SSD),
                   pl.BlockSpec((None, MLA_HEADS, tm), lambda bb, i: (bb, 0, i)), _full((1, D_SSD))],
        out_shape=[SDS((b, s, D_SSD), F32), SDS((b, s, D_SSD), BF16), SDS((b, MLA_HEADS, s), F32),
                   SDS((1, D_SSD), F32)],
        compiler_params=_cparams(2))(dout, attn, mnw, wo)


def adaln_fwd(c_all, w_ada, b_ada, name):
    nb, d = c_all.shape
    n = w_ada.shape[1]

    def body(c_ref, w_ref, b_ref, m_ref, ca_ref):
        cv = c_ref[...]
        ca = (cv * _sigmoid(cv)).astype(BF16)
        ca_ref[...] = ca
        m_ref[...] = jnp.dot(ca, w_ref[...].astype(BF16), preferred_element_type=F32) + b_ref[...]

    return pl.pallas_call(
        body, name=name, out_shape=[SDS((nb, n), F32), SDS((nb, d), BF16)],
        compiler_params=pltpu.CompilerParams(vmem_limit_bytes=VMEM_LIMIT))(c_all, w_ada, b_ada)


def adaln_bwd(c_act, dmod_cols, name):
    d, n = c_act.shape[1], dmod_cols.shape[1]

    def body(c_ref, dm_ref, gw_ref):
        gw_ref[...] = lax.dot_general(c_ref[...], dm_ref[...].astype(BF16), TN_DIMS, preferred_element_type=F32)

    return pl.pallas_call(
        body, name=name, out_shape=SDS((d, n), F32),
        compiler_params=pltpu.CompilerParams(vmem_limit_bytes=VMEM_LIMIT))(c_act, dmod_cols)


def sum_rows(x, name):
    def body(x_ref, o_ref):
        o_ref[...] = jnp.sum(x_ref[...], axis=0, keepdims=True)
    return pl.pallas_call(body, name=name, out_shape=SDS((1, x.shape[1]), F32))(x)


def squeeze_heads(x, et_mat, name):
    def body(x_ref, et_ref, o_ref):
        xv = jnp.broadcast_to(x_ref[...], (8, x.shape[1]))
        o_ref[...] = _dot_01(xv, et_ref[...], 3)[0:1, :]
    return pl.pallas_call(body, name=name, out_shape=SDS((1, LANES), F32))(x, et_mat)


def sum_blocks(x, name):
    n, r, c = x.shape
    tr = next(cand for cand in (256, 128, 64, 32, 16, 8) if r % cand == 0)

    def body(x_ref, o_ref):
        acc = x_ref[0].astype(F32)
        for k in range(1, n):
            acc += x_ref[k].astype(F32)
        o_ref[...] = acc

    return pl.pallas_call(
        body, name=name, grid=(r // tr,), in_specs=[pl.BlockSpec((n, tr, c), lambda i: (0, i, 0))],
        out_specs=pl.BlockSpec((tr, c), lambda i: (i, 0)), out_shape=SDS((r, c), F32),
        compiler_params=_cparams(1))(x)


def _adam_math(w, g, m, v):
    m = ADAM_B1 * m + (1.0 - ADAM_B1) * g
    v = ADAM_B2 * v + (1.0 - ADAM_B2) * (g * g)
    m_hat = m / (1.0 - ADAM_B1 ** ADAM_STEP)
    v_hat = v / (1.0 - ADAM_B2 ** ADAM_STEP)
    return -ADAM_LR * (m_hat / (jnp.sqrt(v_hat) + ADAM_EPS) + ADAM_WD * w), m, v


def adamw(w, g, m, v, name):
    r, c = w.shape[-2:]
    tr = r
    for cand in (512, 256, 128, 64, 32, 16, 8):
        if r % cand == 0 and cand * c * 4 <= 2 * 1024 * 1024:
            tr = cand
            break

    def body(w_ref, g_ref, m_ref, v_ref, d_ref, mo_ref, vo_ref):
        d_ref[...], mo_ref[...], vo_ref[...] = _adam_math(w_ref[...], g_ref[...], m_ref[...], v_ref[...])

    def spec(a):
        return pl.BlockSpec((tr, c), lambda i: (i, 0)) if a.ndim == 2 else pl.BlockSpec((None, tr, c), lambda i: (0, i, 0))

    return pl.pallas_call(
        body, name=name, grid=(r // tr,), in_specs=[spec(w), spec(g), spec(m), spec(v)], out_specs=[spec(w)] * 3,
        out_shape=[SDS(w.shape, F32)] * 3, compiler_params=_cparams(1))(w, g, m, v)


def adamw_blocks(w, blocks, m, v, name):
    r, c = w.shape
    tr = next((cand for cand in range(r // 32 * 16, 0, -16) if r % cand == 0), r)

    def body(w_ref, b_ref, m_ref, v_ref, g_ref, d_ref, mo_ref, vo_ref):
        g = b_ref[0].astype(F32)
        for k in range(1, N_DEV):
            g += b_ref[k].astype(F32)
        g_ref[...] = g
        d_ref[...], mo_ref[...], vo_ref[...] = _adam_math(w_ref[...], g, m_ref[...], v_ref[...])

    spec = pl.BlockSpec((tr, c), lambda i: (i, 0))
    return pl.pallas_call(
        body, name=name, grid=(r // tr,),
        in_specs=[spec, pl.BlockSpec((N_DEV, tr, c), lambda i: (0, i, 0)), spec, spec], out_specs=[spec] * 4,
        out_shape=[SDS((r, c), F32)] * 4, compiler_params=_cparams(1))(w, blocks, m, v)


def adamw_many(ws, gs, ms, vs, name):
    n = len(ws)

    def body(*refs):
        w_r, g_r, m_r, v_r = (refs[k * n:(k + 1) * n] for k in range(4))
        d_r, mo_r, vo_r = (refs[(4 + k) * n:(5 + k) * n] for k in range(3))
        for k in range(n):
            d_r[k][...], mo_r[k][...], vo_r[k][...] = _adam_math(w_r[k][...], g_r[k][...], m_r[k][...], v_r[k][...])

    shapes = [SDS(w.shape, F32) for w in ws]
    outs = pl.pallas_call(body, name=name, out_shape=shapes * 3)(*ws, *gs, *ms, *vs)
    return outs[:n], outs[n:2 * n], outs[2 * n:]


TRANSPOSED = ("ffn1_w_gate", "ffn1_w_up", "ffn2_w_gate", "ffn2_w_up", "w_in", "w_ukv", "w_uq")
GATHER_GROUPS = (("ffn1_w_gate", "ffn1_w_up"), ("ffn2_w_gate", "ffn2_w_up", "ffn2_w_down"),
                 ("ffn1_w_down", "w_in", "w_ukv", "w_uq", "w_out"))
GRAD_GROUPS = (("ffn2", ("ffn2_w_gate", "ffn2_w_up", "ffn2_w_down")), ("mixer", ("w_out", "w_in", "w_ukv", "w_uq")),
               ("ffn1_down", ("ffn1_w_down",)), ("ffn1_gate", ("ffn1_w_gate",)), ("ffn1_up", ("ffn1_w_up",)))


def _shard_view(name, w):
    return w[0].T if name in TRANSPOSED else w[0]


def _shard_unview(name, t):
    return t.T[None] if name in TRANSPOSED else t[None]


def _grad_blocks(name, gw):
    if name == "w_ukv":
        hd = MLA_HEADS * HEAD_PAD
        return jnp.concatenate([gw[:hd].reshape(MLA_HEADS, HEAD_PAD, KV_LORA)[:, :QK_NOPE],
                                gw[hd:].reshape(MLA_HEADS, V_HEAD, KV_LORA)], axis=1)
    if name == "w_uq":
        return gw.reshape(MLA_HEADS, HEAD_PAD, Q_LORA)[:, :QK_DIM]
    return gw.reshape(N_DEV, -1, D_MODEL)


def _pack_rows(arrs):
    parts = []
    for a in arrs:
        flat = a.reshape(-1).astype(F32)
        pad = (-flat.shape[0]) % D_MODEL
        if pad:
            flat = jnp.pad(flat, (0, pad))
        parts.append(flat.reshape(-1, D_MODEL))
    out = jnp.concatenate(parts, axis=0)
    pad = (-out.shape[0]) % 8
    if pad:
        out = jnp.pad(out, ((0, pad), (0, 0)))
    return out


def _unpack_rows(packed, shapes):
    out, row = [], 0
    for shp in shapes:
        n = math.prod(shp)
        nrow = -(-n // D_MODEL)
        out.append(packed[row:row + nrow].reshape(-1)[:n].reshape(shp))
        row += nrow
    return out


IN_PROJ_ROWS = ((0, 0, 2560), (2560, 3200, 16), (2576, 2560, 384), (2960, 2944, 256), (3216, 3216, 32))


def _in_proj_rows(w_t):
    parts = [w_t[src:src + rows] for src, _, rows in sorted(IN_PROJ_ROWS, key=lambda seg: seg[1])]
    return jnp.concatenate(parts + [jnp.zeros((D_IN_PAD - D_IN, D_MODEL), w_t.dtype)], axis=0)


def _rope_tables(positions):
    half = QK_ROPE // 2
    inv_freq = ROPE_THETA ** (-jnp.arange(0, QK_ROPE, 2, dtype=F32) / QK_ROPE)
    ang_t = positions[:, None, :].astype(F32) * inv_freq[:, None]
    cos_t, sin_t = jnp.cos(ang_t), jnp.sin(ang_t)
    b, _, s = ang_t.shape
    ts = min(2048, s)

    def body(c_ref, s_ref, cc_ref, sp_ref, sm_ref):
        row = lax.broadcasted_iota(I32, (half, LANES), 0)
        lane = lax.broadcasted_iota(I32, (half, LANES), 1)
        first, second = lane == QK_NOPE + row, lane == QK_NOPE + half + row

        spread = lambda x, where: _dot_01(x, where.astype(BF16), 3, TN_DIMS)
        lane1 = lax.broadcasted_iota(I32, (1, LANES), 1)
        ones = jnp.where((lane1 < QK_NOPE) | (lane1 >= QK_NOPE + QK_ROPE), 1.0, 0.0)
        cc_ref[...] = spread(c_ref[...], first | second) + ones
        sp_ref[...] = spread(s_ref[...], second)
        sm_ref[...] = -spread(s_ref[...], first)

    src = pl.BlockSpec((None, half, ts), lambda bb, i: (bb, 0, i))
    return pl.pallas_call(
        body, name="rope_tables", grid=(b, s // ts), in_specs=[src, src], out_specs=[_row(ts, LANES)] * 3,
        out_shape=[SDS((b, s, LANES), F32)] * 3, compiler_params=_cparams(2))(cos_t, sin_t)


def weight_views(gathered):
    full = lambda name: gathered[name].reshape(-1, gathered[name].shape[2])
    ukv = full("w_ukv").reshape(MLA_HEADS, QK_NOPE + V_HEAD, KV_LORA)
    wukv_t = jnp.concatenate([jnp.pad(ukv[:, :QK_NOPE], ((0, 0), (0, HEAD_PAD - QK_NOPE), (0, 0))).reshape(-1, KV_LORA),
                              ukv[:, QK_NOPE:].reshape(-1, KV_LORA)], axis=0)
    uq = full("w_uq").reshape(MLA_HEADS, QK_DIM, Q_LORA)
    wuq_t = jnp.pad(uq, ((0, 0), (0, HEAD_PAD - QK_DIM), (0, 0))).reshape(-1, Q_LORA)
    return dict(wg1_t=full("ffn1_w_gate"), wu1_t=full("ffn1_w_up"), wd1=full("ffn1_w_down"),
                wg2_t=full("ffn2_w_gate"), wu2_t=full("ffn2_w_up"), wd2=full("ffn2_w_down"),
                wo=full("w_out"), win_t=_in_proj_rows(full("w_in")), wukv_t=wukv_t, wuq_t=wuq_t)


def _ffn_bwd(tag, dxn, do, dgate, x, h, gg, uu, a, sc, norm_w, wg_t, wu_t, wd, below):
    f2 = wd.shape[0] // 2
    dwd = mm_tn(a, do, f2, D_MODEL, tag + "_dwd")
    dgg, duu = ffn_dact(do, wd, gg, uu, tag + "_dact")
    dwg_t = mm_tn(dgg, h, f2, D_MODEL, tag + "_dwg")
    dwu_t = mm_tn(duu, h, f2, D_MODEL, tag + "_dwu")
    dx, dsc, dsh, dnw, *nxt = dh_norm_bwd([dgg, duu], [wg_t, wu_t], x, dxn, norm_w, sc, tag + "_dh", below)
    return dx, (dsh, dsc, dgate), dnw, (dwg_t, dwu_t, dwd), nxt


def local_step(x, tgt, positions, mod, wv, p, h1=None):
    nb, s, d = x.shape
    sh1, sc1, g1, sh2, sc2, g2, sh3, sc3, g3 = mod
    cc, sp, sm = _rope_tables(positions)
    lane_head = jnp.arange(D_SSD, dtype=I32)[None, :] // SSD_HEAD_DIM
    e_mat = (lane_head == jnp.arange(LANES, dtype=I32)[:, None]).astype(BF16)
    et_mat = e_mat.T
    rr, cl = jnp.arange(LANES, dtype=I32)[:, None], jnp.arange(LANES, dtype=I32)[None, :]
    place = ((cl == rr + (QK_NOPE - SSD_HEADS)) & (rr >= SSD_HEADS) & (rr < SSD_HEADS + QK_ROPE)).astype(F32)
    dtb = jnp.pad(p["dt_bias"], ((0, 0), (0, LANES - SSD_HEADS)))
    alog = jnp.pad(p["a_log"], ((0, 0), (0, LANES - SSD_HEADS)))
    dskip_e = jnp.repeat(p["d_skip"], SSD_HEAD_DIM, axis=1)

    if h1 is None:
        h1 = norm_mod(x, p["norm_ffn1"], sc1, sh1, "ffn1_norm")
    gg1, uu1, a1 = ffn_up(h1, wv["wg1_t"], wv["wu1_t"], "ffn1_up")
    x1, o1, h2 = ffn_down(a1, wv["wd1"], x, g1, 0.5, "ffn1_down", (p["norm_mix"], sc2, sh2))
    z, u, cq, ckv, misc = in_proj(h2, wv["win_t"], "in_proj")
    xs, bm, cm_ = conv_fwd(u, p["conv_w"], p["conv_b"], "conv_fwd")
    ys, y, prev = ssd_fwd(xs, bm, cm_, misc, z, dtb, alog, dskip_e, p["ssd_norm_w"], e_mat, "ssd_fwd")
    q, k, v, vt, qn, kvn = qkv_fwd(cq, ckv, misc, cc, sp, sm, p["q_norm_w"], p["kv_norm_w"], wv["wuq_t"], wv["wukv_t"],
                               place, "qkv_fwd")
    attn, lse = flash_fwd(q, k, vt, "flash_fwd")
    x2, o2, ym, h3 = out_proj(ys, attn, p["mla_norm_w"], wv["wo"], x1, g2, (p["norm_ffn2"], sc3, sh3), "out_proj")
    gg3, uu3, a3 = ffn_up(h3, wv["wg2_t"], wv["wu2_t"], "ffn2_up")
    loss, dx3, dnfin, do3, dg3 = ffn_down_final(a3, wv["wd2"], x2, g3, 0.5, p["norm_final"], tgt, "ffn2_down_loss")

    dx2, dmod3, dnf2, (dwg2, dwu2, dwd2), (dout, dg2) = _ffn_bwd(
        "ffn2", dx3, do3, dg3, x2, h3, gg3, uu3, a3, sc3, p["norm_ffn2"], wv["wg2_t"], wv["wu2_t"], wv["wd2"],
        (o2, g2, 1.0))
    dys, dattn, dlt, dmlan = out_proj_bwd(dout, attn, p["mla_norm_w"], wv["wo"], "out_proj_bwd")
    dwo = mm_tn_blocks([ys, ym], dout, "dwo")
    dxs, dbm, dcm, dz, ddt, dssdn, ddsk_lane, ddtb, dalog = ssd_bwd(
        dys, y, z, xs, bm, cm_, misc, prev, dtb, alog, dskip_e, p["ssd_norm_w"], e_mat, et_mat, "ssd_bwd")
    dq, dk, dv = flash_bwd(q, k, v, dattn, lse, dlt, "flash_bwd")
    dcq, dckv, dmisc, dqp, dkvc, dqn, dkvn = qkv_bwd(dq, dk, dv, ddt, cq, ckv, cc, sp, sm, p["q_norm_w"], p["kv_norm_w"],
                                                     wv["wuq_t"], wv["wukv_t"], place.T, "qkv_bwd")
    dwuq = mm_tn(dqp, qn, MLA_HEADS * HEAD_PAD, Q_LORA, "dwuq")
    dwukv = mm_tn(dkvc, kvn, MLA_HEADS * HEAD_PAD, KV_LORA, "dwukv")
    du, dconv = conv_bwd(dxs, dbm, dcm, u, p["conv_w"], p["conv_b"], "conv_bwd")
    dconv = conv_grads_fold(dconv, "conv_grads_fold")
    dproj = [dz, du, dcq, dckv, dmisc]
    dwin = mm_tn_blocks(dproj, h2, "dwin", IN_PROJ_ROWS)
    dx1, dsc2, dsh2, dnmix, do1, dg1 = dh_norm_bwd(dproj, [wv["win_t"]], x1, dx2, p["norm_mix"], sc2, "mix_dh",
                                                   (o1, g1, 0.5))
    dx0, dmod1, dnf1, (dwg1, dwu1, dwd1), _ = _ffn_bwd(
        "ffn1", dx1, do1, dg1, x, h1, gg1, uu1, a1, sc1, p["norm_ffn1"], wv["wg1_t"], wv["wu1_t"], wv["wd1"], None)

    dmod = jnp.concatenate([*dmod1, dsh2, dsc2, dg2, *dmod3], axis=1).reshape(nb, N_MOD * d)
    return dict(
        loss=loss, dx=dx0, dmod=dmod, norm_ffn1=dnf1, norm_mix=dnmix, norm_ffn2=dnf2, norm_final=dnfin,
        ssd_norm_w=dssdn, mla_norm_w=dmlan, q_norm_w=dqn, kv_norm_w=dkvn,
        dt_bias=ddtb[:, :SSD_HEADS], a_log=dalog[:, :SSD_HEADS],
        d_skip=squeeze_heads(ddsk_lane, et_mat, "d_skip_heads")[:, :SSD_HEADS],
        conv_b=dconv[4:5], conv_w=dconv[0:4],
        gw=dict(ffn1_w_gate=dwg1, ffn1_w_up=dwu1, ffn1_w_down=dwd1, ffn2_w_gate=dwg2, ffn2_w_up=dwu2, ffn2_w_down=dwd2,
                w_out=dwo, w_in=dwin, w_ukv=dwukv, w_uq=dwuq))


def kernel(x, c, positions, w_ada, b_ada, norm_ffn1, ffn1_w_gate, ffn1_w_up, ffn1_w_down, norm_mix, w_in, conv_w, conv_b, dt_bias, a_log, d_skip, ssd_norm_w, q_norm_w, w_uq, kv_norm_w, w_ukv, mla_norm_w, w_out, norm_ffn2, ffn2_w_gate, ffn2_w_up, ffn2_w_down, norm_final, loss_target, m_w_ada, m_b_ada, m_norm_ffn1, m_ffn1_w_gate, m_ffn1_w_up, m_ffn1_w_down, m_norm_mix, m_w_in, m_conv_w, m_conv_b, m_dt_bias, m_a_log, m_d_skip, m_ssd_norm_w, m_q_norm_w, m_w_uq, m_kv_norm_w, m_w_ukv, m_mla_norm_w, m_w_out, m_norm_ffn2, m_ffn2_w_gate, m_ffn2_w_up, m_ffn2_w_down, m_norm_final, v_w_ada, v_b_ada, v_norm_ffn1, v_ffn1_w_gate, v_ffn1_w_up, v_ffn1_w_down, v_norm_mix, v_w_in, v_conv_w, v_conv_b, v_dt_bias, v_a_log, v_d_skip, v_ssd_norm_w, v_q_norm_w, v_w_uq, v_kv_norm_w, v_w_ukv, v_mla_norm_w, v_w_out, v_norm_ffn2, v_ffn2_w_gate, v_ffn2_w_up, v_ffn2_w_down, v_norm_final):
    names = ["w_ada", "b_ada", "norm_ffn1", "ffn1_w_gate", "ffn1_w_up", "ffn1_w_down", "norm_mix", "w_in", "conv_w",
             "conv_b", "dt_bias", "a_log", "d_skip", "ssd_norm_w", "q_norm_w", "w_uq", "kv_norm_w", "w_ukv",
             "mla_norm_w", "w_out", "norm_ffn2", "ffn2_w_gate", "ffn2_w_up", "ffn2_w_down", "norm_final"]
    W = dict(zip(names, (w_ada, b_ada, norm_ffn1, ffn1_w_gate, ffn1_w_up, ffn1_w_down, norm_mix, w_in, conv_w, conv_b, dt_bias, a_log, d_skip, ssd_norm_w, q_norm_w, w_uq, kv_norm_w, w_ukv, mla_norm_w, w_out, norm_ffn2, ffn2_w_gate, ffn2_w_up, ffn2_w_down, norm_final)))
    M = dict(zip(names, (m_w_ada, m_b_ada, m_norm_ffn1, m_ffn1_w_gate, m_ffn1_w_up, m_ffn1_w_down, m_norm_mix, m_w_in, m_conv_w, m_conv_b, m_dt_bias, m_a_log, m_d_skip, m_ssd_norm_w, m_q_norm_w, m_w_uq, m_kv_norm_w, m_w_ukv, m_mla_norm_w, m_w_out, m_norm_ffn2, m_ffn2_w_gate, m_ffn2_w_up, m_ffn2_w_down, m_norm_final)))
    V = dict(zip(names, (v_w_ada, v_b_ada, v_norm_ffn1, v_ffn1_w_gate, v_ffn1_w_up, v_ffn1_w_down, v_norm_mix, v_w_in, v_conv_w, v_conv_b, v_dt_bias, v_a_log, v_d_skip, v_ssd_norm_w, v_q_norm_w, v_w_uq, v_kv_norm_w, v_w_ukv, v_mla_norm_w, v_w_out, v_norm_ffn2, v_ffn2_w_gate, v_ffn2_w_up, v_ffn2_w_down, v_norm_final)))

    nb, s, d = x.shape
    me = 4 * lax.axis_index("x") + 2 * lax.axis_index("y") + lax.axis_index("c")
    n_ada = w_ada.shape[2]

    taps, n_cw = conv_w.shape[1:]
    (cg,) = all_gather8([_pack_rows([c, conv_w[0]])], "gather_c")
    c_all = cg[:, 0:nb].reshape(N_DEV * nb, d)
    conv_w_full = cg[:, nb, 0:taps * n_cw].reshape(N_DEV, taps, n_cw).transpose(1, 0, 2).reshape(taps, N_DEV * n_cw)
    shards = [[_shard_view(name, W[name]).astype(BF16) for name in group] for group in GATHER_GROUPS]

    b_ada_cols = lax.dynamic_slice(b_ada, (0, me * n_ada), (1, n_ada))
    mod_cols, c_act = adaln_fwd(c_all, w_ada[0], b_ada_cols, "adaln_fwd")
    (mod_g,) = all_gather8([mod_cols], "gather_mod")
    mod = lax.dynamic_slice(mod_g, (0, me * nb, 0), (N_DEV, nb, n_ada)).transpose(1, 0, 2).reshape(nb, N_MOD, 1, d)
    mod = [mod[:, k] for k in range(N_MOD)]
    h1, *ffn1_w = norm_mod(x, norm_ffn1, mod[1], mod[0], "ffn1_norm", gather=shards[0])
    gathered = dict(zip(GATHER_GROUPS[0], ffn1_w))
    gathered, h1, shards = lax.optimization_barrier((gathered, h1, shards))
    gathered.update(zip(GATHER_GROUPS[1], sc_all_gather8(shards[1], "gather_w_ffn2", 1)))
    gathered.update(zip(GATHER_GROUPS[2], sc_all_gather8(shards[2], "gather_w_mixer", 7)))
    wv = weight_views(gathered)

    P = dict(W)
    P["conv_w"] = conv_w_full
    P["norm_final"] = norm_final.reshape(1, d)
    R = local_step(x, loss_target, positions, mod, wv, P, h1)

    dmod = R["dmod"]
    partial_shapes = [(1,), (1, d), (1, d), (1, d), (1, d), (1, d), (1, d), (1, Q_LORA), (1, KV_LORA),
                      (1, SSD_HEADS), (1, SSD_HEADS), (1, SSD_HEADS), (1, D_CONV), (4, D_CONV), (1, N_MOD * d),
                      (nb, N_MOD * d)]
    partial = _pack_rows([R["loss"][0, :1], R["norm_ffn1"], R["norm_mix"], R["norm_ffn2"], R["norm_final"],
                          R["ssd_norm_w"], R["mla_norm_w"], R["q_norm_w"], R["kv_norm_w"],
                          R["dt_bias"], R["a_log"], R["d_skip"], R["conv_b"], R["conv_w"],
                          sum_rows(dmod, "dmod_rows"), dmod])
    (partial_g,) = all_gather8([partial], "gather_partials")
    (loss, g_nf1, g_nmix, g_nf2, g_nfin, g_ssdn, g_mlan, g_qn, g_kvn, g_dtb, g_alog, g_dskip, g_convb, g_convw,
     g_bada, _) = _unpack_rows(sum_blocks(partial_g, "sum_partials"), partial_shapes)
    dmod_row = sum(-(-math.prod(shp) // D_MODEL) for shp in partial_shapes[:-1])
    dmod_all = partial_g[:, dmod_row:dmod_row + nb * N_MOD].reshape(N_DEV * nb, N_MOD * d)
    g_wada = adaln_bwd(c_act, lax.dynamic_slice(dmod_all, (0, me * n_ada), (N_DEV * nb, n_ada)), "adaln_bwd")
    n_cw = conv_w.shape[2]
    G = {"w_ada": g_wada[None], "b_ada": g_bada, "norm_ffn1": g_nf1, "norm_mix": g_nmix, "norm_ffn2": g_nf2,
         "norm_final": g_nfin.reshape(d), "ssd_norm_w": g_ssdn, "mla_norm_w": g_mlan, "q_norm_w": g_qn,
         "kv_norm_w": g_kvn, "dt_bias": g_dtb, "a_log": g_alog, "d_skip": g_dskip, "conv_b": g_convb,
         "conv_w": lax.dynamic_slice(g_convw, (0, me * n_cw), (4, n_cw))[None]}

    DW, NM, NV = {}, {}, {}
    gw = R["gw"]
    for k, (tag, group) in enumerate(GRAD_GROUPS):
        send = [_grad_blocks(name, gw[name]).reshape(N_DEV, *_shard_view(name, W[name]).shape) for name in group]
        recv = sc_all_to_all8(send, "exchange_" + tag, 2 + k)
        for name, blocks in zip(group, recv):
            res = adamw_blocks(_shard_view(name, W[name]), blocks, _shard_view(name, M[name]), _shard_view(name, V[name]),
                               "adamw_" + name)
            G[name], DW[name], NM[name], NV[name] = [_shard_unview(name, t) for t in res]
    DW["w_ada"], NM["w_ada"], NV["w_ada"] = adamw(w_ada, g_wada, m_w_ada, v_w_ada, "adamw_w_ada")
    small = [n for n in names if n not in DW]
    as2d = lambda a: a.reshape(-1, a.shape[-1])
    outs = adamw_many([as2d(W[n]) for n in small], [as2d(G[n]) for n in small], [as2d(M[n]) for n in small],
                      [as2d(V[n]) for n in small], "adamw_small")
    for res, dst in zip(outs, (DW, NM, NV)):
        for n, t in zip(small, res):
            dst[n] = t.reshape(W[n].shape)
    return (loss.reshape(()), R["dx"], *[G[n] for n in names], *[DW[n] for n in names], *[NM[n] for n in names],
            *[NV[n] for n in names])
```

```python
import math

import jax
import jax.numpy as jnp
from jax import lax
from jax.experimental import pallas as pl
from jax.experimental.pallas import tpu as pltpu
from jax.experimental.pallas import tpu_sc as plsc

F32, BF16, I32 = jnp.float32, jnp.bfloat16, jnp.int32
HI = lax.Precision.HIGHEST
SDS = jax.ShapeDtypeStruct
MESH = pl.DeviceIdType.MESH

D_MODEL = 1024
D_FF = 2816
D_SSD = 1024
SSD_HEADS = 16
SSD_HEAD_DIM = 64
SSD_GROUPS = 2
SSD_STATE = 128
CHUNK = 128
MLA_HEADS = 8
QK_NOPE = 64
QK_ROPE = 32
QK_DIM = 96
V_HEAD = 128
Q_LORA = 384
KV_LORA = 256
ROPE_THETA = 10000.0
N_MOD = 9
EPS = 1e-6
D_CONV = 1536
D_IN = 3248
D_IN_PAD = 3328
HEAD_PAD = 128
N_DEV = 8
ADAM_LR, ADAM_B1, ADAM_B2, ADAM_EPS, ADAM_WD, ADAM_STEP = 0.001, 0.9, 0.999, 1e-08, 0.01, 10

SAVED_ACT = BF16
VMEM_LIMIT = 56 * 1024 * 1024
LANES = 128
NT_DIMS = (((1,), (1,)), ((), ()))
TN_DIMS = (((0,), (0,)), ((), ()))


def _cparams(n_axes):
    return pltpu.CompilerParams(dimension_semantics=("arbitrary",) * n_axes, vmem_limit_bytes=VMEM_LIMIT)


def _row(tm, d):
    return pl.BlockSpec((None, tm, d), lambda b, i: (b, i, 0))


def _bvec(d):
    return pl.BlockSpec((None, 1, d), lambda b, i: (b, 0, 0))


def _full(shape):
    n = len(shape)
    return pl.BlockSpec(shape, lambda *_: (0,) * n)


def _sigmoid(x):
    return 1.0 / (1.0 + jnp.exp(-x))


def _softplus(x):
    return jnp.maximum(x, 0.0) + jnp.log(1.0 + jnp.exp(-jnp.abs(x)))


def _rms(x):
    return lax.rsqrt(jnp.mean(x * x, axis=-1, keepdims=True) + EPS)


def _rms_bwd(dn, n, r):
    return r * (dn - n * jnp.mean(dn * n, axis=-1, keepdims=True))


def _first_step():
    return (pl.program_id(0) == 0) & (pl.program_id(1) == 0)


def _gather_copies(x_refs, out_refs, send_sems, recv_sems, local_sems, part=None):
    mx, my, mc = lax.axis_index("x"), lax.axis_index("y"), lax.axis_index("c")
    me, sibling = (mx, my, mc), (mx, my, 1 - mc)
    chips = [(1 - mx, my), (mx, 1 - my), (1 - mx, 1 - my)]

    def copy(a, k, block, to, src=None):
        rows = out_refs[a].at[4 * block[0] + 2 * block[1] + block[2]]
        return pltpu.make_async_remote_copy(
            src_ref=rows if src is None else src, dst_ref=rows,
            send_sem=send_sems.at[7 * a + k], recv_sem=recv_sems.at[7 * a + k], device_id=to, device_id_type=MESH)

    arrays = range(len(x_refs))
    mine = [pltpu.make_async_copy(x_refs[a], out_refs[a].at[4 * mx + 2 * my + mc], local_sems.at[a]) for a in arrays]
    first = [[copy(a, 0, me, sibling, src=x_refs[a])] + [copy(a, 1 + j, me, (*chip, mc), src=x_refs[a])
                                                          for j, chip in enumerate(chips)] for a in arrays]
    passed = [[copy(a, 4 + j, (*chip, mc), sibling) for j, chip in enumerate(chips)] for a in arrays]
    if part != "finish":
        for a in arrays:
            mine[a].start()
            for cp in first[a]:
                cp.start()
    if part == "start":
        return
    for a in arrays:
        for j, chip in enumerate(chips):
            copy(a, 1 + j, (*chip, mc), me).wait_recv()
            passed[a][j].start()
    for a in arrays:
        copy(a, 0, sibling, me).wait_recv()
        for j, chip in enumerate(chips):
            copy(a, 4 + j, (*chip, 1 - mc), me).wait_recv()
    for a in arrays:
        for cp in first[a] + passed[a]:
            cp.wait_send()
        mine[a].wait()


def _gather_peers():
    mx, my, mc = lax.axis_index("x"), lax.axis_index("y"), lax.axis_index("c")
    return [(mx, my, 1 - mc), (1 - mx, my, mc), (mx, 1 - my, mc), (1 - mx, 1 - my, mc)]


def _comm_scratch(n):
    return [pltpu.SemaphoreType.DMA((7 * n,)), pltpu.SemaphoreType.DMA((7 * n,)), pltpu.SemaphoreType.DMA((n,))]


def all_gather8(xs, name):
    n = len(xs)

    def body(*refs):
        _gather_copies(refs[:n], refs[n:2 * n], *refs[2 * n:])

    return pl.pallas_call(
        body, name=name,
        out_shape=[SDS((N_DEV, *x.shape), x.dtype) for x in xs],
        in_specs=[pl.BlockSpec(memory_space=pl.ANY)] * n,
        out_specs=[pl.BlockSpec(memory_space=pl.ANY)] * n,
        scratch_shapes=_comm_scratch(n),
    )(*xs)


def _exchange_peers():
    mx, my, mc = lax.axis_index("x"), lax.axis_index("y"), lax.axis_index("c")
    return [(1 - mx if rel & 4 else mx, 1 - my if rel & 2 else my, 1 - mc if rel & 1 else mc) for rel in range(1, N_DEV)]


def _exchange_copies(x_refs, out_refs, send_sems, recv_sems, local_sems):
    mx, my, mc = lax.axis_index("x"), lax.axis_index("y"), lax.axis_index("c")
    me = 4 * mx + 2 * my + mc
    copies = []
    for a, (x_ref, out_ref) in enumerate(zip(x_refs, out_refs)):
        mine = pltpu.make_async_copy(x_ref.at[me], out_ref.at[me], local_sems.at[a])
        mine.start()
        copies.append(mine)
        for k, (px, py, pc) in enumerate(_exchange_peers()):
            cp = pltpu.make_async_remote_copy(
                src_ref=x_ref.at[4 * px + 2 * py + pc], dst_ref=out_ref.at[me],
                send_sem=send_sems.at[7 * a + k], recv_sem=recv_sems.at[7 * a + k],
                device_id=(px, py, pc), device_id_type=MESH)
            cp.start()
            copies.append(cp)
    for cp in copies:
        cp.wait()


def all_to_all8(xs, name):
    n = len(xs)

    def body(*refs):
        _exchange_copies(refs[:n], refs[n:2 * n], *refs[2 * n:])

    return pl.pallas_call(
        body, name=name,
        out_shape=[SDS(x.shape, x.dtype) for x in xs],
        in_specs=[pl.BlockSpec(memory_space=pl.ANY)] * n,
        out_specs=[pl.BlockSpec(memory_space=pl.ANY)] * n,
        scratch_shapes=_comm_scratch(n),
    )(*xs)


def _sequencer_kernel(name, collective_id, n_arrays):
    return pl.kernel(
        mesh=plsc.ScalarSubcoreMesh(axis_name="seq", num_cores=1), name=name,
        scratch_types=tuple(_comm_scratch(n_arrays)),
        compiler_params=pltpu.CompilerParams(collective_id=collective_id))


def _handshake(peers):
    barrier = pltpu.get_barrier_semaphore()
    for peer in peers:
        pl.semaphore_signal(barrier, inc=1, device_id=peer, device_id_type=MESH)
    pl.semaphore_wait(barrier, len(peers))


def _hbm_refs(xs, out_shapes):
    x_refs = [jax.new_ref(x, memory_space=pltpu.MemorySpace.HBM) for x in xs]
    out_refs = [jax.empty_ref(SDS(shp, x.dtype), memory_space=pltpu.MemorySpace.HBM) for x, shp in zip(xs, out_shapes)]
    return x_refs, out_refs


def sc_all_gather8(xs, name, collective_id):
    x_refs, out_refs = _hbm_refs(xs, [(N_DEV, *x.shape) for x in xs])

    @_sequencer_kernel(name, collective_id, len(xs))
    def launch(send_sems, recv_sems, local_sems):
        _handshake(_gather_peers())
        _gather_copies(x_refs, out_refs, send_sems, recv_sems, local_sems)

    launch()
    return [ref[...] for ref in out_refs]


def sc_all_to_all8(xs, name, collective_id):
    x_refs, out_refs = _hbm_refs(xs, [x.shape for x in xs])

    @_sequencer_kernel(name, collective_id, len(xs))
    def launch(send_sems, recv_sems, local_sems):
        _handshake(_exchange_peers())
        _exchange_copies(x_refs, out_refs, send_sems, recv_sems, local_sems)

    launch()
    return [ref[...] for ref in out_refs]


def norm_mod(x, w, sc, sh, name, gather=()):
    b, s, d = x.shape
    tm = min(512, s)
    n = len(gather)
    last = (b - 1, s // tm - 1)

    def body(x_ref, w_ref, sc_ref, sh_ref, *refs):
        h_ref = refs[n]
        if n:
            comm = (refs[:n], refs[n + 1:2 * n + 1], *refs[2 * n + 1:])

            @pl.when(_first_step())
            def _():
                _gather_copies(*comm, part="start")
        xv = x_ref[...]
        nv = xv * _rms(xv)
        h_ref[...] = ((nv * w_ref[...]) * (1.0 + sc_ref[...]) + sh_ref[...]).astype(BF16)
        if n:
            @pl.when((pl.program_id(0) == last[0]) & (pl.program_id(1) == last[1]))
            def _():
                _gather_copies(*comm, part="finish")

    hbm = [pl.BlockSpec(memory_space=pl.ANY)] * n
    res = pl.pallas_call(
        body, name=name, grid=(b, s // tm),
        in_specs=[_row(tm, d), _full((1, d)), _bvec(d), _bvec(d)] + hbm,
        out_specs=[_row(tm, d)] + hbm,
        out_shape=[SDS((b, s, d), BF16)] + [SDS((N_DEV, *g.shape), g.dtype) for g in gather],
        scratch_shapes=_comm_scratch(n) if n else [], compiler_params=_cparams(2))(x, w, sc, sh, *gather)
    return res if n else res[0]


def ffn_up(h, wg_t, wu_t, name):
    b, s, d = h.shape
    f = wg_t.shape[0]
    tm, tn = min(1024, s), f // 2

    def body(h_ref, wg_ref, wu_ref, s_ref, t_ref, a_ref):
        hv = h_ref[...]
        g = lax.dot_general(hv, wg_ref[...], NT_DIMS, preferred_element_type=F32)
        u = lax.dot_general(hv, wu_ref[...], NT_DIMS, preferred_element_type=F32)
        sg = _sigmoid(g)
        silu = g * sg
        s_ref[...] = silu.astype(s_ref.dtype)
        t_ref[...] = (u * (sg + silu * (1.0 - sg))).astype(t_ref.dtype)
        a_ref[...] = (silu * u).astype(BF16)

    hs = pl.BlockSpec((None, tm, d), lambda j, bb, i: (bb, i, 0))
    ws = pl.BlockSpec((tn, d), lambda j, bb, i: (j, 0))
    os_ = pl.BlockSpec((None, tm, tn), lambda j, bb, i: (bb, i, j))
    return pl.pallas_call(
        body, name=name, grid=(f // tn, b, s // tm),
        in_specs=[hs, ws, ws], out_specs=[os_, os_, os_],
        out_shape=[SDS((b, s, f), SAVED_ACT), SDS((b, s, f), SAVED_ACT), SDS((b, s, f), BF16)],
        compiler_params=_cparams(3))(h, wg_t, wu_t)


def _norm_mod_tile(xv, w_ref, sc_ref, sh_ref):
    return ((xv * _rms(xv) * w_ref[...]) * (1.0 + sc_ref[...]) + sh_ref[...]).astype(BF16)


def ffn_down(a, wd, x, gate, scale, name, above=None):
    b, s, f = a.shape
    d = wd.shape[1]
    tm = min(1024, s)

    def body(a_ref, wd_ref, x_ref, g_ref, *rest):
        xn_ref, o_ref = rest[-3:-1] if above else rest
        o = jnp.dot(a_ref[...], wd_ref[...], preferred_element_type=F32)
        xn = x_ref[...] + (scale * g_ref[...]) * o
        xn_ref[...] = xn
        o_ref[...] = o.astype(BF16)
        if above:
            rest[-1][...] = _norm_mod_tile(xn, *rest[0:3])

    extra = above is not None
    return pl.pallas_call(
        body, name=name, grid=(b, s // tm),
        in_specs=[_row(tm, f), _full((f, d)), _row(tm, d), _bvec(d)] + ([_full((1, d)), _bvec(d), _bvec(d)] if extra else []),
        out_specs=[_row(tm, d), _row(tm, d)] + ([_row(tm, d)] if extra else []),
        out_shape=[SDS((b, s, d), F32), SDS((b, s, d), BF16)] + ([SDS((b, s, d), BF16)] if extra else []),
        compiler_params=_cparams(2))(a, wd, x, gate, *(above or ()))


def ffn_down_final(a, wd, x, gate, scale, w_final, tgt, name):
    b, s, f = a.shape
    d = wd.shape[1]
    tm = min(1024, s)

    def body(a_ref, wd_ref, x_ref, g_ref, w_ref, t_ref, loss_ref, dx_ref, dw_ref, do_ref, dg_ref):
        @pl.when(_first_step())
        def _():
            loss_ref[...] = jnp.zeros_like(loss_ref)
            dw_ref[...] = jnp.zeros_like(dw_ref)

        @pl.when(pl.program_id(1) == 0)
        def _():
            dg_ref[...] = jnp.zeros_like(dg_ref)
        o = jnp.dot(a_ref[...], wd_ref[...], preferred_element_type=F32)
        sg = scale * g_ref[...]
        xv = x_ref[...] + sg * o
        r = _rms(xv)
        n = xv * r
        wv = w_ref[...]
        e = n * wv - t_ref[...]
        loss_ref[...] += jnp.sum(e * e) * (0.5 / d)
        dy = e * (1.0 / d)
        dw_ref[...] += jnp.sum(dy * n, axis=0, keepdims=True)
        dx = _rms_bwd(dy * wv, n, r)
        dx_ref[...] = dx
        do_ref[...] = (sg * dx).astype(BF16)
        dg_ref[...] += jnp.sum(scale * dx * o, axis=0, keepdims=True)

    return pl.pallas_call(
        body, name=name, grid=(b, s // tm),
        in_specs=[_row(tm, f), _full((f, d)), _row(tm, d), _bvec(d), _full((1, d)), _row(tm, d)],
        out_specs=[_full((1, LANES)), _row(tm, d), _full((1, d)), _row(tm, d), _bvec(d)],
        out_shape=[SDS((1, LANES), F32), SDS((b, s, d), F32), SDS((1, d), F32), SDS((b, s, d), BF16), SDS((b, 1, d), F32)],
        compiler_params=_cparams(2))(a, wd, x, gate, w_final, tgt)


def ffn_dact(do, wd, silu_g, u_dsilu, name):
    b, s, d = do.shape
    f = wd.shape[0]
    tm, tn = min(1024, s), f // 2

    def body(do_ref, wd_ref, s_ref, t_ref, dg_ref, du_ref):
        da = lax.dot_general(do_ref[...], wd_ref[...], NT_DIMS, preferred_element_type=F32)
        dg_ref[...] = (da * t_ref[...].astype(F32)).astype(BF16)
        du_ref[...] = (da * s_ref[...].astype(F32)).astype(BF16)

    dos = pl.BlockSpec((None, tm, d), lambda j, bb, i: (bb, i, 0))
    ws = pl.BlockSpec((tn, d), lambda j, bb, i: (j, 0))
    es = pl.BlockSpec((None, tm, tn), lambda j, bb, i: (bb, i, j))
    return pl.pallas_call(
        body, name=name, grid=(f // tn, b, s // tm),
        in_specs=[dos, ws, es, es], out_specs=[es, es],
        out_shape=[SDS((b, s, f), BF16), SDS((b, s, f), BF16)], compiler_params=_cparams(3))(do, wd, silu_g, u_dsilu)


def mm_tn(a, bm, tma, tnb, name):
    b, s, ka = a.shape
    nb = bm.shape[2]
    tk = min(2048, s)
    nk = s // tk

    def body(a_ref, b_ref, o_ref, acc):
        first = (pl.program_id(2) == 0) & (pl.program_id(3) == 0)
        last = (pl.program_id(2) == b - 1) & (pl.program_id(3) == nk - 1)
        part = lax.dot_general(a_ref[...], b_ref[...], TN_DIMS, preferred_element_type=F32)

        @pl.when(first)
        def _():
            acc[...] = part

        @pl.when(jnp.logical_not(first))
        def _():
            acc[...] += part

        @pl.when(last)
        def _():
            o_ref[...] = acc[...].astype(BF16)

    return pl.pallas_call(
        body, name=name, grid=(ka // tma, nb // tnb, b, nk),
        in_specs=[pl.BlockSpec((None, tk, tma), lambda i, j, bb, k: (bb, k, i)),
                  pl.BlockSpec((None, tk, tnb), lambda i, j, bb, k: (bb, k, j))],
        out_specs=pl.BlockSpec((tma, tnb), lambda i, j, bb, k: (i, j)),
        out_shape=SDS((ka, nb), BF16), scratch_shapes=[pltpu.VMEM((tma, tnb), F32)],
        compiler_params=_cparams(4))(a, bm)


def mm_tn_blocks(a_blocks, bm, name, out_rows=None):
    b, s, nb = bm.shape
    widths = [a.shape[2] for a in a_blocks]
    starts = [sum(widths[:k]) for k in range(len(widths))]
    n_out = sum(seg[2] for seg in out_rows) if out_rows else sum(widths)
    per = n_out // N_DEV
    pieces = []
    for dst, src, rows in out_rows or ():
        while rows:
            n = min(rows, per - dst % per)
            pieces.append((dst // per, dst % per, src, n))
            dst, src, rows = dst + n, src + n, rows - n
    tk = min(2048 if sum(widths) <= 2048 else 1024, s)
    nk = s // tk
    n = len(a_blocks)

    def body(*refs):
        a_refs, b_ref, o_ref, acc = refs[:n], refs[n], refs[n + 1], refs[n + 2]
        first = (pl.program_id(0) == 0) & (pl.program_id(1) == 0)
        last = (pl.program_id(0) == b - 1) & (pl.program_id(1) == nk - 1)

        @pl.when(first)
        def _():
            acc[...] = jnp.zeros_like(acc)
        bv = b_ref[...]
        for a_ref, st, wd in zip(a_refs, starts, widths):
            acc[st:st + wd, :] += lax.dot_general(a_ref[...], bv, TN_DIMS, preferred_element_type=F32)

        @pl.when(last)
        def _():
            if not out_rows:
                o_ref[...] = acc[...].astype(BF16)
            for blk, dst, src, rows in pieces:
                o_ref[blk, dst:dst + rows, :] = acc[src:src + rows, :].astype(BF16)

    out_shape = (N_DEV, per, nb) if out_rows else (n_out, nb)
    return pl.pallas_call(
        body, name=name, grid=(b, nk),
        in_specs=[_row(tk, wd) for wd in widths] + [_row(tk, nb)],
        out_specs=_full(out_shape), out_shape=SDS(out_shape, BF16),
        scratch_shapes=[pltpu.VMEM((sum(widths), nb), F32)], compiler_params=_cparams(2))(*a_blocks, bm)


def _gate_bwd_specs(tm, d, b, s):
    return ([_row(tm, d), _bvec(d)], [_row(tm, d), _bvec(d)], [SDS((b, s, d), BF16), SDS((b, 1, d), F32)])


def _gate_bwd_tile(dx, scale, o_ref, g_ref, do_ref, dg_ref):
    do_ref[...] = ((scale * g_ref[...]) * dx).astype(BF16)
    dg_ref[...] += jnp.sum(scale * dx * o_ref[...].astype(F32), axis=0, keepdims=True)


def dh_norm_bwd(dys, wts, x, dxn, w, sc, name, below=None):
    b, s, d = x.shape
    tm = min(512, s)
    n_in, n_w = len(dys), len(wts)
    extra_in, extra_out, extra_shape = _gate_bwd_specs(tm, d, b, s) if below else ([], [], [])
    starts = [sum(dy.shape[2] for dy in dys[:k]) for k in range(n_in)]

    def body(*refs):
        dy_refs, w_refs = refs[:n_in], refs[n_in:n_in + n_w]
        x_ref, dxn_ref, nw_ref, sc_ref = refs[n_in + n_w:n_in + n_w + 4]
        rest = refs[n_in + n_w + 4:]
        if below:
            o_ref, g_ref, dx_ref, dsc_ref, dsh_ref, dw_ref, do_ref, dg_ref = rest
        else:
            dx_ref, dsc_ref, dsh_ref, dw_ref = rest

        @pl.when(pl.program_id(1) == 0)
        def _():
            dsc_ref[...] = jnp.zeros_like(dsc_ref)
            dsh_ref[...] = jnp.zeros_like(dsh_ref)
            if below:
                dg_ref[...] = jnp.zeros_like(dg_ref)

        @pl.when(_first_step())
        def _():
            dw_ref[...] = jnp.zeros_like(dw_ref)

        def weight(k):
            return w_refs[k][...] if n_w == n_in else w_refs[0][starts[k]:starts[k] + dys[k].shape[2], :]

        dh = jnp.dot(dy_refs[0][...], weight(0), preferred_element_type=F32)
        for k in range(1, n_in):
            dh += jnp.dot(dy_refs[k][...], weight(k), preferred_element_type=F32)
        xv = x_ref[...]
        r = _rms(xv)
        n = xv * r
        nw = nw_ref[...]
        dsc_ref[...] += jnp.sum(dh * (n * nw), axis=0, keepdims=True)
        dsh_ref[...] += jnp.sum(dh, axis=0, keepdims=True)
        dhn = dh * (1.0 + sc_ref[...])
        dw_ref[...] += jnp.sum(dhn * n, axis=0, keepdims=True)
        dx = dxn_ref[...] + _rms_bwd(dhn * nw, n, r)
        dx_ref[...] = dx
        if below:
            _gate_bwd_tile(dx, below[2], o_ref, g_ref, do_ref, dg_ref)

    resident = lambda shape: pl.BlockSpec(shape, lambda *_: (0,) * len(shape), pipeline_mode=pl.Buffered(1))
    in_specs = [_row(tm, dy.shape[2]) for dy in dys] + [resident(wt.shape) for wt in wts]
    in_specs += [_row(tm, d), _row(tm, d), _full((1, d)), _bvec(d)] + extra_in
    return pl.pallas_call(
        body, name=name, grid=(b, s // tm), in_specs=in_specs,
        out_specs=[_row(tm, d), _bvec(d), _bvec(d), _full((1, d))] + extra_out,
        out_shape=[SDS((b, s, d), F32), SDS((b, 1, d), F32), SDS((b, 1, d), F32), SDS((1, d), F32)] + extra_shape,
        compiler_params=_cparams(2))(*dys, *wts, x, dxn, w, sc, *(below[:2] if below else ()))


def in_proj(h, win_t, name):
    b, s, d = h.shape
    tm = min(512, s)
    widths = (D_SSD, D_SSD + 2 * SSD_GROUPS * SSD_STATE, Q_LORA, KV_LORA, LANES)

    def body(h_ref, w_ref, *outs):
        p = lax.dot_general(h_ref[...], w_ref[...], NT_DIMS, preferred_element_type=F32)
        off = 0
        for o_ref, wd in zip(outs, widths):
            o_ref[...] = p[:, off:off + wd]
            off += wd

    return pl.pallas_call(
        body, name=name, grid=(b, s // tm),
        in_specs=[_row(tm, d), _full(win_t.shape)],
        out_specs=[_row(tm, wd) for wd in widths],
        out_shape=[SDS((b, s, wd), F32) for wd in widths], compiler_params=_cparams(2))(h, win_t)


def _halo_prev(ts, d):
    return pl.BlockSpec((None, 8, d), lambda b, i: (b, jnp.maximum(i * (ts // 8) - 1, 0), 0))


CONV_ROWS = 32


def _conv_head(head, u_ref, up_ref, tile):
    head[0:8, :] = jnp.where(tile > 0, up_ref[...], 0.0)
    head[8:8 + CONV_ROWS, :] = u_ref[0:CONV_ROWS, :]


def _conv_windows(u_ref, head, r0):
    if r0 == 0:
        return [head[5 + k:5 + k + CONV_ROWS, :] for k in range(4)]
    return [u_ref[r0 - 3 + k:r0 - 3 + k + CONV_ROWS, :] for k in range(4)]


def _fold8(t):
    acc = t[0:8, :]
    for r in range(8, CONV_ROWS, 8):
        acc += t[r:r + 8, :]
    return acc


def conv_fwd(u, cw, cb, name):
    b, s, dc = u.shape
    ts = min(512, s)
    widths = (D_SSD, SSD_GROUPS * SSD_STATE, SSD_GROUPS * SSD_STATE)

    def body(u_ref, up_ref, w_ref, b_ref, xs_ref, bm_ref, cm_ref, head):
        _conv_head(head, u_ref, up_ref, pl.program_id(1))
        ws = [w_ref[k:k + 1, :] for k in range(4)]
        bias = b_ref[...]
        for r0 in range(0, ts, CONV_ROWS):
            taps = _conv_windows(u_ref, head, r0)
            v = bias + taps[0] * ws[0] + taps[1] * ws[1] + taps[2] * ws[2] + taps[3] * ws[3]
            y = v * _sigmoid(v)
            rs = slice(r0, r0 + CONV_ROWS)
            xs_ref[rs, :] = y[:, 0:D_SSD]
            bm_ref[rs, :] = y[:, D_SSD:D_SSD + 256]
            cm_ref[rs, :] = y[:, D_SSD + 256:D_SSD + 512]

    return pl.pallas_call(
        body, name=name, grid=(b, s // ts),
        in_specs=[_row(ts, dc), _halo_prev(ts, dc), _full((4, dc)), _full((1, dc))],
        out_specs=[_row(ts, wd) for wd in widths],
        out_shape=[SDS((b, s, wd), F32) for wd in widths],
        scratch_shapes=[pltpu.VMEM((8 + CONV_ROWS, dc), F32)], compiler_params=_cparams(2))(u, u, cw, cb)


def conv_bwd(dxs, dbm, dcm, u, cw, cb, name):
    b, s, dc = u.shape
    ts = min(512, s)
    nt = s // ts

    def body(dxs_ref, dbm_ref, dcm_ref, u_ref, up_ref, w_ref, b_ref, du_ref, dwb_ref, head, dvs):
        @pl.when(_first_step())
        def _():
            dwb_ref[...] = jnp.zeros_like(dwb_ref)

        @pl.when(pl.program_id(1) == 0)
        def _():
            dvs[ts:ts + 8, :] = jnp.zeros((8, dc), F32)
        _conv_head(head, u_ref, up_ref, nt - 1 - pl.program_id(1))
        ws = [w_ref[k:k + 1, :] for k in range(4)]
        bias = b_ref[...]
        for r0 in range(0, ts, CONV_ROWS):
            taps = _conv_windows(u_ref, head, r0)
            v = bias + taps[0] * ws[0] + taps[1] * ws[1] + taps[2] * ws[2] + taps[3] * ws[3]
            sg = _sigmoid(v)
            rs = slice(r0, r0 + CONV_ROWS)
            dy = jnp.concatenate([dxs_ref[rs, :], dbm_ref[rs, :], dcm_ref[rs, :]], axis=1)
            dv = dy * (sg * (1.0 + v * (1.0 - sg)))
            dvs[rs, :] = dv
            for k in range(4):
                dwb_ref[8 * k:8 * k + 8, :] += _fold8(dv * taps[k])
            dwb_ref[32:40, :] += _fold8(dv)
        for r0 in range(0, ts, CONV_ROWS):
            win = [dvs[r0 + 3 - k:r0 + 3 - k + CONV_ROWS, :] for k in range(4)]
            acc = win[0] * ws[0] + win[1] * ws[1] + win[2] * ws[2] + win[3] * ws[3]
            du_ref[r0:r0 + CONV_ROWS, :] = acc.astype(BF16)
        dvs[ts:ts + 8, :] = dvs[0:8, :]

    rows = lambda wd: pl.BlockSpec((None, ts, wd), lambda bb, i: (bb, nt - 1 - i, 0))
    prev = pl.BlockSpec((None, 8, dc), lambda bb, i: (bb, jnp.maximum((nt - 1 - i) * (ts // 8) - 1, 0), 0))
    return pl.pallas_call(
        body, name=name, grid=(b, nt),
        in_specs=[rows(D_SSD), rows(256), rows(256), rows(dc), prev, _full((4, dc)), _full((1, dc))],
        out_specs=[rows(dc), _full((40, dc))],
        out_shape=[SDS((b, s, dc), BF16), SDS((40, dc), F32)],
        scratch_shapes=[pltpu.VMEM((8 + CONV_ROWS, dc), F32), pltpu.VMEM((ts + 8, dc), F32)],
        compiler_params=_cparams(2))(dxs, dbm, dcm, u, u, cw, cb)


def conv_grads_fold(x, name):
    c = x.shape[1]

    def body(x_ref, o_ref):
        o_ref[...] = jnp.zeros_like(o_ref)
        for k in range(5):
            o_ref[k:k + 1, :] = jnp.sum(x_ref[8 * k:8 * k + 8, :], axis=0, keepdims=True)

    return pl.pallas_call(body, name=name, out_shape=SDS((8, c), F32))(x)


def _ssd_common(misc_ref, dtb_ref, alog_ref, e_ref):
    ln = CHUNK
    lane = lax.broadcasted_iota(I32, (ln, LANES), 1)
    lane1 = lax.broadcasted_iota(I32, (1, LANES), 1)
    pre = misc_ref[...] + dtb_ref[...]
    dt_s = jnp.where(lane < SSD_HEADS, _softplus(pre), 0.0)
    a_neg = jnp.where(lane1 < SSD_HEADS, -jnp.exp(alog_ref[...]), 0.0)
    ri = lax.broadcasted_iota(I32, (ln, ln), 0)
    ci = lax.broadcasted_iota(I32, (ln, ln), 1)
    tril = ci <= ri
    acum = jnp.dot(tril.astype(F32), dt_s * a_neg, preferred_element_type=F32, precision=HI)
    both_e = _dot_01(jnp.concatenate([dt_s, acum], axis=0), e_ref[...], 3)
    dt_e, acum_e = both_e[0:ln], both_e[ln:2 * ln]
    return dict(pre=pre, dt_s=dt_s, a_neg=a_neg, tril=tril, ri=ri, ci=ci, acum=acum, acum_t=acum.T,
                dt_e=dt_e, eac_e=jnp.exp(acum_e), del_e=jnp.exp(acum_e[ln - 1:ln, :] - acum_e))


def _dot_01(x, m01, terms, dims=(((1,), (0,)), ((), ()))):
    acc, rest = None, x
    for k in range(terms):
        part = rest.astype(BF16)
        if k + 1 < terms:
            rest = rest - part.astype(F32)
        d = lax.dot_general(part, m01, dims, preferred_element_type=F32)
        acc = d if acc is None else acc + d
    return acc


def _decay(cm, h):
    seg = cm["acum"][:, h:h + 1] - cm["acum_t"][h:h + 1, :]
    return jnp.exp(jnp.where(cm["tril"], seg, -jnp.inf))


def ssd_fwd(xs, bm, cm_, misc, z, dtb, alog, dskip_e, norm_w, e_mat, name):
    b, s, _ = xs.shape
    ln, nc = CHUNK, s // CHUNK
    gw = D_SSD // SSD_GROUPS
    hpg = SSD_HEADS // SSD_GROUPS

    def body(xs_ref, b_ref, c_ref, misc_ref, z_ref, dtb_ref, alog_ref, dsk_ref, nw_ref, e_ref,
             ys_ref, y_ref, p_ref, st, yd):
        @pl.when(pl.program_id(1) == 0)
        def _():
            st[...] = jnp.zeros_like(st)
        cm = _ssd_common(misc_ref, dtb_ref, alog_ref, e_ref)
        xsv = xs_ref[...]
        xdt = xsv * cm["dt_e"]
        xdt_b = xdt.astype(BF16)
        xd_b = (xdt * cm["del_e"]).astype(BF16)
        gam_e = cm["eac_e"][ln - 1:ln, :]
        p_ref[...] = st[...]
        groups = [slice(gw * g, gw * (g + 1)) for g in range(SSD_GROUPS)]
        heads = [slice(SSD_HEAD_DIM * h, SSD_HEAD_DIM * (h + 1)) for h in range(SSD_HEADS)]
        bgs = [b_ref[:, SSD_STATE * g:SSD_STATE * (g + 1)].astype(BF16) for g in range(SSD_GROUPS)]
        cgs = [c_ref[:, SSD_STATE * g:SSD_STATE * (g + 1)].astype(BF16) for g in range(SSD_GROUPS)]
        cbs = [lax.dot_general(cg, bg, NT_DIMS, preferred_element_type=F32) for cg, bg in zip(cgs, bgs)]
        sts = [st[:, gs] for gs in groups]
        yoff = [jnp.dot(cg, st_g.astype(BF16), preferred_element_type=F32) * cm["eac_e"][:, gs]
                for cg, st_g, gs in zip(cgs, sts, groups)]
        news = [lax.dot_general(bg, xd_b[:, gs], TN_DIMS, preferred_element_type=F32) for bg, gs in zip(bgs, groups)]
        for gs, st_g, new in zip(groups, sts, news):
            st[:, gs] = st_g * gam_e[:, gs] + new
        ms = [(cbs[h // hpg] * _decay(cm, h)).astype(BF16) for h in range(SSD_HEADS)]
        for h, hs in enumerate(heads):
            yd[:, hs] = jnp.dot(ms[h], xdt_b[:, hs], preferred_element_type=F32)
        y = yd[...] + jnp.concatenate(yoff, axis=1) + dsk_ref[...] * xsv
        y_ref[...] = y
        zz = z_ref[...]
        yg = y * (zz * _sigmoid(zz))
        outs = []
        for g in range(SSD_GROUPS):
            ygg = yg[:, gw * g:gw * (g + 1)]
            outs.append(ygg * _rms(ygg) * nw_ref[:, gw * g:gw * (g + 1)])
        ys_ref[...] = jnp.concatenate(outs, axis=1).astype(BF16)

    row = lambda d: pl.BlockSpec((None, ln, d), lambda bb, c: (bb, c, 0))
    return pl.pallas_call(
        body, name=name, grid=(b, nc),
        in_specs=[row(D_SSD), row(256), row(256), row(LANES), row(D_SSD), _full((1, LANES)), _full((1, LANES)),
                  _full((1, D_SSD)), _full((1, D_SSD)), _full((LANES, D_SSD))],
        out_specs=[row(D_SSD), row(D_SSD), pl.BlockSpec((None, None, SSD_STATE, D_SSD), lambda bb, c: (bb, c, 0, 0))],
        out_shape=[SDS((b, s, D_SSD), BF16), SDS((b, s, D_SSD), F32), SDS((b, nc, SSD_STATE, D_SSD), F32)],
        scratch_shapes=[pltpu.VMEM((SSD_STATE, D_SSD), F32), pltpu.VMEM((ln, D_SSD), F32)],
        compiler_params=_cparams(2))(xs, bm, cm_, misc, z, dtb, alog, dskip_e, norm_w, e_mat)


def ssd_bwd(dys, y, z, xs, bm, cm_, misc, prev, dtb, alog, dskip_e, norm_w, e_mat, et_mat, name):
    b, s, _ = xs.shape
    ln, nc = CHUNK, s // CHUNK
    gw = D_SSD // SSD_GROUPS
    hpg = SSD_HEADS // SSD_GROUPS

    def body(dys_ref, y_ref, z_ref, xs_ref, b_ref, c_ref, misc_ref, p_ref, dtb_ref, alog_ref, dsk_ref, nw_ref,
             e_ref, et_ref, dxs_ref, db_ref, dc_ref, dz_ref, ddt_ref, dnw_ref, ddsk_ref, ddtb_ref, dalog_ref,
             dst, dxd, dac_t):
        @pl.when(_first_step())
        def _():
            for r_ in (dnw_ref, ddsk_ref, ddtb_ref, dalog_ref):
                r_[...] = jnp.zeros_like(r_)

        @pl.when(pl.program_id(1) == 0)
        def _():
            dst[...] = jnp.zeros_like(dst)

        cm = _ssd_common(misc_ref, dtb_ref, alog_ref, e_ref)
        et = et_ref[...]
        squeeze = lambda t: _dot_01(t, et, 2)
        lane = lax.broadcasted_iota(I32, (ln, LANES), 1)
        sub = lax.broadcasted_iota(I32, (LANES, ln), 0)
        xsv = xs_ref[...]
        xdt = xsv * cm["dt_e"]
        xdt_b = xdt.astype(BF16)
        xd_b = (xdt * cm["del_e"]).astype(BF16)
        eac_e = cm["eac_e"]
        gam_e = eac_e[ln - 1:ln, :]

        yv, zz, dyo = y_ref[...], z_ref[...], dys_ref[...]
        sz = _sigmoid(zz)
        silu_z = zz * sz
        yg = yv * silu_z
        dyg, dnw = [], []
        for g in range(SSD_GROUPS):
            gs = slice(gw * g, gw * (g + 1))
            ygg = yg[:, gs]
            r = _rms(ygg)
            n = ygg * r
            dnw.append(jnp.sum(dyo[:, gs] * n, axis=0, keepdims=True))
            dyg.append(_rms_bwd(dyo[:, gs] * nw_ref[:, gs], n, r))
        dyg = jnp.concatenate(dyg, axis=1)
        dnw_ref[...] += jnp.concatenate(dnw, axis=1)
        dz_ref[...] = (dyg * yv * (sz * (1.0 + zz * (1.0 - sz)))).astype(BF16)
        dy = dyg * silu_z
        ddsk_ref[...] += jnp.sum(dy * xsv, axis=0, keepdims=True)
        dy_b = dy.astype(BF16)

        dacum = jnp.zeros((ln, LANES), F32)
        dac_t[...] = jnp.zeros_like(dac_t)
        w1, dgam = [], []
        for g in range(SSD_GROUPS):
            gs = slice(gw * g, gw * (g + 1))
            ss = slice(SSD_STATE * g, SSD_STATE * (g + 1))
            bg = b_ref[:, ss].astype(BF16)
            cg = c_ref[:, ss].astype(BF16)
            cb = lax.dot_general(cg, bg, NT_DIMS, preferred_element_type=F32)
            pt = p_ref[:, gs]
            pt_b = pt.astype(BF16)
            dst_g = dst[:, gs]
            dst_b = dst_g.astype(BF16)
            edy = (dy[:, gs] * eac_e[:, gs]).astype(BF16)
            dcg = lax.dot_general(edy, pt_b, NT_DIMS, preferred_element_type=F32)
            dpt = lax.dot_general(cg, edy, TN_DIMS, preferred_element_type=F32)
            yoff = jnp.dot(cg, pt_b, preferred_element_type=F32) * eac_e[:, gs]
            dxd_g = jnp.dot(bg, dst_b, preferred_element_type=F32)
            dbg = lax.dot_general(xd_b[:, gs], dst_b, NT_DIMS, preferred_element_type=F32)
            ddel = dxd_g * xdt[:, gs] * cm["del_e"][:, gs]
            w1.append(dy[:, gs] * yoff - ddel)
            dgam.append(jnp.sum(ddel, axis=0, keepdims=True) + jnp.sum(dst_g * pt, axis=0, keepdims=True) * gam_e[:, gs])
            dxd[:, gs] = dxd_g * cm["del_e"][:, gs]
            dst[:, gs] = dst_g * gam_e[:, gs] + dpt
            dcb = jnp.zeros((ln, ln), F32)
            for j in range(hpg):
                h = hpg * g + j
                hs = slice(SSD_HEAD_DIM * h, SSD_HEAD_DIM * (h + 1))
                lam = _decay(cm, h)
                m = cb * lam
                dm = lax.dot_general(dy_b[:, hs], xdt_b[:, hs], NT_DIMS, preferred_element_type=F32)
                dxd[:, hs] += lax.dot_general(m.astype(BF16), dy_b[:, hs], TN_DIMS, preferred_element_type=F32)
                dcb += dm * lam
                wl = dm * m
                dacum += jnp.where(lane == h, jnp.sum(wl, axis=1, keepdims=True), 0.0)
                dac_t[...] -= jnp.where(sub == h, jnp.sum(wl, axis=0, keepdims=True), 0.0)
            dcb_b = dcb.astype(BF16)
            dc_ref[:, ss] = dcg + jnp.dot(dcb_b, bg, preferred_element_type=F32)
            db_ref[:, ss] = dbg + lax.dot_general(dcb_b, cg, TN_DIMS, preferred_element_type=F32)

        dxdt = dxd[...]
        dxs_ref[...] = dy * dsk_ref[...] + dxdt * cm["dt_e"]
        dacum += squeeze(jnp.concatenate(w1, axis=1)) + dac_t[...].T
        dlast = squeeze(jnp.broadcast_to(jnp.concatenate(dgam, axis=1), (8, D_SSD)))[0:1, :]
        dacum += jnp.where(lax.broadcasted_iota(I32, (ln, LANES), 0) == ln - 1, dlast, 0.0)
        triu = (cm["ci"] >= cm["ri"]).astype(F32)
        da = jnp.dot(triu, dacum, preferred_element_type=F32, precision=HI)
        ddt = da * cm["a_neg"] + squeeze(dxdt * xsv)
        dalog_ref[...] += jnp.sum(da * cm["dt_s"], axis=0, keepdims=True) * cm["a_neg"]
        ddt_raw = jnp.where(lane < SSD_HEADS, ddt * _sigmoid(cm["pre"]), 0.0)
        ddt_ref[...] = ddt_raw
        ddtb_ref[...] += jnp.sum(ddt_raw, axis=0, keepdims=True)

    row = lambda d: pl.BlockSpec((None, ln, d), lambda bb, c: (bb, nc - 1 - c, 0))
    return pl.pallas_call(
        body, name=name, grid=(b, nc),
        in_specs=[row(D_SSD), row(D_SSD), row(D_SSD), row(D_SSD), row(256), row(256), row(LANES),
                  pl.BlockSpec((None, None, SSD_STATE, D_SSD), lambda bb, c: (bb, nc - 1 - c, 0, 0)),
                  _full((1, LANES)), _full((1, LANES)), _full((1, D_SSD)), _full((1, D_SSD)),
                  _full((LANES, D_SSD)), _full((D_SSD, LANES))],
        out_specs=[row(D_SSD), row(256), row(256), row(D_SSD), row(LANES),
                   _full((1, D_SSD)), _full((1, D_SSD)), _full((1, LANES)), _full((1, LANES))],
        out_shape=[SDS((b, s, D_SSD), F32), SDS((b, s, 256), F32), SDS((b, s, 256), F32), SDS((b, s, D_SSD), BF16),
                   SDS((b, s, LANES), F32), SDS((1, D_SSD), F32), SDS((1, D_SSD), F32), SDS((1, LANES), F32),
                   SDS((1, LANES), F32)],
        scratch_shapes=[pltpu.VMEM((SSD_STATE, D_SSD), F32), pltpu.VMEM((ln, D_SSD), F32), pltpu.VMEM((LANES, ln), F32)],
        compiler_params=_cparams(2))(dys, y, z, xs, bm, cm_, misc, prev, dtb, alog, dskip_e, norm_w, e_mat, et_mat)


def _rope(xv, cc, sp, sm):
    n = xv.shape[1]
    return xv * cc + pltpu.roll(xv, 16, 1) * sp + pltpu.roll(xv, n - 16, 1) * sm


def _rope_bwd(dy, cc, sp, sm):
    n = dy.shape[1]
    return dy * cc + pltpu.roll(dy * sp, n - 16, 1) + pltpu.roll(dy * sm, 16, 1)


def _tile8(t):
    return jnp.concatenate([t] * MLA_HEADS, axis=1)


def qkv_fwd(cq, ckv, misc, cc, sp, sm, qnw, kvnw, wuq_t, wukv_t, place, name):
    b, s, _ = cq.shape
    tm = _att_tile(s)
    hd = MLA_HEADS * HEAD_PAD

    def body(cq_ref, ckv_ref, misc_ref, cc_ref, sp_ref, sm_ref, qnw_ref, kvnw_ref, wq_ref, wkv_ref, pl_ref,
             q_ref, k_ref, v_ref, vt_ref, qn_ref, kvn_ref):
        cqv, ckvv = cq_ref[...], ckv_ref[...]
        qn = (cqv * _rms(cqv) * qnw_ref[...]).astype(BF16)
        kvn = (ckvv * _rms(ckvv) * kvnw_ref[...]).astype(BF16)
        qn_ref[...] = qn
        kvn_ref[...] = kvn
        cc1, sp1, sm1 = cc_ref[...], sp_ref[...], sm_ref[...]
        q = lax.dot_general(qn, wq_ref[...], NT_DIMS, preferred_element_type=F32)
        q_ref[...] = _rope(q, _tile8(cc1), _tile8(sp1), _tile8(sm1)).astype(BF16)
        kv = lax.dot_general(kvn, wkv_ref[...], NT_DIMS, preferred_element_type=F32)
        kr = jnp.dot(misc_ref[...], pl_ref[...], preferred_element_type=F32, precision=HI)
        kr = _rope(kr, cc1, sp1, sm1)
        k_ref[...] = (kv[:, 0:hd] + _tile8(kr)).astype(BF16)
        v_ref[...] = kv[:, hd:2 * hd].astype(BF16)
        for h in range(MLA_HEADS):
            vt_ref[h] = kv[:, hd + HEAD_PAD * h:hd + HEAD_PAD * (h + 1)].T.astype(BF16)

    return pl.pallas_call(
        body, name=name, grid=(b, s // tm),
        in_specs=[_row(tm, Q_LORA), _row(tm, KV_LORA), _row(tm, LANES), _row(tm, LANES), _row(tm, LANES), _row(tm, LANES),
                  _full((1, Q_LORA)), _full((1, KV_LORA)), _full(wuq_t.shape), _full(wukv_t.shape), _full((LANES, LANES))],
        out_specs=[_row(tm, hd), _row(tm, hd), _row(tm, hd),
                   pl.BlockSpec((None, MLA_HEADS, None, HEAD_PAD, tm), lambda bb, i: (bb, 0, i, 0, 0)),
                   _row(tm, Q_LORA), _row(tm, KV_LORA)],
        out_shape=[SDS((b, s, hd), BF16)] * 3 + [SDS((b, MLA_HEADS, s // tm, HEAD_PAD, tm), BF16),
                                                 SDS((b, s, Q_LORA), BF16), SDS((b, s, KV_LORA), BF16)],
        compiler_params=_cparams(2))(cq, ckv, misc, cc, sp, sm, qnw, kvnw, wuq_t, wukv_t, place)


def qkv_bwd(dq, dk, dv, ddt, cq, ckv, cc, sp, sm, qnw, kvnw, wuq_t, wukv_t, place_t, name):
    b, s, _ = cq.shape
    tm = min(512, s)
    hd = MLA_HEADS * HEAD_PAD

    def body(dq_ref, dk_ref, dv_ref, ddt_ref, cq_ref, ckv_ref, cc_ref, sp_ref, sm_ref, qnw_ref, kvnw_ref,
             wq_ref, wkv_ref, plt_ref, dcq_ref, dckv_ref, dmisc_ref, dqp_ref, dkv_ref, dqnw_ref, dkvnw_ref):
        @pl.when(_first_step())
        def _():
            dqnw_ref[...] = jnp.zeros_like(dqnw_ref)
            dkvnw_ref[...] = jnp.zeros_like(dkvnw_ref)
        cc1, sp1, sm1 = cc_ref[...], sp_ref[...], sm_ref[...]
        dqp = _rope_bwd(dq_ref[...].astype(F32), _tile8(cc1), _tile8(sp1), _tile8(sm1)).astype(BF16)
        dqp_ref[...] = dqp
        dkv_b = jnp.concatenate([dk_ref[...], dv_ref[...]], axis=1)
        dkf = dk_ref[...].astype(F32)
        dkv_ref[...] = dkv_b
        dkr = dkf[:, 0:HEAD_PAD]
        for h in range(1, MLA_HEADS):
            dkr += dkf[:, HEAD_PAD * h:HEAD_PAD * (h + 1)]
        dkr = _rope_bwd(dkr, cc1, sp1, sm1)
        dmisc_ref[...] = (jnp.dot(dkr, plt_ref[...], preferred_element_type=F32, precision=HI) + ddt_ref[...]).astype(BF16)

        def norm_bwd(dn_w, xv, w_ref, dw_ref, dx_ref):
            r = _rms(xv)
            n = xv * r
            dw_ref[...] += jnp.sum(dn_w * n, axis=0, keepdims=True)
            dx_ref[...] = _rms_bwd(dn_w * w_ref[...], n, r).astype(BF16)

        norm_bwd(jnp.dot(dqp, wq_ref[...], preferred_element_type=F32), cq_ref[...], qnw_ref, dqnw_ref, dcq_ref)
        norm_bwd(jnp.dot(dkv_b, wkv_ref[...], preferred_element_type=F32), ckv_ref[...], kvnw_ref, dkvnw_ref, dckv_ref)

    return pl.pallas_call(
        body, name=name, grid=(b, s // tm),
        in_specs=[_row(tm, hd), _row(tm, hd), _row(tm, hd), _row(tm, LANES), _row(tm, Q_LORA), _row(tm, KV_LORA),
                  _row(tm, LANES), _row(tm, LANES), _row(tm, LANES), _full((1, Q_LORA)), _full((1, KV_LORA)),
                  _full(wuq_t.shape), _full(wukv_t.shape), _full((LANES, LANES))],
        out_specs=[_row(tm, Q_LORA), _row(tm, KV_LORA), _row(tm, LANES), _row(tm, hd), _row(tm, 2 * hd),
                   _full((1, Q_LORA)), _full((1, KV_LORA))],
        out_shape=[SDS((b, s, Q_LORA), BF16), SDS((b, s, KV_LORA), BF16), SDS((b, s, LANES), BF16),
                   SDS((b, s, hd), BF16), SDS((b, s, 2 * hd), BF16), SDS((1, Q_LORA), F32), SDS((1, KV_LORA), F32)],
        compiler_params=_cparams(2))(dq, dk, dv, ddt, cq, ckv, cc, sp, sm, qnw, kvnw, wuq_t, wukv_t, place_t)


ATT_SCALE = 1.0 / math.sqrt(QK_DIM)
LOG2E = math.log2(math.e)
ATT_SCALE_LOG2E = ATT_SCALE * LOG2E


ATT_HEADS_PER_STEP = 4
ATT_HEADS_PER_STEP_BWD = 2


def _att_tile(s):
    return min(512, s)


def flash_fwd(q, k, vt, name):
    b, s, hd = q.shape
    t = _att_tile(s)
    nb = s // t
    th = t // 2

    hps = ATT_HEADS_PER_STEP
    hw = hps * HEAD_PAD

    def body(q_ref, k_ref, vt_ref, o_ref, lse_ref, m_s, l_s, acc):
        i = pl.program_id(2)
        m_s[...] = jnp.full_like(m_s, -jnp.inf)
        l_s[...] = jnp.zeros_like(l_s)
        acc[...] = jnp.zeros_like(acc)

        def update(j, diagonal):
            chains = [(hh, half) for hh in range(hps) for half in range(2)]
            lanes = lambda hh: slice(HEAD_PAD * hh, HEAD_PAD * (hh + 1))
            cols = lambda half: slice(th * half, th * (half + 1))
            sts = {}
            nkeys = lambda half: th if diagonal and half == 0 else t
            for hh, half in chains:
                kr = pl.ds(pl.multiple_of(j * t, t), nkeys(half))
                st = lax.dot_general(k_ref[kr, lanes(hh)], q_ref[cols(half), lanes(hh)], NT_DIMS,
                                     preferred_element_type=F32)
                if diagonal:
                    row = lax.broadcasted_iota(I32, (nkeys(half), th), 0)
                    col = lax.broadcasted_iota(I32, (nkeys(half), th), 1) + th * half
                    st = jnp.where(row <= col, st, -jnp.inf)
                sts[hh, half] = st
            pts, alphas = {}, {}
            for hh, half in chains:
                st, cs = sts[hh, half], cols(half)
                m_prev = m_s[hh, :, cs]
                m_new = jnp.maximum(m_prev, jnp.max(st, axis=0, keepdims=True))
                alpha = jnp.exp2((m_prev - m_new) * ATT_SCALE_LOG2E)
                pt = jnp.exp2((st - m_new) * ATT_SCALE_LOG2E)
                l_s[hh, :, cs] = alpha * l_s[hh, :, cs] + jnp.sum(pt, axis=0, keepdims=True)
                m_s[hh, :, cs] = m_new
                pts[hh, half], alphas[hh, half] = pt.astype(BF16), alpha
            for hh, half in chains:
                cs = cols(half)
                acc[hh, :, cs] = alphas[hh, half] * acc[hh, :, cs] + jnp.dot(
                    vt_ref[hh, j, :, 0:nkeys(half)], pts[hh, half], preferred_element_type=F32)

        def step(j, carry):
            update(j, False)
            return carry

        lax.fori_loop(0, i, step, 0)
        update(i, True)
        for hh in range(hps):
            o_ref[:, HEAD_PAD * hh:HEAD_PAD * (hh + 1)] = (acc[hh] / l_s[hh]).T
            lse_ref[hh] = m_s[hh] * ATT_SCALE + jnp.log(l_s[hh])

    qs = pl.BlockSpec((None, t, hw), lambda bb, h, i: (bb, i, h))
    ks = pl.BlockSpec((None, s, hw), lambda bb, h, i: (bb, 0, h))
    vs = pl.BlockSpec((None, hps, nb, HEAD_PAD, t), lambda bb, h, i: (bb, h, 0, 0, 0))
    ls = pl.BlockSpec((None, hps, None, 1, t), lambda bb, h, i: (bb, h, i, 0, 0))
    return pl.pallas_call(
        body, name=name, grid=(b, MLA_HEADS // hps, nb),
        in_specs=[qs, ks, vs], out_specs=[qs, ls],
        out_shape=[SDS((b, s, hd), F32), SDS((b, MLA_HEADS, nb, 1, t), F32)],
        scratch_shapes=[pltpu.VMEM((hps, 1, t), F32), pltpu.VMEM((hps, 1, t), F32), pltpu.VMEM((hps, HEAD_PAD, t), F32)],
        compiler_params=_cparams(3))(q, k, vt)


def flash_bwd(q, k, v, do, lse, dlt, name):
    b, s, hd = q.shape
    t = _att_tile(s)
    nb = s // t
    th = t // 2
    lse_r = lse
    dlt_r = dlt.reshape(b, MLA_HEADS, nb, 1, t)

    hps = ATT_HEADS_PER_STEP_BWD
    hw = hps * HEAD_PAD

    def body(q_ref, k_ref, v_ref, do_ref, lse_ref, dlt_ref, dq_ref, dk_ref, dv_ref, dq_s, dk_s, dv_s):
        dq_s[...] = jnp.zeros_like(dq_s)
        dk_s[...] = jnp.zeros_like(dk_s)
        dv_s[...] = jnp.zeros_like(dv_s)

        def tile(j, i, diagonal):
            chains = [(hh, half) for hh in range(hps) for half in range(2)]
            lanes = lambda hh: slice(HEAD_PAD * hh, HEAD_PAD * (hh + 1))
            keys = lambda half: pl.ds(pl.multiple_of(j * t + th * half, th), th)
            q0 = lambda half: th if diagonal and half == 1 else 0
            qsel = lambda half: pl.ds(pl.multiple_of(i * t + q0(half), th), t - q0(half))
            sts, dpts = {}, {}
            for hh, half in chains:
                ls_, ks, qs, nq = lanes(hh), keys(half), qsel(half), t - q0(half)
                st = lax.dot_general(k_ref[ks, ls_], q_ref[qs, ls_], NT_DIMS, preferred_element_type=F32)
                if diagonal:
                    row = lax.broadcasted_iota(I32, (th, nq), 0) + th * half
                    col = lax.broadcasted_iota(I32, (th, nq), 1) + q0(half)
                    st = jnp.where(row <= col, st, -jnp.inf)
                sts[hh, half] = st
                dpts[hh, half] = lax.dot_general(v_ref[ks, ls_], do_ref[qs, ls_], NT_DIMS, preferred_element_type=F32)
            pts, dsts = {}, {}
            for hh, half in chains:
                qcols = slice(q0(half), t)
                pt = jnp.exp2(sts[hh, half] * ATT_SCALE_LOG2E - lse_ref[hh, i][:, qcols] * LOG2E)
                pts[hh, half] = pt.astype(BF16)
                dsts[hh, half] = (pt * (dpts[hh, half] - dlt_ref[hh, i][:, qcols])).astype(BF16)
            for hh, half in chains:
                ls_, ks, qs = lanes(hh), keys(half), qsel(half)
                dv_s[ks, ls_] += jnp.dot(pts[hh, half], do_ref[qs, ls_], preferred_element_type=F32)
                dk_s[ks, ls_] += jnp.dot(dsts[hh, half], q_ref[qs, ls_], preferred_element_type=F32)
                dq_s[qs, ls_] += lax.dot_general(dsts[hh, half], k_ref[ks, ls_], TN_DIMS, preferred_element_type=F32)

        def key_tile(j, carry):
            tile(j, j, True)

            def query_tile(i, c2):
                tile(j, i, False)
                return c2

            lax.fori_loop(j + 1, nb, query_tile, 0)
            return carry

        lax.fori_loop(0, nb, key_tile, 0)
        dq_ref[...] = (dq_s[...] * ATT_SCALE).astype(BF16)
        dk_ref[...] = (dk_s[...] * ATT_SCALE).astype(BF16)
        dv_ref[...] = dv_s[...].astype(BF16)

    hs = pl.BlockSpec((None, s, hw), lambda bb, h: (bb, 0, h))
    ls = pl.BlockSpec((None, hps, nb, 1, t), lambda bb, h: (bb, h, 0, 0, 0))
    return pl.pallas_call(
        body, name=name, grid=(b, MLA_HEADS // hps),
        in_specs=[hs, hs, hs, hs, ls, ls], out_specs=[hs, hs, hs],
        out_shape=[SDS((b, s, hd), BF16)] * 3, scratch_shapes=[pltpu.VMEM((s, hw), F32)] * 3,
        compiler_params=_cparams(2))(q, k, v, do, lse_r, dlt_r)


def out_proj(ys, attn, mnw, wo, x, gate, above, name):
    b, s, d = x.shape
    tm = min(512, s)

    def body(ys_ref, at_ref, mnw_ref, wo_ref, x_ref, g_ref, nw_ref, sc_ref, sh_ref, xn_ref, o_ref, ym_ref, h_ref):
        av = at_ref[...]
        ym = (av * _rms(av) * mnw_ref[...]).astype(BF16)
        ym_ref[...] = ym
        o = jnp.dot(ys_ref[...], wo_ref[0:D_SSD, :], preferred_element_type=F32)
        o += jnp.dot(ym, wo_ref[D_SSD:2 * D_SSD, :], preferred_element_type=F32)
        xn = x_ref[...] + g_ref[...] * o
        xn_ref[...] = xn
        o_ref[...] = o.astype(BF16)
        h_ref[...] = _norm_mod_tile(xn, nw_ref, sc_ref, sh_ref)

    return pl.pallas_call(
        body, name=name, grid=(b, s // tm),
        in_specs=[_row(tm, D_SSD), _row(tm, D_SSD), _full((1, D_SSD)), _full(wo.shape), _row(tm, d), _bvec(d),
                  _full((1, d)), _bvec(d), _bvec(d)],
        out_specs=[_row(tm, d), _row(tm, d), _row(tm, D_SSD), _row(tm, d)],
        out_shape=[SDS((b, s, d), F32), SDS((b, s, d), BF16), SDS((b, s, D_SSD), BF16), SDS((b, s, d), BF16)],
        compiler_params=_cparams(2))(ys, attn, mnw, wo, x, gate, *above)


def out_proj_bwd(dout, attn, mnw, wo, name):
    b, s, d = dout.shape
    tm = min(512, s)

    def body(do_ref, at_ref, mnw_ref, wo_ref, dys_ref, dat_ref, dlt_ref, dw_ref):
        lane = lax.broadcasted_iota(I32, (tm, LANES), 1)
        @pl.when(_first_step())
        def _():
            dw_ref[...] = jnp.zeros_like(dw_ref)
        dov = do_ref[...]
        dys_ref[...] = lax.dot_general(dov, wo_ref[0:D_SSD, :], NT_DIMS, preferred_element_type=F32)
        dym = lax.dot_general(dov, wo_ref[D_SSD:2 * D_SSD, :], NT_DIMS, preferred_element_type=F32)
        av = at_ref[...]
        r = _rms(av)
        n = av * r
        dw_ref[...] += jnp.sum(dym * n, axis=0, keepdims=True)
        dat = _rms_bwd(dym * mnw_ref[...], n, r)
        dat_ref[...] = dat.astype(BF16)
        prod = dat * av
        cols = jnp.zeros((tm, LANES), F32)
        for h in range(MLA_HEADS):
            cols += jnp.where(lane == h, jnp.sum(prod[:, HEAD_PAD * h:HEAD_PAD * (h + 1)], axis=1, keepdims=True), 0.0)
        dlt_ref[...] = cols.T[0:MLA_HEADS, :]

    return pl.pallas_call(
        body, name=name, grid=(b, s // tm),
        in_specs=[_row(tm, d), _row(tm, D_SSD), _full((1, D_SSD)), _full(wo.shape)],
        out_specs=[_row(tm, D_SSD), _row(tm, D_SSD),
                   pl.BlockSpec((None, MLA_HEADS, tm), lambda bb, i: (bb, 0, i)), _full((1, D_SSD))],
        out_shape=[SDS((b, s, D_SSD), F32), SDS((b, s, D_SSD), BF16), SDS((b, MLA_HEADS, s), F32),
                   SDS((1, D_SSD), F32)],
        compiler_params=_cparams(2))(dout, attn, mnw, wo)


def adaln_fwd(c_all, w_ada, b_ada, name):
    nb, d = c_all.shape
    n = w_ada.shape[1]

    def body(c_ref, w_ref, b_ref, m_ref, ca_ref):
        cv = c_ref[...]
        ca = (cv * _sigmoid(cv)).astype(BF16)
        ca_ref[...] = ca
        m_ref[...] = jnp.dot(ca, w_ref[...].astype(BF16), preferred_element_type=F32) + b_ref[...]

    return pl.pallas_call(
        body, name=name, out_shape=[SDS((nb, n), F32), SDS((nb, d), BF16)],
        compiler_params=pltpu.CompilerParams(vmem_limit_bytes=VMEM_LIMIT))(c_all, w_ada, b_ada)


def adaln_bwd(c_act, dmod_cols, name):
    d, n = c_act.shape[1], dmod_cols.shape[1]

    def body(c_ref, dm_ref, gw_ref):
        gw_ref[...] = lax.dot_general(c_ref[...], dm_ref[...].astype(BF16), TN_DIMS, preferred_element_type=F32)

    return pl.pallas_call(
        body, name=name, out_shape=SDS((d, n), F32),
        compiler_params=pltpu.CompilerParams(vmem_limit_bytes=VMEM_LIMIT))(c_act, dmod_cols)


def sum_rows(x, name):
    def body(x_ref, o_ref):
        o_ref[...] = jnp.sum(x_ref[...], axis=0, keepdims=True)
    return pl.pallas_call(body, name=name, out_shape=SDS((1, x.shape[1]), F32))(x)


def squeeze_heads(x, et_mat, name):
    def body(x_ref, et_ref, o_ref):
        xv = jnp.broadcast_to(x_ref[...], (8, x.shape[1]))
        o_ref[...] = _dot_01(xv, et_ref[...], 3)[0:1, :]
    return pl.pallas_call(body, name=name, out_shape=SDS((1, LANES), F32))(x, et_mat)


def sum_blocks(x, name):
    n, r, c = x.shape
    tr = next(cand for cand in (256, 128, 64, 32, 16, 8) if r % cand == 0)

    def body(x_ref, o_ref):
        acc = x_ref[0].astype(F32)
        for k in range(1, n):
            acc += x_ref[k].astype(F32)
        o_ref[...] = acc

    return pl.pallas_call(
        body, name=name, grid=(r // tr,), in_specs=[pl.BlockSpec((n, tr, c), lambda i: (0, i, 0))],
        out_specs=pl.BlockSpec((tr, c), lambda i: (i, 0)), out_shape=SDS((r, c), F32),
        compiler_params=_cparams(1))(x)


def _adam_math(w, g, m, v):
    m = ADAM_B1 * m + (1.0 - ADAM_B1) * g
    v = ADAM_B2 * v + (1.0 - ADAM_B2) * (g * g)
    m_hat = m / (1.0 - ADAM_B1 ** ADAM_STEP)
    v_hat = v / (1.0 - ADAM_B2 ** ADAM_STEP)
    return -ADAM_LR * (m_hat / (jnp.sqrt(v_hat) + ADAM_EPS) + ADAM_WD * w), m, v


def adamw(w, g, m, v, name):
    r, c = w.shape[-2:]
    tr = r
    for cand in (512, 256, 128, 64, 32, 16, 8):
        if r % cand == 0 and cand * c * 4 <= 2 * 1024 * 1024:
            tr = cand
            break

    def body(w_ref, g_ref, m_ref, v_ref, d_ref, mo_ref, vo_ref):
        d_ref[...], mo_ref[...], vo_ref[...] = _adam_math(w_ref[...], g_ref[...], m_ref[...], v_ref[...])

    def spec(a):
        return pl.BlockSpec((tr, c), lambda i: (i, 0)) if a.ndim == 2 else pl.BlockSpec((None, tr, c), lambda i: (0, i, 0))

    return pl.pallas_call(
        body, name=name, grid=(r // tr,), in_specs=[spec(w), spec(g), spec(m), spec(v)], out_specs=[spec(w)] * 3,
        out_shape=[SDS(w.shape, F32)] * 3, compiler_params=_cparams(1))(w, g, m, v)


def adamw_blocks(w, blocks, m, v, name):
    r, c = w.shape
    tr = next((cand for cand in range(r // 32 * 16, 0, -16) if r % cand == 0), r)

    def body(w_ref, b_ref, m_ref, v_ref, g_ref, d_ref, mo_ref, vo_ref):
        g = b_ref[0].astype(F32)
        for k in range(1, N_DEV):
            g += b_ref[k].astype(F32)
        g_ref[...] = g
        d_ref[...], mo_ref[...], vo_ref[...] = _adam_math(w_ref[...], g, m_ref[...], v_ref[...])

    spec = pl.BlockSpec((tr, c), lambda i: (i, 0))
    return pl.pallas_call(
        body, name=name, grid=(r // tr,),
        in_specs=[spec, pl.BlockSpec((N_DEV, tr, c), lambda i: (0, i, 0)), spec, spec], out_specs=[spec] * 4,
        out_shape=[SDS((r, c), F32)] * 4, compiler_params=_cparams(1))(w, blocks, m, v)


def adamw_many(ws, gs, ms, vs, name):
    n = len(ws)

    def body(*refs):
        w_r, g_r, m_r, v_r = (refs[k * n:(k + 1) * n] for k in range(4))
        d_r, mo_r, vo_r = (refs[(4 + k) * n:(5 + k) * n] for k in range(3))
        for k in range(n):
            d_r[k][...], mo_r[k][...], vo_r[k][...] = _adam_math(w_r[k][...], g_r[k][...], m_r[k][...], v_r[k][...])

    shapes = [SDS(w.shape, F32) for w in ws]
    outs = pl.pallas_call(body, name=name, out_shape=shapes * 3)(*ws, *gs, *ms, *vs)
    return outs[:n], outs[n:2 * n], outs[2 * n:]


TRANSPOSED = ("ffn1_w_gate", "ffn1_w_up", "ffn2_w_gate", "ffn2_w_up", "w_in", "w_ukv", "w_uq")
GATHER_GROUPS = (("ffn1_w_gate", "ffn1_w_up"), ("ffn2_w_gate", "ffn2_w_up", "ffn2_w_down"),
                 ("ffn1_w_down", "w_in", "w_ukv", "w_uq", "w_out"))
GRAD_GROUPS = (("ffn2", ("ffn2_w_gate", "ffn2_w_up", "ffn2_w_down")), ("mixer", ("w_out", "w_in", "w_ukv", "w_uq")),
               ("ffn1_down", ("ffn1_w_down",)), ("ffn1_gate", ("ffn1_w_gate",)), ("ffn1_up", ("ffn1_w_up",)))


def _shard_view(name, w):
    return w[0].T if name in TRANSPOSED else w[0]


def _shard_unview(name, t):
    return t.T[None] if name in TRANSPOSED else t[None]


def _grad_blocks(name, gw):
    if name == "w_ukv":
        hd = MLA_HEADS * HEAD_PAD
        return jnp.concatenate([gw[:hd].reshape(MLA_HEADS, HEAD_PAD, KV_LORA)[:, :QK_NOPE],
                                gw[hd:].reshape(MLA_HEADS, V_HEAD, KV_LORA)], axis=1)
    if name == "w_uq":
        return gw.reshape(MLA_HEADS, HEAD_PAD, Q_LORA)[:, :QK_DIM]
    return gw.reshape(N_DEV, -1, D_MODEL)


def _pack_rows(arrs):
    parts = []
    for a in arrs:
        flat = a.reshape(-1).astype(F32)
        pad = (-flat.shape[0]) % D_MODEL
        if pad:
            flat = jnp.pad(flat, (0, pad))
        parts.append(flat.reshape(-1, D_MODEL))
    out = jnp.concatenate(parts, axis=0)
    pad = (-out.shape[0]) % 8
    if pad:
        out = jnp.pad(out, ((0, pad), (0, 0)))
    return out


def _unpack_rows(packed, shapes):
    out, row = [], 0
    for shp in shapes:
        n = math.prod(shp)
        nrow = -(-n // D_MODEL)
        out.append(packed[row:row + nrow].reshape(-1)[:n].reshape(shp))
        row += nrow
    return out


IN_PROJ_ROWS = ((0, 0, 2560), (2560, 3200, 16), (2576, 2560, 384), (2960, 2944, 256), (3216, 3216, 32))


def in_proj_rows(blocks, name):
    per, d = blocks.shape[1:]
    pieces = []
    for src, dst, rows in IN_PROJ_ROWS:
        while rows:
            n = min(rows, per - src % per)
            pieces.append((src // per, src % per, dst, n))
            src, dst, rows = src + n, dst + n, rows - n

    def body(w_ref, o_ref):
        for blk, src, dst, rows in pieces:
            o_ref[dst:dst + rows, :] = w_ref[blk, src:src + rows, :]
        o_ref[D_IN:D_IN_PAD, :] = jnp.zeros((D_IN_PAD - D_IN, d), blocks.dtype)

    return pl.pallas_call(body, name=name, out_shape=SDS((D_IN_PAD, d), blocks.dtype))(blocks)


def _rope_tables(positions):
    half = QK_ROPE // 2
    inv_freq = ROPE_THETA ** (-jnp.arange(0, QK_ROPE, 2, dtype=F32) / QK_ROPE)
    ang_t = positions[:, None, :].astype(F32) * inv_freq[:, None]
    cos_t, sin_t = jnp.cos(ang_t), jnp.sin(ang_t)
    b, _, s = ang_t.shape
    ts = min(2048, s)

    def body(c_ref, s_ref, cc_ref, sp_ref, sm_ref):
        row = lax.broadcasted_iota(I32, (half, LANES), 0)
        lane = lax.broadcasted_iota(I32, (half, LANES), 1)
        first, second = lane == QK_NOPE + row, lane == QK_NOPE + half + row

        spread = lambda x, where: _dot_01(x, where.astype(BF16), 3, TN_DIMS)
        lane1 = lax.broadcasted_iota(I32, (1, LANES), 1)
        ones = jnp.where((lane1 < QK_NOPE) | (lane1 >= QK_NOPE + QK_ROPE), 1.0, 0.0)
        cc_ref[...] = spread(c_ref[...], first | second) + ones
        sp_ref[...] = spread(s_ref[...], second)
        sm_ref[...] = -spread(s_ref[...], first)

    src = pl.BlockSpec((None, half, ts), lambda bb, i: (bb, 0, i))
    return pl.pallas_call(
        body, name="rope_tables", grid=(b, s // ts), in_specs=[src, src], out_specs=[_row(ts, LANES)] * 3,
        out_shape=[SDS((b, s, LANES), F32)] * 3, compiler_params=_cparams(2))(cos_t, sin_t)


def weight_views(gathered):
    full = lambda name: gathered[name].reshape(-1, gathered[name].shape[2])
    ukv = full("w_ukv").reshape(MLA_HEADS, QK_NOPE + V_HEAD, KV_LORA)
    wukv_t = jnp.concatenate([jnp.pad(ukv[:, :QK_NOPE], ((0, 0), (0, HEAD_PAD - QK_NOPE), (0, 0))).reshape(-1, KV_LORA),
                              ukv[:, QK_NOPE:].reshape(-1, KV_LORA)], axis=0)
    uq = full("w_uq").reshape(MLA_HEADS, QK_DIM, Q_LORA)
    wuq_t = jnp.pad(uq, ((0, 0), (0, HEAD_PAD - QK_DIM), (0, 0))).reshape(-1, Q_LORA)
    return dict(wg1_t=full("ffn1_w_gate"), wu1_t=full("ffn1_w_up"), wd1=full("ffn1_w_down"),
                wg2_t=full("ffn2_w_gate"), wu2_t=full("ffn2_w_up"), wd2=full("ffn2_w_down"),
                wo=full("w_out"), win_t=in_proj_rows(gathered["w_in"], "in_proj_rows"), wukv_t=wukv_t, wuq_t=wuq_t)


def _ffn_bwd(tag, dxn, do, dgate, x, h, gg, uu, a, sc, norm_w, wg_t, wu_t, wd, below):
    f2 = wd.shape[0] // 2
    dwd = mm_tn(a, do, f2, D_MODEL, tag + "_dwd")
    dgg, duu = ffn_dact(do, wd, gg, uu, tag + "_dact")
    dwg_t = mm_tn(dgg, h, f2, D_MODEL, tag + "_dwg")
    dwu_t = mm_tn(duu, h, f2, D_MODEL, tag + "_dwu")
    dx, dsc, dsh, dnw, *nxt = dh_norm_bwd([dgg, duu], [wg_t, wu_t], x, dxn, norm_w, sc, tag + "_dh", below)
    return dx, (dsh, dsc, dgate), dnw, (dwg_t, dwu_t, dwd), nxt


def local_step(x, tgt, positions, mod, wv, p, h1=None):
    nb, s, d = x.shape
    sh1, sc1, g1, sh2, sc2, g2, sh3, sc3, g3 = mod
    cc, sp, sm = _rope_tables(positions)
    lane_head = jnp.arange(D_SSD, dtype=I32)[None, :] // SSD_HEAD_DIM
    e_mat = (lane_head == jnp.arange(LANES, dtype=I32)[:, None]).astype(BF16)
    et_mat = e_mat.T
    rr, cl = jnp.arange(LANES, dtype=I32)[:, None], jnp.arange(LANES, dtype=I32)[None, :]
    place = ((cl == rr + (QK_NOPE - SSD_HEADS)) & (rr >= SSD_HEADS) & (rr < SSD_HEADS + QK_ROPE)).astype(F32)
    dtb = jnp.pad(p["dt_bias"], ((0, 0), (0, LANES - SSD_HEADS)))
    alog = jnp.pad(p["a_log"], ((0, 0), (0, LANES - SSD_HEADS)))
    dskip_e = jnp.repeat(p["d_skip"], SSD_HEAD_DIM, axis=1)

    if h1 is None:
        h1 = norm_mod(x, p["norm_ffn1"], sc1, sh1, "ffn1_norm")
    gg1, uu1, a1 = ffn_up(h1, wv["wg1_t"], wv["wu1_t"], "ffn1_up")
    x1, o1, h2 = ffn_down(a1, wv["wd1"], x, g1, 0.5, "ffn1_down", (p["norm_mix"], sc2, sh2))
    z, u, cq, ckv, misc = in_proj(h2, wv["win_t"], "in_proj")
    xs, bm, cm_ = conv_fwd(u, p["conv_w"], p["conv_b"], "conv_fwd")
    ys, y, prev = ssd_fwd(xs, bm, cm_, misc, z, dtb, alog, dskip_e, p["ssd_norm_w"], e_mat, "ssd_fwd")
    q, k, v, vt, qn, kvn = qkv_fwd(cq, ckv, misc, cc, sp, sm, p["q_norm_w"], p["kv_norm_w"], wv["wuq_t"], wv["wukv_t"],
                               place, "qkv_fwd")
    attn, lse = flash_fwd(q, k, vt, "flash_fwd")
    x2, o2, ym, h3 = out_proj(ys, attn, p["mla_norm_w"], wv["wo"], x1, g2, (p["norm_ffn2"], sc3, sh3), "out_proj")
    gg3, uu3, a3 = ffn_up(h3, wv["wg2_t"], wv["wu2_t"], "ffn2_up")
    loss, dx3, dnfin, do3, dg3 = ffn_down_final(a3, wv["wd2"], x2, g3, 0.5, p["norm_final"], tgt, "ffn2_down_loss")

    dx2, dmod3, dnf2, (dwg2, dwu2, dwd2), (dout, dg2) = _ffn_bwd(
        "ffn2", dx3, do3, dg3, x2, h3, gg3, uu3, a3, sc3, p["norm_ffn2"], wv["wg2_t"], wv["wu2_t"], wv["wd2"],
        (o2, g2, 1.0))
    dys, dattn, dlt, dmlan = out_proj_bwd(dout, attn, p["mla_norm_w"], wv["wo"], "out_proj_bwd")
    dwo = mm_tn_blocks([ys, ym], dout, "dwo")
    dxs, dbm, dcm, dz, ddt, dssdn, ddsk_lane, ddtb, dalog = ssd_bwd(
        dys, y, z, xs, bm, cm_, misc, prev, dtb, alog, dskip_e, p["ssd_norm_w"], e_mat, et_mat, "ssd_bwd")
    dq, dk, dv = flash_bwd(q, k, v, dattn, lse, dlt, "flash_bwd")
    dcq, dckv, dmisc, dqp, dkvc, dqn, dkvn = qkv_bwd(dq, dk, dv, ddt, cq, ckv, cc, sp, sm, p["q_norm_w"], p["kv_norm_w"],
                                                     wv["wuq_t"], wv["wukv_t"], place.T, "qkv_bwd")
    dwuq = mm_tn(dqp, qn, MLA_HEADS * HEAD_PAD, Q_LORA, "dwuq")
    dwukv = mm_tn(dkvc, kvn, MLA_HEADS * HEAD_PAD, KV_LORA, "dwukv")
    du, dconv = conv_bwd(dxs, dbm, dcm, u, p["conv_w"], p["conv_b"], "conv_bwd")
    dconv = conv_grads_fold(dconv, "conv_grads_fold")
    dproj = [dz, du, dcq, dckv, dmisc]
    dwin = mm_tn_blocks(dproj, h2, "dwin", IN_PROJ_ROWS)
    dx1, dsc2, dsh2, dnmix, do1, dg1 = dh_norm_bwd(dproj, [wv["win_t"]], x1, dx2, p["norm_mix"], sc2, "mix_dh",
                                                   (o1, g1, 0.5))
    dx0, dmod1, dnf1, (dwg1, dwu1, dwd1), _ = _ffn_bwd(
        "ffn1", dx1, do1, dg1, x, h1, gg1, uu1, a1, sc1, p["norm_ffn1"], wv["wg1_t"], wv["wu1_t"], wv["wd1"], None)

    dmod = jnp.concatenate([*dmod1, dsh2, dsc2, dg2, *dmod3], axis=1).reshape(nb, N_MOD * d)
    return dict(
        loss=loss, dx=dx0, dmod=dmod, norm_ffn1=dnf1, norm_mix=dnmix, norm_ffn2=dnf2, norm_final=dnfin,
        ssd_norm_w=dssdn, mla_norm_w=dmlan, q_norm_w=dqn, kv_norm_w=dkvn,
        dt_bias=ddtb[:, :SSD_HEADS], a_log=dalog[:, :SSD_HEADS],
        d_skip=squeeze_heads(ddsk_lane, et_mat, "d_skip_heads")[:, :SSD_HEADS],
        conv_b=dconv[4:5], conv_w=dconv[0:4],
        gw=dict(ffn1_w_gate=dwg1, ffn1_w_up=dwu1, ffn1_w_down=dwd1, ffn2_w_gate=dwg2, ffn2_w_up=dwu2, ffn2_w_down=dwd2,
                w_out=dwo, w_in=dwin, w_ukv=dwukv, w_uq=dwuq))


def kernel(x, c, positions, w_ada, b_ada, norm_ffn1, ffn1_w_gate, ffn1_w_up, ffn1_w_down, norm_mix, w_in, conv_w, conv_b, dt_bias, a_log, d_skip, ssd_norm_w, q_norm_w, w_uq, kv_norm_w, w_ukv, mla_norm_w, w_out, norm_ffn2, ffn2_w_gate, ffn2_w_up, ffn2_w_down, norm_final, loss_target, m_w_ada, m_b_ada, m_norm_ffn1, m_ffn1_w_gate, m_ffn1_w_up, m_ffn1_w_down, m_norm_mix, m_w_in, m_conv_w, m_conv_b, m_dt_bias, m_a_log, m_d_skip, m_ssd_norm_w, m_q_norm_w, m_w_uq, m_kv_norm_w, m_w_ukv, m_mla_norm_w, m_w_out, m_norm_ffn2, m_ffn2_w_gate, m_ffn2_w_up, m_ffn2_w_down, m_norm_final, v_w_ada, v_b_ada, v_norm_ffn1, v_ffn1_w_gate, v_ffn1_w_up, v_ffn1_w_down, v_norm_mix, v_w_in, v_conv_w, v_conv_b, v_dt_bias, v_a_log, v_d_skip, v_ssd_norm_w, v_q_norm_w, v_w_uq, v_kv_norm_w, v_w_ukv, v_mla_norm_w, v_w_out, v_norm_ffn2, v_ffn2_w_gate, v_ffn2_w_up, v_ffn2_w_down, v_norm_final):
    names = ["w_ada", "b_ada", "norm_ffn1", "ffn1_w_gate", "ffn1_w_up", "ffn1_w_down", "norm_mix", "w_in", "conv_w",
             "conv_b", "dt_bias", "a_log", "d_skip", "ssd_norm_w", "q_norm_w", "w_uq", "kv_norm_w", "w_ukv",
             "mla_norm_w", "w_out", "norm_ffn2", "ffn2_w_gate", "ffn2_w_up", "ffn2_w_down", "norm_final"]
    W = dict(zip(names, (w_ada, b_ada, norm_ffn1, ffn1_w_gate, ffn1_w_up, ffn1_w_down, norm_mix, w_in, conv_w, conv_b, dt_bias, a_log, d_skip, ssd_norm_w, q_norm_w, w_uq, kv_norm_w, w_ukv, mla_norm_w, w_out, norm_ffn2, ffn2_w_gate, ffn2_w_up, ffn2_w_down, norm_final)))
    M = dict(zip(names, (m_w_ada, m_b_ada, m_norm_ffn1, m_ffn1_w_gate, m_ffn1_w_up, m_ffn1_w_down, m_norm_mix, m_w_in, m_conv_w, m_conv_b, m_dt_bias, m_a_log, m_d_skip, m_ssd_norm_w, m_q_norm_w, m_w_uq, m_kv_norm_w, m_w_ukv, m_mla_norm_w, m_w_out, m_norm_ffn2, m_ffn2_w_gate, m_ffn2_w_up, m_ffn2_w_down, m_norm_final)))
    V = dict(zip(names, (v_w_ada, v_b_ada, v_norm_ffn1, v_ffn1_w_gate, v_ffn1_w_up, v_ffn1_w_down, v_norm_mix, v_w_in, v_conv_w, v_conv_b, v_dt_bias, v_a_log, v_d_skip, v_ssd_norm_w, v_q_norm_w, v_w_uq, v_kv_norm_w, v_w_ukv, v_mla_norm_w, v_w_out, v_norm_ffn2, v_ffn2_w_gate, v_ffn2_w_up, v_ffn2_w_down, v_norm_final)))

    nb, s, d = x.shape
    me = 4 * lax.axis_index("x") + 2 * lax.axis_index("y") + lax.axis_index("c")
    n_ada = w_ada.shape[2]

    taps, n_cw = conv_w.shape[1:]
    (cg,) = all_gather8([_pack_rows([c, conv_w[0]])], "gather_c")
    c_all = cg[:, 0:nb].reshape(N_DEV * nb, d)
    conv_w_full = cg[:, nb, 0:taps * n_cw].reshape(N_DEV, taps, n_cw).transpose(1, 0, 2).reshape(taps, N_DEV * n_cw)
    shards = [[_shard_view(name, W[name]).astype(BF16) for name in group] for group in GATHER_GROUPS]

    b_ada_cols = lax.dynamic_slice(b_ada, (0, me * n_ada), (1, n_ada))
    mod_cols, c_act = adaln_fwd(c_all, w_ada[0], b_ada_cols, "adaln_fwd")
    (mod_g,) = all_gather8([mod_cols], "gather_mod")
    mod = lax.dynamic_slice(mod_g, (0, me * nb, 0), (N_DEV, nb, n_ada)).transpose(1, 0, 2).reshape(nb, N_MOD, 1, d)
    mod = [mod[:, k] for k in range(N_MOD)]
    h1, *ffn1_w = norm_mod(x, norm_ffn1, mod[1], mod[0], "ffn1_norm", gather=shards[0])
    gathered = dict(zip(GATHER_GROUPS[0], ffn1_w))
    gathered, h1, shards = lax.optimization_barrier((gathered, h1, shards))
    gathered.update(zip(GATHER_GROUPS[1], sc_all_gather8(shards[1], "gather_w_ffn2", 1)))
    gathered.update(zip(GATHER_GROUPS[2], sc_all_gather8(shards[2], "gather_w_mixer", 7)))
    wv = weight_views(gathered)

    P = dict(W)
    P["conv_w"] = conv_w_full
    P["norm_final"] = norm_final.reshape(1, d)
    R = local_step(x, loss_target, positions, mod, wv, P, h1)

    dmod = R["dmod"]
    partial_shapes = [(1,), (1, d), (1, d), (1, d), (1, d), (1, d), (1, d), (1, Q_LORA), (1, KV_LORA),
                      (1, SSD_HEADS), (1, SSD_HEADS), (1, SSD_HEADS), (1, D_CONV), (4, D_CONV), (1, N_MOD * d),
                      (nb, N_MOD * d)]
    partial = _pack_rows([R["loss"][0, :1], R["norm_ffn1"], R["norm_mix"], R["norm_ffn2"], R["norm_final"],
                          R["ssd_norm_w"], R["mla_norm_w"], R["q_norm_w"], R["kv_norm_w"],
                          R["dt_bias"], R["a_log"], R["d_skip"], R["conv_b"], R["conv_w"],
                          sum_rows(dmod, "dmod_rows"), dmod])
    (partial_g,) = all_gather8([partial], "gather_partials")
    (loss, g_nf1, g_nmix, g_nf2, g_nfin, g_ssdn, g_mlan, g_qn, g_kvn, g_dtb, g_alog, g_dskip, g_convb, g_convw,
     g_bada, _) = _unpack_rows(sum_blocks(partial_g, "sum_partials"), partial_shapes)
    dmod_row = sum(-(-math.prod(shp) // D_MODEL) for shp in partial_shapes[:-1])
    dmod_all = partial_g[:, dmod_row:dmod_row + nb * N_MOD].reshape(N_DEV * nb, N_MOD * d)
    g_wada = adaln_bwd(c_act, lax.dynamic_slice(dmod_all, (0, me * n_ada), (N_DEV * nb, n_ada)), "adaln_bwd")
    n_cw = conv_w.shape[2]
    G = {"w_ada": g_wada[None], "b_ada": g_bada, "norm_ffn1": g_nf1, "norm_mix": g_nmix, "norm_ffn2": g_nf2,
         "norm_final": g_nfin.reshape(d), "ssd_norm_w": g_ssdn, "mla_norm_w": g_mlan, "q_norm_w": g_qn,
         "kv_norm_w": g_kvn, "dt_bias": g_dtb, "a_log": g_alog, "d_skip": g_dskip, "conv_b": g_convb,
         "conv_w": lax.dynamic_slice(g_convw, (0, me * n_cw), (4, n_cw))[None]}

    DW, NM, NV = {}, {}, {}
    gw = R["gw"]
    for k, (tag, group) in enumerate(GRAD_GROUPS):
        send = [_grad_blocks(name, gw[name]).reshape(N_DEV, *_shard_view(name, W[name]).shape) for name in group]
        recv = sc_all_to_all8(send, "exchange_" + tag, 2 + k)
        for name, blocks in zip(group, recv):
            res = adamw_blocks(_shard_view(name, W[name]), blocks, _shard_view(name, M[name]), _shard_view(name, V[name]),
                               "adamw_" + name)
            G[name], DW[name], NM[name], NV[name] = [_shard_unview(name, t) for t in res]
    DW["w_ada"], NM["w_ada"], NV["w_ada"] = adamw(w_ada, g_wada, m_w_ada, v_w_ada, "adamw_w_ada")
    small = [n for n in names if n not in DW]
    as2d = lambda a: a.reshape(-1, a.shape[-1])
    outs = adamw_many([as2d(W[n]) for n in small], [as2d(G[n]) for n in small], [as2d(M[n]) for n in small],
                      [as2d(V[n]) for n in small], "adamw_small")
    for res, dst in zip(outs, (DW, NM, NV)):
        for n, t in zip(small, res):
            dst[n] = t.reshape(W[n].shape)
    return (loss.reshape(()), R["dx"], *[G[n] for n in names], *[DW[n] for n in names], *[NM[n] for n in names],
            *[NV[n] for n in names])
```
